```python
import jax, jax.numpy as jnp
from jax import lax
import numpy as np

D_MODEL = 1024
BATCH = 8
SEQ = 8192
DEPTH = 2

N_MIXERS = 2
HEAD_DIM = 64
N_Q_HEADS = D_MODEL // HEAD_DIM
N_KV_HEADS = max(1, N_Q_HEADS // 8)
Q_PER_KV = N_Q_HEADS // N_KV_HEADS
QKV_DIM = (N_Q_HEADS + 2 * N_KV_HEADS) * HEAD_DIM
WINDOW = 128
BLOCK = 128
ROPE_THETA = 10000.0
CONV_WIDTH = 31
D_FF = -(-(8 * D_MODEL) // (3 * 256)) * 256
N_ATTN_LAYERS = (DEPTH + 1) // 2
N_CONV_LAYERS = DEPTH // 2
RMS_EPS = 1e-5
LN_EPS = 1e-5

kernel_name = "hybrid_swa_sink_conformer_conv"


def rmsnorm(x, g):
    xf = x.astype(jnp.float32)
    y = xf * lax.rsqrt(jnp.mean(xf * xf, axis=-1, keepdims=True) + RMS_EPS)
    return (y * g.astype(jnp.float32)).astype(x.dtype)


def layernorm(x, g, b):
    xf = x.astype(jnp.float32)
    mu = jnp.mean(xf, axis=-1, keepdims=True)
    xc = xf - mu
    var = jnp.mean(xc * xc, axis=-1, keepdims=True)
    y = xc * lax.rsqrt(var + LN_EPS)
    return (y * g.astype(jnp.float32) + b.astype(jnp.float32)).astype(x.dtype)


def rope_tables(seq_len):
    pos = jnp.arange(seq_len, dtype=jnp.float32)
    inv_freq = ROPE_THETA ** (-jnp.arange(0, HEAD_DIM, 2, dtype=jnp.float32) / HEAD_DIM)
    ang = pos[:, None] * inv_freq[None, :]
    return jnp.cos(ang)[:, None, :], jnp.sin(ang)[:, None, :]


def apply_rope(t, cos, sin):
    tf = t.astype(jnp.float32)
    t1, t2 = tf[..., : HEAD_DIM // 2], tf[..., HEAD_DIM // 2:]
    out = jnp.concatenate([t1 * cos - t2 * sin, t2 * cos + t1 * sin], axis=-1)
    return out.astype(t.dtype)


def band_mask(n_blocks):
    i = jnp.arange(BLOCK)[:, None]
    j = jnp.arange(2 * BLOCK)[None, :]
    rel = i + BLOCK - j
    band = (rel >= 0) & (rel < WINDOW)
    k_pos = jnp.arange(n_blocks)[:, None] * BLOCK - BLOCK + jnp.arange(2 * BLOCK)[None, :]
    return band[None, :, :] & (k_pos >= 0)[:, None, :]


def sliding_window_gqa(h, w_qkv, b_qkv, sinks, w_o, b_o, cos, sin):
    B, S, _ = h.shape
    nb = S // BLOCK
    qkv = h @ w_qkv + b_qkv
    q_end = N_Q_HEADS * HEAD_DIM
    k_end = q_end + N_KV_HEADS * HEAD_DIM
    q = qkv[..., :q_end].reshape(B, S, N_Q_HEADS, HEAD_DIM)
    k = qkv[..., q_end:k_end].reshape(B, S, N_KV_HEADS, HEAD_DIM)
    v = qkv[..., k_end:].reshape(B, S, N_KV_HEADS, HEAD_DIM)
    q = apply_rope(q, cos, sin).reshape(B, nb, BLOCK, N_KV_HEADS, Q_PER_KV, HEAD_DIM)
    k = apply_rope(k, cos, sin).reshape(B, nb, BLOCK, N_KV_HEADS, HEAD_DIM)
    v = v.reshape(B, nb, BLOCK, N_KV_HEADS, HEAD_DIM)

    def with_prev(t):
        prev = jnp.pad(t[:, :-1], ((0, 0), (1, 0), (0, 0), (0, 0), (0, 0)))
        return jnp.concatenate([prev, t], axis=2)

    kb, vb = with_prev(k), with_prev(v)
    scores = jnp.einsum('bnqkgd,bnskd->bnkgqs', q, kb,
                        preferred_element_type=jnp.float32) * (HEAD_DIM ** -0.5)
    mask = band_mask(nb)[None, :, None, None]
    scores = jnp.where(mask, scores, jnp.finfo(jnp.float32).min)
    sink = sinks.astype(jnp.float32).reshape(N_KV_HEADS, Q_PER_KV)[None, None, :, :, None, None]
    m = jnp.maximum(jnp.max(scores, axis=-1, keepdims=True), sink)
    p = jnp.exp(scores - m)
    probs = p / (jnp.sum(p, axis=-1, keepdims=True) + jnp.exp(sink - m))
    out = jnp.einsum('bnkgqs,bnskd->bnqkgd', probs.astype(vb.dtype), vb)
    out = out.reshape(B, S, N_Q_HEADS * HEAD_DIM)
    return out @ w_o + b_o


def conformer_conv(h, w_pw1, b_pw1, w_dw, b_dw, ln_g, ln_b, w_pw2, b_pw2):
    a = h @ w_pw1 + b_pw1
    u = a[..., :D_MODEL] * jax.nn.sigmoid(a[..., D_MODEL:])
    u = lax.conv_general_dilated(
        u, w_dw[:, None, :].astype(u.dtype), window_strides=(1,),
        padding=((CONV_WIDTH - 1, 0),),
        dimension_numbers=('NWC', 'WIO', 'NWC'),
        feature_group_count=D_MODEL) + b_dw
    u = jax.nn.silu(layernorm(u, ln_g, ln_b))
    return u @ w_pw2 + b_pw2


def swiglu(h, w1, w3, w2):
    return (jax.nn.silu(h @ w1) * (h @ w3)) @ w2


def _fwd_setup_inputs(seed: int = 0) -> dict:
    key = jax.random.key(seed)
    ks = jax.random.split(key, 24)
    f32 = jnp.float32
    nrm = lambda k, shape, s: jax.random.normal(k, shape, f32) * s
    D, NA, NC = D_MODEL, N_ATTN_LAYERS, N_CONV_LAYERS
    return {
        "x": nrm(ks[0], (BATCH, SEQ, D), 1.0),
        "norm_mix": 1.0 + nrm(ks[1], (DEPTH, D), 0.02),
        "norm_ffn": 1.0 + nrm(ks[2], (DEPTH, D), 0.02),
        "attn_w_qkv": nrm(ks[3], (NA, D, QKV_DIM), D ** -0.5),
        "attn_b_qkv": nrm(ks[4], (NA, QKV_DIM), 0.02),
        "attn_sinks": nrm(ks[5], (NA, N_Q_HEADS), 0.5),
        "attn_w_o": nrm(ks[6], (NA, N_Q_HEADS * HEAD_DIM, D), (N_Q_HEADS * HEAD_DIM) ** -0.5),
        "attn_b_o": nrm(ks[7], (NA, D), 0.02),
        "conv_w_pw1": nrm(ks[8], (NC, D, 2 * D), D ** -0.5),
        "conv_b_pw1": nrm(ks[9], (NC, 2 * D), 0.02),
        "conv_w_dw": nrm(ks[10], (NC, CONV_WIDTH, D), CONV_WIDTH ** -0.5),
        "conv_b_dw": nrm(ks[11], (NC, D), 0.02),
        "conv_ln_g": 1.0 + nrm(ks[12], (NC, D), 0.02),
        "conv_ln_b": nrm(ks[13], (NC, D), 0.02),
        "conv_w_pw2": nrm(ks[14], (NC, D, D), D ** -0.5),
        "conv_b_pw2": nrm(ks[15], (NC, D), 0.02),
        "ffn_w1": nrm(ks[16], (DEPTH, D, D_FF), D ** -0.5),
        "ffn_w3": nrm(ks[17], (DEPTH, D, D_FF), D ** -0.5),
        "ffn_w2": nrm(ks[18], (DEPTH, D_FF, D), D_FF ** -0.5),
        "norm_final": 1.0 + nrm(ks[19], (D,), 0.02),
    }


def _fwd_reference(x, norm_mix, norm_ffn, attn_w_qkv, attn_b_qkv, attn_sinks, attn_w_o, attn_b_o,
              conv_w_pw1, conv_b_pw1, conv_w_dw, conv_b_dw, conv_ln_g, conv_ln_b,
              conv_w_pw2, conv_b_pw2, ffn_w1, ffn_w3, ffn_w2, norm_final):
    cos, sin = rope_tables(x.shape[1])
    h = x
    for layer in range(DEPTH):
        y = rmsnorm(h, norm_mix[layer])
        idx = layer // N_MIXERS
        if layer % N_MIXERS == 0:
            y = sliding_window_gqa(y, attn_w_qkv[idx], attn_b_qkv[idx], attn_sinks[idx],
                                   attn_w_o[idx], attn_b_o[idx], cos, sin)
        else:
            y = conformer_conv(y, conv_w_pw1[idx], conv_b_pw1[idx], conv_w_dw[idx],
                               conv_b_dw[idx], conv_ln_g[idx], conv_ln_b[idx],
                               conv_w_pw2[idx], conv_b_pw2[idx])
        h = h + y
        h = h + swiglu(rmsnorm(h, norm_ffn[layer]), ffn_w1[layer], ffn_w3[layer], ffn_w2[layer])
    return rmsnorm(h, norm_final)


import jax as _jax
import jax.numpy as _jnp

TWIN_FORMAT = 'train_step'
FWD_PARAMS = ['x', 'norm_mix', 'norm_ffn', 'attn_w_qkv', 'attn_b_qkv', 'attn_sinks', 'attn_w_o', 'attn_b_o', 'conv_w_pw1', 'conv_b_pw1', 'conv_w_dw', 'conv_b_dw', 'conv_ln_g', 'conv_ln_b', 'conv_w_pw2', 'conv_b_pw2', 'ffn_w1', 'ffn_w3', 'ffn_w2', 'norm_final']
TWIN_WEIGHTS = ['norm_mix', 'norm_ffn', 'attn_w_qkv', 'attn_b_qkv', 'attn_sinks', 'attn_w_o', 'attn_b_o', 'conv_w_pw1', 'conv_b_pw1', 'conv_w_dw', 'conv_b_dw', 'conv_ln_g', 'conv_ln_b', 'conv_w_pw2', 'conv_b_pw2', 'ffn_w1', 'ffn_w3', 'ffn_w2', 'norm_final']
TWIN_DIFF_INPUT = 'x'
TWIN_INPUTS = ['x', 'norm_mix', 'norm_ffn', 'attn_w_qkv', 'attn_b_qkv', 'attn_sinks', 'attn_w_o', 'attn_b_o', 'conv_w_pw1', 'conv_b_pw1', 'conv_w_dw', 'conv_b_dw', 'conv_ln_g', 'conv_ln_b', 'conv_w_pw2', 'conv_b_pw2', 'ffn_w1', 'ffn_w3', 'ffn_w2', 'norm_final', 'loss_target', 'm_norm_mix', 'm_norm_ffn', 'm_attn_w_qkv', 'm_attn_b_qkv', 'm_attn_sinks', 'm_attn_w_o', 'm_attn_b_o', 'm_conv_w_pw1', 'm_conv_b_pw1', 'm_conv_w_dw', 'm_conv_b_dw', 'm_conv_ln_g', 'm_conv_ln_b', 'm_conv_w_pw2', 'm_conv_b_pw2', 'm_ffn_w1', 'm_ffn_w3', 'm_ffn_w2', 'm_norm_final', 'v_norm_mix', 'v_norm_ffn', 'v_attn_w_qkv', 'v_attn_b_qkv', 'v_attn_sinks', 'v_attn_w_o', 'v_attn_b_o', 'v_conv_w_pw1', 'v_conv_b_pw1', 'v_conv_w_dw', 'v_conv_b_dw', 'v_conv_ln_g', 'v_conv_ln_b', 'v_conv_w_pw2', 'v_conv_b_pw2', 'v_ffn_w1', 'v_ffn_w3', 'v_ffn_w2', 'v_norm_final']
TWIN_OUTPUTS = ['loss', 'grad_x', 'grad_norm_mix', 'grad_norm_ffn', 'grad_attn_w_qkv', 'grad_attn_b_qkv', 'grad_attn_sinks', 'grad_attn_w_o', 'grad_attn_b_o', 'grad_conv_w_pw1', 'grad_conv_b_pw1', 'grad_conv_w_dw', 'grad_conv_b_dw', 'grad_conv_ln_g', 'grad_conv_ln_b', 'grad_conv_w_pw2', 'grad_conv_b_pw2', 'grad_ffn_w1', 'grad_ffn_w3', 'grad_ffn_w2', 'grad_norm_final', 'delta_norm_mix', 'delta_norm_ffn', 'delta_attn_w_qkv', 'delta_attn_b_qkv', 'delta_attn_sinks', 'delta_attn_w_o', 'delta_attn_b_o', 'delta_conv_w_pw1', 'delta_conv_b_pw1', 'delta_conv_w_dw', 'delta_conv_b_dw', 'delta_conv_ln_g', 'delta_conv_ln_b', 'delta_conv_w_pw2', 'delta_conv_b_pw2', 'delta_ffn_w1', 'delta_ffn_w3', 'delta_ffn_w2', 'delta_norm_final', 'new_m_norm_mix', 'new_m_norm_ffn', 'new_m_attn_w_qkv', 'new_m_attn_b_qkv', 'new_m_attn_sinks', 'new_m_attn_w_o', 'new_m_attn_b_o', 'new_m_conv_w_pw1', 'new_m_conv_b_pw1', 'new_m_conv_w_dw', 'new_m_conv_b_dw', 'new_m_conv_ln_g', 'new_m_conv_ln_b', 'new_m_conv_w_pw2', 'new_m_conv_b_pw2', 'new_m_ffn_w1', 'new_m_ffn_w3', 'new_m_ffn_w2', 'new_m_norm_final', 'new_v_norm_mix', 'new_v_norm_ffn', 'new_v_attn_w_qkv', 'new_v_attn_b_qkv', 'new_v_attn_sinks', 'new_v_attn_w_o', 'new_v_attn_b_o', 'new_v_conv_w_pw1', 'new_v_conv_b_pw1', 'new_v_conv_w_dw', 'new_v_conv_b_dw', 'new_v_conv_ln_g', 'new_v_conv_ln_b', 'new_v_conv_w_pw2', 'new_v_conv_b_pw2', 'new_v_ffn_w1', 'new_v_ffn_w3', 'new_v_ffn_w2', 'new_v_norm_final']
TWIN_LEAF_KINDS = {'loss': 'loss', 'grad_x': 'grad_x', 'grad_norm_mix': 'grad_w', 'grad_norm_ffn': 'grad_w', 'grad_attn_w_qkv': 'grad_w', 'grad_attn_b_qkv': 'grad_w', 'grad_attn_sinks': 'grad_w', 'grad_attn_w_o': 'grad_w', 'grad_attn_b_o': 'grad_w', 'grad_conv_w_pw1': 'grad_w', 'grad_conv_b_pw1': 'grad_w', 'grad_conv_w_dw': 'grad_w', 'grad_conv_b_dw': 'grad_w', 'grad_conv_ln_g': 'grad_w', 'grad_conv_ln_b': 'grad_w', 'grad_conv_w_pw2': 'grad_w', 'grad_conv_b_pw2': 'grad_w', 'grad_ffn_w1': 'grad_w', 'grad_ffn_w3': 'grad_w', 'grad_ffn_w2': 'grad_w', 'grad_norm_final': 'grad_w', 'delta_norm_mix': 'delta_w', 'delta_norm_ffn': 'delta_w', 'delta_attn_w_qkv': 'delta_w', 'delta_attn_b_qkv': 'delta_w', 'delta_attn_sinks': 'delta_w', 'delta_attn_w_o': 'delta_w', 'delta_attn_b_o': 'delta_w', 'delta_conv_w_pw1': 'delta_w', 'delta_conv_b_pw1': 'delta_w', 'delta_conv_w_dw': 'delta_w', 'delta_conv_b_dw': 'delta_w', 'delta_conv_ln_g': 'delta_w', 'delta_conv_ln_b': 'delta_w', 'delta_conv_w_pw2': 'delta_w', 'delta_conv_b_pw2': 'delta_w', 'delta_ffn_w1': 'delta_w', 'delta_ffn_w3': 'delta_w', 'delta_ffn_w2': 'delta_w', 'delta_norm_final': 'delta_w', 'new_m_norm_mix': 'new_m', 'new_m_norm_ffn': 'new_m', 'new_m_attn_w_qkv': 'new_m', 'new_m_attn_b_qkv': 'new_m', 'new_m_attn_sinks': 'new_m', 'new_m_attn_w_o': 'new_m', 'new_m_attn_b_o': 'new_m', 'new_m_conv_w_pw1': 'new_m', 'new_m_conv_b_pw1': 'new_m', 'new_m_conv_w_dw': 'new_m', 'new_m_conv_b_dw': 'new_m', 'new_m_conv_ln_g': 'new_m', 'new_m_conv_ln_b': 'new_m', 'new_m_conv_w_pw2': 'new_m', 'new_m_conv_b_pw2': 'new_m', 'new_m_ffn_w1': 'new_m', 'new_m_ffn_w3': 'new_m', 'new_m_ffn_w2': 'new_m', 'new_m_norm_final': 'new_m', 'new_v_norm_mix': 'new_v', 'new_v_norm_ffn': 'new_v', 'new_v_attn_w_qkv': 'new_v', 'new_v_attn_b_qkv': 'new_v', 'new_v_attn_sinks': 'new_v', 'new_v_attn_w_o': 'new_v', 'new_v_attn_b_o': 'new_v', 'new_v_conv_w_pw1': 'new_v', 'new_v_conv_b_pw1': 'new_v', 'new_v_conv_w_dw': 'new_v', 'new_v_conv_b_dw': 'new_v', 'new_v_conv_ln_g': 'new_v', 'new_v_conv_ln_b': 'new_v', 'new_v_conv_w_pw2': 'new_v', 'new_v_conv_b_pw2': 'new_v', 'new_v_ffn_w1': 'new_v', 'new_v_ffn_w3': 'new_v', 'new_v_ffn_w2': 'new_v', 'new_v_norm_final': 'new_v'}


def _forward(args):
    return _fwd_reference(*[args[k] for k in FWD_PARAMS])


def _output_shape():
    out = _jax.eval_shape(lambda: _forward(_fwd_setup_inputs(0)))
    return out.shape, out.dtype

N_MICROBATCH = 1
ADAM_LR = 0.001
ADAM_B1 = 0.9
ADAM_B2 = 0.999
ADAM_EPS = 1e-08
ADAM_WD = 0.01
ADAM_STEP = 10
PER_EXAMPLE_BATCH_AXIS = {'x': 0, 'loss_target': 0}
SHARED_INPUTS = []
_WEIGHT_DTYPES = {'norm_mix': _jnp.float32, 'norm_ffn': _jnp.float32, 'attn_w_qkv': _jnp.float32, 'attn_b_qkv': _jnp.float32, 'attn_sinks': _jnp.float32, 'attn_w_o': _jnp.float32, 'attn_b_o': _jnp.float32, 'conv_w_pw1': _jnp.float32, 'conv_b_pw1': _jnp.float32, 'conv_w_dw': _jnp.float32, 'conv_b_dw': _jnp.float32, 'conv_ln_g': _jnp.float32, 'conv_ln_b': _jnp.float32, 'conv_w_pw2': _jnp.float32, 'conv_b_pw2': _jnp.float32, 'ffn_w1': _jnp.float32, 'ffn_w3': _jnp.float32, 'ffn_w2': _jnp.float32, 'norm_final': _jnp.float32}
MOMENT_SCALE = {'norm_mix': 1.207186e-01, 'norm_ffn': 1.824649e-01, 'attn_w_qkv': 9.000058e-02, 'attn_b_qkv': 3.543363e-01, 'attn_sinks': 5.328444e-02, 'attn_w_o': 6.330830e-02, 'attn_b_o': 3.431072e-01, 'conv_w_pw1': 9.816658e-02, 'conv_b_pw1': 1.165481e-01, 'conv_w_dw': 1.291527e-01, 'conv_b_dw': 2.700300e-01, 'conv_ln_g': 1.532858e-01, 'conv_ln_b': 1.436421e-01, 'conv_w_pw2': 1.285171e-01, 'conv_b_pw2': 2.514831e-01, 'ffn_w1': 7.784093e-02, 'ffn_w3': 7.557337e-02, 'ffn_w2': 1.249007e-01, 'norm_final': 6.389494e+01}


def _to_microbatches(a, axis):
    t = _jnp.moveaxis(a, axis, 0)
    t = t.reshape((N_MICROBATCH, t.shape[0] // N_MICROBATCH) + t.shape[1:])
    return _jnp.moveaxis(t, 1, axis + 1)


def setup_inputs(seed: int = 0) -> dict:
    inp = _fwd_setup_inputs(seed)
    key = _jax.random.fold_in(_jax.random.key(seed), 7919)
    shape, _ = _output_shape()
    out = dict(inp)
    out["loss_target"] = _jax.random.normal(_jax.random.fold_in(key, 0), shape, _jnp.float32)
    for i, name in enumerate(TWIN_WEIGHTS):
        w = inp[name].astype(_jnp.float32)
        if MOMENT_SCALE is None:
            s = _jnp.sqrt(_jnp.mean(_jnp.square(w)) + 1e-30)
        else:
            s = MOMENT_SCALE[name]
        km, kv = _jax.random.split(_jax.random.fold_in(key, i + 1))
        out[name] = w
        out["m_" + name] = s * _jax.random.normal(km, w.shape, _jnp.float32)
        out["v_" + name] = (s * s) * _jax.random.uniform(kv, w.shape, _jnp.float32, 0.5, 1.5)
    if N_MICROBATCH > 1:
        for name, axis in PER_EXAMPLE_BATCH_AXIS.items():
            out[name] = _to_microbatches(out[name], axis)
    return {'x': out['x'], 'norm_mix': out['norm_mix'], 'norm_ffn': out['norm_ffn'], 'attn_w_qkv': out['attn_w_qkv'], 'attn_b_qkv': out['attn_b_qkv'], 'attn_sinks': out['attn_sinks'], 'attn_w_o': out['attn_w_o'], 'attn_b_o': out['attn_b_o'], 'conv_w_pw1': out['conv_w_pw1'], 'conv_b_pw1': out['conv_b_pw1'], 'conv_w_dw': out['conv_w_dw'], 'conv_b_dw': out['conv_b_dw'], 'conv_ln_g': out['conv_ln_g'], 'conv_ln_b': out['conv_ln_b'], 'conv_w_pw2': out['conv_w_pw2'], 'conv_b_pw2': out['conv_b_pw2'], 'ffn_w1': out['ffn_w1'], 'ffn_w3': out['ffn_w3'], 'ffn_w2': out['ffn_w2'], 'norm_final': out['norm_final'], 'loss_target': out['loss_target'], 'm_norm_mix': out['m_norm_mix'], 'm_norm_ffn': out['m_norm_ffn'], 'm_attn_w_qkv': out['m_attn_w_qkv'], 'm_attn_b_qkv': out['m_attn_b_qkv'], 'm_attn_sinks': out['m_attn_sinks'], 'm_attn_w_o': out['m_attn_w_o'], 'm_attn_b_o': out['m_attn_b_o'], 'm_conv_w_pw1': out['m_conv_w_pw1'], 'm_conv_b_pw1': out['m_conv_b_pw1'], 'm_conv_w_dw': out['m_conv_w_dw'], 'm_conv_b_dw': out['m_conv_b_dw'], 'm_conv_ln_g': out['m_conv_ln_g'], 'm_conv_ln_b': out['m_conv_ln_b'], 'm_conv_w_pw2': out['m_conv_w_pw2'], 'm_conv_b_pw2': out['m_conv_b_pw2'], 'm_ffn_w1': out['m_ffn_w1'], 'm_ffn_w3': out['m_ffn_w3'], 'm_ffn_w2': out['m_ffn_w2'], 'm_norm_final': out['m_norm_final'], 'v_norm_mix': out['v_norm_mix'], 'v_norm_ffn': out['v_norm_ffn'], 'v_attn_w_qkv': out['v_attn_w_qkv'], 'v_attn_b_qkv': out['v_attn_b_qkv'], 'v_attn_sinks': out['v_attn_sinks'], 'v_attn_w_o': out['v_attn_w_o'], 'v_attn_b_o': out['v_attn_b_o'], 'v_conv_w_pw1': out['v_conv_w_pw1'], 'v_conv_b_pw1': out['v_conv_b_pw1'], 'v_conv_w_dw': out['v_conv_w_dw'], 'v_conv_b_dw': out['v_conv_b_dw'], 'v_conv_ln_g': out['v_conv_ln_g'], 'v_conv_ln_b': out['v_conv_ln_b'], 'v_conv_w_pw2': out['v_conv_w_pw2'], 'v_conv_b_pw2': out['v_conv_b_pw2'], 'v_ffn_w1': out['v_ffn_w1'], 'v_ffn_w3': out['v_ffn_w3'], 'v_ffn_w2': out['v_ffn_w2'], 'v_norm_final': out['v_norm_final']}


def _loss(weights, diff, rest, loss_target):
    with _jax.named_scope("forward"):
        args = {**rest, TWIN_DIFF_INPUT: diff, **{k: w.astype(_WEIGHT_DTYPES[k]) for k, w in weights.items()}}
        y = _forward(args)
    with _jax.named_scope("loss_head"):
        err = _jnp.square(y.astype(_jnp.float32) - loss_target)
        return 0.5 * _jnp.sum(_jnp.mean(err, axis=-1)) if err.ndim else 0.5 * err


def _adamw(w, g, m, v):
    m = ADAM_B1 * m + (1.0 - ADAM_B1) * g
    v = ADAM_B2 * v + (1.0 - ADAM_B2) * _jnp.square(g)
    m_hat = m / (1.0 - ADAM_B1 ** ADAM_STEP)
    v_hat = v / (1.0 - ADAM_B2 ** ADAM_STEP)
    delta = -ADAM_LR * (m_hat / (_jnp.sqrt(v_hat) + ADAM_EPS) + ADAM_WD * w)
    return delta, m, v


def reference(x, norm_mix, norm_ffn, attn_w_qkv, attn_b_qkv, attn_sinks, attn_w_o, attn_b_o, conv_w_pw1, conv_b_pw1, conv_w_dw, conv_b_dw, conv_ln_g, conv_ln_b, conv_w_pw2, conv_b_pw2, ffn_w1, ffn_w3, ffn_w2, norm_final, loss_target, m_norm_mix, m_norm_ffn, m_attn_w_qkv, m_attn_b_qkv, m_attn_sinks, m_attn_w_o, m_attn_b_o, m_conv_w_pw1, m_conv_b_pw1, m_conv_w_dw, m_conv_b_dw, m_conv_ln_g, m_conv_ln_b, m_conv_w_pw2, m_conv_b_pw2, m_ffn_w1, m_ffn_w3, m_ffn_w2, m_norm_final, v_norm_mix, v_norm_ffn, v_attn_w_qkv, v_attn_b_qkv, v_attn_sinks, v_attn_w_o, v_attn_b_o, v_conv_w_pw1, v_conv_b_pw1, v_conv_w_dw, v_conv_b_dw, v_conv_ln_g, v_conv_ln_b, v_conv_w_pw2, v_conv_b_pw2, v_ffn_w1, v_ffn_w3, v_ffn_w2, v_norm_final):
    given = dict(x=x, norm_mix=norm_mix, norm_ffn=norm_ffn, attn_w_qkv=attn_w_qkv, attn_b_qkv=attn_b_qkv, attn_sinks=attn_sinks, attn_w_o=attn_w_o, attn_b_o=attn_b_o, conv_w_pw1=conv_w_pw1, conv_b_pw1=conv_b_pw1, conv_w_dw=conv_w_dw, conv_b_dw=conv_b_dw, conv_ln_g=conv_ln_g, conv_ln_b=conv_ln_b, conv_w_pw2=conv_w_pw2, conv_b_pw2=conv_b_pw2, ffn_w1=ffn_w1, ffn_w3=ffn_w3, ffn_w2=ffn_w2, norm_final=norm_final, loss_target=loss_target, m_norm_mix=m_norm_mix, m_norm_ffn=m_norm_ffn, m_attn_w_qkv=m_attn_w_qkv, m_attn_b_qkv=m_attn_b_qkv, m_attn_sinks=m_attn_sinks, m_attn_w_o=m_attn_w_o, m_attn_b_o=m_attn_b_o, m_conv_w_pw1=m_conv_w_pw1, m_conv_b_pw1=m_conv_b_pw1, m_conv_w_dw=m_conv_w_dw, m_conv_b_dw=m_conv_b_dw, m_conv_ln_g=m_conv_ln_g, m_conv_ln_b=m_conv_ln_b, m_conv_w_pw2=m_conv_w_pw2, m_conv_b_pw2=m_conv_b_pw2, m_ffn_w1=m_ffn_w1, m_ffn_w3=m_ffn_w3, m_ffn_w2=m_ffn_w2, m_norm_final=m_norm_final, v_norm_mix=v_norm_mix, v_norm_ffn=v_norm_ffn, v_attn_w_qkv=v_attn_w_qkv, v_attn_b_qkv=v_attn_b_qkv, v_attn_sinks=v_attn_sinks, v_attn_w_o=v_attn_w_o, v_attn_b_o=v_attn_b_o, v_conv_w_pw1=v_conv_w_pw1, v_conv_b_pw1=v_conv_b_pw1, v_conv_w_dw=v_conv_w_dw, v_conv_b_dw=v_conv_b_dw, v_conv_ln_g=v_conv_ln_g, v_conv_ln_b=v_conv_ln_b, v_conv_w_pw2=v_conv_w_pw2, v_conv_b_pw2=v_conv_b_pw2, v_ffn_w1=v_ffn_w1, v_ffn_w3=v_ffn_w3, v_ffn_w2=v_ffn_w2, v_norm_final=v_norm_final)
    weights = {n: given[n] for n in TWIN_WEIGHTS}
    shared = {n: given[n] for n in SHARED_INPUTS}
    per_example = {n: given[n] for n in ['x']}
    grad_fn = _jax.value_and_grad(_loss, argnums=(0, 1))

    def one_microbatch(ex, loss_target):
        ex = dict(ex)
        diff = ex.pop(TWIN_DIFF_INPUT)
        return grad_fn(weights, diff, {**shared, **ex}, loss_target)

    if N_MICROBATCH == 1:
        loss, (grad_w, grad_x) = one_microbatch(per_example, given["loss_target"])
    else:
        def body(carry, xs):
            loss_sum, grad_sum = carry
            l_k, (gw_k, gx_k) = one_microbatch(xs[0], xs[1])
            with _jax.named_scope("update"):
                return (loss_sum + l_k, _jax.tree.map(_jnp.add, grad_sum, gw_k)), gx_k

        init = (_jnp.zeros((), _jnp.float32), _jax.tree.map(_jnp.zeros_like, weights))
        (loss, grad_w), grad_x = _jax.lax.scan(body, init, (per_example, given["loss_target"]))
    with _jax.named_scope("update"):
        delta_w, new_m, new_v = {}, {}, {}
        for n in TWIN_WEIGHTS:
            delta_w[n], new_m[n], new_v[n] = _adamw(weights[n], grad_w[n], given["m_" + n], given["v_" + n])
    return (loss, grad_x, *[grad_w[n] for n in TWIN_WEIGHTS], *[delta_w[n] for n in TWIN_WEIGHTS],
            *[new_m[n] for n in TWIN_WEIGHTS], *[new_v[n] for n in TWIN_WEIGHTS])
```

```python
import functools

import jax
import jax.numpy as jnp
from jax import lax
from jax.experimental import pallas as pl
from jax.experimental.pallas import tpu as pltpu

F32 = jnp.float32
BF16 = jnp.bfloat16
SDS = jax.ShapeDtypeStruct
MESH = pl.DeviceIdType.MESH

HEAD_DIM = 64
N_Q_HEADS = 16
N_KV_HEADS = 2
Q_PER_KV = N_Q_HEADS // N_KV_HEADS
ATTN_BLOCK = 128
ROPE_THETA = 10000.0
CONV_WIDTH = 31
CONV_HALO = 32
RMS_EPS = 1e-5
LN_EPS = 1e-5
ADAM_LR = 0.001
ADAM_B1 = 0.9
ADAM_B2 = 0.999
ADAM_EPS = 1e-08
ADAM_WD = 0.01
ADAM_STEP = 10

V7X_LANES = 128
V7X_VMEM_LIMIT_BYTES = 56 * 1024 * 1024

N_CHIPS = 4
N_DEV = 8
PACK_W = 1024

MASK_VALUE = -1e30


def _params(*semantics):
    return pltpu.CompilerParams(dimension_semantics=semantics, vmem_limit_bytes=V7X_VMEM_LIMIT_BYTES)


def _rows(tm, width):
    return pl.BlockSpec((tm, width), lambda i: (i, 0))


def _whole(shape):
    return pl.BlockSpec(shape, lambda *_: (0,) * len(shape))


def _rms_rstd(h):
    return lax.rsqrt(jnp.mean(h * h, axis=-1, keepdims=True) + RMS_EPS)


def _silu_and_grad(z):
    sg = jax.nn.sigmoid(z)
    return z * sg, sg * (1.0 + z * (1.0 - sg))


def _swap_rope_halves(t):
    w = t.shape[1]
    half = HEAD_DIM // 2
    lane = lax.broadcasted_iota(jnp.int32, t.shape, 1)
    upper = pltpu.roll(t, w - half, 1)
    lower = pltpu.roll(t, half, 1)
    return jnp.where(lane % HEAD_DIM < half, upper, lower)


def _rope(t, cos_ref, sin_ref):
    reps = t.shape[1] // V7X_LANES
    c = jnp.tile(cos_ref[...], (1, reps))
    s = jnp.tile(sin_ref[...], (1, reps))
    return t * c + _swap_rope_halves(t) * s


def _rope_transposed(dt, cos_ref, sin_ref):
    reps = dt.shape[1] // V7X_LANES
    c = jnp.tile(cos_ref[...], (1, reps))
    s = jnp.tile(sin_ref[...], (1, reps))
    return dt * c + _swap_rope_halves(dt * s)


def _rope_tables(seq_len):
    pos = jnp.arange(seq_len, dtype=F32)
    inv_freq = ROPE_THETA ** (-jnp.arange(0, HEAD_DIM, 2, dtype=F32) / HEAD_DIM)
    ang = pos[:, None] * inv_freq[None, :]
    cos, sin = jnp.cos(ang), jnp.sin(ang)
    cos_t = jnp.concatenate([cos, cos, cos, cos], axis=1)
    sin_t = jnp.concatenate([-sin, sin, -sin, sin], axis=1)
    return cos_t, sin_t


def _qkv_proj(h, g, w, b, cos, sin):
    T, D = h.shape
    N = w.shape[1]
    tm = min(512, T)
    rope_w = N - N_KV_HEADS * HEAD_DIM

    def body(h_ref, g_ref, w_ref, b_ref, cos_ref, sin_ref, y_ref, o_ref):
        hh = h_ref[...]
        y = (hh * _rms_rstd(hh) * g_ref[...]).astype(BF16)
        y_ref[...] = y
        acc = jnp.dot(y, w_ref[...], preferred_element_type=F32) + b_ref[...]
        o_ref[:, :rope_w] = _rope(acc[:, :rope_w], cos_ref, sin_ref).astype(BF16)
        o_ref[:, rope_w:] = acc[:, rope_w:].astype(BF16)

    return pl.pallas_call(
        body, name="qkv_proj", grid=(T // tm,),
        in_specs=[_rows(tm, D), _whole((1, D)), _whole((D, N)), _whole((1, N)),
                  _rows(tm, V7X_LANES), _rows(tm, V7X_LANES)],
        out_specs=[_rows(tm, D), _rows(tm, N)],
        out_shape=[SDS((T, D), BF16), SDS((T, N), BF16)],
        compiler_params=_params("parallel"),
    )(h, g, w, b, cos, sin)


def _norm_proj(h, g, w, b):
    T, D = h.shape
    N = w.shape[1]
    tm = min(512, T)

    def body(h_ref, g_ref, w_ref, b_ref, y_ref, o_ref):
        hh = h_ref[...]
        y = (hh * _rms_rstd(hh) * g_ref[...]).astype(BF16)
        y_ref[...] = y
        o_ref[...] = jnp.dot(y, w_ref[...], preferred_element_type=F32) + b_ref[...]

    return pl.pallas_call(
        body, name="pw1_proj", grid=(T // tm,),
        in_specs=[_rows(tm, D), _whole((1, D)), _whole((D, N)), _whole((1, N))],
        out_specs=[_rows(tm, D), _rows(tm, N)],
        out_shape=[SDS((T, D), BF16), SDS((T, N), F32)],
        compiler_params=_params("parallel"),
    )(h, g, w, b)


def _band_mask(n):
    row = lax.broadcasted_iota(jnp.int32, (ATTN_BLOCK, 2 * ATTN_BLOCK), 0)
    col = lax.broadcasted_iota(jnp.int32, (ATTN_BLOCK, 2 * ATTN_BLOCK), 1)
    band = (col > row) & (col <= row + ATTN_BLOCK) & ((col >= ATTN_BLOCK) | (n > 0))
    return jnp.concatenate([band] * Q_PER_KV, axis=0)


def _group_operands(g, q_ref, kc_ref, kp_ref, vc_ref, vp_ref, sink_ref):
    lo = g * HEAD_DIM
    k = jnp.concatenate([kp_ref[:, lo:lo + HEAD_DIM], kc_ref[:, lo:lo + HEAD_DIM]], axis=0)
    v = jnp.concatenate([vp_ref[:, lo:lo + HEAD_DIM], vc_ref[:, lo:lo + HEAD_DIM]], axis=0)
    heads = range(g * Q_PER_KV, (g + 1) * Q_PER_KV)
    q = jnp.concatenate([q_ref[:, h * HEAD_DIM:(h + 1) * HEAD_DIM] for h in heads], axis=0)
    sink = jnp.concatenate(
        [jnp.broadcast_to(sink_ref[0:1, h:h + 1], (ATTN_BLOCK, 1)) for h in heads], axis=0)
    return q, k, v, sink


def _softmax_with_sink(q, k, sink, mask):
    s = lax.dot_general(q, k, (((1,), (1,)), ((), ())), preferred_element_type=F32) * (HEAD_DIM ** -0.5)
    s = jnp.where(mask, s, MASK_VALUE)
    m = jnp.maximum(jnp.max(s, axis=1, keepdims=True), sink)
    p = jnp.exp(s - m)
    e_sink = jnp.exp(sink - m)
    inv = 1.0 / (jnp.sum(p, axis=1, keepdims=True) + e_sink)
    return p * inv, e_sink * inv


def _attn_specs(T):
    nb = T // ATTN_BLOCK
    kcol = N_Q_HEADS * HEAD_DIM // V7X_LANES
    cur = lambda n: jnp.minimum(n, nb - 1)
    prev = lambda n: jnp.maximum(jnp.minimum(n, nb - 1) - 1, 0)
    q_spec = pl.BlockSpec((ATTN_BLOCK, N_Q_HEADS * HEAD_DIM), lambda n: (cur(n), 0))
    kc_spec = pl.BlockSpec((ATTN_BLOCK, V7X_LANES), lambda n: (cur(n), kcol))
    kp_spec = pl.BlockSpec((ATTN_BLOCK, V7X_LANES), lambda n: (prev(n), kcol))
    vc_spec = pl.BlockSpec((ATTN_BLOCK, V7X_LANES), lambda n: (cur(n), kcol + 1))
    vp_spec = pl.BlockSpec((ATTN_BLOCK, V7X_LANES), lambda n: (prev(n), kcol + 1))
    return q_spec, kc_spec, kp_spec, vc_spec, vp_spec


def _attn_fwd(qkv, sinks):
    T = qkv.shape[0]
    nb = T // ATTN_BLOCK
    qw = N_Q_HEADS * HEAD_DIM

    def body(q_ref, kc_ref, kp_ref, vc_ref, vp_ref, sink_ref, o_ref):
        mask = _band_mask(pl.program_id(0))
        for g in range(N_KV_HEADS):
            q, k, v, sink = _group_operands(g, q_ref, kc_ref, kp_ref, vc_ref, vp_ref, sink_ref)
            probs, _ = _softmax_with_sink(q, k, sink, mask)
            o = jnp.dot(probs.astype(BF16), v, preferred_element_type=F32)
            for i in range(Q_PER_KV):
                h = g * Q_PER_KV + i
                o_ref[:, h * HEAD_DIM:(h + 1) * HEAD_DIM] = o[i * ATTN_BLOCK:(i + 1) * ATTN_BLOCK].astype(BF16)

    return pl.pallas_call(
        body, name="attn_fwd", grid=(nb,),
        in_specs=[*_attn_specs(T), _whole((1, N_Q_HEADS))],
        out_specs=_rows(ATTN_BLOCK, qw),
        out_shape=SDS((T, qw), BF16),
        compiler_params=_params("parallel"),
    )(qkv, qkv, qkv, qkv, qkv, sinks)


def _mm_res(name, a, w, b, res):
    T, K = a.shape
    D = w.shape[1]
    tm = min(512, T)

    def body(a_ref, w_ref, b_ref, r_ref, o_ref):
        o_ref[...] = jnp.dot(a_ref[...], w_ref[...], preferred_element_type=F32) + b_ref[...] + r_ref[...]

    return pl.pallas_call(
        body, name=name, grid=(T // tm,),
        in_specs=[_rows(tm, K), _whole((K, D)), _whole((1, D)), _rows(tm, D)],
        out_specs=_rows(tm, D),
        out_shape=SDS((T, D), F32),
        compiler_params=_params("parallel"),
    )(a, w, b, res)


def _ffn_col_tile(dff):
    return dff // 2 if (dff // 2) % V7X_LANES == 0 else dff


def _ffn_up(name, h, g, w1, w3):
    T, D = h.shape
    DFF = w1.shape[1]
    tm = min(512, T)
    tn = _ffn_col_tile(DFF)

    def body(h_ref, g_ref, w1_ref, w3_ref, f_ref, g1_ref, g3_ref, s_ref):
        @pl.when(pl.program_id(1) == 0)
        def _():
            hh = h_ref[...]
            f_ref[...] = (hh * _rms_rstd(hh) * g_ref[...]).astype(BF16)

        f = f_ref[...]
        a = jnp.dot(f, w1_ref[...], preferred_element_type=F32)
        b = jnp.dot(f, w3_ref[...], preferred_element_type=F32)
        g1_ref[...] = a.astype(BF16)
        g3_ref[...] = b.astype(BF16)
        s_ref[...] = (a * jax.nn.sigmoid(a) * b).astype(BF16)

    col = pl.BlockSpec((tm, tn), lambda i, j: (i, j))
    wcol = pl.BlockSpec((D, tn), lambda i, j: (0, j))
    return pl.pallas_call(
        body, name=name, grid=(T // tm, DFF // tn),
        in_specs=[pl.BlockSpec((tm, D), lambda i, j: (i, 0)), pl.BlockSpec((1, D), lambda i, j: (0, 0)), wcol, wcol],
        out_specs=[pl.BlockSpec((tm, D), lambda i, j: (i, 0)), col, col, col],
        out_shape=[SDS((T, D), BF16), SDS((T, DFF), BF16), SDS((T, DFF), BF16), SDS((T, DFF), BF16)],
        compiler_params=_params("parallel", "arbitrary"),
    )(h, g, w1, w3)


def _glu(a, d):
    return a[:, :d] * jax.nn.sigmoid(a[:, d:])


def _conv_tile(T):
    return min(256, T)


def _conv_fwd(a, w_dw, b_dw, ln_g, ln_b):
    T = a.shape[0]
    D = a.shape[1] // 2
    tc = _conv_tile(T)
    per = tc // CONV_HALO

    def body(a_ref, ah_ref, w_ref, bdw_ref, lg_ref, lb_ref, c_ref, act_ref, u_scr):
        i = pl.program_id(0)
        u_scr[0:CONV_HALO, :] = jnp.where(i > 0, _glu(ah_ref[...], D), 0.0)
        u_scr[CONV_HALO:, :] = _glu(a_ref[...], D)
        acc = jnp.zeros((tc, D), F32)
        for j in range(CONV_WIDTH):
            acc = acc + u_scr[pl.ds(CONV_HALO - CONV_WIDTH + 1 + j, tc), :] * w_ref[j:j + 1, :]
        c = acc + bdw_ref[...]
        c_ref[...] = c
        xc = c - jnp.mean(c, axis=-1, keepdims=True)
        z = xc * lax.rsqrt(jnp.mean(xc * xc, axis=-1, keepdims=True) + LN_EPS)
        l = z * lg_ref[...] + lb_ref[...]
        act_ref[...] = (l * jax.nn.sigmoid(l)).astype(BF16)

    return pl.pallas_call(
        body, name="conv_fwd", grid=(T // tc,),
        in_specs=[_rows(tc, 2 * D),
                  pl.BlockSpec((CONV_HALO, 2 * D), lambda i: (jnp.maximum(i * per - 1, 0), 0)),
                  _whole((CONV_WIDTH, D)), _whole((1, D)), _whole((1, D)), _whole((1, D))],
        out_specs=[_rows(tc, D), _rows(tc, D)],
        out_shape=[SDS((T, D), F32), SDS((T, D), BF16)],
        scratch_shapes=[pltpu.VMEM((tc + CONV_HALO, D), F32)],
        compiler_params=_params("parallel"),
    )(a, a, w_dw, b_dw, ln_g, ln_b)


def _final_loss(h, g, target):
    T, D = h.shape
    tm = min(512, T)

    def body(h_ref, g_ref, t_ref, dh_ref, loss_ref, dg_ref):
        @pl.when(pl.program_id(0) == 0)
        def _():
            loss_ref[...] = jnp.zeros_like(loss_ref)
            dg_ref[...] = jnp.zeros_like(dg_ref)

        hh = h_ref[...]
        r = _rms_rstd(hh)
        g = g_ref[...]
        d = hh * r * g - t_ref[...]
        loss_ref[...] += 0.5 * jnp.sum(jnp.mean(d * d, axis=-1, keepdims=True), axis=0, keepdims=True)
        dout = d * (1.0 / D)
        dg_ref[...] += jnp.sum(dout * (hh * r), axis=0, keepdims=True)
        dxh = dout * g
        dh_ref[...] = r * dxh - hh * (r * r * r) * jnp.mean(dxh * hh, axis=-1, keepdims=True)

    return pl.pallas_call(
        body, name="final_loss", grid=(T // tm,),
        in_specs=[_rows(tm, D), _whole((1, D)), _rows(tm, D)],
        out_specs=[_rows(tm, D), _whole((1, 1)), _whole((1, D))],
        out_shape=[SDS((T, D), F32), SDS((1, 1), F32), SDS((1, D), F32)],
        compiler_params=_params("arbitrary"),
    )(h, g, target)


def _ffn_bwd_act(name, dh, w2, g1, g3):
    T, D = dh.shape
    DFF = w2.shape[0]
    tm = min(512, T)
    tn = _ffn_col_tile(DFF)

    def body(dh_ref, w2_ref, g1_ref, g3_ref, dg1_ref, dg3_ref):
        ds = lax.dot_general(dh_ref[...].astype(BF16), w2_ref[...], (((1,), (1,)), ((), ())),
                             preferred_element_type=F32)
        act, dact = _silu_and_grad(g1_ref[...].astype(F32))
        dg1_ref[...] = (ds * g3_ref[...].astype(F32) * dact).astype(BF16)
        dg3_ref[...] = (ds * act).astype(BF16)

    col = pl.BlockSpec((tm, tn), lambda j, i: (i, j))
    return pl.pallas_call(
        body, name=name, grid=(DFF // tn, T // tm),
        in_specs=[pl.BlockSpec((tm, D), lambda j, i: (i, 0)), pl.BlockSpec((tn, D), lambda j, i: (j, 0)), col, col],
        out_specs=[col, col],
        out_shape=[SDS((T, DFF), BF16), SDS((T, DFF), BF16)],
        compiler_params=_params("arbitrary", "arbitrary"),
    )(dh, w2, g1, g3)


def _grad_tile(n):
    return n // 2 if n > 1408 and (n // 2) % V7X_LANES == 0 else n


def _mm_tn(name, a, b):
    T, KA = a.shape
    N = b.shape[1]
    tt = min(512, T)
    tk = _grad_tile(KA)
    tn = _grad_tile(N)

    def body(a_ref, b_ref, o_ref):
        @pl.when(pl.program_id(2) == 0)
        def _():
            o_ref[...] = jnp.zeros_like(o_ref)

        o_ref[...] += lax.dot_general(a_ref[...].astype(BF16), b_ref[...].astype(BF16),
                                      (((0,), (0,)), ((), ())), preferred_element_type=F32)

    return pl.pallas_call(
        body, name=name, grid=(KA // tk, N // tn, T // tt),
        in_specs=[pl.BlockSpec((tt, tk), lambda k, n, t: (t, k)), pl.BlockSpec((tt, tn), lambda k, n, t: (t, n))],
        out_specs=pl.BlockSpec((tk, tn), lambda k, n, t: (k, n)),
        out_shape=SDS((KA, N), F32),
        compiler_params=_params("parallel", "parallel", "arbitrary"),
    )(a, b)


def _mm_nt_normbwd(name, pairs, h, g, dh):
    T, D = h.shape
    tm = min(256, T)
    n_pairs = len(pairs)
    widths = [dy.shape[1] for dy, _ in pairs]

    def body(*refs):
        dy_refs = refs[:n_pairs]
        w_refs = refs[n_pairs:2 * n_pairs]
        h_ref, g_ref, dh_ref, o_ref, dg_ref, cs_ref = refs[2 * n_pairs:]

        @pl.when(pl.program_id(0) == 0)
        def _():
            dg_ref[...] = jnp.zeros_like(dg_ref)
            cs_ref[...] = jnp.zeros_like(cs_ref)

        df = None
        for dy_ref, w_ref in zip(dy_refs, w_refs):
            part = lax.dot_general(dy_ref[...].astype(BF16), w_ref[...], (((1,), (1,)), ((), ())),
                                   preferred_element_type=F32)
            df = part if df is None else df + part
        hh = h_ref[...]
        r = _rms_rstd(hh)
        dg_ref[...] += jnp.sum(df * (hh * r), axis=0, keepdims=True)
        dxh = df * g_ref[...]
        out = dh_ref[...] + (r * dxh - hh * (r * r * r) * jnp.mean(dxh * hh, axis=-1, keepdims=True))
        o_ref[...] = out
        cs_ref[...] += jnp.sum(out, axis=0, keepdims=True)

    return pl.pallas_call(
        body, name=name, grid=(T // tm,),
        in_specs=[*[_rows(tm, n) for n in widths], *[_whole((D, n)) for n in widths],
                  _rows(tm, D), _whole((1, D)), _rows(tm, D)],
        out_specs=[_rows(tm, D), _whole((1, D)), _whole((1, D))],
        out_shape=[SDS((T, D), F32), SDS((1, D), F32), SDS((1, D), F32)],
        compiler_params=_params("arbitrary"),
    )(*[dy for dy, _ in pairs], *[w for _, w in pairs], h, g, dh)


def _mm_nt(name, dy, w, out_dtype):
    T, N = dy.shape
    K = w.shape[0]
    tm = min(512, T)

    def body(dy_ref, w_ref, o_ref):
        o_ref[...] = lax.dot_general(dy_ref[...].astype(BF16), w_ref[...], (((1,), (1,)), ((), ())),
                                     preferred_element_type=F32).astype(out_dtype)

    return pl.pallas_call(
        body, name=name, grid=(T // tm,),
        in_specs=[_rows(tm, N), _whole((K, N))],
        out_specs=_rows(tm, K),
        out_shape=SDS((T, K), out_dtype),
        compiler_params=_params("parallel"),
    )(dy, w)


def _conv_bwd(dact, c, a, w_dw, ln_g, ln_b):
    T, D = c.shape
    tc = _conv_tile(T)
    per = tc // CONV_HALO
    n_tiles = T // tc
    last_halo = T // CONV_HALO - 1
    first_tap = CONV_HALO - CONV_WIDTH + 1

    def ln_bwd(dact_v, c_v, lg, lb):
        xc = c_v - jnp.mean(c_v, axis=-1, keepdims=True)
        rstd = lax.rsqrt(jnp.mean(xc * xc, axis=-1, keepdims=True) + LN_EPS)
        z = xc * rstd
        _, dsilu = _silu_and_grad(z * lg + lb)
        dl = dact_v * dsilu
        dz = dl * lg
        dc = rstd * (dz - jnp.mean(dz, axis=-1, keepdims=True) - z * jnp.mean(dz * z, axis=-1, keepdims=True))
        return dc, dl, z

    def body(dact_ref, dactn_ref, c_ref, cn_ref, a_ref, ah_ref, w_ref, lg_ref, lb_ref,
             da_ref, dlg_ref, dlb_ref, dbdw_ref, dwdw_ref, dbpw1_ref, dc_scr, u_scr):
        i = pl.program_id(0)

        @pl.when(i == 0)
        def _():
            for ref in (dlg_ref, dlb_ref, dbdw_ref, dwdw_ref, dbpw1_ref):
                ref[...] = jnp.zeros_like(ref)

        lg, lb = lg_ref[...], lb_ref[...]
        dc, dl, z = ln_bwd(dact_ref[...], c_ref[...], lg, lb)
        dlg_ref[...] += jnp.sum(dl * z, axis=0, keepdims=True)
        dlb_ref[...] += jnp.sum(dl, axis=0, keepdims=True)
        dbdw_ref[...] += jnp.sum(dc, axis=0, keepdims=True)
        dcn, _, _ = ln_bwd(dactn_ref[...], cn_ref[...], lg, lb)
        dc_scr[0:tc, :] = dc
        dc_scr[tc:, :] = jnp.where(i < n_tiles - 1, dcn, 0.0)

        a_v = a_ref[...]
        a1 = a_v[:, :D]
        sg = jax.nn.sigmoid(a_v[:, D:])
        u_scr[0:CONV_HALO, :] = jnp.where(i > 0, _glu(ah_ref[...], D), 0.0)
        u_scr[CONV_HALO:, :] = a1 * sg

        du = jnp.zeros((tc, D), F32)
        for j in range(CONV_WIDTH):
            du = du + dc_scr[pl.ds(CONV_WIDTH - 1 - j, tc), :] * w_ref[j:j + 1, :]
            dwdw_ref[j:j + 1, :] += jnp.sum(dc * u_scr[pl.ds(first_tap + j, tc), :], axis=0, keepdims=True)

        da1 = du * sg
        da2 = du * a1 * sg * (1.0 - sg)
        da_ref[:, :D] = da1.astype(BF16)
        da_ref[:, D:] = da2.astype(BF16)
        dbpw1_ref[:, :D] += jnp.sum(da1, axis=0, keepdims=True)
        dbpw1_ref[:, D:] += jnp.sum(da2, axis=0, keepdims=True)

    nxt = lambda i: (jnp.minimum((i + 1) * per, last_halo), 0)
    return pl.pallas_call(
        body, name="conv_bwd", grid=(n_tiles,),
        in_specs=[_rows(tc, D), pl.BlockSpec((CONV_HALO, D), nxt),
                  _rows(tc, D), pl.BlockSpec((CONV_HALO, D), nxt),
                  _rows(tc, 2 * D),
                  pl.BlockSpec((CONV_HALO, 2 * D), lambda i: (jnp.maximum(i * per - 1, 0), 0)),
                  _whole((CONV_WIDTH, D)), _whole((1, D)), _whole((1, D))],
        out_specs=[_rows(tc, 2 * D), _whole((1, D)), _whole((1, D)), _whole((1, D)),
                   _whole((CONV_HALO, D)), _whole((1, 2 * D))],
        out_shape=[SDS((T, 2 * D), BF16), SDS((1, D), F32), SDS((1, D), F32), SDS((1, D), F32),
                   SDS((CONV_HALO, D), F32), SDS((1, 2 * D), F32)],
        scratch_shapes=[pltpu.VMEM((tc + CONV_HALO, D), F32), pltpu.VMEM((tc + CONV_HALO, D), F32)],
        compiler_params=_params("arbitrary"),
    )(dact, dact, c, c, a, a, w_dw, ln_g, ln_b)


def _attn_bwd(qkv, dao, cos, sin, sinks):
    T = qkv.shape[0]
    nb = T // ATTN_BLOCK
    qw = N_Q_HEADS * HEAD_DIM
    kw = N_KV_HEADS * HEAD_DIM

    def body(q_ref, kc_ref, kp_ref, vc_ref, vp_ref, do_ref, cos_ref, sin_ref, cosp_ref, sinp_ref, sink_ref,
             dq_ref, dkv_ref, dsink_ref, dbq_ref, dbkv_ref, carry, prev_scr, cur_scr, dq_scr):
        n = pl.program_id(0)

        @pl.when(n == 0)
        def _():
            for ref in (dsink_ref, dbq_ref, dbkv_ref, carry):
                ref[...] = jnp.zeros_like(ref)

        @pl.when(n == nb)
        def _():
            prev_scr[...] = jnp.zeros_like(prev_scr)

        @pl.when(n < nb)
        def _():
            mask = _band_mask(n)
            for g in range(N_KV_HEADS):
                q, k, v, sink = _group_operands(g, q_ref, kc_ref, kp_ref, vc_ref, vp_ref, sink_ref)
                heads = range(g * Q_PER_KV, (g + 1) * Q_PER_KV)
                do = jnp.concatenate([do_ref[:, h * HEAD_DIM:(h + 1) * HEAD_DIM] for h in heads], axis=0)
                probs, p_sink = _softmax_with_sink(q, k, sink, mask)
                dp = lax.dot_general(do, v, (((1,), (1,)), ((), ())), preferred_element_type=F32)
                delta = jnp.sum(probs * dp, axis=1, keepdims=True)
                ds = (probs * (dp - delta) * (HEAD_DIM ** -0.5)).astype(BF16)
                dsink_rows = -(p_sink * delta)
                dq = jnp.dot(ds, k, preferred_element_type=F32)
                dk = lax.dot_general(ds, q, (((0,), (0,)), ((), ())), preferred_element_type=F32)
                dv = lax.dot_general(probs.astype(BF16), do, (((0,), (0,)), ((), ())), preferred_element_type=F32)
                for i, h in enumerate(heads):
                    rows = slice(i * ATTN_BLOCK, (i + 1) * ATTN_BLOCK)
                    dq_scr[:, h * HEAD_DIM:(h + 1) * HEAD_DIM] = dq[rows]
                    dsink_ref[:, h:h + 1] += jnp.sum(dsink_rows[rows], axis=0, keepdims=True)
                lo = g * HEAD_DIM
                prev_scr[:, lo:lo + HEAD_DIM] = dk[:ATTN_BLOCK]
                cur_scr[:, lo:lo + HEAD_DIM] = dk[ATTN_BLOCK:]
                prev_scr[:, kw + lo:kw + lo + HEAD_DIM] = dv[:ATTN_BLOCK]
                cur_scr[:, kw + lo:kw + lo + HEAD_DIM] = dv[ATTN_BLOCK:]
            dq_pre = _rope_transposed(dq_scr[...], cos_ref, sin_ref)
            dq_ref[...] = dq_pre.astype(BF16)
            dbq_ref[...] += jnp.sum(dq_pre, axis=0, keepdims=True)

        tot = carry[...] + prev_scr[...]
        dk_pre = _rope_transposed(tot[:, :kw], cosp_ref, sinp_ref)
        dkv_ref[:, :kw] = dk_pre.astype(BF16)
        dkv_ref[:, kw:] = tot[:, kw:].astype(BF16)
        dbkv_ref[:, :kw] += jnp.sum(dk_pre, axis=0, keepdims=True)
        dbkv_ref[:, kw:] += jnp.sum(tot[:, kw:], axis=0, keepdims=True)

        @pl.when(n < nb)
        def _():
            carry[...] = cur_scr[...]

    cur = lambda n: (jnp.minimum(n, nb - 1), 0)
    out_lag = lambda n: (jnp.maximum(n - 1, 0), 0)
    return pl.pallas_call(
        body, name="attn_bwd", grid=(nb + 1,),
        in_specs=[*_attn_specs(T),
                  pl.BlockSpec((ATTN_BLOCK, qw), cur),
                  pl.BlockSpec((ATTN_BLOCK, V7X_LANES), cur), pl.BlockSpec((ATTN_BLOCK, V7X_LANES), cur),
                  pl.BlockSpec((ATTN_BLOCK, V7X_LANES), out_lag), pl.BlockSpec((ATTN_BLOCK, V7X_LANES), out_lag),
                  _whole((1, N_Q_HEADS))],
        out_specs=[pl.BlockSpec((ATTN_BLOCK, qw), cur), pl.BlockSpec((ATTN_BLOCK, 2 * kw), out_lag),
                   _whole((1, N_Q_HEADS)), _whole((1, qw)), _whole((1, 2 * kw))],
        out_shape=[SDS((T, qw), BF16), SDS((T, 2 * kw), BF16),
                   SDS((1, N_Q_HEADS), F32), SDS((1, qw), F32), SDS((1, 2 * kw), F32)],
        scratch_shapes=[pltpu.VMEM((ATTN_BLOCK, 2 * kw), F32), pltpu.VMEM((ATTN_BLOCK, 2 * kw), F32),
                        pltpu.VMEM((ATTN_BLOCK, 2 * kw), F32), pltpu.VMEM((ATTN_BLOCK, qw), F32)],
        compiler_params=_params("arbitrary"),
    )(qkv, qkv, qkv, qkv, qkv, dao, cos, sin, cos, sin, sinks)


def _local_step(x, target, p):
    T, D = x.shape
    cos, sin = _rope_tables(T)
    qw = N_Q_HEADS * HEAD_DIM
    nm, nf = p["norm_mix"], p["norm_ffn"]
    zero_b = jnp.zeros((1, D), F32)

    y0, qkv = _qkv_proj(x, nm[0:1], p["attn_w_qkv"], p["attn_b_qkv"], cos, sin)
    ao = _attn_fwd(qkv, p["attn_sinks"])
    h1 = _mm_res("attn_out", ao, p["attn_w_o"], p["attn_b_o"], x)
    f0, g1a, g3a, s0 = _ffn_up("ffn0_up", h1, nf[0:1], p["ffn_w1"][0], p["ffn_w3"][0])
    h2 = _mm_res("ffn0_down", s0, p["ffn_w2"][0], zero_b, h1)
    y1, a = _norm_proj(h2, nm[1:2], p["conv_w_pw1"], p["conv_b_pw1"])
    c, act = _conv_fwd(a, p["conv_w_dw"], p["conv_b_dw"], p["conv_ln_g"], p["conv_ln_b"])
    h3 = _mm_res("conv_out", act, p["conv_w_pw2"], p["conv_b_pw2"], h2)
    f1, g1b, g3b, s1 = _ffn_up("ffn1_up", h3, nf[1:2], p["ffn_w1"][1], p["ffn_w3"][1])
    h4 = _mm_res("ffn1_down", s1, p["ffn_w2"][1], zero_b, h3)
    dh4, loss, d_norm_final = _final_loss(h4, p["norm_final"], target)

    g = {}
    dg1, dg3 = _ffn_bwd_act("ffn1_bwd_act", dh4, p["ffn_w2"][1], g1b, g3b)
    dw2_1 = _mm_tn("ffn1_dw2", s1, dh4)
    dw1_1 = _mm_tn("ffn1_dw1", f1, dg1)
    dw3_1 = _mm_tn("ffn1_dw3", f1, dg3)
    dh3, dnf1, db_pw2 = _mm_nt_normbwd("ffn1_bwd_in", [(dg1, p["ffn_w1"][1]), (dg3, p["ffn_w3"][1])], h3, nf[1:2], dh4)

    g["conv_w_pw2"] = _mm_tn("conv_dw_pw2", act, dh3)
    dact = _mm_nt("conv_bwd_out", dh3, p["conv_w_pw2"], F32)
    da, d_ln_g, d_ln_b, d_b_dw, d_w_dw, d_b_pw1 = _conv_bwd(dact, c, a, p["conv_w_dw"], p["conv_ln_g"], p["conv_ln_b"])
    g["conv_w_pw1"] = _mm_tn("conv_dw_pw1", y1, da)
    dh2, dnm1, _ = _mm_nt_normbwd("conv_bwd_in", [(da, p["conv_w_pw1"])], h2, nm[1:2], dh3)

    dg1, dg3 = _ffn_bwd_act("ffn0_bwd_act", dh2, p["ffn_w2"][0], g1a, g3a)
    dw2_0 = _mm_tn("ffn0_dw2", s0, dh2)
    dw1_0 = _mm_tn("ffn0_dw1", f0, dg1)
    dw3_0 = _mm_tn("ffn0_dw3", f0, dg3)
    dh1, dnf0, db_o = _mm_nt_normbwd("ffn0_bwd_in", [(dg1, p["ffn_w1"][0]), (dg3, p["ffn_w3"][0])], h1, nf[0:1], dh2)

    g["attn_w_o"] = _mm_tn("attn_dw_o", ao, dh1)
    dao = _mm_nt("attn_bwd_out", dh1, p["attn_w_o"], BF16)
    dq, dkv, d_sinks, dbq, dbkv = _attn_bwd(qkv, dao, cos, sin, p["attn_sinks"])
    dwq = _mm_tn("attn_dw_q", y0, dq)
    dwkv = _mm_tn("attn_dw_kv", y0, dkv)
    wqkv = p["attn_w_qkv"]
    dx, dnm0, _ = _mm_nt_normbwd("attn_bwd_in", [(dq, wqkv[:, :qw]), (dkv, wqkv[:, qw:])], x, nm[0:1], dh1)

    g["norm_mix"] = jnp.concatenate([dnm0, dnm1], axis=0)
    g["norm_ffn"] = jnp.concatenate([dnf0, dnf1], axis=0)
    g["attn_w_qkv"] = jnp.concatenate([dwq, dwkv], axis=1)
    g["attn_b_qkv"] = jnp.concatenate([dbq, dbkv], axis=1)
    g["attn_sinks"] = d_sinks
    g["attn_b_o"] = db_o
    g["conv_b_pw1"] = d_b_pw1
    g["conv_w_dw"] = d_w_dw[:CONV_WIDTH]
    g["conv_b_dw"] = d_b_dw
    g["conv_ln_g"] = d_ln_g
    g["conv_ln_b"] = d_ln_b
    g["conv_b_pw2"] = db_pw2
    g["ffn_w1"] = jnp.stack([dw1_0, dw1_1])
    g["ffn_w3"] = jnp.stack([dw3_0, dw3_1])
    g["ffn_w2"] = jnp.stack([dw2_0, dw2_1])
    g["norm_final"] = d_norm_final
    return loss, dx, g


ANY = pl.BlockSpec(memory_space=pl.ANY)
VMEM_WHOLE = pl.BlockSpec(memory_space=pltpu.VMEM)


def _my_place():
    return lax.axis_index("x"), lax.axis_index("y"), lax.axis_index("c")


def _other_chips(x, y):
    places = [(1 - x, y), (x, 1 - y), (1 - x, 1 - y)]
    return [(bx, by, 2 * bx + by) for bx, by in places]


def _exchange_small(name, v, reduce):
    r, w = v.shape

    def body(v_ref, o_ref, *rest):
        if reduce:
            buf, send_sems, recv_sems = rest
        else:
            buf = o_ref
            send_sems, recv_sems = rest
        x, y, c = _my_place()
        me = 4 * x + 2 * y + c
        sends = []
        for k in range(1, N_DEV):
            peer = (1 - x if k & 4 else x, 1 - y if k & 2 else y, 1 - c if k & 1 else c)
            cp = pltpu.make_async_remote_copy(
                src_ref=v_ref, dst_ref=buf.at[me], send_sem=send_sems.at[k - 1], recv_sem=recv_sems.at[k - 1],
                device_id=peer, device_id_type=MESH)
            cp.start()
            sends.append(cp)
        buf[me] = v_ref[...]
        for k in range(1, N_DEV):
            src = 4 * (1 - x if k & 4 else x) + 2 * (1 - y if k & 2 else y) + (1 - c if k & 1 else c)
            pltpu.make_async_remote_copy(
                src_ref=v_ref, dst_ref=buf.at[src], send_sem=send_sems.at[k - 1], recv_sem=recv_sems.at[k - 1],
                device_id=(x, y, c), device_id_type=MESH).wait_recv()
        for cp in sends:
            cp.wait_send()
        if reduce:
            acc = buf[0]
            for d in range(1, N_DEV):
                acc = acc + buf[d]
            o_ref[...] = acc

    sems = [pltpu.SemaphoreType.DMA((N_DEV - 1,)), pltpu.SemaphoreType.DMA((N_DEV - 1,))]
    if reduce:
        out_shape = SDS((r, w), F32)
        scratch = [pltpu.VMEM((N_DEV, r, w), F32)] + sems
    else:
        out_shape = SDS((N_DEV, r, w), F32)
        scratch = sems
    return pl.pallas_call(
        body, name=name, out_shape=out_shape, in_specs=[VMEM_WHOLE], out_specs=VMEM_WHOLE,
        scratch_shapes=scratch,
        compiler_params=pltpu.CompilerParams(vmem_limit_bytes=V7X_VMEM_LIMIT_BYTES),
    )(v)


def _gather_weights(shard):
    R, w = shard.shape
    half = R // 2

    def body(w_ref, o_ref, send_sems, recv_sems, local_sem):
        x, y, c = _my_place()
        k = 2 * x + y
        mine = pl.ds(pl.multiple_of(c * half, 16), half)
        theirs = pl.ds(pl.multiple_of((1 - c) * half, 16), half)
        chips = _other_chips(x, y)

        def copy(sem, src, dst, to):
            return pltpu.make_async_remote_copy(
                src_ref=src, dst_ref=dst, send_sem=send_sems.at[sem], recv_sem=recv_sems.at[sem],
                device_id=to, device_id_type=MESH)

        own = pltpu.make_async_copy(w_ref, o_ref.at[k], local_sem)
        own.start()
        sends = [copy(j, w_ref.at[mine], o_ref.at[k, mine], (bx, by, c)) for j, (bx, by, _) in enumerate(chips)]
        for cp in sends:
            cp.start()
        for j, (_, _, kb) in enumerate(chips):
            landed = o_ref.at[kb, mine]
            copy(j, landed, landed, (x, y, c)).wait_recv()
            fwd = copy(3 + j, landed, landed, (x, y, 1 - c))
            fwd.start()
            sends.append(fwd)
        for j, (_, _, kb) in enumerate(chips):
            landed = o_ref.at[kb, theirs]
            copy(3 + j, landed, landed, (x, y, c)).wait_recv()
        for cp in sends:
            cp.wait_send()
        own.wait()

    return pl.pallas_call(
        body, name="gather_weights", out_shape=SDS((N_CHIPS, R, w), shard.dtype),
        in_specs=[ANY], out_specs=ANY,
        scratch_shapes=[pltpu.SemaphoreType.DMA((6,)), pltpu.SemaphoreType.DMA((6,)), pltpu.SemaphoreType.DMA],
    )(shard)


def _swap_halves_with_sibling(grads):
    n, R, w = grads.shape
    half = R // 2

    def body(g_ref, o_ref, send_sem, recv_sem):
        x, y, c = _my_place()
        theirs = pl.ds(pl.multiple_of((1 - c) * half, 8), half)
        cp = pltpu.make_async_remote_copy(
            src_ref=g_ref.at[:, theirs], dst_ref=o_ref, send_sem=send_sem, recv_sem=recv_sem,
            device_id=(x, y, 1 - c), device_id_type=MESH)
        cp.start()
        cp.wait()

    return pl.pallas_call(
        body, name="grads_to_sibling", out_shape=SDS((n, half, w), grads.dtype),
        in_specs=[ANY], out_specs=ANY,
        scratch_shapes=[pltpu.SemaphoreType.DMA, pltpu.SemaphoreType.DMA],
    )(grads)


def _pack_row_tile(rows):
    for t in (464, 512, 256, 128, 64, 32, 16, 8):
        if rows % t == 0:
            return t
    return rows


def _add_sibling_half(grads, from_sibling, c_idx):
    n, R, w = grads.shape
    half = R // 2
    tr = _pack_row_tile(half)
    steps = half // tr

    def body(c_ref, g_ref, s_ref, o_ref):
        o_ref[...] = g_ref[...] + s_ref[...]

    return pl.pallas_call(
        body, name="add_sibling_half",
        grid_spec=pltpu.PrefetchScalarGridSpec(
            num_scalar_prefetch=1, grid=(n, steps),
            in_specs=[pl.BlockSpec((1, tr, w), lambda j, i, c_ref: (j, c_ref[0] * steps + i, 0)),
                      pl.BlockSpec((1, tr, w), lambda j, i, c_ref: (j, i, 0))],
            out_specs=pl.BlockSpec((1, tr, w), lambda j, i, c_ref: (j, i, 0))),
        out_shape=SDS((n, half, w), F32),
        compiler_params=_params("parallel", "parallel"),
    )(c_idx, grads, from_sibling)


def _scatter_to_chips(partial):
    n, half, w = partial.shape

    def body(p_ref, o_ref, send_sems, recv_sems):
        x, y, c = _my_place()
        sends = []
        for j, (bx, by, kb) in enumerate(_other_chips(x, y)):
            cp = pltpu.make_async_remote_copy(
                src_ref=p_ref.at[kb], dst_ref=o_ref.at[j], send_sem=send_sems.at[j], recv_sem=recv_sems.at[j],
                device_id=(bx, by, c), device_id_type=MESH)
            cp.start()
            sends.append(cp)
        for cp in sends:
            cp.wait()

    return pl.pallas_call(
        body, name="grads_to_chips", out_shape=SDS((n - 1, half, w), partial.dtype),
        in_specs=[ANY], out_specs=ANY,
        scratch_shapes=[pltpu.SemaphoreType.DMA((3,)), pltpu.SemaphoreType.DMA((3,))],
    )(partial)


def _sum_chip_partials(partial, received, k_idx):
    n, half, w = partial.shape
    tr = _pack_row_tile(half)

    def body(k_ref, p_ref, r_ref, o_ref):
        o_ref[...] = ((p_ref[0] + r_ref[0]) + r_ref[1]) + r_ref[2]

    return pl.pallas_call(
        body, name="sum_chip_partials",
        grid_spec=pltpu.PrefetchScalarGridSpec(
            num_scalar_prefetch=1, grid=(half // tr,),
            in_specs=[pl.BlockSpec((1, tr, w), lambda i, k_ref: (k_ref[0], i, 0)),
                      pl.BlockSpec((n - 1, tr, w), lambda i, k_ref: (0, i, 0))],
            out_specs=pl.BlockSpec((tr, w), lambda i, k_ref: (i, 0))),
        out_shape=SDS((half, w), F32),
        compiler_params=_params("parallel"),
    )(k_idx, partial, received)


def _join_halves(mine):
    half, w = mine.shape

    def body(h_ref, o_ref, send_sem, recv_sem, local_sem):
        x, y, c = _my_place()
        rows = pl.ds(pl.multiple_of(c * half, 8), half)
        own = pltpu.make_async_copy(h_ref, o_ref.at[rows], local_sem)
        own.start()
        cp = pltpu.make_async_remote_copy(
            src_ref=h_ref, dst_ref=o_ref.at[rows], send_sem=send_sem, recv_sem=recv_sem,
            device_id=(x, y, 1 - c), device_id_type=MESH)
        cp.start()
        cp.wait()
        own.wait()

    return pl.pallas_call(
        body, name="join_halves", out_shape=SDS((2 * half, w), mine.dtype),
        in_specs=[ANY], out_specs=ANY,
        scratch_shapes=[pltpu.SemaphoreType.DMA, pltpu.SemaphoreType.DMA, pltpu.SemaphoreType.DMA],
    )(mine)


def _adamw(name, w, g, m, v):
    rows, width = w.shape
    tr = _pack_row_tile(rows)

    def body(w_ref, g_ref, m_ref, v_ref, d_ref, nm_ref, nv_ref):
        gg = g_ref[...]
        m_new = ADAM_B1 * m_ref[...] + (1.0 - ADAM_B1) * gg
        v_new = ADAM_B2 * v_ref[...] + (1.0 - ADAM_B2) * (gg * gg)
        m_hat = m_new / (1.0 - ADAM_B1 ** ADAM_STEP)
        v_hat = v_new / (1.0 - ADAM_B2 ** ADAM_STEP)
        d_ref[...] = -ADAM_LR * (m_hat / (jnp.sqrt(v_hat) + ADAM_EPS) + ADAM_WD * w_ref[...])
        nm_ref[...] = m_new
        nv_ref[...] = v_new

    spec = _rows(tr, width)
    return pl.pallas_call(
        body, name=name, grid=(rows // tr,),
        in_specs=[spec] * 4, out_specs=[spec] * 3,
        out_shape=[SDS((rows, width), F32)] * 3,
        compiler_params=_params("parallel"),
    )(w, g, m, v)


WEIGHT_NAMES = ['norm_mix', 'norm_ffn', 'attn_w_qkv', 'attn_b_qkv', 'attn_sinks', 'attn_w_o', 'attn_b_o',
                'conv_w_pw1', 'conv_b_pw1', 'conv_w_dw', 'conv_b_dw', 'conv_ln_g', 'conv_ln_b', 'conv_w_pw2',
                'conv_b_pw2', 'ffn_w1', 'ffn_w3', 'ffn_w2', 'norm_final']
BIG = ['attn_w_qkv', 'attn_w_o', 'conv_w_pw1', 'conv_w_pw2', 'ffn_w1', 'ffn_w3', 'ffn_w2']
COLUMN_SPLIT = ('attn_w_qkv', 'conv_w_pw1', 'ffn_w1', 'ffn_w3')
SMALL_SPLIT = ['conv_b_pw1', 'conv_w_dw', 'conv_b_dw', 'conv_ln_g', 'conv_ln_b', 'conv_b_pw2']
SMALL_WHOLE = ['norm_mix', 'norm_ffn', 'attn_b_qkv', 'attn_sinks', 'attn_b_o', 'norm_final']


def _pack_rows(arrays, dtype, row_multiple):
    flat = jnp.concatenate([a.astype(dtype).reshape(-1) for a in arrays])
    rows = -(-flat.shape[0] // PACK_W)
    rows = -(-rows // row_multiple) * row_multiple
    return jnp.pad(flat, (0, rows * PACK_W - flat.shape[0])).reshape(rows, PACK_W)


def _unpack_rows(pack, shapes):
    flat = pack.reshape(-1)
    out, at = [], 0
    for shape in shapes:
        size = 1
        for s in shape:
            size *= s
        out.append(flat[at:at + size].reshape(shape))
        at += size
    return out


def _join_chip_axis(name, parts):
    axis = parts.ndim - 1 if name in COLUMN_SPLIT or name in SMALL_SPLIT else parts.ndim - 2
    moved = jnp.moveaxis(parts, 0, axis - 1)
    shape = list(moved.shape)
    shape[axis - 1:axis + 1] = [shape[axis - 1] * shape[axis]]
    return moved.reshape(shape)


def _split_chip_axis(name, whole, shard_shape):
    axis = len(shard_shape) - 1 if name in COLUMN_SPLIT or name in SMALL_SPLIT else len(shard_shape) - 2
    shape = list(whole.shape)
    shape[axis:axis + 1] = [N_CHIPS, shard_shape[axis]]
    return jnp.moveaxis(whole.reshape(shape), axis, 0)


def kernel(x, norm_mix, norm_ffn, attn_w_qkv, attn_b_qkv, attn_sinks, attn_w_o, attn_b_o, conv_w_pw1, conv_b_pw1, conv_w_dw, conv_b_dw, conv_ln_g, conv_ln_b, conv_w_pw2, conv_b_pw2, ffn_w1, ffn_w3, ffn_w2, norm_final, loss_target, m_norm_mix, m_norm_ffn, m_attn_w_qkv, m_attn_b_qkv, m_attn_sinks, m_attn_w_o, m_attn_b_o, m_conv_w_pw1, m_conv_b_pw1, m_conv_w_dw, m_conv_b_dw, m_conv_ln_g, m_conv_ln_b, m_conv_w_pw2, m_conv_b_pw2, m_ffn_w1, m_ffn_w3, m_ffn_w2, m_norm_final, v_norm_mix, v_norm_ffn, v_attn_w_qkv, v_attn_b_qkv, v_attn_sinks, v_attn_w_o, v_attn_b_o, v_conv_w_pw1, v_conv_b_pw1, v_conv_w_dw, v_conv_b_dw, v_conv_ln_g, v_conv_ln_b, v_conv_w_pw2, v_conv_b_pw2, v_ffn_w1, v_ffn_w3, v_ffn_w2, v_norm_final):
    w = dict(zip(WEIGHT_NAMES, (norm_mix, norm_ffn, attn_w_qkv, attn_b_qkv, attn_sinks, attn_w_o, attn_b_o,
                                conv_w_pw1, conv_b_pw1, conv_w_dw, conv_b_dw, conv_ln_g, conv_ln_b, conv_w_pw2,
                                conv_b_pw2, ffn_w1, ffn_w3, ffn_w2, norm_final)))
    m = dict(zip(WEIGHT_NAMES, (m_norm_mix, m_norm_ffn, m_attn_w_qkv, m_attn_b_qkv, m_attn_sinks, m_attn_w_o,
                                m_attn_b_o, m_conv_w_pw1, m_conv_b_pw1, m_conv_w_dw, m_conv_b_dw, m_conv_ln_g,
                                m_conv_ln_b, m_conv_w_pw2, m_conv_b_pw2, m_ffn_w1, m_ffn_w3, m_ffn_w2, m_norm_final)))
    v = dict(zip(WEIGHT_NAMES, (v_norm_mix, v_norm_ffn, v_attn_w_qkv, v_attn_b_qkv, v_attn_sinks, v_attn_w_o,
                                v_attn_b_o, v_conv_w_pw1, v_conv_b_pw1, v_conv_w_dw, v_conv_b_dw, v_conv_ln_g,
                                v_conv_ln_b, v_conv_w_pw2, v_conv_b_pw2, v_ffn_w1, v_ffn_w3, v_ffn_w2, v_norm_final)))
    T, D = x.shape[1], x.shape[2]
    c_idx = lax.axis_index("c").astype(jnp.int32).reshape(1)
    chip = (2 * lax.axis_index("x") + lax.axis_index("y")).astype(jnp.int32)

    big_shapes = [w[n].shape for n in BIG]
    small_shapes = [w[n].shape for n in SMALL_SPLIT]
    gathered = _gather_weights(_pack_rows([w[n] for n in BIG], BF16, 32))
    full = {}
    per_chip = [_unpack_rows(gathered[j], big_shapes) for j in range(N_CHIPS)]
    for i, n in enumerate(BIG):
        full[n] = _join_chip_axis(n, jnp.stack([per_chip[j][i] for j in range(N_CHIPS)]))
    small_all = _exchange_small("gather_small", _pack_rows([w[n] for n in SMALL_SPLIT], F32, 8), reduce=False)
    per_chip = [_unpack_rows(small_all[2 * j], small_shapes) for j in range(N_CHIPS)]
    for i, n in enumerate(SMALL_SPLIT):
        full[n] = _join_chip_axis(n, jnp.stack([per_chip[j][i] for j in range(N_CHIPS)]))

    p = {
        "norm_mix": norm_mix, "norm_ffn": norm_ffn, "norm_final": norm_final.reshape(1, D),
        "attn_w_qkv": full["attn_w_qkv"][0], "attn_b_qkv": attn_b_qkv, "attn_sinks": attn_sinks,
        "attn_w_o": full["attn_w_o"][0], "attn_b_o": attn_b_o,
        "conv_w_pw1": full["conv_w_pw1"][0], "conv_b_pw1": full["conv_b_pw1"], "conv_w_dw": full["conv_w_dw"][0],
        "conv_b_dw": full["conv_b_dw"], "conv_ln_g": full["conv_ln_g"], "conv_ln_b": full["conv_ln_b"],
        "conv_w_pw2": full["conv_w_pw2"][0], "conv_b_pw2": full["conv_b_pw2"],
        "ffn_w1": full["ffn_w1"], "ffn_w3": full["ffn_w3"], "ffn_w2": full["ffn_w2"],
    }
    loss_part, dx, g = _local_step(x[0], loss_target[0], p)
    g = {n: g[n].reshape((-1,) + g[n].shape[-2:]) if w[n].ndim == 3 else g[n].reshape(w[n].shape[:-1] + (-1,))
         for n in WEIGHT_NAMES}

    by_chip = [_split_chip_axis(n, g[n], w[n].shape) for n in BIG]
    grads = jnp.stack([_pack_rows([parts[j] for parts in by_chip], F32, 16) for j in range(N_CHIPS)])
    from_sibling = _swap_halves_with_sibling(grads)
    partial = _add_sibling_half(grads, from_sibling, c_idx)
    received = _scatter_to_chips(partial)
    my_half = _sum_chip_partials(partial, received, chip.reshape(1))
    g_big = _join_halves(my_half)
    d_big, m_big, v_big = _adamw("adamw_big", _pack_rows([w[n] for n in BIG], F32, 16), g_big,
                                 _pack_rows([m[n] for n in BIG], F32, 16), _pack_rows([v[n] for n in BIG], F32, 16))

    small_whole_shapes = [w[n].shape for n in SMALL_WHOLE]
    small_full_shapes = [g[n].shape for n in SMALL_SPLIT]
    reduced = _exchange_small(
        "reduce_small", _pack_rows([loss_part] + [g[n] for n in SMALL_WHOLE] + [g[n] for n in SMALL_SPLIT], F32, 8),
        reduce=True)
    pieces = _unpack_rows(reduced, [(1,)] + small_whole_shapes + small_full_shapes)
    loss = pieces[0].reshape(())
    g_small = dict(zip(SMALL_WHOLE, pieces[1:1 + len(SMALL_WHOLE)]))
    for n, whole in zip(SMALL_SPLIT, pieces[1 + len(SMALL_WHOLE):]):
        parts = _split_chip_axis(n, whole, w[n].shape)
        g_small[n] = lax.dynamic_index_in_dim(parts, chip, axis=0, keepdims=False)
    small = SMALL_WHOLE + SMALL_SPLIT
    d_small, m_small, v_small = _adamw(
        "adamw_small", _pack_rows([w[n] for n in small], F32, 8), _pack_rows([g_small[n] for n in small], F32, 8),
        _pack_rows([m[n] for n in small], F32, 8), _pack_rows([v[n] for n in small], F32, 8))

    outs = {}
    for tag, big_pack, small_pack in (("g", g_big, None), ("d", d_big, d_small), ("m", m_big, m_small),
                                      ("v", v_big, v_small)):
        vals = dict(zip(BIG, _unpack_rows(big_pack, big_shapes)))
        if small_pack is None:
            vals.update(g_small)
        else:
            vals.update(zip(small, _unpack_rows(small_pack, [w[n].shape for n in small])))
        outs[tag] = vals
    return (loss, dx.reshape(1, T, D), *[outs["g"][n] for n in WEIGHT_NAMES], *[outs["d"][n] for n in WEIGHT_NAMES],
            *[outs["m"][n] for n in WEIGHT_NAMES], *[outs["v"][n] for n in WEIGHT_NAMES])
```

```python
import functools

import jax
import jax.numpy as jnp
from jax import lax
from jax.experimental import pallas as pl
from jax.experimental.pallas import tpu as pltpu

F32 = jnp.float32
BF16 = jnp.bfloat16
SDS = jax.ShapeDtypeStruct
MESH = pl.DeviceIdType.MESH

HEAD_DIM = 64
N_Q_HEADS = 16
N_KV_HEADS = 2
Q_PER_KV = N_Q_HEADS // N_KV_HEADS
ATTN_BLOCK = 128
ROPE_THETA = 10000.0
CONV_WIDTH = 31
CONV_HALO = 32
RMS_EPS = 1e-5
LN_EPS = 1e-5
ADAM_LR = 0.001
ADAM_B1 = 0.9
ADAM_B2 = 0.999
ADAM_EPS = 1e-08
ADAM_WD = 0.01
ADAM_STEP = 10

V7X_LANES = 128
V7X_VMEM_LIMIT_BYTES = 56 * 1024 * 1024

N_CHIPS = 4
N_DEV = 8
PACK_W = 1024

MASK_VALUE = -1e30


def _params(*semantics):
    return pltpu.CompilerParams(dimension_semantics=semantics, vmem_limit_bytes=V7X_VMEM_LIMIT_BYTES)


def _rows(tm, width):
    return pl.BlockSpec((tm, width), lambda i: (i, 0))


def _whole(shape):
    return pl.BlockSpec(shape, lambda *_: (0,) * len(shape))


def _rms_rstd(h):
    return lax.rsqrt(jnp.mean(h * h, axis=-1, keepdims=True) + RMS_EPS)


def _silu_and_grad(z):
    sg = jax.nn.sigmoid(z)
    return z * sg, sg * (1.0 + z * (1.0 - sg))


def _swap_rope_halves(t):
    w = t.shape[1]
    half = HEAD_DIM // 2
    lane = lax.broadcasted_iota(jnp.int32, t.shape, 1)
    upper = pltpu.roll(t, w - half, 1)
    lower = pltpu.roll(t, half, 1)
    return jnp.where(lane % HEAD_DIM < half, upper, lower)


def _rope(t, cos_ref, sin_ref):
    reps = t.shape[1] // V7X_LANES
    c = jnp.tile(cos_ref[...], (1, reps))
    s = jnp.tile(sin_ref[...], (1, reps))
    return t * c + _swap_rope_halves(t) * s


def _rope_transposed(dt, cos_ref, sin_ref):
    reps = dt.shape[1] // V7X_LANES
    c = jnp.tile(cos_ref[...], (1, reps))
    s = jnp.tile(sin_ref[...], (1, reps))
    return dt * c + _swap_rope_halves(dt * s)


def _rope_tables(seq_len):
    pos = jnp.arange(seq_len, dtype=F32)
    inv_freq = ROPE_THETA ** (-jnp.arange(0, HEAD_DIM, 2, dtype=F32) / HEAD_DIM)
    ang = pos[:, None] * inv_freq[None, :]
    cos, sin = jnp.cos(ang), jnp.sin(ang)
    cos_t = jnp.concatenate([cos, cos, cos, cos], axis=1)
    sin_t = jnp.concatenate([-sin, sin, -sin, sin], axis=1)
    return cos_t, sin_t


def _qkv_proj(h, g, w, b, cos, sin):
    T, D = h.shape
    N = w.shape[1]
    tm = min(512, T)
    rope_w = N - N_KV_HEADS * HEAD_DIM

    def body(h_ref, g_ref, w_ref, b_ref, cos_ref, sin_ref, y_ref, o_ref):
        hh = h_ref[...]
        y = (hh * _rms_rstd(hh) * g_ref[...]).astype(BF16)
        y_ref[...] = y
        acc = jnp.dot(y, w_ref[...], preferred_element_type=F32) + b_ref[...]
        o_ref[:, :rope_w] = _rope(acc[:, :rope_w], cos_ref, sin_ref).astype(BF16)
        o_ref[:, rope_w:] = acc[:, rope_w:].astype(BF16)

    return pl.pallas_call(
        body, name="qkv_proj", grid=(T // tm,),
        in_specs=[_rows(tm, D), _whole((1, D)), _whole((D, N)), _whole((1, N)),
                  _rows(tm, V7X_LANES), _rows(tm, V7X_LANES)],
        out_specs=[_rows(tm, D), _rows(tm, N)],
        out_shape=[SDS((T, D), BF16), SDS((T, N), BF16)],
        compiler_params=_params("parallel"),
    )(h, g, w, b, cos, sin)


def _pw1_proj(h, g, w, b):
    T, D = h.shape
    n = w.shape[2]
    N = N_CHIPS * n
    tm = min(512, T)

    def body(h_ref, g_ref, w_ref, b_ref, y_ref, o_ref):
        hh = h_ref[...]
        y = (hh * _rms_rstd(hh) * g_ref[...]).astype(BF16)
        y_ref[...] = y
        for j in range(N_CHIPS):
            cols = slice(j * n, (j + 1) * n)
            o_ref[:, cols] = jnp.dot(y, w_ref[j], preferred_element_type=F32) + b_ref[:, cols]

    return pl.pallas_call(
        body, name="pw1_proj", grid=(T // tm,),
        in_specs=[_rows(tm, D), _whole((1, D)), _whole((N_CHIPS, D, n)), _whole((1, N))],
        out_specs=[_rows(tm, D), _rows(tm, N)],
        out_shape=[SDS((T, D), BF16), SDS((T, N), F32)],
        compiler_params=_params("parallel"),
    )(h, g, w, b)


def _band_mask(n):
    row = lax.broadcasted_iota(jnp.int32, (ATTN_BLOCK, 2 * ATTN_BLOCK), 0)
    col = lax.broadcasted_iota(jnp.int32, (ATTN_BLOCK, 2 * ATTN_BLOCK), 1)
    band = (col > row) & (col <= row + ATTN_BLOCK) & ((col >= ATTN_BLOCK) | (n > 0))
    return jnp.concatenate([band] * Q_PER_KV, axis=0)


def _group_operands(g, q_ref, kc_ref, kp_ref, vc_ref, vp_ref, sink_ref):
    lo = g * HEAD_DIM
    k = jnp.concatenate([kp_ref[:, lo:lo + HEAD_DIM], kc_ref[:, lo:lo + HEAD_DIM]], axis=0)
    v = jnp.concatenate([vp_ref[:, lo:lo + HEAD_DIM], vc_ref[:, lo:lo + HEAD_DIM]], axis=0)
    heads = range(g * Q_PER_KV, (g + 1) * Q_PER_KV)
    q = jnp.concatenate([q_ref[:, h * HEAD_DIM:(h + 1) * HEAD_DIM] for h in heads], axis=0)
    sink = jnp.concatenate(
        [jnp.broadcast_to(sink_ref[0:1, h:h + 1], (ATTN_BLOCK, 1)) for h in heads], axis=0)
    return q, k, v, sink


def _softmax_with_sink(q, k, sink, mask):
    s = lax.dot_general(q, k, (((1,), (1,)), ((), ())), preferred_element_type=F32) * (HEAD_DIM ** -0.5)
    s = jnp.where(mask, s, MASK_VALUE)
    m = jnp.maximum(jnp.max(s, axis=1, keepdims=True), sink)
    p = jnp.exp(s - m)
    e_sink = jnp.exp(sink - m)
    inv = 1.0 / (jnp.sum(p, axis=1, keepdims=True) + e_sink)
    return p * inv, e_sink * inv


def _attn_specs(T):
    nb = T // ATTN_BLOCK
    kcol = N_Q_HEADS * HEAD_DIM // V7X_LANES
    cur = lambda n: jnp.minimum(n, nb - 1)
    prev = lambda n: jnp.maximum(jnp.minimum(n, nb - 1) - 1, 0)
    q_spec = pl.BlockSpec((ATTN_BLOCK, N_Q_HEADS * HEAD_DIM), lambda n: (cur(n), 0))
    kc_spec = pl.BlockSpec((ATTN_BLOCK, V7X_LANES), lambda n: (cur(n), kcol))
    kp_spec = pl.BlockSpec((ATTN_BLOCK, V7X_LANES), lambda n: (prev(n), kcol))
    vc_spec = pl.BlockSpec((ATTN_BLOCK, V7X_LANES), lambda n: (cur(n), kcol + 1))
    vp_spec = pl.BlockSpec((ATTN_BLOCK, V7X_LANES), lambda n: (prev(n), kcol + 1))
    return q_spec, kc_spec, kp_spec, vc_spec, vp_spec


def _attn_fwd(qkv, sinks):
    T = qkv.shape[0]
    nb = T // ATTN_BLOCK
    qw = N_Q_HEADS * HEAD_DIM

    def body(q_ref, kc_ref, kp_ref, vc_ref, vp_ref, sink_ref, o_ref):
        mask = _band_mask(pl.program_id(0))
        for g in range(N_KV_HEADS):
            q, k, v, sink = _group_operands(g, q_ref, kc_ref, kp_ref, vc_ref, vp_ref, sink_ref)
            probs, _ = _softmax_with_sink(q, k, sink, mask)
            o = jnp.dot(probs.astype(BF16), v, preferred_element_type=F32)
            for i in range(Q_PER_KV):
                h = g * Q_PER_KV + i
                o_ref[:, h * HEAD_DIM:(h + 1) * HEAD_DIM] = o[i * ATTN_BLOCK:(i + 1) * ATTN_BLOCK].astype(BF16)

    return pl.pallas_call(
        body, name="attn_fwd", grid=(nb,),
        in_specs=[*_attn_specs(T), _whole((1, N_Q_HEADS))],
        out_specs=_rows(ATTN_BLOCK, qw),
        out_shape=SDS((T, qw), BF16),
        compiler_params=_params("parallel"),
    )(qkv, qkv, qkv, qkv, qkv, sinks)


def _mm_res(name, a, w, b, res):
    T, K = a.shape
    D = w.shape[1]
    tm = min(512, T)

    def body(a_ref, w_ref, b_ref, r_ref, o_ref):
        o_ref[...] = jnp.dot(a_ref[...], w_ref[...], preferred_element_type=F32) + b_ref[...] + r_ref[...]

    return pl.pallas_call(
        body, name=name, grid=(T // tm,),
        in_specs=[_rows(tm, K), _whole((K, D)), _whole((1, D)), _rows(tm, D)],
        out_specs=_rows(tm, D),
        out_shape=SDS((T, D), F32),
        compiler_params=_params("parallel"),
    )(a, w, b, res)


def _ffn_down(name, s, w2, layer, res):
    _, T, n = s.shape
    D = w2.shape[3]
    tm = min(512, T)

    def body(s_ref, w_ref, r_ref, o_ref):
        acc = r_ref[...]
        for j in range(N_CHIPS):
            acc = acc + jnp.dot(s_ref[j], w_ref[j], preferred_element_type=F32)
        o_ref[...] = acc

    return pl.pallas_call(
        body, name=name, grid=(T // tm,),
        in_specs=[pl.BlockSpec((N_CHIPS, tm, n), lambda i: (0, i, 0)),
                  pl.BlockSpec((N_CHIPS, None, n, D), lambda i: (0, layer, 0, 0)), _rows(tm, D)],
        out_specs=_rows(tm, D),
        out_shape=SDS((T, D), F32),
        compiler_params=_params("parallel"),
    )(s, w2, res)


def _ffn_up(name, h, g, w1, w3, layer):
    T, D = h.shape
    n = w1.shape[3]
    tm = min(512, T)

    def body(h_ref, g_ref, w1_ref, w3_ref, f_ref, g1_ref, g3_ref, s_ref):
        @pl.when(pl.program_id(1) == 0)
        def _():
            hh = h_ref[...]
            f_ref[...] = (hh * _rms_rstd(hh) * g_ref[...]).astype(BF16)

        f = f_ref[...]
        a = jnp.dot(f, w1_ref[...], preferred_element_type=F32)
        b = jnp.dot(f, w3_ref[...], preferred_element_type=F32)
        g1_ref[...] = a.astype(BF16)
        g3_ref[...] = b.astype(BF16)
        s_ref[...] = (a * jax.nn.sigmoid(a) * b).astype(BF16)

    slab = pl.BlockSpec((None, tm, n), lambda i, j: (j, i, 0))
    wslab = pl.BlockSpec((None, None, D, n), lambda i, j: (j, layer, 0, 0))
    hidden = SDS((N_CHIPS, T, n), BF16)
    return pl.pallas_call(
        body, name=name, grid=(T // tm, N_CHIPS),
        in_specs=[pl.BlockSpec((tm, D), lambda i, j: (i, 0)), pl.BlockSpec((1, D), lambda i, j: (0, 0)), wslab, wslab],
        out_specs=[pl.BlockSpec((tm, D), lambda i, j: (i, 0)), slab, slab, slab],
        out_shape=[SDS((T, D), BF16), hidden, hidden, hidden],
        compiler_params=_params("parallel", "arbitrary"),
    )(h, g, w1, w3)


def _glu(a, d):
    return a[:, :d] * jax.nn.sigmoid(a[:, d:])


def _conv_tile(T):
    return min(256, T)


def _conv_fwd(a, w_dw, b_dw, ln_g, ln_b):
    T = a.shape[0]
    D = a.shape[1] // 2
    tc = _conv_tile(T)
    per = tc // CONV_HALO

    def body(a_ref, ah_ref, w_ref, bdw_ref, lg_ref, lb_ref, c_ref, act_ref, u_scr):
        i = pl.program_id(0)
        u_scr[0:CONV_HALO, :] = jnp.where(i > 0, _glu(ah_ref[...], D), 0.0)
        u_scr[CONV_HALO:, :] = _glu(a_ref[...], D)
        acc = jnp.zeros((tc, D), F32)
        for j in range(CONV_WIDTH):
            acc = acc + u_scr[pl.ds(CONV_HALO - CONV_WIDTH + 1 + j, tc), :] * w_ref[j:j + 1, :]
        c = acc + bdw_ref[...]
        c_ref[...] = c
        xc = c - jnp.mean(c, axis=-1, keepdims=True)
        z = xc * lax.rsqrt(jnp.mean(xc * xc, axis=-1, keepdims=True) + LN_EPS)
        l = z * lg_ref[...] + lb_ref[...]
        act_ref[...] = (l * jax.nn.sigmoid(l)).astype(BF16)

    return pl.pallas_call(
        body, name="conv_fwd", grid=(T // tc,),
        in_specs=[_rows(tc, 2 * D),
                  pl.BlockSpec((CONV_HALO, 2 * D), lambda i: (jnp.maximum(i * per - 1, 0), 0)),
                  _whole((CONV_WIDTH, D)), _whole((1, D)), _whole((1, D)), _whole((1, D))],
        out_specs=[_rows(tc, D), _rows(tc, D)],
        out_shape=[SDS((T, D), F32), SDS((T, D), BF16)],
        scratch_shapes=[pltpu.VMEM((tc + CONV_HALO, D), F32)],
        compiler_params=_params("parallel"),
    )(a, a, w_dw, b_dw, ln_g, ln_b)


def _final_loss(h, g, target):
    T, D = h.shape
    tm = min(512, T)

    def body(h_ref, g_ref, t_ref, dh_ref, loss_ref, dg_ref):
        @pl.when(pl.program_id(0) == 0)
        def _():
            loss_ref[...] = jnp.zeros_like(loss_ref)
            dg_ref[...] = jnp.zeros_like(dg_ref)

        hh = h_ref[...]
        r = _rms_rstd(hh)
        g = g_ref[...]
        d = hh * r * g - t_ref[...]
        loss_ref[...] += 0.5 * jnp.sum(jnp.mean(d * d, axis=-1, keepdims=True), axis=0, keepdims=True)
        dout = d * (1.0 / D)
        dg_ref[...] += jnp.sum(dout * (hh * r), axis=0, keepdims=True)
        dxh = dout * g
        dh_ref[...] = r * dxh - hh * (r * r * r) * jnp.mean(dxh * hh, axis=-1, keepdims=True)

    return pl.pallas_call(
        body, name="final_loss", grid=(T // tm,),
        in_specs=[_rows(tm, D), _whole((1, D)), _rows(tm, D)],
        out_specs=[_rows(tm, D), _whole((1, 1)), _whole((1, D))],
        out_shape=[SDS((T, D), F32), SDS((1, 1), F32), SDS((1, D), F32)],
        compiler_params=_params("arbitrary"),
    )(h, g, target)


def _ffn_bwd_act(name, dh, w2, layer, g1, g3):
    T, D = dh.shape
    n = w2.shape[2]
    tm = min(512, T)

    def body(dh_ref, w2_ref, g1_ref, g3_ref, dg1_ref, dg3_ref):
        ds = lax.dot_general(dh_ref[...].astype(BF16), w2_ref[...], (((1,), (1,)), ((), ())),
                             preferred_element_type=F32)
        act, dact = _silu_and_grad(g1_ref[...].astype(F32))
        dg1_ref[...] = (ds * g3_ref[...].astype(F32) * dact).astype(BF16)
        dg3_ref[...] = (ds * act).astype(BF16)

    slab = pl.BlockSpec((None, tm, n), lambda i, j: (j, i, 0))
    hidden = SDS((N_CHIPS, T, n), BF16)
    return pl.pallas_call(
        body, name=name, grid=(T // tm, N_CHIPS),
        in_specs=[pl.BlockSpec((tm, D), lambda i, j: (i, 0)),
                  pl.BlockSpec((None, None, n, D), lambda i, j: (j, layer, 0, 0)), slab, slab],
        out_specs=[slab, slab],
        out_shape=[hidden, hidden],
        compiler_params=_params("parallel", "arbitrary"),
    )(dh, w2, g1, g3)


def _dot_tn(a, b):
    return lax.dot_general(a.astype(BF16), b.astype(BF16), (((0,), (0,)), ((), ())), preferred_element_type=F32)


def _dot_nt(a, b):
    return lax.dot_general(a.astype(BF16), b, (((1,), (1,)), ((), ())), preferred_element_type=F32)


def _mm_tn(name, a, b, col_chunks=1):
    a_slabs, b_slabs = a.ndim == 3, b.ndim == 3
    T = a.shape[-2]
    tt = min(512, T)
    ka, nb = a.shape[-1], b.shape[-1]
    if a_slabs or b_slabs:
        out_dims = (N_CHIPS, ka, nb)
    elif col_chunks > 1:
        out_dims = (col_chunks, ka, nb // col_chunks)
    else:
        out_dims = (ka, nb)

    def body(a_ref, b_ref, o_ref):
        @pl.when(pl.program_id(0) == 0)
        def _():
            o_ref[...] = jnp.zeros_like(o_ref)

        if a_slabs:
            bb = b_ref[...].astype(BF16)
            for j in range(N_CHIPS):
                o_ref[j] += _dot_tn(a_ref[j], bb)
        elif b_slabs:
            aa = a_ref[...].astype(BF16)
            for j in range(N_CHIPS):
                o_ref[j] += _dot_tn(aa, b_ref[j])
        elif col_chunks > 1:
            aa = a_ref[...].astype(BF16)
            w = nb // col_chunks
            for j in range(col_chunks):
                o_ref[j] += _dot_tn(aa, b_ref[:, j * w:(j + 1) * w])
        else:
            o_ref[...] += _dot_tn(a_ref[...], b_ref[...])

    def spec(arr, slabs):
        if slabs:
            return pl.BlockSpec((N_CHIPS, tt, arr.shape[-1]), lambda t: (0, t, 0))
        return _rows(tt, arr.shape[-1])

    return pl.pallas_call(
        body, name=name, grid=(T // tt,),
        in_specs=[spec(a, a_slabs), spec(b, b_slabs)],
        out_specs=_whole(out_dims),
        out_shape=SDS(out_dims, F32),
        compiler_params=_params("arbitrary"),
    )(a, b)


def _mm_nt_normbwd(name, pairs, h, g, dh):
    T, D = h.shape
    tm = min(256, T)
    n_pairs = len(pairs)
    kinds = ["slabs" if dy.ndim == 3 else ("quarters" if w.ndim == 3 else "plain") for dy, w, _ in pairs]

    def body(*refs):
        dy_refs = refs[:n_pairs]
        w_refs = refs[n_pairs:2 * n_pairs]
        h_ref, g_ref, dh_ref, o_ref, dg_ref, cs_ref = refs[2 * n_pairs:]

        @pl.when(pl.program_id(0) == 0)
        def _():
            dg_ref[...] = jnp.zeros_like(dg_ref)
            cs_ref[...] = jnp.zeros_like(cs_ref)

        df = jnp.zeros((tm, D), F32)
        for dy_ref, w_ref, kd in zip(dy_refs, w_refs, kinds):
            if kd == "slabs":
                for j in range(N_CHIPS):
                    df = df + _dot_nt(dy_ref[j], w_ref[j])
            elif kd == "quarters":
                n = w_ref.shape[2]
                for j in range(N_CHIPS):
                    df = df + _dot_nt(dy_ref[:, j * n:(j + 1) * n], w_ref[j])
            else:
                df = df + _dot_nt(dy_ref[...], w_ref[...])
        hh = h_ref[...]
        r = _rms_rstd(hh)
        dg_ref[...] += jnp.sum(df * (hh * r), axis=0, keepdims=True)
        dxh = df * g_ref[...]
        out = dh_ref[...] + (r * dxh - hh * (r * r * r) * jnp.mean(dxh * hh, axis=-1, keepdims=True))
        o_ref[...] = out
        cs_ref[...] += jnp.sum(out, axis=0, keepdims=True)

    dy_specs, w_specs = [], []
    for (dy, w, layer), kd in zip(pairs, kinds):
        if kd == "slabs":
            dy_specs.append(pl.BlockSpec((N_CHIPS, tm, dy.shape[2]), lambda i: (0, i, 0)))
            w_specs.append(pl.BlockSpec((N_CHIPS, None, D, w.shape[3]),
                                        functools.partial(lambda i, layer: (0, layer, 0, 0), layer=layer)))
        else:
            dy_specs.append(_rows(tm, dy.shape[1]))
            w_specs.append(_whole(w.shape))

    return pl.pallas_call(
        body, name=name, grid=(T // tm,),
        in_specs=[*dy_specs, *w_specs, _rows(tm, D), _whole((1, D)), _rows(tm, D)],
        out_specs=[_rows(tm, D), _whole((1, D)), _whole((1, D))],
        out_shape=[SDS((T, D), F32), SDS((1, D), F32), SDS((1, D), F32)],
        compiler_params=_params("arbitrary"),
    )(*[dy for dy, _, _ in pairs], *[w for _, w, _ in pairs], h, g, dh)


def _mm_nt(name, dy, w, out_dtype):
    T, N = dy.shape
    K = w.shape[0]
    tm = min(512, T)

    def body(dy_ref, w_ref, o_ref):
        o_ref[...] = lax.dot_general(dy_ref[...].astype(BF16), w_ref[...], (((1,), (1,)), ((), ())),
                                     preferred_element_type=F32).astype(out_dtype)

    return pl.pallas_call(
        body, name=name, grid=(T // tm,),
        in_specs=[_rows(tm, N), _whole((K, N))],
        out_specs=_rows(tm, K),
        out_shape=SDS((T, K), out_dtype),
        compiler_params=_params("parallel"),
    )(dy, w)


def _conv_bwd(dact, c, a, w_dw, ln_g, ln_b):
    T, D = c.shape
    tc = _conv_tile(T)
    per = tc // CONV_HALO
    n_tiles = T // tc
    last_halo = T // CONV_HALO - 1
    first_tap = CONV_HALO - CONV_WIDTH + 1

    def ln_bwd(dact_v, c_v, lg, lb):
        xc = c_v - jnp.mean(c_v, axis=-1, keepdims=True)
        rstd = lax.rsqrt(jnp.mean(xc * xc, axis=-1, keepdims=True) + LN_EPS)
        z = xc * rstd
        _, dsilu = _silu_and_grad(z * lg + lb)
        dl = dact_v * dsilu
        dz = dl * lg
        dc = rstd * (dz - jnp.mean(dz, axis=-1, keepdims=True) - z * jnp.mean(dz * z, axis=-1, keepdims=True))
        return dc, dl, z

    def body(dact_ref, dactn_ref, c_ref, cn_ref, a_ref, ah_ref, w_ref, lg_ref, lb_ref,
             da_ref, dlg_ref, dlb_ref, dbdw_ref, dwdw_ref, dbpw1_ref, dc_scr, u_scr):
        i = pl.program_id(0)

        @pl.when(i == 0)
        def _():
            for ref in (dlg_ref, dlb_ref, dbdw_ref, dwdw_ref, dbpw1_ref):
                ref[...] = jnp.zeros_like(ref)

        lg, lb = lg_ref[...], lb_ref[...]
        dc, dl, z = ln_bwd(dact_ref[...], c_ref[...], lg, lb)
        dlg_ref[...] += jnp.sum(dl * z, axis=0, keepdims=True)
        dlb_ref[...] += jnp.sum(dl, axis=0, keepdims=True)
        dbdw_ref[...] += jnp.sum(dc, axis=0, keepdims=True)
        dcn, _, _ = ln_bwd(dactn_ref[...], cn_ref[...], lg, lb)
        dc_scr[0:tc, :] = dc
        dc_scr[tc:, :] = jnp.where(i < n_tiles - 1, dcn, 0.0)

        a_v = a_ref[...]
        a1 = a_v[:, :D]
        sg = jax.nn.sigmoid(a_v[:, D:])
        u_scr[0:CONV_HALO, :] = jnp.where(i > 0, _glu(ah_ref[...], D), 0.0)
        u_scr[CONV_HALO:, :] = a1 * sg

        du = jnp.zeros((tc, D), F32)
        for j in range(CONV_WIDTH):
            du = du + dc_scr[pl.ds(CONV_WIDTH - 1 - j, tc), :] * w_ref[j:j + 1, :]
            dwdw_ref[j:j + 1, :] += jnp.sum(dc * u_scr[pl.ds(first_tap + j, tc), :], axis=0, keepdims=True)

        da1 = du * sg
        da2 = du * a1 * sg * (1.0 - sg)
        da_ref[:, :D] = da1.astype(BF16)
        da_ref[:, D:] = da2.astype(BF16)
        dbpw1_ref[:, :D] += jnp.sum(da1, axis=0, keepdims=True)
        dbpw1_ref[:, D:] += jnp.sum(da2, axis=0, keepdims=True)

    nxt = lambda i: (jnp.minimum((i + 1) * per, last_halo), 0)
    return pl.pallas_call(
        body, name="conv_bwd", grid=(n_tiles,),
        in_specs=[_rows(tc, D), pl.BlockSpec((CONV_HALO, D), nxt),
                  _rows(tc, D), pl.BlockSpec((CONV_HALO, D), nxt),
                  _rows(tc, 2 * D),
                  pl.BlockSpec((CONV_HALO, 2 * D), lambda i: (jnp.maximum(i * per - 1, 0), 0)),
                  _whole((CONV_WIDTH, D)), _whole((1, D)), _whole((1, D))],
        out_specs=[_rows(tc, 2 * D), _whole((1, D)), _whole((1, D)), _whole((1, D)),
                   _whole((CONV_HALO, D)), _whole((1, 2 * D))],
        out_shape=[SDS((T, 2 * D), BF16), SDS((1, D), F32), SDS((1, D), F32), SDS((1, D), F32),
                   SDS((CONV_HALO, D), F32), SDS((1, 2 * D), F32)],
        scratch_shapes=[pltpu.VMEM((tc + CONV_HALO, D), F32), pltpu.VMEM((tc + CONV_HALO, D), F32)],
        compiler_params=_params("arbitrary"),
    )(dact, dact, c, c, a, a, w_dw, ln_g, ln_b)


def _attn_bwd(qkv, dao, cos, sin, sinks):
    T = qkv.shape[0]
    nb = T // ATTN_BLOCK
    qw = N_Q_HEADS * HEAD_DIM
    kw = N_KV_HEADS * HEAD_DIM

    def body(q_ref, kc_ref, kp_ref, vc_ref, vp_ref, do_ref, cos_ref, sin_ref, cosp_ref, sinp_ref, sink_ref,
             dq_ref, dkv_ref, dsink_ref, dbq_ref, dbkv_ref, carry, prev_scr, cur_scr, dq_scr):
        n = pl.program_id(0)

        @pl.when(n == 0)
        def _():
            for ref in (dsink_ref, dbq_ref, dbkv_ref, carry):
                ref[...] = jnp.zeros_like(ref)

        @pl.when(n == nb)
        def _():
            prev_scr[...] = jnp.zeros_like(prev_scr)

        @pl.when(n < nb)
        def _():
            mask = _band_mask(n)
            for g in range(N_KV_HEADS):
                q, k, v, sink = _group_operands(g, q_ref, kc_ref, kp_ref, vc_ref, vp_ref, sink_ref)
                heads = range(g * Q_PER_KV, (g + 1) * Q_PER_KV)
                do = jnp.concatenate([do_ref[:, h * HEAD_DIM:(h + 1) * HEAD_DIM] for h in heads], axis=0)
                probs, p_sink = _softmax_with_sink(q, k, sink, mask)
                dp = lax.dot_general(do, v, (((1,), (1,)), ((), ())), preferred_element_type=F32)
                delta = jnp.sum(probs * dp, axis=1, keepdims=True)
                ds = (probs * (dp - delta) * (HEAD_DIM ** -0.5)).astype(BF16)
                dsink_rows = -(p_sink * delta)
                dq = jnp.dot(ds, k, preferred_element_type=F32)
                dk = lax.dot_general(ds, q, (((0,), (0,)), ((), ())), preferred_element_type=F32)
                dv = lax.dot_general(probs.astype(BF16), do, (((0,), (0,)), ((), ())), preferred_element_type=F32)
                for i, h in enumerate(heads):
                    rows = slice(i * ATTN_BLOCK, (i + 1) * ATTN_BLOCK)
                    dq_scr[:, h * HEAD_DIM:(h + 1) * HEAD_DIM] = dq[rows]
                    dsink_ref[:, h:h + 1] += jnp.sum(dsink_rows[rows], axis=0, keepdims=True)
                lo = g * HEAD_DIM
                prev_scr[:, lo:lo + HEAD_DIM] = dk[:ATTN_BLOCK]
                cur_scr[:, lo:lo + HEAD_DIM] = dk[ATTN_BLOCK:]
                prev_scr[:, kw + lo:kw + lo + HEAD_DIM] = dv[:ATTN_BLOCK]
                cur_scr[:, kw + lo:kw + lo + HEAD_DIM] = dv[ATTN_BLOCK:]
            dq_pre = _rope_transposed(dq_scr[...], cos_ref, sin_ref)
            dq_ref[...] = dq_pre.astype(BF16)
            dbq_ref[...] += jnp.sum(dq_pre, axis=0, keepdims=True)

        tot = carry[...] + prev_scr[...]
        dk_pre = _rope_transposed(tot[:, :kw], cosp_ref, sinp_ref)
        dkv_ref[:, :kw] = dk_pre.astype(BF16)
        dkv_ref[:, kw:] = tot[:, kw:].astype(BF16)
        dbkv_ref[:, :kw] += jnp.sum(dk_pre, axis=0, keepdims=True)
        dbkv_ref[:, kw:] += jnp.sum(tot[:, kw:], axis=0, keepdims=True)

        @pl.when(n < nb)
        def _():
            carry[...] = cur_scr[...]

    cur = lambda n: (jnp.minimum(n, nb - 1), 0)
    out_lag = lambda n: (jnp.maximum(n - 1, 0), 0)
    return pl.pallas_call(
        body, name="attn_bwd", grid=(nb + 1,),
        in_specs=[*_attn_specs(T),
                  pl.BlockSpec((ATTN_BLOCK, qw), cur),
                  pl.BlockSpec((ATTN_BLOCK, V7X_LANES), cur), pl.BlockSpec((ATTN_BLOCK, V7X_LANES), cur),
                  pl.BlockSpec((ATTN_BLOCK, V7X_LANES), out_lag), pl.BlockSpec((ATTN_BLOCK, V7X_LANES), out_lag),
                  _whole((1, N_Q_HEADS))],
        out_specs=[pl.BlockSpec((ATTN_BLOCK, qw), cur), pl.BlockSpec((ATTN_BLOCK, 2 * kw), out_lag),
                   _whole((1, N_Q_HEADS)), _whole((1, qw)), _whole((1, 2 * kw))],
        out_shape=[SDS((T, qw), BF16), SDS((T, 2 * kw), BF16),
                   SDS((1, N_Q_HEADS), F32), SDS((1, qw), F32), SDS((1, 2 * kw), F32)],
        scratch_shapes=[pltpu.VMEM((ATTN_BLOCK, 2 * kw), F32), pltpu.VMEM((ATTN_BLOCK, 2 * kw), F32),
                        pltpu.VMEM((ATTN_BLOCK, 2 * kw), F32), pltpu.VMEM((ATTN_BLOCK, qw), F32)],
        compiler_params=_params("arbitrary"),
    )(qkv, qkv, qkv, qkv, qkv, dao, cos, sin, cos, sin, sinks)


def _local_step(x, target, p):
    T, D = x.shape
    cos, sin = _rope_tables(T)
    qw = N_Q_HEADS * HEAD_DIM
    nm, nf = p["norm_mix"], p["norm_ffn"]
    zero_b = jnp.zeros((1, D), F32)

    y0, qkv = _qkv_proj(x, nm[0:1], p["attn_w_qkv"], p["attn_b_qkv"], cos, sin)
    ao = _attn_fwd(qkv, p["attn_sinks"])
    h1 = _mm_res("attn_out", ao, p["attn_w_o"], p["attn_b_o"], x)
    w1, w3, w2 = p["ffn_w1"], p["ffn_w3"], p["ffn_w2"]
    f0, g1a, g3a, s0 = _ffn_up("ffn0_up", h1, nf[0:1], w1, w3, 0)
    h2 = _ffn_down("ffn0_down", s0, w2, 0, h1)
    y1, a = _pw1_proj(h2, nm[1:2], p["conv_w_pw1"], p["conv_b_pw1"])
    c, act = _conv_fwd(a, p["conv_w_dw"], p["conv_b_dw"], p["conv_ln_g"], p["conv_ln_b"])
    h3 = _mm_res("conv_out", act, p["conv_w_pw2"], p["conv_b_pw2"], h2)
    f1, g1b, g3b, s1 = _ffn_up("ffn1_up", h3, nf[1:2], w1, w3, 1)
    h4 = _ffn_down("ffn1_down", s1, w2, 1, h3)
    dh4, loss, d_norm_final = _final_loss(h4, p["norm_final"], target)

    g = {}
    dg1, dg3 = _ffn_bwd_act("ffn1_bwd_act", dh4, w2, 1, g1b, g3b)
    dw2_1 = _mm_tn("ffn1_dw2", s1, dh4)
    dw1_1 = _mm_tn("ffn1_dw1", f1, dg1)
    dw3_1 = _mm_tn("ffn1_dw3", f1, dg3)
    dh3, dnf1, db_pw2 = _mm_nt_normbwd("ffn1_bwd_in", [(dg1, w1, 1), (dg3, w3, 1)], h3, nf[1:2], dh4)

    g["conv_w_pw2"] = _mm_tn("conv_dw_pw2", act, dh3)
    dact = _mm_nt("conv_bwd_out", dh3, p["conv_w_pw2"], F32)
    da, d_ln_g, d_ln_b, d_b_dw, d_w_dw, d_b_pw1 = _conv_bwd(dact, c, a, p["conv_w_dw"], p["conv_ln_g"], p["conv_ln_b"])
    g["conv_w_pw1"] = _mm_tn("conv_dw_pw1", y1, da, col_chunks=N_CHIPS)
    dh2, dnm1, _ = _mm_nt_normbwd("conv_bwd_in", [(da, p["conv_w_pw1"], None)], h2, nm[1:2], dh3)

    dg1, dg3 = _ffn_bwd_act("ffn0_bwd_act", dh2, w2, 0, g1a, g3a)
    dw2_0 = _mm_tn("ffn0_dw2", s0, dh2)
    dw1_0 = _mm_tn("ffn0_dw1", f0, dg1)
    dw3_0 = _mm_tn("ffn0_dw3", f0, dg3)
    dh1, dnf0, db_o = _mm_nt_normbwd("ffn0_bwd_in", [(dg1, w1, 0), (dg3, w3, 0)], h1, nf[0:1], dh2)

    g["attn_w_o"] = _mm_tn("attn_dw_o", ao, dh1)
    dao = _mm_nt("attn_bwd_out", dh1, p["attn_w_o"], BF16)
    dq, dkv, d_sinks, dbq, dbkv = _attn_bwd(qkv, dao, cos, sin, p["attn_sinks"])
    dwq = _mm_tn("attn_dw_q", y0, dq)
    dwkv = _mm_tn("attn_dw_kv", y0, dkv)
    wqkv = p["attn_w_qkv"]
    dx, dnm0, _ = _mm_nt_normbwd("attn_bwd_in", [(dq, wqkv[:, :qw], None), (dkv, wqkv[:, qw:], None)], x, nm[0:1], dh1)

    n_qkv = wqkv.shape[1] // N_CHIPS
    dwqkv = jnp.concatenate([dwq, dwkv], axis=1)
    g["attn_w_qkv"] = [jnp.moveaxis(dwqkv.reshape(D, N_CHIPS, n_qkv), 1, 0)]
    g["attn_w_o"] = [g["attn_w_o"].reshape(N_CHIPS, -1, D)]
    g["conv_w_pw1"] = [g["conv_w_pw1"]]
    g["conv_w_pw2"] = [g["conv_w_pw2"].reshape(N_CHIPS, -1, D)]
    g["ffn_w1"] = [dw1_0, dw1_1]
    g["ffn_w3"] = [dw3_0, dw3_1]
    g["ffn_w2"] = [dw2_0, dw2_1]
    g["norm_mix"] = jnp.concatenate([dnm0, dnm1], axis=0)
    g["norm_ffn"] = jnp.concatenate([dnf0, dnf1], axis=0)
    g["attn_b_qkv"] = jnp.concatenate([dbq, dbkv], axis=1)
    g["attn_sinks"] = d_sinks
    g["attn_b_o"] = db_o
    g["conv_b_pw1"] = d_b_pw1
    g["conv_w_dw"] = d_w_dw[:CONV_WIDTH]
    g["conv_b_dw"] = d_b_dw
    g["conv_ln_g"] = d_ln_g
    g["conv_ln_b"] = d_ln_b
    g["conv_b_pw2"] = db_pw2
    g["norm_final"] = d_norm_final
    return loss, dx, g


ANY = pl.BlockSpec(memory_space=pl.ANY)
VMEM_WHOLE = pl.BlockSpec(memory_space=pltpu.VMEM)


def _my_place():
    return lax.axis_index("x"), lax.axis_index("y"), lax.axis_index("c")


def _other_chips(x, y):
    places = [(1 - x, y), (x, 1 - y), (1 - x, 1 - y)]
    return [(bx, by, 2 * bx + by) for bx, by in places]


def _exchange_small(name, v, reduce):
    r, w = v.shape

    def body(v_ref, o_ref, *rest):
        if reduce:
            buf, send_sems, recv_sems = rest
        else:
            buf = o_ref
            send_sems, recv_sems = rest
        x, y, c = _my_place()
        me = 4 * x + 2 * y + c
        sends = []
        for k in range(1, N_DEV):
            peer = (1 - x if k & 4 else x, 1 - y if k & 2 else y, 1 - c if k & 1 else c)
            cp = pltpu.make_async_remote_copy(
                src_ref=v_ref, dst_ref=buf.at[me], send_sem=send_sems.at[k - 1], recv_sem=recv_sems.at[k - 1],
                device_id=peer, device_id_type=MESH)
            cp.start()
            sends.append(cp)
        buf[me] = v_ref[...]
        for k in range(1, N_DEV):
            src = 4 * (1 - x if k & 4 else x) + 2 * (1 - y if k & 2 else y) + (1 - c if k & 1 else c)
            pltpu.make_async_remote_copy(
                src_ref=v_ref, dst_ref=buf.at[src], send_sem=send_sems.at[k - 1], recv_sem=recv_sems.at[k - 1],
                device_id=(x, y, c), device_id_type=MESH).wait_recv()
        for cp in sends:
            cp.wait_send()
        if reduce:
            acc = buf[0]
            for d in range(1, N_DEV):
                acc = acc + buf[d]
            o_ref[...] = acc

    sems = [pltpu.SemaphoreType.DMA((N_DEV - 1,)), pltpu.SemaphoreType.DMA((N_DEV - 1,))]
    if reduce:
        out_shape = SDS((r, w), F32)
        scratch = [pltpu.VMEM((N_DEV, r, w), F32)] + sems
    else:
        out_shape = SDS((N_DEV, r, w), F32)
        scratch = sems
    return pl.pallas_call(
        body, name=name, out_shape=out_shape, in_specs=[VMEM_WHOLE], out_specs=VMEM_WHOLE,
        scratch_shapes=scratch,
        compiler_params=pltpu.CompilerParams(vmem_limit_bytes=V7X_VMEM_LIMIT_BYTES),
    )(v)


def _gather_weights(shards):
    n_w = len(shards)
    per_w = 6

    def body(*refs):
        w_refs, o_refs = refs[:n_w], refs[n_w:2 * n_w]
        send_sems, recv_sems, local_sems = refs[2 * n_w:]
        x, y, c = _my_place()
        k = 2 * x + y
        chips = _other_chips(x, y)

        def copy(sem, src, dst, to):
            return pltpu.make_async_remote_copy(
                src_ref=src, dst_ref=dst, send_sem=send_sems.at[sem], recv_sem=recv_sems.at[sem],
                device_id=to, device_id_type=MESH)

        def halves(i):
            half = w_refs[i].shape[0] // 2
            return (pl.ds(pl.multiple_of(c * half, 16), half), pl.ds(pl.multiple_of((1 - c) * half, 16), half))

        own, sends = [], []
        for i in range(n_w):
            mine, _ = halves(i)
            cp = pltpu.make_async_copy(w_refs[i], o_refs[i].at[k], local_sems.at[i])
            cp.start()
            own.append(cp)
            for j, (bx, by, _) in enumerate(chips):
                cp = copy(per_w * i + j, w_refs[i].at[mine], o_refs[i].at[k, mine], (bx, by, c))
                cp.start()
                sends.append(cp)
        for i in range(n_w):
            mine, _ = halves(i)
            for j, (_, _, kb) in enumerate(chips):
                landed = o_refs[i].at[kb, mine]
                copy(per_w * i + j, landed, landed, (x, y, c)).wait_recv()
                fwd = copy(per_w * i + 3 + j, landed, landed, (x, y, 1 - c))
                fwd.start()
                sends.append(fwd)
        for i in range(n_w):
            _, theirs = halves(i)
            for j, (_, _, kb) in enumerate(chips):
                landed = o_refs[i].at[kb, theirs]
                copy(per_w * i + 3 + j, landed, landed, (x, y, c)).wait_recv()
        for cp in sends:
            cp.wait_send()
        for cp in own:
            cp.wait()

    return pl.pallas_call(
        body, name="gather_weights",
        out_shape=[SDS((N_CHIPS,) + s.shape, s.dtype) for s in shards],
        in_specs=[ANY] * n_w, out_specs=[ANY] * n_w,
        scratch_shapes=[pltpu.SemaphoreType.DMA((per_w * n_w,)), pltpu.SemaphoreType.DMA((per_w * n_w,)),
                        pltpu.SemaphoreType.DMA((n_w,))],
    )(*shards)


def _swap_halves_with_sibling(grads):
    n_g = len(grads)

    def body(*refs):
        g_refs, o_refs = refs[:n_g], refs[n_g:2 * n_g]
        send_sems, recv_sems = refs[2 * n_g:]
        x, y, c = _my_place()
        copies = []
        for i in range(n_g):
            half = g_refs[i].shape[1] // 2
            theirs = pl.ds(pl.multiple_of((1 - c) * half, 8), half)
            cp = pltpu.make_async_remote_copy(
                src_ref=g_refs[i].at[:, theirs], dst_ref=o_refs[i], send_sem=send_sems.at[i],
                recv_sem=recv_sems.at[i], device_id=(x, y, 1 - c), device_id_type=MESH)
            cp.start()
            copies.append(cp)
        for cp in copies:
            cp.wait()

    return pl.pallas_call(
        body, name="grads_to_sibling",
        out_shape=[SDS((g.shape[0], g.shape[1] // 2, g.shape[2]), g.dtype) for g in grads],
        in_specs=[ANY] * n_g, out_specs=[ANY] * n_g,
        scratch_shapes=[pltpu.SemaphoreType.DMA((n_g,)), pltpu.SemaphoreType.DMA((n_g,))],
    )(*grads)


def _pack_row_tile(rows):
    for t in range(min(rows, 512), 7, -1):
        if rows % t == 0 and t % 8 == 0:
            return t
    return rows


def _add_sibling_half(name, grads, from_sibling, c_idx):
    n, R, w = grads.shape
    half = R // 2
    tr = _pack_row_tile(half)
    steps = half // tr

    def body(c_ref, g_ref, s_ref, o_ref):
        o_ref[...] = g_ref[...] + s_ref[...]

    return pl.pallas_call(
        body, name=name,
        grid_spec=pltpu.PrefetchScalarGridSpec(
            num_scalar_prefetch=1, grid=(n, steps),
            in_specs=[pl.BlockSpec((1, tr, w), lambda j, i, c_ref: (j, c_ref[0] * steps + i, 0)),
                      pl.BlockSpec((1, tr, w), lambda j, i, c_ref: (j, i, 0))],
            out_specs=pl.BlockSpec((1, tr, w), lambda j, i, c_ref: (j, i, 0))),
        out_shape=SDS((n, half, w), F32),
        compiler_params=_params("parallel", "parallel"),
    )(c_idx, grads, from_sibling)


def _scatter_to_chips(partials):
    n_p = len(partials)

    def body(*refs):
        p_refs, o_refs = refs[:n_p], refs[n_p:2 * n_p]
        send_sems, recv_sems = refs[2 * n_p:]
        x, y, c = _my_place()
        sends = []
        for i in range(n_p):
            for j, (bx, by, kb) in enumerate(_other_chips(x, y)):
                cp = pltpu.make_async_remote_copy(
                    src_ref=p_refs[i].at[kb], dst_ref=o_refs[i].at[j], send_sem=send_sems.at[3 * i + j],
                    recv_sem=recv_sems.at[3 * i + j], device_id=(bx, by, c), device_id_type=MESH)
                cp.start()
                sends.append(cp)
        for cp in sends:
            cp.wait()

    return pl.pallas_call(
        body, name="grads_to_chips",
        out_shape=[SDS((N_CHIPS - 1,) + p.shape[1:], p.dtype) for p in partials],
        in_specs=[ANY] * n_p, out_specs=[ANY] * n_p,
        scratch_shapes=[pltpu.SemaphoreType.DMA((3 * n_p,)), pltpu.SemaphoreType.DMA((3 * n_p,))],
    )(*partials)


def _sum_chip_partials(name, partial, received, k_idx):
    n, half, w = partial.shape
    tr = _pack_row_tile(half)

    def body(k_ref, p_ref, r_ref, o_ref):
        o_ref[...] = ((p_ref[0] + r_ref[0]) + r_ref[1]) + r_ref[2]

    return pl.pallas_call(
        body, name=name,
        grid_spec=pltpu.PrefetchScalarGridSpec(
            num_scalar_prefetch=1, grid=(half // tr,),
            in_specs=[pl.BlockSpec((1, tr, w), lambda i, k_ref: (k_ref[0], i, 0)),
                      pl.BlockSpec((n - 1, tr, w), lambda i, k_ref: (0, i, 0))],
            out_specs=pl.BlockSpec((tr, w), lambda i, k_ref: (i, 0))),
        out_shape=SDS((half, w), F32),
        compiler_params=_params("parallel"),
    )(k_idx, partial, received)


def _join_halves(halves, groups):
    n_h = len(halves)
    where = {}
    for wi, items in enumerate(groups):
        for layer, item in enumerate(items):
            where[item] = (wi, layer)

    def body(*refs):
        h_refs, o_refs = refs[:n_h], refs[n_h:n_h + len(groups)]
        send_sems, recv_sems, local_sems = refs[n_h + len(groups):]
        x, y, c = _my_place()
        copies = []
        for i in range(n_h):
            wi, layer = where[i]
            half = h_refs[i].shape[0]
            rows = pl.ds(pl.multiple_of(layer * 2 * half + c * half, 8), half)
            own = pltpu.make_async_copy(h_refs[i], o_refs[wi].at[rows], local_sems.at[i])
            own.start()
            cp = pltpu.make_async_remote_copy(
                src_ref=h_refs[i], dst_ref=o_refs[wi].at[rows], send_sem=send_sems.at[i], recv_sem=recv_sems.at[i],
                device_id=(x, y, 1 - c), device_id_type=MESH)
            cp.start()
            copies += [own, cp]
        for cp in copies:
            cp.wait()

    out_shape = [SDS((2 * halves[items[0]].shape[0] * len(items), halves[items[0]].shape[1]), F32) for items in groups]
    return pl.pallas_call(
        body, name="join_halves", out_shape=out_shape,
        in_specs=[ANY] * n_h, out_specs=[ANY] * len(groups),
        scratch_shapes=[pltpu.SemaphoreType.DMA((n_h,)), pltpu.SemaphoreType.DMA((n_h,)),
                        pltpu.SemaphoreType.DMA((n_h,))],
    )(*halves)


def _adamw(name, w, g, m, v):
    rows, width = w.shape
    tr = _pack_row_tile(rows)

    def body(w_ref, g_ref, m_ref, v_ref, d_ref, nm_ref, nv_ref):
        gg = g_ref[...]
        m_new = ADAM_B1 * m_ref[...] + (1.0 - ADAM_B1) * gg
        v_new = ADAM_B2 * v_ref[...] + (1.0 - ADAM_B2) * (gg * gg)
        m_hat = m_new / (1.0 - ADAM_B1 ** ADAM_STEP)
        v_hat = v_new / (1.0 - ADAM_B2 ** ADAM_STEP)
        d_ref[...] = -ADAM_LR * (m_hat / (jnp.sqrt(v_hat) + ADAM_EPS) + ADAM_WD * w_ref[...])
        nm_ref[...] = m_new
        nv_ref[...] = v_new

    spec = _rows(tr, width)
    return pl.pallas_call(
        body, name=name, grid=(rows // tr,),
        in_specs=[spec] * 4, out_specs=[spec] * 3,
        out_shape=[SDS((rows, width), F32)] * 3,
        compiler_params=_params("parallel"),
    )(w, g, m, v)


WEIGHT_NAMES = ['norm_mix', 'norm_ffn', 'attn_w_qkv', 'attn_b_qkv', 'attn_sinks', 'attn_w_o', 'attn_b_o',
                'conv_w_pw1', 'conv_b_pw1', 'conv_w_dw', 'conv_b_dw', 'conv_ln_g', 'conv_ln_b', 'conv_w_pw2',
                'conv_b_pw2', 'ffn_w1', 'ffn_w3', 'ffn_w2', 'norm_final']
BIG = ['attn_w_qkv', 'attn_w_o', 'conv_w_pw1', 'conv_w_pw2', 'ffn_w1', 'ffn_w3', 'ffn_w2']
COLUMN_SPLIT = ('attn_w_qkv', 'conv_w_pw1', 'ffn_w1', 'ffn_w3')
SMALL_SPLIT = ['conv_b_pw1', 'conv_w_dw', 'conv_b_dw', 'conv_ln_g', 'conv_ln_b', 'conv_b_pw2']
SMALL_WHOLE = ['norm_mix', 'norm_ffn', 'attn_b_qkv', 'attn_sinks', 'attn_b_o', 'norm_final']


def _pack_rows(arrays, dtype, row_multiple):
    flat = jnp.concatenate([a.astype(dtype).reshape(-1) for a in arrays])
    rows = -(-flat.shape[0] // PACK_W)
    rows = -(-rows // row_multiple) * row_multiple
    return jnp.pad(flat, (0, rows * PACK_W - flat.shape[0])).reshape(rows, PACK_W)


def _unpack_rows(pack, shapes):
    flat = pack.reshape(-1)
    out, at = [], 0
    for shape in shapes:
        size = 1
        for s in shape:
            size *= s
        out.append(flat[at:at + size].reshape(shape))
        at += size
    return out


def _join_chip_axis(name, parts):
    axis = parts.ndim - 1 if name in COLUMN_SPLIT or name in SMALL_SPLIT else parts.ndim - 2
    moved = jnp.moveaxis(parts, 0, axis - 1)
    shape = list(moved.shape)
    shape[axis - 1:axis + 1] = [shape[axis - 1] * shape[axis]]
    return moved.reshape(shape)


def _split_chip_axis(name, whole, shard_shape):
    axis = len(shard_shape) - 1 if name in COLUMN_SPLIT or name in SMALL_SPLIT else len(shard_shape) - 2
    shape = list(whole.shape)
    shape[axis:axis + 1] = [N_CHIPS, shard_shape[axis]]
    return jnp.moveaxis(whole.reshape(shape), axis, 0)


def kernel(x, norm_mix, norm_ffn, attn_w_qkv, attn_b_qkv, attn_sinks, attn_w_o, attn_b_o, conv_w_pw1, conv_b_pw1, conv_w_dw, conv_b_dw, conv_ln_g, conv_ln_b, conv_w_pw2, conv_b_pw2, ffn_w1, ffn_w3, ffn_w2, norm_final, loss_target, m_norm_mix, m_norm_ffn, m_attn_w_qkv, m_attn_b_qkv, m_attn_sinks, m_attn_w_o, m_attn_b_o, m_conv_w_pw1, m_conv_b_pw1, m_conv_w_dw, m_conv_b_dw, m_conv_ln_g, m_conv_ln_b, m_conv_w_pw2, m_conv_b_pw2, m_ffn_w1, m_ffn_w3, m_ffn_w2, m_norm_final, v_norm_mix, v_norm_ffn, v_attn_w_qkv, v_attn_b_qkv, v_attn_sinks, v_attn_w_o, v_attn_b_o, v_conv_w_pw1, v_conv_b_pw1, v_conv_w_dw, v_conv_b_dw, v_conv_ln_g, v_conv_ln_b, v_conv_w_pw2, v_conv_b_pw2, v_ffn_w1, v_ffn_w3, v_ffn_w2, v_norm_final):
    w = dict(zip(WEIGHT_NAMES, (norm_mix, norm_ffn, attn_w_qkv, attn_b_qkv, attn_sinks, attn_w_o, attn_b_o,
                                conv_w_pw1, conv_b_pw1, conv_w_dw, conv_b_dw, conv_ln_g, conv_ln_b, conv_w_pw2,
                                conv_b_pw2, ffn_w1, ffn_w3, ffn_w2, norm_final)))
    m = dict(zip(WEIGHT_NAMES, (m_norm_mix, m_norm_ffn, m_attn_w_qkv, m_attn_b_qkv, m_attn_sinks, m_attn_w_o,
                                m_attn_b_o, m_conv_w_pw1, m_conv_b_pw1, m_conv_w_dw, m_conv_b_dw, m_conv_ln_g,
                                m_conv_ln_b, m_conv_w_pw2, m_conv_b_pw2, m_ffn_w1, m_ffn_w3, m_ffn_w2, m_norm_final)))
    v = dict(zip(WEIGHT_NAMES, (v_norm_mix, v_norm_ffn, v_attn_w_qkv, v_attn_b_qkv, v_attn_sinks, v_attn_w_o,
                                v_attn_b_o, v_conv_w_pw1, v_conv_b_pw1, v_conv_w_dw, v_conv_b_dw, v_conv_ln_g,
                                v_conv_ln_b, v_conv_w_pw2, v_conv_b_pw2, v_ffn_w1, v_ffn_w3, v_ffn_w2, v_norm_final)))
    T, D = x.shape[1], x.shape[2]
    c_idx = lax.axis_index("c").astype(jnp.int32).reshape(1)
    chip = (2 * lax.axis_index("x") + lax.axis_index("y")).astype(jnp.int32)

    as_rows = lambda a: a.reshape(-1, a.shape[-1])
    gathered = dict(zip(BIG, _gather_weights([as_rows(w[n]).astype(BF16) for n in BIG])))
    layers = ffn_w1.shape[0]
    small_shapes = [w[n].shape for n in SMALL_SPLIT]
    small_all = _exchange_small("gather_small", _pack_rows([w[n] for n in SMALL_SPLIT], F32, 8), reduce=False)
    per_chip = [_unpack_rows(small_all[2 * j], small_shapes) for j in range(N_CHIPS)]
    full = {}
    for i, n in enumerate(SMALL_SPLIT):
        full[n] = _join_chip_axis(n, jnp.stack([per_chip[j][i] for j in range(N_CHIPS)]))

    qkv_parts = gathered["attn_w_qkv"]
    p = {
        "norm_mix": norm_mix, "norm_ffn": norm_ffn, "norm_final": norm_final.reshape(1, D),
        "attn_w_qkv": jnp.moveaxis(qkv_parts, 0, 1).reshape(D, -1), "attn_b_qkv": attn_b_qkv,
        "attn_sinks": attn_sinks, "attn_w_o": gathered["attn_w_o"].reshape(-1, D), "attn_b_o": attn_b_o,
        "conv_w_pw1": gathered["conv_w_pw1"], "conv_b_pw1": full["conv_b_pw1"], "conv_w_dw": full["conv_w_dw"][0],
        "conv_b_dw": full["conv_b_dw"], "conv_ln_g": full["conv_ln_g"], "conv_ln_b": full["conv_ln_b"],
        "conv_w_pw2": gathered["conv_w_pw2"].reshape(-1, D), "conv_b_pw2": full["conv_b_pw2"],
        "ffn_w1": gathered["ffn_w1"].reshape(N_CHIPS, layers, D, -1),
        "ffn_w3": gathered["ffn_w3"].reshape(N_CHIPS, layers, D, -1),
        "ffn_w2": gathered["ffn_w2"].reshape(N_CHIPS, layers, -1, D),
    }
    loss_part, dx, g = _local_step(x[0], loss_target[0], p)
    for n in SMALL_WHOLE + SMALL_SPLIT:
        g[n] = g[n].reshape((-1,) + g[n].shape[-2:]) if w[n].ndim == 3 else g[n].reshape(w[n].shape[:-1] + (-1,))

    items = [it for n in BIG for it in g[n]]
    groups, at = [], 0
    for n in BIG:
        groups.append(list(range(at, at + len(g[n]))))
        at += len(g[n])
    from_sibling = _swap_halves_with_sibling(items)
    partials = [_add_sibling_half(f"add_sibling_half_{i}", it, fs, c_idx)
                for i, (it, fs) in enumerate(zip(items, from_sibling))]
    received = _scatter_to_chips(partials)
    halves = [_sum_chip_partials(f"sum_chip_partials_{i}", pt, rc, chip.reshape(1))
              for i, (pt, rc) in enumerate(zip(partials, received))]
    g_big = dict(zip(BIG, _join_halves(halves, groups)))
    big_out = {}
    for n in BIG:
        d_n, m_n, v_n = _adamw(f"adamw_{n}", as_rows(w[n]), g_big[n], as_rows(m[n]), as_rows(v[n]))
        big_out[n] = [a.reshape(w[n].shape) for a in (g_big[n], d_n, m_n, v_n)]

    small_whole_shapes = [w[n].shape for n in SMALL_WHOLE]
    small_full_shapes = [g[n].shape for n in SMALL_SPLIT]
    reduced = _exchange_small(
        "reduce_small", _pack_rows([loss_part] + [g[n] for n in SMALL_WHOLE] + [g[n] for n in SMALL_SPLIT], F32, 8),
        reduce=True)
    pieces = _unpack_rows(reduced, [(1,)] + small_whole_shapes + small_full_shapes)
    loss = pieces[0].reshape(())
    g_small = dict(zip(SMALL_WHOLE, pieces[1:1 + len(SMALL_WHOLE)]))
    for n, whole in zip(SMALL_SPLIT, pieces[1 + len(SMALL_WHOLE):]):
        parts = _split_chip_axis(n, whole, w[n].shape)
        g_small[n] = lax.dynamic_index_in_dim(parts, chip, axis=0, keepdims=False)
    small = SMALL_WHOLE + SMALL_SPLIT
    d_small, m_small, v_small = _adamw(
        "adamw_small", _pack_rows([w[n] for n in small], F32, 8), _pack_rows([g_small[n] for n in small], F32, 8),
        _pack_rows([m[n] for n in small], F32, 8), _pack_rows([v[n] for n in small], F32, 8))

    outs = {}
    for slot, (tag, small_pack) in enumerate((("g", None), ("d", d_small), ("m", m_small), ("v", v_small))):
        vals = {n: big_out[n][slot] for n in BIG}
        if small_pack is None:
            vals.update(g_small)
        else:
            vals.update(zip(small, _unpack_rows(small_pack, [w[n].shape for n in small])))
        outs[tag] = vals
    return (loss, dx.reshape(1, T, D), *[outs["g"][n] for n in WEIGHT_NAMES], *[outs["d"][n] for n in WEIGHT_NAMES],
            *[outs["m"][n] for n in WEIGHT_NAMES], *[outs["v"][n] for n in WEIGHT_NAMES])
```

```python
import functools

import jax
import jax.numpy as jnp
from jax import lax
from jax.experimental import pallas as pl
from jax.experimental.pallas import tpu as pltpu

F32 = jnp.float32
BF16 = jnp.bfloat16
SDS = jax.ShapeDtypeStruct
MESH = pl.DeviceIdType.MESH

HEAD_DIM = 64
N_Q_HEADS = 16
N_KV_HEADS = 2
Q_PER_KV = N_Q_HEADS // N_KV_HEADS
ATTN_BLOCK = 128
ROPE_THETA = 10000.0
CONV_WIDTH = 31
CONV_HALO = 32
RMS_EPS = 1e-5
LN_EPS = 1e-5
ADAM_LR = 0.001
ADAM_B1 = 0.9
ADAM_B2 = 0.999
ADAM_EPS = 1e-08
ADAM_WD = 0.01
ADAM_STEP = 10

V7X_LANES = 128
V7X_VMEM_LIMIT_BYTES = 56 * 1024 * 1024

N_CHIPS = 4
N_DEV = 8
PACK_W = 1024

MASK_VALUE = -1e30


def _params(*semantics):
    return pltpu.CompilerParams(dimension_semantics=semantics, vmem_limit_bytes=V7X_VMEM_LIMIT_BYTES)


def _rows(tm, width):
    return pl.BlockSpec((tm, width), lambda i: (i, 0))


def _whole(shape):
    return pl.BlockSpec(shape, lambda *_: (0,) * len(shape))


def _rms_rstd(h):
    return lax.rsqrt(jnp.mean(h * h, axis=-1, keepdims=True) + RMS_EPS)


def _silu_and_grad(z):
    sg = jax.nn.sigmoid(z)
    return z * sg, sg * (1.0 + z * (1.0 - sg))


def _swap_rope_halves(t):
    w = t.shape[1]
    half = HEAD_DIM // 2
    lane = lax.broadcasted_iota(jnp.int32, t.shape, 1)
    upper = pltpu.roll(t, w - half, 1)
    lower = pltpu.roll(t, half, 1)
    return jnp.where(lane % HEAD_DIM < half, upper, lower)


def _rope(t, cos_ref, sin_ref):
    reps = t.shape[1] // V7X_LANES
    c = jnp.tile(cos_ref[...], (1, reps))
    s = jnp.tile(sin_ref[...], (1, reps))
    return t * c + _swap_rope_halves(t) * s


def _rope_transposed(dt, cos_ref, sin_ref):
    reps = dt.shape[1] // V7X_LANES
    c = jnp.tile(cos_ref[...], (1, reps))
    s = jnp.tile(sin_ref[...], (1, reps))
    return dt * c + _swap_rope_halves(dt * s)


def _rope_tables(seq_len):
    pos = jnp.arange(seq_len, dtype=F32)
    inv_freq = ROPE_THETA ** (-jnp.arange(0, HEAD_DIM, 2, dtype=F32) / HEAD_DIM)
    ang = pos[:, None] * inv_freq[None, :]
    cos, sin = jnp.cos(ang), jnp.sin(ang)
    cos_t = jnp.concatenate([cos, cos, cos, cos], axis=1)
    sin_t = jnp.concatenate([-sin, sin, -sin, sin], axis=1)
    return cos_t, sin_t


def _qkv_proj(h, g, w, b, cos, sin):
    T, D = h.shape
    N = w.shape[1]
    tm = min(512, T)
    rope_w = N - N_KV_HEADS * HEAD_DIM

    def body(h_ref, g_ref, w_ref, b_ref, cos_ref, sin_ref, y_ref, o_ref):
        hh = h_ref[...]
        y = (hh * _rms_rstd(hh) * g_ref[...]).astype(BF16)
        y_ref[...] = y
        acc = jnp.dot(y, w_ref[...], preferred_element_type=F32) + b_ref[...]
        o_ref[:, :rope_w] = _rope(acc[:, :rope_w], cos_ref, sin_ref).astype(BF16)
        o_ref[:, rope_w:] = acc[:, rope_w:].astype(BF16)

    return pl.pallas_call(
        body, name="qkv_proj", grid=(T // tm,),
        in_specs=[_rows(tm, D), _whole((1, D)), _whole((D, N)), _whole((1, N)),
                  _rows(tm, V7X_LANES), _rows(tm, V7X_LANES)],
        out_specs=[_rows(tm, D), _rows(tm, N)],
        out_shape=[SDS((T, D), BF16), SDS((T, N), BF16)],
        compiler_params=_params("parallel"),
    )(h, g, w, b, cos, sin)


def _pw1_proj(h, g, w, b):
    T, D = h.shape
    n = w.shape[2]
    N = N_CHIPS * n
    tm = min(512, T)

    def body(h_ref, g_ref, w_ref, b_ref, y_ref, o_ref):
        hh = h_ref[...]
        y = (hh * _rms_rstd(hh) * g_ref[...]).astype(BF16)
        y_ref[...] = y
        for j in range(N_CHIPS):
            cols = slice(j * n, (j + 1) * n)
            o_ref[:, cols] = jnp.dot(y, w_ref[j], preferred_element_type=F32) + b_ref[:, cols]

    return pl.pallas_call(
        body, name="pw1_proj", grid=(T // tm,),
        in_specs=[_rows(tm, D), _whole((1, D)), _whole((N_CHIPS, D, n)), _whole((1, N))],
        out_specs=[_rows(tm, D), _rows(tm, N)],
        out_shape=[SDS((T, D), BF16), SDS((T, N), F32)],
        compiler_params=_params("parallel"),
    )(h, g, w, b)


def _band_mask(n):
    row = lax.broadcasted_iota(jnp.int32, (ATTN_BLOCK, 2 * ATTN_BLOCK), 0)
    col = lax.broadcasted_iota(jnp.int32, (ATTN_BLOCK, 2 * ATTN_BLOCK), 1)
    band = (col > row) & (col <= row + ATTN_BLOCK) & ((col >= ATTN_BLOCK) | (n > 0))
    return jnp.concatenate([band] * Q_PER_KV, axis=0)


def _group_operands(g, q_ref, kc_ref, kp_ref, vc_ref, vp_ref, sink_ref):
    lo = g * HEAD_DIM
    k = jnp.concatenate([kp_ref[:, lo:lo + HEAD_DIM], kc_ref[:, lo:lo + HEAD_DIM]], axis=0)
    v = jnp.concatenate([vp_ref[:, lo:lo + HEAD_DIM], vc_ref[:, lo:lo + HEAD_DIM]], axis=0)
    heads = range(g * Q_PER_KV, (g + 1) * Q_PER_KV)
    q = jnp.concatenate([q_ref[:, h * HEAD_DIM:(h + 1) * HEAD_DIM] for h in heads], axis=0)
    sink = jnp.concatenate(
        [jnp.broadcast_to(sink_ref[0:1, h:h + 1], (ATTN_BLOCK, 1)) for h in heads], axis=0)
    return q, k, v, sink


def _softmax_with_sink(q, k, sink, mask):
    s = lax.dot_general(q, k, (((1,), (1,)), ((), ())), preferred_element_type=F32) * (HEAD_DIM ** -0.5)
    s = jnp.where(mask, s, MASK_VALUE)
    m = jnp.maximum(jnp.max(s, axis=1, keepdims=True), sink)
    p = jnp.exp(s - m)
    e_sink = jnp.exp(sink - m)
    inv = 1.0 / (jnp.sum(p, axis=1, keepdims=True) + e_sink)
    return p * inv, e_sink * inv


def _attn_specs(T):
    nb = T // ATTN_BLOCK
    kcol = N_Q_HEADS * HEAD_DIM // V7X_LANES
    cur = lambda n: jnp.minimum(n, nb - 1)
    prev = lambda n: jnp.maximum(jnp.minimum(n, nb - 1) - 1, 0)
    q_spec = pl.BlockSpec((ATTN_BLOCK, N_Q_HEADS * HEAD_DIM), lambda n: (cur(n), 0))
    kc_spec = pl.BlockSpec((ATTN_BLOCK, V7X_LANES), lambda n: (cur(n), kcol))
    kp_spec = pl.BlockSpec((ATTN_BLOCK, V7X_LANES), lambda n: (prev(n), kcol))
    vc_spec = pl.BlockSpec((ATTN_BLOCK, V7X_LANES), lambda n: (cur(n), kcol + 1))
    vp_spec = pl.BlockSpec((ATTN_BLOCK, V7X_LANES), lambda n: (prev(n), kcol + 1))
    return q_spec, kc_spec, kp_spec, vc_spec, vp_spec


def _attn_fwd(qkv, sinks):
    T = qkv.shape[0]
    nb = T // ATTN_BLOCK
    qw = N_Q_HEADS * HEAD_DIM

    def body(q_ref, kc_ref, kp_ref, vc_ref, vp_ref, sink_ref, o_ref):
        mask = _band_mask(pl.program_id(0))
        for g in range(N_KV_HEADS):
            q, k, v, sink = _group_operands(g, q_ref, kc_ref, kp_ref, vc_ref, vp_ref, sink_ref)
            probs, _ = _softmax_with_sink(q, k, sink, mask)
            o = jnp.dot(probs.astype(BF16), v, preferred_element_type=F32)
            for i in range(Q_PER_KV):
                h = g * Q_PER_KV + i
                o_ref[:, h * HEAD_DIM:(h + 1) * HEAD_DIM] = o[i * ATTN_BLOCK:(i + 1) * ATTN_BLOCK].astype(BF16)

    return pl.pallas_call(
        body, name="attn_fwd", grid=(nb,),
        in_specs=[*_attn_specs(T), _whole((1, N_Q_HEADS))],
        out_specs=_rows(ATTN_BLOCK, qw),
        out_shape=SDS((T, qw), BF16),
        compiler_params=_params("parallel"),
    )(qkv, qkv, qkv, qkv, qkv, sinks)


def _mm_res(name, a, w, b, res):
    T, K = a.shape
    D = w.shape[1]
    tm = min(512, T)

    def body(a_ref, w_ref, b_ref, r_ref, o_ref):
        o_ref[...] = jnp.dot(a_ref[...], w_ref[...], preferred_element_type=F32) + b_ref[...] + r_ref[...]

    return pl.pallas_call(
        body, name=name, grid=(T // tm,),
        in_specs=[_rows(tm, K), _whole((K, D)), _whole((1, D)), _rows(tm, D)],
        out_specs=_rows(tm, D),
        out_shape=SDS((T, D), F32),
        compiler_params=_params("parallel"),
    )(a, w, b, res)


def _ffn_down(name, s, w2, layer, res):
    _, T, n = s.shape
    D = w2.shape[3]
    tm = min(512, T)

    def body(s_ref, w_ref, r_ref, o_ref):
        acc = r_ref[...]
        for j in range(N_CHIPS):
            acc = acc + jnp.dot(s_ref[j], w_ref[j], preferred_element_type=F32)
        o_ref[...] = acc

    return pl.pallas_call(
        body, name=name, grid=(T // tm,),
        in_specs=[pl.BlockSpec((N_CHIPS, tm, n), lambda i: (0, i, 0)),
                  pl.BlockSpec((N_CHIPS, None, n, D), lambda i: (0, layer, 0, 0)), _rows(tm, D)],
        out_specs=_rows(tm, D),
        out_shape=SDS((T, D), F32),
        compiler_params=_params("parallel"),
    )(s, w2, res)


def _ffn_up(name, h, g, w1, w3, layer):
    T, D = h.shape
    n = w1.shape[3]
    tm = min(512, T)

    def body(h_ref, g_ref, w1_ref, w3_ref, f_ref, g1_ref, g3_ref, s_ref):
        @pl.when(pl.program_id(1) == 0)
        def _():
            hh = h_ref[...]
            f_ref[...] = (hh * _rms_rstd(hh) * g_ref[...]).astype(BF16)

        f = f_ref[...]
        a = jnp.dot(f, w1_ref[...], preferred_element_type=F32)
        b = jnp.dot(f, w3_ref[...], preferred_element_type=F32)
        g1_ref[...] = a.astype(BF16)
        g3_ref[...] = b.astype(BF16)
        s_ref[...] = (a * jax.nn.sigmoid(a) * b).astype(BF16)

    slab = pl.BlockSpec((None, tm, n), lambda i, j: (j, i, 0))
    wslab = pl.BlockSpec((None, None, D, n), lambda i, j: (j, layer, 0, 0))
    hidden = SDS((N_CHIPS, T, n), BF16)
    return pl.pallas_call(
        body, name=name, grid=(T // tm, N_CHIPS),
        in_specs=[pl.BlockSpec((tm, D), lambda i, j: (i, 0)), pl.BlockSpec((1, D), lambda i, j: (0, 0)), wslab, wslab],
        out_specs=[pl.BlockSpec((tm, D), lambda i, j: (i, 0)), slab, slab, slab],
        out_shape=[SDS((T, D), BF16), hidden, hidden, hidden],
        compiler_params=_params("parallel", "arbitrary"),
    )(h, g, w1, w3)


def _glu(a, d):
    return a[:, :d] * jax.nn.sigmoid(a[:, d:])


def _conv_tile(T):
    return min(256, T)


def _conv_fwd(a, w_dw, b_dw, ln_g, ln_b):
    T = a.shape[0]
    D = a.shape[1] // 2
    tc = _conv_tile(T)
    per = tc // CONV_HALO

    def body(a_ref, ah_ref, w_ref, bdw_ref, lg_ref, lb_ref, c_ref, act_ref, u_scr):
        i = pl.program_id(0)
        u_scr[0:CONV_HALO, :] = jnp.where(i > 0, _glu(ah_ref[...], D), 0.0)
        u_scr[CONV_HALO:, :] = _glu(a_ref[...], D)
        acc = jnp.zeros((tc, D), F32)
        for j in range(CONV_WIDTH):
            acc = acc + u_scr[pl.ds(CONV_HALO - CONV_WIDTH + 1 + j, tc), :] * w_ref[j:j + 1, :]
        c = acc + bdw_ref[...]
        c_ref[...] = c
        xc = c - jnp.mean(c, axis=-1, keepdims=True)
        z = xc * lax.rsqrt(jnp.mean(xc * xc, axis=-1, keepdims=True) + LN_EPS)
        l = z * lg_ref[...] + lb_ref[...]
        act_ref[...] = (l * jax.nn.sigmoid(l)).astype(BF16)

    return pl.pallas_call(
        body, name="conv_fwd", grid=(T // tc,),
        in_specs=[_rows(tc, 2 * D),
                  pl.BlockSpec((CONV_HALO, 2 * D), lambda i: (jnp.maximum(i * per - 1, 0), 0)),
                  _whole((CONV_WIDTH, D)), _whole((1, D)), _whole((1, D)), _whole((1, D))],
        out_specs=[_rows(tc, D), _rows(tc, D)],
        out_shape=[SDS((T, D), F32), SDS((T, D), BF16)],
        scratch_shapes=[pltpu.VMEM((tc + CONV_HALO, D), F32)],
        compiler_params=_params("parallel"),
    )(a, a, w_dw, b_dw, ln_g, ln_b)


def _final_loss(h, g, target):
    T, D = h.shape
    tm = min(512, T)

    def body(h_ref, g_ref, t_ref, dh_ref, loss_ref, dg_ref):
        @pl.when(pl.program_id(0) == 0)
        def _():
            loss_ref[...] = jnp.zeros_like(loss_ref)
            dg_ref[...] = jnp.zeros_like(dg_ref)

        hh = h_ref[...]
        r = _rms_rstd(hh)
        g = g_ref[...]
        d = hh * r * g - t_ref[...]
        loss_ref[...] += 0.5 * jnp.sum(jnp.mean(d * d, axis=-1, keepdims=True), axis=0, keepdims=True)
        dout = d * (1.0 / D)
        dg_ref[...] += jnp.sum(dout * (hh * r), axis=0, keepdims=True)
        dxh = dout * g
        dh_ref[...] = r * dxh - hh * (r * r * r) * jnp.mean(dxh * hh, axis=-1, keepdims=True)

    return pl.pallas_call(
        body, name="final_loss", grid=(T // tm,),
        in_specs=[_rows(tm, D), _whole((1, D)), _rows(tm, D)],
        out_specs=[_rows(tm, D), _whole((1, 1)), _whole((1, D))],
        out_shape=[SDS((T, D), F32), SDS((1, 1), F32), SDS((1, D), F32)],
        compiler_params=_params("arbitrary"),
    )(h, g, target)


def _ffn_bwd_act(name, dh, w2, layer, g1, g3):
    T, D = dh.shape
    n = w2.shape[2]
    tm = min(512, T)

    def body(dh_ref, w2_ref, g1_ref, g3_ref, dg1_ref, dg3_ref):
        ds = lax.dot_general(dh_ref[...].astype(BF16), w2_ref[...], (((1,), (1,)), ((), ())),
                             preferred_element_type=F32)
        act, dact = _silu_and_grad(g1_ref[...].astype(F32))
        dg1_ref[...] = (ds * g3_ref[...].astype(F32) * dact).astype(BF16)
        dg3_ref[...] = (ds * act).astype(BF16)

    slab = pl.BlockSpec((None, tm, n), lambda i, j: (j, i, 0))
    hidden = SDS((N_CHIPS, T, n), BF16)
    return pl.pallas_call(
        body, name=name, grid=(T // tm, N_CHIPS),
        in_specs=[pl.BlockSpec((tm, D), lambda i, j: (i, 0)),
                  pl.BlockSpec((None, None, n, D), lambda i, j: (j, layer, 0, 0)), slab, slab],
        out_specs=[slab, slab],
        out_shape=[hidden, hidden],
        compiler_params=_params("parallel", "arbitrary"),
    )(dh, w2, g1, g3)


def _dot_tn(a, b):
    return lax.dot_general(a.astype(BF16), b.astype(BF16), (((0,), (0,)), ((), ())), preferred_element_type=F32)


def _dot_nt(a, b):
    return lax.dot_general(a.astype(BF16), b, (((1,), (1,)), ((), ())), preferred_element_type=F32)


def _mm_tn(name, a, b, col_chunks=1):
    a_slabs, b_slabs = a.ndim == 3, b.ndim == 3
    T = a.shape[-2]
    tt = min(512, T)
    ka, nb = a.shape[-1], b.shape[-1]
    if a_slabs or b_slabs:
        out_dims = (N_CHIPS, ka, nb)
    elif col_chunks > 1:
        out_dims = (col_chunks, ka, nb // col_chunks)
    else:
        out_dims = (ka, nb)

    def body(a_ref, b_ref, o_ref):
        @pl.when(pl.program_id(0) == 0)
        def _():
            o_ref[...] = jnp.zeros_like(o_ref)

        if a_slabs:
            bb = b_ref[...].astype(BF16)
            for j in range(N_CHIPS):
                o_ref[j] += _dot_tn(a_ref[j], bb)
        elif b_slabs:
            aa = a_ref[...].astype(BF16)
            for j in range(N_CHIPS):
                o_ref[j] += _dot_tn(aa, b_ref[j])
        elif col_chunks > 1:
            aa = a_ref[...].astype(BF16)
            w = nb // col_chunks
            for j in range(col_chunks):
                o_ref[j] += _dot_tn(aa, b_ref[:, j * w:(j + 1) * w])
        else:
            o_ref[...] += _dot_tn(a_ref[...], b_ref[...])

    def spec(arr, slabs):
        if slabs:
            return pl.BlockSpec((N_CHIPS, tt, arr.shape[-1]), lambda t: (0, t, 0))
        return _rows(tt, arr.shape[-1])

    return pl.pallas_call(
        body, name=name, grid=(T // tt,),
        in_specs=[spec(a, a_slabs), spec(b, b_slabs)],
        out_specs=_whole(out_dims),
        out_shape=SDS(out_dims, F32),
        compiler_params=_params("arbitrary"),
    )(a, b)


def _mm_nt_normbwd(name, pairs, h, g, dh):
    T, D = h.shape
    tm = min(256, T)
    n_pairs = len(pairs)
    kinds = ["slabs" if dy.ndim == 3 else ("quarters" if w.ndim == 3 else "plain") for dy, w, _ in pairs]

    def body(*refs):
        dy_refs = refs[:n_pairs]
        w_refs = refs[n_pairs:2 * n_pairs]
        h_ref, g_ref, dh_ref, o_ref, dg_ref, cs_ref = refs[2 * n_pairs:]

        @pl.when(pl.program_id(0) == 0)
        def _():
            dg_ref[...] = jnp.zeros_like(dg_ref)
            cs_ref[...] = jnp.zeros_like(cs_ref)

        df = jnp.zeros((tm, D), F32)
        for dy_ref, w_ref, kd in zip(dy_refs, w_refs, kinds):
            if kd == "slabs":
                for j in range(N_CHIPS):
                    df = df + _dot_nt(dy_ref[j], w_ref[j])
            elif kd == "quarters":
                n = w_ref.shape[2]
                for j in range(N_CHIPS):
                    df = df + _dot_nt(dy_ref[:, j * n:(j + 1) * n], w_ref[j])
            else:
                df = df + _dot_nt(dy_ref[...], w_ref[...])
        hh = h_ref[...]
        r = _rms_rstd(hh)
        dg_ref[...] += jnp.sum(df * (hh * r), axis=0, keepdims=True)
        dxh = df * g_ref[...]
        out = dh_ref[...] + (r * dxh - hh * (r * r * r) * jnp.mean(dxh * hh, axis=-1, keepdims=True))
        o_ref[...] = out
        cs_ref[...] += jnp.sum(out, axis=0, keepdims=True)

    dy_specs, w_specs = [], []
    for (dy, w, layer), kd in zip(pairs, kinds):
        if kd == "slabs":
            dy_specs.append(pl.BlockSpec((N_CHIPS, tm, dy.shape[2]), lambda i: (0, i, 0)))
            w_specs.append(pl.BlockSpec((N_CHIPS, None, D, w.shape[3]),
                                        functools.partial(lambda i, layer: (0, layer, 0, 0), layer=layer)))
        else:
            dy_specs.append(_rows(tm, dy.shape[1]))
            w_specs.append(_whole(w.shape))

    return pl.pallas_call(
        body, name=name, grid=(T // tm,),
        in_specs=[*dy_specs, *w_specs, _rows(tm, D), _whole((1, D)), _rows(tm, D)],
        out_specs=[_rows(tm, D), _whole((1, D)), _whole((1, D))],
        out_shape=[SDS((T, D), F32), SDS((1, D), F32), SDS((1, D), F32)],
        compiler_params=_params("arbitrary"),
    )(*[dy for dy, _, _ in pairs], *[w for _, w, _ in pairs], h, g, dh)


def _mm_nt(name, dy, w, out_dtype):
    T, N = dy.shape
    K = w.shape[0]
    tm = min(512, T)

    def body(dy_ref, w_ref, o_ref):
        o_ref[...] = lax.dot_general(dy_ref[...].astype(BF16), w_ref[...], (((1,), (1,)), ((), ())),
                                     preferred_element_type=F32).astype(out_dtype)

    return pl.pallas_call(
        body, name=name, grid=(T // tm,),
        in_specs=[_rows(tm, N), _whole((K, N))],
        out_specs=_rows(tm, K),
        out_shape=SDS((T, K), out_dtype),
        compiler_params=_params("parallel"),
    )(dy, w)


def _conv_bwd(dact, c, a, w_dw, ln_g, ln_b):
    T, D = c.shape
    tc = _conv_tile(T)
    per = tc // CONV_HALO
    n_tiles = T // tc
    last_halo = T // CONV_HALO - 1
    first_tap = CONV_HALO - CONV_WIDTH + 1

    def ln_bwd(dact_v, c_v, lg, lb):
        xc = c_v - jnp.mean(c_v, axis=-1, keepdims=True)
        rstd = lax.rsqrt(jnp.mean(xc * xc, axis=-1, keepdims=True) + LN_EPS)
        z = xc * rstd
        _, dsilu = _silu_and_grad(z * lg + lb)
        dl = dact_v * dsilu
        dz = dl * lg
        dc = rstd * (dz - jnp.mean(dz, axis=-1, keepdims=True) - z * jnp.mean(dz * z, axis=-1, keepdims=True))
        return dc, dl, z

    def body(dact_ref, dactn_ref, c_ref, cn_ref, a_ref, ah_ref, w_ref, lg_ref, lb_ref,
             da_ref, dlg_ref, dlb_ref, dbdw_ref, dwdw_ref, dbpw1_ref, dc_scr, u_scr):
        i = pl.program_id(0)

        @pl.when(i == 0)
        def _():
            for ref in (dlg_ref, dlb_ref, dbdw_ref, dwdw_ref, dbpw1_ref):
                ref[...] = jnp.zeros_like(ref)

        lg, lb = lg_ref[...], lb_ref[...]
        dc, dl, z = ln_bwd(dact_ref[...], c_ref[...], lg, lb)
        dlg_ref[...] += jnp.sum(dl * z, axis=0, keepdims=True)
        dlb_ref[...] += jnp.sum(dl, axis=0, keepdims=True)
        dbdw_ref[...] += jnp.sum(dc, axis=0, keepdims=True)
        dcn, _, _ = ln_bwd(dactn_ref[...], cn_ref[...], lg, lb)
        dc_scr[0:tc, :] = dc
        dc_scr[tc:, :] = jnp.where(i < n_tiles - 1, dcn, 0.0)

        a_v = a_ref[...]
        a1 = a_v[:, :D]
        sg = jax.nn.sigmoid(a_v[:, D:])
        u_scr[0:CONV_HALO, :] = jnp.where(i > 0, _glu(ah_ref[...], D), 0.0)
        u_scr[CONV_HALO:, :] = a1 * sg

        du = jnp.zeros((tc, D), F32)
        for j in range(CONV_WIDTH):
            du = du + dc_scr[pl.ds(CONV_WIDTH - 1 - j, tc), :] * w_ref[j:j + 1, :]
            dwdw_ref[j:j + 1, :] += jnp.sum(dc * u_scr[pl.ds(first_tap + j, tc), :], axis=0, keepdims=True)

        da1 = du * sg
        da2 = du * a1 * sg * (1.0 - sg)
        da_ref[:, :D] = da1.astype(BF16)
        da_ref[:, D:] = da2.astype(BF16)
        dbpw1_ref[:, :D] += jnp.sum(da1, axis=0, keepdims=True)
        dbpw1_ref[:, D:] += jnp.sum(da2, axis=0, keepdims=True)

    nxt = lambda i: (jnp.minimum((i + 1) * per, last_halo), 0)
    return pl.pallas_call(
        body, name="conv_bwd", grid=(n_tiles,),
        in_specs=[_rows(tc, D), pl.BlockSpec((CONV_HALO, D), nxt),
                  _rows(tc, D), pl.BlockSpec((CONV_HALO, D), nxt),
                  _rows(tc, 2 * D),
                  pl.BlockSpec((CONV_HALO, 2 * D), lambda i: (jnp.maximum(i * per - 1, 0), 0)),
                  _whole((CONV_WIDTH, D)), _whole((1, D)), _whole((1, D))],
        out_specs=[_rows(tc, 2 * D), _whole((1, D)), _whole((1, D)), _whole((1, D)),
                   _whole((CONV_HALO, D)), _whole((1, 2 * D))],
        out_shape=[SDS((T, 2 * D), BF16), SDS((1, D), F32), SDS((1, D), F32), SDS((1, D), F32),
                   SDS((CONV_HALO, D), F32), SDS((1, 2 * D), F32)],
        scratch_shapes=[pltpu.VMEM((tc + CONV_HALO, D), F32), pltpu.VMEM((tc + CONV_HALO, D), F32)],
        compiler_params=_params("arbitrary"),
    )(dact, dact, c, c, a, a, w_dw, ln_g, ln_b)


def _attn_bwd(qkv, dao, cos, sin, sinks):
    T = qkv.shape[0]
    nb = T // ATTN_BLOCK
    qw = N_Q_HEADS * HEAD_DIM
    kw = N_KV_HEADS * HEAD_DIM

    def body(q_ref, kc_ref, kp_ref, vc_ref, vp_ref, do_ref, cos_ref, sin_ref, cosp_ref, sinp_ref, sink_ref,
             dq_ref, dkv_ref, dsink_ref, dbq_ref, dbkv_ref, carry, prev_scr, cur_scr, dq_scr):
        n = pl.program_id(0)

        @pl.when(n == 0)
        def _():
            for ref in (dsink_ref, dbq_ref, dbkv_ref, carry):
                ref[...] = jnp.zeros_like(ref)

        @pl.when(n == nb)
        def _():
            prev_scr[...] = jnp.zeros_like(prev_scr)

        @pl.when(n < nb)
        def _():
            mask = _band_mask(n)
            for g in range(N_KV_HEADS):
                q, k, v, sink = _group_operands(g, q_ref, kc_ref, kp_ref, vc_ref, vp_ref, sink_ref)
                heads = range(g * Q_PER_KV, (g + 1) * Q_PER_KV)
                do = jnp.concatenate([do_ref[:, h * HEAD_DIM:(h + 1) * HEAD_DIM] for h in heads], axis=0)
                probs, p_sink = _softmax_with_sink(q, k, sink, mask)
                dp = lax.dot_general(do, v, (((1,), (1,)), ((), ())), preferred_element_type=F32)
                delta = jnp.sum(probs * dp, axis=1, keepdims=True)
                ds = (probs * (dp - delta) * (HEAD_DIM ** -0.5)).astype(BF16)
                dsink_rows = -(p_sink * delta)
                dq = jnp.dot(ds, k, preferred_element_type=F32)
                dk = lax.dot_general(ds, q, (((0,), (0,)), ((), ())), preferred_element_type=F32)
                dv = lax.dot_general(probs.astype(BF16), do, (((0,), (0,)), ((), ())), preferred_element_type=F32)
                for i, h in enumerate(heads):
                    rows = slice(i * ATTN_BLOCK, (i + 1) * ATTN_BLOCK)
                    dq_scr[:, h * HEAD_DIM:(h + 1) * HEAD_DIM] = dq[rows]
                    dsink_ref[:, h:h + 1] += jnp.sum(dsink_rows[rows], axis=0, keepdims=True)
                lo = g * HEAD_DIM
                prev_scr[:, lo:lo + HEAD_DIM] = dk[:ATTN_BLOCK]
                cur_scr[:, lo:lo + HEAD_DIM] = dk[ATTN_BLOCK:]
                prev_scr[:, kw + lo:kw + lo + HEAD_DIM] = dv[:ATTN_BLOCK]
                cur_scr[:, kw + lo:kw + lo + HEAD_DIM] = dv[ATTN_BLOCK:]
            dq_pre = _rope_transposed(dq_scr[...], cos_ref, sin_ref)
            dq_ref[...] = dq_pre.astype(BF16)
            dbq_ref[...] += jnp.sum(dq_pre, axis=0, keepdims=True)

        tot = carry[...] + prev_scr[...]
        dk_pre = _rope_transposed(tot[:, :kw], cosp_ref, sinp_ref)
        dkv_ref[:, :kw] = dk_pre.astype(BF16)
        dkv_ref[:, kw:] = tot[:, kw:].astype(BF16)
        dbkv_ref[:, :kw] += jnp.sum(dk_pre, axis=0, keepdims=True)
        dbkv_ref[:, kw:] += jnp.sum(tot[:, kw:], axis=0, keepdims=True)

        @pl.when(n < nb)
        def _():
            carry[...] = cur_scr[...]

    cur = lambda n: (jnp.minimum(n, nb - 1), 0)
    out_lag = lambda n: (jnp.maximum(n - 1, 0), 0)
    return pl.pallas_call(
        body, name="attn_bwd", grid=(nb + 1,),
        in_specs=[*_attn_specs(T),
                  pl.BlockSpec((ATTN_BLOCK, qw), cur),
                  pl.BlockSpec((ATTN_BLOCK, V7X_LANES), cur), pl.BlockSpec((ATTN_BLOCK, V7X_LANES), cur),
                  pl.BlockSpec((ATTN_BLOCK, V7X_LANES), out_lag), pl.BlockSpec((ATTN_BLOCK, V7X_LANES), out_lag),
                  _whole((1, N_Q_HEADS))],
        out_specs=[pl.BlockSpec((ATTN_BLOCK, qw), cur), pl.BlockSpec((ATTN_BLOCK, 2 * kw), out_lag),
                   _whole((1, N_Q_HEADS)), _whole((1, qw)), _whole((1, 2 * kw))],
        out_shape=[SDS((T, qw), BF16), SDS((T, 2 * kw), BF16),
                   SDS((1, N_Q_HEADS), F32), SDS((1, qw), F32), SDS((1, 2 * kw), F32)],
        scratch_shapes=[pltpu.VMEM((ATTN_BLOCK, 2 * kw), F32), pltpu.VMEM((ATTN_BLOCK, 2 * kw), F32),
                        pltpu.VMEM((ATTN_BLOCK, 2 * kw), F32), pltpu.VMEM((ATTN_BLOCK, qw), F32)],
        compiler_params=_params("arbitrary"),
    )(qkv, qkv, qkv, qkv, qkv, dao, cos, sin, cos, sin, sinks)


def _local_step(x, target, p, reduce_large):
    T, D = x.shape
    cos, sin = _rope_tables(T)
    qw = N_Q_HEADS * HEAD_DIM
    nm, nf = p["norm_mix"], p["norm_ffn"]

    y0, qkv = _qkv_proj(x, nm[0:1], p["attn_w_qkv"], p["attn_b_qkv"], cos, sin)
    ao = _attn_fwd(qkv, p["attn_sinks"])
    h1 = _mm_res("attn_out", ao, p["attn_w_o"], p["attn_b_o"], x)
    w1, w3, w2 = p["ffn_w1"], p["ffn_w3"], p["ffn_w2"]
    f0, g1a, g3a, s0 = _ffn_up("ffn0_up", h1, nf[0:1], w1, w3, 0)
    h2 = _ffn_down("ffn0_down", s0, w2, 0, h1)
    y1, a = _pw1_proj(h2, nm[1:2], p["conv_w_pw1"], p["conv_b_pw1"])
    c, act = _conv_fwd(a, p["conv_w_dw"], p["conv_b_dw"], p["conv_ln_g"], p["conv_ln_b"])
    h3 = _mm_res("conv_out", act, p["conv_w_pw2"], p["conv_b_pw2"], h2)
    f1, g1b, g3b, s1 = _ffn_up("ffn1_up", h3, nf[1:2], w1, w3, 1)
    h4 = _ffn_down("ffn1_down", s1, w2, 1, h3)
    dh4, loss, d_norm_final = _final_loss(h4, p["norm_final"], target)

    g = {}
    dg1, dg3 = _ffn_bwd_act("ffn1_bwd_act", dh4, w2, 1, g1b, g3b)
    dw2_1 = _mm_tn("ffn1_dw2", s1, dh4)
    dw1_1 = _mm_tn("ffn1_dw1", f1, dg1)
    dw3_1 = _mm_tn("ffn1_dw3", f1, dg3)
    reduce_large("ffn1", {("ffn_w1", 1): dw1_1, ("ffn_w3", 1): dw3_1, ("ffn_w2", 1): dw2_1})
    dh3, dnf1, db_pw2 = _mm_nt_normbwd("ffn1_bwd_in", [(dg1, w1, 1), (dg3, w3, 1)], h3, nf[1:2], dh4)

    dw_pw2 = _mm_tn("conv_dw_pw2", act, dh3)
    dact = _mm_nt("conv_bwd_out", dh3, p["conv_w_pw2"], F32)
    da, d_ln_g, d_ln_b, d_b_dw, d_w_dw, d_b_pw1 = _conv_bwd(dact, c, a, p["conv_w_dw"], p["conv_ln_g"], p["conv_ln_b"])
    dw_pw1 = _mm_tn("conv_dw_pw1", y1, da, col_chunks=N_CHIPS)
    reduce_large("conv", {("conv_w_pw2", 0): dw_pw2.reshape(N_CHIPS, -1, D), ("conv_w_pw1", 0): dw_pw1})
    dh2, dnm1, _ = _mm_nt_normbwd("conv_bwd_in", [(da, p["conv_w_pw1"], None)], h2, nm[1:2], dh3)

    dg1, dg3 = _ffn_bwd_act("ffn0_bwd_act", dh2, w2, 0, g1a, g3a)
    dw2_0 = _mm_tn("ffn0_dw2", s0, dh2)
    dw1_0 = _mm_tn("ffn0_dw1", f0, dg1)
    dw3_0 = _mm_tn("ffn0_dw3", f0, dg3)
    reduce_large("ffn0", {("ffn_w1", 0): dw1_0, ("ffn_w3", 0): dw3_0, ("ffn_w2", 0): dw2_0})
    dh1, dnf0, db_o = _mm_nt_normbwd("ffn0_bwd_in", [(dg1, w1, 0), (dg3, w3, 0)], h1, nf[0:1], dh2)

    dw_o = _mm_tn("attn_dw_o", ao, dh1)
    dao = _mm_nt("attn_bwd_out", dh1, p["attn_w_o"], BF16)
    dq, dkv, d_sinks, dbq, dbkv = _attn_bwd(qkv, dao, cos, sin, p["attn_sinks"])
    dwq = _mm_tn("attn_dw_q", y0, dq)
    dwkv = _mm_tn("attn_dw_kv", y0, dkv)
    wqkv = p["attn_w_qkv"]
    n_qkv = wqkv.shape[1] // N_CHIPS
    dwqkv = jnp.moveaxis(jnp.concatenate([dwq, dwkv], axis=1).reshape(D, N_CHIPS, n_qkv), 1, 0)
    reduce_large("attn", {("attn_w_o", 0): dw_o.reshape(N_CHIPS, -1, D), ("attn_w_qkv", 0): dwqkv})
    dx, dnm0, _ = _mm_nt_normbwd("attn_bwd_in", [(dq, wqkv[:, :qw], None), (dkv, wqkv[:, qw:], None)], x, nm[0:1], dh1)

    g["norm_mix"] = jnp.concatenate([dnm0, dnm1], axis=0)
    g["norm_ffn"] = jnp.concatenate([dnf0, dnf1], axis=0)
    g["attn_b_qkv"] = jnp.concatenate([dbq, dbkv], axis=1)
    g["attn_sinks"] = d_sinks
    g["attn_b_o"] = db_o
    g["conv_b_pw1"] = d_b_pw1
    g["conv_w_dw"] = d_w_dw[:CONV_WIDTH]
    g["conv_b_dw"] = d_b_dw
    g["conv_ln_g"] = d_ln_g
    g["conv_ln_b"] = d_ln_b
    g["conv_b_pw2"] = db_pw2
    g["norm_final"] = d_norm_final
    return loss, dx, g


ANY = pl.BlockSpec(memory_space=pl.ANY)
VMEM_WHOLE = pl.BlockSpec(memory_space=pltpu.VMEM)


def _my_place():
    return lax.axis_index("x"), lax.axis_index("y"), lax.axis_index("c")


def _other_chips(x, y):
    places = [(1 - x, y), (x, 1 - y), (1 - x, 1 - y)]
    return [(bx, by, 2 * bx + by) for bx, by in places]


def _exchange_small(name, v, reduce):
    r, w = v.shape

    def body(v_ref, o_ref, *rest):
        if reduce:
            buf, send_sems, recv_sems = rest
        else:
            buf = o_ref
            send_sems, recv_sems = rest
        x, y, c = _my_place()
        me = 4 * x + 2 * y + c
        sends = []
        for k in range(1, N_DEV):
            peer = (1 - x if k & 4 else x, 1 - y if k & 2 else y, 1 - c if k & 1 else c)
            cp = pltpu.make_async_remote_copy(
                src_ref=v_ref, dst_ref=buf.at[me], send_sem=send_sems.at[k - 1], recv_sem=recv_sems.at[k - 1],
                device_id=peer, device_id_type=MESH)
            cp.start()
            sends.append(cp)
        buf[me] = v_ref[...]
        for k in range(1, N_DEV):
            src = 4 * (1 - x if k & 4 else x) + 2 * (1 - y if k & 2 else y) + (1 - c if k & 1 else c)
            pltpu.make_async_remote_copy(
                src_ref=v_ref, dst_ref=buf.at[src], send_sem=send_sems.at[k - 1], recv_sem=recv_sems.at[k - 1],
                device_id=(x, y, c), device_id_type=MESH).wait_recv()
        for cp in sends:
            cp.wait_send()
        if reduce:
            acc = buf[0]
            for d in range(1, N_DEV):
                acc = acc + buf[d]
            o_ref[...] = acc

    sems = [pltpu.SemaphoreType.DMA((N_DEV - 1,)), pltpu.SemaphoreType.DMA((N_DEV - 1,))]
    if reduce:
        out_shape = SDS((r, w), F32)
        scratch = [pltpu.VMEM((N_DEV, r, w), F32)] + sems
    else:
        out_shape = SDS((N_DEV, r, w), F32)
        scratch = sems
    return pl.pallas_call(
        body, name=name, out_shape=out_shape, in_specs=[VMEM_WHOLE], out_specs=VMEM_WHOLE,
        scratch_shapes=scratch,
        compiler_params=pltpu.CompilerParams(vmem_limit_bytes=V7X_VMEM_LIMIT_BYTES),
    )(v)


def _gather_weights(shards):
    n_w = len(shards)
    per_w = 6

    def body(*refs):
        w_refs, o_refs = refs[:n_w], refs[n_w:2 * n_w]
        send_sems, recv_sems, local_sems = refs[2 * n_w:]
        x, y, c = _my_place()
        k = 2 * x + y
        chips = _other_chips(x, y)

        def copy(sem, src, dst, to):
            return pltpu.make_async_remote_copy(
                src_ref=src, dst_ref=dst, send_sem=send_sems.at[sem], recv_sem=recv_sems.at[sem],
                device_id=to, device_id_type=MESH)

        def halves(i):
            half = w_refs[i].shape[0] // 2
            return (pl.ds(pl.multiple_of(c * half, 16), half), pl.ds(pl.multiple_of((1 - c) * half, 16), half))

        own, sends = [], []
        for i in range(n_w):
            mine, _ = halves(i)
            cp = pltpu.make_async_copy(w_refs[i], o_refs[i].at[k], local_sems.at[i])
            cp.start()
            own.append(cp)
            for j, (bx, by, _) in enumerate(chips):
                cp = copy(per_w * i + j, w_refs[i].at[mine], o_refs[i].at[k, mine], (bx, by, c))
                cp.start()
                sends.append(cp)
        for i in range(n_w):
            mine, _ = halves(i)
            for j, (_, _, kb) in enumerate(chips):
                landed = o_refs[i].at[kb, mine]
                copy(per_w * i + j, landed, landed, (x, y, c)).wait_recv()
                fwd = copy(per_w * i + 3 + j, landed, landed, (x, y, 1 - c))
                fwd.start()
                sends.append(fwd)
        for i in range(n_w):
            _, theirs = halves(i)
            for j, (_, _, kb) in enumerate(chips):
                landed = o_refs[i].at[kb, theirs]
                copy(per_w * i + 3 + j, landed, landed, (x, y, c)).wait_recv()
        for cp in sends:
            cp.wait_send()
        for cp in own:
            cp.wait()

    return pl.pallas_call(
        body, name="gather_weights",
        out_shape=[SDS((N_CHIPS,) + s.shape, s.dtype) for s in shards],
        in_specs=[ANY] * n_w, out_specs=[ANY] * n_w,
        scratch_shapes=[pltpu.SemaphoreType.DMA((per_w * n_w,)), pltpu.SemaphoreType.DMA((per_w * n_w,)),
                        pltpu.SemaphoreType.DMA((n_w,))],
    )(*shards)


def _swap_halves_with_sibling(name, grads):
    n_g = len(grads)

    def body(*refs):
        g_refs, o_refs = refs[:n_g], refs[n_g:2 * n_g]
        send_sems, recv_sems = refs[2 * n_g:]
        x, y, c = _my_place()
        copies = []
        for i in range(n_g):
            half = g_refs[i].shape[1] // 2
            theirs = pl.ds(pl.multiple_of((1 - c) * half, 8), half)
            cp = pltpu.make_async_remote_copy(
                src_ref=g_refs[i].at[:, theirs], dst_ref=o_refs[i], send_sem=send_sems.at[i],
                recv_sem=recv_sems.at[i], device_id=(x, y, 1 - c), device_id_type=MESH)
            cp.start()
            copies.append(cp)
        for cp in copies:
            cp.wait()

    return pl.pallas_call(
        body, name=name,
        out_shape=[SDS((g.shape[0], g.shape[1] // 2, g.shape[2]), g.dtype) for g in grads],
        in_specs=[ANY] * n_g, out_specs=[ANY] * n_g,
        scratch_shapes=[pltpu.SemaphoreType.DMA((n_g,)), pltpu.SemaphoreType.DMA((n_g,))],
    )(*grads)


def _pack_row_tile(rows):
    for t in range(min(rows, 512), 7, -1):
        if rows % t == 0 and t % 8 == 0:
            return t
    return rows


def _add_sibling_half(name, grads, from_sibling, c_idx):
    n, R, w = grads.shape
    half = R // 2
    tr = _pack_row_tile(half)
    steps = half // tr

    def body(c_ref, g_ref, s_ref, o_ref):
        o_ref[...] = g_ref[...] + s_ref[...]

    return pl.pallas_call(
        body, name=name,
        grid_spec=pltpu.PrefetchScalarGridSpec(
            num_scalar_prefetch=1, grid=(n, steps),
            in_specs=[pl.BlockSpec((1, tr, w), lambda j, i, c_ref: (j, c_ref[0] * steps + i, 0)),
                      pl.BlockSpec((1, tr, w), lambda j, i, c_ref: (j, i, 0))],
            out_specs=pl.BlockSpec((1, tr, w), lambda j, i, c_ref: (j, i, 0))),
        out_shape=SDS((n, half, w), F32),
        compiler_params=_params("parallel", "parallel"),
    )(c_idx, grads, from_sibling)


HBM_SPEC = pl.BlockSpec(memory_space=pltpu.HBM)
SEM_SPEC = pl.BlockSpec(memory_space=pltpu.SEMAPHORE)
DATAFLOW = pltpu.SideEffectType.DATAFLOW_SIDE_EFFECTING


def _chip_scatter_copies(p_refs, land_refs, send_sems, recv_sems):
    x, y, c = _my_place()
    return [pltpu.make_async_remote_copy(
        src_ref=p_refs[i].at[kb], dst_ref=land_refs[i].at[j], send_sem=send_sems.at[3 * i + j],
        recv_sem=recv_sems.at[3 * i + j], device_id=(bx, by, c), device_id_type=MESH)
        for i in range(len(p_refs)) for j, (bx, by, kb) in enumerate(_other_chips(x, y))]


def _scatter_start(name, partials):
    n_p = len(partials)

    def body(*refs):
        p_refs, land_refs = refs[:n_p], refs[n_p:2 * n_p]
        send_sems, recv_sems = refs[2 * n_p:2 * n_p + 2]
        for cp in _chip_scatter_copies(p_refs, land_refs, send_sems, recv_sems):
            cp.start()

    lands = [pltpu.with_memory_space_constraint(lax.empty((N_CHIPS - 1,) + p.shape[1:], p.dtype), pltpu.HBM)
             for p in partials]
    out = pl.pallas_call(
        body, name=name,
        out_shape=(pltpu.SemaphoreType.DMA((3 * n_p,)), pltpu.SemaphoreType.DMA((3 * n_p,)),
                   *[pltpu.HBM(p.shape, p.dtype) for p in partials], *[pltpu.HBM(l.shape, l.dtype) for l in lands]),
        in_specs=[HBM_SPEC] * (2 * n_p), out_specs=(SEM_SPEC, SEM_SPEC, *[HBM_SPEC] * (2 * n_p)),
        input_output_aliases={i: 2 + i for i in range(2 * n_p)},
        compiler_params=pltpu.CompilerParams(has_side_effects=DATAFLOW),
    )(*[pltpu.with_memory_space_constraint(p, pltpu.HBM) for p in partials], *lands)
    return out[0], out[1], list(out[2:2 + n_p]), list(out[2 + n_p:])


def _scatter_wait(name, send_sems, recv_sems, partials, lands, after):
    n_p = len(partials)

    def body(*refs):
        p_refs, land_refs = refs[:n_p], refs[n_p:2 * n_p]
        send_sems, recv_sems = refs[2 * n_p:2 * n_p + 2]
        for cp in _chip_scatter_copies(p_refs, land_refs, send_sems, recv_sems):
            cp.wait_send()
            cp.wait_recv()

    out = pl.pallas_call(
        body, name=name,
        out_shape=(*[pltpu.HBM(p.shape, p.dtype) for p in partials], *[pltpu.HBM(l.shape, l.dtype) for l in lands]),
        in_specs=[*[HBM_SPEC] * (2 * n_p), SEM_SPEC, SEM_SPEC, ANY], out_specs=tuple([HBM_SPEC] * (2 * n_p)),
        input_output_aliases={i: i for i in range(2 * n_p)},
        compiler_params=pltpu.CompilerParams(has_side_effects=DATAFLOW),
    )(*partials, *lands, send_sems, recv_sems, after)
    return list(out[:n_p]), list(out[n_p:])


def _sum_chip_partials(name, partial, received, k_idx):
    n, half, w = partial.shape
    tr = _pack_row_tile(half)

    def body(k_ref, p_ref, r_ref, o_ref):
        o_ref[...] = ((p_ref[0] + r_ref[0]) + r_ref[1]) + r_ref[2]

    return pl.pallas_call(
        body, name=name,
        grid_spec=pltpu.PrefetchScalarGridSpec(
            num_scalar_prefetch=1, grid=(half // tr,),
            in_specs=[pl.BlockSpec((1, tr, w), lambda i, k_ref: (k_ref[0], i, 0)),
                      pl.BlockSpec((n - 1, tr, w), lambda i, k_ref: (0, i, 0))],
            out_specs=pl.BlockSpec((tr, w), lambda i, k_ref: (i, 0))),
        out_shape=SDS((half, w), F32),
        compiler_params=_params("parallel"),
    )(k_idx, partial, received)


def _join_halves(halves, groups):
    n_h = len(halves)
    where = {}
    for wi, items in enumerate(groups):
        for layer, item in enumerate(items):
            where[item] = (wi, layer)

    def body(*refs):
        h_refs, o_refs = refs[:n_h], refs[n_h:n_h + len(groups)]
        send_sems, recv_sems, local_sems = refs[n_h + len(groups):]
        x, y, c = _my_place()
        copies = []
        for i in range(n_h):
            wi, layer = where[i]
            half = h_refs[i].shape[0]
            rows = pl.ds(pl.multiple_of(layer * 2 * half + c * half, 8), half)
            own = pltpu.make_async_copy(h_refs[i], o_refs[wi].at[rows], local_sems.at[i])
            own.start()
            cp = pltpu.make_async_remote_copy(
                src_ref=h_refs[i], dst_ref=o_refs[wi].at[rows], send_sem=send_sems.at[i], recv_sem=recv_sems.at[i],
                device_id=(x, y, 1 - c), device_id_type=MESH)
            cp.start()
            copies += [own, cp]
        for cp in copies:
            cp.wait()

    out_shape = [SDS((2 * halves[items[0]].shape[0] * len(items), halves[items[0]].shape[1]), F32) for items in groups]
    return pl.pallas_call(
        body, name="join_halves", out_shape=out_shape,
        in_specs=[ANY] * n_h, out_specs=[ANY] * len(groups),
        scratch_shapes=[pltpu.SemaphoreType.DMA((n_h,)), pltpu.SemaphoreType.DMA((n_h,)),
                        pltpu.SemaphoreType.DMA((n_h,))],
    )(*halves)


def _adamw(name, w, g, m, v):
    rows, width = w.shape
    tr = _pack_row_tile(rows)

    def body(w_ref, g_ref, m_ref, v_ref, d_ref, nm_ref, nv_ref):
        gg = g_ref[...]
        m_new = ADAM_B1 * m_ref[...] + (1.0 - ADAM_B1) * gg
        v_new = ADAM_B2 * v_ref[...] + (1.0 - ADAM_B2) * (gg * gg)
        m_hat = m_new / (1.0 - ADAM_B1 ** ADAM_STEP)
        v_hat = v_new / (1.0 - ADAM_B2 ** ADAM_STEP)
        d_ref[...] = -ADAM_LR * (m_hat / (jnp.sqrt(v_hat) + ADAM_EPS) + ADAM_WD * w_ref[...])
        nm_ref[...] = m_new
        nv_ref[...] = v_new

    spec = _rows(tr, width)
    return pl.pallas_call(
        body, name=name, grid=(rows // tr,),
        in_specs=[spec] * 4, out_specs=[spec] * 3,
        out_shape=[SDS((rows, width), F32)] * 3,
        compiler_params=_params("parallel"),
    )(w, g, m, v)


WEIGHT_NAMES = ['norm_mix', 'norm_ffn', 'attn_w_qkv', 'attn_b_qkv', 'attn_sinks', 'attn_w_o', 'attn_b_o',
                'conv_w_pw1', 'conv_b_pw1', 'conv_w_dw', 'conv_b_dw', 'conv_ln_g', 'conv_ln_b', 'conv_w_pw2',
                'conv_b_pw2', 'ffn_w1', 'ffn_w3', 'ffn_w2', 'norm_final']
BIG = ['attn_w_qkv', 'attn_w_o', 'conv_w_pw1', 'conv_w_pw2', 'ffn_w1', 'ffn_w3', 'ffn_w2']
COLUMN_SPLIT = ('attn_w_qkv', 'conv_w_pw1', 'ffn_w1', 'ffn_w3')
SMALL_SPLIT = ['conv_b_pw1', 'conv_w_dw', 'conv_b_dw', 'conv_ln_g', 'conv_ln_b', 'conv_b_pw2']
SMALL_WHOLE = ['norm_mix', 'norm_ffn', 'attn_b_qkv', 'attn_sinks', 'attn_b_o', 'norm_final']


def _pack_rows(arrays, dtype, row_multiple):
    flat = jnp.concatenate([a.astype(dtype).reshape(-1) for a in arrays])
    rows = -(-flat.shape[0] // PACK_W)
    rows = -(-rows // row_multiple) * row_multiple
    return jnp.pad(flat, (0, rows * PACK_W - flat.shape[0])).reshape(rows, PACK_W)


def _unpack_rows(pack, shapes):
    flat = pack.reshape(-1)
    out, at = [], 0
    for shape in shapes:
        size = 1
        for s in shape:
            size *= s
        out.append(flat[at:at + size].reshape(shape))
        at += size
    return out


def _join_chip_axis(name, parts):
    axis = parts.ndim - 1 if name in COLUMN_SPLIT or name in SMALL_SPLIT else parts.ndim - 2
    moved = jnp.moveaxis(parts, 0, axis - 1)
    shape = list(moved.shape)
    shape[axis - 1:axis + 1] = [shape[axis - 1] * shape[axis]]
    return moved.reshape(shape)


def _split_chip_axis(name, whole, shard_shape):
    axis = len(shard_shape) - 1 if name in COLUMN_SPLIT or name in SMALL_SPLIT else len(shard_shape) - 2
    shape = list(whole.shape)
    shape[axis:axis + 1] = [N_CHIPS, shard_shape[axis]]
    return jnp.moveaxis(whole.reshape(shape), axis, 0)


def kernel(x, norm_mix, norm_ffn, attn_w_qkv, attn_b_qkv, attn_sinks, attn_w_o, attn_b_o, conv_w_pw1, conv_b_pw1, conv_w_dw, conv_b_dw, conv_ln_g, conv_ln_b, conv_w_pw2, conv_b_pw2, ffn_w1, ffn_w3, ffn_w2, norm_final, loss_target, m_norm_mix, m_norm_ffn, m_attn_w_qkv, m_attn_b_qkv, m_attn_sinks, m_attn_w_o, m_attn_b_o, m_conv_w_pw1, m_conv_b_pw1, m_conv_w_dw, m_conv_b_dw, m_conv_ln_g, m_conv_ln_b, m_conv_w_pw2, m_conv_b_pw2, m_ffn_w1, m_ffn_w3, m_ffn_w2, m_norm_final, v_norm_mix, v_norm_ffn, v_attn_w_qkv, v_attn_b_qkv, v_attn_sinks, v_attn_w_o, v_attn_b_o, v_conv_w_pw1, v_conv_b_pw1, v_conv_w_dw, v_conv_b_dw, v_conv_ln_g, v_conv_ln_b, v_conv_w_pw2, v_conv_b_pw2, v_ffn_w1, v_ffn_w3, v_ffn_w2, v_norm_final):
    w = dict(zip(WEIGHT_NAMES, (norm_mix, norm_ffn, attn_w_qkv, attn_b_qkv, attn_sinks, attn_w_o, attn_b_o,
                                conv_w_pw1, conv_b_pw1, conv_w_dw, conv_b_dw, conv_ln_g, conv_ln_b, conv_w_pw2,
                                conv_b_pw2, ffn_w1, ffn_w3, ffn_w2, norm_final)))
    m = dict(zip(WEIGHT_NAMES, (m_norm_mix, m_norm_ffn, m_attn_w_qkv, m_attn_b_qkv, m_attn_sinks, m_attn_w_o,
                                m_attn_b_o, m_conv_w_pw1, m_conv_b_pw1, m_conv_w_dw, m_conv_b_dw, m_conv_ln_g,
                                m_conv_ln_b, m_conv_w_pw2, m_conv_b_pw2, m_ffn_w1, m_ffn_w3, m_ffn_w2, m_norm_final)))
    v = dict(zip(WEIGHT_NAMES, (v_norm_mix, v_norm_ffn, v_attn_w_qkv, v_attn_b_qkv, v_attn_sinks, v_attn_w_o,
                                v_attn_b_o, v_conv_w_pw1, v_conv_b_pw1, v_conv_w_dw, v_conv_b_dw, v_conv_ln_g,
                                v_conv_ln_b, v_conv_w_pw2, v_conv_b_pw2, v_ffn_w1, v_ffn_w3, v_ffn_w2, v_norm_final)))
    T, D = x.shape[1], x.shape[2]
    c_idx = lax.axis_index("c").astype(jnp.int32).reshape(1)
    chip = (2 * lax.axis_index("x") + lax.axis_index("y")).astype(jnp.int32)

    as_rows = lambda a: a.reshape(-1, a.shape[-1])
    gathered = dict(zip(BIG, _gather_weights([as_rows(w[n]).astype(BF16) for n in BIG])))
    layers = ffn_w1.shape[0]
    small_shapes = [w[n].shape for n in SMALL_SPLIT]
    small_all = _exchange_small("gather_small", _pack_rows([w[n] for n in SMALL_SPLIT], F32, 8), reduce=False)
    per_chip = [_unpack_rows(small_all[2 * j], small_shapes) for j in range(N_CHIPS)]
    full = {}
    for i, n in enumerate(SMALL_SPLIT):
        full[n] = _join_chip_axis(n, jnp.stack([per_chip[j][i] for j in range(N_CHIPS)]))

    qkv_parts = gathered["attn_w_qkv"]
    p = {
        "norm_mix": norm_mix, "norm_ffn": norm_ffn, "norm_final": norm_final.reshape(1, D),
        "attn_w_qkv": jnp.moveaxis(qkv_parts, 0, 1).reshape(D, -1), "attn_b_qkv": attn_b_qkv,
        "attn_sinks": attn_sinks, "attn_w_o": gathered["attn_w_o"].reshape(-1, D), "attn_b_o": attn_b_o,
        "conv_w_pw1": gathered["conv_w_pw1"], "conv_b_pw1": full["conv_b_pw1"], "conv_w_dw": full["conv_w_dw"][0],
        "conv_b_dw": full["conv_b_dw"], "conv_ln_g": full["conv_ln_g"], "conv_ln_b": full["conv_ln_b"],
        "conv_w_pw2": gathered["conv_w_pw2"].reshape(-1, D), "conv_b_pw2": full["conv_b_pw2"],
        "ffn_w1": gathered["ffn_w1"].reshape(N_CHIPS, layers, D, -1),
        "ffn_w3": gathered["ffn_w3"].reshape(N_CHIPS, layers, D, -1),
        "ffn_w2": gathered["ffn_w2"].reshape(N_CHIPS, layers, -1, D),
    }
    in_flight = []

    def reduce_large(tag, grads):
        keys = list(grads)
        from_sibling = _swap_halves_with_sibling(f"grads_to_sibling_{tag}", [grads[k] for k in keys])
        partials = [_add_sibling_half(f"add_sibling_half_{tag}{i}", grads[k], fs, c_idx)
                    for i, (k, fs) in enumerate(zip(keys, from_sibling))]
        in_flight.append((tag, keys, _scatter_start(f"scatter_start_{tag}", partials)))

    loss_part, dx, g = _local_step(x[0], loss_target[0], p, reduce_large)
    for n in SMALL_WHOLE + SMALL_SPLIT:
        g[n] = g[n].reshape((-1,) + g[n].shape[-2:]) if w[n].ndim == 3 else g[n].reshape(w[n].shape[:-1] + (-1,))

    reduced_half = {}
    for tag, keys, (send_sems, recv_sems, partials, lands) in in_flight:
        partials, received = _scatter_wait(f"scatter_wait_{tag}", send_sems, recv_sems, partials, lands, dx)
        for i, k in enumerate(keys):
            reduced_half[k] = _sum_chip_partials(f"sum_chip_partials_{tag}{i}", partials[i], received[i],
                                                 chip.reshape(1))
    order = [(n, layer) for n in BIG for layer in range(w[n].shape[0])]
    groups = [[order.index((n, layer)) for layer in range(w[n].shape[0])] for n in BIG]
    g_big = dict(zip(BIG, _join_halves([reduced_half[k] for k in order], groups)))
    big_out = {}
    for n in BIG:
        d_n, m_n, v_n = _adamw(f"adamw_{n}", as_rows(w[n]), g_big[n], as_rows(m[n]), as_rows(v[n]))
        big_out[n] = [a.reshape(w[n].shape) for a in (g_big[n], d_n, m_n, v_n)]

    small_whole_shapes = [w[n].shape for n in SMALL_WHOLE]
    small_full_shapes = [g[n].shape for n in SMALL_SPLIT]
    reduced = _exchange_small(
        "reduce_small", _pack_rows([loss_part] + [g[n] for n in SMALL_WHOLE] + [g[n] for n in SMALL_SPLIT], F32, 8),
        reduce=True)
    pieces = _unpack_rows(reduced, [(1,)] + small_whole_shapes + small_full_shapes)
    loss = pieces[0].reshape(())
    g_small = dict(zip(SMALL_WHOLE, pieces[1:1 + len(SMALL_WHOLE)]))
    for n, whole in zip(SMALL_SPLIT, pieces[1 + len(SMALL_WHOLE):]):
        parts = _split_chip_axis(n, whole, w[n].shape)
        g_small[n] = lax.dynamic_index_in_dim(parts, chip, axis=0, keepdims=False)
    small = SMALL_WHOLE + SMALL_SPLIT
    d_small, m_small, v_small = _adamw(
        "adamw_small", _pack_rows([w[n] for n in small], F32, 8), _pack_rows([g_small[n] for n in small], F32, 8),
        _pack_rows([m[n] for n in small], F32, 8), _pack_rows([v[n] for n in small], F32, 8))

    outs = {}
    for slot, (tag, small_pack) in enumerate((("g", None), ("d", d_small), ("m", m_small), ("v", v_small))):
        vals = {n: big_out[n][slot] for n in BIG}
        if small_pack is None:
            vals.update(g_small)
        else:
            vals.update(zip(small, _unpack_rows(small_pack, [w[n].shape for n in small])))
        outs[tag] = vals
    return (loss, dx.reshape(1, T, D), *[outs["g"][n] for n in WEIGHT_NAMES], *[outs["d"][n] for n in WEIGHT_NAMES],
            *[outs["m"][n] for n in WEIGHT_NAMES], *[outs["v"][n] for n in WEIGHT_NAMES])
```

```python
import functools

import jax
import jax.numpy as jnp
from jax import lax
from jax.experimental import pallas as pl
from jax.experimental.pallas import tpu as pltpu

F32 = jnp.float32
BF16 = jnp.bfloat16
SDS = jax.ShapeDtypeStruct
MESH = pl.DeviceIdType.MESH

HEAD_DIM = 64
N_Q_HEADS = 16
N_KV_HEADS = 2
Q_PER_KV = N_Q_HEADS // N_KV_HEADS
ATTN_BLOCK = 128
ROPE_THETA = 10000.0
CONV_WIDTH = 31
CONV_HALO = 32
RMS_EPS = 1e-5
LN_EPS = 1e-5
ADAM_LR = 0.001
ADAM_B1 = 0.9
ADAM_B2 = 0.999
ADAM_EPS = 1e-08
ADAM_WD = 0.01
ADAM_STEP = 10

V7X_LANES = 128
V7X_VMEM_LIMIT_BYTES = 56 * 1024 * 1024

N_CHIPS = 4
N_DEV = 8
PACK_W = 1024

MASK_VALUE = -1e30


def _params(*semantics):
    return pltpu.CompilerParams(dimension_semantics=semantics, vmem_limit_bytes=V7X_VMEM_LIMIT_BYTES)


def _rows(tm, width):
    return pl.BlockSpec((tm, width), lambda i: (i, 0))


def _whole(shape):
    return pl.BlockSpec(shape, lambda *_: (0,) * len(shape))


def _rms_rstd(h):
    return lax.rsqrt(jnp.mean(h * h, axis=-1, keepdims=True) + RMS_EPS)


def _silu_and_grad(z):
    sg = jax.nn.sigmoid(z)
    return z * sg, sg * (1.0 + z * (1.0 - sg))


def _swap_rope_halves(t):
    w = t.shape[1]
    half = HEAD_DIM // 2
    lane = lax.broadcasted_iota(jnp.int32, t.shape, 1)
    upper = pltpu.roll(t, w - half, 1)
    lower = pltpu.roll(t, half, 1)
    return jnp.where(lane % HEAD_DIM < half, upper, lower)


def _rope(t, cos_ref, sin_ref):
    reps = t.shape[1] // V7X_LANES
    c = jnp.tile(cos_ref[...], (1, reps))
    s = jnp.tile(sin_ref[...], (1, reps))
    return t * c + _swap_rope_halves(t) * s


def _rope_transposed(dt, cos_ref, sin_ref):
    reps = dt.shape[1] // V7X_LANES
    c = jnp.tile(cos_ref[...], (1, reps))
    s = jnp.tile(sin_ref[...], (1, reps))
    return dt * c + _swap_rope_halves(dt * s)


def _rope_tables(seq_len):
    pos = jnp.arange(seq_len, dtype=F32)
    inv_freq = ROPE_THETA ** (-jnp.arange(0, HEAD_DIM, 2, dtype=F32) / HEAD_DIM)
    ang = pos[:, None] * inv_freq[None, :]
    cos, sin = jnp.cos(ang), jnp.sin(ang)
    cos_t = jnp.concatenate([cos, cos, cos, cos], axis=1)
    sin_t = jnp.concatenate([-sin, sin, -sin, sin], axis=1)
    return cos_t, sin_t


def _qkv_proj(h, g, w, b, cos, sin):
    T, D = h.shape
    N = w.shape[1]
    tm = min(512, T)
    rope_w = N - N_KV_HEADS * HEAD_DIM

    def body(h_ref, g_ref, w_ref, b_ref, cos_ref, sin_ref, y_ref, o_ref):
        hh = h_ref[...]
        y = (hh * _rms_rstd(hh) * g_ref[...]).astype(BF16)
        y_ref[...] = y
        acc = jnp.dot(y, w_ref[...], preferred_element_type=F32) + b_ref[...]
        o_ref[:, :rope_w] = _rope(acc[:, :rope_w], cos_ref, sin_ref).astype(BF16)
        o_ref[:, rope_w:] = acc[:, rope_w:].astype(BF16)

    return pl.pallas_call(
        body, name="qkv_proj", grid=(T // tm,),
        in_specs=[_rows(tm, D), _whole((1, D)), _whole((D, N)), _whole((1, N)),
                  _rows(tm, V7X_LANES), _rows(tm, V7X_LANES)],
        out_specs=[_rows(tm, D), _rows(tm, N)],
        out_shape=[SDS((T, D), BF16), SDS((T, N), BF16)],
        compiler_params=_params("parallel"),
    )(h, g, w, b, cos, sin)


def _pw1_proj(h, g, w, b):
    T, D = h.shape
    n = w.shape[2]
    N = N_CHIPS * n
    tm = min(512, T)

    def body(h_ref, g_ref, w_ref, b_ref, y_ref, o_ref):
        hh = h_ref[...]
        y = (hh * _rms_rstd(hh) * g_ref[...]).astype(BF16)
        y_ref[...] = y
        for j in range(N_CHIPS):
            cols = slice(j * n, (j + 1) * n)
            o_ref[:, cols] = jnp.dot(y, w_ref[j], preferred_element_type=F32) + b_ref[:, cols]

    return pl.pallas_call(
        body, name="pw1_proj", grid=(T // tm,),
        in_specs=[_rows(tm, D), _whole((1, D)), _whole((N_CHIPS, D, n)), _whole((1, N))],
        out_specs=[_rows(tm, D), _rows(tm, N)],
        out_shape=[SDS((T, D), BF16), SDS((T, N), F32)],
        compiler_params=_params("parallel"),
    )(h, g, w, b)


def _band_mask(n):
    row = lax.broadcasted_iota(jnp.int32, (ATTN_BLOCK, 2 * ATTN_BLOCK), 0)
    col = lax.broadcasted_iota(jnp.int32, (ATTN_BLOCK, 2 * ATTN_BLOCK), 1)
    band = (col > row) & (col <= row + ATTN_BLOCK) & ((col >= ATTN_BLOCK) | (n > 0))
    return jnp.concatenate([band] * Q_PER_KV, axis=0)


def _group_operands(g, q_ref, kc_ref, kp_ref, vc_ref, vp_ref, sink_ref):
    lo = g * HEAD_DIM
    k = jnp.concatenate([kp_ref[:, lo:lo + HEAD_DIM], kc_ref[:, lo:lo + HEAD_DIM]], axis=0)
    v = jnp.concatenate([vp_ref[:, lo:lo + HEAD_DIM], vc_ref[:, lo:lo + HEAD_DIM]], axis=0)
    heads = range(g * Q_PER_KV, (g + 1) * Q_PER_KV)
    q = jnp.concatenate([q_ref[:, h * HEAD_DIM:(h + 1) * HEAD_DIM] for h in heads], axis=0)
    sink = jnp.concatenate(
        [jnp.broadcast_to(sink_ref[0:1, h:h + 1], (ATTN_BLOCK, 1)) for h in heads], axis=0)
    return q, k, v, sink


def _softmax_with_sink(q, k, sink, mask):
    s = lax.dot_general(q, k, (((1,), (1,)), ((), ())), preferred_element_type=F32) * (HEAD_DIM ** -0.5)
    s = jnp.where(mask, s, MASK_VALUE)
    m = jnp.maximum(jnp.max(s, axis=1, keepdims=True), sink)
    p = jnp.exp(s - m)
    e_sink = jnp.exp(sink - m)
    inv = 1.0 / (jnp.sum(p, axis=1, keepdims=True) + e_sink)
    return p * inv, e_sink * inv


def _attn_specs(T):
    nb = T // ATTN_BLOCK
    kcol = N_Q_HEADS * HEAD_DIM // V7X_LANES
    cur = lambda n: jnp.minimum(n, nb - 1)
    prev = lambda n: jnp.maximum(jnp.minimum(n, nb - 1) - 1, 0)
    q_spec = pl.BlockSpec((ATTN_BLOCK, N_Q_HEADS * HEAD_DIM), lambda n: (cur(n), 0))
    kc_spec = pl.BlockSpec((ATTN_BLOCK, V7X_LANES), lambda n: (cur(n), kcol))
    kp_spec = pl.BlockSpec((ATTN_BLOCK, V7X_LANES), lambda n: (prev(n), kcol))
    vc_spec = pl.BlockSpec((ATTN_BLOCK, V7X_LANES), lambda n: (cur(n), kcol + 1))
    vp_spec = pl.BlockSpec((ATTN_BLOCK, V7X_LANES), lambda n: (prev(n), kcol + 1))
    return q_spec, kc_spec, kp_spec, vc_spec, vp_spec


def _attn_fwd(qkv, sinks):
    T = qkv.shape[0]
    nb = T // ATTN_BLOCK
    qw = N_Q_HEADS * HEAD_DIM

    def body(q_ref, kc_ref, kp_ref, vc_ref, vp_ref, sink_ref, o_ref):
        mask = _band_mask(pl.program_id(0))
        for g in range(N_KV_HEADS):
            q, k, v, sink = _group_operands(g, q_ref, kc_ref, kp_ref, vc_ref, vp_ref, sink_ref)
            probs, _ = _softmax_with_sink(q, k, sink, mask)
            o = jnp.dot(probs.astype(BF16), v, preferred_element_type=F32)
            for i in range(Q_PER_KV):
                h = g * Q_PER_KV + i
                o_ref[:, h * HEAD_DIM:(h + 1) * HEAD_DIM] = o[i * ATTN_BLOCK:(i + 1) * ATTN_BLOCK].astype(BF16)

    return pl.pallas_call(
        body, name="attn_fwd", grid=(nb,),
        in_specs=[*_attn_specs(T), _whole((1, N_Q_HEADS))],
        out_specs=_rows(ATTN_BLOCK, qw),
        out_shape=SDS((T, qw), BF16),
        compiler_params=_params("parallel"),
    )(qkv, qkv, qkv, qkv, qkv, sinks)


def _mm_res(name, a, w, b, res):
    T, K = a.shape
    D = w.shape[1]
    tm = min(512, T)

    def body(a_ref, w_ref, b_ref, r_ref, o_ref):
        o_ref[...] = jnp.dot(a_ref[...], w_ref[...], preferred_element_type=F32) + b_ref[...] + r_ref[...]

    return pl.pallas_call(
        body, name=name, grid=(T // tm,),
        in_specs=[_rows(tm, K), _whole((K, D)), _whole((1, D)), _rows(tm, D)],
        out_specs=_rows(tm, D),
        out_shape=SDS((T, D), F32),
        compiler_params=_params("parallel"),
    )(a, w, b, res)


def _ffn_down(name, s, w2, layer, res):
    _, T, n = s.shape
    D = w2.shape[3]
    tm = min(512, T)

    def body(s_ref, w_ref, r_ref, o_ref):
        acc = r_ref[...]
        for j in range(N_CHIPS):
            acc = acc + jnp.dot(s_ref[j], w_ref[j], preferred_element_type=F32)
        o_ref[...] = acc

    return pl.pallas_call(
        body, name=name, grid=(T // tm,),
        in_specs=[pl.BlockSpec((N_CHIPS, tm, n), lambda i: (0, i, 0)),
                  pl.BlockSpec((N_CHIPS, None, n, D), lambda i: (0, layer, 0, 0)), _rows(tm, D)],
        out_specs=_rows(tm, D),
        out_shape=SDS((T, D), F32),
        compiler_params=_params("parallel"),
    )(s, w2, res)


def _ffn_up(name, h, g, w1, w3, layer):
    T, D = h.shape
    n = w1.shape[3]
    tm = min(512, T)

    def body(h_ref, g_ref, w1_ref, w3_ref, f_ref, g1_ref, g3_ref, s_ref):
        @pl.when(pl.program_id(1) == 0)
        def _():
            hh = h_ref[...]
            f_ref[...] = (hh * _rms_rstd(hh) * g_ref[...]).astype(BF16)

        f = f_ref[...]
        a = jnp.dot(f, w1_ref[...], preferred_element_type=F32)
        b = jnp.dot(f, w3_ref[...], preferred_element_type=F32)
        g1_ref[...] = a.astype(BF16)
        g3_ref[...] = b.astype(BF16)
        s_ref[...] = (a * jax.nn.sigmoid(a) * b).astype(BF16)

    slab = pl.BlockSpec((None, tm, n), lambda i, j: (j, i, 0))
    wslab = pl.BlockSpec((None, None, D, n), lambda i, j: (j, layer, 0, 0))
    hidden = SDS((N_CHIPS, T, n), BF16)
    return pl.pallas_call(
        body, name=name, grid=(T // tm, N_CHIPS),
        in_specs=[pl.BlockSpec((tm, D), lambda i, j: (i, 0)), pl.BlockSpec((1, D), lambda i, j: (0, 0)), wslab, wslab],
        out_specs=[pl.BlockSpec((tm, D), lambda i, j: (i, 0)), slab, slab, slab],
        out_shape=[SDS((T, D), BF16), hidden, hidden, hidden],
        compiler_params=_params("parallel", "arbitrary"),
    )(h, g, w1, w3)


def _glu(a, d):
    return a[:, :d] * jax.nn.sigmoid(a[:, d:])


def _conv_tile(T):
    return min(256, T)


def _conv_fwd(a, w_dw, b_dw, ln_g, ln_b):
    T = a.shape[0]
    D = a.shape[1] // 2
    tc = _conv_tile(T)
    per = tc // CONV_HALO

    def body(a_ref, ah_ref, w_ref, bdw_ref, lg_ref, lb_ref, c_ref, act_ref, u_scr):
        i = pl.program_id(0)
        u_scr[0:CONV_HALO, :] = jnp.where(i > 0, _glu(ah_ref[...], D), 0.0)
        u_scr[CONV_HALO:, :] = _glu(a_ref[...], D)
        acc = jnp.zeros((tc, D), F32)
        for j in range(CONV_WIDTH):
            acc = acc + u_scr[pl.ds(CONV_HALO - CONV_WIDTH + 1 + j, tc), :] * w_ref[j:j + 1, :]
        c = acc + bdw_ref[...]
        c_ref[...] = c
        xc = c - jnp.mean(c, axis=-1, keepdims=True)
        z = xc * lax.rsqrt(jnp.mean(xc * xc, axis=-1, keepdims=True) + LN_EPS)
        l = z * lg_ref[...] + lb_ref[...]
        act_ref[...] = (l * jax.nn.sigmoid(l)).astype(BF16)

    return pl.pallas_call(
        body, name="conv_fwd", grid=(T // tc,),
        in_specs=[_rows(tc, 2 * D),
                  pl.BlockSpec((CONV_HALO, 2 * D), lambda i: (jnp.maximum(i * per - 1, 0), 0)),
                  _whole((CONV_WIDTH, D)), _whole((1, D)), _whole((1, D)), _whole((1, D))],
        out_specs=[_rows(tc, D), _rows(tc, D)],
        out_shape=[SDS((T, D), F32), SDS((T, D), BF16)],
        scratch_shapes=[pltpu.VMEM((tc + CONV_HALO, D), F32)],
        compiler_params=_params("parallel"),
    )(a, a, w_dw, b_dw, ln_g, ln_b)


def _final_loss(h, g, target):
    T, D = h.shape
    tm = min(512, T)

    def body(h_ref, g_ref, t_ref, dh_ref, loss_ref, dg_ref):
        @pl.when(pl.program_id(0) == 0)
        def _():
            loss_ref[...] = jnp.zeros_like(loss_ref)
            dg_ref[...] = jnp.zeros_like(dg_ref)

        hh = h_ref[...]
        r = _rms_rstd(hh)
        g = g_ref[...]
        d = hh * r * g - t_ref[...]
        loss_ref[...] += 0.5 * jnp.sum(jnp.mean(d * d, axis=-1, keepdims=True), axis=0, keepdims=True)
        dout = d * (1.0 / D)
        dg_ref[...] += jnp.sum(dout * (hh * r), axis=0, keepdims=True)
        dxh = dout * g
        dh_ref[...] = r * dxh - hh * (r * r * r) * jnp.mean(dxh * hh, axis=-1, keepdims=True)

    return pl.pallas_call(
        body, name="final_loss", grid=(T // tm,),
        in_specs=[_rows(tm, D), _whole((1, D)), _rows(tm, D)],
        out_specs=[_rows(tm, D), _whole((1, 1)), _whole((1, D))],
        out_shape=[SDS((T, D), F32), SDS((1, 1), F32), SDS((1, D), F32)],
        compiler_params=_params("arbitrary"),
    )(h, g, target)


def _ffn_bwd_act(name, dh, w2, layer, g1, g3):
    T, D = dh.shape
    n = w2.shape[2]
    tm = min(512, T)

    def body(dh_ref, w2_ref, g1_ref, g3_ref, dg1_ref, dg3_ref):
        ds = lax.dot_general(dh_ref[...].astype(BF16), w2_ref[...], (((1,), (1,)), ((), ())),
                             preferred_element_type=F32)
        act, dact = _silu_and_grad(g1_ref[...].astype(F32))
        dg1_ref[...] = (ds * g3_ref[...].astype(F32) * dact).astype(BF16)
        dg3_ref[...] = (ds * act).astype(BF16)

    slab = pl.BlockSpec((None, tm, n), lambda i, j: (j, i, 0))
    hidden = SDS((N_CHIPS, T, n), BF16)
    return pl.pallas_call(
        body, name=name, grid=(T // tm, N_CHIPS),
        in_specs=[pl.BlockSpec((tm, D), lambda i, j: (i, 0)),
                  pl.BlockSpec((None, None, n, D), lambda i, j: (j, layer, 0, 0)), slab, slab],
        out_specs=[slab, slab],
        out_shape=[hidden, hidden],
        compiler_params=_params("parallel", "arbitrary"),
    )(dh, w2, g1, g3)


def _dot_tn(a, b):
    return lax.dot_general(a.astype(BF16), b.astype(BF16), (((0,), (0,)), ((), ())), preferred_element_type=F32)


def _dot_nt(a, b):
    return lax.dot_general(a.astype(BF16), b, (((1,), (1,)), ((), ())), preferred_element_type=F32)


def _mm_tn(name, a, b, col_chunks=1):
    a_slabs, b_slabs = a.ndim == 3, b.ndim == 3
    T = a.shape[-2]
    tt = min(512, T)
    ka, nb = a.shape[-1], b.shape[-1]
    if a_slabs or b_slabs:
        out_dims = (N_CHIPS, ka, nb)
    elif col_chunks > 1:
        out_dims = (col_chunks, ka, nb // col_chunks)
    else:
        out_dims = (ka, nb)

    def body(a_ref, b_ref, o_ref):
        @pl.when(pl.program_id(0) == 0)
        def _():
            o_ref[...] = jnp.zeros_like(o_ref)

        if a_slabs:
            bb = b_ref[...].astype(BF16)
            for j in range(N_CHIPS):
                o_ref[j] += _dot_tn(a_ref[j], bb)
        elif b_slabs:
            aa = a_ref[...].astype(BF16)
            for j in range(N_CHIPS):
                o_ref[j] += _dot_tn(aa, b_ref[j])
        elif col_chunks > 1:
            aa = a_ref[...].astype(BF16)
            w = nb // col_chunks
            for j in range(col_chunks):
                o_ref[j] += _dot_tn(aa, b_ref[:, j * w:(j + 1) * w])
        else:
            o_ref[...] += _dot_tn(a_ref[...], b_ref[...])

    def spec(arr, slabs):
        if slabs:
            return pl.BlockSpec((N_CHIPS, tt, arr.shape[-1]), lambda t: (0, t, 0))
        return _rows(tt, arr.shape[-1])

    return pl.pallas_call(
        body, name=name, grid=(T // tt,),
        in_specs=[spec(a, a_slabs), spec(b, b_slabs)],
        out_specs=_whole(out_dims),
        out_shape=SDS(out_dims, F32),
        compiler_params=_params("arbitrary"),
    )(a, b)


def _mm_nt_normbwd(name, pairs, h, g, dh, after):
    T, D = h.shape
    tm = min(256, T)
    n_pairs = len(pairs)
    kinds = ["slabs" if dy.ndim == 3 else ("quarters" if w.ndim == 3 else "plain") for dy, w, _ in pairs]

    def body(*refs):
        dy_refs = refs[:n_pairs]
        w_refs = refs[n_pairs:2 * n_pairs]
        h_ref, g_ref, dh_ref, _, o_ref, dg_ref, cs_ref = refs[2 * n_pairs:]

        @pl.when(pl.program_id(0) == 0)
        def _():
            dg_ref[...] = jnp.zeros_like(dg_ref)
            cs_ref[...] = jnp.zeros_like(cs_ref)

        df = jnp.zeros((tm, D), F32)
        for dy_ref, w_ref, kd in zip(dy_refs, w_refs, kinds):
            if kd == "slabs":
                for j in range(N_CHIPS):
                    df = df + _dot_nt(dy_ref[j], w_ref[j])
            elif kd == "quarters":
                n = w_ref.shape[2]
                for j in range(N_CHIPS):
                    df = df + _dot_nt(dy_ref[:, j * n:(j + 1) * n], w_ref[j])
            else:
                df = df + _dot_nt(dy_ref[...], w_ref[...])
        hh = h_ref[...]
        r = _rms_rstd(hh)
        dg_ref[...] += jnp.sum(df * (hh * r), axis=0, keepdims=True)
        dxh = df * g_ref[...]
        out = dh_ref[...] + (r * dxh - hh * (r * r * r) * jnp.mean(dxh * hh, axis=-1, keepdims=True))
        o_ref[...] = out
        cs_ref[...] += jnp.sum(out, axis=0, keepdims=True)

    dy_specs, w_specs = [], []
    for (dy, w, layer), kd in zip(pairs, kinds):
        if kd == "slabs":
            dy_specs.append(pl.BlockSpec((N_CHIPS, tm, dy.shape[2]), lambda i: (0, i, 0)))
            w_specs.append(pl.BlockSpec((N_CHIPS, None, D, w.shape[3]),
                                        functools.partial(lambda i, layer: (0, layer, 0, 0), layer=layer)))
        else:
            dy_specs.append(_rows(tm, dy.shape[1]))
            w_specs.append(_whole(w.shape))

    return pl.pallas_call(
        body, name=name, grid=(T // tm,),
        in_specs=[*dy_specs, *w_specs, _rows(tm, D), _whole((1, D)), _rows(tm, D), pl.BlockSpec(memory_space=pl.ANY)],
        out_specs=[_rows(tm, D), _whole((1, D)), _whole((1, D))],
        out_shape=[SDS((T, D), F32), SDS((1, D), F32), SDS((1, D), F32)],
        compiler_params=_params("arbitrary"),
    )(*[dy for dy, _, _ in pairs], *[w for _, w, _ in pairs], h, g, dh, after)


def _mm_nt(name, dy, w, out_dtype):
    T, N = dy.shape
    K = w.shape[0]
    tm = min(512, T)

    def body(dy_ref, w_ref, o_ref):
        o_ref[...] = lax.dot_general(dy_ref[...].astype(BF16), w_ref[...], (((1,), (1,)), ((), ())),
                                     preferred_element_type=F32).astype(out_dtype)

    return pl.pallas_call(
        body, name=name, grid=(T // tm,),
        in_specs=[_rows(tm, N), _whole((K, N))],
        out_specs=_rows(tm, K),
        out_shape=SDS((T, K), out_dtype),
        compiler_params=_params("parallel"),
    )(dy, w)


def _conv_bwd(dact, c, a, w_dw, ln_g, ln_b):
    T, D = c.shape
    tc = _conv_tile(T)
    per = tc // CONV_HALO
    n_tiles = T // tc
    last_halo = T // CONV_HALO - 1
    first_tap = CONV_HALO - CONV_WIDTH + 1

    def ln_bwd(dact_v, c_v, lg, lb):
        xc = c_v - jnp.mean(c_v, axis=-1, keepdims=True)
        rstd = lax.rsqrt(jnp.mean(xc * xc, axis=-1, keepdims=True) + LN_EPS)
        z = xc * rstd
        _, dsilu = _silu_and_grad(z * lg + lb)
        dl = dact_v * dsilu
        dz = dl * lg
        dc = rstd * (dz - jnp.mean(dz, axis=-1, keepdims=True) - z * jnp.mean(dz * z, axis=-1, keepdims=True))
        return dc, dl, z

    def body(dact_ref, dactn_ref, c_ref, cn_ref, a_ref, ah_ref, w_ref, lg_ref, lb_ref,
             da_ref, dlg_ref, dlb_ref, dbdw_ref, dwdw_ref, dbpw1_ref, dc_scr, u_scr):
        i = pl.program_id(0)

        @pl.when(i == 0)
        def _():
            for ref in (dlg_ref, dlb_ref, dbdw_ref, dwdw_ref, dbpw1_ref):
                ref[...] = jnp.zeros_like(ref)

        lg, lb = lg_ref[...], lb_ref[...]
        dc, dl, z = ln_bwd(dact_ref[...], c_ref[...], lg, lb)
        dlg_ref[...] += jnp.sum(dl * z, axis=0, keepdims=True)
        dlb_ref[...] += jnp.sum(dl, axis=0, keepdims=True)
        dbdw_ref[...] += jnp.sum(dc, axis=0, keepdims=True)
        dcn, _, _ = ln_bwd(dactn_ref[...], cn_ref[...], lg, lb)
        dc_scr[0:tc, :] = dc
        dc_scr[tc:, :] = jnp.where(i < n_tiles - 1, dcn, 0.0)

        a_v = a_ref[...]
        a1 = a_v[:, :D]
        sg = jax.nn.sigmoid(a_v[:, D:])
        u_scr[0:CONV_HALO, :] = jnp.where(i > 0, _glu(ah_ref[...], D), 0.0)
        u_scr[CONV_HALO:, :] = a1 * sg

        du = jnp.zeros((tc, D), F32)
        for j in range(CONV_WIDTH):
            du = du + dc_scr[pl.ds(CONV_WIDTH - 1 - j, tc), :] * w_ref[j:j + 1, :]
            dwdw_ref[j:j + 1, :] += jnp.sum(dc * u_scr[pl.ds(first_tap + j, tc), :], axis=0, keepdims=True)

        da1 = du * sg
        da2 = du * a1 * sg * (1.0 - sg)
        da_ref[:, :D] = da1.astype(BF16)
        da_ref[:, D:] = da2.astype(BF16)
        dbpw1_ref[:, :D] += jnp.sum(da1, axis=0, keepdims=True)
        dbpw1_ref[:, D:] += jnp.sum(da2, axis=0, keepdims=True)

    nxt = lambda i: (jnp.minimum((i + 1) * per, last_halo), 0)
    return pl.pallas_call(
        body, name="conv_bwd", grid=(n_tiles,),
        in_specs=[_rows(tc, D), pl.BlockSpec((CONV_HALO, D), nxt),
                  _rows(tc, D), pl.BlockSpec((CONV_HALO, D), nxt),
                  _rows(tc, 2 * D),
                  pl.BlockSpec((CONV_HALO, 2 * D), lambda i: (jnp.maximum(i * per - 1, 0), 0)),
                  _whole((CONV_WIDTH, D)), _whole((1, D)), _whole((1, D))],
        out_specs=[_rows(tc, 2 * D), _whole((1, D)), _whole((1, D)), _whole((1, D)),
                   _whole((CONV_HALO, D)), _whole((1, 2 * D))],
        out_shape=[SDS((T, 2 * D), BF16), SDS((1, D), F32), SDS((1, D), F32), SDS((1, D), F32),
                   SDS((CONV_HALO, D), F32), SDS((1, 2 * D), F32)],
        scratch_shapes=[pltpu.VMEM((tc + CONV_HALO, D), F32), pltpu.VMEM((tc + CONV_HALO, D), F32)],
        compiler_params=_params("arbitrary"),
    )(dact, dact, c, c, a, a, w_dw, ln_g, ln_b)


def _attn_bwd(qkv, dao, cos, sin, sinks):
    T = qkv.shape[0]
    nb = T // ATTN_BLOCK
    qw = N_Q_HEADS * HEAD_DIM
    kw = N_KV_HEADS * HEAD_DIM

    def body(q_ref, kc_ref, kp_ref, vc_ref, vp_ref, do_ref, cos_ref, sin_ref, cosp_ref, sinp_ref, sink_ref,
             dq_ref, dkv_ref, dsink_ref, dbq_ref, dbkv_ref, carry, prev_scr, cur_scr, dq_scr):
        n = pl.program_id(0)

        @pl.when(n == 0)
        def _():
            for ref in (dsink_ref, dbq_ref, dbkv_ref, carry):
                ref[...] = jnp.zeros_like(ref)

        @pl.when(n == nb)
        def _():
            prev_scr[...] = jnp.zeros_like(prev_scr)

        @pl.when(n < nb)
        def _():
            mask = _band_mask(n)
            for g in range(N_KV_HEADS):
                q, k, v, sink = _group_operands(g, q_ref, kc_ref, kp_ref, vc_ref, vp_ref, sink_ref)
                heads = range(g * Q_PER_KV, (g + 1) * Q_PER_KV)
                do = jnp.concatenate([do_ref[:, h * HEAD_DIM:(h + 1) * HEAD_DIM] for h in heads], axis=0)
                probs, p_sink = _softmax_with_sink(q, k, sink, mask)
                dp = lax.dot_general(do, v, (((1,), (1,)), ((), ())), preferred_element_type=F32)
                delta = jnp.sum(probs * dp, axis=1, keepdims=True)
                ds = (probs * (dp - delta) * (HEAD_DIM ** -0.5)).astype(BF16)
                dsink_rows = -(p_sink * delta)
                dq = jnp.dot(ds, k, preferred_element_type=F32)
                dk = lax.dot_general(ds, q, (((0,), (0,)), ((), ())), preferred_element_type=F32)
                dv = lax.dot_general(probs.astype(BF16), do, (((0,), (0,)), ((), ())), preferred_element_type=F32)
                for i, h in enumerate(heads):
                    rows = slice(i * ATTN_BLOCK, (i + 1) * ATTN_BLOCK)
                    dq_scr[:, h * HEAD_DIM:(h + 1) * HEAD_DIM] = dq[rows]
                    dsink_ref[:, h:h + 1] += jnp.sum(dsink_rows[rows], axis=0, keepdims=True)
                lo = g * HEAD_DIM
                prev_scr[:, lo:lo + HEAD_DIM] = dk[:ATTN_BLOCK]
                cur_scr[:, lo:lo + HEAD_DIM] = dk[ATTN_BLOCK:]
                prev_scr[:, kw + lo:kw + lo + HEAD_DIM] = dv[:ATTN_BLOCK]
                cur_scr[:, kw + lo:kw + lo + HEAD_DIM] = dv[ATTN_BLOCK:]
            dq_pre = _rope_transposed(dq_scr[...], cos_ref, sin_ref)
            dq_ref[...] = dq_pre.astype(BF16)
            dbq_ref[...] += jnp.sum(dq_pre, axis=0, keepdims=True)

        tot = carry[...] + prev_scr[...]
        dk_pre = _rope_transposed(tot[:, :kw], cosp_ref, sinp_ref)
        dkv_ref[:, :kw] = dk_pre.astype(BF16)
        dkv_ref[:, kw:] = tot[:, kw:].astype(BF16)
        dbkv_ref[:, :kw] += jnp.sum(dk_pre, axis=0, keepdims=True)
        dbkv_ref[:, kw:] += jnp.sum(tot[:, kw:], axis=0, keepdims=True)

        @pl.when(n < nb)
        def _():
            carry[...] = cur_scr[...]

    cur = lambda n: (jnp.minimum(n, nb - 1), 0)
    out_lag = lambda n: (jnp.maximum(n - 1, 0), 0)
    return pl.pallas_call(
        body, name="attn_bwd", grid=(nb + 1,),
        in_specs=[*_attn_specs(T),
                  pl.BlockSpec((ATTN_BLOCK, qw), cur),
                  pl.BlockSpec((ATTN_BLOCK, V7X_LANES), cur), pl.BlockSpec((ATTN_BLOCK, V7X_LANES), cur),
                  pl.BlockSpec((ATTN_BLOCK, V7X_LANES), out_lag), pl.BlockSpec((ATTN_BLOCK, V7X_LANES), out_lag),
                  _whole((1, N_Q_HEADS))],
        out_specs=[pl.BlockSpec((ATTN_BLOCK, qw), cur), pl.BlockSpec((ATTN_BLOCK, 2 * kw), out_lag),
                   _whole((1, N_Q_HEADS)), _whole((1, qw)), _whole((1, 2 * kw))],
        out_shape=[SDS((T, qw), BF16), SDS((T, 2 * kw), BF16),
                   SDS((1, N_Q_HEADS), F32), SDS((1, qw), F32), SDS((1, 2 * kw), F32)],
        scratch_shapes=[pltpu.VMEM((ATTN_BLOCK, 2 * kw), F32), pltpu.VMEM((ATTN_BLOCK, 2 * kw), F32),
                        pltpu.VMEM((ATTN_BLOCK, 2 * kw), F32), pltpu.VMEM((ATTN_BLOCK, qw), F32)],
        compiler_params=_params("arbitrary"),
    )(qkv, qkv, qkv, qkv, qkv, dao, cos, sin, cos, sin, sinks)


def _local_step(x, target, p, reduce_large):
    T, D = x.shape
    cos, sin = _rope_tables(T)
    qw = N_Q_HEADS * HEAD_DIM
    nm, nf = p["norm_mix"], p["norm_ffn"]

    y0, qkv = _qkv_proj(x, nm[0:1], p["attn_w_qkv"], p["attn_b_qkv"], cos, sin)
    ao = _attn_fwd(qkv, p["attn_sinks"])
    h1 = _mm_res("attn_out", ao, p["attn_w_o"], p["attn_b_o"], x)
    w1, w3, w2 = p["ffn_w1"], p["ffn_w3"], p["ffn_w2"]
    f0, g1a, g3a, s0 = _ffn_up("ffn0_up", h1, nf[0:1], w1, w3, 0)
    h2 = _ffn_down("ffn0_down", s0, w2, 0, h1)
    y1, a = _pw1_proj(h2, nm[1:2], p["conv_w_pw1"], p["conv_b_pw1"])
    c, act = _conv_fwd(a, p["conv_w_dw"], p["conv_b_dw"], p["conv_ln_g"], p["conv_ln_b"])
    h3 = _mm_res("conv_out", act, p["conv_w_pw2"], p["conv_b_pw2"], h2)
    f1, g1b, g3b, s1 = _ffn_up("ffn1_up", h3, nf[1:2], w1, w3, 1)
    h4 = _ffn_down("ffn1_down", s1, w2, 1, h3)
    dh4, loss, d_norm_final = _final_loss(h4, p["norm_final"], target)

    g = {}
    dg1, dg3 = _ffn_bwd_act("ffn1_bwd_act", dh4, w2, 1, g1b, g3b)
    dw2_1 = _mm_tn("ffn1_dw2", s1, dh4)
    dw1_1 = _mm_tn("ffn1_dw1", f1, dg1)
    dw3_1 = _mm_tn("ffn1_dw3", f1, dg3)
    started = reduce_large("ffn1", {("ffn_w1", 1): dw1_1, ("ffn_w3", 1): dw3_1, ("ffn_w2", 1): dw2_1})
    dh3, dnf1, db_pw2 = _mm_nt_normbwd("ffn1_bwd_in", [(dg1, w1, 1), (dg3, w3, 1)], h3, nf[1:2], dh4, started)

    dw_pw2 = _mm_tn("conv_dw_pw2", act, dh3)
    dact = _mm_nt("conv_bwd_out", dh3, p["conv_w_pw2"], F32)
    da, d_ln_g, d_ln_b, d_b_dw, d_w_dw, d_b_pw1 = _conv_bwd(dact, c, a, p["conv_w_dw"], p["conv_ln_g"], p["conv_ln_b"])
    dw_pw1 = _mm_tn("conv_dw_pw1", y1, da, col_chunks=N_CHIPS)
    started = reduce_large("conv", {("conv_w_pw2", 0): dw_pw2.reshape(N_CHIPS, -1, D), ("conv_w_pw1", 0): dw_pw1})
    dh2, dnm1, _ = _mm_nt_normbwd("conv_bwd_in", [(da, p["conv_w_pw1"], None)], h2, nm[1:2], dh3, started)

    dg1, dg3 = _ffn_bwd_act("ffn0_bwd_act", dh2, w2, 0, g1a, g3a)
    dw2_0 = _mm_tn("ffn0_dw2", s0, dh2)
    dw1_0 = _mm_tn("ffn0_dw1", f0, dg1)
    dw3_0 = _mm_tn("ffn0_dw3", f0, dg3)
    started = reduce_large("ffn0", {("ffn_w1", 0): dw1_0, ("ffn_w3", 0): dw3_0, ("ffn_w2", 0): dw2_0})
    dh1, dnf0, db_o = _mm_nt_normbwd("ffn0_bwd_in", [(dg1, w1, 0), (dg3, w3, 0)], h1, nf[0:1], dh2, started)

    dw_o = _mm_tn("attn_dw_o", ao, dh1)
    dao = _mm_nt("attn_bwd_out", dh1, p["attn_w_o"], BF16)
    dq, dkv, d_sinks, dbq, dbkv = _attn_bwd(qkv, dao, cos, sin, p["attn_sinks"])
    dwq = _mm_tn("attn_dw_q", y0, dq)
    dwkv = _mm_tn("attn_dw_kv", y0, dkv)
    wqkv = p["attn_w_qkv"]
    n_qkv = wqkv.shape[1] // N_CHIPS
    dwqkv = jnp.moveaxis(jnp.concatenate([dwq, dwkv], axis=1).reshape(D, N_CHIPS, n_qkv), 1, 0)
    started = reduce_large("attn", {("attn_w_o", 0): dw_o.reshape(N_CHIPS, -1, D), ("attn_w_qkv", 0): dwqkv})
    dx, dnm0, _ = _mm_nt_normbwd("attn_bwd_in", [(dq, wqkv[:, :qw], None), (dkv, wqkv[:, qw:], None)], x, nm[0:1], dh1,
                                 started)

    g["norm_mix"] = jnp.concatenate([dnm0, dnm1], axis=0)
    g["norm_ffn"] = jnp.concatenate([dnf0, dnf1], axis=0)
    g["attn_b_qkv"] = jnp.concatenate([dbq, dbkv], axis=1)
    g["attn_sinks"] = d_sinks
    g["attn_b_o"] = db_o
    g["conv_b_pw1"] = d_b_pw1
    g["conv_w_dw"] = d_w_dw[:CONV_WIDTH]
    g["conv_b_dw"] = d_b_dw
    g["conv_ln_g"] = d_ln_g
    g["conv_ln_b"] = d_ln_b
    g["conv_b_pw2"] = db_pw2
    g["norm_final"] = d_norm_final
    return loss, dx, g


ANY = pl.BlockSpec(memory_space=pl.ANY)
VMEM_WHOLE = pl.BlockSpec(memory_space=pltpu.VMEM)


def _my_place():
    return lax.axis_index("x"), lax.axis_index("y"), lax.axis_index("c")


def _other_chips(x, y):
    places = [(1 - x, y), (x, 1 - y), (1 - x, 1 - y)]
    return [(bx, by, 2 * bx + by) for bx, by in places]


def _exchange_small(name, v, reduce):
    r, w = v.shape

    def body(v_ref, o_ref, *rest):
        if reduce:
            buf, send_sems, recv_sems = rest
        else:
            buf = o_ref
            send_sems, recv_sems = rest
        x, y, c = _my_place()
        me = 4 * x + 2 * y + c
        sends = []
        for k in range(1, N_DEV):
            peer = (1 - x if k & 4 else x, 1 - y if k & 2 else y, 1 - c if k & 1 else c)
            cp = pltpu.make_async_remote_copy(
                src_ref=v_ref, dst_ref=buf.at[me], send_sem=send_sems.at[k - 1], recv_sem=recv_sems.at[k - 1],
                device_id=peer, device_id_type=MESH)
            cp.start()
            sends.append(cp)
        buf[me] = v_ref[...]
        for k in range(1, N_DEV):
            src = 4 * (1 - x if k & 4 else x) + 2 * (1 - y if k & 2 else y) + (1 - c if k & 1 else c)
            pltpu.make_async_remote_copy(
                src_ref=v_ref, dst_ref=buf.at[src], send_sem=send_sems.at[k - 1], recv_sem=recv_sems.at[k - 1],
                device_id=(x, y, c), device_id_type=MESH).wait_recv()
        for cp in sends:
            cp.wait_send()
        if reduce:
            acc = buf[0]
            for d in range(1, N_DEV):
                acc = acc + buf[d]
            o_ref[...] = acc

    sems = [pltpu.SemaphoreType.DMA((N_DEV - 1,)), pltpu.SemaphoreType.DMA((N_DEV - 1,))]
    if reduce:
        out_shape = SDS((r, w), F32)
        scratch = [pltpu.VMEM((N_DEV, r, w), F32)] + sems
    else:
        out_shape = SDS((N_DEV, r, w), F32)
        scratch = sems
    return pl.pallas_call(
        body, name=name, out_shape=out_shape, in_specs=[VMEM_WHOLE], out_specs=VMEM_WHOLE,
        scratch_shapes=scratch,
        compiler_params=pltpu.CompilerParams(vmem_limit_bytes=V7X_VMEM_LIMIT_BYTES),
    )(v)


def _gather_weights(shards):
    n_w = len(shards)
    per_w = 6

    def body(*refs):
        w_refs, o_refs = refs[:n_w], refs[n_w:2 * n_w]
        send_sems, recv_sems, local_sems = refs[2 * n_w:]
        x, y, c = _my_place()
        k = 2 * x + y
        chips = _other_chips(x, y)

        def copy(sem, src, dst, to):
            return pltpu.make_async_remote_copy(
                src_ref=src, dst_ref=dst, send_sem=send_sems.at[sem], recv_sem=recv_sems.at[sem],
                device_id=to, device_id_type=MESH)

        def halves(i):
            half = w_refs[i].shape[0] // 2
            return (pl.ds(pl.multiple_of(c * half, 16), half), pl.ds(pl.multiple_of((1 - c) * half, 16), half))

        own, sends = [], []
        for i in range(n_w):
            mine, _ = halves(i)
            cp = pltpu.make_async_copy(w_refs[i], o_refs[i].at[k], local_sems.at[i])
            cp.start()
            own.append(cp)
            for j, (bx, by, _) in enumerate(chips):
                cp = copy(per_w * i + j, w_refs[i].at[mine], o_refs[i].at[k, mine], (bx, by, c))
                cp.start()
                sends.append(cp)
        for i in range(n_w):
            mine, _ = halves(i)
            for j, (_, _, kb) in enumerate(chips):
                landed = o_refs[i].at[kb, mine]
                copy(per_w * i + j, landed, landed, (x, y, c)).wait_recv()
                fwd = copy(per_w * i + 3 + j, landed, landed, (x, y, 1 - c))
                fwd.start()
                sends.append(fwd)
        for i in range(n_w):
            _, theirs = halves(i)
            for j, (_, _, kb) in enumerate(chips):
                landed = o_refs[i].at[kb, theirs]
                copy(per_w * i + 3 + j, landed, landed, (x, y, c)).wait_recv()
        for cp in sends:
            cp.wait_send()
        for cp in own:
            cp.wait()

    return pl.pallas_call(
        body, name="gather_weights",
        out_shape=[SDS((N_CHIPS,) + s.shape, s.dtype) for s in shards],
        in_specs=[ANY] * n_w, out_specs=[ANY] * n_w,
        scratch_shapes=[pltpu.SemaphoreType.DMA((per_w * n_w,)), pltpu.SemaphoreType.DMA((per_w * n_w,)),
                        pltpu.SemaphoreType.DMA((n_w,))],
    )(*shards)


def _swap_halves_with_sibling(name, grads):
    n_g = len(grads)

    def body(*refs):
        g_refs, o_refs = refs[:n_g], refs[n_g:2 * n_g]
        send_sems, recv_sems = refs[2 * n_g:]
        x, y, c = _my_place()
        copies = []
        for i in range(n_g):
            half = g_refs[i].shape[1] // 2
            theirs = pl.ds(pl.multiple_of((1 - c) * half, 8), half)
            cp = pltpu.make_async_remote_copy(
                src_ref=g_refs[i].at[:, theirs], dst_ref=o_refs[i], send_sem=send_sems.at[i],
                recv_sem=recv_sems.at[i], device_id=(x, y, 1 - c), device_id_type=MESH)
            cp.start()
            copies.append(cp)
        for cp in copies:
            cp.wait()

    return pl.pallas_call(
        body, name=name,
        out_shape=[SDS((g.shape[0], g.shape[1] // 2, g.shape[2]), g.dtype) for g in grads],
        in_specs=[ANY] * n_g, out_specs=[ANY] * n_g,
        scratch_shapes=[pltpu.SemaphoreType.DMA((n_g,)), pltpu.SemaphoreType.DMA((n_g,))],
    )(*grads)


def _pack_row_tile(rows):
    for t in range(min(rows, 512), 7, -1):
        if rows % t == 0 and t % 8 == 0:
            return t
    return rows


def _add_sibling_half(name, grads, from_sibling, c_idx):
    n, R, w = grads.shape
    half = R // 2
    tr = _pack_row_tile(half)
    steps = half // tr

    def body(c_ref, g_ref, s_ref, o_ref):
        o_ref[...] = g_ref[...] + s_ref[...]

    return pl.pallas_call(
        body, name=name,
        grid_spec=pltpu.PrefetchScalarGridSpec(
            num_scalar_prefetch=1, grid=(n, steps),
            in_specs=[pl.BlockSpec((1, tr, w), lambda j, i, c_ref: (j, c_ref[0] * steps + i, 0)),
                      pl.BlockSpec((1, tr, w), lambda j, i, c_ref: (j, i, 0))],
            out_specs=pl.BlockSpec((1, tr, w), lambda j, i, c_ref: (j, i, 0))),
        out_shape=SDS((n, half, w), F32),
        compiler_params=_params("parallel", "parallel"),
    )(c_idx, grads, from_sibling)


HBM_SPEC = pl.BlockSpec(memory_space=pltpu.HBM)
SEM_SPEC = pl.BlockSpec(memory_space=pltpu.SEMAPHORE)
DATAFLOW = pltpu.SideEffectType.DATAFLOW_SIDE_EFFECTING


def _chip_scatter_copies(p_refs, land_refs, send_sems, recv_sems):
    x, y, c = _my_place()
    return [pltpu.make_async_remote_copy(
        src_ref=p_refs[i].at[kb], dst_ref=land_refs[i].at[j], send_sem=send_sems.at[3 * i + j],
        recv_sem=recv_sems.at[3 * i + j], device_id=(bx, by, c), device_id_type=MESH)
        for i in range(len(p_refs)) for j, (bx, by, kb) in enumerate(_other_chips(x, y))]


def _scatter_start(name, partials):
    n_p = len(partials)

    def body(*refs):
        p_refs, land_refs = refs[:n_p], refs[n_p:2 * n_p]
        send_sems, recv_sems = refs[2 * n_p:2 * n_p + 2]
        for cp in _chip_scatter_copies(p_refs, land_refs, send_sems, recv_sems):
            cp.start()
        refs[-1][...] = jnp.zeros_like(refs[-1])

    lands = [pltpu.with_memory_space_constraint(lax.empty((N_CHIPS - 1,) + p.shape[1:], p.dtype), pltpu.HBM)
             for p in partials]
    out = pl.pallas_call(
        body, name=name,
        out_shape=(pltpu.SemaphoreType.DMA((3 * n_p,)), pltpu.SemaphoreType.DMA((3 * n_p,)),
                   *[pltpu.HBM(p.shape, p.dtype) for p in partials], *[pltpu.HBM(l.shape, l.dtype) for l in lands],
                   SDS((8, V7X_LANES), F32)),
        in_specs=[HBM_SPEC] * (2 * n_p), out_specs=(SEM_SPEC, SEM_SPEC, *[HBM_SPEC] * (2 * n_p), VMEM_WHOLE),
        input_output_aliases={i: 2 + i for i in range(2 * n_p)},
        compiler_params=pltpu.CompilerParams(has_side_effects=DATAFLOW),
    )(*[pltpu.with_memory_space_constraint(p, pltpu.HBM) for p in partials], *lands)
    return out[0], out[1], list(out[2:2 + n_p]), list(out[2 + n_p:2 + 2 * n_p]), out[-1]


def _scatter_wait(name, send_sems, recv_sems, partials, lands, after):
    n_p = len(partials)

    def body(*refs):
        p_refs, land_refs = refs[:n_p], refs[n_p:2 * n_p]
        send_sems, recv_sems = refs[2 * n_p:2 * n_p + 2]
        for cp in _chip_scatter_copies(p_refs, land_refs, send_sems, recv_sems):
            cp.wait_send()
            cp.wait_recv()

    out = pl.pallas_call(
        body, name=name,
        out_shape=(*[pltpu.HBM(p.shape, p.dtype) for p in partials], *[pltpu.HBM(l.shape, l.dtype) for l in lands]),
        in_specs=[*[HBM_SPEC] * (2 * n_p), SEM_SPEC, SEM_SPEC, ANY], out_specs=tuple([HBM_SPEC] * (2 * n_p)),
        input_output_aliases={i: i for i in range(2 * n_p)},
        compiler_params=pltpu.CompilerParams(has_side_effects=DATAFLOW),
    )(*partials, *lands, send_sems, recv_sems, after)
    return list(out[:n_p]), list(out[n_p:])


def _sum_chip_partials(name, partial, received, k_idx):
    n, half, w = partial.shape
    tr = _pack_row_tile(half)

    def body(k_ref, p_ref, r_ref, o_ref):
        o_ref[...] = ((p_ref[0] + r_ref[0]) + r_ref[1]) + r_ref[2]

    return pl.pallas_call(
        body, name=name,
        grid_spec=pltpu.PrefetchScalarGridSpec(
            num_scalar_prefetch=1, grid=(half // tr,),
            in_specs=[pl.BlockSpec((1, tr, w), lambda i, k_ref: (k_ref[0], i, 0)),
                      pl.BlockSpec((n - 1, tr, w), lambda i, k_ref: (0, i, 0))],
            out_specs=pl.BlockSpec((tr, w), lambda i, k_ref: (i, 0))),
        out_shape=SDS((half, w), F32),
        compiler_params=_params("parallel"),
    )(k_idx, partial, received)


def _join_halves(halves, groups):
    n_h = len(halves)
    where = {}
    for wi, items in enumerate(groups):
        for layer, item in enumerate(items):
            where[item] = (wi, layer)

    def body(*refs):
        h_refs, o_refs = refs[:n_h], refs[n_h:n_h + len(groups)]
        send_sems, recv_sems, local_sems = refs[n_h + len(groups):]
        x, y, c = _my_place()
        copies = []
        for i in range(n_h):
            wi, layer = where[i]
            half = h_refs[i].shape[0]
            rows = pl.ds(pl.multiple_of(layer * 2 * half + c * half, 8), half)
            own = pltpu.make_async_copy(h_refs[i], o_refs[wi].at[rows], local_sems.at[i])
            own.start()
            cp = pltpu.make_async_remote_copy(
                src_ref=h_refs[i], dst_ref=o_refs[wi].at[rows], send_sem=send_sems.at[i], recv_sem=recv_sems.at[i],
                device_id=(x, y, 1 - c), device_id_type=MESH)
            cp.start()
            copies += [own, cp]
        for cp in copies:
            cp.wait()

    out_shape = [SDS((2 * halves[items[0]].shape[0] * len(items), halves[items[0]].shape[1]), F32) for items in groups]
    return pl.pallas_call(
        body, name="join_halves", out_shape=out_shape,
        in_specs=[ANY] * n_h, out_specs=[ANY] * len(groups),
        scratch_shapes=[pltpu.SemaphoreType.DMA((n_h,)), pltpu.SemaphoreType.DMA((n_h,)),
                        pltpu.SemaphoreType.DMA((n_h,))],
    )(*halves)


def _adamw(name, w, g, m, v):
    rows, width = w.shape
    tr = _pack_row_tile(rows)

    def body(w_ref, g_ref, m_ref, v_ref, d_ref, nm_ref, nv_ref):
        gg = g_ref[...]
        m_new = ADAM_B1 * m_ref[...] + (1.0 - ADAM_B1) * gg
        v_new = ADAM_B2 * v_ref[...] + (1.0 - ADAM_B2) * (gg * gg)
        m_hat = m_new / (1.0 - ADAM_B1 ** ADAM_STEP)
        v_hat = v_new / (1.0 - ADAM_B2 ** ADAM_STEP)
        d_ref[...] = -ADAM_LR * (m_hat / (jnp.sqrt(v_hat) + ADAM_EPS) + ADAM_WD * w_ref[...])
        nm_ref[...] = m_new
        nv_ref[...] = v_new

    spec = _rows(tr, width)
    return pl.pallas_call(
        body, name=name, grid=(rows // tr,),
        in_specs=[spec] * 4, out_specs=[spec] * 3,
        out_shape=[SDS((rows, width), F32)] * 3,
        compiler_params=_params("parallel"),
    )(w, g, m, v)


WEIGHT_NAMES = ['norm_mix', 'norm_ffn', 'attn_w_qkv', 'attn_b_qkv', 'attn_sinks', 'attn_w_o', 'attn_b_o',
                'conv_w_pw1', 'conv_b_pw1', 'conv_w_dw', 'conv_b_dw', 'conv_ln_g', 'conv_ln_b', 'conv_w_pw2',
                'conv_b_pw2', 'ffn_w1', 'ffn_w3', 'ffn_w2', 'norm_final']
BIG = ['attn_w_qkv', 'attn_w_o', 'conv_w_pw1', 'conv_w_pw2', 'ffn_w1', 'ffn_w3', 'ffn_w2']
COLUMN_SPLIT = ('attn_w_qkv', 'conv_w_pw1', 'ffn_w1', 'ffn_w3')
SMALL_SPLIT = ['conv_b_pw1', 'conv_w_dw', 'conv_b_dw', 'conv_ln_g', 'conv_ln_b', 'conv_b_pw2']
SMALL_WHOLE = ['norm_mix', 'norm_ffn', 'attn_b_qkv', 'attn_sinks', 'attn_b_o', 'norm_final']


def _pack_rows(arrays, dtype, row_multiple):
    flat = jnp.concatenate([a.astype(dtype).reshape(-1) for a in arrays])
    rows = -(-flat.shape[0] // PACK_W)
    rows = -(-rows // row_multiple) * row_multiple
    return jnp.pad(flat, (0, rows * PACK_W - flat.shape[0])).reshape(rows, PACK_W)


def _unpack_rows(pack, shapes):
    flat = pack.reshape(-1)
    out, at = [], 0
    for shape in shapes:
        size = 1
        for s in shape:
            size *= s
        out.append(flat[at:at + size].reshape(shape))
        at += size
    return out


def _join_chip_axis(name, parts):
    axis = parts.ndim - 1 if name in COLUMN_SPLIT or name in SMALL_SPLIT else parts.ndim - 2
    moved = jnp.moveaxis(parts, 0, axis - 1)
    shape = list(moved.shape)
    shape[axis - 1:axis + 1] = [shape[axis - 1] * shape[axis]]
    return moved.reshape(shape)


def _split_chip_axis(name, whole, shard_shape):
    axis = len(shard_shape) - 1 if name in COLUMN_SPLIT or name in SMALL_SPLIT else len(shard_shape) - 2
    shape = list(whole.shape)
    shape[axis:axis + 1] = [N_CHIPS, shard_shape[axis]]
    return jnp.moveaxis(whole.reshape(shape), axis, 0)


def kernel(x, norm_mix, norm_ffn, attn_w_qkv, attn_b_qkv, attn_sinks, attn_w_o, attn_b_o, conv_w_pw1, conv_b_pw1, conv_w_dw, conv_b_dw, conv_ln_g, conv_ln_b, conv_w_pw2, conv_b_pw2, ffn_w1, ffn_w3, ffn_w2, norm_final, loss_target, m_norm_mix, m_norm_ffn, m_attn_w_qkv, m_attn_b_qkv, m_attn_sinks, m_attn_w_o, m_attn_b_o, m_conv_w_pw1, m_conv_b_pw1, m_conv_w_dw, m_conv_b_dw, m_conv_ln_g, m_conv_ln_b, m_conv_w_pw2, m_conv_b_pw2, m_ffn_w1, m_ffn_w3, m_ffn_w2, m_norm_final, v_norm_mix, v_norm_ffn, v_attn_w_qkv, v_attn_b_qkv, v_attn_sinks, v_attn_w_o, v_attn_b_o, v_conv_w_pw1, v_conv_b_pw1, v_conv_w_dw, v_conv_b_dw, v_conv_ln_g, v_conv_ln_b, v_conv_w_pw2, v_conv_b_pw2, v_ffn_w1, v_ffn_w3, v_ffn_w2, v_norm_final):
    w = dict(zip(WEIGHT_NAMES, (norm_mix, norm_ffn, attn_w_qkv, attn_b_qkv, attn_sinks, attn_w_o, attn_b_o,
                                conv_w_pw1, conv_b_pw1, conv_w_dw, conv_b_dw, conv_ln_g, conv_ln_b, conv_w_pw2,
                                conv_b_pw2, ffn_w1, ffn_w3, ffn_w2, norm_final)))
    m = dict(zip(WEIGHT_NAMES, (m_norm_mix, m_norm_ffn, m_attn_w_qkv, m_attn_b_qkv, m_attn_sinks, m_attn_w_o,
                                m_attn_b_o, m_conv_w_pw1, m_conv_b_pw1, m_conv_w_dw, m_conv_b_dw, m_conv_ln_g,
                                m_conv_ln_b, m_conv_w_pw2, m_conv_b_pw2, m_ffn_w1, m_ffn_w3, m_ffn_w2, m_norm_final)))
    v = dict(zip(WEIGHT_NAMES, (v_norm_mix, v_norm_ffn, v_attn_w_qkv, v_attn_b_qkv, v_attn_sinks, v_attn_w_o,
                                v_attn_b_o, v_conv_w_pw1, v_conv_b_pw1, v_conv_w_dw, v_conv_b_dw, v_conv_ln_g,
                                v_conv_ln_b, v_conv_w_pw2, v_conv_b_pw2, v_ffn_w1, v_ffn_w3, v_ffn_w2, v_norm_final)))
    T, D = x.shape[1], x.shape[2]
    c_idx = lax.axis_index("c").astype(jnp.int32).reshape(1)
    chip = (2 * lax.axis_index("x") + lax.axis_index("y")).astype(jnp.int32)

    as_rows = lambda a: a.reshape(-1, a.shape[-1])
    gathered = dict(zip(BIG, _gather_weights([as_rows(w[n]).astype(BF16) for n in BIG])))
    layers = ffn_w1.shape[0]
    small_shapes = [w[n].shape for n in SMALL_SPLIT]
    small_all = _exchange_small("gather_small", _pack_rows([w[n] for n in SMALL_SPLIT], F32, 8), reduce=False)
    per_chip = [_unpack_rows(small_all[2 * j], small_shapes) for j in range(N_CHIPS)]
    full = {}
    for i, n in enumerate(SMALL_SPLIT):
        full[n] = _join_chip_axis(n, jnp.stack([per_chip[j][i] for j in range(N_CHIPS)]))

    qkv_parts = gathered["attn_w_qkv"]
    p = {
        "norm_mix": norm_mix, "norm_ffn": norm_ffn, "norm_final": norm_final.reshape(1, D),
        "attn_w_qkv": jnp.moveaxis(qkv_parts, 0, 1).reshape(D, -1), "attn_b_qkv": attn_b_qkv,
        "attn_sinks": attn_sinks, "attn_w_o": gathered["attn_w_o"].reshape(-1, D), "attn_b_o": attn_b_o,
        "conv_w_pw1": gathered["conv_w_pw1"], "conv_b_pw1": full["conv_b_pw1"], "conv_w_dw": full["conv_w_dw"][0],
        "conv_b_dw": full["conv_b_dw"], "conv_ln_g": full["conv_ln_g"], "conv_ln_b": full["conv_ln_b"],
        "conv_w_pw2": gathered["conv_w_pw2"].reshape(-1, D), "conv_b_pw2": full["conv_b_pw2"],
        "ffn_w1": gathered["ffn_w1"].reshape(N_CHIPS, layers, D, -1),
        "ffn_w3": gathered["ffn_w3"].reshape(N_CHIPS, layers, D, -1),
        "ffn_w2": gathered["ffn_w2"].reshape(N_CHIPS, layers, -1, D),
    }
    in_flight = []

    def reduce_large(tag, grads):
        keys = list(grads)
        from_sibling = _swap_halves_with_sibling(f"grads_to_sibling_{tag}", [grads[k] for k in keys])
        partials = [_add_sibling_half(f"add_sibling_half_{tag}{i}", grads[k], fs, c_idx)
                    for i, (k, fs) in enumerate(zip(keys, from_sibling))]
        *handles, started = _scatter_start(f"scatter_start_{tag}", partials)
        in_flight.append((tag, keys, handles))
        return started

    loss_part, dx, g = _local_step(x[0], loss_target[0], p, reduce_large)
    for n in SMALL_WHOLE + SMALL_SPLIT:
        g[n] = g[n].reshape((-1,) + g[n].shape[-2:]) if w[n].ndim == 3 else g[n].reshape(w[n].shape[:-1] + (-1,))

    reduced_half = {}
    for tag, keys, (send_sems, recv_sems, partials, lands) in in_flight:
        partials, received = _scatter_wait(f"scatter_wait_{tag}", send_sems, recv_sems, partials, lands, dx)
        for i, k in enumerate(keys):
            reduced_half[k] = _sum_chip_partials(f"sum_chip_partials_{tag}{i}", partials[i], received[i],
                                                 chip.reshape(1))
    order = [(n, layer) for n in BIG for layer in range(w[n].shape[0])]
    groups = [[order.index((n, layer)) for layer in range(w[n].shape[0])] for n in BIG]
    g_big = dict(zip(BIG, _join_halves([reduced_half[k] for k in order], groups)))
    big_out = {}
    for n in BIG:
        d_n, m_n, v_n = _adamw(f"adamw_{n}", as_rows(w[n]), g_big[n], as_rows(m[n]), as_rows(v[n]))
        big_out[n] = [a.reshape(w[n].shape) for a in (g_big[n], d_n, m_n, v_n)]

    small_whole_shapes = [w[n].shape for n in SMALL_WHOLE]
    small_full_shapes = [g[n].shape for n in SMALL_SPLIT]
    reduced = _exchange_small(
        "reduce_small", _pack_rows([loss_part] + [g[n] for n in SMALL_WHOLE] + [g[n] for n in SMALL_SPLIT], F32, 8),
        reduce=True)
    pieces = _unpack_rows(reduced, [(1,)] + small_whole_shapes + small_full_shapes)
    loss = pieces[0].reshape(())
    g_small = dict(zip(SMALL_WHOLE, pieces[1:1 + len(SMALL_WHOLE)]))
    for n, whole in zip(SMALL_SPLIT, pieces[1 + len(SMALL_WHOLE):]):
        parts = _split_chip_axis(n, whole, w[n].shape)
        g_small[n] = lax.dynamic_index_in_dim(parts, chip, axis=0, keepdims=False)
    small = SMALL_WHOLE + SMALL_SPLIT
    d_small, m_small, v_small = _adamw(
        "adamw_small", _pack_rows([w[n] for n in small], F32, 8), _pack_rows([g_small[n] for n in small], F32, 8),
        _pack_rows([m[n] for n in small], F32, 8), _pack_rows([v[n] for n in small], F32, 8))

    outs = {}
    for slot, (tag, small_pack) in enumerate((("g", None), ("d", d_small), ("m", m_small), ("v", v_small))):
        vals = {n: big_out[n][slot] for n in BIG}
        if small_pack is None:
            vals.update(g_small)
        else:
            vals.update(zip(small, _unpack_rows(small_pack, [w[n].shape for n in small])))
        outs[tag] = vals
    return (loss, dx.reshape(1, T, D), *[outs["g"][n] for n in WEIGHT_NAMES], *[outs["d"][n] for n in WEIGHT_NAMES],
            *[outs["m"][n] for n in WEIGHT_NAMES], *[outs["v"][n] for n in WEIGHT_NAMES])
```

```python
import functools

import jax
import jax.numpy as jnp
from jax import lax
from jax.experimental import pallas as pl
from jax.experimental.pallas import tpu as pltpu

F32 = jnp.float32
BF16 = jnp.bfloat16
SDS = jax.ShapeDtypeStruct
MESH = pl.DeviceIdType.MESH

HEAD_DIM = 64
N_Q_HEADS = 16
N_KV_HEADS = 2
Q_PER_KV = N_Q_HEADS // N_KV_HEADS
ATTN_BLOCK = 128
ROPE_THETA = 10000.0
CONV_WIDTH = 31
CONV_HALO = 32
RMS_EPS = 1e-5
LN_EPS = 1e-5
ADAM_LR = 0.001
ADAM_B1 = 0.9
ADAM_B2 = 0.999
ADAM_EPS = 1e-08
ADAM_WD = 0.01
ADAM_STEP = 10

V7X_LANES = 128
V7X_VMEM_LIMIT_BYTES = 56 * 1024 * 1024

N_CHIPS = 4
N_DEV = 8
PACK_W = 1024

MASK_VALUE = -1e30


def _params(*semantics):
    return pltpu.CompilerParams(dimension_semantics=semantics, vmem_limit_bytes=V7X_VMEM_LIMIT_BYTES)


def _rows(tm, width):
    return pl.BlockSpec((tm, width), lambda i: (i, 0))


def _whole(shape):
    return pl.BlockSpec(shape, lambda *_: (0,) * len(shape))


def _rms_rstd(h):
    return lax.rsqrt(jnp.mean(h * h, axis=-1, keepdims=True) + RMS_EPS)


def _silu_and_grad(z):
    sg = jax.nn.sigmoid(z)
    return z * sg, sg * (1.0 + z * (1.0 - sg))


def _swap_rope_halves(t):
    w = t.shape[1]
    half = HEAD_DIM // 2
    lane = lax.broadcasted_iota(jnp.int32, t.shape, 1)
    upper = pltpu.roll(t, w - half, 1)
    lower = pltpu.roll(t, half, 1)
    return jnp.where(lane % HEAD_DIM < half, upper, lower)


def _rope(t, cos_ref, sin_ref):
    reps = t.shape[1] // V7X_LANES
    c = jnp.tile(cos_ref[...], (1, reps))
    s = jnp.tile(sin_ref[...], (1, reps))
    return t * c + _swap_rope_halves(t) * s


def _rope_transposed(dt, cos_ref, sin_ref):
    reps = dt.shape[1] // V7X_LANES
    c = jnp.tile(cos_ref[...], (1, reps))
    s = jnp.tile(sin_ref[...], (1, reps))
    return dt * c + _swap_rope_halves(dt * s)


def _rope_tables(seq_len):
    pos = jnp.arange(seq_len, dtype=F32)
    inv_freq = ROPE_THETA ** (-jnp.arange(0, HEAD_DIM, 2, dtype=F32) / HEAD_DIM)
    ang = pos[:, None] * inv_freq[None, :]
    cos, sin = jnp.cos(ang), jnp.sin(ang)
    cos_t = jnp.concatenate([cos, cos, cos, cos], axis=1)
    sin_t = jnp.concatenate([-sin, sin, -sin, sin], axis=1)
    return cos_t, sin_t


def _qkv_proj(h, g, w, b, cos, sin, after):
    T, D = h.shape
    N = w.shape[1]
    tm = min(512, T)
    rope_w = N - N_KV_HEADS * HEAD_DIM

    def body(h_ref, g_ref, w_ref, b_ref, cos_ref, sin_ref, _, y_ref, o_ref):
        hh = h_ref[...]
        y = (hh * _rms_rstd(hh) * g_ref[...]).astype(BF16)
        y_ref[...] = y
        acc = jnp.dot(y, w_ref[...], preferred_element_type=F32) + b_ref[...]
        o_ref[:, :rope_w] = _rope(acc[:, :rope_w], cos_ref, sin_ref).astype(BF16)
        o_ref[:, rope_w:] = acc[:, rope_w:].astype(BF16)

    return pl.pallas_call(
        body, name="qkv_proj", grid=(T // tm,),
        in_specs=[_rows(tm, D), _whole((1, D)), _whole((D, N)), _whole((1, N)),
                  _rows(tm, V7X_LANES), _rows(tm, V7X_LANES), pl.BlockSpec(memory_space=pl.ANY)],
        out_specs=[_rows(tm, D), _rows(tm, N)],
        out_shape=[SDS((T, D), BF16), SDS((T, N), BF16)],
        compiler_params=_params("parallel"),
    )(h, g, w, b, cos, sin, after)


def _pw1_proj(h, g, w, b):
    T, D = h.shape
    n = w.shape[2]
    N = N_CHIPS * n
    tm = min(512, T)

    def body(h_ref, g_ref, w_ref, b_ref, y_ref, o_ref):
        hh = h_ref[...]
        y = (hh * _rms_rstd(hh) * g_ref[...]).astype(BF16)
        y_ref[...] = y
        for j in range(N_CHIPS):
            cols = slice(j * n, (j + 1) * n)
            o_ref[:, cols] = jnp.dot(y, w_ref[j], preferred_element_type=F32) + b_ref[:, cols]

    return pl.pallas_call(
        body, name="pw1_proj", grid=(T // tm,),
        in_specs=[_rows(tm, D), _whole((1, D)), _whole((N_CHIPS, D, n)), _whole((1, N))],
        out_specs=[_rows(tm, D), _rows(tm, N)],
        out_shape=[SDS((T, D), BF16), SDS((T, N), F32)],
        compiler_params=_params("parallel"),
    )(h, g, w, b)


def _band_mask(n):
    row = lax.broadcasted_iota(jnp.int32, (ATTN_BLOCK, 2 * ATTN_BLOCK), 0)
    col = lax.broadcasted_iota(jnp.int32, (ATTN_BLOCK, 2 * ATTN_BLOCK), 1)
    band = (col > row) & (col <= row + ATTN_BLOCK) & ((col >= ATTN_BLOCK) | (n > 0))
    return jnp.concatenate([band] * Q_PER_KV, axis=0)


def _group_operands(g, q_ref, kc_ref, kp_ref, vc_ref, vp_ref, sink_ref):
    lo = g * HEAD_DIM
    k = jnp.concatenate([kp_ref[:, lo:lo + HEAD_DIM], kc_ref[:, lo:lo + HEAD_DIM]], axis=0)
    v = jnp.concatenate([vp_ref[:, lo:lo + HEAD_DIM], vc_ref[:, lo:lo + HEAD_DIM]], axis=0)
    heads = range(g * Q_PER_KV, (g + 1) * Q_PER_KV)
    q = jnp.concatenate([q_ref[:, h * HEAD_DIM:(h + 1) * HEAD_DIM] for h in heads], axis=0)
    sink = jnp.concatenate(
        [jnp.broadcast_to(sink_ref[0:1, h:h + 1], (ATTN_BLOCK, 1)) for h in heads], axis=0)
    return q, k, v, sink


def _softmax_with_sink(q, k, sink, mask):
    s = lax.dot_general(q, k, (((1,), (1,)), ((), ())), preferred_element_type=F32) * (HEAD_DIM ** -0.5)
    s = jnp.where(mask, s, MASK_VALUE)
    m = jnp.maximum(jnp.max(s, axis=1, keepdims=True), sink)
    p = jnp.exp(s - m)
    e_sink = jnp.exp(sink - m)
    inv = 1.0 / (jnp.sum(p, axis=1, keepdims=True) + e_sink)
    return p * inv, e_sink * inv


def _attn_specs(T):
    nb = T // ATTN_BLOCK
    kcol = N_Q_HEADS * HEAD_DIM // V7X_LANES
    cur = lambda n: jnp.minimum(n, nb - 1)
    prev = lambda n: jnp.maximum(jnp.minimum(n, nb - 1) - 1, 0)
    q_spec = pl.BlockSpec((ATTN_BLOCK, N_Q_HEADS * HEAD_DIM), lambda n: (cur(n), 0))
    kc_spec = pl.BlockSpec((ATTN_BLOCK, V7X_LANES), lambda n: (cur(n), kcol))
    kp_spec = pl.BlockSpec((ATTN_BLOCK, V7X_LANES), lambda n: (prev(n), kcol))
    vc_spec = pl.BlockSpec((ATTN_BLOCK, V7X_LANES), lambda n: (cur(n), kcol + 1))
    vp_spec = pl.BlockSpec((ATTN_BLOCK, V7X_LANES), lambda n: (prev(n), kcol + 1))
    return q_spec, kc_spec, kp_spec, vc_spec, vp_spec


def _attn_fwd(qkv, sinks):
    T = qkv.shape[0]
    nb = T // ATTN_BLOCK
    qw = N_Q_HEADS * HEAD_DIM

    def body(q_ref, kc_ref, kp_ref, vc_ref, vp_ref, sink_ref, o_ref):
        mask = _band_mask(pl.program_id(0))
        for g in range(N_KV_HEADS):
            q, k, v, sink = _group_operands(g, q_ref, kc_ref, kp_ref, vc_ref, vp_ref, sink_ref)
            probs, _ = _softmax_with_sink(q, k, sink, mask)
            o = jnp.dot(probs.astype(BF16), v, preferred_element_type=F32)
            for i in range(Q_PER_KV):
                h = g * Q_PER_KV + i
                o_ref[:, h * HEAD_DIM:(h + 1) * HEAD_DIM] = o[i * ATTN_BLOCK:(i + 1) * ATTN_BLOCK].astype(BF16)

    return pl.pallas_call(
        body, name="attn_fwd", grid=(nb,),
        in_specs=[*_attn_specs(T), _whole((1, N_Q_HEADS))],
        out_specs=_rows(ATTN_BLOCK, qw),
        out_shape=SDS((T, qw), BF16),
        compiler_params=_params("parallel"),
    )(qkv, qkv, qkv, qkv, qkv, sinks)


def _mm_res(name, a, w, b, res):
    T, K = a.shape
    D = w.shape[1]
    tm = min(512, T)

    def body(a_ref, w_ref, b_ref, r_ref, o_ref):
        o_ref[...] = jnp.dot(a_ref[...], w_ref[...], preferred_element_type=F32) + b_ref[...] + r_ref[...]

    return pl.pallas_call(
        body, name=name, grid=(T // tm,),
        in_specs=[_rows(tm, K), _whole((K, D)), _whole((1, D)), _rows(tm, D)],
        out_specs=_rows(tm, D),
        out_shape=SDS((T, D), F32),
        compiler_params=_params("parallel"),
    )(a, w, b, res)


def _ffn_down(name, s, w2, layer, res):
    _, T, n = s.shape
    D = w2.shape[3]
    tm = min(512, T)

    def body(s_ref, w_ref, r_ref, o_ref):
        acc = r_ref[...]
        for j in range(N_CHIPS):
            acc = acc + jnp.dot(s_ref[j], w_ref[j], preferred_element_type=F32)
        o_ref[...] = acc

    return pl.pallas_call(
        body, name=name, grid=(T // tm,),
        in_specs=[pl.BlockSpec((N_CHIPS, tm, n), lambda i: (0, i, 0)),
                  pl.BlockSpec((N_CHIPS, None, n, D), lambda i: (0, layer, 0, 0)), _rows(tm, D)],
        out_specs=_rows(tm, D),
        out_shape=SDS((T, D), F32),
        compiler_params=_params("parallel"),
    )(s, w2, res)


def _ffn_up(name, h, g, w1, w3, layer):
    T, D = h.shape
    n = w1.shape[3]
    tm = min(512, T)

    def body(h_ref, g_ref, w1_ref, w3_ref, f_ref, g1_ref, g3_ref, s_ref):
        @pl.when(pl.program_id(1) == 0)
        def _():
            hh = h_ref[...]
            f_ref[...] = (hh * _rms_rstd(hh) * g_ref[...]).astype(BF16)

        f = f_ref[...]
        a = jnp.dot(f, w1_ref[...], preferred_element_type=F32)
        b = jnp.dot(f, w3_ref[...], preferred_element_type=F32)
        g1_ref[...] = a.astype(BF16)
        g3_ref[...] = b.astype(BF16)
        s_ref[...] = (a * jax.nn.sigmoid(a) * b).astype(BF16)

    slab = pl.BlockSpec((None, tm, n), lambda i, j: (j, i, 0))
    wslab = pl.BlockSpec((None, None, D, n), lambda i, j: (j, layer, 0, 0))
    hidden = SDS((N_CHIPS, T, n), BF16)
    return pl.pallas_call(
        body, name=name, grid=(T // tm, N_CHIPS),
        in_specs=[pl.BlockSpec((tm, D), lambda i, j: (i, 0)), pl.BlockSpec((1, D), lambda i, j: (0, 0)), wslab, wslab],
        out_specs=[pl.BlockSpec((tm, D), lambda i, j: (i, 0)), slab, slab, slab],
        out_shape=[SDS((T, D), BF16), hidden, hidden, hidden],
        compiler_params=_params("parallel", "arbitrary"),
    )(h, g, w1, w3)


def _glu(a, d):
    return a[:, :d] * jax.nn.sigmoid(a[:, d:])


def _conv_tile(T):
    return min(256, T)


def _conv_fwd(a, w_dw, b_dw, ln_g, ln_b):
    T = a.shape[0]
    D = a.shape[1] // 2
    tc = _conv_tile(T)
    per = tc // CONV_HALO

    def body(a_ref, ah_ref, w_ref, bdw_ref, lg_ref, lb_ref, c_ref, act_ref, u_scr):
        i = pl.program_id(0)
        u_scr[0:CONV_HALO, :] = jnp.where(i > 0, _glu(ah_ref[...], D), 0.0)
        u_scr[CONV_HALO:, :] = _glu(a_ref[...], D)
        acc = jnp.zeros((tc, D), F32)
        for j in range(CONV_WIDTH):
            acc = acc + u_scr[pl.ds(CONV_HALO - CONV_WIDTH + 1 + j, tc), :] * w_ref[j:j + 1, :]
        c = acc + bdw_ref[...]
        c_ref[...] = c
        xc = c - jnp.mean(c, axis=-1, keepdims=True)
        z = xc * lax.rsqrt(jnp.mean(xc * xc, axis=-1, keepdims=True) + LN_EPS)
        l = z * lg_ref[...] + lb_ref[...]
        act_ref[...] = (l * jax.nn.sigmoid(l)).astype(BF16)

    return pl.pallas_call(
        body, name="conv_fwd", grid=(T // tc,),
        in_specs=[_rows(tc, 2 * D),
                  pl.BlockSpec((CONV_HALO, 2 * D), lambda i: (jnp.maximum(i * per - 1, 0), 0)),
                  _whole((CONV_WIDTH, D)), _whole((1, D)), _whole((1, D)), _whole((1, D))],
        out_specs=[_rows(tc, D), _rows(tc, D)],
        out_shape=[SDS((T, D), F32), SDS((T, D), BF16)],
        scratch_shapes=[pltpu.VMEM((tc + CONV_HALO, D), F32)],
        compiler_params=_params("parallel"),
    )(a, a, w_dw, b_dw, ln_g, ln_b)


def _final_loss(h, g, target):
    T, D = h.shape
    tm = min(512, T)

    def body(h_ref, g_ref, t_ref, dh_ref, loss_ref, dg_ref):
        @pl.when(pl.program_id(0) == 0)
        def _():
            loss_ref[...] = jnp.zeros_like(loss_ref)
            dg_ref[...] = jnp.zeros_like(dg_ref)

        hh = h_ref[...]
        r = _rms_rstd(hh)
        g = g_ref[...]
        d = hh * r * g - t_ref[...]
        loss_ref[...] += 0.5 * jnp.sum(jnp.mean(d * d, axis=-1, keepdims=True), axis=0, keepdims=True)
        dout = d * (1.0 / D)
        dg_ref[...] += jnp.sum(dout * (hh * r), axis=0, keepdims=True)
        dxh = dout * g
        dh_ref[...] = r * dxh - hh * (r * r * r) * jnp.mean(dxh * hh, axis=-1, keepdims=True)

    return pl.pallas_call(
        body, name="final_loss", grid=(T // tm,),
        in_specs=[_rows(tm, D), _whole((1, D)), _rows(tm, D)],
        out_specs=[_rows(tm, D), _whole((1, 1)), _whole((1, D))],
        out_shape=[SDS((T, D), F32), SDS((1, 1), F32), SDS((1, D), F32)],
        compiler_params=_params("arbitrary"),
    )(h, g, target)


def _ffn_bwd_act(name, dh, w2, layer, g1, g3):
    T, D = dh.shape
    n = w2.shape[2]
    tm = min(512, T)

    def body(dh_ref, w2_ref, g1_ref, g3_ref, dg1_ref, dg3_ref):
        ds = lax.dot_general(dh_ref[...].astype(BF16), w2_ref[...], (((1,), (1,)), ((), ())),
                             preferred_element_type=F32)
        act, dact = _silu_and_grad(g1_ref[...].astype(F32))
        dg1_ref[...] = (ds * g3_ref[...].astype(F32) * dact).astype(BF16)
        dg3_ref[...] = (ds * act).astype(BF16)

    slab = pl.BlockSpec((None, tm, n), lambda i, j: (j, i, 0))
    hidden = SDS((N_CHIPS, T, n), BF16)
    return pl.pallas_call(
        body, name=name, grid=(T // tm, N_CHIPS),
        in_specs=[pl.BlockSpec((tm, D), lambda i, j: (i, 0)),
                  pl.BlockSpec((None, None, n, D), lambda i, j: (j, layer, 0, 0)), slab, slab],
        out_specs=[slab, slab],
        out_shape=[hidden, hidden],
        compiler_params=_params("parallel", "arbitrary"),
    )(dh, w2, g1, g3)


def _dot_tn(a, b):
    return lax.dot_general(a.astype(BF16), b.astype(BF16), (((0,), (0,)), ((), ())), preferred_element_type=F32)


def _dot_nt(a, b):
    return lax.dot_general(a.astype(BF16), b, (((1,), (1,)), ((), ())), preferred_element_type=F32)


def _mm_tn(name, a, b, col_chunks=1):
    a_slabs, b_slabs = a.ndim == 3, b.ndim == 3
    T = a.shape[-2]
    tt = min(512, T)
    ka, nb = a.shape[-1], b.shape[-1]
    if a_slabs or b_slabs:
        out_dims = (N_CHIPS, ka, nb)
    elif col_chunks > 1:
        out_dims = (col_chunks, ka, nb // col_chunks)
    else:
        out_dims = (ka, nb)

    def body(a_ref, b_ref, o_ref):
        @pl.when(pl.program_id(0) == 0)
        def _():
            o_ref[...] = jnp.zeros_like(o_ref)

        if a_slabs:
            bb = b_ref[...].astype(BF16)
            for j in range(N_CHIPS):
                o_ref[j] += _dot_tn(a_ref[j], bb)
        elif b_slabs:
            aa = a_ref[...].astype(BF16)
            for j in range(N_CHIPS):
                o_ref[j] += _dot_tn(aa, b_ref[j])
        elif col_chunks > 1:
            aa = a_ref[...].astype(BF16)
            w = nb // col_chunks
            for j in range(col_chunks):
                o_ref[j] += _dot_tn(aa, b_ref[:, j * w:(j + 1) * w])
        else:
            o_ref[...] += _dot_tn(a_ref[...], b_ref[...])

    def spec(arr, slabs):
        if slabs:
            return pl.BlockSpec((N_CHIPS, tt, arr.shape[-1]), lambda t: (0, t, 0))
        return _rows(tt, arr.shape[-1])

    return pl.pallas_call(
        body, name=name, grid=(T // tt,),
        in_specs=[spec(a, a_slabs), spec(b, b_slabs)],
        out_specs=_whole(out_dims),
        out_shape=SDS(out_dims, F32),
        compiler_params=_params("arbitrary"),
    )(a, b)


def _mm_nt_normbwd(name, pairs, h, g, dh, after):
    T, D = h.shape
    tm = min(256, T)
    n_pairs = len(pairs)
    kinds = ["slabs" if dy.ndim == 3 else ("quarters" if w.ndim == 3 else "plain") for dy, w, _ in pairs]

    def body(*refs):
        dy_refs = refs[:n_pairs]
        w_refs = refs[n_pairs:2 * n_pairs]
        h_ref, g_ref, dh_ref, _, o_ref, dg_ref, cs_ref = refs[2 * n_pairs:]

        @pl.when(pl.program_id(0) == 0)
        def _():
            dg_ref[...] = jnp.zeros_like(dg_ref)
            cs_ref[...] = jnp.zeros_like(cs_ref)

        df = jnp.zeros((tm, D), F32)
        for dy_ref, w_ref, kd in zip(dy_refs, w_refs, kinds):
            if kd == "slabs":
                for j in range(N_CHIPS):
                    df = df + _dot_nt(dy_ref[j], w_ref[j])
            elif kd == "quarters":
                n = w_ref.shape[2]
                for j in range(N_CHIPS):
                    df = df + _dot_nt(dy_ref[:, j * n:(j + 1) * n], w_ref[j])
            else:
                df = df + _dot_nt(dy_ref[...], w_ref[...])
        hh = h_ref[...]
        r = _rms_rstd(hh)
        dg_ref[...] += jnp.sum(df * (hh * r), axis=0, keepdims=True)
        dxh = df * g_ref[...]
        out = dh_ref[...] + (r * dxh - hh * (r * r * r) * jnp.mean(dxh * hh, axis=-1, keepdims=True))
        o_ref[...] = out
        cs_ref[...] += jnp.sum(out, axis=0, keepdims=True)

    dy_specs, w_specs = [], []
    for (dy, w, layer), kd in zip(pairs, kinds):
        if kd == "slabs":
            dy_specs.append(pl.BlockSpec((N_CHIPS, tm, dy.shape[2]), lambda i: (0, i, 0)))
            w_specs.append(pl.BlockSpec((N_CHIPS, None, D, w.shape[3]),
                                        functools.partial(lambda i, layer: (0, layer, 0, 0), layer=layer)))
        else:
            dy_specs.append(_rows(tm, dy.shape[1]))
            w_specs.append(_whole(w.shape))

    return pl.pallas_call(
        body, name=name, grid=(T // tm,),
        in_specs=[*dy_specs, *w_specs, _rows(tm, D), _whole((1, D)), _rows(tm, D), pl.BlockSpec(memory_space=pl.ANY)],
        out_specs=[_rows(tm, D), _whole((1, D)), _whole((1, D))],
        out_shape=[SDS((T, D), F32), SDS((1, D), F32), SDS((1, D), F32)],
        compiler_params=_params("arbitrary"),
    )(*[dy for dy, _, _ in pairs], *[w for _, w, _ in pairs], h, g, dh, after)


def _mm_nt(name, dy, w, out_dtype):
    T, N = dy.shape
    K = w.shape[0]
    tm = min(512, T)

    def body(dy_ref, w_ref, o_ref):
        o_ref[...] = lax.dot_general(dy_ref[...].astype(BF16), w_ref[...], (((1,), (1,)), ((), ())),
                                     preferred_element_type=F32).astype(out_dtype)

    return pl.pallas_call(
        body, name=name, grid=(T // tm,),
        in_specs=[_rows(tm, N), _whole((K, N))],
        out_specs=_rows(tm, K),
        out_shape=SDS((T, K), out_dtype),
        compiler_params=_params("parallel"),
    )(dy, w)


def _conv_bwd(dact, c, a, w_dw, ln_g, ln_b):
    T, D = c.shape
    tc = _conv_tile(T)
    per = tc // CONV_HALO
    n_tiles = T // tc
    last_halo = T // CONV_HALO - 1
    first_tap = CONV_HALO - CONV_WIDTH + 1

    def ln_bwd(dact_v, c_v, lg, lb):
        xc = c_v - jnp.mean(c_v, axis=-1, keepdims=True)
        rstd = lax.rsqrt(jnp.mean(xc * xc, axis=-1, keepdims=True) + LN_EPS)
        z = xc * rstd
        _, dsilu = _silu_and_grad(z * lg + lb)
        dl = dact_v * dsilu
        dz = dl * lg
        dc = rstd * (dz - jnp.mean(dz, axis=-1, keepdims=True) - z * jnp.mean(dz * z, axis=-1, keepdims=True))
        return dc, dl, z

    def body(dact_ref, dactn_ref, c_ref, cn_ref, a_ref, ah_ref, w_ref, lg_ref, lb_ref,
             da_ref, dlg_ref, dlb_ref, dbdw_ref, dwdw_ref, dbpw1_ref, dc_scr, u_scr):
        i = pl.program_id(0)

        @pl.when(i == 0)
        def _():
            for ref in (dlg_ref, dlb_ref, dbdw_ref, dwdw_ref, dbpw1_ref):
                ref[...] = jnp.zeros_like(ref)

        lg, lb = lg_ref[...], lb_ref[...]
        dc, dl, z = ln_bwd(dact_ref[...], c_ref[...], lg, lb)
        dlg_ref[...] += jnp.sum(dl * z, axis=0, keepdims=True)
        dlb_ref[...] += jnp.sum(dl, axis=0, keepdims=True)
        dbdw_ref[...] += jnp.sum(dc, axis=0, keepdims=True)
        dcn, _, _ = ln_bwd(dactn_ref[...], cn_ref[...], lg, lb)
        dc_scr[0:tc, :] = dc
        dc_scr[tc:, :] = jnp.where(i < n_tiles - 1, dcn, 0.0)

        a_v = a_ref[...]
        a1 = a_v[:, :D]
        sg = jax.nn.sigmoid(a_v[:, D:])
        u_scr[0:CONV_HALO, :] = jnp.where(i > 0, _glu(ah_ref[...], D), 0.0)
        u_scr[CONV_HALO:, :] = a1 * sg

        du = jnp.zeros((tc, D), F32)
        for j in range(CONV_WIDTH):
            du = du + dc_scr[pl.ds(CONV_WIDTH - 1 - j, tc), :] * w_ref[j:j + 1, :]
            dwdw_ref[j:j + 1, :] += jnp.sum(dc * u_scr[pl.ds(first_tap + j, tc), :], axis=0, keepdims=True)

        da1 = du * sg
        da2 = du * a1 * sg * (1.0 - sg)
        da_ref[:, :D] = da1.astype(BF16)
        da_ref[:, D:] = da2.astype(BF16)
        dbpw1_ref[:, :D] += jnp.sum(da1, axis=0, keepdims=True)
        dbpw1_ref[:, D:] += jnp.sum(da2, axis=0, keepdims=True)

    nxt = lambda i: (jnp.minimum((i + 1) * per, last_halo), 0)
    return pl.pallas_call(
        body, name="conv_bwd", grid=(n_tiles,),
        in_specs=[_rows(tc, D), pl.BlockSpec((CONV_HALO, D), nxt),
                  _rows(tc, D), pl.BlockSpec((CONV_HALO, D), nxt),
                  _rows(tc, 2 * D),
                  pl.BlockSpec((CONV_HALO, 2 * D), lambda i: (jnp.maximum(i * per - 1, 0), 0)),
                  _whole((CONV_WIDTH, D)), _whole((1, D)), _whole((1, D))],
        out_specs=[_rows(tc, 2 * D), _whole((1, D)), _whole((1, D)), _whole((1, D)),
                   _whole((CONV_HALO, D)), _whole((1, 2 * D))],
        out_shape=[SDS((T, 2 * D), BF16), SDS((1, D), F32), SDS((1, D), F32), SDS((1, D), F32),
                   SDS((CONV_HALO, D), F32), SDS((1, 2 * D), F32)],
        scratch_shapes=[pltpu.VMEM((tc + CONV_HALO, D), F32), pltpu.VMEM((tc + CONV_HALO, D), F32)],
        compiler_params=_params("arbitrary"),
    )(dact, dact, c, c, a, a, w_dw, ln_g, ln_b)


def _attn_bwd(qkv, dao, cos, sin, sinks):
    T = qkv.shape[0]
    nb = T // ATTN_BLOCK
    qw = N_Q_HEADS * HEAD_DIM
    kw = N_KV_HEADS * HEAD_DIM

    def body(q_ref, kc_ref, kp_ref, vc_ref, vp_ref, do_ref, cos_ref, sin_ref, cosp_ref, sinp_ref, sink_ref,
             dq_ref, dkv_ref, dsink_ref, dbq_ref, dbkv_ref, carry, prev_scr, cur_scr, dq_scr):
        n = pl.program_id(0)

        @pl.when(n == 0)
        def _():
            for ref in (dsink_ref, dbq_ref, dbkv_ref, carry):
                ref[...] = jnp.zeros_like(ref)

        @pl.when(n == nb)
        def _():
            prev_scr[...] = jnp.zeros_like(prev_scr)

        @pl.when(n < nb)
        def _():
            mask = _band_mask(n)
            for g in range(N_KV_HEADS):
                q, k, v, sink = _group_operands(g, q_ref, kc_ref, kp_ref, vc_ref, vp_ref, sink_ref)
                heads = range(g * Q_PER_KV, (g + 1) * Q_PER_KV)
                do = jnp.concatenate([do_ref[:, h * HEAD_DIM:(h + 1) * HEAD_DIM] for h in heads], axis=0)
                probs, p_sink = _softmax_with_sink(q, k, sink, mask)
                dp = lax.dot_general(do, v, (((1,), (1,)), ((), ())), preferred_element_type=F32)
                delta = jnp.sum(probs * dp, axis=1, keepdims=True)
                ds = (probs * (dp - delta) * (HEAD_DIM ** -0.5)).astype(BF16)
                dsink_rows = -(p_sink * delta)
                dq = jnp.dot(ds, k, preferred_element_type=F32)
                dk = lax.dot_general(ds, q, (((0,), (0,)), ((), ())), preferred_element_type=F32)
                dv = lax.dot_general(probs.astype(BF16), do, (((0,), (0,)), ((), ())), preferred_element_type=F32)
                for i, h in enumerate(heads):
                    rows = slice(i * ATTN_BLOCK, (i + 1) * ATTN_BLOCK)
                    dq_scr[:, h * HEAD_DIM:(h + 1) * HEAD_DIM] = dq[rows]
                    dsink_ref[:, h:h + 1] += jnp.sum(dsink_rows[rows], axis=0, keepdims=True)
                lo = g * HEAD_DIM
                prev_scr[:, lo:lo + HEAD_DIM] = dk[:ATTN_BLOCK]
                cur_scr[:, lo:lo + HEAD_DIM] = dk[ATTN_BLOCK:]
                prev_scr[:, kw + lo:kw + lo + HEAD_DIM] = dv[:ATTN_BLOCK]
                cur_scr[:, kw + lo:kw + lo + HEAD_DIM] = dv[ATTN_BLOCK:]
            dq_pre = _rope_transposed(dq_scr[...], cos_ref, sin_ref)
            dq_ref[...] = dq_pre.astype(BF16)
            dbq_ref[...] += jnp.sum(dq_pre, axis=0, keepdims=True)

        tot = carry[...] + prev_scr[...]
        dk_pre = _rope_transposed(tot[:, :kw], cosp_ref, sinp_ref)
        dkv_ref[:, :kw] = dk_pre.astype(BF16)
        dkv_ref[:, kw:] = tot[:, kw:].astype(BF16)
        dbkv_ref[:, :kw] += jnp.sum(dk_pre, axis=0, keepdims=True)
        dbkv_ref[:, kw:] += jnp.sum(tot[:, kw:], axis=0, keepdims=True)

        @pl.when(n < nb)
        def _():
            carry[...] = cur_scr[...]

    cur = lambda n: (jnp.minimum(n, nb - 1), 0)
    out_lag = lambda n: (jnp.maximum(n - 1, 0), 0)
    return pl.pallas_call(
        body, name="attn_bwd", grid=(nb + 1,),
        in_specs=[*_attn_specs(T),
                  pl.BlockSpec((ATTN_BLOCK, qw), cur),
                  pl.BlockSpec((ATTN_BLOCK, V7X_LANES), cur), pl.BlockSpec((ATTN_BLOCK, V7X_LANES), cur),
                  pl.BlockSpec((ATTN_BLOCK, V7X_LANES), out_lag), pl.BlockSpec((ATTN_BLOCK, V7X_LANES), out_lag),
                  _whole((1, N_Q_HEADS))],
        out_specs=[pl.BlockSpec((ATTN_BLOCK, qw), cur), pl.BlockSpec((ATTN_BLOCK, 2 * kw), out_lag),
                   _whole((1, N_Q_HEADS)), _whole((1, qw)), _whole((1, 2 * kw))],
        out_shape=[SDS((T, qw), BF16), SDS((T, 2 * kw), BF16),
                   SDS((1, N_Q_HEADS), F32), SDS((1, qw), F32), SDS((1, 2 * kw), F32)],
        scratch_shapes=[pltpu.VMEM((ATTN_BLOCK, 2 * kw), F32), pltpu.VMEM((ATTN_BLOCK, 2 * kw), F32),
                        pltpu.VMEM((ATTN_BLOCK, 2 * kw), F32), pltpu.VMEM((ATTN_BLOCK, qw), F32)],
        compiler_params=_params("arbitrary"),
    )(qkv, qkv, qkv, qkv, qkv, dao, cos, sin, cos, sin, sinks)


def _local_step(x, target, p, reduce_large):
    T, D = x.shape
    cos, sin = _rope_tables(T)
    qw = N_Q_HEADS * HEAD_DIM
    nm, nf = p["norm_mix"], p["norm_ffn"]

    y0, qkv = _qkv_proj(x, nm[0:1], p["attn_w_qkv"], p["attn_b_qkv"], cos, sin, p["gather_started"])
    ao = _attn_fwd(qkv, p["attn_sinks"])
    h1 = _mm_res("attn_out", ao, p["attn_w_o"], p["attn_b_o"], x)
    p = {**p, **p["other_weights"](h1)}
    w1, w3, w2 = p["ffn_w1"], p["ffn_w3"], p["ffn_w2"]
    f0, g1a, g3a, s0 = _ffn_up("ffn0_up", h1, nf[0:1], w1, w3, 0)
    h2 = _ffn_down("ffn0_down", s0, w2, 0, h1)
    y1, a = _pw1_proj(h2, nm[1:2], p["conv_w_pw1"], p["conv_b_pw1"])
    c, act = _conv_fwd(a, p["conv_w_dw"], p["conv_b_dw"], p["conv_ln_g"], p["conv_ln_b"])
    h3 = _mm_res("conv_out", act, p["conv_w_pw2"], p["conv_b_pw2"], h2)
    f1, g1b, g3b, s1 = _ffn_up("ffn1_up", h3, nf[1:2], w1, w3, 1)
    h4 = _ffn_down("ffn1_down", s1, w2, 1, h3)
    dh4, loss, d_norm_final = _final_loss(h4, p["norm_final"], target)

    g = {}
    dg1, dg3 = _ffn_bwd_act("ffn1_bwd_act", dh4, w2, 1, g1b, g3b)
    dw2_1 = _mm_tn("ffn1_dw2", s1, dh4)
    dw1_1 = _mm_tn("ffn1_dw1", f1, dg1)
    dw3_1 = _mm_tn("ffn1_dw3", f1, dg3)
    started = reduce_large("ffn1", {("ffn_w1", 1): dw1_1, ("ffn_w3", 1): dw3_1, ("ffn_w2", 1): dw2_1})
    dh3, dnf1, db_pw2 = _mm_nt_normbwd("ffn1_bwd_in", [(dg1, w1, 1), (dg3, w3, 1)], h3, nf[1:2], dh4, started)

    dw_pw2 = _mm_tn("conv_dw_pw2", act, dh3)
    dact = _mm_nt("conv_bwd_out", dh3, p["conv_w_pw2"], F32)
    da, d_ln_g, d_ln_b, d_b_dw, d_w_dw, d_b_pw1 = _conv_bwd(dact, c, a, p["conv_w_dw"], p["conv_ln_g"], p["conv_ln_b"])
    dw_pw1 = _mm_tn("conv_dw_pw1", y1, da, col_chunks=N_CHIPS)
    started = reduce_large("conv", {("conv_w_pw2", 0): dw_pw2.reshape(N_CHIPS, -1, D), ("conv_w_pw1", 0): dw_pw1})
    dh2, dnm1, _ = _mm_nt_normbwd("conv_bwd_in", [(da, p["conv_w_pw1"], None)], h2, nm[1:2], dh3, started)

    dg1, dg3 = _ffn_bwd_act("ffn0_bwd_act", dh2, w2, 0, g1a, g3a)
    dw2_0 = _mm_tn("ffn0_dw2", s0, dh2)
    dw1_0 = _mm_tn("ffn0_dw1", f0, dg1)
    dw3_0 = _mm_tn("ffn0_dw3", f0, dg3)
    started = reduce_large("ffn0", {("ffn_w1", 0): dw1_0, ("ffn_w3", 0): dw3_0, ("ffn_w2", 0): dw2_0})
    dh1, dnf0, db_o = _mm_nt_normbwd("ffn0_bwd_in", [(dg1, w1, 0), (dg3, w3, 0)], h1, nf[0:1], dh2, started)

    dw_o = _mm_tn("attn_dw_o", ao, dh1)
    dao = _mm_nt("attn_bwd_out", dh1, p["attn_w_o"], BF16)
    dq, dkv, d_sinks, dbq, dbkv = _attn_bwd(qkv, dao, cos, sin, p["attn_sinks"])
    dwq = _mm_tn("attn_dw_q", y0, dq)
    dwkv = _mm_tn("attn_dw_kv", y0, dkv)
    wqkv = p["attn_w_qkv"]
    n_qkv = wqkv.shape[1] // N_CHIPS
    dwqkv = jnp.moveaxis(jnp.concatenate([dwq, dwkv], axis=1).reshape(D, N_CHIPS, n_qkv), 1, 0)
    started = reduce_large("attn", {("attn_w_o", 0): dw_o.reshape(N_CHIPS, -1, D), ("attn_w_qkv", 0): dwqkv})
    dx, dnm0, _ = _mm_nt_normbwd("attn_bwd_in", [(dq, wqkv[:, :qw], None), (dkv, wqkv[:, qw:], None)], x, nm[0:1], dh1,
                                 started)

    g["norm_mix"] = jnp.concatenate([dnm0, dnm1], axis=0)
    g["norm_ffn"] = jnp.concatenate([dnf0, dnf1], axis=0)
    g["attn_b_qkv"] = jnp.concatenate([dbq, dbkv], axis=1)
    g["attn_sinks"] = d_sinks
    g["attn_b_o"] = db_o
    g["conv_b_pw1"] = d_b_pw1
    g["conv_w_dw"] = d_w_dw[:CONV_WIDTH]
    g["conv_b_dw"] = d_b_dw
    g["conv_ln_g"] = d_ln_g
    g["conv_ln_b"] = d_ln_b
    g["conv_b_pw2"] = db_pw2
    g["norm_final"] = d_norm_final
    return loss, dx, g


ANY = pl.BlockSpec(memory_space=pl.ANY)
VMEM_WHOLE = pl.BlockSpec(memory_space=pltpu.VMEM)


def _my_place():
    return lax.axis_index("x"), lax.axis_index("y"), lax.axis_index("c")


def _other_chips(x, y):
    places = [(1 - x, y), (x, 1 - y), (1 - x, 1 - y)]
    return [(bx, by, 2 * bx + by) for bx, by in places]


def _exchange_small(name, v, reduce):
    r, w = v.shape

    def body(v_ref, o_ref, *rest):
        if reduce:
            buf, send_sems, recv_sems = rest
        else:
            buf = o_ref
            send_sems, recv_sems = rest
        x, y, c = _my_place()
        me = 4 * x + 2 * y + c
        sends = []
        for k in range(1, N_DEV):
            peer = (1 - x if k & 4 else x, 1 - y if k & 2 else y, 1 - c if k & 1 else c)
            cp = pltpu.make_async_remote_copy(
                src_ref=v_ref, dst_ref=buf.at[me], send_sem=send_sems.at[k - 1], recv_sem=recv_sems.at[k - 1],
                device_id=peer, device_id_type=MESH)
            cp.start()
            sends.append(cp)
        buf[me] = v_ref[...]
        for k in range(1, N_DEV):
            src = 4 * (1 - x if k & 4 else x) + 2 * (1 - y if k & 2 else y) + (1 - c if k & 1 else c)
            pltpu.make_async_remote_copy(
                src_ref=v_ref, dst_ref=buf.at[src], send_sem=send_sems.at[k - 1], recv_sem=recv_sems.at[k - 1],
                device_id=(x, y, c), device_id_type=MESH).wait_recv()
        for cp in sends:
            cp.wait_send()
        if reduce:
            acc = buf[0]
            for d in range(1, N_DEV):
                acc = acc + buf[d]
            o_ref[...] = acc

    sems = [pltpu.SemaphoreType.DMA((N_DEV - 1,)), pltpu.SemaphoreType.DMA((N_DEV - 1,))]
    if reduce:
        out_shape = SDS((r, w), F32)
        scratch = [pltpu.VMEM((N_DEV, r, w), F32)] + sems
    else:
        out_shape = SDS((N_DEV, r, w), F32)
        scratch = sems
    return pl.pallas_call(
        body, name=name, out_shape=out_shape, in_specs=[VMEM_WHOLE], out_specs=VMEM_WHOLE,
        scratch_shapes=scratch,
        compiler_params=pltpu.CompilerParams(vmem_limit_bytes=V7X_VMEM_LIMIT_BYTES),
    )(v)


def _cast_into_slot(name, gathered, shard, chip_idx):
    rows, cols = shard.shape
    tr = _pack_row_tile(rows)

    def body(k_ref, s_ref, g_ref, o_ref):
        o_ref[...] = s_ref[...].astype(BF16)

    return pl.pallas_call(
        body, name=name,
        grid_spec=pltpu.PrefetchScalarGridSpec(
            num_scalar_prefetch=1, grid=(rows // tr,),
            in_specs=[pl.BlockSpec((tr, cols), lambda i, k_ref: (i, 0)), pl.BlockSpec(memory_space=pl.ANY)],
            out_specs=pl.BlockSpec((None, tr, cols), lambda i, k_ref: (k_ref[0], i, 0))),
        out_shape=SDS(gathered.shape, BF16),
        input_output_aliases={2: 0},
        compiler_params=_params("parallel"),
    )(chip_idx, shard, gathered)


def _row_halves(ref, c):
    half = ref.shape[1] // 2
    return pl.ds(pl.multiple_of(c * half, 16), half), pl.ds(pl.multiple_of((1 - c) * half, 16), half)


def _gather_ici_copies(refs, send_sems, recv_sems):
    x, y, c = _my_place()
    k = 2 * x + y
    pairs = []
    for i, ref in enumerate(refs):
        mine, _ = _row_halves(ref, c)
        for j, (bx, by, kb) in enumerate(_other_chips(x, y)):
            sems = dict(send_sem=send_sems.at[3 * i + j], recv_sem=recv_sems.at[3 * i + j], device_id_type=MESH)
            send = pltpu.make_async_remote_copy(src_ref=ref.at[k, mine], dst_ref=ref.at[k, mine],
                                                device_id=(bx, by, c), **sems)
            arrival = pltpu.make_async_remote_copy(src_ref=ref.at[kb, mine], dst_ref=ref.at[kb, mine],
                                                   device_id=(bx, by, c), **sems)
            pairs.append((send, arrival))
    return pairs


def _gather_d2d_copies(refs, send_sems, recv_sems, first_sem):
    x, y, c = _my_place()
    pairs = []
    for i, ref in enumerate(refs):
        mine, theirs = _row_halves(ref, c)
        for j, (_, _, kb) in enumerate(_other_chips(x, y)):
            sem = first_sem + 3 * i + j
            sems = dict(send_sem=send_sems.at[sem], recv_sem=recv_sems.at[sem], device_id=(x, y, 1 - c),
                        device_id_type=MESH)
            send = pltpu.make_async_remote_copy(src_ref=ref.at[kb, mine], dst_ref=ref.at[kb, mine], **sems)
            arrival = pltpu.make_async_remote_copy(src_ref=ref.at[kb, theirs], dst_ref=ref.at[kb, theirs], **sems)
            pairs.append((send, arrival))
    return pairs


def _run_copies(pairs):
    for send, _ in pairs:
        send.start()
    for send, arrival in pairs:
        send.wait_send()
        arrival.wait_recv()


def _gather_now(name, gathered):
    n_w = len(gathered)

    def body(*refs):
        in_refs = refs[:n_w]
        send_sems, recv_sems = refs[2 * n_w:]
        _run_copies(_gather_ici_copies(in_refs, send_sems, recv_sems))
        _run_copies(_gather_d2d_copies(in_refs, send_sems, recv_sems, 3 * n_w))

    return pl.pallas_call(
        body, name=name, out_shape=[SDS(g.shape, g.dtype) for g in gathered],
        in_specs=[ANY] * n_w, out_specs=[ANY] * n_w, input_output_aliases={i: i for i in range(n_w)},
        scratch_shapes=[pltpu.SemaphoreType.DMA((6 * n_w,)), pltpu.SemaphoreType.DMA((6 * n_w,))],
    )(*gathered)


def _gather_start(name, gathered):
    n_w = len(gathered)

    def body(*refs):
        in_refs = refs[:n_w]
        send_sems, recv_sems = refs[n_w:n_w + 2]
        for send, _ in _gather_ici_copies(in_refs, send_sems, recv_sems):
            send.start()
        refs[-1][...] = jnp.zeros_like(refs[-1])

    out = pl.pallas_call(
        body, name=name,
        out_shape=(pltpu.SemaphoreType.DMA((3 * n_w,)), pltpu.SemaphoreType.DMA((3 * n_w,)),
                   *[pltpu.HBM(g.shape, g.dtype) for g in gathered], SDS((8, V7X_LANES), F32)),
        in_specs=[HBM_SPEC] * n_w, out_specs=(SEM_SPEC, SEM_SPEC, *[HBM_SPEC] * n_w, VMEM_WHOLE),
        input_output_aliases={i: 2 + i for i in range(n_w)},
        compiler_params=pltpu.CompilerParams(has_side_effects=DATAFLOW),
    )(*[pltpu.with_memory_space_constraint(g, pltpu.HBM) for g in gathered])
    return out[0], out[1], list(out[2:2 + n_w]), out[-1]


def _gather_wait(name, send_sems, recv_sems, gathered, after):
    n_w = len(gathered)

    def body(*refs):
        in_refs = refs[:n_w]
        send_sems, recv_sems = refs[n_w:n_w + 2]
        for send, arrival in _gather_ici_copies(in_refs, send_sems, recv_sems):
            send.wait_send()
            arrival.wait_recv()

    out = pl.pallas_call(
        body, name=name, out_shape=tuple(pltpu.HBM(g.shape, g.dtype) for g in gathered),
        in_specs=[*[HBM_SPEC] * n_w, SEM_SPEC, SEM_SPEC, ANY], out_specs=tuple([HBM_SPEC] * n_w),
        input_output_aliases={i: i for i in range(n_w)},
        compiler_params=pltpu.CompilerParams(has_side_effects=DATAFLOW),
    )(*gathered, send_sems, recv_sems, after)
    return list(out)


def _swap_fetched_with_sibling(name, gathered):
    n_w = len(gathered)

    def body(*refs):
        in_refs = refs[:n_w]
        send_sems, recv_sems = refs[2 * n_w:]
        _run_copies(_gather_d2d_copies(in_refs, send_sems, recv_sems, 0))

    return pl.pallas_call(
        body, name=name, out_shape=[SDS(g.shape, g.dtype) for g in gathered],
        in_specs=[ANY] * n_w, out_specs=[ANY] * n_w, input_output_aliases={i: i for i in range(n_w)},
        scratch_shapes=[pltpu.SemaphoreType.DMA((3 * n_w,)), pltpu.SemaphoreType.DMA((3 * n_w,))],
    )(*gathered)


def _swap_halves_with_sibling(name, grads):
    n_g = len(grads)

    def body(*refs):
        g_refs, o_refs = refs[:n_g], refs[n_g:2 * n_g]
        send_sems, recv_sems = refs[2 * n_g:]
        x, y, c = _my_place()
        copies = []
        for i in range(n_g):
            half = g_refs[i].shape[1] // 2
            theirs = pl.ds(pl.multiple_of((1 - c) * half, 8), half)
            cp = pltpu.make_async_remote_copy(
                src_ref=g_refs[i].at[:, theirs], dst_ref=o_refs[i], send_sem=send_sems.at[i],
                recv_sem=recv_sems.at[i], device_id=(x, y, 1 - c), device_id_type=MESH)
            cp.start()
            copies.append(cp)
        for cp in copies:
            cp.wait()

    return pl.pallas_call(
        body, name=name,
        out_shape=[SDS((g.shape[0], g.shape[1] // 2, g.shape[2]), g.dtype) for g in grads],
        in_specs=[ANY] * n_g, out_specs=[ANY] * n_g,
        scratch_shapes=[pltpu.SemaphoreType.DMA((n_g,)), pltpu.SemaphoreType.DMA((n_g,))],
    )(*grads)


def _pack_row_tile(rows):
    for t in range(min(rows, 512), 7, -1):
        if rows % t == 0 and t % 8 == 0:
            return t
    return rows


def _add_sibling_half(name, grads, from_sibling, c_idx):
    n, R, w = grads.shape
    half = R // 2
    tr = _pack_row_tile(half)
    steps = half // tr

    def body(c_ref, g_ref, s_ref, o_ref):
        o_ref[...] = g_ref[...] + s_ref[...]

    return pl.pallas_call(
        body, name=name,
        grid_spec=pltpu.PrefetchScalarGridSpec(
            num_scalar_prefetch=1, grid=(n, steps),
            in_specs=[pl.BlockSpec((1, tr, w), lambda j, i, c_ref: (j, c_ref[0] * steps + i, 0)),
                      pl.BlockSpec((1, tr, w), lambda j, i, c_ref: (j, i, 0))],
            out_specs=pl.BlockSpec((1, tr, w), lambda j, i, c_ref: (j, i, 0))),
        out_shape=SDS((n, half, w), F32),
        compiler_params=_params("parallel", "parallel"),
    )(c_idx, grads, from_sibling)


HBM_SPEC = pl.BlockSpec(memory_space=pltpu.HBM)
SEM_SPEC = pl.BlockSpec(memory_space=pltpu.SEMAPHORE)
DATAFLOW = pltpu.SideEffectType.DATAFLOW_SIDE_EFFECTING


def _chip_scatter_copies(p_refs, land_refs, send_sems, recv_sems):
    x, y, c = _my_place()
    return [pltpu.make_async_remote_copy(
        src_ref=p_refs[i].at[kb], dst_ref=land_refs[i].at[j], send_sem=send_sems.at[3 * i + j],
        recv_sem=recv_sems.at[3 * i + j], device_id=(bx, by, c), device_id_type=MESH)
        for i in range(len(p_refs)) for j, (bx, by, kb) in enumerate(_other_chips(x, y))]


def _scatter_start(name, partials):
    n_p = len(partials)

    def body(*refs):
        p_refs, land_refs = refs[:n_p], refs[n_p:2 * n_p]
        send_sems, recv_sems = refs[2 * n_p:2 * n_p + 2]
        for cp in _chip_scatter_copies(p_refs, land_refs, send_sems, recv_sems):
            cp.start()
        refs[-1][...] = jnp.zeros_like(refs[-1])

    lands = [pltpu.with_memory_space_constraint(lax.empty((N_CHIPS - 1,) + p.shape[1:], p.dtype), pltpu.HBM)
             for p in partials]
    out = pl.pallas_call(
        body, name=name,
        out_shape=(pltpu.SemaphoreType.DMA((3 * n_p,)), pltpu.SemaphoreType.DMA((3 * n_p,)),
                   *[pltpu.HBM(p.shape, p.dtype) for p in partials], *[pltpu.HBM(l.shape, l.dtype) for l in lands],
                   SDS((8, V7X_LANES), F32)),
        in_specs=[HBM_SPEC] * (2 * n_p), out_specs=(SEM_SPEC, SEM_SPEC, *[HBM_SPEC] * (2 * n_p), VMEM_WHOLE),
        input_output_aliases={i: 2 + i for i in range(2 * n_p)},
        compiler_params=pltpu.CompilerParams(has_side_effects=DATAFLOW),
    )(*[pltpu.with_memory_space_constraint(p, pltpu.HBM) for p in partials], *lands)
    return out[0], out[1], list(out[2:2 + n_p]), list(out[2 + n_p:2 + 2 * n_p]), out[-1]


def _scatter_wait(name, send_sems, recv_sems, partials, lands, after):
    n_p = len(partials)

    def body(*refs):
        p_refs, land_refs = refs[:n_p], refs[n_p:2 * n_p]
        send_sems, recv_sems = refs[2 * n_p:2 * n_p + 2]
        for cp in _chip_scatter_copies(p_refs, land_refs, send_sems, recv_sems):
            cp.wait_send()
            cp.wait_recv()

    out = pl.pallas_call(
        body, name=name,
        out_shape=(*[pltpu.HBM(p.shape, p.dtype) for p in partials], *[pltpu.HBM(l.shape, l.dtype) for l in lands]),
        in_specs=[*[HBM_SPEC] * (2 * n_p), SEM_SPEC, SEM_SPEC, ANY], out_specs=tuple([HBM_SPEC] * (2 * n_p)),
        input_output_aliases={i: i for i in range(2 * n_p)},
        compiler_params=pltpu.CompilerParams(has_side_effects=DATAFLOW),
    )(*partials, *lands, send_sems, recv_sems, after)
    return list(out[:n_p]), list(out[n_p:])


def _sum_chip_partials(name, partial, received, shard, layer, place):
    n, half, w = partial.shape
    tr = _pack_row_tile(half)
    steps = half // tr

    def body(place_ref, p_ref, r_ref, shard_ref, o_ref):
        o_ref[...] = ((p_ref[0] + r_ref[0]) + r_ref[1]) + r_ref[2]

    return pl.pallas_call(
        body, name=name,
        grid_spec=pltpu.PrefetchScalarGridSpec(
            num_scalar_prefetch=1, grid=(steps,),
            in_specs=[pl.BlockSpec((1, tr, w), lambda i, place_ref: (place_ref[0], i, 0)),
                      pl.BlockSpec((n - 1, tr, w), lambda i, place_ref: (0, i, 0)),
                      pl.BlockSpec(memory_space=pl.ANY)],
            out_specs=pl.BlockSpec((tr, w), lambda i, place_ref: ((2 * layer + place_ref[1]) * steps + i, 0))),
        out_shape=SDS(shard.shape, F32),
        input_output_aliases={3: 0},
        compiler_params=_params("parallel"),
    )(place, partial, received, shard)


def _join_halves(shards, layers):
    n_s = len(shards)
    n_sem = sum(layers)

    def body(*refs):
        in_refs = refs[:n_s]
        send_sems, recv_sems = refs[2 * n_s:]
        x, y, c = _my_place()
        copies, sem = [], 0
        for ref, n_layers in zip(in_refs, layers):
            half = ref.shape[0] // (2 * n_layers)
            for layer in range(n_layers):
                mine = pl.ds(pl.multiple_of(layer * 2 * half + c * half, 8), half)
                theirs = pl.ds(pl.multiple_of(layer * 2 * half + (1 - c) * half, 8), half)
                send = pltpu.make_async_remote_copy(
                    src_ref=ref.at[mine], dst_ref=ref.at[mine], send_sem=send_sems.at[sem], recv_sem=recv_sems.at[sem],
                    device_id=(x, y, 1 - c), device_id_type=MESH)
                send.start()
                arrival = pltpu.make_async_remote_copy(
                    src_ref=ref.at[theirs], dst_ref=ref.at[theirs], send_sem=send_sems.at[sem],
                    recv_sem=recv_sems.at[sem], device_id=(x, y, 1 - c), device_id_type=MESH)
                copies.append((send, arrival))
                sem += 1
        for send, arrival in copies:
            send.wait_send()
            arrival.wait_recv()

    return pl.pallas_call(
        body, name="join_halves", out_shape=[SDS(s.shape, s.dtype) for s in shards],
        in_specs=[ANY] * n_s, out_specs=[ANY] * n_s,
        input_output_aliases={i: i for i in range(n_s)},
        scratch_shapes=[pltpu.SemaphoreType.DMA((n_sem,)), pltpu.SemaphoreType.DMA((n_sem,))],
    )(*shards)


def _adamw(name, w, g, m, v):
    rows, width = w.shape
    tr = _pack_row_tile(rows)

    def body(w_ref, g_ref, m_ref, v_ref, d_ref, nm_ref, nv_ref):
        gg = g_ref[...]
        m_new = ADAM_B1 * m_ref[...] + (1.0 - ADAM_B1) * gg
        v_new = ADAM_B2 * v_ref[...] + (1.0 - ADAM_B2) * (gg * gg)
        m_hat = m_new / (1.0 - ADAM_B1 ** ADAM_STEP)
        v_hat = v_new / (1.0 - ADAM_B2 ** ADAM_STEP)
        d_ref[...] = -ADAM_LR * (m_hat / (jnp.sqrt(v_hat) + ADAM_EPS) + ADAM_WD * w_ref[...])
        nm_ref[...] = m_new
        nv_ref[...] = v_new

    spec = _rows(tr, width)
    return pl.pallas_call(
        body, name=name, grid=(rows // tr,),
        in_specs=[spec] * 4, out_specs=[spec] * 3,
        out_shape=[SDS((rows, width), F32)] * 3,
        compiler_params=_params("parallel"),
    )(w, g, m, v)


WEIGHT_NAMES = ['norm_mix', 'norm_ffn', 'attn_w_qkv', 'attn_b_qkv', 'attn_sinks', 'attn_w_o', 'attn_b_o',
                'conv_w_pw1', 'conv_b_pw1', 'conv_w_dw', 'conv_b_dw', 'conv_ln_g', 'conv_ln_b', 'conv_w_pw2',
                'conv_b_pw2', 'ffn_w1', 'ffn_w3', 'ffn_w2', 'norm_final']
BIG = ['attn_w_qkv', 'attn_w_o', 'conv_w_pw1', 'conv_w_pw2', 'ffn_w1', 'ffn_w3', 'ffn_w2']
COLUMN_SPLIT = ('attn_w_qkv', 'conv_w_pw1', 'ffn_w1', 'ffn_w3')
SMALL_SPLIT = ['conv_b_pw1', 'conv_w_dw', 'conv_b_dw', 'conv_ln_g', 'conv_ln_b', 'conv_b_pw2']
SMALL_WHOLE = ['norm_mix', 'norm_ffn', 'attn_b_qkv', 'attn_sinks', 'attn_b_o', 'norm_final']


def _pack_rows(arrays, dtype, row_multiple):
    flat = jnp.concatenate([a.astype(dtype).reshape(-1) for a in arrays])
    rows = -(-flat.shape[0] // PACK_W)
    rows = -(-rows // row_multiple) * row_multiple
    return jnp.pad(flat, (0, rows * PACK_W - flat.shape[0])).reshape(rows, PACK_W)


def _unpack_rows(pack, shapes):
    flat = pack.reshape(-1)
    out, at = [], 0
    for shape in shapes:
        size = 1
        for s in shape:
            size *= s
        out.append(flat[at:at + size].reshape(shape))
        at += size
    return out


def _join_chip_axis(name, parts):
    axis = parts.ndim - 1 if name in COLUMN_SPLIT or name in SMALL_SPLIT else parts.ndim - 2
    moved = jnp.moveaxis(parts, 0, axis - 1)
    shape = list(moved.shape)
    shape[axis - 1:axis + 1] = [shape[axis - 1] * shape[axis]]
    return moved.reshape(shape)


def _split_chip_axis(name, whole, shard_shape):
    axis = len(shard_shape) - 1 if name in COLUMN_SPLIT or name in SMALL_SPLIT else len(shard_shape) - 2
    shape = list(whole.shape)
    shape[axis:axis + 1] = [N_CHIPS, shard_shape[axis]]
    return jnp.moveaxis(whole.reshape(shape), axis, 0)


def kernel(x, norm_mix, norm_ffn, attn_w_qkv, attn_b_qkv, attn_sinks, attn_w_o, attn_b_o, conv_w_pw1, conv_b_pw1, conv_w_dw, conv_b_dw, conv_ln_g, conv_ln_b, conv_w_pw2, conv_b_pw2, ffn_w1, ffn_w3, ffn_w2, norm_final, loss_target, m_norm_mix, m_norm_ffn, m_attn_w_qkv, m_attn_b_qkv, m_attn_sinks, m_attn_w_o, m_attn_b_o, m_conv_w_pw1, m_conv_b_pw1, m_conv_w_dw, m_conv_b_dw, m_conv_ln_g, m_conv_ln_b, m_conv_w_pw2, m_conv_b_pw2, m_ffn_w1, m_ffn_w3, m_ffn_w2, m_norm_final, v_norm_mix, v_norm_ffn, v_attn_w_qkv, v_attn_b_qkv, v_attn_sinks, v_attn_w_o, v_attn_b_o, v_conv_w_pw1, v_conv_b_pw1, v_conv_w_dw, v_conv_b_dw, v_conv_ln_g, v_conv_ln_b, v_conv_w_pw2, v_conv_b_pw2, v_ffn_w1, v_ffn_w3, v_ffn_w2, v_norm_final):
    w = dict(zip(WEIGHT_NAMES, (norm_mix, norm_ffn, attn_w_qkv, attn_b_qkv, attn_sinks, attn_w_o, attn_b_o,
                                conv_w_pw1, conv_b_pw1, conv_w_dw, conv_b_dw, conv_ln_g, conv_ln_b, conv_w_pw2,
                                conv_b_pw2, ffn_w1, ffn_w3, ffn_w2, norm_final)))
    m = dict(zip(WEIGHT_NAMES, (m_norm_mix, m_norm_ffn, m_attn_w_qkv, m_attn_b_qkv, m_attn_sinks, m_attn_w_o,
                                m_attn_b_o, m_conv_w_pw1, m_conv_b_pw1, m_conv_w_dw, m_conv_b_dw, m_conv_ln_g,
                                m_conv_ln_b, m_conv_w_pw2, m_conv_b_pw2, m_ffn_w1, m_ffn_w3, m_ffn_w2, m_norm_final)))
    v = dict(zip(WEIGHT_NAMES, (v_norm_mix, v_norm_ffn, v_attn_w_qkv, v_attn_b_qkv, v_attn_sinks, v_attn_w_o,
                                v_attn_b_o, v_conv_w_pw1, v_conv_b_pw1, v_conv_w_dw, v_conv_b_dw, v_conv_ln_g,
                                v_conv_ln_b, v_conv_w_pw2, v_conv_b_pw2, v_ffn_w1, v_ffn_w3, v_ffn_w2, v_norm_final)))
    T, D = x.shape[1], x.shape[2]
    c_idx = lax.axis_index("c").astype(jnp.int32).reshape(1)
    chip = (2 * lax.axis_index("x") + lax.axis_index("y")).astype(jnp.int32)

    as_rows = lambda a: a.reshape(-1, a.shape[-1])
    slabs = {n: _cast_into_slot(f"cast_{n}", lax.empty((N_CHIPS,) + as_rows(w[n]).shape, BF16), as_rows(w[n]),
                                chip.reshape(1)) for n in BIG}
    first, later = BIG[:2], BIG[2:]
    qkv_parts, w_o_parts = _gather_now("gather_attn", [slabs[n] for n in first])
    send_sems, recv_sems, travelling, gather_started = _gather_start("gather_start", [slabs[n] for n in later])
    layers = ffn_w1.shape[0]
    small_shapes = [w[n].shape for n in SMALL_SPLIT]
    small_all = _exchange_small("gather_small", _pack_rows([w[n] for n in SMALL_SPLIT], F32, 8), reduce=False)
    per_chip = [_unpack_rows(small_all[2 * j], small_shapes) for j in range(N_CHIPS)]
    full = {}
    for i, n in enumerate(SMALL_SPLIT):
        full[n] = _join_chip_axis(n, jnp.stack([per_chip[j][i] for j in range(N_CHIPS)]))

    def other_weights(after):
        landed = _gather_wait("gather_wait", send_sems, recv_sems, travelling, after)
        gathered = dict(zip(later, _swap_fetched_with_sibling("gather_swap", landed)))
        return {"conv_w_pw1": gathered["conv_w_pw1"], "conv_w_pw2": gathered["conv_w_pw2"].reshape(-1, D),
                "ffn_w1": gathered["ffn_w1"].reshape(N_CHIPS, layers, D, -1),
                "ffn_w3": gathered["ffn_w3"].reshape(N_CHIPS, layers, D, -1),
                "ffn_w2": gathered["ffn_w2"].reshape(N_CHIPS, layers, -1, D)}

    p = {
        "norm_mix": norm_mix, "norm_ffn": norm_ffn, "norm_final": norm_final.reshape(1, D),
        "attn_w_qkv": jnp.moveaxis(qkv_parts, 0, 1).reshape(D, -1), "attn_b_qkv": attn_b_qkv,
        "attn_sinks": attn_sinks, "attn_w_o": w_o_parts.reshape(-1, D), "attn_b_o": attn_b_o,
        "conv_b_pw1": full["conv_b_pw1"], "conv_w_dw": full["conv_w_dw"][0],
        "conv_b_dw": full["conv_b_dw"], "conv_ln_g": full["conv_ln_g"], "conv_ln_b": full["conv_ln_b"],
        "conv_b_pw2": full["conv_b_pw2"], "gather_started": gather_started, "other_weights": other_weights,
    }
    in_flight = []

    def reduce_large(tag, grads):
        keys = list(grads)
        from_sibling = _swap_halves_with_sibling(f"grads_to_sibling_{tag}", [grads[k] for k in keys])
        partials = [_add_sibling_half(f"add_sibling_half_{tag}{i}", grads[k], fs, c_idx)
                    for i, (k, fs) in enumerate(zip(keys, from_sibling))]
        *handles, started = _scatter_start(f"scatter_start_{tag}", partials)
        in_flight.append((tag, keys, handles))
        return started

    loss_part, dx, g = _local_step(x[0], loss_target[0], p, reduce_large)
    for n in SMALL_WHOLE + SMALL_SPLIT:
        g[n] = g[n].reshape((-1,) + g[n].shape[-2:]) if w[n].ndim == 3 else g[n].reshape(w[n].shape[:-1] + (-1,))

    place = jnp.stack([chip, c_idx[0]])
    shard_grad = {n: lax.empty(as_rows(w[n]).shape, F32) for n in BIG}
    for tag, keys, (send_sems, recv_sems, partials, lands) in in_flight:
        partials, received = _scatter_wait(f"scatter_wait_{tag}", send_sems, recv_sems, partials, lands, dx)
        for i, (n, layer) in enumerate(keys):
            shard_grad[n] = _sum_chip_partials(f"sum_chip_partials_{tag}{i}", partials[i], received[i],
                                               shard_grad[n], layer, place)
    g_big = dict(zip(BIG, _join_halves([shard_grad[n] for n in BIG], [w[n].shape[0] for n in BIG])))
    big_out = {}
    for n in BIG:
        d_n, m_n, v_n = _adamw(f"adamw_{n}", as_rows(w[n]), g_big[n], as_rows(m[n]), as_rows(v[n]))
        big_out[n] = [a.reshape(w[n].shape) for a in (g_big[n], d_n, m_n, v_n)]

    small_whole_shapes = [w[n].shape for n in SMALL_WHOLE]
    small_full_shapes = [g[n].shape for n in SMALL_SPLIT]
    reduced = _exchange_small(
        "reduce_small", _pack_rows([loss_part] + [g[n] for n in SMALL_WHOLE] + [g[n] for n in SMALL_SPLIT], F32, 8),
        reduce=True)
    pieces = _unpack_rows(reduced, [(1,)] + small_whole_shapes + small_full_shapes)
    loss = pieces[0].reshape(())
    g_small = dict(zip(SMALL_WHOLE, pieces[1:1 + len(SMALL_WHOLE)]))
    for n, whole in zip(SMALL_SPLIT, pieces[1 + len(SMALL_WHOLE):]):
        parts = _split_chip_axis(n, whole, w[n].shape)
        g_small[n] = lax.dynamic_index_in_dim(parts, chip, axis=0, keepdims=False)
    small = SMALL_WHOLE + SMALL_SPLIT
    d_small, m_small, v_small = _adamw(
        "adamw_small", _pack_rows([w[n] for n in small], F32, 8), _pack_rows([g_small[n] for n in small], F32, 8),
        _pack_rows([m[n] for n in small], F32, 8), _pack_rows([v[n] for n in small], F32, 8))

    outs = {}
    for slot, (tag, small_pack) in enumerate((("g", None), ("d", d_small), ("m", m_small), ("v", v_small))):
        vals = {n: big_out[n][slot] for n in BIG}
        if small_pack is None:
            vals.update(g_small)
        else:
            vals.update(zip(small, _unpack_rows(small_pack, [w[n].shape for n in small])))
        outs[tag] = vals
    return (loss, dx.reshape(1, T, D), *[outs["g"][n] for n in WEIGHT_NAMES], *[outs["d"][n] for n in WEIGHT_NAMES],
            *[outs["m"][n] for n in WEIGHT_NAMES], *[outs["v"][n] for n in WEIGHT_NAMES])
```

```python
import functools

import jax
import jax.numpy as jnp
from jax import lax
from jax.experimental import pallas as pl
from jax.experimental.pallas import tpu as pltpu

F32 = jnp.float32
BF16 = jnp.bfloat16
SDS = jax.ShapeDtypeStruct
MESH = pl.DeviceIdType.MESH

HEAD_DIM = 64
N_Q_HEADS = 16
N_KV_HEADS = 2
Q_PER_KV = N_Q_HEADS // N_KV_HEADS
ATTN_BLOCK = 128
ROPE_THETA = 10000.0
CONV_WIDTH = 31
CONV_HALO = 32
CONV_FIRST_TAP = CONV_HALO - CONV_WIDTH + 1
CONV_ROW_CHUNK = 64
CONV_LANE_CHUNK = 256
CONV_GRAD_UNROLL = 4
RMS_EPS = 1e-5
LN_EPS = 1e-5
ADAM_LR = 0.001
ADAM_B1 = 0.9
ADAM_B2 = 0.999
ADAM_EPS = 1e-08
ADAM_WD = 0.01
ADAM_STEP = 10

V7X_LANES = 128
V7X_SUBLANES = 8
V7X_VMEM_LIMIT_BYTES = 56 * 1024 * 1024

N_CHIPS = 4
N_DEV = 8
PACK_W = 1024

MASK_VALUE = -1e30


def _params(*semantics):
    return pltpu.CompilerParams(dimension_semantics=semantics, vmem_limit_bytes=V7X_VMEM_LIMIT_BYTES)


def _rows(tm, width):
    return pl.BlockSpec((tm, width), lambda i: (i, 0))


def _whole(shape):
    return pl.BlockSpec(shape, lambda *_: (0,) * len(shape))


def _rms_rstd(h):
    return lax.rsqrt(jnp.mean(h * h, axis=-1, keepdims=True) + RMS_EPS)


def _silu_and_grad(z):
    sg = jax.nn.sigmoid(z)
    return z * sg, sg * (1.0 + z * (1.0 - sg))


def _swap_rope_halves(t):
    w = t.shape[1]
    half = HEAD_DIM // 2
    lane = lax.broadcasted_iota(jnp.int32, t.shape, 1)
    upper = pltpu.roll(t, w - half, 1)
    lower = pltpu.roll(t, half, 1)
    return jnp.where(lane % HEAD_DIM < half, upper, lower)


def _rope(t, cos_ref, sin_ref):
    reps = t.shape[1] // V7X_LANES
    c = jnp.tile(cos_ref[...], (1, reps))
    s = jnp.tile(sin_ref[...], (1, reps))
    return t * c + _swap_rope_halves(t) * s


def _rope_transposed(dt, cos_ref, sin_ref):
    reps = dt.shape[1] // V7X_LANES
    c = jnp.tile(cos_ref[...], (1, reps))
    s = jnp.tile(sin_ref[...], (1, reps))
    return dt * c + _swap_rope_halves(dt * s)


def _rope_tables(seq_len):
    pos = jnp.arange(seq_len, dtype=F32)
    inv_freq = ROPE_THETA ** (-jnp.arange(0, HEAD_DIM, 2, dtype=F32) / HEAD_DIM)
    ang = pos[:, None] * inv_freq[None, :]
    cos, sin = jnp.cos(ang), jnp.sin(ang)
    cos_t = jnp.concatenate([cos, cos, cos, cos], axis=1)
    sin_t = jnp.concatenate([-sin, sin, -sin, sin], axis=1)
    return cos_t, sin_t


def _qkv_proj(h, g, w, b, cos, sin, after):
    T, D = h.shape
    N = w.shape[1]
    tm = min(512, T)
    rope_w = N - N_KV_HEADS * HEAD_DIM

    def body(h_ref, g_ref, w_ref, b_ref, cos_ref, sin_ref, _, y_ref, o_ref):
        hh = h_ref[...]
        y = (hh * _rms_rstd(hh) * g_ref[...]).astype(BF16)
        y_ref[...] = y
        acc = jnp.dot(y, w_ref[...], preferred_element_type=F32) + b_ref[...]
        o_ref[:, :rope_w] = _rope(acc[:, :rope_w], cos_ref, sin_ref).astype(BF16)
        o_ref[:, rope_w:] = acc[:, rope_w:].astype(BF16)

    return pl.pallas_call(
        body, name="qkv_proj", grid=(T // tm,),
        in_specs=[_rows(tm, D), _whole((1, D)), _whole((D, N)), _whole((1, N)),
                  _rows(tm, V7X_LANES), _rows(tm, V7X_LANES), pl.BlockSpec(memory_space=pl.ANY)],
        out_specs=[_rows(tm, D), _rows(tm, N)],
        out_shape=[SDS((T, D), BF16), SDS((T, N), BF16)],
        compiler_params=_params("parallel"),
    )(h, g, w, b, cos, sin, after)


def _pw1_proj(h, g, w, b):
    T, D = h.shape
    n = w.shape[2]
    N = N_CHIPS * n
    tm = min(512, T)

    def body(h_ref, g_ref, w_ref, b_ref, y_ref, o_ref):
        hh = h_ref[...]
        y = (hh * _rms_rstd(hh) * g_ref[...]).astype(BF16)
        y_ref[...] = y
        for j in range(N_CHIPS):
            cols = slice(j * n, (j + 1) * n)
            o_ref[:, cols] = jnp.dot(y, w_ref[j], preferred_element_type=F32) + b_ref[:, cols]

    return pl.pallas_call(
        body, name="pw1_proj", grid=(T // tm,),
        in_specs=[_rows(tm, D), _whole((1, D)), _whole((N_CHIPS, D, n)), _whole((1, N))],
        out_specs=[_rows(tm, D), _rows(tm, N)],
        out_shape=[SDS((T, D), BF16), SDS((T, N), F32)],
        compiler_params=_params("parallel"),
    )(h, g, w, b)


def _band_mask(n):
    row = lax.broadcasted_iota(jnp.int32, (ATTN_BLOCK, 2 * ATTN_BLOCK), 0)
    col = lax.broadcasted_iota(jnp.int32, (ATTN_BLOCK, 2 * ATTN_BLOCK), 1)
    band = (col > row) & (col <= row + ATTN_BLOCK) & ((col >= ATTN_BLOCK) | (n > 0))
    return jnp.concatenate([band] * Q_PER_KV, axis=0)


def _group_operands(g, q_ref, kc_ref, kp_ref, vc_ref, vp_ref, sink_ref):
    lo = g * HEAD_DIM
    k = jnp.concatenate([kp_ref[:, lo:lo + HEAD_DIM], kc_ref[:, lo:lo + HEAD_DIM]], axis=0)
    v = jnp.concatenate([vp_ref[:, lo:lo + HEAD_DIM], vc_ref[:, lo:lo + HEAD_DIM]], axis=0)
    heads = range(g * Q_PER_KV, (g + 1) * Q_PER_KV)
    q = jnp.concatenate([q_ref[:, h * HEAD_DIM:(h + 1) * HEAD_DIM] for h in heads], axis=0)
    sink = jnp.concatenate(
        [jnp.broadcast_to(sink_ref[0:1, h:h + 1], (ATTN_BLOCK, 1)) for h in heads], axis=0)
    return q, k, v, sink


def _softmax_with_sink(q, k, sink, mask):
    s = lax.dot_general(q, k, (((1,), (1,)), ((), ())), preferred_element_type=F32) * (HEAD_DIM ** -0.5)
    s = jnp.where(mask, s, MASK_VALUE)
    m = jnp.maximum(jnp.max(s, axis=1, keepdims=True), sink)
    p = jnp.exp(s - m)
    e_sink = jnp.exp(sink - m)
    inv = 1.0 / (jnp.sum(p, axis=1, keepdims=True) + e_sink)
    return p * inv, e_sink * inv


def _attn_specs(T):
    nb = T // ATTN_BLOCK
    kcol = N_Q_HEADS * HEAD_DIM // V7X_LANES
    cur = lambda n: jnp.minimum(n, nb - 1)
    prev = lambda n: jnp.maximum(jnp.minimum(n, nb - 1) - 1, 0)
    q_spec = pl.BlockSpec((ATTN_BLOCK, N_Q_HEADS * HEAD_DIM), lambda n: (cur(n), 0))
    kc_spec = pl.BlockSpec((ATTN_BLOCK, V7X_LANES), lambda n: (cur(n), kcol))
    kp_spec = pl.BlockSpec((ATTN_BLOCK, V7X_LANES), lambda n: (prev(n), kcol))
    vc_spec = pl.BlockSpec((ATTN_BLOCK, V7X_LANES), lambda n: (cur(n), kcol + 1))
    vp_spec = pl.BlockSpec((ATTN_BLOCK, V7X_LANES), lambda n: (prev(n), kcol + 1))
    return q_spec, kc_spec, kp_spec, vc_spec, vp_spec


def _attn_fwd(qkv, sinks):
    T = qkv.shape[0]
    nb = T // ATTN_BLOCK
    qw = N_Q_HEADS * HEAD_DIM

    def body(q_ref, kc_ref, kp_ref, vc_ref, vp_ref, sink_ref, o_ref):
        mask = _band_mask(pl.program_id(0))
        for g in range(N_KV_HEADS):
            q, k, v, sink = _group_operands(g, q_ref, kc_ref, kp_ref, vc_ref, vp_ref, sink_ref)
            probs, _ = _softmax_with_sink(q, k, sink, mask)
            o = jnp.dot(probs.astype(BF16), v, preferred_element_type=F32)
            for i in range(Q_PER_KV):
                h = g * Q_PER_KV + i
                o_ref[:, h * HEAD_DIM:(h + 1) * HEAD_DIM] = o[i * ATTN_BLOCK:(i + 1) * ATTN_BLOCK].astype(BF16)

    return pl.pallas_call(
        body, name="attn_fwd", grid=(nb,),
        in_specs=[*_attn_specs(T), _whole((1, N_Q_HEADS))],
        out_specs=_rows(ATTN_BLOCK, qw),
        out_shape=SDS((T, qw), BF16),
        compiler_params=_params("parallel"),
    )(qkv, qkv, qkv, qkv, qkv, sinks)


def _mm_res(name, a, w, b, res):
    T, K = a.shape
    D = w.shape[1]
    tm = min(512, T)

    def body(a_ref, w_ref, b_ref, r_ref, o_ref):
        o_ref[...] = jnp.dot(a_ref[...], w_ref[...], preferred_element_type=F32) + b_ref[...] + r_ref[...]

    return pl.pallas_call(
        body, name=name, grid=(T // tm,),
        in_specs=[_rows(tm, K), _whole((K, D)), _whole((1, D)), _rows(tm, D)],
        out_specs=_rows(tm, D),
        out_shape=SDS((T, D), F32),
        compiler_params=_params("parallel"),
    )(a, w, b, res)


def _ffn_down(name, s, w2, layer, res):
    _, T, n = s.shape
    D = w2.shape[3]
    tm = min(512, T)

    def body(s_ref, w_ref, r_ref, o_ref):
        acc = r_ref[...]
        for j in range(N_CHIPS):
            acc = acc + jnp.dot(s_ref[j], w_ref[j], preferred_element_type=F32)
        o_ref[...] = acc

    return pl.pallas_call(
        body, name=name, grid=(T // tm,),
        in_specs=[pl.BlockSpec((N_CHIPS, tm, n), lambda i: (0, i, 0)),
                  pl.BlockSpec((N_CHIPS, None, n, D), lambda i: (0, layer, 0, 0)), _rows(tm, D)],
        out_specs=_rows(tm, D),
        out_shape=SDS((T, D), F32),
        compiler_params=_params("parallel"),
    )(s, w2, res)


def _ffn_up(name, h, g, w1, w3, layer):
    T, D = h.shape
    n = w1.shape[3]
    tm = min(512, T)

    def body(h_ref, g_ref, w1_ref, w3_ref, f_ref, g1_ref, g3_ref, s_ref):
        @pl.when(pl.program_id(1) == 0)
        def _():
            hh = h_ref[...]
            f_ref[...] = (hh * _rms_rstd(hh) * g_ref[...]).astype(BF16)

        f = f_ref[...]
        a = jnp.dot(f, w1_ref[...], preferred_element_type=F32)
        b = jnp.dot(f, w3_ref[...], preferred_element_type=F32)
        g1_ref[...] = a.astype(BF16)
        g3_ref[...] = b.astype(BF16)
        s_ref[...] = (a * jax.nn.sigmoid(a) * b).astype(BF16)

    slab = pl.BlockSpec((None, tm, n), lambda i, j: (j, i, 0))
    wslab = pl.BlockSpec((None, None, D, n), lambda i, j: (j, layer, 0, 0))
    hidden = SDS((N_CHIPS, T, n), BF16)
    return pl.pallas_call(
        body, name=name, grid=(T // tm, N_CHIPS),
        in_specs=[pl.BlockSpec((tm, D), lambda i, j: (i, 0)), pl.BlockSpec((1, D), lambda i, j: (0, 0)), wslab, wslab],
        out_specs=[pl.BlockSpec((tm, D), lambda i, j: (i, 0)), slab, slab, slab],
        out_shape=[SDS((T, D), BF16), hidden, hidden, hidden],
        compiler_params=_params("parallel", "arbitrary"),
    )(h, g, w1, w3)


def _glu(a, d):
    return a[:, :d] * jax.nn.sigmoid(a[:, d:])


def _conv_tile(T):
    return min(256, T)


def _fill_shifted(sh_ref, tc):
    n = tc + CONV_HALO - V7X_SUBLANES
    for r in range(1, V7X_SUBLANES):
        sh_ref[r, 0:n, :] = sh_ref[0, pl.ds(r, n), :]


def _depthwise_taps(sh_ref, w_ref, offsets, bias_ref, out_ref, tc):
    D = out_ref.shape[1]

    def chunk(i, carry):
        t0 = pl.multiple_of(i * CONV_ROW_CHUNK, CONV_ROW_CHUNK)
        for cb in range(D // CONV_LANE_CHUNK):
            cs = slice(cb * CONV_LANE_CHUNK, (cb + 1) * CONV_LANE_CHUNK)
            acc = jnp.zeros((CONV_ROW_CHUNK, CONV_LANE_CHUNK), F32)
            for r in range(V7X_SUBLANES):
                taps = [(j, o // V7X_SUBLANES) for j, o in enumerate(offsets) if o % V7X_SUBLANES == r]
                if not taps:
                    continue
                span = CONV_ROW_CHUNK + V7X_SUBLANES * max(q for _, q in taps)
                rows = sh_ref[r, pl.ds(t0, span), cs]
                for j, q in taps:
                    acc = acc + rows[V7X_SUBLANES * q:V7X_SUBLANES * q + CONV_ROW_CHUNK] * w_ref[j:j + 1, cs]
            if bias_ref is not None:
                acc = acc + bias_ref[:, cs]
            out_ref[pl.ds(t0, CONV_ROW_CHUNK), cs] = acc
        return carry

    lax.fori_loop(0, tc // CONV_ROW_CHUNK, chunk, 0)


def _depthwise_tap_grads(dy_sh, x_sh, offsets, dw_ref, tc):
    D = dw_ref.shape[1]
    for cb in range(D // V7X_LANES):
        cs = slice(cb * V7X_LANES, (cb + 1) * V7X_LANES)

        def row_tiles(i, accs, cs=cs):
            for k in range(CONV_GRAD_UNROLL):
                t0 = pl.multiple_of(i * (CONV_GRAD_UNROLL * V7X_SUBLANES), V7X_SUBLANES) + k * V7X_SUBLANES
                d = dy_sh[0, pl.ds(t0, V7X_SUBLANES), cs]
                accs = tuple(
                    acc + d * x_sh[o % V7X_SUBLANES, pl.ds(t0 + o // V7X_SUBLANES * V7X_SUBLANES, V7X_SUBLANES), cs]
                    for acc, o in zip(accs, offsets))
            return accs

        zero = jnp.zeros((V7X_SUBLANES, V7X_LANES), F32)
        accs = lax.fori_loop(0, tc // (CONV_GRAD_UNROLL * V7X_SUBLANES), row_tiles, tuple(zero for _ in offsets))
        for j, acc in enumerate(accs):
            dw_ref[j:j + 1, cs] += jnp.sum(acc, axis=0, keepdims=True)


def _conv_fwd(a, w_dw, b_dw, ln_g, ln_b):
    T = a.shape[0]
    D = a.shape[1] // 2
    tc = _conv_tile(T)
    per = tc // CONV_HALO

    def body(a_ref, ah_ref, w_ref, bdw_ref, lg_ref, lb_ref, c_ref, act_ref, u_sh):
        i = pl.program_id(0)
        u_sh[0, 0:CONV_HALO, :] = jnp.where(i > 0, _glu(ah_ref[...], D), 0.0)
        u_sh[0, CONV_HALO:, :] = _glu(a_ref[...], D)
        _fill_shifted(u_sh, tc)
        _depthwise_taps(u_sh, w_ref, [CONV_FIRST_TAP + j for j in range(CONV_WIDTH)], bdw_ref, c_ref, tc)
        c = c_ref[...]
        xc = c - jnp.mean(c, axis=-1, keepdims=True)
        z = xc * lax.rsqrt(jnp.mean(xc * xc, axis=-1, keepdims=True) + LN_EPS)
        l = z * lg_ref[...] + lb_ref[...]
        act_ref[...] = (l * jax.nn.sigmoid(l)).astype(BF16)

    return pl.pallas_call(
        body, name="conv_fwd", grid=(T // tc,),
        in_specs=[_rows(tc, 2 * D),
                  pl.BlockSpec((CONV_HALO, 2 * D), lambda i: (jnp.maximum(i * per - 1, 0), 0)),
                  _whole((CONV_WIDTH, D)), _whole((1, D)), _whole((1, D)), _whole((1, D))],
        out_specs=[_rows(tc, D), _rows(tc, D)],
        out_shape=[SDS((T, D), F32), SDS((T, D), BF16)],
        scratch_shapes=[pltpu.VMEM((V7X_SUBLANES, tc + CONV_HALO, D), F32)],
        compiler_params=_params("parallel"),
    )(a, a, w_dw, b_dw, ln_g, ln_b)


def _final_loss(h, g, target):
    T, D = h.shape
    tm = min(512, T)

    def body(h_ref, g_ref, t_ref, dh_ref, loss_ref, dg_ref):
        @pl.when(pl.program_id(0) == 0)
        def _():
            loss_ref[...] = jnp.zeros_like(loss_ref)
            dg_ref[...] = jnp.zeros_like(dg_ref)

        hh = h_ref[...]
        r = _rms_rstd(hh)
        g = g_ref[...]
        d = hh * r * g - t_ref[...]
        loss_ref[...] += 0.5 * jnp.sum(jnp.mean(d * d, axis=-1, keepdims=True), axis=0, keepdims=True)
        dout = d * (1.0 / D)
        dg_ref[...] += jnp.sum(dout * (hh * r), axis=0, keepdims=True)
        dxh = dout * g
        dh_ref[...] = r * dxh - hh * (r * r * r) * jnp.mean(dxh * hh, axis=-1, keepdims=True)

    return pl.pallas_call(
        body, name="final_loss", grid=(T // tm,),
        in_specs=[_rows(tm, D), _whole((1, D)), _rows(tm, D)],
        out_specs=[_rows(tm, D), _whole((1, 1)), _whole((1, D))],
        out_shape=[SDS((T, D), F32), SDS((1, 1), F32), SDS((1, D), F32)],
        compiler_params=_params("arbitrary"),
    )(h, g, target)


def _ffn_bwd_act(name, dh, w2, layer, g1, g3):
    T, D = dh.shape
    n = w2.shape[2]
    tm = min(512, T)

    def body(dh_ref, w2_ref, g1_ref, g3_ref, dg1_ref, dg3_ref):
        ds = lax.dot_general(dh_ref[...].astype(BF16), w2_ref[...], (((1,), (1,)), ((), ())),
                             preferred_element_type=F32)
        act, dact = _silu_and_grad(g1_ref[...].astype(F32))
        dg1_ref[...] = (ds * g3_ref[...].astype(F32) * dact).astype(BF16)
        dg3_ref[...] = (ds * act).astype(BF16)

    slab = pl.BlockSpec((None, tm, n), lambda i, j: (j, i, 0))
    hidden = SDS((N_CHIPS, T, n), BF16)
    return pl.pallas_call(
        body, name=name, grid=(T // tm, N_CHIPS),
        in_specs=[pl.BlockSpec((tm, D), lambda i, j: (i, 0)),
                  pl.BlockSpec((None, None, n, D), lambda i, j: (j, layer, 0, 0)), slab, slab],
        out_specs=[slab, slab],
        out_shape=[hidden, hidden],
        compiler_params=_params("parallel", "arbitrary"),
    )(dh, w2, g1, g3)


def _dot_tn(a, b):
    return lax.dot_general(a.astype(BF16), b.astype(BF16), (((0,), (0,)), ((), ())), preferred_element_type=F32)


def _dot_nt(a, b):
    return lax.dot_general(a.astype(BF16), b, (((1,), (1,)), ((), ())), preferred_element_type=F32)


def _mm_tn(name, a, b, col_chunks=1):
    a_slabs, b_slabs = a.ndim == 3, b.ndim == 3
    T = a.shape[-2]
    tt = min(512, T)
    ka, nb = a.shape[-1], b.shape[-1]
    if a_slabs or b_slabs:
        out_dims = (N_CHIPS, ka, nb)
    elif col_chunks > 1:
        out_dims = (col_chunks, ka, nb // col_chunks)
    else:
        out_dims = (ka, nb)

    def body(a_ref, b_ref, o_ref):
        @pl.when(pl.program_id(0) == 0)
        def _():
            o_ref[...] = jnp.zeros_like(o_ref)

        if a_slabs:
            bb = b_ref[...].astype(BF16)
            for j in range(N_CHIPS):
                o_ref[j] += _dot_tn(a_ref[j], bb)
        elif b_slabs:
            aa = a_ref[...].astype(BF16)
            for j in range(N_CHIPS):
                o_ref[j] += _dot_tn(aa, b_ref[j])
        elif col_chunks > 1:
            aa = a_ref[...].astype(BF16)
            w = nb // col_chunks
            for j in range(col_chunks):
                o_ref[j] += _dot_tn(aa, b_ref[:, j * w:(j + 1) * w])
        else:
            o_ref[...] += _dot_tn(a_ref[...], b_ref[...])

    def spec(arr, slabs):
        if slabs:
            return pl.BlockSpec((N_CHIPS, tt, arr.shape[-1]), lambda t: (0, t, 0))
        return _rows(tt, arr.shape[-1])

    return pl.pallas_call(
        body, name=name, grid=(T // tt,),
        in_specs=[spec(a, a_slabs), spec(b, b_slabs)],
        out_specs=_whole(out_dims),
        out_shape=SDS(out_dims, F32),
        compiler_params=_params("arbitrary"),
    )(a, b)


def _mm_nt_normbwd(name, pairs, h, g, dh, after):
    T, D = h.shape
    tm = min(256, T)
    n_pairs = len(pairs)
    kinds = ["slabs" if dy.ndim == 3 else ("quarters" if w.ndim == 3 else "plain") for dy, w, _ in pairs]

    def body(*refs):
        dy_refs = refs[:n_pairs]
        w_refs = refs[n_pairs:2 * n_pairs]
        h_ref, g_ref, dh_ref, _, o_ref, dg_ref, cs_ref = refs[2 * n_pairs:]

        @pl.when(pl.program_id(0) == 0)
        def _():
            dg_ref[...] = jnp.zeros_like(dg_ref)
            cs_ref[...] = jnp.zeros_like(cs_ref)

        df = jnp.zeros((tm, D), F32)
        for dy_ref, w_ref, kd in zip(dy_refs, w_refs, kinds):
            if kd == "slabs":
                for j in range(N_CHIPS):
                    df = df + _dot_nt(dy_ref[j], w_ref[j])
            elif kd == "quarters":
                n = w_ref.shape[2]
                for j in range(N_CHIPS):
                    df = df + _dot_nt(dy_ref[:, j * n:(j + 1) * n], w_ref[j])
            else:
                df = df + _dot_nt(dy_ref[...], w_ref[...])
        hh = h_ref[...]
        r = _rms_rstd(hh)
        dg_ref[...] += jnp.sum(df * (hh * r), axis=0, keepdims=True)
        dxh = df * g_ref[...]
        out = dh_ref[...] + (r * dxh - hh * (r * r * r) * jnp.mean(dxh * hh, axis=-1, keepdims=True))
        o_ref[...] = out
        cs_ref[...] += jnp.sum(out, axis=0, keepdims=True)

    dy_specs, w_specs = [], []
    for (dy, w, layer), kd in zip(pairs, kinds):
        if kd == "slabs":
            dy_specs.append(pl.BlockSpec((N_CHIPS, tm, dy.shape[2]), lambda i: (0, i, 0)))
            w_specs.append(pl.BlockSpec((N_CHIPS, None, D, w.shape[3]),
                                        functools.partial(lambda i, layer: (0, layer, 0, 0), layer=layer)))
        else:
            dy_specs.append(_rows(tm, dy.shape[1]))
            w_specs.append(_whole(w.shape))

    return pl.pallas_call(
        body, name=name, grid=(T // tm,),
        in_specs=[*dy_specs, *w_specs, _rows(tm, D), _whole((1, D)), _rows(tm, D), pl.BlockSpec(memory_space=pl.ANY)],
        out_specs=[_rows(tm, D), _whole((1, D)), _whole((1, D))],
        out_shape=[SDS((T, D), F32), SDS((1, D), F32), SDS((1, D), F32)],
        compiler_params=_params("arbitrary"),
    )(*[dy for dy, _, _ in pairs], *[w for _, w, _ in pairs], h, g, dh, after)


def _mm_nt(name, dy, w, out_dtype):
    T, N = dy.shape
    K = w.shape[0]
    tm = min(512, T)

    def body(dy_ref, w_ref, o_ref):
        o_ref[...] = lax.dot_general(dy_ref[...].astype(BF16), w_ref[...], (((1,), (1,)), ((), ())),
                                     preferred_element_type=F32).astype(out_dtype)

    return pl.pallas_call(
        body, name=name, grid=(T // tm,),
        in_specs=[_rows(tm, N), _whole((K, N))],
        out_specs=_rows(tm, K),
        out_shape=SDS((T, K), out_dtype),
        compiler_params=_params("parallel"),
    )(dy, w)


def _conv_bwd(dact, c, a, w_dw, ln_g, ln_b):
    T, D = c.shape
    tc = _conv_tile(T)
    per = tc // CONV_HALO
    n_tiles = T // tc
    last_halo = T // CONV_HALO - 1

    def ln_bwd(dact_v, c_v, lg, lb):
        xc = c_v - jnp.mean(c_v, axis=-1, keepdims=True)
        rstd = lax.rsqrt(jnp.mean(xc * xc, axis=-1, keepdims=True) + LN_EPS)
        z = xc * rstd
        _, dsilu = _silu_and_grad(z * lg + lb)
        dl = dact_v * dsilu
        dz = dl * lg
        dc = rstd * (dz - jnp.mean(dz, axis=-1, keepdims=True) - z * jnp.mean(dz * z, axis=-1, keepdims=True))
        return dc, dl, z

    def body(dact_ref, dactn_ref, c_ref, cn_ref, a_ref, ah_ref, w_ref, lg_ref, lb_ref,
             da_ref, dlg_ref, dlb_ref, dbdw_ref, dwdw_ref, dbpw1_ref, dc_sh, u_sh, du_scr):
        i = pl.program_id(0)

        @pl.when(i == 0)
        def _():
            for ref in (dlg_ref, dlb_ref, dbdw_ref, dwdw_ref, dbpw1_ref):
                ref[...] = jnp.zeros_like(ref)

        lg, lb = lg_ref[...], lb_ref[...]
        dc, dl, z = ln_bwd(dact_ref[...], c_ref[...], lg, lb)
        dlg_ref[...] += jnp.sum(dl * z, axis=0, keepdims=True)
        dlb_ref[...] += jnp.sum(dl, axis=0, keepdims=True)
        dbdw_ref[...] += jnp.sum(dc, axis=0, keepdims=True)
        dcn, _, _ = ln_bwd(dactn_ref[...], cn_ref[...], lg, lb)
        dc_sh[0, 0:tc, :] = dc
        dc_sh[0, tc:, :] = jnp.where(i < n_tiles - 1, dcn, 0.0)
        _fill_shifted(dc_sh, tc)

        a_v = a_ref[...]
        a1 = a_v[:, :D]
        sg = jax.nn.sigmoid(a_v[:, D:])
        u_sh[0, 0:CONV_HALO, :] = jnp.where(i > 0, _glu(ah_ref[...], D), 0.0)
        u_sh[0, CONV_HALO:, :] = a1 * sg
        _fill_shifted(u_sh, tc)

        _depthwise_taps(dc_sh, w_ref, [CONV_WIDTH - 1 - j for j in range(CONV_WIDTH)], None, du_scr, tc)
        _depthwise_tap_grads(dc_sh, u_sh, [CONV_FIRST_TAP + j for j in range(CONV_WIDTH)], dwdw_ref, tc)

        du = du_scr[...]
        da1 = du * sg
        da2 = du * a1 * sg * (1.0 - sg)
        da_ref[:, :D] = da1.astype(BF16)
        da_ref[:, D:] = da2.astype(BF16)
        dbpw1_ref[:, :D] += jnp.sum(da1, axis=0, keepdims=True)
        dbpw1_ref[:, D:] += jnp.sum(da2, axis=0, keepdims=True)

    nxt = lambda i: (jnp.minimum((i + 1) * per, last_halo), 0)
    return pl.pallas_call(
        body, name="conv_bwd", grid=(n_tiles,),
        in_specs=[_rows(tc, D), pl.BlockSpec((CONV_HALO, D), nxt),
                  _rows(tc, D), pl.BlockSpec((CONV_HALO, D), nxt),
                  _rows(tc, 2 * D),
                  pl.BlockSpec((CONV_HALO, 2 * D), lambda i: (jnp.maximum(i * per - 1, 0), 0)),
                  _whole((CONV_WIDTH, D)), _whole((1, D)), _whole((1, D))],
        out_specs=[_rows(tc, 2 * D), _whole((1, D)), _whole((1, D)), _whole((1, D)),
                   _whole((CONV_HALO, D)), _whole((1, 2 * D))],
        out_shape=[SDS((T, 2 * D), BF16), SDS((1, D), F32), SDS((1, D), F32), SDS((1, D), F32),
                   SDS((CONV_HALO, D), F32), SDS((1, 2 * D), F32)],
        scratch_shapes=[pltpu.VMEM((V7X_SUBLANES, tc + CONV_HALO, D), F32),
                        pltpu.VMEM((V7X_SUBLANES, tc + CONV_HALO, D), F32), pltpu.VMEM((tc, D), F32)],
        compiler_params=_params("arbitrary"),
    )(dact, dact, c, c, a, a, w_dw, ln_g, ln_b)


def _attn_bwd(qkv, dao, cos, sin, sinks):
    T = qkv.shape[0]
    nb = T // ATTN_BLOCK
    qw = N_Q_HEADS * HEAD_DIM
    kw = N_KV_HEADS * HEAD_DIM

    def body(q_ref, kc_ref, kp_ref, vc_ref, vp_ref, do_ref, cos_ref, sin_ref, cosp_ref, sinp_ref, sink_ref,
             dq_ref, dkv_ref, dsink_ref, dbq_ref, dbkv_ref, carry, prev_scr, cur_scr, dq_scr):
        n = pl.program_id(0)

        @pl.when(n == 0)
        def _():
            for ref in (dsink_ref, dbq_ref, dbkv_ref, carry):
                ref[...] = jnp.zeros_like(ref)

        @pl.when(n == nb)
        def _():
            prev_scr[...] = jnp.zeros_like(prev_scr)

        @pl.when(n < nb)
        def _():
            mask = _band_mask(n)
            for g in range(N_KV_HEADS):
                q, k, v, sink = _group_operands(g, q_ref, kc_ref, kp_ref, vc_ref, vp_ref, sink_ref)
                heads = range(g * Q_PER_KV, (g + 1) * Q_PER_KV)
                do = jnp.concatenate([do_ref[:, h * HEAD_DIM:(h + 1) * HEAD_DIM] for h in heads], axis=0)
                probs, p_sink = _softmax_with_sink(q, k, sink, mask)
                dp = lax.dot_general(do, v, (((1,), (1,)), ((), ())), preferred_element_type=F32)
                delta = jnp.sum(probs * dp, axis=1, keepdims=True)
                ds = (probs * (dp - delta) * (HEAD_DIM ** -0.5)).astype(BF16)
                dsink_rows = -(p_sink * delta)
                dq = jnp.dot(ds, k, preferred_element_type=F32)
                dk = lax.dot_general(ds, q, (((0,), (0,)), ((), ())), preferred_element_type=F32)
                dv = lax.dot_general(probs.astype(BF16), do, (((0,), (0,)), ((), ())), preferred_element_type=F32)
                for i, h in enumerate(heads):
                    rows = slice(i * ATTN_BLOCK, (i + 1) * ATTN_BLOCK)
                    dq_scr[:, h * HEAD_DIM:(h + 1) * HEAD_DIM] = dq[rows]
                    dsink_ref[:, h:h + 1] += jnp.sum(dsink_rows[rows], axis=0, keepdims=True)
                lo = g * HEAD_DIM
                prev_scr[:, lo:lo + HEAD_DIM] = dk[:ATTN_BLOCK]
                cur_scr[:, lo:lo + HEAD_DIM] = dk[ATTN_BLOCK:]
                prev_scr[:, kw + lo:kw + lo + HEAD_DIM] = dv[:ATTN_BLOCK]
                cur_scr[:, kw + lo:kw + lo + HEAD_DIM] = dv[ATTN_BLOCK:]
            dq_pre = _rope_transposed(dq_scr[...], cos_ref, sin_ref)
            dq_ref[...] = dq_pre.astype(BF16)
            dbq_ref[...] += jnp.sum(dq_pre, axis=0, keepdims=True)

        tot = carry[...] + prev_scr[...]
        dk_pre = _rope_transposed(tot[:, :kw], cosp_ref, sinp_ref)
        dkv_ref[:, :kw] = dk_pre.astype(BF16)
        dkv_ref[:, kw:] = tot[:, kw:].astype(BF16)
        dbkv_ref[:, :kw] += jnp.sum(dk_pre, axis=0, keepdims=True)
        dbkv_ref[:, kw:] += jnp.sum(tot[:, kw:], axis=0, keepdims=True)

        @pl.when(n < nb)
        def _():
            carry[...] = cur_scr[...]

    cur = lambda n: (jnp.minimum(n, nb - 1), 0)
    out_lag = lambda n: (jnp.maximum(n - 1, 0), 0)
    return pl.pallas_call(
        body, name="attn_bwd", grid=(nb + 1,),
        in_specs=[*_attn_specs(T),
                  pl.BlockSpec((ATTN_BLOCK, qw), cur),
                  pl.BlockSpec((ATTN_BLOCK, V7X_LANES), cur), pl.BlockSpec((ATTN_BLOCK, V7X_LANES), cur),
                  pl.BlockSpec((ATTN_BLOCK, V7X_LANES), out_lag), pl.BlockSpec((ATTN_BLOCK, V7X_LANES), out_lag),
                  _whole((1, N_Q_HEADS))],
        out_specs=[pl.BlockSpec((ATTN_BLOCK, qw), cur), pl.BlockSpec((ATTN_BLOCK, 2 * kw), out_lag),
                   _whole((1, N_Q_HEADS)), _whole((1, qw)), _whole((1, 2 * kw))],
        out_shape=[SDS((T, qw), BF16), SDS((T, 2 * kw), BF16),
                   SDS((1, N_Q_HEADS), F32), SDS((1, qw), F32), SDS((1, 2 * kw), F32)],
        scratch_shapes=[pltpu.VMEM((ATTN_BLOCK, 2 * kw), F32), pltpu.VMEM((ATTN_BLOCK, 2 * kw), F32),
                        pltpu.VMEM((ATTN_BLOCK, 2 * kw), F32), pltpu.VMEM((ATTN_BLOCK, qw), F32)],
        compiler_params=_params("arbitrary"),
    )(qkv, qkv, qkv, qkv, qkv, dao, cos, sin, cos, sin, sinks)


def _local_step(x, target, p, reduce_large):
    T, D = x.shape
    cos, sin = _rope_tables(T)
    qw = N_Q_HEADS * HEAD_DIM
    nm, nf = p["norm_mix"], p["norm_ffn"]

    y0, qkv = _qkv_proj(x, nm[0:1], p["attn_w_qkv"], p["attn_b_qkv"], cos, sin, p["gather_started"])
    ao = _attn_fwd(qkv, p["attn_sinks"])
    h1 = _mm_res("attn_out", ao, p["attn_w_o"], p["attn_b_o"], x)
    p = {**p, **p["other_weights"](h1)}
    w1, w3, w2 = p["ffn_w1"], p["ffn_w3"], p["ffn_w2"]
    f0, g1a, g3a, s0 = _ffn_up("ffn0_up", h1, nf[0:1], w1, w3, 0)
    h2 = _ffn_down("ffn0_down", s0, w2, 0, h1)
    y1, a = _pw1_proj(h2, nm[1:2], p["conv_w_pw1"], p["conv_b_pw1"])
    c, act = _conv_fwd(a, p["conv_w_dw"], p["conv_b_dw"], p["conv_ln_g"], p["conv_ln_b"])
    h3 = _mm_res("conv_out", act, p["conv_w_pw2"], p["conv_b_pw2"], h2)
    f1, g1b, g3b, s1 = _ffn_up("ffn1_up", h3, nf[1:2], w1, w3, 1)
    h4 = _ffn_down("ffn1_down", s1, w2, 1, h3)
    dh4, loss, d_norm_final = _final_loss(h4, p["norm_final"], target)

    g = {}
    dg1, dg3 = _ffn_bwd_act("ffn1_bwd_act", dh4, w2, 1, g1b, g3b)
    dw2_1 = _mm_tn("ffn1_dw2", s1, dh4)
    dw1_1 = _mm_tn("ffn1_dw1", f1, dg1)
    dw3_1 = _mm_tn("ffn1_dw3", f1, dg3)
    started = reduce_large("ffn1", {("ffn_w1", 1): dw1_1, ("ffn_w3", 1): dw3_1, ("ffn_w2", 1): dw2_1})
    dh3, dnf1, db_pw2 = _mm_nt_normbwd("ffn1_bwd_in", [(dg1, w1, 1), (dg3, w3, 1)], h3, nf[1:2], dh4, started)

    dw_pw2 = _mm_tn("conv_dw_pw2", act, dh3)
    dact = _mm_nt("conv_bwd_out", dh3, p["conv_w_pw2"], F32)
    da, d_ln_g, d_ln_b, d_b_dw, d_w_dw, d_b_pw1 = _conv_bwd(dact, c, a, p["conv_w_dw"], p["conv_ln_g"], p["conv_ln_b"])
    dw_pw1 = _mm_tn("conv_dw_pw1", y1, da, col_chunks=N_CHIPS)
    started = reduce_large("conv", {("conv_w_pw2", 0): dw_pw2.reshape(N_CHIPS, -1, D), ("conv_w_pw1", 0): dw_pw1})
    dh2, dnm1, _ = _mm_nt_normbwd("conv_bwd_in", [(da, p["conv_w_pw1"], None)], h2, nm[1:2], dh3, started)

    dg1, dg3 = _ffn_bwd_act("ffn0_bwd_act", dh2, w2, 0, g1a, g3a)
    dw2_0 = _mm_tn("ffn0_dw2", s0, dh2)
    dw1_0 = _mm_tn("ffn0_dw1", f0, dg1)
    dw3_0 = _mm_tn("ffn0_dw3", f0, dg3)
    started = reduce_large("ffn0", {("ffn_w1", 0): dw1_0, ("ffn_w3", 0): dw3_0, ("ffn_w2", 0): dw2_0})
    dh1, dnf0, db_o = _mm_nt_normbwd("ffn0_bwd_in", [(dg1, w1, 0), (dg3, w3, 0)], h1, nf[0:1], dh2, started)

    dw_o = _mm_tn("attn_dw_o", ao, dh1)
    dao = _mm_nt("attn_bwd_out", dh1, p["attn_w_o"], BF16)
    dq, dkv, d_sinks, dbq, dbkv = _attn_bwd(qkv, dao, cos, sin, p["attn_sinks"])
    dwq = _mm_tn("attn_dw_q", y0, dq)
    dwkv = _mm_tn("attn_dw_kv", y0, dkv)
    wqkv = p["attn_w_qkv"]
    n_qkv = wqkv.shape[1] // N_CHIPS
    dwqkv = jnp.moveaxis(jnp.concatenate([dwq, dwkv], axis=1).reshape(D, N_CHIPS, n_qkv), 1, 0)
    started = reduce_large("attn", {("attn_w_o", 0): dw_o.reshape(N_CHIPS, -1, D), ("attn_w_qkv", 0): dwqkv})
    dx, dnm0, _ = _mm_nt_normbwd("attn_bwd_in", [(dq, wqkv[:, :qw], None), (dkv, wqkv[:, qw:], None)], x, nm[0:1], dh1,
                                 started)

    g["norm_mix"] = jnp.concatenate([dnm0, dnm1], axis=0)
    g["norm_ffn"] = jnp.concatenate([dnf0, dnf1], axis=0)
    g["attn_b_qkv"] = jnp.concatenate([dbq, dbkv], axis=1)
    g["attn_sinks"] = d_sinks
    g["attn_b_o"] = db_o
    g["conv_b_pw1"] = d_b_pw1
    g["conv_w_dw"] = d_w_dw[:CONV_WIDTH]
    g["conv_b_dw"] = d_b_dw
    g["conv_ln_g"] = d_ln_g
    g["conv_ln_b"] = d_ln_b
    g["conv_b_pw2"] = db_pw2
    g["norm_final"] = d_norm_final
    return loss, dx, g


ANY = pl.BlockSpec(memory_space=pl.ANY)
VMEM_WHOLE = pl.BlockSpec(memory_space=pltpu.VMEM)


def _my_place():
    return lax.axis_index("x"), lax.axis_index("y"), lax.axis_index("c")


def _other_chips(x, y):
    places = [(1 - x, y), (x, 1 - y), (1 - x, 1 - y)]
    return [(bx, by, 2 * bx + by) for bx, by in places]


def _exchange_small(name, v, reduce):
    r, w = v.shape

    def body(v_ref, o_ref, *rest):
        if reduce:
            buf, send_sems, recv_sems = rest
        else:
            buf = o_ref
            send_sems, recv_sems = rest
        x, y, c = _my_place()
        me = 4 * x + 2 * y + c
        sends = []
        for k in range(1, N_DEV):
            peer = (1 - x if k & 4 else x, 1 - y if k & 2 else y, 1 - c if k & 1 else c)
            cp = pltpu.make_async_remote_copy(
                src_ref=v_ref, dst_ref=buf.at[me], send_sem=send_sems.at[k - 1], recv_sem=recv_sems.at[k - 1],
                device_id=peer, device_id_type=MESH)
            cp.start()
            sends.append(cp)
        buf[me] = v_ref[...]
        for k in range(1, N_DEV):
            src = 4 * (1 - x if k & 4 else x) + 2 * (1 - y if k & 2 else y) + (1 - c if k & 1 else c)
            pltpu.make_async_remote_copy(
                src_ref=v_ref, dst_ref=buf.at[src], send_sem=send_sems.at[k - 1], recv_sem=recv_sems.at[k - 1],
                device_id=(x, y, c), device_id_type=MESH).wait_recv()
        for cp in sends:
            cp.wait_send()
        if reduce:
            acc = buf[0]
            for d in range(1, N_DEV):
                acc = acc + buf[d]
            o_ref[...] = acc

    sems = [pltpu.SemaphoreType.DMA((N_DEV - 1,)), pltpu.SemaphoreType.DMA((N_DEV - 1,))]
    if reduce:
        out_shape = SDS((r, w), F32)
        scratch = [pltpu.VMEM((N_DEV, r, w), F32)] + sems
    else:
        out_shape = SDS((N_DEV, r, w), F32)
        scratch = sems
    return pl.pallas_call(
        body, name=name, out_shape=out_shape, in_specs=[VMEM_WHOLE], out_specs=VMEM_WHOLE,
        scratch_shapes=scratch,
        compiler_params=pltpu.CompilerParams(vmem_limit_bytes=V7X_VMEM_LIMIT_BYTES),
    )(v)


def _cast_into_slot(name, gathered, shard, chip_idx):
    rows, cols = shard.shape
    tr = _pack_row_tile(rows)

    def body(k_ref, s_ref, g_ref, o_ref):
        o_ref[...] = s_ref[...].astype(BF16)

    return pl.pallas_call(
        body, name=name,
        grid_spec=pltpu.PrefetchScalarGridSpec(
            num_scalar_prefetch=1, grid=(rows // tr,),
            in_specs=[pl.BlockSpec((tr, cols), lambda i, k_ref: (i, 0)), pl.BlockSpec(memory_space=pl.ANY)],
            out_specs=pl.BlockSpec((None, tr, cols), lambda i, k_ref: (k_ref[0], i, 0))),
        out_shape=SDS(gathered.shape, BF16),
        input_output_aliases={2: 0},
        compiler_params=_params("parallel"),
    )(chip_idx, shard, gathered)


def _row_halves(ref, c):
    half = ref.shape[1] // 2
    return pl.ds(pl.multiple_of(c * half, 16), half), pl.ds(pl.multiple_of((1 - c) * half, 16), half)


def _gather_ici_copies(refs, send_sems, recv_sems):
    x, y, c = _my_place()
    k = 2 * x + y
    pairs = []
    for i, ref in enumerate(refs):
        mine, _ = _row_halves(ref, c)
        for j, (bx, by, kb) in enumerate(_other_chips(x, y)):
            sems = dict(send_sem=send_sems.at[3 * i + j], recv_sem=recv_sems.at[3 * i + j], device_id_type=MESH)
            send = pltpu.make_async_remote_copy(src_ref=ref.at[k, mine], dst_ref=ref.at[k, mine],
                                                device_id=(bx, by, c), **sems)
            arrival = pltpu.make_async_remote_copy(src_ref=ref.at[kb, mine], dst_ref=ref.at[kb, mine],
                                                   device_id=(bx, by, c), **sems)
            pairs.append((send, arrival))
    return pairs


def _gather_d2d_copies(refs, send_sems, recv_sems, first_sem):
    x, y, c = _my_place()
    pairs = []
    for i, ref in enumerate(refs):
        mine, theirs = _row_halves(ref, c)
        for j, (_, _, kb) in enumerate(_other_chips(x, y)):
            sem = first_sem + 3 * i + j
            sems = dict(send_sem=send_sems.at[sem], recv_sem=recv_sems.at[sem], device_id=(x, y, 1 - c),
                        device_id_type=MESH)
            send = pltpu.make_async_remote_copy(src_ref=ref.at[kb, mine], dst_ref=ref.at[kb, mine], **sems)
            arrival = pltpu.make_async_remote_copy(src_ref=ref.at[kb, theirs], dst_ref=ref.at[kb, theirs], **sems)
            pairs.append((send, arrival))
    return pairs


def _run_copies(pairs):
    for send, _ in pairs:
        send.start()
    for send, arrival in pairs:
        send.wait_send()
        arrival.wait_recv()


def _gather_now(name, gathered):
    n_w = len(gathered)

    def body(*refs):
        in_refs = refs[:n_w]
        send_sems, recv_sems = refs[2 * n_w:]
        _run_copies(_gather_ici_copies(in_refs, send_sems, recv_sems))
        _run_copies(_gather_d2d_copies(in_refs, send_sems, recv_sems, 3 * n_w))

    return pl.pallas_call(
        body, name=name, out_shape=[SDS(g.shape, g.dtype) for g in gathered],
        in_specs=[ANY] * n_w, out_specs=[ANY] * n_w, input_output_aliases={i: i for i in range(n_w)},
        scratch_shapes=[pltpu.SemaphoreType.DMA((6 * n_w,)), pltpu.SemaphoreType.DMA((6 * n_w,))],
    )(*gathered)


def _gather_start(name, gathered):
    n_w = len(gathered)

    def body(*refs):
        in_refs = refs[:n_w]
        send_sems, recv_sems = refs[n_w:n_w + 2]
        for send, _ in _gather_ici_copies(in_refs, send_sems, recv_sems):
            send.start()
        refs[-1][...] = jnp.zeros_like(refs[-1])

    out = pl.pallas_call(
        body, name=name,
        out_shape=(pltpu.SemaphoreType.DMA((3 * n_w,)), pltpu.SemaphoreType.DMA((3 * n_w,)),
                   *[pltpu.HBM(g.shape, g.dtype) for g in gathered], SDS((8, V7X_LANES), F32)),
        in_specs=[HBM_SPEC] * n_w, out_specs=(SEM_SPEC, SEM_SPEC, *[HBM_SPEC] * n_w, VMEM_WHOLE),
        input_output_aliases={i: 2 + i for i in range(n_w)},
        compiler_params=pltpu.CompilerParams(has_side_effects=DATAFLOW),
    )(*[pltpu.with_memory_space_constraint(g, pltpu.HBM) for g in gathered])
    return out[0], out[1], list(out[2:2 + n_w]), out[-1]


def _gather_wait(name, send_sems, recv_sems, gathered, after):
    n_w = len(gathered)

    def body(*refs):
        in_refs = refs[:n_w]
        send_sems, recv_sems = refs[n_w:n_w + 2]
        for send, arrival in _gather_ici_copies(in_refs, send_sems, recv_sems):
            send.wait_send()
            arrival.wait_recv()

    out = pl.pallas_call(
        body, name=name, out_shape=tuple(pltpu.HBM(g.shape, g.dtype) for g in gathered),
        in_specs=[*[HBM_SPEC] * n_w, SEM_SPEC, SEM_SPEC, ANY], out_specs=tuple([HBM_SPEC] * n_w),
        input_output_aliases={i: i for i in range(n_w)},
        compiler_params=pltpu.CompilerParams(has_side_effects=DATAFLOW),
    )(*gathered, send_sems, recv_sems, after)
    return list(out)


def _swap_fetched_with_sibling(name, gathered):
    n_w = len(gathered)

    def body(*refs):
        in_refs = refs[:n_w]
        send_sems, recv_sems = refs[2 * n_w:]
        _run_copies(_gather_d2d_copies(in_refs, send_sems, recv_sems, 0))

    return pl.pallas_call(
        body, name=name, out_shape=[SDS(g.shape, g.dtype) for g in gathered],
        in_specs=[ANY] * n_w, out_specs=[ANY] * n_w, input_output_aliases={i: i for i in range(n_w)},
        scratch_shapes=[pltpu.SemaphoreType.DMA((3 * n_w,)), pltpu.SemaphoreType.DMA((3 * n_w,))],
    )(*gathered)


def _swap_halves_with_sibling(name, grads):
    n_g = len(grads)

    def body(*refs):
        g_refs, o_refs = refs[:n_g], refs[n_g:2 * n_g]
        send_sems, recv_sems = refs[2 * n_g:]
        x, y, c = _my_place()
        copies = []
        for i in range(n_g):
            half = g_refs[i].shape[1] // 2
            theirs = pl.ds(pl.multiple_of((1 - c) * half, 8), half)
            cp = pltpu.make_async_remote_copy(
                src_ref=g_refs[i].at[:, theirs], dst_ref=o_refs[i], send_sem=send_sems.at[i],
                recv_sem=recv_sems.at[i], device_id=(x, y, 1 - c), device_id_type=MESH)
            cp.start()
            copies.append(cp)
        for cp in copies:
            cp.wait()

    return pl.pallas_call(
        body, name=name,
        out_shape=[SDS((g.shape[0], g.shape[1] // 2, g.shape[2]), g.dtype) for g in grads],
        in_specs=[ANY] * n_g, out_specs=[ANY] * n_g,
        scratch_shapes=[pltpu.SemaphoreType.DMA((n_g,)), pltpu.SemaphoreType.DMA((n_g,))],
    )(*grads)


def _pack_row_tile(rows):
    for t in range(min(rows, 512), 7, -1):
        if rows % t == 0 and t % 8 == 0:
            return t
    return rows


def _add_sibling_half(name, grads, from_sibling, c_idx):
    n, R, w = grads.shape
    half = R // 2
    tr = _pack_row_tile(half)
    steps = half // tr

    def body(c_ref, g_ref, s_ref, o_ref):
        o_ref[...] = g_ref[...] + s_ref[...]

    return pl.pallas_call(
        body, name=name,
        grid_spec=pltpu.PrefetchScalarGridSpec(
            num_scalar_prefetch=1, grid=(n, steps),
            in_specs=[pl.BlockSpec((1, tr, w), lambda j, i, c_ref: (j, c_ref[0] * steps + i, 0)),
                      pl.BlockSpec((1, tr, w), lambda j, i, c_ref: (j, i, 0))],
            out_specs=pl.BlockSpec((1, tr, w), lambda j, i, c_ref: (j, i, 0))),
        out_shape=SDS((n, half, w), F32),
        compiler_params=_params("parallel", "parallel"),
    )(c_idx, grads, from_sibling)


HBM_SPEC = pl.BlockSpec(memory_space=pltpu.HBM)
SEM_SPEC = pl.BlockSpec(memory_space=pltpu.SEMAPHORE)
DATAFLOW = pltpu.SideEffectType.DATAFLOW_SIDE_EFFECTING


def _chip_scatter_copies(p_refs, land_refs, send_sems, recv_sems):
    x, y, c = _my_place()
    return [pltpu.make_async_remote_copy(
        src_ref=p_refs[i].at[kb], dst_ref=land_refs[i].at[j], send_sem=send_sems.at[3 * i + j],
        recv_sem=recv_sems.at[3 * i + j], device_id=(bx, by, c), device_id_type=MESH)
        for i in range(len(p_refs)) for j, (bx, by, kb) in enumerate(_other_chips(x, y))]


def _scatter_start(name, partials):
    n_p = len(partials)

    def body(*refs):
        p_refs, land_refs = refs[:n_p], refs[n_p:2 * n_p]
        send_sems, recv_sems = refs[2 * n_p:2 * n_p + 2]
        for cp in _chip_scatter_copies(p_refs, land_refs, send_sems, recv_sems):
            cp.start()
        refs[-1][...] = jnp.zeros_like(refs[-1])

    lands = [pltpu.with_memory_space_constraint(lax.empty((N_CHIPS - 1,) + p.shape[1:], p.dtype), pltpu.HBM)
             for p in partials]
    out = pl.pallas_call(
        body, name=name,
        out_shape=(pltpu.SemaphoreType.DMA((3 * n_p,)), pltpu.SemaphoreType.DMA((3 * n_p,)),
                   *[pltpu.HBM(p.shape, p.dtype) for p in partials], *[pltpu.HBM(l.shape, l.dtype) for l in lands],
                   SDS((8, V7X_LANES), F32)),
        in_specs=[HBM_SPEC] * (2 * n_p), out_specs=(SEM_SPEC, SEM_SPEC, *[HBM_SPEC] * (2 * n_p), VMEM_WHOLE),
        input_output_aliases={i: 2 + i for i in range(2 * n_p)},
        compiler_params=pltpu.CompilerParams(has_side_effects=DATAFLOW),
    )(*[pltpu.with_memory_space_constraint(p, pltpu.HBM) for p in partials], *lands)
    return out[0], out[1], list(out[2:2 + n_p]), list(out[2 + n_p:2 + 2 * n_p]), out[-1]


def _scatter_wait(name, send_sems, recv_sems, partials, lands, after):
    n_p = len(partials)

    def body(*refs):
        p_refs, land_refs = refs[:n_p], refs[n_p:2 * n_p]
        send_sems, recv_sems = refs[2 * n_p:2 * n_p + 2]
        for cp in _chip_scatter_copies(p_refs, land_refs, send_sems, recv_sems):
            cp.wait_send()
            cp.wait_recv()

    out = pl.pallas_call(
        body, name=name,
        out_shape=(*[pltpu.HBM(p.shape, p.dtype) for p in partials], *[pltpu.HBM(l.shape, l.dtype) for l in lands]),
        in_specs=[*[HBM_SPEC] * (2 * n_p), SEM_SPEC, SEM_SPEC, ANY], out_specs=tuple([HBM_SPEC] * (2 * n_p)),
        input_output_aliases={i: i for i in range(2 * n_p)},
        compiler_params=pltpu.CompilerParams(has_side_effects=DATAFLOW),
    )(*partials, *lands, send_sems, recv_sems, after)
    return list(out[:n_p]), list(out[n_p:])


def _sum_chip_partials(name, partial, received, shard, layer, place):
    n, half, w = partial.shape
    tr = _pack_row_tile(half)
    steps = half // tr

    def body(place_ref, p_ref, r_ref, shard_ref, o_ref):
        o_ref[...] = ((p_ref[0] + r_ref[0]) + r_ref[1]) + r_ref[2]

    return pl.pallas_call(
        body, name=name,
        grid_spec=pltpu.PrefetchScalarGridSpec(
            num_scalar_prefetch=1, grid=(steps,),
            in_specs=[pl.BlockSpec((1, tr, w), lambda i, place_ref: (place_ref[0], i, 0)),
                      pl.BlockSpec((n - 1, tr, w), lambda i, place_ref: (0, i, 0)),
                      pl.BlockSpec(memory_space=pl.ANY)],
            out_specs=pl.BlockSpec((tr, w), lambda i, place_ref: ((2 * layer + place_ref[1]) * steps + i, 0))),
        out_shape=SDS(shard.shape, F32),
        input_output_aliases={3: 0},
        compiler_params=_params("parallel"),
    )(place, partial, received, shard)


def _join_halves(shards, layers):
    n_s = len(shards)
    n_sem = sum(layers)

    def body(*refs):
        in_refs = refs[:n_s]
        send_sems, recv_sems = refs[2 * n_s:]
        x, y, c = _my_place()
        copies, sem = [], 0
        for ref, n_layers in zip(in_refs, layers):
            half = ref.shape[0] // (2 * n_layers)
            for layer in range(n_layers):
                mine = pl.ds(pl.multiple_of(layer * 2 * half + c * half, 8), half)
                theirs = pl.ds(pl.multiple_of(layer * 2 * half + (1 - c) * half, 8), half)
                send = pltpu.make_async_remote_copy(
                    src_ref=ref.at[mine], dst_ref=ref.at[mine], send_sem=send_sems.at[sem], recv_sem=recv_sems.at[sem],
                    device_id=(x, y, 1 - c), device_id_type=MESH)
                send.start()
                arrival = pltpu.make_async_remote_copy(
                    src_ref=ref.at[theirs], dst_ref=ref.at[theirs], send_sem=send_sems.at[sem],
                    recv_sem=recv_sems.at[sem], device_id=(x, y, 1 - c), device_id_type=MESH)
                copies.append((send, arrival))
                sem += 1
        for send, arrival in copies:
            send.wait_send()
            arrival.wait_recv()

    return pl.pallas_call(
        body, name="join_halves", out_shape=[SDS(s.shape, s.dtype) for s in shards],
        in_specs=[ANY] * n_s, out_specs=[ANY] * n_s,
        input_output_aliases={i: i for i in range(n_s)},
        scratch_shapes=[pltpu.SemaphoreType.DMA((n_sem,)), pltpu.SemaphoreType.DMA((n_sem,))],
    )(*shards)


def _adamw(name, w, g, m, v):
    rows, width = w.shape
    tr = _pack_row_tile(rows)

    def body(w_ref, g_ref, m_ref, v_ref, d_ref, nm_ref, nv_ref):
        gg = g_ref[...]
        m_new = ADAM_B1 * m_ref[...] + (1.0 - ADAM_B1) * gg
        v_new = ADAM_B2 * v_ref[...] + (1.0 - ADAM_B2) * (gg * gg)
        m_hat = m_new / (1.0 - ADAM_B1 ** ADAM_STEP)
        v_hat = v_new / (1.0 - ADAM_B2 ** ADAM_STEP)
        d_ref[...] = -ADAM_LR * (m_hat / (jnp.sqrt(v_hat) + ADAM_EPS) + ADAM_WD * w_ref[...])
        nm_ref[...] = m_new
        nv_ref[...] = v_new

    spec = _rows(tr, width)
    return pl.pallas_call(
        body, name=name, grid=(rows // tr,),
        in_specs=[spec] * 4, out_specs=[spec] * 3,
        out_shape=[SDS((rows, width), F32)] * 3,
        compiler_params=_params("parallel"),
    )(w, g, m, v)


WEIGHT_NAMES = ['norm_mix', 'norm_ffn', 'attn_w_qkv', 'attn_b_qkv', 'attn_sinks', 'attn_w_o', 'attn_b_o',
                'conv_w_pw1', 'conv_b_pw1', 'conv_w_dw', 'conv_b_dw', 'conv_ln_g', 'conv_ln_b', 'conv_w_pw2',
                'conv_b_pw2', 'ffn_w1', 'ffn_w3', 'ffn_w2', 'norm_final']
BIG = ['attn_w_qkv', 'attn_w_o', 'conv_w_pw1', 'conv_w_pw2', 'ffn_w1', 'ffn_w3', 'ffn_w2']
COLUMN_SPLIT = ('attn_w_qkv', 'conv_w_pw1', 'ffn_w1', 'ffn_w3')
SMALL_SPLIT = ['conv_b_pw1', 'conv_w_dw', 'conv_b_dw', 'conv_ln_g', 'conv_ln_b', 'conv_b_pw2']
SMALL_WHOLE = ['norm_mix', 'norm_ffn', 'attn_b_qkv', 'attn_sinks', 'attn_b_o', 'norm_final']


def _pack_rows(arrays, dtype, row_multiple):
    flat = jnp.concatenate([a.astype(dtype).reshape(-1) for a in arrays])
    rows = -(-flat.shape[0] // PACK_W)
    rows = -(-rows // row_multiple) * row_multiple
    return jnp.pad(flat, (0, rows * PACK_W - flat.shape[0])).reshape(rows, PACK_W)


def _unpack_rows(pack, shapes):
    flat = pack.reshape(-1)
    out, at = [], 0
    for shape in shapes:
        size = 1
        for s in shape:
            size *= s
        out.append(flat[at:at + size].reshape(shape))
        at += size
    return out


def _join_chip_axis(name, parts):
    axis = parts.ndim - 1 if name in COLUMN_SPLIT or name in SMALL_SPLIT else parts.ndim - 2
    moved = jnp.moveaxis(parts, 0, axis - 1)
    shape = list(moved.shape)
    shape[axis - 1:axis + 1] = [shape[axis - 1] * shape[axis]]
    return moved.reshape(shape)


def _split_chip_axis(name, whole, shard_shape):
    axis = len(shard_shape) - 1 if name in COLUMN_SPLIT or name in SMALL_SPLIT else len(shard_shape) - 2
    shape = list(whole.shape)
    shape[axis:axis + 1] = [N_CHIPS, shard_shape[axis]]
    return jnp.moveaxis(whole.reshape(shape), axis, 0)


def kernel(x, norm_mix, norm_ffn, attn_w_qkv, attn_b_qkv, attn_sinks, attn_w_o, attn_b_o, conv_w_pw1, conv_b_pw1, conv_w_dw, conv_b_dw, conv_ln_g, conv_ln_b, conv_w_pw2, conv_b_pw2, ffn_w1, ffn_w3, ffn_w2, norm_final, loss_target, m_norm_mix, m_norm_ffn, m_attn_w_qkv, m_attn_b_qkv, m_attn_sinks, m_attn_w_o, m_attn_b_o, m_conv_w_pw1, m_conv_b_pw1, m_conv_w_dw, m_conv_b_dw, m_conv_ln_g, m_conv_ln_b, m_conv_w_pw2, m_conv_b_pw2, m_ffn_w1, m_ffn_w3, m_ffn_w2, m_norm_final, v_norm_mix, v_norm_ffn, v_attn_w_qkv, v_attn_b_qkv, v_attn_sinks, v_attn_w_o, v_attn_b_o, v_conv_w_pw1, v_conv_b_pw1, v_conv_w_dw, v_conv_b_dw, v_conv_ln_g, v_conv_ln_b, v_conv_w_pw2, v_conv_b_pw2, v_ffn_w1, v_ffn_w3, v_ffn_w2, v_norm_final):
    w = dict(zip(WEIGHT_NAMES, (norm_mix, norm_ffn, attn_w_qkv, attn_b_qkv, attn_sinks, attn_w_o, attn_b_o,
                                conv_w_pw1, conv_b_pw1, conv_w_dw, conv_b_dw, conv_ln_g, conv_ln_b, conv_w_pw2,
                                conv_b_pw2, ffn_w1, ffn_w3, ffn_w2, norm_final)))
    m = dict(zip(WEIGHT_NAMES, (m_norm_mix, m_norm_ffn, m_attn_w_qkv, m_attn_b_qkv, m_attn_sinks, m_attn_w_o,
                                m_attn_b_o, m_conv_w_pw1, m_conv_b_pw1, m_conv_w_dw, m_conv_b_dw, m_conv_ln_g,
                                m_conv_ln_b, m_conv_w_pw2, m_conv_b_pw2, m_ffn_w1, m_ffn_w3, m_ffn_w2, m_norm_final)))
    v = dict(zip(WEIGHT_NAMES, (v_norm_mix, v_norm_ffn, v_attn_w_qkv, v_attn_b_qkv, v_attn_sinks, v_attn_w_o,
                                v_attn_b_o, v_conv_w_pw1, v_conv_b_pw1, v_conv_w_dw, v_conv_b_dw, v_conv_ln_g,
                                v_conv_ln_b, v_conv_w_pw2, v_conv_b_pw2, v_ffn_w1, v_ffn_w3, v_ffn_w2, v_norm_final)))
    T, D = x.shape[1], x.shape[2]
    c_idx = lax.axis_index("c").astype(jnp.int32).reshape(1)
    chip = (2 * lax.axis_index("x") + lax.axis_index("y")).astype(jnp.int32)

    as_rows = lambda a: a.reshape(-1, a.shape[-1])
    slabs = {n: _cast_into_slot(f"cast_{n}", lax.empty((N_CHIPS,) + as_rows(w[n]).shape, BF16), as_rows(w[n]),
                                chip.reshape(1)) for n in BIG}
    first, later = BIG[:2], BIG[2:]
    qkv_parts, w_o_parts = _gather_now("gather_attn", [slabs[n] for n in first])
    send_sems, recv_sems, travelling, gather_started = _gather_start("gather_start", [slabs[n] for n in later])
    layers = ffn_w1.shape[0]
    small_shapes = [w[n].shape for n in SMALL_SPLIT]
    small_all = _exchange_small("gather_small", _pack_rows([w[n] for n in SMALL_SPLIT], F32, 8), reduce=False)
    per_chip = [_unpack_rows(small_all[2 * j], small_shapes) for j in range(N_CHIPS)]
    full = {}
    for i, n in enumerate(SMALL_SPLIT):
        full[n] = _join_chip_axis(n, jnp.stack([per_chip[j][i] for j in range(N_CHIPS)]))

    def other_weights(after):
        landed = _gather_wait("gather_wait", send_sems, recv_sems, travelling, after)
        gathered = dict(zip(later, _swap_fetched_with_sibling("gather_swap", landed)))
        return {"conv_w_pw1": gathered["conv_w_pw1"], "conv_w_pw2": gathered["conv_w_pw2"].reshape(-1, D),
                "ffn_w1": gathered["ffn_w1"].reshape(N_CHIPS, layers, D, -1),
                "ffn_w3": gathered["ffn_w3"].reshape(N_CHIPS, layers, D, -1),
                "ffn_w2": gathered["ffn_w2"].reshape(N_CHIPS, layers, -1, D)}

    p = {
        "norm_mix": norm_mix, "norm_ffn": norm_ffn, "norm_final": norm_final.reshape(1, D),
        "attn_w_qkv": jnp.moveaxis(qkv_parts, 0, 1).reshape(D, -1), "attn_b_qkv": attn_b_qkv,
        "attn_sinks": attn_sinks, "attn_w_o": w_o_parts.reshape(-1, D), "attn_b_o": attn_b_o,
        "conv_b_pw1": full["conv_b_pw1"], "conv_w_dw": full["conv_w_dw"][0],
        "conv_b_dw": full["conv_b_dw"], "conv_ln_g": full["conv_ln_g"], "conv_ln_b": full["conv_ln_b"],
        "conv_b_pw2": full["conv_b_pw2"], "gather_started": gather_started, "other_weights": other_weights,
    }
    in_flight = []

    def reduce_large(tag, grads):
        keys = list(grads)
        from_sibling = _swap_halves_with_sibling(f"grads_to_sibling_{tag}", [grads[k] for k in keys])
        partials = [_add_sibling_half(f"add_sibling_half_{tag}{i}", grads[k], fs, c_idx)
                    for i, (k, fs) in enumerate(zip(keys, from_sibling))]
        *handles, started = _scatter_start(f"scatter_start_{tag}", partials)
        in_flight.append((tag, keys, handles))
        return started

    loss_part, dx, g = _local_step(x[0], loss_target[0], p, reduce_large)
    for n in SMALL_WHOLE + SMALL_SPLIT:
        g[n] = g[n].reshape((-1,) + g[n].shape[-2:]) if w[n].ndim == 3 else g[n].reshape(w[n].shape[:-1] + (-1,))

    place = jnp.stack([chip, c_idx[0]])
    shard_grad = {n: lax.empty(as_rows(w[n]).shape, F32) for n in BIG}
    for tag, keys, (send_sems, recv_sems, partials, lands) in in_flight:
        partials, received = _scatter_wait(f"scatter_wait_{tag}", send_sems, recv_sems, partials, lands, dx)
        for i, (n, layer) in enumerate(keys):
            shard_grad[n] = _sum_chip_partials(f"sum_chip_partials_{tag}{i}", partials[i], received[i],
                                               shard_grad[n], layer, place)
    g_big = dict(zip(BIG, _join_halves([shard_grad[n] for n in BIG], [w[n].shape[0] for n in BIG])))
    big_out = {}
    for n in BIG:
        d_n, m_n, v_n = _adamw(f"adamw_{n}", as_rows(w[n]), g_big[n], as_rows(m[n]), as_rows(v[n]))
        big_out[n] = [a.reshape(w[n].shape) for a in (g_big[n], d_n, m_n, v_n)]

    small_whole_shapes = [w[n].shape for n in SMALL_WHOLE]
    small_full_shapes = [g[n].shape for n in SMALL_SPLIT]
    reduced = _exchange_small(
        "reduce_small", _pack_rows([loss_part] + [g[n] for n in SMALL_WHOLE] + [g[n] for n in SMALL_SPLIT], F32, 8),
        reduce=True)
    pieces = _unpack_rows(reduced, [(1,)] + small_whole_shapes + small_full_shapes)
    loss = pieces[0].reshape(())
    g_small = dict(zip(SMALL_WHOLE, pieces[1:1 + len(SMALL_WHOLE)]))
    for n, whole in zip(SMALL_SPLIT, pieces[1 + len(SMALL_WHOLE):]):
        parts = _split_chip_axis(n, whole, w[n].shape)
        g_small[n] = lax.dynamic_index_in_dim(parts, chip, axis=0, keepdims=False)
    small = SMALL_WHOLE + SMALL_SPLIT
    d_small, m_small, v_small = _adamw(
        "adamw_small", _pack_rows([w[n] for n in small], F32, 8), _pack_rows([g_small[n] for n in small], F32, 8),
        _pack_rows([m[n] for n in small], F32, 8), _pack_rows([v[n] for n in small], F32, 8))

    outs = {}
    for slot, (tag, small_pack) in enumerate((("g", None), ("d", d_small), ("m", m_small), ("v", v_small))):
        vals = {n: big_out[n][slot] for n in BIG}
        if small_pack is None:
            vals.update(g_small)
        else:
            vals.update(zip(small, _unpack_rows(small_pack, [w[n].shape for n in small])))
        outs[tag] = vals
    return (loss, dx.reshape(1, T, D), *[outs["g"][n] for n in WEIGHT_NAMES], *[outs["d"][n] for n in WEIGHT_NAMES],
            *[outs["m"][n] for n in WEIGHT_NAMES], *[outs["v"][n] for n in WEIGHT_NAMES])
```

```python
import functools

import jax
import jax.numpy as jnp
from jax import lax
from jax.experimental import pallas as pl
from jax.experimental.pallas import tpu as pltpu

F32 = jnp.float32
BF16 = jnp.bfloat16
SDS = jax.ShapeDtypeStruct
MESH = pl.DeviceIdType.MESH

HEAD_DIM = 64
N_Q_HEADS = 16
N_KV_HEADS = 2
Q_PER_KV = N_Q_HEADS // N_KV_HEADS
ATTN_BLOCK = 128
ROPE_THETA = 10000.0
CONV_WIDTH = 31
CONV_HALO = 32
CONV_FIRST_TAP = CONV_HALO - CONV_WIDTH + 1
CONV_ROW_CHUNK = 64
CONV_LANE_CHUNK = 256
CONV_GRAD_UNROLL = 4
RMS_EPS = 1e-5
LN_EPS = 1e-5
ADAM_LR = 0.001
ADAM_B1 = 0.9
ADAM_B2 = 0.999
ADAM_EPS = 1e-08
ADAM_WD = 0.01
ADAM_STEP = 10

V7X_LANES = 128
V7X_SUBLANES = 8
V7X_VMEM_LIMIT_BYTES = 56 * 1024 * 1024

N_CHIPS = 4
N_DEV = 8
PACK_W = 1024

MASK_VALUE = -1e30


def _params(*semantics):
    return pltpu.CompilerParams(dimension_semantics=semantics, vmem_limit_bytes=V7X_VMEM_LIMIT_BYTES)


def _rows(tm, width):
    return pl.BlockSpec((tm, width), lambda i: (i, 0))


def _whole(shape):
    return pl.BlockSpec(shape, lambda *_: (0,) * len(shape))


def _rms_rstd(h):
    return lax.rsqrt(jnp.mean(h * h, axis=-1, keepdims=True) + RMS_EPS)


def _silu_and_grad(z):
    sg = jax.nn.sigmoid(z)
    return z * sg, sg * (1.0 + z * (1.0 - sg))


def _swap_rope_halves(t):
    w = t.shape[1]
    half = HEAD_DIM // 2
    lane = lax.broadcasted_iota(jnp.int32, t.shape, 1)
    upper = pltpu.roll(t, w - half, 1)
    lower = pltpu.roll(t, half, 1)
    return jnp.where(lane % HEAD_DIM < half, upper, lower)


def _rope(t, cos_ref, sin_ref):
    reps = t.shape[1] // V7X_LANES
    c = jnp.tile(cos_ref[...], (1, reps))
    s = jnp.tile(sin_ref[...], (1, reps))
    return t * c + _swap_rope_halves(t) * s


def _rope_transposed(dt, cos_ref, sin_ref):
    reps = dt.shape[1] // V7X_LANES
    c = jnp.tile(cos_ref[...], (1, reps))
    s = jnp.tile(sin_ref[...], (1, reps))
    return dt * c + _swap_rope_halves(dt * s)


def _rope_tables(seq_len):
    pos = jnp.arange(seq_len, dtype=F32)
    inv_freq = ROPE_THETA ** (-jnp.arange(0, HEAD_DIM, 2, dtype=F32) / HEAD_DIM)
    ang = pos[:, None] * inv_freq[None, :]
    cos, sin = jnp.cos(ang), jnp.sin(ang)
    cos_t = jnp.concatenate([cos, cos, cos, cos], axis=1)
    sin_t = jnp.concatenate([-sin, sin, -sin, sin], axis=1)
    return cos_t, sin_t


def _qkv_proj(h, g, w, b, cos, sin, after):
    T, D = h.shape
    N = w.shape[1]
    tm = min(512, T)
    rope_w = N - N_KV_HEADS * HEAD_DIM

    def body(h_ref, g_ref, w_ref, b_ref, cos_ref, sin_ref, _, y_ref, o_ref):
        hh = h_ref[...]
        y = (hh * _rms_rstd(hh) * g_ref[...]).astype(BF16)
        y_ref[...] = y
        acc = jnp.dot(y, w_ref[...], preferred_element_type=F32) + b_ref[...]
        o_ref[:, :rope_w] = _rope(acc[:, :rope_w], cos_ref, sin_ref).astype(BF16)
        o_ref[:, rope_w:] = acc[:, rope_w:].astype(BF16)

    return pl.pallas_call(
        body, name="qkv_proj", grid=(T // tm,),
        in_specs=[_rows(tm, D), _whole((1, D)), _whole((D, N)), _whole((1, N)),
                  _rows(tm, V7X_LANES), _rows(tm, V7X_LANES), pl.BlockSpec(memory_space=pl.ANY)],
        out_specs=[_rows(tm, D), _rows(tm, N)],
        out_shape=[SDS((T, D), BF16), SDS((T, N), BF16)],
        compiler_params=_params("parallel"),
    )(h, g, w, b, cos, sin, after)


def _pw1_proj(h, g, w, b):
    T, D = h.shape
    n = w.shape[2]
    N = N_CHIPS * n
    tm = min(512, T)

    def body(h_ref, g_ref, w_ref, b_ref, y_ref, o_ref):
        hh = h_ref[...]
        y = (hh * _rms_rstd(hh) * g_ref[...]).astype(BF16)
        y_ref[...] = y
        for j in range(N_CHIPS):
            cols = slice(j * n, (j + 1) * n)
            o_ref[:, cols] = jnp.dot(y, w_ref[j], preferred_element_type=F32) + b_ref[:, cols]

    return pl.pallas_call(
        body, name="pw1_proj", grid=(T // tm,),
        in_specs=[_rows(tm, D), _whole((1, D)), _whole((N_CHIPS, D, n)), _whole((1, N))],
        out_specs=[_rows(tm, D), _rows(tm, N)],
        out_shape=[SDS((T, D), BF16), SDS((T, N), F32)],
        compiler_params=_params("parallel"),
    )(h, g, w, b)


PAIRS_PER_KV = Q_PER_KV // 2


def _upper_lanes(shape):
    return lax.broadcasted_iota(jnp.int32, shape, 1) >= HEAD_DIM


def _swap_lane_halves(t):
    return pltpu.roll(t.astype(F32), HEAD_DIM, 1).astype(t.dtype)


def _kv_operands(g, t):
    swapped = _swap_lane_halves(t)
    in_lower, in_upper = (t, swapped) if g == 0 else (swapped, t)
    upper = _upper_lanes(t.shape)
    zero = jnp.zeros_like(t)
    return jnp.where(upper, zero, in_lower), jnp.where(upper, in_upper, zero)


def _group_heads(g):
    pairs = range(g * PAIRS_PER_KV, (g + 1) * PAIRS_PER_KV)
    return [2 * hp for hp in pairs] + [2 * hp + 1 for hp in pairs]


def _pair_rows(ref, g):
    pairs = range(g * PAIRS_PER_KV, (g + 1) * PAIRS_PER_KV)
    return jnp.concatenate([ref[:, hp * 2 * HEAD_DIM:(hp + 1) * 2 * HEAD_DIM] for hp in pairs], axis=0)


def _from_previous_block(rows):
    row = lax.broadcasted_iota(jnp.int32, (ATTN_BLOCK, ATTN_BLOCK), 0)
    col = lax.broadcasted_iota(jnp.int32, (ATTN_BLOCK, ATTN_BLOCK), 1)
    return jnp.concatenate([col > row] * (rows // ATTN_BLOCK), axis=0)


def _folded_probs(n, q_groups, k_prev_groups, k_cur_groups, sink_ref, prev_part):
    def scores(q, k):
        return lax.dot_general(q, k, (((1,), (1,)), ((), ())), preferred_element_type=F32)

    s_prev = jnp.concatenate([scores(q, k[i]) for q, k in zip(q_groups, k_prev_groups) for i in range(2)], axis=0)
    s_cur = jnp.concatenate([scores(q, k[i]) for q, k in zip(q_groups, k_cur_groups) for i in range(2)], axis=0)
    s_prev = jnp.where(n > 0, s_prev, MASK_VALUE * (HEAD_DIM ** 0.5))
    s = jnp.where(prev_part, s_prev, s_cur) * (HEAD_DIM ** -0.5)
    heads = [h for g in range(N_KV_HEADS) for h in _group_heads(g)]
    sink = jnp.concatenate([jnp.broadcast_to(sink_ref[0:1, h:h + 1], (ATTN_BLOCK, 1)) for h in heads], axis=0)
    m = jnp.maximum(jnp.max(s, axis=1, keepdims=True), sink)
    p = jnp.exp(s - m)
    e_sink = jnp.exp(sink - m)
    inv = 1.0 / (jnp.sum(p, axis=1, keepdims=True) + e_sink)
    return p * inv, e_sink * inv


def _split_folded(t, prev_part):
    tb = t.astype(BF16)
    zero = jnp.zeros_like(tb)
    return jnp.where(prev_part, tb, zero), jnp.where(prev_part, zero, tb)


def _attn_specs(T):
    nb = T // ATTN_BLOCK
    kcol = N_Q_HEADS * HEAD_DIM // V7X_LANES
    cur = lambda n: jnp.minimum(n, nb - 1)
    prev = lambda n: jnp.maximum(jnp.minimum(n, nb - 1) - 1, 0)
    q_spec = pl.BlockSpec((ATTN_BLOCK, N_Q_HEADS * HEAD_DIM), lambda n: (cur(n), 0))
    kc_spec = pl.BlockSpec((ATTN_BLOCK, V7X_LANES), lambda n: (cur(n), kcol))
    kp_spec = pl.BlockSpec((ATTN_BLOCK, V7X_LANES), lambda n: (prev(n), kcol))
    vc_spec = pl.BlockSpec((ATTN_BLOCK, V7X_LANES), lambda n: (cur(n), kcol + 1))
    vp_spec = pl.BlockSpec((ATTN_BLOCK, V7X_LANES), lambda n: (prev(n), kcol + 1))
    return q_spec, kc_spec, kp_spec, vc_spec, vp_spec


def _attn_fwd(qkv, sinks):
    T = qkv.shape[0]
    nb = T // ATTN_BLOCK
    qw = N_Q_HEADS * HEAD_DIM

    def body(q_ref, kc_ref, kp_ref, vc_ref, vp_ref, sink_ref, o_ref):
        n = pl.program_id(0)
        prev_part = _from_previous_block(N_Q_HEADS * ATTN_BLOCK)
        half = PAIRS_PER_KV * ATTN_BLOCK
        groups = range(N_KV_HEADS)
        probs, _ = _folded_probs(n, [_pair_rows(q_ref, g) for g in groups],
                                 [_kv_operands(g, kp_ref[...]) for g in groups],
                                 [_kv_operands(g, kc_ref[...]) for g in groups], sink_ref, prev_part)
        p_prev, p_cur = _split_folded(probs, prev_part)
        for g in groups:
            v_prev, v_cur = _kv_operands(g, vp_ref[...]), _kv_operands(g, vc_ref[...])
            even, odd = slice(2 * g * half, (2 * g + 1) * half), slice((2 * g + 1) * half, (2 * g + 2) * half)
            o = (jnp.dot(p_prev[even], v_prev[0], preferred_element_type=F32)
                 + jnp.dot(p_cur[even], v_cur[0], preferred_element_type=F32)
                 + jnp.dot(p_prev[odd], v_prev[1], preferred_element_type=F32)
                 + jnp.dot(p_cur[odd], v_cur[1], preferred_element_type=F32))
            for i in range(PAIRS_PER_KV):
                hp = g * PAIRS_PER_KV + i
                o_ref[:, hp * 2 * HEAD_DIM:(hp + 1) * 2 * HEAD_DIM] = (
                    o[i * ATTN_BLOCK:(i + 1) * ATTN_BLOCK].astype(BF16))

    return pl.pallas_call(
        body, name="attn_fwd", grid=(nb,),
        in_specs=[*_attn_specs(T), _whole((1, N_Q_HEADS))],
        out_specs=_rows(ATTN_BLOCK, qw),
        out_shape=SDS((T, qw), BF16),
        compiler_params=_params("parallel"),
    )(qkv, qkv, qkv, qkv, qkv, sinks)


def _mm_res(name, a, w, b, res):
    T, K = a.shape
    D = w.shape[1]
    tm = min(512, T)

    def body(a_ref, w_ref, b_ref, r_ref, o_ref):
        o_ref[...] = jnp.dot(a_ref[...], w_ref[...], preferred_element_type=F32) + b_ref[...] + r_ref[...]

    return pl.pallas_call(
        body, name=name, grid=(T // tm,),
        in_specs=[_rows(tm, K), _whole((K, D)), _whole((1, D)), _rows(tm, D)],
        out_specs=_rows(tm, D),
        out_shape=SDS((T, D), F32),
        compiler_params=_params("parallel"),
    )(a, w, b, res)


def _ffn_down(name, s, w2, layer, res):
    _, T, n = s.shape
    D = w2.shape[3]
    tm = min(512, T)

    def body(s_ref, w_ref, r_ref, o_ref):
        acc = r_ref[...]
        for j in range(N_CHIPS):
            acc = acc + jnp.dot(s_ref[j], w_ref[j], preferred_element_type=F32)
        o_ref[...] = acc

    return pl.pallas_call(
        body, name=name, grid=(T // tm,),
        in_specs=[pl.BlockSpec((N_CHIPS, tm, n), lambda i: (0, i, 0)),
                  pl.BlockSpec((N_CHIPS, None, n, D), lambda i: (0, layer, 0, 0)), _rows(tm, D)],
        out_specs=_rows(tm, D),
        out_shape=SDS((T, D), F32),
        compiler_params=_params("parallel"),
    )(s, w2, res)


def _ffn_up(name, h, g, w1, w3, layer):
    T, D = h.shape
    n = w1.shape[3]
    tm = min(512, T)

    def body(h_ref, g_ref, w1_ref, w3_ref, f_ref, g1_ref, g3_ref, s_ref):
        @pl.when(pl.program_id(1) == 0)
        def _():
            hh = h_ref[...]
            f_ref[...] = (hh * _rms_rstd(hh) * g_ref[...]).astype(BF16)

        f = f_ref[...]
        a = jnp.dot(f, w1_ref[...], preferred_element_type=F32)
        b = jnp.dot(f, w3_ref[...], preferred_element_type=F32)
        g1_ref[...] = a.astype(BF16)
        g3_ref[...] = b.astype(BF16)
        s_ref[...] = (a * jax.nn.sigmoid(a) * b).astype(BF16)

    slab = pl.BlockSpec((None, tm, n), lambda i, j: (j, i, 0))
    wslab = pl.BlockSpec((None, None, D, n), lambda i, j: (j, layer, 0, 0))
    hidden = SDS((N_CHIPS, T, n), BF16)
    return pl.pallas_call(
        body, name=name, grid=(T // tm, N_CHIPS),
        in_specs=[pl.BlockSpec((tm, D), lambda i, j: (i, 0)), pl.BlockSpec((1, D), lambda i, j: (0, 0)), wslab, wslab],
        out_specs=[pl.BlockSpec((tm, D), lambda i, j: (i, 0)), slab, slab, slab],
        out_shape=[SDS((T, D), BF16), hidden, hidden, hidden],
        compiler_params=_params("parallel", "arbitrary"),
    )(h, g, w1, w3)


def _glu(a, d):
    return a[:, :d] * jax.nn.sigmoid(a[:, d:])


def _conv_tile(T):
    return min(256, T)


def _fill_shifted(sh_ref, tc):
    n = tc + CONV_HALO - V7X_SUBLANES
    for r in range(1, V7X_SUBLANES):
        sh_ref[r, 0:n, :] = sh_ref[0, pl.ds(r, n), :]


def _depthwise_taps(sh_ref, w_ref, offsets, bias_ref, out_ref, tc):
    D = out_ref.shape[1]

    def chunk(i, carry):
        t0 = pl.multiple_of(i * CONV_ROW_CHUNK, CONV_ROW_CHUNK)
        for cb in range(D // CONV_LANE_CHUNK):
            cs = slice(cb * CONV_LANE_CHUNK, (cb + 1) * CONV_LANE_CHUNK)
            acc = jnp.zeros((CONV_ROW_CHUNK, CONV_LANE_CHUNK), F32)
            for r in range(V7X_SUBLANES):
                taps = [(j, o // V7X_SUBLANES) for j, o in enumerate(offsets) if o % V7X_SUBLANES == r]
                if not taps:
                    continue
                span = CONV_ROW_CHUNK + V7X_SUBLANES * max(q for _, q in taps)
                rows = sh_ref[r, pl.ds(t0, span), cs]
                for j, q in taps:
                    acc = acc + rows[V7X_SUBLANES * q:V7X_SUBLANES * q + CONV_ROW_CHUNK] * w_ref[j:j + 1, cs]
            if bias_ref is not None:
                acc = acc + bias_ref[:, cs]
            out_ref[pl.ds(t0, CONV_ROW_CHUNK), cs] = acc
        return carry

    lax.fori_loop(0, tc // CONV_ROW_CHUNK, chunk, 0)


def _depthwise_tap_grads(dy_sh, x_sh, offsets, dw_ref, tc):
    D = dw_ref.shape[1]
    for cb in range(D // V7X_LANES):
        cs = slice(cb * V7X_LANES, (cb + 1) * V7X_LANES)

        def row_tiles(i, accs, cs=cs):
            for k in range(CONV_GRAD_UNROLL):
                t0 = pl.multiple_of(i * (CONV_GRAD_UNROLL * V7X_SUBLANES), V7X_SUBLANES) + k * V7X_SUBLANES
                d = dy_sh[0, pl.ds(t0, V7X_SUBLANES), cs]
                accs = tuple(
                    acc + d * x_sh[o % V7X_SUBLANES, pl.ds(t0 + o // V7X_SUBLANES * V7X_SUBLANES, V7X_SUBLANES), cs]
                    for acc, o in zip(accs, offsets))
            return accs

        zero = jnp.zeros((V7X_SUBLANES, V7X_LANES), F32)
        accs = lax.fori_loop(0, tc // (CONV_GRAD_UNROLL * V7X_SUBLANES), row_tiles, tuple(zero for _ in offsets))
        for j, acc in enumerate(accs):
            dw_ref[j:j + 1, cs] += jnp.sum(acc, axis=0, keepdims=True)


def _conv_fwd(a, w_dw, b_dw, ln_g, ln_b):
    T = a.shape[0]
    D = a.shape[1] // 2
    tc = _conv_tile(T)
    per = tc // CONV_HALO

    def body(a_ref, ah_ref, w_ref, bdw_ref, lg_ref, lb_ref, c_ref, act_ref, u_sh):
        i = pl.program_id(0)
        u_sh[0, 0:CONV_HALO, :] = jnp.where(i > 0, _glu(ah_ref[...], D), 0.0)
        u_sh[0, CONV_HALO:, :] = _glu(a_ref[...], D)
        _fill_shifted(u_sh, tc)
        _depthwise_taps(u_sh, w_ref, [CONV_FIRST_TAP + j for j in range(CONV_WIDTH)], bdw_ref, c_ref, tc)
        c = c_ref[...]
        xc = c - jnp.mean(c, axis=-1, keepdims=True)
        z = xc * lax.rsqrt(jnp.mean(xc * xc, axis=-1, keepdims=True) + LN_EPS)
        l = z * lg_ref[...] + lb_ref[...]
        act_ref[...] = (l * jax.nn.sigmoid(l)).astype(BF16)

    return pl.pallas_call(
        body, name="conv_fwd", grid=(T // tc,),
        in_specs=[_rows(tc, 2 * D),
                  pl.BlockSpec((CONV_HALO, 2 * D), lambda i: (jnp.maximum(i * per - 1, 0), 0)),
                  _whole((CONV_WIDTH, D)), _whole((1, D)), _whole((1, D)), _whole((1, D))],
        out_specs=[_rows(tc, D), _rows(tc, D)],
        out_shape=[SDS((T, D), F32), SDS((T, D), BF16)],
        scratch_shapes=[pltpu.VMEM((V7X_SUBLANES, tc + CONV_HALO, D), F32)],
        compiler_params=_params("parallel"),
    )(a, a, w_dw, b_dw, ln_g, ln_b)


def _final_loss(h, g, target):
    T, D = h.shape
    tm = min(512, T)

    def body(h_ref, g_ref, t_ref, dh_ref, loss_ref, dg_ref):
        @pl.when(pl.program_id(0) == 0)
        def _():
            loss_ref[...] = jnp.zeros_like(loss_ref)
            dg_ref[...] = jnp.zeros_like(dg_ref)

        hh = h_ref[...]
        r = _rms_rstd(hh)
        g = g_ref[...]
        d = hh * r * g - t_ref[...]
        loss_ref[...] += 0.5 * jnp.sum(jnp.mean(d * d, axis=-1, keepdims=True), axis=0, keepdims=True)
        dout = d * (1.0 / D)
        dg_ref[...] += jnp.sum(dout * (hh * r), axis=0, keepdims=True)
        dxh = dout * g
        dh_ref[...] = r * dxh - hh * (r * r * r) * jnp.mean(dxh * hh, axis=-1, keepdims=True)

    return pl.pallas_call(
        body, name="final_loss", grid=(T // tm,),
        in_specs=[_rows(tm, D), _whole((1, D)), _rows(tm, D)],
        out_specs=[_rows(tm, D), _whole((1, 1)), _whole((1, D))],
        out_shape=[SDS((T, D), F32), SDS((1, 1), F32), SDS((1, D), F32)],
        compiler_params=_params("arbitrary"),
    )(h, g, target)


def _ffn_bwd_act(name, dh, w2, layer, g1, g3):
    T, D = dh.shape
    n = w2.shape[2]
    tm = min(512, T)

    def body(dh_ref, w2_ref, g1_ref, g3_ref, dg1_ref, dg3_ref):
        ds = lax.dot_general(dh_ref[...].astype(BF16), w2_ref[...], (((1,), (1,)), ((), ())),
                             preferred_element_type=F32)
        act, dact = _silu_and_grad(g1_ref[...].astype(F32))
        dg1_ref[...] = (ds * g3_ref[...].astype(F32) * dact).astype(BF16)
        dg3_ref[...] = (ds * act).astype(BF16)

    slab = pl.BlockSpec((None, tm, n), lambda i, j: (j, i, 0))
    hidden = SDS((N_CHIPS, T, n), BF16)
    return pl.pallas_call(
        body, name=name, grid=(T // tm, N_CHIPS),
        in_specs=[pl.BlockSpec((tm, D), lambda i, j: (i, 0)),
                  pl.BlockSpec((None, None, n, D), lambda i, j: (j, layer, 0, 0)), slab, slab],
        out_specs=[slab, slab],
        out_shape=[hidden, hidden],
        compiler_params=_params("parallel", "arbitrary"),
    )(dh, w2, g1, g3)


def _dot_tn(a, b):
    return lax.dot_general(a.astype(BF16), b.astype(BF16), (((0,), (0,)), ((), ())), preferred_element_type=F32)


def _dot_nt(a, b):
    return lax.dot_general(a.astype(BF16), b, (((1,), (1,)), ((), ())), preferred_element_type=F32)


def _mm_tn(name, a, b, col_chunks=1):
    a_slabs, b_slabs = a.ndim == 3, b.ndim == 3
    T = a.shape[-2]
    tt = min(512, T)
    ka, nb = a.shape[-1], b.shape[-1]
    if a_slabs or b_slabs:
        out_dims = (N_CHIPS, ka, nb)
    elif col_chunks > 1:
        out_dims = (col_chunks, ka, nb // col_chunks)
    else:
        out_dims = (ka, nb)

    def body(a_ref, b_ref, o_ref):
        @pl.when(pl.program_id(0) == 0)
        def _():
            o_ref[...] = jnp.zeros_like(o_ref)

        if a_slabs:
            bb = b_ref[...].astype(BF16)
            for j in range(N_CHIPS):
                o_ref[j] += _dot_tn(a_ref[j], bb)
        elif b_slabs:
            aa = a_ref[...].astype(BF16)
            for j in range(N_CHIPS):
                o_ref[j] += _dot_tn(aa, b_ref[j])
        elif col_chunks > 1:
            aa = a_ref[...].astype(BF16)
            w = nb // col_chunks
            for j in range(col_chunks):
                o_ref[j] += _dot_tn(aa, b_ref[:, j * w:(j + 1) * w])
        else:
            o_ref[...] += _dot_tn(a_ref[...], b_ref[...])

    def spec(arr, slabs):
        if slabs:
            return pl.BlockSpec((N_CHIPS, tt, arr.shape[-1]), lambda t: (0, t, 0))
        return _rows(tt, arr.shape[-1])

    return pl.pallas_call(
        body, name=name, grid=(T // tt,),
        in_specs=[spec(a, a_slabs), spec(b, b_slabs)],
        out_specs=_whole(out_dims),
        out_shape=SDS(out_dims, F32),
        compiler_params=_params("arbitrary"),
    )(a, b)


def _mm_nt_normbwd(name, pairs, h, g, dh, after):
    T, D = h.shape
    tm = min(256, T)
    n_pairs = len(pairs)
    kinds = ["slabs" if dy.ndim == 3 else ("quarters" if w.ndim == 3 else "plain") for dy, w, _ in pairs]

    def body(*refs):
        dy_refs = refs[:n_pairs]
        w_refs = refs[n_pairs:2 * n_pairs]
        h_ref, g_ref, dh_ref, _, o_ref, dg_ref, cs_ref = refs[2 * n_pairs:]

        @pl.when(pl.program_id(0) == 0)
        def _():
            dg_ref[...] = jnp.zeros_like(dg_ref)
            cs_ref[...] = jnp.zeros_like(cs_ref)

        df = jnp.zeros((tm, D), F32)
        for dy_ref, w_ref, kd in zip(dy_refs, w_refs, kinds):
            if kd == "slabs":
                for j in range(N_CHIPS):
                    df = df + _dot_nt(dy_ref[j], w_ref[j])
            elif kd == "quarters":
                n = w_ref.shape[2]
                for j in range(N_CHIPS):
                    df = df + _dot_nt(dy_ref[:, j * n:(j + 1) * n], w_ref[j])
            else:
                df = df + _dot_nt(dy_ref[...], w_ref[...])
        hh = h_ref[...]
        r = _rms_rstd(hh)
        dg_ref[...] += jnp.sum(df * (hh * r), axis=0, keepdims=True)
        dxh = df * g_ref[...]
        out = dh_ref[...] + (r * dxh - hh * (r * r * r) * jnp.mean(dxh * hh, axis=-1, keepdims=True))
        o_ref[...] = out
        cs_ref[...] += jnp.sum(out, axis=0, keepdims=True)

    dy_specs, w_specs = [], []
    for (dy, w, layer), kd in zip(pairs, kinds):
        if kd == "slabs":
            dy_specs.append(pl.BlockSpec((N_CHIPS, tm, dy.shape[2]), lambda i: (0, i, 0)))
            w_specs.append(pl.BlockSpec((N_CHIPS, None, D, w.shape[3]),
                                        functools.partial(lambda i, layer: (0, layer, 0, 0), layer=layer)))
        else:
            dy_specs.append(_rows(tm, dy.shape[1]))
            w_specs.append(_whole(w.shape))

    return pl.pallas_call(
        body, name=name, grid=(T // tm,),
        in_specs=[*dy_specs, *w_specs, _rows(tm, D), _whole((1, D)), _rows(tm, D), pl.BlockSpec(memory_space=pl.ANY)],
        out_specs=[_rows(tm, D), _whole((1, D)), _whole((1, D))],
        out_shape=[SDS((T, D), F32), SDS((1, D), F32), SDS((1, D), F32)],
        compiler_params=_params("arbitrary"),
    )(*[dy for dy, _, _ in pairs], *[w for _, w, _ in pairs], h, g, dh, after)


def _mm_nt(name, dy, w, out_dtype):
    T, N = dy.shape
    K = w.shape[0]
    tm = min(512, T)

    def body(dy_ref, w_ref, o_ref):
        o_ref[...] = lax.dot_general(dy_ref[...].astype(BF16), w_ref[...], (((1,), (1,)), ((), ())),
                                     preferred_element_type=F32).astype(out_dtype)

    return pl.pallas_call(
        body, name=name, grid=(T // tm,),
        in_specs=[_rows(tm, N), _whole((K, N))],
        out_specs=_rows(tm, K),
        out_shape=SDS((T, K), out_dtype),
        compiler_params=_params("parallel"),
    )(dy, w)


def _conv_bwd(dact, c, a, w_dw, ln_g, ln_b):
    T, D = c.shape
    tc = _conv_tile(T)
    per = tc // CONV_HALO
    n_tiles = T // tc
    last_halo = T // CONV_HALO - 1

    def ln_bwd(dact_v, c_v, lg, lb):
        xc = c_v - jnp.mean(c_v, axis=-1, keepdims=True)
        rstd = lax.rsqrt(jnp.mean(xc * xc, axis=-1, keepdims=True) + LN_EPS)
        z = xc * rstd
        _, dsilu = _silu_and_grad(z * lg + lb)
        dl = dact_v * dsilu
        dz = dl * lg
        dc = rstd * (dz - jnp.mean(dz, axis=-1, keepdims=True) - z * jnp.mean(dz * z, axis=-1, keepdims=True))
        return dc, dl, z

    def body(dact_ref, dactn_ref, c_ref, cn_ref, a_ref, ah_ref, w_ref, lg_ref, lb_ref,
             da_ref, dlg_ref, dlb_ref, dbdw_ref, dwdw_ref, dbpw1_ref, dc_sh, u_sh, du_scr):
        i = pl.program_id(0)

        @pl.when(i == 0)
        def _():
            for ref in (dlg_ref, dlb_ref, dbdw_ref, dwdw_ref, dbpw1_ref):
                ref[...] = jnp.zeros_like(ref)

        lg, lb = lg_ref[...], lb_ref[...]
        dc, dl, z = ln_bwd(dact_ref[...], c_ref[...], lg, lb)
        dlg_ref[...] += jnp.sum(dl * z, axis=0, keepdims=True)
        dlb_ref[...] += jnp.sum(dl, axis=0, keepdims=True)
        dbdw_ref[...] += jnp.sum(dc, axis=0, keepdims=True)
        dcn, _, _ = ln_bwd(dactn_ref[...], cn_ref[...], lg, lb)
        dc_sh[0, 0:tc, :] = dc
        dc_sh[0, tc:, :] = jnp.where(i < n_tiles - 1, dcn, 0.0)
        _fill_shifted(dc_sh, tc)

        a_v = a_ref[...]
        a1 = a_v[:, :D]
        sg = jax.nn.sigmoid(a_v[:, D:])
        u_sh[0, 0:CONV_HALO, :] = jnp.where(i > 0, _glu(ah_ref[...], D), 0.0)
        u_sh[0, CONV_HALO:, :] = a1 * sg
        _fill_shifted(u_sh, tc)

        _depthwise_taps(dc_sh, w_ref, [CONV_WIDTH - 1 - j for j in range(CONV_WIDTH)], None, du_scr, tc)
        _depthwise_tap_grads(dc_sh, u_sh, [CONV_FIRST_TAP + j for j in range(CONV_WIDTH)], dwdw_ref, tc)

        du = du_scr[...]
        da1 = du * sg
        da2 = du * a1 * sg * (1.0 - sg)
        da_ref[:, :D] = da1.astype(BF16)
        da_ref[:, D:] = da2.astype(BF16)
        dbpw1_ref[:, :D] += jnp.sum(da1, axis=0, keepdims=True)
        dbpw1_ref[:, D:] += jnp.sum(da2, axis=0, keepdims=True)

    nxt = lambda i: (jnp.minimum((i + 1) * per, last_halo), 0)
    return pl.pallas_call(
        body, name="conv_bwd", grid=(n_tiles,),
        in_specs=[_rows(tc, D), pl.BlockSpec((CONV_HALO, D), nxt),
                  _rows(tc, D), pl.BlockSpec((CONV_HALO, D), nxt),
                  _rows(tc, 2 * D),
                  pl.BlockSpec((CONV_HALO, 2 * D), lambda i: (jnp.maximum(i * per - 1, 0), 0)),
                  _whole((CONV_WIDTH, D)), _whole((1, D)), _whole((1, D))],
        out_specs=[_rows(tc, 2 * D), _whole((1, D)), _whole((1, D)), _whole((1, D)),
                   _whole((CONV_HALO, D)), _whole((1, 2 * D))],
        out_shape=[SDS((T, 2 * D), BF16), SDS((1, D), F32), SDS((1, D), F32), SDS((1, D), F32),
                   SDS((CONV_HALO, D), F32), SDS((1, 2 * D), F32)],
        scratch_shapes=[pltpu.VMEM((V7X_SUBLANES, tc + CONV_HALO, D), F32),
                        pltpu.VMEM((V7X_SUBLANES, tc + CONV_HALO, D), F32), pltpu.VMEM((tc, D), F32)],
        compiler_params=_params("arbitrary"),
    )(dact, dact, c, c, a, a, w_dw, ln_g, ln_b)


def _attn_bwd(qkv, dao, cos, sin, sinks):
    T = qkv.shape[0]
    nb = T // ATTN_BLOCK
    qw = N_Q_HEADS * HEAD_DIM
    kw = N_KV_HEADS * HEAD_DIM

    def body(q_ref, kc_ref, kp_ref, vc_ref, vp_ref, do_ref, cos_ref, sin_ref, cosp_ref, sinp_ref, sink_ref,
             dq_ref, dkv_ref, dsink_ref, dbq_ref, dbkv_ref, carry, prev_scr, cur_scr, dq_scr):
        n = pl.program_id(0)

        @pl.when(n == 0)
        def _():
            for ref in (dsink_ref, dbq_ref, dbkv_ref, carry):
                ref[...] = jnp.zeros_like(ref)

        @pl.when(n == nb)
        def _():
            prev_scr[...] = jnp.zeros_like(prev_scr)

        @pl.when(n < nb)
        def _():
            prev_part = _from_previous_block(N_Q_HEADS * ATTN_BLOCK)
            half = PAIRS_PER_KV * ATTN_BLOCK
            upper = _upper_lanes((ATTN_BLOCK, 2 * HEAD_DIM))
            groups = range(N_KV_HEADS)

            def nt(a, b):
                return lax.dot_general(a, b, (((1,), (1,)), ((), ())), preferred_element_type=F32)

            def kv_grad(even_rows, odd_rows, x):
                even = lax.dot_general(even_rows, x, (((0,), (0,)), ((), ())), preferred_element_type=F32)
                odd = lax.dot_general(odd_rows, x, (((0,), (0,)), ((), ())), preferred_element_type=F32)
                t = jnp.where(upper, odd, even)
                return t + _swap_lane_halves(t)

            q = [_pair_rows(q_ref, g) for g in groups]
            do = [_pair_rows(do_ref, g) for g in groups]
            k_prev = [_kv_operands(g, kp_ref[...]) for g in groups]
            k_cur = [_kv_operands(g, kc_ref[...]) for g in groups]
            v_prev = [_kv_operands(g, vp_ref[...]) for g in groups]
            v_cur = [_kv_operands(g, vc_ref[...]) for g in groups]
            probs, p_sink = _folded_probs(n, q, k_prev, k_cur, sink_ref, prev_part)
            dp_prev = jnp.concatenate([nt(do[g], v_prev[g][i]) for g in groups for i in range(2)], axis=0)
            dp_cur = jnp.concatenate([nt(do[g], v_cur[g][i]) for g in groups for i in range(2)], axis=0)
            dp = jnp.where(prev_part, dp_prev, dp_cur)
            delta = jnp.sum(probs * dp, axis=1, keepdims=True)
            ds_prev, ds_cur = _split_folded(probs * (dp - delta) * (HEAD_DIM ** -0.5), prev_part)
            p_prev, p_cur = _split_folded(probs, prev_part)
            dsink_rows = -(p_sink * delta)
            heads = [h for g in groups for h in _group_heads(g)]
            for i, h in enumerate(heads):
                dsink_ref[:, h:h + 1] += jnp.sum(dsink_rows[i * ATTN_BLOCK:(i + 1) * ATTN_BLOCK], axis=0,
                                                 keepdims=True)
            kv_grads = []
            for g in groups:
                even, odd = slice(2 * g * half, (2 * g + 1) * half), slice((2 * g + 1) * half, (2 * g + 2) * half)
                dq = (jnp.dot(ds_prev[even], k_prev[g][0], preferred_element_type=F32)
                      + jnp.dot(ds_cur[even], k_cur[g][0], preferred_element_type=F32)
                      + jnp.dot(ds_prev[odd], k_prev[g][1], preferred_element_type=F32)
                      + jnp.dot(ds_cur[odd], k_cur[g][1], preferred_element_type=F32))
                for i in range(PAIRS_PER_KV):
                    hp = g * PAIRS_PER_KV + i
                    dq_scr[:, hp * 2 * HEAD_DIM:(hp + 1) * 2 * HEAD_DIM] = dq[i * ATTN_BLOCK:(i + 1) * ATTN_BLOCK]
                kv_grads.append((kv_grad(ds_prev[even], ds_prev[odd], q[g]), kv_grad(ds_cur[even], ds_cur[odd], q[g]),
                                 kv_grad(p_prev[even], p_prev[odd], do[g]), kv_grad(p_cur[even], p_cur[odd], do[g])))
            (dkp0, dkc0, dvp0, dvc0), (dkp1, dkc1, dvp1, dvc1) = kv_grads
            prev_scr[:, :kw] = jnp.where(upper, dkp1, dkp0)
            prev_scr[:, kw:] = jnp.where(upper, dvp1, dvp0)
            cur_scr[:, :kw] = jnp.where(upper, dkc1, dkc0)
            cur_scr[:, kw:] = jnp.where(upper, dvc1, dvc0)
            dq_pre = _rope_transposed(dq_scr[...], cos_ref, sin_ref)
            dq_ref[...] = dq_pre.astype(BF16)
            dbq_ref[...] += jnp.sum(dq_pre, axis=0, keepdims=True)

        tot = carry[...] + prev_scr[...]
        dk_pre = _rope_transposed(tot[:, :kw], cosp_ref, sinp_ref)
        dkv_ref[:, :kw] = dk_pre.astype(BF16)
        dkv_ref[:, kw:] = tot[:, kw:].astype(BF16)
        dbkv_ref[:, :kw] += jnp.sum(dk_pre, axis=0, keepdims=True)
        dbkv_ref[:, kw:] += jnp.sum(tot[:, kw:], axis=0, keepdims=True)

        @pl.when(n < nb)
        def _():
            carry[...] = cur_scr[...]

    cur = lambda n: (jnp.minimum(n, nb - 1), 0)
    out_lag = lambda n: (jnp.maximum(n - 1, 0), 0)
    return pl.pallas_call(
        body, name="attn_bwd", grid=(nb + 1,),
        in_specs=[*_attn_specs(T),
                  pl.BlockSpec((ATTN_BLOCK, qw), cur),
                  pl.BlockSpec((ATTN_BLOCK, V7X_LANES), cur), pl.BlockSpec((ATTN_BLOCK, V7X_LANES), cur),
                  pl.BlockSpec((ATTN_BLOCK, V7X_LANES), out_lag), pl.BlockSpec((ATTN_BLOCK, V7X_LANES), out_lag),
                  _whole((1, N_Q_HEADS))],
        out_specs=[pl.BlockSpec((ATTN_BLOCK, qw), cur), pl.BlockSpec((ATTN_BLOCK, 2 * kw), out_lag),
                   _whole((1, N_Q_HEADS)), _whole((1, qw)), _whole((1, 2 * kw))],
        out_shape=[SDS((T, qw), BF16), SDS((T, 2 * kw), BF16),
                   SDS((1, N_Q_HEADS), F32), SDS((1, qw), F32), SDS((1, 2 * kw), F32)],
        scratch_shapes=[pltpu.VMEM((ATTN_BLOCK, 2 * kw), F32), pltpu.VMEM((ATTN_BLOCK, 2 * kw), F32),
                        pltpu.VMEM((ATTN_BLOCK, 2 * kw), F32), pltpu.VMEM((ATTN_BLOCK, qw), F32)],
        compiler_params=_params("arbitrary"),
    )(qkv, qkv, qkv, qkv, qkv, dao, cos, sin, cos, sin, sinks)


def _local_step(x, target, p, reduce_large):
    T, D = x.shape
    cos, sin = _rope_tables(T)
    qw = N_Q_HEADS * HEAD_DIM
    nm, nf = p["norm_mix"], p["norm_ffn"]

    y0, qkv = _qkv_proj(x, nm[0:1], p["attn_w_qkv"], p["attn_b_qkv"], cos, sin, p["gather_started"])
    ao = _attn_fwd(qkv, p["attn_sinks"])
    h1 = _mm_res("attn_out", ao, p["attn_w_o"], p["attn_b_o"], x)
    p = {**p, **p["other_weights"](h1)}
    w1, w3, w2 = p["ffn_w1"], p["ffn_w3"], p["ffn_w2"]
    f0, g1a, g3a, s0 = _ffn_up("ffn0_up", h1, nf[0:1], w1, w3, 0)
    h2 = _ffn_down("ffn0_down", s0, w2, 0, h1)
    y1, a = _pw1_proj(h2, nm[1:2], p["conv_w_pw1"], p["conv_b_pw1"])
    c, act = _conv_fwd(a, p["conv_w_dw"], p["conv_b_dw"], p["conv_ln_g"], p["conv_ln_b"])
    h3 = _mm_res("conv_out", act, p["conv_w_pw2"], p["conv_b_pw2"], h2)
    f1, g1b, g3b, s1 = _ffn_up("ffn1_up", h3, nf[1:2], w1, w3, 1)
    h4 = _ffn_down("ffn1_down", s1, w2, 1, h3)
    dh4, loss, d_norm_final = _final_loss(h4, p["norm_final"], target)

    g = {}
    dg1, dg3 = _ffn_bwd_act("ffn1_bwd_act", dh4, w2, 1, g1b, g3b)
    dw2_1 = _mm_tn("ffn1_dw2", s1, dh4)
    dw1_1 = _mm_tn("ffn1_dw1", f1, dg1)
    dw3_1 = _mm_tn("ffn1_dw3", f1, dg3)
    started = reduce_large("ffn1", {("ffn_w1", 1): dw1_1, ("ffn_w3", 1): dw3_1, ("ffn_w2", 1): dw2_1})
    dh3, dnf1, db_pw2 = _mm_nt_normbwd("ffn1_bwd_in", [(dg1, w1, 1), (dg3, w3, 1)], h3, nf[1:2], dh4, started)

    dw_pw2 = _mm_tn("conv_dw_pw2", act, dh3)
    dact = _mm_nt("conv_bwd_out", dh3, p["conv_w_pw2"], F32)
    da, d_ln_g, d_ln_b, d_b_dw, d_w_dw, d_b_pw1 = _conv_bwd(dact, c, a, p["conv_w_dw"], p["conv_ln_g"], p["conv_ln_b"])
    dw_pw1 = _mm_tn("conv_dw_pw1", y1, da, col_chunks=N_CHIPS)
    started = reduce_large("conv", {("conv_w_pw2", 0): dw_pw2.reshape(N_CHIPS, -1, D), ("conv_w_pw1", 0): dw_pw1})
    dh2, dnm1, _ = _mm_nt_normbwd("conv_bwd_in", [(da, p["conv_w_pw1"], None)], h2, nm[1:2], dh3, started)

    dg1, dg3 = _ffn_bwd_act("ffn0_bwd_act", dh2, w2, 0, g1a, g3a)
    dw2_0 = _mm_tn("ffn0_dw2", s0, dh2)
    dw1_0 = _mm_tn("ffn0_dw1", f0, dg1)
    dw3_0 = _mm_tn("ffn0_dw3", f0, dg3)
    started = reduce_large("ffn0", {("ffn_w1", 0): dw1_0, ("ffn_w3", 0): dw3_0, ("ffn_w2", 0): dw2_0})
    dh1, dnf0, db_o = _mm_nt_normbwd("ffn0_bwd_in", [(dg1, w1, 0), (dg3, w3, 0)], h1, nf[0:1], dh2, started)

    dw_o = _mm_tn("attn_dw_o", ao, dh1)
    dao = _mm_nt("attn_bwd_out", dh1, p["attn_w_o"], BF16)
    dq, dkv, d_sinks, dbq, dbkv = _attn_bwd(qkv, dao, cos, sin, p["attn_sinks"])
    dwq = _mm_tn("attn_dw_q", y0, dq)
    dwkv = _mm_tn("attn_dw_kv", y0, dkv)
    wqkv = p["attn_w_qkv"]
    n_qkv = wqkv.shape[1] // N_CHIPS
    dwqkv = jnp.moveaxis(jnp.concatenate([dwq, dwkv], axis=1).reshape(D, N_CHIPS, n_qkv), 1, 0)
    started = reduce_large("attn", {("attn_w_o", 0): dw_o.reshape(N_CHIPS, -1, D), ("attn_w_qkv", 0): dwqkv})
    dx, dnm0, _ = _mm_nt_normbwd("attn_bwd_in", [(dq, wqkv[:, :qw], None), (dkv, wqkv[:, qw:], None)], x, nm[0:1], dh1,
                                 started)

    g["norm_mix"] = jnp.concatenate([dnm0, dnm1], axis=0)
    g["norm_ffn"] = jnp.concatenate([dnf0, dnf1], axis=0)
    g["attn_b_qkv"] = jnp.concatenate([dbq, dbkv], axis=1)
    g["attn_sinks"] = d_sinks
    g["attn_b_o"] = db_o
    g["conv_b_pw1"] = d_b_pw1
    g["conv_w_dw"] = d_w_dw[:CONV_WIDTH]
    g["conv_b_dw"] = d_b_dw
    g["conv_ln_g"] = d_ln_g
    g["conv_ln_b"] = d_ln_b
    g["conv_b_pw2"] = db_pw2
    g["norm_final"] = d_norm_final
    return loss, dx, g


ANY = pl.BlockSpec(memory_space=pl.ANY)
VMEM_WHOLE = pl.BlockSpec(memory_space=pltpu.VMEM)


def _my_place():
    return lax.axis_index("x"), lax.axis_index("y"), lax.axis_index("c")


def _other_chips(x, y):
    places = [(1 - x, y), (x, 1 - y), (1 - x, 1 - y)]
    return [(bx, by, 2 * bx + by) for bx, by in places]


def _exchange_small(name, v, reduce):
    r, w = v.shape

    def body(v_ref, o_ref, *rest):
        if reduce:
            buf, send_sems, recv_sems = rest
        else:
            buf = o_ref
            send_sems, recv_sems = rest
        x, y, c = _my_place()
        me = 4 * x + 2 * y + c
        sends = []
        for k in range(1, N_DEV):
            peer = (1 - x if k & 4 else x, 1 - y if k & 2 else y, 1 - c if k & 1 else c)
            cp = pltpu.make_async_remote_copy(
                src_ref=v_ref, dst_ref=buf.at[me], send_sem=send_sems.at[k - 1], recv_sem=recv_sems.at[k - 1],
                device_id=peer, device_id_type=MESH)
            cp.start()
            sends.append(cp)
        buf[me] = v_ref[...]
        for k in range(1, N_DEV):
            src = 4 * (1 - x if k & 4 else x) + 2 * (1 - y if k & 2 else y) + (1 - c if k & 1 else c)
            pltpu.make_async_remote_copy(
                src_ref=v_ref, dst_ref=buf.at[src], send_sem=send_sems.at[k - 1], recv_sem=recv_sems.at[k - 1],
                device_id=(x, y, c), device_id_type=MESH).wait_recv()
        for cp in sends:
            cp.wait_send()
        if reduce:
            acc = buf[0]
            for d in range(1, N_DEV):
                acc = acc + buf[d]
            o_ref[...] = acc

    sems = [pltpu.SemaphoreType.DMA((N_DEV - 1,)), pltpu.SemaphoreType.DMA((N_DEV - 1,))]
    if reduce:
        out_shape = SDS((r, w), F32)
        scratch = [pltpu.VMEM((N_DEV, r, w), F32)] + sems
    else:
        out_shape = SDS((N_DEV, r, w), F32)
        scratch = sems
    return pl.pallas_call(
        body, name=name, out_shape=out_shape, in_specs=[VMEM_WHOLE], out_specs=VMEM_WHOLE,
        scratch_shapes=scratch,
        compiler_params=pltpu.CompilerParams(vmem_limit_bytes=V7X_VMEM_LIMIT_BYTES),
    )(v)


def _cast_into_slot(name, gathered, shard, chip_idx):
    rows, cols = shard.shape
    tr = _pack_row_tile(rows)

    def body(k_ref, s_ref, g_ref, o_ref):
        o_ref[...] = s_ref[...].astype(BF16)

    return pl.pallas_call(
        body, name=name,
        grid_spec=pltpu.PrefetchScalarGridSpec(
            num_scalar_prefetch=1, grid=(rows // tr,),
            in_specs=[pl.BlockSpec((tr, cols), lambda i, k_ref: (i, 0)), pl.BlockSpec(memory_space=pl.ANY)],
            out_specs=pl.BlockSpec((None, tr, cols), lambda i, k_ref: (k_ref[0], i, 0))),
        out_shape=SDS(gathered.shape, BF16),
        input_output_aliases={2: 0},
        compiler_params=_params("parallel"),
    )(chip_idx, shard, gathered)


def _row_halves(ref, c):
    half = ref.shape[1] // 2
    return pl.ds(pl.multiple_of(c * half, 16), half), pl.ds(pl.multiple_of((1 - c) * half, 16), half)


def _gather_ici_copies(refs, send_sems, recv_sems):
    x, y, c = _my_place()
    k = 2 * x + y
    pairs = []
    for i, ref in enumerate(refs):
        mine, _ = _row_halves(ref, c)
        for j, (bx, by, kb) in enumerate(_other_chips(x, y)):
            sems = dict(send_sem=send_sems.at[3 * i + j], recv_sem=recv_sems.at[3 * i + j], device_id_type=MESH)
            send = pltpu.make_async_remote_copy(src_ref=ref.at[k, mine], dst_ref=ref.at[k, mine],
                                                device_id=(bx, by, c), **sems)
            arrival = pltpu.make_async_remote_copy(src_ref=ref.at[kb, mine], dst_ref=ref.at[kb, mine],
                                                   device_id=(bx, by, c), **sems)
            pairs.append((send, arrival))
    return pairs


def _gather_d2d_copies(refs, send_sems, recv_sems, first_sem):
    x, y, c = _my_place()
    pairs = []
    for i, ref in enumerate(refs):
        mine, theirs = _row_halves(ref, c)
        for j, (_, _, kb) in enumerate(_other_chips(x, y)):
            sem = first_sem + 3 * i + j
            sems = dict(send_sem=send_sems.at[sem], recv_sem=recv_sems.at[sem], device_id=(x, y, 1 - c),
                        device_id_type=MESH)
            send = pltpu.make_async_remote_copy(src_ref=ref.at[kb, mine], dst_ref=ref.at[kb, mine], **sems)
            arrival = pltpu.make_async_remote_copy(src_ref=ref.at[kb, theirs], dst_ref=ref.at[kb, theirs], **sems)
            pairs.append((send, arrival))
    return pairs


def _run_copies(pairs):
    for send, _ in pairs:
        send.start()
    for send, arrival in pairs:
        send.wait_send()
        arrival.wait_recv()


def _gather_now(name, gathered):
    n_w = len(gathered)

    def body(*refs):
        in_refs = refs[:n_w]
        send_sems, recv_sems = refs[2 * n_w:]
        _run_copies(_gather_ici_copies(in_refs, send_sems, recv_sems))
        _run_copies(_gather_d2d_copies(in_refs, send_sems, recv_sems, 3 * n_w))

    return pl.pallas_call(
        body, name=name, out_shape=[SDS(g.shape, g.dtype) for g in gathered],
        in_specs=[ANY] * n_w, out_specs=[ANY] * n_w, input_output_aliases={i: i for i in range(n_w)},
        scratch_shapes=[pltpu.SemaphoreType.DMA((6 * n_w,)), pltpu.SemaphoreType.DMA((6 * n_w,))],
    )(*gathered)


def _gather_start(name, gathered):
    n_w = len(gathered)

    def body(*refs):
        in_refs = refs[:n_w]
        send_sems, recv_sems = refs[n_w:n_w + 2]
        for send, _ in _gather_ici_copies(in_refs, send_sems, recv_sems):
            send.start()
        refs[-1][...] = jnp.zeros_like(refs[-1])

    out = pl.pallas_call(
        body, name=name,
        out_shape=(pltpu.SemaphoreType.DMA((3 * n_w,)), pltpu.SemaphoreType.DMA((3 * n_w,)),
                   *[pltpu.HBM(g.shape, g.dtype) for g in gathered], SDS((8, V7X_LANES), F32)),
        in_specs=[HBM_SPEC] * n_w, out_specs=(SEM_SPEC, SEM_SPEC, *[HBM_SPEC] * n_w, VMEM_WHOLE),
        input_output_aliases={i: 2 + i for i in range(n_w)},
        compiler_params=pltpu.CompilerParams(has_side_effects=DATAFLOW),
    )(*[pltpu.with_memory_space_constraint(g, pltpu.HBM) for g in gathered])
    return out[0], out[1], list(out[2:2 + n_w]), out[-1]


def _gather_wait(name, send_sems, recv_sems, gathered, after):
    n_w = len(gathered)

    def body(*refs):
        in_refs = refs[:n_w]
        send_sems, recv_sems = refs[n_w:n_w + 2]
        for send, arrival in _gather_ici_copies(in_refs, send_sems, recv_sems):
            send.wait_send()
            arrival.wait_recv()

    out = pl.pallas_call(
        body, name=name, out_shape=tuple(pltpu.HBM(g.shape, g.dtype) for g in gathered),
        in_specs=[*[HBM_SPEC] * n_w, SEM_SPEC, SEM_SPEC, ANY], out_specs=tuple([HBM_SPEC] * n_w),
        input_output_aliases={i: i for i in range(n_w)},
        compiler_params=pltpu.CompilerParams(has_side_effects=DATAFLOW),
    )(*gathered, send_sems, recv_sems, after)
    return list(out)


def _swap_fetched_with_sibling(name, gathered):
    n_w = len(gathered)

    def body(*refs):
        in_refs = refs[:n_w]
        send_sems, recv_sems = refs[2 * n_w:]
        _run_copies(_gather_d2d_copies(in_refs, send_sems, recv_sems, 0))

    return pl.pallas_call(
        body, name=name, out_shape=[SDS(g.shape, g.dtype) for g in gathered],
        in_specs=[ANY] * n_w, out_specs=[ANY] * n_w, input_output_aliases={i: i for i in range(n_w)},
        scratch_shapes=[pltpu.SemaphoreType.DMA((3 * n_w,)), pltpu.SemaphoreType.DMA((3 * n_w,))],
    )(*gathered)


def _swap_halves_with_sibling(name, grads):
    n_g = len(grads)

    def body(*refs):
        g_refs, o_refs = refs[:n_g], refs[n_g:2 * n_g]
        send_sems, recv_sems = refs[2 * n_g:]
        x, y, c = _my_place()
        copies = []
        for i in range(n_g):
            half = g_refs[i].shape[1] // 2
            theirs = pl.ds(pl.multiple_of((1 - c) * half, 8), half)
            cp = pltpu.make_async_remote_copy(
                src_ref=g_refs[i].at[:, theirs], dst_ref=o_refs[i], send_sem=send_sems.at[i],
                recv_sem=recv_sems.at[i], device_id=(x, y, 1 - c), device_id_type=MESH)
            cp.start()
            copies.append(cp)
        for cp in copies:
            cp.wait()

    return pl.pallas_call(
        body, name=name,
        out_shape=[SDS((g.shape[0], g.shape[1] // 2, g.shape[2]), g.dtype) for g in grads],
        in_specs=[ANY] * n_g, out_specs=[ANY] * n_g,
        scratch_shapes=[pltpu.SemaphoreType.DMA((n_g,)), pltpu.SemaphoreType.DMA((n_g,))],
    )(*grads)


def _pack_row_tile(rows):
    for t in range(min(rows, 512), 7, -1):
        if rows % t == 0 and t % 8 == 0:
            return t
    return rows


def _add_sibling_half(name, grads, from_sibling, c_idx):
    n, R, w = grads.shape
    half = R // 2
    tr = _pack_row_tile(half)
    steps = half // tr

    def body(c_ref, g_ref, s_ref, o_ref):
        o_ref[...] = g_ref[...] + s_ref[...]

    return pl.pallas_call(
        body, name=name,
        grid_spec=pltpu.PrefetchScalarGridSpec(
            num_scalar_prefetch=1, grid=(n, steps),
            in_specs=[pl.BlockSpec((1, tr, w), lambda j, i, c_ref: (j, c_ref[0] * steps + i, 0)),
                      pl.BlockSpec((1, tr, w), lambda j, i, c_ref: (j, i, 0))],
            out_specs=pl.BlockSpec((1, tr, w), lambda j, i, c_ref: (j, i, 0))),
        out_shape=SDS((n, half, w), F32),
        compiler_params=_params("parallel", "parallel"),
    )(c_idx, grads, from_sibling)


HBM_SPEC = pl.BlockSpec(memory_space=pltpu.HBM)
SEM_SPEC = pl.BlockSpec(memory_space=pltpu.SEMAPHORE)
DATAFLOW = pltpu.SideEffectType.DATAFLOW_SIDE_EFFECTING


def _chip_scatter_copies(p_refs, land_refs, send_sems, recv_sems):
    x, y, c = _my_place()
    return [pltpu.make_async_remote_copy(
        src_ref=p_refs[i].at[kb], dst_ref=land_refs[i].at[j], send_sem=send_sems.at[3 * i + j],
        recv_sem=recv_sems.at[3 * i + j], device_id=(bx, by, c), device_id_type=MESH)
        for i in range(len(p_refs)) for j, (bx, by, kb) in enumerate(_other_chips(x, y))]


def _scatter_start(name, partials):
    n_p = len(partials)

    def body(*refs):
        p_refs, land_refs = refs[:n_p], refs[n_p:2 * n_p]
        send_sems, recv_sems = refs[2 * n_p:2 * n_p + 2]
        for cp in _chip_scatter_copies(p_refs, land_refs, send_sems, recv_sems):
            cp.start()
        refs[-1][...] = jnp.zeros_like(refs[-1])

    lands = [pltpu.with_memory_space_constraint(lax.empty((N_CHIPS - 1,) + p.shape[1:], p.dtype), pltpu.HBM)
             for p in partials]
    out = pl.pallas_call(
        body, name=name,
        out_shape=(pltpu.SemaphoreType.DMA((3 * n_p,)), pltpu.SemaphoreType.DMA((3 * n_p,)),
                   *[pltpu.HBM(p.shape, p.dtype) for p in partials], *[pltpu.HBM(l.shape, l.dtype) for l in lands],
                   SDS((8, V7X_LANES), F32)),
        in_specs=[HBM_SPEC] * (2 * n_p), out_specs=(SEM_SPEC, SEM_SPEC, *[HBM_SPEC] * (2 * n_p), VMEM_WHOLE),
        input_output_aliases={i: 2 + i for i in range(2 * n_p)},
        compiler_params=pltpu.CompilerParams(has_side_effects=DATAFLOW),
    )(*[pltpu.with_memory_space_constraint(p, pltpu.HBM) for p in partials], *lands)
    return out[0], out[1], list(out[2:2 + n_p]), list(out[2 + n_p:2 + 2 * n_p]), out[-1]


def _scatter_wait(name, send_sems, recv_sems, partials, lands, after):
    n_p = len(partials)

    def body(*refs):
        p_refs, land_refs = refs[:n_p], refs[n_p:2 * n_p]
        send_sems, recv_sems = refs[2 * n_p:2 * n_p + 2]
        for cp in _chip_scatter_copies(p_refs, land_refs, send_sems, recv_sems):
            cp.wait_send()
            cp.wait_recv()

    out = pl.pallas_call(
        body, name=name,
        out_shape=(*[pltpu.HBM(p.shape, p.dtype) for p in partials], *[pltpu.HBM(l.shape, l.dtype) for l in lands]),
        in_specs=[*[HBM_SPEC] * (2 * n_p), SEM_SPEC, SEM_SPEC, ANY], out_specs=tuple([HBM_SPEC] * (2 * n_p)),
        input_output_aliases={i: i for i in range(2 * n_p)},
        compiler_params=pltpu.CompilerParams(has_side_effects=DATAFLOW),
    )(*partials, *lands, send_sems, recv_sems, after)
    return list(out[:n_p]), list(out[n_p:])


def _sum_chip_partials(name, partial, received, shard, layer, place):
    n, half, w = partial.shape
    tr = _pack_row_tile(half)
    steps = half // tr

    def body(place_ref, p_ref, r_ref, shard_ref, o_ref):
        o_ref[...] = ((p_ref[0] + r_ref[0]) + r_ref[1]) + r_ref[2]

    return pl.pallas_call(
        body, name=name,
        grid_spec=pltpu.PrefetchScalarGridSpec(
            num_scalar_prefetch=1, grid=(steps,),
            in_specs=[pl.BlockSpec((1, tr, w), lambda i, place_ref: (place_ref[0], i, 0)),
                      pl.BlockSpec((n - 1, tr, w), lambda i, place_ref: (0, i, 0)),
                      pl.BlockSpec(memory_space=pl.ANY)],
            out_specs=pl.BlockSpec((tr, w), lambda i, place_ref: ((2 * layer + place_ref[1]) * steps + i, 0))),
        out_shape=SDS(shard.shape, F32),
        input_output_aliases={3: 0},
        compiler_params=_params("parallel"),
    )(place, partial, received, shard)


def _join_halves(shards, layers):
    n_s = len(shards)
    n_sem = sum(layers)

    def body(*refs):
        in_refs = refs[:n_s]
        send_sems, recv_sems = refs[2 * n_s:]
        x, y, c = _my_place()
        copies, sem = [], 0
        for ref, n_layers in zip(in_refs, layers):
            half = ref.shape[0] // (2 * n_layers)
            for layer in range(n_layers):
                mine = pl.ds(pl.multiple_of(layer * 2 * half + c * half, 8), half)
                theirs = pl.ds(pl.multiple_of(layer * 2 * half + (1 - c) * half, 8), half)
                send = pltpu.make_async_remote_copy(
                    src_ref=ref.at[mine], dst_ref=ref.at[mine], send_sem=send_sems.at[sem], recv_sem=recv_sems.at[sem],
                    device_id=(x, y, 1 - c), device_id_type=MESH)
                send.start()
                arrival = pltpu.make_async_remote_copy(
                    src_ref=ref.at[theirs], dst_ref=ref.at[theirs], send_sem=send_sems.at[sem],
                    recv_sem=recv_sems.at[sem], device_id=(x, y, 1 - c), device_id_type=MESH)
                copies.append((send, arrival))
                sem += 1
        for send, arrival in copies:
            send.wait_send()
            arrival.wait_recv()

    return pl.pallas_call(
        body, name="join_halves", out_shape=[SDS(s.shape, s.dtype) for s in shards],
        in_specs=[ANY] * n_s, out_specs=[ANY] * n_s,
        input_output_aliases={i: i for i in range(n_s)},
        scratch_shapes=[pltpu.SemaphoreType.DMA((n_sem,)), pltpu.SemaphoreType.DMA((n_sem,))],
    )(*shards)


def _adamw(name, w, g, m, v):
    rows, width = w.shape
    tr = _pack_row_tile(rows)

    def body(w_ref, g_ref, m_ref, v_ref, d_ref, nm_ref, nv_ref):
        gg = g_ref[...]
        m_new = ADAM_B1 * m_ref[...] + (1.0 - ADAM_B1) * gg
        v_new = ADAM_B2 * v_ref[...] + (1.0 - ADAM_B2) * (gg * gg)
        m_hat = m_new / (1.0 - ADAM_B1 ** ADAM_STEP)
        v_hat = v_new / (1.0 - ADAM_B2 ** ADAM_STEP)
        d_ref[...] = -ADAM_LR * (m_hat / (jnp.sqrt(v_hat) + ADAM_EPS) + ADAM_WD * w_ref[...])
        nm_ref[...] = m_new
        nv_ref[...] = v_new

    spec = _rows(tr, width)
    return pl.pallas_call(
        body, name=name, grid=(rows // tr,),
        in_specs=[spec] * 4, out_specs=[spec] * 3,
        out_shape=[SDS((rows, width), F32)] * 3,
        compiler_params=_params("parallel"),
    )(w, g, m, v)


WEIGHT_NAMES = ['norm_mix', 'norm_ffn', 'attn_w_qkv', 'attn_b_qkv', 'attn_sinks', 'attn_w_o', 'attn_b_o',
                'conv_w_pw1', 'conv_b_pw1', 'conv_w_dw', 'conv_b_dw', 'conv_ln_g', 'conv_ln_b', 'conv_w_pw2',
                'conv_b_pw2', 'ffn_w1', 'ffn_w3', 'ffn_w2', 'norm_final']
BIG = ['attn_w_qkv', 'attn_w_o', 'conv_w_pw1', 'conv_w_pw2', 'ffn_w1', 'ffn_w3', 'ffn_w2']
COLUMN_SPLIT = ('attn_w_qkv', 'conv_w_pw1', 'ffn_w1', 'ffn_w3')
SMALL_SPLIT = ['conv_b_pw1', 'conv_w_dw', 'conv_b_dw', 'conv_ln_g', 'conv_ln_b', 'conv_b_pw2']
SMALL_WHOLE = ['norm_mix', 'norm_ffn', 'attn_b_qkv', 'attn_sinks', 'attn_b_o', 'norm_final']


def _pack_rows(arrays, dtype, row_multiple):
    flat = jnp.concatenate([a.astype(dtype).reshape(-1) for a in arrays])
    rows = -(-flat.shape[0] // PACK_W)
    rows = -(-rows // row_multiple) * row_multiple
    return jnp.pad(flat, (0, rows * PACK_W - flat.shape[0])).reshape(rows, PACK_W)


def _unpack_rows(pack, shapes):
    flat = pack.reshape(-1)
    out, at = [], 0
    for shape in shapes:
        size = 1
        for s in shape:
            size *= s
        out.append(flat[at:at + size].reshape(shape))
        at += size
    return out


def _join_chip_axis(name, parts):
    axis = parts.ndim - 1 if name in COLUMN_SPLIT or name in SMALL_SPLIT else parts.ndim - 2
    moved = jnp.moveaxis(parts, 0, axis - 1)
    shape = list(moved.shape)
    shape[axis - 1:axis + 1] = [shape[axis - 1] * shape[axis]]
    return moved.reshape(shape)


def _split_chip_axis(name, whole, shard_shape):
    axis = len(shard_shape) - 1 if name in COLUMN_SPLIT or name in SMALL_SPLIT else len(shard_shape) - 2
    shape = list(whole.shape)
    shape[axis:axis + 1] = [N_CHIPS, shard_shape[axis]]
    return jnp.moveaxis(whole.reshape(shape), axis, 0)


def kernel(x, norm_mix, norm_ffn, attn_w_qkv, attn_b_qkv, attn_sinks, attn_w_o, attn_b_o, conv_w_pw1, conv_b_pw1, conv_w_dw, conv_b_dw, conv_ln_g, conv_ln_b, conv_w_pw2, conv_b_pw2, ffn_w1, ffn_w3, ffn_w2, norm_final, loss_target, m_norm_mix, m_norm_ffn, m_attn_w_qkv, m_attn_b_qkv, m_attn_sinks, m_attn_w_o, m_attn_b_o, m_conv_w_pw1, m_conv_b_pw1, m_conv_w_dw, m_conv_b_dw, m_conv_ln_g, m_conv_ln_b, m_conv_w_pw2, m_conv_b_pw2, m_ffn_w1, m_ffn_w3, m_ffn_w2, m_norm_final, v_norm_mix, v_norm_ffn, v_attn_w_qkv, v_attn_b_qkv, v_attn_sinks, v_attn_w_o, v_attn_b_o, v_conv_w_pw1, v_conv_b_pw1, v_conv_w_dw, v_conv_b_dw, v_conv_ln_g, v_conv_ln_b, v_conv_w_pw2, v_conv_b_pw2, v_ffn_w1, v_ffn_w3, v_ffn_w2, v_norm_final):
    w = dict(zip(WEIGHT_NAMES, (norm_mix, norm_ffn, attn_w_qkv, attn_b_qkv, attn_sinks, attn_w_o, attn_b_o,
                                conv_w_pw1, conv_b_pw1, conv_w_dw, conv_b_dw, conv_ln_g, conv_ln_b, conv_w_pw2,
                                conv_b_pw2, ffn_w1, ffn_w3, ffn_w2, norm_final)))
    m = dict(zip(WEIGHT_NAMES, (m_norm_mix, m_norm_ffn, m_attn_w_qkv, m_attn_b_qkv, m_attn_sinks, m_attn_w_o,
                                m_attn_b_o, m_conv_w_pw1, m_conv_b_pw1, m_conv_w_dw, m_conv_b_dw, m_conv_ln_g,
                                m_conv_ln_b, m_conv_w_pw2, m_conv_b_pw2, m_ffn_w1, m_ffn_w3, m_ffn_w2, m_norm_final)))
    v = dict(zip(WEIGHT_NAMES, (v_norm_mix, v_norm_ffn, v_attn_w_qkv, v_attn_b_qkv, v_attn_sinks, v_attn_w_o,
                                v_attn_b_o, v_conv_w_pw1, v_conv_b_pw1, v_conv_w_dw, v_conv_b_dw, v_conv_ln_g,
                                v_conv_ln_b, v_conv_w_pw2, v_conv_b_pw2, v_ffn_w1, v_ffn_w3, v_ffn_w2, v_norm_final)))
    T, D = x.shape[1], x.shape[2]
    c_idx = lax.axis_index("c").astype(jnp.int32).reshape(1)
    chip = (2 * lax.axis_index("x") + lax.axis_index("y")).astype(jnp.int32)

    as_rows = lambda a: a.reshape(-1, a.shape[-1])
    slabs = {n: _cast_into_slot(f"cast_{n}", lax.empty((N_CHIPS,) + as_rows(w[n]).shape, BF16), as_rows(w[n]),
                                chip.reshape(1)) for n in BIG}
    first, later = BIG[:2], BIG[2:]
    qkv_parts, w_o_parts = _gather_now("gather_attn", [slabs[n] for n in first])
    send_sems, recv_sems, travelling, gather_started = _gather_start("gather_start", [slabs[n] for n in later])
    layers = ffn_w1.shape[0]
    small_shapes = [w[n].shape for n in SMALL_SPLIT]
    small_all = _exchange_small("gather_small", _pack_rows([w[n] for n in SMALL_SPLIT], F32, 8), reduce=False)
    per_chip = [_unpack_rows(small_all[2 * j], small_shapes) for j in range(N_CHIPS)]
    full = {}
    for i, n in enumerate(SMALL_SPLIT):
        full[n] = _join_chip_axis(n, jnp.stack([per_chip[j][i] for j in range(N_CHIPS)]))

    def other_weights(after):
        landed = _gather_wait("gather_wait", send_sems, recv_sems, travelling, after)
        gathered = dict(zip(later, _swap_fetched_with_sibling("gather_swap", landed)))
        return {"conv_w_pw1": gathered["conv_w_pw1"], "conv_w_pw2": gathered["conv_w_pw2"].reshape(-1, D),
                "ffn_w1": gathered["ffn_w1"].reshape(N_CHIPS, layers, D, -1),
                "ffn_w3": gathered["ffn_w3"].reshape(N_CHIPS, layers, D, -1),
                "ffn_w2": gathered["ffn_w2"].reshape(N_CHIPS, layers, -1, D)}

    p = {
        "norm_mix": norm_mix, "norm_ffn": norm_ffn, "norm_final": norm_final.reshape(1, D),
        "attn_w_qkv": jnp.moveaxis(qkv_parts, 0, 1).reshape(D, -1), "attn_b_qkv": attn_b_qkv,
        "attn_sinks": attn_sinks, "attn_w_o": w_o_parts.reshape(-1, D), "attn_b_o": attn_b_o,
        "conv_b_pw1": full["conv_b_pw1"], "conv_w_dw": full["conv_w_dw"][0],
        "conv_b_dw": full["conv_b_dw"], "conv_ln_g": full["conv_ln_g"], "conv_ln_b": full["conv_ln_b"],
        "conv_b_pw2": full["conv_b_pw2"], "gather_started": gather_started, "other_weights": other_weights,
    }
    in_flight = []

    def reduce_large(tag, grads):
        keys = list(grads)
        from_sibling = _swap_halves_with_sibling(f"grads_to_sibling_{tag}", [grads[k] for k in keys])
        partials = [_add_sibling_half(f"add_sibling_half_{tag}{i}", grads[k], fs, c_idx)
                    for i, (k, fs) in enumerate(zip(keys, from_sibling))]
        *handles, started = _scatter_start(f"scatter_start_{tag}", partials)
        in_flight.append((tag, keys, handles))
        return started

    loss_part, dx, g = _local_step(x[0], loss_target[0], p, reduce_large)
    for n in SMALL_WHOLE + SMALL_SPLIT:
        g[n] = g[n].reshape((-1,) + g[n].shape[-2:]) if w[n].ndim == 3 else g[n].reshape(w[n].shape[:-1] + (-1,))

    place = jnp.stack([chip, c_idx[0]])
    shard_grad = {n: lax.empty(as_rows(w[n]).shape, F32) for n in BIG}
    for tag, keys, (send_sems, recv_sems, partials, lands) in in_flight:
        partials, received = _scatter_wait(f"scatter_wait_{tag}", send_sems, recv_sems, partials, lands, dx)
        for i, (n, layer) in enumerate(keys):
            shard_grad[n] = _sum_chip_partials(f"sum_chip_partials_{tag}{i}", partials[i], received[i],
                                               shard_grad[n], layer, place)
    g_big = dict(zip(BIG, _join_halves([shard_grad[n] for n in BIG], [w[n].shape[0] for n in BIG])))
    big_out = {}
    for n in BIG:
        d_n, m_n, v_n = _adamw(f"adamw_{n}", as_rows(w[n]), g_big[n], as_rows(m[n]), as_rows(v[n]))
        big_out[n] = [a.reshape(w[n].shape) for a in (g_big[n], d_n, m_n, v_n)]

    small_whole_shapes = [w[n].shape for n in SMALL_WHOLE]
    small_full_shapes = [g[n].shape for n in SMALL_SPLIT]
    reduced = _exchange_small(
        "reduce_small", _pack_rows([loss_part] + [g[n] for n in SMALL_WHOLE] + [g[n] for n in SMALL_SPLIT], F32, 8),
        reduce=True)
    pieces = _unpack_rows(reduced, [(1,)] + small_whole_shapes + small_full_shapes)
    loss = pieces[0].reshape(())
    g_small = dict(zip(SMALL_WHOLE, pieces[1:1 + len(SMALL_WHOLE)]))
    for n, whole in zip(SMALL_SPLIT, pieces[1 + len(SMALL_WHOLE):]):
        parts = _split_chip_axis(n, whole, w[n].shape)
        g_small[n] = lax.dynamic_index_in_dim(parts, chip, axis=0, keepdims=False)
    small = SMALL_WHOLE + SMALL_SPLIT
    d_small, m_small, v_small = _adamw(
        "adamw_small", _pack_rows([w[n] for n in small], F32, 8), _pack_rows([g_small[n] for n in small], F32, 8),
        _pack_rows([m[n] for n in small], F32, 8), _pack_rows([v[n] for n in small], F32, 8))

    outs = {}
    for slot, (tag, small_pack) in enumerate((("g", None), ("d", d_small), ("m", m_small), ("v", v_small))):
        vals = {n: big_out[n][slot] for n in BIG}
        if small_pack is None:
            vals.update(g_small)
        else:
            vals.update(zip(small, _unpack_rows(small_pack, [w[n].shape for n in small])))
        outs[tag] = vals
    return (loss, dx.reshape(1, T, D), *[outs["g"][n] for n in WEIGHT_NAMES], *[outs["d"][n] for n in WEIGHT_NAMES],
            *[outs["m"][n] for n in WEIGHT_NAMES], *[outs["v"][n] for n in WEIGHT_NAMES])
```

```python
import functools

import jax
import jax.numpy as jnp
from jax import lax
from jax.experimental import pallas as pl
from jax.experimental.pallas import tpu as pltpu

F32 = jnp.float32
BF16 = jnp.bfloat16
SDS = jax.ShapeDtypeStruct
MESH = pl.DeviceIdType.MESH

HEAD_DIM = 64
N_Q_HEADS = 16
N_KV_HEADS = 2
Q_PER_KV = N_Q_HEADS // N_KV_HEADS
ATTN_BLOCK = 128
ROPE_THETA = 10000.0
CONV_WIDTH = 31
CONV_HALO = 32
CONV_FIRST_TAP = CONV_HALO - CONV_WIDTH + 1
CONV_ROW_CHUNK = 64
CONV_LANE_CHUNK = 256
CONV_GRAD_UNROLL = 4
RMS_EPS = 1e-5
LN_EPS = 1e-5
ADAM_LR = 0.001
ADAM_B1 = 0.9
ADAM_B2 = 0.999
ADAM_EPS = 1e-08
ADAM_WD = 0.01
ADAM_STEP = 10

V7X_LANES = 128
V7X_SUBLANES = 8
V7X_VMEM_LIMIT_BYTES = 56 * 1024 * 1024

N_CHIPS = 4
N_DEV = 8
PACK_W = 1024

MASK_VALUE = -1e30


def _params(*semantics):
    return pltpu.CompilerParams(dimension_semantics=semantics, vmem_limit_bytes=V7X_VMEM_LIMIT_BYTES)


def _rows(tm, width):
    return pl.BlockSpec((tm, width), lambda i: (i, 0))


def _whole(shape):
    return pl.BlockSpec(shape, lambda *_: (0,) * len(shape))


def _rms_rstd(h):
    return lax.rsqrt(jnp.mean(h * h, axis=-1, keepdims=True) + RMS_EPS)


def _silu_and_grad(z):
    sg = jax.nn.sigmoid(z)
    return z * sg, sg * (1.0 + z * (1.0 - sg))


def _swap_rope_halves(t):
    w = t.shape[1]
    half = HEAD_DIM // 2
    lane = lax.broadcasted_iota(jnp.int32, t.shape, 1)
    upper = pltpu.roll(t, w - half, 1)
    lower = pltpu.roll(t, half, 1)
    return jnp.where(lane % HEAD_DIM < half, upper, lower)


def _rope(t, cos_ref, sin_ref):
    reps = t.shape[1] // V7X_LANES
    c = jnp.tile(cos_ref[...], (1, reps))
    s = jnp.tile(sin_ref[...], (1, reps))
    return t * c + _swap_rope_halves(t) * s


def _rope_transposed(dt, cos_ref, sin_ref):
    reps = dt.shape[1] // V7X_LANES
    c = jnp.tile(cos_ref[...], (1, reps))
    s = jnp.tile(sin_ref[...], (1, reps))
    return dt * c + _swap_rope_halves(dt * s)


def _rope_tables(seq_len):
    pos = jnp.arange(seq_len, dtype=F32)
    inv_freq = ROPE_THETA ** (-jnp.arange(0, HEAD_DIM, 2, dtype=F32) / HEAD_DIM)
    ang = pos[:, None] * inv_freq[None, :]
    cos, sin = jnp.cos(ang), jnp.sin(ang)
    cos_t = jnp.concatenate([cos, cos, cos, cos], axis=1)
    sin_t = jnp.concatenate([-sin, sin, -sin, sin], axis=1)
    return cos_t, sin_t


def _qkv_proj(h, g, w, b, cos, sin, after):
    T, D = h.shape
    N = w.shape[1]
    tm = min(512, T)
    rope_w = N - N_KV_HEADS * HEAD_DIM

    def body(h_ref, g_ref, w_ref, b_ref, cos_ref, sin_ref, _, y_ref, o_ref):
        hh = h_ref[...]
        y = (hh * _rms_rstd(hh) * g_ref[...]).astype(BF16)
        y_ref[...] = y
        acc = jnp.dot(y, w_ref[...], preferred_element_type=F32) + b_ref[...]
        o_ref[:, :rope_w] = _rope(acc[:, :rope_w], cos_ref, sin_ref).astype(BF16)
        o_ref[:, rope_w:] = acc[:, rope_w:].astype(BF16)

    return pl.pallas_call(
        body, name="qkv_proj", grid=(T // tm,),
        in_specs=[_rows(tm, D), _whole((1, D)), _whole((D, N)), _whole((1, N)),
                  _rows(tm, V7X_LANES), _rows(tm, V7X_LANES), pl.BlockSpec(memory_space=pl.ANY)],
        out_specs=[_rows(tm, D), _rows(tm, N)],
        out_shape=[SDS((T, D), BF16), SDS((T, N), BF16)],
        compiler_params=_params("parallel"),
    )(h, g, w, b, cos, sin, after)


def _pw1_proj(h, g, w, b):
    T, D = h.shape
    n = w.shape[2]
    N = N_CHIPS * n
    tm = min(512, T)

    def body(h_ref, g_ref, w_ref, b_ref, y_ref, o_ref):
        hh = h_ref[...]
        y = (hh * _rms_rstd(hh) * g_ref[...]).astype(BF16)
        y_ref[...] = y
        for j in range(N_CHIPS):
            cols = slice(j * n, (j + 1) * n)
            o_ref[:, cols] = jnp.dot(y, w_ref[j], preferred_element_type=F32) + b_ref[:, cols]

    return pl.pallas_call(
        body, name="pw1_proj", grid=(T // tm,),
        in_specs=[_rows(tm, D), _whole((1, D)), _whole((N_CHIPS, D, n)), _whole((1, N))],
        out_specs=[_rows(tm, D), _rows(tm, N)],
        out_shape=[SDS((T, D), BF16), SDS((T, N), F32)],
        compiler_params=_params("parallel"),
    )(h, g, w, b)


PAIRS_PER_KV = Q_PER_KV // 2


def _upper_lanes(shape):
    return lax.broadcasted_iota(jnp.int32, shape, 1) >= HEAD_DIM


def _swap_lane_halves(t):
    return pltpu.roll(t.astype(F32), HEAD_DIM, 1).astype(t.dtype)


def _kv_operands(g, t):
    swapped = _swap_lane_halves(t)
    in_lower, in_upper = (t, swapped) if g == 0 else (swapped, t)
    upper = _upper_lanes(t.shape)
    zero = jnp.zeros_like(t)
    return jnp.where(upper, zero, in_lower), jnp.where(upper, in_upper, zero)


def _group_heads(g):
    pairs = range(g * PAIRS_PER_KV, (g + 1) * PAIRS_PER_KV)
    return [2 * hp for hp in pairs] + [2 * hp + 1 for hp in pairs]


def _pair_rows(ref, g):
    pairs = range(g * PAIRS_PER_KV, (g + 1) * PAIRS_PER_KV)
    return jnp.concatenate([ref[:, hp * 2 * HEAD_DIM:(hp + 1) * 2 * HEAD_DIM] for hp in pairs], axis=0)


def _from_previous_block(rows):
    row = lax.broadcasted_iota(jnp.int32, (ATTN_BLOCK, ATTN_BLOCK), 0)
    col = lax.broadcasted_iota(jnp.int32, (ATTN_BLOCK, ATTN_BLOCK), 1)
    return jnp.concatenate([col > row] * (rows // ATTN_BLOCK), axis=0)


def _folded_probs(n, q_groups, k_prev_groups, k_cur_groups, sink_ref, prev_part):
    def scores(q, k):
        return lax.dot_general(q, k, (((1,), (1,)), ((), ())), preferred_element_type=F32)

    s_prev = jnp.concatenate([scores(q, k[i]) for q, k in zip(q_groups, k_prev_groups) for i in range(2)], axis=0)
    s_cur = jnp.concatenate([scores(q, k[i]) for q, k in zip(q_groups, k_cur_groups) for i in range(2)], axis=0)
    s_prev = jnp.where(n > 0, s_prev, MASK_VALUE * (HEAD_DIM ** 0.5))
    s = jnp.where(prev_part, s_prev, s_cur) * (HEAD_DIM ** -0.5)
    heads = [h for g in range(N_KV_HEADS) for h in _group_heads(g)]
    sink = jnp.concatenate([jnp.broadcast_to(sink_ref[0:1, h:h + 1], (ATTN_BLOCK, 1)) for h in heads], axis=0)
    m = jnp.maximum(jnp.max(s, axis=1, keepdims=True), sink)
    p = jnp.exp(s - m)
    e_sink = jnp.exp(sink - m)
    inv = 1.0 / (jnp.sum(p, axis=1, keepdims=True) + e_sink)
    return p * inv, e_sink * inv


def _split_folded(t, prev_part):
    tb = t.astype(BF16)
    zero = jnp.zeros_like(tb)
    return jnp.where(prev_part, tb, zero), jnp.where(prev_part, zero, tb)


def _attn_specs(T):
    nb = T // ATTN_BLOCK
    kcol = N_Q_HEADS * HEAD_DIM // V7X_LANES
    cur = lambda n: jnp.minimum(n, nb - 1)
    prev = lambda n: jnp.maximum(jnp.minimum(n, nb - 1) - 1, 0)
    q_spec = pl.BlockSpec((ATTN_BLOCK, N_Q_HEADS * HEAD_DIM), lambda n: (cur(n), 0))
    kc_spec = pl.BlockSpec((ATTN_BLOCK, V7X_LANES), lambda n: (cur(n), kcol))
    kp_spec = pl.BlockSpec((ATTN_BLOCK, V7X_LANES), lambda n: (prev(n), kcol))
    vc_spec = pl.BlockSpec((ATTN_BLOCK, V7X_LANES), lambda n: (cur(n), kcol + 1))
    vp_spec = pl.BlockSpec((ATTN_BLOCK, V7X_LANES), lambda n: (prev(n), kcol + 1))
    return q_spec, kc_spec, kp_spec, vc_spec, vp_spec


def _attn_fwd(qkv, sinks):
    T = qkv.shape[0]
    nb = T // ATTN_BLOCK
    qw = N_Q_HEADS * HEAD_DIM

    def body(q_ref, kc_ref, kp_ref, vc_ref, vp_ref, sink_ref, o_ref):
        n = pl.program_id(0)
        prev_part = _from_previous_block(N_Q_HEADS * ATTN_BLOCK)
        half = PAIRS_PER_KV * ATTN_BLOCK
        groups = range(N_KV_HEADS)
        probs, _ = _folded_probs(n, [_pair_rows(q_ref, g) for g in groups],
                                 [_kv_operands(g, kp_ref[...]) for g in groups],
                                 [_kv_operands(g, kc_ref[...]) for g in groups], sink_ref, prev_part)
        p_prev, p_cur = _split_folded(probs, prev_part)
        for g in groups:
            v_prev, v_cur = _kv_operands(g, vp_ref[...]), _kv_operands(g, vc_ref[...])
            even, odd = slice(2 * g * half, (2 * g + 1) * half), slice((2 * g + 1) * half, (2 * g + 2) * half)
            o = (jnp.dot(p_prev[even], v_prev[0], preferred_element_type=F32)
                 + jnp.dot(p_cur[even], v_cur[0], preferred_element_type=F32)
                 + jnp.dot(p_prev[odd], v_prev[1], preferred_element_type=F32)
                 + jnp.dot(p_cur[odd], v_cur[1], preferred_element_type=F32))
            for i in range(PAIRS_PER_KV):
                hp = g * PAIRS_PER_KV + i
                o_ref[:, hp * 2 * HEAD_DIM:(hp + 1) * 2 * HEAD_DIM] = (
                    o[i * ATTN_BLOCK:(i + 1) * ATTN_BLOCK].astype(BF16))

    return pl.pallas_call(
        body, name="attn_fwd", grid=(nb,),
        in_specs=[*_attn_specs(T), _whole((1, N_Q_HEADS))],
        out_specs=_rows(ATTN_BLOCK, qw),
        out_shape=SDS((T, qw), BF16),
        compiler_params=_params("parallel"),
    )(qkv, qkv, qkv, qkv, qkv, sinks)


def _mm_res(name, a, w, b, res, g):
    T, K = a.shape
    D = w.shape[1]
    tm = min(512, T)

    def body(a_ref, w_ref, b_ref, r_ref, g_ref, o_ref, f_ref):
        h = jnp.dot(a_ref[...], w_ref[...], preferred_element_type=F32) + b_ref[...] + r_ref[...]
        o_ref[...] = h
        f_ref[...] = (h * _rms_rstd(h) * g_ref[...]).astype(BF16)

    return pl.pallas_call(
        body, name=name, grid=(T // tm,),
        in_specs=[_rows(tm, K), _whole((K, D)), _whole((1, D)), _rows(tm, D), _whole((1, D))],
        out_specs=[_rows(tm, D), _rows(tm, D)],
        out_shape=[SDS((T, D), F32), SDS((T, D), BF16)],
        compiler_params=_params("parallel"),
    )(a, w, b, res, g)


def _ffn_down(name, s, w2, layer, res):
    _, T, n = s.shape
    D = w2.shape[3]
    tm = min(512, T)

    def body(s_ref, w_ref, r_ref, o_ref):
        acc = r_ref[...]
        for j in range(N_CHIPS):
            acc = acc + jnp.dot(s_ref[j], w_ref[j], preferred_element_type=F32)
        o_ref[...] = acc

    return pl.pallas_call(
        body, name=name, grid=(T // tm,),
        in_specs=[pl.BlockSpec((N_CHIPS, tm, n), lambda i: (0, i, 0)),
                  pl.BlockSpec((N_CHIPS, None, n, D), lambda i: (0, layer, 0, 0)), _rows(tm, D)],
        out_specs=_rows(tm, D),
        out_shape=SDS((T, D), F32),
        compiler_params=_params("parallel"),
    )(s, w2, res)


def _ffn_up(name, f, w1, w3, layer):
    T, D = f.shape
    n = w1.shape[3]
    tm = min(1024, T)

    def body(f_ref, w1_ref, w3_ref, act_ref, gg_ref, s_ref):
        ff = f_ref[...]
        g1 = jnp.dot(ff, w1_ref[...], preferred_element_type=F32)
        g3 = jnp.dot(ff, w3_ref[...], preferred_element_type=F32)
        act, dact = _silu_and_grad(g1)
        act_ref[...] = act.astype(BF16)
        gg_ref[...] = (g3 * dact).astype(BF16)
        s_ref[...] = (act * g3).astype(BF16)

    slab = pl.BlockSpec((None, tm, n), lambda j, i: (j, i, 0))
    wslab = pl.BlockSpec((None, None, D, n), lambda j, i: (j, layer, 0, 0))
    hidden = SDS((N_CHIPS, T, n), BF16)
    return pl.pallas_call(
        body, name=name, grid=(N_CHIPS, T // tm),
        in_specs=[pl.BlockSpec((tm, D), lambda j, i: (i, 0)), wslab, wslab],
        out_specs=[slab, slab, slab],
        out_shape=[hidden, hidden, hidden],
        compiler_params=_params("parallel", "parallel"),
    )(f, w1, w3)


def _glu(a, d):
    return a[:, :d] * jax.nn.sigmoid(a[:, d:])


def _conv_tile(T):
    return min(256, T)


def _fill_shifted(sh_ref, tc):
    n = tc + CONV_HALO - V7X_SUBLANES
    for r in range(1, V7X_SUBLANES):
        sh_ref[r, 0:n, :] = sh_ref[0, pl.ds(r, n), :]


def _depthwise_taps(sh_ref, w_ref, offsets, bias_ref, out_ref, tc):
    D = out_ref.shape[1]

    def chunk(i, carry):
        t0 = pl.multiple_of(i * CONV_ROW_CHUNK, CONV_ROW_CHUNK)
        for cb in range(D // CONV_LANE_CHUNK):
            cs = slice(cb * CONV_LANE_CHUNK, (cb + 1) * CONV_LANE_CHUNK)
            acc = jnp.zeros((CONV_ROW_CHUNK, CONV_LANE_CHUNK), F32)
            for r in range(V7X_SUBLANES):
                taps = [(j, o // V7X_SUBLANES) for j, o in enumerate(offsets) if o % V7X_SUBLANES == r]
                if not taps:
                    continue
                span = CONV_ROW_CHUNK + V7X_SUBLANES * max(q for _, q in taps)
                rows = sh_ref[r, pl.ds(t0, span), cs]
                for j, q in taps:
                    acc = acc + rows[V7X_SUBLANES * q:V7X_SUBLANES * q + CONV_ROW_CHUNK] * w_ref[j:j + 1, cs]
            if bias_ref is not None:
                acc = acc + bias_ref[:, cs]
            out_ref[pl.ds(t0, CONV_ROW_CHUNK), cs] = acc
        return carry

    lax.fori_loop(0, tc // CONV_ROW_CHUNK, chunk, 0)


def _depthwise_tap_grads(dy_sh, x_sh, offsets, dw_ref, tc):
    D = dw_ref.shape[1]
    for cb in range(D // V7X_LANES):
        cs = slice(cb * V7X_LANES, (cb + 1) * V7X_LANES)

        def row_tiles(i, accs, cs=cs):
            for k in range(CONV_GRAD_UNROLL):
                t0 = pl.multiple_of(i * (CONV_GRAD_UNROLL * V7X_SUBLANES), V7X_SUBLANES) + k * V7X_SUBLANES
                d = dy_sh[0, pl.ds(t0, V7X_SUBLANES), cs]
                accs = tuple(
                    acc + d * x_sh[o % V7X_SUBLANES, pl.ds(t0 + o // V7X_SUBLANES * V7X_SUBLANES, V7X_SUBLANES), cs]
                    for acc, o in zip(accs, offsets))
            return accs

        zero = jnp.zeros((V7X_SUBLANES, V7X_LANES), F32)
        accs = lax.fori_loop(0, tc // (CONV_GRAD_UNROLL * V7X_SUBLANES), row_tiles, tuple(zero for _ in offsets))
        for j, acc in enumerate(accs):
            dw_ref[j:j + 1, cs] += jnp.sum(acc, axis=0, keepdims=True)


def _conv_fwd(a, w_dw, b_dw, ln_g, ln_b):
    T = a.shape[0]
    D = a.shape[1] // 2
    tc = _conv_tile(T)
    per = tc // CONV_HALO

    def body(a_ref, ah_ref, w_ref, bdw_ref, lg_ref, lb_ref, c_ref, act_ref, u_sh):
        i = pl.program_id(0)
        u_sh[0, 0:CONV_HALO, :] = jnp.where(i > 0, _glu(ah_ref[...], D), 0.0)
        u_sh[0, CONV_HALO:, :] = _glu(a_ref[...], D)
        _fill_shifted(u_sh, tc)
        _depthwise_taps(u_sh, w_ref, [CONV_FIRST_TAP + j for j in range(CONV_WIDTH)], bdw_ref, c_ref, tc)
        c = c_ref[...]
        xc = c - jnp.mean(c, axis=-1, keepdims=True)
        z = xc * lax.rsqrt(jnp.mean(xc * xc, axis=-1, keepdims=True) + LN_EPS)
        l = z * lg_ref[...] + lb_ref[...]
        act_ref[...] = (l * jax.nn.sigmoid(l)).astype(BF16)

    return pl.pallas_call(
        body, name="conv_fwd", grid=(T // tc,),
        in_specs=[_rows(tc, 2 * D),
                  pl.BlockSpec((CONV_HALO, 2 * D), lambda i: (jnp.maximum(i * per - 1, 0), 0)),
                  _whole((CONV_WIDTH, D)), _whole((1, D)), _whole((1, D)), _whole((1, D))],
        out_specs=[_rows(tc, D), _rows(tc, D)],
        out_shape=[SDS((T, D), F32), SDS((T, D), BF16)],
        scratch_shapes=[pltpu.VMEM((V7X_SUBLANES, tc + CONV_HALO, D), F32)],
        compiler_params=_params("parallel"),
    )(a, a, w_dw, b_dw, ln_g, ln_b)


def _final_loss(h, g, target):
    T, D = h.shape
    tm = min(512, T)

    def body(h_ref, g_ref, t_ref, dh_ref, loss_ref, dg_ref):
        @pl.when(pl.program_id(0) == 0)
        def _():
            loss_ref[...] = jnp.zeros_like(loss_ref)
            dg_ref[...] = jnp.zeros_like(dg_ref)

        hh = h_ref[...]
        r = _rms_rstd(hh)
        g = g_ref[...]
        d = hh * r * g - t_ref[...]
        loss_ref[...] += 0.5 * jnp.sum(jnp.mean(d * d, axis=-1, keepdims=True), axis=0, keepdims=True)
        dout = d * (1.0 / D)
        dg_ref[...] += jnp.sum(dout * (hh * r), axis=0, keepdims=True)
        dxh = dout * g
        dh_ref[...] = r * dxh - hh * (r * r * r) * jnp.mean(dxh * hh, axis=-1, keepdims=True)

    return pl.pallas_call(
        body, name="final_loss", grid=(T // tm,),
        in_specs=[_rows(tm, D), _whole((1, D)), _rows(tm, D)],
        out_specs=[_rows(tm, D), _whole((1, 1)), _whole((1, D))],
        out_shape=[SDS((T, D), F32), SDS((1, 1), F32), SDS((1, D), F32)],
        compiler_params=_params("arbitrary"),
    )(h, g, target)


def _ffn_bwd_act(name, dh, w2, layer, act, gate_grad):
    T, D = dh.shape
    n = w2.shape[2]
    tm = min(1024, T)

    def body(dh_ref, w2_ref, act_ref, gg_ref, dg1_ref, dg3_ref):
        ds = lax.dot_general(dh_ref[...].astype(BF16), w2_ref[...], (((1,), (1,)), ((), ())),
                             preferred_element_type=F32)
        dg1_ref[...] = (ds * gg_ref[...].astype(F32)).astype(BF16)
        dg3_ref[...] = (ds * act_ref[...].astype(F32)).astype(BF16)

    slab = pl.BlockSpec((None, tm, n), lambda i, j: (j, i, 0))
    hidden = SDS((N_CHIPS, T, n), BF16)
    return pl.pallas_call(
        body, name=name, grid=(T // tm, N_CHIPS),
        in_specs=[pl.BlockSpec((tm, D), lambda i, j: (i, 0)),
                  pl.BlockSpec((None, None, n, D), lambda i, j: (j, layer, 0, 0)), slab, slab],
        out_specs=[slab, slab],
        out_shape=[hidden, hidden],
        compiler_params=_params("parallel", "arbitrary"),
    )(dh, w2, act, gate_grad)


def _dot_tn(a, b):
    return lax.dot_general(a.astype(BF16), b.astype(BF16), (((0,), (0,)), ((), ())), preferred_element_type=F32)


def _dot_nt(a, b):
    return lax.dot_general(a.astype(BF16), b, (((1,), (1,)), ((), ())), preferred_element_type=F32)


def _mm_tn(name, a, b, col_chunks=1):
    a_slabs, b_slabs = a.ndim == 3, b.ndim == 3
    T = a.shape[-2]
    tt = min(512, T)
    ka, nb = a.shape[-1], b.shape[-1]
    if a_slabs or b_slabs:
        out_dims = (N_CHIPS, ka, nb)
    elif col_chunks > 1:
        out_dims = (col_chunks, ka, nb // col_chunks)
    else:
        out_dims = (ka, nb)

    def body(a_ref, b_ref, o_ref):
        @pl.when(pl.program_id(0) == 0)
        def _():
            o_ref[...] = jnp.zeros_like(o_ref)

        if a_slabs:
            bb = b_ref[...].astype(BF16)
            for j in range(N_CHIPS):
                o_ref[j] += _dot_tn(a_ref[j], bb)
        elif b_slabs:
            aa = a_ref[...].astype(BF16)
            for j in range(N_CHIPS):
                o_ref[j] += _dot_tn(aa, b_ref[j])
        elif col_chunks > 1:
            aa = a_ref[...].astype(BF16)
            w = nb // col_chunks
            for j in range(col_chunks):
                o_ref[j] += _dot_tn(aa, b_ref[:, j * w:(j + 1) * w])
        else:
            o_ref[...] += _dot_tn(a_ref[...], b_ref[...])

    def spec(arr, slabs):
        if slabs:
            return pl.BlockSpec((N_CHIPS, tt, arr.shape[-1]), lambda t: (0, t, 0))
        return _rows(tt, arr.shape[-1])

    return pl.pallas_call(
        body, name=name, grid=(T // tt,),
        in_specs=[spec(a, a_slabs), spec(b, b_slabs)],
        out_specs=_whole(out_dims),
        out_shape=SDS(out_dims, F32),
        compiler_params=_params("arbitrary"),
    )(a, b)


def _mm_nt_normbwd(name, pairs, h, g, dh, after):
    T, D = h.shape
    tm = min(256, T)
    n_pairs = len(pairs)
    kinds = ["slabs" if dy.ndim == 3 else ("quarters" if w.ndim == 3 else "plain") for dy, w, _ in pairs]

    def body(*refs):
        dy_refs = refs[:n_pairs]
        w_refs = refs[n_pairs:2 * n_pairs]
        h_ref, g_ref, dh_ref, _, o_ref, dg_ref, cs_ref = refs[2 * n_pairs:]

        @pl.when(pl.program_id(0) == 0)
        def _():
            dg_ref[...] = jnp.zeros_like(dg_ref)
            cs_ref[...] = jnp.zeros_like(cs_ref)

        df = jnp.zeros((tm, D), F32)
        for dy_ref, w_ref, kd in zip(dy_refs, w_refs, kinds):
            if kd == "slabs":
                for j in range(N_CHIPS):
                    df = df + _dot_nt(dy_ref[j], w_ref[j])
            elif kd == "quarters":
                n = w_ref.shape[2]
                for j in range(N_CHIPS):
                    df = df + _dot_nt(dy_ref[:, j * n:(j + 1) * n], w_ref[j])
            else:
                df = df + _dot_nt(dy_ref[...], w_ref[...])
        hh = h_ref[...]
        r = _rms_rstd(hh)
        dg_ref[...] += jnp.sum(df * (hh * r), axis=0, keepdims=True)
        dxh = df * g_ref[...]
        out = dh_ref[...] + (r * dxh - hh * (r * r * r) * jnp.mean(dxh * hh, axis=-1, keepdims=True))
        o_ref[...] = out
        cs_ref[...] += jnp.sum(out, axis=0, keepdims=True)

    dy_specs, w_specs = [], []
    for (dy, w, layer), kd in zip(pairs, kinds):
        if kd == "slabs":
            dy_specs.append(pl.BlockSpec((N_CHIPS, tm, dy.shape[2]), lambda i: (0, i, 0)))
            w_specs.append(pl.BlockSpec((N_CHIPS, None, D, w.shape[3]),
                                        functools.partial(lambda i, layer: (0, layer, 0, 0), layer=layer)))
        else:
            dy_specs.append(_rows(tm, dy.shape[1]))
            w_specs.append(_whole(w.shape))

    return pl.pallas_call(
        body, name=name, grid=(T // tm,),
        in_specs=[*dy_specs, *w_specs, _rows(tm, D), _whole((1, D)), _rows(tm, D), pl.BlockSpec(memory_space=pl.ANY)],
        out_specs=[_rows(tm, D), _whole((1, D)), _whole((1, D))],
        out_shape=[SDS((T, D), F32), SDS((1, D), F32), SDS((1, D), F32)],
        compiler_params=_params("arbitrary"),
    )(*[dy for dy, _, _ in pairs], *[w for _, w, _ in pairs], h, g, dh, after)


def _mm_nt(name, dy, w, out_dtype):
    T, N = dy.shape
    K = w.shape[0]
    tm = min(512, T)

    def body(dy_ref, w_ref, o_ref):
        o_ref[...] = lax.dot_general(dy_ref[...].astype(BF16), w_ref[...], (((1,), (1,)), ((), ())),
                                     preferred_element_type=F32).astype(out_dtype)

    return pl.pallas_call(
        body, name=name, grid=(T // tm,),
        in_specs=[_rows(tm, N), _whole((K, N))],
        out_specs=_rows(tm, K),
        out_shape=SDS((T, K), out_dtype),
        compiler_params=_params("parallel"),
    )(dy, w)


def _conv_bwd(dact, c, a, w_dw, ln_g, ln_b):
    T, D = c.shape
    tc = _conv_tile(T)
    per = tc // CONV_HALO
    n_tiles = T // tc
    last_halo = T // CONV_HALO - 1

    def ln_bwd(dact_v, c_v, lg, lb):
        xc = c_v - jnp.mean(c_v, axis=-1, keepdims=True)
        rstd = lax.rsqrt(jnp.mean(xc * xc, axis=-1, keepdims=True) + LN_EPS)
        z = xc * rstd
        _, dsilu = _silu_and_grad(z * lg + lb)
        dl = dact_v * dsilu
        dz = dl * lg
        dc = rstd * (dz - jnp.mean(dz, axis=-1, keepdims=True) - z * jnp.mean(dz * z, axis=-1, keepdims=True))
        return dc, dl, z

    def body(dact_ref, dactn_ref, c_ref, cn_ref, a_ref, ah_ref, w_ref, lg_ref, lb_ref,
             da_ref, dlg_ref, dlb_ref, dbdw_ref, dwdw_ref, dbpw1_ref, dc_sh, u_sh, du_scr):
        i = pl.program_id(0)

        @pl.when(i == 0)
        def _():
            for ref in (dlg_ref, dlb_ref, dbdw_ref, dwdw_ref, dbpw1_ref):
                ref[...] = jnp.zeros_like(ref)

        lg, lb = lg_ref[...], lb_ref[...]
        dc, dl, z = ln_bwd(dact_ref[...], c_ref[...], lg, lb)
        dlg_ref[...] += jnp.sum(dl * z, axis=0, keepdims=True)
        dlb_ref[...] += jnp.sum(dl, axis=0, keepdims=True)
        dbdw_ref[...] += jnp.sum(dc, axis=0, keepdims=True)
        dcn, _, _ = ln_bwd(dactn_ref[...], cn_ref[...], lg, lb)
        dc_sh[0, 0:tc, :] = dc
        dc_sh[0, tc:, :] = jnp.where(i < n_tiles - 1, dcn, 0.0)
        _fill_shifted(dc_sh, tc)

        a_v = a_ref[...]
        a1 = a_v[:, :D]
        sg = jax.nn.sigmoid(a_v[:, D:])
        u_sh[0, 0:CONV_HALO, :] = jnp.where(i > 0, _glu(ah_ref[...], D), 0.0)
        u_sh[0, CONV_HALO:, :] = a1 * sg
        _fill_shifted(u_sh, tc)

        _depthwise_taps(dc_sh, w_ref, [CONV_WIDTH - 1 - j for j in range(CONV_WIDTH)], None, du_scr, tc)
        _depthwise_tap_grads(dc_sh, u_sh, [CONV_FIRST_TAP + j for j in range(CONV_WIDTH)], dwdw_ref, tc)

        du = du_scr[...]
        da1 = du * sg
        da2 = du * a1 * sg * (1.0 - sg)
        da_ref[:, :D] = da1.astype(BF16)
        da_ref[:, D:] = da2.astype(BF16)
        dbpw1_ref[:, :D] += jnp.sum(da1, axis=0, keepdims=True)
        dbpw1_ref[:, D:] += jnp.sum(da2, axis=0, keepdims=True)

    nxt = lambda i: (jnp.minimum((i + 1) * per, last_halo), 0)
    return pl.pallas_call(
        body, name="conv_bwd", grid=(n_tiles,),
        in_specs=[_rows(tc, D), pl.BlockSpec((CONV_HALO, D), nxt),
                  _rows(tc, D), pl.BlockSpec((CONV_HALO, D), nxt),
                  _rows(tc, 2 * D),
                  pl.BlockSpec((CONV_HALO, 2 * D), lambda i: (jnp.maximum(i * per - 1, 0), 0)),
                  _whole((CONV_WIDTH, D)), _whole((1, D)), _whole((1, D))],
        out_specs=[_rows(tc, 2 * D), _whole((1, D)), _whole((1, D)), _whole((1, D)),
                   _whole((CONV_HALO, D)), _whole((1, 2 * D))],
        out_shape=[SDS((T, 2 * D), BF16), SDS((1, D), F32), SDS((1, D), F32), SDS((1, D), F32),
                   SDS((CONV_HALO, D), F32), SDS((1, 2 * D), F32)],
        scratch_shapes=[pltpu.VMEM((V7X_SUBLANES, tc + CONV_HALO, D), F32),
                        pltpu.VMEM((V7X_SUBLANES, tc + CONV_HALO, D), F32), pltpu.VMEM((tc, D), F32)],
        compiler_params=_params("arbitrary"),
    )(dact, dact, c, c, a, a, w_dw, ln_g, ln_b)


def _attn_bwd(qkv, dao, cos, sin, sinks):
    T = qkv.shape[0]
    nb = T // ATTN_BLOCK
    qw = N_Q_HEADS * HEAD_DIM
    kw = N_KV_HEADS * HEAD_DIM

    def body(q_ref, kc_ref, kp_ref, vc_ref, vp_ref, do_ref, cos_ref, sin_ref, cosp_ref, sinp_ref, sink_ref,
             dq_ref, dkv_ref, dsink_ref, dbq_ref, dbkv_ref, carry, prev_scr, cur_scr, dq_scr):
        n = pl.program_id(0)

        @pl.when(n == 0)
        def _():
            for ref in (dsink_ref, dbq_ref, dbkv_ref, carry):
                ref[...] = jnp.zeros_like(ref)

        @pl.when(n == nb)
        def _():
            prev_scr[...] = jnp.zeros_like(prev_scr)

        @pl.when(n < nb)
        def _():
            prev_part = _from_previous_block(N_Q_HEADS * ATTN_BLOCK)
            half = PAIRS_PER_KV * ATTN_BLOCK
            upper = _upper_lanes((ATTN_BLOCK, 2 * HEAD_DIM))
            groups = range(N_KV_HEADS)

            def nt(a, b):
                return lax.dot_general(a, b, (((1,), (1,)), ((), ())), preferred_element_type=F32)

            def kv_grad(even_rows, odd_rows, x):
                even = lax.dot_general(even_rows, x, (((0,), (0,)), ((), ())), preferred_element_type=F32)
                odd = lax.dot_general(odd_rows, x, (((0,), (0,)), ((), ())), preferred_element_type=F32)
                t = jnp.where(upper, odd, even)
                return t + _swap_lane_halves(t)

            q = [_pair_rows(q_ref, g) for g in groups]
            do = [_pair_rows(do_ref, g) for g in groups]
            k_prev = [_kv_operands(g, kp_ref[...]) for g in groups]
            k_cur = [_kv_operands(g, kc_ref[...]) for g in groups]
            v_prev = [_kv_operands(g, vp_ref[...]) for g in groups]
            v_cur = [_kv_operands(g, vc_ref[...]) for g in groups]
            probs, p_sink = _folded_probs(n, q, k_prev, k_cur, sink_ref, prev_part)
            dp_prev = jnp.concatenate([nt(do[g], v_prev[g][i]) for g in groups for i in range(2)], axis=0)
            dp_cur = jnp.concatenate([nt(do[g], v_cur[g][i]) for g in groups for i in range(2)], axis=0)
            dp = jnp.where(prev_part, dp_prev, dp_cur)
            delta = jnp.sum(probs * dp, axis=1, keepdims=True)
            ds_prev, ds_cur = _split_folded(probs * (dp - delta) * (HEAD_DIM ** -0.5), prev_part)
            p_prev, p_cur = _split_folded(probs, prev_part)
            dsink_rows = -(p_sink * delta)
            heads = [h for g in groups for h in _group_heads(g)]
            for i, h in enumerate(heads):
                dsink_ref[:, h:h + 1] += jnp.sum(dsink_rows[i * ATTN_BLOCK:(i + 1) * ATTN_BLOCK], axis=0,
                                                 keepdims=True)
            kv_grads = []
            for g in groups:
                even, odd = slice(2 * g * half, (2 * g + 1) * half), slice((2 * g + 1) * half, (2 * g + 2) * half)
                dq = (jnp.dot(ds_prev[even], k_prev[g][0], preferred_element_type=F32)
                      + jnp.dot(ds_cur[even], k_cur[g][0], preferred_element_type=F32)
                      + jnp.dot(ds_prev[odd], k_prev[g][1], preferred_element_type=F32)
                      + jnp.dot(ds_cur[odd], k_cur[g][1], preferred_element_type=F32))
                for i in range(PAIRS_PER_KV):
                    hp = g * PAIRS_PER_KV + i
                    dq_scr[:, hp * 2 * HEAD_DIM:(hp + 1) * 2 * HEAD_DIM] = dq[i * ATTN_BLOCK:(i + 1) * ATTN_BLOCK]
                kv_grads.append((kv_grad(ds_prev[even], ds_prev[odd], q[g]), kv_grad(ds_cur[even], ds_cur[odd], q[g]),
                                 kv_grad(p_prev[even], p_prev[odd], do[g]), kv_grad(p_cur[even], p_cur[odd], do[g])))
            (dkp0, dkc0, dvp0, dvc0), (dkp1, dkc1, dvp1, dvc1) = kv_grads
            prev_scr[:, :kw] = jnp.where(upper, dkp1, dkp0)
            prev_scr[:, kw:] = jnp.where(upper, dvp1, dvp0)
            cur_scr[:, :kw] = jnp.where(upper, dkc1, dkc0)
            cur_scr[:, kw:] = jnp.where(upper, dvc1, dvc0)
            dq_pre = _rope_transposed(dq_scr[...], cos_ref, sin_ref)
            dq_ref[...] = dq_pre.astype(BF16)
            dbq_ref[...] += jnp.sum(dq_pre, axis=0, keepdims=True)

        tot = carry[...] + prev_scr[...]
        dk_pre = _rope_transposed(tot[:, :kw], cosp_ref, sinp_ref)
        dkv_ref[:, :kw] = dk_pre.astype(BF16)
        dkv_ref[:, kw:] = tot[:, kw:].astype(BF16)
        dbkv_ref[:, :kw] += jnp.sum(dk_pre, axis=0, keepdims=True)
        dbkv_ref[:, kw:] += jnp.sum(tot[:, kw:], axis=0, keepdims=True)

        @pl.when(n < nb)
        def _():
            carry[...] = cur_scr[...]

    cur = lambda n: (jnp.minimum(n, nb - 1), 0)
    out_lag = lambda n: (jnp.maximum(n - 1, 0), 0)
    return pl.pallas_call(
        body, name="attn_bwd", grid=(nb + 1,),
        in_specs=[*_attn_specs(T),
                  pl.BlockSpec((ATTN_BLOCK, qw), cur),
                  pl.BlockSpec((ATTN_BLOCK, V7X_LANES), cur), pl.BlockSpec((ATTN_BLOCK, V7X_LANES), cur),
                  pl.BlockSpec((ATTN_BLOCK, V7X_LANES), out_lag), pl.BlockSpec((ATTN_BLOCK, V7X_LANES), out_lag),
                  _whole((1, N_Q_HEADS))],
        out_specs=[pl.BlockSpec((ATTN_BLOCK, qw), cur), pl.BlockSpec((ATTN_BLOCK, 2 * kw), out_lag),
                   _whole((1, N_Q_HEADS)), _whole((1, qw)), _whole((1, 2 * kw))],
        out_shape=[SDS((T, qw), BF16), SDS((T, 2 * kw), BF16),
                   SDS((1, N_Q_HEADS), F32), SDS((1, qw), F32), SDS((1, 2 * kw), F32)],
        scratch_shapes=[pltpu.VMEM((ATTN_BLOCK, 2 * kw), F32), pltpu.VMEM((ATTN_BLOCK, 2 * kw), F32),
                        pltpu.VMEM((ATTN_BLOCK, 2 * kw), F32), pltpu.VMEM((ATTN_BLOCK, qw), F32)],
        compiler_params=_params("arbitrary"),
    )(qkv, qkv, qkv, qkv, qkv, dao, cos, sin, cos, sin, sinks)


def _local_step(x, target, p, reduce_large):
    T, D = x.shape
    cos, sin = _rope_tables(T)
    qw = N_Q_HEADS * HEAD_DIM
    nm, nf = p["norm_mix"], p["norm_ffn"]

    y0, qkv = _qkv_proj(x, nm[0:1], p["attn_w_qkv"], p["attn_b_qkv"], cos, sin, p["gather_started"])
    ao = _attn_fwd(qkv, p["attn_sinks"])
    h1, f0 = _mm_res("attn_out", ao, p["attn_w_o"], p["attn_b_o"], x, nf[0:1])
    p = {**p, **p["other_weights"](h1)}
    w1, w3, w2 = p["ffn_w1"], p["ffn_w3"], p["ffn_w2"]
    act0, gg0, s0 = _ffn_up("ffn0_up", f0, w1, w3, 0)
    h2 = _ffn_down("ffn0_down", s0, w2, 0, h1)
    y1, a = _pw1_proj(h2, nm[1:2], p["conv_w_pw1"], p["conv_b_pw1"])
    c, act = _conv_fwd(a, p["conv_w_dw"], p["conv_b_dw"], p["conv_ln_g"], p["conv_ln_b"])
    h3, f1 = _mm_res("conv_out", act, p["conv_w_pw2"], p["conv_b_pw2"], h2, nf[1:2])
    act1, gg1, s1 = _ffn_up("ffn1_up", f1, w1, w3, 1)
    h4 = _ffn_down("ffn1_down", s1, w2, 1, h3)
    dh4, loss, d_norm_final = _final_loss(h4, p["norm_final"], target)

    g = {}
    dg1, dg3 = _ffn_bwd_act("ffn1_bwd_act", dh4, w2, 1, act1, gg1)
    dw2_1 = _mm_tn("ffn1_dw2", s1, dh4)
    dw1_1 = _mm_tn("ffn1_dw1", f1, dg1)
    dw3_1 = _mm_tn("ffn1_dw3", f1, dg3)
    started = reduce_large("ffn1", {("ffn_w1", 1): dw1_1, ("ffn_w3", 1): dw3_1, ("ffn_w2", 1): dw2_1})
    dh3, dnf1, db_pw2 = _mm_nt_normbwd("ffn1_bwd_in", [(dg1, w1, 1), (dg3, w3, 1)], h3, nf[1:2], dh4, started)

    dw_pw2 = _mm_tn("conv_dw_pw2", act, dh3)
    dact = _mm_nt("conv_bwd_out", dh3, p["conv_w_pw2"], F32)
    da, d_ln_g, d_ln_b, d_b_dw, d_w_dw, d_b_pw1 = _conv_bwd(dact, c, a, p["conv_w_dw"], p["conv_ln_g"], p["conv_ln_b"])
    dw_pw1 = _mm_tn("conv_dw_pw1", y1, da, col_chunks=N_CHIPS)
    started = reduce_large("conv", {("conv_w_pw2", 0): dw_pw2.reshape(N_CHIPS, -1, D), ("conv_w_pw1", 0): dw_pw1})
    dh2, dnm1, _ = _mm_nt_normbwd("conv_bwd_in", [(da, p["conv_w_pw1"], None)], h2, nm[1:2], dh3, started)

    dg1, dg3 = _ffn_bwd_act("ffn0_bwd_act", dh2, w2, 0, act0, gg0)
    dw2_0 = _mm_tn("ffn0_dw2", s0, dh2)
    dw1_0 = _mm_tn("ffn0_dw1", f0, dg1)
    dw3_0 = _mm_tn("ffn0_dw3", f0, dg3)
    started = reduce_large("ffn0", {("ffn_w1", 0): dw1_0, ("ffn_w3", 0): dw3_0, ("ffn_w2", 0): dw2_0})
    dh1, dnf0, db_o = _mm_nt_normbwd("ffn0_bwd_in", [(dg1, w1, 0), (dg3, w3, 0)], h1, nf[0:1], dh2, started)

    dw_o = _mm_tn("attn_dw_o", ao, dh1)
    dao = _mm_nt("attn_bwd_out", dh1, p["attn_w_o"], BF16)
    dq, dkv, d_sinks, dbq, dbkv = _attn_bwd(qkv, dao, cos, sin, p["attn_sinks"])
    dwq = _mm_tn("attn_dw_q", y0, dq)
    dwkv = _mm_tn("attn_dw_kv", y0, dkv)
    wqkv = p["attn_w_qkv"]
    n_qkv = wqkv.shape[1] // N_CHIPS
    dwqkv = jnp.moveaxis(jnp.concatenate([dwq, dwkv], axis=1).reshape(D, N_CHIPS, n_qkv), 1, 0)
    started = reduce_large("attn", {("attn_w_o", 0): dw_o.reshape(N_CHIPS, -1, D), ("attn_w_qkv", 0): dwqkv})
    dx, dnm0, _ = _mm_nt_normbwd("attn_bwd_in", [(dq, wqkv[:, :qw], None), (dkv, wqkv[:, qw:], None)], x, nm[0:1], dh1,
                                 started)

    g["norm_mix"] = jnp.concatenate([dnm0, dnm1], axis=0)
    g["norm_ffn"] = jnp.concatenate([dnf0, dnf1], axis=0)
    g["attn_b_qkv"] = jnp.concatenate([dbq, dbkv], axis=1)
    g["attn_sinks"] = d_sinks
    g["attn_b_o"] = db_o
    g["conv_b_pw1"] = d_b_pw1
    g["conv_w_dw"] = d_w_dw[:CONV_WIDTH]
    g["conv_b_dw"] = d_b_dw
    g["conv_ln_g"] = d_ln_g
    g["conv_ln_b"] = d_ln_b
    g["conv_b_pw2"] = db_pw2
    g["norm_final"] = d_norm_final
    return loss, dx, g


ANY = pl.BlockSpec(memory_space=pl.ANY)
VMEM_WHOLE = pl.BlockSpec(memory_space=pltpu.VMEM)


def _my_place():
    return lax.axis_index("x"), lax.axis_index("y"), lax.axis_index("c")


def _other_chips(x, y):
    places = [(1 - x, y), (x, 1 - y), (1 - x, 1 - y)]
    return [(bx, by, 2 * bx + by) for bx, by in places]


def _exchange_small(name, v, reduce):
    r, w = v.shape

    def body(v_ref, o_ref, *rest):
        if reduce:
            buf, send_sems, recv_sems = rest
        else:
            buf = o_ref
            send_sems, recv_sems = rest
        x, y, c = _my_place()
        me = 4 * x + 2 * y + c
        sends = []
        for k in range(1, N_DEV):
            peer = (1 - x if k & 4 else x, 1 - y if k & 2 else y, 1 - c if k & 1 else c)
            cp = pltpu.make_async_remote_copy(
                src_ref=v_ref, dst_ref=buf.at[me], send_sem=send_sems.at[k - 1], recv_sem=recv_sems.at[k - 1],
                device_id=peer, device_id_type=MESH)
            cp.start()
            sends.append(cp)
        buf[me] = v_ref[...]
        for k in range(1, N_DEV):
            src = 4 * (1 - x if k & 4 else x) + 2 * (1 - y if k & 2 else y) + (1 - c if k & 1 else c)
            pltpu.make_async_remote_copy(
                src_ref=v_ref, dst_ref=buf.at[src], send_sem=send_sems.at[k - 1], recv_sem=recv_sems.at[k - 1],
                device_id=(x, y, c), device_id_type=MESH).wait_recv()
        for cp in sends:
            cp.wait_send()
        if reduce:
            acc = buf[0]
            for d in range(1, N_DEV):
                acc = acc + buf[d]
            o_ref[...] = acc

    sems = [pltpu.SemaphoreType.DMA((N_DEV - 1,)), pltpu.SemaphoreType.DMA((N_DEV - 1,))]
    if reduce:
        out_shape = SDS((r, w), F32)
        scratch = [pltpu.VMEM((N_DEV, r, w), F32)] + sems
    else:
        out_shape = SDS((N_DEV, r, w), F32)
        scratch = sems
    return pl.pallas_call(
        body, name=name, out_shape=out_shape, in_specs=[VMEM_WHOLE], out_specs=VMEM_WHOLE,
        scratch_shapes=scratch,
        compiler_params=pltpu.CompilerParams(vmem_limit_bytes=V7X_VMEM_LIMIT_BYTES),
    )(v)


def _cast_into_slot(name, gathered, shard, chip_idx):
    rows, cols = shard.shape
    tr = _pack_row_tile(rows)

    def body(k_ref, s_ref, g_ref, o_ref):
        o_ref[...] = s_ref[...].astype(BF16)

    return pl.pallas_call(
        body, name=name,
        grid_spec=pltpu.PrefetchScalarGridSpec(
            num_scalar_prefetch=1, grid=(rows // tr,),
            in_specs=[pl.BlockSpec((tr, cols), lambda i, k_ref: (i, 0)), pl.BlockSpec(memory_space=pl.ANY)],
            out_specs=pl.BlockSpec((None, tr, cols), lambda i, k_ref: (k_ref[0], i, 0))),
        out_shape=SDS(gathered.shape, BF16),
        input_output_aliases={2: 0},
        compiler_params=_params("parallel"),
    )(chip_idx, shard, gathered)


def _row_halves(ref, c):
    half = ref.shape[1] // 2
    return pl.ds(pl.multiple_of(c * half, 16), half), pl.ds(pl.multiple_of((1 - c) * half, 16), half)


def _gather_ici_copies(refs, send_sems, recv_sems):
    x, y, c = _my_place()
    k = 2 * x + y
    pairs = []
    for i, ref in enumerate(refs):
        mine, _ = _row_halves(ref, c)
        for j, (bx, by, kb) in enumerate(_other_chips(x, y)):
            sems = dict(send_sem=send_sems.at[3 * i + j], recv_sem=recv_sems.at[3 * i + j], device_id_type=MESH)
            send = pltpu.make_async_remote_copy(src_ref=ref.at[k, mine], dst_ref=ref.at[k, mine],
                                                device_id=(bx, by, c), **sems)
            arrival = pltpu.make_async_remote_copy(src_ref=ref.at[kb, mine], dst_ref=ref.at[kb, mine],
                                                   device_id=(bx, by, c), **sems)
            pairs.append((send, arrival))
    return pairs


def _gather_d2d_copies(refs, send_sems, recv_sems, first_sem):
    x, y, c = _my_place()
    pairs = []
    for i, ref in enumerate(refs):
        mine, theirs = _row_halves(ref, c)
        for j, (_, _, kb) in enumerate(_other_chips(x, y)):
            sem = first_sem + 3 * i + j
            sems = dict(send_sem=send_sems.at[sem], recv_sem=recv_sems.at[sem], device_id=(x, y, 1 - c),
                        device_id_type=MESH)
            send = pltpu.make_async_remote_copy(src_ref=ref.at[kb, mine], dst_ref=ref.at[kb, mine], **sems)
            arrival = pltpu.make_async_remote_copy(src_ref=ref.at[kb, theirs], dst_ref=ref.at[kb, theirs], **sems)
            pairs.append((send, arrival))
    return pairs


def _run_copies(pairs):
    for send, _ in pairs:
        send.start()
    for send, arrival in pairs:
        send.wait_send()
        arrival.wait_recv()


def _gather_now(name, gathered):
    n_w = len(gathered)

    def body(*refs):
        in_refs = refs[:n_w]
        send_sems, recv_sems = refs[2 * n_w:]
        _run_copies(_gather_ici_copies(in_refs, send_sems, recv_sems))
        _run_copies(_gather_d2d_copies(in_refs, send_sems, recv_sems, 3 * n_w))

    return pl.pallas_call(
        body, name=name, out_shape=[SDS(g.shape, g.dtype) for g in gathered],
        in_specs=[ANY] * n_w, out_specs=[ANY] * n_w, input_output_aliases={i: i for i in range(n_w)},
        scratch_shapes=[pltpu.SemaphoreType.DMA((6 * n_w,)), pltpu.SemaphoreType.DMA((6 * n_w,))],
    )(*gathered)


def _gather_start(name, gathered):
    n_w = len(gathered)

    def body(*refs):
        in_refs = refs[:n_w]
        send_sems, recv_sems = refs[n_w:n_w + 2]
        for send, _ in _gather_ici_copies(in_refs, send_sems, recv_sems):
            send.start()
        refs[-1][...] = jnp.zeros_like(refs[-1])

    out = pl.pallas_call(
        body, name=name,
        out_shape=(pltpu.SemaphoreType.DMA((3 * n_w,)), pltpu.SemaphoreType.DMA((3 * n_w,)),
                   *[pltpu.HBM(g.shape, g.dtype) for g in gathered], SDS((8, V7X_LANES), F32)),
        in_specs=[HBM_SPEC] * n_w, out_specs=(SEM_SPEC, SEM_SPEC, *[HBM_SPEC] * n_w, VMEM_WHOLE),
        input_output_aliases={i: 2 + i for i in range(n_w)},
        compiler_params=pltpu.CompilerParams(has_side_effects=DATAFLOW),
    )(*[pltpu.with_memory_space_constraint(g, pltpu.HBM) for g in gathered])
    return out[0], out[1], list(out[2:2 + n_w]), out[-1]


def _gather_wait(name, send_sems, recv_sems, gathered, after):
    n_w = len(gathered)

    def body(*refs):
        in_refs = refs[:n_w]
        send_sems, recv_sems = refs[n_w:n_w + 2]
        for send, arrival in _gather_ici_copies(in_refs, send_sems, recv_sems):
            send.wait_send()
            arrival.wait_recv()

    out = pl.pallas_call(
        body, name=name, out_shape=tuple(pltpu.HBM(g.shape, g.dtype) for g in gathered),
        in_specs=[*[HBM_SPEC] * n_w, SEM_SPEC, SEM_SPEC, ANY], out_specs=tuple([HBM_SPEC] * n_w),
        input_output_aliases={i: i for i in range(n_w)},
        compiler_params=pltpu.CompilerParams(has_side_effects=DATAFLOW),
    )(*gathered, send_sems, recv_sems, after)
    return list(out)


def _swap_fetched_with_sibling(name, gathered):
    n_w = len(gathered)

    def body(*refs):
        in_refs = refs[:n_w]
        send_sems, recv_sems = refs[2 * n_w:]
        _run_copies(_gather_d2d_copies(in_refs, send_sems, recv_sems, 0))

    return pl.pallas_call(
        body, name=name, out_shape=[SDS(g.shape, g.dtype) for g in gathered],
        in_specs=[ANY] * n_w, out_specs=[ANY] * n_w, input_output_aliases={i: i for i in range(n_w)},
        scratch_shapes=[pltpu.SemaphoreType.DMA((3 * n_w,)), pltpu.SemaphoreType.DMA((3 * n_w,))],
    )(*gathered)


def _swap_halves_with_sibling(name, grads):
    n_g = len(grads)

    def body(*refs):
        g_refs, o_refs = refs[:n_g], refs[n_g:2 * n_g]
        send_sems, recv_sems = refs[2 * n_g:]
        x, y, c = _my_place()
        copies = []
        for i in range(n_g):
            half = g_refs[i].shape[1] // 2
            theirs = pl.ds(pl.multiple_of((1 - c) * half, 8), half)
            cp = pltpu.make_async_remote_copy(
                src_ref=g_refs[i].at[:, theirs], dst_ref=o_refs[i], send_sem=send_sems.at[i],
                recv_sem=recv_sems.at[i], device_id=(x, y, 1 - c), device_id_type=MESH)
            cp.start()
            copies.append(cp)
        for cp in copies:
            cp.wait()

    return pl.pallas_call(
        body, name=name,
        out_shape=[SDS((g.shape[0], g.shape[1] // 2, g.shape[2]), g.dtype) for g in grads],
        in_specs=[ANY] * n_g, out_specs=[ANY] * n_g,
        scratch_shapes=[pltpu.SemaphoreType.DMA((n_g,)), pltpu.SemaphoreType.DMA((n_g,))],
    )(*grads)


def _pack_row_tile(rows):
    for t in range(min(rows, 512), 7, -1):
        if rows % t == 0 and t % 8 == 0:
            return t
    return rows


def _add_sibling_half(name, grads, from_sibling, c_idx):
    n, R, w = grads.shape
    half = R // 2
    tr = _pack_row_tile(half)
    steps = half // tr

    def body(c_ref, g_ref, s_ref, o_ref):
        o_ref[...] = g_ref[...] + s_ref[...]

    return pl.pallas_call(
        body, name=name,
        grid_spec=pltpu.PrefetchScalarGridSpec(
            num_scalar_prefetch=1, grid=(n, steps),
            in_specs=[pl.BlockSpec((1, tr, w), lambda j, i, c_ref: (j, c_ref[0] * steps + i, 0)),
                      pl.BlockSpec((1, tr, w), lambda j, i, c_ref: (j, i, 0))],
            out_specs=pl.BlockSpec((1, tr, w), lambda j, i, c_ref: (j, i, 0))),
        out_shape=SDS((n, half, w), F32),
        compiler_params=_params("parallel", "parallel"),
    )(c_idx, grads, from_sibling)


HBM_SPEC = pl.BlockSpec(memory_space=pltpu.HBM)
SEM_SPEC = pl.BlockSpec(memory_space=pltpu.SEMAPHORE)
DATAFLOW = pltpu.SideEffectType.DATAFLOW_SIDE_EFFECTING


def _chip_scatter_copies(p_refs, land_refs, send_sems, recv_sems):
    x, y, c = _my_place()
    return [pltpu.make_async_remote_copy(
        src_ref=p_refs[i].at[kb], dst_ref=land_refs[i].at[j], send_sem=send_sems.at[3 * i + j],
        recv_sem=recv_sems.at[3 * i + j], device_id=(bx, by, c), device_id_type=MESH)
        for i in range(len(p_refs)) for j, (bx, by, kb) in enumerate(_other_chips(x, y))]


def _scatter_start(name, partials):
    n_p = len(partials)

    def body(*refs):
        p_refs, land_refs = refs[:n_p], refs[n_p:2 * n_p]
        send_sems, recv_sems = refs[2 * n_p:2 * n_p + 2]
        for cp in _chip_scatter_copies(p_refs, land_refs, send_sems, recv_sems):
            cp.start()
        refs[-1][...] = jnp.zeros_like(refs[-1])

    lands = [pltpu.with_memory_space_constraint(lax.empty((N_CHIPS - 1,) + p.shape[1:], p.dtype), pltpu.HBM)
             for p in partials]
    out = pl.pallas_call(
        body, name=name,
        out_shape=(pltpu.SemaphoreType.DMA((3 * n_p,)), pltpu.SemaphoreType.DMA((3 * n_p,)),
                   *[pltpu.HBM(p.shape, p.dtype) for p in partials], *[pltpu.HBM(l.shape, l.dtype) for l in lands],
                   SDS((8, V7X_LANES), F32)),
        in_specs=[HBM_SPEC] * (2 * n_p), out_specs=(SEM_SPEC, SEM_SPEC, *[HBM_SPEC] * (2 * n_p), VMEM_WHOLE),
        input_output_aliases={i: 2 + i for i in range(2 * n_p)},
        compiler_params=pltpu.CompilerParams(has_side_effects=DATAFLOW),
    )(*[pltpu.with_memory_space_constraint(p, pltpu.HBM) for p in partials], *lands)
    return out[0], out[1], list(out[2:2 + n_p]), list(out[2 + n_p:2 + 2 * n_p]), out[-1]


def _scatter_wait(name, send_sems, recv_sems, partials, lands, after):
    n_p = len(partials)

    def body(*refs):
        p_refs, land_refs = refs[:n_p], refs[n_p:2 * n_p]
        send_sems, recv_sems = refs[2 * n_p:2 * n_p + 2]
        for cp in _chip_scatter_copies(p_refs, land_refs, send_sems, recv_sems):
            cp.wait_send()
            cp.wait_recv()

    out = pl.pallas_call(
        body, name=name,
        out_shape=(*[pltpu.HBM(p.shape, p.dtype) for p in partials], *[pltpu.HBM(l.shape, l.dtype) for l in lands]),
        in_specs=[*[HBM_SPEC] * (2 * n_p), SEM_SPEC, SEM_SPEC, ANY], out_specs=tuple([HBM_SPEC] * (2 * n_p)),
        input_output_aliases={i: i for i in range(2 * n_p)},
        compiler_params=pltpu.CompilerParams(has_side_effects=DATAFLOW),
    )(*partials, *lands, send_sems, recv_sems, after)
    return list(out[:n_p]), list(out[n_p:])


def _sum_chip_partials(name, partial, received, shard, layer, place):
    n, half, w = partial.shape
    tr = _pack_row_tile(half)
    steps = half // tr

    def body(place_ref, p_ref, r_ref, shard_ref, o_ref):
        o_ref[...] = ((p_ref[0] + r_ref[0]) + r_ref[1]) + r_ref[2]

    return pl.pallas_call(
        body, name=name,
        grid_spec=pltpu.PrefetchScalarGridSpec(
            num_scalar_prefetch=1, grid=(steps,),
            in_specs=[pl.BlockSpec((1, tr, w), lambda i, place_ref: (place_ref[0], i, 0)),
                      pl.BlockSpec((n - 1, tr, w), lambda i, place_ref: (0, i, 0)),
                      pl.BlockSpec(memory_space=pl.ANY)],
            out_specs=pl.BlockSpec((tr, w), lambda i, place_ref: ((2 * layer + place_ref[1]) * steps + i, 0))),
        out_shape=SDS(shard.shape, F32),
        input_output_aliases={3: 0},
        compiler_params=_params("parallel"),
    )(place, partial, received, shard)


def _join_halves(shards, layers):
    n_s = len(shards)
    n_sem = sum(layers)

    def body(*refs):
        in_refs = refs[:n_s]
        send_sems, recv_sems = refs[2 * n_s:]
        x, y, c = _my_place()
        copies, sem = [], 0
        for ref, n_layers in zip(in_refs, layers):
            half = ref.shape[0] // (2 * n_layers)
            for layer in range(n_layers):
                mine = pl.ds(pl.multiple_of(layer * 2 * half + c * half, 8), half)
                theirs = pl.ds(pl.multiple_of(layer * 2 * half + (1 - c) * half, 8), half)
                send = pltpu.make_async_remote_copy(
                    src_ref=ref.at[mine], dst_ref=ref.at[mine], send_sem=send_sems.at[sem], recv_sem=recv_sems.at[sem],
                    device_id=(x, y, 1 - c), device_id_type=MESH)
                send.start()
                arrival = pltpu.make_async_remote_copy(
                    src_ref=ref.at[theirs], dst_ref=ref.at[theirs], send_sem=send_sems.at[sem],
                    recv_sem=recv_sems.at[sem], device_id=(x, y, 1 - c), device_id_type=MESH)
                copies.append((send, arrival))
                sem += 1
        for send, arrival in copies:
            send.wait_send()
            arrival.wait_recv()

    return pl.pallas_call(
        body, name="join_halves", out_shape=[SDS(s.shape, s.dtype) for s in shards],
        in_specs=[ANY] * n_s, out_specs=[ANY] * n_s,
        input_output_aliases={i: i for i in range(n_s)},
        scratch_shapes=[pltpu.SemaphoreType.DMA((n_sem,)), pltpu.SemaphoreType.DMA((n_sem,))],
    )(*shards)


def _adamw(name, w, g, m, v):
    rows, width = w.shape
    tr = _pack_row_tile(rows)

    def body(w_ref, g_ref, m_ref, v_ref, g_out_ref, d_ref, nm_ref, nv_ref):
        gg = g_ref[...]
        g_out_ref[...] = gg
        m_new = ADAM_B1 * m_ref[...] + (1.0 - ADAM_B1) * gg
        v_new = ADAM_B2 * v_ref[...] + (1.0 - ADAM_B2) * (gg * gg)
        m_hat = m_new / (1.0 - ADAM_B1 ** ADAM_STEP)
        v_hat = v_new / (1.0 - ADAM_B2 ** ADAM_STEP)
        d_ref[...] = -ADAM_LR * (m_hat / (jnp.sqrt(v_hat) + ADAM_EPS) + ADAM_WD * w_ref[...])
        nm_ref[...] = m_new
        nv_ref[...] = v_new

    spec = _rows(tr, width)
    return pl.pallas_call(
        body, name=name, grid=(rows // tr,),
        in_specs=[spec] * 4, out_specs=[spec] * 4,
        out_shape=[SDS((rows, width), F32)] * 4,
        compiler_params=_params("parallel"),
    )(w, g, m, v)


WEIGHT_NAMES = ['norm_mix', 'norm_ffn', 'attn_w_qkv', 'attn_b_qkv', 'attn_sinks', 'attn_w_o', 'attn_b_o',
                'conv_w_pw1', 'conv_b_pw1', 'conv_w_dw', 'conv_b_dw', 'conv_ln_g', 'conv_ln_b', 'conv_w_pw2',
                'conv_b_pw2', 'ffn_w1', 'ffn_w3', 'ffn_w2', 'norm_final']
BIG = ['attn_w_qkv', 'attn_w_o', 'conv_w_pw1', 'conv_w_pw2', 'ffn_w1', 'ffn_w3', 'ffn_w2']
COLUMN_SPLIT = ('attn_w_qkv', 'conv_w_pw1', 'ffn_w1', 'ffn_w3')
SMALL_SPLIT = ['conv_b_pw1', 'conv_w_dw', 'conv_b_dw', 'conv_ln_g', 'conv_ln_b', 'conv_b_pw2']
SMALL_WHOLE = ['norm_mix', 'norm_ffn', 'attn_b_qkv', 'attn_sinks', 'attn_b_o', 'norm_final']


def _pack_rows(arrays, dtype, row_multiple):
    flat = jnp.concatenate([a.astype(dtype).reshape(-1) for a in arrays])
    rows = -(-flat.shape[0] // PACK_W)
    rows = -(-rows // row_multiple) * row_multiple
    return jnp.pad(flat, (0, rows * PACK_W - flat.shape[0])).reshape(rows, PACK_W)


def _unpack_rows(pack, shapes):
    flat = pack.reshape(-1)
    out, at = [], 0
    for shape in shapes:
        size = 1
        for s in shape:
            size *= s
        out.append(flat[at:at + size].reshape(shape))
        at += size
    return out


def _join_chip_axis(name, parts):
    axis = parts.ndim - 1 if name in COLUMN_SPLIT or name in SMALL_SPLIT else parts.ndim - 2
    moved = jnp.moveaxis(parts, 0, axis - 1)
    shape = list(moved.shape)
    shape[axis - 1:axis + 1] = [shape[axis - 1] * shape[axis]]
    return moved.reshape(shape)


def _split_chip_axis(name, whole, shard_shape):
    axis = len(shard_shape) - 1 if name in COLUMN_SPLIT or name in SMALL_SPLIT else len(shard_shape) - 2
    shape = list(whole.shape)
    shape[axis:axis + 1] = [N_CHIPS, shard_shape[axis]]
    return jnp.moveaxis(whole.reshape(shape), axis, 0)


def kernel(x, norm_mix, norm_ffn, attn_w_qkv, attn_b_qkv, attn_sinks, attn_w_o, attn_b_o, conv_w_pw1, conv_b_pw1, conv_w_dw, conv_b_dw, conv_ln_g, conv_ln_b, conv_w_pw2, conv_b_pw2, ffn_w1, ffn_w3, ffn_w2, norm_final, loss_target, m_norm_mix, m_norm_ffn, m_attn_w_qkv, m_attn_b_qkv, m_attn_sinks, m_attn_w_o, m_attn_b_o, m_conv_w_pw1, m_conv_b_pw1, m_conv_w_dw, m_conv_b_dw, m_conv_ln_g, m_conv_ln_b, m_conv_w_pw2, m_conv_b_pw2, m_ffn_w1, m_ffn_w3, m_ffn_w2, m_norm_final, v_norm_mix, v_norm_ffn, v_attn_w_qkv, v_attn_b_qkv, v_attn_sinks, v_attn_w_o, v_attn_b_o, v_conv_w_pw1, v_conv_b_pw1, v_conv_w_dw, v_conv_b_dw, v_conv_ln_g, v_conv_ln_b, v_conv_w_pw2, v_conv_b_pw2, v_ffn_w1, v_ffn_w3, v_ffn_w2, v_norm_final):
    w = dict(zip(WEIGHT_NAMES, (norm_mix, norm_ffn, attn_w_qkv, attn_b_qkv, attn_sinks, attn_w_o, attn_b_o,
                                conv_w_pw1, conv_b_pw1, conv_w_dw, conv_b_dw, conv_ln_g, conv_ln_b, conv_w_pw2,
                                conv_b_pw2, ffn_w1, ffn_w3, ffn_w2, norm_final)))
    m = dict(zip(WEIGHT_NAMES, (m_norm_mix, m_norm_ffn, m_attn_w_qkv, m_attn_b_qkv, m_attn_sinks, m_attn_w_o,
                                m_attn_b_o, m_conv_w_pw1, m_conv_b_pw1, m_conv_w_dw, m_conv_b_dw, m_conv_ln_g,
                                m_conv_ln_b, m_conv_w_pw2, m_conv_b_pw2, m_ffn_w1, m_ffn_w3, m_ffn_w2, m_norm_final)))
    v = dict(zip(WEIGHT_NAMES, (v_norm_mix, v_norm_ffn, v_attn_w_qkv, v_attn_b_qkv, v_attn_sinks, v_attn_w_o,
                                v_attn_b_o, v_conv_w_pw1, v_conv_b_pw1, v_conv_w_dw, v_conv_b_dw, v_conv_ln_g,
                                v_conv_ln_b, v_conv_w_pw2, v_conv_b_pw2, v_ffn_w1, v_ffn_w3, v_ffn_w2, v_norm_final)))
    T, D = x.shape[1], x.shape[2]
    c_idx = lax.axis_index("c").astype(jnp.int32).reshape(1)
    chip = (2 * lax.axis_index("x") + lax.axis_index("y")).astype(jnp.int32)

    as_rows = lambda a: a.reshape(-1, a.shape[-1])
    slabs = {n: _cast_into_slot(f"cast_{n}", lax.empty((N_CHIPS,) + as_rows(w[n]).shape, BF16), as_rows(w[n]),
                                chip.reshape(1)) for n in BIG}
    first, later = BIG[:2], BIG[2:]
    qkv_parts, w_o_parts = _gather_now("gather_attn", [slabs[n] for n in first])
    send_sems, recv_sems, travelling, gather_started = _gather_start("gather_start", [slabs[n] for n in later])
    layers = ffn_w1.shape[0]
    small_shapes = [w[n].shape for n in SMALL_SPLIT]
    small_all = _exchange_small("gather_small", _pack_rows([w[n] for n in SMALL_SPLIT], F32, 8), reduce=False)
    per_chip = [_unpack_rows(small_all[2 * j], small_shapes) for j in range(N_CHIPS)]
    full = {}
    for i, n in enumerate(SMALL_SPLIT):
        full[n] = _join_chip_axis(n, jnp.stack([per_chip[j][i] for j in range(N_CHIPS)]))

    def other_weights(after):
        landed = _gather_wait("gather_wait", send_sems, recv_sems, travelling, after)
        gathered = dict(zip(later, _swap_fetched_with_sibling("gather_swap", landed)))
        return {"conv_w_pw1": gathered["conv_w_pw1"], "conv_w_pw2": gathered["conv_w_pw2"].reshape(-1, D),
                "ffn_w1": gathered["ffn_w1"].reshape(N_CHIPS, layers, D, -1),
                "ffn_w3": gathered["ffn_w3"].reshape(N_CHIPS, layers, D, -1),
                "ffn_w2": gathered["ffn_w2"].reshape(N_CHIPS, layers, -1, D)}

    p = {
        "norm_mix": norm_mix, "norm_ffn": norm_ffn, "norm_final": norm_final.reshape(1, D),
        "attn_w_qkv": jnp.moveaxis(qkv_parts, 0, 1).reshape(D, -1), "attn_b_qkv": attn_b_qkv,
        "attn_sinks": attn_sinks, "attn_w_o": w_o_parts.reshape(-1, D), "attn_b_o": attn_b_o,
        "conv_b_pw1": full["conv_b_pw1"], "conv_w_dw": full["conv_w_dw"][0],
        "conv_b_dw": full["conv_b_dw"], "conv_ln_g": full["conv_ln_g"], "conv_ln_b": full["conv_ln_b"],
        "conv_b_pw2": full["conv_b_pw2"], "gather_started": gather_started, "other_weights": other_weights,
    }
    in_flight = []

    def reduce_large(tag, grads):
        keys = list(grads)
        from_sibling = _swap_halves_with_sibling(f"grads_to_sibling_{tag}", [grads[k] for k in keys])
        partials = [_add_sibling_half(f"add_sibling_half_{tag}{i}", grads[k], fs, c_idx)
                    for i, (k, fs) in enumerate(zip(keys, from_sibling))]
        *handles, started = _scatter_start(f"scatter_start_{tag}", partials)
        in_flight.append((tag, keys, handles))
        return started

    loss_part, dx, g = _local_step(x[0], loss_target[0], p, reduce_large)
    for n in SMALL_WHOLE + SMALL_SPLIT:
        g[n] = g[n].reshape((-1,) + g[n].shape[-2:]) if w[n].ndim == 3 else g[n].reshape(w[n].shape[:-1] + (-1,))

    place = jnp.stack([chip, c_idx[0]])
    shard_grad = {n: lax.empty(as_rows(w[n]).shape, F32) for n in BIG}
    for tag, keys, (send_sems, recv_sems, partials, lands) in in_flight:
        partials, received = _scatter_wait(f"scatter_wait_{tag}", send_sems, recv_sems, partials, lands, dx)
        for i, (n, layer) in enumerate(keys):
            shard_grad[n] = _sum_chip_partials(f"sum_chip_partials_{tag}{i}", partials[i], received[i],
                                               shard_grad[n], layer, place)
    g_big = dict(zip(BIG, _join_halves([shard_grad[n] for n in BIG], [w[n].shape[0] for n in BIG])))
    big_out = {}
    for n in BIG:
        step = _adamw(f"adamw_{n}", as_rows(w[n]), g_big[n], as_rows(m[n]), as_rows(v[n]))
        big_out[n] = [a.reshape(w[n].shape) for a in step]

    small_whole_shapes = [w[n].shape for n in SMALL_WHOLE]
    small_full_shapes = [g[n].shape for n in SMALL_SPLIT]
    reduced = _exchange_small(
        "reduce_small", _pack_rows([loss_part] + [g[n] for n in SMALL_WHOLE] + [g[n] for n in SMALL_SPLIT], F32, 8),
        reduce=True)
    pieces = _unpack_rows(reduced, [(1,)] + small_whole_shapes + small_full_shapes)
    loss = pieces[0].reshape(())
    g_small = dict(zip(SMALL_WHOLE, pieces[1:1 + len(SMALL_WHOLE)]))
    for n, whole in zip(SMALL_SPLIT, pieces[1 + len(SMALL_WHOLE):]):
        parts = _split_chip_axis(n, whole, w[n].shape)
        g_small[n] = lax.dynamic_index_in_dim(parts, chip, axis=0, keepdims=False)
    small = SMALL_WHOLE + SMALL_SPLIT
    _, d_small, m_small, v_small = _adamw(
        "adamw_small", _pack_rows([w[n] for n in small], F32, 8), _pack_rows([g_small[n] for n in small], F32, 8),
        _pack_rows([m[n] for n in small], F32, 8), _pack_rows([v[n] for n in small], F32, 8))

    outs = {}
    for slot, (tag, small_pack) in enumerate((("g", None), ("d", d_small), ("m", m_small), ("v", v_small))):
        vals = {n: big_out[n][slot] for n in BIG}
        if small_pack is None:
            vals.update(g_small)
        else:
            vals.update(zip(small, _unpack_rows(small_pack, [w[n].shape for n in small])))
        outs[tag] = vals
    return (loss, dx.reshape(1, T, D), *[outs["g"][n] for n in WEIGHT_NAMES], *[outs["d"][n] for n in WEIGHT_NAMES],
            *[outs["m"][n] for n in WEIGHT_NAMES], *[outs["v"][n] for n in WEIGHT_NAMES])
```

```python
import functools

import jax
import jax.numpy as jnp
from jax import lax
from jax.experimental import pallas as pl
from jax.experimental.pallas import tpu as pltpu

F32 = jnp.float32
BF16 = jnp.bfloat16
SDS = jax.ShapeDtypeStruct
MESH = pl.DeviceIdType.MESH

HEAD_DIM = 64
N_Q_HEADS = 16
N_KV_HEADS = 2
Q_PER_KV = N_Q_HEADS // N_KV_HEADS
ATTN_BLOCK = 128
ROPE_THETA = 10000.0
CONV_WIDTH = 31
CONV_HALO = 32
CONV_FIRST_TAP = CONV_HALO - CONV_WIDTH + 1
CONV_ROW_CHUNK = 64
CONV_LANE_CHUNK = 256
CONV_GRAD_UNROLL = 4
RMS_EPS = 1e-5
LN_EPS = 1e-5
ADAM_LR = 0.001
ADAM_B1 = 0.9
ADAM_B2 = 0.999
ADAM_EPS = 1e-08
ADAM_WD = 0.01
ADAM_STEP = 10

V7X_LANES = 128
V7X_SUBLANES = 8
V7X_VMEM_LIMIT_BYTES = 56 * 1024 * 1024

N_CHIPS = 4
N_DEV = 8
PACK_W = 1024

MASK_VALUE = -1e30


def _params(*semantics):
    return pltpu.CompilerParams(dimension_semantics=semantics, vmem_limit_bytes=V7X_VMEM_LIMIT_BYTES)


def _rows(tm, width):
    return pl.BlockSpec((tm, width), lambda i: (i, 0))


def _whole(shape):
    return pl.BlockSpec(shape, lambda *_: (0,) * len(shape))


def _rms_rstd(h):
    return lax.rsqrt(jnp.mean(h * h, axis=-1, keepdims=True) + RMS_EPS)


def _silu_and_grad(z):
    sg = jax.nn.sigmoid(z)
    return z * sg, sg * (1.0 + z * (1.0 - sg))


def _swap_rope_halves(t):
    w = t.shape[1]
    half = HEAD_DIM // 2
    lane = lax.broadcasted_iota(jnp.int32, t.shape, 1)
    upper = pltpu.roll(t, w - half, 1)
    lower = pltpu.roll(t, half, 1)
    return jnp.where(lane % HEAD_DIM < half, upper, lower)


def _rope(t, cos_ref, sin_ref):
    reps = t.shape[1] // V7X_LANES
    c = jnp.tile(cos_ref[...], (1, reps))
    s = jnp.tile(sin_ref[...], (1, reps))
    return t * c + _swap_rope_halves(t) * s


def _rope_transposed(dt, cos_ref, sin_ref):
    reps = dt.shape[1] // V7X_LANES
    c = jnp.tile(cos_ref[...], (1, reps))
    s = jnp.tile(sin_ref[...], (1, reps))
    return dt * c + _swap_rope_halves(dt * s)


def _rope_tables(seq_len):
    pos = jnp.arange(seq_len, dtype=F32)
    inv_freq = ROPE_THETA ** (-jnp.arange(0, HEAD_DIM, 2, dtype=F32) / HEAD_DIM)
    ang = pos[:, None] * inv_freq[None, :]
    cos, sin = jnp.cos(ang), jnp.sin(ang)
    cos_t = jnp.concatenate([cos, cos, cos, cos], axis=1)
    sin_t = jnp.concatenate([-sin, sin, -sin, sin], axis=1)
    return cos_t, sin_t


def _qkv_proj(h, g, w, b, cos, sin, after):
    T, D = h.shape
    N = w.shape[1]
    tm = min(512, T)
    rope_w = N - N_KV_HEADS * HEAD_DIM

    def body(h_ref, g_ref, w_ref, b_ref, cos_ref, sin_ref, _, y_ref, o_ref):
        hh = h_ref[...]
        y = (hh * _rms_rstd(hh) * g_ref[...]).astype(BF16)
        y_ref[...] = y
        acc = jnp.dot(y, w_ref[...], preferred_element_type=F32) + b_ref[...]
        o_ref[:, :rope_w] = _rope(acc[:, :rope_w], cos_ref, sin_ref).astype(BF16)
        o_ref[:, rope_w:] = acc[:, rope_w:].astype(BF16)

    return pl.pallas_call(
        body, name="qkv_proj", grid=(T // tm,),
        in_specs=[_rows(tm, D), _whole((1, D)), _whole((D, N)), _whole((1, N)),
                  _rows(tm, V7X_LANES), _rows(tm, V7X_LANES), pl.BlockSpec(memory_space=pl.ANY)],
        out_specs=[_rows(tm, D), _rows(tm, N)],
        out_shape=[SDS((T, D), BF16), SDS((T, N), BF16)],
        compiler_params=_params("parallel"),
    )(h, g, w, b, cos, sin, after)


def _pw1_proj(h, g, w, b):
    T, D = h.shape
    n = w.shape[2]
    N = N_CHIPS * n
    tm = min(512, T)

    def body(h_ref, g_ref, w_ref, b_ref, y_ref, o_ref):
        hh = h_ref[...]
        y = (hh * _rms_rstd(hh) * g_ref[...]).astype(BF16)
        y_ref[...] = y
        for j in range(N_CHIPS):
            cols = slice(j * n, (j + 1) * n)
            o_ref[:, cols] = jnp.dot(y, w_ref[j], preferred_element_type=F32) + b_ref[:, cols]

    return pl.pallas_call(
        body, name="pw1_proj", grid=(T // tm,),
        in_specs=[_rows(tm, D), _whole((1, D)), _whole((N_CHIPS, D, n)), _whole((1, N))],
        out_specs=[_rows(tm, D), _rows(tm, N)],
        out_shape=[SDS((T, D), BF16), SDS((T, N), F32)],
        compiler_params=_params("parallel"),
    )(h, g, w, b)


PAIRS_PER_KV = Q_PER_KV // 2


def _upper_lanes(shape):
    return lax.broadcasted_iota(jnp.int32, shape, 1) >= HEAD_DIM


def _swap_lane_halves(t):
    return pltpu.roll(t.astype(F32), HEAD_DIM, 1).astype(t.dtype)


def _kv_operands(g, t):
    swapped = _swap_lane_halves(t)
    in_lower, in_upper = (t, swapped) if g == 0 else (swapped, t)
    upper = _upper_lanes(t.shape)
    zero = jnp.zeros_like(t)
    return jnp.where(upper, zero, in_lower), jnp.where(upper, in_upper, zero)


def _group_heads(g):
    pairs = range(g * PAIRS_PER_KV, (g + 1) * PAIRS_PER_KV)
    return [2 * hp for hp in pairs] + [2 * hp + 1 for hp in pairs]


def _pair_rows(ref, g):
    pairs = range(g * PAIRS_PER_KV, (g + 1) * PAIRS_PER_KV)
    return jnp.concatenate([ref[:, hp * 2 * HEAD_DIM:(hp + 1) * 2 * HEAD_DIM] for hp in pairs], axis=0)


def _from_previous_block(rows):
    row = lax.broadcasted_iota(jnp.int32, (ATTN_BLOCK, ATTN_BLOCK), 0)
    col = lax.broadcasted_iota(jnp.int32, (ATTN_BLOCK, ATTN_BLOCK), 1)
    return jnp.concatenate([col > row] * (rows // ATTN_BLOCK), axis=0)


def _folded_probs(n, q_groups, k_prev_groups, k_cur_groups, sink_ref, prev_part):
    def scores(q, k):
        return lax.dot_general(q, k, (((1,), (1,)), ((), ())), preferred_element_type=F32)

    s_prev = jnp.concatenate([scores(q, k[i]) for q, k in zip(q_groups, k_prev_groups) for i in range(2)], axis=0)
    s_cur = jnp.concatenate([scores(q, k[i]) for q, k in zip(q_groups, k_cur_groups) for i in range(2)], axis=0)
    s_prev = jnp.where(n > 0, s_prev, MASK_VALUE * (HEAD_DIM ** 0.5))
    s = jnp.where(prev_part, s_prev, s_cur) * (HEAD_DIM ** -0.5)
    heads = [h for g in range(N_KV_HEADS) for h in _group_heads(g)]
    sink = jnp.concatenate([jnp.broadcast_to(sink_ref[0:1, h:h + 1], (ATTN_BLOCK, 1)) for h in heads], axis=0)
    m = jnp.maximum(jnp.max(s, axis=1, keepdims=True), sink)
    p = jnp.exp(s - m)
    e_sink = jnp.exp(sink - m)
    inv = 1.0 / (jnp.sum(p, axis=1, keepdims=True) + e_sink)
    return p * inv, e_sink * inv


def _split_folded(t, prev_part):
    tb = t.astype(BF16)
    zero = jnp.zeros_like(tb)
    return jnp.where(prev_part, tb, zero), jnp.where(prev_part, zero, tb)


def _attn_specs(T):
    nb = T // ATTN_BLOCK
    kcol = N_Q_HEADS * HEAD_DIM // V7X_LANES
    cur = lambda n: jnp.minimum(n, nb - 1)
    prev = lambda n: jnp.maximum(jnp.minimum(n, nb - 1) - 1, 0)
    q_spec = pl.BlockSpec((ATTN_BLOCK, N_Q_HEADS * HEAD_DIM), lambda n: (cur(n), 0))
    kc_spec = pl.BlockSpec((ATTN_BLOCK, V7X_LANES), lambda n: (cur(n), kcol))
    kp_spec = pl.BlockSpec((ATTN_BLOCK, V7X_LANES), lambda n: (prev(n), kcol))
    vc_spec = pl.BlockSpec((ATTN_BLOCK, V7X_LANES), lambda n: (cur(n), kcol + 1))
    vp_spec = pl.BlockSpec((ATTN_BLOCK, V7X_LANES), lambda n: (prev(n), kcol + 1))
    return q_spec, kc_spec, kp_spec, vc_spec, vp_spec


def _attn_fwd(qkv, sinks):
    T = qkv.shape[0]
    nb = T // ATTN_BLOCK
    qw = N_Q_HEADS * HEAD_DIM

    def body(q_ref, kc_ref, kp_ref, vc_ref, vp_ref, sink_ref, o_ref):
        n = pl.program_id(0)
        prev_part = _from_previous_block(N_Q_HEADS * ATTN_BLOCK)
        half = PAIRS_PER_KV * ATTN_BLOCK
        groups = range(N_KV_HEADS)
        probs, _ = _folded_probs(n, [_pair_rows(q_ref, g) for g in groups],
                                 [_kv_operands(g, kp_ref[...]) for g in groups],
                                 [_kv_operands(g, kc_ref[...]) for g in groups], sink_ref, prev_part)
        p_prev, p_cur = _split_folded(probs, prev_part)
        for g in groups:
            v_prev, v_cur = _kv_operands(g, vp_ref[...]), _kv_operands(g, vc_ref[...])
            even, odd = slice(2 * g * half, (2 * g + 1) * half), slice((2 * g + 1) * half, (2 * g + 2) * half)
            o = (jnp.dot(p_prev[even], v_prev[0], preferred_element_type=F32)
                 + jnp.dot(p_cur[even], v_cur[0], preferred_element_type=F32)
                 + jnp.dot(p_prev[odd], v_prev[1], preferred_element_type=F32)
                 + jnp.dot(p_cur[odd], v_cur[1], preferred_element_type=F32))
            for i in range(PAIRS_PER_KV):
                hp = g * PAIRS_PER_KV + i
                o_ref[:, hp * 2 * HEAD_DIM:(hp + 1) * 2 * HEAD_DIM] = (
                    o[i * ATTN_BLOCK:(i + 1) * ATTN_BLOCK].astype(BF16))

    return pl.pallas_call(
        body, name="attn_fwd", grid=(nb,),
        in_specs=[*_attn_specs(T), _whole((1, N_Q_HEADS))],
        out_specs=_rows(ATTN_BLOCK, qw),
        out_shape=SDS((T, qw), BF16),
        compiler_params=_params("parallel"),
    )(qkv, qkv, qkv, qkv, qkv, sinks)


def _mm_res(name, a, w, b, res, g):
    T, K = a.shape
    D = w.shape[1]
    tm = min(512, T)

    def body(a_ref, w_ref, b_ref, r_ref, g_ref, o_ref, f_ref):
        h = jnp.dot(a_ref[...], w_ref[...], preferred_element_type=F32) + b_ref[...] + r_ref[...]
        o_ref[...] = h
        f_ref[...] = (h * _rms_rstd(h) * g_ref[...]).astype(BF16)

    return pl.pallas_call(
        body, name=name, grid=(T // tm,),
        in_specs=[_rows(tm, K), _whole((K, D)), _whole((1, D)), _rows(tm, D), _whole((1, D))],
        out_specs=[_rows(tm, D), _rows(tm, D)],
        out_shape=[SDS((T, D), F32), SDS((T, D), BF16)],
        compiler_params=_params("parallel"),
    )(a, w, b, res, g)


def _ffn_down(name, s, w2, layer, res):
    _, T, n = s.shape
    D = w2.shape[3]
    tm = min(512, T)

    def body(s_ref, w_ref, r_ref, o_ref):
        acc = r_ref[...]
        for j in range(N_CHIPS):
            acc = acc + jnp.dot(s_ref[j], w_ref[j], preferred_element_type=F32)
        o_ref[...] = acc

    return pl.pallas_call(
        body, name=name, grid=(T // tm,),
        in_specs=[pl.BlockSpec((N_CHIPS, tm, n), lambda i: (0, i, 0)),
                  pl.BlockSpec((N_CHIPS, None, n, D), lambda i: (0, layer, 0, 0)), _rows(tm, D)],
        out_specs=_rows(tm, D),
        out_shape=SDS((T, D), F32),
        compiler_params=_params("parallel"),
    )(s, w2, res)


def _ffn_up(name, f, w1, w3, layer):
    T, D = f.shape
    n = w1.shape[3]
    tm = min(1024, T)

    def body(f_ref, w1_ref, w3_ref, act_ref, gg_ref, s_ref):
        ff = f_ref[...]
        g1 = jnp.dot(ff, w1_ref[...], preferred_element_type=F32)
        g3 = jnp.dot(ff, w3_ref[...], preferred_element_type=F32)
        act, dact = _silu_and_grad(g1)
        act_ref[...] = act.astype(BF16)
        gg_ref[...] = (g3 * dact).astype(BF16)
        s_ref[...] = (act * g3).astype(BF16)

    slab = pl.BlockSpec((None, tm, n), lambda j, i: (j, i, 0))
    wslab = pl.BlockSpec((None, None, D, n), lambda j, i: (j, layer, 0, 0))
    hidden = SDS((N_CHIPS, T, n), BF16)
    return pl.pallas_call(
        body, name=name, grid=(N_CHIPS, T // tm),
        in_specs=[pl.BlockSpec((tm, D), lambda j, i: (i, 0)), wslab, wslab],
        out_specs=[slab, slab, slab],
        out_shape=[hidden, hidden, hidden],
        compiler_params=_params("parallel", "parallel"),
    )(f, w1, w3)


def _glu(a, d):
    return a[:, :d] * jax.nn.sigmoid(a[:, d:])


def _conv_tile(T):
    return min(256, T)


def _fill_shifted(sh_ref, tc):
    n = tc + CONV_HALO - V7X_SUBLANES
    for r in range(1, V7X_SUBLANES):
        sh_ref[r, 0:n, :] = sh_ref[0, pl.ds(r, n), :]


def _depthwise_taps(sh_ref, w_ref, offsets, bias_ref, out_ref, tc):
    D = out_ref.shape[1]

    def chunk(i, carry):
        t0 = pl.multiple_of(i * CONV_ROW_CHUNK, CONV_ROW_CHUNK)
        for cb in range(D // CONV_LANE_CHUNK):
            cs = slice(cb * CONV_LANE_CHUNK, (cb + 1) * CONV_LANE_CHUNK)
            acc = jnp.zeros((CONV_ROW_CHUNK, CONV_LANE_CHUNK), F32)
            for r in range(V7X_SUBLANES):
                taps = [(j, o // V7X_SUBLANES) for j, o in enumerate(offsets) if o % V7X_SUBLANES == r]
                if not taps:
                    continue
                span = CONV_ROW_CHUNK + V7X_SUBLANES * max(q for _, q in taps)
                rows = sh_ref[r, pl.ds(t0, span), cs]
                for j, q in taps:
                    acc = acc + rows[V7X_SUBLANES * q:V7X_SUBLANES * q + CONV_ROW_CHUNK] * w_ref[j:j + 1, cs]
            if bias_ref is not None:
                acc = acc + bias_ref[:, cs]
            out_ref[pl.ds(t0, CONV_ROW_CHUNK), cs] = acc
        return carry

    lax.fori_loop(0, tc // CONV_ROW_CHUNK, chunk, 0)


def _depthwise_tap_grads(dy_sh, x_sh, offsets, dw_ref, tc):
    D = dw_ref.shape[1]
    for cb in range(D // V7X_LANES):
        cs = slice(cb * V7X_LANES, (cb + 1) * V7X_LANES)

        def row_tiles(i, accs, cs=cs):
            for k in range(CONV_GRAD_UNROLL):
                t0 = pl.multiple_of(i * (CONV_GRAD_UNROLL * V7X_SUBLANES), V7X_SUBLANES) + k * V7X_SUBLANES
                d = dy_sh[0, pl.ds(t0, V7X_SUBLANES), cs]
                accs = tuple(
                    acc + d * x_sh[o % V7X_SUBLANES, pl.ds(t0 + o // V7X_SUBLANES * V7X_SUBLANES, V7X_SUBLANES), cs]
                    for acc, o in zip(accs, offsets))
            return accs

        zero = jnp.zeros((V7X_SUBLANES, V7X_LANES), F32)
        accs = lax.fori_loop(0, tc // (CONV_GRAD_UNROLL * V7X_SUBLANES), row_tiles, tuple(zero for _ in offsets))
        for j, acc in enumerate(accs):
            dw_ref[j:j + 1, cs] += jnp.sum(acc, axis=0, keepdims=True)


def _conv_fwd(a, w_dw, b_dw, ln_g, ln_b):
    T = a.shape[0]
    D = a.shape[1] // 2
    tc = _conv_tile(T)
    per = tc // CONV_HALO

    def body(a_ref, ah_ref, w_ref, bdw_ref, lg_ref, lb_ref, c_ref, act_ref, u_sh):
        i = pl.program_id(0)
        u_sh[0, 0:CONV_HALO, :] = jnp.where(i > 0, _glu(ah_ref[...], D), 0.0)
        u_sh[0, CONV_HALO:, :] = _glu(a_ref[...], D)
        _fill_shifted(u_sh, tc)
        _depthwise_taps(u_sh, w_ref, [CONV_FIRST_TAP + j for j in range(CONV_WIDTH)], bdw_ref, c_ref, tc)
        c = c_ref[...]
        xc = c - jnp.mean(c, axis=-1, keepdims=True)
        z = xc * lax.rsqrt(jnp.mean(xc * xc, axis=-1, keepdims=True) + LN_EPS)
        l = z * lg_ref[...] + lb_ref[...]
        act_ref[...] = (l * jax.nn.sigmoid(l)).astype(BF16)

    return pl.pallas_call(
        body, name="conv_fwd", grid=(T // tc,),
        in_specs=[_rows(tc, 2 * D),
                  pl.BlockSpec((CONV_HALO, 2 * D), lambda i: (jnp.maximum(i * per - 1, 0), 0)),
                  _whole((CONV_WIDTH, D)), _whole((1, D)), _whole((1, D)), _whole((1, D))],
        out_specs=[_rows(tc, D), _rows(tc, D)],
        out_shape=[SDS((T, D), F32), SDS((T, D), BF16)],
        scratch_shapes=[pltpu.VMEM((V7X_SUBLANES, tc + CONV_HALO, D), F32)],
        compiler_params=_params("parallel"),
    )(a, a, w_dw, b_dw, ln_g, ln_b)


def _final_loss(h, g, target):
    T, D = h.shape
    tm = min(512, T)

    def body(h_ref, g_ref, t_ref, dh_ref, loss_ref, dg_ref):
        @pl.when(pl.program_id(0) == 0)
        def _():
            loss_ref[...] = jnp.zeros_like(loss_ref)
            dg_ref[...] = jnp.zeros_like(dg_ref)

        hh = h_ref[...]
        r = _rms_rstd(hh)
        g = g_ref[...]
        d = hh * r * g - t_ref[...]
        loss_ref[...] += 0.5 * jnp.sum(jnp.mean(d * d, axis=-1, keepdims=True), axis=0, keepdims=True)
        dout = d * (1.0 / D)
        dg_ref[...] += jnp.sum(dout * (hh * r), axis=0, keepdims=True)
        dxh = dout * g
        dh_ref[...] = r * dxh - hh * (r * r * r) * jnp.mean(dxh * hh, axis=-1, keepdims=True)

    return pl.pallas_call(
        body, name="final_loss", grid=(T // tm,),
        in_specs=[_rows(tm, D), _whole((1, D)), _rows(tm, D)],
        out_specs=[_rows(tm, D), _whole((1, 1)), _whole((1, D))],
        out_shape=[SDS((T, D), F32), SDS((1, 1), F32), SDS((1, D), F32)],
        compiler_params=_params("arbitrary"),
    )(h, g, target)


def _ffn_bwd_act(name, dh, w2, layer, act, gate_grad):
    T, D = dh.shape
    n = w2.shape[2]
    tm = min(1024, T)

    def body(dh_ref, w2_ref, act_ref, gg_ref, dg1_ref, dg3_ref):
        ds = lax.dot_general(dh_ref[...].astype(BF16), w2_ref[...], (((1,), (1,)), ((), ())),
                             preferred_element_type=F32)
        dg1_ref[...] = (ds * gg_ref[...].astype(F32)).astype(BF16)
        dg3_ref[...] = (ds * act_ref[...].astype(F32)).astype(BF16)

    slab = pl.BlockSpec((None, tm, n), lambda i, j: (j, i, 0))
    hidden = SDS((N_CHIPS, T, n), BF16)
    return pl.pallas_call(
        body, name=name, grid=(T // tm, N_CHIPS),
        in_specs=[pl.BlockSpec((tm, D), lambda i, j: (i, 0)),
                  pl.BlockSpec((None, None, n, D), lambda i, j: (j, layer, 0, 0)), slab, slab],
        out_specs=[slab, slab],
        out_shape=[hidden, hidden],
        compiler_params=_params("parallel", "arbitrary"),
    )(dh, w2, act, gate_grad)


def _dot_tn(a, b):
    return lax.dot_general(a.astype(BF16), b.astype(BF16), (((0,), (0,)), ((), ())), preferred_element_type=F32)


def _dot_nt(a, b):
    return lax.dot_general(a.astype(BF16), b, (((1,), (1,)), ((), ())), preferred_element_type=F32)


def _mm_tn(name, a, b, col_chunks=1, after=None):
    a_slabs, b_slabs = a.ndim == 3, b.ndim == 3
    T = a.shape[-2]
    tt = min(1024, T)
    ka, nb = a.shape[-1], b.shape[-1]
    if a_slabs or b_slabs:
        out_dims = (N_CHIPS, ka, nb)
    elif col_chunks > 1:
        out_dims = (col_chunks, ka, nb // col_chunks)
    else:
        out_dims = (ka, nb)

    def body(a_ref, b_ref, *rest):
        o_ref = rest[-1]

        @pl.when(pl.program_id(0) == 0)
        def _():
            o_ref[...] = jnp.zeros_like(o_ref)

        if a_slabs:
            bb = b_ref[...].astype(BF16)
            for j in range(N_CHIPS):
                o_ref[j] += _dot_tn(a_ref[j], bb)
        elif b_slabs:
            aa = a_ref[...].astype(BF16)
            for j in range(N_CHIPS):
                o_ref[j] += _dot_tn(aa, b_ref[j])
        elif col_chunks > 1:
            aa = a_ref[...].astype(BF16)
            w = nb // col_chunks
            for j in range(col_chunks):
                o_ref[j] += _dot_tn(aa, b_ref[:, j * w:(j + 1) * w])
        else:
            o_ref[...] += _dot_tn(a_ref[...], b_ref[...])

    def spec(arr, slabs):
        if slabs:
            return pl.BlockSpec((N_CHIPS, tt, arr.shape[-1]), lambda t: (0, t, 0))
        return _rows(tt, arr.shape[-1])

    return pl.pallas_call(
        body, name=name, grid=(T // tt,),
        in_specs=[spec(a, a_slabs), spec(b, b_slabs)] + ([] if after is None else [pl.BlockSpec(memory_space=pl.ANY)]),
        out_specs=_whole(out_dims),
        out_shape=SDS(out_dims, F32),
        compiler_params=_params("arbitrary"),
    )(a, b, *([] if after is None else [after]))


def _mm_nt_normbwd(name, pairs, h, g, dh, after):
    T, D = h.shape
    tm = min(512, T)
    n_pairs = len(pairs)
    kinds = ["slabs" if dy.ndim == 3 else ("quarters" if w.ndim == 3 else "plain") for dy, w, _ in pairs]

    def body(*refs):
        dy_refs = refs[:n_pairs]
        w_refs = refs[n_pairs:2 * n_pairs]
        h_ref, g_ref, dh_ref, _, o_ref, dg_ref, cs_ref = refs[2 * n_pairs:]

        @pl.when(pl.program_id(0) == 0)
        def _():
            dg_ref[...] = jnp.zeros_like(dg_ref)
            cs_ref[...] = jnp.zeros_like(cs_ref)

        df = jnp.zeros((tm, D), F32)
        for dy_ref, w_ref, kd in zip(dy_refs, w_refs, kinds):
            if kd == "slabs":
                for j in range(N_CHIPS):
                    df = df + _dot_nt(dy_ref[j], w_ref[j])
            elif kd == "quarters":
                n = w_ref.shape[2]
                for j in range(N_CHIPS):
                    df = df + _dot_nt(dy_ref[:, j * n:(j + 1) * n], w_ref[j])
            else:
                df = df + _dot_nt(dy_ref[...], w_ref[...])
        hh = h_ref[...]
        r = _rms_rstd(hh)
        dg_ref[...] += jnp.sum(df * (hh * r), axis=0, keepdims=True)
        dxh = df * g_ref[...]
        out = dh_ref[...] + (r * dxh - hh * (r * r * r) * jnp.mean(dxh * hh, axis=-1, keepdims=True))
        o_ref[...] = out
        cs_ref[...] += jnp.sum(out, axis=0, keepdims=True)

    dy_specs, w_specs = [], []
    for (dy, w, layer), kd in zip(pairs, kinds):
        if kd == "slabs":
            dy_specs.append(pl.BlockSpec((N_CHIPS, tm, dy.shape[2]), lambda i: (0, i, 0)))
            w_specs.append(pl.BlockSpec((N_CHIPS, None, D, w.shape[3]),
                                        functools.partial(lambda i, layer: (0, layer, 0, 0), layer=layer),
                                        pipeline_mode=pl.Buffered(1)))
        else:
            dy_specs.append(_rows(tm, dy.shape[1]))
            w_specs.append(_whole(w.shape))

    return pl.pallas_call(
        body, name=name, grid=(T // tm,),
        in_specs=[*dy_specs, *w_specs, _rows(tm, D), _whole((1, D)), _rows(tm, D), pl.BlockSpec(memory_space=pl.ANY)],
        out_specs=[_rows(tm, D), _whole((1, D)), _whole((1, D))],
        out_shape=[SDS((T, D), F32), SDS((1, D), F32), SDS((1, D), F32)],
        compiler_params=_params("arbitrary"),
    )(*[dy for dy, _, _ in pairs], *[w for _, w, _ in pairs], h, g, dh, after)


def _mm_nt(name, dy, w, out_dtype):
    T, N = dy.shape
    K = w.shape[0]
    tm = min(512, T)

    def body(dy_ref, w_ref, o_ref):
        o_ref[...] = lax.dot_general(dy_ref[...].astype(BF16), w_ref[...], (((1,), (1,)), ((), ())),
                                     preferred_element_type=F32).astype(out_dtype)

    return pl.pallas_call(
        body, name=name, grid=(T // tm,),
        in_specs=[_rows(tm, N), _whole((K, N))],
        out_specs=_rows(tm, K),
        out_shape=SDS((T, K), out_dtype),
        compiler_params=_params("parallel"),
    )(dy, w)


def _conv_bwd(dact, c, a, w_dw, ln_g, ln_b):
    T, D = c.shape
    tc = _conv_tile(T)
    per = tc // CONV_HALO
    n_tiles = T // tc
    last_halo = T // CONV_HALO - 1

    def ln_bwd(dact_v, c_v, lg, lb):
        xc = c_v - jnp.mean(c_v, axis=-1, keepdims=True)
        rstd = lax.rsqrt(jnp.mean(xc * xc, axis=-1, keepdims=True) + LN_EPS)
        z = xc * rstd
        _, dsilu = _silu_and_grad(z * lg + lb)
        dl = dact_v * dsilu
        dz = dl * lg
        dc = rstd * (dz - jnp.mean(dz, axis=-1, keepdims=True) - z * jnp.mean(dz * z, axis=-1, keepdims=True))
        return dc, dl, z

    def body(dact_ref, dactn_ref, c_ref, cn_ref, a_ref, ah_ref, w_ref, lg_ref, lb_ref,
             da_ref, dlg_ref, dlb_ref, dbdw_ref, dwdw_ref, dbpw1_ref, dc_sh, u_sh, du_scr):
        i = pl.program_id(0)

        @pl.when(i == 0)
        def _():
            for ref in (dlg_ref, dlb_ref, dbdw_ref, dwdw_ref, dbpw1_ref):
                ref[...] = jnp.zeros_like(ref)

        lg, lb = lg_ref[...], lb_ref[...]
        dc, dl, z = ln_bwd(dact_ref[...], c_ref[...], lg, lb)
        dlg_ref[...] += jnp.sum(dl * z, axis=0, keepdims=True)
        dlb_ref[...] += jnp.sum(dl, axis=0, keepdims=True)
        dbdw_ref[...] += jnp.sum(dc, axis=0, keepdims=True)
        dcn, _, _ = ln_bwd(dactn_ref[...], cn_ref[...], lg, lb)
        dc_sh[0, 0:tc, :] = dc
        dc_sh[0, tc:, :] = jnp.where(i < n_tiles - 1, dcn, 0.0)
        _fill_shifted(dc_sh, tc)

        a_v = a_ref[...]
        a1 = a_v[:, :D]
        sg = jax.nn.sigmoid(a_v[:, D:])
        u_sh[0, 0:CONV_HALO, :] = jnp.where(i > 0, _glu(ah_ref[...], D), 0.0)
        u_sh[0, CONV_HALO:, :] = a1 * sg
        _fill_shifted(u_sh, tc)

        _depthwise_taps(dc_sh, w_ref, [CONV_WIDTH - 1 - j for j in range(CONV_WIDTH)], None, du_scr, tc)
        _depthwise_tap_grads(dc_sh, u_sh, [CONV_FIRST_TAP + j for j in range(CONV_WIDTH)], dwdw_ref, tc)

        du = du_scr[...]
        da1 = du * sg
        da2 = du * a1 * sg * (1.0 - sg)
        da_ref[:, :D] = da1.astype(BF16)
        da_ref[:, D:] = da2.astype(BF16)
        dbpw1_ref[:, :D] += jnp.sum(da1, axis=0, keepdims=True)
        dbpw1_ref[:, D:] += jnp.sum(da2, axis=0, keepdims=True)

    nxt = lambda i: (jnp.minimum((i + 1) * per, last_halo), 0)
    return pl.pallas_call(
        body, name="conv_bwd", grid=(n_tiles,),
        in_specs=[_rows(tc, D), pl.BlockSpec((CONV_HALO, D), nxt),
                  _rows(tc, D), pl.BlockSpec((CONV_HALO, D), nxt),
                  _rows(tc, 2 * D),
                  pl.BlockSpec((CONV_HALO, 2 * D), lambda i: (jnp.maximum(i * per - 1, 0), 0)),
                  _whole((CONV_WIDTH, D)), _whole((1, D)), _whole((1, D))],
        out_specs=[_rows(tc, 2 * D), _whole((1, D)), _whole((1, D)), _whole((1, D)),
                   _whole((CONV_HALO, D)), _whole((1, 2 * D))],
        out_shape=[SDS((T, 2 * D), BF16), SDS((1, D), F32), SDS((1, D), F32), SDS((1, D), F32),
                   SDS((CONV_HALO, D), F32), SDS((1, 2 * D), F32)],
        scratch_shapes=[pltpu.VMEM((V7X_SUBLANES, tc + CONV_HALO, D), F32),
                        pltpu.VMEM((V7X_SUBLANES, tc + CONV_HALO, D), F32), pltpu.VMEM((tc, D), F32)],
        compiler_params=_params("arbitrary"),
    )(dact, dact, c, c, a, a, w_dw, ln_g, ln_b)


def _attn_bwd(qkv, dao, cos, sin, sinks):
    T = qkv.shape[0]
    nb = T // ATTN_BLOCK
    qw = N_Q_HEADS * HEAD_DIM
    kw = N_KV_HEADS * HEAD_DIM

    def body(q_ref, kc_ref, kp_ref, vc_ref, vp_ref, do_ref, cos_ref, sin_ref, cosp_ref, sinp_ref, sink_ref,
             dq_ref, dkv_ref, dsink_ref, dbq_ref, dbkv_ref, carry, prev_scr, cur_scr, dq_scr):
        n = pl.program_id(0)

        @pl.when(n == 0)
        def _():
            for ref in (dsink_ref, dbq_ref, dbkv_ref, carry):
                ref[...] = jnp.zeros_like(ref)

        @pl.when(n == nb)
        def _():
            prev_scr[...] = jnp.zeros_like(prev_scr)

        @pl.when(n < nb)
        def _():
            prev_part = _from_previous_block(N_Q_HEADS * ATTN_BLOCK)
            half = PAIRS_PER_KV * ATTN_BLOCK
            upper = _upper_lanes((ATTN_BLOCK, 2 * HEAD_DIM))
            groups = range(N_KV_HEADS)

            def nt(a, b):
                return lax.dot_general(a, b, (((1,), (1,)), ((), ())), preferred_element_type=F32)

            def kv_grad(even_rows, odd_rows, x):
                even = lax.dot_general(even_rows, x, (((0,), (0,)), ((), ())), preferred_element_type=F32)
                odd = lax.dot_general(odd_rows, x, (((0,), (0,)), ((), ())), preferred_element_type=F32)
                t = jnp.where(upper, odd, even)
                return t + _swap_lane_halves(t)

            q = [_pair_rows(q_ref, g) for g in groups]
            do = [_pair_rows(do_ref, g) for g in groups]
            k_prev = [_kv_operands(g, kp_ref[...]) for g in groups]
            k_cur = [_kv_operands(g, kc_ref[...]) for g in groups]
            v_prev = [_kv_operands(g, vp_ref[...]) for g in groups]
            v_cur = [_kv_operands(g, vc_ref[...]) for g in groups]
            probs, p_sink = _folded_probs(n, q, k_prev, k_cur, sink_ref, prev_part)
            dp_prev = jnp.concatenate([nt(do[g], v_prev[g][i]) for g in groups for i in range(2)], axis=0)
            dp_cur = jnp.concatenate([nt(do[g], v_cur[g][i]) for g in groups for i in range(2)], axis=0)
            dp = jnp.where(prev_part, dp_prev, dp_cur)
            delta = jnp.sum(probs * dp, axis=1, keepdims=True)
            ds_prev, ds_cur = _split_folded(probs * (dp - delta) * (HEAD_DIM ** -0.5), prev_part)
            p_prev, p_cur = _split_folded(probs, prev_part)
            dsink_rows = -(p_sink * delta)
            heads = [h for g in groups for h in _group_heads(g)]
            for i, h in enumerate(heads):
                dsink_ref[:, h:h + 1] += jnp.sum(dsink_rows[i * ATTN_BLOCK:(i + 1) * ATTN_BLOCK], axis=0,
                                                 keepdims=True)
            kv_grads = []
            for g in groups:
                even, odd = slice(2 * g * half, (2 * g + 1) * half), slice((2 * g + 1) * half, (2 * g + 2) * half)
                dq = (jnp.dot(ds_prev[even], k_prev[g][0], preferred_element_type=F32)
                      + jnp.dot(ds_cur[even], k_cur[g][0], preferred_element_type=F32)
                      + jnp.dot(ds_prev[odd], k_prev[g][1], preferred_element_type=F32)
                      + jnp.dot(ds_cur[odd], k_cur[g][1], preferred_element_type=F32))
                for i in range(PAIRS_PER_KV):
                    hp = g * PAIRS_PER_KV + i
                    dq_scr[:, hp * 2 * HEAD_DIM:(hp + 1) * 2 * HEAD_DIM] = dq[i * ATTN_BLOCK:(i + 1) * ATTN_BLOCK]
                kv_grads.append((kv_grad(ds_prev[even], ds_prev[odd], q[g]), kv_grad(ds_cur[even], ds_cur[odd], q[g]),
                                 kv_grad(p_prev[even], p_prev[odd], do[g]), kv_grad(p_cur[even], p_cur[odd], do[g])))
            (dkp0, dkc0, dvp0, dvc0), (dkp1, dkc1, dvp1, dvc1) = kv_grads
            prev_scr[:, :kw] = jnp.where(upper, dkp1, dkp0)
            prev_scr[:, kw:] = jnp.where(upper, dvp1, dvp0)
            cur_scr[:, :kw] = jnp.where(upper, dkc1, dkc0)
            cur_scr[:, kw:] = jnp.where(upper, dvc1, dvc0)
            dq_pre = _rope_transposed(dq_scr[...], cos_ref, sin_ref)
            dq_ref[...] = dq_pre.astype(BF16)
            dbq_ref[...] += jnp.sum(dq_pre, axis=0, keepdims=True)

        tot = carry[...] + prev_scr[...]
        dk_pre = _rope_transposed(tot[:, :kw], cosp_ref, sinp_ref)
        dkv_ref[:, :kw] = dk_pre.astype(BF16)
        dkv_ref[:, kw:] = tot[:, kw:].astype(BF16)
        dbkv_ref[:, :kw] += jnp.sum(dk_pre, axis=0, keepdims=True)
        dbkv_ref[:, kw:] += jnp.sum(tot[:, kw:], axis=0, keepdims=True)

        @pl.when(n < nb)
        def _():
            carry[...] = cur_scr[...]

    cur = lambda n: (jnp.minimum(n, nb - 1), 0)
    out_lag = lambda n: (jnp.maximum(n - 1, 0), 0)
    return pl.pallas_call(
        body, name="attn_bwd", grid=(nb + 1,),
        in_specs=[*_attn_specs(T),
                  pl.BlockSpec((ATTN_BLOCK, qw), cur),
                  pl.BlockSpec((ATTN_BLOCK, V7X_LANES), cur), pl.BlockSpec((ATTN_BLOCK, V7X_LANES), cur),
                  pl.BlockSpec((ATTN_BLOCK, V7X_LANES), out_lag), pl.BlockSpec((ATTN_BLOCK, V7X_LANES), out_lag),
                  _whole((1, N_Q_HEADS))],
        out_specs=[pl.BlockSpec((ATTN_BLOCK, qw), cur), pl.BlockSpec((ATTN_BLOCK, 2 * kw), out_lag),
                   _whole((1, N_Q_HEADS)), _whole((1, qw)), _whole((1, 2 * kw))],
        out_shape=[SDS((T, qw), BF16), SDS((T, 2 * kw), BF16),
                   SDS((1, N_Q_HEADS), F32), SDS((1, qw), F32), SDS((1, 2 * kw), F32)],
        scratch_shapes=[pltpu.VMEM((ATTN_BLOCK, 2 * kw), F32), pltpu.VMEM((ATTN_BLOCK, 2 * kw), F32),
                        pltpu.VMEM((ATTN_BLOCK, 2 * kw), F32), pltpu.VMEM((ATTN_BLOCK, qw), F32)],
        compiler_params=_params("arbitrary"),
    )(qkv, qkv, qkv, qkv, qkv, dao, cos, sin, cos, sin, sinks)


def _local_step(x, target, p, reduce_begin, reduce_send):
    T, D = x.shape
    cos, sin = _rope_tables(T)
    qw = N_Q_HEADS * HEAD_DIM
    nm, nf = p["norm_mix"], p["norm_ffn"]

    y0, qkv = _qkv_proj(x, nm[0:1], p["attn_w_qkv"], p["attn_b_qkv"], cos, sin, p["gather_started"])
    ao = _attn_fwd(qkv, p["attn_sinks"])
    h1, f0 = _mm_res("attn_out", ao, p["attn_w_o"], p["attn_b_o"], x, nf[0:1])
    p = {**p, **p["other_weights"](h1)}
    w1, w3, w2 = p["ffn_w1"], p["ffn_w3"], p["ffn_w2"]
    act0, gg0, s0 = _ffn_up("ffn0_up", f0, w1, w3, 0)
    h2 = _ffn_down("ffn0_down", s0, w2, 0, h1)
    y1, a = _pw1_proj(h2, nm[1:2], p["conv_w_pw1"], p["conv_b_pw1"])
    c, act = _conv_fwd(a, p["conv_w_dw"], p["conv_b_dw"], p["conv_ln_g"], p["conv_ln_b"])
    h3, f1 = _mm_res("conv_out", act, p["conv_w_pw2"], p["conv_b_pw2"], h2, nf[1:2])
    act1, gg1, s1 = _ffn_up("ffn1_up", f1, w1, w3, 1)
    h4 = _ffn_down("ffn1_down", s1, w2, 1, h3)
    dh4, loss, d_norm_final = _final_loss(h4, p["norm_final"], target)

    g = {}
    dg1, dg3 = _ffn_bwd_act("ffn1_bwd_act", dh4, w2, 1, act1, gg1)
    dw2_1 = _mm_tn("ffn1_dw2", s1, dh4)
    dw1_1 = _mm_tn("ffn1_dw1", f1, dg1)
    dw3_1 = _mm_tn("ffn1_dw3", f1, dg3)
    begun = reduce_begin("ffn1", {("ffn_w1", 1): dw1_1, ("ffn_w3", 1): dw3_1, ("ffn_w2", 1): dw2_1})
    dh3, dnf1, db_pw2 = _mm_nt_normbwd("ffn1_bwd_in", [(dg1, w1, 1), (dg3, w3, 1)], h3, nf[1:2], dh4, begun)
    sent = reduce_send("ffn1", dh3)

    dw_pw2 = _mm_tn("conv_dw_pw2", act, dh3, after=sent)
    dact = _mm_nt("conv_bwd_out", dh3, p["conv_w_pw2"], F32)
    da, d_ln_g, d_ln_b, d_b_dw, d_w_dw, d_b_pw1 = _conv_bwd(dact, c, a, p["conv_w_dw"], p["conv_ln_g"], p["conv_ln_b"])
    dw_pw1 = _mm_tn("conv_dw_pw1", y1, da, col_chunks=N_CHIPS)
    begun = reduce_begin("conv", {("conv_w_pw2", 0): dw_pw2.reshape(N_CHIPS, -1, D), ("conv_w_pw1", 0): dw_pw1})
    dh2, dnm1, _ = _mm_nt_normbwd("conv_bwd_in", [(da, p["conv_w_pw1"], None)], h2, nm[1:2], dh3, begun)
    sent = reduce_send("conv", dh2)

    dg1, dg3 = _ffn_bwd_act("ffn0_bwd_act", dh2, w2, 0, act0, gg0)
    dw2_0 = _mm_tn("ffn0_dw2", s0, dh2, after=sent)
    dw1_0 = _mm_tn("ffn0_dw1", f0, dg1)
    dw3_0 = _mm_tn("ffn0_dw3", f0, dg3)
    begun = reduce_begin("ffn0", {("ffn_w1", 0): dw1_0, ("ffn_w3", 0): dw3_0, ("ffn_w2", 0): dw2_0})
    dh1, dnf0, db_o = _mm_nt_normbwd("ffn0_bwd_in", [(dg1, w1, 0), (dg3, w3, 0)], h1, nf[0:1], dh2, begun)
    sent = reduce_send("ffn0", dh1)

    dw_o = _mm_tn("attn_dw_o", ao, dh1, after=sent)
    dao = _mm_nt("attn_bwd_out", dh1, p["attn_w_o"], BF16)
    dq, dkv, d_sinks, dbq, dbkv = _attn_bwd(qkv, dao, cos, sin, p["attn_sinks"])
    dwq = _mm_tn("attn_dw_q", y0, dq)
    dwkv = _mm_tn("attn_dw_kv", y0, dkv)
    wqkv = p["attn_w_qkv"]
    n_qkv = wqkv.shape[1] // N_CHIPS
    dwqkv = jnp.moveaxis(jnp.concatenate([dwq, dwkv], axis=1).reshape(D, N_CHIPS, n_qkv), 1, 0)
    begun = reduce_begin("attn", {("attn_w_o", 0): dw_o.reshape(N_CHIPS, -1, D), ("attn_w_qkv", 0): dwqkv})
    dx, dnm0, _ = _mm_nt_normbwd("attn_bwd_in", [(dq, wqkv[:, :qw], None), (dkv, wqkv[:, qw:], None)], x, nm[0:1], dh1,
                                 begun)
    reduce_send("attn", dx)

    g["norm_mix"] = jnp.concatenate([dnm0, dnm1], axis=0)
    g["norm_ffn"] = jnp.concatenate([dnf0, dnf1], axis=0)
    g["attn_b_qkv"] = jnp.concatenate([dbq, dbkv], axis=1)
    g["attn_sinks"] = d_sinks
    g["attn_b_o"] = db_o
    g["conv_b_pw1"] = d_b_pw1
    g["conv_w_dw"] = d_w_dw[:CONV_WIDTH]
    g["conv_b_dw"] = d_b_dw
    g["conv_ln_g"] = d_ln_g
    g["conv_ln_b"] = d_ln_b
    g["conv_b_pw2"] = db_pw2
    g["norm_final"] = d_norm_final
    return loss, dx, g


ANY = pl.BlockSpec(memory_space=pl.ANY)
VMEM_WHOLE = pl.BlockSpec(memory_space=pltpu.VMEM)


def _my_place():
    return lax.axis_index("x"), lax.axis_index("y"), lax.axis_index("c")


def _other_chips(x, y):
    places = [(1 - x, y), (x, 1 - y), (1 - x, 1 - y)]
    return [(bx, by, 2 * bx + by) for bx, by in places]


def _exchange_small(name, v, reduce):
    r, w = v.shape

    def body(v_ref, o_ref, *rest):
        if reduce:
            buf, send_sems, recv_sems = rest
        else:
            buf = o_ref
            send_sems, recv_sems = rest
        x, y, c = _my_place()
        me = 4 * x + 2 * y + c
        sends = []
        for k in range(1, N_DEV):
            peer = (1 - x if k & 4 else x, 1 - y if k & 2 else y, 1 - c if k & 1 else c)
            cp = pltpu.make_async_remote_copy(
                src_ref=v_ref, dst_ref=buf.at[me], send_sem=send_sems.at[k - 1], recv_sem=recv_sems.at[k - 1],
                device_id=peer, device_id_type=MESH)
            cp.start()
            sends.append(cp)
        buf[me] = v_ref[...]
        for k in range(1, N_DEV):
            src = 4 * (1 - x if k & 4 else x) + 2 * (1 - y if k & 2 else y) + (1 - c if k & 1 else c)
            pltpu.make_async_remote_copy(
                src_ref=v_ref, dst_ref=buf.at[src], send_sem=send_sems.at[k - 1], recv_sem=recv_sems.at[k - 1],
                device_id=(x, y, c), device_id_type=MESH).wait_recv()
        for cp in sends:
            cp.wait_send()
        if reduce:
            acc = buf[0]
            for d in range(1, N_DEV):
                acc = acc + buf[d]
            o_ref[...] = acc

    sems = [pltpu.SemaphoreType.DMA((N_DEV - 1,)), pltpu.SemaphoreType.DMA((N_DEV - 1,))]
    if reduce:
        out_shape = SDS((r, w), F32)
        scratch = [pltpu.VMEM((N_DEV, r, w), F32)] + sems
    else:
        out_shape = SDS((N_DEV, r, w), F32)
        scratch = sems
    return pl.pallas_call(
        body, name=name, out_shape=out_shape, in_specs=[VMEM_WHOLE], out_specs=VMEM_WHOLE,
        scratch_shapes=scratch,
        compiler_params=pltpu.CompilerParams(vmem_limit_bytes=V7X_VMEM_LIMIT_BYTES),
    )(v)


def _cast_into_slot(name, gathered, shard, chip_idx):
    rows, cols = shard.shape
    tr = _pack_row_tile(rows)

    def body(k_ref, s_ref, g_ref, o_ref):
        o_ref[...] = s_ref[...].astype(BF16)

    return pl.pallas_call(
        body, name=name,
        grid_spec=pltpu.PrefetchScalarGridSpec(
            num_scalar_prefetch=1, grid=(rows // tr,),
            in_specs=[pl.BlockSpec((tr, cols), lambda i, k_ref: (i, 0)), pl.BlockSpec(memory_space=pl.ANY)],
            out_specs=pl.BlockSpec((None, tr, cols), lambda i, k_ref: (k_ref[0], i, 0))),
        out_shape=SDS(gathered.shape, BF16),
        input_output_aliases={2: 0},
        compiler_params=_params("parallel"),
    )(chip_idx, shard, gathered)


def _row_halves(ref, c):
    half = ref.shape[1] // 2
    return pl.ds(pl.multiple_of(c * half, 16), half), pl.ds(pl.multiple_of((1 - c) * half, 16), half)


def _gather_ici_copies(refs, send_sems, recv_sems):
    x, y, c = _my_place()
    k = 2 * x + y
    pairs = []
    for i, ref in enumerate(refs):
        mine, _ = _row_halves(ref, c)
        for j, (bx, by, kb) in enumerate(_other_chips(x, y)):
            sems = dict(send_sem=send_sems.at[3 * i + j], recv_sem=recv_sems.at[3 * i + j], device_id_type=MESH)
            send = pltpu.make_async_remote_copy(src_ref=ref.at[k, mine], dst_ref=ref.at[k, mine],
                                                device_id=(bx, by, c), **sems)
            arrival = pltpu.make_async_remote_copy(src_ref=ref.at[kb, mine], dst_ref=ref.at[kb, mine],
                                                   device_id=(bx, by, c), **sems)
            pairs.append((send, arrival))
    return pairs


def _gather_d2d_copies(refs, send_sems, recv_sems, first_sem):
    x, y, c = _my_place()
    pairs = []
    for i, ref in enumerate(refs):
        mine, theirs = _row_halves(ref, c)
        for j, (_, _, kb) in enumerate(_other_chips(x, y)):
            sem = first_sem + 3 * i + j
            sems = dict(send_sem=send_sems.at[sem], recv_sem=recv_sems.at[sem], device_id=(x, y, 1 - c),
                        device_id_type=MESH)
            send = pltpu.make_async_remote_copy(src_ref=ref.at[kb, mine], dst_ref=ref.at[kb, mine], **sems)
            arrival = pltpu.make_async_remote_copy(src_ref=ref.at[kb, theirs], dst_ref=ref.at[kb, theirs], **sems)
            pairs.append((send, arrival))
    return pairs


def _run_copies(pairs):
    for send, _ in pairs:
        send.start()
    for send, arrival in pairs:
        send.wait_send()
        arrival.wait_recv()


def _gather_now(name, gathered):
    n_w = len(gathered)

    def body(*refs):
        in_refs = refs[:n_w]
        send_sems, recv_sems = refs[2 * n_w:]
        _run_copies(_gather_ici_copies(in_refs, send_sems, recv_sems))
        _run_copies(_gather_d2d_copies(in_refs, send_sems, recv_sems, 3 * n_w))

    return pl.pallas_call(
        body, name=name, out_shape=[SDS(g.shape, g.dtype) for g in gathered],
        in_specs=[ANY] * n_w, out_specs=[ANY] * n_w, input_output_aliases={i: i for i in range(n_w)},
        scratch_shapes=[pltpu.SemaphoreType.DMA((6 * n_w,)), pltpu.SemaphoreType.DMA((6 * n_w,))],
    )(*gathered)


def _gather_start(name, gathered):
    n_w = len(gathered)

    def body(*refs):
        in_refs = refs[:n_w]
        send_sems, recv_sems = refs[n_w:n_w + 2]
        for send, _ in _gather_ici_copies(in_refs, send_sems, recv_sems):
            send.start()
        refs[-1][...] = jnp.zeros_like(refs[-1])

    out = pl.pallas_call(
        body, name=name,
        out_shape=(pltpu.SemaphoreType.DMA((3 * n_w,)), pltpu.SemaphoreType.DMA((3 * n_w,)),
                   *[pltpu.HBM(g.shape, g.dtype) for g in gathered], SDS((8, V7X_LANES), F32)),
        in_specs=[HBM_SPEC] * n_w, out_specs=(SEM_SPEC, SEM_SPEC, *[HBM_SPEC] * n_w, VMEM_WHOLE),
        input_output_aliases={i: 2 + i for i in range(n_w)},
        compiler_params=pltpu.CompilerParams(has_side_effects=DATAFLOW),
    )(*[pltpu.with_memory_space_constraint(g, pltpu.HBM) for g in gathered])
    return out[0], out[1], list(out[2:2 + n_w]), out[-1]


def _gather_wait(name, send_sems, recv_sems, gathered, after):
    n_w = len(gathered)

    def body(*refs):
        in_refs = refs[:n_w]
        send_sems, recv_sems = refs[n_w:n_w + 2]
        for send, arrival in _gather_ici_copies(in_refs, send_sems, recv_sems):
            send.wait_send()
            arrival.wait_recv()

    out = pl.pallas_call(
        body, name=name, out_shape=tuple(pltpu.HBM(g.shape, g.dtype) for g in gathered),
        in_specs=[*[HBM_SPEC] * n_w, SEM_SPEC, SEM_SPEC, ANY], out_specs=tuple([HBM_SPEC] * n_w),
        input_output_aliases={i: i for i in range(n_w)},
        compiler_params=pltpu.CompilerParams(has_side_effects=DATAFLOW),
    )(*gathered, send_sems, recv_sems, after)
    return list(out)


def _swap_fetched_with_sibling(name, gathered):
    n_w = len(gathered)

    def body(*refs):
        in_refs = refs[:n_w]
        send_sems, recv_sems = refs[2 * n_w:]
        _run_copies(_gather_d2d_copies(in_refs, send_sems, recv_sems, 0))

    return pl.pallas_call(
        body, name=name, out_shape=[SDS(g.shape, g.dtype) for g in gathered],
        in_specs=[ANY] * n_w, out_specs=[ANY] * n_w, input_output_aliases={i: i for i in range(n_w)},
        scratch_shapes=[pltpu.SemaphoreType.DMA((3 * n_w,)), pltpu.SemaphoreType.DMA((3 * n_w,))],
    )(*gathered)


def _sibling_swap_copies(g_refs, land_refs, send_sems, recv_sems):
    x, y, c = _my_place()
    copies = []
    for i, g_ref in enumerate(g_refs):
        half = g_ref.shape[1] // 2
        theirs = pl.ds(pl.multiple_of((1 - c) * half, 8), half)
        copies.append(pltpu.make_async_remote_copy(
            src_ref=g_ref.at[:, theirs], dst_ref=land_refs[i], send_sem=send_sems.at[i], recv_sem=recv_sems.at[i],
            device_id=(x, y, 1 - c), device_id_type=MESH))
    return copies


def _sibling_swap_start(name, grads):
    n_g = len(grads)

    def body(*refs):
        g_refs, land_refs = refs[:n_g], refs[n_g:2 * n_g]
        send_sems, recv_sems = refs[2 * n_g:2 * n_g + 2]
        for cp in _sibling_swap_copies(g_refs, land_refs, send_sems, recv_sems):
            cp.start()
        refs[-1][...] = jnp.zeros_like(refs[-1])

    lands = [pltpu.with_memory_space_constraint(lax.empty((g.shape[0], g.shape[1] // 2, g.shape[2]), g.dtype),
                                                pltpu.HBM) for g in grads]
    out = pl.pallas_call(
        body, name=name,
        out_shape=(pltpu.SemaphoreType.DMA((n_g,)), pltpu.SemaphoreType.DMA((n_g,)),
                   *[pltpu.HBM(g.shape, g.dtype) for g in grads], *[pltpu.HBM(l.shape, l.dtype) for l in lands],
                   SDS((8, V7X_LANES), F32)),
        in_specs=[HBM_SPEC] * (2 * n_g), out_specs=(SEM_SPEC, SEM_SPEC, *[HBM_SPEC] * (2 * n_g), VMEM_WHOLE),
        input_output_aliases={i: 2 + i for i in range(2 * n_g)},
        compiler_params=pltpu.CompilerParams(has_side_effects=DATAFLOW),
    )(*[pltpu.with_memory_space_constraint(g, pltpu.HBM) for g in grads], *lands)
    return out[0], out[1], list(out[2:2 + n_g]), list(out[2 + n_g:2 + 2 * n_g]), out[-1]


def _sibling_swap_wait(name, send_sems, recv_sems, grads, lands, after):
    n_g = len(grads)

    def body(*refs):
        g_refs, land_refs = refs[:n_g], refs[n_g:2 * n_g]
        send_sems, recv_sems = refs[2 * n_g:2 * n_g + 2]
        for cp in _sibling_swap_copies(g_refs, land_refs, send_sems, recv_sems):
            cp.wait_send()
            cp.wait_recv()

    out = pl.pallas_call(
        body, name=name,
        out_shape=(*[pltpu.HBM(g.shape, g.dtype) for g in grads], *[pltpu.HBM(l.shape, l.dtype) for l in lands]),
        in_specs=[*[HBM_SPEC] * (2 * n_g), SEM_SPEC, SEM_SPEC, ANY], out_specs=tuple([HBM_SPEC] * (2 * n_g)),
        input_output_aliases={i: i for i in range(2 * n_g)},
        compiler_params=pltpu.CompilerParams(has_side_effects=DATAFLOW),
    )(*grads, *lands, send_sems, recv_sems, after)
    return list(out[:n_g]), list(out[n_g:])


def _pack_row_tile(rows):
    for t in range(min(rows, 512), 7, -1):
        if rows % t == 0 and t % 8 == 0:
            return t
    return rows


def _add_sibling_half(name, grads, from_sibling, c_idx):
    n, R, w = grads.shape
    half = R // 2
    tr = _pack_row_tile(half)
    steps = half // tr

    def body(c_ref, g_ref, s_ref, o_ref):
        o_ref[...] = g_ref[...] + s_ref[...]

    return pl.pallas_call(
        body, name=name,
        grid_spec=pltpu.PrefetchScalarGridSpec(
            num_scalar_prefetch=1, grid=(n, steps),
            in_specs=[pl.BlockSpec((1, tr, w), lambda j, i, c_ref: (j, c_ref[0] * steps + i, 0)),
                      pl.BlockSpec((1, tr, w), lambda j, i, c_ref: (j, i, 0))],
            out_specs=pl.BlockSpec((1, tr, w), lambda j, i, c_ref: (j, i, 0))),
        out_shape=SDS((n, half, w), F32),
        compiler_params=_params("parallel", "parallel"),
    )(c_idx, grads, from_sibling)


HBM_SPEC = pl.BlockSpec(memory_space=pltpu.HBM)
SEM_SPEC = pl.BlockSpec(memory_space=pltpu.SEMAPHORE)
DATAFLOW = pltpu.SideEffectType.DATAFLOW_SIDE_EFFECTING


def _chip_scatter_copies(p_refs, land_refs, send_sems, recv_sems):
    x, y, c = _my_place()
    return [pltpu.make_async_remote_copy(
        src_ref=p_refs[i].at[kb], dst_ref=land_refs[i].at[j], send_sem=send_sems.at[3 * i + j],
        recv_sem=recv_sems.at[3 * i + j], device_id=(bx, by, c), device_id_type=MESH)
        for i in range(len(p_refs)) for j, (bx, by, kb) in enumerate(_other_chips(x, y))]


def _scatter_start(name, partials):
    n_p = len(partials)

    def body(*refs):
        p_refs, land_refs = refs[:n_p], refs[n_p:2 * n_p]
        send_sems, recv_sems = refs[2 * n_p:2 * n_p + 2]
        for cp in _chip_scatter_copies(p_refs, land_refs, send_sems, recv_sems):
            cp.start()
        refs[-1][...] = jnp.zeros_like(refs[-1])

    lands = [pltpu.with_memory_space_constraint(lax.empty((N_CHIPS - 1,) + p.shape[1:], p.dtype), pltpu.HBM)
             for p in partials]
    out = pl.pallas_call(
        body, name=name,
        out_shape=(pltpu.SemaphoreType.DMA((3 * n_p,)), pltpu.SemaphoreType.DMA((3 * n_p,)),
                   *[pltpu.HBM(p.shape, p.dtype) for p in partials], *[pltpu.HBM(l.shape, l.dtype) for l in lands],
                   SDS((8, V7X_LANES), F32)),
        in_specs=[HBM_SPEC] * (2 * n_p), out_specs=(SEM_SPEC, SEM_SPEC, *[HBM_SPEC] * (2 * n_p), VMEM_WHOLE),
        input_output_aliases={i: 2 + i for i in range(2 * n_p)},
        compiler_params=pltpu.CompilerParams(has_side_effects=DATAFLOW),
    )(*[pltpu.with_memory_space_constraint(p, pltpu.HBM) for p in partials], *lands)
    return out[0], out[1], list(out[2:2 + n_p]), list(out[2 + n_p:2 + 2 * n_p]), out[-1]


def _scatter_wait(name, send_sems, recv_sems, partials, lands, after):
    n_p = len(partials)

    def body(*refs):
        p_refs, land_refs = refs[:n_p], refs[n_p:2 * n_p]
        send_sems, recv_sems = refs[2 * n_p:2 * n_p + 2]
        for cp in _chip_scatter_copies(p_refs, land_refs, send_sems, recv_sems):
            cp.wait_send()
            cp.wait_recv()

    out = pl.pallas_call(
        body, name=name,
        out_shape=(*[pltpu.HBM(p.shape, p.dtype) for p in partials], *[pltpu.HBM(l.shape, l.dtype) for l in lands]),
        in_specs=[*[HBM_SPEC] * (2 * n_p), SEM_SPEC, SEM_SPEC, ANY], out_specs=tuple([HBM_SPEC] * (2 * n_p)),
        input_output_aliases={i: i for i in range(2 * n_p)},
        compiler_params=pltpu.CompilerParams(has_side_effects=DATAFLOW),
    )(*partials, *lands, send_sems, recv_sems, after)
    return list(out[:n_p]), list(out[n_p:])


def _sum_chip_partials(name, partial, received, shard, layer, place):
    n, half, w = partial.shape
    tr = _pack_row_tile(half)
    steps = half // tr

    def body(place_ref, p_ref, r_ref, shard_ref, o_ref):
        o_ref[...] = ((p_ref[0] + r_ref[0]) + r_ref[1]) + r_ref[2]

    return pl.pallas_call(
        body, name=name,
        grid_spec=pltpu.PrefetchScalarGridSpec(
            num_scalar_prefetch=1, grid=(steps,),
            in_specs=[pl.BlockSpec((1, tr, w), lambda i, place_ref: (place_ref[0], i, 0)),
                      pl.BlockSpec((n - 1, tr, w), lambda i, place_ref: (0, i, 0)),
                      pl.BlockSpec(memory_space=pl.ANY)],
            out_specs=pl.BlockSpec((tr, w), lambda i, place_ref: ((2 * layer + place_ref[1]) * steps + i, 0))),
        out_shape=SDS(shard.shape, F32),
        input_output_aliases={3: 0},
        compiler_params=_params("parallel"),
    )(place, partial, received, shard)


def _join_halves(shards, layers):
    n_s = len(shards)
    n_sem = sum(layers)

    def body(*refs):
        in_refs = refs[:n_s]
        send_sems, recv_sems = refs[2 * n_s:]
        x, y, c = _my_place()
        copies, sem = [], 0
        for ref, n_layers in zip(in_refs, layers):
            half = ref.shape[0] // (2 * n_layers)
            for layer in range(n_layers):
                mine = pl.ds(pl.multiple_of(layer * 2 * half + c * half, 8), half)
                theirs = pl.ds(pl.multiple_of(layer * 2 * half + (1 - c) * half, 8), half)
                send = pltpu.make_async_remote_copy(
                    src_ref=ref.at[mine], dst_ref=ref.at[mine], send_sem=send_sems.at[sem], recv_sem=recv_sems.at[sem],
                    device_id=(x, y, 1 - c), device_id_type=MESH)
                send.start()
                arrival = pltpu.make_async_remote_copy(
                    src_ref=ref.at[theirs], dst_ref=ref.at[theirs], send_sem=send_sems.at[sem],
                    recv_sem=recv_sems.at[sem], device_id=(x, y, 1 - c), device_id_type=MESH)
                copies.append((send, arrival))
                sem += 1
        for send, arrival in copies:
            send.wait_send()
            arrival.wait_recv()

    return pl.pallas_call(
        body, name="join_halves", out_shape=[SDS(s.shape, s.dtype) for s in shards],
        in_specs=[ANY] * n_s, out_specs=[ANY] * n_s,
        input_output_aliases={i: i for i in range(n_s)},
        scratch_shapes=[pltpu.SemaphoreType.DMA((n_sem,)), pltpu.SemaphoreType.DMA((n_sem,))],
    )(*shards)


def _adamw(name, w, g, m, v):
    rows, width = w.shape
    tr = _pack_row_tile(rows)

    def body(w_ref, g_ref, m_ref, v_ref, g_out_ref, d_ref, nm_ref, nv_ref):
        gg = g_ref[...]
        g_out_ref[...] = gg
        m_new = ADAM_B1 * m_ref[...] + (1.0 - ADAM_B1) * gg
        v_new = ADAM_B2 * v_ref[...] + (1.0 - ADAM_B2) * (gg * gg)
        m_hat = m_new / (1.0 - ADAM_B1 ** ADAM_STEP)
        v_hat = v_new / (1.0 - ADAM_B2 ** ADAM_STEP)
        d_ref[...] = -ADAM_LR * (m_hat / (jnp.sqrt(v_hat) + ADAM_EPS) + ADAM_WD * w_ref[...])
        nm_ref[...] = m_new
        nv_ref[...] = v_new

    spec = _rows(tr, width)
    return pl.pallas_call(
        body, name=name, grid=(rows // tr,),
        in_specs=[spec] * 4, out_specs=[spec] * 4,
        out_shape=[SDS((rows, width), F32)] * 4,
        compiler_params=_params("parallel"),
    )(w, g, m, v)


WEIGHT_NAMES = ['norm_mix', 'norm_ffn', 'attn_w_qkv', 'attn_b_qkv', 'attn_sinks', 'attn_w_o', 'attn_b_o',
                'conv_w_pw1', 'conv_b_pw1', 'conv_w_dw', 'conv_b_dw', 'conv_ln_g', 'conv_ln_b', 'conv_w_pw2',
                'conv_b_pw2', 'ffn_w1', 'ffn_w3', 'ffn_w2', 'norm_final']
BIG = ['attn_w_qkv', 'attn_w_o', 'conv_w_pw1', 'conv_w_pw2', 'ffn_w1', 'ffn_w3', 'ffn_w2']
COLUMN_SPLIT = ('attn_w_qkv', 'conv_w_pw1', 'ffn_w1', 'ffn_w3')
SMALL_SPLIT = ['conv_b_pw1', 'conv_w_dw', 'conv_b_dw', 'conv_ln_g', 'conv_ln_b', 'conv_b_pw2']
SMALL_WHOLE = ['norm_mix', 'norm_ffn', 'attn_b_qkv', 'attn_sinks', 'attn_b_o', 'norm_final']


def _pack_rows(arrays, dtype, row_multiple):
    flat = jnp.concatenate([a.astype(dtype).reshape(-1) for a in arrays])
    rows = -(-flat.shape[0] // PACK_W)
    rows = -(-rows // row_multiple) * row_multiple
    return jnp.pad(flat, (0, rows * PACK_W - flat.shape[0])).reshape(rows, PACK_W)


def _unpack_rows(pack, shapes):
    flat = pack.reshape(-1)
    out, at = [], 0
    for shape in shapes:
        size = 1
        for s in shape:
            size *= s
        out.append(flat[at:at + size].reshape(shape))
        at += size
    return out


def _join_chip_axis(name, parts):
    axis = parts.ndim - 1 if name in COLUMN_SPLIT or name in SMALL_SPLIT else parts.ndim - 2
    moved = jnp.moveaxis(parts, 0, axis - 1)
    shape = list(moved.shape)
    shape[axis - 1:axis + 1] = [shape[axis - 1] * shape[axis]]
    return moved.reshape(shape)


def _split_chip_axis(name, whole, shard_shape):
    axis = len(shard_shape) - 1 if name in COLUMN_SPLIT or name in SMALL_SPLIT else len(shard_shape) - 2
    shape = list(whole.shape)
    shape[axis:axis + 1] = [N_CHIPS, shard_shape[axis]]
    return jnp.moveaxis(whole.reshape(shape), axis, 0)


def kernel(x, norm_mix, norm_ffn, attn_w_qkv, attn_b_qkv, attn_sinks, attn_w_o, attn_b_o, conv_w_pw1, conv_b_pw1, conv_w_dw, conv_b_dw, conv_ln_g, conv_ln_b, conv_w_pw2, conv_b_pw2, ffn_w1, ffn_w3, ffn_w2, norm_final, loss_target, m_norm_mix, m_norm_ffn, m_attn_w_qkv, m_attn_b_qkv, m_attn_sinks, m_attn_w_o, m_attn_b_o, m_conv_w_pw1, m_conv_b_pw1, m_conv_w_dw, m_conv_b_dw, m_conv_ln_g, m_conv_ln_b, m_conv_w_pw2, m_conv_b_pw2, m_ffn_w1, m_ffn_w3, m_ffn_w2, m_norm_final, v_norm_mix, v_norm_ffn, v_attn_w_qkv, v_attn_b_qkv, v_attn_sinks, v_attn_w_o, v_attn_b_o, v_conv_w_pw1, v_conv_b_pw1, v_conv_w_dw, v_conv_b_dw, v_conv_ln_g, v_conv_ln_b, v_conv_w_pw2, v_conv_b_pw2, v_ffn_w1, v_ffn_w3, v_ffn_w2, v_norm_final):
    w = dict(zip(WEIGHT_NAMES, (norm_mix, norm_ffn, attn_w_qkv, attn_b_qkv, attn_sinks, attn_w_o, attn_b_o,
                                conv_w_pw1, conv_b_pw1, conv_w_dw, conv_b_dw, conv_ln_g, conv_ln_b, conv_w_pw2,
                                conv_b_pw2, ffn_w1, ffn_w3, ffn_w2, norm_final)))
    m = dict(zip(WEIGHT_NAMES, (m_norm_mix, m_norm_ffn, m_attn_w_qkv, m_attn_b_qkv, m_attn_sinks, m_attn_w_o,
                                m_attn_b_o, m_conv_w_pw1, m_conv_b_pw1, m_conv_w_dw, m_conv_b_dw, m_conv_ln_g,
                                m_conv_ln_b, m_conv_w_pw2, m_conv_b_pw2, m_ffn_w1, m_ffn_w3, m_ffn_w2, m_norm_final)))
    v = dict(zip(WEIGHT_NAMES, (v_norm_mix, v_norm_ffn, v_attn_w_qkv, v_attn_b_qkv, v_attn_sinks, v_attn_w_o,
                                v_attn_b_o, v_conv_w_pw1, v_conv_b_pw1, v_conv_w_dw, v_conv_b_dw, v_conv_ln_g,
                                v_conv_ln_b, v_conv_w_pw2, v_conv_b_pw2, v_ffn_w1, v_ffn_w3, v_ffn_w2, v_norm_final)))
    T, D = x.shape[1], x.shape[2]
    c_idx = lax.axis_index("c").astype(jnp.int32).reshape(1)
    chip = (2 * lax.axis_index("x") + lax.axis_index("y")).astype(jnp.int32)

    as_rows = lambda a: a.reshape(-1, a.shape[-1])
    slabs = {n: _cast_into_slot(f"cast_{n}", lax.empty((N_CHIPS,) + as_rows(w[n]).shape, BF16), as_rows(w[n]),
                                chip.reshape(1)) for n in BIG}
    first, later = BIG[:2], BIG[2:]
    qkv_parts, w_o_parts = _gather_now("gather_attn", [slabs[n] for n in first])
    send_sems, recv_sems, travelling, gather_started = _gather_start("gather_start", [slabs[n] for n in later])
    layers = ffn_w1.shape[0]
    small_shapes = [w[n].shape for n in SMALL_SPLIT]
    small_all = _exchange_small("gather_small", _pack_rows([w[n] for n in SMALL_SPLIT], F32, 8), reduce=False)
    per_chip = [_unpack_rows(small_all[2 * j], small_shapes) for j in range(N_CHIPS)]
    full = {}
    for i, n in enumerate(SMALL_SPLIT):
        full[n] = _join_chip_axis(n, jnp.stack([per_chip[j][i] for j in range(N_CHIPS)]))

    def other_weights(after):
        landed = _gather_wait("gather_wait", send_sems, recv_sems, travelling, after)
        gathered = dict(zip(later, _swap_fetched_with_sibling("gather_swap", landed)))
        return {"conv_w_pw1": gathered["conv_w_pw1"], "conv_w_pw2": gathered["conv_w_pw2"].reshape(-1, D),
                "ffn_w1": gathered["ffn_w1"].reshape(N_CHIPS, layers, D, -1),
                "ffn_w3": gathered["ffn_w3"].reshape(N_CHIPS, layers, D, -1),
                "ffn_w2": gathered["ffn_w2"].reshape(N_CHIPS, layers, -1, D)}

    p = {
        "norm_mix": norm_mix, "norm_ffn": norm_ffn, "norm_final": norm_final.reshape(1, D),
        "attn_w_qkv": jnp.moveaxis(qkv_parts, 0, 1).reshape(D, -1), "attn_b_qkv": attn_b_qkv,
        "attn_sinks": attn_sinks, "attn_w_o": w_o_parts.reshape(-1, D), "attn_b_o": attn_b_o,
        "conv_b_pw1": full["conv_b_pw1"], "conv_w_dw": full["conv_w_dw"][0],
        "conv_b_dw": full["conv_b_dw"], "conv_ln_g": full["conv_ln_g"], "conv_ln_b": full["conv_ln_b"],
        "conv_b_pw2": full["conv_b_pw2"], "gather_started": gather_started, "other_weights": other_weights,
    }
    swapping, in_flight = {}, []

    def reduce_begin(tag, grads):
        keys = list(grads)
        *handles, begun = _sibling_swap_start(f"sibling_swap_start_{tag}", [grads[k] for k in keys])
        swapping[tag] = (keys, handles)
        return begun

    def reduce_send(tag, after):
        keys, (swap_send, swap_recv, grads, lands) = swapping[tag]
        grads, from_sibling = _sibling_swap_wait(f"sibling_swap_wait_{tag}", swap_send, swap_recv, grads, lands, after)
        partials = [_add_sibling_half(f"add_sibling_half_{tag}{i}", gr, fs, c_idx)
                    for i, (gr, fs) in enumerate(zip(grads, from_sibling))]
        *handles, sent = _scatter_start(f"scatter_start_{tag}", partials)
        in_flight.append((tag, keys, handles))
        return sent

    loss_part, dx, g = _local_step(x[0], loss_target[0], p, reduce_begin, reduce_send)
    for n in SMALL_WHOLE + SMALL_SPLIT:
        g[n] = g[n].reshape((-1,) + g[n].shape[-2:]) if w[n].ndim == 3 else g[n].reshape(w[n].shape[:-1] + (-1,))

    place = jnp.stack([chip, c_idx[0]])
    shard_grad = {n: lax.empty(as_rows(w[n]).shape, F32) for n in BIG}
    for tag, keys, (send_sems, recv_sems, partials, lands) in in_flight:
        partials, received = _scatter_wait(f"scatter_wait_{tag}", send_sems, recv_sems, partials, lands, dx)
        for i, (n, layer) in enumerate(keys):
            shard_grad[n] = _sum_chip_partials(f"sum_chip_partials_{tag}{i}", partials[i], received[i],
                                               shard_grad[n], layer, place)
    g_big = dict(zip(BIG, _join_halves([shard_grad[n] for n in BIG], [w[n].shape[0] for n in BIG])))
    big_out = {}
    for n in BIG:
        step = _adamw(f"adamw_{n}", as_rows(w[n]), g_big[n], as_rows(m[n]), as_rows(v[n]))
        big_out[n] = [a.reshape(w[n].shape) for a in step]

    small_whole_shapes = [w[n].shape for n in SMALL_WHOLE]
    small_full_shapes = [g[n].shape for n in SMALL_SPLIT]
    reduced = _exchange_small(
        "reduce_small", _pack_rows([loss_part] + [g[n] for n in SMALL_WHOLE] + [g[n] for n in SMALL_SPLIT], F32, 8),
        reduce=True)
    pieces = _unpack_rows(reduced, [(1,)] + small_whole_shapes + small_full_shapes)
    loss = pieces[0].reshape(())
    g_small = dict(zip(SMALL_WHOLE, pieces[1:1 + len(SMALL_WHOLE)]))
    for n, whole in zip(SMALL_SPLIT, pieces[1 + len(SMALL_WHOLE):]):
        parts = _split_chip_axis(n, whole, w[n].shape)
        g_small[n] = lax.dynamic_index_in_dim(parts, chip, axis=0, keepdims=False)
    small = SMALL_WHOLE + SMALL_SPLIT
    _, d_small, m_small, v_small = _adamw(
        "adamw_small", _pack_rows([w[n] for n in small], F32, 8), _pack_rows([g_small[n] for n in small], F32, 8),
        _pack_rows([m[n] for n in small], F32, 8), _pack_rows([v[n] for n in small], F32, 8))

    outs = {}
    for slot, (tag, small_pack) in enumerate((("g", None), ("d", d_small), ("m", m_small), ("v", v_small))):
        vals = {n: big_out[n][slot] for n in BIG}
        if small_pack is None:
            vals.update(g_small)
        else:
            vals.update(zip(small, _unpack_rows(small_pack, [w[n].shape for n in small])))
        outs[tag] = vals
    return (loss, dx.reshape(1, T, D), *[outs["g"][n] for n in WEIGHT_NAMES], *[outs["d"][n] for n in WEIGHT_NAMES],
            *[outs["m"][n] for n in WEIGHT_NAMES], *[outs["v"][n] for n in WEIGHT_NAMES])
```

```python
import functools

import jax
import jax.numpy as jnp
from jax import lax
from jax.experimental import pallas as pl
from jax.experimental.pallas import tpu as pltpu

F32 = jnp.float32
BF16 = jnp.bfloat16
SDS = jax.ShapeDtypeStruct
MESH = pl.DeviceIdType.MESH

HEAD_DIM = 64
N_Q_HEADS = 16
N_KV_HEADS = 2
Q_PER_KV = N_Q_HEADS // N_KV_HEADS
ATTN_BLOCK = 128
ROPE_THETA = 10000.0
CONV_WIDTH = 31
CONV_HALO = 32
CONV_FIRST_TAP = CONV_HALO - CONV_WIDTH + 1
CONV_ROW_CHUNK = 64
CONV_LANE_CHUNK = 256
CONV_GRAD_UNROLL = 4
RMS_EPS = 1e-5
LN_EPS = 1e-5
ADAM_LR = 0.001
ADAM_B1 = 0.9
ADAM_B2 = 0.999
ADAM_EPS = 1e-08
ADAM_WD = 0.01
ADAM_STEP = 10

V7X_LANES = 128
V7X_SUBLANES = 8
V7X_VMEM_LIMIT_BYTES = 56 * 1024 * 1024

N_CHIPS = 4
N_DEV = 8
PACK_W = 1024

MASK_VALUE = -1e30


def _params(*semantics):
    return pltpu.CompilerParams(dimension_semantics=semantics, vmem_limit_bytes=V7X_VMEM_LIMIT_BYTES)


def _rows(tm, width):
    return pl.BlockSpec((tm, width), lambda i: (i, 0))


def _whole(shape):
    return pl.BlockSpec(shape, lambda *_: (0,) * len(shape))


def _rms_rstd(h):
    return lax.rsqrt(jnp.mean(h * h, axis=-1, keepdims=True) + RMS_EPS)


def _silu_and_grad(z):
    sg = jax.nn.sigmoid(z)
    return z * sg, sg * (1.0 + z * (1.0 - sg))


def _swap_rope_halves(t):
    w = t.shape[1]
    half = HEAD_DIM // 2
    lane = lax.broadcasted_iota(jnp.int32, t.shape, 1)
    upper = pltpu.roll(t, w - half, 1)
    lower = pltpu.roll(t, half, 1)
    return jnp.where(lane % HEAD_DIM < half, upper, lower)


def _rope(t, cos_ref, sin_ref):
    reps = t.shape[1] // V7X_LANES
    c = jnp.tile(cos_ref[...], (1, reps))
    s = jnp.tile(sin_ref[...], (1, reps))
    return t * c + _swap_rope_halves(t) * s


def _rope_transposed(dt, cos_ref, sin_ref):
    reps = dt.shape[1] // V7X_LANES
    c = jnp.tile(cos_ref[...], (1, reps))
    s = jnp.tile(sin_ref[...], (1, reps))
    return dt * c + _swap_rope_halves(dt * s)


def _rope_tables(seq_len):
    pos = jnp.arange(seq_len, dtype=F32)
    inv_freq = ROPE_THETA ** (-jnp.arange(0, HEAD_DIM, 2, dtype=F32) / HEAD_DIM)
    ang = pos[:, None] * jnp.tile(inv_freq, 2 * V7X_LANES // HEAD_DIM)[None, :]
    upper_half = jnp.arange(V7X_LANES) % HEAD_DIM >= HEAD_DIM // 2
    return jnp.cos(ang), jnp.where(upper_half[None, :], jnp.sin(ang), -jnp.sin(ang))


def _qkv_proj(h, g, w, b, cos, sin, after):
    T, D = h.shape
    N = w.shape[1]
    tm = min(512, T)
    rope_w = N - N_KV_HEADS * HEAD_DIM

    def body(h_ref, g_ref, w_ref, b_ref, cos_ref, sin_ref, _, y_ref, o_ref):
        hh = h_ref[...]
        y = (hh * _rms_rstd(hh) * g_ref[...]).astype(BF16)
        y_ref[...] = y
        acc = jnp.dot(y, w_ref[...], preferred_element_type=F32) + b_ref[...]
        o_ref[:, :rope_w] = _rope(acc[:, :rope_w], cos_ref, sin_ref).astype(BF16)
        o_ref[:, rope_w:] = acc[:, rope_w:].astype(BF16)

    return pl.pallas_call(
        body, name="qkv_proj", grid=(T // tm,),
        in_specs=[_rows(tm, D), _whole((1, D)), _whole((D, N)), _whole((1, N)),
                  _rows(tm, V7X_LANES), _rows(tm, V7X_LANES), pl.BlockSpec(memory_space=pl.ANY)],
        out_specs=[_rows(tm, D), _rows(tm, N)],
        out_shape=[SDS((T, D), BF16), SDS((T, N), BF16)],
        compiler_params=_params("parallel"),
    )(h, g, w, b, cos, sin, after)


def _pw1_proj(h, g, w, b):
    T, D = h.shape
    n = w.shape[2]
    N = N_CHIPS * n
    tm = min(512, T)

    def body(h_ref, g_ref, w_ref, b_ref, y_ref, o_ref):
        hh = h_ref[...]
        y = (hh * _rms_rstd(hh) * g_ref[...]).astype(BF16)
        y_ref[...] = y
        for j in range(N_CHIPS):
            cols = slice(j * n, (j + 1) * n)
            o_ref[:, cols] = jnp.dot(y, w_ref[j], preferred_element_type=F32) + b_ref[:, cols]

    return pl.pallas_call(
        body, name="pw1_proj", grid=(T // tm,),
        in_specs=[_rows(tm, D), _whole((1, D)), _whole((N_CHIPS, D, n)), _whole((1, N))],
        out_specs=[_rows(tm, D), _rows(tm, N)],
        out_shape=[SDS((T, D), BF16), SDS((T, N), F32)],
        compiler_params=_params("parallel"),
    )(h, g, w, b)


PAIRS_PER_KV = Q_PER_KV // 2


def _upper_lanes(shape):
    return lax.broadcasted_iota(jnp.int32, shape, 1) >= HEAD_DIM


def _swap_lane_halves(t):
    return pltpu.roll(t.astype(F32), HEAD_DIM, 1).astype(t.dtype)


def _kv_operands(g, t):
    swapped = _swap_lane_halves(t)
    in_lower, in_upper = (t, swapped) if g == 0 else (swapped, t)
    upper = _upper_lanes(t.shape)
    zero = jnp.zeros_like(t)
    return jnp.where(upper, zero, in_lower), jnp.where(upper, in_upper, zero)


def _group_heads(g):
    pairs = range(g * PAIRS_PER_KV, (g + 1) * PAIRS_PER_KV)
    return [2 * hp for hp in pairs] + [2 * hp + 1 for hp in pairs]


def _pair_rows(ref, g):
    pairs = range(g * PAIRS_PER_KV, (g + 1) * PAIRS_PER_KV)
    return jnp.concatenate([ref[:, hp * 2 * HEAD_DIM:(hp + 1) * 2 * HEAD_DIM] for hp in pairs], axis=0)


def _from_previous_block(rows):
    row = lax.broadcasted_iota(jnp.int32, (ATTN_BLOCK, ATTN_BLOCK), 0)
    col = lax.broadcasted_iota(jnp.int32, (ATTN_BLOCK, ATTN_BLOCK), 1)
    return jnp.concatenate([col > row] * (rows // ATTN_BLOCK), axis=0)


def _folded_probs(n, q_groups, k_prev_groups, k_cur_groups, sink_ref, prev_part):
    def scores(q, k):
        return lax.dot_general(q, k, (((1,), (1,)), ((), ())), preferred_element_type=F32)

    s_prev = jnp.concatenate([scores(q, k[i]) for q, k in zip(q_groups, k_prev_groups) for i in range(2)], axis=0)
    s_cur = jnp.concatenate([scores(q, k[i]) for q, k in zip(q_groups, k_cur_groups) for i in range(2)], axis=0)
    s_prev = jnp.where(n > 0, s_prev, MASK_VALUE * (HEAD_DIM ** 0.5))
    s = jnp.where(prev_part, s_prev, s_cur) * (HEAD_DIM ** -0.5)
    heads = [h for g in range(N_KV_HEADS) for h in _group_heads(g)]
    sink = jnp.concatenate([jnp.broadcast_to(sink_ref[0:1, h:h + 1], (ATTN_BLOCK, 1)) for h in heads], axis=0)
    m = jnp.maximum(jnp.max(s, axis=1, keepdims=True), sink)
    p = jnp.exp(s - m)
    e_sink = jnp.exp(sink - m)
    inv = 1.0 / (jnp.sum(p, axis=1, keepdims=True) + e_sink)
    return p * inv, e_sink * inv


def _split_folded(t, prev_part):
    tb = t.astype(BF16)
    zero = jnp.zeros_like(tb)
    return jnp.where(prev_part, tb, zero), jnp.where(prev_part, zero, tb)


def _attn_specs(T):
    nb = T // ATTN_BLOCK
    kcol = N_Q_HEADS * HEAD_DIM // V7X_LANES
    cur = lambda n: jnp.minimum(n, nb - 1)
    prev = lambda n: jnp.maximum(jnp.minimum(n, nb - 1) - 1, 0)
    q_spec = pl.BlockSpec((ATTN_BLOCK, N_Q_HEADS * HEAD_DIM), lambda n: (cur(n), 0))
    kc_spec = pl.BlockSpec((ATTN_BLOCK, V7X_LANES), lambda n: (cur(n), kcol))
    kp_spec = pl.BlockSpec((ATTN_BLOCK, V7X_LANES), lambda n: (prev(n), kcol))
    vc_spec = pl.BlockSpec((ATTN_BLOCK, V7X_LANES), lambda n: (cur(n), kcol + 1))
    vp_spec = pl.BlockSpec((ATTN_BLOCK, V7X_LANES), lambda n: (prev(n), kcol + 1))
    return q_spec, kc_spec, kp_spec, vc_spec, vp_spec


def _attn_fwd(qkv, sinks):
    T = qkv.shape[0]
    nb = T // ATTN_BLOCK
    qw = N_Q_HEADS * HEAD_DIM

    def body(q_ref, kc_ref, kp_ref, vc_ref, vp_ref, sink_ref, o_ref):
        n = pl.program_id(0)
        prev_part = _from_previous_block(N_Q_HEADS * ATTN_BLOCK)
        half = PAIRS_PER_KV * ATTN_BLOCK
        groups = range(N_KV_HEADS)
        probs, _ = _folded_probs(n, [_pair_rows(q_ref, g) for g in groups],
                                 [_kv_operands(g, kp_ref[...]) for g in groups],
                                 [_kv_operands(g, kc_ref[...]) for g in groups], sink_ref, prev_part)
        p_prev, p_cur = _split_folded(probs, prev_part)
        for g in groups:
            v_prev, v_cur = _kv_operands(g, vp_ref[...]), _kv_operands(g, vc_ref[...])
            even, odd = slice(2 * g * half, (2 * g + 1) * half), slice((2 * g + 1) * half, (2 * g + 2) * half)
            o = (jnp.dot(p_prev[even], v_prev[0], preferred_element_type=F32)
                 + jnp.dot(p_cur[even], v_cur[0], preferred_element_type=F32)
                 + jnp.dot(p_prev[odd], v_prev[1], preferred_element_type=F32)
                 + jnp.dot(p_cur[odd], v_cur[1], preferred_element_type=F32))
            for i in range(PAIRS_PER_KV):
                hp = g * PAIRS_PER_KV + i
                o_ref[:, hp * 2 * HEAD_DIM:(hp + 1) * 2 * HEAD_DIM] = (
                    o[i * ATTN_BLOCK:(i + 1) * ATTN_BLOCK].astype(BF16))

    return pl.pallas_call(
        body, name="attn_fwd", grid=(nb,),
        in_specs=[*_attn_specs(T), _whole((1, N_Q_HEADS))],
        out_specs=_rows(ATTN_BLOCK, qw),
        out_shape=SDS((T, qw), BF16),
        compiler_params=_params("parallel"),
    )(qkv, qkv, qkv, qkv, qkv, sinks)


def _mm_res(name, a, w, b, res, g):
    T, K = a.shape
    D = w.shape[1]
    tm = min(512, T)

    def body(a_ref, w_ref, b_ref, r_ref, g_ref, o_ref, f_ref):
        h = jnp.dot(a_ref[...], w_ref[...], preferred_element_type=F32) + b_ref[...] + r_ref[...]
        o_ref[...] = h
        f_ref[...] = (h * _rms_rstd(h) * g_ref[...]).astype(BF16)

    return pl.pallas_call(
        body, name=name, grid=(T // tm,),
        in_specs=[_rows(tm, K), _whole((K, D)), _whole((1, D)), _rows(tm, D), _whole((1, D))],
        out_specs=[_rows(tm, D), _rows(tm, D)],
        out_shape=[SDS((T, D), F32), SDS((T, D), BF16)],
        compiler_params=_params("parallel"),
    )(a, w, b, res, g)


def _ffn_down(name, s, w2, layer, res):
    _, T, n = s.shape
    D = w2.shape[3]
    tm = min(512, T)

    def body(s_ref, w_ref, r_ref, o_ref):
        acc = r_ref[...]
        for j in range(N_CHIPS):
            acc = acc + jnp.dot(s_ref[j], w_ref[j], preferred_element_type=F32)
        o_ref[...] = acc

    return pl.pallas_call(
        body, name=name, grid=(T // tm,),
        in_specs=[pl.BlockSpec((N_CHIPS, tm, n), lambda i: (0, i, 0)),
                  pl.BlockSpec((N_CHIPS, None, n, D), lambda i: (0, layer, 0, 0)), _rows(tm, D)],
        out_specs=_rows(tm, D),
        out_shape=SDS((T, D), F32),
        compiler_params=_params("parallel"),
    )(s, w2, res)


def _ffn_up(name, f, w1, w3, layer):
    T, D = f.shape
    n = w1.shape[3]
    tm = min(1024, T)

    def body(f_ref, w1_ref, w3_ref, act_ref, gg_ref, s_ref):
        ff = f_ref[...]
        g1 = jnp.dot(ff, w1_ref[...], preferred_element_type=F32)
        g3 = jnp.dot(ff, w3_ref[...], preferred_element_type=F32)
        act, dact = _silu_and_grad(g1)
        act_ref[...] = act.astype(BF16)
        gg_ref[...] = (g3 * dact).astype(BF16)
        s_ref[...] = (act * g3).astype(BF16)

    slab = pl.BlockSpec((None, tm, n), lambda j, i: (j, i, 0))
    wslab = pl.BlockSpec((None, None, D, n), lambda j, i: (j, layer, 0, 0))
    hidden = SDS((N_CHIPS, T, n), BF16)
    return pl.pallas_call(
        body, name=name, grid=(N_CHIPS, T // tm),
        in_specs=[pl.BlockSpec((tm, D), lambda j, i: (i, 0)), wslab, wslab],
        out_specs=[slab, slab, slab],
        out_shape=[hidden, hidden, hidden],
        compiler_params=_params("parallel", "parallel"),
    )(f, w1, w3)


def _glu(a, d):
    return a[:, :d] * jax.nn.sigmoid(a[:, d:])


def _conv_tile(T):
    return min(256, T)


def _fill_shifted(sh_ref, tc):
    n = tc + CONV_HALO - V7X_SUBLANES
    for r in range(1, V7X_SUBLANES):
        sh_ref[r, 0:n, :] = sh_ref[0, pl.ds(r, n), :]


def _depthwise_taps(sh_ref, w_ref, offsets, bias_ref, out_ref, tc):
    D = out_ref.shape[1]

    def chunk(i, carry):
        t0 = pl.multiple_of(i * CONV_ROW_CHUNK, CONV_ROW_CHUNK)
        for cb in range(D // CONV_LANE_CHUNK):
            cs = slice(cb * CONV_LANE_CHUNK, (cb + 1) * CONV_LANE_CHUNK)
            acc = jnp.zeros((CONV_ROW_CHUNK, CONV_LANE_CHUNK), F32)
            for r in range(V7X_SUBLANES):
                taps = [(j, o // V7X_SUBLANES) for j, o in enumerate(offsets) if o % V7X_SUBLANES == r]
                if not taps:
                    continue
                span = CONV_ROW_CHUNK + V7X_SUBLANES * max(q for _, q in taps)
                rows = sh_ref[r, pl.ds(t0, span), cs]
                for j, q in taps:
                    acc = acc + rows[V7X_SUBLANES * q:V7X_SUBLANES * q + CONV_ROW_CHUNK] * w_ref[j:j + 1, cs]
            if bias_ref is not None:
                acc = acc + bias_ref[:, cs]
            out_ref[pl.ds(t0, CONV_ROW_CHUNK), cs] = acc
        return carry

    lax.fori_loop(0, tc // CONV_ROW_CHUNK, chunk, 0)


def _depthwise_tap_grads(dy_sh, x_sh, offsets, dw_ref, tc):
    D = dw_ref.shape[1]
    for cb in range(D // V7X_LANES):
        cs = slice(cb * V7X_LANES, (cb + 1) * V7X_LANES)

        def row_tiles(i, accs, cs=cs):
            for k in range(CONV_GRAD_UNROLL):
                t0 = pl.multiple_of(i * (CONV_GRAD_UNROLL * V7X_SUBLANES), V7X_SUBLANES) + k * V7X_SUBLANES
                d = dy_sh[0, pl.ds(t0, V7X_SUBLANES), cs]
                accs = tuple(
                    acc + d * x_sh[o % V7X_SUBLANES, pl.ds(t0 + o // V7X_SUBLANES * V7X_SUBLANES, V7X_SUBLANES), cs]
                    for acc, o in zip(accs, offsets))
            return accs

        zero = jnp.zeros((V7X_SUBLANES, V7X_LANES), F32)
        accs = lax.fori_loop(0, tc // (CONV_GRAD_UNROLL * V7X_SUBLANES), row_tiles, tuple(zero for _ in offsets))
        for j, acc in enumerate(accs):
            dw_ref[j:j + 1, cs] += jnp.sum(acc, axis=0, keepdims=True)


def _conv_fwd(a, w_dw, b_dw, ln_g, ln_b):
    T = a.shape[0]
    D = a.shape[1] // 2
    tc = _conv_tile(T)
    per = tc // CONV_HALO

    def body(a_ref, ah_ref, w_ref, bdw_ref, lg_ref, lb_ref, c_ref, act_ref, u_sh):
        i = pl.program_id(0)
        u_sh[0, 0:CONV_HALO, :] = jnp.where(i > 0, _glu(ah_ref[...], D), 0.0)
        u_sh[0, CONV_HALO:, :] = _glu(a_ref[...], D)
        _fill_shifted(u_sh, tc)
        _depthwise_taps(u_sh, w_ref, [CONV_FIRST_TAP + j for j in range(CONV_WIDTH)], bdw_ref, c_ref, tc)
        c = c_ref[...]
        xc = c - jnp.mean(c, axis=-1, keepdims=True)
        z = xc * lax.rsqrt(jnp.mean(xc * xc, axis=-1, keepdims=True) + LN_EPS)
        l = z * lg_ref[...] + lb_ref[...]
        act_ref[...] = (l * jax.nn.sigmoid(l)).astype(BF16)

    return pl.pallas_call(
        body, name="conv_fwd", grid=(T // tc,),
        in_specs=[_rows(tc, 2 * D),
                  pl.BlockSpec((CONV_HALO, 2 * D), lambda i: (jnp.maximum(i * per - 1, 0), 0)),
                  _whole((CONV_WIDTH, D)), _whole((1, D)), _whole((1, D)), _whole((1, D))],
        out_specs=[_rows(tc, D), _rows(tc, D)],
        out_shape=[SDS((T, D), F32), SDS((T, D), BF16)],
        scratch_shapes=[pltpu.VMEM((V7X_SUBLANES, tc + CONV_HALO, D), F32)],
        compiler_params=_params("parallel"),
    )(a, a, w_dw, b_dw, ln_g, ln_b)


def _final_loss(h, g, target):
    T, D = h.shape
    tm = min(512, T)

    def body(h_ref, g_ref, t_ref, dh_ref, loss_ref, dg_ref):
        @pl.when(pl.program_id(0) == 0)
        def _():
            loss_ref[...] = jnp.zeros_like(loss_ref)
            dg_ref[...] = jnp.zeros_like(dg_ref)

        hh = h_ref[...]
        r = _rms_rstd(hh)
        g = g_ref[...]
        d = hh * r * g - t_ref[...]
        loss_ref[...] += 0.5 * jnp.sum(jnp.mean(d * d, axis=-1, keepdims=True), axis=0, keepdims=True)
        dout = d * (1.0 / D)
        dg_ref[...] += jnp.sum(dout * (hh * r), axis=0, keepdims=True)
        dxh = dout * g
        dh_ref[...] = r * dxh - hh * (r * r * r) * jnp.mean(dxh * hh, axis=-1, keepdims=True)

    return pl.pallas_call(
        body, name="final_loss", grid=(T // tm,),
        in_specs=[_rows(tm, D), _whole((1, D)), _rows(tm, D)],
        out_specs=[_rows(tm, D), _whole((1, 1)), _whole((1, D))],
        out_shape=[SDS((T, D), F32), SDS((1, 1), F32), SDS((1, D), F32)],
        compiler_params=_params("arbitrary"),
    )(h, g, target)


def _ffn_bwd_act(name, dh, w2, layer, act, gate_grad):
    T, D = dh.shape
    n = w2.shape[2]
    tm = min(1024, T)

    def body(dh_ref, w2_ref, act_ref, gg_ref, dg1_ref, dg3_ref):
        ds = lax.dot_general(dh_ref[...].astype(BF16), w2_ref[...], (((1,), (1,)), ((), ())),
                             preferred_element_type=F32)
        dg1_ref[...] = (ds * gg_ref[...].astype(F32)).astype(BF16)
        dg3_ref[...] = (ds * act_ref[...].astype(F32)).astype(BF16)

    slab = pl.BlockSpec((None, tm, n), lambda i, j: (j, i, 0))
    hidden = SDS((N_CHIPS, T, n), BF16)
    return pl.pallas_call(
        body, name=name, grid=(T // tm, N_CHIPS),
        in_specs=[pl.BlockSpec((tm, D), lambda i, j: (i, 0)),
                  pl.BlockSpec((None, None, n, D), lambda i, j: (j, layer, 0, 0)), slab, slab],
        out_specs=[slab, slab],
        out_shape=[hidden, hidden],
        compiler_params=_params("parallel", "arbitrary"),
    )(dh, w2, act, gate_grad)


def _dot_tn(a, b):
    return lax.dot_general(a.astype(BF16), b.astype(BF16), (((0,), (0,)), ((), ())), preferred_element_type=F32)


def _dot_nt(a, b):
    return lax.dot_general(a.astype(BF16), b, (((1,), (1,)), ((), ())), preferred_element_type=F32)


def _mm_tn(name, a, b, col_chunks=1, after=None):
    a_slabs, b_slabs = a.ndim == 3, b.ndim == 3
    T = a.shape[-2]
    tt = min(1024, T)
    ka, nb = a.shape[-1], b.shape[-1]
    if a_slabs or b_slabs:
        out_dims = (N_CHIPS, ka, nb)
    elif col_chunks > 1:
        out_dims = (col_chunks, ka, nb // col_chunks)
    else:
        out_dims = (ka, nb)

    def body(a_ref, b_ref, *rest):
        o_ref = rest[-1]

        @pl.when(pl.program_id(0) == 0)
        def _():
            o_ref[...] = jnp.zeros_like(o_ref)

        if a_slabs:
            bb = b_ref[...].astype(BF16)
            for j in range(N_CHIPS):
                o_ref[j] += _dot_tn(a_ref[j], bb)
        elif b_slabs:
            aa = a_ref[...].astype(BF16)
            for j in range(N_CHIPS):
                o_ref[j] += _dot_tn(aa, b_ref[j])
        elif col_chunks > 1:
            aa = a_ref[...].astype(BF16)
            w = nb // col_chunks
            for j in range(col_chunks):
                o_ref[j] += _dot_tn(aa, b_ref[:, j * w:(j + 1) * w])
        else:
            o_ref[...] += _dot_tn(a_ref[...], b_ref[...])

    def spec(arr, slabs):
        if slabs:
            return pl.BlockSpec((N_CHIPS, tt, arr.shape[-1]), lambda t: (0, t, 0))
        return _rows(tt, arr.shape[-1])

    return pl.pallas_call(
        body, name=name, grid=(T // tt,),
        in_specs=[spec(a, a_slabs), spec(b, b_slabs)] + ([] if after is None else [pl.BlockSpec(memory_space=pl.ANY)]),
        out_specs=_whole(out_dims),
        out_shape=SDS(out_dims, F32),
        compiler_params=_params("arbitrary"),
    )(a, b, *([] if after is None else [after]))


def _mm_nt_normbwd(name, pairs, h, g, dh, after):
    T, D = h.shape
    tm = min(512, T)
    n_pairs = len(pairs)
    kinds = ["slabs" if dy.ndim == 3 else ("quarters" if w.ndim == 3 else "plain") for dy, w, _ in pairs]

    def body(*refs):
        dy_refs = refs[:n_pairs]
        w_refs = refs[n_pairs:2 * n_pairs]
        h_ref, g_ref, dh_ref, _, o_ref, dg_ref, cs_ref = refs[2 * n_pairs:]

        @pl.when(pl.program_id(0) == 0)
        def _():
            dg_ref[...] = jnp.zeros_like(dg_ref)
            cs_ref[...] = jnp.zeros_like(cs_ref)

        df = jnp.zeros((tm, D), F32)
        for dy_ref, w_ref, kd in zip(dy_refs, w_refs, kinds):
            if kd == "slabs":
                for j in range(N_CHIPS):
                    df = df + _dot_nt(dy_ref[j], w_ref[j])
            elif kd == "quarters":
                n = w_ref.shape[2]
                for j in range(N_CHIPS):
                    df = df + _dot_nt(dy_ref[:, j * n:(j + 1) * n], w_ref[j])
            else:
                df = df + _dot_nt(dy_ref[...], w_ref[...])
        hh = h_ref[...]
        r = _rms_rstd(hh)
        dg_ref[...] += jnp.sum(df * (hh * r), axis=0, keepdims=True)
        dxh = df * g_ref[...]
        out = dh_ref[...] + (r * dxh - hh * (r * r * r) * jnp.mean(dxh * hh, axis=-1, keepdims=True))
        o_ref[...] = out
        cs_ref[...] += jnp.sum(out, axis=0, keepdims=True)

    dy_specs, w_specs = [], []
    for (dy, w, layer), kd in zip(pairs, kinds):
        if kd == "slabs":
            dy_specs.append(pl.BlockSpec((N_CHIPS, tm, dy.shape[2]), lambda i: (0, i, 0)))
            w_specs.append(pl.BlockSpec((N_CHIPS, None, D, w.shape[3]),
                                        functools.partial(lambda i, layer: (0, layer, 0, 0), layer=layer),
                                        pipeline_mode=pl.Buffered(1)))
        else:
            dy_specs.append(_rows(tm, dy.shape[1]))
            w_specs.append(_whole(w.shape))

    return pl.pallas_call(
        body, name=name, grid=(T // tm,),
        in_specs=[*dy_specs, *w_specs, _rows(tm, D), _whole((1, D)), _rows(tm, D), pl.BlockSpec(memory_space=pl.ANY)],
        out_specs=[_rows(tm, D), _whole((1, D)), _whole((1, D))],
        out_shape=[SDS((T, D), F32), SDS((1, D), F32), SDS((1, D), F32)],
        compiler_params=_params("arbitrary"),
    )(*[dy for dy, _, _ in pairs], *[w for _, w, _ in pairs], h, g, dh, after)


def _mm_nt(name, dy, w, out_dtype):
    T, N = dy.shape
    K = w.shape[0]
    tm = min(512, T)

    def body(dy_ref, w_ref, o_ref):
        o_ref[...] = lax.dot_general(dy_ref[...].astype(BF16), w_ref[...], (((1,), (1,)), ((), ())),
                                     preferred_element_type=F32).astype(out_dtype)

    return pl.pallas_call(
        body, name=name, grid=(T // tm,),
        in_specs=[_rows(tm, N), _whole((K, N))],
        out_specs=_rows(tm, K),
        out_shape=SDS((T, K), out_dtype),
        compiler_params=_params("parallel"),
    )(dy, w)


def _conv_bwd(dact, c, a, w_dw, ln_g, ln_b):
    T, D = c.shape
    tc = _conv_tile(T)
    per = tc // CONV_HALO
    n_tiles = T // tc
    last_halo = T // CONV_HALO - 1

    def ln_bwd(dact_v, c_v, lg, lb):
        xc = c_v - jnp.mean(c_v, axis=-1, keepdims=True)
        rstd = lax.rsqrt(jnp.mean(xc * xc, axis=-1, keepdims=True) + LN_EPS)
        z = xc * rstd
        _, dsilu = _silu_and_grad(z * lg + lb)
        dl = dact_v * dsilu
        dz = dl * lg
        dc = rstd * (dz - jnp.mean(dz, axis=-1, keepdims=True) - z * jnp.mean(dz * z, axis=-1, keepdims=True))
        return dc, dl, z

    def body(dact_ref, dactn_ref, c_ref, cn_ref, a_ref, ah_ref, w_ref, lg_ref, lb_ref,
             da_ref, dlg_ref, dlb_ref, dbdw_ref, dwdw_ref, dbpw1_ref, dc_sh, u_sh, du_scr):
        i = pl.program_id(0)

        @pl.when(i == 0)
        def _():
            for ref in (dlg_ref, dlb_ref, dbdw_ref, dwdw_ref, dbpw1_ref):
                ref[...] = jnp.zeros_like(ref)

        lg, lb = lg_ref[...], lb_ref[...]
        dc, dl, z = ln_bwd(dact_ref[...], c_ref[...], lg, lb)
        dlg_ref[...] += jnp.sum(dl * z, axis=0, keepdims=True)
        dlb_ref[...] += jnp.sum(dl, axis=0, keepdims=True)
        dbdw_ref[...] += jnp.sum(dc, axis=0, keepdims=True)
        dcn, _, _ = ln_bwd(dactn_ref[...], cn_ref[...], lg, lb)
        dc_sh[0, 0:tc, :] = dc
        dc_sh[0, tc:, :] = jnp.where(i < n_tiles - 1, dcn, 0.0)
        _fill_shifted(dc_sh, tc)

        a_v = a_ref[...]
        a1 = a_v[:, :D]
        sg = jax.nn.sigmoid(a_v[:, D:])
        u_sh[0, 0:CONV_HALO, :] = jnp.where(i > 0, _glu(ah_ref[...], D), 0.0)
        u_sh[0, CONV_HALO:, :] = a1 * sg
        _fill_shifted(u_sh, tc)

        _depthwise_taps(dc_sh, w_ref, [CONV_WIDTH - 1 - j for j in range(CONV_WIDTH)], None, du_scr, tc)
        _depthwise_tap_grads(dc_sh, u_sh, [CONV_FIRST_TAP + j for j in range(CONV_WIDTH)], dwdw_ref, tc)

        du = du_scr[...]
        da1 = du * sg
        da2 = du * a1 * sg * (1.0 - sg)
        da_ref[:, :D] = da1.astype(BF16)
        da_ref[:, D:] = da2.astype(BF16)
        dbpw1_ref[:, :D] += jnp.sum(da1, axis=0, keepdims=True)
        dbpw1_ref[:, D:] += jnp.sum(da2, axis=0, keepdims=True)

    nxt = lambda i: (jnp.minimum((i + 1) * per, last_halo), 0)
    return pl.pallas_call(
        body, name="conv_bwd", grid=(n_tiles,),
        in_specs=[_rows(tc, D), pl.BlockSpec((CONV_HALO, D), nxt),
                  _rows(tc, D), pl.BlockSpec((CONV_HALO, D), nxt),
                  _rows(tc, 2 * D),
                  pl.BlockSpec((CONV_HALO, 2 * D), lambda i: (jnp.maximum(i * per - 1, 0), 0)),
                  _whole((CONV_WIDTH, D)), _whole((1, D)), _whole((1, D))],
        out_specs=[_rows(tc, 2 * D), _whole((1, D)), _whole((1, D)), _whole((1, D)),
                   _whole((CONV_HALO, D)), _whole((1, 2 * D))],
        out_shape=[SDS((T, 2 * D), BF16), SDS((1, D), F32), SDS((1, D), F32), SDS((1, D), F32),
                   SDS((CONV_HALO, D), F32), SDS((1, 2 * D), F32)],
        scratch_shapes=[pltpu.VMEM((V7X_SUBLANES, tc + CONV_HALO, D), F32),
                        pltpu.VMEM((V7X_SUBLANES, tc + CONV_HALO, D), F32), pltpu.VMEM((tc, D), F32)],
        compiler_params=_params("arbitrary"),
    )(dact, dact, c, c, a, a, w_dw, ln_g, ln_b)


def _attn_bwd(qkv, dao, cos, sin, sinks):
    T = qkv.shape[0]
    nb = T // ATTN_BLOCK
    qw = N_Q_HEADS * HEAD_DIM
    kw = N_KV_HEADS * HEAD_DIM

    def body(q_ref, kc_ref, kp_ref, vc_ref, vp_ref, do_ref, cos_ref, sin_ref, cosp_ref, sinp_ref, sink_ref,
             dq_ref, dkv_ref, dsink_ref, dbq_ref, dbkv_ref, carry, prev_scr, cur_scr, dq_scr):
        n = pl.program_id(0)

        @pl.when(n == 0)
        def _():
            for ref in (dsink_ref, dbq_ref, dbkv_ref, carry):
                ref[...] = jnp.zeros_like(ref)

        @pl.when(n == nb)
        def _():
            prev_scr[...] = jnp.zeros_like(prev_scr)

        @pl.when(n < nb)
        def _():
            prev_part = _from_previous_block(N_Q_HEADS * ATTN_BLOCK)
            half = PAIRS_PER_KV * ATTN_BLOCK
            upper = _upper_lanes((ATTN_BLOCK, 2 * HEAD_DIM))
            groups = range(N_KV_HEADS)

            def nt(a, b):
                return lax.dot_general(a, b, (((1,), (1,)), ((), ())), preferred_element_type=F32)

            def kv_grad(even_rows, odd_rows, x):
                even = lax.dot_general(even_rows, x, (((0,), (0,)), ((), ())), preferred_element_type=F32)
                odd = lax.dot_general(odd_rows, x, (((0,), (0,)), ((), ())), preferred_element_type=F32)
                t = jnp.where(upper, odd, even)
                return t + _swap_lane_halves(t)

            q = [_pair_rows(q_ref, g) for g in groups]
            do = [_pair_rows(do_ref, g) for g in groups]
            k_prev = [_kv_operands(g, kp_ref[...]) for g in groups]
            k_cur = [_kv_operands(g, kc_ref[...]) for g in groups]
            v_prev = [_kv_operands(g, vp_ref[...]) for g in groups]
            v_cur = [_kv_operands(g, vc_ref[...]) for g in groups]
            probs, p_sink = _folded_probs(n, q, k_prev, k_cur, sink_ref, prev_part)
            dp_prev = jnp.concatenate([nt(do[g], v_prev[g][i]) for g in groups for i in range(2)], axis=0)
            dp_cur = jnp.concatenate([nt(do[g], v_cur[g][i]) for g in groups for i in range(2)], axis=0)
            dp = jnp.where(prev_part, dp_prev, dp_cur)
            delta = jnp.sum(probs * dp, axis=1, keepdims=True)
            ds_prev, ds_cur = _split_folded(probs * (dp - delta) * (HEAD_DIM ** -0.5), prev_part)
            p_prev, p_cur = _split_folded(probs, prev_part)
            dsink_rows = -(p_sink * delta)
            heads = [h for g in groups for h in _group_heads(g)]
            for i, h in enumerate(heads):
                dsink_ref[:, h:h + 1] += jnp.sum(dsink_rows[i * ATTN_BLOCK:(i + 1) * ATTN_BLOCK], axis=0,
                                                 keepdims=True)
            kv_grads = []
            for g in groups:
                even, odd = slice(2 * g * half, (2 * g + 1) * half), slice((2 * g + 1) * half, (2 * g + 2) * half)
                dq = (jnp.dot(ds_prev[even], k_prev[g][0], preferred_element_type=F32)
                      + jnp.dot(ds_cur[even], k_cur[g][0], preferred_element_type=F32)
                      + jnp.dot(ds_prev[odd], k_prev[g][1], preferred_element_type=F32)
                      + jnp.dot(ds_cur[odd], k_cur[g][1], preferred_element_type=F32))
                for i in range(PAIRS_PER_KV):
                    hp = g * PAIRS_PER_KV + i
                    dq_scr[:, hp * 2 * HEAD_DIM:(hp + 1) * 2 * HEAD_DIM] = dq[i * ATTN_BLOCK:(i + 1) * ATTN_BLOCK]
                kv_grads.append((kv_grad(ds_prev[even], ds_prev[odd], q[g]), kv_grad(ds_cur[even], ds_cur[odd], q[g]),
                                 kv_grad(p_prev[even], p_prev[odd], do[g]), kv_grad(p_cur[even], p_cur[odd], do[g])))
            (dkp0, dkc0, dvp0, dvc0), (dkp1, dkc1, dvp1, dvc1) = kv_grads
            prev_scr[:, :kw] = jnp.where(upper, dkp1, dkp0)
            prev_scr[:, kw:] = jnp.where(upper, dvp1, dvp0)
            cur_scr[:, :kw] = jnp.where(upper, dkc1, dkc0)
            cur_scr[:, kw:] = jnp.where(upper, dvc1, dvc0)
            dq_pre = _rope_transposed(dq_scr[...], cos_ref, sin_ref)
            dq_ref[...] = dq_pre.astype(BF16)
            dbq_ref[...] += jnp.sum(dq_pre, axis=0, keepdims=True)

        tot = carry[...] + prev_scr[...]
        dk_pre = _rope_transposed(tot[:, :kw], cosp_ref, sinp_ref)
        dkv_ref[:, :kw] = dk_pre.astype(BF16)
        dkv_ref[:, kw:] = tot[:, kw:].astype(BF16)
        dbkv_ref[:, :kw] += jnp.sum(dk_pre, axis=0, keepdims=True)
        dbkv_ref[:, kw:] += jnp.sum(tot[:, kw:], axis=0, keepdims=True)

        @pl.when(n < nb)
        def _():
            carry[...] = cur_scr[...]

    cur = lambda n: (jnp.minimum(n, nb - 1), 0)
    out_lag = lambda n: (jnp.maximum(n - 1, 0), 0)
    return pl.pallas_call(
        body, name="attn_bwd", grid=(nb + 1,),
        in_specs=[*_attn_specs(T),
                  pl.BlockSpec((ATTN_BLOCK, qw), cur),
                  pl.BlockSpec((ATTN_BLOCK, V7X_LANES), cur), pl.BlockSpec((ATTN_BLOCK, V7X_LANES), cur),
                  pl.BlockSpec((ATTN_BLOCK, V7X_LANES), out_lag), pl.BlockSpec((ATTN_BLOCK, V7X_LANES), out_lag),
                  _whole((1, N_Q_HEADS))],
        out_specs=[pl.BlockSpec((ATTN_BLOCK, qw), cur), pl.BlockSpec((ATTN_BLOCK, 2 * kw), out_lag),
                   _whole((1, N_Q_HEADS)), _whole((1, qw)), _whole((1, 2 * kw))],
        out_shape=[SDS((T, qw), BF16), SDS((T, 2 * kw), BF16),
                   SDS((1, N_Q_HEADS), F32), SDS((1, qw), F32), SDS((1, 2 * kw), F32)],
        scratch_shapes=[pltpu.VMEM((ATTN_BLOCK, 2 * kw), F32), pltpu.VMEM((ATTN_BLOCK, 2 * kw), F32),
                        pltpu.VMEM((ATTN_BLOCK, 2 * kw), F32), pltpu.VMEM((ATTN_BLOCK, qw), F32)],
        compiler_params=_params("arbitrary"),
    )(qkv, qkv, qkv, qkv, qkv, dao, cos, sin, cos, sin, sinks)


def _local_step(x, target, p, reduce_begin, reduce_send):
    T, D = x.shape
    cos, sin = _rope_tables(T)
    qw = N_Q_HEADS * HEAD_DIM
    nm, nf = p["norm_mix"], p["norm_ffn"]

    y0, qkv = _qkv_proj(x, nm[0:1], p["attn_w_qkv"], p["attn_b_qkv"], cos, sin, p["gather_started"])
    ao = _attn_fwd(qkv, p["attn_sinks"])
    h1, f0 = _mm_res("attn_out", ao, p["attn_w_o"], p["attn_b_o"], x, nf[0:1])
    p = {**p, **p["other_weights"](h1)}
    w1, w3, w2 = p["ffn_w1"], p["ffn_w3"], p["ffn_w2"]
    act0, gg0, s0 = _ffn_up("ffn0_up", f0, w1, w3, 0)
    h2 = _ffn_down("ffn0_down", s0, w2, 0, h1)
    y1, a = _pw1_proj(h2, nm[1:2], p["conv_w_pw1"], p["conv_b_pw1"])
    c, act = _conv_fwd(a, p["conv_w_dw"], p["conv_b_dw"], p["conv_ln_g"], p["conv_ln_b"])
    h3, f1 = _mm_res("conv_out", act, p["conv_w_pw2"], p["conv_b_pw2"], h2, nf[1:2])
    act1, gg1, s1 = _ffn_up("ffn1_up", f1, w1, w3, 1)
    h4 = _ffn_down("ffn1_down", s1, w2, 1, h3)
    dh4, loss, d_norm_final = _final_loss(h4, p["norm_final"], target)

    g = {}
    dg1, dg3 = _ffn_bwd_act("ffn1_bwd_act", dh4, w2, 1, act1, gg1)
    dw2_1 = _mm_tn("ffn1_dw2", s1, dh4)
    dw1_1 = _mm_tn("ffn1_dw1", f1, dg1)
    dw3_1 = _mm_tn("ffn1_dw3", f1, dg3)
    begun = reduce_begin("ffn1", {("ffn_w1", 1): dw1_1, ("ffn_w3", 1): dw3_1, ("ffn_w2", 1): dw2_1})
    dh3, dnf1, db_pw2 = _mm_nt_normbwd("ffn1_bwd_in", [(dg1, w1, 1), (dg3, w3, 1)], h3, nf[1:2], dh4, begun)
    sent = reduce_send("ffn1", dh3)

    dw_pw2 = _mm_tn("conv_dw_pw2", act, dh3, after=sent)
    dact = _mm_nt("conv_bwd_out", dh3, p["conv_w_pw2"], F32)
    da, d_ln_g, d_ln_b, d_b_dw, d_w_dw, d_b_pw1 = _conv_bwd(dact, c, a, p["conv_w_dw"], p["conv_ln_g"], p["conv_ln_b"])
    dw_pw1 = _mm_tn("conv_dw_pw1", y1, da, col_chunks=N_CHIPS)
    begun = reduce_begin("conv", {("conv_w_pw2", 0): dw_pw2.reshape(N_CHIPS, -1, D), ("conv_w_pw1", 0): dw_pw1})
    dh2, dnm1, _ = _mm_nt_normbwd("conv_bwd_in", [(da, p["conv_w_pw1"], None)], h2, nm[1:2], dh3, begun)
    sent = reduce_send("conv", dh2)

    dg1, dg3 = _ffn_bwd_act("ffn0_bwd_act", dh2, w2, 0, act0, gg0)
    dw2_0 = _mm_tn("ffn0_dw2", s0, dh2, after=sent)
    dw1_0 = _mm_tn("ffn0_dw1", f0, dg1)
    dw3_0 = _mm_tn("ffn0_dw3", f0, dg3)
    begun = reduce_begin("ffn0", {("ffn_w1", 0): dw1_0, ("ffn_w3", 0): dw3_0, ("ffn_w2", 0): dw2_0})
    dh1, dnf0, db_o = _mm_nt_normbwd("ffn0_bwd_in", [(dg1, w1, 0), (dg3, w3, 0)], h1, nf[0:1], dh2, begun)
    sent = reduce_send("ffn0", dh1)

    dw_o = _mm_tn("attn_dw_o", ao, dh1, after=sent)
    dao = _mm_nt("attn_bwd_out", dh1, p["attn_w_o"], BF16)
    dq, dkv, d_sinks, dbq, dbkv = _attn_bwd(qkv, dao, cos, sin, p["attn_sinks"])
    dwq = _mm_tn("attn_dw_q", y0, dq)
    dwkv = _mm_tn("attn_dw_kv", y0, dkv)
    wqkv = p["attn_w_qkv"]
    n_qkv = wqkv.shape[1] // N_CHIPS
    dwqkv = jnp.moveaxis(jnp.concatenate([dwq, dwkv], axis=1).reshape(D, N_CHIPS, n_qkv), 1, 0)
    begun = reduce_begin("attn", {("attn_w_o", 0): dw_o.reshape(N_CHIPS, -1, D), ("attn_w_qkv", 0): dwqkv})
    dx, dnm0, _ = _mm_nt_normbwd("attn_bwd_in", [(dq, wqkv[:, :qw], None), (dkv, wqkv[:, qw:], None)], x, nm[0:1], dh1,
                                 begun)
    reduce_send("attn", dx)

    g["norm_mix"] = jnp.concatenate([dnm0, dnm1], axis=0)
    g["norm_ffn"] = jnp.concatenate([dnf0, dnf1], axis=0)
    g["attn_b_qkv"] = jnp.concatenate([dbq, dbkv], axis=1)
    g["attn_sinks"] = d_sinks
    g["attn_b_o"] = db_o
    g["conv_b_pw1"] = d_b_pw1
    g["conv_w_dw"] = d_w_dw[:CONV_WIDTH]
    g["conv_b_dw"] = d_b_dw
    g["conv_ln_g"] = d_ln_g
    g["conv_ln_b"] = d_ln_b
    g["conv_b_pw2"] = db_pw2
    g["norm_final"] = d_norm_final
    return loss, dx, g


ANY = pl.BlockSpec(memory_space=pl.ANY)
VMEM_WHOLE = pl.BlockSpec(memory_space=pltpu.VMEM)


def _my_place():
    return lax.axis_index("x"), lax.axis_index("y"), lax.axis_index("c")


def _other_chips(x, y):
    places = [(1 - x, y), (x, 1 - y), (1 - x, 1 - y)]
    return [(bx, by, 2 * bx + by) for bx, by in places]


def _exchange_small(name, v, reduce):
    r, w = v.shape

    def body(v_ref, o_ref, *rest):
        if reduce:
            buf, send_sems, recv_sems = rest
        else:
            buf = o_ref
            send_sems, recv_sems = rest
        x, y, c = _my_place()
        me = 4 * x + 2 * y + c
        sends = []
        for k in range(1, N_DEV):
            peer = (1 - x if k & 4 else x, 1 - y if k & 2 else y, 1 - c if k & 1 else c)
            cp = pltpu.make_async_remote_copy(
                src_ref=v_ref, dst_ref=buf.at[me], send_sem=send_sems.at[k - 1], recv_sem=recv_sems.at[k - 1],
                device_id=peer, device_id_type=MESH)
            cp.start()
            sends.append(cp)
        buf[me] = v_ref[...]
        for k in range(1, N_DEV):
            src = 4 * (1 - x if k & 4 else x) + 2 * (1 - y if k & 2 else y) + (1 - c if k & 1 else c)
            pltpu.make_async_remote_copy(
                src_ref=v_ref, dst_ref=buf.at[src], send_sem=send_sems.at[k - 1], recv_sem=recv_sems.at[k - 1],
                device_id=(x, y, c), device_id_type=MESH).wait_recv()
        for cp in sends:
            cp.wait_send()
        if reduce:
            acc = buf[0]
            for d in range(1, N_DEV):
                acc = acc + buf[d]
            o_ref[...] = acc

    sems = [pltpu.SemaphoreType.DMA((N_DEV - 1,)), pltpu.SemaphoreType.DMA((N_DEV - 1,))]
    if reduce:
        out_shape = SDS((r, w), F32)
        scratch = [pltpu.VMEM((N_DEV, r, w), F32)] + sems
    else:
        out_shape = SDS((N_DEV, r, w), F32)
        scratch = sems
    return pl.pallas_call(
        body, name=name, out_shape=out_shape, in_specs=[VMEM_WHOLE], out_specs=VMEM_WHOLE,
        scratch_shapes=scratch,
        compiler_params=pltpu.CompilerParams(vmem_limit_bytes=V7X_VMEM_LIMIT_BYTES),
    )(v)


def _cast_into_slot(name, gathered, shard, chip_idx):
    rows, cols = shard.shape
    tr = _pack_row_tile(rows)

    def body(k_ref, s_ref, g_ref, o_ref):
        o_ref[...] = s_ref[...].astype(BF16)

    return pl.pallas_call(
        body, name=name,
        grid_spec=pltpu.PrefetchScalarGridSpec(
            num_scalar_prefetch=1, grid=(rows // tr,),
            in_specs=[pl.BlockSpec((tr, cols), lambda i, k_ref: (i, 0)), pl.BlockSpec(memory_space=pl.ANY)],
            out_specs=pl.BlockSpec((None, tr, cols), lambda i, k_ref: (k_ref[0], i, 0))),
        out_shape=SDS(gathered.shape, BF16),
        input_output_aliases={2: 0},
        compiler_params=_params("parallel"),
    )(chip_idx, shard, gathered)


def _row_halves(ref, c):
    half = ref.shape[1] // 2
    return pl.ds(pl.multiple_of(c * half, 16), half), pl.ds(pl.multiple_of((1 - c) * half, 16), half)


def _gather_ici_copies(refs, send_sems, recv_sems):
    x, y, c = _my_place()
    k = 2 * x + y
    pairs = []
    for i, ref in enumerate(refs):
        mine, _ = _row_halves(ref, c)
        for j, (bx, by, kb) in enumerate(_other_chips(x, y)):
            sems = dict(send_sem=send_sems.at[3 * i + j], recv_sem=recv_sems.at[3 * i + j], device_id_type=MESH)
            send = pltpu.make_async_remote_copy(src_ref=ref.at[k, mine], dst_ref=ref.at[k, mine],
                                                device_id=(bx, by, c), **sems)
            arrival = pltpu.make_async_remote_copy(src_ref=ref.at[kb, mine], dst_ref=ref.at[kb, mine],
                                                   device_id=(bx, by, c), **sems)
            pairs.append((send, arrival))
    return pairs


def _gather_d2d_copies(refs, send_sems, recv_sems, first_sem):
    x, y, c = _my_place()
    pairs = []
    for i, ref in enumerate(refs):
        mine, theirs = _row_halves(ref, c)
        for j, (_, _, kb) in enumerate(_other_chips(x, y)):
            sem = first_sem + 3 * i + j
            sems = dict(send_sem=send_sems.at[sem], recv_sem=recv_sems.at[sem], device_id=(x, y, 1 - c),
                        device_id_type=MESH)
            send = pltpu.make_async_remote_copy(src_ref=ref.at[kb, mine], dst_ref=ref.at[kb, mine], **sems)
            arrival = pltpu.make_async_remote_copy(src_ref=ref.at[kb, theirs], dst_ref=ref.at[kb, theirs], **sems)
            pairs.append((send, arrival))
    return pairs


def _run_copies(pairs):
    for send, _ in pairs:
        send.start()
    for send, arrival in pairs:
        send.wait_send()
        arrival.wait_recv()


def _gather_now(name, gathered):
    n_w = len(gathered)

    def body(*refs):
        in_refs = refs[:n_w]
        send_sems, recv_sems = refs[2 * n_w:]
        _run_copies(_gather_ici_copies(in_refs, send_sems, recv_sems))
        _run_copies(_gather_d2d_copies(in_refs, send_sems, recv_sems, 3 * n_w))

    return pl.pallas_call(
        body, name=name, out_shape=[SDS(g.shape, g.dtype) for g in gathered],
        in_specs=[ANY] * n_w, out_specs=[ANY] * n_w, input_output_aliases={i: i for i in range(n_w)},
        scratch_shapes=[pltpu.SemaphoreType.DMA((6 * n_w,)), pltpu.SemaphoreType.DMA((6 * n_w,))],
    )(*gathered)


def _gather_start(name, gathered):
    n_w = len(gathered)

    def body(*refs):
        in_refs = refs[:n_w]
        send_sems, recv_sems = refs[n_w:n_w + 2]
        for send, _ in _gather_ici_copies(in_refs, send_sems, recv_sems):
            send.start()
        refs[-1][...] = jnp.zeros_like(refs[-1])

    out = pl.pallas_call(
        body, name=name,
        out_shape=(pltpu.SemaphoreType.DMA((3 * n_w,)), pltpu.SemaphoreType.DMA((3 * n_w,)),
                   *[pltpu.HBM(g.shape, g.dtype) for g in gathered], SDS((8, V7X_LANES), F32)),
        in_specs=[HBM_SPEC] * n_w, out_specs=(SEM_SPEC, SEM_SPEC, *[HBM_SPEC] * n_w, VMEM_WHOLE),
        input_output_aliases={i: 2 + i for i in range(n_w)},
        compiler_params=pltpu.CompilerParams(has_side_effects=DATAFLOW),
    )(*[pltpu.with_memory_space_constraint(g, pltpu.HBM) for g in gathered])
    return out[0], out[1], list(out[2:2 + n_w]), out[-1]


def _gather_wait(name, send_sems, recv_sems, gathered, after):
    n_w = len(gathered)

    def body(*refs):
        in_refs = refs[:n_w]
        send_sems, recv_sems = refs[n_w:n_w + 2]
        for send, arrival in _gather_ici_copies(in_refs, send_sems, recv_sems):
            send.wait_send()
            arrival.wait_recv()

    out = pl.pallas_call(
        body, name=name, out_shape=tuple(pltpu.HBM(g.shape, g.dtype) for g in gathered),
        in_specs=[*[HBM_SPEC] * n_w, SEM_SPEC, SEM_SPEC, ANY], out_specs=tuple([HBM_SPEC] * n_w),
        input_output_aliases={i: i for i in range(n_w)},
        compiler_params=pltpu.CompilerParams(has_side_effects=DATAFLOW),
    )(*gathered, send_sems, recv_sems, after)
    return list(out)


def _swap_fetched_with_sibling(name, gathered):
    n_w = len(gathered)

    def body(*refs):
        in_refs = refs[:n_w]
        send_sems, recv_sems = refs[2 * n_w:]
        _run_copies(_gather_d2d_copies(in_refs, send_sems, recv_sems, 0))

    return pl.pallas_call(
        body, name=name, out_shape=[SDS(g.shape, g.dtype) for g in gathered],
        in_specs=[ANY] * n_w, out_specs=[ANY] * n_w, input_output_aliases={i: i for i in range(n_w)},
        scratch_shapes=[pltpu.SemaphoreType.DMA((3 * n_w,)), pltpu.SemaphoreType.DMA((3 * n_w,))],
    )(*gathered)


def _sibling_swap_copies(g_refs, land_refs, send_sems, recv_sems):
    x, y, c = _my_place()
    copies = []
    for i, g_ref in enumerate(g_refs):
        half = g_ref.shape[1] // 2
        theirs = pl.ds(pl.multiple_of((1 - c) * half, 8), half)
        copies.append(pltpu.make_async_remote_copy(
            src_ref=g_ref.at[:, theirs], dst_ref=land_refs[i], send_sem=send_sems.at[i], recv_sem=recv_sems.at[i],
            device_id=(x, y, 1 - c), device_id_type=MESH))
    return copies


def _sibling_swap_start(name, grads):
    n_g = len(grads)

    def body(*refs):
        g_refs, land_refs = refs[:n_g], refs[n_g:2 * n_g]
        send_sems, recv_sems = refs[2 * n_g:2 * n_g + 2]
        for cp in _sibling_swap_copies(g_refs, land_refs, send_sems, recv_sems):
            cp.start()
        refs[-1][...] = jnp.zeros_like(refs[-1])

    lands = [pltpu.with_memory_space_constraint(lax.empty((g.shape[0], g.shape[1] // 2, g.shape[2]), g.dtype),
                                                pltpu.HBM) for g in grads]
    out = pl.pallas_call(
        body, name=name,
        out_shape=(pltpu.SemaphoreType.DMA((n_g,)), pltpu.SemaphoreType.DMA((n_g,)),
                   *[pltpu.HBM(g.shape, g.dtype) for g in grads], *[pltpu.HBM(l.shape, l.dtype) for l in lands],
                   SDS((8, V7X_LANES), F32)),
        in_specs=[HBM_SPEC] * (2 * n_g), out_specs=(SEM_SPEC, SEM_SPEC, *[HBM_SPEC] * (2 * n_g), VMEM_WHOLE),
        input_output_aliases={i: 2 + i for i in range(2 * n_g)},
        compiler_params=pltpu.CompilerParams(has_side_effects=DATAFLOW),
    )(*[pltpu.with_memory_space_constraint(g, pltpu.HBM) for g in grads], *lands)
    return out[0], out[1], list(out[2:2 + n_g]), list(out[2 + n_g:2 + 2 * n_g]), out[-1]


def _sibling_swap_wait(name, send_sems, recv_sems, grads, lands, after):
    n_g = len(grads)

    def body(*refs):
        g_refs, land_refs = refs[:n_g], refs[n_g:2 * n_g]
        send_sems, recv_sems = refs[2 * n_g:2 * n_g + 2]
        for cp in _sibling_swap_copies(g_refs, land_refs, send_sems, recv_sems):
            cp.wait_send()
            cp.wait_recv()

    out = pl.pallas_call(
        body, name=name,
        out_shape=(*[pltpu.HBM(g.shape, g.dtype) for g in grads], *[pltpu.HBM(l.shape, l.dtype) for l in lands]),
        in_specs=[*[HBM_SPEC] * (2 * n_g), SEM_SPEC, SEM_SPEC, ANY], out_specs=tuple([HBM_SPEC] * (2 * n_g)),
        input_output_aliases={i: i for i in range(2 * n_g)},
        compiler_params=pltpu.CompilerParams(has_side_effects=DATAFLOW),
    )(*grads, *lands, send_sems, recv_sems, after)
    return list(out[:n_g]), list(out[n_g:])


def _pack_row_tile(rows):
    for t in range(min(rows, 512), 7, -1):
        if rows % t == 0 and t % 8 == 0:
            return t
    return rows


def _add_sibling_half(name, grads, from_sibling, c_idx):
    n, R, w = grads.shape
    half = R // 2
    tr = _pack_row_tile(half)
    steps = half // tr

    def body(c_ref, g_ref, s_ref, o_ref):
        o_ref[...] = (g_ref[...] + s_ref[...]).astype(BF16)

    return pl.pallas_call(
        body, name=name,
        grid_spec=pltpu.PrefetchScalarGridSpec(
            num_scalar_prefetch=1, grid=(n, steps),
            in_specs=[pl.BlockSpec((1, tr, w), lambda j, i, c_ref: (j, c_ref[0] * steps + i, 0)),
                      pl.BlockSpec((1, tr, w), lambda j, i, c_ref: (j, i, 0))],
            out_specs=pl.BlockSpec((1, tr, w), lambda j, i, c_ref: (j, i, 0))),
        out_shape=SDS((n, half, w), BF16),
        compiler_params=_params("parallel", "parallel"),
    )(c_idx, grads, from_sibling)


HBM_SPEC = pl.BlockSpec(memory_space=pltpu.HBM)
SEM_SPEC = pl.BlockSpec(memory_space=pltpu.SEMAPHORE)
DATAFLOW = pltpu.SideEffectType.DATAFLOW_SIDE_EFFECTING


def _chip_scatter_copies(p_refs, land_refs, send_sems, recv_sems):
    x, y, c = _my_place()
    return [pltpu.make_async_remote_copy(
        src_ref=p_refs[i].at[kb], dst_ref=land_refs[i].at[j], send_sem=send_sems.at[3 * i + j],
        recv_sem=recv_sems.at[3 * i + j], device_id=(bx, by, c), device_id_type=MESH)
        for i in range(len(p_refs)) for j, (bx, by, kb) in enumerate(_other_chips(x, y))]


def _scatter_start(name, partials):
    n_p = len(partials)

    def body(*refs):
        p_refs, land_refs = refs[:n_p], refs[n_p:2 * n_p]
        send_sems, recv_sems = refs[2 * n_p:2 * n_p + 2]
        for cp in _chip_scatter_copies(p_refs, land_refs, send_sems, recv_sems):
            cp.start()
        refs[-1][...] = jnp.zeros_like(refs[-1])

    lands = [pltpu.with_memory_space_constraint(lax.empty((N_CHIPS - 1,) + p.shape[1:], p.dtype), pltpu.HBM)
             for p in partials]
    out = pl.pallas_call(
        body, name=name,
        out_shape=(pltpu.SemaphoreType.DMA((3 * n_p,)), pltpu.SemaphoreType.DMA((3 * n_p,)),
                   *[pltpu.HBM(p.shape, p.dtype) for p in partials], *[pltpu.HBM(l.shape, l.dtype) for l in lands],
                   SDS((8, V7X_LANES), F32)),
        in_specs=[HBM_SPEC] * (2 * n_p), out_specs=(SEM_SPEC, SEM_SPEC, *[HBM_SPEC] * (2 * n_p), VMEM_WHOLE),
        input_output_aliases={i: 2 + i for i in range(2 * n_p)},
        compiler_params=pltpu.CompilerParams(has_side_effects=DATAFLOW),
    )(*[pltpu.with_memory_space_constraint(p, pltpu.HBM) for p in partials], *lands)
    return out[0], out[1], list(out[2:2 + n_p]), list(out[2 + n_p:2 + 2 * n_p]), out[-1]


def _scatter_wait(name, send_sems, recv_sems, partials, lands, after):
    n_p = len(partials)

    def body(*refs):
        p_refs, land_refs = refs[:n_p], refs[n_p:2 * n_p]
        send_sems, recv_sems = refs[2 * n_p:2 * n_p + 2]
        for cp in _chip_scatter_copies(p_refs, land_refs, send_sems, recv_sems):
            cp.wait_send()
            cp.wait_recv()

    out = pl.pallas_call(
        body, name=name,
        out_shape=(*[pltpu.HBM(p.shape, p.dtype) for p in partials], *[pltpu.HBM(l.shape, l.dtype) for l in lands]),
        in_specs=[*[HBM_SPEC] * (2 * n_p), SEM_SPEC, SEM_SPEC, ANY], out_specs=tuple([HBM_SPEC] * (2 * n_p)),
        input_output_aliases={i: i for i in range(2 * n_p)},
        compiler_params=pltpu.CompilerParams(has_side_effects=DATAFLOW),
    )(*partials, *lands, send_sems, recv_sems, after)
    return list(out[:n_p]), list(out[n_p:])


def _sum_chip_partials(name, grads, from_sibling, received, shard, layer, place):
    n, half, w = from_sibling.shape
    tr = _pack_row_tile(half)
    steps = half // tr

    def body(place_ref, g_ref, s_ref, r_ref, shard_ref, o_ref):
        own = g_ref[0] + s_ref[0]
        o_ref[...] = ((own + r_ref[0].astype(F32)) + r_ref[1].astype(F32)) + r_ref[2].astype(F32)

    return pl.pallas_call(
        body, name=name,
        grid_spec=pltpu.PrefetchScalarGridSpec(
            num_scalar_prefetch=1, grid=(steps,),
            in_specs=[pl.BlockSpec((1, tr, w), lambda i, place_ref: (place_ref[0], place_ref[1] * steps + i, 0)),
                      pl.BlockSpec((1, tr, w), lambda i, place_ref: (place_ref[0], i, 0)),
                      pl.BlockSpec((n - 1, tr, w), lambda i, place_ref: (0, i, 0)),
                      pl.BlockSpec(memory_space=pl.ANY)],
            out_specs=pl.BlockSpec((tr, w), lambda i, place_ref: ((2 * layer + place_ref[1]) * steps + i, 0))),
        out_shape=SDS(shard.shape, F32),
        input_output_aliases={4: 0},
        compiler_params=_params("parallel"),
    )(place, grads, from_sibling, received, shard)


def _join_halves(shards, layers):
    n_s = len(shards)
    n_sem = sum(layers)

    def body(*refs):
        in_refs = refs[:n_s]
        send_sems, recv_sems = refs[2 * n_s:]
        x, y, c = _my_place()
        copies, sem = [], 0
        for ref, n_layers in zip(in_refs, layers):
            half = ref.shape[0] // (2 * n_layers)
            for layer in range(n_layers):
                mine = pl.ds(pl.multiple_of(layer * 2 * half + c * half, 8), half)
                theirs = pl.ds(pl.multiple_of(layer * 2 * half + (1 - c) * half, 8), half)
                send = pltpu.make_async_remote_copy(
                    src_ref=ref.at[mine], dst_ref=ref.at[mine], send_sem=send_sems.at[sem], recv_sem=recv_sems.at[sem],
                    device_id=(x, y, 1 - c), device_id_type=MESH)
                send.start()
                arrival = pltpu.make_async_remote_copy(
                    src_ref=ref.at[theirs], dst_ref=ref.at[theirs], send_sem=send_sems.at[sem],
                    recv_sem=recv_sems.at[sem], device_id=(x, y, 1 - c), device_id_type=MESH)
                copies.append((send, arrival))
                sem += 1
        for send, arrival in copies:
            send.wait_send()
            arrival.wait_recv()

    return pl.pallas_call(
        body, name="join_halves", out_shape=[SDS(s.shape, s.dtype) for s in shards],
        in_specs=[ANY] * n_s, out_specs=[ANY] * n_s,
        input_output_aliases={i: i for i in range(n_s)},
        scratch_shapes=[pltpu.SemaphoreType.DMA((n_sem,)), pltpu.SemaphoreType.DMA((n_sem,))],
    )(*shards)


def _adamw(name, w, g, m, v):
    rows, width = w.shape
    tr = _pack_row_tile(rows)

    def body(w_ref, g_ref, m_ref, v_ref, g_out_ref, d_ref, nm_ref, nv_ref):
        gg = g_ref[...]
        g_out_ref[...] = gg
        m_new = ADAM_B1 * m_ref[...] + (1.0 - ADAM_B1) * gg
        v_new = ADAM_B2 * v_ref[...] + (1.0 - ADAM_B2) * (gg * gg)
        m_hat = m_new / (1.0 - ADAM_B1 ** ADAM_STEP)
        v_hat = v_new / (1.0 - ADAM_B2 ** ADAM_STEP)
        d_ref[...] = -ADAM_LR * (m_hat / (jnp.sqrt(v_hat) + ADAM_EPS) + ADAM_WD * w_ref[...])
        nm_ref[...] = m_new
        nv_ref[...] = v_new

    spec = _rows(tr, width)
    return pl.pallas_call(
        body, name=name, grid=(rows // tr,),
        in_specs=[spec] * 4, out_specs=[spec] * 4,
        out_shape=[SDS((rows, width), F32)] * 4,
        compiler_params=_params("parallel"),
    )(w, g, m, v)


WEIGHT_NAMES = ['norm_mix', 'norm_ffn', 'attn_w_qkv', 'attn_b_qkv', 'attn_sinks', 'attn_w_o', 'attn_b_o',
                'conv_w_pw1', 'conv_b_pw1', 'conv_w_dw', 'conv_b_dw', 'conv_ln_g', 'conv_ln_b', 'conv_w_pw2',
                'conv_b_pw2', 'ffn_w1', 'ffn_w3', 'ffn_w2', 'norm_final']
BIG = ['attn_w_qkv', 'attn_w_o', 'conv_w_pw1', 'conv_w_pw2', 'ffn_w1', 'ffn_w3', 'ffn_w2']
COLUMN_SPLIT = ('attn_w_qkv', 'conv_w_pw1', 'ffn_w1', 'ffn_w3')
SMALL_SPLIT = ['conv_b_pw1', 'conv_w_dw', 'conv_b_dw', 'conv_ln_g', 'conv_ln_b', 'conv_b_pw2']
SMALL_WHOLE = ['norm_mix', 'norm_ffn', 'attn_b_qkv', 'attn_sinks', 'attn_b_o', 'norm_final']


def _pack_rows(arrays, dtype, row_multiple):
    pieces = []
    for a in arrays:
        a = a.astype(dtype)
        pieces += [a[i] for i in range(a.shape[0])] if a.ndim == 2 and a.shape[0] > 1 else [a.reshape(-1)]
    flat = jnp.concatenate(pieces)
    rows = -(-flat.shape[0] // PACK_W)
    rows = -(-rows // row_multiple) * row_multiple
    return jnp.pad(flat, (0, rows * PACK_W - flat.shape[0])).reshape(rows, PACK_W)


def _unpack_rows(pack, shapes):
    flat = pack.reshape(-1)
    out, at = [], 0
    for shape in shapes:
        size = 1
        for s in shape:
            size *= s
        out.append(flat[at:at + size].reshape(shape))
        at += size
    return out


def _join_chip_axis(name, parts):
    axis = parts.ndim - 1 if name in COLUMN_SPLIT or name in SMALL_SPLIT else parts.ndim - 2
    moved = jnp.moveaxis(parts, 0, axis - 1)
    shape = list(moved.shape)
    shape[axis - 1:axis + 1] = [shape[axis - 1] * shape[axis]]
    return moved.reshape(shape)


def _split_chip_axis(name, whole, shard_shape):
    axis = len(shard_shape) - 1 if name in COLUMN_SPLIT or name in SMALL_SPLIT else len(shard_shape) - 2
    shape = list(whole.shape)
    shape[axis:axis + 1] = [N_CHIPS, shard_shape[axis]]
    return jnp.moveaxis(whole.reshape(shape), axis, 0)


def kernel(x, norm_mix, norm_ffn, attn_w_qkv, attn_b_qkv, attn_sinks, attn_w_o, attn_b_o, conv_w_pw1, conv_b_pw1, conv_w_dw, conv_b_dw, conv_ln_g, conv_ln_b, conv_w_pw2, conv_b_pw2, ffn_w1, ffn_w3, ffn_w2, norm_final, loss_target, m_norm_mix, m_norm_ffn, m_attn_w_qkv, m_attn_b_qkv, m_attn_sinks, m_attn_w_o, m_attn_b_o, m_conv_w_pw1, m_conv_b_pw1, m_conv_w_dw, m_conv_b_dw, m_conv_ln_g, m_conv_ln_b, m_conv_w_pw2, m_conv_b_pw2, m_ffn_w1, m_ffn_w3, m_ffn_w2, m_norm_final, v_norm_mix, v_norm_ffn, v_attn_w_qkv, v_attn_b_qkv, v_attn_sinks, v_attn_w_o, v_attn_b_o, v_conv_w_pw1, v_conv_b_pw1, v_conv_w_dw, v_conv_b_dw, v_conv_ln_g, v_conv_ln_b, v_conv_w_pw2, v_conv_b_pw2, v_ffn_w1, v_ffn_w3, v_ffn_w2, v_norm_final):
    w = dict(zip(WEIGHT_NAMES, (norm_mix, norm_ffn, attn_w_qkv, attn_b_qkv, attn_sinks, attn_w_o, attn_b_o,
                                conv_w_pw1, conv_b_pw1, conv_w_dw, conv_b_dw, conv_ln_g, conv_ln_b, conv_w_pw2,
                                conv_b_pw2, ffn_w1, ffn_w3, ffn_w2, norm_final)))
    m = dict(zip(WEIGHT_NAMES, (m_norm_mix, m_norm_ffn, m_attn_w_qkv, m_attn_b_qkv, m_attn_sinks, m_attn_w_o,
                                m_attn_b_o, m_conv_w_pw1, m_conv_b_pw1, m_conv_w_dw, m_conv_b_dw, m_conv_ln_g,
                                m_conv_ln_b, m_conv_w_pw2, m_conv_b_pw2, m_ffn_w1, m_ffn_w3, m_ffn_w2, m_norm_final)))
    v = dict(zip(WEIGHT_NAMES, (v_norm_mix, v_norm_ffn, v_attn_w_qkv, v_attn_b_qkv, v_attn_sinks, v_attn_w_o,
                                v_attn_b_o, v_conv_w_pw1, v_conv_b_pw1, v_conv_w_dw, v_conv_b_dw, v_conv_ln_g,
                                v_conv_ln_b, v_conv_w_pw2, v_conv_b_pw2, v_ffn_w1, v_ffn_w3, v_ffn_w2, v_norm_final)))
    T, D = x.shape[1], x.shape[2]
    c_idx = lax.axis_index("c").astype(jnp.int32).reshape(1)
    chip = (2 * lax.axis_index("x") + lax.axis_index("y")).astype(jnp.int32)

    as_rows = lambda a: a.reshape(-1, a.shape[-1])
    slabs = {n: _cast_into_slot(f"cast_{n}", lax.empty((N_CHIPS,) + as_rows(w[n]).shape, BF16), as_rows(w[n]),
                                chip.reshape(1)) for n in BIG}
    first, later = BIG[:2], BIG[2:]
    qkv_parts, w_o_parts = _gather_now("gather_attn", [slabs[n] for n in first])
    send_sems, recv_sems, travelling, gather_started = _gather_start("gather_start", [slabs[n] for n in later])
    layers = ffn_w1.shape[0]
    small_shapes = [w[n].shape for n in SMALL_SPLIT]
    small_all = _exchange_small("gather_small", _pack_rows([w[n] for n in SMALL_SPLIT], F32, 8), reduce=False)
    per_chip = [_unpack_rows(small_all[2 * j], small_shapes) for j in range(N_CHIPS)]
    full = {}
    for i, n in enumerate(SMALL_SPLIT):
        full[n] = _join_chip_axis(n, jnp.stack([per_chip[j][i] for j in range(N_CHIPS)]))

    def other_weights(after):
        landed = _gather_wait("gather_wait", send_sems, recv_sems, travelling, after)
        gathered = dict(zip(later, _swap_fetched_with_sibling("gather_swap", landed)))
        return {"conv_w_pw1": gathered["conv_w_pw1"], "conv_w_pw2": gathered["conv_w_pw2"].reshape(-1, D),
                "ffn_w1": gathered["ffn_w1"].reshape(N_CHIPS, layers, D, -1),
                "ffn_w3": gathered["ffn_w3"].reshape(N_CHIPS, layers, D, -1),
                "ffn_w2": gathered["ffn_w2"].reshape(N_CHIPS, layers, -1, D)}

    p = {
        "norm_mix": norm_mix, "norm_ffn": norm_ffn, "norm_final": norm_final.reshape(1, D),
        "attn_w_qkv": jnp.moveaxis(qkv_parts, 0, 1).reshape(D, -1), "attn_b_qkv": attn_b_qkv,
        "attn_sinks": attn_sinks, "attn_w_o": w_o_parts.reshape(-1, D), "attn_b_o": attn_b_o,
        "conv_b_pw1": full["conv_b_pw1"], "conv_w_dw": full["conv_w_dw"][0],
        "conv_b_dw": full["conv_b_dw"], "conv_ln_g": full["conv_ln_g"], "conv_ln_b": full["conv_ln_b"],
        "conv_b_pw2": full["conv_b_pw2"], "gather_started": gather_started, "other_weights": other_weights,
    }
    swapping, in_flight = {}, []

    def reduce_begin(tag, grads):
        keys = list(grads)
        *handles, begun = _sibling_swap_start(f"sibling_swap_start_{tag}", [grads[k] for k in keys])
        swapping[tag] = (keys, handles)
        return begun

    def reduce_send(tag, after):
        keys, (swap_send, swap_recv, grads, lands) = swapping[tag]
        grads, from_sibling = _sibling_swap_wait(f"sibling_swap_wait_{tag}", swap_send, swap_recv, grads, lands, after)
        partials = [_add_sibling_half(f"add_sibling_half_{tag}{i}", gr, fs, c_idx)
                    for i, (gr, fs) in enumerate(zip(grads, from_sibling))]
        *handles, sent = _scatter_start(f"scatter_start_{tag}", partials)
        in_flight.append((tag, keys, handles, grads, from_sibling))
        return sent

    loss_part, dx, g = _local_step(x[0], loss_target[0], p, reduce_begin, reduce_send)
    for n in SMALL_WHOLE + SMALL_SPLIT:
        g[n] = g[n].reshape((-1,) + g[n].shape[-2:]) if w[n].ndim == 3 else g[n].reshape(w[n].shape[:-1] + (-1,))

    place = jnp.stack([chip, c_idx[0]])
    shard_grad = {n: lax.empty(as_rows(w[n]).shape, F32) for n in BIG}
    for tag, keys, (send_sems, recv_sems, partials, lands), grads, from_sibling in in_flight:
        _, received = _scatter_wait(f"scatter_wait_{tag}", send_sems, recv_sems, partials, lands, dx)
        for i, (n, layer) in enumerate(keys):
            shard_grad[n] = _sum_chip_partials(f"sum_chip_partials_{tag}{i}", grads[i], from_sibling[i], received[i],
                                               shard_grad[n], layer, place)
    g_big = dict(zip(BIG, _join_halves([shard_grad[n] for n in BIG], [w[n].shape[0] for n in BIG])))
    big_out = {}
    for n in BIG:
        step = _adamw(f"adamw_{n}", as_rows(w[n]), g_big[n], as_rows(m[n]), as_rows(v[n]))
        big_out[n] = [a.reshape(w[n].shape) for a in step]

    small_whole_shapes = [w[n].shape for n in SMALL_WHOLE]
    small_full_shapes = [g[n].shape for n in SMALL_SPLIT]
    reduced = _exchange_small(
        "reduce_small", _pack_rows([loss_part] + [g[n] for n in SMALL_WHOLE] + [g[n] for n in SMALL_SPLIT], F32, 8),
        reduce=True)
    pieces = _unpack_rows(reduced, [(1,)] + small_whole_shapes + small_full_shapes)
    loss = pieces[0].reshape(())
    g_small = dict(zip(SMALL_WHOLE, pieces[1:1 + len(SMALL_WHOLE)]))
    for n, whole in zip(SMALL_SPLIT, pieces[1 + len(SMALL_WHOLE):]):
        parts = _split_chip_axis(n, whole, w[n].shape)
        g_small[n] = lax.dynamic_index_in_dim(parts, chip, axis=0, keepdims=False)
    small = SMALL_WHOLE + SMALL_SPLIT
    _, d_small, m_small, v_small = _adamw(
        "adamw_small", _pack_rows([w[n] for n in small], F32, 8), _pack_rows([g_small[n] for n in small], F32, 8),
        _pack_rows([m[n] for n in small], F32, 8), _pack_rows([v[n] for n in small], F32, 8))

    outs = {}
    for slot, (tag, small_pack) in enumerate((("g", None), ("d", d_small), ("m", m_small), ("v", v_small))):
        vals = {n: big_out[n][slot] for n in BIG}
        if small_pack is None:
            vals.update(g_small)
        else:
            vals.update(zip(small, _unpack_rows(small_pack, [w[n].shape for n in small])))
        outs[tag] = vals
    return (loss, dx.reshape(1, T, D), *[outs["g"][n] for n in WEIGHT_NAMES], *[outs["d"][n] for n in WEIGHT_NAMES],
            *[outs["m"][n] for n in WEIGHT_NAMES], *[outs["v"][n] for n in WEIGHT_NAMES])
```

```python
import functools

import jax
import jax.numpy as jnp
from jax import lax
from jax.experimental import pallas as pl
from jax.experimental.pallas import tpu as pltpu

F32 = jnp.float32
BF16 = jnp.bfloat16
SDS = jax.ShapeDtypeStruct
MESH = pl.DeviceIdType.MESH

HEAD_DIM = 64
N_Q_HEADS = 16
N_KV_HEADS = 2
Q_PER_KV = N_Q_HEADS // N_KV_HEADS
ATTN_BLOCK = 128
ROPE_THETA = 10000.0
CONV_WIDTH = 31
CONV_HALO = 32
CONV_FIRST_TAP = CONV_HALO - CONV_WIDTH + 1
CONV_ROW_CHUNK = 64
CONV_LANE_CHUNK = 256
CONV_GRAD_UNROLL = 4
RMS_EPS = 1e-5
LN_EPS = 1e-5
ADAM_LR = 0.001
ADAM_B1 = 0.9
ADAM_B2 = 0.999
ADAM_EPS = 1e-08
ADAM_WD = 0.01
ADAM_STEP = 10

V7X_LANES = 128
V7X_SUBLANES = 8
V7X_VMEM_LIMIT_BYTES = 56 * 1024 * 1024

N_CHIPS = 4
N_DEV = 8
PACK_W = 1024

MASK_VALUE = -1e30


def _params(*semantics):
    return pltpu.CompilerParams(dimension_semantics=semantics, vmem_limit_bytes=V7X_VMEM_LIMIT_BYTES)


def _rows(tm, width):
    return pl.BlockSpec((tm, width), lambda i: (i, 0))


def _whole(shape):
    return pl.BlockSpec(shape, lambda *_: (0,) * len(shape))


def _rms_rstd(h):
    return lax.rsqrt(jnp.mean(h * h, axis=-1, keepdims=True) + RMS_EPS)


def _silu_and_grad(z):
    sg = jax.nn.sigmoid(z)
    return z * sg, sg * (1.0 + z * (1.0 - sg))


def _swap_rope_halves(t):
    w = t.shape[1]
    half = HEAD_DIM // 2
    lane = lax.broadcasted_iota(jnp.int32, t.shape, 1)
    upper = pltpu.roll(t, w - half, 1)
    lower = pltpu.roll(t, half, 1)
    return jnp.where(lane % HEAD_DIM < half, upper, lower)


def _rope(t, cos_ref, sin_ref):
    reps = t.shape[1] // V7X_LANES
    c = jnp.tile(cos_ref[...], (1, reps))
    s = jnp.tile(sin_ref[...], (1, reps))
    return t * c + _swap_rope_halves(t) * s


def _rope_transposed(dt, cos_ref, sin_ref):
    reps = dt.shape[1] // V7X_LANES
    c = jnp.tile(cos_ref[...], (1, reps))
    s = jnp.tile(sin_ref[...], (1, reps))
    return dt * c + _swap_rope_halves(dt * s)


def _rope_tables(seq_len):
    pos = jnp.arange(seq_len, dtype=F32)
    inv_freq = ROPE_THETA ** (-jnp.arange(0, HEAD_DIM, 2, dtype=F32) / HEAD_DIM)
    ang = pos[:, None] * jnp.tile(inv_freq, 2 * V7X_LANES // HEAD_DIM)[None, :]
    upper_half = jnp.arange(V7X_LANES) % HEAD_DIM >= HEAD_DIM // 2
    return jnp.cos(ang), jnp.where(upper_half[None, :], jnp.sin(ang), -jnp.sin(ang))


def _qkv_proj(h, g, w, b, cos, sin, after):
    T, D = h.shape
    N = w.shape[1]
    tm = min(512, T)
    rope_w = N - N_KV_HEADS * HEAD_DIM

    def body(h_ref, g_ref, w_ref, b_ref, cos_ref, sin_ref, _, y_ref, o_ref):
        hh = h_ref[...]
        y = (hh * _rms_rstd(hh) * g_ref[...]).astype(BF16)
        y_ref[...] = y
        acc = jnp.dot(y, w_ref[...], preferred_element_type=F32) + b_ref[...]
        o_ref[:, :rope_w] = _rope(acc[:, :rope_w], cos_ref, sin_ref).astype(BF16)
        o_ref[:, rope_w:] = acc[:, rope_w:].astype(BF16)

    return pl.pallas_call(
        body, name="qkv_proj", grid=(T // tm,),
        in_specs=[_rows(tm, D), _whole((1, D)), _whole((D, N)), _whole((1, N)),
                  _rows(tm, V7X_LANES), _rows(tm, V7X_LANES), pl.BlockSpec(memory_space=pl.ANY)],
        out_specs=[_rows(tm, D), _rows(tm, N)],
        out_shape=[SDS((T, D), BF16), SDS((T, N), BF16)],
        compiler_params=_params("parallel"),
    )(h, g, w, b, cos, sin, after)


def _pw1_proj(h, g, w, b):
    T, D = h.shape
    n = w.shape[2]
    N = N_CHIPS * n
    tm = min(512, T)

    def body(h_ref, g_ref, w_ref, b_ref, y_ref, o_ref):
        hh = h_ref[...]
        y = (hh * _rms_rstd(hh) * g_ref[...]).astype(BF16)
        y_ref[...] = y
        for j in range(N_CHIPS):
            cols = slice(j * n, (j + 1) * n)
            o_ref[:, cols] = jnp.dot(y, w_ref[j], preferred_element_type=F32) + b_ref[:, cols]

    return pl.pallas_call(
        body, name="pw1_proj", grid=(T // tm,),
        in_specs=[_rows(tm, D), _whole((1, D)), _whole((N_CHIPS, D, n)), _whole((1, N))],
        out_specs=[_rows(tm, D), _rows(tm, N)],
        out_shape=[SDS((T, D), BF16), SDS((T, N), F32)],
        compiler_params=_params("parallel"),
    )(h, g, w, b)


PAIRS_PER_KV = Q_PER_KV // 2


def _upper_lanes(shape):
    return lax.broadcasted_iota(jnp.int32, shape, 1) >= HEAD_DIM


def _swap_lane_halves(t):
    return pltpu.roll(t.astype(F32), HEAD_DIM, 1).astype(t.dtype)


def _kv_operands(g, t):
    swapped = _swap_lane_halves(t)
    in_lower, in_upper = (t, swapped) if g == 0 else (swapped, t)
    upper = _upper_lanes(t.shape)
    zero = jnp.zeros_like(t)
    return jnp.where(upper, zero, in_lower), jnp.where(upper, in_upper, zero)


def _group_heads(g):
    pairs = range(g * PAIRS_PER_KV, (g + 1) * PAIRS_PER_KV)
    return [2 * hp for hp in pairs] + [2 * hp + 1 for hp in pairs]


def _all_heads():
    return [h for g in range(N_KV_HEADS) for h in _group_heads(g)]


def _pair_rows(ref, g):
    pairs = range(g * PAIRS_PER_KV, (g + 1) * PAIRS_PER_KV)
    return jnp.concatenate([ref[:, hp * 2 * HEAD_DIM:(hp + 1) * 2 * HEAD_DIM] for hp in pairs], axis=0)


def _from_previous_block(rows):
    row = lax.broadcasted_iota(jnp.int32, (ATTN_BLOCK, ATTN_BLOCK), 0)
    col = lax.broadcasted_iota(jnp.int32, (ATTN_BLOCK, ATTN_BLOCK), 1)
    return jnp.concatenate([col > row] * (rows // ATTN_BLOCK), axis=0)


def _folded_probs(n, q_groups, k_prev_groups, k_cur_groups, sink_ref, prev_part):
    def scores(q, k):
        return lax.dot_general(q, k, (((1,), (1,)), ((), ())), preferred_element_type=F32)

    s_prev = jnp.concatenate([scores(q, k[i]) for q, k in zip(q_groups, k_prev_groups) for i in range(2)], axis=0)
    s_cur = jnp.concatenate([scores(q, k[i]) for q, k in zip(q_groups, k_cur_groups) for i in range(2)], axis=0)
    s_prev = jnp.where(n > 0, s_prev, MASK_VALUE * (HEAD_DIM ** 0.5))
    s = jnp.where(prev_part, s_prev, s_cur) * (HEAD_DIM ** -0.5)
    heads = [h for g in range(N_KV_HEADS) for h in _group_heads(g)]
    sink = jnp.concatenate([jnp.broadcast_to(sink_ref[0:1, h:h + 1], (ATTN_BLOCK, 1)) for h in heads], axis=0)
    m = jnp.maximum(jnp.max(s, axis=1, keepdims=True), sink)
    p = jnp.exp(s - m)
    e_sink = jnp.exp(sink - m)
    inv = 1.0 / (jnp.sum(p, axis=1, keepdims=True) + e_sink)
    return p * inv, e_sink * inv


def _split_folded(t, prev_part):
    tb = t.astype(BF16)
    zero = jnp.zeros_like(tb)
    return jnp.where(prev_part, tb, zero), jnp.where(prev_part, zero, tb)


def _attn_specs(T):
    nb = T // ATTN_BLOCK
    kcol = N_Q_HEADS * HEAD_DIM // V7X_LANES
    cur = lambda n: jnp.minimum(n, nb - 1)
    prev = lambda n: jnp.maximum(jnp.minimum(n, nb - 1) - 1, 0)
    q_spec = pl.BlockSpec((ATTN_BLOCK, N_Q_HEADS * HEAD_DIM), lambda n: (cur(n), 0))
    kc_spec = pl.BlockSpec((ATTN_BLOCK, V7X_LANES), lambda n: (cur(n), kcol))
    kp_spec = pl.BlockSpec((ATTN_BLOCK, V7X_LANES), lambda n: (prev(n), kcol))
    vc_spec = pl.BlockSpec((ATTN_BLOCK, V7X_LANES), lambda n: (cur(n), kcol + 1))
    vp_spec = pl.BlockSpec((ATTN_BLOCK, V7X_LANES), lambda n: (prev(n), kcol + 1))
    return q_spec, kc_spec, kp_spec, vc_spec, vp_spec


def _attn_fwd(qkv, sinks):
    T = qkv.shape[0]
    nb = T // ATTN_BLOCK
    qw = N_Q_HEADS * HEAD_DIM
    all_rows = N_Q_HEADS * ATTN_BLOCK

    def body(q_ref, kc_ref, kp_ref, vc_ref, vp_ref, sink_ref, o_ref, probs_ref, psink_ref):
        n = pl.program_id(0)
        prev_part = _from_previous_block(all_rows)
        half = PAIRS_PER_KV * ATTN_BLOCK
        groups = range(N_KV_HEADS)
        probs, p_sink = _folded_probs(n, [_pair_rows(q_ref, g) for g in groups],
                                      [_kv_operands(g, kp_ref[...]) for g in groups],
                                      [_kv_operands(g, kc_ref[...]) for g in groups], sink_ref, prev_part)
        probs_ref[...] = probs.astype(BF16)
        lane = lax.broadcasted_iota(jnp.int32, (ATTN_BLOCK, V7X_LANES), 1)
        sink_tile = jnp.zeros((ATTN_BLOCK, V7X_LANES), F32)
        for i, h in enumerate(_all_heads()):
            sink_tile = jnp.where(lane == h, p_sink[i * ATTN_BLOCK:(i + 1) * ATTN_BLOCK], sink_tile)
        psink_ref[...] = sink_tile
        p_prev, p_cur = _split_folded(probs, prev_part)
        for g in groups:
            v_prev, v_cur = _kv_operands(g, vp_ref[...]), _kv_operands(g, vc_ref[...])
            even, odd = slice(2 * g * half, (2 * g + 1) * half), slice((2 * g + 1) * half, (2 * g + 2) * half)
            o = (jnp.dot(p_prev[even], v_prev[0], preferred_element_type=F32)
                 + jnp.dot(p_cur[even], v_cur[0], preferred_element_type=F32)
                 + jnp.dot(p_prev[odd], v_prev[1], preferred_element_type=F32)
                 + jnp.dot(p_cur[odd], v_cur[1], preferred_element_type=F32))
            for i in range(PAIRS_PER_KV):
                hp = g * PAIRS_PER_KV + i
                o_ref[:, hp * 2 * HEAD_DIM:(hp + 1) * 2 * HEAD_DIM] = (
                    o[i * ATTN_BLOCK:(i + 1) * ATTN_BLOCK].astype(BF16))

    return pl.pallas_call(
        body, name="attn_fwd", grid=(nb,),
        in_specs=[*_attn_specs(T), _whole((1, N_Q_HEADS))],
        out_specs=[_rows(ATTN_BLOCK, qw), pl.BlockSpec((None, all_rows, ATTN_BLOCK), lambda n: (n, 0, 0)),
                   _rows(ATTN_BLOCK, V7X_LANES)],
        out_shape=[SDS((T, qw), BF16), SDS((nb, all_rows, ATTN_BLOCK), BF16), SDS((T, V7X_LANES), F32)],
        compiler_params=_params("parallel"),
    )(qkv, qkv, qkv, qkv, qkv, sinks)


def _mm_res(name, a, w, b, res, g):
    T, K = a.shape
    D = w.shape[1]
    tm = min(512, T)

    def body(a_ref, w_ref, b_ref, r_ref, g_ref, o_ref, f_ref):
        h = jnp.dot(a_ref[...], w_ref[...], preferred_element_type=F32) + b_ref[...] + r_ref[...]
        o_ref[...] = h
        f_ref[...] = (h * _rms_rstd(h) * g_ref[...]).astype(BF16)

    return pl.pallas_call(
        body, name=name, grid=(T // tm,),
        in_specs=[_rows(tm, K), _whole((K, D)), _whole((1, D)), _rows(tm, D), _whole((1, D))],
        out_specs=[_rows(tm, D), _rows(tm, D)],
        out_shape=[SDS((T, D), F32), SDS((T, D), BF16)],
        compiler_params=_params("parallel"),
    )(a, w, b, res, g)


def _ffn_down(name, s, w2, layer, res):
    _, T, n = s.shape
    D = w2.shape[3]
    tm = min(512, T)

    def body(s_ref, w_ref, r_ref, o_ref):
        acc = r_ref[...]
        for j in range(N_CHIPS):
            acc = acc + jnp.dot(s_ref[j], w_ref[j], preferred_element_type=F32)
        o_ref[...] = acc

    return pl.pallas_call(
        body, name=name, grid=(T // tm,),
        in_specs=[pl.BlockSpec((N_CHIPS, tm, n), lambda i: (0, i, 0)),
                  pl.BlockSpec((N_CHIPS, None, n, D), lambda i: (0, layer, 0, 0)), _rows(tm, D)],
        out_specs=_rows(tm, D),
        out_shape=SDS((T, D), F32),
        compiler_params=_params("parallel"),
    )(s, w2, res)


def _ffn_up(name, f, w1, w3, layer):
    T, D = f.shape
    n = w1.shape[3]
    tm = min(1024, T)

    def body(f_ref, w1_ref, w3_ref, act_ref, gg_ref, s_ref):
        ff = f_ref[...]
        g1 = jnp.dot(ff, w1_ref[...], preferred_element_type=F32)
        g3 = jnp.dot(ff, w3_ref[...], preferred_element_type=F32)
        act, dact = _silu_and_grad(g1)
        act_ref[...] = act.astype(BF16)
        gg_ref[...] = (g3 * dact).astype(BF16)
        s_ref[...] = (act * g3).astype(BF16)

    slab = pl.BlockSpec((None, tm, n), lambda j, i: (j, i, 0))
    wslab = pl.BlockSpec((None, None, D, n), lambda j, i: (j, layer, 0, 0))
    hidden = SDS((N_CHIPS, T, n), BF16)
    return pl.pallas_call(
        body, name=name, grid=(N_CHIPS, T // tm),
        in_specs=[pl.BlockSpec((tm, D), lambda j, i: (i, 0)), wslab, wslab],
        out_specs=[slab, slab, slab],
        out_shape=[hidden, hidden, hidden],
        compiler_params=_params("parallel", "parallel"),
    )(f, w1, w3)


def _glu(a, d):
    return a[:, :d] * jax.nn.sigmoid(a[:, d:])


def _conv_tile(T):
    return min(256, T)


def _fill_shifted(sh_ref, tc):
    n = tc + CONV_HALO - V7X_SUBLANES
    for r in range(1, V7X_SUBLANES):
        sh_ref[r, 0:n, :] = sh_ref[0, pl.ds(r, n), :]


def _depthwise_taps(sh_ref, w_ref, offsets, bias_ref, out_ref, tc):
    D = out_ref.shape[1]

    def chunk(i, carry):
        t0 = pl.multiple_of(i * CONV_ROW_CHUNK, CONV_ROW_CHUNK)
        for cb in range(D // CONV_LANE_CHUNK):
            cs = slice(cb * CONV_LANE_CHUNK, (cb + 1) * CONV_LANE_CHUNK)
            acc = jnp.zeros((CONV_ROW_CHUNK, CONV_LANE_CHUNK), F32)
            for r in range(V7X_SUBLANES):
                taps = [(j, o // V7X_SUBLANES) for j, o in enumerate(offsets) if o % V7X_SUBLANES == r]
                if not taps:
                    continue
                span = CONV_ROW_CHUNK + V7X_SUBLANES * max(q for _, q in taps)
                rows = sh_ref[r, pl.ds(t0, span), cs]
                for j, q in taps:
                    acc = acc + rows[V7X_SUBLANES * q:V7X_SUBLANES * q + CONV_ROW_CHUNK] * w_ref[j:j + 1, cs]
            if bias_ref is not None:
                acc = acc + bias_ref[:, cs]
            out_ref[pl.ds(t0, CONV_ROW_CHUNK), cs] = acc
        return carry

    lax.fori_loop(0, tc // CONV_ROW_CHUNK, chunk, 0)


def _depthwise_tap_grads(dy_sh, x_sh, offsets, dw_ref, tc):
    D = dw_ref.shape[1]
    for cb in range(D // V7X_LANES):
        cs = slice(cb * V7X_LANES, (cb + 1) * V7X_LANES)

        def row_tiles(i, accs, cs=cs):
            for k in range(CONV_GRAD_UNROLL):
                t0 = pl.multiple_of(i * (CONV_GRAD_UNROLL * V7X_SUBLANES), V7X_SUBLANES) + k * V7X_SUBLANES
                d = dy_sh[0, pl.ds(t0, V7X_SUBLANES), cs]
                accs = tuple(
                    acc + d * x_sh[o % V7X_SUBLANES, pl.ds(t0 + o // V7X_SUBLANES * V7X_SUBLANES, V7X_SUBLANES), cs]
                    for acc, o in zip(accs, offsets))
            return accs

        zero = jnp.zeros((V7X_SUBLANES, V7X_LANES), F32)
        accs = lax.fori_loop(0, tc // (CONV_GRAD_UNROLL * V7X_SUBLANES), row_tiles, tuple(zero for _ in offsets))
        for j, acc in enumerate(accs):
            dw_ref[j:j + 1, cs] += jnp.sum(acc, axis=0, keepdims=True)


def _conv_fwd(a, w_dw, b_dw, ln_g, ln_b):
    T = a.shape[0]
    D = a.shape[1] // 2
    tc = _conv_tile(T)
    per = tc // CONV_HALO

    def body(a_ref, ah_ref, w_ref, bdw_ref, lg_ref, lb_ref, c_ref, act_ref, u_sh):
        i = pl.program_id(0)
        u_sh[0, 0:CONV_HALO, :] = jnp.where(i > 0, _glu(ah_ref[...], D), 0.0)
        u_sh[0, CONV_HALO:, :] = _glu(a_ref[...], D)
        _fill_shifted(u_sh, tc)
        _depthwise_taps(u_sh, w_ref, [CONV_FIRST_TAP + j for j in range(CONV_WIDTH)], bdw_ref, c_ref, tc)
        c = c_ref[...]
        xc = c - jnp.mean(c, axis=-1, keepdims=True)
        z = xc * lax.rsqrt(jnp.mean(xc * xc, axis=-1, keepdims=True) + LN_EPS)
        l = z * lg_ref[...] + lb_ref[...]
        act_ref[...] = (l * jax.nn.sigmoid(l)).astype(BF16)

    return pl.pallas_call(
        body, name="conv_fwd", grid=(T // tc,),
        in_specs=[_rows(tc, 2 * D),
                  pl.BlockSpec((CONV_HALO, 2 * D), lambda i: (jnp.maximum(i * per - 1, 0), 0)),
                  _whole((CONV_WIDTH, D)), _whole((1, D)), _whole((1, D)), _whole((1, D))],
        out_specs=[_rows(tc, D), _rows(tc, D)],
        out_shape=[SDS((T, D), F32), SDS((T, D), BF16)],
        scratch_shapes=[pltpu.VMEM((V7X_SUBLANES, tc + CONV_HALO, D), F32)],
        compiler_params=_params("parallel"),
    )(a, a, w_dw, b_dw, ln_g, ln_b)


def _final_loss(h, g, target):
    T, D = h.shape
    tm = min(512, T)

    def body(h_ref, g_ref, t_ref, dh_ref, loss_ref, dg_ref):
        @pl.when(pl.program_id(0) == 0)
        def _():
            loss_ref[...] = jnp.zeros_like(loss_ref)
            dg_ref[...] = jnp.zeros_like(dg_ref)

        hh = h_ref[...]
        r = _rms_rstd(hh)
        g = g_ref[...]
        d = hh * r * g - t_ref[...]
        loss_ref[...] += 0.5 * jnp.sum(jnp.mean(d * d, axis=-1, keepdims=True), axis=0, keepdims=True)
        dout = d * (1.0 / D)
        dg_ref[...] += jnp.sum(dout * (hh * r), axis=0, keepdims=True)
        dxh = dout * g
        dh_ref[...] = r * dxh - hh * (r * r * r) * jnp.mean(dxh * hh, axis=-1, keepdims=True)

    return pl.pallas_call(
        body, name="final_loss", grid=(T // tm,),
        in_specs=[_rows(tm, D), _whole((1, D)), _rows(tm, D)],
        out_specs=[_rows(tm, D), _whole((1, 1)), _whole((1, D))],
        out_shape=[SDS((T, D), F32), SDS((1, 1), F32), SDS((1, D), F32)],
        compiler_params=_params("arbitrary"),
    )(h, g, target)


def _ffn_bwd_act(name, dh, w2, layer, act, gate_grad):
    T, D = dh.shape
    n = w2.shape[2]
    tm = min(1024, T)

    def body(dh_ref, w2_ref, act_ref, gg_ref, dg1_ref, dg3_ref):
        ds = lax.dot_general(dh_ref[...].astype(BF16), w2_ref[...], (((1,), (1,)), ((), ())),
                             preferred_element_type=F32)
        dg1_ref[...] = (ds * gg_ref[...].astype(F32)).astype(BF16)
        dg3_ref[...] = (ds * act_ref[...].astype(F32)).astype(BF16)

    slab = pl.BlockSpec((None, tm, n), lambda i, j: (j, i, 0))
    hidden = SDS((N_CHIPS, T, n), BF16)
    return pl.pallas_call(
        body, name=name, grid=(T // tm, N_CHIPS),
        in_specs=[pl.BlockSpec((tm, D), lambda i, j: (i, 0)),
                  pl.BlockSpec((None, None, n, D), lambda i, j: (j, layer, 0, 0)), slab, slab],
        out_specs=[slab, slab],
        out_shape=[hidden, hidden],
        compiler_params=_params("parallel", "arbitrary"),
    )(dh, w2, act, gate_grad)


def _dot_tn(a, b):
    return lax.dot_general(a.astype(BF16), b.astype(BF16), (((0,), (0,)), ((), ())), preferred_element_type=F32)


def _dot_nt(a, b):
    return lax.dot_general(a.astype(BF16), b, (((1,), (1,)), ((), ())), preferred_element_type=F32)


def _mm_tn(name, a, b, col_chunks=1, after=None):
    a_slabs, b_slabs = a.ndim == 3, b.ndim == 3
    T = a.shape[-2]
    tt = min(1024, T)
    ka, nb = a.shape[-1], b.shape[-1]
    if a_slabs or b_slabs:
        out_dims = (N_CHIPS, ka, nb)
    elif col_chunks > 1:
        out_dims = (col_chunks, ka, nb // col_chunks)
    else:
        out_dims = (ka, nb)

    def body(a_ref, b_ref, *rest):
        o_ref = rest[-1]

        @pl.when(pl.program_id(0) == 0)
        def _():
            o_ref[...] = jnp.zeros_like(o_ref)

        if a_slabs:
            bb = b_ref[...].astype(BF16)
            for j in range(N_CHIPS):
                o_ref[j] += _dot_tn(a_ref[j], bb)
        elif b_slabs:
            aa = a_ref[...].astype(BF16)
            for j in range(N_CHIPS):
                o_ref[j] += _dot_tn(aa, b_ref[j])
        elif col_chunks > 1:
            aa = a_ref[...].astype(BF16)
            w = nb // col_chunks
            for j in range(col_chunks):
                o_ref[j] += _dot_tn(aa, b_ref[:, j * w:(j + 1) * w])
        else:
            o_ref[...] += _dot_tn(a_ref[...], b_ref[...])

    def spec(arr, slabs):
        if slabs:
            return pl.BlockSpec((N_CHIPS, tt, arr.shape[-1]), lambda t: (0, t, 0))
        return _rows(tt, arr.shape[-1])

    return pl.pallas_call(
        body, name=name, grid=(T // tt,),
        in_specs=[spec(a, a_slabs), spec(b, b_slabs)] + ([] if after is None else [pl.BlockSpec(memory_space=pl.ANY)]),
        out_specs=_whole(out_dims),
        out_shape=SDS(out_dims, F32),
        compiler_params=_params("arbitrary"),
    )(a, b, *([] if after is None else [after]))


def _mm_nt_normbwd(name, pairs, h, g, dh, after):
    T, D = h.shape
    tm = min(512, T)
    n_pairs = len(pairs)
    kinds = ["slabs" if dy.ndim == 3 else ("quarters" if w.ndim == 3 else "plain") for dy, w, _ in pairs]

    def body(*refs):
        dy_refs = refs[:n_pairs]
        w_refs = refs[n_pairs:2 * n_pairs]
        h_ref, g_ref, dh_ref, _, o_ref, dg_ref, cs_ref = refs[2 * n_pairs:]

        @pl.when(pl.program_id(0) == 0)
        def _():
            dg_ref[...] = jnp.zeros_like(dg_ref)
            cs_ref[...] = jnp.zeros_like(cs_ref)

        df = jnp.zeros((tm, D), F32)
        for dy_ref, w_ref, kd in zip(dy_refs, w_refs, kinds):
            if kd == "slabs":
                for j in range(N_CHIPS):
                    df = df + _dot_nt(dy_ref[j], w_ref[j])
            elif kd == "quarters":
                n = w_ref.shape[2]
                for j in range(N_CHIPS):
                    df = df + _dot_nt(dy_ref[:, j * n:(j + 1) * n], w_ref[j])
            else:
                df = df + _dot_nt(dy_ref[...], w_ref[...])
        hh = h_ref[...]
        r = _rms_rstd(hh)
        dg_ref[...] += jnp.sum(df * (hh * r), axis=0, keepdims=True)
        dxh = df * g_ref[...]
        out = dh_ref[...] + (r * dxh - hh * (r * r * r) * jnp.mean(dxh * hh, axis=-1, keepdims=True))
        o_ref[...] = out
        cs_ref[...] += jnp.sum(out, axis=0, keepdims=True)

    dy_specs, w_specs = [], []
    for (dy, w, layer), kd in zip(pairs, kinds):
        if kd == "slabs":
            dy_specs.append(pl.BlockSpec((N_CHIPS, tm, dy.shape[2]), lambda i: (0, i, 0)))
            w_specs.append(pl.BlockSpec((N_CHIPS, None, D, w.shape[3]),
                                        functools.partial(lambda i, layer: (0, layer, 0, 0), layer=layer),
                                        pipeline_mode=pl.Buffered(1)))
        else:
            dy_specs.append(_rows(tm, dy.shape[1]))
            w_specs.append(_whole(w.shape))

    return pl.pallas_call(
        body, name=name, grid=(T // tm,),
        in_specs=[*dy_specs, *w_specs, _rows(tm, D), _whole((1, D)), _rows(tm, D), pl.BlockSpec(memory_space=pl.ANY)],
        out_specs=[_rows(tm, D), _whole((1, D)), _whole((1, D))],
        out_shape=[SDS((T, D), F32), SDS((1, D), F32), SDS((1, D), F32)],
        compiler_params=_params("arbitrary"),
    )(*[dy for dy, _, _ in pairs], *[w for _, w, _ in pairs], h, g, dh, after)


def _mm_nt(name, dy, w, out_dtype):
    T, N = dy.shape
    K = w.shape[0]
    tm = min(512, T)

    def body(dy_ref, w_ref, o_ref):
        o_ref[...] = lax.dot_general(dy_ref[...].astype(BF16), w_ref[...], (((1,), (1,)), ((), ())),
                                     preferred_element_type=F32).astype(out_dtype)

    return pl.pallas_call(
        body, name=name, grid=(T // tm,),
        in_specs=[_rows(tm, N), _whole((K, N))],
        out_specs=_rows(tm, K),
        out_shape=SDS((T, K), out_dtype),
        compiler_params=_params("parallel"),
    )(dy, w)


def _conv_bwd(dact, c, a, w_dw, ln_g, ln_b):
    T, D = c.shape
    tc = _conv_tile(T)
    per = tc // CONV_HALO
    n_tiles = T // tc
    last_halo = T // CONV_HALO - 1

    def ln_bwd(dact_v, c_v, lg, lb):
        xc = c_v - jnp.mean(c_v, axis=-1, keepdims=True)
        rstd = lax.rsqrt(jnp.mean(xc * xc, axis=-1, keepdims=True) + LN_EPS)
        z = xc * rstd
        _, dsilu = _silu_and_grad(z * lg + lb)
        dl = dact_v * dsilu
        dz = dl * lg
        dc = rstd * (dz - jnp.mean(dz, axis=-1, keepdims=True) - z * jnp.mean(dz * z, axis=-1, keepdims=True))
        return dc, dl, z

    def body(dact_ref, dactn_ref, c_ref, cn_ref, a_ref, ah_ref, w_ref, lg_ref, lb_ref,
             da_ref, dlg_ref, dlb_ref, dbdw_ref, dwdw_ref, dbpw1_ref, dc_sh, u_sh, du_scr):
        i = pl.program_id(0)

        @pl.when(i == 0)
        def _():
            for ref in (dlg_ref, dlb_ref, dbdw_ref, dwdw_ref, dbpw1_ref):
                ref[...] = jnp.zeros_like(ref)

        lg, lb = lg_ref[...], lb_ref[...]
        dc, dl, z = ln_bwd(dact_ref[...], c_ref[...], lg, lb)
        dlg_ref[...] += jnp.sum(dl * z, axis=0, keepdims=True)
        dlb_ref[...] += jnp.sum(dl, axis=0, keepdims=True)
        dbdw_ref[...] += jnp.sum(dc, axis=0, keepdims=True)
        dcn, _, _ = ln_bwd(dactn_ref[...], cn_ref[...], lg, lb)
        dc_sh[0, 0:tc, :] = dc
        dc_sh[0, tc:, :] = jnp.where(i < n_tiles - 1, dcn, 0.0)
        _fill_shifted(dc_sh, tc)

        a_v = a_ref[...]
        a1 = a_v[:, :D]
        sg = jax.nn.sigmoid(a_v[:, D:])
        u_sh[0, 0:CONV_HALO, :] = jnp.where(i > 0, _glu(ah_ref[...], D), 0.0)
        u_sh[0, CONV_HALO:, :] = a1 * sg
        _fill_shifted(u_sh, tc)

        _depthwise_taps(dc_sh, w_ref, [CONV_WIDTH - 1 - j for j in range(CONV_WIDTH)], None, du_scr, tc)
        _depthwise_tap_grads(dc_sh, u_sh, [CONV_FIRST_TAP + j for j in range(CONV_WIDTH)], dwdw_ref, tc)

        du = du_scr[...]
        da1 = du * sg
        da2 = du * a1 * sg * (1.0 - sg)
        da_ref[:, :D] = da1.astype(BF16)
        da_ref[:, D:] = da2.astype(BF16)
        dbpw1_ref[:, :D] += jnp.sum(da1, axis=0, keepdims=True)
        dbpw1_ref[:, D:] += jnp.sum(da2, axis=0, keepdims=True)

    nxt = lambda i: (jnp.minimum((i + 1) * per, last_halo), 0)
    return pl.pallas_call(
        body, name="conv_bwd", grid=(n_tiles,),
        in_specs=[_rows(tc, D), pl.BlockSpec((CONV_HALO, D), nxt),
                  _rows(tc, D), pl.BlockSpec((CONV_HALO, D), nxt),
                  _rows(tc, 2 * D),
                  pl.BlockSpec((CONV_HALO, 2 * D), lambda i: (jnp.maximum(i * per - 1, 0), 0)),
                  _whole((CONV_WIDTH, D)), _whole((1, D)), _whole((1, D))],
        out_specs=[_rows(tc, 2 * D), _whole((1, D)), _whole((1, D)), _whole((1, D)),
                   _whole((CONV_HALO, D)), _whole((1, 2 * D))],
        out_shape=[SDS((T, 2 * D), BF16), SDS((1, D), F32), SDS((1, D), F32), SDS((1, D), F32),
                   SDS((CONV_HALO, D), F32), SDS((1, 2 * D), F32)],
        scratch_shapes=[pltpu.VMEM((V7X_SUBLANES, tc + CONV_HALO, D), F32),
                        pltpu.VMEM((V7X_SUBLANES, tc + CONV_HALO, D), F32), pltpu.VMEM((tc, D), F32)],
        compiler_params=_params("arbitrary"),
    )(dact, dact, c, c, a, a, w_dw, ln_g, ln_b)


def _attn_bwd(qkv, dao, cos, sin, probs_saved, psink_saved):
    T = qkv.shape[0]
    nb = T // ATTN_BLOCK
    qw = N_Q_HEADS * HEAD_DIM
    kw = N_KV_HEADS * HEAD_DIM

    def body(q_ref, kc_ref, kp_ref, vc_ref, vp_ref, do_ref, cos_ref, sin_ref, cosp_ref, sinp_ref, probs_ref, psink_ref,
             dq_ref, dkv_ref, dsink_ref, dbq_ref, dbkv_ref, carry, prev_scr, cur_scr, dq_scr):
        n = pl.program_id(0)

        @pl.when(n == 0)
        def _():
            for ref in (dsink_ref, dbq_ref, dbkv_ref, carry):
                ref[...] = jnp.zeros_like(ref)

        @pl.when(n == nb)
        def _():
            prev_scr[...] = jnp.zeros_like(prev_scr)

        @pl.when(n < nb)
        def _():
            prev_part = _from_previous_block(N_Q_HEADS * ATTN_BLOCK)
            half = PAIRS_PER_KV * ATTN_BLOCK
            upper = _upper_lanes((ATTN_BLOCK, 2 * HEAD_DIM))
            groups = range(N_KV_HEADS)

            def nt(a, b):
                return lax.dot_general(a, b, (((1,), (1,)), ((), ())), preferred_element_type=F32)

            def kv_grad(even_rows, odd_rows, x):
                even = lax.dot_general(even_rows, x, (((0,), (0,)), ((), ())), preferred_element_type=F32)
                odd = lax.dot_general(odd_rows, x, (((0,), (0,)), ((), ())), preferred_element_type=F32)
                t = jnp.where(upper, odd, even)
                return t + _swap_lane_halves(t)

            q = [_pair_rows(q_ref, g) for g in groups]
            do = [_pair_rows(do_ref, g) for g in groups]
            k_prev = [_kv_operands(g, kp_ref[...]) for g in groups]
            k_cur = [_kv_operands(g, kc_ref[...]) for g in groups]
            v_prev = [_kv_operands(g, vp_ref[...]) for g in groups]
            v_cur = [_kv_operands(g, vc_ref[...]) for g in groups]
            probs = probs_ref[...].astype(F32)
            dp_prev = jnp.concatenate([nt(do[g], v_prev[g][i]) for g in groups for i in range(2)], axis=0)
            dp_cur = jnp.concatenate([nt(do[g], v_cur[g][i]) for g in groups for i in range(2)], axis=0)
            dp = jnp.where(prev_part, dp_prev, dp_cur)
            delta = jnp.sum(probs * dp, axis=1, keepdims=True)
            ds_prev, ds_cur = _split_folded(probs * (dp - delta) * (HEAD_DIM ** -0.5), prev_part)
            p_prev, p_cur = _split_folded(probs_ref[...], prev_part)
            for i, h in enumerate(_all_heads()):
                rows = slice(i * ATTN_BLOCK, (i + 1) * ATTN_BLOCK)
                dsink_ref[:, h:h + 1] += jnp.sum(-(psink_ref[:, h:h + 1] * delta[rows]), axis=0, keepdims=True)
            kv_grads = []
            for g in groups:
                even, odd = slice(2 * g * half, (2 * g + 1) * half), slice((2 * g + 1) * half, (2 * g + 2) * half)
                dq = (jnp.dot(ds_prev[even], k_prev[g][0], preferred_element_type=F32)
                      + jnp.dot(ds_cur[even], k_cur[g][0], preferred_element_type=F32)
                      + jnp.dot(ds_prev[odd], k_prev[g][1], preferred_element_type=F32)
                      + jnp.dot(ds_cur[odd], k_cur[g][1], preferred_element_type=F32))
                for i in range(PAIRS_PER_KV):
                    hp = g * PAIRS_PER_KV + i
                    dq_scr[:, hp * 2 * HEAD_DIM:(hp + 1) * 2 * HEAD_DIM] = dq[i * ATTN_BLOCK:(i + 1) * ATTN_BLOCK]
                kv_grads.append((kv_grad(ds_prev[even], ds_prev[odd], q[g]), kv_grad(ds_cur[even], ds_cur[odd], q[g]),
                                 kv_grad(p_prev[even], p_prev[odd], do[g]), kv_grad(p_cur[even], p_cur[odd], do[g])))
            (dkp0, dkc0, dvp0, dvc0), (dkp1, dkc1, dvp1, dvc1) = kv_grads
            prev_scr[:, :kw] = jnp.where(upper, dkp1, dkp0)
            prev_scr[:, kw:] = jnp.where(upper, dvp1, dvp0)
            cur_scr[:, :kw] = jnp.where(upper, dkc1, dkc0)
            cur_scr[:, kw:] = jnp.where(upper, dvc1, dvc0)
            dq_pre = _rope_transposed(dq_scr[...], cos_ref, sin_ref)
            dq_ref[...] = dq_pre.astype(BF16)
            dbq_ref[...] += jnp.sum(dq_pre, axis=0, keepdims=True)

        tot = carry[...] + prev_scr[...]
        dk_pre = _rope_transposed(tot[:, :kw], cosp_ref, sinp_ref)
        dkv_ref[:, :kw] = dk_pre.astype(BF16)
        dkv_ref[:, kw:] = tot[:, kw:].astype(BF16)
        dbkv_ref[:, :kw] += jnp.sum(dk_pre, axis=0, keepdims=True)
        dbkv_ref[:, kw:] += jnp.sum(tot[:, kw:], axis=0, keepdims=True)

        @pl.when(n < nb)
        def _():
            carry[...] = cur_scr[...]

    cur = lambda n: (jnp.minimum(n, nb - 1), 0)
    out_lag = lambda n: (jnp.maximum(n - 1, 0), 0)
    return pl.pallas_call(
        body, name="attn_bwd", grid=(nb + 1,),
        in_specs=[*_attn_specs(T),
                  pl.BlockSpec((ATTN_BLOCK, qw), cur),
                  pl.BlockSpec((ATTN_BLOCK, V7X_LANES), cur), pl.BlockSpec((ATTN_BLOCK, V7X_LANES), cur),
                  pl.BlockSpec((ATTN_BLOCK, V7X_LANES), out_lag), pl.BlockSpec((ATTN_BLOCK, V7X_LANES), out_lag),
                  pl.BlockSpec((None, N_Q_HEADS * ATTN_BLOCK, ATTN_BLOCK), lambda n: (jnp.minimum(n, nb - 1), 0, 0)),
                  pl.BlockSpec((ATTN_BLOCK, V7X_LANES), cur)],
        out_specs=[pl.BlockSpec((ATTN_BLOCK, qw), cur), pl.BlockSpec((ATTN_BLOCK, 2 * kw), out_lag),
                   _whole((1, N_Q_HEADS)), _whole((1, qw)), _whole((1, 2 * kw))],
        out_shape=[SDS((T, qw), BF16), SDS((T, 2 * kw), BF16),
                   SDS((1, N_Q_HEADS), F32), SDS((1, qw), F32), SDS((1, 2 * kw), F32)],
        scratch_shapes=[pltpu.VMEM((ATTN_BLOCK, 2 * kw), F32), pltpu.VMEM((ATTN_BLOCK, 2 * kw), F32),
                        pltpu.VMEM((ATTN_BLOCK, 2 * kw), F32), pltpu.VMEM((ATTN_BLOCK, qw), F32)],
        compiler_params=_params("arbitrary"),
    )(qkv, qkv, qkv, qkv, qkv, dao, cos, sin, cos, sin, probs_saved, psink_saved)


def _local_step(x, target, p, reduce_begin, reduce_send):
    T, D = x.shape
    cos, sin = _rope_tables(T)
    qw = N_Q_HEADS * HEAD_DIM
    nm, nf = p["norm_mix"], p["norm_ffn"]

    y0, qkv = _qkv_proj(x, nm[0:1], p["attn_w_qkv"], p["attn_b_qkv"], cos, sin, p["gather_started"])
    ao, attn_probs, sink_probs = _attn_fwd(qkv, p["attn_sinks"])
    h1, f0 = _mm_res("attn_out", ao, p["attn_w_o"], p["attn_b_o"], x, nf[0:1])
    p = {**p, **p["other_weights"](h1)}
    w1, w3, w2 = p["ffn_w1"], p["ffn_w3"], p["ffn_w2"]
    act0, gg0, s0 = _ffn_up("ffn0_up", f0, w1, w3, 0)
    h2 = _ffn_down("ffn0_down", s0, w2, 0, h1)
    y1, a = _pw1_proj(h2, nm[1:2], p["conv_w_pw1"], p["conv_b_pw1"])
    c, act = _conv_fwd(a, p["conv_w_dw"], p["conv_b_dw"], p["conv_ln_g"], p["conv_ln_b"])
    h3, f1 = _mm_res("conv_out", act, p["conv_w_pw2"], p["conv_b_pw2"], h2, nf[1:2])
    act1, gg1, s1 = _ffn_up("ffn1_up", f1, w1, w3, 1)
    h4 = _ffn_down("ffn1_down", s1, w2, 1, h3)
    dh4, loss, d_norm_final = _final_loss(h4, p["norm_final"], target)

    g = {}
    dg1, dg3 = _ffn_bwd_act("ffn1_bwd_act", dh4, w2, 1, act1, gg1)
    dw2_1 = _mm_tn("ffn1_dw2", s1, dh4)
    dw1_1 = _mm_tn("ffn1_dw1", f1, dg1)
    dw3_1 = _mm_tn("ffn1_dw3", f1, dg3)
    begun = reduce_begin("ffn1", {("ffn_w1", 1): dw1_1, ("ffn_w3", 1): dw3_1, ("ffn_w2", 1): dw2_1})
    dh3, dnf1, db_pw2 = _mm_nt_normbwd("ffn1_bwd_in", [(dg1, w1, 1), (dg3, w3, 1)], h3, nf[1:2], dh4, begun)
    sent = reduce_send("ffn1", dh3)

    dw_pw2 = _mm_tn("conv_dw_pw2", act, dh3, after=sent)
    dact = _mm_nt("conv_bwd_out", dh3, p["conv_w_pw2"], F32)
    da, d_ln_g, d_ln_b, d_b_dw, d_w_dw, d_b_pw1 = _conv_bwd(dact, c, a, p["conv_w_dw"], p["conv_ln_g"], p["conv_ln_b"])
    dw_pw1 = _mm_tn("conv_dw_pw1", y1, da, col_chunks=N_CHIPS)
    begun = reduce_begin("conv", {("conv_w_pw2", 0): dw_pw2.reshape(N_CHIPS, -1, D), ("conv_w_pw1", 0): dw_pw1})
    dh2, dnm1, _ = _mm_nt_normbwd("conv_bwd_in", [(da, p["conv_w_pw1"], None)], h2, nm[1:2], dh3, begun)
    sent = reduce_send("conv", dh2)

    dg1, dg3 = _ffn_bwd_act("ffn0_bwd_act", dh2, w2, 0, act0, gg0)
    dw2_0 = _mm_tn("ffn0_dw2", s0, dh2, after=sent)
    dw1_0 = _mm_tn("ffn0_dw1", f0, dg1)
    dw3_0 = _mm_tn("ffn0_dw3", f0, dg3)
    begun = reduce_begin("ffn0", {("ffn_w1", 0): dw1_0, ("ffn_w3", 0): dw3_0, ("ffn_w2", 0): dw2_0})
    dh1, dnf0, db_o = _mm_nt_normbwd("ffn0_bwd_in", [(dg1, w1, 0), (dg3, w3, 0)], h1, nf[0:1], dh2, begun)
    sent = reduce_send("ffn0", dh1)

    dw_o = _mm_tn("attn_dw_o", ao, dh1, after=sent)
    dao = _mm_nt("attn_bwd_out", dh1, p["attn_w_o"], BF16)
    dq, dkv, d_sinks, dbq, dbkv = _attn_bwd(qkv, dao, cos, sin, attn_probs, sink_probs)
    dwq = _mm_tn("attn_dw_q", y0, dq)
    dwkv = _mm_tn("attn_dw_kv", y0, dkv)
    wqkv = p["attn_w_qkv"]
    n_qkv = wqkv.shape[1] // N_CHIPS
    dwqkv = jnp.moveaxis(jnp.concatenate([dwq, dwkv], axis=1).reshape(D, N_CHIPS, n_qkv), 1, 0)
    begun = reduce_begin("attn", {("attn_w_o", 0): dw_o.reshape(N_CHIPS, -1, D), ("attn_w_qkv", 0): dwqkv})
    dx, dnm0, _ = _mm_nt_normbwd("attn_bwd_in", [(dq, wqkv[:, :qw], None), (dkv, wqkv[:, qw:], None)], x, nm[0:1], dh1,
                                 begun)
    reduce_send("attn", dx)

    g["norm_mix"] = jnp.concatenate([dnm0, dnm1], axis=0)
    g["norm_ffn"] = jnp.concatenate([dnf0, dnf1], axis=0)
    g["attn_b_qkv"] = jnp.concatenate([dbq, dbkv], axis=1)
    g["attn_sinks"] = d_sinks
    g["attn_b_o"] = db_o
    g["conv_b_pw1"] = d_b_pw1
    g["conv_w_dw"] = d_w_dw[:CONV_WIDTH]
    g["conv_b_dw"] = d_b_dw
    g["conv_ln_g"] = d_ln_g
    g["conv_ln_b"] = d_ln_b
    g["conv_b_pw2"] = db_pw2
    g["norm_final"] = d_norm_final
    return loss, dx, g


ANY = pl.BlockSpec(memory_space=pl.ANY)
VMEM_WHOLE = pl.BlockSpec(memory_space=pltpu.VMEM)


def _my_place():
    return lax.axis_index("x"), lax.axis_index("y"), lax.axis_index("c")


def _other_chips(x, y):
    places = [(1 - x, y), (x, 1 - y), (1 - x, 1 - y)]
    return [(bx, by, 2 * bx + by) for bx, by in places]


def _exchange_small(name, v, reduce):
    r, w = v.shape

    def body(v_ref, o_ref, *rest):
        if reduce:
            buf, send_sems, recv_sems = rest
        else:
            buf = o_ref
            send_sems, recv_sems = rest
        x, y, c = _my_place()
        me = 4 * x + 2 * y + c
        sends = []
        for k in range(1, N_DEV):
            peer = (1 - x if k & 4 else x, 1 - y if k & 2 else y, 1 - c if k & 1 else c)
            cp = pltpu.make_async_remote_copy(
                src_ref=v_ref, dst_ref=buf.at[me], send_sem=send_sems.at[k - 1], recv_sem=recv_sems.at[k - 1],
                device_id=peer, device_id_type=MESH)
            cp.start()
            sends.append(cp)
        buf[me] = v_ref[...]
        for k in range(1, N_DEV):
            src = 4 * (1 - x if k & 4 else x) + 2 * (1 - y if k & 2 else y) + (1 - c if k & 1 else c)
            pltpu.make_async_remote_copy(
                src_ref=v_ref, dst_ref=buf.at[src], send_sem=send_sems.at[k - 1], recv_sem=recv_sems.at[k - 1],
                device_id=(x, y, c), device_id_type=MESH).wait_recv()
        for cp in sends:
            cp.wait_send()
        if reduce:
            acc = buf[0]
            for d in range(1, N_DEV):
                acc = acc + buf[d]
            o_ref[...] = acc

    sems = [pltpu.SemaphoreType.DMA((N_DEV - 1,)), pltpu.SemaphoreType.DMA((N_DEV - 1,))]
    if reduce:
        out_shape = SDS((r, w), F32)
        scratch = [pltpu.VMEM((N_DEV, r, w), F32)] + sems
    else:
        out_shape = SDS((N_DEV, r, w), F32)
        scratch = sems
    return pl.pallas_call(
        body, name=name, out_shape=out_shape, in_specs=[VMEM_WHOLE], out_specs=VMEM_WHOLE,
        scratch_shapes=scratch,
        compiler_params=pltpu.CompilerParams(vmem_limit_bytes=V7X_VMEM_LIMIT_BYTES),
    )(v)


def _cast_into_slot(name, gathered, shard, chip_idx):
    rows, cols = shard.shape
    tr = _pack_row_tile(rows)

    def body(k_ref, s_ref, g_ref, o_ref):
        o_ref[...] = s_ref[...].astype(BF16)

    return pl.pallas_call(
        body, name=name,
        grid_spec=pltpu.PrefetchScalarGridSpec(
            num_scalar_prefetch=1, grid=(rows // tr,),
            in_specs=[pl.BlockSpec((tr, cols), lambda i, k_ref: (i, 0)), pl.BlockSpec(memory_space=pl.ANY)],
            out_specs=pl.BlockSpec((None, tr, cols), lambda i, k_ref: (k_ref[0], i, 0))),
        out_shape=SDS(gathered.shape, BF16),
        input_output_aliases={2: 0},
        compiler_params=_params("parallel"),
    )(chip_idx, shard, gathered)


def _row_halves(ref, c):
    half = ref.shape[1] // 2
    return pl.ds(pl.multiple_of(c * half, 16), half), pl.ds(pl.multiple_of((1 - c) * half, 16), half)


def _gather_ici_copies(refs, send_sems, recv_sems):
    x, y, c = _my_place()
    k = 2 * x + y
    pairs = []
    for i, ref in enumerate(refs):
        mine, _ = _row_halves(ref, c)
        for j, (bx, by, kb) in enumerate(_other_chips(x, y)):
            sems = dict(send_sem=send_sems.at[3 * i + j], recv_sem=recv_sems.at[3 * i + j], device_id_type=MESH)
            send = pltpu.make_async_remote_copy(src_ref=ref.at[k, mine], dst_ref=ref.at[k, mine],
                                                device_id=(bx, by, c), **sems)
            arrival = pltpu.make_async_remote_copy(src_ref=ref.at[kb, mine], dst_ref=ref.at[kb, mine],
                                                   device_id=(bx, by, c), **sems)
            pairs.append((send, arrival))
    return pairs


def _gather_d2d_copies(refs, send_sems, recv_sems, first_sem):
    x, y, c = _my_place()
    pairs = []
    for i, ref in enumerate(refs):
        mine, theirs = _row_halves(ref, c)
        for j, (_, _, kb) in enumerate(_other_chips(x, y)):
            sem = first_sem + 3 * i + j
            sems = dict(send_sem=send_sems.at[sem], recv_sem=recv_sems.at[sem], device_id=(x, y, 1 - c),
                        device_id_type=MESH)
            send = pltpu.make_async_remote_copy(src_ref=ref.at[kb, mine], dst_ref=ref.at[kb, mine], **sems)
            arrival = pltpu.make_async_remote_copy(src_ref=ref.at[kb, theirs], dst_ref=ref.at[kb, theirs], **sems)
            pairs.append((send, arrival))
    return pairs


def _run_copies(pairs):
    for send, _ in pairs:
        send.start()
    for send, arrival in pairs:
        send.wait_send()
        arrival.wait_recv()


def _gather_now(name, gathered):
    n_w = len(gathered)

    def body(*refs):
        in_refs = refs[:n_w]
        send_sems, recv_sems = refs[2 * n_w:]
        _run_copies(_gather_ici_copies(in_refs, send_sems, recv_sems))
        _run_copies(_gather_d2d_copies(in_refs, send_sems, recv_sems, 3 * n_w))

    return pl.pallas_call(
        body, name=name, out_shape=[SDS(g.shape, g.dtype) for g in gathered],
        in_specs=[ANY] * n_w, out_specs=[ANY] * n_w, input_output_aliases={i: i for i in range(n_w)},
        scratch_shapes=[pltpu.SemaphoreType.DMA((6 * n_w,)), pltpu.SemaphoreType.DMA((6 * n_w,))],
    )(*gathered)


def _gather_start(name, gathered):
    n_w = len(gathered)

    def body(*refs):
        in_refs = refs[:n_w]
        send_sems, recv_sems = refs[n_w:n_w + 2]
        for send, _ in _gather_ici_copies(in_refs, send_sems, recv_sems):
            send.start()
        refs[-1][...] = jnp.zeros_like(refs[-1])

    out = pl.pallas_call(
        body, name=name,
        out_shape=(pltpu.SemaphoreType.DMA((3 * n_w,)), pltpu.SemaphoreType.DMA((3 * n_w,)),
                   *[pltpu.HBM(g.shape, g.dtype) for g in gathered], SDS((8, V7X_LANES), F32)),
        in_specs=[HBM_SPEC] * n_w, out_specs=(SEM_SPEC, SEM_SPEC, *[HBM_SPEC] * n_w, VMEM_WHOLE),
        input_output_aliases={i: 2 + i for i in range(n_w)},
        compiler_params=pltpu.CompilerParams(has_side_effects=DATAFLOW),
    )(*[pltpu.with_memory_space_constraint(g, pltpu.HBM) for g in gathered])
    return out[0], out[1], list(out[2:2 + n_w]), out[-1]


def _gather_wait(name, send_sems, recv_sems, gathered, after):
    n_w = len(gathered)

    def body(*refs):
        in_refs = refs[:n_w]
        send_sems, recv_sems = refs[n_w:n_w + 2]
        for send, arrival in _gather_ici_copies(in_refs, send_sems, recv_sems):
            send.wait_send()
            arrival.wait_recv()

    out = pl.pallas_call(
        body, name=name, out_shape=tuple(pltpu.HBM(g.shape, g.dtype) for g in gathered),
        in_specs=[*[HBM_SPEC] * n_w, SEM_SPEC, SEM_SPEC, ANY], out_specs=tuple([HBM_SPEC] * n_w),
        input_output_aliases={i: i for i in range(n_w)},
        compiler_params=pltpu.CompilerParams(has_side_effects=DATAFLOW),
    )(*gathered, send_sems, recv_sems, after)
    return list(out)


def _swap_fetched_with_sibling(name, gathered):
    n_w = len(gathered)

    def body(*refs):
        in_refs = refs[:n_w]
        send_sems, recv_sems = refs[2 * n_w:]
        _run_copies(_gather_d2d_copies(in_refs, send_sems, recv_sems, 0))

    return pl.pallas_call(
        body, name=name, out_shape=[SDS(g.shape, g.dtype) for g in gathered],
        in_specs=[ANY] * n_w, out_specs=[ANY] * n_w, input_output_aliases={i: i for i in range(n_w)},
        scratch_shapes=[pltpu.SemaphoreType.DMA((3 * n_w,)), pltpu.SemaphoreType.DMA((3 * n_w,))],
    )(*gathered)


def _sibling_swap_copies(g_refs, land_refs, send_sems, recv_sems):
    x, y, c = _my_place()
    copies = []
    for i, g_ref in enumerate(g_refs):
        half = g_ref.shape[1] // 2
        theirs = pl.ds(pl.multiple_of((1 - c) * half, 8), half)
        copies.append(pltpu.make_async_remote_copy(
            src_ref=g_ref.at[:, theirs], dst_ref=land_refs[i], send_sem=send_sems.at[i], recv_sem=recv_sems.at[i],
            device_id=(x, y, 1 - c), device_id_type=MESH))
    return copies


def _sibling_swap_start(name, grads):
    n_g = len(grads)

    def body(*refs):
        g_refs, land_refs = refs[:n_g], refs[n_g:2 * n_g]
        send_sems, recv_sems = refs[2 * n_g:2 * n_g + 2]
        for cp in _sibling_swap_copies(g_refs, land_refs, send_sems, recv_sems):
            cp.start()
        refs[-1][...] = jnp.zeros_like(refs[-1])

    lands = [pltpu.with_memory_space_constraint(lax.empty((g.shape[0], g.shape[1] // 2, g.shape[2]), g.dtype),
                                                pltpu.HBM) for g in grads]
    out = pl.pallas_call(
        body, name=name,
        out_shape=(pltpu.SemaphoreType.DMA((n_g,)), pltpu.SemaphoreType.DMA((n_g,)),
                   *[pltpu.HBM(g.shape, g.dtype) for g in grads], *[pltpu.HBM(l.shape, l.dtype) for l in lands],
                   SDS((8, V7X_LANES), F32)),
        in_specs=[HBM_SPEC] * (2 * n_g), out_specs=(SEM_SPEC, SEM_SPEC, *[HBM_SPEC] * (2 * n_g), VMEM_WHOLE),
        input_output_aliases={i: 2 + i for i in range(2 * n_g)},
        compiler_params=pltpu.CompilerParams(has_side_effects=DATAFLOW),
    )(*[pltpu.with_memory_space_constraint(g, pltpu.HBM) for g in grads], *lands)
    return out[0], out[1], list(out[2:2 + n_g]), list(out[2 + n_g:2 + 2 * n_g]), out[-1]


def _sibling_swap_wait(name, send_sems, recv_sems, grads, lands, after):
    n_g = len(grads)

    def body(*refs):
        g_refs, land_refs = refs[:n_g], refs[n_g:2 * n_g]
        send_sems, recv_sems = refs[2 * n_g:2 * n_g + 2]
        for cp in _sibling_swap_copies(g_refs, land_refs, send_sems, recv_sems):
            cp.wait_send()
            cp.wait_recv()

    out = pl.pallas_call(
        body, name=name,
        out_shape=(*[pltpu.HBM(g.shape, g.dtype) for g in grads], *[pltpu.HBM(l.shape, l.dtype) for l in lands]),
        in_specs=[*[HBM_SPEC] * (2 * n_g), SEM_SPEC, SEM_SPEC, ANY], out_specs=tuple([HBM_SPEC] * (2 * n_g)),
        input_output_aliases={i: i for i in range(2 * n_g)},
        compiler_params=pltpu.CompilerParams(has_side_effects=DATAFLOW),
    )(*grads, *lands, send_sems, recv_sems, after)
    return list(out[:n_g]), list(out[n_g:])


def _pack_row_tile(rows):
    for t in range(min(rows, 512), 7, -1):
        if rows % t == 0 and t % 8 == 0:
            return t
    return rows


def _add_sibling_half(name, grads, from_sibling, c_idx):
    n, R, w = grads.shape
    half = R // 2
    tr = _pack_row_tile(half)
    steps = half // tr

    def body(c_ref, g_ref, s_ref, o_ref):
        o_ref[...] = (g_ref[...] + s_ref[...]).astype(BF16)

    return pl.pallas_call(
        body, name=name,
        grid_spec=pltpu.PrefetchScalarGridSpec(
            num_scalar_prefetch=1, grid=(n, steps),
            in_specs=[pl.BlockSpec((1, tr, w), lambda j, i, c_ref: (j, c_ref[0] * steps + i, 0)),
                      pl.BlockSpec((1, tr, w), lambda j, i, c_ref: (j, i, 0))],
            out_specs=pl.BlockSpec((1, tr, w), lambda j, i, c_ref: (j, i, 0))),
        out_shape=SDS((n, half, w), BF16),
        compiler_params=_params("parallel", "parallel"),
    )(c_idx, grads, from_sibling)


HBM_SPEC = pl.BlockSpec(memory_space=pltpu.HBM)
SEM_SPEC = pl.BlockSpec(memory_space=pltpu.SEMAPHORE)
DATAFLOW = pltpu.SideEffectType.DATAFLOW_SIDE_EFFECTING


def _chip_scatter_copies(p_refs, land_refs, send_sems, recv_sems):
    x, y, c = _my_place()
    return [pltpu.make_async_remote_copy(
        src_ref=p_refs[i].at[kb], dst_ref=land_refs[i].at[j], send_sem=send_sems.at[3 * i + j],
        recv_sem=recv_sems.at[3 * i + j], device_id=(bx, by, c), device_id_type=MESH)
        for i in range(len(p_refs)) for j, (bx, by, kb) in enumerate(_other_chips(x, y))]


def _scatter_start(name, partials):
    n_p = len(partials)

    def body(*refs):
        p_refs, land_refs = refs[:n_p], refs[n_p:2 * n_p]
        send_sems, recv_sems = refs[2 * n_p:2 * n_p + 2]
        for cp in _chip_scatter_copies(p_refs, land_refs, send_sems, recv_sems):
            cp.start()
        refs[-1][...] = jnp.zeros_like(refs[-1])

    lands = [pltpu.with_memory_space_constraint(lax.empty((N_CHIPS - 1,) + p.shape[1:], p.dtype), pltpu.HBM)
             for p in partials]
    out = pl.pallas_call(
        body, name=name,
        out_shape=(pltpu.SemaphoreType.DMA((3 * n_p,)), pltpu.SemaphoreType.DMA((3 * n_p,)),
                   *[pltpu.HBM(p.shape, p.dtype) for p in partials], *[pltpu.HBM(l.shape, l.dtype) for l in lands],
                   SDS((8, V7X_LANES), F32)),
        in_specs=[HBM_SPEC] * (2 * n_p), out_specs=(SEM_SPEC, SEM_SPEC, *[HBM_SPEC] * (2 * n_p), VMEM_WHOLE),
        input_output_aliases={i: 2 + i for i in range(2 * n_p)},
        compiler_params=pltpu.CompilerParams(has_side_effects=DATAFLOW),
    )(*[pltpu.with_memory_space_constraint(p, pltpu.HBM) for p in partials], *lands)
    return out[0], out[1], list(out[2:2 + n_p]), list(out[2 + n_p:2 + 2 * n_p]), out[-1]


def _scatter_wait(name, send_sems, recv_sems, partials, lands, after):
    n_p = len(partials)

    def body(*refs):
        p_refs, land_refs = refs[:n_p], refs[n_p:2 * n_p]
        send_sems, recv_sems = refs[2 * n_p:2 * n_p + 2]
        for cp in _chip_scatter_copies(p_refs, land_refs, send_sems, recv_sems):
            cp.wait_send()
            cp.wait_recv()

    out = pl.pallas_call(
        body, name=name,
        out_shape=(*[pltpu.HBM(p.shape, p.dtype) for p in partials], *[pltpu.HBM(l.shape, l.dtype) for l in lands]),
        in_specs=[*[HBM_SPEC] * (2 * n_p), SEM_SPEC, SEM_SPEC, ANY], out_specs=tuple([HBM_SPEC] * (2 * n_p)),
        input_output_aliases={i: i for i in range(2 * n_p)},
        compiler_params=pltpu.CompilerParams(has_side_effects=DATAFLOW),
    )(*partials, *lands, send_sems, recv_sems, after)
    return list(out[:n_p]), list(out[n_p:])


def _sum_chip_partials(name, grads, from_sibling, received, shard, layer, place):
    n, half, w = from_sibling.shape
    tr = _pack_row_tile(half)
    steps = half // tr

    def body(place_ref, g_ref, s_ref, r_ref, shard_ref, o_ref):
        own = g_ref[0] + s_ref[0]
        o_ref[...] = ((own + r_ref[0].astype(F32)) + r_ref[1].astype(F32)) + r_ref[2].astype(F32)

    return pl.pallas_call(
        body, name=name,
        grid_spec=pltpu.PrefetchScalarGridSpec(
            num_scalar_prefetch=1, grid=(steps,),
            in_specs=[pl.BlockSpec((1, tr, w), lambda i, place_ref: (place_ref[0], place_ref[1] * steps + i, 0)),
                      pl.BlockSpec((1, tr, w), lambda i, place_ref: (place_ref[0], i, 0)),
                      pl.BlockSpec((n - 1, tr, w), lambda i, place_ref: (0, i, 0)),
                      pl.BlockSpec(memory_space=pl.ANY)],
            out_specs=pl.BlockSpec((tr, w), lambda i, place_ref: ((2 * layer + place_ref[1]) * steps + i, 0))),
        out_shape=SDS(shard.shape, F32),
        input_output_aliases={4: 0},
        compiler_params=_params("parallel"),
    )(place, grads, from_sibling, received, shard)


def _join_halves(shards, layers):
    n_s = len(shards)
    n_sem = sum(layers)

    def body(*refs):
        in_refs = refs[:n_s]
        send_sems, recv_sems = refs[2 * n_s:]
        x, y, c = _my_place()
        copies, sem = [], 0
        for ref, n_layers in zip(in_refs, layers):
            half = ref.shape[0] // (2 * n_layers)
            for layer in range(n_layers):
                mine = pl.ds(pl.multiple_of(layer * 2 * half + c * half, 8), half)
                theirs = pl.ds(pl.multiple_of(layer * 2 * half + (1 - c) * half, 8), half)
                send = pltpu.make_async_remote_copy(
                    src_ref=ref.at[mine], dst_ref=ref.at[mine], send_sem=send_sems.at[sem], recv_sem=recv_sems.at[sem],
                    device_id=(x, y, 1 - c), device_id_type=MESH)
                send.start()
                arrival = pltpu.make_async_remote_copy(
                    src_ref=ref.at[theirs], dst_ref=ref.at[theirs], send_sem=send_sems.at[sem],
                    recv_sem=recv_sems.at[sem], device_id=(x, y, 1 - c), device_id_type=MESH)
                copies.append((send, arrival))
                sem += 1
        for send, arrival in copies:
            send.wait_send()
            arrival.wait_recv()

    return pl.pallas_call(
        body, name="join_halves", out_shape=[SDS(s.shape, s.dtype) for s in shards],
        in_specs=[ANY] * n_s, out_specs=[ANY] * n_s,
        input_output_aliases={i: i for i in range(n_s)},
        scratch_shapes=[pltpu.SemaphoreType.DMA((n_sem,)), pltpu.SemaphoreType.DMA((n_sem,))],
    )(*shards)


def _adamw(name, w, g, m, v):
    rows, width = w.shape
    tr = _pack_row_tile(rows)

    def body(w_ref, g_ref, m_ref, v_ref, g_out_ref, d_ref, nm_ref, nv_ref):
        gg = g_ref[...]
        g_out_ref[...] = gg
        m_new = ADAM_B1 * m_ref[...] + (1.0 - ADAM_B1) * gg
        v_new = ADAM_B2 * v_ref[...] + (1.0 - ADAM_B2) * (gg * gg)
        m_hat = m_new / (1.0 - ADAM_B1 ** ADAM_STEP)
        v_hat = v_new / (1.0 - ADAM_B2 ** ADAM_STEP)
        d_ref[...] = -ADAM_LR * (m_hat / (jnp.sqrt(v_hat) + ADAM_EPS) + ADAM_WD * w_ref[...])
        nm_ref[...] = m_new
        nv_ref[...] = v_new

    spec = _rows(tr, width)
    return pl.pallas_call(
        body, name=name, grid=(rows // tr,),
        in_specs=[spec] * 4, out_specs=[spec] * 4,
        out_shape=[SDS((rows, width), F32)] * 4,
        compiler_params=_params("parallel"),
    )(w, g, m, v)


WEIGHT_NAMES = ['norm_mix', 'norm_ffn', 'attn_w_qkv', 'attn_b_qkv', 'attn_sinks', 'attn_w_o', 'attn_b_o',
                'conv_w_pw1', 'conv_b_pw1', 'conv_w_dw', 'conv_b_dw', 'conv_ln_g', 'conv_ln_b', 'conv_w_pw2',
                'conv_b_pw2', 'ffn_w1', 'ffn_w3', 'ffn_w2', 'norm_final']
BIG = ['attn_w_qkv', 'attn_w_o', 'conv_w_pw1', 'conv_w_pw2', 'ffn_w1', 'ffn_w3', 'ffn_w2']
COLUMN_SPLIT = ('attn_w_qkv', 'conv_w_pw1', 'ffn_w1', 'ffn_w3')
SMALL_SPLIT = ['conv_b_pw1', 'conv_w_dw', 'conv_b_dw', 'conv_ln_g', 'conv_ln_b', 'conv_b_pw2']
SMALL_WHOLE = ['norm_mix', 'norm_ffn', 'attn_b_qkv', 'attn_sinks', 'attn_b_o', 'norm_final']


def _keeps_rows(shape):
    return len(shape) == 2 and shape[0] > 1 and shape[1] == PACK_W


def _pack_rows(arrays, dtype, row_multiple):
    blocks = [jnp.pad(a.astype(dtype), ((0, -a.shape[0] % V7X_SUBLANES), (0, 0)))
              for a in arrays if _keeps_rows(a.shape)]
    flat = jnp.concatenate([a.astype(dtype).reshape(-1) for a in arrays if not _keeps_rows(a.shape)])
    multiple = max(row_multiple, V7X_SUBLANES)
    rows = -(-(-(-flat.shape[0] // PACK_W)) // multiple) * multiple
    blocks.append(jnp.pad(flat, (0, rows * PACK_W - flat.shape[0])).reshape(rows, PACK_W))
    return jnp.concatenate(blocks, axis=0) if len(blocks) > 1 else blocks[0]


def _unpack_rows(pack, shapes):
    out, row = {}, 0
    for i, shape in enumerate(shapes):
        if _keeps_rows(shape):
            out[i] = pack[row:row + shape[0]]
            row += -(-shape[0] // V7X_SUBLANES) * V7X_SUBLANES
    flat, at = pack[row:].reshape(-1), 0
    for i, shape in enumerate(shapes):
        if not _keeps_rows(shape):
            size = 1
            for s in shape:
                size *= s
            out[i] = flat[at:at + size].reshape(shape)
            at += size
    return [out[i] for i in range(len(shapes))]


def _join_chip_axis(name, parts):
    axis = parts.ndim - 1 if name in COLUMN_SPLIT or name in SMALL_SPLIT else parts.ndim - 2
    moved = jnp.moveaxis(parts, 0, axis - 1)
    shape = list(moved.shape)
    shape[axis - 1:axis + 1] = [shape[axis - 1] * shape[axis]]
    return moved.reshape(shape)


def _split_chip_axis(name, whole, shard_shape):
    axis = len(shard_shape) - 1 if name in COLUMN_SPLIT or name in SMALL_SPLIT else len(shard_shape) - 2
    shape = list(whole.shape)
    shape[axis:axis + 1] = [N_CHIPS, shard_shape[axis]]
    return jnp.moveaxis(whole.reshape(shape), axis, 0)


def kernel(x, norm_mix, norm_ffn, attn_w_qkv, attn_b_qkv, attn_sinks, attn_w_o, attn_b_o, conv_w_pw1, conv_b_pw1, conv_w_dw, conv_b_dw, conv_ln_g, conv_ln_b, conv_w_pw2, conv_b_pw2, ffn_w1, ffn_w3, ffn_w2, norm_final, loss_target, m_norm_mix, m_norm_ffn, m_attn_w_qkv, m_attn_b_qkv, m_attn_sinks, m_attn_w_o, m_attn_b_o, m_conv_w_pw1, m_conv_b_pw1, m_conv_w_dw, m_conv_b_dw, m_conv_ln_g, m_conv_ln_b, m_conv_w_pw2, m_conv_b_pw2, m_ffn_w1, m_ffn_w3, m_ffn_w2, m_norm_final, v_norm_mix, v_norm_ffn, v_attn_w_qkv, v_attn_b_qkv, v_attn_sinks, v_attn_w_o, v_attn_b_o, v_conv_w_pw1, v_conv_b_pw1, v_conv_w_dw, v_conv_b_dw, v_conv_ln_g, v_conv_ln_b, v_conv_w_pw2, v_conv_b_pw2, v_ffn_w1, v_ffn_w3, v_ffn_w2, v_norm_final):
    w = dict(zip(WEIGHT_NAMES, (norm_mix, norm_ffn, attn_w_qkv, attn_b_qkv, attn_sinks, attn_w_o, attn_b_o,
                                conv_w_pw1, conv_b_pw1, conv_w_dw, conv_b_dw, conv_ln_g, conv_ln_b, conv_w_pw2,
                                conv_b_pw2, ffn_w1, ffn_w3, ffn_w2, norm_final)))
    m = dict(zip(WEIGHT_NAMES, (m_norm_mix, m_norm_ffn, m_attn_w_qkv, m_attn_b_qkv, m_attn_sinks, m_attn_w_o,
                                m_attn_b_o, m_conv_w_pw1, m_conv_b_pw1, m_conv_w_dw, m_conv_b_dw, m_conv_ln_g,
                                m_conv_ln_b, m_conv_w_pw2, m_conv_b_pw2, m_ffn_w1, m_ffn_w3, m_ffn_w2, m_norm_final)))
    v = dict(zip(WEIGHT_NAMES, (v_norm_mix, v_norm_ffn, v_attn_w_qkv, v_attn_b_qkv, v_attn_sinks, v_attn_w_o,
                                v_attn_b_o, v_conv_w_pw1, v_conv_b_pw1, v_conv_w_dw, v_conv_b_dw, v_conv_ln_g,
                                v_conv_ln_b, v_conv_w_pw2, v_conv_b_pw2, v_ffn_w1, v_ffn_w3, v_ffn_w2, v_norm_final)))
    T, D = x.shape[1], x.shape[2]
    c_idx = lax.axis_index("c").astype(jnp.int32).reshape(1)
    chip = (2 * lax.axis_index("x") + lax.axis_index("y")).astype(jnp.int32)

    as_rows = lambda a: a.reshape(-1, a.shape[-1])
    slabs = {n: _cast_into_slot(f"cast_{n}", lax.empty((N_CHIPS,) + as_rows(w[n]).shape, BF16), as_rows(w[n]),
                                chip.reshape(1)) for n in BIG}
    first, later = BIG[:2], BIG[2:]
    qkv_parts, w_o_parts = _gather_now("gather_attn", [slabs[n] for n in first])
    send_sems, recv_sems, travelling, gather_started = _gather_start("gather_start", [slabs[n] for n in later])
    layers = ffn_w1.shape[0]
    small_shapes = [w[n].shape for n in SMALL_SPLIT]
    small_all = _exchange_small("gather_small", _pack_rows([w[n] for n in SMALL_SPLIT], F32, 8), reduce=False)
    per_chip = [_unpack_rows(small_all[2 * j], small_shapes) for j in range(N_CHIPS)]
    full = {}
    for i, n in enumerate(SMALL_SPLIT):
        full[n] = _join_chip_axis(n, jnp.stack([per_chip[j][i] for j in range(N_CHIPS)]))

    def other_weights(after):
        landed = _gather_wait("gather_wait", send_sems, recv_sems, travelling, after)
        gathered = dict(zip(later, _swap_fetched_with_sibling("gather_swap", landed)))
        return {"conv_w_pw1": gathered["conv_w_pw1"], "conv_w_pw2": gathered["conv_w_pw2"].reshape(-1, D),
                "ffn_w1": gathered["ffn_w1"].reshape(N_CHIPS, layers, D, -1),
                "ffn_w3": gathered["ffn_w3"].reshape(N_CHIPS, layers, D, -1),
                "ffn_w2": gathered["ffn_w2"].reshape(N_CHIPS, layers, -1, D)}

    p = {
        "norm_mix": norm_mix, "norm_ffn": norm_ffn, "norm_final": norm_final.reshape(1, D),
        "attn_w_qkv": jnp.moveaxis(qkv_parts, 0, 1).reshape(D, -1), "attn_b_qkv": attn_b_qkv,
        "attn_sinks": attn_sinks, "attn_w_o": w_o_parts.reshape(-1, D), "attn_b_o": attn_b_o,
        "conv_b_pw1": full["conv_b_pw1"], "conv_w_dw": full["conv_w_dw"][0],
        "conv_b_dw": full["conv_b_dw"], "conv_ln_g": full["conv_ln_g"], "conv_ln_b": full["conv_ln_b"],
        "conv_b_pw2": full["conv_b_pw2"], "gather_started": gather_started, "other_weights": other_weights,
    }
    swapping, in_flight = {}, []

    def reduce_begin(tag, grads):
        keys = list(grads)
        *handles, begun = _sibling_swap_start(f"sibling_swap_start_{tag}", [grads[k] for k in keys])
        swapping[tag] = (keys, handles)
        return begun

    def reduce_send(tag, after):
        keys, (swap_send, swap_recv, grads, lands) = swapping[tag]
        grads, from_sibling = _sibling_swap_wait(f"sibling_swap_wait_{tag}", swap_send, swap_recv, grads, lands, after)
        partials = [_add_sibling_half(f"add_sibling_half_{tag}{i}", gr, fs, c_idx)
                    for i, (gr, fs) in enumerate(zip(grads, from_sibling))]
        *handles, sent = _scatter_start(f"scatter_start_{tag}", partials)
        in_flight.append((tag, keys, handles, grads, from_sibling))
        return sent

    loss_part, dx, g = _local_step(x[0], loss_target[0], p, reduce_begin, reduce_send)
    for n in SMALL_WHOLE + SMALL_SPLIT:
        g[n] = g[n].reshape((-1,) + g[n].shape[-2:]) if w[n].ndim == 3 else g[n].reshape(w[n].shape[:-1] + (-1,))

    place = jnp.stack([chip, c_idx[0]])
    shard_grad = {n: lax.empty(as_rows(w[n]).shape, F32) for n in BIG}
    for tag, keys, (send_sems, recv_sems, partials, lands), grads, from_sibling in in_flight:
        _, received = _scatter_wait(f"scatter_wait_{tag}", send_sems, recv_sems, partials, lands, dx)
        for i, (n, layer) in enumerate(keys):
            shard_grad[n] = _sum_chip_partials(f"sum_chip_partials_{tag}{i}", grads[i], from_sibling[i], received[i],
                                               shard_grad[n], layer, place)
    g_big = dict(zip(BIG, _join_halves([shard_grad[n] for n in BIG], [w[n].shape[0] for n in BIG])))
    big_out = {}
    for n in BIG:
        step = _adamw(f"adamw_{n}", as_rows(w[n]), g_big[n], as_rows(m[n]), as_rows(v[n]))
        big_out[n] = [a.reshape(w[n].shape) for a in step]

    small_whole_shapes = [w[n].shape for n in SMALL_WHOLE]
    small_full_shapes = [g[n].shape for n in SMALL_SPLIT]
    reduced = _exchange_small(
        "reduce_small", _pack_rows([loss_part] + [g[n] for n in SMALL_WHOLE] + [g[n] for n in SMALL_SPLIT], F32, 8),
        reduce=True)
    pieces = _unpack_rows(reduced, [(1,)] + small_whole_shapes + small_full_shapes)
    loss = pieces[0].reshape(())
    g_small = dict(zip(SMALL_WHOLE, pieces[1:1 + len(SMALL_WHOLE)]))
    for n, whole in zip(SMALL_SPLIT, pieces[1 + len(SMALL_WHOLE):]):
        parts = _split_chip_axis(n, whole, w[n].shape)
        g_small[n] = lax.dynamic_index_in_dim(parts, chip, axis=0, keepdims=False)
    small = SMALL_WHOLE + SMALL_SPLIT
    _, d_small, m_small, v_small = _adamw(
        "adamw_small", _pack_rows([w[n] for n in small], F32, 8), _pack_rows([g_small[n] for n in small], F32, 8),
        _pack_rows([m[n] for n in small], F32, 8), _pack_rows([v[n] for n in small], F32, 8))

    outs = {}
    for slot, (tag, small_pack) in enumerate((("g", None), ("d", d_small), ("m", m_small), ("v", v_small))):
        vals = {n: big_out[n][slot] for n in BIG}
        if small_pack is None:
            vals.update(g_small)
        else:
            vals.update(zip(small, _unpack_rows(small_pack, [w[n].shape for n in small])))
        outs[tag] = vals
    return (loss, dx.reshape(1, T, D), *[outs["g"][n] for n in WEIGHT_NAMES], *[outs["d"][n] for n in WEIGHT_NAMES],
            *[outs["m"][n] for n in WEIGHT_NAMES], *[outs["v"][n] for n in WEIGHT_NAMES])
```

```python
import functools

import jax
import jax.numpy as jnp
from jax import lax
from jax.experimental import pallas as pl
from jax.experimental.pallas import tpu as pltpu

F32 = jnp.float32
BF16 = jnp.bfloat16
SDS = jax.ShapeDtypeStruct
MESH = pl.DeviceIdType.MESH

HEAD_DIM = 64
N_Q_HEADS = 16
N_KV_HEADS = 2
Q_PER_KV = N_Q_HEADS // N_KV_HEADS
ATTN_BLOCK = 128
ROPE_THETA = 10000.0
CONV_WIDTH = 31
CONV_HALO = 32
CONV_FIRST_TAP = CONV_HALO - CONV_WIDTH + 1
CONV_ROW_CHUNK = 64
CONV_LANE_CHUNK = 256
CONV_GRAD_UNROLL = 4
RMS_EPS = 1e-5
LN_EPS = 1e-5
ADAM_LR = 0.001
ADAM_B1 = 0.9
ADAM_B2 = 0.999
ADAM_EPS = 1e-08
ADAM_WD = 0.01
ADAM_STEP = 10

V7X_LANES = 128
V7X_SUBLANES = 8
V7X_VMEM_LIMIT_BYTES = 56 * 1024 * 1024

N_CHIPS = 4
N_DEV = 8
PACK_W = 1024

MASK_VALUE = -1e30


def _params(*semantics):
    return pltpu.CompilerParams(dimension_semantics=semantics, vmem_limit_bytes=V7X_VMEM_LIMIT_BYTES)


def _rows(tm, width):
    return pl.BlockSpec((tm, width), lambda i: (i, 0))


def _whole(shape):
    return pl.BlockSpec(shape, lambda *_: (0,) * len(shape))


def _rms_rstd(h):
    return lax.rsqrt(jnp.mean(h * h, axis=-1, keepdims=True) + RMS_EPS)


def _silu_and_grad(z):
    sg = jax.nn.sigmoid(z)
    return z * sg, sg * (1.0 + z * (1.0 - sg))


def _swap_rope_halves(t):
    w = t.shape[1]
    half = HEAD_DIM // 2
    lane = lax.broadcasted_iota(jnp.int32, t.shape, 1)
    upper = pltpu.roll(t, w - half, 1)
    lower = pltpu.roll(t, half, 1)
    return jnp.where(lane % HEAD_DIM < half, upper, lower)


def _rope(t, cos_ref, sin_ref):
    reps = t.shape[1] // V7X_LANES
    c = jnp.tile(cos_ref[...], (1, reps))
    s = jnp.tile(sin_ref[...], (1, reps))
    return t * c + _swap_rope_halves(t) * s


def _rope_transposed(dt, cos_ref, sin_ref):
    reps = dt.shape[1] // V7X_LANES
    c = jnp.tile(cos_ref[...], (1, reps))
    s = jnp.tile(sin_ref[...], (1, reps))
    return dt * c + _swap_rope_halves(dt * s)


def _rope_tables(seq_len):
    pos = jnp.arange(seq_len, dtype=F32)
    inv_freq = ROPE_THETA ** (-jnp.arange(0, HEAD_DIM, 2, dtype=F32) / HEAD_DIM)
    ang = pos[:, None] * jnp.tile(inv_freq, 2 * V7X_LANES // HEAD_DIM)[None, :]
    upper_half = jnp.arange(V7X_LANES) % HEAD_DIM >= HEAD_DIM // 2
    return jnp.cos(ang), jnp.where(upper_half[None, :], jnp.sin(ang), -jnp.sin(ang))


def _qkv_proj(h, g, w, b, cos, sin, after):
    T, D = h.shape
    N = w.shape[0]
    tm = min(512, T)
    rope_w = N - N_KV_HEADS * HEAD_DIM

    def body(h_ref, g_ref, w_ref, b_ref, cos_ref, sin_ref, _, y_ref, o_ref):
        hh = h_ref[...]
        y = (hh * _rms_rstd(hh) * g_ref[...]).astype(BF16)
        y_ref[...] = y
        acc = _dot_nt(y, w_ref[...]) + b_ref[...]
        o_ref[:, :rope_w] = _rope(acc[:, :rope_w], cos_ref, sin_ref).astype(BF16)
        o_ref[:, rope_w:] = acc[:, rope_w:].astype(BF16)

    return pl.pallas_call(
        body, name="qkv_proj", grid=(T // tm,),
        in_specs=[_rows(tm, D), _whole((1, D)), _whole((N, D)), _whole((1, N)),
                  _rows(tm, V7X_LANES), _rows(tm, V7X_LANES), pl.BlockSpec(memory_space=pl.ANY)],
        out_specs=[_rows(tm, D), _rows(tm, N)],
        out_shape=[SDS((T, D), BF16), SDS((T, N), BF16)],
        compiler_params=_params("parallel"),
    )(h, g, w, b, cos, sin, after)


def _pw1_proj(h, g, w, b):
    T, D = h.shape
    n = w.shape[2]
    N = N_CHIPS * n
    tm = min(512, T)

    def body(h_ref, g_ref, w_ref, b_ref, y_ref, o_ref):
        hh = h_ref[...]
        y = (hh * _rms_rstd(hh) * g_ref[...]).astype(BF16)
        y_ref[...] = y
        for j in range(N_CHIPS):
            cols = slice(j * n, (j + 1) * n)
            o_ref[:, cols] = jnp.dot(y, w_ref[j], preferred_element_type=F32) + b_ref[:, cols]

    return pl.pallas_call(
        body, name="pw1_proj", grid=(T // tm,),
        in_specs=[_rows(tm, D), _whole((1, D)), _whole((N_CHIPS, D, n)), _whole((1, N))],
        out_specs=[_rows(tm, D), _rows(tm, N)],
        out_shape=[SDS((T, D), BF16), SDS((T, N), F32)],
        compiler_params=_params("parallel"),
    )(h, g, w, b)


PAIRS_PER_KV = Q_PER_KV // 2


def _upper_lanes(shape):
    return lax.broadcasted_iota(jnp.int32, shape, 1) >= HEAD_DIM


def _swap_lane_halves(t):
    return pltpu.roll(t.astype(F32), HEAD_DIM, 1).astype(t.dtype)


def _kv_operands(g, t):
    swapped = _swap_lane_halves(t)
    in_lower, in_upper = (t, swapped) if g == 0 else (swapped, t)
    upper = _upper_lanes(t.shape)
    zero = jnp.zeros_like(t)
    return jnp.where(upper, zero, in_lower), jnp.where(upper, in_upper, zero)


def _group_heads(g):
    pairs = range(g * PAIRS_PER_KV, (g + 1) * PAIRS_PER_KV)
    return [2 * hp for hp in pairs] + [2 * hp + 1 for hp in pairs]


def _all_heads():
    return [h for g in range(N_KV_HEADS) for h in _group_heads(g)]


def _pair_rows(ref, g):
    pairs = range(g * PAIRS_PER_KV, (g + 1) * PAIRS_PER_KV)
    return jnp.concatenate([ref[:, hp * 2 * HEAD_DIM:(hp + 1) * 2 * HEAD_DIM] for hp in pairs], axis=0)


def _from_previous_block(rows):
    row = lax.broadcasted_iota(jnp.int32, (ATTN_BLOCK, ATTN_BLOCK), 0)
    col = lax.broadcasted_iota(jnp.int32, (ATTN_BLOCK, ATTN_BLOCK), 1)
    return jnp.concatenate([col > row] * (rows // ATTN_BLOCK), axis=0)


def _folded_probs(n, q_groups, k_prev_groups, k_cur_groups, sink_ref, prev_part):
    def scores(q, k):
        return lax.dot_general(q, k, (((1,), (1,)), ((), ())), preferred_element_type=F32)

    s_prev = jnp.concatenate([scores(q, k[i]) for q, k in zip(q_groups, k_prev_groups) for i in range(2)], axis=0)
    s_cur = jnp.concatenate([scores(q, k[i]) for q, k in zip(q_groups, k_cur_groups) for i in range(2)], axis=0)
    s_prev = jnp.where(n > 0, s_prev, MASK_VALUE * (HEAD_DIM ** 0.5))
    s = jnp.where(prev_part, s_prev, s_cur) * (HEAD_DIM ** -0.5)
    heads = [h for g in range(N_KV_HEADS) for h in _group_heads(g)]
    sink = jnp.concatenate([jnp.broadcast_to(sink_ref[0:1, h:h + 1], (ATTN_BLOCK, 1)) for h in heads], axis=0)
    m = jnp.maximum(jnp.max(s, axis=1, keepdims=True), sink)
    p = jnp.exp(s - m)
    e_sink = jnp.exp(sink - m)
    inv = 1.0 / (jnp.sum(p, axis=1, keepdims=True) + e_sink)
    return p * inv, e_sink * inv


def _split_folded(t, prev_part):
    tb = t.astype(BF16)
    zero = jnp.zeros_like(tb)
    return jnp.where(prev_part, tb, zero), jnp.where(prev_part, zero, tb)


def _attn_specs(T):
    nb = T // ATTN_BLOCK
    kcol = N_Q_HEADS * HEAD_DIM // V7X_LANES
    cur = lambda n: jnp.minimum(n, nb - 1)
    prev = lambda n: jnp.maximum(jnp.minimum(n, nb - 1) - 1, 0)
    q_spec = pl.BlockSpec((ATTN_BLOCK, N_Q_HEADS * HEAD_DIM), lambda n: (cur(n), 0))
    kc_spec = pl.BlockSpec((ATTN_BLOCK, V7X_LANES), lambda n: (cur(n), kcol))
    kp_spec = pl.BlockSpec((ATTN_BLOCK, V7X_LANES), lambda n: (prev(n), kcol))
    vc_spec = pl.BlockSpec((ATTN_BLOCK, V7X_LANES), lambda n: (cur(n), kcol + 1))
    vp_spec = pl.BlockSpec((ATTN_BLOCK, V7X_LANES), lambda n: (prev(n), kcol + 1))
    return q_spec, kc_spec, kp_spec, vc_spec, vp_spec


def _attn_fwd(qkv, sinks):
    T = qkv.shape[0]
    nb = T // ATTN_BLOCK
    qw = N_Q_HEADS * HEAD_DIM
    all_rows = N_Q_HEADS * ATTN_BLOCK

    def body(q_ref, kc_ref, kp_ref, vc_ref, vp_ref, sink_ref, o_ref, probs_ref, psink_ref):
        n = pl.program_id(0)
        prev_part = _from_previous_block(all_rows)
        half = PAIRS_PER_KV * ATTN_BLOCK
        groups = range(N_KV_HEADS)
        probs, p_sink = _folded_probs(n, [_pair_rows(q_ref, g) for g in groups],
                                      [_kv_operands(g, kp_ref[...]) for g in groups],
                                      [_kv_operands(g, kc_ref[...]) for g in groups], sink_ref, prev_part)
        probs_ref[...] = probs.astype(BF16)
        lane = lax.broadcasted_iota(jnp.int32, (ATTN_BLOCK, V7X_LANES), 1)
        sink_tile = jnp.zeros((ATTN_BLOCK, V7X_LANES), F32)
        for i, h in enumerate(_all_heads()):
            sink_tile = jnp.where(lane == h, p_sink[i * ATTN_BLOCK:(i + 1) * ATTN_BLOCK], sink_tile)
        psink_ref[...] = sink_tile
        p_prev, p_cur = _split_folded(probs, prev_part)
        for g in groups:
            v_prev, v_cur = _kv_operands(g, vp_ref[...]), _kv_operands(g, vc_ref[...])
            even, odd = slice(2 * g * half, (2 * g + 1) * half), slice((2 * g + 1) * half, (2 * g + 2) * half)
            o = (jnp.dot(p_prev[even], v_prev[0], preferred_element_type=F32)
                 + jnp.dot(p_cur[even], v_cur[0], preferred_element_type=F32)
                 + jnp.dot(p_prev[odd], v_prev[1], preferred_element_type=F32)
                 + jnp.dot(p_cur[odd], v_cur[1], preferred_element_type=F32))
            for i in range(PAIRS_PER_KV):
                hp = g * PAIRS_PER_KV + i
                o_ref[:, hp * 2 * HEAD_DIM:(hp + 1) * 2 * HEAD_DIM] = (
                    o[i * ATTN_BLOCK:(i + 1) * ATTN_BLOCK].astype(BF16))

    return pl.pallas_call(
        body, name="attn_fwd", grid=(nb,),
        in_specs=[*_attn_specs(T), _whole((1, N_Q_HEADS))],
        out_specs=[_rows(ATTN_BLOCK, qw), pl.BlockSpec((None, all_rows, ATTN_BLOCK), lambda n: (n, 0, 0)),
                   _rows(ATTN_BLOCK, V7X_LANES)],
        out_shape=[SDS((T, qw), BF16), SDS((nb, all_rows, ATTN_BLOCK), BF16), SDS((T, V7X_LANES), F32)],
        compiler_params=_params("parallel"),
    )(qkv, qkv, qkv, qkv, qkv, sinks)


def _mm_res(name, a, w, b, res, g):
    T, K = a.shape
    D = w.shape[1]
    tm = min(512, T)

    def body(a_ref, w_ref, b_ref, r_ref, g_ref, o_ref, f_ref):
        h = jnp.dot(a_ref[...], w_ref[...], preferred_element_type=F32) + b_ref[...] + r_ref[...]
        o_ref[...] = h
        f_ref[...] = (h * _rms_rstd(h) * g_ref[...]).astype(BF16)

    return pl.pallas_call(
        body, name=name, grid=(T // tm,),
        in_specs=[_rows(tm, K), _whole((K, D)), _whole((1, D)), _rows(tm, D), _whole((1, D))],
        out_specs=[_rows(tm, D), _rows(tm, D)],
        out_shape=[SDS((T, D), F32), SDS((T, D), BF16)],
        compiler_params=_params("parallel"),
    )(a, w, b, res, g)


def _ffn_down(name, s, w2, layer, res):
    _, T, n = s.shape
    D = w2.shape[3]
    tm = min(512, T)

    def body(s_ref, w_ref, r_ref, o_ref):
        acc = r_ref[...]
        for j in range(N_CHIPS):
            acc = acc + jnp.dot(s_ref[j], w_ref[j], preferred_element_type=F32)
        o_ref[...] = acc

    return pl.pallas_call(
        body, name=name, grid=(T // tm,),
        in_specs=[pl.BlockSpec((N_CHIPS, tm, n), lambda i: (0, i, 0)),
                  pl.BlockSpec((N_CHIPS, None, n, D), lambda i: (0, layer, 0, 0)), _rows(tm, D)],
        out_specs=_rows(tm, D),
        out_shape=SDS((T, D), F32),
        compiler_params=_params("parallel"),
    )(s, w2, res)


def _ffn_up(name, f, w1, w3, layer):
    T, D = f.shape
    n = w1.shape[2]
    tm = min(1024, T)

    def body(f_ref, w1_ref, w3_ref, act_ref, gg_ref, s_ref):
        ff = f_ref[...]
        g1 = _dot_nt(ff, w1_ref[...])
        g3 = _dot_nt(ff, w3_ref[...])
        act, dact = _silu_and_grad(g1)
        act_ref[...] = act.astype(BF16)
        gg_ref[...] = (g3 * dact).astype(BF16)
        s_ref[...] = (act * g3).astype(BF16)

    slab = pl.BlockSpec((None, tm, n), lambda j, i: (j, i, 0))
    wslab = pl.BlockSpec((None, None, n, D), lambda j, i: (j, layer, 0, 0))
    hidden = SDS((N_CHIPS, T, n), BF16)
    return pl.pallas_call(
        body, name=name, grid=(N_CHIPS, T // tm),
        in_specs=[pl.BlockSpec((tm, D), lambda j, i: (i, 0)), wslab, wslab],
        out_specs=[slab, slab, slab],
        out_shape=[hidden, hidden, hidden],
        compiler_params=_params("parallel", "parallel"),
    )(f, w1, w3)


def _glu(a, d):
    return a[:, :d] * jax.nn.sigmoid(a[:, d:])


def _conv_tile(T):
    return min(256, T)


def _fill_shifted(sh_ref, tc):
    n = tc + CONV_HALO - V7X_SUBLANES
    for r in range(1, V7X_SUBLANES):
        sh_ref[r, 0:n, :] = sh_ref[0, pl.ds(r, n), :]


def _depthwise_taps(sh_ref, w_ref, offsets, bias_ref, out_ref, tc):
    D = out_ref.shape[1]

    def chunk(i, carry):
        t0 = pl.multiple_of(i * CONV_ROW_CHUNK, CONV_ROW_CHUNK)
        for cb in range(D // CONV_LANE_CHUNK):
            cs = slice(cb * CONV_LANE_CHUNK, (cb + 1) * CONV_LANE_CHUNK)
            acc = jnp.zeros((CONV_ROW_CHUNK, CONV_LANE_CHUNK), F32)
            for r in range(V7X_SUBLANES):
                taps = [(j, o // V7X_SUBLANES) for j, o in enumerate(offsets) if o % V7X_SUBLANES == r]
                if not taps:
                    continue
                span = CONV_ROW_CHUNK + V7X_SUBLANES * max(q for _, q in taps)
                rows = sh_ref[r, pl.ds(t0, span), cs]
                for j, q in taps:
                    acc = acc + rows[V7X_SUBLANES * q:V7X_SUBLANES * q + CONV_ROW_CHUNK] * w_ref[j:j + 1, cs]
            if bias_ref is not None:
                acc = acc + bias_ref[:, cs]
            out_ref[pl.ds(t0, CONV_ROW_CHUNK), cs] = acc
        return carry

    lax.fori_loop(0, tc // CONV_ROW_CHUNK, chunk, 0)


def _depthwise_tap_grads(dy_sh, x_sh, offsets, dw_ref, tc):
    D = dw_ref.shape[1]
    for cb in range(D // V7X_LANES):
        cs = slice(cb * V7X_LANES, (cb + 1) * V7X_LANES)

        def row_tiles(i, accs, cs=cs):
            for k in range(CONV_GRAD_UNROLL):
                t0 = pl.multiple_of(i * (CONV_GRAD_UNROLL * V7X_SUBLANES), V7X_SUBLANES) + k * V7X_SUBLANES
                d = dy_sh[0, pl.ds(t0, V7X_SUBLANES), cs]
                accs = tuple(
                    acc + d * x_sh[o % V7X_SUBLANES, pl.ds(t0 + o // V7X_SUBLANES * V7X_SUBLANES, V7X_SUBLANES), cs]
                    for acc, o in zip(accs, offsets))
            return accs

        zero = jnp.zeros((V7X_SUBLANES, V7X_LANES), F32)
        accs = lax.fori_loop(0, tc // (CONV_GRAD_UNROLL * V7X_SUBLANES), row_tiles, tuple(zero for _ in offsets))
        for j, acc in enumerate(accs):
            dw_ref[j:j + 1, cs] += jnp.sum(acc, axis=0, keepdims=True)


def _conv_fwd(a, w_dw, b_dw, ln_g, ln_b):
    T = a.shape[0]
    D = a.shape[1] // 2
    tc = _conv_tile(T)
    per = tc // CONV_HALO

    def body(a_ref, ah_ref, w_ref, bdw_ref, lg_ref, lb_ref, c_ref, act_ref, u_sh):
        i = pl.program_id(0)
        u_sh[0, 0:CONV_HALO, :] = jnp.where(i > 0, _glu(ah_ref[...], D), 0.0)
        u_sh[0, CONV_HALO:, :] = _glu(a_ref[...], D)
        _fill_shifted(u_sh, tc)
        _depthwise_taps(u_sh, w_ref, [CONV_FIRST_TAP + j for j in range(CONV_WIDTH)], bdw_ref, c_ref, tc)
        c = c_ref[...]
        xc = c - jnp.mean(c, axis=-1, keepdims=True)
        z = xc * lax.rsqrt(jnp.mean(xc * xc, axis=-1, keepdims=True) + LN_EPS)
        l = z * lg_ref[...] + lb_ref[...]
        act_ref[...] = (l * jax.nn.sigmoid(l)).astype(BF16)

    return pl.pallas_call(
        body, name="conv_fwd", grid=(T // tc,),
        in_specs=[_rows(tc, 2 * D),
                  pl.BlockSpec((CONV_HALO, 2 * D), lambda i: (jnp.maximum(i * per - 1, 0), 0)),
                  _whole((CONV_WIDTH, D)), _whole((1, D)), _whole((1, D)), _whole((1, D))],
        out_specs=[_rows(tc, D), _rows(tc, D)],
        out_shape=[SDS((T, D), F32), SDS((T, D), BF16)],
        scratch_shapes=[pltpu.VMEM((V7X_SUBLANES, tc + CONV_HALO, D), F32)],
        compiler_params=_params("parallel"),
    )(a, a, w_dw, b_dw, ln_g, ln_b)


def _final_loss(h, g, target):
    T, D = h.shape
    tm = min(512, T)

    def body(h_ref, g_ref, t_ref, dh_ref, loss_ref, dg_ref):
        @pl.when(pl.program_id(0) == 0)
        def _():
            loss_ref[...] = jnp.zeros_like(loss_ref)
            dg_ref[...] = jnp.zeros_like(dg_ref)

        hh = h_ref[...]
        r = _rms_rstd(hh)
        g = g_ref[...]
        d = hh * r * g - t_ref[...]
        loss_ref[...] += 0.5 * jnp.sum(jnp.mean(d * d, axis=-1, keepdims=True), axis=0, keepdims=True)
        dout = d * (1.0 / D)
        dg_ref[...] += jnp.sum(dout * (hh * r), axis=0, keepdims=True)
        dxh = dout * g
        dh_ref[...] = r * dxh - hh * (r * r * r) * jnp.mean(dxh * hh, axis=-1, keepdims=True)

    return pl.pallas_call(
        body, name="final_loss", grid=(T // tm,),
        in_specs=[_rows(tm, D), _whole((1, D)), _rows(tm, D)],
        out_specs=[_rows(tm, D), _whole((1, 1)), _whole((1, D))],
        out_shape=[SDS((T, D), F32), SDS((1, 1), F32), SDS((1, D), F32)],
        compiler_params=_params("arbitrary"),
    )(h, g, target)


def _ffn_bwd_act(name, dh, w2, layer, act, gate_grad):
    T, D = dh.shape
    n = w2.shape[2]
    tm = min(1024, T)

    def body(dh_ref, w2_ref, act_ref, gg_ref, dg1_ref, dg3_ref):
        ds = lax.dot_general(dh_ref[...].astype(BF16), w2_ref[...], (((1,), (1,)), ((), ())),
                             preferred_element_type=F32)
        dg1_ref[...] = (ds * gg_ref[...].astype(F32)).astype(BF16)
        dg3_ref[...] = (ds * act_ref[...].astype(F32)).astype(BF16)

    slab = pl.BlockSpec((None, tm, n), lambda i, j: (j, i, 0))
    hidden = SDS((N_CHIPS, T, n), BF16)
    return pl.pallas_call(
        body, name=name, grid=(T // tm, N_CHIPS),
        in_specs=[pl.BlockSpec((tm, D), lambda i, j: (i, 0)),
                  pl.BlockSpec((None, None, n, D), lambda i, j: (j, layer, 0, 0)), slab, slab],
        out_specs=[slab, slab],
        out_shape=[hidden, hidden],
        compiler_params=_params("parallel", "arbitrary"),
    )(dh, w2, act, gate_grad)


def _dot_tn(a, b):
    return lax.dot_general(a.astype(BF16), b.astype(BF16), (((0,), (0,)), ((), ())), preferred_element_type=F32)


def _dot_nt(a, b):
    return lax.dot_general(a.astype(BF16), b, (((1,), (1,)), ((), ())), preferred_element_type=F32)


def _mm_tn(name, a, b, col_chunks=1, after=None):
    a_slabs, b_slabs = a.ndim == 3, b.ndim == 3
    T = a.shape[-2]
    tt = min(1024, T)
    ka, nb = a.shape[-1], b.shape[-1]
    if a_slabs or b_slabs:
        out_dims = (N_CHIPS, ka, nb)
    elif col_chunks > 1:
        out_dims = (col_chunks, ka, nb // col_chunks)
    else:
        out_dims = (ka, nb)

    def body(a_ref, b_ref, *rest):
        o_ref = rest[-1]

        @pl.when(pl.program_id(0) == 0)
        def _():
            o_ref[...] = jnp.zeros_like(o_ref)

        if a_slabs:
            bb = b_ref[...].astype(BF16)
            for j in range(N_CHIPS):
                o_ref[j] += _dot_tn(a_ref[j], bb)
        elif b_slabs:
            aa = a_ref[...].astype(BF16)
            for j in range(N_CHIPS):
                o_ref[j] += _dot_tn(aa, b_ref[j])
        elif col_chunks > 1:
            aa = a_ref[...].astype(BF16)
            w = nb // col_chunks
            for j in range(col_chunks):
                o_ref[j] += _dot_tn(aa, b_ref[:, j * w:(j + 1) * w])
        else:
            o_ref[...] += _dot_tn(a_ref[...], b_ref[...])

    def spec(arr, slabs):
        if slabs:
            return pl.BlockSpec((N_CHIPS, tt, arr.shape[-1]), lambda t: (0, t, 0))
        return _rows(tt, arr.shape[-1])

    return pl.pallas_call(
        body, name=name, grid=(T // tt,),
        in_specs=[spec(a, a_slabs), spec(b, b_slabs)] + ([] if after is None else [pl.BlockSpec(memory_space=pl.ANY)]),
        out_specs=_whole(out_dims),
        out_shape=SDS(out_dims, F32),
        compiler_params=_params("arbitrary"),
    )(a, b, *([] if after is None else [after]))


def _mm_nt_normbwd(name, pairs, h, g, dh, after):
    T, D = h.shape
    tm = min(512, T)
    n_pairs = len(pairs)
    kinds = ["slabs" if dy.ndim == 3 else ("quarters" if w.ndim == 3 else "plain") for dy, w, _ in pairs]

    def body(*refs):
        dy_refs = refs[:n_pairs]
        w_refs = refs[n_pairs:2 * n_pairs]
        h_ref, g_ref, dh_ref, _, o_ref, dg_ref, cs_ref = refs[2 * n_pairs:]

        @pl.when(pl.program_id(0) == 0)
        def _():
            dg_ref[...] = jnp.zeros_like(dg_ref)
            cs_ref[...] = jnp.zeros_like(cs_ref)

        df = jnp.zeros((tm, D), F32)
        for dy_ref, w_ref, kd in zip(dy_refs, w_refs, kinds):
            if kd == "slabs":
                for j in range(N_CHIPS):
                    df = df + jnp.dot(dy_ref[j], w_ref[j], preferred_element_type=F32)
            elif kd == "quarters":
                n = w_ref.shape[2]
                for j in range(N_CHIPS):
                    df = df + _dot_nt(dy_ref[:, j * n:(j + 1) * n], w_ref[j])
            else:
                df = df + jnp.dot(dy_ref[...], w_ref[...], preferred_element_type=F32)
        hh = h_ref[...]
        r = _rms_rstd(hh)
        dg_ref[...] += jnp.sum(df * (hh * r), axis=0, keepdims=True)
        dxh = df * g_ref[...]
        out = dh_ref[...] + (r * dxh - hh * (r * r * r) * jnp.mean(dxh * hh, axis=-1, keepdims=True))
        o_ref[...] = out
        cs_ref[...] += jnp.sum(out, axis=0, keepdims=True)

    dy_specs, w_specs = [], []
    for (dy, w, layer), kd in zip(pairs, kinds):
        if kd == "slabs":
            dy_specs.append(pl.BlockSpec((N_CHIPS, tm, dy.shape[2]), lambda i: (0, i, 0)))
            w_specs.append(pl.BlockSpec((N_CHIPS, None, w.shape[2], D),
                                        functools.partial(lambda i, layer: (0, layer, 0, 0), layer=layer),
                                        pipeline_mode=pl.Buffered(1)))
        else:
            dy_specs.append(_rows(tm, dy.shape[1]))
            w_specs.append(_whole(w.shape))

    return pl.pallas_call(
        body, name=name, grid=(T // tm,),
        in_specs=[*dy_specs, *w_specs, _rows(tm, D), _whole((1, D)), _rows(tm, D), pl.BlockSpec(memory_space=pl.ANY)],
        out_specs=[_rows(tm, D), _whole((1, D)), _whole((1, D))],
        out_shape=[SDS((T, D), F32), SDS((1, D), F32), SDS((1, D), F32)],
        compiler_params=_params("arbitrary"),
    )(*[dy for dy, _, _ in pairs], *[w for _, w, _ in pairs], h, g, dh, after)


def _mm_nt(name, dy, w, out_dtype):
    T, N = dy.shape
    K = w.shape[0]
    tm = min(512, T)

    def body(dy_ref, w_ref, o_ref):
        o_ref[...] = lax.dot_general(dy_ref[...].astype(BF16), w_ref[...], (((1,), (1,)), ((), ())),
                                     preferred_element_type=F32).astype(out_dtype)

    return pl.pallas_call(
        body, name=name, grid=(T // tm,),
        in_specs=[_rows(tm, N), _whole((K, N))],
        out_specs=_rows(tm, K),
        out_shape=SDS((T, K), out_dtype),
        compiler_params=_params("parallel"),
    )(dy, w)


def _conv_bwd(dact, c, a, w_dw, ln_g, ln_b):
    T, D = c.shape
    tc = _conv_tile(T)
    per = tc // CONV_HALO
    n_tiles = T // tc
    last_halo = T // CONV_HALO - 1

    def ln_bwd(dact_v, c_v, lg, lb):
        xc = c_v - jnp.mean(c_v, axis=-1, keepdims=True)
        rstd = lax.rsqrt(jnp.mean(xc * xc, axis=-1, keepdims=True) + LN_EPS)
        z = xc * rstd
        _, dsilu = _silu_and_grad(z * lg + lb)
        dl = dact_v * dsilu
        dz = dl * lg
        dc = rstd * (dz - jnp.mean(dz, axis=-1, keepdims=True) - z * jnp.mean(dz * z, axis=-1, keepdims=True))
        return dc, dl, z

    def body(dact_ref, dactn_ref, c_ref, cn_ref, a_ref, ah_ref, w_ref, lg_ref, lb_ref,
             da_ref, dlg_ref, dlb_ref, dbdw_ref, dwdw_ref, dbpw1_ref, dc_sh, u_sh, du_scr):
        i = pl.program_id(0)

        @pl.when(i == 0)
        def _():
            for ref in (dlg_ref, dlb_ref, dbdw_ref, dwdw_ref, dbpw1_ref):
                ref[...] = jnp.zeros_like(ref)

        lg, lb = lg_ref[...], lb_ref[...]
        dc, dl, z = ln_bwd(dact_ref[...], c_ref[...], lg, lb)
        dlg_ref[...] += jnp.sum(dl * z, axis=0, keepdims=True)
        dlb_ref[...] += jnp.sum(dl, axis=0, keepdims=True)
        dbdw_ref[...] += jnp.sum(dc, axis=0, keepdims=True)
        dcn, _, _ = ln_bwd(dactn_ref[...], cn_ref[...], lg, lb)
        dc_sh[0, 0:tc, :] = dc
        dc_sh[0, tc:, :] = jnp.where(i < n_tiles - 1, dcn, 0.0)
        _fill_shifted(dc_sh, tc)

        a_v = a_ref[...]
        a1 = a_v[:, :D]
        sg = jax.nn.sigmoid(a_v[:, D:])
        u_sh[0, 0:CONV_HALO, :] = jnp.where(i > 0, _glu(ah_ref[...], D), 0.0)
        u_sh[0, CONV_HALO:, :] = a1 * sg
        _fill_shifted(u_sh, tc)

        _depthwise_taps(dc_sh, w_ref, [CONV_WIDTH - 1 - j for j in range(CONV_WIDTH)], None, du_scr, tc)
        _depthwise_tap_grads(dc_sh, u_sh, [CONV_FIRST_TAP + j for j in range(CONV_WIDTH)], dwdw_ref, tc)

        du = du_scr[...]
        da1 = du * sg
        da2 = du * a1 * sg * (1.0 - sg)
        da_ref[:, :D] = da1.astype(BF16)
        da_ref[:, D:] = da2.astype(BF16)
        dbpw1_ref[:, :D] += jnp.sum(da1, axis=0, keepdims=True)
        dbpw1_ref[:, D:] += jnp.sum(da2, axis=0, keepdims=True)

    nxt = lambda i: (jnp.minimum((i + 1) * per, last_halo), 0)
    return pl.pallas_call(
        body, name="conv_bwd", grid=(n_tiles,),
        in_specs=[_rows(tc, D), pl.BlockSpec((CONV_HALO, D), nxt),
                  _rows(tc, D), pl.BlockSpec((CONV_HALO, D), nxt),
                  _rows(tc, 2 * D),
                  pl.BlockSpec((CONV_HALO, 2 * D), lambda i: (jnp.maximum(i * per - 1, 0), 0)),
                  _whole((CONV_WIDTH, D)), _whole((1, D)), _whole((1, D))],
        out_specs=[_rows(tc, 2 * D), _whole((1, D)), _whole((1, D)), _whole((1, D)),
                   _whole((CONV_HALO, D)), _whole((1, 2 * D))],
        out_shape=[SDS((T, 2 * D), BF16), SDS((1, D), F32), SDS((1, D), F32), SDS((1, D), F32),
                   SDS((CONV_HALO, D), F32), SDS((1, 2 * D), F32)],
        scratch_shapes=[pltpu.VMEM((V7X_SUBLANES, tc + CONV_HALO, D), F32),
                        pltpu.VMEM((V7X_SUBLANES, tc + CONV_HALO, D), F32), pltpu.VMEM((tc, D), F32)],
        compiler_params=_params("arbitrary"),
    )(dact, dact, c, c, a, a, w_dw, ln_g, ln_b)


def _attn_bwd(qkv, dao, cos, sin, probs_saved, psink_saved):
    T = qkv.shape[0]
    nb = T // ATTN_BLOCK
    qw = N_Q_HEADS * HEAD_DIM
    kw = N_KV_HEADS * HEAD_DIM

    def body(q_ref, kc_ref, kp_ref, vc_ref, vp_ref, do_ref, cos_ref, sin_ref, cosp_ref, sinp_ref, probs_ref, psink_ref,
             dq_ref, dkv_ref, dsink_ref, dbq_ref, dbkv_ref, carry, prev_scr, cur_scr, dq_scr):
        n = pl.program_id(0)

        @pl.when(n == 0)
        def _():
            for ref in (dsink_ref, dbq_ref, dbkv_ref, carry):
                ref[...] = jnp.zeros_like(ref)

        @pl.when(n == nb)
        def _():
            prev_scr[...] = jnp.zeros_like(prev_scr)

        @pl.when(n < nb)
        def _():
            prev_part = _from_previous_block(N_Q_HEADS * ATTN_BLOCK)
            half = PAIRS_PER_KV * ATTN_BLOCK
            upper = _upper_lanes((ATTN_BLOCK, 2 * HEAD_DIM))
            groups = range(N_KV_HEADS)

            def nt(a, b):
                return lax.dot_general(a, b, (((1,), (1,)), ((), ())), preferred_element_type=F32)

            def kv_grad(even_rows, odd_rows, x):
                even = lax.dot_general(even_rows, x, (((0,), (0,)), ((), ())), preferred_element_type=F32)
                odd = lax.dot_general(odd_rows, x, (((0,), (0,)), ((), ())), preferred_element_type=F32)
                t = jnp.where(upper, odd, even)
                return t + _swap_lane_halves(t)

            q = [_pair_rows(q_ref, g) for g in groups]
            do = [_pair_rows(do_ref, g) for g in groups]
            k_prev = [_kv_operands(g, kp_ref[...]) for g in groups]
            k_cur = [_kv_operands(g, kc_ref[...]) for g in groups]
            v_prev = [_kv_operands(g, vp_ref[...]) for g in groups]
            v_cur = [_kv_operands(g, vc_ref[...]) for g in groups]
            probs = probs_ref[...].astype(F32)
            dp_prev = jnp.concatenate([nt(do[g], v_prev[g][i]) for g in groups for i in range(2)], axis=0)
            dp_cur = jnp.concatenate([nt(do[g], v_cur[g][i]) for g in groups for i in range(2)], axis=0)
            dp = jnp.where(prev_part, dp_prev, dp_cur)
            delta = jnp.sum(probs * dp, axis=1, keepdims=True)
            ds_prev, ds_cur = _split_folded(probs * (dp - delta) * (HEAD_DIM ** -0.5), prev_part)
            p_prev, p_cur = _split_folded(probs_ref[...], prev_part)
            for i, h in enumerate(_all_heads()):
                rows = slice(i * ATTN_BLOCK, (i + 1) * ATTN_BLOCK)
                dsink_ref[:, h:h + 1] += jnp.sum(-(psink_ref[:, h:h + 1] * delta[rows]), axis=0, keepdims=True)
            kv_grads = []
            for g in groups:
                even, odd = slice(2 * g * half, (2 * g + 1) * half), slice((2 * g + 1) * half, (2 * g + 2) * half)
                dq = (jnp.dot(ds_prev[even], k_prev[g][0], preferred_element_type=F32)
                      + jnp.dot(ds_cur[even], k_cur[g][0], preferred_element_type=F32)
                      + jnp.dot(ds_prev[odd], k_prev[g][1], preferred_element_type=F32)
                      + jnp.dot(ds_cur[odd], k_cur[g][1], preferred_element_type=F32))
                for i in range(PAIRS_PER_KV):
                    hp = g * PAIRS_PER_KV + i
                    dq_scr[:, hp * 2 * HEAD_DIM:(hp + 1) * 2 * HEAD_DIM] = dq[i * ATTN_BLOCK:(i + 1) * ATTN_BLOCK]
                kv_grads.append((kv_grad(ds_prev[even], ds_prev[odd], q[g]), kv_grad(ds_cur[even], ds_cur[odd], q[g]),
                                 kv_grad(p_prev[even], p_prev[odd], do[g]), kv_grad(p_cur[even], p_cur[odd], do[g])))
            (dkp0, dkc0, dvp0, dvc0), (dkp1, dkc1, dvp1, dvc1) = kv_grads
            prev_scr[:, :kw] = jnp.where(upper, dkp1, dkp0)
            prev_scr[:, kw:] = jnp.where(upper, dvp1, dvp0)
            cur_scr[:, :kw] = jnp.where(upper, dkc1, dkc0)
            cur_scr[:, kw:] = jnp.where(upper, dvc1, dvc0)
            dq_pre = _rope_transposed(dq_scr[...], cos_ref, sin_ref)
            dq_ref[...] = dq_pre.astype(BF16)
            dbq_ref[...] += jnp.sum(dq_pre, axis=0, keepdims=True)

        tot = carry[...] + prev_scr[...]
        dk_pre = _rope_transposed(tot[:, :kw], cosp_ref, sinp_ref)
        dkv_ref[:, :kw] = dk_pre.astype(BF16)
        dkv_ref[:, kw:] = tot[:, kw:].astype(BF16)
        dbkv_ref[:, :kw] += jnp.sum(dk_pre, axis=0, keepdims=True)
        dbkv_ref[:, kw:] += jnp.sum(tot[:, kw:], axis=0, keepdims=True)

        @pl.when(n < nb)
        def _():
            carry[...] = cur_scr[...]

    cur = lambda n: (jnp.minimum(n, nb - 1), 0)
    out_lag = lambda n: (jnp.maximum(n - 1, 0), 0)
    return pl.pallas_call(
        body, name="attn_bwd", grid=(nb + 1,),
        in_specs=[*_attn_specs(T),
                  pl.BlockSpec((ATTN_BLOCK, qw), cur),
                  pl.BlockSpec((ATTN_BLOCK, V7X_LANES), cur), pl.BlockSpec((ATTN_BLOCK, V7X_LANES), cur),
                  pl.BlockSpec((ATTN_BLOCK, V7X_LANES), out_lag), pl.BlockSpec((ATTN_BLOCK, V7X_LANES), out_lag),
                  pl.BlockSpec((None, N_Q_HEADS * ATTN_BLOCK, ATTN_BLOCK), lambda n: (jnp.minimum(n, nb - 1), 0, 0)),
                  pl.BlockSpec((ATTN_BLOCK, V7X_LANES), cur)],
        out_specs=[pl.BlockSpec((ATTN_BLOCK, qw), cur), pl.BlockSpec((ATTN_BLOCK, 2 * kw), out_lag),
                   _whole((1, N_Q_HEADS)), _whole((1, qw)), _whole((1, 2 * kw))],
        out_shape=[SDS((T, qw), BF16), SDS((T, 2 * kw), BF16),
                   SDS((1, N_Q_HEADS), F32), SDS((1, qw), F32), SDS((1, 2 * kw), F32)],
        scratch_shapes=[pltpu.VMEM((ATTN_BLOCK, 2 * kw), F32), pltpu.VMEM((ATTN_BLOCK, 2 * kw), F32),
                        pltpu.VMEM((ATTN_BLOCK, 2 * kw), F32), pltpu.VMEM((ATTN_BLOCK, qw), F32)],
        compiler_params=_params("arbitrary"),
    )(qkv, qkv, qkv, qkv, qkv, dao, cos, sin, cos, sin, probs_saved, psink_saved)


def _local_step(x, target, p, reduce_begin, reduce_send):
    T, D = x.shape
    cos, sin = _rope_tables(T)
    qw = N_Q_HEADS * HEAD_DIM
    nm, nf = p["norm_mix"], p["norm_ffn"]

    y0, qkv = _qkv_proj(x, nm[0:1], p["attn_w_qkv"], p["attn_b_qkv"], cos, sin, p["gather_started"])
    ao, attn_probs, sink_probs = _attn_fwd(qkv, p["attn_sinks"])
    h1, f0 = _mm_res("attn_out", ao, p["attn_w_o"], p["attn_b_o"], x, nf[0:1])
    p = {**p, **p["other_weights"](h1)}
    w1, w3, w2 = p["ffn_w1"], p["ffn_w3"], p["ffn_w2"]
    act0, gg0, s0 = _ffn_up("ffn0_up", f0, w1, w3, 0)
    h2 = _ffn_down("ffn0_down", s0, w2, 0, h1)
    y1, a = _pw1_proj(h2, nm[1:2], p["conv_w_pw1"], p["conv_b_pw1"])
    c, act = _conv_fwd(a, p["conv_w_dw"], p["conv_b_dw"], p["conv_ln_g"], p["conv_ln_b"])
    h3, f1 = _mm_res("conv_out", act, p["conv_w_pw2"], p["conv_b_pw2"], h2, nf[1:2])
    act1, gg1, s1 = _ffn_up("ffn1_up", f1, w1, w3, 1)
    h4 = _ffn_down("ffn1_down", s1, w2, 1, h3)
    dh4, loss, d_norm_final = _final_loss(h4, p["norm_final"], target)

    g = {}
    dg1, dg3 = _ffn_bwd_act("ffn1_bwd_act", dh4, w2, 1, act1, gg1)
    dw2_1 = _mm_tn("ffn1_dw2", s1, dh4)
    dw1_1 = _mm_tn("ffn1_dw1", dg1, f1)
    dw3_1 = _mm_tn("ffn1_dw3", dg3, f1)
    begun = reduce_begin("ffn1", {("ffn_w1", 1): dw1_1, ("ffn_w3", 1): dw3_1, ("ffn_w2", 1): dw2_1})
    dh3, dnf1, db_pw2 = _mm_nt_normbwd("ffn1_bwd_in", [(dg1, w1, 1), (dg3, w3, 1)], h3, nf[1:2], dh4, begun)
    sent = reduce_send("ffn1", dh3)

    dw_pw2 = _mm_tn("conv_dw_pw2", act, dh3, after=sent)
    dact = _mm_nt("conv_bwd_out", dh3, p["conv_w_pw2"], F32)
    da, d_ln_g, d_ln_b, d_b_dw, d_w_dw, d_b_pw1 = _conv_bwd(dact, c, a, p["conv_w_dw"], p["conv_ln_g"], p["conv_ln_b"])
    dw_pw1 = _mm_tn("conv_dw_pw1", y1, da, col_chunks=N_CHIPS)
    begun = reduce_begin("conv", {("conv_w_pw2", 0): dw_pw2.reshape(N_CHIPS, -1, D), ("conv_w_pw1", 0): dw_pw1})
    dh2, dnm1, _ = _mm_nt_normbwd("conv_bwd_in", [(da, p["conv_w_pw1"], None)], h2, nm[1:2], dh3, begun)
    sent = reduce_send("conv", dh2)

    dg1, dg3 = _ffn_bwd_act("ffn0_bwd_act", dh2, w2, 0, act0, gg0)
    dw2_0 = _mm_tn("ffn0_dw2", s0, dh2, after=sent)
    dw1_0 = _mm_tn("ffn0_dw1", dg1, f0)
    dw3_0 = _mm_tn("ffn0_dw3", dg3, f0)
    begun = reduce_begin("ffn0", {("ffn_w1", 0): dw1_0, ("ffn_w3", 0): dw3_0, ("ffn_w2", 0): dw2_0})
    dh1, dnf0, db_o = _mm_nt_normbwd("ffn0_bwd_in", [(dg1, w1, 0), (dg3, w3, 0)], h1, nf[0:1], dh2, begun)
    sent = reduce_send("ffn0", dh1)

    dw_o = _mm_tn("attn_dw_o", ao, dh1, after=sent)
    dao = _mm_nt("attn_bwd_out", dh1, p["attn_w_o"], BF16)
    dq, dkv, d_sinks, dbq, dbkv = _attn_bwd(qkv, dao, cos, sin, attn_probs, sink_probs)
    dwq = _mm_tn("attn_dw_q", dq, y0)
    dwkv = _mm_tn("attn_dw_kv", dkv, y0)
    wqkv = p["attn_w_qkv"]
    dwqkv = jnp.concatenate([dwq, dwkv], axis=0).reshape(N_CHIPS, -1, D)
    begun = reduce_begin("attn", {("attn_w_o", 0): dw_o.reshape(N_CHIPS, -1, D), ("attn_w_qkv", 0): dwqkv})
    dx, dnm0, _ = _mm_nt_normbwd("attn_bwd_in", [(dq, wqkv[:qw], None), (dkv, wqkv[qw:], None)], x, nm[0:1], dh1,
                                 begun)
    reduce_send("attn", dx)

    g["norm_mix"] = jnp.concatenate([dnm0, dnm1], axis=0)
    g["norm_ffn"] = jnp.concatenate([dnf0, dnf1], axis=0)
    g["attn_b_qkv"] = jnp.concatenate([dbq, dbkv], axis=1)
    g["attn_sinks"] = d_sinks
    g["attn_b_o"] = db_o
    g["conv_b_pw1"] = d_b_pw1
    g["conv_w_dw"] = d_w_dw[:CONV_WIDTH]
    g["conv_b_dw"] = d_b_dw
    g["conv_ln_g"] = d_ln_g
    g["conv_ln_b"] = d_ln_b
    g["conv_b_pw2"] = db_pw2
    g["norm_final"] = d_norm_final
    return loss, dx, g


ANY = pl.BlockSpec(memory_space=pl.ANY)
VMEM_WHOLE = pl.BlockSpec(memory_space=pltpu.VMEM)


def _my_place():
    return lax.axis_index("x"), lax.axis_index("y"), lax.axis_index("c")


def _other_chips(x, y):
    places = [(1 - x, y), (x, 1 - y), (1 - x, 1 - y)]
    return [(bx, by, 2 * bx + by) for bx, by in places]


def _exchange_small(name, v, reduce):
    r, w = v.shape

    def body(v_ref, o_ref, *rest):
        if reduce:
            buf, send_sems, recv_sems = rest
        else:
            buf = o_ref
            send_sems, recv_sems = rest
        x, y, c = _my_place()
        me = 4 * x + 2 * y + c
        sends = []
        for k in range(1, N_DEV):
            peer = (1 - x if k & 4 else x, 1 - y if k & 2 else y, 1 - c if k & 1 else c)
            cp = pltpu.make_async_remote_copy(
                src_ref=v_ref, dst_ref=buf.at[me], send_sem=send_sems.at[k - 1], recv_sem=recv_sems.at[k - 1],
                device_id=peer, device_id_type=MESH)
            cp.start()
            sends.append(cp)
        buf[me] = v_ref[...]
        for k in range(1, N_DEV):
            src = 4 * (1 - x if k & 4 else x) + 2 * (1 - y if k & 2 else y) + (1 - c if k & 1 else c)
            pltpu.make_async_remote_copy(
                src_ref=v_ref, dst_ref=buf.at[src], send_sem=send_sems.at[k - 1], recv_sem=recv_sems.at[k - 1],
                device_id=(x, y, c), device_id_type=MESH).wait_recv()
        for cp in sends:
            cp.wait_send()
        if reduce:
            acc = buf[0]
            for d in range(1, N_DEV):
                acc = acc + buf[d]
            o_ref[...] = acc

    sems = [pltpu.SemaphoreType.DMA((N_DEV - 1,)), pltpu.SemaphoreType.DMA((N_DEV - 1,))]
    if reduce:
        out_shape = SDS((r, w), F32)
        scratch = [pltpu.VMEM((N_DEV, r, w), F32)] + sems
    else:
        out_shape = SDS((N_DEV, r, w), F32)
        scratch = sems
    return pl.pallas_call(
        body, name=name, out_shape=out_shape, in_specs=[VMEM_WHOLE], out_specs=VMEM_WHOLE,
        scratch_shapes=scratch,
        compiler_params=pltpu.CompilerParams(vmem_limit_bytes=V7X_VMEM_LIMIT_BYTES),
    )(v)


def _cast_into_slot(name, gathered, shard, chip_idx):
    rows, cols = shard.shape
    tr = _pack_row_tile(rows)

    def body(k_ref, s_ref, g_ref, o_ref):
        o_ref[...] = s_ref[...].astype(BF16)

    return pl.pallas_call(
        body, name=name,
        grid_spec=pltpu.PrefetchScalarGridSpec(
            num_scalar_prefetch=1, grid=(rows // tr,),
            in_specs=[pl.BlockSpec((tr, cols), lambda i, k_ref: (i, 0)), pl.BlockSpec(memory_space=pl.ANY)],
            out_specs=pl.BlockSpec((None, tr, cols), lambda i, k_ref: (k_ref[0], i, 0))),
        out_shape=SDS(gathered.shape, BF16),
        input_output_aliases={2: 0},
        compiler_params=_params("parallel"),
    )(chip_idx, shard, gathered)


def _row_halves(ref, c):
    half = ref.shape[1] // 2
    return pl.ds(pl.multiple_of(c * half, 16), half), pl.ds(pl.multiple_of((1 - c) * half, 16), half)


def _gather_ici_copies(refs, send_sems, recv_sems):
    x, y, c = _my_place()
    k = 2 * x + y
    pairs = []
    for i, ref in enumerate(refs):
        mine, _ = _row_halves(ref, c)
        for j, (bx, by, kb) in enumerate(_other_chips(x, y)):
            sems = dict(send_sem=send_sems.at[3 * i + j], recv_sem=recv_sems.at[3 * i + j], device_id_type=MESH)
            send = pltpu.make_async_remote_copy(src_ref=ref.at[k, mine], dst_ref=ref.at[k, mine],
                                                device_id=(bx, by, c), **sems)
            arrival = pltpu.make_async_remote_copy(src_ref=ref.at[kb, mine], dst_ref=ref.at[kb, mine],
                                                   device_id=(bx, by, c), **sems)
            pairs.append((send, arrival))
    return pairs


def _gather_d2d_copies(refs, send_sems, recv_sems, first_sem):
    x, y, c = _my_place()
    pairs = []
    for i, ref in enumerate(refs):
        mine, theirs = _row_halves(ref, c)
        for j, (_, _, kb) in enumerate(_other_chips(x, y)):
            sem = first_sem + 3 * i + j
            sems = dict(send_sem=send_sems.at[sem], recv_sem=recv_sems.at[sem], device_id=(x, y, 1 - c),
                        device_id_type=MESH)
            send = pltpu.make_async_remote_copy(src_ref=ref.at[kb, mine], dst_ref=ref.at[kb, mine], **sems)
            arrival = pltpu.make_async_remote_copy(src_ref=ref.at[kb, theirs], dst_ref=ref.at[kb, theirs], **sems)
            pairs.append((send, arrival))
    return pairs


def _run_copies(pairs):
    for send, _ in pairs:
        send.start()
    for send, arrival in pairs:
        send.wait_send()
        arrival.wait_recv()


def _gather_now(name, gathered):
    n_w = len(gathered)

    def body(*refs):
        in_refs = refs[:n_w]
        send_sems, recv_sems = refs[2 * n_w:]
        _run_copies(_gather_ici_copies(in_refs, send_sems, recv_sems))
        _run_copies(_gather_d2d_copies(in_refs, send_sems, recv_sems, 3 * n_w))

    return pl.pallas_call(
        body, name=name, out_shape=[SDS(g.shape, g.dtype) for g in gathered],
        in_specs=[ANY] * n_w, out_specs=[ANY] * n_w, input_output_aliases={i: i for i in range(n_w)},
        scratch_shapes=[pltpu.SemaphoreType.DMA((6 * n_w,)), pltpu.SemaphoreType.DMA((6 * n_w,))],
    )(*gathered)


def _gather_start(name, gathered):
    n_w = len(gathered)

    def body(*refs):
        in_refs = refs[:n_w]
        send_sems, recv_sems = refs[n_w:n_w + 2]
        for send, _ in _gather_ici_copies(in_refs, send_sems, recv_sems):
            send.start()
        refs[-1][...] = jnp.zeros_like(refs[-1])

    out = pl.pallas_call(
        body, name=name,
        out_shape=(pltpu.SemaphoreType.DMA((3 * n_w,)), pltpu.SemaphoreType.DMA((3 * n_w,)),
                   *[pltpu.HBM(g.shape, g.dtype) for g in gathered], SDS((8, V7X_LANES), F32)),
        in_specs=[HBM_SPEC] * n_w, out_specs=(SEM_SPEC, SEM_SPEC, *[HBM_SPEC] * n_w, VMEM_WHOLE),
        input_output_aliases={i: 2 + i for i in range(n_w)},
        compiler_params=pltpu.CompilerParams(has_side_effects=DATAFLOW),
    )(*[pltpu.with_memory_space_constraint(g, pltpu.HBM) for g in gathered])
    return out[0], out[1], list(out[2:2 + n_w]), out[-1]


def _gather_wait(name, send_sems, recv_sems, gathered, after):
    n_w = len(gathered)

    def body(*refs):
        in_refs = refs[:n_w]
        send_sems, recv_sems = refs[n_w:n_w + 2]
        for send, arrival in _gather_ici_copies(in_refs, send_sems, recv_sems):
            send.wait_send()
            arrival.wait_recv()

    out = pl.pallas_call(
        body, name=name, out_shape=tuple(pltpu.HBM(g.shape, g.dtype) for g in gathered),
        in_specs=[*[HBM_SPEC] * n_w, SEM_SPEC, SEM_SPEC, ANY], out_specs=tuple([HBM_SPEC] * n_w),
        input_output_aliases={i: i for i in range(n_w)},
        compiler_params=pltpu.CompilerParams(has_side_effects=DATAFLOW),
    )(*gathered, send_sems, recv_sems, after)
    return list(out)


def _swap_fetched_with_sibling(name, gathered):
    n_w = len(gathered)

    def body(*refs):
        in_refs = refs[:n_w]
        send_sems, recv_sems = refs[2 * n_w:]
        _run_copies(_gather_d2d_copies(in_refs, send_sems, recv_sems, 0))

    return pl.pallas_call(
        body, name=name, out_shape=[SDS(g.shape, g.dtype) for g in gathered],
        in_specs=[ANY] * n_w, out_specs=[ANY] * n_w, input_output_aliases={i: i for i in range(n_w)},
        scratch_shapes=[pltpu.SemaphoreType.DMA((3 * n_w,)), pltpu.SemaphoreType.DMA((3 * n_w,))],
    )(*gathered)


def _sibling_swap_copies(g_refs, land_refs, send_sems, recv_sems):
    x, y, c = _my_place()
    copies = []
    for i, g_ref in enumerate(g_refs):
        half = g_ref.shape[1] // 2
        theirs = pl.ds(pl.multiple_of((1 - c) * half, 8), half)
        copies.append(pltpu.make_async_remote_copy(
            src_ref=g_ref.at[:, theirs], dst_ref=land_refs[i], send_sem=send_sems.at[i], recv_sem=recv_sems.at[i],
            device_id=(x, y, 1 - c), device_id_type=MESH))
    return copies


def _sibling_swap_start(name, grads):
    n_g = len(grads)

    def body(*refs):
        g_refs, land_refs = refs[:n_g], refs[n_g:2 * n_g]
        send_sems, recv_sems = refs[2 * n_g:2 * n_g + 2]
        for cp in _sibling_swap_copies(g_refs, land_refs, send_sems, recv_sems):
            cp.start()
        refs[-1][...] = jnp.zeros_like(refs[-1])

    lands = [pltpu.with_memory_space_constraint(lax.empty((g.shape[0], g.shape[1] // 2, g.shape[2]), g.dtype),
                                                pltpu.HBM) for g in grads]
    out = pl.pallas_call(
        body, name=name,
        out_shape=(pltpu.SemaphoreType.DMA((n_g,)), pltpu.SemaphoreType.DMA((n_g,)),
                   *[pltpu.HBM(g.shape, g.dtype) for g in grads], *[pltpu.HBM(l.shape, l.dtype) for l in lands],
                   SDS((8, V7X_LANES), F32)),
        in_specs=[HBM_SPEC] * (2 * n_g), out_specs=(SEM_SPEC, SEM_SPEC, *[HBM_SPEC] * (2 * n_g), VMEM_WHOLE),
        input_output_aliases={i: 2 + i for i in range(2 * n_g)},
        compiler_params=pltpu.CompilerParams(has_side_effects=DATAFLOW),
    )(*[pltpu.with_memory_space_constraint(g, pltpu.HBM) for g in grads], *lands)
    return out[0], out[1], list(out[2:2 + n_g]), list(out[2 + n_g:2 + 2 * n_g]), out[-1]


def _sibling_swap_wait(name, send_sems, recv_sems, grads, lands, after):
    n_g = len(grads)

    def body(*refs):
        g_refs, land_refs = refs[:n_g], refs[n_g:2 * n_g]
        send_sems, recv_sems = refs[2 * n_g:2 * n_g + 2]
        for cp in _sibling_swap_copies(g_refs, land_refs, send_sems, recv_sems):
            cp.wait_send()
            cp.wait_recv()

    out = pl.pallas_call(
        body, name=name,
        out_shape=(*[pltpu.HBM(g.shape, g.dtype) for g in grads], *[pltpu.HBM(l.shape, l.dtype) for l in lands]),
        in_specs=[*[HBM_SPEC] * (2 * n_g), SEM_SPEC, SEM_SPEC, ANY], out_specs=tuple([HBM_SPEC] * (2 * n_g)),
        input_output_aliases={i: i for i in range(2 * n_g)},
        compiler_params=pltpu.CompilerParams(has_side_effects=DATAFLOW),
    )(*grads, *lands, send_sems, recv_sems, after)
    return list(out[:n_g]), list(out[n_g:])


def _pack_row_tile(rows):
    for t in range(min(rows, 512), 7, -1):
        if rows % t == 0 and t % 8 == 0:
            return t
    return rows


def _add_sibling_half(name, grads, from_sibling, c_idx):
    n, R, w = grads.shape
    half = R // 2
    tr = _pack_row_tile(half)
    steps = half // tr

    def body(c_ref, g_ref, s_ref, o_ref):
        o_ref[...] = (g_ref[...] + s_ref[...]).astype(BF16)

    return pl.pallas_call(
        body, name=name,
        grid_spec=pltpu.PrefetchScalarGridSpec(
            num_scalar_prefetch=1, grid=(n, steps),
            in_specs=[pl.BlockSpec((1, tr, w), lambda j, i, c_ref: (j, c_ref[0] * steps + i, 0)),
                      pl.BlockSpec((1, tr, w), lambda j, i, c_ref: (j, i, 0))],
            out_specs=pl.BlockSpec((1, tr, w), lambda j, i, c_ref: (j, i, 0))),
        out_shape=SDS((n, half, w), BF16),
        compiler_params=_params("parallel", "parallel"),
    )(c_idx, grads, from_sibling)


HBM_SPEC = pl.BlockSpec(memory_space=pltpu.HBM)
SEM_SPEC = pl.BlockSpec(memory_space=pltpu.SEMAPHORE)
DATAFLOW = pltpu.SideEffectType.DATAFLOW_SIDE_EFFECTING


def _chip_scatter_copies(p_refs, land_refs, send_sems, recv_sems):
    x, y, c = _my_place()
    return [pltpu.make_async_remote_copy(
        src_ref=p_refs[i].at[kb], dst_ref=land_refs[i].at[j], send_sem=send_sems.at[3 * i + j],
        recv_sem=recv_sems.at[3 * i + j], device_id=(bx, by, c), device_id_type=MESH)
        for i in range(len(p_refs)) for j, (bx, by, kb) in enumerate(_other_chips(x, y))]


def _scatter_start(name, partials):
    n_p = len(partials)

    def body(*refs):
        p_refs, land_refs = refs[:n_p], refs[n_p:2 * n_p]
        send_sems, recv_sems = refs[2 * n_p:2 * n_p + 2]
        for cp in _chip_scatter_copies(p_refs, land_refs, send_sems, recv_sems):
            cp.start()
        refs[-1][...] = jnp.zeros_like(refs[-1])

    lands = [pltpu.with_memory_space_constraint(lax.empty((N_CHIPS - 1,) + p.shape[1:], p.dtype), pltpu.HBM)
             for p in partials]
    out = pl.pallas_call(
        body, name=name,
        out_shape=(pltpu.SemaphoreType.DMA((3 * n_p,)), pltpu.SemaphoreType.DMA((3 * n_p,)),
                   *[pltpu.HBM(p.shape, p.dtype) for p in partials], *[pltpu.HBM(l.shape, l.dtype) for l in lands],
                   SDS((8, V7X_LANES), F32)),
        in_specs=[HBM_SPEC] * (2 * n_p), out_specs=(SEM_SPEC, SEM_SPEC, *[HBM_SPEC] * (2 * n_p), VMEM_WHOLE),
        input_output_aliases={i: 2 + i for i in range(2 * n_p)},
        compiler_params=pltpu.CompilerParams(has_side_effects=DATAFLOW),
    )(*[pltpu.with_memory_space_constraint(p, pltpu.HBM) for p in partials], *lands)
    return out[0], out[1], list(out[2:2 + n_p]), list(out[2 + n_p:2 + 2 * n_p]), out[-1]


def _scatter_wait(name, send_sems, recv_sems, partials, lands, after):
    n_p = len(partials)

    def body(*refs):
        p_refs, land_refs = refs[:n_p], refs[n_p:2 * n_p]
        send_sems, recv_sems = refs[2 * n_p:2 * n_p + 2]
        for cp in _chip_scatter_copies(p_refs, land_refs, send_sems, recv_sems):
            cp.wait_send()
            cp.wait_recv()

    out = pl.pallas_call(
        body, name=name,
        out_shape=(*[pltpu.HBM(p.shape, p.dtype) for p in partials], *[pltpu.HBM(l.shape, l.dtype) for l in lands]),
        in_specs=[*[HBM_SPEC] * (2 * n_p), SEM_SPEC, SEM_SPEC, ANY], out_specs=tuple([HBM_SPEC] * (2 * n_p)),
        input_output_aliases={i: i for i in range(2 * n_p)},
        compiler_params=pltpu.CompilerParams(has_side_effects=DATAFLOW),
    )(*partials, *lands, send_sems, recv_sems, after)
    return list(out[:n_p]), list(out[n_p:])


def _sum_chip_partials(name, grads, from_sibling, received, shard, layer, place):
    n, half, w = from_sibling.shape
    tr = _pack_row_tile(half)
    steps = half // tr

    def body(place_ref, g_ref, s_ref, r_ref, shard_ref, o_ref):
        own = g_ref[0] + s_ref[0]
        o_ref[...] = ((own + r_ref[0].astype(F32)) + r_ref[1].astype(F32)) + r_ref[2].astype(F32)

    return pl.pallas_call(
        body, name=name,
        grid_spec=pltpu.PrefetchScalarGridSpec(
            num_scalar_prefetch=1, grid=(steps,),
            in_specs=[pl.BlockSpec((1, tr, w), lambda i, place_ref: (place_ref[0], place_ref[1] * steps + i, 0)),
                      pl.BlockSpec((1, tr, w), lambda i, place_ref: (place_ref[0], i, 0)),
                      pl.BlockSpec((n - 1, tr, w), lambda i, place_ref: (0, i, 0)),
                      pl.BlockSpec(memory_space=pl.ANY)],
            out_specs=pl.BlockSpec((tr, w), lambda i, place_ref: ((2 * layer + place_ref[1]) * steps + i, 0))),
        out_shape=SDS(shard.shape, F32),
        input_output_aliases={4: 0},
        compiler_params=_params("parallel"),
    )(place, grads, from_sibling, received, shard)


def _join_halves(shards, layers):
    n_s = len(shards)
    n_sem = sum(layers)

    def body(*refs):
        in_refs = refs[:n_s]
        send_sems, recv_sems = refs[2 * n_s:]
        x, y, c = _my_place()
        copies, sem = [], 0
        for ref, n_layers in zip(in_refs, layers):
            half = ref.shape[0] // (2 * n_layers)
            for layer in range(n_layers):
                mine = pl.ds(pl.multiple_of(layer * 2 * half + c * half, 8), half)
                theirs = pl.ds(pl.multiple_of(layer * 2 * half + (1 - c) * half, 8), half)
                send = pltpu.make_async_remote_copy(
                    src_ref=ref.at[mine], dst_ref=ref.at[mine], send_sem=send_sems.at[sem], recv_sem=recv_sems.at[sem],
                    device_id=(x, y, 1 - c), device_id_type=MESH)
                send.start()
                arrival = pltpu.make_async_remote_copy(
                    src_ref=ref.at[theirs], dst_ref=ref.at[theirs], send_sem=send_sems.at[sem],
                    recv_sem=recv_sems.at[sem], device_id=(x, y, 1 - c), device_id_type=MESH)
                copies.append((send, arrival))
                sem += 1
        for send, arrival in copies:
            send.wait_send()
            arrival.wait_recv()

    return pl.pallas_call(
        body, name="join_halves", out_shape=[SDS(s.shape, s.dtype) for s in shards],
        in_specs=[ANY] * n_s, out_specs=[ANY] * n_s,
        input_output_aliases={i: i for i in range(n_s)},
        scratch_shapes=[pltpu.SemaphoreType.DMA((n_sem,)), pltpu.SemaphoreType.DMA((n_sem,))],
    )(*shards)


def _adamw(name, w, g, m, v):
    rows, width = w.shape
    tr = _pack_row_tile(rows)

    def body(w_ref, g_ref, m_ref, v_ref, g_out_ref, d_ref, nm_ref, nv_ref):
        gg = g_ref[...]
        g_out_ref[...] = gg
        m_new = ADAM_B1 * m_ref[...] + (1.0 - ADAM_B1) * gg
        v_new = ADAM_B2 * v_ref[...] + (1.0 - ADAM_B2) * (gg * gg)
        m_hat = m_new / (1.0 - ADAM_B1 ** ADAM_STEP)
        v_hat = v_new / (1.0 - ADAM_B2 ** ADAM_STEP)
        d_ref[...] = -ADAM_LR * (m_hat / (jnp.sqrt(v_hat) + ADAM_EPS) + ADAM_WD * w_ref[...])
        nm_ref[...] = m_new
        nv_ref[...] = v_new

    spec = _rows(tr, width)
    return pl.pallas_call(
        body, name=name, grid=(rows // tr,),
        in_specs=[spec] * 4, out_specs=[spec] * 4,
        out_shape=[SDS((rows, width), F32)] * 4,
        compiler_params=_params("parallel"),
    )(w, g, m, v)


WEIGHT_NAMES = ['norm_mix', 'norm_ffn', 'attn_w_qkv', 'attn_b_qkv', 'attn_sinks', 'attn_w_o', 'attn_b_o',
                'conv_w_pw1', 'conv_b_pw1', 'conv_w_dw', 'conv_b_dw', 'conv_ln_g', 'conv_ln_b', 'conv_w_pw2',
                'conv_b_pw2', 'ffn_w1', 'ffn_w3', 'ffn_w2', 'norm_final']
BIG = ['attn_w_qkv', 'attn_w_o', 'conv_w_pw1', 'conv_w_pw2', 'ffn_w1', 'ffn_w3', 'ffn_w2']
COLUMN_SPLIT = ('attn_w_qkv', 'conv_w_pw1', 'ffn_w1', 'ffn_w3')
STORED_TRANSPOSED = ('attn_w_qkv', 'ffn_w1', 'ffn_w3')
SMALL_SPLIT = ['conv_b_pw1', 'conv_w_dw', 'conv_b_dw', 'conv_ln_g', 'conv_ln_b', 'conv_b_pw2']
SMALL_WHOLE = ['norm_mix', 'norm_ffn', 'attn_b_qkv', 'attn_sinks', 'attn_b_o', 'norm_final']


def _keeps_rows(shape):
    return len(shape) == 2 and shape[0] > 1 and shape[1] == PACK_W


def _pack_rows(arrays, dtype, row_multiple):
    blocks = [jnp.pad(a.astype(dtype), ((0, -a.shape[0] % V7X_SUBLANES), (0, 0)))
              for a in arrays if _keeps_rows(a.shape)]
    flat = jnp.concatenate([a.astype(dtype).reshape(-1) for a in arrays if not _keeps_rows(a.shape)])
    multiple = max(row_multiple, V7X_SUBLANES)
    rows = -(-(-(-flat.shape[0] // PACK_W)) // multiple) * multiple
    blocks.append(jnp.pad(flat, (0, rows * PACK_W - flat.shape[0])).reshape(rows, PACK_W))
    return jnp.concatenate(blocks, axis=0) if len(blocks) > 1 else blocks[0]


def _unpack_rows(pack, shapes):
    out, row = {}, 0
    for i, shape in enumerate(shapes):
        if _keeps_rows(shape):
            out[i] = pack[row:row + shape[0]]
            row += -(-shape[0] // V7X_SUBLANES) * V7X_SUBLANES
    flat, at = pack[row:].reshape(-1), 0
    for i, shape in enumerate(shapes):
        if not _keeps_rows(shape):
            size = 1
            for s in shape:
                size *= s
            out[i] = flat[at:at + size].reshape(shape)
            at += size
    return [out[i] for i in range(len(shapes))]


def _join_chip_axis(name, parts):
    axis = parts.ndim - 1 if name in COLUMN_SPLIT or name in SMALL_SPLIT else parts.ndim - 2
    moved = jnp.moveaxis(parts, 0, axis - 1)
    shape = list(moved.shape)
    shape[axis - 1:axis + 1] = [shape[axis - 1] * shape[axis]]
    return moved.reshape(shape)


def _split_chip_axis(name, whole, shard_shape):
    axis = len(shard_shape) - 1 if name in COLUMN_SPLIT or name in SMALL_SPLIT else len(shard_shape) - 2
    shape = list(whole.shape)
    shape[axis:axis + 1] = [N_CHIPS, shard_shape[axis]]
    return jnp.moveaxis(whole.reshape(shape), axis, 0)


def kernel(x, norm_mix, norm_ffn, attn_w_qkv, attn_b_qkv, attn_sinks, attn_w_o, attn_b_o, conv_w_pw1, conv_b_pw1, conv_w_dw, conv_b_dw, conv_ln_g, conv_ln_b, conv_w_pw2, conv_b_pw2, ffn_w1, ffn_w3, ffn_w2, norm_final, loss_target, m_norm_mix, m_norm_ffn, m_attn_w_qkv, m_attn_b_qkv, m_attn_sinks, m_attn_w_o, m_attn_b_o, m_conv_w_pw1, m_conv_b_pw1, m_conv_w_dw, m_conv_b_dw, m_conv_ln_g, m_conv_ln_b, m_conv_w_pw2, m_conv_b_pw2, m_ffn_w1, m_ffn_w3, m_ffn_w2, m_norm_final, v_norm_mix, v_norm_ffn, v_attn_w_qkv, v_attn_b_qkv, v_attn_sinks, v_attn_w_o, v_attn_b_o, v_conv_w_pw1, v_conv_b_pw1, v_conv_w_dw, v_conv_b_dw, v_conv_ln_g, v_conv_ln_b, v_conv_w_pw2, v_conv_b_pw2, v_ffn_w1, v_ffn_w3, v_ffn_w2, v_norm_final):
    w = dict(zip(WEIGHT_NAMES, (norm_mix, norm_ffn, attn_w_qkv, attn_b_qkv, attn_sinks, attn_w_o, attn_b_o,
                                conv_w_pw1, conv_b_pw1, conv_w_dw, conv_b_dw, conv_ln_g, conv_ln_b, conv_w_pw2,
                                conv_b_pw2, ffn_w1, ffn_w3, ffn_w2, norm_final)))
    m = dict(zip(WEIGHT_NAMES, (m_norm_mix, m_norm_ffn, m_attn_w_qkv, m_attn_b_qkv, m_attn_sinks, m_attn_w_o,
                                m_attn_b_o, m_conv_w_pw1, m_conv_b_pw1, m_conv_w_dw, m_conv_b_dw, m_conv_ln_g,
                                m_conv_ln_b, m_conv_w_pw2, m_conv_b_pw2, m_ffn_w1, m_ffn_w3, m_ffn_w2, m_norm_final)))
    v = dict(zip(WEIGHT_NAMES, (v_norm_mix, v_norm_ffn, v_attn_w_qkv, v_attn_b_qkv, v_attn_sinks, v_attn_w_o,
                                v_attn_b_o, v_conv_w_pw1, v_conv_b_pw1, v_conv_w_dw, v_conv_b_dw, v_conv_ln_g,
                                v_conv_ln_b, v_conv_w_pw2, v_conv_b_pw2, v_ffn_w1, v_ffn_w3, v_ffn_w2, v_norm_final)))
    T, D = x.shape[1], x.shape[2]
    c_idx = lax.axis_index("c").astype(jnp.int32).reshape(1)
    chip = (2 * lax.axis_index("x") + lax.axis_index("y")).astype(jnp.int32)

    def as_rows(n, a):
        a = jnp.swapaxes(a, -1, -2) if n in STORED_TRANSPOSED else a
        return a.reshape(-1, a.shape[-1])

    def from_rows(n, rows):
        shape = w[n].shape[:-2] + w[n].shape[:-3:-1] if n in STORED_TRANSPOSED else w[n].shape
        a = rows.reshape(shape)
        return jnp.swapaxes(a, -1, -2) if n in STORED_TRANSPOSED else a

    slabs = {n: _cast_into_slot(f"cast_{n}", lax.empty((N_CHIPS,) + as_rows(n, w[n]).shape, BF16), as_rows(n, w[n]),
                                chip.reshape(1)) for n in BIG}
    first, later = BIG[:2], BIG[2:]
    qkv_parts, w_o_parts = _gather_now("gather_attn", [slabs[n] for n in first])
    send_sems, recv_sems, travelling, gather_started = _gather_start("gather_start", [slabs[n] for n in later])
    layers = ffn_w1.shape[0]
    small_shapes = [w[n].shape for n in SMALL_SPLIT]
    small_all = _exchange_small("gather_small", _pack_rows([w[n] for n in SMALL_SPLIT], F32, 8), reduce=False)
    per_chip = [_unpack_rows(small_all[2 * j], small_shapes) for j in range(N_CHIPS)]
    full = {}
    for i, n in enumerate(SMALL_SPLIT):
        full[n] = _join_chip_axis(n, jnp.stack([per_chip[j][i] for j in range(N_CHIPS)]))

    def other_weights(after):
        landed = _gather_wait("gather_wait", send_sems, recv_sems, travelling, after)
        gathered = dict(zip(later, _swap_fetched_with_sibling("gather_swap", landed)))
        return {"conv_w_pw1": gathered["conv_w_pw1"], "conv_w_pw2": gathered["conv_w_pw2"].reshape(-1, D),
                "ffn_w1": gathered["ffn_w1"].reshape(N_CHIPS, layers, -1, D),
                "ffn_w3": gathered["ffn_w3"].reshape(N_CHIPS, layers, -1, D),
                "ffn_w2": gathered["ffn_w2"].reshape(N_CHIPS, layers, -1, D)}

    p = {
        "norm_mix": norm_mix, "norm_ffn": norm_ffn, "norm_final": norm_final.reshape(1, D),
        "attn_w_qkv": qkv_parts.reshape(-1, D), "attn_b_qkv": attn_b_qkv,
        "attn_sinks": attn_sinks, "attn_w_o": w_o_parts.reshape(-1, D), "attn_b_o": attn_b_o,
        "conv_b_pw1": full["conv_b_pw1"], "conv_w_dw": full["conv_w_dw"][0],
        "conv_b_dw": full["conv_b_dw"], "conv_ln_g": full["conv_ln_g"], "conv_ln_b": full["conv_ln_b"],
        "conv_b_pw2": full["conv_b_pw2"], "gather_started": gather_started, "other_weights": other_weights,
    }
    swapping, in_flight = {}, []

    def reduce_begin(tag, grads):
        keys = list(grads)
        *handles, begun = _sibling_swap_start(f"sibling_swap_start_{tag}", [grads[k] for k in keys])
        swapping[tag] = (keys, handles)
        return begun

    def reduce_send(tag, after):
        keys, (swap_send, swap_recv, grads, lands) = swapping[tag]
        grads, from_sibling = _sibling_swap_wait(f"sibling_swap_wait_{tag}", swap_send, swap_recv, grads, lands, after)
        partials = [_add_sibling_half(f"add_sibling_half_{tag}{i}", gr, fs, c_idx)
                    for i, (gr, fs) in enumerate(zip(grads, from_sibling))]
        *handles, sent = _scatter_start(f"scatter_start_{tag}", partials)
        in_flight.append((tag, keys, handles, grads, from_sibling))
        return sent

    loss_part, dx, g = _local_step(x[0], loss_target[0], p, reduce_begin, reduce_send)
    for n in SMALL_WHOLE + SMALL_SPLIT:
        g[n] = g[n].reshape((-1,) + g[n].shape[-2:]) if w[n].ndim == 3 else g[n].reshape(w[n].shape[:-1] + (-1,))

    place = jnp.stack([chip, c_idx[0]])
    shard_grad = {n: lax.empty(as_rows(n, w[n]).shape, F32) for n in BIG}
    for tag, keys, (send_sems, recv_sems, partials, lands), grads, from_sibling in in_flight:
        _, received = _scatter_wait(f"scatter_wait_{tag}", send_sems, recv_sems, partials, lands, dx)
        for i, (n, layer) in enumerate(keys):
            shard_grad[n] = _sum_chip_partials(f"sum_chip_partials_{tag}{i}", grads[i], from_sibling[i], received[i],
                                               shard_grad[n], layer, place)
    g_big = dict(zip(BIG, _join_halves([shard_grad[n] for n in BIG], [w[n].shape[0] for n in BIG])))
    big_out = {}
    for n in BIG:
        step = _adamw(f"adamw_{n}", as_rows(n, w[n]), g_big[n], as_rows(n, m[n]), as_rows(n, v[n]))
        big_out[n] = [from_rows(n, a) for a in step]

    small_whole_shapes = [w[n].shape for n in SMALL_WHOLE]
    small_full_shapes = [g[n].shape for n in SMALL_SPLIT]
    reduced = _exchange_small(
        "reduce_small", _pack_rows([loss_part] + [g[n] for n in SMALL_WHOLE] + [g[n] for n in SMALL_SPLIT], F32, 8),
        reduce=True)
    pieces = _unpack_rows(reduced, [(1,)] + small_whole_shapes + small_full_shapes)
    loss = pieces[0].reshape(())
    g_small = dict(zip(SMALL_WHOLE, pieces[1:1 + len(SMALL_WHOLE)]))
    for n, whole in zip(SMALL_SPLIT, pieces[1 + len(SMALL_WHOLE):]):
        parts = _split_chip_axis(n, whole, w[n].shape)
        g_small[n] = lax.dynamic_index_in_dim(parts, chip, axis=0, keepdims=False)
    small = SMALL_WHOLE + SMALL_SPLIT
    _, d_small, m_small, v_small = _adamw(
        "adamw_small", _pack_rows([w[n] for n in small], F32, 8), _pack_rows([g_small[n] for n in small], F32, 8),
        _pack_rows([m[n] for n in small], F32, 8), _pack_rows([v[n] for n in small], F32, 8))

    outs = {}
    for slot, (tag, small_pack) in enumerate((("g", None), ("d", d_small), ("m", m_small), ("v", v_small))):
        vals = {n: big_out[n][slot] for n in BIG}
        if small_pack is None:
            vals.update(g_small)
        else:
            vals.update(zip(small, _unpack_rows(small_pack, [w[n].shape for n in small])))
        outs[tag] = vals
    return (loss, dx.reshape(1, T, D), *[outs["g"][n] for n in WEIGHT_NAMES], *[outs["d"][n] for n in WEIGHT_NAMES],
            *[outs["m"][n] for n in WEIGHT_NAMES], *[outs["v"][n] for n in WEIGHT_NAMES])
```

```python
import functools

import jax
import jax.numpy as jnp
from jax import lax
from jax.experimental import pallas as pl
from jax.experimental.pallas import tpu as pltpu

F32 = jnp.float32
BF16 = jnp.bfloat16
SDS = jax.ShapeDtypeStruct
MESH = pl.DeviceIdType.MESH

HEAD_DIM = 64
N_Q_HEADS = 16
N_KV_HEADS = 2
Q_PER_KV = N_Q_HEADS // N_KV_HEADS
ATTN_BLOCK = 128
ROPE_THETA = 10000.0
CONV_WIDTH = 31
CONV_HALO = 32
CONV_FIRST_TAP = CONV_HALO - CONV_WIDTH + 1
CONV_ROW_CHUNK = 64
CONV_LANE_CHUNK = 256
CONV_GRAD_UNROLL = 4
RMS_EPS = 1e-5
LN_EPS = 1e-5
ADAM_LR = 0.001
ADAM_B1 = 0.9
ADAM_B2 = 0.999
ADAM_EPS = 1e-08
ADAM_WD = 0.01
ADAM_STEP = 10

V7X_LANES = 128
V7X_SUBLANES = 8
V7X_VMEM_LIMIT_BYTES = 56 * 1024 * 1024

N_CHIPS = 4
N_DEV = 8
PACK_W = 1024

MASK_VALUE = -1e30


def _params(*semantics):
    return pltpu.CompilerParams(dimension_semantics=semantics, vmem_limit_bytes=V7X_VMEM_LIMIT_BYTES)


def _rows(tm, width):
    return pl.BlockSpec((tm, width), lambda i: (i, 0))


def _whole(shape):
    return pl.BlockSpec(shape, lambda *_: (0,) * len(shape))


def _rms_rstd(h):
    return lax.rsqrt(jnp.mean(h * h, axis=-1, keepdims=True) + RMS_EPS)


def _silu_and_grad(z):
    sg = jax.nn.sigmoid(z)
    return z * sg, sg * (1.0 + z * (1.0 - sg))


def _swap_rope_halves(t):
    w = t.shape[1]
    half = HEAD_DIM // 2
    lane = lax.broadcasted_iota(jnp.int32, t.shape, 1)
    upper = pltpu.roll(t, w - half, 1)
    lower = pltpu.roll(t, half, 1)
    return jnp.where(lane % HEAD_DIM < half, upper, lower)


def _rope(t, cos_ref, sin_ref):
    reps = t.shape[1] // V7X_LANES
    c = jnp.tile(cos_ref[...], (1, reps))
    s = jnp.tile(sin_ref[...], (1, reps))
    return t * c + _swap_rope_halves(t) * s


def _rope_transposed(dt, cos_ref, sin_ref):
    reps = dt.shape[1] // V7X_LANES
    c = jnp.tile(cos_ref[...], (1, reps))
    s = jnp.tile(sin_ref[...], (1, reps))
    return dt * c + _swap_rope_halves(dt * s)


def _rope_tables(seq_len):
    pos = jnp.arange(seq_len, dtype=F32)
    inv_freq = ROPE_THETA ** (-jnp.arange(0, HEAD_DIM, 2, dtype=F32) / HEAD_DIM)
    ang = pos[:, None] * jnp.tile(inv_freq, 2 * V7X_LANES // HEAD_DIM)[None, :]
    upper_half = jnp.arange(V7X_LANES) % HEAD_DIM >= HEAD_DIM // 2
    return jnp.cos(ang), jnp.where(upper_half[None, :], jnp.sin(ang), -jnp.sin(ang))


def _qkv_proj(h, g, w, b, cos, sin, after):
    T, D = h.shape
    N = w.shape[0]
    tm = min(512, T)
    rope_w = N - N_KV_HEADS * HEAD_DIM

    def body(h_ref, g_ref, w_ref, b_ref, cos_ref, sin_ref, _, y_ref, o_ref):
        hh = h_ref[...]
        y = (hh * _rms_rstd(hh) * g_ref[...]).astype(BF16)
        y_ref[...] = y
        acc = _dot_nt(y, w_ref[...]) + b_ref[...]
        o_ref[:, :rope_w] = _rope(acc[:, :rope_w], cos_ref, sin_ref).astype(BF16)
        o_ref[:, rope_w:] = acc[:, rope_w:].astype(BF16)

    return pl.pallas_call(
        body, name="qkv_proj", grid=(T // tm,),
        in_specs=[_rows(tm, D), _whole((1, D)), _whole((N, D)), _whole((1, N)),
                  _rows(tm, V7X_LANES), _rows(tm, V7X_LANES), pl.BlockSpec(memory_space=pl.ANY)],
        out_specs=[_rows(tm, D), _rows(tm, N)],
        out_shape=[SDS((T, D), BF16), SDS((T, N), BF16)],
        compiler_params=_params("parallel"),
    )(h, g, w, b, cos, sin, after)


def _pw1_proj(h, g, w, b):
    T, D = h.shape
    n = w.shape[2]
    N = N_CHIPS * n
    tm = min(512, T)

    def body(h_ref, g_ref, w_ref, b_ref, y_ref, o_ref):
        hh = h_ref[...]
        y = (hh * _rms_rstd(hh) * g_ref[...]).astype(BF16)
        y_ref[...] = y
        for j in range(N_CHIPS):
            cols = slice(j * n, (j + 1) * n)
            o_ref[:, cols] = jnp.dot(y, w_ref[j], preferred_element_type=F32) + b_ref[:, cols]

    return pl.pallas_call(
        body, name="pw1_proj", grid=(T // tm,),
        in_specs=[_rows(tm, D), _whole((1, D)), _whole((N_CHIPS, D, n)), _whole((1, N))],
        out_specs=[_rows(tm, D), _rows(tm, N)],
        out_shape=[SDS((T, D), BF16), SDS((T, N), F32)],
        compiler_params=_params("parallel"),
    )(h, g, w, b)


PAIRS_PER_KV = Q_PER_KV // 2


def _upper_lanes(shape):
    return lax.broadcasted_iota(jnp.int32, shape, 1) >= HEAD_DIM


def _swap_lane_halves(t):
    return pltpu.roll(t.astype(F32), HEAD_DIM, 1).astype(t.dtype)


def _kv_operands(g, t):
    swapped = _swap_lane_halves(t)
    in_lower, in_upper = (t, swapped) if g == 0 else (swapped, t)
    upper = _upper_lanes(t.shape)
    zero = jnp.zeros_like(t)
    return jnp.where(upper, zero, in_lower), jnp.where(upper, in_upper, zero)


def _group_heads(g):
    pairs = range(g * PAIRS_PER_KV, (g + 1) * PAIRS_PER_KV)
    return [2 * hp for hp in pairs] + [2 * hp + 1 for hp in pairs]


def _all_heads():
    return [h for g in range(N_KV_HEADS) for h in _group_heads(g)]


def _pair_rows(ref, g):
    pairs = range(g * PAIRS_PER_KV, (g + 1) * PAIRS_PER_KV)
    return jnp.concatenate([ref[:, hp * 2 * HEAD_DIM:(hp + 1) * 2 * HEAD_DIM] for hp in pairs], axis=0)


def _from_previous_block(rows):
    row = lax.broadcasted_iota(jnp.int32, (ATTN_BLOCK, ATTN_BLOCK), 0)
    col = lax.broadcasted_iota(jnp.int32, (ATTN_BLOCK, ATTN_BLOCK), 1)
    return jnp.concatenate([col > row] * (rows // ATTN_BLOCK), axis=0)


def _folded_probs(n, q_groups, k_prev_groups, k_cur_groups, sink_ref, prev_part):
    def scores(q, k):
        return lax.dot_general(q, k, (((1,), (1,)), ((), ())), preferred_element_type=F32)

    s_prev = jnp.concatenate([scores(q, k[i]) for q, k in zip(q_groups, k_prev_groups) for i in range(2)], axis=0)
    s_cur = jnp.concatenate([scores(q, k[i]) for q, k in zip(q_groups, k_cur_groups) for i in range(2)], axis=0)
    s_prev = jnp.where(n > 0, s_prev, MASK_VALUE * (HEAD_DIM ** 0.5))
    s = jnp.where(prev_part, s_prev, s_cur) * (HEAD_DIM ** -0.5)
    heads = [h for g in range(N_KV_HEADS) for h in _group_heads(g)]
    sink = jnp.concatenate([jnp.broadcast_to(sink_ref[0:1, h:h + 1], (ATTN_BLOCK, 1)) for h in heads], axis=0)
    m = jnp.maximum(jnp.max(s, axis=1, keepdims=True), sink)
    p = jnp.exp(s - m)
    e_sink = jnp.exp(sink - m)
    inv = 1.0 / (jnp.sum(p, axis=1, keepdims=True) + e_sink)
    return p * inv, e_sink * inv


def _split_folded(t, prev_part):
    tb = t.astype(BF16)
    zero = jnp.zeros_like(tb)
    return jnp.where(prev_part, tb, zero), jnp.where(prev_part, zero, tb)


def _attn_specs(T):
    nb = T // ATTN_BLOCK
    kcol = N_Q_HEADS * HEAD_DIM // V7X_LANES
    cur = lambda n: jnp.minimum(n, nb - 1)
    prev = lambda n: jnp.maximum(jnp.minimum(n, nb - 1) - 1, 0)
    q_spec = pl.BlockSpec((ATTN_BLOCK, N_Q_HEADS * HEAD_DIM), lambda n: (cur(n), 0))
    kc_spec = pl.BlockSpec((ATTN_BLOCK, V7X_LANES), lambda n: (cur(n), kcol))
    kp_spec = pl.BlockSpec((ATTN_BLOCK, V7X_LANES), lambda n: (prev(n), kcol))
    vc_spec = pl.BlockSpec((ATTN_BLOCK, V7X_LANES), lambda n: (cur(n), kcol + 1))
    vp_spec = pl.BlockSpec((ATTN_BLOCK, V7X_LANES), lambda n: (prev(n), kcol + 1))
    return q_spec, kc_spec, kp_spec, vc_spec, vp_spec


def _attn_fwd(qkv, sinks):
    T = qkv.shape[0]
    nb = T // ATTN_BLOCK
    qw = N_Q_HEADS * HEAD_DIM
    all_rows = N_Q_HEADS * ATTN_BLOCK

    def body(q_ref, kc_ref, kp_ref, vc_ref, vp_ref, sink_ref, o_ref, probs_ref, psink_ref):
        n = pl.program_id(0)
        prev_part = _from_previous_block(all_rows)
        half = PAIRS_PER_KV * ATTN_BLOCK
        groups = range(N_KV_HEADS)
        probs, p_sink = _folded_probs(n, [_pair_rows(q_ref, g) for g in groups],
                                      [_kv_operands(g, kp_ref[...]) for g in groups],
                                      [_kv_operands(g, kc_ref[...]) for g in groups], sink_ref, prev_part)
        probs_ref[...] = probs.astype(BF16)
        lane = lax.broadcasted_iota(jnp.int32, (ATTN_BLOCK, V7X_LANES), 1)
        sink_tile = jnp.zeros((ATTN_BLOCK, V7X_LANES), F32)
        for i, h in enumerate(_all_heads()):
            sink_tile = jnp.where(lane == h, p_sink[i * ATTN_BLOCK:(i + 1) * ATTN_BLOCK], sink_tile)
        psink_ref[...] = sink_tile
        p_prev, p_cur = _split_folded(probs, prev_part)
        for g in groups:
            v_prev, v_cur = _kv_operands(g, vp_ref[...]), _kv_operands(g, vc_ref[...])
            even, odd = slice(2 * g * half, (2 * g + 1) * half), slice((2 * g + 1) * half, (2 * g + 2) * half)
            o = (jnp.dot(p_prev[even], v_prev[0], preferred_element_type=F32)
                 + jnp.dot(p_cur[even], v_cur[0], preferred_element_type=F32)
                 + jnp.dot(p_prev[odd], v_prev[1], preferred_element_type=F32)
                 + jnp.dot(p_cur[odd], v_cur[1], preferred_element_type=F32))
            for i in range(PAIRS_PER_KV):
                hp = g * PAIRS_PER_KV + i
                o_ref[:, hp * 2 * HEAD_DIM:(hp + 1) * 2 * HEAD_DIM] = (
                    o[i * ATTN_BLOCK:(i + 1) * ATTN_BLOCK].astype(BF16))

    return pl.pallas_call(
        body, name="attn_fwd", grid=(nb,),
        in_specs=[*_attn_specs(T), _whole((1, N_Q_HEADS))],
        out_specs=[_rows(ATTN_BLOCK, qw), pl.BlockSpec((None, all_rows, ATTN_BLOCK), lambda n: (n, 0, 0)),
                   _rows(ATTN_BLOCK, V7X_LANES)],
        out_shape=[SDS((T, qw), BF16), SDS((nb, all_rows, ATTN_BLOCK), BF16), SDS((T, V7X_LANES), F32)],
        compiler_params=_params("parallel"),
    )(qkv, qkv, qkv, qkv, qkv, sinks)


def _mm_res(name, a, w, b, res, g):
    T, K = a.shape
    D = w.shape[1]
    tm = min(512, T)

    def body(a_ref, w_ref, b_ref, r_ref, g_ref, o_ref, f_ref):
        h = jnp.dot(a_ref[...], w_ref[...], preferred_element_type=F32) + b_ref[...] + r_ref[...]
        o_ref[...] = h
        f_ref[...] = (h * _rms_rstd(h) * g_ref[...]).astype(BF16)

    return pl.pallas_call(
        body, name=name, grid=(T // tm,),
        in_specs=[_rows(tm, K), _whole((K, D)), _whole((1, D)), _rows(tm, D), _whole((1, D))],
        out_specs=[_rows(tm, D), _rows(tm, D)],
        out_shape=[SDS((T, D), F32), SDS((T, D), BF16)],
        compiler_params=_params("parallel"),
    )(a, w, b, res, g)


def _ffn_down(name, s, w2, layer, res):
    _, T, n = s.shape
    D = w2.shape[3]
    tm = min(512, T)

    def body(s_ref, w_ref, r_ref, o_ref):
        acc = r_ref[...]
        for j in range(N_CHIPS):
            acc = acc + jnp.dot(s_ref[j], w_ref[j], preferred_element_type=F32)
        o_ref[...] = acc

    return pl.pallas_call(
        body, name=name, grid=(T // tm,),
        in_specs=[pl.BlockSpec((N_CHIPS, tm, n), lambda i: (0, i, 0)),
                  pl.BlockSpec((N_CHIPS, None, n, D), lambda i: (0, layer, 0, 0)), _rows(tm, D)],
        out_specs=_rows(tm, D),
        out_shape=SDS((T, D), F32),
        compiler_params=_params("parallel"),
    )(s, w2, res)


def _ffn_up(name, f, w1, w3, layer):
    T, D = f.shape
    n = w1.shape[2]
    tm = min(1024, T)

    def body(f_ref, w1_ref, w3_ref, act_ref, gg_ref, s_ref):
        ff = f_ref[...]
        g1 = _dot_nt(ff, w1_ref[...])
        g3 = _dot_nt(ff, w3_ref[...])
        act, dact = _silu_and_grad(g1)
        act_ref[...] = act.astype(BF16)
        gg_ref[...] = (g3 * dact).astype(BF16)
        s_ref[...] = (act * g3).astype(BF16)

    slab = pl.BlockSpec((None, tm, n), lambda j, i: (j, i, 0))
    wslab = pl.BlockSpec((None, None, n, D), lambda j, i: (j, layer, 0, 0))
    hidden = SDS((N_CHIPS, T, n), BF16)
    return pl.pallas_call(
        body, name=name, grid=(N_CHIPS, T // tm),
        in_specs=[pl.BlockSpec((tm, D), lambda j, i: (i, 0)), wslab, wslab],
        out_specs=[slab, slab, slab],
        out_shape=[hidden, hidden, hidden],
        compiler_params=_params("parallel", "parallel"),
    )(f, w1, w3)


def _glu(a, d):
    return a[:, :d] * jax.nn.sigmoid(a[:, d:])


def _conv_tile(T):
    return min(256, T)


def _fill_shifted(sh_ref, tc):
    n = tc + CONV_HALO - V7X_SUBLANES
    for r in range(1, V7X_SUBLANES):
        sh_ref[r, 0:n, :] = sh_ref[0, pl.ds(r, n), :]


def _depthwise_taps(sh_ref, w_ref, offsets, bias_ref, out_ref, tc):
    D = out_ref.shape[1]

    def chunk(i, carry):
        t0 = pl.multiple_of(i * CONV_ROW_CHUNK, CONV_ROW_CHUNK)
        for cb in range(D // CONV_LANE_CHUNK):
            cs = slice(cb * CONV_LANE_CHUNK, (cb + 1) * CONV_LANE_CHUNK)
            acc = jnp.zeros((CONV_ROW_CHUNK, CONV_LANE_CHUNK), F32)
            for r in range(V7X_SUBLANES):
                taps = [(j, o // V7X_SUBLANES) for j, o in enumerate(offsets) if o % V7X_SUBLANES == r]
                if not taps:
                    continue
                span = CONV_ROW_CHUNK + V7X_SUBLANES * max(q for _, q in taps)
                rows = sh_ref[r, pl.ds(t0, span), cs]
                for j, q in taps:
                    acc = acc + rows[V7X_SUBLANES * q:V7X_SUBLANES * q + CONV_ROW_CHUNK] * w_ref[j:j + 1, cs]
            if bias_ref is not None:
                acc = acc + bias_ref[:, cs]
            out_ref[pl.ds(t0, CONV_ROW_CHUNK), cs] = acc
        return carry

    lax.fori_loop(0, tc // CONV_ROW_CHUNK, chunk, 0)


def _depthwise_tap_grads(dy_sh, x_sh, offsets, dw_ref, tc):
    D = dw_ref.shape[1]
    for cb in range(D // V7X_LANES):
        cs = slice(cb * V7X_LANES, (cb + 1) * V7X_LANES)

        def row_tiles(i, accs, cs=cs):
            for k in range(CONV_GRAD_UNROLL):
                t0 = pl.multiple_of(i * (CONV_GRAD_UNROLL * V7X_SUBLANES), V7X_SUBLANES) + k * V7X_SUBLANES
                d = dy_sh[0, pl.ds(t0, V7X_SUBLANES), cs]
                accs = tuple(
                    acc + d * x_sh[o % V7X_SUBLANES, pl.ds(t0 + o // V7X_SUBLANES * V7X_SUBLANES, V7X_SUBLANES), cs]
                    for acc, o in zip(accs, offsets))
            return accs

        zero = jnp.zeros((V7X_SUBLANES, V7X_LANES), F32)
        accs = lax.fori_loop(0, tc // (CONV_GRAD_UNROLL * V7X_SUBLANES), row_tiles, tuple(zero for _ in offsets))
        for j, acc in enumerate(accs):
            dw_ref[j:j + 1, cs] += jnp.sum(acc, axis=0, keepdims=True)


def _conv_fwd(a, w_dw, b_dw, ln_g, ln_b):
    T = a.shape[0]
    D = a.shape[1] // 2
    tc = _conv_tile(T)
    per = tc // CONV_HALO

    def body(a_ref, ah_ref, w_ref, bdw_ref, lg_ref, lb_ref, c_ref, act_ref, u_sh):
        i = pl.program_id(0)
        u_sh[0, 0:CONV_HALO, :] = jnp.where(i > 0, _glu(ah_ref[...], D), 0.0)
        u_sh[0, CONV_HALO:, :] = _glu(a_ref[...], D)
        _fill_shifted(u_sh, tc)
        _depthwise_taps(u_sh, w_ref, [CONV_FIRST_TAP + j for j in range(CONV_WIDTH)], bdw_ref, c_ref, tc)
        c = c_ref[...]
        xc = c - jnp.mean(c, axis=-1, keepdims=True)
        z = xc * lax.rsqrt(jnp.mean(xc * xc, axis=-1, keepdims=True) + LN_EPS)
        l = z * lg_ref[...] + lb_ref[...]
        act_ref[...] = (l * jax.nn.sigmoid(l)).astype(BF16)

    return pl.pallas_call(
        body, name="conv_fwd", grid=(T // tc,),
        in_specs=[_rows(tc, 2 * D),
                  pl.BlockSpec((CONV_HALO, 2 * D), lambda i: (jnp.maximum(i * per - 1, 0), 0)),
                  _whole((CONV_WIDTH, D)), _whole((1, D)), _whole((1, D)), _whole((1, D))],
        out_specs=[_rows(tc, D), _rows(tc, D)],
        out_shape=[SDS((T, D), F32), SDS((T, D), BF16)],
        scratch_shapes=[pltpu.VMEM((V7X_SUBLANES, tc + CONV_HALO, D), F32)],
        compiler_params=_params("parallel"),
    )(a, a, w_dw, b_dw, ln_g, ln_b)


def _final_loss(h, g, target):
    T, D = h.shape
    tm = min(512, T)

    def body(h_ref, g_ref, t_ref, dh_ref, loss_ref, dg_ref):
        @pl.when(pl.program_id(0) == 0)
        def _():
            loss_ref[...] = jnp.zeros_like(loss_ref)
            dg_ref[...] = jnp.zeros_like(dg_ref)

        hh = h_ref[...]
        r = _rms_rstd(hh)
        g = g_ref[...]
        d = hh * r * g - t_ref[...]
        loss_ref[...] += 0.5 * jnp.sum(jnp.mean(d * d, axis=-1, keepdims=True), axis=0, keepdims=True)
        dout = d * (1.0 / D)
        dg_ref[...] += jnp.sum(dout * (hh * r), axis=0, keepdims=True)
        dxh = dout * g
        dh_ref[...] = r * dxh - hh * (r * r * r) * jnp.mean(dxh * hh, axis=-1, keepdims=True)

    return pl.pallas_call(
        body, name="final_loss", grid=(T // tm,),
        in_specs=[_rows(tm, D), _whole((1, D)), _rows(tm, D)],
        out_specs=[_rows(tm, D), _whole((1, 1)), _whole((1, D))],
        out_shape=[SDS((T, D), F32), SDS((1, 1), F32), SDS((1, D), F32)],
        compiler_params=_params("arbitrary"),
    )(h, g, target)


def _ffn_bwd_act(name, dh, w2, layer, act, gate_grad):
    T, D = dh.shape
    n = w2.shape[2]
    tm = min(1024, T)

    def body(dh_ref, w2_ref, act_ref, gg_ref, dg1_ref, dg3_ref):
        ds = lax.dot_general(dh_ref[...].astype(BF16), w2_ref[...], (((1,), (1,)), ((), ())),
                             preferred_element_type=F32)
        dg1_ref[...] = (ds * gg_ref[...].astype(F32)).astype(BF16)
        dg3_ref[...] = (ds * act_ref[...].astype(F32)).astype(BF16)

    slab = pl.BlockSpec((None, tm, n), lambda i, j: (j, i, 0))
    hidden = SDS((N_CHIPS, T, n), BF16)
    return pl.pallas_call(
        body, name=name, grid=(T // tm, N_CHIPS),
        in_specs=[pl.BlockSpec((tm, D), lambda i, j: (i, 0)),
                  pl.BlockSpec((None, None, n, D), lambda i, j: (j, layer, 0, 0)), slab, slab],
        out_specs=[slab, slab],
        out_shape=[hidden, hidden],
        compiler_params=_params("parallel", "arbitrary"),
    )(dh, w2, act, gate_grad)


def _dot_tn(a, b):
    return lax.dot_general(a.astype(BF16), b.astype(BF16), (((0,), (0,)), ((), ())), preferred_element_type=F32)


def _dot_nt(a, b):
    return lax.dot_general(a.astype(BF16), b, (((1,), (1,)), ((), ())), preferred_element_type=F32)


def _mm_tn(name, a, b, col_chunks=1, after=None):
    a_slabs, b_slabs = a.ndim == 3, b.ndim == 3
    T = a.shape[-2]
    tt = min(1024, T)
    ka, nb = a.shape[-1], b.shape[-1]
    if a_slabs or b_slabs:
        out_dims = (N_CHIPS, ka, nb)
    elif col_chunks > 1:
        out_dims = (col_chunks, ka, nb // col_chunks)
    else:
        out_dims = (ka, nb)

    def body(a_ref, b_ref, *rest):
        o_ref = rest[-1]

        @pl.when(pl.program_id(0) == 0)
        def _():
            o_ref[...] = jnp.zeros_like(o_ref)

        if a_slabs:
            bb = b_ref[...].astype(BF16)
            for j in range(N_CHIPS):
                o_ref[j] += _dot_tn(a_ref[j], bb)
        elif b_slabs:
            aa = a_ref[...].astype(BF16)
            for j in range(N_CHIPS):
                o_ref[j] += _dot_tn(aa, b_ref[j])
        elif col_chunks > 1:
            aa = a_ref[...].astype(BF16)
            w = nb // col_chunks
            for j in range(col_chunks):
                o_ref[j] += _dot_tn(aa, b_ref[:, j * w:(j + 1) * w])
        else:
            o_ref[...] += _dot_tn(a_ref[...], b_ref[...])

    def spec(arr, slabs):
        if slabs:
            return pl.BlockSpec((N_CHIPS, tt, arr.shape[-1]), lambda t: (0, t, 0))
        return _rows(tt, arr.shape[-1])

    return pl.pallas_call(
        body, name=name, grid=(T // tt,),
        in_specs=[spec(a, a_slabs), spec(b, b_slabs)] + ([] if after is None else [pl.BlockSpec(memory_space=pl.ANY)]),
        out_specs=_whole(out_dims),
        out_shape=SDS(out_dims, F32),
        compiler_params=_params("arbitrary"),
    )(a, b, *([] if after is None else [after]))


def _mm_nt_normbwd(name, pairs, h, g, dh, after):
    T, D = h.shape
    tm = min(512, T)
    n_pairs = len(pairs)
    kinds = ["slabs" if dy.ndim == 3 else ("quarters" if w.ndim == 3 else "plain") for dy, w, _ in pairs]

    def body(*refs):
        dy_refs = refs[:n_pairs]
        w_refs = refs[n_pairs:2 * n_pairs]
        h_ref, g_ref, dh_ref, _, o_ref, dg_ref, cs_ref = refs[2 * n_pairs:]

        @pl.when(pl.program_id(0) == 0)
        def _():
            dg_ref[...] = jnp.zeros_like(dg_ref)
            cs_ref[...] = jnp.zeros_like(cs_ref)

        df = jnp.zeros((tm, D), F32)
        for dy_ref, w_ref, kd in zip(dy_refs, w_refs, kinds):
            if kd == "slabs":
                for j in range(N_CHIPS):
                    df = df + jnp.dot(dy_ref[j], w_ref[j], preferred_element_type=F32)
            elif kd == "quarters":
                n = w_ref.shape[2]
                for j in range(N_CHIPS):
                    df = df + _dot_nt(dy_ref[:, j * n:(j + 1) * n], w_ref[j])
            else:
                df = df + jnp.dot(dy_ref[...], w_ref[...], preferred_element_type=F32)
        hh = h_ref[...]
        r = _rms_rstd(hh)
        dg_ref[...] += jnp.sum(df * (hh * r), axis=0, keepdims=True)
        dxh = df * g_ref[...]
        out = dh_ref[...] + (r * dxh - hh * (r * r * r) * jnp.mean(dxh * hh, axis=-1, keepdims=True))
        o_ref[...] = out
        cs_ref[...] += jnp.sum(out, axis=0, keepdims=True)

    dy_specs, w_specs = [], []
    for (dy, w, layer), kd in zip(pairs, kinds):
        if kd == "slabs":
            dy_specs.append(pl.BlockSpec((N_CHIPS, tm, dy.shape[2]), lambda i: (0, i, 0)))
            w_specs.append(pl.BlockSpec((N_CHIPS, None, w.shape[2], D),
                                        functools.partial(lambda i, layer: (0, layer, 0, 0), layer=layer),
                                        pipeline_mode=pl.Buffered(1)))
        else:
            dy_specs.append(_rows(tm, dy.shape[1]))
            w_specs.append(_whole(w.shape))

    return pl.pallas_call(
        body, name=name, grid=(T // tm,),
        in_specs=[*dy_specs, *w_specs, _rows(tm, D), _whole((1, D)), _rows(tm, D), pl.BlockSpec(memory_space=pl.ANY)],
        out_specs=[_rows(tm, D), _whole((1, D)), _whole((1, D))],
        out_shape=[SDS((T, D), F32), SDS((1, D), F32), SDS((1, D), F32)],
        compiler_params=_params("arbitrary"),
    )(*[dy for dy, _, _ in pairs], *[w for _, w, _ in pairs], h, g, dh, after)


def _mm_nt(name, dy, w, out_dtype):
    T, N = dy.shape
    K = w.shape[0]
    tm = min(512, T)

    def body(dy_ref, w_ref, o_ref):
        o_ref[...] = lax.dot_general(dy_ref[...].astype(BF16), w_ref[...], (((1,), (1,)), ((), ())),
                                     preferred_element_type=F32).astype(out_dtype)

    return pl.pallas_call(
        body, name=name, grid=(T // tm,),
        in_specs=[_rows(tm, N), _whole((K, N))],
        out_specs=_rows(tm, K),
        out_shape=SDS((T, K), out_dtype),
        compiler_params=_params("parallel"),
    )(dy, w)


def _conv_bwd(dact, c, a, w_dw, ln_g, ln_b):
    T, D = c.shape
    tc = _conv_tile(T)
    per = tc // CONV_HALO
    n_tiles = T // tc
    last_halo = T // CONV_HALO - 1

    def ln_bwd(dact_v, c_v, lg, lb):
        xc = c_v - jnp.mean(c_v, axis=-1, keepdims=True)
        rstd = lax.rsqrt(jnp.mean(xc * xc, axis=-1, keepdims=True) + LN_EPS)
        z = xc * rstd
        _, dsilu = _silu_and_grad(z * lg + lb)
        dl = dact_v * dsilu
        dz = dl * lg
        dc = rstd * (dz - jnp.mean(dz, axis=-1, keepdims=True) - z * jnp.mean(dz * z, axis=-1, keepdims=True))
        return dc, dl, z

    def body(dact_ref, dactn_ref, c_ref, cn_ref, a_ref, ah_ref, w_ref, lg_ref, lb_ref,
             da_ref, dlg_ref, dlb_ref, dbdw_ref, dwdw_ref, dbpw1_ref, dc_sh, u_sh, du_scr):
        i = pl.program_id(0)

        @pl.when(i == 0)
        def _():
            for ref in (dlg_ref, dlb_ref, dbdw_ref, dwdw_ref, dbpw1_ref):
                ref[...] = jnp.zeros_like(ref)

        lg, lb = lg_ref[...], lb_ref[...]
        dc, dl, z = ln_bwd(dact_ref[...], c_ref[...], lg, lb)
        dlg_ref[...] += jnp.sum(dl * z, axis=0, keepdims=True)
        dlb_ref[...] += jnp.sum(dl, axis=0, keepdims=True)
        dbdw_ref[...] += jnp.sum(dc, axis=0, keepdims=True)
        dcn, _, _ = ln_bwd(dactn_ref[...], cn_ref[...], lg, lb)
        dc_sh[0, 0:tc, :] = dc
        dc_sh[0, tc:, :] = jnp.where(i < n_tiles - 1, dcn, 0.0)
        _fill_shifted(dc_sh, tc)

        a_v = a_ref[...]
        a1 = a_v[:, :D]
        sg = jax.nn.sigmoid(a_v[:, D:])
        u_sh[0, 0:CONV_HALO, :] = jnp.where(i > 0, _glu(ah_ref[...], D), 0.0)
        u_sh[0, CONV_HALO:, :] = a1 * sg
        _fill_shifted(u_sh, tc)

        _depthwise_taps(dc_sh, w_ref, [CONV_WIDTH - 1 - j for j in range(CONV_WIDTH)], None, du_scr, tc)
        _depthwise_tap_grads(dc_sh, u_sh, [CONV_FIRST_TAP + j for j in range(CONV_WIDTH)], dwdw_ref, tc)

        du = du_scr[...]
        da1 = du * sg
        da2 = du * a1 * sg * (1.0 - sg)
        da_ref[:, :D] = da1.astype(BF16)
        da_ref[:, D:] = da2.astype(BF16)
        dbpw1_ref[:, :D] += jnp.sum(da1, axis=0, keepdims=True)
        dbpw1_ref[:, D:] += jnp.sum(da2, axis=0, keepdims=True)

    nxt = lambda i: (jnp.minimum((i + 1) * per, last_halo), 0)
    return pl.pallas_call(
        body, name="conv_bwd", grid=(n_tiles,),
        in_specs=[_rows(tc, D), pl.BlockSpec((CONV_HALO, D), nxt),
                  _rows(tc, D), pl.BlockSpec((CONV_HALO, D), nxt),
                  _rows(tc, 2 * D),
                  pl.BlockSpec((CONV_HALO, 2 * D), lambda i: (jnp.maximum(i * per - 1, 0), 0)),
                  _whole((CONV_WIDTH, D)), _whole((1, D)), _whole((1, D))],
        out_specs=[_rows(tc, 2 * D), _whole((1, D)), _whole((1, D)), _whole((1, D)),
                   _whole((CONV_HALO, D)), _whole((1, 2 * D))],
        out_shape=[SDS((T, 2 * D), BF16), SDS((1, D), F32), SDS((1, D), F32), SDS((1, D), F32),
                   SDS((CONV_HALO, D), F32), SDS((1, 2 * D), F32)],
        scratch_shapes=[pltpu.VMEM((V7X_SUBLANES, tc + CONV_HALO, D), F32),
                        pltpu.VMEM((V7X_SUBLANES, tc + CONV_HALO, D), F32), pltpu.VMEM((tc, D), F32)],
        compiler_params=_params("arbitrary"),
    )(dact, dact, c, c, a, a, w_dw, ln_g, ln_b)


def _attn_bwd(qkv, dao, cos, sin, probs_saved, psink_saved):
    T = qkv.shape[0]
    nb = T // ATTN_BLOCK
    qw = N_Q_HEADS * HEAD_DIM
    kw = N_KV_HEADS * HEAD_DIM

    def body(q_ref, kc_ref, kp_ref, vc_ref, vp_ref, do_ref, cos_ref, sin_ref, cosp_ref, sinp_ref, probs_ref, psink_ref,
             dq_ref, dkv_ref, dsink_ref, dbq_ref, dbkv_ref, carry, prev_scr, cur_scr, dq_scr):
        n = pl.program_id(0)

        @pl.when(n == 0)
        def _():
            for ref in (dsink_ref, dbq_ref, dbkv_ref, carry):
                ref[...] = jnp.zeros_like(ref)

        @pl.when(n == nb)
        def _():
            prev_scr[...] = jnp.zeros_like(prev_scr)

        @pl.when(n < nb)
        def _():
            prev_part = _from_previous_block(N_Q_HEADS * ATTN_BLOCK)
            half = PAIRS_PER_KV * ATTN_BLOCK
            upper = _upper_lanes((ATTN_BLOCK, 2 * HEAD_DIM))
            groups = range(N_KV_HEADS)

            def nt(a, b):
                return lax.dot_general(a, b, (((1,), (1,)), ((), ())), preferred_element_type=F32)

            def kv_grad(even_rows, odd_rows, x):
                even = lax.dot_general(even_rows, x, (((0,), (0,)), ((), ())), preferred_element_type=F32)
                odd = lax.dot_general(odd_rows, x, (((0,), (0,)), ((), ())), preferred_element_type=F32)
                t = jnp.where(upper, odd, even)
                return t + _swap_lane_halves(t)

            q = [_pair_rows(q_ref, g) for g in groups]
            do = [_pair_rows(do_ref, g) for g in groups]
            k_prev = [_kv_operands(g, kp_ref[...]) for g in groups]
            k_cur = [_kv_operands(g, kc_ref[...]) for g in groups]
            v_prev = [_kv_operands(g, vp_ref[...]) for g in groups]
            v_cur = [_kv_operands(g, vc_ref[...]) for g in groups]
            probs = probs_ref[...].astype(F32)
            dp_prev = jnp.concatenate([nt(do[g], v_prev[g][i]) for g in groups for i in range(2)], axis=0)
            dp_cur = jnp.concatenate([nt(do[g], v_cur[g][i]) for g in groups for i in range(2)], axis=0)
            dp = jnp.where(prev_part, dp_prev, dp_cur)
            delta = jnp.sum(probs * dp, axis=1, keepdims=True)
            ds_prev, ds_cur = _split_folded(probs * (dp - delta) * (HEAD_DIM ** -0.5), prev_part)
            p_prev, p_cur = _split_folded(probs_ref[...], prev_part)
            for i, h in enumerate(_all_heads()):
                rows = slice(i * ATTN_BLOCK, (i + 1) * ATTN_BLOCK)
                dsink_ref[:, h:h + 1] += jnp.sum(-(psink_ref[:, h:h + 1] * delta[rows]), axis=0, keepdims=True)
            kv_grads = []
            for g in groups:
                even, odd = slice(2 * g * half, (2 * g + 1) * half), slice((2 * g + 1) * half, (2 * g + 2) * half)
                dq = (jnp.dot(ds_prev[even], k_prev[g][0], preferred_element_type=F32)
                      + jnp.dot(ds_cur[even], k_cur[g][0], preferred_element_type=F32)
                      + jnp.dot(ds_prev[odd], k_prev[g][1], preferred_element_type=F32)
                      + jnp.dot(ds_cur[odd], k_cur[g][1], preferred_element_type=F32))
                for i in range(PAIRS_PER_KV):
                    hp = g * PAIRS_PER_KV + i
                    dq_scr[:, hp * 2 * HEAD_DIM:(hp + 1) * 2 * HEAD_DIM] = dq[i * ATTN_BLOCK:(i + 1) * ATTN_BLOCK]
                kv_grads.append((kv_grad(ds_prev[even], ds_prev[odd], q[g]), kv_grad(ds_cur[even], ds_cur[odd], q[g]),
                                 kv_grad(p_prev[even], p_prev[odd], do[g]), kv_grad(p_cur[even], p_cur[odd], do[g])))
            (dkp0, dkc0, dvp0, dvc0), (dkp1, dkc1, dvp1, dvc1) = kv_grads
            prev_scr[:, :kw] = jnp.where(upper, dkp1, dkp0)
            prev_scr[:, kw:] = jnp.where(upper, dvp1, dvp0)
            cur_scr[:, :kw] = jnp.where(upper, dkc1, dkc0)
            cur_scr[:, kw:] = jnp.where(upper, dvc1, dvc0)
            dq_pre = _rope_transposed(dq_scr[...], cos_ref, sin_ref)
            dq_ref[...] = dq_pre.astype(BF16)
            dbq_ref[...] += jnp.sum(dq_pre, axis=0, keepdims=True)

        tot = carry[...] + prev_scr[...]
        dk_pre = _rope_transposed(tot[:, :kw], cosp_ref, sinp_ref)
        dkv_ref[:, :kw] = dk_pre.astype(BF16)
        dkv_ref[:, kw:] = tot[:, kw:].astype(BF16)
        dbkv_ref[:, :kw] += jnp.sum(dk_pre, axis=0, keepdims=True)
        dbkv_ref[:, kw:] += jnp.sum(tot[:, kw:], axis=0, keepdims=True)

        @pl.when(n < nb)
        def _():
            carry[...] = cur_scr[...]

    cur = lambda n: (jnp.minimum(n, nb - 1), 0)
    out_lag = lambda n: (jnp.maximum(n - 1, 0), 0)
    return pl.pallas_call(
        body, name="attn_bwd", grid=(nb + 1,),
        in_specs=[*_attn_specs(T),
                  pl.BlockSpec((ATTN_BLOCK, qw), cur),
                  pl.BlockSpec((ATTN_BLOCK, V7X_LANES), cur), pl.BlockSpec((ATTN_BLOCK, V7X_LANES), cur),
                  pl.BlockSpec((ATTN_BLOCK, V7X_LANES), out_lag), pl.BlockSpec((ATTN_BLOCK, V7X_LANES), out_lag),
                  pl.BlockSpec((None, N_Q_HEADS * ATTN_BLOCK, ATTN_BLOCK), lambda n: (jnp.minimum(n, nb - 1), 0, 0)),
                  pl.BlockSpec((ATTN_BLOCK, V7X_LANES), cur)],
        out_specs=[pl.BlockSpec((ATTN_BLOCK, qw), cur), pl.BlockSpec((ATTN_BLOCK, 2 * kw), out_lag),
                   _whole((1, N_Q_HEADS)), _whole((1, qw)), _whole((1, 2 * kw))],
        out_shape=[SDS((T, qw), BF16), SDS((T, 2 * kw), BF16),
                   SDS((1, N_Q_HEADS), F32), SDS((1, qw), F32), SDS((1, 2 * kw), F32)],
        scratch_shapes=[pltpu.VMEM((ATTN_BLOCK, 2 * kw), F32), pltpu.VMEM((ATTN_BLOCK, 2 * kw), F32),
                        pltpu.VMEM((ATTN_BLOCK, 2 * kw), F32), pltpu.VMEM((ATTN_BLOCK, qw), F32)],
        compiler_params=_params("arbitrary"),
    )(qkv, qkv, qkv, qkv, qkv, dao, cos, sin, cos, sin, probs_saved, psink_saved)


def _local_step(x, target, p, reduce_begin, reduce_send):
    T, D = x.shape
    cos, sin = _rope_tables(T)
    qw = N_Q_HEADS * HEAD_DIM
    nm, nf = p["norm_mix"], p["norm_ffn"]

    y0, qkv = _qkv_proj(x, nm[0:1], p["attn_w_qkv"], p["attn_b_qkv"], cos, sin, p["gather_started"])
    ao, attn_probs, sink_probs = _attn_fwd(qkv, p["attn_sinks"])
    h1, f0 = _mm_res("attn_out", ao, p["attn_w_o"], p["attn_b_o"], x, nf[0:1])
    p = {**p, **p["other_weights"](h1)}
    w1, w3, w2 = p["ffn_w1"], p["ffn_w3"], p["ffn_w2"]
    act0, gg0, s0 = _ffn_up("ffn0_up", f0, w1, w3, 0)
    h2 = _ffn_down("ffn0_down", s0, w2, 0, h1)
    y1, a = _pw1_proj(h2, nm[1:2], p["conv_w_pw1"], p["conv_b_pw1"])
    c, act = _conv_fwd(a, p["conv_w_dw"], p["conv_b_dw"], p["conv_ln_g"], p["conv_ln_b"])
    h3, f1 = _mm_res("conv_out", act, p["conv_w_pw2"], p["conv_b_pw2"], h2, nf[1:2])
    act1, gg1, s1 = _ffn_up("ffn1_up", f1, w1, w3, 1)
    h4 = _ffn_down("ffn1_down", s1, w2, 1, h3)
    dh4, loss, d_norm_final = _final_loss(h4, p["norm_final"], target)

    g = {}
    dg1, dg3 = _ffn_bwd_act("ffn1_bwd_act", dh4, w2, 1, act1, gg1)
    dw2_1 = _mm_tn("ffn1_dw2", s1, dh4)
    dw1_1 = _mm_tn("ffn1_dw1", dg1, f1)
    dw3_1 = _mm_tn("ffn1_dw3", dg3, f1)
    begun = reduce_begin("ffn1", {("ffn_w1", 1): dw1_1, ("ffn_w3", 1): dw3_1, ("ffn_w2", 1): dw2_1})
    dh3, dnf1, db_pw2 = _mm_nt_normbwd("ffn1_bwd_in", [(dg1, w1, 1), (dg3, w3, 1)], h3, nf[1:2], dh4, begun)
    sent = reduce_send("ffn1", dh3)

    dw_pw2 = _mm_tn("conv_dw_pw2", act, dh3, after=sent)
    dact = _mm_nt("conv_bwd_out", dh3, p["conv_w_pw2"], F32)
    da, d_ln_g, d_ln_b, d_b_dw, d_w_dw, d_b_pw1 = _conv_bwd(dact, c, a, p["conv_w_dw"], p["conv_ln_g"], p["conv_ln_b"])
    dw_pw1 = _mm_tn("conv_dw_pw1", y1, da, col_chunks=N_CHIPS)
    begun = reduce_begin("conv", {("conv_w_pw2", 0): dw_pw2.reshape(N_CHIPS, -1, D), ("conv_w_pw1", 0): dw_pw1})
    dh2, dnm1, _ = _mm_nt_normbwd("conv_bwd_in", [(da, p["conv_w_pw1"], None)], h2, nm[1:2], dh3, begun)
    sent = reduce_send("conv", dh2)

    dg1, dg3 = _ffn_bwd_act("ffn0_bwd_act", dh2, w2, 0, act0, gg0)
    dw2_0 = _mm_tn("ffn0_dw2", s0, dh2, after=sent)
    dw1_0 = _mm_tn("ffn0_dw1", dg1, f0)
    dw3_0 = _mm_tn("ffn0_dw3", dg3, f0)
    begun = reduce_begin("ffn0", {("ffn_w1", 0): dw1_0, ("ffn_w3", 0): dw3_0, ("ffn_w2", 0): dw2_0})
    dh1, dnf0, db_o = _mm_nt_normbwd("ffn0_bwd_in", [(dg1, w1, 0), (dg3, w3, 0)], h1, nf[0:1], dh2, begun)
    sent = reduce_send("ffn0", dh1)

    dw_o = _mm_tn("attn_dw_o", ao, dh1, after=sent)
    dao = _mm_nt("attn_bwd_out", dh1, p["attn_w_o"], BF16)
    dq, dkv, d_sinks, dbq, dbkv = _attn_bwd(qkv, dao, cos, sin, attn_probs, sink_probs)
    dwq = _mm_tn("attn_dw_q", dq, y0)
    dwkv = _mm_tn("attn_dw_kv", dkv, y0)
    wqkv = p["attn_w_qkv"]
    dwqkv = jnp.concatenate([dwq, dwkv], axis=0).reshape(N_CHIPS, -1, D)
    begun = reduce_begin("attn", {("attn_w_o", 0): dw_o.reshape(N_CHIPS, -1, D), ("attn_w_qkv", 0): dwqkv})
    dx, dnm0, _ = _mm_nt_normbwd("attn_bwd_in", [(dq, wqkv[:qw], None), (dkv, wqkv[qw:], None)], x, nm[0:1], dh1,
                                 begun)
    reduce_send("attn", dx)

    g["norm_mix"] = jnp.concatenate([dnm0, dnm1], axis=0)
    g["norm_ffn"] = jnp.concatenate([dnf0, dnf1], axis=0)
    g["attn_b_qkv"] = jnp.concatenate([dbq, dbkv], axis=1)
    g["attn_sinks"] = d_sinks
    g["attn_b_o"] = db_o
    g["conv_b_pw1"] = d_b_pw1
    g["conv_w_dw"] = d_w_dw[:CONV_WIDTH]
    g["conv_b_dw"] = d_b_dw
    g["conv_ln_g"] = d_ln_g
    g["conv_ln_b"] = d_ln_b
    g["conv_b_pw2"] = db_pw2
    g["norm_final"] = d_norm_final
    return loss, dx, g


ANY = pl.BlockSpec(memory_space=pl.ANY)
VMEM_WHOLE = pl.BlockSpec(memory_space=pltpu.VMEM)


def _my_place():
    return lax.axis_index("x"), lax.axis_index("y"), lax.axis_index("c")


def _other_chips(x, y):
    places = [(1 - x, y), (x, 1 - y), (1 - x, 1 - y)]
    return [(bx, by, 2 * bx + by) for bx, by in places]


def _exchange_small(name, v, reduce):
    r, w = v.shape

    def body(v_ref, o_ref, *rest):
        if reduce:
            buf, send_sems, recv_sems = rest
        else:
            buf = o_ref
            send_sems, recv_sems = rest
        x, y, c = _my_place()
        me = 4 * x + 2 * y + c
        sends = []
        for k in range(1, N_DEV):
            peer = (1 - x if k & 4 else x, 1 - y if k & 2 else y, 1 - c if k & 1 else c)
            cp = pltpu.make_async_remote_copy(
                src_ref=v_ref, dst_ref=buf.at[me], send_sem=send_sems.at[k - 1], recv_sem=recv_sems.at[k - 1],
                device_id=peer, device_id_type=MESH)
            cp.start()
            sends.append(cp)
        buf[me] = v_ref[...]
        for k in range(1, N_DEV):
            src = 4 * (1 - x if k & 4 else x) + 2 * (1 - y if k & 2 else y) + (1 - c if k & 1 else c)
            pltpu.make_async_remote_copy(
                src_ref=v_ref, dst_ref=buf.at[src], send_sem=send_sems.at[k - 1], recv_sem=recv_sems.at[k - 1],
                device_id=(x, y, c), device_id_type=MESH).wait_recv()
        for cp in sends:
            cp.wait_send()
        if reduce:
            acc = buf[0]
            for d in range(1, N_DEV):
                acc = acc + buf[d]
            o_ref[...] = acc

    sems = [pltpu.SemaphoreType.DMA((N_DEV - 1,)), pltpu.SemaphoreType.DMA((N_DEV - 1,))]
    if reduce:
        out_shape = SDS((r, w), F32)
        scratch = [pltpu.VMEM((N_DEV, r, w), F32)] + sems
    else:
        out_shape = SDS((N_DEV, r, w), F32)
        scratch = sems
    return pl.pallas_call(
        body, name=name, out_shape=out_shape, in_specs=[VMEM_WHOLE], out_specs=VMEM_WHOLE,
        scratch_shapes=scratch,
        compiler_params=pltpu.CompilerParams(vmem_limit_bytes=V7X_VMEM_LIMIT_BYTES),
    )(v)


def _cast_into_slot(name, gathered, shard, chip_idx):
    rows, cols = shard.shape
    tr = _pack_row_tile(rows)

    def body(k_ref, s_ref, g_ref, o_ref):
        o_ref[...] = s_ref[...].astype(BF16)

    return pl.pallas_call(
        body, name=name,
        grid_spec=pltpu.PrefetchScalarGridSpec(
            num_scalar_prefetch=1, grid=(rows // tr,),
            in_specs=[pl.BlockSpec((tr, cols), lambda i, k_ref: (i, 0)), pl.BlockSpec(memory_space=pl.ANY)],
            out_specs=pl.BlockSpec((None, tr, cols), lambda i, k_ref: (k_ref[0], i, 0))),
        out_shape=SDS(gathered.shape, BF16),
        input_output_aliases={2: 0},
        compiler_params=_params("parallel"),
    )(chip_idx, shard, gathered)


def _row_halves(ref, c):
    half = ref.shape[1] // 2
    return pl.ds(pl.multiple_of(c * half, 16), half), pl.ds(pl.multiple_of((1 - c) * half, 16), half)


def _gather_ici_copies(refs, send_sems, recv_sems):
    x, y, c = _my_place()
    k = 2 * x + y
    pairs = []
    for i, ref in enumerate(refs):
        mine, _ = _row_halves(ref, c)
        for j, (bx, by, kb) in enumerate(_other_chips(x, y)):
            sems = dict(send_sem=send_sems.at[3 * i + j], recv_sem=recv_sems.at[3 * i + j], device_id_type=MESH)
            send = pltpu.make_async_remote_copy(src_ref=ref.at[k, mine], dst_ref=ref.at[k, mine],
                                                device_id=(bx, by, c), **sems)
            arrival = pltpu.make_async_remote_copy(src_ref=ref.at[kb, mine], dst_ref=ref.at[kb, mine],
                                                   device_id=(bx, by, c), **sems)
            pairs.append((send, arrival))
    return pairs


def _gather_d2d_copies(refs, send_sems, recv_sems, first_sem):
    x, y, c = _my_place()
    pairs = []
    for i, ref in enumerate(refs):
        mine, theirs = _row_halves(ref, c)
        for j, (_, _, kb) in enumerate(_other_chips(x, y)):
            sem = first_sem + 3 * i + j
            sems = dict(send_sem=send_sems.at[sem], recv_sem=recv_sems.at[sem], device_id=(x, y, 1 - c),
                        device_id_type=MESH)
            send = pltpu.make_async_remote_copy(src_ref=ref.at[kb, mine], dst_ref=ref.at[kb, mine], **sems)
            arrival = pltpu.make_async_remote_copy(src_ref=ref.at[kb, theirs], dst_ref=ref.at[kb, theirs], **sems)
            pairs.append((send, arrival))
    return pairs


def _run_copies(pairs):
    for send, _ in pairs:
        send.start()
    for send, arrival in pairs:
        send.wait_send()
        arrival.wait_recv()


def _gather_now(name, gathered):
    n_w = len(gathered)

    def body(*refs):
        in_refs = refs[:n_w]
        send_sems, recv_sems = refs[2 * n_w:]
        _run_copies(_gather_ici_copies(in_refs, send_sems, recv_sems))
        _run_copies(_gather_d2d_copies(in_refs, send_sems, recv_sems, 3 * n_w))

    return pl.pallas_call(
        body, name=name, out_shape=[SDS(g.shape, g.dtype) for g in gathered],
        in_specs=[ANY] * n_w, out_specs=[ANY] * n_w, input_output_aliases={i: i for i in range(n_w)},
        scratch_shapes=[pltpu.SemaphoreType.DMA((6 * n_w,)), pltpu.SemaphoreType.DMA((6 * n_w,))],
    )(*gathered)


def _gather_start(name, gathered, after):
    n_w = len(gathered)

    def body(*refs):
        in_refs = refs[:n_w]
        send_sems, recv_sems = refs[n_w + 1:n_w + 3]
        for send, _ in _gather_ici_copies(in_refs, send_sems, recv_sems):
            send.start()
        refs[-1][...] = jnp.zeros_like(refs[-1])

    out = pl.pallas_call(
        body, name=name,
        out_shape=(pltpu.SemaphoreType.DMA((3 * n_w,)), pltpu.SemaphoreType.DMA((3 * n_w,)),
                   *[pltpu.HBM(g.shape, g.dtype) for g in gathered], SDS((8, V7X_LANES), F32)),
        in_specs=[*[HBM_SPEC] * n_w, ANY], out_specs=(SEM_SPEC, SEM_SPEC, *[HBM_SPEC] * n_w, VMEM_WHOLE),
        input_output_aliases={i: 2 + i for i in range(n_w)},
        compiler_params=pltpu.CompilerParams(has_side_effects=DATAFLOW),
    )(*[pltpu.with_memory_space_constraint(g, pltpu.HBM) for g in gathered], after)
    return out[0], out[1], list(out[2:2 + n_w]), out[-1]


def _gather_wait(name, send_sems, recv_sems, gathered, after):
    n_w = len(gathered)

    def body(*refs):
        in_refs = refs[:n_w]
        send_sems, recv_sems = refs[n_w:n_w + 2]
        for send, arrival in _gather_ici_copies(in_refs, send_sems, recv_sems):
            send.wait_send()
            arrival.wait_recv()

    out = pl.pallas_call(
        body, name=name, out_shape=tuple(pltpu.HBM(g.shape, g.dtype) for g in gathered),
        in_specs=[*[HBM_SPEC] * n_w, SEM_SPEC, SEM_SPEC, ANY], out_specs=tuple([HBM_SPEC] * n_w),
        input_output_aliases={i: i for i in range(n_w)},
        compiler_params=pltpu.CompilerParams(has_side_effects=DATAFLOW),
    )(*gathered, send_sems, recv_sems, after)
    return list(out)


def _swap_fetched_with_sibling(name, gathered):
    n_w = len(gathered)

    def body(*refs):
        in_refs = refs[:n_w]
        send_sems, recv_sems = refs[2 * n_w:]
        _run_copies(_gather_d2d_copies(in_refs, send_sems, recv_sems, 0))

    return pl.pallas_call(
        body, name=name, out_shape=[SDS(g.shape, g.dtype) for g in gathered],
        in_specs=[ANY] * n_w, out_specs=[ANY] * n_w, input_output_aliases={i: i for i in range(n_w)},
        scratch_shapes=[pltpu.SemaphoreType.DMA((3 * n_w,)), pltpu.SemaphoreType.DMA((3 * n_w,))],
    )(*gathered)


def _sibling_swap_copies(g_refs, land_refs, send_sems, recv_sems):
    x, y, c = _my_place()
    copies = []
    for i, g_ref in enumerate(g_refs):
        half = g_ref.shape[1] // 2
        theirs = pl.ds(pl.multiple_of((1 - c) * half, 8), half)
        copies.append(pltpu.make_async_remote_copy(
            src_ref=g_ref.at[:, theirs], dst_ref=land_refs[i], send_sem=send_sems.at[i], recv_sem=recv_sems.at[i],
            device_id=(x, y, 1 - c), device_id_type=MESH))
    return copies


def _sibling_swap_start(name, grads):
    n_g = len(grads)

    def body(*refs):
        g_refs, land_refs = refs[:n_g], refs[n_g:2 * n_g]
        send_sems, recv_sems = refs[2 * n_g:2 * n_g + 2]
        for cp in _sibling_swap_copies(g_refs, land_refs, send_sems, recv_sems):
            cp.start()
        refs[-1][...] = jnp.zeros_like(refs[-1])

    lands = [pltpu.with_memory_space_constraint(lax.empty((g.shape[0], g.shape[1] // 2, g.shape[2]), g.dtype),
                                                pltpu.HBM) for g in grads]
    out = pl.pallas_call(
        body, name=name,
        out_shape=(pltpu.SemaphoreType.DMA((n_g,)), pltpu.SemaphoreType.DMA((n_g,)),
                   *[pltpu.HBM(g.shape, g.dtype) for g in grads], *[pltpu.HBM(l.shape, l.dtype) for l in lands],
                   SDS((8, V7X_LANES), F32)),
        in_specs=[HBM_SPEC] * (2 * n_g), out_specs=(SEM_SPEC, SEM_SPEC, *[HBM_SPEC] * (2 * n_g), VMEM_WHOLE),
        input_output_aliases={i: 2 + i for i in range(2 * n_g)},
        compiler_params=pltpu.CompilerParams(has_side_effects=DATAFLOW),
    )(*[pltpu.with_memory_space_constraint(g, pltpu.HBM) for g in grads], *lands)
    return out[0], out[1], list(out[2:2 + n_g]), list(out[2 + n_g:2 + 2 * n_g]), out[-1]


def _sibling_swap_wait(name, send_sems, recv_sems, grads, lands, after):
    n_g = len(grads)

    def body(*refs):
        g_refs, land_refs = refs[:n_g], refs[n_g:2 * n_g]
        send_sems, recv_sems = refs[2 * n_g:2 * n_g + 2]
        for cp in _sibling_swap_copies(g_refs, land_refs, send_sems, recv_sems):
            cp.wait_send()
            cp.wait_recv()

    out = pl.pallas_call(
        body, name=name,
        out_shape=(*[pltpu.HBM(g.shape, g.dtype) for g in grads], *[pltpu.HBM(l.shape, l.dtype) for l in lands]),
        in_specs=[*[HBM_SPEC] * (2 * n_g), SEM_SPEC, SEM_SPEC, ANY], out_specs=tuple([HBM_SPEC] * (2 * n_g)),
        input_output_aliases={i: i for i in range(2 * n_g)},
        compiler_params=pltpu.CompilerParams(has_side_effects=DATAFLOW),
    )(*grads, *lands, send_sems, recv_sems, after)
    return list(out[:n_g]), list(out[n_g:])


def _pack_row_tile(rows):
    for t in range(min(rows, 512), 7, -1):
        if rows % t == 0 and t % 8 == 0:
            return t
    return rows


def _add_sibling_half(name, grads, from_sibling, c_idx):
    n, R, w = grads.shape
    half = R // 2
    tr = _pack_row_tile(half)
    steps = half // tr

    def body(c_ref, g_ref, s_ref, o_ref):
        o_ref[...] = (g_ref[...] + s_ref[...]).astype(BF16)

    return pl.pallas_call(
        body, name=name,
        grid_spec=pltpu.PrefetchScalarGridSpec(
            num_scalar_prefetch=1, grid=(n, steps),
            in_specs=[pl.BlockSpec((1, tr, w), lambda j, i, c_ref: (j, c_ref[0] * steps + i, 0)),
                      pl.BlockSpec((1, tr, w), lambda j, i, c_ref: (j, i, 0))],
            out_specs=pl.BlockSpec((1, tr, w), lambda j, i, c_ref: (j, i, 0))),
        out_shape=SDS((n, half, w), BF16),
        compiler_params=_params("parallel", "parallel"),
    )(c_idx, grads, from_sibling)


HBM_SPEC = pl.BlockSpec(memory_space=pltpu.HBM)
SEM_SPEC = pl.BlockSpec(memory_space=pltpu.SEMAPHORE)
DATAFLOW = pltpu.SideEffectType.DATAFLOW_SIDE_EFFECTING


def _chip_scatter_copies(p_refs, land_refs, send_sems, recv_sems):
    x, y, c = _my_place()
    return [pltpu.make_async_remote_copy(
        src_ref=p_refs[i].at[kb], dst_ref=land_refs[i].at[j], send_sem=send_sems.at[3 * i + j],
        recv_sem=recv_sems.at[3 * i + j], device_id=(bx, by, c), device_id_type=MESH)
        for i in range(len(p_refs)) for j, (bx, by, kb) in enumerate(_other_chips(x, y))]


def _scatter_start(name, partials):
    n_p = len(partials)

    def body(*refs):
        p_refs, land_refs = refs[:n_p], refs[n_p:2 * n_p]
        send_sems, recv_sems = refs[2 * n_p:2 * n_p + 2]
        for cp in _chip_scatter_copies(p_refs, land_refs, send_sems, recv_sems):
            cp.start()
        refs[-1][...] = jnp.zeros_like(refs[-1])

    lands = [pltpu.with_memory_space_constraint(lax.empty((N_CHIPS - 1,) + p.shape[1:], p.dtype), pltpu.HBM)
             for p in partials]
    out = pl.pallas_call(
        body, name=name,
        out_shape=(pltpu.SemaphoreType.DMA((3 * n_p,)), pltpu.SemaphoreType.DMA((3 * n_p,)),
                   *[pltpu.HBM(p.shape, p.dtype) for p in partials], *[pltpu.HBM(l.shape, l.dtype) for l in lands],
                   SDS((8, V7X_LANES), F32)),
        in_specs=[HBM_SPEC] * (2 * n_p), out_specs=(SEM_SPEC, SEM_SPEC, *[HBM_SPEC] * (2 * n_p), VMEM_WHOLE),
        input_output_aliases={i: 2 + i for i in range(2 * n_p)},
        compiler_params=pltpu.CompilerParams(has_side_effects=DATAFLOW),
    )(*[pltpu.with_memory_space_constraint(p, pltpu.HBM) for p in partials], *lands)
    return out[0], out[1], list(out[2:2 + n_p]), list(out[2 + n_p:2 + 2 * n_p]), out[-1]


def _scatter_wait(name, send_sems, recv_sems, partials, lands, after):
    n_p = len(partials)

    def body(*refs):
        p_refs, land_refs = refs[:n_p], refs[n_p:2 * n_p]
        send_sems, recv_sems = refs[2 * n_p:2 * n_p + 2]
        for cp in _chip_scatter_copies(p_refs, land_refs, send_sems, recv_sems):
            cp.wait_send()
            cp.wait_recv()

    out = pl.pallas_call(
        body, name=name,
        out_shape=(*[pltpu.HBM(p.shape, p.dtype) for p in partials], *[pltpu.HBM(l.shape, l.dtype) for l in lands]),
        in_specs=[*[HBM_SPEC] * (2 * n_p), SEM_SPEC, SEM_SPEC, ANY], out_specs=tuple([HBM_SPEC] * (2 * n_p)),
        input_output_aliases={i: i for i in range(2 * n_p)},
        compiler_params=pltpu.CompilerParams(has_side_effects=DATAFLOW),
    )(*partials, *lands, send_sems, recv_sems, after)
    return list(out[:n_p]), list(out[n_p:])


def _sum_chip_partials(name, grads, from_sibling, received, shard, layer, place):
    n, half, w = from_sibling.shape
    tr = _pack_row_tile(half)
    steps = half // tr

    def body(place_ref, g_ref, s_ref, r_ref, shard_ref, o_ref):
        own = g_ref[0] + s_ref[0]
        o_ref[...] = ((own + r_ref[0].astype(F32)) + r_ref[1].astype(F32)) + r_ref[2].astype(F32)

    return pl.pallas_call(
        body, name=name,
        grid_spec=pltpu.PrefetchScalarGridSpec(
            num_scalar_prefetch=1, grid=(steps,),
            in_specs=[pl.BlockSpec((1, tr, w), lambda i, place_ref: (place_ref[0], place_ref[1] * steps + i, 0)),
                      pl.BlockSpec((1, tr, w), lambda i, place_ref: (place_ref[0], i, 0)),
                      pl.BlockSpec((n - 1, tr, w), lambda i, place_ref: (0, i, 0)),
                      pl.BlockSpec(memory_space=pl.ANY)],
            out_specs=pl.BlockSpec((tr, w), lambda i, place_ref: ((2 * layer + place_ref[1]) * steps + i, 0))),
        out_shape=SDS(shard.shape, F32),
        input_output_aliases={4: 0},
        compiler_params=_params("parallel"),
    )(place, grads, from_sibling, received, shard)


def _join_halves(shards, layers):
    n_s = len(shards)
    n_sem = sum(layers)

    def body(*refs):
        in_refs = refs[:n_s]
        send_sems, recv_sems = refs[2 * n_s:]
        x, y, c = _my_place()
        copies, sem = [], 0
        for ref, n_layers in zip(in_refs, layers):
            half = ref.shape[0] // (2 * n_layers)
            for layer in range(n_layers):
                mine = pl.ds(pl.multiple_of(layer * 2 * half + c * half, 8), half)
                theirs = pl.ds(pl.multiple_of(layer * 2 * half + (1 - c) * half, 8), half)
                send = pltpu.make_async_remote_copy(
                    src_ref=ref.at[mine], dst_ref=ref.at[mine], send_sem=send_sems.at[sem], recv_sem=recv_sems.at[sem],
                    device_id=(x, y, 1 - c), device_id_type=MESH)
                send.start()
                arrival = pltpu.make_async_remote_copy(
                    src_ref=ref.at[theirs], dst_ref=ref.at[theirs], send_sem=send_sems.at[sem],
                    recv_sem=recv_sems.at[sem], device_id=(x, y, 1 - c), device_id_type=MESH)
                copies.append((send, arrival))
                sem += 1
        for send, arrival in copies:
            send.wait_send()
            arrival.wait_recv()

    return pl.pallas_call(
        body, name="join_halves", out_shape=[SDS(s.shape, s.dtype) for s in shards],
        in_specs=[ANY] * n_s, out_specs=[ANY] * n_s,
        input_output_aliases={i: i for i in range(n_s)},
        scratch_shapes=[pltpu.SemaphoreType.DMA((n_sem,)), pltpu.SemaphoreType.DMA((n_sem,))],
    )(*shards)


def _adamw(name, w, g, m, v):
    rows, width = w.shape
    tr = _pack_row_tile(rows)

    def body(w_ref, g_ref, m_ref, v_ref, g_out_ref, d_ref, nm_ref, nv_ref):
        gg = g_ref[...]
        g_out_ref[...] = gg
        m_new = ADAM_B1 * m_ref[...] + (1.0 - ADAM_B1) * gg
        v_new = ADAM_B2 * v_ref[...] + (1.0 - ADAM_B2) * (gg * gg)
        m_hat = m_new / (1.0 - ADAM_B1 ** ADAM_STEP)
        v_hat = v_new / (1.0 - ADAM_B2 ** ADAM_STEP)
        d_ref[...] = -ADAM_LR * (m_hat / (jnp.sqrt(v_hat) + ADAM_EPS) + ADAM_WD * w_ref[...])
        nm_ref[...] = m_new
        nv_ref[...] = v_new

    spec = _rows(tr, width)
    return pl.pallas_call(
        body, name=name, grid=(rows // tr,),
        in_specs=[spec] * 4, out_specs=[spec] * 4,
        out_shape=[SDS((rows, width), F32)] * 4,
        compiler_params=_params("parallel"),
    )(w, g, m, v)


WEIGHT_NAMES = ['norm_mix', 'norm_ffn', 'attn_w_qkv', 'attn_b_qkv', 'attn_sinks', 'attn_w_o', 'attn_b_o',
                'conv_w_pw1', 'conv_b_pw1', 'conv_w_dw', 'conv_b_dw', 'conv_ln_g', 'conv_ln_b', 'conv_w_pw2',
                'conv_b_pw2', 'ffn_w1', 'ffn_w3', 'ffn_w2', 'norm_final']
BIG = ['attn_w_qkv', 'attn_w_o', 'conv_w_pw1', 'conv_w_pw2', 'ffn_w1', 'ffn_w3', 'ffn_w2']
COLUMN_SPLIT = ('attn_w_qkv', 'conv_w_pw1', 'ffn_w1', 'ffn_w3')
STORED_TRANSPOSED = ('attn_w_qkv', 'ffn_w1', 'ffn_w3')
SMALL_SPLIT = ['conv_b_pw1', 'conv_w_dw', 'conv_b_dw', 'conv_ln_g', 'conv_ln_b', 'conv_b_pw2']
SMALL_WHOLE = ['norm_mix', 'norm_ffn', 'attn_b_qkv', 'attn_sinks', 'attn_b_o', 'norm_final']


def _keeps_rows(shape):
    return len(shape) == 2 and shape[0] > 1 and shape[1] == PACK_W


def _pack_rows(arrays, dtype, row_multiple):
    blocks = [jnp.pad(a.astype(dtype), ((0, -a.shape[0] % V7X_SUBLANES), (0, 0)))
              for a in arrays if _keeps_rows(a.shape)]
    flat = jnp.concatenate([a.astype(dtype).reshape(-1) for a in arrays if not _keeps_rows(a.shape)])
    multiple = max(row_multiple, V7X_SUBLANES)
    rows = -(-(-(-flat.shape[0] // PACK_W)) // multiple) * multiple
    blocks.append(jnp.pad(flat, (0, rows * PACK_W - flat.shape[0])).reshape(rows, PACK_W))
    return jnp.concatenate(blocks, axis=0) if len(blocks) > 1 else blocks[0]


def _unpack_rows(pack, shapes):
    out, row = {}, 0
    for i, shape in enumerate(shapes):
        if _keeps_rows(shape):
            out[i] = pack[row:row + shape[0]]
            row += -(-shape[0] // V7X_SUBLANES) * V7X_SUBLANES
    flat, at = pack[row:].reshape(-1), 0
    for i, shape in enumerate(shapes):
        if not _keeps_rows(shape):
            size = 1
            for s in shape:
                size *= s
            out[i] = flat[at:at + size].reshape(shape)
            at += size
    return [out[i] for i in range(len(shapes))]


def _join_chip_axis(name, parts):
    axis = parts.ndim - 1 if name in COLUMN_SPLIT or name in SMALL_SPLIT else parts.ndim - 2
    moved = jnp.moveaxis(parts, 0, axis - 1)
    shape = list(moved.shape)
    shape[axis - 1:axis + 1] = [shape[axis - 1] * shape[axis]]
    return moved.reshape(shape)


def _split_chip_axis(name, whole, shard_shape):
    axis = len(shard_shape) - 1 if name in COLUMN_SPLIT or name in SMALL_SPLIT else len(shard_shape) - 2
    shape = list(whole.shape)
    shape[axis:axis + 1] = [N_CHIPS, shard_shape[axis]]
    return jnp.moveaxis(whole.reshape(shape), axis, 0)


def kernel(x, norm_mix, norm_ffn, attn_w_qkv, attn_b_qkv, attn_sinks, attn_w_o, attn_b_o, conv_w_pw1, conv_b_pw1, conv_w_dw, conv_b_dw, conv_ln_g, conv_ln_b, conv_w_pw2, conv_b_pw2, ffn_w1, ffn_w3, ffn_w2, norm_final, loss_target, m_norm_mix, m_norm_ffn, m_attn_w_qkv, m_attn_b_qkv, m_attn_sinks, m_attn_w_o, m_attn_b_o, m_conv_w_pw1, m_conv_b_pw1, m_conv_w_dw, m_conv_b_dw, m_conv_ln_g, m_conv_ln_b, m_conv_w_pw2, m_conv_b_pw2, m_ffn_w1, m_ffn_w3, m_ffn_w2, m_norm_final, v_norm_mix, v_norm_ffn, v_attn_w_qkv, v_attn_b_qkv, v_attn_sinks, v_attn_w_o, v_attn_b_o, v_conv_w_pw1, v_conv_b_pw1, v_conv_w_dw, v_conv_b_dw, v_conv_ln_g, v_conv_ln_b, v_conv_w_pw2, v_conv_b_pw2, v_ffn_w1, v_ffn_w3, v_ffn_w2, v_norm_final):
    w = dict(zip(WEIGHT_NAMES, (norm_mix, norm_ffn, attn_w_qkv, attn_b_qkv, attn_sinks, attn_w_o, attn_b_o,
                                conv_w_pw1, conv_b_pw1, conv_w_dw, conv_b_dw, conv_ln_g, conv_ln_b, conv_w_pw2,
                                conv_b_pw2, ffn_w1, ffn_w3, ffn_w2, norm_final)))
    m = dict(zip(WEIGHT_NAMES, (m_norm_mix, m_norm_ffn, m_attn_w_qkv, m_attn_b_qkv, m_attn_sinks, m_attn_w_o,
                                m_attn_b_o, m_conv_w_pw1, m_conv_b_pw1, m_conv_w_dw, m_conv_b_dw, m_conv_ln_g,
                                m_conv_ln_b, m_conv_w_pw2, m_conv_b_pw2, m_ffn_w1, m_ffn_w3, m_ffn_w2, m_norm_final)))
    v = dict(zip(WEIGHT_NAMES, (v_norm_mix, v_norm_ffn, v_attn_w_qkv, v_attn_b_qkv, v_attn_sinks, v_attn_w_o,
                                v_attn_b_o, v_conv_w_pw1, v_conv_b_pw1, v_conv_w_dw, v_conv_b_dw, v_conv_ln_g,
                                v_conv_ln_b, v_conv_w_pw2, v_conv_b_pw2, v_ffn_w1, v_ffn_w3, v_ffn_w2, v_norm_final)))
    T, D = x.shape[1], x.shape[2]
    c_idx = lax.axis_index("c").astype(jnp.int32).reshape(1)
    chip = (2 * lax.axis_index("x") + lax.axis_index("y")).astype(jnp.int32)

    def as_rows(n, a):
        a = jnp.swapaxes(a, -1, -2) if n in STORED_TRANSPOSED else a
        return a.reshape(-1, a.shape[-1])

    def from_rows(n, rows):
        shape = w[n].shape[:-2] + w[n].shape[:-3:-1] if n in STORED_TRANSPOSED else w[n].shape
        a = rows.reshape(shape)
        return jnp.swapaxes(a, -1, -2) if n in STORED_TRANSPOSED else a

    slabs = {n: _cast_into_slot(f"cast_{n}", lax.empty((N_CHIPS,) + as_rows(n, w[n]).shape, BF16), as_rows(n, w[n]),
                                chip.reshape(1)) for n in BIG}
    first, later = BIG[:2], BIG[2:]
    qkv_parts, w_o_parts = _gather_now("gather_attn", [slabs[n] for n in first])
    send_sems, recv_sems, travelling, gather_started = _gather_start("gather_start", [slabs[n] for n in later],
                                                                     qkv_parts)
    layers = ffn_w1.shape[0]
    small_shapes = [w[n].shape for n in SMALL_SPLIT]
    small_all = _exchange_small("gather_small", _pack_rows([w[n] for n in SMALL_SPLIT], F32, 8), reduce=False)
    per_chip = [_unpack_rows(small_all[2 * j], small_shapes) for j in range(N_CHIPS)]
    full = {}
    for i, n in enumerate(SMALL_SPLIT):
        full[n] = _join_chip_axis(n, jnp.stack([per_chip[j][i] for j in range(N_CHIPS)]))

    def other_weights(after):
        landed = _gather_wait("gather_wait", send_sems, recv_sems, travelling, after)
        gathered = dict(zip(later, _swap_fetched_with_sibling("gather_swap", landed)))
        return {"conv_w_pw1": gathered["conv_w_pw1"], "conv_w_pw2": gathered["conv_w_pw2"].reshape(-1, D),
                "ffn_w1": gathered["ffn_w1"].reshape(N_CHIPS, layers, -1, D),
                "ffn_w3": gathered["ffn_w3"].reshape(N_CHIPS, layers, -1, D),
                "ffn_w2": gathered["ffn_w2"].reshape(N_CHIPS, layers, -1, D)}

    p = {
        "norm_mix": norm_mix, "norm_ffn": norm_ffn, "norm_final": norm_final.reshape(1, D),
        "attn_w_qkv": qkv_parts.reshape(-1, D), "attn_b_qkv": attn_b_qkv,
        "attn_sinks": attn_sinks, "attn_w_o": w_o_parts.reshape(-1, D), "attn_b_o": attn_b_o,
        "conv_b_pw1": full["conv_b_pw1"], "conv_w_dw": full["conv_w_dw"][0],
        "conv_b_dw": full["conv_b_dw"], "conv_ln_g": full["conv_ln_g"], "conv_ln_b": full["conv_ln_b"],
        "conv_b_pw2": full["conv_b_pw2"], "gather_started": gather_started, "other_weights": other_weights,
    }
    swapping, in_flight = {}, []

    def reduce_begin(tag, grads):
        keys = list(grads)
        *handles, begun = _sibling_swap_start(f"sibling_swap_start_{tag}", [grads[k] for k in keys])
        swapping[tag] = (keys, handles)
        return begun

    def reduce_send(tag, after):
        keys, (swap_send, swap_recv, grads, lands) = swapping[tag]
        grads, from_sibling = _sibling_swap_wait(f"sibling_swap_wait_{tag}", swap_send, swap_recv, grads, lands, after)
        partials = [_add_sibling_half(f"add_sibling_half_{tag}{i}", gr, fs, c_idx)
                    for i, (gr, fs) in enumerate(zip(grads, from_sibling))]
        *handles, sent = _scatter_start(f"scatter_start_{tag}", partials)
        in_flight.append((tag, keys, handles, grads, from_sibling))
        return sent

    loss_part, dx, g = _local_step(x[0], loss_target[0], p, reduce_begin, reduce_send)
    for n in SMALL_WHOLE + SMALL_SPLIT:
        g[n] = g[n].reshape((-1,) + g[n].shape[-2:]) if w[n].ndim == 3 else g[n].reshape(w[n].shape[:-1] + (-1,))

    place = jnp.stack([chip, c_idx[0]])
    shard_grad = {n: lax.empty(as_rows(n, w[n]).shape, F32) for n in BIG}
    for tag, keys, (send_sems, recv_sems, partials, lands), grads, from_sibling in in_flight:
        _, received = _scatter_wait(f"scatter_wait_{tag}", send_sems, recv_sems, partials, lands, dx)
        for i, (n, layer) in enumerate(keys):
            shard_grad[n] = _sum_chip_partials(f"sum_chip_partials_{tag}{i}", grads[i], from_sibling[i], received[i],
                                               shard_grad[n], layer, place)
    g_big = dict(zip(BIG, _join_halves([shard_grad[n] for n in BIG], [w[n].shape[0] for n in BIG])))
    big_out = {}
    for n in BIG:
        step = _adamw(f"adamw_{n}", as_rows(n, w[n]), g_big[n], as_rows(n, m[n]), as_rows(n, v[n]))
        big_out[n] = [from_rows(n, a) for a in step]

    small_whole_shapes = [w[n].shape for n in SMALL_WHOLE]
    small_full_shapes = [g[n].shape for n in SMALL_SPLIT]
    reduced = _exchange_small(
        "reduce_small", _pack_rows([loss_part] + [g[n] for n in SMALL_WHOLE] + [g[n] for n in SMALL_SPLIT], F32, 8),
        reduce=True)
    pieces = _unpack_rows(reduced, [(1,)] + small_whole_shapes + small_full_shapes)
    loss = pieces[0].reshape(())
    g_small = dict(zip(SMALL_WHOLE, pieces[1:1 + len(SMALL_WHOLE)]))
    for n, whole in zip(SMALL_SPLIT, pieces[1 + len(SMALL_WHOLE):]):
        parts = _split_chip_axis(n, whole, w[n].shape)
        g_small[n] = lax.dynamic_index_in_dim(parts, chip, axis=0, keepdims=False)
    small = SMALL_WHOLE + SMALL_SPLIT
    _, d_small, m_small, v_small = _adamw(
        "adamw_small", _pack_rows([w[n] for n in small], F32, 8), _pack_rows([g_small[n] for n in small], F32, 8),
        _pack_rows([m[n] for n in small], F32, 8), _pack_rows([v[n] for n in small], F32, 8))

    outs = {}
    for slot, (tag, small_pack) in enumerate((("g", None), ("d", d_small), ("m", m_small), ("v", v_small))):
        vals = {n: big_out[n][slot] for n in BIG}
        if small_pack is None:
            vals.update(g_small)
        else:
            vals.update(zip(small, _unpack_rows(small_pack, [w[n].shape for n in small])))
        outs[tag] = vals
    return (loss, dx.reshape(1, T, D), *[outs["g"][n] for n in WEIGHT_NAMES], *[outs["d"][n] for n in WEIGHT_NAMES],
            *[outs["m"][n] for n in WEIGHT_NAMES], *[outs["v"][n] for n in WEIGHT_NAMES])
```

```python
import functools

import jax
import jax.numpy as jnp
from jax import lax
from jax.experimental import pallas as pl
from jax.experimental.pallas import tpu as pltpu

F32 = jnp.float32
BF16 = jnp.bfloat16
SDS = jax.ShapeDtypeStruct
MESH = pl.DeviceIdType.MESH

HEAD_DIM = 64
N_Q_HEADS = 16
N_KV_HEADS = 2
Q_PER_KV = N_Q_HEADS // N_KV_HEADS
ATTN_BLOCK = 128
ROPE_THETA = 10000.0
CONV_WIDTH = 31
CONV_HALO = 32
CONV_FIRST_TAP = CONV_HALO - CONV_WIDTH + 1
CONV_ROW_CHUNK = 64
CONV_LANE_CHUNK = 256
CONV_GRAD_UNROLL = 8
RMS_EPS = 1e-5
LN_EPS = 1e-5
ADAM_LR = 0.001
ADAM_B1 = 0.9
ADAM_B2 = 0.999
ADAM_EPS = 1e-08
ADAM_WD = 0.01
ADAM_STEP = 10

V7X_LANES = 128
V7X_SUBLANES = 8
V7X_VMEM_LIMIT_BYTES = 56 * 1024 * 1024

N_CHIPS = 4
N_DEV = 8
PACK_W = 1024

MASK_VALUE = -1e30


def _params(*semantics):
    return pltpu.CompilerParams(dimension_semantics=semantics, vmem_limit_bytes=V7X_VMEM_LIMIT_BYTES)


def _rows(tm, width):
    return pl.BlockSpec((tm, width), lambda i: (i, 0))


def _whole(shape):
    return pl.BlockSpec(shape, lambda *_: (0,) * len(shape))


def _rms_rstd(h):
    return lax.rsqrt(jnp.mean(h * h, axis=-1, keepdims=True) + RMS_EPS)


def _silu_and_grad(z):
    sg = jax.nn.sigmoid(z)
    return z * sg, sg * (1.0 + z * (1.0 - sg))


def _swap_rope_halves(t):
    w = t.shape[1]
    half = HEAD_DIM // 2
    lane = lax.broadcasted_iota(jnp.int32, t.shape, 1)
    upper = pltpu.roll(t, w - half, 1)
    lower = pltpu.roll(t, half, 1)
    return jnp.where(lane % HEAD_DIM < half, upper, lower)


def _rope(t, cos_ref, sin_ref):
    reps = t.shape[1] // V7X_LANES
    c = jnp.tile(cos_ref[...], (1, reps))
    s = jnp.tile(sin_ref[...], (1, reps))
    return t * c + _swap_rope_halves(t) * s


def _rope_transposed(dt, cos_ref, sin_ref):
    reps = dt.shape[1] // V7X_LANES
    c = jnp.tile(cos_ref[...], (1, reps))
    s = jnp.tile(sin_ref[...], (1, reps))
    return dt * c + _swap_rope_halves(dt * s)


def _rope_tables(seq_len):
    pos = jnp.arange(seq_len, dtype=F32)
    inv_freq = ROPE_THETA ** (-jnp.arange(0, HEAD_DIM, 2, dtype=F32) / HEAD_DIM)
    ang = pos[:, None] * jnp.tile(inv_freq, 2 * V7X_LANES // HEAD_DIM)[None, :]
    upper_half = jnp.arange(V7X_LANES) % HEAD_DIM >= HEAD_DIM // 2
    return jnp.cos(ang), jnp.where(upper_half[None, :], jnp.sin(ang), -jnp.sin(ang))


def _qkv_proj(h, g, w, b, cos, sin, after):
    T, D = h.shape
    N = w.shape[0]
    tm = min(512, T)
    rope_w = N - N_KV_HEADS * HEAD_DIM

    def body(h_ref, g_ref, w_ref, b_ref, cos_ref, sin_ref, _, y_ref, o_ref):
        hh = h_ref[...]
        y = (hh * _rms_rstd(hh) * g_ref[...]).astype(BF16)
        y_ref[...] = y
        acc = _dot_nt(y, w_ref[...]) + b_ref[...]
        o_ref[:, :rope_w] = _rope(acc[:, :rope_w], cos_ref, sin_ref).astype(BF16)
        o_ref[:, rope_w:] = acc[:, rope_w:].astype(BF16)

    return pl.pallas_call(
        body, name="qkv_proj", grid=(T // tm,),
        in_specs=[_rows(tm, D), _whole((1, D)), _whole((N, D)), _whole((1, N)),
                  _rows(tm, V7X_LANES), _rows(tm, V7X_LANES), pl.BlockSpec(memory_space=pl.ANY)],
        out_specs=[_rows(tm, D), _rows(tm, N)],
        out_shape=[SDS((T, D), BF16), SDS((T, N), BF16)],
        compiler_params=_params("parallel"),
    )(h, g, w, b, cos, sin, after)


def _pw1_proj(h, g, w, b):
    T, D = h.shape
    n = w.shape[2]
    N = N_CHIPS * n
    tm = min(512, T)

    def body(h_ref, g_ref, w_ref, b_ref, y_ref, o_ref):
        hh = h_ref[...]
        y = (hh * _rms_rstd(hh) * g_ref[...]).astype(BF16)
        y_ref[...] = y
        for j in range(N_CHIPS):
            cols = slice(j * n, (j + 1) * n)
            o_ref[:, cols] = jnp.dot(y, w_ref[j], preferred_element_type=F32) + b_ref[:, cols]

    return pl.pallas_call(
        body, name="pw1_proj", grid=(T // tm,),
        in_specs=[_rows(tm, D), _whole((1, D)), _whole((N_CHIPS, D, n)), _whole((1, N))],
        out_specs=[_rows(tm, D), _rows(tm, N)],
        out_shape=[SDS((T, D), BF16), SDS((T, N), F32)],
        compiler_params=_params("parallel"),
    )(h, g, w, b)


PAIRS_PER_KV = Q_PER_KV // 2


def _upper_lanes(shape):
    return lax.broadcasted_iota(jnp.int32, shape, 1) >= HEAD_DIM


def _swap_lane_halves(t):
    return pltpu.roll(t.astype(F32), HEAD_DIM, 1).astype(t.dtype)


def _kv_operands(g, t):
    swapped = _swap_lane_halves(t)
    in_lower, in_upper = (t, swapped) if g == 0 else (swapped, t)
    upper = _upper_lanes(t.shape)
    zero = jnp.zeros_like(t)
    return jnp.where(upper, zero, in_lower), jnp.where(upper, in_upper, zero)


def _group_heads(g):
    pairs = range(g * PAIRS_PER_KV, (g + 1) * PAIRS_PER_KV)
    return [2 * hp for hp in pairs] + [2 * hp + 1 for hp in pairs]


def _all_heads():
    return [h for g in range(N_KV_HEADS) for h in _group_heads(g)]


def _pair_rows(ref, g):
    pairs = range(g * PAIRS_PER_KV, (g + 1) * PAIRS_PER_KV)
    return jnp.concatenate([ref[:, hp * 2 * HEAD_DIM:(hp + 1) * 2 * HEAD_DIM] for hp in pairs], axis=0)


def _from_previous_block(rows):
    row = lax.broadcasted_iota(jnp.int32, (ATTN_BLOCK, ATTN_BLOCK), 0)
    col = lax.broadcasted_iota(jnp.int32, (ATTN_BLOCK, ATTN_BLOCK), 1)
    return jnp.concatenate([col > row] * (rows // ATTN_BLOCK), axis=0)


def _folded_probs(n, q_groups, k_prev_groups, k_cur_groups, sink_ref, prev_part):
    def scores(q, k):
        return lax.dot_general(q, k, (((1,), (1,)), ((), ())), preferred_element_type=F32)

    s_prev = jnp.concatenate([scores(q, k[i]) for q, k in zip(q_groups, k_prev_groups) for i in range(2)], axis=0)
    s_cur = jnp.concatenate([scores(q, k[i]) for q, k in zip(q_groups, k_cur_groups) for i in range(2)], axis=0)
    s_prev = jnp.where(n > 0, s_prev, MASK_VALUE * (HEAD_DIM ** 0.5))
    s = jnp.where(prev_part, s_prev, s_cur) * (HEAD_DIM ** -0.5)
    heads = [h for g in range(N_KV_HEADS) for h in _group_heads(g)]
    sink = jnp.concatenate([jnp.broadcast_to(sink_ref[0:1, h:h + 1], (ATTN_BLOCK, 1)) for h in heads], axis=0)
    m = jnp.maximum(jnp.max(s, axis=1, keepdims=True), sink)
    p = jnp.exp(s - m)
    e_sink = jnp.exp(sink - m)
    inv = 1.0 / (jnp.sum(p, axis=1, keepdims=True) + e_sink)
    return p * inv, e_sink * inv


def _split_folded(t, prev_part):
    tb = t.astype(BF16)
    zero = jnp.zeros_like(tb)
    return jnp.where(prev_part, tb, zero), jnp.where(prev_part, zero, tb)


def _attn_specs(T):
    nb = T // ATTN_BLOCK
    kcol = N_Q_HEADS * HEAD_DIM // V7X_LANES
    cur = lambda n: jnp.minimum(n, nb - 1)
    prev = lambda n: jnp.maximum(jnp.minimum(n, nb - 1) - 1, 0)
    q_spec = pl.BlockSpec((ATTN_BLOCK, N_Q_HEADS * HEAD_DIM), lambda n: (cur(n), 0))
    kc_spec = pl.BlockSpec((ATTN_BLOCK, V7X_LANES), lambda n: (cur(n), kcol))
    kp_spec = pl.BlockSpec((ATTN_BLOCK, V7X_LANES), lambda n: (prev(n), kcol))
    vc_spec = pl.BlockSpec((ATTN_BLOCK, V7X_LANES), lambda n: (cur(n), kcol + 1))
    vp_spec = pl.BlockSpec((ATTN_BLOCK, V7X_LANES), lambda n: (prev(n), kcol + 1))
    return q_spec, kc_spec, kp_spec, vc_spec, vp_spec


def _attn_fwd(qkv, sinks):
    T = qkv.shape[0]
    nb = T // ATTN_BLOCK
    qw = N_Q_HEADS * HEAD_DIM
    all_rows = N_Q_HEADS * ATTN_BLOCK

    def body(q_ref, kc_ref, kp_ref, vc_ref, vp_ref, sink_ref, o_ref, probs_ref, psink_ref):
        n = pl.program_id(0)
        prev_part = _from_previous_block(all_rows)
        half = PAIRS_PER_KV * ATTN_BLOCK
        groups = range(N_KV_HEADS)
        probs, p_sink = _folded_probs(n, [_pair_rows(q_ref, g) for g in groups],
                                      [_kv_operands(g, kp_ref[...]) for g in groups],
                                      [_kv_operands(g, kc_ref[...]) for g in groups], sink_ref, prev_part)
        probs_ref[...] = probs.astype(BF16)
        lane = lax.broadcasted_iota(jnp.int32, (ATTN_BLOCK, V7X_LANES), 1)
        sink_tile = jnp.zeros((ATTN_BLOCK, V7X_LANES), F32)
        for i, h in enumerate(_all_heads()):
            sink_tile = jnp.where(lane == h, p_sink[i * ATTN_BLOCK:(i + 1) * ATTN_BLOCK], sink_tile)
        psink_ref[...] = sink_tile
        p_prev, p_cur = _split_folded(probs, prev_part)
        for g in groups:
            v_prev, v_cur = _kv_operands(g, vp_ref[...]), _kv_operands(g, vc_ref[...])
            even, odd = slice(2 * g * half, (2 * g + 1) * half), slice((2 * g + 1) * half, (2 * g + 2) * half)
            o = (jnp.dot(p_prev[even], v_prev[0], preferred_element_type=F32)
                 + jnp.dot(p_cur[even], v_cur[0], preferred_element_type=F32)
                 + jnp.dot(p_prev[odd], v_prev[1], preferred_element_type=F32)
                 + jnp.dot(p_cur[odd], v_cur[1], preferred_element_type=F32))
            for i in range(PAIRS_PER_KV):
                hp = g * PAIRS_PER_KV + i
                o_ref[:, hp * 2 * HEAD_DIM:(hp + 1) * 2 * HEAD_DIM] = (
                    o[i * ATTN_BLOCK:(i + 1) * ATTN_BLOCK].astype(BF16))

    return pl.pallas_call(
        body, name="attn_fwd", grid=(nb,),
        in_specs=[*_attn_specs(T), _whole((1, N_Q_HEADS))],
        out_specs=[_rows(ATTN_BLOCK, qw), pl.BlockSpec((None, all_rows, ATTN_BLOCK), lambda n: (n, 0, 0)),
                   _rows(ATTN_BLOCK, V7X_LANES)],
        out_shape=[SDS((T, qw), BF16), SDS((nb, all_rows, ATTN_BLOCK), BF16), SDS((T, V7X_LANES), F32)],
        compiler_params=_params("parallel"),
    )(qkv, qkv, qkv, qkv, qkv, sinks)


def _mm_res(name, a, w, b, res, g):
    T, K = a.shape
    D = w.shape[1]
    tm = min(512, T)

    def body(a_ref, w_ref, b_ref, r_ref, g_ref, o_ref, f_ref):
        h = jnp.dot(a_ref[...], w_ref[...], preferred_element_type=F32) + b_ref[...] + r_ref[...]
        o_ref[...] = h
        f_ref[...] = (h * _rms_rstd(h) * g_ref[...]).astype(BF16)

    return pl.pallas_call(
        body, name=name, grid=(T // tm,),
        in_specs=[_rows(tm, K), _whole((K, D)), _whole((1, D)), _rows(tm, D), _whole((1, D))],
        out_specs=[_rows(tm, D), _rows(tm, D)],
        out_shape=[SDS((T, D), F32), SDS((T, D), BF16)],
        compiler_params=_params("parallel"),
    )(a, w, b, res, g)


def _ffn_down(name, s, w2, layer, res):
    _, T, n = s.shape
    D = w2.shape[3]
    tm = min(512, T)

    def body(s_ref, w_ref, r_ref, o_ref):
        acc = r_ref[...]
        for j in range(N_CHIPS):
            acc = acc + jnp.dot(s_ref[j], w_ref[j], preferred_element_type=F32)
        o_ref[...] = acc

    return pl.pallas_call(
        body, name=name, grid=(T // tm,),
        in_specs=[pl.BlockSpec((N_CHIPS, tm, n), lambda i: (0, i, 0)),
                  pl.BlockSpec((N_CHIPS, None, n, D), lambda i: (0, layer, 0, 0)), _rows(tm, D)],
        out_specs=_rows(tm, D),
        out_shape=SDS((T, D), F32),
        compiler_params=_params("parallel"),
    )(s, w2, res)


def _ffn_up(name, f, w1, w3, layer):
    T, D = f.shape
    n = w1.shape[2]
    tm = min(1024, T)

    def body(f_ref, w1_ref, w3_ref, act_ref, gg_ref, s_ref):
        ff = f_ref[...]
        g1 = _dot_nt(ff, w1_ref[...])
        g3 = _dot_nt(ff, w3_ref[...])
        act, dact = _silu_and_grad(g1)
        act_ref[...] = act.astype(BF16)
        gg_ref[...] = (g3 * dact).astype(BF16)
        s_ref[...] = (act * g3).astype(BF16)

    slab = pl.BlockSpec((None, tm, n), lambda j, i: (j, i, 0))
    wslab = pl.BlockSpec((None, None, n, D), lambda j, i: (j, layer, 0, 0))
    hidden = SDS((N_CHIPS, T, n), BF16)
    return pl.pallas_call(
        body, name=name, grid=(N_CHIPS, T // tm),
        in_specs=[pl.BlockSpec((tm, D), lambda j, i: (i, 0)), wslab, wslab],
        out_specs=[slab, slab, slab],
        out_shape=[hidden, hidden, hidden],
        compiler_params=_params("parallel", "parallel"),
    )(f, w1, w3)


def _glu(a, d):
    return a[:, :d] * jax.nn.sigmoid(a[:, d:])


def _conv_tile(T):
    return min(256, T)


def _fill_shifted(sh_ref, tc):
    n = tc + CONV_HALO - V7X_SUBLANES
    for r in range(1, V7X_SUBLANES):
        sh_ref[r, 0:n, :] = sh_ref[0, pl.ds(r, n), :]


def _depthwise_taps(sh_ref, w_ref, offsets, bias_ref, out_ref, tc):
    D = out_ref.shape[1]

    def chunk(i, carry):
        t0 = pl.multiple_of(i * CONV_ROW_CHUNK, CONV_ROW_CHUNK)
        for cb in range(D // CONV_LANE_CHUNK):
            cs = slice(cb * CONV_LANE_CHUNK, (cb + 1) * CONV_LANE_CHUNK)
            acc = jnp.zeros((CONV_ROW_CHUNK, CONV_LANE_CHUNK), F32)
            for r in range(V7X_SUBLANES):
                taps = [(j, o // V7X_SUBLANES) for j, o in enumerate(offsets) if o % V7X_SUBLANES == r]
                if not taps:
                    continue
                span = CONV_ROW_CHUNK + V7X_SUBLANES * max(q for _, q in taps)
                rows = sh_ref[r, pl.ds(t0, span), cs]
                for j, q in taps:
                    acc = acc + rows[V7X_SUBLANES * q:V7X_SUBLANES * q + CONV_ROW_CHUNK] * w_ref[j:j + 1, cs]
            if bias_ref is not None:
                acc = acc + bias_ref[:, cs]
            out_ref[pl.ds(t0, CONV_ROW_CHUNK), cs] = acc
        return carry

    lax.fori_loop(0, tc // CONV_ROW_CHUNK, chunk, 0)


def _depthwise_tap_grads(dy_sh, x_sh, offsets, dw_ref, tc):
    D = dw_ref.shape[1]
    for cb in range(D // V7X_LANES):
        cs = slice(cb * V7X_LANES, (cb + 1) * V7X_LANES)

        def row_tiles(i, accs, cs=cs):
            for k in range(CONV_GRAD_UNROLL):
                t0 = pl.multiple_of(i * (CONV_GRAD_UNROLL * V7X_SUBLANES), V7X_SUBLANES) + k * V7X_SUBLANES
                d = dy_sh[0, pl.ds(t0, V7X_SUBLANES), cs]
                accs = tuple(
                    acc + d * x_sh[o % V7X_SUBLANES, pl.ds(t0 + o // V7X_SUBLANES * V7X_SUBLANES, V7X_SUBLANES), cs]
                    for acc, o in zip(accs, offsets))
            return accs

        zero = jnp.zeros((V7X_SUBLANES, V7X_LANES), F32)
        accs = lax.fori_loop(0, tc // (CONV_GRAD_UNROLL * V7X_SUBLANES), row_tiles, tuple(zero for _ in offsets))
        for j, acc in enumerate(accs):
            dw_ref[j:j + 1, cs] += jnp.sum(acc, axis=0, keepdims=True)


def _conv_fwd(a, w_dw, b_dw, ln_g, ln_b):
    T = a.shape[0]
    D = a.shape[1] // 2
    tc = _conv_tile(T)
    per = tc // CONV_HALO

    def body(a_ref, ah_ref, w_ref, bdw_ref, lg_ref, lb_ref, c_ref, act_ref, u_sh):
        i = pl.program_id(0)
        u_sh[0, 0:CONV_HALO, :] = jnp.where(i > 0, _glu(ah_ref[...], D), 0.0)
        u_sh[0, CONV_HALO:, :] = _glu(a_ref[...], D)
        _fill_shifted(u_sh, tc)
        _depthwise_taps(u_sh, w_ref, [CONV_FIRST_TAP + j for j in range(CONV_WIDTH)], bdw_ref, c_ref, tc)
        c = c_ref[...]
        xc = c - jnp.mean(c, axis=-1, keepdims=True)
        z = xc * lax.rsqrt(jnp.mean(xc * xc, axis=-1, keepdims=True) + LN_EPS)
        l = z * lg_ref[...] + lb_ref[...]
        act_ref[...] = (l * jax.nn.sigmoid(l)).astype(BF16)

    return pl.pallas_call(
        body, name="conv_fwd", grid=(T // tc,),
        in_specs=[_rows(tc, 2 * D),
                  pl.BlockSpec((CONV_HALO, 2 * D), lambda i: (jnp.maximum(i * per - 1, 0), 0)),
                  _whole((CONV_WIDTH, D)), _whole((1, D)), _whole((1, D)), _whole((1, D))],
        out_specs=[_rows(tc, D), _rows(tc, D)],
        out_shape=[SDS((T, D), F32), SDS((T, D), BF16)],
        scratch_shapes=[pltpu.VMEM((V7X_SUBLANES, tc + CONV_HALO, D), F32)],
        compiler_params=_params("parallel"),
    )(a, a, w_dw, b_dw, ln_g, ln_b)


def _ffn_down_loss(name, s, w2, layer, res, g, target):
    _, T, n = s.shape
    D = w2.shape[3]
    tm = min(512, T)

    def body(s_ref, w_ref, r_ref, g_ref, t_ref, dh_ref, loss_ref, dg_ref):
        @pl.when(pl.program_id(0) == 0)
        def _():
            loss_ref[...] = jnp.zeros_like(loss_ref)
            dg_ref[...] = jnp.zeros_like(dg_ref)

        hh = r_ref[...]
        for j in range(N_CHIPS):
            hh = hh + jnp.dot(s_ref[j], w_ref[j], preferred_element_type=F32)
        r = _rms_rstd(hh)
        g = g_ref[...]
        d = hh * r * g - t_ref[...]
        loss_ref[...] += 0.5 * jnp.sum(jnp.mean(d * d, axis=-1, keepdims=True), axis=0, keepdims=True)
        dout = d * (1.0 / D)
        dg_ref[...] += jnp.sum(dout * (hh * r), axis=0, keepdims=True)
        dxh = dout * g
        dh_ref[...] = r * dxh - hh * (r * r * r) * jnp.mean(dxh * hh, axis=-1, keepdims=True)

    return pl.pallas_call(
        body, name=name, grid=(T // tm,),
        in_specs=[pl.BlockSpec((N_CHIPS, tm, n), lambda i: (0, i, 0)),
                  pl.BlockSpec((N_CHIPS, None, n, D), lambda i: (0, layer, 0, 0)), _rows(tm, D),
                  _whole((1, D)), _rows(tm, D)],
        out_specs=[_rows(tm, D), _whole((1, 1)), _whole((1, D))],
        out_shape=[SDS((T, D), F32), SDS((1, 1), F32), SDS((1, D), F32)],
        compiler_params=_params("arbitrary"),
    )(s, w2, res, g, target)


def _ffn_bwd_act(name, dh, w2, layer, act, gate_grad):
    T, D = dh.shape
    n = w2.shape[2]
    tm = min(1024, T)

    def body(dh_ref, w2_ref, act_ref, gg_ref, dg1_ref, dg3_ref):
        ds = lax.dot_general(dh_ref[...].astype(BF16), w2_ref[...], (((1,), (1,)), ((), ())),
                             preferred_element_type=F32)
        dg1_ref[...] = (ds * gg_ref[...].astype(F32)).astype(BF16)
        dg3_ref[...] = (ds * act_ref[...].astype(F32)).astype(BF16)

    slab = pl.BlockSpec((None, tm, n), lambda i, j: (j, i, 0))
    hidden = SDS((N_CHIPS, T, n), BF16)
    return pl.pallas_call(
        body, name=name, grid=(T // tm, N_CHIPS),
        in_specs=[pl.BlockSpec((tm, D), lambda i, j: (i, 0)),
                  pl.BlockSpec((None, None, n, D), lambda i, j: (j, layer, 0, 0)), slab, slab],
        out_specs=[slab, slab],
        out_shape=[hidden, hidden],
        compiler_params=_params("parallel", "arbitrary"),
    )(dh, w2, act, gate_grad)


def _dot_tn(a, b):
    return lax.dot_general(a.astype(BF16), b.astype(BF16), (((0,), (0,)), ((), ())), preferred_element_type=F32)


def _dot_nt(a, b):
    return lax.dot_general(a.astype(BF16), b, (((1,), (1,)), ((), ())), preferred_element_type=F32)


def _mm_tn(name, a, b, col_chunks=1, after=None):
    a_slabs, b_slabs = a.ndim == 3, b.ndim == 3
    T = a.shape[-2]
    tt = min(1024, T)
    ka, nb = a.shape[-1], b.shape[-1]
    if a_slabs or b_slabs:
        out_dims = (N_CHIPS, ka, nb)
    elif col_chunks > 1:
        out_dims = (col_chunks, ka, nb // col_chunks)
    else:
        out_dims = (ka, nb)

    def body(a_ref, b_ref, *rest):
        o_ref = rest[-1]

        @pl.when(pl.program_id(0) == 0)
        def _():
            o_ref[...] = jnp.zeros_like(o_ref)

        if a_slabs:
            bb = b_ref[...].astype(BF16)
            for j in range(N_CHIPS):
                o_ref[j] += _dot_tn(a_ref[j], bb)
        elif b_slabs:
            aa = a_ref[...].astype(BF16)
            for j in range(N_CHIPS):
                o_ref[j] += _dot_tn(aa, b_ref[j])
        elif col_chunks > 1:
            aa = a_ref[...].astype(BF16)
            w = nb // col_chunks
            for j in range(col_chunks):
                o_ref[j] += _dot_tn(aa, b_ref[:, j * w:(j + 1) * w])
        else:
            o_ref[...] += _dot_tn(a_ref[...], b_ref[...])

    def spec(arr, slabs):
        if slabs:
            return pl.BlockSpec((N_CHIPS, tt, arr.shape[-1]), lambda t: (0, t, 0))
        return _rows(tt, arr.shape[-1])

    return pl.pallas_call(
        body, name=name, grid=(T // tt,),
        in_specs=[spec(a, a_slabs), spec(b, b_slabs)] + ([] if after is None else [pl.BlockSpec(memory_space=pl.ANY)]),
        out_specs=_whole(out_dims),
        out_shape=SDS(out_dims, F32),
        compiler_params=_params("arbitrary"),
    )(a, b, *([] if after is None else [after]))


def _mm_nt_normbwd(name, pairs, h, g, dh, after):
    T, D = h.shape
    tm = min(512, T)
    n_pairs = len(pairs)
    kinds = ["slabs" if dy.ndim == 3 else ("quarters" if w.ndim == 3 else "plain") for dy, w, _ in pairs]

    def body(*refs):
        dy_refs = refs[:n_pairs]
        w_refs = refs[n_pairs:2 * n_pairs]
        h_ref, g_ref, dh_ref, _, o_ref, dg_ref, cs_ref = refs[2 * n_pairs:]

        @pl.when(pl.program_id(0) == 0)
        def _():
            dg_ref[...] = jnp.zeros_like(dg_ref)
            cs_ref[...] = jnp.zeros_like(cs_ref)

        df = jnp.zeros((tm, D), F32)
        for dy_ref, w_ref, kd in zip(dy_refs, w_refs, kinds):
            if kd == "slabs":
                for j in range(N_CHIPS):
                    df = df + jnp.dot(dy_ref[j], w_ref[j], preferred_element_type=F32)
            elif kd == "quarters":
                n = w_ref.shape[2]
                for j in range(N_CHIPS):
                    df = df + _dot_nt(dy_ref[:, j * n:(j + 1) * n], w_ref[j])
            else:
                df = df + jnp.dot(dy_ref[...], w_ref[...], preferred_element_type=F32)
        hh = h_ref[...]
        r = _rms_rstd(hh)
        dg_ref[...] += jnp.sum(df * (hh * r), axis=0, keepdims=True)
        dxh = df * g_ref[...]
        out = dh_ref[...] + (r * dxh - hh * (r * r * r) * jnp.mean(dxh * hh, axis=-1, keepdims=True))
        o_ref[...] = out
        cs_ref[...] += jnp.sum(out, axis=0, keepdims=True)

    dy_specs, w_specs = [], []
    for (dy, w, layer), kd in zip(pairs, kinds):
        if kd == "slabs":
            dy_specs.append(pl.BlockSpec((N_CHIPS, tm, dy.shape[2]), lambda i: (0, i, 0)))
            w_specs.append(pl.BlockSpec((N_CHIPS, None, w.shape[2], D),
                                        functools.partial(lambda i, layer: (0, layer, 0, 0), layer=layer),
                                        pipeline_mode=pl.Buffered(1)))
        else:
            dy_specs.append(_rows(tm, dy.shape[1]))
            w_specs.append(_whole(w.shape))

    return pl.pallas_call(
        body, name=name, grid=(T // tm,),
        in_specs=[*dy_specs, *w_specs, _rows(tm, D), _whole((1, D)), _rows(tm, D), pl.BlockSpec(memory_space=pl.ANY)],
        out_specs=[_rows(tm, D), _whole((1, D)), _whole((1, D))],
        out_shape=[SDS((T, D), F32), SDS((1, D), F32), SDS((1, D), F32)],
        compiler_params=_params("arbitrary"),
    )(*[dy for dy, _, _ in pairs], *[w for _, w, _ in pairs], h, g, dh, after)


def _mm_nt(name, dy, w, out_dtype):
    T, N = dy.shape
    K = w.shape[0]
    tm = min(512, T)

    def body(dy_ref, w_ref, o_ref):
        o_ref[...] = lax.dot_general(dy_ref[...].astype(BF16), w_ref[...], (((1,), (1,)), ((), ())),
                                     preferred_element_type=F32).astype(out_dtype)

    return pl.pallas_call(
        body, name=name, grid=(T // tm,),
        in_specs=[_rows(tm, N), _whole((K, N))],
        out_specs=_rows(tm, K),
        out_shape=SDS((T, K), out_dtype),
        compiler_params=_params("parallel"),
    )(dy, w)


def _conv_bwd(dact, c, a, w_dw, ln_g, ln_b):
    T, D = c.shape
    tc = _conv_tile(T)
    per = tc // CONV_HALO
    n_tiles = T // tc
    last_halo = T // CONV_HALO - 1

    def ln_bwd(dact_v, c_v, lg, lb):
        xc = c_v - jnp.mean(c_v, axis=-1, keepdims=True)
        rstd = lax.rsqrt(jnp.mean(xc * xc, axis=-1, keepdims=True) + LN_EPS)
        z = xc * rstd
        _, dsilu = _silu_and_grad(z * lg + lb)
        dl = dact_v * dsilu
        dz = dl * lg
        dc = rstd * (dz - jnp.mean(dz, axis=-1, keepdims=True) - z * jnp.mean(dz * z, axis=-1, keepdims=True))
        return dc, dl, z

    def body(dact_ref, dactn_ref, c_ref, cn_ref, a_ref, ah_ref, w_ref, lg_ref, lb_ref,
             da_ref, dlg_ref, dlb_ref, dbdw_ref, dwdw_ref, dbpw1_ref, dc_sh, u_sh, du_scr):
        i = pl.program_id(0)

        @pl.when(i == 0)
        def _():
            for ref in (dlg_ref, dlb_ref, dbdw_ref, dwdw_ref, dbpw1_ref):
                ref[...] = jnp.zeros_like(ref)

        lg, lb = lg_ref[...], lb_ref[...]
        dc, dl, z = ln_bwd(dact_ref[...], c_ref[...], lg, lb)
        dlg_ref[...] += jnp.sum(dl * z, axis=0, keepdims=True)
        dlb_ref[...] += jnp.sum(dl, axis=0, keepdims=True)
        dbdw_ref[...] += jnp.sum(dc, axis=0, keepdims=True)
        dcn, _, _ = ln_bwd(dactn_ref[...], cn_ref[...], lg, lb)
        dc_sh[0, 0:tc, :] = dc
        dc_sh[0, tc:, :] = jnp.where(i < n_tiles - 1, dcn, 0.0)
        _fill_shifted(dc_sh, tc)

        a_v = a_ref[...]
        a1 = a_v[:, :D]
        sg = jax.nn.sigmoid(a_v[:, D:])
        u_sh[0, 0:CONV_HALO, :] = jnp.where(i > 0, _glu(ah_ref[...], D), 0.0)
        u_sh[0, CONV_HALO:, :] = a1 * sg
        _fill_shifted(u_sh, tc)

        _depthwise_taps(dc_sh, w_ref, [CONV_WIDTH - 1 - j for j in range(CONV_WIDTH)], None, du_scr, tc)
        _depthwise_tap_grads(dc_sh, u_sh, [CONV_FIRST_TAP + j for j in range(CONV_WIDTH)], dwdw_ref, tc)

        du = du_scr[...]
        da1 = du * sg
        da2 = du * a1 * sg * (1.0 - sg)
        da_ref[:, :D] = da1.astype(BF16)
        da_ref[:, D:] = da2.astype(BF16)
        dbpw1_ref[:, :D] += jnp.sum(da1, axis=0, keepdims=True)
        dbpw1_ref[:, D:] += jnp.sum(da2, axis=0, keepdims=True)

    nxt = lambda i: (jnp.minimum((i + 1) * per, last_halo), 0)
    return pl.pallas_call(
        body, name="conv_bwd", grid=(n_tiles,),
        in_specs=[_rows(tc, D), pl.BlockSpec((CONV_HALO, D), nxt),
                  _rows(tc, D), pl.BlockSpec((CONV_HALO, D), nxt),
                  _rows(tc, 2 * D),
                  pl.BlockSpec((CONV_HALO, 2 * D), lambda i: (jnp.maximum(i * per - 1, 0), 0)),
                  _whole((CONV_WIDTH, D)), _whole((1, D)), _whole((1, D))],
        out_specs=[_rows(tc, 2 * D), _whole((1, D)), _whole((1, D)), _whole((1, D)),
                   _whole((CONV_HALO, D)), _whole((1, 2 * D))],
        out_shape=[SDS((T, 2 * D), BF16), SDS((1, D), F32), SDS((1, D), F32), SDS((1, D), F32),
                   SDS((CONV_HALO, D), F32), SDS((1, 2 * D), F32)],
        scratch_shapes=[pltpu.VMEM((V7X_SUBLANES, tc + CONV_HALO, D), F32),
                        pltpu.VMEM((V7X_SUBLANES, tc + CONV_HALO, D), F32), pltpu.VMEM((tc, D), F32)],
        compiler_params=_params("arbitrary"),
    )(dact, dact, c, c, a, a, w_dw, ln_g, ln_b)


def _attn_bwd(qkv, dao, cos, sin, probs_saved, psink_saved):
    T = qkv.shape[0]
    nb = T // ATTN_BLOCK
    qw = N_Q_HEADS * HEAD_DIM
    kw = N_KV_HEADS * HEAD_DIM

    def body(q_ref, kc_ref, kp_ref, vc_ref, vp_ref, do_ref, cos_ref, sin_ref, cosp_ref, sinp_ref, probs_ref, psink_ref,
             dq_ref, dkv_ref, dsink_ref, dbq_ref, dbkv_ref, carry, prev_scr, cur_scr, dq_scr):
        n = pl.program_id(0)

        @pl.when(n == 0)
        def _():
            for ref in (dsink_ref, dbq_ref, dbkv_ref, carry):
                ref[...] = jnp.zeros_like(ref)

        @pl.when(n == nb)
        def _():
            prev_scr[...] = jnp.zeros_like(prev_scr)

        @pl.when(n < nb)
        def _():
            prev_part = _from_previous_block(N_Q_HEADS * ATTN_BLOCK)
            half = PAIRS_PER_KV * ATTN_BLOCK
            upper = _upper_lanes((ATTN_BLOCK, 2 * HEAD_DIM))
            groups = range(N_KV_HEADS)

            def nt(a, b):
                return lax.dot_general(a, b, (((1,), (1,)), ((), ())), preferred_element_type=F32)

            def kv_grad(even_rows, odd_rows, x):
                even = lax.dot_general(even_rows, x, (((0,), (0,)), ((), ())), preferred_element_type=F32)
                odd = lax.dot_general(odd_rows, x, (((0,), (0,)), ((), ())), preferred_element_type=F32)
                t = jnp.where(upper, odd, even)
                return t + _swap_lane_halves(t)

            q = [_pair_rows(q_ref, g) for g in groups]
            do = [_pair_rows(do_ref, g) for g in groups]
            k_prev = [_kv_operands(g, kp_ref[...]) for g in groups]
            k_cur = [_kv_operands(g, kc_ref[...]) for g in groups]
            v_prev = [_kv_operands(g, vp_ref[...]) for g in groups]
            v_cur = [_kv_operands(g, vc_ref[...]) for g in groups]
            probs = probs_ref[...].astype(F32)
            dp_prev = jnp.concatenate([nt(do[g], v_prev[g][i]) for g in groups for i in range(2)], axis=0)
            dp_cur = jnp.concatenate([nt(do[g], v_cur[g][i]) for g in groups for i in range(2)], axis=0)
            dp = jnp.where(prev_part, dp_prev, dp_cur)
            delta = jnp.sum(probs * dp, axis=1, keepdims=True)
            ds_prev, ds_cur = _split_folded(probs * (dp - delta) * (HEAD_DIM ** -0.5), prev_part)
            p_prev, p_cur = _split_folded(probs_ref[...], prev_part)
            for i, h in enumerate(_all_heads()):
                rows = slice(i * ATTN_BLOCK, (i + 1) * ATTN_BLOCK)
                dsink_ref[:, h:h + 1] += jnp.sum(-(psink_ref[:, h:h + 1] * delta[rows]), axis=0, keepdims=True)
            kv_grads = []
            for g in groups:
                even, odd = slice(2 * g * half, (2 * g + 1) * half), slice((2 * g + 1) * half, (2 * g + 2) * half)
                dq = (jnp.dot(ds_prev[even], k_prev[g][0], preferred_element_type=F32)
                      + jnp.dot(ds_cur[even], k_cur[g][0], preferred_element_type=F32)
                      + jnp.dot(ds_prev[odd], k_prev[g][1], preferred_element_type=F32)
                      + jnp.dot(ds_cur[odd], k_cur[g][1], preferred_element_type=F32))
                for i in range(PAIRS_PER_KV):
                    hp = g * PAIRS_PER_KV + i
                    dq_scr[:, hp * 2 * HEAD_DIM:(hp + 1) * 2 * HEAD_DIM] = dq[i * ATTN_BLOCK:(i + 1) * ATTN_BLOCK]
                kv_grads.append((kv_grad(ds_prev[even], ds_prev[odd], q[g]), kv_grad(ds_cur[even], ds_cur[odd], q[g]),
                                 kv_grad(p_prev[even], p_prev[odd], do[g]), kv_grad(p_cur[even], p_cur[odd], do[g])))
            (dkp0, dkc0, dvp0, dvc0), (dkp1, dkc1, dvp1, dvc1) = kv_grads
            prev_scr[:, :kw] = jnp.where(upper, dkp1, dkp0)
            prev_scr[:, kw:] = jnp.where(upper, dvp1, dvp0)
            cur_scr[:, :kw] = jnp.where(upper, dkc1, dkc0)
            cur_scr[:, kw:] = jnp.where(upper, dvc1, dvc0)
            dq_pre = _rope_transposed(dq_scr[...], cos_ref, sin_ref)
            dq_ref[...] = dq_pre.astype(BF16)
            dbq_ref[...] += jnp.sum(dq_pre, axis=0, keepdims=True)

        tot = carry[...] + prev_scr[...]
        dk_pre = _rope_transposed(tot[:, :kw], cosp_ref, sinp_ref)
        dkv_ref[:, :kw] = dk_pre.astype(BF16)
        dkv_ref[:, kw:] = tot[:, kw:].astype(BF16)
        dbkv_ref[:, :kw] += jnp.sum(dk_pre, axis=0, keepdims=True)
        dbkv_ref[:, kw:] += jnp.sum(tot[:, kw:], axis=0, keepdims=True)

        @pl.when(n < nb)
        def _():
            carry[...] = cur_scr[...]

    cur = lambda n: (jnp.minimum(n, nb - 1), 0)
    out_lag = lambda n: (jnp.maximum(n - 1, 0), 0)
    return pl.pallas_call(
        body, name="attn_bwd", grid=(nb + 1,),
        in_specs=[*_attn_specs(T),
                  pl.BlockSpec((ATTN_BLOCK, qw), cur),
                  pl.BlockSpec((ATTN_BLOCK, V7X_LANES), cur), pl.BlockSpec((ATTN_BLOCK, V7X_LANES), cur),
                  pl.BlockSpec((ATTN_BLOCK, V7X_LANES), out_lag), pl.BlockSpec((ATTN_BLOCK, V7X_LANES), out_lag),
                  pl.BlockSpec((None, N_Q_HEADS * ATTN_BLOCK, ATTN_BLOCK), lambda n: (jnp.minimum(n, nb - 1), 0, 0)),
                  pl.BlockSpec((ATTN_BLOCK, V7X_LANES), cur)],
        out_specs=[pl.BlockSpec((ATTN_BLOCK, qw), cur), pl.BlockSpec((ATTN_BLOCK, 2 * kw), out_lag),
                   _whole((1, N_Q_HEADS)), _whole((1, qw)), _whole((1, 2 * kw))],
        out_shape=[SDS((T, qw), BF16), SDS((T, 2 * kw), BF16),
                   SDS((1, N_Q_HEADS), F32), SDS((1, qw), F32), SDS((1, 2 * kw), F32)],
        scratch_shapes=[pltpu.VMEM((ATTN_BLOCK, 2 * kw), F32), pltpu.VMEM((ATTN_BLOCK, 2 * kw), F32),
                        pltpu.VMEM((ATTN_BLOCK, 2 * kw), F32), pltpu.VMEM((ATTN_BLOCK, qw), F32)],
        compiler_params=_params("arbitrary"),
    )(qkv, qkv, qkv, qkv, qkv, dao, cos, sin, cos, sin, probs_saved, psink_saved)


def _local_step(x, target, p, reduce_begin, reduce_send):
    T, D = x.shape
    cos, sin = _rope_tables(T)
    qw = N_Q_HEADS * HEAD_DIM
    nm, nf = p["norm_mix"], p["norm_ffn"]

    y0, qkv = _qkv_proj(x, nm[0:1], p["attn_w_qkv"], p["attn_b_qkv"], cos, sin, p["gather_started"])
    ao, attn_probs, sink_probs = _attn_fwd(qkv, p["attn_sinks"])
    h1, f0 = _mm_res("attn_out", ao, p["attn_w_o"], p["attn_b_o"], x, nf[0:1])
    p = {**p, **p["other_weights"](h1)}
    w1, w3, w2 = p["ffn_w1"], p["ffn_w3"], p["ffn_w2"]
    act0, gg0, s0 = _ffn_up("ffn0_up", f0, w1, w3, 0)
    h2 = _ffn_down("ffn0_down", s0, w2, 0, h1)
    y1, a = _pw1_proj(h2, nm[1:2], p["conv_w_pw1"], p["conv_b_pw1"])
    c, act = _conv_fwd(a, p["conv_w_dw"], p["conv_b_dw"], p["conv_ln_g"], p["conv_ln_b"])
    h3, f1 = _mm_res("conv_out", act, p["conv_w_pw2"], p["conv_b_pw2"], h2, nf[1:2])
    act1, gg1, s1 = _ffn_up("ffn1_up", f1, w1, w3, 1)
    dh4, loss, d_norm_final = _ffn_down_loss("ffn1_down_loss", s1, w2, 1, h3, p["norm_final"], target)

    g = {}
    dg1, dg3 = _ffn_bwd_act("ffn1_bwd_act", dh4, w2, 1, act1, gg1)
    dw2_1 = _mm_tn("ffn1_dw2", s1, dh4)
    dw1_1 = _mm_tn("ffn1_dw1", dg1, f1)
    dw3_1 = _mm_tn("ffn1_dw3", dg3, f1)
    begun = reduce_begin("ffn1", {("ffn_w1", 1): dw1_1, ("ffn_w3", 1): dw3_1, ("ffn_w2", 1): dw2_1})
    dh3, dnf1, db_pw2 = _mm_nt_normbwd("ffn1_bwd_in", [(dg1, w1, 1), (dg3, w3, 1)], h3, nf[1:2], dh4, begun)
    sent = reduce_send("ffn1", dh3)

    dw_pw2 = _mm_tn("conv_dw_pw2", act, dh3, after=sent)
    dact = _mm_nt("conv_bwd_out", dh3, p["conv_w_pw2"], F32)
    da, d_ln_g, d_ln_b, d_b_dw, d_w_dw, d_b_pw1 = _conv_bwd(dact, c, a, p["conv_w_dw"], p["conv_ln_g"], p["conv_ln_b"])
    dw_pw1 = _mm_tn("conv_dw_pw1", y1, da, col_chunks=N_CHIPS)
    begun = reduce_begin("conv", {("conv_w_pw2", 0): dw_pw2.reshape(N_CHIPS, -1, D), ("conv_w_pw1", 0): dw_pw1})
    dh2, dnm1, _ = _mm_nt_normbwd("conv_bwd_in", [(da, p["conv_w_pw1"], None)], h2, nm[1:2], dh3, begun)
    sent = reduce_send("conv", dh2)

    dg1, dg3 = _ffn_bwd_act("ffn0_bwd_act", dh2, w2, 0, act0, gg0)
    dw2_0 = _mm_tn("ffn0_dw2", s0, dh2, after=sent)
    dw1_0 = _mm_tn("ffn0_dw1", dg1, f0)
    dw3_0 = _mm_tn("ffn0_dw3", dg3, f0)
    begun = reduce_begin("ffn0", {("ffn_w1", 0): dw1_0, ("ffn_w3", 0): dw3_0, ("ffn_w2", 0): dw2_0})
    dh1, dnf0, db_o = _mm_nt_normbwd("ffn0_bwd_in", [(dg1, w1, 0), (dg3, w3, 0)], h1, nf[0:1], dh2, begun)
    sent = reduce_send("ffn0", dh1)

    dw_o = _mm_tn("attn_dw_o", ao, dh1, after=sent)
    dao = _mm_nt("attn_bwd_out", dh1, p["attn_w_o"], BF16)
    dq, dkv, d_sinks, dbq, dbkv = _attn_bwd(qkv, dao, cos, sin, attn_probs, sink_probs)
    dwq = _mm_tn("attn_dw_q", dq, y0)
    dwkv = _mm_tn("attn_dw_kv", dkv, y0)
    wqkv = p["attn_w_qkv"]
    dwqkv = jnp.concatenate([dwq, dwkv], axis=0).reshape(N_CHIPS, -1, D)
    begun = reduce_begin("attn", {("attn_w_o", 0): dw_o.reshape(N_CHIPS, -1, D), ("attn_w_qkv", 0): dwqkv})
    dx, dnm0, _ = _mm_nt_normbwd("attn_bwd_in", [(dq, wqkv[:qw], None), (dkv, wqkv[qw:], None)], x, nm[0:1], dh1,
                                 begun)
    reduce_send("attn", dx)

    g["norm_mix"] = jnp.concatenate([dnm0, dnm1], axis=0)
    g["norm_ffn"] = jnp.concatenate([dnf0, dnf1], axis=0)
    g["attn_b_qkv"] = jnp.concatenate([dbq, dbkv], axis=1)
    g["attn_sinks"] = d_sinks
    g["attn_b_o"] = db_o
    g["conv_b_pw1"] = d_b_pw1
    g["conv_w_dw"] = d_w_dw[:CONV_WIDTH]
    g["conv_b_dw"] = d_b_dw
    g["conv_ln_g"] = d_ln_g
    g["conv_ln_b"] = d_ln_b
    g["conv_b_pw2"] = db_pw2
    g["norm_final"] = d_norm_final
    return loss, dx, g


ANY = pl.BlockSpec(memory_space=pl.ANY)
VMEM_WHOLE = pl.BlockSpec(memory_space=pltpu.VMEM)


def _my_place():
    return lax.axis_index("x"), lax.axis_index("y"), lax.axis_index("c")


def _other_chips(x, y):
    places = [(1 - x, y), (x, 1 - y), (1 - x, 1 - y)]
    return [(bx, by, 2 * bx + by) for bx, by in places]


def _gather_small(name, v):
    r, w = v.shape

    def body(v_ref, o_ref, send_sems, recv_sems):
        x, y, c = _my_place()
        pairs = _all_to_all_copies(v_ref, o_ref, send_sems, recv_sems)
        for send, _ in pairs:
            send.start()
        o_ref[4 * x + 2 * y + c] = v_ref[...]
        for send, arrival in pairs:
            arrival.wait_recv()
            send.wait_send()

    return pl.pallas_call(
        body, name=name, out_shape=SDS((N_DEV, r, w), F32), in_specs=[VMEM_WHOLE], out_specs=VMEM_WHOLE,
        scratch_shapes=[pltpu.SemaphoreType.DMA((N_DEV - 1,)), pltpu.SemaphoreType.DMA((N_DEV - 1,))],
        compiler_params=pltpu.CompilerParams(vmem_limit_bytes=V7X_VMEM_LIMIT_BYTES),
    )(v)


def _cast_into_slot(name, gathered, shard, chip_idx):
    rows, cols = shard.shape
    tr = _pack_row_tile(rows)

    def body(k_ref, s_ref, g_ref, o_ref):
        o_ref[...] = s_ref[...].astype(BF16)

    return pl.pallas_call(
        body, name=name,
        grid_spec=pltpu.PrefetchScalarGridSpec(
            num_scalar_prefetch=1, grid=(rows // tr,),
            in_specs=[pl.BlockSpec((tr, cols), lambda i, k_ref: (i, 0)), pl.BlockSpec(memory_space=pl.ANY)],
            out_specs=pl.BlockSpec((None, tr, cols), lambda i, k_ref: (k_ref[0], i, 0))),
        out_shape=SDS(gathered.shape, BF16),
        input_output_aliases={2: 0},
        compiler_params=_params("parallel"),
    )(chip_idx, shard, gathered)


def _row_halves(ref, c):
    half = ref.shape[1] // 2
    return pl.ds(pl.multiple_of(c * half, 16), half), pl.ds(pl.multiple_of((1 - c) * half, 16), half)


def _gather_ici_copies(refs, send_sems, recv_sems):
    x, y, c = _my_place()
    k = 2 * x + y
    pairs = []
    for i, ref in enumerate(refs):
        mine, _ = _row_halves(ref, c)
        for j, (bx, by, kb) in enumerate(_other_chips(x, y)):
            sems = dict(send_sem=send_sems.at[3 * i + j], recv_sem=recv_sems.at[3 * i + j], device_id_type=MESH)
            send = pltpu.make_async_remote_copy(src_ref=ref.at[k, mine], dst_ref=ref.at[k, mine],
                                                device_id=(bx, by, c), **sems)
            arrival = pltpu.make_async_remote_copy(src_ref=ref.at[kb, mine], dst_ref=ref.at[kb, mine],
                                                   device_id=(bx, by, c), **sems)
            pairs.append((send, arrival))
    return pairs


def _gather_d2d_copies(refs, send_sems, recv_sems, first_sem):
    x, y, c = _my_place()
    pairs = []
    for i, ref in enumerate(refs):
        mine, theirs = _row_halves(ref, c)
        for j, (_, _, kb) in enumerate(_other_chips(x, y)):
            sem = first_sem + 3 * i + j
            sems = dict(send_sem=send_sems.at[sem], recv_sem=recv_sems.at[sem], device_id=(x, y, 1 - c),
                        device_id_type=MESH)
            send = pltpu.make_async_remote_copy(src_ref=ref.at[kb, mine], dst_ref=ref.at[kb, mine], **sems)
            arrival = pltpu.make_async_remote_copy(src_ref=ref.at[kb, theirs], dst_ref=ref.at[kb, theirs], **sems)
            pairs.append((send, arrival))
    return pairs


def _run_copies(pairs):
    for send, _ in pairs:
        send.start()
    for send, arrival in pairs:
        send.wait_send()
        arrival.wait_recv()


def _gather_now(name, gathered):
    n_w = len(gathered)

    def body(*refs):
        in_refs = refs[:n_w]
        send_sems, recv_sems = refs[2 * n_w:]
        _run_copies(_gather_ici_copies(in_refs, send_sems, recv_sems))
        _run_copies(_gather_d2d_copies(in_refs, send_sems, recv_sems, 3 * n_w))

    return pl.pallas_call(
        body, name=name, out_shape=[SDS(g.shape, g.dtype) for g in gathered],
        in_specs=[ANY] * n_w, out_specs=[ANY] * n_w, input_output_aliases={i: i for i in range(n_w)},
        scratch_shapes=[pltpu.SemaphoreType.DMA((6 * n_w,)), pltpu.SemaphoreType.DMA((6 * n_w,))],
    )(*gathered)


def _gather_start(name, gathered, after):
    n_w = len(gathered)

    def body(*refs):
        in_refs = refs[:n_w]
        send_sems, recv_sems = refs[n_w + 1:n_w + 3]
        for send, _ in _gather_ici_copies(in_refs, send_sems, recv_sems):
            send.start()
        refs[-1][...] = jnp.zeros_like(refs[-1])

    out = pl.pallas_call(
        body, name=name,
        out_shape=(pltpu.SemaphoreType.DMA((3 * n_w,)), pltpu.SemaphoreType.DMA((3 * n_w,)),
                   *[pltpu.HBM(g.shape, g.dtype) for g in gathered], SDS((8, V7X_LANES), F32)),
        in_specs=[*[HBM_SPEC] * n_w, ANY], out_specs=(SEM_SPEC, SEM_SPEC, *[HBM_SPEC] * n_w, VMEM_WHOLE),
        input_output_aliases={i: 2 + i for i in range(n_w)},
        compiler_params=pltpu.CompilerParams(has_side_effects=DATAFLOW),
    )(*[pltpu.with_memory_space_constraint(g, pltpu.HBM) for g in gathered], after)
    return out[0], out[1], list(out[2:2 + n_w]), out[-1]


def _gather_wait(name, send_sems, recv_sems, gathered, after):
    n_w = len(gathered)

    def body(*refs):
        in_refs = refs[:n_w]
        send_sems, recv_sems = refs[n_w:n_w + 2]
        for send, arrival in _gather_ici_copies(in_refs, send_sems, recv_sems):
            send.wait_send()
            arrival.wait_recv()

    out = pl.pallas_call(
        body, name=name, out_shape=tuple(pltpu.HBM(g.shape, g.dtype) for g in gathered),
        in_specs=[*[HBM_SPEC] * n_w, SEM_SPEC, SEM_SPEC, ANY], out_specs=tuple([HBM_SPEC] * n_w),
        input_output_aliases={i: i for i in range(n_w)},
        compiler_params=pltpu.CompilerParams(has_side_effects=DATAFLOW),
    )(*gathered, send_sems, recv_sems, after)
    return list(out)


def _swap_fetched_with_sibling(name, gathered):
    n_w = len(gathered)

    def body(*refs):
        in_refs = refs[:n_w]
        send_sems, recv_sems = refs[2 * n_w:]
        _run_copies(_gather_d2d_copies(in_refs, send_sems, recv_sems, 0))

    return pl.pallas_call(
        body, name=name, out_shape=[SDS(g.shape, g.dtype) for g in gathered],
        in_specs=[ANY] * n_w, out_specs=[ANY] * n_w, input_output_aliases={i: i for i in range(n_w)},
        scratch_shapes=[pltpu.SemaphoreType.DMA((3 * n_w,)), pltpu.SemaphoreType.DMA((3 * n_w,))],
    )(*gathered)


def _sibling_swap_copies(g_refs, land_refs, send_sems, recv_sems):
    x, y, c = _my_place()
    copies = []
    for i, g_ref in enumerate(g_refs):
        half = g_ref.shape[1] // 2
        theirs = pl.ds(pl.multiple_of((1 - c) * half, 8), half)
        copies.append(pltpu.make_async_remote_copy(
            src_ref=g_ref.at[:, theirs], dst_ref=land_refs[i], send_sem=send_sems.at[i], recv_sem=recv_sems.at[i],
            device_id=(x, y, 1 - c), device_id_type=MESH))
    return copies


def _sibling_swap_start(name, grads):
    n_g = len(grads)

    def body(*refs):
        g_refs, land_refs = refs[:n_g], refs[n_g:2 * n_g]
        send_sems, recv_sems = refs[2 * n_g:2 * n_g + 2]
        for cp in _sibling_swap_copies(g_refs, land_refs, send_sems, recv_sems):
            cp.start()
        refs[-1][...] = jnp.zeros_like(refs[-1])

    lands = [pltpu.with_memory_space_constraint(lax.empty((g.shape[0], g.shape[1] // 2, g.shape[2]), g.dtype),
                                                pltpu.HBM) for g in grads]
    out = pl.pallas_call(
        body, name=name,
        out_shape=(pltpu.SemaphoreType.DMA((n_g,)), pltpu.SemaphoreType.DMA((n_g,)),
                   *[pltpu.HBM(g.shape, g.dtype) for g in grads], *[pltpu.HBM(l.shape, l.dtype) for l in lands],
                   SDS((8, V7X_LANES), F32)),
        in_specs=[HBM_SPEC] * (2 * n_g), out_specs=(SEM_SPEC, SEM_SPEC, *[HBM_SPEC] * (2 * n_g), VMEM_WHOLE),
        input_output_aliases={i: 2 + i for i in range(2 * n_g)},
        compiler_params=pltpu.CompilerParams(has_side_effects=DATAFLOW),
    )(*[pltpu.with_memory_space_constraint(g, pltpu.HBM) for g in grads], *lands)
    return out[0], out[1], list(out[2:2 + n_g]), list(out[2 + n_g:2 + 2 * n_g]), out[-1]


def _sibling_swap_wait(name, send_sems, recv_sems, grads, lands, after):
    n_g = len(grads)

    def body(*refs):
        g_refs, land_refs = refs[:n_g], refs[n_g:2 * n_g]
        send_sems, recv_sems = refs[2 * n_g:2 * n_g + 2]
        for cp in _sibling_swap_copies(g_refs, land_refs, send_sems, recv_sems):
            cp.wait_send()
            cp.wait_recv()

    out = pl.pallas_call(
        body, name=name,
        out_shape=(*[pltpu.HBM(g.shape, g.dtype) for g in grads], *[pltpu.HBM(l.shape, l.dtype) for l in lands]),
        in_specs=[*[HBM_SPEC] * (2 * n_g), SEM_SPEC, SEM_SPEC, ANY], out_specs=tuple([HBM_SPEC] * (2 * n_g)),
        input_output_aliases={i: i for i in range(2 * n_g)},
        compiler_params=pltpu.CompilerParams(has_side_effects=DATAFLOW),
    )(*grads, *lands, send_sems, recv_sems, after)
    return list(out[:n_g]), list(out[n_g:])


def _pack_row_tile(rows):
    for t in range(min(rows, 512), 7, -1):
        if rows % t == 0 and t % 8 == 0:
            return t
    return rows


def _add_sibling_half(name, grads, from_sibling, c_idx):
    n, R, w = grads.shape
    half = R // 2
    tr = _pack_row_tile(half)
    steps = half // tr

    def body(c_ref, g_ref, s_ref, o_ref):
        o_ref[...] = (g_ref[...] + s_ref[...]).astype(BF16)

    return pl.pallas_call(
        body, name=name,
        grid_spec=pltpu.PrefetchScalarGridSpec(
            num_scalar_prefetch=1, grid=(n, steps),
            in_specs=[pl.BlockSpec((1, tr, w), lambda j, i, c_ref: (j, c_ref[0] * steps + i, 0)),
                      pl.BlockSpec((1, tr, w), lambda j, i, c_ref: (j, i, 0))],
            out_specs=pl.BlockSpec((1, tr, w), lambda j, i, c_ref: (j, i, 0))),
        out_shape=SDS((n, half, w), BF16),
        compiler_params=_params("parallel", "parallel"),
    )(c_idx, grads, from_sibling)


HBM_SPEC = pl.BlockSpec(memory_space=pltpu.HBM)
SEM_SPEC = pl.BlockSpec(memory_space=pltpu.SEMAPHORE)
DATAFLOW = pltpu.SideEffectType.DATAFLOW_SIDE_EFFECTING


def _chip_scatter_copies(p_refs, land_refs, send_sems, recv_sems):
    x, y, c = _my_place()
    return [pltpu.make_async_remote_copy(
        src_ref=p_refs[i].at[kb], dst_ref=land_refs[i].at[j], send_sem=send_sems.at[3 * i + j],
        recv_sem=recv_sems.at[3 * i + j], device_id=(bx, by, c), device_id_type=MESH)
        for i in range(len(p_refs)) for j, (bx, by, kb) in enumerate(_other_chips(x, y))]


def _scatter_start(name, partials):
    n_p = len(partials)

    def body(*refs):
        p_refs, land_refs = refs[:n_p], refs[n_p:2 * n_p]
        send_sems, recv_sems = refs[2 * n_p:2 * n_p + 2]
        for cp in _chip_scatter_copies(p_refs, land_refs, send_sems, recv_sems):
            cp.start()
        refs[-1][...] = jnp.zeros_like(refs[-1])

    lands = [pltpu.with_memory_space_constraint(lax.empty((N_CHIPS - 1,) + p.shape[1:], p.dtype), pltpu.HBM)
             for p in partials]
    out = pl.pallas_call(
        body, name=name,
        out_shape=(pltpu.SemaphoreType.DMA((3 * n_p,)), pltpu.SemaphoreType.DMA((3 * n_p,)),
                   *[pltpu.HBM(p.shape, p.dtype) for p in partials], *[pltpu.HBM(l.shape, l.dtype) for l in lands],
                   SDS((8, V7X_LANES), F32)),
        in_specs=[HBM_SPEC] * (2 * n_p), out_specs=(SEM_SPEC, SEM_SPEC, *[HBM_SPEC] * (2 * n_p), VMEM_WHOLE),
        input_output_aliases={i: 2 + i for i in range(2 * n_p)},
        compiler_params=pltpu.CompilerParams(has_side_effects=DATAFLOW),
    )(*[pltpu.with_memory_space_constraint(p, pltpu.HBM) for p in partials], *lands)
    return out[0], out[1], list(out[2:2 + n_p]), list(out[2 + n_p:2 + 2 * n_p]), out[-1]


def _scatter_wait(name, send_sems, recv_sems, partials, lands, after):
    n_p = len(partials)

    def body(*refs):
        p_refs, land_refs = refs[:n_p], refs[n_p:2 * n_p]
        send_sems, recv_sems = refs[2 * n_p:2 * n_p + 2]
        for cp in _chip_scatter_copies(p_refs, land_refs, send_sems, recv_sems):
            cp.wait_send()
            cp.wait_recv()

    out = pl.pallas_call(
        body, name=name,
        out_shape=(*[pltpu.HBM(p.shape, p.dtype) for p in partials], *[pltpu.HBM(l.shape, l.dtype) for l in lands]),
        in_specs=[*[HBM_SPEC] * (2 * n_p), SEM_SPEC, SEM_SPEC, ANY], out_specs=tuple([HBM_SPEC] * (2 * n_p)),
        input_output_aliases={i: i for i in range(2 * n_p)},
        compiler_params=pltpu.CompilerParams(has_side_effects=DATAFLOW),
    )(*partials, *lands, send_sems, recv_sems, after)
    return list(out[:n_p]), list(out[n_p:])


def _sum_chip_partials(name, grads, from_sibling, received, shard, layer, place):
    n, half, w = from_sibling.shape
    tr = _pack_row_tile(half)
    steps = half // tr

    def body(place_ref, g_ref, s_ref, r_ref, shard_ref, o_ref):
        own = g_ref[0] + s_ref[0]
        o_ref[...] = ((own + r_ref[0].astype(F32)) + r_ref[1].astype(F32)) + r_ref[2].astype(F32)

    return pl.pallas_call(
        body, name=name,
        grid_spec=pltpu.PrefetchScalarGridSpec(
            num_scalar_prefetch=1, grid=(steps,),
            in_specs=[pl.BlockSpec((1, tr, w), lambda i, place_ref: (place_ref[0], place_ref[1] * steps + i, 0)),
                      pl.BlockSpec((1, tr, w), lambda i, place_ref: (place_ref[0], i, 0)),
                      pl.BlockSpec((n - 1, tr, w), lambda i, place_ref: (0, i, 0)),
                      pl.BlockSpec(memory_space=pl.ANY)],
            out_specs=pl.BlockSpec((tr, w), lambda i, place_ref: ((2 * layer + place_ref[1]) * steps + i, 0))),
        out_shape=SDS(shard.shape, F32),
        input_output_aliases={4: 0},
        compiler_params=_params("parallel"),
    )(place, grads, from_sibling, received, shard)


def _join_halves(shards, layers):
    n_s = len(shards)
    n_sem = sum(layers)

    def body(*refs):
        in_refs = refs[:n_s]
        send_sems, recv_sems = refs[2 * n_s:]
        x, y, c = _my_place()
        copies, sem = [], 0
        for ref, n_layers in zip(in_refs, layers):
            half = ref.shape[0] // (2 * n_layers)
            for layer in range(n_layers):
                mine = pl.ds(pl.multiple_of(layer * 2 * half + c * half, 8), half)
                theirs = pl.ds(pl.multiple_of(layer * 2 * half + (1 - c) * half, 8), half)
                send = pltpu.make_async_remote_copy(
                    src_ref=ref.at[mine], dst_ref=ref.at[mine], send_sem=send_sems.at[sem], recv_sem=recv_sems.at[sem],
                    device_id=(x, y, 1 - c), device_id_type=MESH)
                send.start()
                arrival = pltpu.make_async_remote_copy(
                    src_ref=ref.at[theirs], dst_ref=ref.at[theirs], send_sem=send_sems.at[sem],
                    recv_sem=recv_sems.at[sem], device_id=(x, y, 1 - c), device_id_type=MESH)
                copies.append((send, arrival))
                sem += 1
        for send, arrival in copies:
            send.wait_send()
            arrival.wait_recv()

    return pl.pallas_call(
        body, name="join_halves", out_shape=[SDS(s.shape, s.dtype) for s in shards],
        in_specs=[ANY] * n_s, out_specs=[ANY] * n_s,
        input_output_aliases={i: i for i in range(n_s)},
        scratch_shapes=[pltpu.SemaphoreType.DMA((n_sem,)), pltpu.SemaphoreType.DMA((n_sem,))],
    )(*shards)


def _all_to_all_copies(v_ref, land_ref, send_sems, recv_sems):
    x, y, c = _my_place()
    me = 4 * x + 2 * y + c
    pairs = []
    for k in range(1, N_DEV):
        px, py, pc = (1 - x if k & 4 else x), (1 - y if k & 2 else y), (1 - c if k & 1 else c)
        sems = dict(send_sem=send_sems.at[k - 1], recv_sem=recv_sems.at[k - 1], device_id=(px, py, pc),
                    device_id_type=MESH)
        send = pltpu.make_async_remote_copy(src_ref=v_ref, dst_ref=land_ref.at[me], **sems)
        arrival = pltpu.make_async_remote_copy(src_ref=v_ref, dst_ref=land_ref.at[4 * px + 2 * py + pc], **sems)
        pairs.append((send, arrival))
    return pairs


def _small_reduce_start(name, v):
    def body(v_ref, land_ref, send_sems, recv_sems, v_out, land_out):
        for send, _ in _all_to_all_copies(v_ref, land_ref, send_sems, recv_sems):
            send.start()

    land = pltpu.with_memory_space_constraint(jnp.zeros((N_DEV,) + v.shape, v.dtype), pltpu.HBM)
    return pl.pallas_call(
        body, name=name,
        out_shape=(pltpu.SemaphoreType.DMA((N_DEV - 1,)), pltpu.SemaphoreType.DMA((N_DEV - 1,)),
                   pltpu.HBM(v.shape, v.dtype), pltpu.HBM(land.shape, land.dtype)),
        in_specs=[HBM_SPEC, HBM_SPEC], out_specs=(SEM_SPEC, SEM_SPEC, HBM_SPEC, HBM_SPEC),
        input_output_aliases={0: 2, 1: 3},
        compiler_params=pltpu.CompilerParams(has_side_effects=DATAFLOW),
    )(pltpu.with_memory_space_constraint(v, pltpu.HBM), land)


def _small_reduce_wait(name, send_sems, recv_sems, v, land, after):
    def body(v_ref, land_ref, send_sems, recv_sems, after_ref, v_out, land_out):
        for send, arrival in _all_to_all_copies(v_ref, land_ref, send_sems, recv_sems):
            send.wait_send()
            arrival.wait_recv()

    return pl.pallas_call(
        body, name=name, out_shape=(pltpu.HBM(v.shape, v.dtype), pltpu.HBM(land.shape, land.dtype)),
        in_specs=[HBM_SPEC, HBM_SPEC, SEM_SPEC, SEM_SPEC, ANY], out_specs=(HBM_SPEC, HBM_SPEC),
        input_output_aliases={0: 0, 1: 1},
        compiler_params=pltpu.CompilerParams(has_side_effects=DATAFLOW),
    )(v, land, send_sems, recv_sems, after)


def _sum_device_slots(name, v, land, me):
    r, w = v.shape

    def body(me_ref, v_ref, land_ref, o_ref):
        mine = v_ref[...]
        acc = jnp.where(me_ref[0] == 0, mine, land_ref[0])
        for d in range(1, N_DEV):
            acc = acc + jnp.where(me_ref[0] == d, mine, land_ref[d])
        o_ref[...] = acc

    return pl.pallas_call(
        body, name=name,
        grid_spec=pltpu.PrefetchScalarGridSpec(
            num_scalar_prefetch=1, grid=(1,),
            in_specs=[pl.BlockSpec((r, w), lambda i, me_ref: (0, 0)),
                      pl.BlockSpec((N_DEV, r, w), lambda i, me_ref: (0, 0, 0))],
            out_specs=pl.BlockSpec((r, w), lambda i, me_ref: (0, 0))),
        out_shape=SDS((r, w), F32),
        compiler_params=_params("arbitrary"),
    )(me, v, land)


def _adamw(name, w, g, m, v):
    rows, width = w.shape
    tr = _pack_row_tile(rows)

    def body(w_ref, g_ref, m_ref, v_ref, g_out_ref, d_ref, nm_ref, nv_ref):
        gg = g_ref[...]
        g_out_ref[...] = gg
        m_new = ADAM_B1 * m_ref[...] + (1.0 - ADAM_B1) * gg
        v_new = ADAM_B2 * v_ref[...] + (1.0 - ADAM_B2) * (gg * gg)
        m_hat = m_new / (1.0 - ADAM_B1 ** ADAM_STEP)
        v_hat = v_new / (1.0 - ADAM_B2 ** ADAM_STEP)
        d_ref[...] = -ADAM_LR * (m_hat / (jnp.sqrt(v_hat) + ADAM_EPS) + ADAM_WD * w_ref[...])
        nm_ref[...] = m_new
        nv_ref[...] = v_new

    spec = _rows(tr, width)
    return pl.pallas_call(
        body, name=name, grid=(rows // tr,),
        in_specs=[spec] * 4, out_specs=[spec] * 4,
        out_shape=[SDS((rows, width), F32)] * 4,
        compiler_params=_params("parallel"),
    )(w, g, m, v)


WEIGHT_NAMES = ['norm_mix', 'norm_ffn', 'attn_w_qkv', 'attn_b_qkv', 'attn_sinks', 'attn_w_o', 'attn_b_o',
                'conv_w_pw1', 'conv_b_pw1', 'conv_w_dw', 'conv_b_dw', 'conv_ln_g', 'conv_ln_b', 'conv_w_pw2',
                'conv_b_pw2', 'ffn_w1', 'ffn_w3', 'ffn_w2', 'norm_final']
BIG = ['attn_w_qkv', 'attn_w_o', 'conv_w_pw1', 'conv_w_pw2', 'ffn_w1', 'ffn_w3', 'ffn_w2']
COLUMN_SPLIT = ('attn_w_qkv', 'conv_w_pw1', 'ffn_w1', 'ffn_w3')
STORED_TRANSPOSED = ('attn_w_qkv', 'ffn_w1', 'ffn_w3')
SMALL_SPLIT = ['conv_b_pw1', 'conv_w_dw', 'conv_b_dw', 'conv_ln_g', 'conv_ln_b', 'conv_b_pw2']
SMALL_WHOLE = ['norm_mix', 'norm_ffn', 'attn_b_qkv', 'attn_sinks', 'attn_b_o', 'norm_final']


def _keeps_rows(shape):
    return len(shape) == 2 and shape[0] > 1 and shape[1] == PACK_W


def _pack_rows(arrays, dtype, row_multiple):
    blocks = [jnp.pad(a.astype(dtype), ((0, -a.shape[0] % V7X_SUBLANES), (0, 0)))
              for a in arrays if _keeps_rows(a.shape)]
    flat = jnp.concatenate([a.astype(dtype).reshape(-1) for a in arrays if not _keeps_rows(a.shape)])
    multiple = max(row_multiple, V7X_SUBLANES)
    rows = -(-(-(-flat.shape[0] // PACK_W)) // multiple) * multiple
    blocks.append(jnp.pad(flat, (0, rows * PACK_W - flat.shape[0])).reshape(rows, PACK_W))
    return jnp.concatenate(blocks, axis=0) if len(blocks) > 1 else blocks[0]


def _unpack_rows(pack, shapes):
    out, row = {}, 0
    for i, shape in enumerate(shapes):
        if _keeps_rows(shape):
            out[i] = pack[row:row + shape[0]]
            row += -(-shape[0] // V7X_SUBLANES) * V7X_SUBLANES
    flat, at = pack[row:].reshape(-1), 0
    for i, shape in enumerate(shapes):
        if not _keeps_rows(shape):
            size = 1
            for s in shape:
                size *= s
            out[i] = flat[at:at + size].reshape(shape)
            at += size
    return [out[i] for i in range(len(shapes))]


def _join_chip_axis(name, parts):
    axis = parts.ndim - 1 if name in COLUMN_SPLIT or name in SMALL_SPLIT else parts.ndim - 2
    moved = jnp.moveaxis(parts, 0, axis - 1)
    shape = list(moved.shape)
    shape[axis - 1:axis + 1] = [shape[axis - 1] * shape[axis]]
    return moved.reshape(shape)


def _split_chip_axis(name, whole, shard_shape):
    axis = len(shard_shape) - 1 if name in COLUMN_SPLIT or name in SMALL_SPLIT else len(shard_shape) - 2
    shape = list(whole.shape)
    shape[axis:axis + 1] = [N_CHIPS, shard_shape[axis]]
    return jnp.moveaxis(whole.reshape(shape), axis, 0)


def kernel(x, norm_mix, norm_ffn, attn_w_qkv, attn_b_qkv, attn_sinks, attn_w_o, attn_b_o, conv_w_pw1, conv_b_pw1, conv_w_dw, conv_b_dw, conv_ln_g, conv_ln_b, conv_w_pw2, conv_b_pw2, ffn_w1, ffn_w3, ffn_w2, norm_final, loss_target, m_norm_mix, m_norm_ffn, m_attn_w_qkv, m_attn_b_qkv, m_attn_sinks, m_attn_w_o, m_attn_b_o, m_conv_w_pw1, m_conv_b_pw1, m_conv_w_dw, m_conv_b_dw, m_conv_ln_g, m_conv_ln_b, m_conv_w_pw2, m_conv_b_pw2, m_ffn_w1, m_ffn_w3, m_ffn_w2, m_norm_final, v_norm_mix, v_norm_ffn, v_attn_w_qkv, v_attn_b_qkv, v_attn_sinks, v_attn_w_o, v_attn_b_o, v_conv_w_pw1, v_conv_b_pw1, v_conv_w_dw, v_conv_b_dw, v_conv_ln_g, v_conv_ln_b, v_conv_w_pw2, v_conv_b_pw2, v_ffn_w1, v_ffn_w3, v_ffn_w2, v_norm_final):
    w = dict(zip(WEIGHT_NAMES, (norm_mix, norm_ffn, attn_w_qkv, attn_b_qkv, attn_sinks, attn_w_o, attn_b_o,
                                conv_w_pw1, conv_b_pw1, conv_w_dw, conv_b_dw, conv_ln_g, conv_ln_b, conv_w_pw2,
                                conv_b_pw2, ffn_w1, ffn_w3, ffn_w2, norm_final)))
    m = dict(zip(WEIGHT_NAMES, (m_norm_mix, m_norm_ffn, m_attn_w_qkv, m_attn_b_qkv, m_attn_sinks, m_attn_w_o,
                                m_attn_b_o, m_conv_w_pw1, m_conv_b_pw1, m_conv_w_dw, m_conv_b_dw, m_conv_ln_g,
                                m_conv_ln_b, m_conv_w_pw2, m_conv_b_pw2, m_ffn_w1, m_ffn_w3, m_ffn_w2, m_norm_final)))
    v = dict(zip(WEIGHT_NAMES, (v_norm_mix, v_norm_ffn, v_attn_w_qkv, v_attn_b_qkv, v_attn_sinks, v_attn_w_o,
                                v_attn_b_o, v_conv_w_pw1, v_conv_b_pw1, v_conv_w_dw, v_conv_b_dw, v_conv_ln_g,
                                v_conv_ln_b, v_conv_w_pw2, v_conv_b_pw2, v_ffn_w1, v_ffn_w3, v_ffn_w2, v_norm_final)))
    T, D = x.shape[1], x.shape[2]
    c_idx = lax.axis_index("c").astype(jnp.int32).reshape(1)
    chip = (2 * lax.axis_index("x") + lax.axis_index("y")).astype(jnp.int32)

    def as_rows(n, a):
        a = jnp.swapaxes(a, -1, -2) if n in STORED_TRANSPOSED else a
        return a.reshape(-1, a.shape[-1])

    def from_rows(n, rows):
        shape = w[n].shape[:-2] + w[n].shape[:-3:-1] if n in STORED_TRANSPOSED else w[n].shape
        a = rows.reshape(shape)
        return jnp.swapaxes(a, -1, -2) if n in STORED_TRANSPOSED else a

    slabs = {n: _cast_into_slot(f"cast_{n}", lax.empty((N_CHIPS,) + as_rows(n, w[n]).shape, BF16), as_rows(n, w[n]),
                                chip.reshape(1)) for n in BIG}
    first, later = BIG[:2], BIG[2:]
    qkv_parts, w_o_parts = _gather_now("gather_attn", [slabs[n] for n in first])
    send_sems, recv_sems, travelling, gather_started = _gather_start("gather_start", [slabs[n] for n in later],
                                                                     qkv_parts)
    layers = ffn_w1.shape[0]
    small_shapes = [w[n].shape for n in SMALL_SPLIT]
    small_all = _gather_small("gather_small", _pack_rows([w[n] for n in SMALL_SPLIT], F32, 8))
    per_chip = [_unpack_rows(small_all[2 * j], small_shapes) for j in range(N_CHIPS)]
    full = {}
    for i, n in enumerate(SMALL_SPLIT):
        full[n] = _join_chip_axis(n, jnp.stack([per_chip[j][i] for j in range(N_CHIPS)]))

    def other_weights(after):
        landed = _gather_wait("gather_wait", send_sems, recv_sems, travelling, after)
        gathered = dict(zip(later, _swap_fetched_with_sibling("gather_swap", landed)))
        return {"conv_w_pw1": gathered["conv_w_pw1"], "conv_w_pw2": gathered["conv_w_pw2"].reshape(-1, D),
                "ffn_w1": gathered["ffn_w1"].reshape(N_CHIPS, layers, -1, D),
                "ffn_w3": gathered["ffn_w3"].reshape(N_CHIPS, layers, -1, D),
                "ffn_w2": gathered["ffn_w2"].reshape(N_CHIPS, layers, -1, D)}

    p = {
        "norm_mix": norm_mix, "norm_ffn": norm_ffn, "norm_final": norm_final.reshape(1, D),
        "attn_w_qkv": qkv_parts.reshape(-1, D), "attn_b_qkv": attn_b_qkv,
        "attn_sinks": attn_sinks, "attn_w_o": w_o_parts.reshape(-1, D), "attn_b_o": attn_b_o,
        "conv_b_pw1": full["conv_b_pw1"], "conv_w_dw": full["conv_w_dw"][0],
        "conv_b_dw": full["conv_b_dw"], "conv_ln_g": full["conv_ln_g"], "conv_ln_b": full["conv_ln_b"],
        "conv_b_pw2": full["conv_b_pw2"], "gather_started": gather_started, "other_weights": other_weights,
    }
    swapping, in_flight = {}, []

    def reduce_begin(tag, grads):
        keys = list(grads)
        *handles, begun = _sibling_swap_start(f"sibling_swap_start_{tag}", [grads[k] for k in keys])
        swapping[tag] = (keys, handles)
        return begun

    def reduce_send(tag, after):
        keys, (swap_send, swap_recv, grads, lands) = swapping[tag]
        grads, from_sibling = _sibling_swap_wait(f"sibling_swap_wait_{tag}", swap_send, swap_recv, grads, lands, after)
        partials = [_add_sibling_half(f"add_sibling_half_{tag}{i}", gr, fs, c_idx)
                    for i, (gr, fs) in enumerate(zip(grads, from_sibling))]
        *handles, sent = _scatter_start(f"scatter_start_{tag}", partials)
        in_flight.append((tag, keys, handles, grads, from_sibling))
        return sent

    loss_part, dx, g = _local_step(x[0], loss_target[0], p, reduce_begin, reduce_send)
    for n in SMALL_WHOLE + SMALL_SPLIT:
        g[n] = g[n].reshape((-1,) + g[n].shape[-2:]) if w[n].ndim == 3 else g[n].reshape(w[n].shape[:-1] + (-1,))

    small_pack = _pack_rows([loss_part] + [g[n] for n in SMALL_WHOLE] + [g[n] for n in SMALL_SPLIT], F32, 8)
    small_send, small_recv, small_pack, small_land = _small_reduce_start("small_reduce_start", small_pack)

    place = jnp.stack([chip, c_idx[0]])
    shard_grad = {n: lax.empty(as_rows(n, w[n]).shape, F32) for n in BIG}
    for tag, keys, (send_sems, recv_sems, partials, lands), grads, from_sibling in in_flight:
        _, received = _scatter_wait(f"scatter_wait_{tag}", send_sems, recv_sems, partials, lands, dx)
        for i, (n, layer) in enumerate(keys):
            shard_grad[n] = _sum_chip_partials(f"sum_chip_partials_{tag}{i}", grads[i], from_sibling[i], received[i],
                                               shard_grad[n], layer, place)
    g_big = dict(zip(BIG, _join_halves([shard_grad[n] for n in BIG], [w[n].shape[0] for n in BIG])))
    big_out = {}
    for n in BIG:
        step = _adamw(f"adamw_{n}", as_rows(n, w[n]), g_big[n], as_rows(n, m[n]), as_rows(n, v[n]))
        big_out[n] = [from_rows(n, a) for a in step]

    small_whole_shapes = [w[n].shape for n in SMALL_WHOLE]
    small_full_shapes = [g[n].shape for n in SMALL_SPLIT]
    small_pack, small_land = _small_reduce_wait("small_reduce_wait", small_send, small_recv, small_pack, small_land,
                                                big_out[BIG[-1]][1])
    reduced = _sum_device_slots("small_reduce_sum", small_pack, small_land, (2 * chip + c_idx[0]).reshape(1))
    pieces = _unpack_rows(reduced, [(1,)] + small_whole_shapes + small_full_shapes)
    loss = pieces[0].reshape(())
    g_small = dict(zip(SMALL_WHOLE, pieces[1:1 + len(SMALL_WHOLE)]))
    for n, whole in zip(SMALL_SPLIT, pieces[1 + len(SMALL_WHOLE):]):
        parts = _split_chip_axis(n, whole, w[n].shape)
        g_small[n] = lax.dynamic_index_in_dim(parts, chip, axis=0, keepdims=False)
    small = SMALL_WHOLE + SMALL_SPLIT
    _, d_small, m_small, v_small = _adamw(
        "adamw_small", _pack_rows([w[n] for n in small], F32, 8), _pack_rows([g_small[n] for n in small], F32, 8),
        _pack_rows([m[n] for n in small], F32, 8), _pack_rows([v[n] for n in small], F32, 8))

    outs = {}
    for slot, (tag, small_pack) in enumerate((("g", None), ("d", d_small), ("m", m_small), ("v", v_small))):
        vals = {n: big_out[n][slot] for n in BIG}
        if small_pack is None:
            vals.update(g_small)
        else:
            vals.update(zip(small, _unpack_rows(small_pack, [w[n].shape for n in small])))
        outs[tag] = vals
    return (loss, dx.reshape(1, T, D), *[outs["g"][n] for n in WEIGHT_NAMES], *[outs["d"][n] for n in WEIGHT_NAMES],
            *[outs["m"][n] for n in WEIGHT_NAMES], *[outs["v"][n] for n in WEIGHT_NAMES])
```

```python
import functools

import jax
import jax.numpy as jnp
from jax import lax
from jax.experimental import pallas as pl
from jax.experimental.pallas import tpu as pltpu

F32 = jnp.float32
BF16 = jnp.bfloat16
SDS = jax.ShapeDtypeStruct
MESH = pl.DeviceIdType.MESH

HEAD_DIM = 64
N_Q_HEADS = 16
N_KV_HEADS = 2
Q_PER_KV = N_Q_HEADS // N_KV_HEADS
ATTN_BLOCK = 128
ROPE_THETA = 10000.0
CONV_WIDTH = 31
CONV_HALO = 32
CONV_FIRST_TAP = CONV_HALO - CONV_WIDTH + 1
CONV_ROW_CHUNK = 64
CONV_LANE_CHUNK = 256
CONV_GRAD_UNROLL = 8
RMS_EPS = 1e-5
LN_EPS = 1e-5
ADAM_LR = 0.001
ADAM_B1 = 0.9
ADAM_B2 = 0.999
ADAM_EPS = 1e-08
ADAM_WD = 0.01
ADAM_STEP = 10

V7X_LANES = 128
V7X_SUBLANES = 8
V7X_VMEM_LIMIT_BYTES = 56 * 1024 * 1024

N_CHIPS = 4
N_DEV = 8
PACK_W = 1024

MASK_VALUE = -1e30


def _params(*semantics):
    return pltpu.CompilerParams(dimension_semantics=semantics, vmem_limit_bytes=V7X_VMEM_LIMIT_BYTES)


def _rows(tm, width):
    return pl.BlockSpec((tm, width), lambda i: (i, 0))


def _whole(shape):
    return pl.BlockSpec(shape, lambda *_: (0,) * len(shape))


def _rms_rstd(h):
    return lax.rsqrt(jnp.mean(h * h, axis=-1, keepdims=True) + RMS_EPS)


def _silu_and_grad(z):
    sg = jax.nn.sigmoid(z)
    return z * sg, sg * (1.0 + z * (1.0 - sg))


def _swap_rope_halves(t):
    w = t.shape[1]
    half = HEAD_DIM // 2
    lane = lax.broadcasted_iota(jnp.int32, t.shape, 1)
    upper = pltpu.roll(t, w - half, 1)
    lower = pltpu.roll(t, half, 1)
    return jnp.where(lane % HEAD_DIM < half, upper, lower)


def _rope(t, cos_ref, sin_ref):
    reps = t.shape[1] // V7X_LANES
    c = jnp.tile(cos_ref[...], (1, reps))
    s = jnp.tile(sin_ref[...], (1, reps))
    return t * c + _swap_rope_halves(t) * s


def _rope_transposed(dt, cos_ref, sin_ref):
    reps = dt.shape[1] // V7X_LANES
    c = jnp.tile(cos_ref[...], (1, reps))
    s = jnp.tile(sin_ref[...], (1, reps))
    return dt * c + _swap_rope_halves(dt * s)


def _rope_tables(seq_len):
    pos = jnp.arange(seq_len, dtype=F32)
    inv_freq = ROPE_THETA ** (-jnp.arange(0, HEAD_DIM, 2, dtype=F32) / HEAD_DIM)
    ang = pos[:, None] * jnp.tile(inv_freq, 2 * V7X_LANES // HEAD_DIM)[None, :]
    upper_half = jnp.arange(V7X_LANES) % HEAD_DIM >= HEAD_DIM // 2
    return jnp.cos(ang), jnp.where(upper_half[None, :], jnp.sin(ang), -jnp.sin(ang))


def _qkv_proj(h, g, w, b, cos, sin, after):
    T, D = h.shape
    N = w.shape[0]
    tm = min(512, T)
    rope_w = N - N_KV_HEADS * HEAD_DIM

    def body(h_ref, g_ref, w_ref, b_ref, cos_ref, sin_ref, _, y_ref, o_ref):
        hh = h_ref[...]
        y = (hh * _rms_rstd(hh) * g_ref[...]).astype(BF16)
        y_ref[...] = y
        acc = _dot_nt(y, w_ref[...]) + b_ref[...]
        o_ref[:, :rope_w] = _rope(acc[:, :rope_w], cos_ref, sin_ref).astype(BF16)
        o_ref[:, rope_w:] = acc[:, rope_w:].astype(BF16)

    return pl.pallas_call(
        body, name="qkv_proj", grid=(T // tm,),
        in_specs=[_rows(tm, D), _whole((1, D)), _whole((N, D)), _whole((1, N)),
                  _rows(tm, V7X_LANES), _rows(tm, V7X_LANES), pl.BlockSpec(memory_space=pl.ANY)],
        out_specs=[_rows(tm, D), _rows(tm, N)],
        out_shape=[SDS((T, D), BF16), SDS((T, N), BF16)],
        compiler_params=_params("parallel"),
    )(h, g, w, b, cos, sin, after)


def _pw1_proj(h, g, w, b):
    T, D = h.shape
    n = w.shape[2]
    N = N_CHIPS * n
    tm = min(512, T)

    def body(h_ref, g_ref, w_ref, b_ref, y_ref, o_ref):
        hh = h_ref[...]
        y = (hh * _rms_rstd(hh) * g_ref[...]).astype(BF16)
        y_ref[...] = y
        for j in range(N_CHIPS):
            cols = slice(j * n, (j + 1) * n)
            o_ref[:, cols] = jnp.dot(y, w_ref[j], preferred_element_type=F32) + b_ref[:, cols]

    return pl.pallas_call(
        body, name="pw1_proj", grid=(T // tm,),
        in_specs=[_rows(tm, D), _whole((1, D)), _whole((N_CHIPS, D, n)), _whole((1, N))],
        out_specs=[_rows(tm, D), _rows(tm, N)],
        out_shape=[SDS((T, D), BF16), SDS((T, N), F32)],
        compiler_params=_params("parallel"),
    )(h, g, w, b)


PAIRS_PER_KV = Q_PER_KV // 2


def _upper_lanes(shape):
    return lax.broadcasted_iota(jnp.int32, shape, 1) >= HEAD_DIM


def _swap_lane_halves(t):
    return pltpu.roll(t.astype(F32), HEAD_DIM, 1).astype(t.dtype)


def _kv_operands(g, t):
    swapped = _swap_lane_halves(t)
    in_lower, in_upper = (t, swapped) if g == 0 else (swapped, t)
    upper = _upper_lanes(t.shape)
    zero = jnp.zeros_like(t)
    return jnp.where(upper, zero, in_lower), jnp.where(upper, in_upper, zero)


def _group_heads(g):
    pairs = range(g * PAIRS_PER_KV, (g + 1) * PAIRS_PER_KV)
    return [2 * hp for hp in pairs] + [2 * hp + 1 for hp in pairs]


def _all_heads():
    return [h for g in range(N_KV_HEADS) for h in _group_heads(g)]


def _pair_rows(ref, g):
    pairs = range(g * PAIRS_PER_KV, (g + 1) * PAIRS_PER_KV)
    return jnp.concatenate([ref[:, hp * 2 * HEAD_DIM:(hp + 1) * 2 * HEAD_DIM] for hp in pairs], axis=0)


def _from_previous_block(rows):
    row = lax.broadcasted_iota(jnp.int32, (ATTN_BLOCK, ATTN_BLOCK), 0)
    col = lax.broadcasted_iota(jnp.int32, (ATTN_BLOCK, ATTN_BLOCK), 1)
    return jnp.concatenate([col > row] * (rows // ATTN_BLOCK), axis=0)


def _folded_probs(n, q_groups, k_prev_groups, k_cur_groups, sink_ref, prev_part):
    def scores(q, k):
        return lax.dot_general(q, k, (((1,), (1,)), ((), ())), preferred_element_type=F32)

    s_prev = jnp.concatenate([scores(q, k[i]) for q, k in zip(q_groups, k_prev_groups) for i in range(2)], axis=0)
    s_cur = jnp.concatenate([scores(q, k[i]) for q, k in zip(q_groups, k_cur_groups) for i in range(2)], axis=0)
    s_prev = jnp.where(n > 0, s_prev, MASK_VALUE * (HEAD_DIM ** 0.5))
    s = jnp.where(prev_part, s_prev, s_cur) * (HEAD_DIM ** -0.5)
    heads = [h for g in range(N_KV_HEADS) for h in _group_heads(g)]
    sink = jnp.concatenate([jnp.broadcast_to(sink_ref[0:1, h:h + 1], (ATTN_BLOCK, 1)) for h in heads], axis=0)
    m = jnp.maximum(jnp.max(s, axis=1, keepdims=True), sink)
    p = jnp.exp(s - m)
    e_sink = jnp.exp(sink - m)
    inv = 1.0 / (jnp.sum(p, axis=1, keepdims=True) + e_sink)
    return p * inv, e_sink * inv


def _split_folded(t, prev_part):
    tb = t.astype(BF16)
    zero = jnp.zeros_like(tb)
    return jnp.where(prev_part, tb, zero), jnp.where(prev_part, zero, tb)


def _attn_specs(T):
    nb = T // ATTN_BLOCK
    kcol = N_Q_HEADS * HEAD_DIM // V7X_LANES
    cur = lambda n: jnp.minimum(n, nb - 1)
    prev = lambda n: jnp.maximum(jnp.minimum(n, nb - 1) - 1, 0)
    q_spec = pl.BlockSpec((ATTN_BLOCK, N_Q_HEADS * HEAD_DIM), lambda n: (cur(n), 0))
    kc_spec = pl.BlockSpec((ATTN_BLOCK, V7X_LANES), lambda n: (cur(n), kcol))
    kp_spec = pl.BlockSpec((ATTN_BLOCK, V7X_LANES), lambda n: (prev(n), kcol))
    vc_spec = pl.BlockSpec((ATTN_BLOCK, V7X_LANES), lambda n: (cur(n), kcol + 1))
    vp_spec = pl.BlockSpec((ATTN_BLOCK, V7X_LANES), lambda n: (prev(n), kcol + 1))
    return q_spec, kc_spec, kp_spec, vc_spec, vp_spec


def _attn_fwd(qkv, sinks):
    T = qkv.shape[0]
    nb = T // ATTN_BLOCK
    qw = N_Q_HEADS * HEAD_DIM
    all_rows = N_Q_HEADS * ATTN_BLOCK

    def body(q_ref, kc_ref, kp_ref, vc_ref, vp_ref, sink_ref, o_ref, probs_ref, psink_ref):
        n = pl.program_id(0)
        prev_part = _from_previous_block(all_rows)
        half = PAIRS_PER_KV * ATTN_BLOCK
        groups = range(N_KV_HEADS)
        probs, p_sink = _folded_probs(n, [_pair_rows(q_ref, g) for g in groups],
                                      [_kv_operands(g, kp_ref[...]) for g in groups],
                                      [_kv_operands(g, kc_ref[...]) for g in groups], sink_ref, prev_part)
        probs_ref[...] = probs.astype(BF16)
        lane = lax.broadcasted_iota(jnp.int32, (ATTN_BLOCK, V7X_LANES), 1)
        sink_tile = jnp.zeros((ATTN_BLOCK, V7X_LANES), F32)
        for i, h in enumerate(_all_heads()):
            sink_tile = jnp.where(lane == h, p_sink[i * ATTN_BLOCK:(i + 1) * ATTN_BLOCK], sink_tile)
        psink_ref[...] = sink_tile
        p_prev, p_cur = _split_folded(probs, prev_part)
        for g in groups:
            v_prev, v_cur = _kv_operands(g, vp_ref[...]), _kv_operands(g, vc_ref[...])
            even, odd = slice(2 * g * half, (2 * g + 1) * half), slice((2 * g + 1) * half, (2 * g + 2) * half)
            o = (jnp.dot(p_prev[even], v_prev[0], preferred_element_type=F32)
                 + jnp.dot(p_cur[even], v_cur[0], preferred_element_type=F32)
                 + jnp.dot(p_prev[odd], v_prev[1], preferred_element_type=F32)
                 + jnp.dot(p_cur[odd], v_cur[1], preferred_element_type=F32))
            for i in range(PAIRS_PER_KV):
                hp = g * PAIRS_PER_KV + i
                o_ref[:, hp * 2 * HEAD_DIM:(hp + 1) * 2 * HEAD_DIM] = (
                    o[i * ATTN_BLOCK:(i + 1) * ATTN_BLOCK].astype(BF16))

    return pl.pallas_call(
        body, name="attn_fwd", grid=(nb,),
        in_specs=[*_attn_specs(T), _whole((1, N_Q_HEADS))],
        out_specs=[_rows(ATTN_BLOCK, qw), pl.BlockSpec((None, all_rows, ATTN_BLOCK), lambda n: (n, 0, 0)),
                   _rows(ATTN_BLOCK, V7X_LANES)],
        out_shape=[SDS((T, qw), BF16), SDS((nb, all_rows, ATTN_BLOCK), BF16), SDS((T, V7X_LANES), F32)],
        compiler_params=_params("parallel"),
    )(qkv, qkv, qkv, qkv, qkv, sinks)


def _mm_res(name, a, w, b, res, g):
    T, K = a.shape
    D = w.shape[1]
    tm = min(512, T)

    def body(a_ref, w_ref, b_ref, r_ref, g_ref, o_ref, f_ref):
        h = jnp.dot(a_ref[...], w_ref[...], preferred_element_type=F32) + b_ref[...] + r_ref[...]
        o_ref[...] = h
        f_ref[...] = (h * _rms_rstd(h) * g_ref[...]).astype(BF16)

    return pl.pallas_call(
        body, name=name, grid=(T // tm,),
        in_specs=[_rows(tm, K), _whole((K, D)), _whole((1, D)), _rows(tm, D), _whole((1, D))],
        out_specs=[_rows(tm, D), _rows(tm, D)],
        out_shape=[SDS((T, D), F32), SDS((T, D), BF16)],
        compiler_params=_params("parallel"),
    )(a, w, b, res, g)


def _ffn_down(name, s, w2, layer, res):
    _, T, n = s.shape
    D = w2.shape[3]
    tm = min(512, T)

    def body(s_ref, w_ref, r_ref, o_ref):
        acc = r_ref[...]
        for j in range(N_CHIPS):
            acc = acc + jnp.dot(s_ref[j], w_ref[j], preferred_element_type=F32)
        o_ref[...] = acc

    return pl.pallas_call(
        body, name=name, grid=(T // tm,),
        in_specs=[pl.BlockSpec((N_CHIPS, tm, n), lambda i: (0, i, 0)),
                  pl.BlockSpec((N_CHIPS, None, n, D), lambda i: (0, layer, 0, 0)), _rows(tm, D)],
        out_specs=_rows(tm, D),
        out_shape=SDS((T, D), F32),
        compiler_params=_params("parallel"),
    )(s, w2, res)


def _ffn_up(name, f, w1, w3, layer):
    T, D = f.shape
    n = w1.shape[2]
    tm = min(1024, T)

    def body(f_ref, w1_ref, w3_ref, act_ref, gg_ref, s_ref):
        ff = f_ref[...]
        g1 = _dot_nt(ff, w1_ref[...])
        g3 = _dot_nt(ff, w3_ref[...])
        act, dact = _silu_and_grad(g1)
        act_ref[...] = act.astype(BF16)
        gg_ref[...] = (g3 * dact).astype(BF16)
        s_ref[...] = (act * g3).astype(BF16)

    slab = pl.BlockSpec((None, tm, n), lambda j, i: (j, i, 0))
    wslab = pl.BlockSpec((None, None, n, D), lambda j, i: (j, layer, 0, 0))
    hidden = SDS((N_CHIPS, T, n), BF16)
    return pl.pallas_call(
        body, name=name, grid=(N_CHIPS, T // tm),
        in_specs=[pl.BlockSpec((tm, D), lambda j, i: (i, 0)), wslab, wslab],
        out_specs=[slab, slab, slab],
        out_shape=[hidden, hidden, hidden],
        compiler_params=_params("parallel", "parallel"),
    )(f, w1, w3)


def _glu(a, d):
    return a[:, :d] * jax.nn.sigmoid(a[:, d:])


def _conv_tile(T):
    return min(256, T)


def _fill_shifted(sh_ref, tc):
    n = tc + CONV_HALO - V7X_SUBLANES
    for r in range(1, V7X_SUBLANES):
        sh_ref[r, 0:n, :] = sh_ref[0, pl.ds(r, n), :]


def _depthwise_taps(sh_ref, w_ref, offsets, bias_ref, out_ref, tc):
    D = out_ref.shape[1]

    def chunk(i, carry):
        t0 = pl.multiple_of(i * CONV_ROW_CHUNK, CONV_ROW_CHUNK)
        for cb in range(D // CONV_LANE_CHUNK):
            cs = slice(cb * CONV_LANE_CHUNK, (cb + 1) * CONV_LANE_CHUNK)
            acc = jnp.zeros((CONV_ROW_CHUNK, CONV_LANE_CHUNK), F32)
            for r in range(V7X_SUBLANES):
                taps = [(j, o // V7X_SUBLANES) for j, o in enumerate(offsets) if o % V7X_SUBLANES == r]
                if not taps:
                    continue
                span = CONV_ROW_CHUNK + V7X_SUBLANES * max(q for _, q in taps)
                rows = sh_ref[r, pl.ds(t0, span), cs]
                for j, q in taps:
                    acc = acc + rows[V7X_SUBLANES * q:V7X_SUBLANES * q + CONV_ROW_CHUNK] * w_ref[j:j + 1, cs]
            if bias_ref is not None:
                acc = acc + bias_ref[:, cs]
            out_ref[pl.ds(t0, CONV_ROW_CHUNK), cs] = acc
        return carry

    lax.fori_loop(0, tc // CONV_ROW_CHUNK, chunk, 0)


def _depthwise_tap_grads(dy_sh, x_sh, offsets, dw_ref, tc):
    D = dw_ref.shape[1]
    for cb in range(D // V7X_LANES):
        cs = slice(cb * V7X_LANES, (cb + 1) * V7X_LANES)

        def row_tiles(i, accs, cs=cs):
            for k in range(CONV_GRAD_UNROLL):
                t0 = pl.multiple_of(i * (CONV_GRAD_UNROLL * V7X_SUBLANES), V7X_SUBLANES) + k * V7X_SUBLANES
                d = dy_sh[0, pl.ds(t0, V7X_SUBLANES), cs]
                accs = tuple(
                    acc + d * x_sh[o % V7X_SUBLANES, pl.ds(t0 + o // V7X_SUBLANES * V7X_SUBLANES, V7X_SUBLANES), cs]
                    for acc, o in zip(accs, offsets))
            return accs

        zero = jnp.zeros((V7X_SUBLANES, V7X_LANES), F32)
        accs = lax.fori_loop(0, tc // (CONV_GRAD_UNROLL * V7X_SUBLANES), row_tiles, tuple(zero for _ in offsets))
        for j, acc in enumerate(accs):
            dw_ref[j:j + 1, cs] += jnp.sum(acc, axis=0, keepdims=True)


def _conv_fwd(a, w_dw, b_dw, ln_g, ln_b):
    T = a.shape[0]
    D = a.shape[1] // 2
    tc = _conv_tile(T)
    per = tc // CONV_HALO

    def body(a_ref, ah_ref, w_ref, bdw_ref, lg_ref, lb_ref, c_ref, act_ref, u_sh):
        i = pl.program_id(0)
        u_sh[0, 0:CONV_HALO, :] = jnp.where(i > 0, _glu(ah_ref[...], D), 0.0)
        u_sh[0, CONV_HALO:, :] = _glu(a_ref[...], D)
        _fill_shifted(u_sh, tc)
        _depthwise_taps(u_sh, w_ref, [CONV_FIRST_TAP + j for j in range(CONV_WIDTH)], bdw_ref, c_ref, tc)
        c = c_ref[...]
        xc = c - jnp.mean(c, axis=-1, keepdims=True)
        z = xc * lax.rsqrt(jnp.mean(xc * xc, axis=-1, keepdims=True) + LN_EPS)
        l = z * lg_ref[...] + lb_ref[...]
        act_ref[...] = (l * jax.nn.sigmoid(l)).astype(BF16)

    return pl.pallas_call(
        body, name="conv_fwd", grid=(T // tc,),
        in_specs=[_rows(tc, 2 * D),
                  pl.BlockSpec((CONV_HALO, 2 * D), lambda i: (jnp.maximum(i * per - 1, 0), 0)),
                  _whole((CONV_WIDTH, D)), _whole((1, D)), _whole((1, D)), _whole((1, D))],
        out_specs=[_rows(tc, D), _rows(tc, D)],
        out_shape=[SDS((T, D), F32), SDS((T, D), BF16)],
        scratch_shapes=[pltpu.VMEM((V7X_SUBLANES, tc + CONV_HALO, D), F32)],
        compiler_params=_params("parallel"),
    )(a, a, w_dw, b_dw, ln_g, ln_b)


def _ffn_down_loss(name, s, w2, layer, res, g, target):
    _, T, n = s.shape
    D = w2.shape[3]
    tm = min(512, T)

    def body(s_ref, w_ref, r_ref, g_ref, t_ref, dh_ref, loss_ref, dg_ref):
        @pl.when(pl.program_id(0) == 0)
        def _():
            loss_ref[...] = jnp.zeros_like(loss_ref)
            dg_ref[...] = jnp.zeros_like(dg_ref)

        hh = r_ref[...]
        for j in range(N_CHIPS):
            hh = hh + jnp.dot(s_ref[j], w_ref[j], preferred_element_type=F32)
        r = _rms_rstd(hh)
        g = g_ref[...]
        d = hh * r * g - t_ref[...]
        loss_ref[...] += 0.5 * jnp.sum(jnp.mean(d * d, axis=-1, keepdims=True), axis=0, keepdims=True)
        dout = d * (1.0 / D)
        dg_ref[...] += jnp.sum(dout * (hh * r), axis=0, keepdims=True)
        dxh = dout * g
        dh_ref[...] = r * dxh - hh * (r * r * r) * jnp.mean(dxh * hh, axis=-1, keepdims=True)

    return pl.pallas_call(
        body, name=name, grid=(T // tm,),
        in_specs=[pl.BlockSpec((N_CHIPS, tm, n), lambda i: (0, i, 0)),
                  pl.BlockSpec((N_CHIPS, None, n, D), lambda i: (0, layer, 0, 0)), _rows(tm, D),
                  _whole((1, D)), _rows(tm, D)],
        out_specs=[_rows(tm, D), _whole((1, 1)), _whole((1, D))],
        out_shape=[SDS((T, D), F32), SDS((1, 1), F32), SDS((1, D), F32)],
        compiler_params=_params("arbitrary"),
    )(s, w2, res, g, target)


def _ffn_bwd_down(name, dh, w2, layer, act, gate_grad, s, after=None):
    T, D = dh.shape
    n = w2.shape[2]
    tm = min(256, T)

    def body(dh_ref, w2_ref, act_ref, gg_ref, s_ref, *rest):
        dg1_ref, dg3_ref, dw_ref = rest[-3:]

        @pl.when(pl.program_id(0) == 0)
        def _():
            dw_ref[...] = jnp.zeros_like(dw_ref)

        dhb = dh_ref[...].astype(BF16)
        for j in range(N_CHIPS):
            ds = _dot_nt(dhb, w2_ref[j])
            dg1_ref[j] = (ds * gg_ref[j].astype(F32)).astype(BF16)
            dg3_ref[j] = (ds * act_ref[j].astype(F32)).astype(BF16)
            dw_ref[j] += _dot_tn(s_ref[j], dhb)

    slabs = pl.BlockSpec((N_CHIPS, tm, n), lambda i: (0, i, 0))
    hidden = SDS((N_CHIPS, T, n), BF16)
    return pl.pallas_call(
        body, name=name, grid=(T // tm,),
        in_specs=[_rows(tm, D),
                  pl.BlockSpec((N_CHIPS, None, n, D), lambda i: (0, layer, 0, 0), pipeline_mode=pl.Buffered(1)),
                  slabs, slabs, slabs] + ([] if after is None else [pl.BlockSpec(memory_space=pl.ANY)]),
        out_specs=[slabs, slabs, _whole((N_CHIPS, n, D))],
        out_shape=[hidden, hidden, SDS((N_CHIPS, n, D), F32)],
        compiler_params=_params("arbitrary"),
    )(dh, w2, act, gate_grad, s, *([] if after is None else [after]))


def _dot_tn(a, b):
    return lax.dot_general(a.astype(BF16), b.astype(BF16), (((0,), (0,)), ((), ())), preferred_element_type=F32)


def _dot_nt(a, b):
    return lax.dot_general(a.astype(BF16), b, (((1,), (1,)), ((), ())), preferred_element_type=F32)


def _mm_tn(name, a, b, col_chunks=1, after=None):
    a_slabs, b_slabs = a.ndim == 3, b.ndim == 3
    T = a.shape[-2]
    tt = min(1024, T)
    ka, nb = a.shape[-1], b.shape[-1]
    if a_slabs or b_slabs:
        out_dims = (N_CHIPS, ka, nb)
    elif col_chunks > 1:
        out_dims = (col_chunks, ka, nb // col_chunks)
    else:
        out_dims = (ka, nb)

    def body(a_ref, b_ref, *rest):
        o_ref = rest[-1]

        @pl.when(pl.program_id(0) == 0)
        def _():
            o_ref[...] = jnp.zeros_like(o_ref)

        if a_slabs:
            bb = b_ref[...].astype(BF16)
            for j in range(N_CHIPS):
                o_ref[j] += _dot_tn(a_ref[j], bb)
        elif b_slabs:
            aa = a_ref[...].astype(BF16)
            for j in range(N_CHIPS):
                o_ref[j] += _dot_tn(aa, b_ref[j])
        elif col_chunks > 1:
            aa = a_ref[...].astype(BF16)
            w = nb // col_chunks
            for j in range(col_chunks):
                o_ref[j] += _dot_tn(aa, b_ref[:, j * w:(j + 1) * w])
        else:
            o_ref[...] += _dot_tn(a_ref[...], b_ref[...])

    def spec(arr, slabs):
        if slabs:
            return pl.BlockSpec((N_CHIPS, tt, arr.shape[-1]), lambda t: (0, t, 0))
        return _rows(tt, arr.shape[-1])

    return pl.pallas_call(
        body, name=name, grid=(T // tt,),
        in_specs=[spec(a, a_slabs), spec(b, b_slabs)] + ([] if after is None else [pl.BlockSpec(memory_space=pl.ANY)]),
        out_specs=_whole(out_dims),
        out_shape=SDS(out_dims, F32),
        compiler_params=_params("arbitrary"),
    )(a, b, *([] if after is None else [after]))


def _mm_nt_normbwd(name, pairs, h, g, dh, after):
    T, D = h.shape
    tm = min(512, T)
    n_pairs = len(pairs)
    kinds = ["slabs" if dy.ndim == 3 else ("quarters" if w.ndim == 3 else "plain") for dy, w, _ in pairs]

    def body(*refs):
        dy_refs = refs[:n_pairs]
        w_refs = refs[n_pairs:2 * n_pairs]
        h_ref, g_ref, dh_ref, _, o_ref, dg_ref, cs_ref = refs[2 * n_pairs:]

        @pl.when(pl.program_id(0) == 0)
        def _():
            dg_ref[...] = jnp.zeros_like(dg_ref)
            cs_ref[...] = jnp.zeros_like(cs_ref)

        df = jnp.zeros((tm, D), F32)
        for dy_ref, w_ref, kd in zip(dy_refs, w_refs, kinds):
            if kd == "slabs":
                for j in range(N_CHIPS):
                    df = df + jnp.dot(dy_ref[j], w_ref[j], preferred_element_type=F32)
            elif kd == "quarters":
                n = w_ref.shape[2]
                for j in range(N_CHIPS):
                    df = df + _dot_nt(dy_ref[:, j * n:(j + 1) * n], w_ref[j])
            else:
                df = df + jnp.dot(dy_ref[...], w_ref[...], preferred_element_type=F32)
        hh = h_ref[...]
        r = _rms_rstd(hh)
        dg_ref[...] += jnp.sum(df * (hh * r), axis=0, keepdims=True)
        dxh = df * g_ref[...]
        out = dh_ref[...] + (r * dxh - hh * (r * r * r) * jnp.mean(dxh * hh, axis=-1, keepdims=True))
        o_ref[...] = out
        cs_ref[...] += jnp.sum(out, axis=0, keepdims=True)

    dy_specs, w_specs = [], []
    for (dy, w, layer), kd in zip(pairs, kinds):
        if kd == "slabs":
            dy_specs.append(pl.BlockSpec((N_CHIPS, tm, dy.shape[2]), lambda i: (0, i, 0)))
            w_specs.append(pl.BlockSpec((N_CHIPS, None, w.shape[2], D),
                                        functools.partial(lambda i, layer: (0, layer, 0, 0), layer=layer),
                                        pipeline_mode=pl.Buffered(1)))
        else:
            dy_specs.append(_rows(tm, dy.shape[1]))
            w_specs.append(_whole(w.shape))

    return pl.pallas_call(
        body, name=name, grid=(T // tm,),
        in_specs=[*dy_specs, *w_specs, _rows(tm, D), _whole((1, D)), _rows(tm, D), pl.BlockSpec(memory_space=pl.ANY)],
        out_specs=[_rows(tm, D), _whole((1, D)), _whole((1, D))],
        out_shape=[SDS((T, D), F32), SDS((1, D), F32), SDS((1, D), F32)],
        compiler_params=_params("arbitrary"),
    )(*[dy for dy, _, _ in pairs], *[w for _, w, _ in pairs], h, g, dh, after)


def _mm_nt(name, dy, w, out_dtype):
    T, N = dy.shape
    K = w.shape[0]
    tm = min(512, T)

    def body(dy_ref, w_ref, o_ref):
        o_ref[...] = lax.dot_general(dy_ref[...].astype(BF16), w_ref[...], (((1,), (1,)), ((), ())),
                                     preferred_element_type=F32).astype(out_dtype)

    return pl.pallas_call(
        body, name=name, grid=(T // tm,),
        in_specs=[_rows(tm, N), _whole((K, N))],
        out_specs=_rows(tm, K),
        out_shape=SDS((T, K), out_dtype),
        compiler_params=_params("parallel"),
    )(dy, w)


def _conv_bwd(dact, c, a, w_dw, ln_g, ln_b):
    T, D = c.shape
    tc = _conv_tile(T)
    per = tc // CONV_HALO
    n_tiles = T // tc
    last_halo = T // CONV_HALO - 1

    def ln_bwd(dact_v, c_v, lg, lb):
        xc = c_v - jnp.mean(c_v, axis=-1, keepdims=True)
        rstd = lax.rsqrt(jnp.mean(xc * xc, axis=-1, keepdims=True) + LN_EPS)
        z = xc * rstd
        _, dsilu = _silu_and_grad(z * lg + lb)
        dl = dact_v * dsilu
        dz = dl * lg
        dc = rstd * (dz - jnp.mean(dz, axis=-1, keepdims=True) - z * jnp.mean(dz * z, axis=-1, keepdims=True))
        return dc, dl, z

    def body(dact_ref, dactn_ref, c_ref, cn_ref, a_ref, ah_ref, w_ref, lg_ref, lb_ref,
             da_ref, dlg_ref, dlb_ref, dbdw_ref, dwdw_ref, dbpw1_ref, dc_sh, u_sh, du_scr):
        i = pl.program_id(0)

        @pl.when(i == 0)
        def _():
            for ref in (dlg_ref, dlb_ref, dbdw_ref, dwdw_ref, dbpw1_ref):
                ref[...] = jnp.zeros_like(ref)

        lg, lb = lg_ref[...], lb_ref[...]
        dc, dl, z = ln_bwd(dact_ref[...], c_ref[...], lg, lb)
        dlg_ref[...] += jnp.sum(dl * z, axis=0, keepdims=True)
        dlb_ref[...] += jnp.sum(dl, axis=0, keepdims=True)
        dbdw_ref[...] += jnp.sum(dc, axis=0, keepdims=True)
        dcn, _, _ = ln_bwd(dactn_ref[...], cn_ref[...], lg, lb)
        dc_sh[0, 0:tc, :] = dc
        dc_sh[0, tc:, :] = jnp.where(i < n_tiles - 1, dcn, 0.0)
        _fill_shifted(dc_sh, tc)

        a_v = a_ref[...]
        a1 = a_v[:, :D]
        sg = jax.nn.sigmoid(a_v[:, D:])
        u_sh[0, 0:CONV_HALO, :] = jnp.where(i > 0, _glu(ah_ref[...], D), 0.0)
        u_sh[0, CONV_HALO:, :] = a1 * sg
        _fill_shifted(u_sh, tc)

        _depthwise_taps(dc_sh, w_ref, [CONV_WIDTH - 1 - j for j in range(CONV_WIDTH)], None, du_scr, tc)
        _depthwise_tap_grads(dc_sh, u_sh, [CONV_FIRST_TAP + j for j in range(CONV_WIDTH)], dwdw_ref, tc)

        du = du_scr[...]
        da1 = du * sg
        da2 = du * a1 * sg * (1.0 - sg)
        da_ref[:, :D] = da1.astype(BF16)
        da_ref[:, D:] = da2.astype(BF16)
        dbpw1_ref[:, :D] += jnp.sum(da1, axis=0, keepdims=True)
        dbpw1_ref[:, D:] += jnp.sum(da2, axis=0, keepdims=True)

    nxt = lambda i: (jnp.minimum((i + 1) * per, last_halo), 0)
    return pl.pallas_call(
        body, name="conv_bwd", grid=(n_tiles,),
        in_specs=[_rows(tc, D), pl.BlockSpec((CONV_HALO, D), nxt),
                  _rows(tc, D), pl.BlockSpec((CONV_HALO, D), nxt),
                  _rows(tc, 2 * D),
                  pl.BlockSpec((CONV_HALO, 2 * D), lambda i: (jnp.maximum(i * per - 1, 0), 0)),
                  _whole((CONV_WIDTH, D)), _whole((1, D)), _whole((1, D))],
        out_specs=[_rows(tc, 2 * D), _whole((1, D)), _whole((1, D)), _whole((1, D)),
                   _whole((CONV_HALO, D)), _whole((1, 2 * D))],
        out_shape=[SDS((T, 2 * D), BF16), SDS((1, D), F32), SDS((1, D), F32), SDS((1, D), F32),
                   SDS((CONV_HALO, D), F32), SDS((1, 2 * D), F32)],
        scratch_shapes=[pltpu.VMEM((V7X_SUBLANES, tc + CONV_HALO, D), F32),
                        pltpu.VMEM((V7X_SUBLANES, tc + CONV_HALO, D), F32), pltpu.VMEM((tc, D), F32)],
        compiler_params=_params("arbitrary"),
    )(dact, dact, c, c, a, a, w_dw, ln_g, ln_b)


def _attn_bwd(qkv, dao, cos, sin, probs_saved, psink_saved):
    T = qkv.shape[0]
    nb = T // ATTN_BLOCK
    qw = N_Q_HEADS * HEAD_DIM
    kw = N_KV_HEADS * HEAD_DIM

    def body(q_ref, kc_ref, kp_ref, vc_ref, vp_ref, do_ref, cos_ref, sin_ref, cosp_ref, sinp_ref, probs_ref, psink_ref,
             dq_ref, dkv_ref, dsink_ref, dbq_ref, dbkv_ref, carry, prev_scr, cur_scr, dq_scr):
        n = pl.program_id(0)

        @pl.when(n == 0)
        def _():
            for ref in (dsink_ref, dbq_ref, dbkv_ref, carry):
                ref[...] = jnp.zeros_like(ref)

        @pl.when(n == nb)
        def _():
            prev_scr[...] = jnp.zeros_like(prev_scr)

        @pl.when(n < nb)
        def _():
            prev_part = _from_previous_block(N_Q_HEADS * ATTN_BLOCK)
            half = PAIRS_PER_KV * ATTN_BLOCK
            upper = _upper_lanes((ATTN_BLOCK, 2 * HEAD_DIM))
            groups = range(N_KV_HEADS)

            def nt(a, b):
                return lax.dot_general(a, b, (((1,), (1,)), ((), ())), preferred_element_type=F32)

            def kv_grad(even_rows, odd_rows, x):
                even = lax.dot_general(even_rows, x, (((0,), (0,)), ((), ())), preferred_element_type=F32)
                odd = lax.dot_general(odd_rows, x, (((0,), (0,)), ((), ())), preferred_element_type=F32)
                t = jnp.where(upper, odd, even)
                return t + _swap_lane_halves(t)

            q = [_pair_rows(q_ref, g) for g in groups]
            do = [_pair_rows(do_ref, g) for g in groups]
            k_prev = [_kv_operands(g, kp_ref[...]) for g in groups]
            k_cur = [_kv_operands(g, kc_ref[...]) for g in groups]
            v_prev = [_kv_operands(g, vp_ref[...]) for g in groups]
            v_cur = [_kv_operands(g, vc_ref[...]) for g in groups]
            probs = probs_ref[...].astype(F32)
            dp_prev = jnp.concatenate([nt(do[g], v_prev[g][i]) for g in groups for i in range(2)], axis=0)
            dp_cur = jnp.concatenate([nt(do[g], v_cur[g][i]) for g in groups for i in range(2)], axis=0)
            dp = jnp.where(prev_part, dp_prev, dp_cur)
            delta = jnp.sum(probs * dp, axis=1, keepdims=True)
            ds_prev, ds_cur = _split_folded(probs * (dp - delta) * (HEAD_DIM ** -0.5), prev_part)
            p_prev, p_cur = _split_folded(probs_ref[...], prev_part)
            for i, h in enumerate(_all_heads()):
                rows = slice(i * ATTN_BLOCK, (i + 1) * ATTN_BLOCK)
                dsink_ref[:, h:h + 1] += jnp.sum(-(psink_ref[:, h:h + 1] * delta[rows]), axis=0, keepdims=True)
            kv_grads = []
            for g in groups:
                even, odd = slice(2 * g * half, (2 * g + 1) * half), slice((2 * g + 1) * half, (2 * g + 2) * half)
                dq = (jnp.dot(ds_prev[even], k_prev[g][0], preferred_element_type=F32)
                      + jnp.dot(ds_cur[even], k_cur[g][0], preferred_element_type=F32)
                      + jnp.dot(ds_prev[odd], k_prev[g][1], preferred_element_type=F32)
                      + jnp.dot(ds_cur[odd], k_cur[g][1], preferred_element_type=F32))
                for i in range(PAIRS_PER_KV):
                    hp = g * PAIRS_PER_KV + i
                    dq_scr[:, hp * 2 * HEAD_DIM:(hp + 1) * 2 * HEAD_DIM] = dq[i * ATTN_BLOCK:(i + 1) * ATTN_BLOCK]
                kv_grads.append((kv_grad(ds_prev[even], ds_prev[odd], q[g]), kv_grad(ds_cur[even], ds_cur[odd], q[g]),
                                 kv_grad(p_prev[even], p_prev[odd], do[g]), kv_grad(p_cur[even], p_cur[odd], do[g])))
            (dkp0, dkc0, dvp0, dvc0), (dkp1, dkc1, dvp1, dvc1) = kv_grads
            prev_scr[:, :kw] = jnp.where(upper, dkp1, dkp0)
            prev_scr[:, kw:] = jnp.where(upper, dvp1, dvp0)
            cur_scr[:, :kw] = jnp.where(upper, dkc1, dkc0)
            cur_scr[:, kw:] = jnp.where(upper, dvc1, dvc0)
            dq_pre = _rope_transposed(dq_scr[...], cos_ref, sin_ref)
            dq_ref[...] = dq_pre.astype(BF16)
            dbq_ref[...] += jnp.sum(dq_pre, axis=0, keepdims=True)

        tot = carry[...] + prev_scr[...]
        dk_pre = _rope_transposed(tot[:, :kw], cosp_ref, sinp_ref)
        dkv_ref[:, :kw] = dk_pre.astype(BF16)
        dkv_ref[:, kw:] = tot[:, kw:].astype(BF16)
        dbkv_ref[:, :kw] += jnp.sum(dk_pre, axis=0, keepdims=True)
        dbkv_ref[:, kw:] += jnp.sum(tot[:, kw:], axis=0, keepdims=True)

        @pl.when(n < nb)
        def _():
            carry[...] = cur_scr[...]

    cur = lambda n: (jnp.minimum(n, nb - 1), 0)
    out_lag = lambda n: (jnp.maximum(n - 1, 0), 0)
    return pl.pallas_call(
        body, name="attn_bwd", grid=(nb + 1,),
        in_specs=[*_attn_specs(T),
                  pl.BlockSpec((ATTN_BLOCK, qw), cur),
                  pl.BlockSpec((ATTN_BLOCK, V7X_LANES), cur), pl.BlockSpec((ATTN_BLOCK, V7X_LANES), cur),
                  pl.BlockSpec((ATTN_BLOCK, V7X_LANES), out_lag), pl.BlockSpec((ATTN_BLOCK, V7X_LANES), out_lag),
                  pl.BlockSpec((None, N_Q_HEADS * ATTN_BLOCK, ATTN_BLOCK), lambda n: (jnp.minimum(n, nb - 1), 0, 0)),
                  pl.BlockSpec((ATTN_BLOCK, V7X_LANES), cur)],
        out_specs=[pl.BlockSpec((ATTN_BLOCK, qw), cur), pl.BlockSpec((ATTN_BLOCK, 2 * kw), out_lag),
                   _whole((1, N_Q_HEADS)), _whole((1, qw)), _whole((1, 2 * kw))],
        out_shape=[SDS((T, qw), BF16), SDS((T, 2 * kw), BF16),
                   SDS((1, N_Q_HEADS), F32), SDS((1, qw), F32), SDS((1, 2 * kw), F32)],
        scratch_shapes=[pltpu.VMEM((ATTN_BLOCK, 2 * kw), F32), pltpu.VMEM((ATTN_BLOCK, 2 * kw), F32),
                        pltpu.VMEM((ATTN_BLOCK, 2 * kw), F32), pltpu.VMEM((ATTN_BLOCK, qw), F32)],
        compiler_params=_params("arbitrary"),
    )(qkv, qkv, qkv, qkv, qkv, dao, cos, sin, cos, sin, probs_saved, psink_saved)


def _local_step(x, target, p, reduce_begin, reduce_send):
    T, D = x.shape
    cos, sin = _rope_tables(T)
    qw = N_Q_HEADS * HEAD_DIM
    nm, nf = p["norm_mix"], p["norm_ffn"]

    y0, qkv = _qkv_proj(x, nm[0:1], p["attn_w_qkv"], p["attn_b_qkv"], cos, sin, p["gather_started"])
    ao, attn_probs, sink_probs = _attn_fwd(qkv, p["attn_sinks"])
    h1, f0 = _mm_res("attn_out", ao, p["attn_w_o"], p["attn_b_o"], x, nf[0:1])
    p = {**p, **p["other_weights"](h1)}
    w1, w3, w2 = p["ffn_w1"], p["ffn_w3"], p["ffn_w2"]
    act0, gg0, s0 = _ffn_up("ffn0_up", f0, w1, w3, 0)
    h2 = _ffn_down("ffn0_down", s0, w2, 0, h1)
    y1, a = _pw1_proj(h2, nm[1:2], p["conv_w_pw1"], p["conv_b_pw1"])
    c, act = _conv_fwd(a, p["conv_w_dw"], p["conv_b_dw"], p["conv_ln_g"], p["conv_ln_b"])
    h3, f1 = _mm_res("conv_out", act, p["conv_w_pw2"], p["conv_b_pw2"], h2, nf[1:2])
    act1, gg1, s1 = _ffn_up("ffn1_up", f1, w1, w3, 1)
    dh4, loss, d_norm_final = _ffn_down_loss("ffn1_down_loss", s1, w2, 1, h3, p["norm_final"], target)

    g = {}
    dg1, dg3, dw2_1 = _ffn_bwd_down("ffn1_bwd_down", dh4, w2, 1, act1, gg1, s1)
    dw1_1 = _mm_tn("ffn1_dw1", dg1, f1)
    dw3_1 = _mm_tn("ffn1_dw3", dg3, f1)
    begun = reduce_begin("ffn1", {("ffn_w1", 1): dw1_1, ("ffn_w3", 1): dw3_1, ("ffn_w2", 1): dw2_1})
    dh3, dnf1, db_pw2 = _mm_nt_normbwd("ffn1_bwd_in", [(dg1, w1, 1), (dg3, w3, 1)], h3, nf[1:2], dh4, begun)
    sent = reduce_send("ffn1", dh3)

    dw_pw2 = _mm_tn("conv_dw_pw2", act, dh3, after=sent)
    dact = _mm_nt("conv_bwd_out", dh3, p["conv_w_pw2"], F32)
    da, d_ln_g, d_ln_b, d_b_dw, d_w_dw, d_b_pw1 = _conv_bwd(dact, c, a, p["conv_w_dw"], p["conv_ln_g"], p["conv_ln_b"])
    dw_pw1 = _mm_tn("conv_dw_pw1", y1, da, col_chunks=N_CHIPS)
    begun = reduce_begin("conv", {("conv_w_pw2", 0): dw_pw2.reshape(N_CHIPS, -1, D), ("conv_w_pw1", 0): dw_pw1})
    dh2, dnm1, _ = _mm_nt_normbwd("conv_bwd_in", [(da, p["conv_w_pw1"], None)], h2, nm[1:2], dh3, begun)
    sent = reduce_send("conv", dh2)

    dg1, dg3, dw2_0 = _ffn_bwd_down("ffn0_bwd_down", dh2, w2, 0, act0, gg0, s0, after=sent)
    dw1_0 = _mm_tn("ffn0_dw1", dg1, f0)
    dw3_0 = _mm_tn("ffn0_dw3", dg3, f0)
    begun = reduce_begin("ffn0", {("ffn_w1", 0): dw1_0, ("ffn_w3", 0): dw3_0, ("ffn_w2", 0): dw2_0})
    dh1, dnf0, db_o = _mm_nt_normbwd("ffn0_bwd_in", [(dg1, w1, 0), (dg3, w3, 0)], h1, nf[0:1], dh2, begun)
    sent = reduce_send("ffn0", dh1)

    dw_o = _mm_tn("attn_dw_o", ao, dh1, after=sent)
    dao = _mm_nt("attn_bwd_out", dh1, p["attn_w_o"], BF16)
    dq, dkv, d_sinks, dbq, dbkv = _attn_bwd(qkv, dao, cos, sin, attn_probs, sink_probs)
    dwq = _mm_tn("attn_dw_q", dq, y0)
    dwkv = _mm_tn("attn_dw_kv", dkv, y0)
    wqkv = p["attn_w_qkv"]
    dwqkv = jnp.concatenate([dwq, dwkv], axis=0).reshape(N_CHIPS, -1, D)
    begun = reduce_begin("attn", {("attn_w_o", 0): dw_o.reshape(N_CHIPS, -1, D), ("attn_w_qkv", 0): dwqkv})
    dx, dnm0, _ = _mm_nt_normbwd("attn_bwd_in", [(dq, wqkv[:qw], None), (dkv, wqkv[qw:], None)], x, nm[0:1], dh1,
                                 begun)
    reduce_send("attn", dx)

    g["norm_mix"] = jnp.concatenate([dnm0, dnm1], axis=0)
    g["norm_ffn"] = jnp.concatenate([dnf0, dnf1], axis=0)
    g["attn_b_qkv"] = jnp.concatenate([dbq, dbkv], axis=1)
    g["attn_sinks"] = d_sinks
    g["attn_b_o"] = db_o
    g["conv_b_pw1"] = d_b_pw1
    g["conv_w_dw"] = d_w_dw[:CONV_WIDTH]
    g["conv_b_dw"] = d_b_dw
    g["conv_ln_g"] = d_ln_g
    g["conv_ln_b"] = d_ln_b
    g["conv_b_pw2"] = db_pw2
    g["norm_final"] = d_norm_final
    return loss, dx, g


ANY = pl.BlockSpec(memory_space=pl.ANY)
VMEM_WHOLE = pl.BlockSpec(memory_space=pltpu.VMEM)


def _my_place():
    return lax.axis_index("x"), lax.axis_index("y"), lax.axis_index("c")


def _other_chips(x, y):
    places = [(1 - x, y), (x, 1 - y), (1 - x, 1 - y)]
    return [(bx, by, 2 * bx + by) for bx, by in places]


def _gather_small(name, v):
    r, w = v.shape

    def body(v_ref, o_ref, send_sems, recv_sems):
        x, y, c = _my_place()
        pairs = _all_to_all_copies(v_ref, o_ref, send_sems, recv_sems)
        for send, _ in pairs:
            send.start()
        o_ref[4 * x + 2 * y + c] = v_ref[...]
        for send, arrival in pairs:
            arrival.wait_recv()
            send.wait_send()

    return pl.pallas_call(
        body, name=name, out_shape=SDS((N_DEV, r, w), F32), in_specs=[VMEM_WHOLE], out_specs=VMEM_WHOLE,
        scratch_shapes=[pltpu.SemaphoreType.DMA((N_DEV - 1,)), pltpu.SemaphoreType.DMA((N_DEV - 1,))],
        compiler_params=pltpu.CompilerParams(vmem_limit_bytes=V7X_VMEM_LIMIT_BYTES),
    )(v)


def _cast_into_slot(name, gathered, shard, chip_idx):
    rows, cols = shard.shape
    tr = _pack_row_tile(rows)

    def body(k_ref, s_ref, g_ref, o_ref):
        o_ref[...] = s_ref[...].astype(BF16)

    return pl.pallas_call(
        body, name=name,
        grid_spec=pltpu.PrefetchScalarGridSpec(
            num_scalar_prefetch=1, grid=(rows // tr,),
            in_specs=[pl.BlockSpec((tr, cols), lambda i, k_ref: (i, 0)), pl.BlockSpec(memory_space=pl.ANY)],
            out_specs=pl.BlockSpec((None, tr, cols), lambda i, k_ref: (k_ref[0], i, 0))),
        out_shape=SDS(gathered.shape, BF16),
        input_output_aliases={2: 0},
        compiler_params=_params("parallel"),
    )(chip_idx, shard, gathered)


def _row_halves(ref, c):
    half = ref.shape[1] // 2
    return pl.ds(pl.multiple_of(c * half, 16), half), pl.ds(pl.multiple_of((1 - c) * half, 16), half)


def _gather_ici_copies(refs, send_sems, recv_sems):
    x, y, c = _my_place()
    k = 2 * x + y
    pairs = []
    for i, ref in enumerate(refs):
        mine, _ = _row_halves(ref, c)
        for j, (bx, by, kb) in enumerate(_other_chips(x, y)):
            sems = dict(send_sem=send_sems.at[3 * i + j], recv_sem=recv_sems.at[3 * i + j], device_id_type=MESH)
            send = pltpu.make_async_remote_copy(src_ref=ref.at[k, mine], dst_ref=ref.at[k, mine],
                                                device_id=(bx, by, c), **sems)
            arrival = pltpu.make_async_remote_copy(src_ref=ref.at[kb, mine], dst_ref=ref.at[kb, mine],
                                                   device_id=(bx, by, c), **sems)
            pairs.append((send, arrival))
    return pairs


def _gather_d2d_copies(refs, send_sems, recv_sems, first_sem):
    x, y, c = _my_place()
    pairs = []
    for i, ref in enumerate(refs):
        mine, theirs = _row_halves(ref, c)
        for j, (_, _, kb) in enumerate(_other_chips(x, y)):
            sem = first_sem + 3 * i + j
            sems = dict(send_sem=send_sems.at[sem], recv_sem=recv_sems.at[sem], device_id=(x, y, 1 - c),
                        device_id_type=MESH)
            send = pltpu.make_async_remote_copy(src_ref=ref.at[kb, mine], dst_ref=ref.at[kb, mine], **sems)
            arrival = pltpu.make_async_remote_copy(src_ref=ref.at[kb, theirs], dst_ref=ref.at[kb, theirs], **sems)
            pairs.append((send, arrival))
    return pairs


def _run_copies(pairs):
    for send, _ in pairs:
        send.start()
    for send, arrival in pairs:
        send.wait_send()
        arrival.wait_recv()


def _gather_now(name, gathered):
    n_w = len(gathered)

    def body(*refs):
        in_refs = refs[:n_w]
        send_sems, recv_sems = refs[2 * n_w:]
        _run_copies(_gather_ici_copies(in_refs, send_sems, recv_sems))
        _run_copies(_gather_d2d_copies(in_refs, send_sems, recv_sems, 3 * n_w))

    return pl.pallas_call(
        body, name=name, out_shape=[SDS(g.shape, g.dtype) for g in gathered],
        in_specs=[ANY] * n_w, out_specs=[ANY] * n_w, input_output_aliases={i: i for i in range(n_w)},
        scratch_shapes=[pltpu.SemaphoreType.DMA((6 * n_w,)), pltpu.SemaphoreType.DMA((6 * n_w,))],
    )(*gathered)


def _gather_start(name, gathered, after):
    n_w = len(gathered)

    def body(*refs):
        in_refs = refs[:n_w]
        send_sems, recv_sems = refs[n_w + 1:n_w + 3]
        for send, _ in _gather_ici_copies(in_refs, send_sems, recv_sems):
            send.start()
        refs[-1][...] = jnp.zeros_like(refs[-1])

    out = pl.pallas_call(
        body, name=name,
        out_shape=(pltpu.SemaphoreType.DMA((3 * n_w,)), pltpu.SemaphoreType.DMA((3 * n_w,)),
                   *[pltpu.HBM(g.shape, g.dtype) for g in gathered], SDS((8, V7X_LANES), F32)),
        in_specs=[*[HBM_SPEC] * n_w, ANY], out_specs=(SEM_SPEC, SEM_SPEC, *[HBM_SPEC] * n_w, VMEM_WHOLE),
        input_output_aliases={i: 2 + i for i in range(n_w)},
        compiler_params=pltpu.CompilerParams(has_side_effects=DATAFLOW),
    )(*[pltpu.with_memory_space_constraint(g, pltpu.HBM) for g in gathered], after)
    return out[0], out[1], list(out[2:2 + n_w]), out[-1]


def _gather_wait(name, send_sems, recv_sems, gathered, after):
    n_w = len(gathered)

    def body(*refs):
        in_refs = refs[:n_w]
        send_sems, recv_sems = refs[n_w:n_w + 2]
        for send, arrival in _gather_ici_copies(in_refs, send_sems, recv_sems):
            send.wait_send()
            arrival.wait_recv()

    out = pl.pallas_call(
        body, name=name, out_shape=tuple(pltpu.HBM(g.shape, g.dtype) for g in gathered),
        in_specs=[*[HBM_SPEC] * n_w, SEM_SPEC, SEM_SPEC, ANY], out_specs=tuple([HBM_SPEC] * n_w),
        input_output_aliases={i: i for i in range(n_w)},
        compiler_params=pltpu.CompilerParams(has_side_effects=DATAFLOW),
    )(*gathered, send_sems, recv_sems, after)
    return list(out)


def _swap_fetched_with_sibling(name, gathered):
    n_w = len(gathered)

    def body(*refs):
        in_refs = refs[:n_w]
        send_sems, recv_sems = refs[2 * n_w:]
        _run_copies(_gather_d2d_copies(in_refs, send_sems, recv_sems, 0))

    return pl.pallas_call(
        body, name=name, out_shape=[SDS(g.shape, g.dtype) for g in gathered],
        in_specs=[ANY] * n_w, out_specs=[ANY] * n_w, input_output_aliases={i: i for i in range(n_w)},
        scratch_shapes=[pltpu.SemaphoreType.DMA((3 * n_w,)), pltpu.SemaphoreType.DMA((3 * n_w,))],
    )(*gathered)


def _sibling_swap_copies(g_refs, land_refs, send_sems, recv_sems):
    x, y, c = _my_place()
    copies = []
    for i, g_ref in enumerate(g_refs):
        half = g_ref.shape[1] // 2
        theirs = pl.ds(pl.multiple_of((1 - c) * half, 8), half)
        copies.append(pltpu.make_async_remote_copy(
            src_ref=g_ref.at[:, theirs], dst_ref=land_refs[i], send_sem=send_sems.at[i], recv_sem=recv_sems.at[i],
            device_id=(x, y, 1 - c), device_id_type=MESH))
    return copies


def _sibling_swap_start(name, grads):
    n_g = len(grads)

    def body(*refs):
        g_refs, land_refs = refs[:n_g], refs[n_g:2 * n_g]
        send_sems, recv_sems = refs[2 * n_g:2 * n_g + 2]
        for cp in _sibling_swap_copies(g_refs, land_refs, send_sems, recv_sems):
            cp.start()
        refs[-1][...] = jnp.zeros_like(refs[-1])

    lands = [pltpu.with_memory_space_constraint(lax.empty((g.shape[0], g.shape[1] // 2, g.shape[2]), g.dtype),
                                                pltpu.HBM) for g in grads]
    out = pl.pallas_call(
        body, name=name,
        out_shape=(pltpu.SemaphoreType.DMA((n_g,)), pltpu.SemaphoreType.DMA((n_g,)),
                   *[pltpu.HBM(g.shape, g.dtype) for g in grads], *[pltpu.HBM(l.shape, l.dtype) for l in lands],
                   SDS((8, V7X_LANES), F32)),
        in_specs=[HBM_SPEC] * (2 * n_g), out_specs=(SEM_SPEC, SEM_SPEC, *[HBM_SPEC] * (2 * n_g), VMEM_WHOLE),
        input_output_aliases={i: 2 + i for i in range(2 * n_g)},
        compiler_params=pltpu.CompilerParams(has_side_effects=DATAFLOW),
    )(*[pltpu.with_memory_space_constraint(g, pltpu.HBM) for g in grads], *lands)
    return out[0], out[1], list(out[2:2 + n_g]), list(out[2 + n_g:2 + 2 * n_g]), out[-1]


def _sibling_swap_wait(name, send_sems, recv_sems, grads, lands, after):
    n_g = len(grads)

    def body(*refs):
        g_refs, land_refs = refs[:n_g], refs[n_g:2 * n_g]
        send_sems, recv_sems = refs[2 * n_g:2 * n_g + 2]
        for cp in _sibling_swap_copies(g_refs, land_refs, send_sems, recv_sems):
            cp.wait_send()
            cp.wait_recv()

    out = pl.pallas_call(
        body, name=name,
        out_shape=(*[pltpu.HBM(g.shape, g.dtype) for g in grads], *[pltpu.HBM(l.shape, l.dtype) for l in lands]),
        in_specs=[*[HBM_SPEC] * (2 * n_g), SEM_SPEC, SEM_SPEC, ANY], out_specs=tuple([HBM_SPEC] * (2 * n_g)),
        input_output_aliases={i: i for i in range(2 * n_g)},
        compiler_params=pltpu.CompilerParams(has_side_effects=DATAFLOW),
    )(*grads, *lands, send_sems, recv_sems, after)
    return list(out[:n_g]), list(out[n_g:])


def _pack_row_tile(rows):
    for t in range(min(rows, 512), 7, -1):
        if rows % t == 0 and t % 8 == 0:
            return t
    return rows


def _add_sibling_half(name, grads, from_sibling, c_idx):
    n, R, w = grads.shape
    half = R // 2
    tr = _pack_row_tile(half)
    steps = half // tr

    def body(c_ref, g_ref, s_ref, o_ref):
        o_ref[...] = (g_ref[...] + s_ref[...]).astype(BF16)

    return pl.pallas_call(
        body, name=name,
        grid_spec=pltpu.PrefetchScalarGridSpec(
            num_scalar_prefetch=1, grid=(n, steps),
            in_specs=[pl.BlockSpec((1, tr, w), lambda j, i, c_ref: (j, c_ref[0] * steps + i, 0)),
                      pl.BlockSpec((1, tr, w), lambda j, i, c_ref: (j, i, 0))],
            out_specs=pl.BlockSpec((1, tr, w), lambda j, i, c_ref: (j, i, 0))),
        out_shape=SDS((n, half, w), BF16),
        compiler_params=_params("parallel", "parallel"),
    )(c_idx, grads, from_sibling)


HBM_SPEC = pl.BlockSpec(memory_space=pltpu.HBM)
SEM_SPEC = pl.BlockSpec(memory_space=pltpu.SEMAPHORE)
DATAFLOW = pltpu.SideEffectType.DATAFLOW_SIDE_EFFECTING


def _chip_scatter_copies(p_refs, land_refs, send_sems, recv_sems):
    x, y, c = _my_place()
    return [pltpu.make_async_remote_copy(
        src_ref=p_refs[i].at[kb], dst_ref=land_refs[i].at[j], send_sem=send_sems.at[3 * i + j],
        recv_sem=recv_sems.at[3 * i + j], device_id=(bx, by, c), device_id_type=MESH)
        for i in range(len(p_refs)) for j, (bx, by, kb) in enumerate(_other_chips(x, y))]


def _scatter_start(name, partials):
    n_p = len(partials)

    def body(*refs):
        p_refs, land_refs = refs[:n_p], refs[n_p:2 * n_p]
        send_sems, recv_sems = refs[2 * n_p:2 * n_p + 2]
        for cp in _chip_scatter_copies(p_refs, land_refs, send_sems, recv_sems):
            cp.start()
        refs[-1][...] = jnp.zeros_like(refs[-1])

    lands = [pltpu.with_memory_space_constraint(lax.empty((N_CHIPS - 1,) + p.shape[1:], p.dtype), pltpu.HBM)
             for p in partials]
    out = pl.pallas_call(
        body, name=name,
        out_shape=(pltpu.SemaphoreType.DMA((3 * n_p,)), pltpu.SemaphoreType.DMA((3 * n_p,)),
                   *[pltpu.HBM(p.shape, p.dtype) for p in partials], *[pltpu.HBM(l.shape, l.dtype) for l in lands],
                   SDS((8, V7X_LANES), F32)),
        in_specs=[HBM_SPEC] * (2 * n_p), out_specs=(SEM_SPEC, SEM_SPEC, *[HBM_SPEC] * (2 * n_p), VMEM_WHOLE),
        input_output_aliases={i: 2 + i for i in range(2 * n_p)},
        compiler_params=pltpu.CompilerParams(has_side_effects=DATAFLOW),
    )(*[pltpu.with_memory_space_constraint(p, pltpu.HBM) for p in partials], *lands)
    return out[0], out[1], list(out[2:2 + n_p]), list(out[2 + n_p:2 + 2 * n_p]), out[-1]


def _scatter_wait(name, send_sems, recv_sems, partials, lands, after):
    n_p = len(partials)

    def body(*refs):
        p_refs, land_refs = refs[:n_p], refs[n_p:2 * n_p]
        send_sems, recv_sems = refs[2 * n_p:2 * n_p + 2]
        for cp in _chip_scatter_copies(p_refs, land_refs, send_sems, recv_sems):
            cp.wait_send()
            cp.wait_recv()

    out = pl.pallas_call(
        body, name=name,
        out_shape=(*[pltpu.HBM(p.shape, p.dtype) for p in partials], *[pltpu.HBM(l.shape, l.dtype) for l in lands]),
        in_specs=[*[HBM_SPEC] * (2 * n_p), SEM_SPEC, SEM_SPEC, ANY], out_specs=tuple([HBM_SPEC] * (2 * n_p)),
        input_output_aliases={i: i for i in range(2 * n_p)},
        compiler_params=pltpu.CompilerParams(has_side_effects=DATAFLOW),
    )(*partials, *lands, send_sems, recv_sems, after)
    return list(out[:n_p]), list(out[n_p:])


def _sum_chip_partials(name, grads, from_sibling, received, shard, layer, place):
    n, half, w = from_sibling.shape
    tr = _pack_row_tile(half)
    steps = half // tr

    def body(place_ref, g_ref, s_ref, r_ref, shard_ref, o_ref):
        own = g_ref[0] + s_ref[0]
        o_ref[...] = ((own + r_ref[0].astype(F32)) + r_ref[1].astype(F32)) + r_ref[2].astype(F32)

    return pl.pallas_call(
        body, name=name,
        grid_spec=pltpu.PrefetchScalarGridSpec(
            num_scalar_prefetch=1, grid=(steps,),
            in_specs=[pl.BlockSpec((1, tr, w), lambda i, place_ref: (place_ref[0], place_ref[1] * steps + i, 0)),
                      pl.BlockSpec((1, tr, w), lambda i, place_ref: (place_ref[0], i, 0)),
                      pl.BlockSpec((n - 1, tr, w), lambda i, place_ref: (0, i, 0)),
                      pl.BlockSpec(memory_space=pl.ANY)],
            out_specs=pl.BlockSpec((tr, w), lambda i, place_ref: ((2 * layer + place_ref[1]) * steps + i, 0))),
        out_shape=SDS(shard.shape, F32),
        input_output_aliases={4: 0},
        compiler_params=_params("parallel"),
    )(place, grads, from_sibling, received, shard)


def _join_halves(shards, layers):
    n_s = len(shards)
    n_sem = sum(layers)

    def body(*refs):
        in_refs = refs[:n_s]
        send_sems, recv_sems = refs[2 * n_s:]
        x, y, c = _my_place()
        copies, sem = [], 0
        for ref, n_layers in zip(in_refs, layers):
            half = ref.shape[0] // (2 * n_layers)
            for layer in range(n_layers):
                mine = pl.ds(pl.multiple_of(layer * 2 * half + c * half, 8), half)
                theirs = pl.ds(pl.multiple_of(layer * 2 * half + (1 - c) * half, 8), half)
                send = pltpu.make_async_remote_copy(
                    src_ref=ref.at[mine], dst_ref=ref.at[mine], send_sem=send_sems.at[sem], recv_sem=recv_sems.at[sem],
                    device_id=(x, y, 1 - c), device_id_type=MESH)
                send.start()
                arrival = pltpu.make_async_remote_copy(
                    src_ref=ref.at[theirs], dst_ref=ref.at[theirs], send_sem=send_sems.at[sem],
                    recv_sem=recv_sems.at[sem], device_id=(x, y, 1 - c), device_id_type=MESH)
                copies.append((send, arrival))
                sem += 1
        for send, arrival in copies:
            send.wait_send()
            arrival.wait_recv()

    return pl.pallas_call(
        body, name="join_halves", out_shape=[SDS(s.shape, s.dtype) for s in shards],
        in_specs=[ANY] * n_s, out_specs=[ANY] * n_s,
        input_output_aliases={i: i for i in range(n_s)},
        scratch_shapes=[pltpu.SemaphoreType.DMA((n_sem,)), pltpu.SemaphoreType.DMA((n_sem,))],
    )(*shards)


def _all_to_all_copies(v_ref, land_ref, send_sems, recv_sems):
    x, y, c = _my_place()
    me = 4 * x + 2 * y + c
    pairs = []
    for k in range(1, N_DEV):
        px, py, pc = (1 - x if k & 4 else x), (1 - y if k & 2 else y), (1 - c if k & 1 else c)
        sems = dict(send_sem=send_sems.at[k - 1], recv_sem=recv_sems.at[k - 1], device_id=(px, py, pc),
                    device_id_type=MESH)
        send = pltpu.make_async_remote_copy(src_ref=v_ref, dst_ref=land_ref.at[me], **sems)
        arrival = pltpu.make_async_remote_copy(src_ref=v_ref, dst_ref=land_ref.at[4 * px + 2 * py + pc], **sems)
        pairs.append((send, arrival))
    return pairs


def _small_reduce_start(name, v):
    def body(v_ref, land_ref, send_sems, recv_sems, v_out, land_out):
        for send, _ in _all_to_all_copies(v_ref, land_ref, send_sems, recv_sems):
            send.start()

    land = pltpu.with_memory_space_constraint(jnp.zeros((N_DEV,) + v.shape, v.dtype), pltpu.HBM)
    return pl.pallas_call(
        body, name=name,
        out_shape=(pltpu.SemaphoreType.DMA((N_DEV - 1,)), pltpu.SemaphoreType.DMA((N_DEV - 1,)),
                   pltpu.HBM(v.shape, v.dtype), pltpu.HBM(land.shape, land.dtype)),
        in_specs=[HBM_SPEC, HBM_SPEC], out_specs=(SEM_SPEC, SEM_SPEC, HBM_SPEC, HBM_SPEC),
        input_output_aliases={0: 2, 1: 3},
        compiler_params=pltpu.CompilerParams(has_side_effects=DATAFLOW),
    )(pltpu.with_memory_space_constraint(v, pltpu.HBM), land)


def _small_reduce_wait(name, send_sems, recv_sems, v, land, after):
    def body(v_ref, land_ref, send_sems, recv_sems, after_ref, v_out, land_out):
        for send, arrival in _all_to_all_copies(v_ref, land_ref, send_sems, recv_sems):
            send.wait_send()
            arrival.wait_recv()

    return pl.pallas_call(
        body, name=name, out_shape=(pltpu.HBM(v.shape, v.dtype), pltpu.HBM(land.shape, land.dtype)),
        in_specs=[HBM_SPEC, HBM_SPEC, SEM_SPEC, SEM_SPEC, ANY], out_specs=(HBM_SPEC, HBM_SPEC),
        input_output_aliases={0: 0, 1: 1},
        compiler_params=pltpu.CompilerParams(has_side_effects=DATAFLOW),
    )(v, land, send_sems, recv_sems, after)


def _sum_device_slots(name, v, land, me):
    r, w = v.shape

    def body(me_ref, v_ref, land_ref, o_ref):
        mine = v_ref[...]
        acc = jnp.where(me_ref[0] == 0, mine, land_ref[0])
        for d in range(1, N_DEV):
            acc = acc + jnp.where(me_ref[0] == d, mine, land_ref[d])
        o_ref[...] = acc

    return pl.pallas_call(
        body, name=name,
        grid_spec=pltpu.PrefetchScalarGridSpec(
            num_scalar_prefetch=1, grid=(1,),
            in_specs=[pl.BlockSpec((r, w), lambda i, me_ref: (0, 0)),
                      pl.BlockSpec((N_DEV, r, w), lambda i, me_ref: (0, 0, 0))],
            out_specs=pl.BlockSpec((r, w), lambda i, me_ref: (0, 0))),
        out_shape=SDS((r, w), F32),
        compiler_params=_params("arbitrary"),
    )(me, v, land)


def _adamw(name, w, g, m, v):
    rows, width = w.shape
    tr = _pack_row_tile(rows)

    def body(w_ref, g_ref, m_ref, v_ref, g_out_ref, d_ref, nm_ref, nv_ref):
        gg = g_ref[...]
        g_out_ref[...] = gg
        m_new = ADAM_B1 * m_ref[...] + (1.0 - ADAM_B1) * gg
        v_new = ADAM_B2 * v_ref[...] + (1.0 - ADAM_B2) * (gg * gg)
        m_hat = m_new / (1.0 - ADAM_B1 ** ADAM_STEP)
        v_hat = v_new / (1.0 - ADAM_B2 ** ADAM_STEP)
        d_ref[...] = -ADAM_LR * (m_hat / (jnp.sqrt(v_hat) + ADAM_EPS) + ADAM_WD * w_ref[...])
        nm_ref[...] = m_new
        nv_ref[...] = v_new

    spec = _rows(tr, width)
    return pl.pallas_call(
        body, name=name, grid=(rows // tr,),
        in_specs=[spec] * 4, out_specs=[spec] * 4,
        out_shape=[SDS((rows, width), F32)] * 4,
        compiler_params=_params("parallel"),
    )(w, g, m, v)


WEIGHT_NAMES = ['norm_mix', 'norm_ffn', 'attn_w_qkv', 'attn_b_qkv', 'attn_sinks', 'attn_w_o', 'attn_b_o',
                'conv_w_pw1', 'conv_b_pw1', 'conv_w_dw', 'conv_b_dw', 'conv_ln_g', 'conv_ln_b', 'conv_w_pw2',
                'conv_b_pw2', 'ffn_w1', 'ffn_w3', 'ffn_w2', 'norm_final']
BIG = ['attn_w_qkv', 'attn_w_o', 'conv_w_pw1', 'conv_w_pw2', 'ffn_w1', 'ffn_w3', 'ffn_w2']
COLUMN_SPLIT = ('attn_w_qkv', 'conv_w_pw1', 'ffn_w1', 'ffn_w3')
STORED_TRANSPOSED = ('attn_w_qkv', 'ffn_w1', 'ffn_w3')
SMALL_SPLIT = ['conv_b_pw1', 'conv_w_dw', 'conv_b_dw', 'conv_ln_g', 'conv_ln_b', 'conv_b_pw2']
SMALL_WHOLE = ['norm_mix', 'norm_ffn', 'attn_b_qkv', 'attn_sinks', 'attn_b_o', 'norm_final']


def _keeps_rows(shape):
    return len(shape) == 2 and shape[0] > 1 and shape[1] == PACK_W


def _pack_rows(arrays, dtype, row_multiple):
    blocks = [jnp.pad(a.astype(dtype), ((0, -a.shape[0] % V7X_SUBLANES), (0, 0)))
              for a in arrays if _keeps_rows(a.shape)]
    flat = jnp.concatenate([a.astype(dtype).reshape(-1) for a in arrays if not _keeps_rows(a.shape)])
    multiple = max(row_multiple, V7X_SUBLANES)
    rows = -(-(-(-flat.shape[0] // PACK_W)) // multiple) * multiple
    blocks.append(jnp.pad(flat, (0, rows * PACK_W - flat.shape[0])).reshape(rows, PACK_W))
    return jnp.concatenate(blocks, axis=0) if len(blocks) > 1 else blocks[0]


def _unpack_rows(pack, shapes):
    out, row = {}, 0
    for i, shape in enumerate(shapes):
        if _keeps_rows(shape):
            out[i] = pack[row:row + shape[0]]
            row += -(-shape[0] // V7X_SUBLANES) * V7X_SUBLANES
    flat, at = pack[row:].reshape(-1), 0
    for i, shape in enumerate(shapes):
        if not _keeps_rows(shape):
            size = 1
            for s in shape:
                size *= s
            out[i] = flat[at:at + size].reshape(shape)
            at += size
    return [out[i] for i in range(len(shapes))]


def _join_chip_axis(name, parts):
    axis = parts.ndim - 1 if name in COLUMN_SPLIT or name in SMALL_SPLIT else parts.ndim - 2
    moved = jnp.moveaxis(parts, 0, axis - 1)
    shape = list(moved.shape)
    shape[axis - 1:axis + 1] = [shape[axis - 1] * shape[axis]]
    return moved.reshape(shape)


def _split_chip_axis(name, whole, shard_shape):
    axis = len(shard_shape) - 1 if name in COLUMN_SPLIT or name in SMALL_SPLIT else len(shard_shape) - 2
    shape = list(whole.shape)
    shape[axis:axis + 1] = [N_CHIPS, shard_shape[axis]]
    return jnp.moveaxis(whole.reshape(shape), axis, 0)


def kernel(x, norm_mix, norm_ffn, attn_w_qkv, attn_b_qkv, attn_sinks, attn_w_o, attn_b_o, conv_w_pw1, conv_b_pw1, conv_w_dw, conv_b_dw, conv_ln_g, conv_ln_b, conv_w_pw2, conv_b_pw2, ffn_w1, ffn_w3, ffn_w2, norm_final, loss_target, m_norm_mix, m_norm_ffn, m_attn_w_qkv, m_attn_b_qkv, m_attn_sinks, m_attn_w_o, m_attn_b_o, m_conv_w_pw1, m_conv_b_pw1, m_conv_w_dw, m_conv_b_dw, m_conv_ln_g, m_conv_ln_b, m_conv_w_pw2, m_conv_b_pw2, m_ffn_w1, m_ffn_w3, m_ffn_w2, m_norm_final, v_norm_mix, v_norm_ffn, v_attn_w_qkv, v_attn_b_qkv, v_attn_sinks, v_attn_w_o, v_attn_b_o, v_conv_w_pw1, v_conv_b_pw1, v_conv_w_dw, v_conv_b_dw, v_conv_ln_g, v_conv_ln_b, v_conv_w_pw2, v_conv_b_pw2, v_ffn_w1, v_ffn_w3, v_ffn_w2, v_norm_final):
    w = dict(zip(WEIGHT_NAMES, (norm_mix, norm_ffn, attn_w_qkv, attn_b_qkv, attn_sinks, attn_w_o, attn_b_o,
                                conv_w_pw1, conv_b_pw1, conv_w_dw, conv_b_dw, conv_ln_g, conv_ln_b, conv_w_pw2,
                                conv_b_pw2, ffn_w1, ffn_w3, ffn_w2, norm_final)))
    m = dict(zip(WEIGHT_NAMES, (m_norm_mix, m_norm_ffn, m_attn_w_qkv, m_attn_b_qkv, m_attn_sinks, m_attn_w_o,
                                m_attn_b_o, m_conv_w_pw1, m_conv_b_pw1, m_conv_w_dw, m_conv_b_dw, m_conv_ln_g,
                                m_conv_ln_b, m_conv_w_pw2, m_conv_b_pw2, m_ffn_w1, m_ffn_w3, m_ffn_w2, m_norm_final)))
    v = dict(zip(WEIGHT_NAMES, (v_norm_mix, v_norm_ffn, v_attn_w_qkv, v_attn_b_qkv, v_attn_sinks, v_attn_w_o,
                                v_attn_b_o, v_conv_w_pw1, v_conv_b_pw1, v_conv_w_dw, v_conv_b_dw, v_conv_ln_g,
                                v_conv_ln_b, v_conv_w_pw2, v_conv_b_pw2, v_ffn_w1, v_ffn_w3, v_ffn_w2, v_norm_final)))
    T, D = x.shape[1], x.shape[2]
    c_idx = lax.axis_index("c").astype(jnp.int32).reshape(1)
    chip = (2 * lax.axis_index("x") + lax.axis_index("y")).astype(jnp.int32)

    def as_rows(n, a):
        a = jnp.swapaxes(a, -1, -2) if n in STORED_TRANSPOSED else a
        return a.reshape(-1, a.shape[-1])

    def from_rows(n, rows):
        shape = w[n].shape[:-2] + w[n].shape[:-3:-1] if n in STORED_TRANSPOSED else w[n].shape
        a = rows.reshape(shape)
        return jnp.swapaxes(a, -1, -2) if n in STORED_TRANSPOSED else a

    slabs = {n: _cast_into_slot(f"cast_{n}", lax.empty((N_CHIPS,) + as_rows(n, w[n]).shape, BF16), as_rows(n, w[n]),
                                chip.reshape(1)) for n in BIG}
    first, later = BIG[:2], BIG[2:]
    qkv_parts, w_o_parts = _gather_now("gather_attn", [slabs[n] for n in first])
    send_sems, recv_sems, travelling, gather_started = _gather_start("gather_start", [slabs[n] for n in later],
                                                                     qkv_parts)
    layers = ffn_w1.shape[0]
    small_shapes = [w[n].shape for n in SMALL_SPLIT]
    small_all = _gather_small("gather_small", _pack_rows([w[n] for n in SMALL_SPLIT], F32, 8))
    per_chip = [_unpack_rows(small_all[2 * j], small_shapes) for j in range(N_CHIPS)]
    full = {}
    for i, n in enumerate(SMALL_SPLIT):
        full[n] = _join_chip_axis(n, jnp.stack([per_chip[j][i] for j in range(N_CHIPS)]))

    def other_weights(after):
        landed = _gather_wait("gather_wait", send_sems, recv_sems, travelling, after)
        gathered = dict(zip(later, _swap_fetched_with_sibling("gather_swap", landed)))
        return {"conv_w_pw1": gathered["conv_w_pw1"], "conv_w_pw2": gathered["conv_w_pw2"].reshape(-1, D),
                "ffn_w1": gathered["ffn_w1"].reshape(N_CHIPS, layers, -1, D),
                "ffn_w3": gathered["ffn_w3"].reshape(N_CHIPS, layers, -1, D),
                "ffn_w2": gathered["ffn_w2"].reshape(N_CHIPS, layers, -1, D)}

    p = {
        "norm_mix": norm_mix, "norm_ffn": norm_ffn, "norm_final": norm_final.reshape(1, D),
        "attn_w_qkv": qkv_parts.reshape(-1, D), "attn_b_qkv": attn_b_qkv,
        "attn_sinks": attn_sinks, "attn_w_o": w_o_parts.reshape(-1, D), "attn_b_o": attn_b_o,
        "conv_b_pw1": full["conv_b_pw1"], "conv_w_dw": full["conv_w_dw"][0],
        "conv_b_dw": full["conv_b_dw"], "conv_ln_g": full["conv_ln_g"], "conv_ln_b": full["conv_ln_b"],
        "conv_b_pw2": full["conv_b_pw2"], "gather_started": gather_started, "other_weights": other_weights,
    }
    swapping, in_flight = {}, []

    def reduce_begin(tag, grads):
        keys = list(grads)
        *handles, begun = _sibling_swap_start(f"sibling_swap_start_{tag}", [grads[k] for k in keys])
        swapping[tag] = (keys, handles)
        return begun

    def reduce_send(tag, after):
        keys, (swap_send, swap_recv, grads, lands) = swapping[tag]
        grads, from_sibling = _sibling_swap_wait(f"sibling_swap_wait_{tag}", swap_send, swap_recv, grads, lands, after)
        partials = [_add_sibling_half(f"add_sibling_half_{tag}{i}", gr, fs, c_idx)
                    for i, (gr, fs) in enumerate(zip(grads, from_sibling))]
        *handles, sent = _scatter_start(f"scatter_start_{tag}", partials)
        in_flight.append((tag, keys, handles, grads, from_sibling))
        return sent

    loss_part, dx, g = _local_step(x[0], loss_target[0], p, reduce_begin, reduce_send)
    for n in SMALL_WHOLE + SMALL_SPLIT:
        g[n] = g[n].reshape((-1,) + g[n].shape[-2:]) if w[n].ndim == 3 else g[n].reshape(w[n].shape[:-1] + (-1,))

    small_pack = _pack_rows([loss_part] + [g[n] for n in SMALL_WHOLE] + [g[n] for n in SMALL_SPLIT], F32, 8)
    small_send, small_recv, small_pack, small_land = _small_reduce_start("small_reduce_start", small_pack)

    place = jnp.stack([chip, c_idx[0]])
    shard_grad = {n: lax.empty(as_rows(n, w[n]).shape, F32) for n in BIG}
    for tag, keys, (send_sems, recv_sems, partials, lands), grads, from_sibling in in_flight:
        _, received = _scatter_wait(f"scatter_wait_{tag}", send_sems, recv_sems, partials, lands, small_pack)
        for i, (n, layer) in enumerate(keys):
            shard_grad[n] = _sum_chip_partials(f"sum_chip_partials_{tag}{i}", grads[i], from_sibling[i], received[i],
                                               shard_grad[n], layer, place)
    g_big = dict(zip(BIG, _join_halves([shard_grad[n] for n in BIG], [w[n].shape[0] for n in BIG])))
    big_out = {}
    for n in BIG:
        step = _adamw(f"adamw_{n}", as_rows(n, w[n]), g_big[n], as_rows(n, m[n]), as_rows(n, v[n]))
        big_out[n] = [from_rows(n, a) for a in step]

    small_whole_shapes = [w[n].shape for n in SMALL_WHOLE]
    small_full_shapes = [g[n].shape for n in SMALL_SPLIT]
    small_pack, small_land = _small_reduce_wait("small_reduce_wait", small_send, small_recv, small_pack, small_land,
                                                big_out[BIG[-1]][1])
    reduced = _sum_device_slots("small_reduce_sum", small_pack, small_land, (2 * chip + c_idx[0]).reshape(1))
    pieces = _unpack_rows(reduced, [(1,)] + small_whole_shapes + small_full_shapes)
    loss = pieces[0].reshape(())
    g_small = dict(zip(SMALL_WHOLE, pieces[1:1 + len(SMALL_WHOLE)]))
    for n, whole in zip(SMALL_SPLIT, pieces[1 + len(SMALL_WHOLE):]):
        parts = _split_chip_axis(n, whole, w[n].shape)
        g_small[n] = lax.dynamic_index_in_dim(parts, chip, axis=0, keepdims=False)
    small = SMALL_WHOLE + SMALL_SPLIT
    _, d_small, m_small, v_small = _adamw(
        "adamw_small", _pack_rows([w[n] for n in small], F32, 8), _pack_rows([g_small[n] for n in small], F32, 8),
        _pack_rows([m[n] for n in small], F32, 8), _pack_rows([v[n] for n in small], F32, 8))

    outs = {}
    for slot, (tag, small_pack) in enumerate((("g", None), ("d", d_small), ("m", m_small), ("v", v_small))):
        vals = {n: big_out[n][slot] for n in BIG}
        if small_pack is None:
            vals.update(g_small)
        else:
            vals.update(zip(small, _unpack_rows(small_pack, [w[n].shape for n in small])))
        outs[tag] = vals
    return (loss, dx.reshape(1, T, D), *[outs["g"][n] for n in WEIGHT_NAMES], *[outs["d"][n] for n in WEIGHT_NAMES],
            *[outs["m"][n] for n in WEIGHT_NAMES], *[outs["v"][n] for n in WEIGHT_NAMES])
```

```python
import functools

import jax
import jax.numpy as jnp
from jax import lax
from jax.experimental import pallas as pl
from jax.experimental.pallas import tpu as pltpu

F32 = jnp.float32
BF16 = jnp.bfloat16
SDS = jax.ShapeDtypeStruct
MESH = pl.DeviceIdType.MESH

HEAD_DIM = 64
N_Q_HEADS = 16
N_KV_HEADS = 2
Q_PER_KV = N_Q_HEADS // N_KV_HEADS
ATTN_BLOCK = 128
ROPE_THETA = 10000.0
CONV_WIDTH = 31
CONV_HALO = 32
CONV_FIRST_TAP = CONV_HALO - CONV_WIDTH + 1
CONV_ROW_CHUNK = 64
CONV_LANE_CHUNK = 256
CONV_GRAD_UNROLL = 8
RMS_EPS = 1e-5
LN_EPS = 1e-5
ADAM_LR = 0.001
ADAM_B1 = 0.9
ADAM_B2 = 0.999
ADAM_EPS = 1e-08
ADAM_WD = 0.01
ADAM_STEP = 10

V7X_LANES = 128
V7X_SUBLANES = 8
V7X_VMEM_LIMIT_BYTES = 56 * 1024 * 1024

N_CHIPS = 4
N_DEV = 8
PACK_W = 1024

MASK_VALUE = -1e30


def _params(*semantics):
    return pltpu.CompilerParams(dimension_semantics=semantics, vmem_limit_bytes=V7X_VMEM_LIMIT_BYTES)


def _rows(tm, width):
    return pl.BlockSpec((tm, width), lambda i: (i, 0))


def _whole(shape):
    return pl.BlockSpec(shape, lambda *_: (0,) * len(shape))


def _rms_rstd(h):
    return lax.rsqrt(jnp.mean(h * h, axis=-1, keepdims=True) + RMS_EPS)


def _silu_and_grad(z):
    sg = jax.nn.sigmoid(z)
    return z * sg, sg * (1.0 + z * (1.0 - sg))


def _swap_rope_halves(t):
    w = t.shape[1]
    half = HEAD_DIM // 2
    lane = lax.broadcasted_iota(jnp.int32, t.shape, 1)
    upper = pltpu.roll(t, w - half, 1)
    lower = pltpu.roll(t, half, 1)
    return jnp.where(lane % HEAD_DIM < half, upper, lower)


def _rope(t, cos_ref, sin_ref):
    reps = t.shape[1] // V7X_LANES
    c = jnp.tile(cos_ref[...], (1, reps))
    s = jnp.tile(sin_ref[...], (1, reps))
    return t * c + _swap_rope_halves(t) * s


def _rope_transposed(dt, cos_ref, sin_ref):
    reps = dt.shape[1] // V7X_LANES
    c = jnp.tile(cos_ref[...], (1, reps))
    s = jnp.tile(sin_ref[...], (1, reps))
    return dt * c + _swap_rope_halves(dt * s)


def _rope_tables(seq_len):
    pos = jnp.arange(seq_len, dtype=F32)
    inv_freq = ROPE_THETA ** (-jnp.arange(0, HEAD_DIM, 2, dtype=F32) / HEAD_DIM)
    ang = pos[:, None] * jnp.tile(inv_freq, 2 * V7X_LANES // HEAD_DIM)[None, :]
    upper_half = jnp.arange(V7X_LANES) % HEAD_DIM >= HEAD_DIM // 2
    return jnp.cos(ang), jnp.where(upper_half[None, :], jnp.sin(ang), -jnp.sin(ang))


def _qkv_proj(h, g, w, b, cos, sin, after):
    T, D = h.shape
    N = w.shape[0]
    tm = min(512, T)
    rope_w = N - N_KV_HEADS * HEAD_DIM

    def body(h_ref, g_ref, w_ref, b_ref, cos_ref, sin_ref, _, y_ref, o_ref):
        hh = h_ref[...]
        y = (hh * _rms_rstd(hh) * g_ref[...]).astype(BF16)
        y_ref[...] = y
        acc = _dot_nt(y, w_ref[...]) + b_ref[...]
        o_ref[:, :rope_w] = _rope(acc[:, :rope_w], cos_ref, sin_ref).astype(BF16)
        o_ref[:, rope_w:] = acc[:, rope_w:].astype(BF16)

    return pl.pallas_call(
        body, name="qkv_proj", grid=(T // tm,),
        in_specs=[_rows(tm, D), _whole((1, D)), _whole((N, D)), _whole((1, N)),
                  _rows(tm, V7X_LANES), _rows(tm, V7X_LANES), pl.BlockSpec(memory_space=pl.ANY)],
        out_specs=[_rows(tm, D), _rows(tm, N)],
        out_shape=[SDS((T, D), BF16), SDS((T, N), BF16)],
        compiler_params=_params("parallel"),
    )(h, g, w, b, cos, sin, after)


def _pw1_proj(h, g, w, b):
    T, D = h.shape
    n = w.shape[2]
    N = N_CHIPS * n
    tm = min(512, T)

    def body(h_ref, g_ref, w_ref, b_ref, y_ref, o_ref):
        hh = h_ref[...]
        y = (hh * _rms_rstd(hh) * g_ref[...]).astype(BF16)
        y_ref[...] = y
        for j in range(N_CHIPS):
            cols = slice(j * n, (j + 1) * n)
            o_ref[:, cols] = jnp.dot(y, w_ref[j], preferred_element_type=F32) + b_ref[:, cols]

    return pl.pallas_call(
        body, name="pw1_proj", grid=(T // tm,),
        in_specs=[_rows(tm, D), _whole((1, D)), _whole((N_CHIPS, D, n)), _whole((1, N))],
        out_specs=[_rows(tm, D), _rows(tm, N)],
        out_shape=[SDS((T, D), BF16), SDS((T, N), F32)],
        compiler_params=_params("parallel"),
    )(h, g, w, b)


PAIRS_PER_KV = Q_PER_KV // 2


def _upper_lanes(shape):
    return lax.broadcasted_iota(jnp.int32, shape, 1) >= HEAD_DIM


def _swap_lane_halves(t):
    return pltpu.roll(t.astype(F32), HEAD_DIM, 1).astype(t.dtype)


def _kv_operands(g, t):
    swapped = _swap_lane_halves(t)
    in_lower, in_upper = (t, swapped) if g == 0 else (swapped, t)
    upper = _upper_lanes(t.shape)
    zero = jnp.zeros_like(t)
    return jnp.where(upper, zero, in_lower), jnp.where(upper, in_upper, zero)


def _group_heads(g):
    pairs = range(g * PAIRS_PER_KV, (g + 1) * PAIRS_PER_KV)
    return [2 * hp for hp in pairs] + [2 * hp + 1 for hp in pairs]


def _all_heads():
    return [h for g in range(N_KV_HEADS) for h in _group_heads(g)]


def _pair_rows(ref, g):
    pairs = range(g * PAIRS_PER_KV, (g + 1) * PAIRS_PER_KV)
    return jnp.concatenate([ref[:, hp * 2 * HEAD_DIM:(hp + 1) * 2 * HEAD_DIM] for hp in pairs], axis=0)


def _from_previous_block(rows):
    row = lax.broadcasted_iota(jnp.int32, (ATTN_BLOCK, ATTN_BLOCK), 0)
    col = lax.broadcasted_iota(jnp.int32, (ATTN_BLOCK, ATTN_BLOCK), 1)
    return jnp.concatenate([col > row] * (rows // ATTN_BLOCK), axis=0)


def _folded_probs(n, q_groups, k_prev_groups, k_cur_groups, sink_ref, prev_part):
    def scores(q, k):
        return lax.dot_general(q, k, (((1,), (1,)), ((), ())), preferred_element_type=F32)

    s_prev = jnp.concatenate([scores(q, k[i]) for q, k in zip(q_groups, k_prev_groups) for i in range(2)], axis=0)
    s_cur = jnp.concatenate([scores(q, k[i]) for q, k in zip(q_groups, k_cur_groups) for i in range(2)], axis=0)
    s_prev = jnp.where(n > 0, s_prev, MASK_VALUE * (HEAD_DIM ** 0.5))
    s = jnp.where(prev_part, s_prev, s_cur) * (HEAD_DIM ** -0.5)
    heads = [h for g in range(N_KV_HEADS) for h in _group_heads(g)]
    sink = jnp.concatenate([jnp.broadcast_to(sink_ref[0:1, h:h + 1], (ATTN_BLOCK, 1)) for h in heads], axis=0)
    m = jnp.maximum(jnp.max(s, axis=1, keepdims=True), sink)
    p = jnp.exp(s - m)
    e_sink = jnp.exp(sink - m)
    inv = 1.0 / (jnp.sum(p, axis=1, keepdims=True) + e_sink)
    return p * inv, e_sink * inv


def _split_folded(t, prev_part):
    tb = t.astype(BF16)
    zero = jnp.zeros_like(tb)
    return jnp.where(prev_part, tb, zero), jnp.where(prev_part, zero, tb)


def _attn_specs(T):
    nb = T // ATTN_BLOCK
    kcol = N_Q_HEADS * HEAD_DIM // V7X_LANES
    cur = lambda n: jnp.minimum(n, nb - 1)
    prev = lambda n: jnp.maximum(jnp.minimum(n, nb - 1) - 1, 0)
    q_spec = pl.BlockSpec((ATTN_BLOCK, N_Q_HEADS * HEAD_DIM), lambda n: (cur(n), 0))
    kc_spec = pl.BlockSpec((ATTN_BLOCK, V7X_LANES), lambda n: (cur(n), kcol))
    kp_spec = pl.BlockSpec((ATTN_BLOCK, V7X_LANES), lambda n: (prev(n), kcol))
    vc_spec = pl.BlockSpec((ATTN_BLOCK, V7X_LANES), lambda n: (cur(n), kcol + 1))
    vp_spec = pl.BlockSpec((ATTN_BLOCK, V7X_LANES), lambda n: (prev(n), kcol + 1))
    return q_spec, kc_spec, kp_spec, vc_spec, vp_spec


def _attn_fwd(qkv, sinks):
    T = qkv.shape[0]
    nb = T // ATTN_BLOCK
    qw = N_Q_HEADS * HEAD_DIM
    all_rows = N_Q_HEADS * ATTN_BLOCK

    def body(q_ref, kc_ref, kp_ref, vc_ref, vp_ref, sink_ref, o_ref, probs_ref, psink_ref):
        n = pl.program_id(0)
        prev_part = _from_previous_block(all_rows)
        half = PAIRS_PER_KV * ATTN_BLOCK
        groups = range(N_KV_HEADS)
        probs, p_sink = _folded_probs(n, [_pair_rows(q_ref, g) for g in groups],
                                      [_kv_operands(g, kp_ref[...]) for g in groups],
                                      [_kv_operands(g, kc_ref[...]) for g in groups], sink_ref, prev_part)
        probs_ref[...] = probs.astype(BF16)
        lane = lax.broadcasted_iota(jnp.int32, (ATTN_BLOCK, V7X_LANES), 1)
        sink_tile = jnp.zeros((ATTN_BLOCK, V7X_LANES), F32)
        for i, h in enumerate(_all_heads()):
            sink_tile = jnp.where(lane == h, p_sink[i * ATTN_BLOCK:(i + 1) * ATTN_BLOCK], sink_tile)
        psink_ref[...] = sink_tile
        p_prev, p_cur = _split_folded(probs, prev_part)
        for g in groups:
            v_prev, v_cur = _kv_operands(g, vp_ref[...]), _kv_operands(g, vc_ref[...])
            even, odd = slice(2 * g * half, (2 * g + 1) * half), slice((2 * g + 1) * half, (2 * g + 2) * half)
            o = (jnp.dot(p_prev[even], v_prev[0], preferred_element_type=F32)
                 + jnp.dot(p_cur[even], v_cur[0], preferred_element_type=F32)
                 + jnp.dot(p_prev[odd], v_prev[1], preferred_element_type=F32)
                 + jnp.dot(p_cur[odd], v_cur[1], preferred_element_type=F32))
            for i in range(PAIRS_PER_KV):
                hp = g * PAIRS_PER_KV + i
                o_ref[:, hp * 2 * HEAD_DIM:(hp + 1) * 2 * HEAD_DIM] = (
                    o[i * ATTN_BLOCK:(i + 1) * ATTN_BLOCK].astype(BF16))

    return pl.pallas_call(
        body, name="attn_fwd", grid=(nb,),
        in_specs=[*_attn_specs(T), _whole((1, N_Q_HEADS))],
        out_specs=[_rows(ATTN_BLOCK, qw), pl.BlockSpec((None, all_rows, ATTN_BLOCK), lambda n: (n, 0, 0)),
                   _rows(ATTN_BLOCK, V7X_LANES)],
        out_shape=[SDS((T, qw), BF16), SDS((nb, all_rows, ATTN_BLOCK), BF16), SDS((T, V7X_LANES), F32)],
        compiler_params=_params("parallel"),
    )(qkv, qkv, qkv, qkv, qkv, sinks)


def _mm_res(name, a, w, b, res, g):
    T, K = a.shape
    D = w.shape[1]
    tm = min(512, T)

    def body(a_ref, w_ref, b_ref, r_ref, g_ref, o_ref, f_ref):
        h = jnp.dot(a_ref[...], w_ref[...], preferred_element_type=F32) + b_ref[...] + r_ref[...]
        o_ref[...] = h
        f_ref[...] = (h * _rms_rstd(h) * g_ref[...]).astype(BF16)

    return pl.pallas_call(
        body, name=name, grid=(T // tm,),
        in_specs=[_rows(tm, K), _whole((K, D)), _whole((1, D)), _rows(tm, D), _whole((1, D))],
        out_specs=[_rows(tm, D), _rows(tm, D)],
        out_shape=[SDS((T, D), F32), SDS((T, D), BF16)],
        compiler_params=_params("parallel"),
    )(a, w, b, res, g)


def _ffn_down(name, s, w2, layer, res):
    _, T, n = s.shape
    D = w2.shape[3]
    tm = min(512, T)

    def body(s_ref, w_ref, r_ref, o_ref):
        acc = r_ref[...]
        for j in range(N_CHIPS):
            acc = acc + jnp.dot(s_ref[j], w_ref[j], preferred_element_type=F32)
        o_ref[...] = acc

    return pl.pallas_call(
        body, name=name, grid=(T // tm,),
        in_specs=[pl.BlockSpec((N_CHIPS, tm, n), lambda i: (0, i, 0)),
                  pl.BlockSpec((N_CHIPS, None, n, D), lambda i: (0, layer, 0, 0)), _rows(tm, D)],
        out_specs=_rows(tm, D),
        out_shape=SDS((T, D), F32),
        compiler_params=_params("parallel"),
    )(s, w2, res)


def _ffn_up(name, f, w1, w3, layer):
    T, D = f.shape
    n = w1.shape[2]
    tm = min(1024, T)

    def body(f_ref, w1_ref, w3_ref, act_ref, gg_ref, s_ref):
        ff = f_ref[...]
        g1 = _dot_nt(ff, w1_ref[...])
        g3 = _dot_nt(ff, w3_ref[...])
        act, dact = _silu_and_grad(g1)
        act_ref[...] = act.astype(BF16)
        gg_ref[...] = (g3 * dact).astype(BF16)
        s_ref[...] = (act * g3).astype(BF16)

    slab = pl.BlockSpec((None, tm, n), lambda j, i: (j, i, 0))
    wslab = pl.BlockSpec((None, None, n, D), lambda j, i: (j, layer, 0, 0))
    hidden = SDS((N_CHIPS, T, n), BF16)
    return pl.pallas_call(
        body, name=name, grid=(N_CHIPS, T // tm),
        in_specs=[pl.BlockSpec((tm, D), lambda j, i: (i, 0)), wslab, wslab],
        out_specs=[slab, slab, slab],
        out_shape=[hidden, hidden, hidden],
        compiler_params=_params("parallel", "parallel"),
    )(f, w1, w3)


def _glu(a, d):
    return a[:, :d] * jax.nn.sigmoid(a[:, d:])


def _conv_tile(T):
    return min(256, T)


def _fill_shifted(sh_ref, tc):
    n = tc + CONV_HALO - V7X_SUBLANES
    for r in range(1, V7X_SUBLANES):
        sh_ref[r, 0:n, :] = sh_ref[0, pl.ds(r, n), :]


def _depthwise_taps(sh_ref, w_ref, offsets, bias_ref, out_ref, tc):
    D = out_ref.shape[1]

    def chunk(i, carry):
        t0 = pl.multiple_of(i * CONV_ROW_CHUNK, CONV_ROW_CHUNK)
        for cb in range(D // CONV_LANE_CHUNK):
            cs = slice(cb * CONV_LANE_CHUNK, (cb + 1) * CONV_LANE_CHUNK)
            acc = jnp.zeros((CONV_ROW_CHUNK, CONV_LANE_CHUNK), F32)
            for r in range(V7X_SUBLANES):
                taps = [(j, o // V7X_SUBLANES) for j, o in enumerate(offsets) if o % V7X_SUBLANES == r]
                if not taps:
                    continue
                span = CONV_ROW_CHUNK + V7X_SUBLANES * max(q for _, q in taps)
                rows = sh_ref[r, pl.ds(t0, span), cs]
                for j, q in taps:
                    acc = acc + rows[V7X_SUBLANES * q:V7X_SUBLANES * q + CONV_ROW_CHUNK] * w_ref[j:j + 1, cs]
            if bias_ref is not None:
                acc = acc + bias_ref[:, cs]
            out_ref[pl.ds(t0, CONV_ROW_CHUNK), cs] = acc
        return carry

    lax.fori_loop(0, tc // CONV_ROW_CHUNK, chunk, 0)


def _depthwise_tap_grads(dy_sh, x_sh, offsets, dw_ref, tc):
    D = dw_ref.shape[1]
    for cb in range(D // V7X_LANES):
        cs = slice(cb * V7X_LANES, (cb + 1) * V7X_LANES)

        def row_tiles(i, accs, cs=cs):
            for k in range(CONV_GRAD_UNROLL):
                t0 = pl.multiple_of(i * (CONV_GRAD_UNROLL * V7X_SUBLANES), V7X_SUBLANES) + k * V7X_SUBLANES
                d = dy_sh[0, pl.ds(t0, V7X_SUBLANES), cs]
                accs = tuple(
                    acc + d * x_sh[o % V7X_SUBLANES, pl.ds(t0 + o // V7X_SUBLANES * V7X_SUBLANES, V7X_SUBLANES), cs]
                    for acc, o in zip(accs, offsets))
            return accs

        zero = jnp.zeros((V7X_SUBLANES, V7X_LANES), F32)
        accs = lax.fori_loop(0, tc // (CONV_GRAD_UNROLL * V7X_SUBLANES), row_tiles, tuple(zero for _ in offsets))
        for j, acc in enumerate(accs):
            dw_ref[j:j + 1, cs] += jnp.sum(acc, axis=0, keepdims=True)


def _conv_fwd(a, w_dw, b_dw, ln_g, ln_b, w_pw2, b_pw2, res, g_next):
    T = a.shape[0]
    D = a.shape[1] // 2
    tc = _conv_tile(T)
    per = tc // CONV_HALO

    def body(a_ref, ah_ref, w_ref, bdw_ref, lg_ref, lb_ref, wp_ref, bp_ref, r_ref, g_ref,
             c_ref, act_ref, h_ref, f_ref, u_sh):
        i = pl.program_id(0)
        u_sh[0, 0:CONV_HALO, :] = jnp.where(i > 0, _glu(ah_ref[...], D), 0.0)
        u_sh[0, CONV_HALO:, :] = _glu(a_ref[...], D)
        _fill_shifted(u_sh, tc)
        _depthwise_taps(u_sh, w_ref, [CONV_FIRST_TAP + j for j in range(CONV_WIDTH)], bdw_ref, c_ref, tc)
        c = c_ref[...]
        xc = c - jnp.mean(c, axis=-1, keepdims=True)
        z = xc * lax.rsqrt(jnp.mean(xc * xc, axis=-1, keepdims=True) + LN_EPS)
        l = z * lg_ref[...] + lb_ref[...]
        act = (l * jax.nn.sigmoid(l)).astype(BF16)
        act_ref[...] = act
        h = jnp.dot(act, wp_ref[...], preferred_element_type=F32) + bp_ref[...] + r_ref[...]
        h_ref[...] = h
        f_ref[...] = (h * _rms_rstd(h) * g_ref[...]).astype(BF16)

    return pl.pallas_call(
        body, name="conv_fwd", grid=(T // tc,),
        in_specs=[_rows(tc, 2 * D),
                  pl.BlockSpec((CONV_HALO, 2 * D), lambda i: (jnp.maximum(i * per - 1, 0), 0)),
                  _whole((CONV_WIDTH, D)), _whole((1, D)), _whole((1, D)), _whole((1, D)),
                  _whole((D, D)), _whole((1, D)), _rows(tc, D), _whole((1, D))],
        out_specs=[_rows(tc, D), _rows(tc, D), _rows(tc, D), _rows(tc, D)],
        out_shape=[SDS((T, D), F32), SDS((T, D), BF16), SDS((T, D), F32), SDS((T, D), BF16)],
        scratch_shapes=[pltpu.VMEM((V7X_SUBLANES, tc + CONV_HALO, D), F32)],
        compiler_params=_params("parallel"),
    )(a, a, w_dw, b_dw, ln_g, ln_b, w_pw2, b_pw2, res, g_next)


def _ffn_down_loss(name, s, w2, layer, res, g, target):
    _, T, n = s.shape
    D = w2.shape[3]
    tm = min(512, T)

    def body(s_ref, w_ref, r_ref, g_ref, t_ref, dh_ref, loss_ref, dg_ref):
        @pl.when(pl.program_id(0) == 0)
        def _():
            loss_ref[...] = jnp.zeros_like(loss_ref)
            dg_ref[...] = jnp.zeros_like(dg_ref)

        hh = r_ref[...]
        for j in range(N_CHIPS):
            hh = hh + jnp.dot(s_ref[j], w_ref[j], preferred_element_type=F32)
        r = _rms_rstd(hh)
        g = g_ref[...]
        d = hh * r * g - t_ref[...]
        loss_ref[...] += 0.5 * jnp.sum(jnp.mean(d * d, axis=-1, keepdims=True), axis=0, keepdims=True)
        dout = d * (1.0 / D)
        dg_ref[...] += jnp.sum(dout * (hh * r), axis=0, keepdims=True)
        dxh = dout * g
        dh_ref[...] = r * dxh - hh * (r * r * r) * jnp.mean(dxh * hh, axis=-1, keepdims=True)

    return pl.pallas_call(
        body, name=name, grid=(T // tm,),
        in_specs=[pl.BlockSpec((N_CHIPS, tm, n), lambda i: (0, i, 0)),
                  pl.BlockSpec((N_CHIPS, None, n, D), lambda i: (0, layer, 0, 0)), _rows(tm, D),
                  _whole((1, D)), _rows(tm, D)],
        out_specs=[_rows(tm, D), _whole((1, 1)), _whole((1, D))],
        out_shape=[SDS((T, D), F32), SDS((1, 1), F32), SDS((1, D), F32)],
        compiler_params=_params("arbitrary"),
    )(s, w2, res, g, target)


def _ffn_bwd_down(name, dh, w2, layer, act, gate_grad, s, after=None):
    T, D = dh.shape
    n = w2.shape[2]
    tm = min(256, T)

    def body(dh_ref, w2_ref, act_ref, gg_ref, s_ref, *rest):
        dg1_ref, dg3_ref, dw_ref = rest[-3:]

        @pl.when(pl.program_id(0) == 0)
        def _():
            dw_ref[...] = jnp.zeros_like(dw_ref)

        dhb = dh_ref[...].astype(BF16)
        for j in range(N_CHIPS):
            ds = _dot_nt(dhb, w2_ref[j])
            dg1_ref[j] = (ds * gg_ref[j].astype(F32)).astype(BF16)
            dg3_ref[j] = (ds * act_ref[j].astype(F32)).astype(BF16)
            dw_ref[j] += _dot_tn(s_ref[j], dhb)

    slabs = pl.BlockSpec((N_CHIPS, tm, n), lambda i: (0, i, 0))
    hidden = SDS((N_CHIPS, T, n), BF16)
    return pl.pallas_call(
        body, name=name, grid=(T // tm,),
        in_specs=[_rows(tm, D),
                  pl.BlockSpec((N_CHIPS, None, n, D), lambda i: (0, layer, 0, 0), pipeline_mode=pl.Buffered(1)),
                  slabs, slabs, slabs] + ([] if after is None else [pl.BlockSpec(memory_space=pl.ANY)]),
        out_specs=[slabs, slabs, _whole((N_CHIPS, n, D))],
        out_shape=[hidden, hidden, SDS((N_CHIPS, n, D), F32)],
        compiler_params=_params("arbitrary"),
    )(dh, w2, act, gate_grad, s, *([] if after is None else [after]))


def _dot_tn(a, b):
    return lax.dot_general(a.astype(BF16), b.astype(BF16), (((0,), (0,)), ((), ())), preferred_element_type=F32)


def _dot_nt(a, b):
    return lax.dot_general(a.astype(BF16), b, (((1,), (1,)), ((), ())), preferred_element_type=F32)


def _mm_tn(name, a, b, col_chunks=1, after=None):
    a_slabs, b_slabs = a.ndim == 3, b.ndim == 3
    T = a.shape[-2]
    tt = min(1024, T)
    ka, nb = a.shape[-1], b.shape[-1]
    if a_slabs or b_slabs:
        out_dims = (N_CHIPS, ka, nb)
    elif col_chunks > 1:
        out_dims = (col_chunks, ka, nb // col_chunks)
    else:
        out_dims = (ka, nb)

    def body(a_ref, b_ref, *rest):
        o_ref = rest[-1]

        @pl.when(pl.program_id(0) == 0)
        def _():
            o_ref[...] = jnp.zeros_like(o_ref)

        if a_slabs:
            bb = b_ref[...].astype(BF16)
            for j in range(N_CHIPS):
                o_ref[j] += _dot_tn(a_ref[j], bb)
        elif b_slabs:
            aa = a_ref[...].astype(BF16)
            for j in range(N_CHIPS):
                o_ref[j] += _dot_tn(aa, b_ref[j])
        elif col_chunks > 1:
            aa = a_ref[...].astype(BF16)
            w = nb // col_chunks
            for j in range(col_chunks):
                o_ref[j] += _dot_tn(aa, b_ref[:, j * w:(j + 1) * w])
        else:
            o_ref[...] += _dot_tn(a_ref[...], b_ref[...])

    def spec(arr, slabs):
        if slabs:
            return pl.BlockSpec((N_CHIPS, tt, arr.shape[-1]), lambda t: (0, t, 0))
        return _rows(tt, arr.shape[-1])

    return pl.pallas_call(
        body, name=name, grid=(T // tt,),
        in_specs=[spec(a, a_slabs), spec(b, b_slabs)] + ([] if after is None else [pl.BlockSpec(memory_space=pl.ANY)]),
        out_specs=_whole(out_dims),
        out_shape=SDS(out_dims, F32),
        compiler_params=_params("arbitrary"),
    )(a, b, *([] if after is None else [after]))


def _mm_nt_normbwd(name, pairs, h, g, dh, after):
    T, D = h.shape
    tm = min(512, T)
    n_pairs = len(pairs)
    kinds = ["slabs" if dy.ndim == 3 else ("quarters" if w.ndim == 3 else "plain") for dy, w, _ in pairs]

    def body(*refs):
        dy_refs = refs[:n_pairs]
        w_refs = refs[n_pairs:2 * n_pairs]
        h_ref, g_ref, dh_ref, _, o_ref, dg_ref, cs_ref = refs[2 * n_pairs:]

        @pl.when(pl.program_id(0) == 0)
        def _():
            dg_ref[...] = jnp.zeros_like(dg_ref)
            cs_ref[...] = jnp.zeros_like(cs_ref)

        df = jnp.zeros((tm, D), F32)
        for dy_ref, w_ref, kd in zip(dy_refs, w_refs, kinds):
            if kd == "slabs":
                for j in range(N_CHIPS):
                    df = df + jnp.dot(dy_ref[j], w_ref[j], preferred_element_type=F32)
            elif kd == "quarters":
                n = w_ref.shape[2]
                for j in range(N_CHIPS):
                    df = df + _dot_nt(dy_ref[:, j * n:(j + 1) * n], w_ref[j])
            else:
                df = df + jnp.dot(dy_ref[...], w_ref[...], preferred_element_type=F32)
        hh = h_ref[...]
        r = _rms_rstd(hh)
        dg_ref[...] += jnp.sum(df * (hh * r), axis=0, keepdims=True)
        dxh = df * g_ref[...]
        out = dh_ref[...] + (r * dxh - hh * (r * r * r) * jnp.mean(dxh * hh, axis=-1, keepdims=True))
        o_ref[...] = out
        cs_ref[...] += jnp.sum(out, axis=0, keepdims=True)

    dy_specs, w_specs = [], []
    for (dy, w, layer), kd in zip(pairs, kinds):
        if kd == "slabs":
            dy_specs.append(pl.BlockSpec((N_CHIPS, tm, dy.shape[2]), lambda i: (0, i, 0)))
            w_specs.append(pl.BlockSpec((N_CHIPS, None, w.shape[2], D),
                                        functools.partial(lambda i, layer: (0, layer, 0, 0), layer=layer),
                                        pipeline_mode=pl.Buffered(1)))
        else:
            dy_specs.append(_rows(tm, dy.shape[1]))
            w_specs.append(_whole(w.shape))

    return pl.pallas_call(
        body, name=name, grid=(T // tm,),
        in_specs=[*dy_specs, *w_specs, _rows(tm, D), _whole((1, D)), _rows(tm, D), pl.BlockSpec(memory_space=pl.ANY)],
        out_specs=[_rows(tm, D), _whole((1, D)), _whole((1, D))],
        out_shape=[SDS((T, D), F32), SDS((1, D), F32), SDS((1, D), F32)],
        compiler_params=_params("arbitrary"),
    )(*[dy for dy, _, _ in pairs], *[w for _, w, _ in pairs], h, g, dh, after)


def _mm_nt(name, dy, w, out_dtype):
    T, N = dy.shape
    K = w.shape[0]
    tm = min(512, T)

    def body(dy_ref, w_ref, o_ref):
        o_ref[...] = lax.dot_general(dy_ref[...].astype(BF16), w_ref[...], (((1,), (1,)), ((), ())),
                                     preferred_element_type=F32).astype(out_dtype)

    return pl.pallas_call(
        body, name=name, grid=(T // tm,),
        in_specs=[_rows(tm, N), _whole((K, N))],
        out_specs=_rows(tm, K),
        out_shape=SDS((T, K), out_dtype),
        compiler_params=_params("parallel"),
    )(dy, w)


def _conv_bwd(dh, w_pw2, c, a, w_dw, ln_g, ln_b):
    T, D = c.shape
    tc = _conv_tile(T)
    per = tc // CONV_HALO
    n_tiles = T // tc
    last_halo = T // CONV_HALO - 1

    def ln_bwd(dact_v, c_v, lg, lb):
        xc = c_v - jnp.mean(c_v, axis=-1, keepdims=True)
        rstd = lax.rsqrt(jnp.mean(xc * xc, axis=-1, keepdims=True) + LN_EPS)
        z = xc * rstd
        _, dsilu = _silu_and_grad(z * lg + lb)
        dl = dact_v * dsilu
        dz = dl * lg
        dc = rstd * (dz - jnp.mean(dz, axis=-1, keepdims=True) - z * jnp.mean(dz * z, axis=-1, keepdims=True))
        return dc, dl, z

    def body(dh_ref, dhn_ref, wp_ref, c_ref, cn_ref, a_ref, ah_ref, w_ref, lg_ref, lb_ref,
             da_ref, dlg_ref, dlb_ref, dbdw_ref, dwdw_ref, dbpw1_ref, dc_sh, u_sh, du_scr):
        i = pl.program_id(0)

        @pl.when(i == 0)
        def _():
            for ref in (dlg_ref, dlb_ref, dbdw_ref, dwdw_ref, dbpw1_ref):
                ref[...] = jnp.zeros_like(ref)

        lg, lb = lg_ref[...], lb_ref[...]
        dc, dl, z = ln_bwd(_dot_nt(dh_ref[...], wp_ref[...]), c_ref[...], lg, lb)
        dlg_ref[...] += jnp.sum(dl * z, axis=0, keepdims=True)
        dlb_ref[...] += jnp.sum(dl, axis=0, keepdims=True)
        dbdw_ref[...] += jnp.sum(dc, axis=0, keepdims=True)
        dcn, _, _ = ln_bwd(_dot_nt(dhn_ref[...], wp_ref[...]), cn_ref[...], lg, lb)
        dc_sh[0, 0:tc, :] = dc
        dc_sh[0, tc:, :] = jnp.where(i < n_tiles - 1, dcn, 0.0)
        _fill_shifted(dc_sh, tc)

        a_v = a_ref[...]
        a1 = a_v[:, :D]
        sg = jax.nn.sigmoid(a_v[:, D:])
        u_sh[0, 0:CONV_HALO, :] = jnp.where(i > 0, _glu(ah_ref[...], D), 0.0)
        u_sh[0, CONV_HALO:, :] = a1 * sg
        _fill_shifted(u_sh, tc)

        _depthwise_taps(dc_sh, w_ref, [CONV_WIDTH - 1 - j for j in range(CONV_WIDTH)], None, du_scr, tc)
        _depthwise_tap_grads(dc_sh, u_sh, [CONV_FIRST_TAP + j for j in range(CONV_WIDTH)], dwdw_ref, tc)

        du = du_scr[...]
        da1 = du * sg
        da2 = du * a1 * sg * (1.0 - sg)
        da_ref[:, :D] = da1.astype(BF16)
        da_ref[:, D:] = da2.astype(BF16)
        dbpw1_ref[:, :D] += jnp.sum(da1, axis=0, keepdims=True)
        dbpw1_ref[:, D:] += jnp.sum(da2, axis=0, keepdims=True)

    nxt = lambda i: (jnp.minimum((i + 1) * per, last_halo), 0)
    return pl.pallas_call(
        body, name="conv_bwd", grid=(n_tiles,),
        in_specs=[_rows(tc, D), pl.BlockSpec((CONV_HALO, D), nxt), _whole((D, D)),
                  _rows(tc, D), pl.BlockSpec((CONV_HALO, D), nxt),
                  _rows(tc, 2 * D),
                  pl.BlockSpec((CONV_HALO, 2 * D), lambda i: (jnp.maximum(i * per - 1, 0), 0)),
                  _whole((CONV_WIDTH, D)), _whole((1, D)), _whole((1, D))],
        out_specs=[_rows(tc, 2 * D), _whole((1, D)), _whole((1, D)), _whole((1, D)),
                   _whole((CONV_HALO, D)), _whole((1, 2 * D))],
        out_shape=[SDS((T, 2 * D), BF16), SDS((1, D), F32), SDS((1, D), F32), SDS((1, D), F32),
                   SDS((CONV_HALO, D), F32), SDS((1, 2 * D), F32)],
        scratch_shapes=[pltpu.VMEM((V7X_SUBLANES, tc + CONV_HALO, D), F32),
                        pltpu.VMEM((V7X_SUBLANES, tc + CONV_HALO, D), F32), pltpu.VMEM((tc, D), F32)],
        compiler_params=_params("arbitrary"),
    )(dh, dh, w_pw2, c, c, a, a, w_dw, ln_g, ln_b)


def _attn_bwd(qkv, dao, cos, sin, probs_saved, psink_saved):
    T = qkv.shape[0]
    nb = T // ATTN_BLOCK
    qw = N_Q_HEADS * HEAD_DIM
    kw = N_KV_HEADS * HEAD_DIM

    def body(q_ref, kc_ref, kp_ref, vc_ref, vp_ref, do_ref, cos_ref, sin_ref, cosp_ref, sinp_ref, probs_ref, psink_ref,
             dq_ref, dkv_ref, dsink_ref, dbq_ref, dbkv_ref, carry, prev_scr, cur_scr, dq_scr):
        n = pl.program_id(0)

        @pl.when(n == 0)
        def _():
            for ref in (dsink_ref, dbq_ref, dbkv_ref, carry):
                ref[...] = jnp.zeros_like(ref)

        @pl.when(n == nb)
        def _():
            prev_scr[...] = jnp.zeros_like(prev_scr)

        @pl.when(n < nb)
        def _():
            prev_part = _from_previous_block(N_Q_HEADS * ATTN_BLOCK)
            half = PAIRS_PER_KV * ATTN_BLOCK
            upper = _upper_lanes((ATTN_BLOCK, 2 * HEAD_DIM))
            groups = range(N_KV_HEADS)

            def nt(a, b):
                return lax.dot_general(a, b, (((1,), (1,)), ((), ())), preferred_element_type=F32)

            def kv_grad(even_rows, odd_rows, x):
                even = lax.dot_general(even_rows, x, (((0,), (0,)), ((), ())), preferred_element_type=F32)
                odd = lax.dot_general(odd_rows, x, (((0,), (0,)), ((), ())), preferred_element_type=F32)
                t = jnp.where(upper, odd, even)
                return t + _swap_lane_halves(t)

            q = [_pair_rows(q_ref, g) for g in groups]
            do = [_pair_rows(do_ref, g) for g in groups]
            k_prev = [_kv_operands(g, kp_ref[...]) for g in groups]
            k_cur = [_kv_operands(g, kc_ref[...]) for g in groups]
            v_prev = [_kv_operands(g, vp_ref[...]) for g in groups]
            v_cur = [_kv_operands(g, vc_ref[...]) for g in groups]
            probs = probs_ref[...].astype(F32)
            dp_prev = jnp.concatenate([nt(do[g], v_prev[g][i]) for g in groups for i in range(2)], axis=0)
            dp_cur = jnp.concatenate([nt(do[g], v_cur[g][i]) for g in groups for i in range(2)], axis=0)
            dp = jnp.where(prev_part, dp_prev, dp_cur)
            delta = jnp.sum(probs * dp, axis=1, keepdims=True)
            ds_prev, ds_cur = _split_folded(probs * (dp - delta) * (HEAD_DIM ** -0.5), prev_part)
            p_prev, p_cur = _split_folded(probs_ref[...], prev_part)
            for i, h in enumerate(_all_heads()):
                rows = slice(i * ATTN_BLOCK, (i + 1) * ATTN_BLOCK)
                dsink_ref[:, h:h + 1] += jnp.sum(-(psink_ref[:, h:h + 1] * delta[rows]), axis=0, keepdims=True)
            kv_grads = []
            for g in groups:
                even, odd = slice(2 * g * half, (2 * g + 1) * half), slice((2 * g + 1) * half, (2 * g + 2) * half)
                dq = (jnp.dot(ds_prev[even], k_prev[g][0], preferred_element_type=F32)
                      + jnp.dot(ds_cur[even], k_cur[g][0], preferred_element_type=F32)
                      + jnp.dot(ds_prev[odd], k_prev[g][1], preferred_element_type=F32)
                      + jnp.dot(ds_cur[odd], k_cur[g][1], preferred_element_type=F32))
                for i in range(PAIRS_PER_KV):
                    hp = g * PAIRS_PER_KV + i
                    dq_scr[:, hp * 2 * HEAD_DIM:(hp + 1) * 2 * HEAD_DIM] = dq[i * ATTN_BLOCK:(i + 1) * ATTN_BLOCK]
                kv_grads.append((kv_grad(ds_prev[even], ds_prev[odd], q[g]), kv_grad(ds_cur[even], ds_cur[odd], q[g]),
                                 kv_grad(p_prev[even], p_prev[odd], do[g]), kv_grad(p_cur[even], p_cur[odd], do[g])))
            (dkp0, dkc0, dvp0, dvc0), (dkp1, dkc1, dvp1, dvc1) = kv_grads
            prev_scr[:, :kw] = jnp.where(upper, dkp1, dkp0)
            prev_scr[:, kw:] = jnp.where(upper, dvp1, dvp0)
            cur_scr[:, :kw] = jnp.where(upper, dkc1, dkc0)
            cur_scr[:, kw:] = jnp.where(upper, dvc1, dvc0)
            dq_pre = _rope_transposed(dq_scr[...], cos_ref, sin_ref)
            dq_ref[...] = dq_pre.astype(BF16)
            dbq_ref[...] += jnp.sum(dq_pre, axis=0, keepdims=True)

        tot = carry[...] + prev_scr[...]
        dk_pre = _rope_transposed(tot[:, :kw], cosp_ref, sinp_ref)
        dkv_ref[:, :kw] = dk_pre.astype(BF16)
        dkv_ref[:, kw:] = tot[:, kw:].astype(BF16)
        dbkv_ref[:, :kw] += jnp.sum(dk_pre, axis=0, keepdims=True)
        dbkv_ref[:, kw:] += jnp.sum(tot[:, kw:], axis=0, keepdims=True)

        @pl.when(n < nb)
        def _():
            carry[...] = cur_scr[...]

    cur = lambda n: (jnp.minimum(n, nb - 1), 0)
    out_lag = lambda n: (jnp.maximum(n - 1, 0), 0)
    return pl.pallas_call(
        body, name="attn_bwd", grid=(nb + 1,),
        in_specs=[*_attn_specs(T),
                  pl.BlockSpec((ATTN_BLOCK, qw), cur),
                  pl.BlockSpec((ATTN_BLOCK, V7X_LANES), cur), pl.BlockSpec((ATTN_BLOCK, V7X_LANES), cur),
                  pl.BlockSpec((ATTN_BLOCK, V7X_LANES), out_lag), pl.BlockSpec((ATTN_BLOCK, V7X_LANES), out_lag),
                  pl.BlockSpec((None, N_Q_HEADS * ATTN_BLOCK, ATTN_BLOCK), lambda n: (jnp.minimum(n, nb - 1), 0, 0)),
                  pl.BlockSpec((ATTN_BLOCK, V7X_LANES), cur)],
        out_specs=[pl.BlockSpec((ATTN_BLOCK, qw), cur), pl.BlockSpec((ATTN_BLOCK, 2 * kw), out_lag),
                   _whole((1, N_Q_HEADS)), _whole((1, qw)), _whole((1, 2 * kw))],
        out_shape=[SDS((T, qw), BF16), SDS((T, 2 * kw), BF16),
                   SDS((1, N_Q_HEADS), F32), SDS((1, qw), F32), SDS((1, 2 * kw), F32)],
        scratch_shapes=[pltpu.VMEM((ATTN_BLOCK, 2 * kw), F32), pltpu.VMEM((ATTN_BLOCK, 2 * kw), F32),
                        pltpu.VMEM((ATTN_BLOCK, 2 * kw), F32), pltpu.VMEM((ATTN_BLOCK, qw), F32)],
        compiler_params=_params("arbitrary"),
    )(qkv, qkv, qkv, qkv, qkv, dao, cos, sin, cos, sin, probs_saved, psink_saved)


def _local_step(x, target, p, reduce_begin, reduce_send):
    T, D = x.shape
    cos, sin = _rope_tables(T)
    qw = N_Q_HEADS * HEAD_DIM
    nm, nf = p["norm_mix"], p["norm_ffn"]

    y0, qkv = _qkv_proj(x, nm[0:1], p["attn_w_qkv"], p["attn_b_qkv"], cos, sin, p["gather_started"])
    ao, attn_probs, sink_probs = _attn_fwd(qkv, p["attn_sinks"])
    h1, f0 = _mm_res("attn_out", ao, p["attn_w_o"], p["attn_b_o"], x, nf[0:1])
    p = {**p, **p["other_weights"](h1)}
    w1, w3, w2 = p["ffn_w1"], p["ffn_w3"], p["ffn_w2"]
    act0, gg0, s0 = _ffn_up("ffn0_up", f0, w1, w3, 0)
    h2 = _ffn_down("ffn0_down", s0, w2, 0, h1)
    y1, a = _pw1_proj(h2, nm[1:2], p["conv_w_pw1"], p["conv_b_pw1"])
    c, act, h3, f1 = _conv_fwd(a, p["conv_w_dw"], p["conv_b_dw"], p["conv_ln_g"], p["conv_ln_b"],
                               p["conv_w_pw2"], p["conv_b_pw2"], h2, nf[1:2])
    act1, gg1, s1 = _ffn_up("ffn1_up", f1, w1, w3, 1)
    dh4, loss, d_norm_final = _ffn_down_loss("ffn1_down_loss", s1, w2, 1, h3, p["norm_final"], target)

    g = {}
    dg1, dg3, dw2_1 = _ffn_bwd_down("ffn1_bwd_down", dh4, w2, 1, act1, gg1, s1)
    dw1_1 = _mm_tn("ffn1_dw1", dg1, f1)
    dw3_1 = _mm_tn("ffn1_dw3", dg3, f1)
    begun = reduce_begin("ffn1", {("ffn_w1", 1): dw1_1, ("ffn_w3", 1): dw3_1, ("ffn_w2", 1): dw2_1})
    dh3, dnf1, db_pw2 = _mm_nt_normbwd("ffn1_bwd_in", [(dg1, w1, 1), (dg3, w3, 1)], h3, nf[1:2], dh4, begun)
    sent = reduce_send("ffn1", dh3)

    dw_pw2 = _mm_tn("conv_dw_pw2", act, dh3, after=sent)
    da, d_ln_g, d_ln_b, d_b_dw, d_w_dw, d_b_pw1 = _conv_bwd(dh3, p["conv_w_pw2"], c, a, p["conv_w_dw"],
                                                            p["conv_ln_g"], p["conv_ln_b"])
    dw_pw1 = _mm_tn("conv_dw_pw1", y1, da, col_chunks=N_CHIPS)
    begun = reduce_begin("conv", {("conv_w_pw2", 0): dw_pw2.reshape(N_CHIPS, -1, D), ("conv_w_pw1", 0): dw_pw1})
    dh2, dnm1, _ = _mm_nt_normbwd("conv_bwd_in", [(da, p["conv_w_pw1"], None)], h2, nm[1:2], dh3, begun)
    sent = reduce_send("conv", dh2)

    dg1, dg3, dw2_0 = _ffn_bwd_down("ffn0_bwd_down", dh2, w2, 0, act0, gg0, s0, after=sent)
    dw1_0 = _mm_tn("ffn0_dw1", dg1, f0)
    dw3_0 = _mm_tn("ffn0_dw3", dg3, f0)
    begun = reduce_begin("ffn0", {("ffn_w1", 0): dw1_0, ("ffn_w3", 0): dw3_0, ("ffn_w2", 0): dw2_0})
    dh1, dnf0, db_o = _mm_nt_normbwd("ffn0_bwd_in", [(dg1, w1, 0), (dg3, w3, 0)], h1, nf[0:1], dh2, begun)
    sent = reduce_send("ffn0", dh1)

    dw_o = _mm_tn("attn_dw_o", ao, dh1, after=sent)
    dao = _mm_nt("attn_bwd_out", dh1, p["attn_w_o"], BF16)
    dq, dkv, d_sinks, dbq, dbkv = _attn_bwd(qkv, dao, cos, sin, attn_probs, sink_probs)
    dwq = _mm_tn("attn_dw_q", dq, y0)
    dwkv = _mm_tn("attn_dw_kv", dkv, y0)
    wqkv = p["attn_w_qkv"]
    dwqkv = jnp.concatenate([dwq, dwkv], axis=0).reshape(N_CHIPS, -1, D)
    begun = reduce_begin("attn", {("attn_w_o", 0): dw_o.reshape(N_CHIPS, -1, D), ("attn_w_qkv", 0): dwqkv})
    dx, dnm0, _ = _mm_nt_normbwd("attn_bwd_in", [(dq, wqkv[:qw], None), (dkv, wqkv[qw:], None)], x, nm[0:1], dh1,
                                 begun)
    reduce_send("attn", dx)

    g["norm_mix"] = jnp.concatenate([dnm0, dnm1], axis=0)
    g["norm_ffn"] = jnp.concatenate([dnf0, dnf1], axis=0)
    g["attn_b_qkv"] = jnp.concatenate([dbq, dbkv], axis=1)
    g["attn_sinks"] = d_sinks
    g["attn_b_o"] = db_o
    g["conv_b_pw1"] = d_b_pw1
    g["conv_w_dw"] = d_w_dw[:CONV_WIDTH]
    g["conv_b_dw"] = d_b_dw
    g["conv_ln_g"] = d_ln_g
    g["conv_ln_b"] = d_ln_b
    g["conv_b_pw2"] = db_pw2
    g["norm_final"] = d_norm_final
    return loss, dx, g


ANY = pl.BlockSpec(memory_space=pl.ANY)
VMEM_WHOLE = pl.BlockSpec(memory_space=pltpu.VMEM)


def _my_place():
    return lax.axis_index("x"), lax.axis_index("y"), lax.axis_index("c")


def _other_chips(x, y):
    places = [(1 - x, y), (x, 1 - y), (1 - x, 1 - y)]
    return [(bx, by, 2 * bx + by) for bx, by in places]


def _gather_small(name, v):
    r, w = v.shape

    def body(v_ref, o_ref, send_sems, recv_sems):
        x, y, c = _my_place()
        pairs = _all_to_all_copies(v_ref, o_ref, send_sems, recv_sems)
        for send, _ in pairs:
            send.start()
        o_ref[4 * x + 2 * y + c] = v_ref[...]
        for send, arrival in pairs:
            arrival.wait_recv()
            send.wait_send()

    return pl.pallas_call(
        body, name=name, out_shape=SDS((N_DEV, r, w), F32), in_specs=[VMEM_WHOLE], out_specs=VMEM_WHOLE,
        scratch_shapes=[pltpu.SemaphoreType.DMA((N_DEV - 1,)), pltpu.SemaphoreType.DMA((N_DEV - 1,))],
        compiler_params=pltpu.CompilerParams(vmem_limit_bytes=V7X_VMEM_LIMIT_BYTES),
    )(v)


def _cast_into_slot(name, gathered, shard, chip_idx):
    rows, cols = shard.shape
    tr = _pack_row_tile(rows)

    def body(k_ref, s_ref, g_ref, o_ref):
        o_ref[...] = s_ref[...].astype(BF16)

    return pl.pallas_call(
        body, name=name,
        grid_spec=pltpu.PrefetchScalarGridSpec(
            num_scalar_prefetch=1, grid=(rows // tr,),
            in_specs=[pl.BlockSpec((tr, cols), lambda i, k_ref: (i, 0)), pl.BlockSpec(memory_space=pl.ANY)],
            out_specs=pl.BlockSpec((None, tr, cols), lambda i, k_ref: (k_ref[0], i, 0))),
        out_shape=SDS(gathered.shape, BF16),
        input_output_aliases={2: 0},
        compiler_params=_params("parallel"),
    )(chip_idx, shard, gathered)


def _row_halves(ref, c):
    half = ref.shape[1] // 2
    return pl.ds(pl.multiple_of(c * half, 16), half), pl.ds(pl.multiple_of((1 - c) * half, 16), half)


def _gather_ici_copies(refs, send_sems, recv_sems):
    x, y, c = _my_place()
    k = 2 * x + y
    pairs = []
    for i, ref in enumerate(refs):
        mine, _ = _row_halves(ref, c)
        for j, (bx, by, kb) in enumerate(_other_chips(x, y)):
            sems = dict(send_sem=send_sems.at[3 * i + j], recv_sem=recv_sems.at[3 * i + j], device_id_type=MESH)
            send = pltpu.make_async_remote_copy(src_ref=ref.at[k, mine], dst_ref=ref.at[k, mine],
                                                device_id=(bx, by, c), **sems)
            arrival = pltpu.make_async_remote_copy(src_ref=ref.at[kb, mine], dst_ref=ref.at[kb, mine],
                                                   device_id=(bx, by, c), **sems)
            pairs.append((send, arrival))
    return pairs


def _gather_d2d_copies(refs, send_sems, recv_sems, first_sem):
    x, y, c = _my_place()
    pairs = []
    for i, ref in enumerate(refs):
        mine, theirs = _row_halves(ref, c)
        for j, (_, _, kb) in enumerate(_other_chips(x, y)):
            sem = first_sem + 3 * i + j
            sems = dict(send_sem=send_sems.at[sem], recv_sem=recv_sems.at[sem], device_id=(x, y, 1 - c),
                        device_id_type=MESH)
            send = pltpu.make_async_remote_copy(src_ref=ref.at[kb, mine], dst_ref=ref.at[kb, mine], **sems)
            arrival = pltpu.make_async_remote_copy(src_ref=ref.at[kb, theirs], dst_ref=ref.at[kb, theirs], **sems)
            pairs.append((send, arrival))
    return pairs


def _run_copies(pairs):
    for send, _ in pairs:
        send.start()
    for send, arrival in pairs:
        send.wait_send()
        arrival.wait_recv()


def _gather_now(name, gathered):
    n_w = len(gathered)

    def body(*refs):
        in_refs = refs[:n_w]
        send_sems, recv_sems = refs[2 * n_w:]
        _run_copies(_gather_ici_copies(in_refs, send_sems, recv_sems))
        _run_copies(_gather_d2d_copies(in_refs, send_sems, recv_sems, 3 * n_w))

    return pl.pallas_call(
        body, name=name, out_shape=[SDS(g.shape, g.dtype) for g in gathered],
        in_specs=[ANY] * n_w, out_specs=[ANY] * n_w, input_output_aliases={i: i for i in range(n_w)},
        scratch_shapes=[pltpu.SemaphoreType.DMA((6 * n_w,)), pltpu.SemaphoreType.DMA((6 * n_w,))],
    )(*gathered)


def _gather_start(name, gathered, after):
    n_w = len(gathered)

    def body(*refs):
        in_refs = refs[:n_w]
        send_sems, recv_sems = refs[n_w + 1:n_w + 3]
        for send, _ in _gather_ici_copies(in_refs, send_sems, recv_sems):
            send.start()
        refs[-1][...] = jnp.zeros_like(refs[-1])

    out = pl.pallas_call(
        body, name=name,
        out_shape=(pltpu.SemaphoreType.DMA((3 * n_w,)), pltpu.SemaphoreType.DMA((3 * n_w,)),
                   *[pltpu.HBM(g.shape, g.dtype) for g in gathered], SDS((8, V7X_LANES), F32)),
        in_specs=[*[HBM_SPEC] * n_w, ANY], out_specs=(SEM_SPEC, SEM_SPEC, *[HBM_SPEC] * n_w, VMEM_WHOLE),
        input_output_aliases={i: 2 + i for i in range(n_w)},
        compiler_params=pltpu.CompilerParams(has_side_effects=DATAFLOW),
    )(*[pltpu.with_memory_space_constraint(g, pltpu.HBM) for g in gathered], after)
    return out[0], out[1], list(out[2:2 + n_w]), out[-1]


def _gather_wait(name, send_sems, recv_sems, gathered, after):
    n_w = len(gathered)

    def body(*refs):
        in_refs = refs[:n_w]
        send_sems, recv_sems = refs[n_w:n_w + 2]
        for send, arrival in _gather_ici_copies(in_refs, send_sems, recv_sems):
            send.wait_send()
            arrival.wait_recv()

    out = pl.pallas_call(
        body, name=name, out_shape=tuple(pltpu.HBM(g.shape, g.dtype) for g in gathered),
        in_specs=[*[HBM_SPEC] * n_w, SEM_SPEC, SEM_SPEC, ANY], out_specs=tuple([HBM_SPEC] * n_w),
        input_output_aliases={i: i for i in range(n_w)},
        compiler_params=pltpu.CompilerParams(has_side_effects=DATAFLOW),
    )(*gathered, send_sems, recv_sems, after)
    return list(out)


def _swap_fetched_with_sibling(name, gathered):
    n_w = len(gathered)

    def body(*refs):
        in_refs = refs[:n_w]
        send_sems, recv_sems = refs[2 * n_w:]
        _run_copies(_gather_d2d_copies(in_refs, send_sems, recv_sems, 0))

    return pl.pallas_call(
        body, name=name, out_shape=[SDS(g.shape, g.dtype) for g in gathered],
        in_specs=[ANY] * n_w, out_specs=[ANY] * n_w, input_output_aliases={i: i for i in range(n_w)},
        scratch_shapes=[pltpu.SemaphoreType.DMA((3 * n_w,)), pltpu.SemaphoreType.DMA((3 * n_w,))],
    )(*gathered)


def _sibling_swap_copies(g_refs, land_refs, send_sems, recv_sems):
    x, y, c = _my_place()
    copies = []
    for i, g_ref in enumerate(g_refs):
        half = g_ref.shape[1] // 2
        theirs = pl.ds(pl.multiple_of((1 - c) * half, 8), half)
        copies.append(pltpu.make_async_remote_copy(
            src_ref=g_ref.at[:, theirs], dst_ref=land_refs[i], send_sem=send_sems.at[i], recv_sem=recv_sems.at[i],
            device_id=(x, y, 1 - c), device_id_type=MESH))
    return copies


def _sibling_swap_start(name, grads):
    n_g = len(grads)

    def body(*refs):
        g_refs, land_refs = refs[:n_g], refs[n_g:2 * n_g]
        send_sems, recv_sems = refs[2 * n_g:2 * n_g + 2]
        for cp in _sibling_swap_copies(g_refs, land_refs, send_sems, recv_sems):
            cp.start()
        refs[-1][...] = jnp.zeros_like(refs[-1])

    lands = [pltpu.with_memory_space_constraint(lax.empty((g.shape[0], g.shape[1] // 2, g.shape[2]), g.dtype),
                                                pltpu.HBM) for g in grads]
    out = pl.pallas_call(
        body, name=name,
        out_shape=(pltpu.SemaphoreType.DMA((n_g,)), pltpu.SemaphoreType.DMA((n_g,)),
                   *[pltpu.HBM(g.shape, g.dtype) for g in grads], *[pltpu.HBM(l.shape, l.dtype) for l in lands],
                   SDS((8, V7X_LANES), F32)),
        in_specs=[HBM_SPEC] * (2 * n_g), out_specs=(SEM_SPEC, SEM_SPEC, *[HBM_SPEC] * (2 * n_g), VMEM_WHOLE),
        input_output_aliases={i: 2 + i for i in range(2 * n_g)},
        compiler_params=pltpu.CompilerParams(has_side_effects=DATAFLOW),
    )(*[pltpu.with_memory_space_constraint(g, pltpu.HBM) for g in grads], *lands)
    return out[0], out[1], list(out[2:2 + n_g]), list(out[2 + n_g:2 + 2 * n_g]), out[-1]


def _sibling_swap_wait(name, send_sems, recv_sems, grads, lands, after):
    n_g = len(grads)

    def body(*refs):
        g_refs, land_refs = refs[:n_g], refs[n_g:2 * n_g]
        send_sems, recv_sems = refs[2 * n_g:2 * n_g + 2]
        for cp in _sibling_swap_copies(g_refs, land_refs, send_sems, recv_sems):
            cp.wait_send()
            cp.wait_recv()

    out = pl.pallas_call(
        body, name=name,
        out_shape=(*[pltpu.HBM(g.shape, g.dtype) for g in grads], *[pltpu.HBM(l.shape, l.dtype) for l in lands]),
        in_specs=[*[HBM_SPEC] * (2 * n_g), SEM_SPEC, SEM_SPEC, ANY], out_specs=tuple([HBM_SPEC] * (2 * n_g)),
        input_output_aliases={i: i for i in range(2 * n_g)},
        compiler_params=pltpu.CompilerParams(has_side_effects=DATAFLOW),
    )(*grads, *lands, send_sems, recv_sems, after)
    return list(out[:n_g]), list(out[n_g:])


def _pack_row_tile(rows):
    for t in range(min(rows, 512), 7, -1):
        if rows % t == 0 and t % 8 == 0:
            return t
    return rows


def _add_sibling_half(name, grads, from_sibling, c_idx):
    n, R, w = grads.shape
    half = R // 2
    tr = _pack_row_tile(half)
    steps = half // tr

    def body(c_ref, g_ref, s_ref, o_ref):
        o_ref[...] = (g_ref[...] + s_ref[...]).astype(BF16)

    return pl.pallas_call(
        body, name=name,
        grid_spec=pltpu.PrefetchScalarGridSpec(
            num_scalar_prefetch=1, grid=(n, steps),
            in_specs=[pl.BlockSpec((1, tr, w), lambda j, i, c_ref: (j, c_ref[0] * steps + i, 0)),
                      pl.BlockSpec((1, tr, w), lambda j, i, c_ref: (j, i, 0))],
            out_specs=pl.BlockSpec((1, tr, w), lambda j, i, c_ref: (j, i, 0))),
        out_shape=SDS((n, half, w), BF16),
        compiler_params=_params("parallel", "parallel"),
    )(c_idx, grads, from_sibling)


HBM_SPEC = pl.BlockSpec(memory_space=pltpu.HBM)
SEM_SPEC = pl.BlockSpec(memory_space=pltpu.SEMAPHORE)
DATAFLOW = pltpu.SideEffectType.DATAFLOW_SIDE_EFFECTING


def _chip_scatter_copies(p_refs, land_refs, send_sems, recv_sems):
    x, y, c = _my_place()
    return [pltpu.make_async_remote_copy(
        src_ref=p_refs[i].at[kb], dst_ref=land_refs[i].at[j], send_sem=send_sems.at[3 * i + j],
        recv_sem=recv_sems.at[3 * i + j], device_id=(bx, by, c), device_id_type=MESH)
        for i in range(len(p_refs)) for j, (bx, by, kb) in enumerate(_other_chips(x, y))]


def _scatter_start(name, partials):
    n_p = len(partials)

    def body(*refs):
        p_refs, land_refs = refs[:n_p], refs[n_p:2 * n_p]
        send_sems, recv_sems = refs[2 * n_p:2 * n_p + 2]
        for cp in _chip_scatter_copies(p_refs, land_refs, send_sems, recv_sems):
            cp.start()
        refs[-1][...] = jnp.zeros_like(refs[-1])

    lands = [pltpu.with_memory_space_constraint(lax.empty((N_CHIPS - 1,) + p.shape[1:], p.dtype), pltpu.HBM)
             for p in partials]
    out = pl.pallas_call(
        body, name=name,
        out_shape=(pltpu.SemaphoreType.DMA((3 * n_p,)), pltpu.SemaphoreType.DMA((3 * n_p,)),
                   *[pltpu.HBM(p.shape, p.dtype) for p in partials], *[pltpu.HBM(l.shape, l.dtype) for l in lands],
                   SDS((8, V7X_LANES), F32)),
        in_specs=[HBM_SPEC] * (2 * n_p), out_specs=(SEM_SPEC, SEM_SPEC, *[HBM_SPEC] * (2 * n_p), VMEM_WHOLE),
        input_output_aliases={i: 2 + i for i in range(2 * n_p)},
        compiler_params=pltpu.CompilerParams(has_side_effects=DATAFLOW),
    )(*[pltpu.with_memory_space_constraint(p, pltpu.HBM) for p in partials], *lands)
    return out[0], out[1], list(out[2:2 + n_p]), list(out[2 + n_p:2 + 2 * n_p]), out[-1]


def _scatter_wait(name, send_sems, recv_sems, partials, lands, after):
    n_p = len(partials)

    def body(*refs):
        p_refs, land_refs = refs[:n_p], refs[n_p:2 * n_p]
        send_sems, recv_sems = refs[2 * n_p:2 * n_p + 2]
        for cp in _chip_scatter_copies(p_refs, land_refs, send_sems, recv_sems):
            cp.wait_send()
            cp.wait_recv()

    out = pl.pallas_call(
        body, name=name,
        out_shape=(*[pltpu.HBM(p.shape, p.dtype) for p in partials], *[pltpu.HBM(l.shape, l.dtype) for l in lands]),
        in_specs=[*[HBM_SPEC] * (2 * n_p), SEM_SPEC, SEM_SPEC, ANY], out_specs=tuple([HBM_SPEC] * (2 * n_p)),
        input_output_aliases={i: i for i in range(2 * n_p)},
        compiler_params=pltpu.CompilerParams(has_side_effects=DATAFLOW),
    )(*partials, *lands, send_sems, recv_sems, after)
    return list(out[:n_p]), list(out[n_p:])


def _sum_chip_partials(name, grads, from_sibling, received, shard, layer, place):
    n, half, w = from_sibling.shape
    tr = _pack_row_tile(half)
    steps = half // tr

    def body(place_ref, g_ref, s_ref, r_ref, shard_ref, o_ref):
        own = g_ref[0] + s_ref[0]
        o_ref[...] = ((own + r_ref[0].astype(F32)) + r_ref[1].astype(F32)) + r_ref[2].astype(F32)

    return pl.pallas_call(
        body, name=name,
        grid_spec=pltpu.PrefetchScalarGridSpec(
            num_scalar_prefetch=1, grid=(steps,),
            in_specs=[pl.BlockSpec((1, tr, w), lambda i, place_ref: (place_ref[0], place_ref[1] * steps + i, 0)),
                      pl.BlockSpec((1, tr, w), lambda i, place_ref: (place_ref[0], i, 0)),
                      pl.BlockSpec((n - 1, tr, w), lambda i, place_ref: (0, i, 0)),
                      pl.BlockSpec(memory_space=pl.ANY)],
            out_specs=pl.BlockSpec((tr, w), lambda i, place_ref: ((2 * layer + place_ref[1]) * steps + i, 0))),
        out_shape=SDS(shard.shape, F32),
        input_output_aliases={4: 0},
        compiler_params=_params("parallel"),
    )(place, grads, from_sibling, received, shard)


def _join_halves(shards, layers):
    n_s = len(shards)
    n_sem = sum(layers)

    def body(*refs):
        in_refs = refs[:n_s]
        send_sems, recv_sems = refs[2 * n_s:]
        x, y, c = _my_place()
        copies, sem = [], 0
        for ref, n_layers in zip(in_refs, layers):
            half = ref.shape[0] // (2 * n_layers)
            for layer in range(n_layers):
                mine = pl.ds(pl.multiple_of(layer * 2 * half + c * half, 8), half)
                theirs = pl.ds(pl.multiple_of(layer * 2 * half + (1 - c) * half, 8), half)
                send = pltpu.make_async_remote_copy(
                    src_ref=ref.at[mine], dst_ref=ref.at[mine], send_sem=send_sems.at[sem], recv_sem=recv_sems.at[sem],
                    device_id=(x, y, 1 - c), device_id_type=MESH)
                send.start()
                arrival = pltpu.make_async_remote_copy(
                    src_ref=ref.at[theirs], dst_ref=ref.at[theirs], send_sem=send_sems.at[sem],
                    recv_sem=recv_sems.at[sem], device_id=(x, y, 1 - c), device_id_type=MESH)
                copies.append((send, arrival))
                sem += 1
        for send, arrival in copies:
            send.wait_send()
            arrival.wait_recv()

    return pl.pallas_call(
        body, name="join_halves", out_shape=[SDS(s.shape, s.dtype) for s in shards],
        in_specs=[ANY] * n_s, out_specs=[ANY] * n_s,
        input_output_aliases={i: i for i in range(n_s)},
        scratch_shapes=[pltpu.SemaphoreType.DMA((n_sem,)), pltpu.SemaphoreType.DMA((n_sem,))],
    )(*shards)


def _all_to_all_copies(v_ref, land_ref, send_sems, recv_sems):
    x, y, c = _my_place()
    me = 4 * x + 2 * y + c
    pairs = []
    for k in range(1, N_DEV):
        px, py, pc = (1 - x if k & 4 else x), (1 - y if k & 2 else y), (1 - c if k & 1 else c)
        sems = dict(send_sem=send_sems.at[k - 1], recv_sem=recv_sems.at[k - 1], device_id=(px, py, pc),
                    device_id_type=MESH)
        send = pltpu.make_async_remote_copy(src_ref=v_ref, dst_ref=land_ref.at[me], **sems)
        arrival = pltpu.make_async_remote_copy(src_ref=v_ref, dst_ref=land_ref.at[4 * px + 2 * py + pc], **sems)
        pairs.append((send, arrival))
    return pairs


def _small_reduce_start(name, v):
    def body(v_ref, land_ref, send_sems, recv_sems, v_out, land_out):
        for send, _ in _all_to_all_copies(v_ref, land_ref, send_sems, recv_sems):
            send.start()

    land = pltpu.with_memory_space_constraint(jnp.zeros((N_DEV,) + v.shape, v.dtype), pltpu.HBM)
    return pl.pallas_call(
        body, name=name,
        out_shape=(pltpu.SemaphoreType.DMA((N_DEV - 1,)), pltpu.SemaphoreType.DMA((N_DEV - 1,)),
                   pltpu.HBM(v.shape, v.dtype), pltpu.HBM(land.shape, land.dtype)),
        in_specs=[HBM_SPEC, HBM_SPEC], out_specs=(SEM_SPEC, SEM_SPEC, HBM_SPEC, HBM_SPEC),
        input_output_aliases={0: 2, 1: 3},
        compiler_params=pltpu.CompilerParams(has_side_effects=DATAFLOW),
    )(pltpu.with_memory_space_constraint(v, pltpu.HBM), land)


def _small_reduce_wait(name, send_sems, recv_sems, v, land, after):
    def body(v_ref, land_ref, send_sems, recv_sems, after_ref, v_out, land_out):
        for send, arrival in _all_to_all_copies(v_ref, land_ref, send_sems, recv_sems):
            send.wait_send()
            arrival.wait_recv()

    return pl.pallas_call(
        body, name=name, out_shape=(pltpu.HBM(v.shape, v.dtype), pltpu.HBM(land.shape, land.dtype)),
        in_specs=[HBM_SPEC, HBM_SPEC, SEM_SPEC, SEM_SPEC, ANY], out_specs=(HBM_SPEC, HBM_SPEC),
        input_output_aliases={0: 0, 1: 1},
        compiler_params=pltpu.CompilerParams(has_side_effects=DATAFLOW),
    )(v, land, send_sems, recv_sems, after)


def _sum_device_slots(name, v, land, me):
    r, w = v.shape

    def body(me_ref, v_ref, land_ref, o_ref):
        mine = v_ref[...]
        acc = jnp.where(me_ref[0] == 0, mine, land_ref[0])
        for d in range(1, N_DEV):
            acc = acc + jnp.where(me_ref[0] == d, mine, land_ref[d])
        o_ref[...] = acc

    return pl.pallas_call(
        body, name=name,
        grid_spec=pltpu.PrefetchScalarGridSpec(
            num_scalar_prefetch=1, grid=(1,),
            in_specs=[pl.BlockSpec((r, w), lambda i, me_ref: (0, 0)),
                      pl.BlockSpec((N_DEV, r, w), lambda i, me_ref: (0, 0, 0))],
            out_specs=pl.BlockSpec((r, w), lambda i, me_ref: (0, 0))),
        out_shape=SDS((r, w), F32),
        compiler_params=_params("arbitrary"),
    )(me, v, land)


def _adamw(name, w, g, m, v):
    rows, width = w.shape
    tr = _pack_row_tile(rows)

    def body(w_ref, g_ref, m_ref, v_ref, g_out_ref, d_ref, nm_ref, nv_ref):
        gg = g_ref[...]
        g_out_ref[...] = gg
        m_new = ADAM_B1 * m_ref[...] + (1.0 - ADAM_B1) * gg
        v_new = ADAM_B2 * v_ref[...] + (1.0 - ADAM_B2) * (gg * gg)
        m_hat = m_new / (1.0 - ADAM_B1 ** ADAM_STEP)
        v_hat = v_new / (1.0 - ADAM_B2 ** ADAM_STEP)
        d_ref[...] = -ADAM_LR * (m_hat / (jnp.sqrt(v_hat) + ADAM_EPS) + ADAM_WD * w_ref[...])
        nm_ref[...] = m_new
        nv_ref[...] = v_new

    spec = _rows(tr, width)
    return pl.pallas_call(
        body, name=name, grid=(rows // tr,),
        in_specs=[spec] * 4, out_specs=[spec] * 4,
        out_shape=[SDS((rows, width), F32)] * 4,
        compiler_params=_params("parallel"),
    )(w, g, m, v)


WEIGHT_NAMES = ['norm_mix', 'norm_ffn', 'attn_w_qkv', 'attn_b_qkv', 'attn_sinks', 'attn_w_o', 'attn_b_o',
                'conv_w_pw1', 'conv_b_pw1', 'conv_w_dw', 'conv_b_dw', 'conv_ln_g', 'conv_ln_b', 'conv_w_pw2',
                'conv_b_pw2', 'ffn_w1', 'ffn_w3', 'ffn_w2', 'norm_final']
BIG = ['attn_w_qkv', 'attn_w_o', 'conv_w_pw1', 'conv_w_pw2', 'ffn_w1', 'ffn_w3', 'ffn_w2']
COLUMN_SPLIT = ('attn_w_qkv', 'conv_w_pw1', 'ffn_w1', 'ffn_w3')
STORED_TRANSPOSED = ('attn_w_qkv', 'ffn_w1', 'ffn_w3')
SMALL_SPLIT = ['conv_b_pw1', 'conv_w_dw', 'conv_b_dw', 'conv_ln_g', 'conv_ln_b', 'conv_b_pw2']
SMALL_WHOLE = ['norm_mix', 'norm_ffn', 'attn_b_qkv', 'attn_sinks', 'attn_b_o', 'norm_final']


def _keeps_rows(shape):
    return len(shape) == 2 and shape[0] > 1 and shape[1] == PACK_W


def _pack_rows(arrays, dtype, row_multiple):
    blocks = [jnp.pad(a.astype(dtype), ((0, -a.shape[0] % V7X_SUBLANES), (0, 0)))
              for a in arrays if _keeps_rows(a.shape)]
    flat = jnp.concatenate([a.astype(dtype).reshape(-1) for a in arrays if not _keeps_rows(a.shape)])
    multiple = max(row_multiple, V7X_SUBLANES)
    rows = -(-(-(-flat.shape[0] // PACK_W)) // multiple) * multiple
    blocks.append(jnp.pad(flat, (0, rows * PACK_W - flat.shape[0])).reshape(rows, PACK_W))
    return jnp.concatenate(blocks, axis=0) if len(blocks) > 1 else blocks[0]


def _unpack_rows(pack, shapes):
    out, row = {}, 0
    for i, shape in enumerate(shapes):
        if _keeps_rows(shape):
            out[i] = pack[row:row + shape[0]]
            row += -(-shape[0] // V7X_SUBLANES) * V7X_SUBLANES
    flat, at = pack[row:].reshape(-1), 0
    for i, shape in enumerate(shapes):
        if not _keeps_rows(shape):
            size = 1
            for s in shape:
                size *= s
            out[i] = flat[at:at + size].reshape(shape)
            at += size
    return [out[i] for i in range(len(shapes))]


def _join_chip_axis(name, parts):
    axis = parts.ndim - 1 if name in COLUMN_SPLIT or name in SMALL_SPLIT else parts.ndim - 2
    moved = jnp.moveaxis(parts, 0, axis - 1)
    shape = list(moved.shape)
    shape[axis - 1:axis + 1] = [shape[axis - 1] * shape[axis]]
    return moved.reshape(shape)


def _split_chip_axis(name, whole, shard_shape):
    axis = len(shard_shape) - 1 if name in COLUMN_SPLIT or name in SMALL_SPLIT else len(shard_shape) - 2
    shape = list(whole.shape)
    shape[axis:axis + 1] = [N_CHIPS, shard_shape[axis]]
    return jnp.moveaxis(whole.reshape(shape), axis, 0)


def kernel(x, norm_mix, norm_ffn, attn_w_qkv, attn_b_qkv, attn_sinks, attn_w_o, attn_b_o, conv_w_pw1, conv_b_pw1, conv_w_dw, conv_b_dw, conv_ln_g, conv_ln_b, conv_w_pw2, conv_b_pw2, ffn_w1, ffn_w3, ffn_w2, norm_final, loss_target, m_norm_mix, m_norm_ffn, m_attn_w_qkv, m_attn_b_qkv, m_attn_sinks, m_attn_w_o, m_attn_b_o, m_conv_w_pw1, m_conv_b_pw1, m_conv_w_dw, m_conv_b_dw, m_conv_ln_g, m_conv_ln_b, m_conv_w_pw2, m_conv_b_pw2, m_ffn_w1, m_ffn_w3, m_ffn_w2, m_norm_final, v_norm_mix, v_norm_ffn, v_attn_w_qkv, v_attn_b_qkv, v_attn_sinks, v_attn_w_o, v_attn_b_o, v_conv_w_pw1, v_conv_b_pw1, v_conv_w_dw, v_conv_b_dw, v_conv_ln_g, v_conv_ln_b, v_conv_w_pw2, v_conv_b_pw2, v_ffn_w1, v_ffn_w3, v_ffn_w2, v_norm_final):
    w = dict(zip(WEIGHT_NAMES, (norm_mix, norm_ffn, attn_w_qkv, attn_b_qkv, attn_sinks, attn_w_o, attn_b_o,
                                conv_w_pw1, conv_b_pw1, conv_w_dw, conv_b_dw, conv_ln_g, conv_ln_b, conv_w_pw2,
                                conv_b_pw2, ffn_w1, ffn_w3, ffn_w2, norm_final)))
    m = dict(zip(WEIGHT_NAMES, (m_norm_mix, m_norm_ffn, m_attn_w_qkv, m_attn_b_qkv, m_attn_sinks, m_attn_w_o,
                                m_attn_b_o, m_conv_w_pw1, m_conv_b_pw1, m_conv_w_dw, m_conv_b_dw, m_conv_ln_g,
                                m_conv_ln_b, m_conv_w_pw2, m_conv_b_pw2, m_ffn_w1, m_ffn_w3, m_ffn_w2, m_norm_final)))
    v = dict(zip(WEIGHT_NAMES, (v_norm_mix, v_norm_ffn, v_attn_w_qkv, v_attn_b_qkv, v_attn_sinks, v_attn_w_o,
                                v_attn_b_o, v_conv_w_pw1, v_conv_b_pw1, v_conv_w_dw, v_conv_b_dw, v_conv_ln_g,
                                v_conv_ln_b, v_conv_w_pw2, v_conv_b_pw2, v_ffn_w1, v_ffn_w3, v_ffn_w2, v_norm_final)))
    T, D = x.shape[1], x.shape[2]
    c_idx = lax.axis_index("c").astype(jnp.int32).reshape(1)
    chip = (2 * lax.axis_index("x") + lax.axis_index("y")).astype(jnp.int32)

    def as_rows(n, a):
        a = jnp.swapaxes(a, -1, -2) if n in STORED_TRANSPOSED else a
        return a.reshape(-1, a.shape[-1])

    def from_rows(n, rows):
        shape = w[n].shape[:-2] + w[n].shape[:-3:-1] if n in STORED_TRANSPOSED else w[n].shape
        a = rows.reshape(shape)
        return jnp.swapaxes(a, -1, -2) if n in STORED_TRANSPOSED else a

    slabs = {n: _cast_into_slot(f"cast_{n}", lax.empty((N_CHIPS,) + as_rows(n, w[n]).shape, BF16), as_rows(n, w[n]),
                                chip.reshape(1)) for n in BIG}
    first, later = BIG[:2], BIG[2:]
    qkv_parts, w_o_parts = _gather_now("gather_attn", [slabs[n] for n in first])
    send_sems, recv_sems, travelling, gather_started = _gather_start("gather_start", [slabs[n] for n in later],
                                                                     qkv_parts)
    layers = ffn_w1.shape[0]
    small_shapes = [w[n].shape for n in SMALL_SPLIT]
    small_all = _gather_small("gather_small", _pack_rows([w[n] for n in SMALL_SPLIT], F32, 8))
    per_chip = [_unpack_rows(small_all[2 * j], small_shapes) for j in range(N_CHIPS)]
    full = {}
    for i, n in enumerate(SMALL_SPLIT):
        full[n] = _join_chip_axis(n, jnp.stack([per_chip[j][i] for j in range(N_CHIPS)]))

    def other_weights(after):
        landed = _gather_wait("gather_wait", send_sems, recv_sems, travelling, after)
        gathered = dict(zip(later, _swap_fetched_with_sibling("gather_swap", landed)))
        return {"conv_w_pw1": gathered["conv_w_pw1"], "conv_w_pw2": gathered["conv_w_pw2"].reshape(-1, D),
                "ffn_w1": gathered["ffn_w1"].reshape(N_CHIPS, layers, -1, D),
                "ffn_w3": gathered["ffn_w3"].reshape(N_CHIPS, layers, -1, D),
                "ffn_w2": gathered["ffn_w2"].reshape(N_CHIPS, layers, -1, D)}

    p = {
        "norm_mix": norm_mix, "norm_ffn": norm_ffn, "norm_final": norm_final.reshape(1, D),
        "attn_w_qkv": qkv_parts.reshape(-1, D), "attn_b_qkv": attn_b_qkv,
        "attn_sinks": attn_sinks, "attn_w_o": w_o_parts.reshape(-1, D), "attn_b_o": attn_b_o,
        "conv_b_pw1": full["conv_b_pw1"], "conv_w_dw": full["conv_w_dw"][0],
        "conv_b_dw": full["conv_b_dw"], "conv_ln_g": full["conv_ln_g"], "conv_ln_b": full["conv_ln_b"],
        "conv_b_pw2": full["conv_b_pw2"], "gather_started": gather_started, "other_weights": other_weights,
    }
    swapping, in_flight = {}, []

    def reduce_begin(tag, grads):
        keys = list(grads)
        *handles, begun = _sibling_swap_start(f"sibling_swap_start_{tag}", [grads[k] for k in keys])
        swapping[tag] = (keys, handles)
        return begun

    def reduce_send(tag, after):
        keys, (swap_send, swap_recv, grads, lands) = swapping[tag]
        grads, from_sibling = _sibling_swap_wait(f"sibling_swap_wait_{tag}", swap_send, swap_recv, grads, lands, after)
        partials = [_add_sibling_half(f"add_sibling_half_{tag}{i}", gr, fs, c_idx)
                    for i, (gr, fs) in enumerate(zip(grads, from_sibling))]
        *handles, sent = _scatter_start(f"scatter_start_{tag}", partials)
        in_flight.append((tag, keys, handles, grads, from_sibling))
        return sent

    loss_part, dx, g = _local_step(x[0], loss_target[0], p, reduce_begin, reduce_send)
    for n in SMALL_WHOLE + SMALL_SPLIT:
        g[n] = g[n].reshape((-1,) + g[n].shape[-2:]) if w[n].ndim == 3 else g[n].reshape(w[n].shape[:-1] + (-1,))

    small_pack = _pack_rows([loss_part] + [g[n] for n in SMALL_WHOLE] + [g[n] for n in SMALL_SPLIT], F32, 8)
    small_send, small_recv, small_pack, small_land = _small_reduce_start("small_reduce_start", small_pack)

    place = jnp.stack([chip, c_idx[0]])
    shard_grad = {n: lax.empty(as_rows(n, w[n]).shape, F32) for n in BIG}
    for tag, keys, (send_sems, recv_sems, partials, lands), grads, from_sibling in in_flight:
        _, received = _scatter_wait(f"scatter_wait_{tag}", send_sems, recv_sems, partials, lands, small_pack)
        for i, (n, layer) in enumerate(keys):
            shard_grad[n] = _sum_chip_partials(f"sum_chip_partials_{tag}{i}", grads[i], from_sibling[i], received[i],
                                               shard_grad[n], layer, place)
    g_big = dict(zip(BIG, _join_halves([shard_grad[n] for n in BIG], [w[n].shape[0] for n in BIG])))
    big_out = {}
    for n in BIG:
        step = _adamw(f"adamw_{n}", as_rows(n, w[n]), g_big[n], as_rows(n, m[n]), as_rows(n, v[n]))
        big_out[n] = [from_rows(n, a) for a in step]

    small_whole_shapes = [w[n].shape for n in SMALL_WHOLE]
    small_full_shapes = [g[n].shape for n in SMALL_SPLIT]
    small_pack, small_land = _small_reduce_wait("small_reduce_wait", small_send, small_recv, small_pack, small_land,
                                                big_out[BIG[-1]][1])
    reduced = _sum_device_slots("small_reduce_sum", small_pack, small_land, (2 * chip + c_idx[0]).reshape(1))
    pieces = _unpack_rows(reduced, [(1,)] + small_whole_shapes + small_full_shapes)
    loss = pieces[0].reshape(())
    g_small = dict(zip(SMALL_WHOLE, pieces[1:1 + len(SMALL_WHOLE)]))
    for n, whole in zip(SMALL_SPLIT, pieces[1 + len(SMALL_WHOLE):]):
        parts = _split_chip_axis(n, whole, w[n].shape)
        g_small[n] = lax.dynamic_index_in_dim(parts, chip, axis=0, keepdims=False)
    small = SMALL_WHOLE + SMALL_SPLIT
    _, d_small, m_small, v_small = _adamw(
        "adamw_small", _pack_rows([w[n] for n in small], F32, 8), _pack_rows([g_small[n] for n in small], F32, 8),
        _pack_rows([m[n] for n in small], F32, 8), _pack_rows([v[n] for n in small], F32, 8))

    outs = {}
    for slot, (tag, small_pack) in enumerate((("g", None), ("d", d_small), ("m", m_small), ("v", v_small))):
        vals = {n: big_out[n][slot] for n in BIG}
        if small_pack is None:
            vals.update(g_small)
        else:
            vals.update(zip(small, _unpack_rows(small_pack, [w[n].shape for n in small])))
        outs[tag] = vals
    return (loss, dx.reshape(1, T, D), *[outs["g"][n] for n in WEIGHT_NAMES], *[outs["d"][n] for n in WEIGHT_NAMES],
            *[outs["m"][n] for n in WEIGHT_NAMES], *[outs["v"][n] for n in WEIGHT_NAMES])
```

```python
import functools

import jax
import jax.numpy as jnp
from jax import lax
from jax.experimental import pallas as pl
from jax.experimental.pallas import tpu as pltpu

F32 = jnp.float32
BF16 = jnp.bfloat16
SDS = jax.ShapeDtypeStruct
MESH = pl.DeviceIdType.MESH

HEAD_DIM = 64
N_Q_HEADS = 16
N_KV_HEADS = 2
Q_PER_KV = N_Q_HEADS // N_KV_HEADS
ATTN_BLOCK = 128
ROPE_THETA = 10000.0
CONV_WIDTH = 31
CONV_HALO = 32
CONV_FIRST_TAP = CONV_HALO - CONV_WIDTH + 1
CONV_ROW_CHUNK = 64
CONV_LANE_CHUNK = 256
CONV_GRAD_UNROLL = 8
RMS_EPS = 1e-5
LN_EPS = 1e-5
ADAM_LR = 0.001
ADAM_B1 = 0.9
ADAM_B2 = 0.999
ADAM_EPS = 1e-08
ADAM_WD = 0.01
ADAM_STEP = 10

V7X_LANES = 128
V7X_SUBLANES = 8
V7X_VMEM_LIMIT_BYTES = 56 * 1024 * 1024

N_CHIPS = 4
N_DEV = 8
PACK_W = 1024

MASK_VALUE = -1e30


def _params(*semantics):
    return pltpu.CompilerParams(dimension_semantics=semantics, vmem_limit_bytes=V7X_VMEM_LIMIT_BYTES)


def _rows(tm, width):
    return pl.BlockSpec((tm, width), lambda i: (i, 0))


def _whole(shape):
    return pl.BlockSpec(shape, lambda *_: (0,) * len(shape))


def _rms_rstd(h):
    return lax.rsqrt(jnp.mean(h * h, axis=-1, keepdims=True) + RMS_EPS)


def _silu_and_grad(z):
    sg = jax.nn.sigmoid(z)
    return z * sg, sg * (1.0 + z * (1.0 - sg))


def _swap_rope_halves(t):
    w = t.shape[1]
    half = HEAD_DIM // 2
    lane = lax.broadcasted_iota(jnp.int32, t.shape, 1)
    upper = pltpu.roll(t, w - half, 1)
    lower = pltpu.roll(t, half, 1)
    return jnp.where(lane % HEAD_DIM < half, upper, lower)


def _rope(t, cos_ref, sin_ref):
    reps = t.shape[1] // V7X_LANES
    c = jnp.tile(cos_ref[...], (1, reps))
    s = jnp.tile(sin_ref[...], (1, reps))
    return t * c + _swap_rope_halves(t) * s


def _rope_transposed(dt, cos_ref, sin_ref):
    reps = dt.shape[1] // V7X_LANES
    c = jnp.tile(cos_ref[...], (1, reps))
    s = jnp.tile(sin_ref[...], (1, reps))
    return dt * c + _swap_rope_halves(dt * s)


def _rope_tables(seq_len):
    pos = jnp.arange(seq_len, dtype=F32)
    inv_freq = ROPE_THETA ** (-jnp.arange(0, HEAD_DIM, 2, dtype=F32) / HEAD_DIM)
    ang = pos[:, None] * jnp.tile(inv_freq, 2 * V7X_LANES // HEAD_DIM)[None, :]
    upper_half = jnp.arange(V7X_LANES) % HEAD_DIM >= HEAD_DIM // 2
    return jnp.cos(ang), jnp.where(upper_half[None, :], jnp.sin(ang), -jnp.sin(ang))


def _qkv_proj(h, g, w, b, cos, sin, after):
    T, D = h.shape
    N = w.shape[0]
    tm = min(512, T)
    rope_w = N - N_KV_HEADS * HEAD_DIM

    def body(h_ref, g_ref, w_ref, b_ref, cos_ref, sin_ref, _, y_ref, o_ref):
        hh = h_ref[...]
        y = (hh * _rms_rstd(hh) * g_ref[...]).astype(BF16)
        y_ref[...] = y
        acc = _dot_nt(y, w_ref[...]) + b_ref[...]
        o_ref[:, :rope_w] = _rope(acc[:, :rope_w], cos_ref, sin_ref).astype(BF16)
        o_ref[:, rope_w:] = acc[:, rope_w:].astype(BF16)

    return pl.pallas_call(
        body, name="qkv_proj", grid=(T // tm,),
        in_specs=[_rows(tm, D), _whole((1, D)), _whole((N, D)), _whole((1, N)),
                  _rows(tm, V7X_LANES), _rows(tm, V7X_LANES), pl.BlockSpec(memory_space=pl.ANY)],
        out_specs=[_rows(tm, D), _rows(tm, N)],
        out_shape=[SDS((T, D), BF16), SDS((T, N), BF16)],
        compiler_params=_params("parallel"),
    )(h, g, w, b, cos, sin, after)


def _pw1_proj(h, g, w, b):
    T, D = h.shape
    n = w.shape[2]
    N = N_CHIPS * n
    tm = min(512, T)

    def body(h_ref, g_ref, w_ref, b_ref, y_ref, o_ref):
        hh = h_ref[...]
        y = (hh * _rms_rstd(hh) * g_ref[...]).astype(BF16)
        y_ref[...] = y
        for j in range(N_CHIPS):
            cols = slice(j * n, (j + 1) * n)
            o_ref[:, cols] = jnp.dot(y, w_ref[j], preferred_element_type=F32) + b_ref[:, cols]

    return pl.pallas_call(
        body, name="pw1_proj", grid=(T // tm,),
        in_specs=[_rows(tm, D), _whole((1, D)), _whole((N_CHIPS, D, n)), _whole((1, N))],
        out_specs=[_rows(tm, D), _rows(tm, N)],
        out_shape=[SDS((T, D), BF16), SDS((T, N), F32)],
        compiler_params=_params("parallel"),
    )(h, g, w, b)


PAIRS_PER_KV = Q_PER_KV // 2


def _upper_lanes(shape):
    return lax.broadcasted_iota(jnp.int32, shape, 1) >= HEAD_DIM


def _swap_lane_halves(t):
    return pltpu.roll(t.astype(F32), HEAD_DIM, 1).astype(t.dtype)


def _kv_operands(g, t):
    swapped = _swap_lane_halves(t)
    in_lower, in_upper = (t, swapped) if g == 0 else (swapped, t)
    upper = _upper_lanes(t.shape)
    zero = jnp.zeros_like(t)
    return jnp.where(upper, zero, in_lower), jnp.where(upper, in_upper, zero)


def _group_heads(g):
    pairs = range(g * PAIRS_PER_KV, (g + 1) * PAIRS_PER_KV)
    return [2 * hp for hp in pairs] + [2 * hp + 1 for hp in pairs]


def _all_heads():
    return [h for g in range(N_KV_HEADS) for h in _group_heads(g)]


def _pair_rows(ref, g):
    pairs = range(g * PAIRS_PER_KV, (g + 1) * PAIRS_PER_KV)
    return jnp.concatenate([ref[:, hp * 2 * HEAD_DIM:(hp + 1) * 2 * HEAD_DIM] for hp in pairs], axis=0)


def _from_previous_block(rows):
    row = lax.broadcasted_iota(jnp.int32, (ATTN_BLOCK, ATTN_BLOCK), 0)
    col = lax.broadcasted_iota(jnp.int32, (ATTN_BLOCK, ATTN_BLOCK), 1)
    return jnp.concatenate([col > row] * (rows // ATTN_BLOCK), axis=0)


def _folded_probs(n, q_groups, k_prev_groups, k_cur_groups, sink_ref, prev_part):
    def scores(q, k):
        return lax.dot_general(q, k, (((1,), (1,)), ((), ())), preferred_element_type=F32)

    s_prev = jnp.concatenate([scores(q, k[i]) for q, k in zip(q_groups, k_prev_groups) for i in range(2)], axis=0)
    s_cur = jnp.concatenate([scores(q, k[i]) for q, k in zip(q_groups, k_cur_groups) for i in range(2)], axis=0)
    s_prev = jnp.where(n > 0, s_prev, MASK_VALUE * (HEAD_DIM ** 0.5))
    s = jnp.where(prev_part, s_prev, s_cur) * (HEAD_DIM ** -0.5)
    heads = [h for g in range(N_KV_HEADS) for h in _group_heads(g)]
    sink = jnp.concatenate([jnp.broadcast_to(sink_ref[0:1, h:h + 1], (ATTN_BLOCK, 1)) for h in heads], axis=0)
    m = jnp.maximum(jnp.max(s, axis=1, keepdims=True), sink)
    p = jnp.exp(s - m)
    e_sink = jnp.exp(sink - m)
    inv = 1.0 / (jnp.sum(p, axis=1, keepdims=True) + e_sink)
    return p * inv, e_sink * inv


def _split_folded(t, prev_part):
    tb = t.astype(BF16)
    zero = jnp.zeros_like(tb)
    return jnp.where(prev_part, tb, zero), jnp.where(prev_part, zero, tb)


def _attn_specs(T):
    nb = T // ATTN_BLOCK
    kcol = N_Q_HEADS * HEAD_DIM // V7X_LANES
    cur = lambda n: jnp.minimum(n, nb - 1)
    prev = lambda n: jnp.maximum(jnp.minimum(n, nb - 1) - 1, 0)
    q_spec = pl.BlockSpec((ATTN_BLOCK, N_Q_HEADS * HEAD_DIM), lambda n: (cur(n), 0))
    kc_spec = pl.BlockSpec((ATTN_BLOCK, V7X_LANES), lambda n: (cur(n), kcol))
    kp_spec = pl.BlockSpec((ATTN_BLOCK, V7X_LANES), lambda n: (prev(n), kcol))
    vc_spec = pl.BlockSpec((ATTN_BLOCK, V7X_LANES), lambda n: (cur(n), kcol + 1))
    vp_spec = pl.BlockSpec((ATTN_BLOCK, V7X_LANES), lambda n: (prev(n), kcol + 1))
    return q_spec, kc_spec, kp_spec, vc_spec, vp_spec


def _attn_fwd(qkv, sinks):
    T = qkv.shape[0]
    nb = T // ATTN_BLOCK
    qw = N_Q_HEADS * HEAD_DIM
    all_rows = N_Q_HEADS * ATTN_BLOCK

    def body(q_ref, kc_ref, kp_ref, vc_ref, vp_ref, sink_ref, o_ref, probs_ref, psink_ref):
        n = pl.program_id(0)
        prev_part = _from_previous_block(all_rows)
        half = PAIRS_PER_KV * ATTN_BLOCK
        groups = range(N_KV_HEADS)
        probs, p_sink = _folded_probs(n, [_pair_rows(q_ref, g) for g in groups],
                                      [_kv_operands(g, kp_ref[...]) for g in groups],
                                      [_kv_operands(g, kc_ref[...]) for g in groups], sink_ref, prev_part)
        probs_ref[...] = probs.astype(BF16)
        lane = lax.broadcasted_iota(jnp.int32, (ATTN_BLOCK, V7X_LANES), 1)
        sink_tile = jnp.zeros((ATTN_BLOCK, V7X_LANES), F32)
        for i, h in enumerate(_all_heads()):
            sink_tile = jnp.where(lane == h, p_sink[i * ATTN_BLOCK:(i + 1) * ATTN_BLOCK], sink_tile)
        psink_ref[...] = sink_tile
        p_prev, p_cur = _split_folded(probs, prev_part)
        for g in groups:
            v_prev, v_cur = _kv_operands(g, vp_ref[...]), _kv_operands(g, vc_ref[...])
            even, odd = slice(2 * g * half, (2 * g + 1) * half), slice((2 * g + 1) * half, (2 * g + 2) * half)
            o = (jnp.dot(p_prev[even], v_prev[0], preferred_element_type=F32)
                 + jnp.dot(p_cur[even], v_cur[0], preferred_element_type=F32)
                 + jnp.dot(p_prev[odd], v_prev[1], preferred_element_type=F32)
                 + jnp.dot(p_cur[odd], v_cur[1], preferred_element_type=F32))
            for i in range(PAIRS_PER_KV):
                hp = g * PAIRS_PER_KV + i
                o_ref[:, hp * 2 * HEAD_DIM:(hp + 1) * 2 * HEAD_DIM] = (
                    o[i * ATTN_BLOCK:(i + 1) * ATTN_BLOCK].astype(BF16))

    return pl.pallas_call(
        body, name="attn_fwd", grid=(nb,),
        in_specs=[*_attn_specs(T), _whole((1, N_Q_HEADS))],
        out_specs=[_rows(ATTN_BLOCK, qw), pl.BlockSpec((None, all_rows, ATTN_BLOCK), lambda n: (n, 0, 0)),
                   _rows(ATTN_BLOCK, V7X_LANES)],
        out_shape=[SDS((T, qw), BF16), SDS((nb, all_rows, ATTN_BLOCK), BF16), SDS((T, V7X_LANES), F32)],
        compiler_params=_params("parallel"),
    )(qkv, qkv, qkv, qkv, qkv, sinks)


def _mm_res(name, a, w, b, res, g):
    T, K = a.shape
    D = w.shape[1]
    tm = min(512, T)

    def body(a_ref, w_ref, b_ref, r_ref, g_ref, o_ref, f_ref):
        h = jnp.dot(a_ref[...], w_ref[...], preferred_element_type=F32) + b_ref[...] + r_ref[...]
        o_ref[...] = h
        f_ref[...] = (h * _rms_rstd(h) * g_ref[...]).astype(BF16)

    return pl.pallas_call(
        body, name=name, grid=(T // tm,),
        in_specs=[_rows(tm, K), _whole((K, D)), _whole((1, D)), _rows(tm, D), _whole((1, D))],
        out_specs=[_rows(tm, D), _rows(tm, D)],
        out_shape=[SDS((T, D), F32), SDS((T, D), BF16)],
        compiler_params=_params("parallel"),
    )(a, w, b, res, g)


def _ffn_down(name, s, w2, layer, res):
    _, T, n = s.shape
    D = w2.shape[3]
    tm = min(512, T)

    def body(s_ref, w_ref, r_ref, o_ref):
        acc = r_ref[...]
        for j in range(N_CHIPS):
            acc = acc + jnp.dot(s_ref[j], w_ref[j], preferred_element_type=F32)
        o_ref[...] = acc

    return pl.pallas_call(
        body, name=name, grid=(T // tm,),
        in_specs=[pl.BlockSpec((N_CHIPS, tm, n), lambda i: (0, i, 0)),
                  pl.BlockSpec((N_CHIPS, None, n, D), lambda i: (0, layer, 0, 0)), _rows(tm, D)],
        out_specs=_rows(tm, D),
        out_shape=SDS((T, D), F32),
        compiler_params=_params("parallel"),
    )(s, w2, res)


def _ffn_up(name, f, w1, w3, layer):
    T, D = f.shape
    n = w1.shape[2]
    tm = min(1024, T)

    def body(f_ref, w1_ref, w3_ref, act_ref, gg_ref, s_ref):
        ff = f_ref[...]
        g1 = _dot_nt(ff, w1_ref[...])
        g3 = _dot_nt(ff, w3_ref[...])
        act, dact = _silu_and_grad(g1)
        act_ref[...] = act.astype(BF16)
        gg_ref[...] = (g3 * dact).astype(BF16)
        s_ref[...] = (act * g3).astype(BF16)

    slab = pl.BlockSpec((None, tm, n), lambda j, i: (j, i, 0))
    wslab = pl.BlockSpec((None, None, n, D), lambda j, i: (j, layer, 0, 0))
    hidden = SDS((N_CHIPS, T, n), BF16)
    return pl.pallas_call(
        body, name=name, grid=(N_CHIPS, T // tm),
        in_specs=[pl.BlockSpec((tm, D), lambda j, i: (i, 0)), wslab, wslab],
        out_specs=[slab, slab, slab],
        out_shape=[hidden, hidden, hidden],
        compiler_params=_params("parallel", "parallel"),
    )(f, w1, w3)


def _glu(a, d):
    return a[:, :d] * jax.nn.sigmoid(a[:, d:])


def _conv_tile(T):
    return min(256, T)


def _fill_shifted(sh_ref, tc):
    n = tc + CONV_HALO - V7X_SUBLANES
    for r in range(1, V7X_SUBLANES):
        sh_ref[r, 0:n, :] = sh_ref[0, pl.ds(r, n), :]


def _depthwise_taps(sh_ref, w_ref, offsets, bias_ref, out_ref, tc):
    D = out_ref.shape[1]

    def chunk(i, carry):
        t0 = pl.multiple_of(i * CONV_ROW_CHUNK, CONV_ROW_CHUNK)
        for cb in range(D // CONV_LANE_CHUNK):
            cs = slice(cb * CONV_LANE_CHUNK, (cb + 1) * CONV_LANE_CHUNK)
            acc = jnp.zeros((CONV_ROW_CHUNK, CONV_LANE_CHUNK), F32)
            for r in range(V7X_SUBLANES):
                taps = [(j, o // V7X_SUBLANES) for j, o in enumerate(offsets) if o % V7X_SUBLANES == r]
                if not taps:
                    continue
                span = CONV_ROW_CHUNK + V7X_SUBLANES * max(q for _, q in taps)
                rows = sh_ref[r, pl.ds(t0, span), cs]
                for j, q in taps:
                    acc = acc + rows[V7X_SUBLANES * q:V7X_SUBLANES * q + CONV_ROW_CHUNK] * w_ref[j:j + 1, cs]
            if bias_ref is not None:
                acc = acc + bias_ref[:, cs]
            out_ref[pl.ds(t0, CONV_ROW_CHUNK), cs] = acc
        return carry

    lax.fori_loop(0, tc // CONV_ROW_CHUNK, chunk, 0)


def _depthwise_tap_grads(dy_sh, x_sh, offsets, dw_ref, tc):
    D = dw_ref.shape[1]
    for cb in range(D // V7X_LANES):
        cs = slice(cb * V7X_LANES, (cb + 1) * V7X_LANES)

        def row_tiles(i, accs, cs=cs):
            for k in range(CONV_GRAD_UNROLL):
                t0 = pl.multiple_of(i * (CONV_GRAD_UNROLL * V7X_SUBLANES), V7X_SUBLANES) + k * V7X_SUBLANES
                d = dy_sh[0, pl.ds(t0, V7X_SUBLANES), cs]
                accs = tuple(
                    acc + d * x_sh[o % V7X_SUBLANES, pl.ds(t0 + o // V7X_SUBLANES * V7X_SUBLANES, V7X_SUBLANES), cs]
                    for acc, o in zip(accs, offsets))
            return accs

        zero = jnp.zeros((V7X_SUBLANES, V7X_LANES), F32)
        accs = lax.fori_loop(0, tc // (CONV_GRAD_UNROLL * V7X_SUBLANES), row_tiles, tuple(zero for _ in offsets))
        for j, acc in enumerate(accs):
            dw_ref[j:j + 1, cs] += jnp.sum(acc, axis=0, keepdims=True)


def _conv_fwd(a, w_dw, b_dw, ln_g, ln_b, w_pw2, b_pw2, res, g_next):
    T = a.shape[0]
    D = a.shape[1] // 2
    tc = _conv_tile(T)
    per = tc // CONV_HALO

    def body(a_ref, ah_ref, w_ref, bdw_ref, lg_ref, lb_ref, wp_ref, bp_ref, r_ref, g_ref,
             c_ref, act_ref, h_ref, f_ref, u_sh):
        i = pl.program_id(0)
        u_sh[0, 0:CONV_HALO, :] = jnp.where(i > 0, _glu(ah_ref[...], D), 0.0)
        u_sh[0, CONV_HALO:, :] = _glu(a_ref[...], D)
        _fill_shifted(u_sh, tc)
        _depthwise_taps(u_sh, w_ref, [CONV_FIRST_TAP + j for j in range(CONV_WIDTH)], bdw_ref, c_ref, tc)
        c = c_ref[...]
        xc = c - jnp.mean(c, axis=-1, keepdims=True)
        z = xc * lax.rsqrt(jnp.mean(xc * xc, axis=-1, keepdims=True) + LN_EPS)
        l = z * lg_ref[...] + lb_ref[...]
        act = (l * jax.nn.sigmoid(l)).astype(BF16)
        act_ref[...] = act
        h = jnp.dot(act, wp_ref[...], preferred_element_type=F32) + bp_ref[...] + r_ref[...]
        h_ref[...] = h
        f_ref[...] = (h * _rms_rstd(h) * g_ref[...]).astype(BF16)

    return pl.pallas_call(
        body, name="conv_fwd", grid=(T // tc,),
        in_specs=[_rows(tc, 2 * D),
                  pl.BlockSpec((CONV_HALO, 2 * D), lambda i: (jnp.maximum(i * per - 1, 0), 0)),
                  _whole((CONV_WIDTH, D)), _whole((1, D)), _whole((1, D)), _whole((1, D)),
                  _whole((D, D)), _whole((1, D)), _rows(tc, D), _whole((1, D))],
        out_specs=[_rows(tc, D), _rows(tc, D), _rows(tc, D), _rows(tc, D)],
        out_shape=[SDS((T, D), F32), SDS((T, D), BF16), SDS((T, D), F32), SDS((T, D), BF16)],
        scratch_shapes=[pltpu.VMEM((V7X_SUBLANES, tc + CONV_HALO, D), F32)],
        compiler_params=_params("parallel"),
    )(a, a, w_dw, b_dw, ln_g, ln_b, w_pw2, b_pw2, res, g_next)


def _ffn_down_loss(name, s, w2, layer, res, g, target):
    _, T, n = s.shape
    D = w2.shape[3]
    tm = min(512, T)

    def body(s_ref, w_ref, r_ref, g_ref, t_ref, dh_ref, loss_ref, dg_ref):
        @pl.when(pl.program_id(0) == 0)
        def _():
            loss_ref[...] = jnp.zeros_like(loss_ref)
            dg_ref[...] = jnp.zeros_like(dg_ref)

        hh = r_ref[...]
        for j in range(N_CHIPS):
            hh = hh + jnp.dot(s_ref[j], w_ref[j], preferred_element_type=F32)
        r = _rms_rstd(hh)
        g = g_ref[...]
        d = hh * r * g - t_ref[...]
        loss_ref[...] += 0.5 * jnp.sum(jnp.mean(d * d, axis=-1, keepdims=True), axis=0, keepdims=True)
        dout = d * (1.0 / D)
        dg_ref[...] += jnp.sum(dout * (hh * r), axis=0, keepdims=True)
        dxh = dout * g
        dh_ref[...] = r * dxh - hh * (r * r * r) * jnp.mean(dxh * hh, axis=-1, keepdims=True)

    return pl.pallas_call(
        body, name=name, grid=(T // tm,),
        in_specs=[pl.BlockSpec((N_CHIPS, tm, n), lambda i: (0, i, 0)),
                  pl.BlockSpec((N_CHIPS, None, n, D), lambda i: (0, layer, 0, 0)), _rows(tm, D),
                  _whole((1, D)), _rows(tm, D)],
        out_specs=[_rows(tm, D), _whole((1, 1)), _whole((1, D))],
        out_shape=[SDS((T, D), F32), SDS((1, 1), F32), SDS((1, D), F32)],
        compiler_params=_params("arbitrary"),
    )(s, w2, res, g, target)


def _ffn_bwd_down(name, dh, w2, layer, act, gate_grad, s, after=None):
    T, D = dh.shape
    n = w2.shape[2]
    tm = min(256, T)

    def body(dh_ref, w2_ref, act_ref, gg_ref, s_ref, *rest):
        dg1_ref, dg3_ref, dw_ref = rest[-3:]

        @pl.when(pl.program_id(0) == 0)
        def _():
            dw_ref[...] = jnp.zeros_like(dw_ref)

        dhb = dh_ref[...].astype(BF16)
        for j in range(N_CHIPS):
            ds = _dot_nt(dhb, w2_ref[j])
            dg1_ref[j] = (ds * gg_ref[j].astype(F32)).astype(BF16)
            dg3_ref[j] = (ds * act_ref[j].astype(F32)).astype(BF16)
            dw_ref[j] += _dot_tn(s_ref[j], dhb)

    slabs = pl.BlockSpec((N_CHIPS, tm, n), lambda i: (0, i, 0))
    hidden = SDS((N_CHIPS, T, n), BF16)
    return pl.pallas_call(
        body, name=name, grid=(T // tm,),
        in_specs=[_rows(tm, D),
                  pl.BlockSpec((N_CHIPS, None, n, D), lambda i: (0, layer, 0, 0), pipeline_mode=pl.Buffered(1)),
                  slabs, slabs, slabs] + ([] if after is None else [pl.BlockSpec(memory_space=pl.ANY)]),
        out_specs=[slabs, slabs, _whole((N_CHIPS, n, D))],
        out_shape=[hidden, hidden, SDS((N_CHIPS, n, D), F32)],
        compiler_params=_params("arbitrary"),
    )(dh, w2, act, gate_grad, s, *([] if after is None else [after]))


def _dot_tn(a, b):
    return lax.dot_general(a.astype(BF16), b.astype(BF16), (((0,), (0,)), ((), ())), preferred_element_type=F32)


def _dot_nt(a, b):
    return lax.dot_general(a.astype(BF16), b, (((1,), (1,)), ((), ())), preferred_element_type=F32)


def _mm_tn(name, a, b, col_chunks=1, after=None):
    a_slabs, b_slabs = a.ndim == 3, b.ndim == 3
    T = a.shape[-2]
    tt = min(1024, T)
    ka, nb = a.shape[-1], b.shape[-1]
    if a_slabs or b_slabs:
        out_dims = (N_CHIPS, ka, nb)
    elif col_chunks > 1:
        out_dims = (col_chunks, ka, nb // col_chunks)
    else:
        out_dims = (ka, nb)

    def body(a_ref, b_ref, *rest):
        o_ref = rest[-1]

        @pl.when(pl.program_id(0) == 0)
        def _():
            o_ref[...] = jnp.zeros_like(o_ref)

        if a_slabs:
            bb = b_ref[...].astype(BF16)
            for j in range(N_CHIPS):
                o_ref[j] += _dot_tn(a_ref[j], bb)
        elif b_slabs:
            aa = a_ref[...].astype(BF16)
            for j in range(N_CHIPS):
                o_ref[j] += _dot_tn(aa, b_ref[j])
        elif col_chunks > 1:
            aa = a_ref[...].astype(BF16)
            w = nb // col_chunks
            for j in range(col_chunks):
                o_ref[j] += _dot_tn(aa, b_ref[:, j * w:(j + 1) * w])
        else:
            o_ref[...] += _dot_tn(a_ref[...], b_ref[...])

    def spec(arr, slabs):
        if slabs:
            return pl.BlockSpec((N_CHIPS, tt, arr.shape[-1]), lambda t: (0, t, 0))
        return _rows(tt, arr.shape[-1])

    return pl.pallas_call(
        body, name=name, grid=(T // tt,),
        in_specs=[spec(a, a_slabs), spec(b, b_slabs)] + ([] if after is None else [pl.BlockSpec(memory_space=pl.ANY)]),
        out_specs=_whole(out_dims),
        out_shape=SDS(out_dims, F32),
        compiler_params=_params("arbitrary"),
    )(a, b, *([] if after is None else [after]))


def _mm_nt_normbwd(name, pairs, h, g, dh, after):
    T, D = h.shape
    tm = min(512, T)
    n_pairs = len(pairs)
    kinds = ["slabs" if dy.ndim == 3 else ("quarters" if w.ndim == 3 else "plain") for dy, w, _ in pairs]

    def body(*refs):
        dy_refs = refs[:n_pairs]
        w_refs = refs[n_pairs:2 * n_pairs]
        h_ref, g_ref, dh_ref, _, o_ref, dg_ref, cs_ref = refs[2 * n_pairs:]

        @pl.when(pl.program_id(0) == 0)
        def _():
            dg_ref[...] = jnp.zeros_like(dg_ref)
            cs_ref[...] = jnp.zeros_like(cs_ref)

        df = jnp.zeros((tm, D), F32)
        for dy_ref, w_ref, kd in zip(dy_refs, w_refs, kinds):
            if kd == "slabs":
                for j in range(N_CHIPS):
                    df = df + jnp.dot(dy_ref[j], w_ref[j], preferred_element_type=F32)
            elif kd == "quarters":
                n = w_ref.shape[2]
                for j in range(N_CHIPS):
                    df = df + _dot_nt(dy_ref[:, j * n:(j + 1) * n], w_ref[j])
            else:
                df = df + jnp.dot(dy_ref[...], w_ref[...], preferred_element_type=F32)
        hh = h_ref[...]
        r = _rms_rstd(hh)
        dg_ref[...] += jnp.sum(df * (hh * r), axis=0, keepdims=True)
        dxh = df * g_ref[...]
        out = dh_ref[...] + (r * dxh - hh * (r * r * r) * jnp.mean(dxh * hh, axis=-1, keepdims=True))
        o_ref[...] = out
        cs_ref[...] += jnp.sum(out, axis=0, keepdims=True)

    dy_specs, w_specs = [], []
    for (dy, w, layer), kd in zip(pairs, kinds):
        if kd == "slabs":
            dy_specs.append(pl.BlockSpec((N_CHIPS, tm, dy.shape[2]), lambda i: (0, i, 0)))
            w_specs.append(pl.BlockSpec((N_CHIPS, None, w.shape[2], D),
                                        functools.partial(lambda i, layer: (0, layer, 0, 0), layer=layer),
                                        pipeline_mode=pl.Buffered(1)))
        else:
            dy_specs.append(_rows(tm, dy.shape[1]))
            w_specs.append(_whole(w.shape))

    return pl.pallas_call(
        body, name=name, grid=(T // tm,),
        in_specs=[*dy_specs, *w_specs, _rows(tm, D), _whole((1, D)), _rows(tm, D), pl.BlockSpec(memory_space=pl.ANY)],
        out_specs=[_rows(tm, D), _whole((1, D)), _whole((1, D))],
        out_shape=[SDS((T, D), F32), SDS((1, D), F32), SDS((1, D), F32)],
        compiler_params=_params("arbitrary"),
    )(*[dy for dy, _, _ in pairs], *[w for _, w, _ in pairs], h, g, dh, after)


def _mm_nt(name, dy, w, out_dtype):
    T, N = dy.shape
    K = w.shape[0]
    tm = min(512, T)

    def body(dy_ref, w_ref, o_ref):
        o_ref[...] = lax.dot_general(dy_ref[...].astype(BF16), w_ref[...], (((1,), (1,)), ((), ())),
                                     preferred_element_type=F32).astype(out_dtype)

    return pl.pallas_call(
        body, name=name, grid=(T // tm,),
        in_specs=[_rows(tm, N), _whole((K, N))],
        out_specs=_rows(tm, K),
        out_shape=SDS((T, K), out_dtype),
        compiler_params=_params("parallel"),
    )(dy, w)


def _conv_bwd(dh, w_pw2, c, a, w_dw, ln_g, ln_b):
    T, D = c.shape
    tc = _conv_tile(T)
    per = tc // CONV_HALO
    n_tiles = T // tc
    last_halo = T // CONV_HALO - 1

    def ln_bwd(dact_v, c_v, lg, lb):
        xc = c_v - jnp.mean(c_v, axis=-1, keepdims=True)
        rstd = lax.rsqrt(jnp.mean(xc * xc, axis=-1, keepdims=True) + LN_EPS)
        z = xc * rstd
        _, dsilu = _silu_and_grad(z * lg + lb)
        dl = dact_v * dsilu
        dz = dl * lg
        dc = rstd * (dz - jnp.mean(dz, axis=-1, keepdims=True) - z * jnp.mean(dz * z, axis=-1, keepdims=True))
        return dc, dl, z

    def body(dh_ref, dhn_ref, wp_ref, c_ref, cn_ref, a_ref, ah_ref, w_ref, lg_ref, lb_ref,
             da_ref, dlg_ref, dlb_ref, dbdw_ref, dwdw_ref, dbpw1_ref, dc_sh, u_sh, du_scr):
        i = pl.program_id(0)

        @pl.when(i == 0)
        def _():
            for ref in (dlg_ref, dlb_ref, dbdw_ref, dwdw_ref, dbpw1_ref):
                ref[...] = jnp.zeros_like(ref)

        lg, lb = lg_ref[...], lb_ref[...]
        dh_rows = jnp.concatenate([dh_ref[...].astype(BF16), dhn_ref[...].astype(BF16)], axis=0)
        dact = _dot_nt(dh_rows, wp_ref[...])
        dc, dl, z = ln_bwd(dact[:tc], c_ref[...], lg, lb)
        dlg_ref[...] += jnp.sum(dl * z, axis=0, keepdims=True)
        dlb_ref[...] += jnp.sum(dl, axis=0, keepdims=True)
        dbdw_ref[...] += jnp.sum(dc, axis=0, keepdims=True)
        dcn, _, _ = ln_bwd(dact[tc:], cn_ref[...], lg, lb)
        dc_sh[0, 0:tc, :] = dc
        dc_sh[0, tc:, :] = jnp.where(i < n_tiles - 1, dcn, 0.0)
        _fill_shifted(dc_sh, tc)

        a_v = a_ref[...]
        a1 = a_v[:, :D]
        sg = jax.nn.sigmoid(a_v[:, D:])
        u_sh[0, 0:CONV_HALO, :] = jnp.where(i > 0, _glu(ah_ref[...], D), 0.0)
        u_sh[0, CONV_HALO:, :] = a1 * sg
        _fill_shifted(u_sh, tc)

        _depthwise_taps(dc_sh, w_ref, [CONV_WIDTH - 1 - j for j in range(CONV_WIDTH)], None, du_scr, tc)
        _depthwise_tap_grads(dc_sh, u_sh, [CONV_FIRST_TAP + j for j in range(CONV_WIDTH)], dwdw_ref, tc)

        du = du_scr[...]
        da1 = du * sg
        da2 = du * a1 * sg * (1.0 - sg)
        da_ref[:, :D] = da1.astype(BF16)
        da_ref[:, D:] = da2.astype(BF16)
        dbpw1_ref[:, :D] += jnp.sum(da1, axis=0, keepdims=True)
        dbpw1_ref[:, D:] += jnp.sum(da2, axis=0, keepdims=True)

    nxt = lambda i: (jnp.minimum((i + 1) * per, last_halo), 0)
    return pl.pallas_call(
        body, name="conv_bwd", grid=(n_tiles,),
        in_specs=[_rows(tc, D), pl.BlockSpec((CONV_HALO, D), nxt), _whole((D, D)),
                  _rows(tc, D), pl.BlockSpec((CONV_HALO, D), nxt),
                  _rows(tc, 2 * D),
                  pl.BlockSpec((CONV_HALO, 2 * D), lambda i: (jnp.maximum(i * per - 1, 0), 0)),
                  _whole((CONV_WIDTH, D)), _whole((1, D)), _whole((1, D))],
        out_specs=[_rows(tc, 2 * D), _whole((1, D)), _whole((1, D)), _whole((1, D)),
                   _whole((CONV_HALO, D)), _whole((1, 2 * D))],
        out_shape=[SDS((T, 2 * D), BF16), SDS((1, D), F32), SDS((1, D), F32), SDS((1, D), F32),
                   SDS((CONV_HALO, D), F32), SDS((1, 2 * D), F32)],
        scratch_shapes=[pltpu.VMEM((V7X_SUBLANES, tc + CONV_HALO, D), F32),
                        pltpu.VMEM((V7X_SUBLANES, tc + CONV_HALO, D), F32), pltpu.VMEM((tc, D), F32)],
        compiler_params=_params("arbitrary"),
    )(dh, dh, w_pw2, c, c, a, a, w_dw, ln_g, ln_b)


def _attn_bwd(qkv, dao, cos, sin, probs_saved, psink_saved):
    T = qkv.shape[0]
    nb = T // ATTN_BLOCK
    qw = N_Q_HEADS * HEAD_DIM
    kw = N_KV_HEADS * HEAD_DIM

    def body(q_ref, kc_ref, kp_ref, vc_ref, vp_ref, do_ref, cos_ref, sin_ref, cosp_ref, sinp_ref, probs_ref, psink_ref,
             dq_ref, dkv_ref, dsink_ref, dbq_ref, dbkv_ref, carry, prev_scr, cur_scr, dq_scr):
        n = pl.program_id(0)

        @pl.when(n == 0)
        def _():
            for ref in (dsink_ref, dbq_ref, dbkv_ref, carry):
                ref[...] = jnp.zeros_like(ref)

        @pl.when(n == nb)
        def _():
            prev_scr[...] = jnp.zeros_like(prev_scr)

        @pl.when(n < nb)
        def _():
            prev_part = _from_previous_block(N_Q_HEADS * ATTN_BLOCK)
            half = PAIRS_PER_KV * ATTN_BLOCK
            upper = _upper_lanes((ATTN_BLOCK, 2 * HEAD_DIM))
            groups = range(N_KV_HEADS)

            def nt(a, b):
                return lax.dot_general(a, b, (((1,), (1,)), ((), ())), preferred_element_type=F32)

            def kv_grad(even_rows, odd_rows, x):
                even = lax.dot_general(even_rows, x, (((0,), (0,)), ((), ())), preferred_element_type=F32)
                odd = lax.dot_general(odd_rows, x, (((0,), (0,)), ((), ())), preferred_element_type=F32)
                t = jnp.where(upper, odd, even)
                return t + _swap_lane_halves(t)

            q = [_pair_rows(q_ref, g) for g in groups]
            do = [_pair_rows(do_ref, g) for g in groups]
            k_prev = [_kv_operands(g, kp_ref[...]) for g in groups]
            k_cur = [_kv_operands(g, kc_ref[...]) for g in groups]
            v_prev = [_kv_operands(g, vp_ref[...]) for g in groups]
            v_cur = [_kv_operands(g, vc_ref[...]) for g in groups]
            probs = probs_ref[...].astype(F32)
            dp_prev = jnp.concatenate([nt(do[g], v_prev[g][i]) for g in groups for i in range(2)], axis=0)
            dp_cur = jnp.concatenate([nt(do[g], v_cur[g][i]) for g in groups for i in range(2)], axis=0)
            dp = jnp.where(prev_part, dp_prev, dp_cur)
            delta = jnp.sum(probs * dp, axis=1, keepdims=True)
            ds_prev, ds_cur = _split_folded(probs * (dp - delta) * (HEAD_DIM ** -0.5), prev_part)
            p_prev, p_cur = _split_folded(probs_ref[...], prev_part)
            for i, h in enumerate(_all_heads()):
                rows = slice(i * ATTN_BLOCK, (i + 1) * ATTN_BLOCK)
                dsink_ref[:, h:h + 1] += jnp.sum(-(psink_ref[:, h:h + 1] * delta[rows]), axis=0, keepdims=True)
            kv_grads = []
            for g in groups:
                even, odd = slice(2 * g * half, (2 * g + 1) * half), slice((2 * g + 1) * half, (2 * g + 2) * half)
                dq = (jnp.dot(ds_prev[even], k_prev[g][0], preferred_element_type=F32)
                      + jnp.dot(ds_cur[even], k_cur[g][0], preferred_element_type=F32)
                      + jnp.dot(ds_prev[odd], k_prev[g][1], preferred_element_type=F32)
                      + jnp.dot(ds_cur[odd], k_cur[g][1], preferred_element_type=F32))
                for i in range(PAIRS_PER_KV):
                    hp = g * PAIRS_PER_KV + i
                    dq_scr[:, hp * 2 * HEAD_DIM:(hp + 1) * 2 * HEAD_DIM] = dq[i * ATTN_BLOCK:(i + 1) * ATTN_BLOCK]
                kv_grads.append((kv_grad(ds_prev[even], ds_prev[odd], q[g]), kv_grad(ds_cur[even], ds_cur[odd], q[g]),
                                 kv_grad(p_prev[even], p_prev[odd], do[g]), kv_grad(p_cur[even], p_cur[odd], do[g])))
            (dkp0, dkc0, dvp0, dvc0), (dkp1, dkc1, dvp1, dvc1) = kv_grads
            prev_scr[:, :kw] = jnp.where(upper, dkp1, dkp0)
            prev_scr[:, kw:] = jnp.where(upper, dvp1, dvp0)
            cur_scr[:, :kw] = jnp.where(upper, dkc1, dkc0)
            cur_scr[:, kw:] = jnp.where(upper, dvc1, dvc0)
            dq_pre = _rope_transposed(dq_scr[...], cos_ref, sin_ref)
            dq_ref[...] = dq_pre.astype(BF16)
            dbq_ref[...] += jnp.sum(dq_pre, axis=0, keepdims=True)

        tot = carry[...] + prev_scr[...]
        dk_pre = _rope_transposed(tot[:, :kw], cosp_ref, sinp_ref)
        dkv_ref[:, :kw] = dk_pre.astype(BF16)
        dkv_ref[:, kw:] = tot[:, kw:].astype(BF16)
        dbkv_ref[:, :kw] += jnp.sum(dk_pre, axis=0, keepdims=True)
        dbkv_ref[:, kw:] += jnp.sum(tot[:, kw:], axis=0, keepdims=True)

        @pl.when(n < nb)
        def _():
            carry[...] = cur_scr[...]

    cur = lambda n: (jnp.minimum(n, nb - 1), 0)
    out_lag = lambda n: (jnp.maximum(n - 1, 0), 0)
    return pl.pallas_call(
        body, name="attn_bwd", grid=(nb + 1,),
        in_specs=[*_attn_specs(T),
                  pl.BlockSpec((ATTN_BLOCK, qw), cur),
                  pl.BlockSpec((ATTN_BLOCK, V7X_LANES), cur), pl.BlockSpec((ATTN_BLOCK, V7X_LANES), cur),
                  pl.BlockSpec((ATTN_BLOCK, V7X_LANES), out_lag), pl.BlockSpec((ATTN_BLOCK, V7X_LANES), out_lag),
                  pl.BlockSpec((None, N_Q_HEADS * ATTN_BLOCK, ATTN_BLOCK), lambda n: (jnp.minimum(n, nb - 1), 0, 0)),
                  pl.BlockSpec((ATTN_BLOCK, V7X_LANES), cur)],
        out_specs=[pl.BlockSpec((ATTN_BLOCK, qw), cur), pl.BlockSpec((ATTN_BLOCK, 2 * kw), out_lag),
                   _whole((1, N_Q_HEADS)), _whole((1, qw)), _whole((1, 2 * kw))],
        out_shape=[SDS((T, qw), BF16), SDS((T, 2 * kw), BF16),
                   SDS((1, N_Q_HEADS), F32), SDS((1, qw), F32), SDS((1, 2 * kw), F32)],
        scratch_shapes=[pltpu.VMEM((ATTN_BLOCK, 2 * kw), F32), pltpu.VMEM((ATTN_BLOCK, 2 * kw), F32),
                        pltpu.VMEM((ATTN_BLOCK, 2 * kw), F32), pltpu.VMEM((ATTN_BLOCK, qw), F32)],
        compiler_params=_params("arbitrary"),
    )(qkv, qkv, qkv, qkv, qkv, dao, cos, sin, cos, sin, probs_saved, psink_saved)


def _local_step(x, target, p, reduce_begin, reduce_send):
    T, D = x.shape
    cos, sin = _rope_tables(T)
    qw = N_Q_HEADS * HEAD_DIM
    nm, nf = p["norm_mix"], p["norm_ffn"]

    y0, qkv = _qkv_proj(x, nm[0:1], p["attn_w_qkv"], p["attn_b_qkv"], cos, sin, p["gather_started"])
    ao, attn_probs, sink_probs = _attn_fwd(qkv, p["attn_sinks"])
    h1, f0 = _mm_res("attn_out", ao, p["attn_w_o"], p["attn_b_o"], x, nf[0:1])
    p = {**p, **p["other_weights"](h1)}
    w1, w3, w2 = p["ffn_w1"], p["ffn_w3"], p["ffn_w2"]
    act0, gg0, s0 = _ffn_up("ffn0_up", f0, w1, w3, 0)
    h2 = _ffn_down("ffn0_down", s0, w2, 0, h1)
    y1, a = _pw1_proj(h2, nm[1:2], p["conv_w_pw1"], p["conv_b_pw1"])
    c, act, h3, f1 = _conv_fwd(a, p["conv_w_dw"], p["conv_b_dw"], p["conv_ln_g"], p["conv_ln_b"],
                               p["conv_w_pw2"], p["conv_b_pw2"], h2, nf[1:2])
    act1, gg1, s1 = _ffn_up("ffn1_up", f1, w1, w3, 1)
    dh4, loss, d_norm_final = _ffn_down_loss("ffn1_down_loss", s1, w2, 1, h3, p["norm_final"], target)

    g = {}
    dg1, dg3, dw2_1 = _ffn_bwd_down("ffn1_bwd_down", dh4, w2, 1, act1, gg1, s1)
    dw1_1 = _mm_tn("ffn1_dw1", dg1, f1)
    dw3_1 = _mm_tn("ffn1_dw3", dg3, f1)
    begun = reduce_begin("ffn1", {("ffn_w1", 1): dw1_1, ("ffn_w3", 1): dw3_1, ("ffn_w2", 1): dw2_1})
    dh3, dnf1, db_pw2 = _mm_nt_normbwd("ffn1_bwd_in", [(dg1, w1, 1), (dg3, w3, 1)], h3, nf[1:2], dh4, begun)
    sent = reduce_send("ffn1", dh3)

    dw_pw2 = _mm_tn("conv_dw_pw2", act, dh3, after=sent)
    da, d_ln_g, d_ln_b, d_b_dw, d_w_dw, d_b_pw1 = _conv_bwd(dh3, p["conv_w_pw2"], c, a, p["conv_w_dw"],
                                                            p["conv_ln_g"], p["conv_ln_b"])
    dw_pw1 = _mm_tn("conv_dw_pw1", y1, da, col_chunks=N_CHIPS)
    begun = reduce_begin("conv", {("conv_w_pw2", 0): dw_pw2.reshape(N_CHIPS, -1, D), ("conv_w_pw1", 0): dw_pw1})
    dh2, dnm1, _ = _mm_nt_normbwd("conv_bwd_in", [(da, p["conv_w_pw1"], None)], h2, nm[1:2], dh3, begun)
    sent = reduce_send("conv", dh2)

    dg1, dg3, dw2_0 = _ffn_bwd_down("ffn0_bwd_down", dh2, w2, 0, act0, gg0, s0, after=sent)
    dw1_0 = _mm_tn("ffn0_dw1", dg1, f0)
    dw3_0 = _mm_tn("ffn0_dw3", dg3, f0)
    begun = reduce_begin("ffn0", {("ffn_w1", 0): dw1_0, ("ffn_w3", 0): dw3_0, ("ffn_w2", 0): dw2_0})
    dh1, dnf0, db_o = _mm_nt_normbwd("ffn0_bwd_in", [(dg1, w1, 0), (dg3, w3, 0)], h1, nf[0:1], dh2, begun)
    sent = reduce_send("ffn0", dh1)

    dw_o = _mm_tn("attn_dw_o", ao, dh1, after=sent)
    dao = _mm_nt("attn_bwd_out", dh1, p["attn_w_o"], BF16)
    dq, dkv, d_sinks, dbq, dbkv = _attn_bwd(qkv, dao, cos, sin, attn_probs, sink_probs)
    dwq = _mm_tn("attn_dw_q", dq, y0)
    dwkv = _mm_tn("attn_dw_kv", dkv, y0)
    wqkv = p["attn_w_qkv"]
    dwqkv = jnp.concatenate([dwq, dwkv], axis=0).reshape(N_CHIPS, -1, D)
    begun = reduce_begin("attn", {("attn_w_o", 0): dw_o.reshape(N_CHIPS, -1, D), ("attn_w_qkv", 0): dwqkv})
    dx, dnm0, _ = _mm_nt_normbwd("attn_bwd_in", [(dq, wqkv[:qw], None), (dkv, wqkv[qw:], None)], x, nm[0:1], dh1,
                                 begun)
    reduce_send("attn", dx)

    g["norm_mix"] = jnp.concatenate([dnm0, dnm1], axis=0)
    g["norm_ffn"] = jnp.concatenate([dnf0, dnf1], axis=0)
    g["attn_b_qkv"] = jnp.concatenate([dbq, dbkv], axis=1)
    g["attn_sinks"] = d_sinks
    g["attn_b_o"] = db_o
    g["conv_b_pw1"] = d_b_pw1
    g["conv_w_dw"] = d_w_dw[:CONV_WIDTH]
    g["conv_b_dw"] = d_b_dw
    g["conv_ln_g"] = d_ln_g
    g["conv_ln_b"] = d_ln_b
    g["conv_b_pw2"] = db_pw2
    g["norm_final"] = d_norm_final
    return loss, dx, g


ANY = pl.BlockSpec(memory_space=pl.ANY)
VMEM_WHOLE = pl.BlockSpec(memory_space=pltpu.VMEM)


def _my_place():
    return lax.axis_index("x"), lax.axis_index("y"), lax.axis_index("c")


def _other_chips(x, y):
    places = [(1 - x, y), (x, 1 - y), (1 - x, 1 - y)]
    return [(bx, by, 2 * bx + by) for bx, by in places]


def _gather_small(name, v):
    r, w = v.shape

    def body(v_ref, o_ref, send_sems, recv_sems):
        x, y, c = _my_place()
        pairs = _all_to_all_copies(v_ref, o_ref, send_sems, recv_sems)
        for send, _ in pairs:
            send.start()
        o_ref[4 * x + 2 * y + c] = v_ref[...]
        for send, arrival in pairs:
            arrival.wait_recv()
            send.wait_send()

    return pl.pallas_call(
        body, name=name, out_shape=SDS((N_DEV, r, w), F32), in_specs=[VMEM_WHOLE], out_specs=VMEM_WHOLE,
        scratch_shapes=[pltpu.SemaphoreType.DMA((N_DEV - 1,)), pltpu.SemaphoreType.DMA((N_DEV - 1,))],
        compiler_params=pltpu.CompilerParams(vmem_limit_bytes=V7X_VMEM_LIMIT_BYTES),
    )(v)


def _cast_into_slot(name, gathered, shard, chip_idx):
    rows, cols = shard.shape
    tr = _pack_row_tile(rows)

    def body(k_ref, s_ref, g_ref, o_ref):
        o_ref[...] = s_ref[...].astype(BF16)

    return pl.pallas_call(
        body, name=name,
        grid_spec=pltpu.PrefetchScalarGridSpec(
            num_scalar_prefetch=1, grid=(rows // tr,),
            in_specs=[pl.BlockSpec((tr, cols), lambda i, k_ref: (i, 0)), pl.BlockSpec(memory_space=pl.ANY)],
            out_specs=pl.BlockSpec((None, tr, cols), lambda i, k_ref: (k_ref[0], i, 0))),
        out_shape=SDS(gathered.shape, BF16),
        input_output_aliases={2: 0},
        compiler_params=_params("parallel"),
    )(chip_idx, shard, gathered)


def _row_halves(ref, c):
    half = ref.shape[1] // 2
    return pl.ds(pl.multiple_of(c * half, 16), half), pl.ds(pl.multiple_of((1 - c) * half, 16), half)


def _gather_ici_copies(refs, send_sems, recv_sems):
    x, y, c = _my_place()
    k = 2 * x + y
    pairs = []
    for i, ref in enumerate(refs):
        mine, _ = _row_halves(ref, c)
        for j, (bx, by, kb) in enumerate(_other_chips(x, y)):
            sems = dict(send_sem=send_sems.at[3 * i + j], recv_sem=recv_sems.at[3 * i + j], device_id_type=MESH)
            send = pltpu.make_async_remote_copy(src_ref=ref.at[k, mine], dst_ref=ref.at[k, mine],
                                                device_id=(bx, by, c), **sems)
            arrival = pltpu.make_async_remote_copy(src_ref=ref.at[kb, mine], dst_ref=ref.at[kb, mine],
                                                   device_id=(bx, by, c), **sems)
            pairs.append((send, arrival))
    return pairs


def _gather_d2d_copies(refs, send_sems, recv_sems, first_sem):
    x, y, c = _my_place()
    pairs = []
    for i, ref in enumerate(refs):
        mine, theirs = _row_halves(ref, c)
        for j, (_, _, kb) in enumerate(_other_chips(x, y)):
            sem = first_sem + 3 * i + j
            sems = dict(send_sem=send_sems.at[sem], recv_sem=recv_sems.at[sem], device_id=(x, y, 1 - c),
                        device_id_type=MESH)
            send = pltpu.make_async_remote_copy(src_ref=ref.at[kb, mine], dst_ref=ref.at[kb, mine], **sems)
            arrival = pltpu.make_async_remote_copy(src_ref=ref.at[kb, theirs], dst_ref=ref.at[kb, theirs], **sems)
            pairs.append((send, arrival))
    return pairs


def _run_copies(pairs):
    for send, _ in pairs:
        send.start()
    for send, arrival in pairs:
        send.wait_send()
        arrival.wait_recv()


def _gather_now(name, gathered):
    n_w = len(gathered)

    def body(*refs):
        in_refs = refs[:n_w]
        send_sems, recv_sems = refs[2 * n_w:]
        _run_copies(_gather_ici_copies(in_refs, send_sems, recv_sems))
        _run_copies(_gather_d2d_copies(in_refs, send_sems, recv_sems, 3 * n_w))

    return pl.pallas_call(
        body, name=name, out_shape=[SDS(g.shape, g.dtype) for g in gathered],
        in_specs=[ANY] * n_w, out_specs=[ANY] * n_w, input_output_aliases={i: i for i in range(n_w)},
        scratch_shapes=[pltpu.SemaphoreType.DMA((6 * n_w,)), pltpu.SemaphoreType.DMA((6 * n_w,))],
    )(*gathered)


def _gather_start(name, gathered, after):
    n_w = len(gathered)

    def body(*refs):
        in_refs = refs[:n_w]
        send_sems, recv_sems = refs[n_w + 1:n_w + 3]
        for send, _ in _gather_ici_copies(in_refs, send_sems, recv_sems):
            send.start()
        refs[-1][...] = jnp.zeros_like(refs[-1])

    out = pl.pallas_call(
        body, name=name,
        out_shape=(pltpu.SemaphoreType.DMA((3 * n_w,)), pltpu.SemaphoreType.DMA((3 * n_w,)),
                   *[pltpu.HBM(g.shape, g.dtype) for g in gathered], SDS((8, V7X_LANES), F32)),
        in_specs=[*[HBM_SPEC] * n_w, ANY], out_specs=(SEM_SPEC, SEM_SPEC, *[HBM_SPEC] * n_w, VMEM_WHOLE),
        input_output_aliases={i: 2 + i for i in range(n_w)},
        compiler_params=pltpu.CompilerParams(has_side_effects=DATAFLOW),
    )(*[pltpu.with_memory_space_constraint(g, pltpu.HBM) for g in gathered], after)
    return out[0], out[1], list(out[2:2 + n_w]), out[-1]


def _gather_wait(name, send_sems, recv_sems, gathered, after):
    n_w = len(gathered)

    def body(*refs):
        in_refs = refs[:n_w]
        send_sems, recv_sems = refs[n_w:n_w + 2]
        for send, arrival in _gather_ici_copies(in_refs, send_sems, recv_sems):
            send.wait_send()
            arrival.wait_recv()

    out = pl.pallas_call(
        body, name=name, out_shape=tuple(pltpu.HBM(g.shape, g.dtype) for g in gathered),
        in_specs=[*[HBM_SPEC] * n_w, SEM_SPEC, SEM_SPEC, ANY], out_specs=tuple([HBM_SPEC] * n_w),
        input_output_aliases={i: i for i in range(n_w)},
        compiler_params=pltpu.CompilerParams(has_side_effects=DATAFLOW),
    )(*gathered, send_sems, recv_sems, after)
    return list(out)


def _swap_fetched_with_sibling(name, gathered):
    n_w = len(gathered)

    def body(*refs):
        in_refs = refs[:n_w]
        send_sems, recv_sems = refs[2 * n_w:]
        _run_copies(_gather_d2d_copies(in_refs, send_sems, recv_sems, 0))

    return pl.pallas_call(
        body, name=name, out_shape=[SDS(g.shape, g.dtype) for g in gathered],
        in_specs=[ANY] * n_w, out_specs=[ANY] * n_w, input_output_aliases={i: i for i in range(n_w)},
        scratch_shapes=[pltpu.SemaphoreType.DMA((3 * n_w,)), pltpu.SemaphoreType.DMA((3 * n_w,))],
    )(*gathered)


def _sibling_swap_copies(g_refs, land_refs, send_sems, recv_sems):
    x, y, c = _my_place()
    copies = []
    for i, g_ref in enumerate(g_refs):
        half = g_ref.shape[1] // 2
        theirs = pl.ds(pl.multiple_of((1 - c) * half, 8), half)
        copies.append(pltpu.make_async_remote_copy(
            src_ref=g_ref.at[:, theirs], dst_ref=land_refs[i], send_sem=send_sems.at[i], recv_sem=recv_sems.at[i],
            device_id=(x, y, 1 - c), device_id_type=MESH))
    return copies


def _sibling_swap_start(name, grads):
    n_g = len(grads)

    def body(*refs):
        g_refs, land_refs = refs[:n_g], refs[n_g:2 * n_g]
        send_sems, recv_sems = refs[2 * n_g:2 * n_g + 2]
        for cp in _sibling_swap_copies(g_refs, land_refs, send_sems, recv_sems):
            cp.start()
        refs[-1][...] = jnp.zeros_like(refs[-1])

    lands = [pltpu.with_memory_space_constraint(lax.empty((g.shape[0], g.shape[1] // 2, g.shape[2]), g.dtype),
                                                pltpu.HBM) for g in grads]
    out = pl.pallas_call(
        body, name=name,
        out_shape=(pltpu.SemaphoreType.DMA((n_g,)), pltpu.SemaphoreType.DMA((n_g,)),
                   *[pltpu.HBM(g.shape, g.dtype) for g in grads], *[pltpu.HBM(l.shape, l.dtype) for l in lands],
                   SDS((8, V7X_LANES), F32)),
        in_specs=[HBM_SPEC] * (2 * n_g), out_specs=(SEM_SPEC, SEM_SPEC, *[HBM_SPEC] * (2 * n_g), VMEM_WHOLE),
        input_output_aliases={i: 2 + i for i in range(2 * n_g)},
        compiler_params=pltpu.CompilerParams(has_side_effects=DATAFLOW),
    )(*[pltpu.with_memory_space_constraint(g, pltpu.HBM) for g in grads], *lands)
    return out[0], out[1], list(out[2:2 + n_g]), list(out[2 + n_g:2 + 2 * n_g]), out[-1]


def _sibling_swap_wait(name, send_sems, recv_sems, grads, lands, after):
    n_g = len(grads)

    def body(*refs):
        g_refs, land_refs = refs[:n_g], refs[n_g:2 * n_g]
        send_sems, recv_sems = refs[2 * n_g:2 * n_g + 2]
        for cp in _sibling_swap_copies(g_refs, land_refs, send_sems, recv_sems):
            cp.wait_send()
            cp.wait_recv()

    out = pl.pallas_call(
        body, name=name,
        out_shape=(*[pltpu.HBM(g.shape, g.dtype) for g in grads], *[pltpu.HBM(l.shape, l.dtype) for l in lands]),
        in_specs=[*[HBM_SPEC] * (2 * n_g), SEM_SPEC, SEM_SPEC, ANY], out_specs=tuple([HBM_SPEC] * (2 * n_g)),
        input_output_aliases={i: i for i in range(2 * n_g)},
        compiler_params=pltpu.CompilerParams(has_side_effects=DATAFLOW),
    )(*grads, *lands, send_sems, recv_sems, after)
    return list(out[:n_g]), list(out[n_g:])


def _pack_row_tile(rows):
    for t in range(min(rows, 512), 7, -1):
        if rows % t == 0 and t % 8 == 0:
            return t
    return rows


def _add_sibling_half(name, grads, from_sibling, c_idx):
    n, R, w = grads.shape
    half = R // 2
    tr = _pack_row_tile(half)
    steps = half // tr

    def body(c_ref, g_ref, s_ref, o_ref):
        o_ref[...] = (g_ref[...] + s_ref[...]).astype(BF16)

    return pl.pallas_call(
        body, name=name,
        grid_spec=pltpu.PrefetchScalarGridSpec(
            num_scalar_prefetch=1, grid=(n, steps),
            in_specs=[pl.BlockSpec((1, tr, w), lambda j, i, c_ref: (j, c_ref[0] * steps + i, 0)),
                      pl.BlockSpec((1, tr, w), lambda j, i, c_ref: (j, i, 0))],
            out_specs=pl.BlockSpec((1, tr, w), lambda j, i, c_ref: (j, i, 0))),
        out_shape=SDS((n, half, w), BF16),
        compiler_params=_params("parallel", "parallel"),
    )(c_idx, grads, from_sibling)


HBM_SPEC = pl.BlockSpec(memory_space=pltpu.HBM)
SEM_SPEC = pl.BlockSpec(memory_space=pltpu.SEMAPHORE)
DATAFLOW = pltpu.SideEffectType.DATAFLOW_SIDE_EFFECTING


def _chip_scatter_copies(p_refs, land_refs, send_sems, recv_sems):
    x, y, c = _my_place()
    return [pltpu.make_async_remote_copy(
        src_ref=p_refs[i].at[kb], dst_ref=land_refs[i].at[j], send_sem=send_sems.at[3 * i + j],
        recv_sem=recv_sems.at[3 * i + j], device_id=(bx, by, c), device_id_type=MESH)
        for i in range(len(p_refs)) for j, (bx, by, kb) in enumerate(_other_chips(x, y))]


def _scatter_start(name, partials):
    n_p = len(partials)

    def body(*refs):
        p_refs, land_refs = refs[:n_p], refs[n_p:2 * n_p]
        send_sems, recv_sems = refs[2 * n_p:2 * n_p + 2]
        for cp in _chip_scatter_copies(p_refs, land_refs, send_sems, recv_sems):
            cp.start()
        refs[-1][...] = jnp.zeros_like(refs[-1])

    lands = [pltpu.with_memory_space_constraint(lax.empty((N_CHIPS - 1,) + p.shape[1:], p.dtype), pltpu.HBM)
             for p in partials]
    out = pl.pallas_call(
        body, name=name,
        out_shape=(pltpu.SemaphoreType.DMA((3 * n_p,)), pltpu.SemaphoreType.DMA((3 * n_p,)),
                   *[pltpu.HBM(p.shape, p.dtype) for p in partials], *[pltpu.HBM(l.shape, l.dtype) for l in lands],
                   SDS((8, V7X_LANES), F32)),
        in_specs=[HBM_SPEC] * (2 * n_p), out_specs=(SEM_SPEC, SEM_SPEC, *[HBM_SPEC] * (2 * n_p), VMEM_WHOLE),
        input_output_aliases={i: 2 + i for i in range(2 * n_p)},
        compiler_params=pltpu.CompilerParams(has_side_effects=DATAFLOW),
    )(*[pltpu.with_memory_space_constraint(p, pltpu.HBM) for p in partials], *lands)
    return out[0], out[1], list(out[2:2 + n_p]), list(out[2 + n_p:2 + 2 * n_p]), out[-1]


def _scatter_wait(name, send_sems, recv_sems, partials, lands, after):
    n_p = len(partials)

    def body(*refs):
        p_refs, land_refs = refs[:n_p], refs[n_p:2 * n_p]
        send_sems, recv_sems = refs[2 * n_p:2 * n_p + 2]
        for cp in _chip_scatter_copies(p_refs, land_refs, send_sems, recv_sems):
            cp.wait_send()
            cp.wait_recv()

    out = pl.pallas_call(
        body, name=name,
        out_shape=(*[pltpu.HBM(p.shape, p.dtype) for p in partials], *[pltpu.HBM(l.shape, l.dtype) for l in lands]),
        in_specs=[*[HBM_SPEC] * (2 * n_p), SEM_SPEC, SEM_SPEC, ANY], out_specs=tuple([HBM_SPEC] * (2 * n_p)),
        input_output_aliases={i: i for i in range(2 * n_p)},
        compiler_params=pltpu.CompilerParams(has_side_effects=DATAFLOW),
    )(*partials, *lands, send_sems, recv_sems, after)
    return list(out[:n_p]), list(out[n_p:])


def _sum_chip_partials(name, grads, from_sibling, received, shard, layer, place):
    n, half, w = from_sibling.shape
    tr = _pack_row_tile(half)
    steps = half // tr

    def body(place_ref, g_ref, s_ref, r_ref, shard_ref, o_ref):
        own = g_ref[0] + s_ref[0]
        o_ref[...] = ((own + r_ref[0].astype(F32)) + r_ref[1].astype(F32)) + r_ref[2].astype(F32)

    return pl.pallas_call(
        body, name=name,
        grid_spec=pltpu.PrefetchScalarGridSpec(
            num_scalar_prefetch=1, grid=(steps,),
            in_specs=[pl.BlockSpec((1, tr, w), lambda i, place_ref: (place_ref[0], place_ref[1] * steps + i, 0)),
                      pl.BlockSpec((1, tr, w), lambda i, place_ref: (place_ref[0], i, 0)),
                      pl.BlockSpec((n - 1, tr, w), lambda i, place_ref: (0, i, 0)),
                      pl.BlockSpec(memory_space=pl.ANY)],
            out_specs=pl.BlockSpec((tr, w), lambda i, place_ref: ((2 * layer + place_ref[1]) * steps + i, 0))),
        out_shape=SDS(shard.shape, F32),
        input_output_aliases={4: 0},
        compiler_params=_params("parallel"),
    )(place, grads, from_sibling, received, shard)


def _join_halves(shards, layers):
    n_s = len(shards)
    n_sem = sum(layers)

    def body(*refs):
        in_refs = refs[:n_s]
        send_sems, recv_sems = refs[2 * n_s:]
        x, y, c = _my_place()
        copies, sem = [], 0
        for ref, n_layers in zip(in_refs, layers):
            half = ref.shape[0] // (2 * n_layers)
            for layer in range(n_layers):
                mine = pl.ds(pl.multiple_of(layer * 2 * half + c * half, 8), half)
                theirs = pl.ds(pl.multiple_of(layer * 2 * half + (1 - c) * half, 8), half)
                send = pltpu.make_async_remote_copy(
                    src_ref=ref.at[mine], dst_ref=ref.at[mine], send_sem=send_sems.at[sem], recv_sem=recv_sems.at[sem],
                    device_id=(x, y, 1 - c), device_id_type=MESH)
                send.start()
                arrival = pltpu.make_async_remote_copy(
                    src_ref=ref.at[theirs], dst_ref=ref.at[theirs], send_sem=send_sems.at[sem],
                    recv_sem=recv_sems.at[sem], device_id=(x, y, 1 - c), device_id_type=MESH)
                copies.append((send, arrival))
                sem += 1
        for send, arrival in copies:
            send.wait_send()
            arrival.wait_recv()

    return pl.pallas_call(
        body, name="join_halves", out_shape=[SDS(s.shape, s.dtype) for s in shards],
        in_specs=[ANY] * n_s, out_specs=[ANY] * n_s,
        input_output_aliases={i: i for i in range(n_s)},
        scratch_shapes=[pltpu.SemaphoreType.DMA((n_sem,)), pltpu.SemaphoreType.DMA((n_sem,))],
    )(*shards)


def _all_to_all_copies(v_ref, land_ref, send_sems, recv_sems):
    x, y, c = _my_place()
    me = 4 * x + 2 * y + c
    pairs = []
    for k in range(1, N_DEV):
        px, py, pc = (1 - x if k & 4 else x), (1 - y if k & 2 else y), (1 - c if k & 1 else c)
        sems = dict(send_sem=send_sems.at[k - 1], recv_sem=recv_sems.at[k - 1], device_id=(px, py, pc),
                    device_id_type=MESH)
        send = pltpu.make_async_remote_copy(src_ref=v_ref, dst_ref=land_ref.at[me], **sems)
        arrival = pltpu.make_async_remote_copy(src_ref=v_ref, dst_ref=land_ref.at[4 * px + 2 * py + pc], **sems)
        pairs.append((send, arrival))
    return pairs


def _small_reduce_start(name, v):
    def body(v_ref, land_ref, send_sems, recv_sems, v_out, land_out):
        for send, _ in _all_to_all_copies(v_ref, land_ref, send_sems, recv_sems):
            send.start()

    land = pltpu.with_memory_space_constraint(jnp.zeros((N_DEV,) + v.shape, v.dtype), pltpu.HBM)
    return pl.pallas_call(
        body, name=name,
        out_shape=(pltpu.SemaphoreType.DMA((N_DEV - 1,)), pltpu.SemaphoreType.DMA((N_DEV - 1,)),
                   pltpu.HBM(v.shape, v.dtype), pltpu.HBM(land.shape, land.dtype)),
        in_specs=[HBM_SPEC, HBM_SPEC], out_specs=(SEM_SPEC, SEM_SPEC, HBM_SPEC, HBM_SPEC),
        input_output_aliases={0: 2, 1: 3},
        compiler_params=pltpu.CompilerParams(has_side_effects=DATAFLOW),
    )(pltpu.with_memory_space_constraint(v, pltpu.HBM), land)


def _small_reduce_wait(name, send_sems, recv_sems, v, land, after):
    def body(v_ref, land_ref, send_sems, recv_sems, after_ref, v_out, land_out):
        for send, arrival in _all_to_all_copies(v_ref, land_ref, send_sems, recv_sems):
            send.wait_send()
            arrival.wait_recv()

    return pl.pallas_call(
        body, name=name, out_shape=(pltpu.HBM(v.shape, v.dtype), pltpu.HBM(land.shape, land.dtype)),
        in_specs=[HBM_SPEC, HBM_SPEC, SEM_SPEC, SEM_SPEC, ANY], out_specs=(HBM_SPEC, HBM_SPEC),
        input_output_aliases={0: 0, 1: 1},
        compiler_params=pltpu.CompilerParams(has_side_effects=DATAFLOW),
    )(v, land, send_sems, recv_sems, after)


def _sum_device_slots(name, v, land, me):
    r, w = v.shape

    def body(me_ref, v_ref, land_ref, o_ref):
        mine = v_ref[...]
        acc = jnp.where(me_ref[0] == 0, mine, land_ref[0])
        for d in range(1, N_DEV):
            acc = acc + jnp.where(me_ref[0] == d, mine, land_ref[d])
        o_ref[...] = acc

    return pl.pallas_call(
        body, name=name,
        grid_spec=pltpu.PrefetchScalarGridSpec(
            num_scalar_prefetch=1, grid=(1,),
            in_specs=[pl.BlockSpec((r, w), lambda i, me_ref: (0, 0)),
                      pl.BlockSpec((N_DEV, r, w), lambda i, me_ref: (0, 0, 0))],
            out_specs=pl.BlockSpec((r, w), lambda i, me_ref: (0, 0))),
        out_shape=SDS((r, w), F32),
        compiler_params=_params("arbitrary"),
    )(me, v, land)


def _adamw(name, w, g, m, v):
    rows, width = w.shape
    tr = _pack_row_tile(rows)

    def body(w_ref, g_ref, m_ref, v_ref, g_out_ref, d_ref, nm_ref, nv_ref):
        gg = g_ref[...]
        g_out_ref[...] = gg
        m_new = ADAM_B1 * m_ref[...] + (1.0 - ADAM_B1) * gg
        v_new = ADAM_B2 * v_ref[...] + (1.0 - ADAM_B2) * (gg * gg)
        m_hat = m_new / (1.0 - ADAM_B1 ** ADAM_STEP)
        v_hat = v_new / (1.0 - ADAM_B2 ** ADAM_STEP)
        d_ref[...] = -ADAM_LR * (m_hat / (jnp.sqrt(v_hat) + ADAM_EPS) + ADAM_WD * w_ref[...])
        nm_ref[...] = m_new
        nv_ref[...] = v_new

    spec = _rows(tr, width)
    return pl.pallas_call(
        body, name=name, grid=(rows // tr,),
        in_specs=[spec] * 4, out_specs=[spec] * 4,
        out_shape=[SDS((rows, width), F32)] * 4,
        compiler_params=_params("parallel"),
    )(w, g, m, v)


WEIGHT_NAMES = ['norm_mix', 'norm_ffn', 'attn_w_qkv', 'attn_b_qkv', 'attn_sinks', 'attn_w_o', 'attn_b_o',
                'conv_w_pw1', 'conv_b_pw1', 'conv_w_dw', 'conv_b_dw', 'conv_ln_g', 'conv_ln_b', 'conv_w_pw2',
                'conv_b_pw2', 'ffn_w1', 'ffn_w3', 'ffn_w2', 'norm_final']
BIG = ['attn_w_qkv', 'attn_w_o', 'conv_w_pw1', 'conv_w_pw2', 'ffn_w1', 'ffn_w3', 'ffn_w2']
COLUMN_SPLIT = ('attn_w_qkv', 'conv_w_pw1', 'ffn_w1', 'ffn_w3')
STORED_TRANSPOSED = ('attn_w_qkv', 'ffn_w1', 'ffn_w3')
SMALL_SPLIT = ['conv_b_pw1', 'conv_w_dw', 'conv_b_dw', 'conv_ln_g', 'conv_ln_b', 'conv_b_pw2']
SMALL_WHOLE = ['norm_mix', 'norm_ffn', 'attn_b_qkv', 'attn_sinks', 'attn_b_o', 'norm_final']


def _keeps_rows(shape):
    return len(shape) == 2 and shape[0] > 1 and shape[1] == PACK_W


def _pack_rows(arrays, dtype, row_multiple):
    blocks = [jnp.pad(a.astype(dtype), ((0, -a.shape[0] % V7X_SUBLANES), (0, 0)))
              for a in arrays if _keeps_rows(a.shape)]
    flat = jnp.concatenate([a.astype(dtype).reshape(-1) for a in arrays if not _keeps_rows(a.shape)])
    multiple = max(row_multiple, V7X_SUBLANES)
    rows = -(-(-(-flat.shape[0] // PACK_W)) // multiple) * multiple
    blocks.append(jnp.pad(flat, (0, rows * PACK_W - flat.shape[0])).reshape(rows, PACK_W))
    return jnp.concatenate(blocks, axis=0) if len(blocks) > 1 else blocks[0]


def _unpack_rows(pack, shapes):
    out, row = {}, 0
    for i, shape in enumerate(shapes):
        if _keeps_rows(shape):
            out[i] = pack[row:row + shape[0]]
            row += -(-shape[0] // V7X_SUBLANES) * V7X_SUBLANES
    flat, at = pack[row:].reshape(-1), 0
    for i, shape in enumerate(shapes):
        if not _keeps_rows(shape):
            size = 1
            for s in shape:
                size *= s
            out[i] = flat[at:at + size].reshape(shape)
            at += size
    return [out[i] for i in range(len(shapes))]


def _join_chip_axis(name, parts):
    axis = parts.ndim - 1 if name in COLUMN_SPLIT or name in SMALL_SPLIT else parts.ndim - 2
    moved = jnp.moveaxis(parts, 0, axis - 1)
    shape = list(moved.shape)
    shape[axis - 1:axis + 1] = [shape[axis - 1] * shape[axis]]
    return moved.reshape(shape)


def _split_chip_axis(name, whole, shard_shape):
    axis = len(shard_shape) - 1 if name in COLUMN_SPLIT or name in SMALL_SPLIT else len(shard_shape) - 2
    shape = list(whole.shape)
    shape[axis:axis + 1] = [N_CHIPS, shard_shape[axis]]
    return jnp.moveaxis(whole.reshape(shape), axis, 0)


def kernel(x, norm_mix, norm_ffn, attn_w_qkv, attn_b_qkv, attn_sinks, attn_w_o, attn_b_o, conv_w_pw1, conv_b_pw1, conv_w_dw, conv_b_dw, conv_ln_g, conv_ln_b, conv_w_pw2, conv_b_pw2, ffn_w1, ffn_w3, ffn_w2, norm_final, loss_target, m_norm_mix, m_norm_ffn, m_attn_w_qkv, m_attn_b_qkv, m_attn_sinks, m_attn_w_o, m_attn_b_o, m_conv_w_pw1, m_conv_b_pw1, m_conv_w_dw, m_conv_b_dw, m_conv_ln_g, m_conv_ln_b, m_conv_w_pw2, m_conv_b_pw2, m_ffn_w1, m_ffn_w3, m_ffn_w2, m_norm_final, v_norm_mix, v_norm_ffn, v_attn_w_qkv, v_attn_b_qkv, v_attn_sinks, v_attn_w_o, v_attn_b_o, v_conv_w_pw1, v_conv_b_pw1, v_conv_w_dw, v_conv_b_dw, v_conv_ln_g, v_conv_ln_b, v_conv_w_pw2, v_conv_b_pw2, v_ffn_w1, v_ffn_w3, v_ffn_w2, v_norm_final):
    w = dict(zip(WEIGHT_NAMES, (norm_mix, norm_ffn, attn_w_qkv, attn_b_qkv, attn_sinks, attn_w_o, attn_b_o,
                                conv_w_pw1, conv_b_pw1, conv_w_dw, conv_b_dw, conv_ln_g, conv_ln_b, conv_w_pw2,
                                conv_b_pw2, ffn_w1, ffn_w3, ffn_w2, norm_final)))
    m = dict(zip(WEIGHT_NAMES, (m_norm_mix, m_norm_ffn, m_attn_w_qkv, m_attn_b_qkv, m_attn_sinks, m_attn_w_o,
                                m_attn_b_o, m_conv_w_pw1, m_conv_b_pw1, m_conv_w_dw, m_conv_b_dw, m_conv_ln_g,
                                m_conv_ln_b, m_conv_w_pw2, m_conv_b_pw2, m_ffn_w1, m_ffn_w3, m_ffn_w2, m_norm_final)))
    v = dict(zip(WEIGHT_NAMES, (v_norm_mix, v_norm_ffn, v_attn_w_qkv, v_attn_b_qkv, v_attn_sinks, v_attn_w_o,
                                v_attn_b_o, v_conv_w_pw1, v_conv_b_pw1, v_conv_w_dw, v_conv_b_dw, v_conv_ln_g,
                                v_conv_ln_b, v_conv_w_pw2, v_conv_b_pw2, v_ffn_w1, v_ffn_w3, v_ffn_w2, v_norm_final)))
    T, D = x.shape[1], x.shape[2]
    c_idx = lax.axis_index("c").astype(jnp.int32).reshape(1)
    chip = (2 * lax.axis_index("x") + lax.axis_index("y")).astype(jnp.int32)

    def as_rows(n, a):
        a = jnp.swapaxes(a, -1, -2) if n in STORED_TRANSPOSED else a
        return a.reshape(-1, a.shape[-1])

    def from_rows(n, rows):
        shape = w[n].shape[:-2] + w[n].shape[:-3:-1] if n in STORED_TRANSPOSED else w[n].shape
        a = rows.reshape(shape)
        return jnp.swapaxes(a, -1, -2) if n in STORED_TRANSPOSED else a

    slabs = {n: _cast_into_slot(f"cast_{n}", lax.empty((N_CHIPS,) + as_rows(n, w[n]).shape, BF16), as_rows(n, w[n]),
                                chip.reshape(1)) for n in BIG}
    first, later = BIG[:2], BIG[2:]
    qkv_parts, w_o_parts = _gather_now("gather_attn", [slabs[n] for n in first])
    send_sems, recv_sems, travelling, gather_started = _gather_start("gather_start", [slabs[n] for n in later],
                                                                     qkv_parts)
    layers = ffn_w1.shape[0]
    small_shapes = [w[n].shape for n in SMALL_SPLIT]
    small_all = _gather_small("gather_small", _pack_rows([w[n] for n in SMALL_SPLIT], F32, 8))
    per_chip = [_unpack_rows(small_all[2 * j], small_shapes) for j in range(N_CHIPS)]
    full = {}
    for i, n in enumerate(SMALL_SPLIT):
        full[n] = _join_chip_axis(n, jnp.stack([per_chip[j][i] for j in range(N_CHIPS)]))

    def other_weights(after):
        landed = _gather_wait("gather_wait", send_sems, recv_sems, travelling, after)
        gathered = dict(zip(later, _swap_fetched_with_sibling("gather_swap", landed)))
        return {"conv_w_pw1": gathered["conv_w_pw1"], "conv_w_pw2": gathered["conv_w_pw2"].reshape(-1, D),
                "ffn_w1": gathered["ffn_w1"].reshape(N_CHIPS, layers, -1, D),
                "ffn_w3": gathered["ffn_w3"].reshape(N_CHIPS, layers, -1, D),
                "ffn_w2": gathered["ffn_w2"].reshape(N_CHIPS, layers, -1, D)}

    p = {
        "norm_mix": norm_mix, "norm_ffn": norm_ffn, "norm_final": norm_final.reshape(1, D),
        "attn_w_qkv": qkv_parts.reshape(-1, D), "attn_b_qkv": attn_b_qkv,
        "attn_sinks": attn_sinks, "attn_w_o": w_o_parts.reshape(-1, D), "attn_b_o": attn_b_o,
        "conv_b_pw1": full["conv_b_pw1"], "conv_w_dw": full["conv_w_dw"][0],
        "conv_b_dw": full["conv_b_dw"], "conv_ln_g": full["conv_ln_g"], "conv_ln_b": full["conv_ln_b"],
        "conv_b_pw2": full["conv_b_pw2"], "gather_started": gather_started, "other_weights": other_weights,
    }
    swapping, in_flight = {}, []

    def reduce_begin(tag, grads):
        keys = list(grads)
        *handles, begun = _sibling_swap_start(f"sibling_swap_start_{tag}", [grads[k] for k in keys])
        swapping[tag] = (keys, handles)
        return begun

    def reduce_send(tag, after):
        keys, (swap_send, swap_recv, grads, lands) = swapping[tag]
        grads, from_sibling = _sibling_swap_wait(f"sibling_swap_wait_{tag}", swap_send, swap_recv, grads, lands, after)
        partials = [_add_sibling_half(f"add_sibling_half_{tag}{i}", gr, fs, c_idx)
                    for i, (gr, fs) in enumerate(zip(grads, from_sibling))]
        *handles, sent = _scatter_start(f"scatter_start_{tag}", partials)
        in_flight.append((tag, keys, handles, grads, from_sibling))
        return sent

    loss_part, dx, g = _local_step(x[0], loss_target[0], p, reduce_begin, reduce_send)
    for n in SMALL_WHOLE + SMALL_SPLIT:
        g[n] = g[n].reshape((-1,) + g[n].shape[-2:]) if w[n].ndim == 3 else g[n].reshape(w[n].shape[:-1] + (-1,))

    small_pack = _pack_rows([loss_part] + [g[n] for n in SMALL_WHOLE] + [g[n] for n in SMALL_SPLIT], F32, 8)
    small_send, small_recv, small_pack, small_land = _small_reduce_start("small_reduce_start", small_pack)

    place = jnp.stack([chip, c_idx[0]])
    shard_grad = {n: lax.empty(as_rows(n, w[n]).shape, F32) for n in BIG}
    for tag, keys, (send_sems, recv_sems, partials, lands), grads, from_sibling in in_flight:
        _, received = _scatter_wait(f"scatter_wait_{tag}", send_sems, recv_sems, partials, lands, small_pack)
        for i, (n, layer) in enumerate(keys):
            shard_grad[n] = _sum_chip_partials(f"sum_chip_partials_{tag}{i}", grads[i], from_sibling[i], received[i],
                                               shard_grad[n], layer, place)
    g_big = dict(zip(BIG, _join_halves([shard_grad[n] for n in BIG], [w[n].shape[0] for n in BIG])))
    big_out = {}
    for n in BIG:
        step = _adamw(f"adamw_{n}", as_rows(n, w[n]), g_big[n], as_rows(n, m[n]), as_rows(n, v[n]))
        big_out[n] = [from_rows(n, a) for a in step]

    small_whole_shapes = [w[n].shape for n in SMALL_WHOLE]
    small_full_shapes = [g[n].shape for n in SMALL_SPLIT]
    small_pack, small_land = _small_reduce_wait("small_reduce_wait", small_send, small_recv, small_pack, small_land,
                                                big_out[BIG[-1]][1])
    reduced = _sum_device_slots("small_reduce_sum", small_pack, small_land, (2 * chip + c_idx[0]).reshape(1))
    pieces = _unpack_rows(reduced, [(1,)] + small_whole_shapes + small_full_shapes)
    loss = pieces[0].reshape(())
    g_small = dict(zip(SMALL_WHOLE, pieces[1:1 + len(SMALL_WHOLE)]))
    for n, whole in zip(SMALL_SPLIT, pieces[1 + len(SMALL_WHOLE):]):
        parts = _split_chip_axis(n, whole, w[n].shape)
        g_small[n] = lax.dynamic_index_in_dim(parts, chip, axis=0, keepdims=False)
    small = SMALL_WHOLE + SMALL_SPLIT
    _, d_small, m_small, v_small = _adamw(
        "adamw_small", _pack_rows([w[n] for n in small], F32, 8), _pack_rows([g_small[n] for n in small], F32, 8),
        _pack_rows([m[n] for n in small], F32, 8), _pack_rows([v[n] for n in small], F32, 8))

    outs = {}
    for slot, (tag, small_pack) in enumerate((("g", None), ("d", d_small), ("m", m_small), ("v", v_small))):
        vals = {n: big_out[n][slot] for n in BIG}
        if small_pack is None:
            vals.update(g_small)
        else:
            vals.update(zip(small, _unpack_rows(small_pack, [w[n].shape for n in small])))
        outs[tag] = vals
    return (loss, dx.reshape(1, T, D), *[outs["g"][n] for n in WEIGHT_NAMES], *[outs["d"][n] for n in WEIGHT_NAMES],
            *[outs["m"][n] for n in WEIGHT_NAMES], *[outs["v"][n] for n in WEIGHT_NAMES])
```

```python
import functools

import jax
import jax.numpy as jnp
from jax import lax
from jax.experimental import pallas as pl
from jax.experimental.pallas import tpu as pltpu

F32 = jnp.float32
BF16 = jnp.bfloat16
SDS = jax.ShapeDtypeStruct
MESH = pl.DeviceIdType.MESH

HEAD_DIM = 64
N_Q_HEADS = 16
N_KV_HEADS = 2
Q_PER_KV = N_Q_HEADS // N_KV_HEADS
ATTN_BLOCK = 128
ROPE_THETA = 10000.0
CONV_WIDTH = 31
CONV_HALO = 32
CONV_FIRST_TAP = CONV_HALO - CONV_WIDTH + 1
CONV_ROW_CHUNK = 64
CONV_LANE_CHUNK = 256
CONV_GRAD_UNROLL = 8
RMS_EPS = 1e-5
LN_EPS = 1e-5
ADAM_LR = 0.001
ADAM_B1 = 0.9
ADAM_B2 = 0.999
ADAM_EPS = 1e-08
ADAM_WD = 0.01
ADAM_STEP = 10

V7X_LANES = 128
V7X_SUBLANES = 8
V7X_VMEM_LIMIT_BYTES = 56 * 1024 * 1024

N_CHIPS = 4
N_DEV = 8
PACK_W = 1024

MASK_VALUE = -1e30


def _params(*semantics):
    return pltpu.CompilerParams(dimension_semantics=semantics, vmem_limit_bytes=V7X_VMEM_LIMIT_BYTES)


def _rows(tm, width):
    return pl.BlockSpec((tm, width), lambda i: (i, 0))


def _whole(shape):
    return pl.BlockSpec(shape, lambda *_: (0,) * len(shape))


def _rms_rstd(h):
    return lax.rsqrt(jnp.mean(h * h, axis=-1, keepdims=True) + RMS_EPS)


def _silu_and_grad(z):
    sg = jax.nn.sigmoid(z)
    return z * sg, sg * (1.0 + z * (1.0 - sg))


def _swap_rope_halves(t):
    w = t.shape[1]
    half = HEAD_DIM // 2
    lane = lax.broadcasted_iota(jnp.int32, t.shape, 1)
    upper = pltpu.roll(t, w - half, 1)
    lower = pltpu.roll(t, half, 1)
    return jnp.where(lane % HEAD_DIM < half, upper, lower)


def _rope(t, cos_ref, sin_ref):
    reps = t.shape[1] // V7X_LANES
    c = jnp.tile(cos_ref[...], (1, reps))
    s = jnp.tile(sin_ref[...], (1, reps))
    return t * c + _swap_rope_halves(t) * s


def _rope_transposed(dt, cos_ref, sin_ref):
    reps = dt.shape[1] // V7X_LANES
    c = jnp.tile(cos_ref[...], (1, reps))
    s = jnp.tile(sin_ref[...], (1, reps))
    return dt * c + _swap_rope_halves(dt * s)


def _rope_tables(seq_len):
    pos = jnp.arange(seq_len, dtype=F32)
    inv_freq = ROPE_THETA ** (-jnp.arange(0, HEAD_DIM, 2, dtype=F32) / HEAD_DIM)
    ang = pos[:, None] * jnp.tile(inv_freq, 2 * V7X_LANES // HEAD_DIM)[None, :]
    upper_half = jnp.arange(V7X_LANES) % HEAD_DIM >= HEAD_DIM // 2
    return jnp.cos(ang), jnp.where(upper_half[None, :], jnp.sin(ang), -jnp.sin(ang))


def _qkv_proj(h, g, w, b, cos, sin, after):
    T, D = h.shape
    N = w.shape[0]
    tm = min(512, T)
    rope_w = N - N_KV_HEADS * HEAD_DIM

    def body(h_ref, g_ref, w_ref, b_ref, cos_ref, sin_ref, _, y_ref, o_ref):
        hh = h_ref[...]
        y = (hh * _rms_rstd(hh) * g_ref[...]).astype(BF16)
        y_ref[...] = y
        acc = _dot_nt(y, w_ref[...]) + b_ref[...]
        o_ref[:, :rope_w] = _rope(acc[:, :rope_w], cos_ref, sin_ref).astype(BF16)
        o_ref[:, rope_w:] = acc[:, rope_w:].astype(BF16)

    return pl.pallas_call(
        body, name="qkv_proj", grid=(T // tm,),
        in_specs=[_rows(tm, D), _whole((1, D)), _whole((N, D)), _whole((1, N)),
                  _rows(tm, V7X_LANES), _rows(tm, V7X_LANES), pl.BlockSpec(memory_space=pl.ANY)],
        out_specs=[_rows(tm, D), _rows(tm, N)],
        out_shape=[SDS((T, D), BF16), SDS((T, N), BF16)],
        compiler_params=_params("parallel"),
    )(h, g, w, b, cos, sin, after)


def _pw1_proj(h, g, w, b):
    T, D = h.shape
    n = w.shape[2]
    N = N_CHIPS * n
    tm = min(512, T)

    def body(h_ref, g_ref, w_ref, b_ref, y_ref, o_ref):
        hh = h_ref[...]
        y = (hh * _rms_rstd(hh) * g_ref[...]).astype(BF16)
        y_ref[...] = y
        for j in range(N_CHIPS):
            cols = slice(j * n, (j + 1) * n)
            o_ref[:, cols] = jnp.dot(y, w_ref[j], preferred_element_type=F32) + b_ref[:, cols]

    return pl.pallas_call(
        body, name="pw1_proj", grid=(T // tm,),
        in_specs=[_rows(tm, D), _whole((1, D)), _whole((N_CHIPS, D, n)), _whole((1, N))],
        out_specs=[_rows(tm, D), _rows(tm, N)],
        out_shape=[SDS((T, D), BF16), SDS((T, N), F32)],
        compiler_params=_params("parallel"),
    )(h, g, w, b)


PAIRS_PER_KV = Q_PER_KV // 2


def _upper_lanes(shape):
    return lax.broadcasted_iota(jnp.int32, shape, 1) >= HEAD_DIM


def _swap_lane_halves(t):
    return pltpu.roll(t.astype(F32), HEAD_DIM, 1).astype(t.dtype)


def _kv_operands(g, t):
    swapped = _swap_lane_halves(t)
    in_lower, in_upper = (t, swapped) if g == 0 else (swapped, t)
    upper = _upper_lanes(t.shape)
    zero = jnp.zeros_like(t)
    return jnp.where(upper, zero, in_lower), jnp.where(upper, in_upper, zero)


def _group_heads(g):
    pairs = range(g * PAIRS_PER_KV, (g + 1) * PAIRS_PER_KV)
    return [2 * hp for hp in pairs] + [2 * hp + 1 for hp in pairs]


def _all_heads():
    return [h for g in range(N_KV_HEADS) for h in _group_heads(g)]


def _pair_rows(ref, g):
    pairs = range(g * PAIRS_PER_KV, (g + 1) * PAIRS_PER_KV)
    return jnp.concatenate([ref[:, hp * 2 * HEAD_DIM:(hp + 1) * 2 * HEAD_DIM] for hp in pairs], axis=0)


def _from_previous_block(rows):
    row = lax.broadcasted_iota(jnp.int32, (ATTN_BLOCK, ATTN_BLOCK), 0)
    col = lax.broadcasted_iota(jnp.int32, (ATTN_BLOCK, ATTN_BLOCK), 1)
    return jnp.concatenate([col > row] * (rows // ATTN_BLOCK), axis=0)


def _folded_probs(n, q_groups, k_prev_groups, k_cur_groups, sink_ref, prev_part):
    def scores(q, k):
        return lax.dot_general(q, k, (((1,), (1,)), ((), ())), preferred_element_type=F32)

    s_prev = jnp.concatenate([scores(q, k[i]) for q, k in zip(q_groups, k_prev_groups) for i in range(2)], axis=0)
    s_cur = jnp.concatenate([scores(q, k[i]) for q, k in zip(q_groups, k_cur_groups) for i in range(2)], axis=0)
    s_prev = jnp.where(n > 0, s_prev, MASK_VALUE * (HEAD_DIM ** 0.5))
    s = jnp.where(prev_part, s_prev, s_cur) * (HEAD_DIM ** -0.5)
    heads = [h for g in range(N_KV_HEADS) for h in _group_heads(g)]
    sink = jnp.concatenate([jnp.broadcast_to(sink_ref[0:1, h:h + 1], (ATTN_BLOCK, 1)) for h in heads], axis=0)
    m = jnp.maximum(jnp.max(s, axis=1, keepdims=True), sink)
    p = jnp.exp(s - m)
    e_sink = jnp.exp(sink - m)
    inv = 1.0 / (jnp.sum(p, axis=1, keepdims=True) + e_sink)
    return p * inv, e_sink * inv


def _split_folded(t, prev_part):
    tb = t.astype(BF16)
    zero = jnp.zeros_like(tb)
    return jnp.where(prev_part, tb, zero), jnp.where(prev_part, zero, tb)


def _attn_specs(T):
    nb = T // ATTN_BLOCK
    kcol = N_Q_HEADS * HEAD_DIM // V7X_LANES
    cur = lambda n: jnp.minimum(n, nb - 1)
    prev = lambda n: jnp.maximum(jnp.minimum(n, nb - 1) - 1, 0)
    q_spec = pl.BlockSpec((ATTN_BLOCK, N_Q_HEADS * HEAD_DIM), lambda n: (cur(n), 0))
    kc_spec = pl.BlockSpec((ATTN_BLOCK, V7X_LANES), lambda n: (cur(n), kcol))
    kp_spec = pl.BlockSpec((ATTN_BLOCK, V7X_LANES), lambda n: (prev(n), kcol))
    vc_spec = pl.BlockSpec((ATTN_BLOCK, V7X_LANES), lambda n: (cur(n), kcol + 1))
    vp_spec = pl.BlockSpec((ATTN_BLOCK, V7X_LANES), lambda n: (prev(n), kcol + 1))
    return q_spec, kc_spec, kp_spec, vc_spec, vp_spec


def _attn_fwd(qkv, sinks):
    T = qkv.shape[0]
    nb = T // ATTN_BLOCK
    qw = N_Q_HEADS * HEAD_DIM
    all_rows = N_Q_HEADS * ATTN_BLOCK

    def body(q_ref, kc_ref, kp_ref, vc_ref, vp_ref, sink_ref, o_ref, probs_ref, psink_ref):
        n = pl.program_id(0)
        prev_part = _from_previous_block(all_rows)
        half = PAIRS_PER_KV * ATTN_BLOCK
        groups = range(N_KV_HEADS)
        probs, p_sink = _folded_probs(n, [_pair_rows(q_ref, g) for g in groups],
                                      [_kv_operands(g, kp_ref[...]) for g in groups],
                                      [_kv_operands(g, kc_ref[...]) for g in groups], sink_ref, prev_part)
        probs_ref[...] = probs.astype(BF16)
        lane = lax.broadcasted_iota(jnp.int32, (ATTN_BLOCK, V7X_LANES), 1)
        sink_tile = jnp.zeros((ATTN_BLOCK, V7X_LANES), F32)
        for i, h in enumerate(_all_heads()):
            sink_tile = jnp.where(lane == h, p_sink[i * ATTN_BLOCK:(i + 1) * ATTN_BLOCK], sink_tile)
        psink_ref[...] = sink_tile
        p_prev, p_cur = _split_folded(probs, prev_part)
        for g in groups:
            v_prev, v_cur = _kv_operands(g, vp_ref[...]), _kv_operands(g, vc_ref[...])
            even, odd = slice(2 * g * half, (2 * g + 1) * half), slice((2 * g + 1) * half, (2 * g + 2) * half)
            o = (jnp.dot(p_prev[even], v_prev[0], preferred_element_type=F32)
                 + jnp.dot(p_cur[even], v_cur[0], preferred_element_type=F32)
                 + jnp.dot(p_prev[odd], v_prev[1], preferred_element_type=F32)
                 + jnp.dot(p_cur[odd], v_cur[1], preferred_element_type=F32))
            for i in range(PAIRS_PER_KV):
                hp = g * PAIRS_PER_KV + i
                o_ref[:, hp * 2 * HEAD_DIM:(hp + 1) * 2 * HEAD_DIM] = (
                    o[i * ATTN_BLOCK:(i + 1) * ATTN_BLOCK].astype(BF16))

    return pl.pallas_call(
        body, name="attn_fwd", grid=(nb,),
        in_specs=[*_attn_specs(T), _whole((1, N_Q_HEADS))],
        out_specs=[_rows(ATTN_BLOCK, qw), pl.BlockSpec((None, all_rows, ATTN_BLOCK), lambda n: (n, 0, 0)),
                   _rows(ATTN_BLOCK, V7X_LANES)],
        out_shape=[SDS((T, qw), BF16), SDS((nb, all_rows, ATTN_BLOCK), BF16), SDS((T, V7X_LANES), F32)],
        compiler_params=_params("parallel"),
    )(qkv, qkv, qkv, qkv, qkv, sinks)


def _mm_res(name, a, w, b, res, g):
    T, K = a.shape
    D = w.shape[1]
    tm = min(512, T)

    def body(a_ref, w_ref, b_ref, r_ref, g_ref, o_ref, f_ref):
        h = jnp.dot(a_ref[...], w_ref[...], preferred_element_type=F32) + b_ref[...] + r_ref[...]
        o_ref[...] = h
        f_ref[...] = (h * _rms_rstd(h) * g_ref[...]).astype(BF16)

    return pl.pallas_call(
        body, name=name, grid=(T // tm,),
        in_specs=[_rows(tm, K), _whole((K, D)), _whole((1, D)), _rows(tm, D), _whole((1, D))],
        out_specs=[_rows(tm, D), _rows(tm, D)],
        out_shape=[SDS((T, D), F32), SDS((T, D), BF16)],
        compiler_params=_params("parallel"),
    )(a, w, b, res, g)


def _ffn_down(name, s, w2, layer, res):
    _, T, n = s.shape
    D = w2.shape[3]
    tm = min(512, T)

    def body(s_ref, w_ref, r_ref, o_ref):
        acc = r_ref[...]
        for j in range(N_CHIPS):
            acc = acc + jnp.dot(s_ref[j], w_ref[j], preferred_element_type=F32)
        o_ref[...] = acc

    return pl.pallas_call(
        body, name=name, grid=(T // tm,),
        in_specs=[pl.BlockSpec((N_CHIPS, tm, n), lambda i: (0, i, 0)),
                  pl.BlockSpec((N_CHIPS, None, n, D), lambda i: (0, layer, 0, 0)), _rows(tm, D)],
        out_specs=_rows(tm, D),
        out_shape=SDS((T, D), F32),
        compiler_params=_params("parallel"),
    )(s, w2, res)


def _ffn_up(name, f, w1, w3, layer, after=None):
    T, D = f.shape
    n = w1.shape[2]
    tm = min(1024, T)

    def body(f_ref, w1_ref, w3_ref, *rest):
        act_ref, gg_ref, s_ref = rest[-3:]
        ff = f_ref[...]
        g1 = _dot_nt(ff, w1_ref[...])
        g3 = _dot_nt(ff, w3_ref[...])
        act, dact = _silu_and_grad(g1)
        act_ref[...] = act.astype(BF16)
        gg_ref[...] = (g3 * dact).astype(BF16)
        s_ref[...] = (act * g3).astype(BF16)

    slab = pl.BlockSpec((None, tm, n), lambda j, i: (j, i, 0))
    wslab = pl.BlockSpec((None, None, n, D), lambda j, i: (j, layer, 0, 0))
    hidden = SDS((N_CHIPS, T, n), BF16)
    return pl.pallas_call(
        body, name=name, grid=(N_CHIPS, T // tm),
        in_specs=[pl.BlockSpec((tm, D), lambda j, i: (i, 0)), wslab, wslab]
        + ([] if after is None else [pl.BlockSpec(memory_space=pl.ANY)]),
        out_specs=[slab, slab, slab],
        out_shape=[hidden, hidden, hidden],
        compiler_params=_params("parallel", "parallel"),
    )(f, w1, w3, *([] if after is None else [after]))


def _glu(a, d):
    return a[:, :d] * jax.nn.sigmoid(a[:, d:])


def _conv_tile(T):
    return min(256, T)


def _fill_shifted(sh_ref, tc):
    n = tc + CONV_HALO - V7X_SUBLANES
    for r in range(1, V7X_SUBLANES):
        sh_ref[r, 0:n, :] = sh_ref[0, pl.ds(r, n), :]


def _depthwise_taps(sh_ref, w_ref, offsets, bias_ref, out_ref, tc):
    D = out_ref.shape[1]

    def chunk(i, carry):
        t0 = pl.multiple_of(i * CONV_ROW_CHUNK, CONV_ROW_CHUNK)
        for cb in range(D // CONV_LANE_CHUNK):
            cs = slice(cb * CONV_LANE_CHUNK, (cb + 1) * CONV_LANE_CHUNK)
            acc = jnp.zeros((CONV_ROW_CHUNK, CONV_LANE_CHUNK), F32)
            for r in range(V7X_SUBLANES):
                taps = [(j, o // V7X_SUBLANES) for j, o in enumerate(offsets) if o % V7X_SUBLANES == r]
                if not taps:
                    continue
                span = CONV_ROW_CHUNK + V7X_SUBLANES * max(q for _, q in taps)
                rows = sh_ref[r, pl.ds(t0, span), cs]
                for j, q in taps:
                    acc = acc + rows[V7X_SUBLANES * q:V7X_SUBLANES * q + CONV_ROW_CHUNK] * w_ref[j:j + 1, cs]
            if bias_ref is not None:
                acc = acc + bias_ref[:, cs]
            out_ref[pl.ds(t0, CONV_ROW_CHUNK), cs] = acc
        return carry

    lax.fori_loop(0, tc // CONV_ROW_CHUNK, chunk, 0)


def _depthwise_tap_grads(dy_sh, x_sh, offsets, dw_ref, tc):
    D = dw_ref.shape[1]
    for cb in range(D // V7X_LANES):
        cs = slice(cb * V7X_LANES, (cb + 1) * V7X_LANES)

        def row_tiles(i, accs, cs=cs):
            for k in range(CONV_GRAD_UNROLL):
                t0 = pl.multiple_of(i * (CONV_GRAD_UNROLL * V7X_SUBLANES), V7X_SUBLANES) + k * V7X_SUBLANES
                d = dy_sh[0, pl.ds(t0, V7X_SUBLANES), cs]
                accs = tuple(
                    acc + d * x_sh[o % V7X_SUBLANES, pl.ds(t0 + o // V7X_SUBLANES * V7X_SUBLANES, V7X_SUBLANES), cs]
                    for acc, o in zip(accs, offsets))
            return accs

        zero = jnp.zeros((V7X_SUBLANES, V7X_LANES), F32)
        accs = lax.fori_loop(0, tc // (CONV_GRAD_UNROLL * V7X_SUBLANES), row_tiles, tuple(zero for _ in offsets))
        for j, acc in enumerate(accs):
            dw_ref[j:j + 1, cs] += jnp.sum(acc, axis=0, keepdims=True)


def _conv_fwd(a, w_dw, b_dw, ln_g, ln_b, w_pw2, b_pw2, res, g_next):
    T = a.shape[0]
    D = a.shape[1] // 2
    tc = _conv_tile(T)
    per = tc // CONV_HALO

    def body(a_ref, ah_ref, w_ref, bdw_ref, lg_ref, lb_ref, wp_ref, bp_ref, r_ref, g_ref,
             c_ref, act_ref, h_ref, f_ref, u_sh):
        i = pl.program_id(0)
        u_sh[0, 0:CONV_HALO, :] = jnp.where(i > 0, _glu(ah_ref[...], D), 0.0)
        u_sh[0, CONV_HALO:, :] = _glu(a_ref[...], D)
        _fill_shifted(u_sh, tc)
        _depthwise_taps(u_sh, w_ref, [CONV_FIRST_TAP + j for j in range(CONV_WIDTH)], bdw_ref, c_ref, tc)
        c = c_ref[...]
        xc = c - jnp.mean(c, axis=-1, keepdims=True)
        z = xc * lax.rsqrt(jnp.mean(xc * xc, axis=-1, keepdims=True) + LN_EPS)
        l = z * lg_ref[...] + lb_ref[...]
        act = (l * jax.nn.sigmoid(l)).astype(BF16)
        act_ref[...] = act
        h = jnp.dot(act, wp_ref[...], preferred_element_type=F32) + bp_ref[...] + r_ref[...]
        h_ref[...] = h
        f_ref[...] = (h * _rms_rstd(h) * g_ref[...]).astype(BF16)

    return pl.pallas_call(
        body, name="conv_fwd", grid=(T // tc,),
        in_specs=[_rows(tc, 2 * D),
                  pl.BlockSpec((CONV_HALO, 2 * D), lambda i: (jnp.maximum(i * per - 1, 0), 0)),
                  _whole((CONV_WIDTH, D)), _whole((1, D)), _whole((1, D)), _whole((1, D)),
                  _whole((D, D)), _whole((1, D)), _rows(tc, D), _whole((1, D))],
        out_specs=[_rows(tc, D), _rows(tc, D), _rows(tc, D), _rows(tc, D)],
        out_shape=[SDS((T, D), F32), SDS((T, D), BF16), SDS((T, D), F32), SDS((T, D), BF16)],
        scratch_shapes=[pltpu.VMEM((V7X_SUBLANES, tc + CONV_HALO, D), F32)],
        compiler_params=_params("parallel"),
    )(a, a, w_dw, b_dw, ln_g, ln_b, w_pw2, b_pw2, res, g_next)


def _ffn_down_loss(name, s, w2, layer, res, g, target):
    _, T, n = s.shape
    D = w2.shape[3]
    tm = min(512, T)

    def body(s_ref, w_ref, r_ref, g_ref, t_ref, dh_ref, loss_ref, dg_ref):
        @pl.when(pl.program_id(0) == 0)
        def _():
            loss_ref[...] = jnp.zeros_like(loss_ref)
            dg_ref[...] = jnp.zeros_like(dg_ref)

        hh = r_ref[...]
        for j in range(N_CHIPS):
            hh = hh + jnp.dot(s_ref[j], w_ref[j], preferred_element_type=F32)
        r = _rms_rstd(hh)
        g = g_ref[...]
        d = hh * r * g - t_ref[...]
        loss_ref[...] += 0.5 * jnp.sum(jnp.mean(d * d, axis=-1, keepdims=True), axis=0, keepdims=True)
        dout = d * (1.0 / D)
        dg_ref[...] += jnp.sum(dout * (hh * r), axis=0, keepdims=True)
        dxh = dout * g
        dh_ref[...] = r * dxh - hh * (r * r * r) * jnp.mean(dxh * hh, axis=-1, keepdims=True)

    return pl.pallas_call(
        body, name=name, grid=(T // tm,),
        in_specs=[pl.BlockSpec((N_CHIPS, tm, n), lambda i: (0, i, 0)),
                  pl.BlockSpec((N_CHIPS, None, n, D), lambda i: (0, layer, 0, 0)), _rows(tm, D),
                  _whole((1, D)), _rows(tm, D)],
        out_specs=[_rows(tm, D), _whole((1, 1)), _whole((1, D))],
        out_shape=[SDS((T, D), F32), SDS((1, 1), F32), SDS((1, D), F32)],
        compiler_params=_params("arbitrary"),
    )(s, w2, res, g, target)


def _ffn_bwd_down(name, dh, w2, layer, act, gate_grad, s, after=None):
    T, D = dh.shape
    n = w2.shape[2]
    tm = min(256, T)

    def body(dh_ref, w2_ref, act_ref, gg_ref, s_ref, *rest):
        dg1_ref, dg3_ref, dw_ref = rest[-3:]

        @pl.when(pl.program_id(0) == 0)
        def _():
            dw_ref[...] = jnp.zeros_like(dw_ref)

        dhb = dh_ref[...].astype(BF16)
        for j in range(N_CHIPS):
            ds = _dot_nt(dhb, w2_ref[j])
            dg1_ref[j] = (ds * gg_ref[j].astype(F32)).astype(BF16)
            dg3_ref[j] = (ds * act_ref[j].astype(F32)).astype(BF16)
            dw_ref[j] += _dot_tn(s_ref[j], dhb)

    slabs = pl.BlockSpec((N_CHIPS, tm, n), lambda i: (0, i, 0))
    hidden = SDS((N_CHIPS, T, n), BF16)
    return pl.pallas_call(
        body, name=name, grid=(T // tm,),
        in_specs=[_rows(tm, D),
                  pl.BlockSpec((N_CHIPS, None, n, D), lambda i: (0, layer, 0, 0), pipeline_mode=pl.Buffered(1)),
                  slabs, slabs, slabs] + ([] if after is None else [pl.BlockSpec(memory_space=pl.ANY)]),
        out_specs=[slabs, slabs, _whole((N_CHIPS, n, D))],
        out_shape=[hidden, hidden, SDS((N_CHIPS, n, D), F32)],
        compiler_params=_params("arbitrary"),
    )(dh, w2, act, gate_grad, s, *([] if after is None else [after]))


def _dot_tn(a, b):
    return lax.dot_general(a.astype(BF16), b.astype(BF16), (((0,), (0,)), ((), ())), preferred_element_type=F32)


def _dot_nt(a, b):
    return lax.dot_general(a.astype(BF16), b, (((1,), (1,)), ((), ())), preferred_element_type=F32)


def _mm_tn(name, a, b, col_chunks=1, after=None):
    a_slabs, b_slabs = a.ndim == 3, b.ndim == 3
    T = a.shape[-2]
    tt = min(1024, T)
    ka, nb = a.shape[-1], b.shape[-1]
    if a_slabs or b_slabs:
        out_dims = (N_CHIPS, ka, nb)
    elif col_chunks > 1:
        out_dims = (col_chunks, ka, nb // col_chunks)
    else:
        out_dims = (ka, nb)

    def body(a_ref, b_ref, *rest):
        o_ref = rest[-1]

        @pl.when(pl.program_id(0) == 0)
        def _():
            o_ref[...] = jnp.zeros_like(o_ref)

        if a_slabs:
            bb = b_ref[...].astype(BF16)
            for j in range(N_CHIPS):
                o_ref[j] += _dot_tn(a_ref[j], bb)
        elif b_slabs:
            aa = a_ref[...].astype(BF16)
            for j in range(N_CHIPS):
                o_ref[j] += _dot_tn(aa, b_ref[j])
        elif col_chunks > 1:
            aa = a_ref[...].astype(BF16)
            w = nb // col_chunks
            for j in range(col_chunks):
                o_ref[j] += _dot_tn(aa, b_ref[:, j * w:(j + 1) * w])
        else:
            o_ref[...] += _dot_tn(a_ref[...], b_ref[...])

    def spec(arr, slabs):
        if slabs:
            return pl.BlockSpec((N_CHIPS, tt, arr.shape[-1]), lambda t: (0, t, 0))
        return _rows(tt, arr.shape[-1])

    return pl.pallas_call(
        body, name=name, grid=(T // tt,),
        in_specs=[spec(a, a_slabs), spec(b, b_slabs)] + ([] if after is None else [pl.BlockSpec(memory_space=pl.ANY)]),
        out_specs=_whole(out_dims),
        out_shape=SDS(out_dims, F32),
        compiler_params=_params("arbitrary"),
    )(a, b, *([] if after is None else [after]))


def _mm_nt_normbwd(name, pairs, h, g, dh, after):
    T, D = h.shape
    tm = min(512, T)
    n_pairs = len(pairs)
    kinds = ["slabs" if dy.ndim == 3 else ("quarters" if w.ndim == 3 else "plain") for dy, w, _ in pairs]

    def body(*refs):
        dy_refs = refs[:n_pairs]
        w_refs = refs[n_pairs:2 * n_pairs]
        h_ref, g_ref, dh_ref, _, o_ref, dg_ref, cs_ref = refs[2 * n_pairs:]

        @pl.when(pl.program_id(0) == 0)
        def _():
            dg_ref[...] = jnp.zeros_like(dg_ref)
            cs_ref[...] = jnp.zeros_like(cs_ref)

        df = jnp.zeros((tm, D), F32)
        for dy_ref, w_ref, kd in zip(dy_refs, w_refs, kinds):
            if kd == "slabs":
                for j in range(N_CHIPS):
                    df = df + jnp.dot(dy_ref[j], w_ref[j], preferred_element_type=F32)
            elif kd == "quarters":
                n = w_ref.shape[2]
                for j in range(N_CHIPS):
                    df = df + _dot_nt(dy_ref[:, j * n:(j + 1) * n], w_ref[j])
            else:
                df = df + jnp.dot(dy_ref[...], w_ref[...], preferred_element_type=F32)
        hh = h_ref[...]
        r = _rms_rstd(hh)
        dg_ref[...] += jnp.sum(df * (hh * r), axis=0, keepdims=True)
        dxh = df * g_ref[...]
        out = dh_ref[...] + (r * dxh - hh * (r * r * r) * jnp.mean(dxh * hh, axis=-1, keepdims=True))
        o_ref[...] = out
        cs_ref[...] += jnp.sum(out, axis=0, keepdims=True)

    dy_specs, w_specs = [], []
    for (dy, w, layer), kd in zip(pairs, kinds):
        if kd == "slabs":
            dy_specs.append(pl.BlockSpec((N_CHIPS, tm, dy.shape[2]), lambda i: (0, i, 0)))
            w_specs.append(pl.BlockSpec((N_CHIPS, None, w.shape[2], D),
                                        functools.partial(lambda i, layer: (0, layer, 0, 0), layer=layer),
                                        pipeline_mode=pl.Buffered(1)))
        else:
            dy_specs.append(_rows(tm, dy.shape[1]))
            w_specs.append(_whole(w.shape))

    return pl.pallas_call(
        body, name=name, grid=(T // tm,),
        in_specs=[*dy_specs, *w_specs, _rows(tm, D), _whole((1, D)), _rows(tm, D), pl.BlockSpec(memory_space=pl.ANY)],
        out_specs=[_rows(tm, D), _whole((1, D)), _whole((1, D))],
        out_shape=[SDS((T, D), F32), SDS((1, D), F32), SDS((1, D), F32)],
        compiler_params=_params("arbitrary"),
    )(*[dy for dy, _, _ in pairs], *[w for _, w, _ in pairs], h, g, dh, after)


def _mm_nt(name, dy, w, out_dtype):
    T, N = dy.shape
    K = w.shape[0]
    tm = min(512, T)

    def body(dy_ref, w_ref, o_ref):
        o_ref[...] = lax.dot_general(dy_ref[...].astype(BF16), w_ref[...], (((1,), (1,)), ((), ())),
                                     preferred_element_type=F32).astype(out_dtype)

    return pl.pallas_call(
        body, name=name, grid=(T // tm,),
        in_specs=[_rows(tm, N), _whole((K, N))],
        out_specs=_rows(tm, K),
        out_shape=SDS((T, K), out_dtype),
        compiler_params=_params("parallel"),
    )(dy, w)


def _conv_bwd(dh, w_pw2, c, a, w_dw, ln_g, ln_b):
    T, D = c.shape
    tc = _conv_tile(T)
    per = tc // CONV_HALO
    n_tiles = T // tc
    last_halo = T // CONV_HALO - 1

    def ln_bwd(dact_v, c_v, lg, lb):
        xc = c_v - jnp.mean(c_v, axis=-1, keepdims=True)
        rstd = lax.rsqrt(jnp.mean(xc * xc, axis=-1, keepdims=True) + LN_EPS)
        z = xc * rstd
        _, dsilu = _silu_and_grad(z * lg + lb)
        dl = dact_v * dsilu
        dz = dl * lg
        dc = rstd * (dz - jnp.mean(dz, axis=-1, keepdims=True) - z * jnp.mean(dz * z, axis=-1, keepdims=True))
        return dc, dl, z

    def body(dh_ref, dhn_ref, wp_ref, c_ref, cn_ref, a_ref, ah_ref, w_ref, lg_ref, lb_ref,
             da_ref, dlg_ref, dlb_ref, dbdw_ref, dwdw_ref, dbpw1_ref, dc_sh, u_sh, du_scr):
        i = pl.program_id(0)

        @pl.when(i == 0)
        def _():
            for ref in (dlg_ref, dlb_ref, dbdw_ref, dwdw_ref, dbpw1_ref):
                ref[...] = jnp.zeros_like(ref)

        lg, lb = lg_ref[...], lb_ref[...]
        dh_rows = jnp.concatenate([dh_ref[...].astype(BF16), dhn_ref[...].astype(BF16)], axis=0)
        dact = _dot_nt(dh_rows, wp_ref[...])
        dc, dl, z = ln_bwd(dact[:tc], c_ref[...], lg, lb)
        dlg_ref[...] += jnp.sum(dl * z, axis=0, keepdims=True)
        dlb_ref[...] += jnp.sum(dl, axis=0, keepdims=True)
        dbdw_ref[...] += jnp.sum(dc, axis=0, keepdims=True)
        dcn, _, _ = ln_bwd(dact[tc:], cn_ref[...], lg, lb)
        dc_sh[0, 0:tc, :] = dc
        dc_sh[0, tc:, :] = jnp.where(i < n_tiles - 1, dcn, 0.0)
        _fill_shifted(dc_sh, tc)

        a_v = a_ref[...]
        a1 = a_v[:, :D]
        sg = jax.nn.sigmoid(a_v[:, D:])
        u_sh[0, 0:CONV_HALO, :] = jnp.where(i > 0, _glu(ah_ref[...], D), 0.0)
        u_sh[0, CONV_HALO:, :] = a1 * sg
        _fill_shifted(u_sh, tc)

        _depthwise_taps(dc_sh, w_ref, [CONV_WIDTH - 1 - j for j in range(CONV_WIDTH)], None, du_scr, tc)
        _depthwise_tap_grads(dc_sh, u_sh, [CONV_FIRST_TAP + j for j in range(CONV_WIDTH)], dwdw_ref, tc)

        du = du_scr[...]
        da1 = du * sg
        da2 = du * a1 * sg * (1.0 - sg)
        da_ref[:, :D] = da1.astype(BF16)
        da_ref[:, D:] = da2.astype(BF16)
        dbpw1_ref[:, :D] += jnp.sum(da1, axis=0, keepdims=True)
        dbpw1_ref[:, D:] += jnp.sum(da2, axis=0, keepdims=True)

    nxt = lambda i: (jnp.minimum((i + 1) * per, last_halo), 0)
    return pl.pallas_call(
        body, name="conv_bwd", grid=(n_tiles,),
        in_specs=[_rows(tc, D), pl.BlockSpec((CONV_HALO, D), nxt), _whole((D, D)),
                  _rows(tc, D), pl.BlockSpec((CONV_HALO, D), nxt),
                  _rows(tc, 2 * D),
                  pl.BlockSpec((CONV_HALO, 2 * D), lambda i: (jnp.maximum(i * per - 1, 0), 0)),
                  _whole((CONV_WIDTH, D)), _whole((1, D)), _whole((1, D))],
        out_specs=[_rows(tc, 2 * D), _whole((1, D)), _whole((1, D)), _whole((1, D)),
                   _whole((CONV_HALO, D)), _whole((1, 2 * D))],
        out_shape=[SDS((T, 2 * D), BF16), SDS((1, D), F32), SDS((1, D), F32), SDS((1, D), F32),
                   SDS((CONV_HALO, D), F32), SDS((1, 2 * D), F32)],
        scratch_shapes=[pltpu.VMEM((V7X_SUBLANES, tc + CONV_HALO, D), F32),
                        pltpu.VMEM((V7X_SUBLANES, tc + CONV_HALO, D), F32), pltpu.VMEM((tc, D), F32)],
        compiler_params=_params("arbitrary"),
    )(dh, dh, w_pw2, c, c, a, a, w_dw, ln_g, ln_b)


def _attn_bwd(qkv, dao, cos, sin, probs_saved, psink_saved):
    T = qkv.shape[0]
    nb = T // ATTN_BLOCK
    qw = N_Q_HEADS * HEAD_DIM
    kw = N_KV_HEADS * HEAD_DIM

    def body(q_ref, kc_ref, kp_ref, vc_ref, vp_ref, do_ref, cos_ref, sin_ref, cosp_ref, sinp_ref, probs_ref, psink_ref,
             dq_ref, dkv_ref, dsink_ref, dbq_ref, dbkv_ref, carry, prev_scr, cur_scr, dq_scr):
        n = pl.program_id(0)

        @pl.when(n == 0)
        def _():
            for ref in (dsink_ref, dbq_ref, dbkv_ref, carry):
                ref[...] = jnp.zeros_like(ref)

        @pl.when(n == nb)
        def _():
            prev_scr[...] = jnp.zeros_like(prev_scr)

        @pl.when(n < nb)
        def _():
            prev_part = _from_previous_block(N_Q_HEADS * ATTN_BLOCK)
            half = PAIRS_PER_KV * ATTN_BLOCK
            upper = _upper_lanes((ATTN_BLOCK, 2 * HEAD_DIM))
            groups = range(N_KV_HEADS)

            def nt(a, b):
                return lax.dot_general(a, b, (((1,), (1,)), ((), ())), preferred_element_type=F32)

            def kv_grad(even_rows, odd_rows, x):
                even = lax.dot_general(even_rows, x, (((0,), (0,)), ((), ())), preferred_element_type=F32)
                odd = lax.dot_general(odd_rows, x, (((0,), (0,)), ((), ())), preferred_element_type=F32)
                t = jnp.where(upper, odd, even)
                return t + _swap_lane_halves(t)

            q = [_pair_rows(q_ref, g) for g in groups]
            do = [_pair_rows(do_ref, g) for g in groups]
            k_prev = [_kv_operands(g, kp_ref[...]) for g in groups]
            k_cur = [_kv_operands(g, kc_ref[...]) for g in groups]
            v_prev = [_kv_operands(g, vp_ref[...]) for g in groups]
            v_cur = [_kv_operands(g, vc_ref[...]) for g in groups]
            probs = probs_ref[...].astype(F32)
            dp_prev = jnp.concatenate([nt(do[g], v_prev[g][i]) for g in groups for i in range(2)], axis=0)
            dp_cur = jnp.concatenate([nt(do[g], v_cur[g][i]) for g in groups for i in range(2)], axis=0)
            dp = jnp.where(prev_part, dp_prev, dp_cur)
            delta = jnp.sum(probs * dp, axis=1, keepdims=True)
            ds_prev, ds_cur = _split_folded(probs * (dp - delta) * (HEAD_DIM ** -0.5), prev_part)
            p_prev, p_cur = _split_folded(probs_ref[...], prev_part)
            for i, h in enumerate(_all_heads()):
                rows = slice(i * ATTN_BLOCK, (i + 1) * ATTN_BLOCK)
                dsink_ref[:, h:h + 1] += jnp.sum(-(psink_ref[:, h:h + 1] * delta[rows]), axis=0, keepdims=True)
            kv_grads = []
            for g in groups:
                even, odd = slice(2 * g * half, (2 * g + 1) * half), slice((2 * g + 1) * half, (2 * g + 2) * half)
                dq = (jnp.dot(ds_prev[even], k_prev[g][0], preferred_element_type=F32)
                      + jnp.dot(ds_cur[even], k_cur[g][0], preferred_element_type=F32)
                      + jnp.dot(ds_prev[odd], k_prev[g][1], preferred_element_type=F32)
                      + jnp.dot(ds_cur[odd], k_cur[g][1], preferred_element_type=F32))
                for i in range(PAIRS_PER_KV):
                    hp = g * PAIRS_PER_KV + i
                    dq_scr[:, hp * 2 * HEAD_DIM:(hp + 1) * 2 * HEAD_DIM] = dq[i * ATTN_BLOCK:(i + 1) * ATTN_BLOCK]
                kv_grads.append((kv_grad(ds_prev[even], ds_prev[odd], q[g]), kv_grad(ds_cur[even], ds_cur[odd], q[g]),
                                 kv_grad(p_prev[even], p_prev[odd], do[g]), kv_grad(p_cur[even], p_cur[odd], do[g])))
            (dkp0, dkc0, dvp0, dvc0), (dkp1, dkc1, dvp1, dvc1) = kv_grads
            prev_scr[:, :kw] = jnp.where(upper, dkp1, dkp0)
            prev_scr[:, kw:] = jnp.where(upper, dvp1, dvp0)
            cur_scr[:, :kw] = jnp.where(upper, dkc1, dkc0)
            cur_scr[:, kw:] = jnp.where(upper, dvc1, dvc0)
            dq_pre = _rope_transposed(dq_scr[...], cos_ref, sin_ref)
            dq_ref[...] = dq_pre.astype(BF16)
            dbq_ref[...] += jnp.sum(dq_pre, axis=0, keepdims=True)

        tot = carry[...] + prev_scr[...]
        dk_pre = _rope_transposed(tot[:, :kw], cosp_ref, sinp_ref)
        dkv_ref[:, :kw] = dk_pre.astype(BF16)
        dkv_ref[:, kw:] = tot[:, kw:].astype(BF16)
        dbkv_ref[:, :kw] += jnp.sum(dk_pre, axis=0, keepdims=True)
        dbkv_ref[:, kw:] += jnp.sum(tot[:, kw:], axis=0, keepdims=True)

        @pl.when(n < nb)
        def _():
            carry[...] = cur_scr[...]

    cur = lambda n: (jnp.minimum(n, nb - 1), 0)
    out_lag = lambda n: (jnp.maximum(n - 1, 0), 0)
    return pl.pallas_call(
        body, name="attn_bwd", grid=(nb + 1,),
        in_specs=[*_attn_specs(T),
                  pl.BlockSpec((ATTN_BLOCK, qw), cur),
                  pl.BlockSpec((ATTN_BLOCK, V7X_LANES), cur), pl.BlockSpec((ATTN_BLOCK, V7X_LANES), cur),
                  pl.BlockSpec((ATTN_BLOCK, V7X_LANES), out_lag), pl.BlockSpec((ATTN_BLOCK, V7X_LANES), out_lag),
                  pl.BlockSpec((None, N_Q_HEADS * ATTN_BLOCK, ATTN_BLOCK), lambda n: (jnp.minimum(n, nb - 1), 0, 0)),
                  pl.BlockSpec((ATTN_BLOCK, V7X_LANES), cur)],
        out_specs=[pl.BlockSpec((ATTN_BLOCK, qw), cur), pl.BlockSpec((ATTN_BLOCK, 2 * kw), out_lag),
                   _whole((1, N_Q_HEADS)), _whole((1, qw)), _whole((1, 2 * kw))],
        out_shape=[SDS((T, qw), BF16), SDS((T, 2 * kw), BF16),
                   SDS((1, N_Q_HEADS), F32), SDS((1, qw), F32), SDS((1, 2 * kw), F32)],
        scratch_shapes=[pltpu.VMEM((ATTN_BLOCK, 2 * kw), F32), pltpu.VMEM((ATTN_BLOCK, 2 * kw), F32),
                        pltpu.VMEM((ATTN_BLOCK, 2 * kw), F32), pltpu.VMEM((ATTN_BLOCK, qw), F32)],
        compiler_params=_params("arbitrary"),
    )(qkv, qkv, qkv, qkv, qkv, dao, cos, sin, cos, sin, probs_saved, psink_saved)


def _local_step(x, target, p, reduce_begin, reduce_send):
    T, D = x.shape
    cos, sin = _rope_tables(T)
    qw = N_Q_HEADS * HEAD_DIM
    nm, nf = p["norm_mix"], p["norm_ffn"]

    y0, qkv = _qkv_proj(x, nm[0:1], p["attn_w_qkv"], p["attn_b_qkv"], cos, sin, p["gather_started"])
    ao, attn_probs, sink_probs = _attn_fwd(qkv, p["attn_sinks"])
    h1, f0 = _mm_res("attn_out", ao, p["attn_w_o"], p["attn_b_o"], x, nf[0:1])
    p = {**p, **p["other_weights"](h1)}
    w1, w3 = p["ffn_w1"], p["ffn_w3"]
    act0, gg0, s0 = _ffn_up("ffn0_up", f0, w1, w3, 0, after=p["swap_started"])
    p = {**p, **p["rest_of_weights"](s0)}
    w2 = p["ffn_w2"]
    h2 = _ffn_down("ffn0_down", s0, w2, 0, h1)
    y1, a = _pw1_proj(h2, nm[1:2], p["conv_w_pw1"], p["conv_b_pw1"])
    c, act, h3, f1 = _conv_fwd(a, p["conv_w_dw"], p["conv_b_dw"], p["conv_ln_g"], p["conv_ln_b"],
                               p["conv_w_pw2"], p["conv_b_pw2"], h2, nf[1:2])
    act1, gg1, s1 = _ffn_up("ffn1_up", f1, w1, w3, 1)
    dh4, loss, d_norm_final = _ffn_down_loss("ffn1_down_loss", s1, w2, 1, h3, p["norm_final"], target)

    g = {}
    dg1, dg3, dw2_1 = _ffn_bwd_down("ffn1_bwd_down", dh4, w2, 1, act1, gg1, s1)
    dw1_1 = _mm_tn("ffn1_dw1", dg1, f1)
    dw3_1 = _mm_tn("ffn1_dw3", dg3, f1)
    begun = reduce_begin("ffn1", {("ffn_w1", 1): dw1_1, ("ffn_w3", 1): dw3_1, ("ffn_w2", 1): dw2_1})
    dh3, dnf1, db_pw2 = _mm_nt_normbwd("ffn1_bwd_in", [(dg1, w1, 1), (dg3, w3, 1)], h3, nf[1:2], dh4, begun)
    sent = reduce_send("ffn1", dh3)

    dw_pw2 = _mm_tn("conv_dw_pw2", act, dh3, after=sent)
    da, d_ln_g, d_ln_b, d_b_dw, d_w_dw, d_b_pw1 = _conv_bwd(dh3, p["conv_w_pw2"], c, a, p["conv_w_dw"],
                                                            p["conv_ln_g"], p["conv_ln_b"])
    dw_pw1 = _mm_tn("conv_dw_pw1", y1, da, col_chunks=N_CHIPS)
    begun = reduce_begin("conv", {("conv_w_pw2", 0): dw_pw2.reshape(N_CHIPS, -1, D), ("conv_w_pw1", 0): dw_pw1})
    dh2, dnm1, _ = _mm_nt_normbwd("conv_bwd_in", [(da, p["conv_w_pw1"], None)], h2, nm[1:2], dh3, begun)
    sent = reduce_send("conv", dh2)

    dg1, dg3, dw2_0 = _ffn_bwd_down("ffn0_bwd_down", dh2, w2, 0, act0, gg0, s0, after=sent)
    dw1_0 = _mm_tn("ffn0_dw1", dg1, f0)
    dw3_0 = _mm_tn("ffn0_dw3", dg3, f0)
    begun = reduce_begin("ffn0", {("ffn_w1", 0): dw1_0, ("ffn_w3", 0): dw3_0, ("ffn_w2", 0): dw2_0})
    dh1, dnf0, db_o = _mm_nt_normbwd("ffn0_bwd_in", [(dg1, w1, 0), (dg3, w3, 0)], h1, nf[0:1], dh2, begun)
    sent = reduce_send("ffn0", dh1)

    dw_o = _mm_tn("attn_dw_o", ao, dh1, after=sent)
    dao = _mm_nt("attn_bwd_out", dh1, p["attn_w_o"], BF16)
    dq, dkv, d_sinks, dbq, dbkv = _attn_bwd(qkv, dao, cos, sin, attn_probs, sink_probs)
    dwq = _mm_tn("attn_dw_q", dq, y0)
    dwkv = _mm_tn("attn_dw_kv", dkv, y0)
    wqkv = p["attn_w_qkv"]
    dwqkv = jnp.concatenate([dwq, dwkv], axis=0).reshape(N_CHIPS, -1, D)
    begun = reduce_begin("attn", {("attn_w_o", 0): dw_o.reshape(N_CHIPS, -1, D), ("attn_w_qkv", 0): dwqkv})
    sent = reduce_send("attn", begun)
    dx, dnm0, _ = _mm_nt_normbwd("attn_bwd_in", [(dq, wqkv[:qw], None), (dkv, wqkv[qw:], None)], x, nm[0:1], dh1,
                                 sent)

    g["norm_mix"] = jnp.concatenate([dnm0, dnm1], axis=0)
    g["norm_ffn"] = jnp.concatenate([dnf0, dnf1], axis=0)
    g["attn_b_qkv"] = jnp.concatenate([dbq, dbkv], axis=1)
    g["attn_sinks"] = d_sinks
    g["attn_b_o"] = db_o
    g["conv_b_pw1"] = d_b_pw1
    g["conv_w_dw"] = d_w_dw[:CONV_WIDTH]
    g["conv_b_dw"] = d_b_dw
    g["conv_ln_g"] = d_ln_g
    g["conv_ln_b"] = d_ln_b
    g["conv_b_pw2"] = db_pw2
    g["norm_final"] = d_norm_final
    return loss, dx, g


ANY = pl.BlockSpec(memory_space=pl.ANY)
VMEM_WHOLE = pl.BlockSpec(memory_space=pltpu.VMEM)


def _my_place():
    return lax.axis_index("x"), lax.axis_index("y"), lax.axis_index("c")


def _other_chips(x, y):
    places = [(1 - x, y), (x, 1 - y), (1 - x, 1 - y)]
    return [(bx, by, 2 * bx + by) for bx, by in places]


def _gather_small(name, v):
    r, w = v.shape

    def body(v_ref, o_ref, send_sems, recv_sems):
        x, y, c = _my_place()
        pairs = _all_to_all_copies(v_ref, o_ref, send_sems, recv_sems)
        for send, _ in pairs:
            send.start()
        o_ref[4 * x + 2 * y + c] = v_ref[...]
        for send, arrival in pairs:
            arrival.wait_recv()
            send.wait_send()

    return pl.pallas_call(
        body, name=name, out_shape=SDS((N_DEV, r, w), F32), in_specs=[VMEM_WHOLE], out_specs=VMEM_WHOLE,
        scratch_shapes=[pltpu.SemaphoreType.DMA((N_DEV - 1,)), pltpu.SemaphoreType.DMA((N_DEV - 1,))],
        compiler_params=pltpu.CompilerParams(vmem_limit_bytes=V7X_VMEM_LIMIT_BYTES),
    )(v)


def _cast_into_slot(name, gathered, shard, chip_idx):
    rows, cols = shard.shape
    tr = _pack_row_tile(rows)

    def body(k_ref, s_ref, g_ref, o_ref):
        o_ref[...] = s_ref[...].astype(BF16)

    return pl.pallas_call(
        body, name=name,
        grid_spec=pltpu.PrefetchScalarGridSpec(
            num_scalar_prefetch=1, grid=(rows // tr,),
            in_specs=[pl.BlockSpec((tr, cols), lambda i, k_ref: (i, 0)), pl.BlockSpec(memory_space=pl.ANY)],
            out_specs=pl.BlockSpec((None, tr, cols), lambda i, k_ref: (k_ref[0], i, 0))),
        out_shape=SDS(gathered.shape, BF16),
        input_output_aliases={2: 0},
        compiler_params=_params("parallel"),
    )(chip_idx, shard, gathered)


def _row_halves(ref, c):
    half = ref.shape[1] // 2
    return pl.ds(pl.multiple_of(c * half, 16), half), pl.ds(pl.multiple_of((1 - c) * half, 16), half)


def _gather_ici_copies(refs, send_sems, recv_sems):
    x, y, c = _my_place()
    k = 2 * x + y
    pairs = []
    for i, ref in enumerate(refs):
        mine, _ = _row_halves(ref, c)
        for j, (bx, by, kb) in enumerate(_other_chips(x, y)):
            sems = dict(send_sem=send_sems.at[3 * i + j], recv_sem=recv_sems.at[3 * i + j], device_id_type=MESH)
            send = pltpu.make_async_remote_copy(src_ref=ref.at[k, mine], dst_ref=ref.at[k, mine],
                                                device_id=(bx, by, c), **sems)
            arrival = pltpu.make_async_remote_copy(src_ref=ref.at[kb, mine], dst_ref=ref.at[kb, mine],
                                                   device_id=(bx, by, c), **sems)
            pairs.append((send, arrival))
    return pairs


def _gather_d2d_copies(refs, send_sems, recv_sems, first_sem):
    x, y, c = _my_place()
    pairs = []
    for i, ref in enumerate(refs):
        mine, theirs = _row_halves(ref, c)
        for j, (_, _, kb) in enumerate(_other_chips(x, y)):
            sem = first_sem + 3 * i + j
            sems = dict(send_sem=send_sems.at[sem], recv_sem=recv_sems.at[sem], device_id=(x, y, 1 - c),
                        device_id_type=MESH)
            send = pltpu.make_async_remote_copy(src_ref=ref.at[kb, mine], dst_ref=ref.at[kb, mine], **sems)
            arrival = pltpu.make_async_remote_copy(src_ref=ref.at[kb, theirs], dst_ref=ref.at[kb, theirs], **sems)
            pairs.append((send, arrival))
    return pairs


def _run_copies(pairs):
    for send, _ in pairs:
        send.start()
    for send, arrival in pairs:
        send.wait_send()
        arrival.wait_recv()


def _gather_now(name, gathered):
    n_w = len(gathered)

    def body(*refs):
        in_refs = refs[:n_w]
        send_sems, recv_sems = refs[2 * n_w:]
        _run_copies(_gather_ici_copies(in_refs, send_sems, recv_sems))
        _run_copies(_gather_d2d_copies(in_refs, send_sems, recv_sems, 3 * n_w))

    return pl.pallas_call(
        body, name=name, out_shape=[SDS(g.shape, g.dtype) for g in gathered],
        in_specs=[ANY] * n_w, out_specs=[ANY] * n_w, input_output_aliases={i: i for i in range(n_w)},
        scratch_shapes=[pltpu.SemaphoreType.DMA((6 * n_w,)), pltpu.SemaphoreType.DMA((6 * n_w,))],
    )(*gathered)


def _gather_stage_copies(stage, refs, send_sems, recv_sems):
    if stage == "ici":
        return _gather_ici_copies(refs, send_sems, recv_sems)
    return _gather_d2d_copies(refs, send_sems, recv_sems, 0)


def _gather_start(name, gathered, after, stage="ici"):
    n_w = len(gathered)

    def body(*refs):
        in_refs = refs[:n_w]
        send_sems, recv_sems = refs[n_w + 1:n_w + 3]
        for send, _ in _gather_stage_copies(stage, in_refs, send_sems, recv_sems):
            send.start()
        refs[-1][...] = jnp.zeros_like(refs[-1])

    out = pl.pallas_call(
        body, name=name,
        out_shape=(pltpu.SemaphoreType.DMA((3 * n_w,)), pltpu.SemaphoreType.DMA((3 * n_w,)),
                   *[pltpu.HBM(g.shape, g.dtype) for g in gathered], SDS((8, V7X_LANES), F32)),
        in_specs=[*[HBM_SPEC] * n_w, ANY], out_specs=(SEM_SPEC, SEM_SPEC, *[HBM_SPEC] * n_w, VMEM_WHOLE),
        input_output_aliases={i: 2 + i for i in range(n_w)},
        compiler_params=pltpu.CompilerParams(has_side_effects=DATAFLOW),
    )(*[pltpu.with_memory_space_constraint(g, pltpu.HBM) for g in gathered], after)
    return out[0], out[1], list(out[2:2 + n_w]), out[-1]


def _gather_wait(name, send_sems, recv_sems, gathered, after, stage="ici"):
    n_w = len(gathered)

    def body(*refs):
        in_refs = refs[:n_w]
        send_sems, recv_sems = refs[n_w:n_w + 2]
        for send, arrival in _gather_stage_copies(stage, in_refs, send_sems, recv_sems):
            send.wait_send()
            arrival.wait_recv()

    out = pl.pallas_call(
        body, name=name, out_shape=tuple(pltpu.HBM(g.shape, g.dtype) for g in gathered),
        in_specs=[*[HBM_SPEC] * n_w, SEM_SPEC, SEM_SPEC, ANY], out_specs=tuple([HBM_SPEC] * n_w),
        input_output_aliases={i: i for i in range(n_w)},
        compiler_params=pltpu.CompilerParams(has_side_effects=DATAFLOW),
    )(*gathered, send_sems, recv_sems, after)
    return list(out)


def _swap_fetched_with_sibling(name, gathered):
    n_w = len(gathered)

    def body(*refs):
        in_refs = refs[:n_w]
        send_sems, recv_sems = refs[2 * n_w:]
        _run_copies(_gather_d2d_copies(in_refs, send_sems, recv_sems, 0))

    return pl.pallas_call(
        body, name=name, out_shape=[SDS(g.shape, g.dtype) for g in gathered],
        in_specs=[ANY] * n_w, out_specs=[ANY] * n_w, input_output_aliases={i: i for i in range(n_w)},
        scratch_shapes=[pltpu.SemaphoreType.DMA((3 * n_w,)), pltpu.SemaphoreType.DMA((3 * n_w,))],
    )(*gathered)


def _sibling_swap_copies(g_refs, land_refs, send_sems, recv_sems):
    x, y, c = _my_place()
    copies = []
    for i, g_ref in enumerate(g_refs):
        half = g_ref.shape[1] // 2
        theirs = pl.ds(pl.multiple_of((1 - c) * half, 8), half)
        copies.append(pltpu.make_async_remote_copy(
            src_ref=g_ref.at[:, theirs], dst_ref=land_refs[i], send_sem=send_sems.at[i], recv_sem=recv_sems.at[i],
            device_id=(x, y, 1 - c), device_id_type=MESH))
    return copies


def _sibling_swap_start(name, grads):
    n_g = len(grads)

    def body(*refs):
        g_refs, land_refs = refs[:n_g], refs[n_g:2 * n_g]
        send_sems, recv_sems = refs[2 * n_g:2 * n_g + 2]
        for cp in _sibling_swap_copies(g_refs, land_refs, send_sems, recv_sems):
            cp.start()
        refs[-1][...] = jnp.zeros_like(refs[-1])

    lands = [pltpu.with_memory_space_constraint(lax.empty((g.shape[0], g.shape[1] // 2, g.shape[2]), g.dtype),
                                                pltpu.HBM) for g in grads]
    out = pl.pallas_call(
        body, name=name,
        out_shape=(pltpu.SemaphoreType.DMA((n_g,)), pltpu.SemaphoreType.DMA((n_g,)),
                   *[pltpu.HBM(g.shape, g.dtype) for g in grads], *[pltpu.HBM(l.shape, l.dtype) for l in lands],
                   SDS((8, V7X_LANES), F32)),
        in_specs=[HBM_SPEC] * (2 * n_g), out_specs=(SEM_SPEC, SEM_SPEC, *[HBM_SPEC] * (2 * n_g), VMEM_WHOLE),
        input_output_aliases={i: 2 + i for i in range(2 * n_g)},
        compiler_params=pltpu.CompilerParams(has_side_effects=DATAFLOW),
    )(*[pltpu.with_memory_space_constraint(g, pltpu.HBM) for g in grads], *lands)
    return out[0], out[1], list(out[2:2 + n_g]), list(out[2 + n_g:2 + 2 * n_g]), out[-1]


def _sibling_swap_wait(name, send_sems, recv_sems, grads, lands, after):
    n_g = len(grads)

    def body(*refs):
        g_refs, land_refs = refs[:n_g], refs[n_g:2 * n_g]
        send_sems, recv_sems = refs[2 * n_g:2 * n_g + 2]
        for cp in _sibling_swap_copies(g_refs, land_refs, send_sems, recv_sems):
            cp.wait_send()
            cp.wait_recv()

    out = pl.pallas_call(
        body, name=name,
        out_shape=(*[pltpu.HBM(g.shape, g.dtype) for g in grads], *[pltpu.HBM(l.shape, l.dtype) for l in lands]),
        in_specs=[*[HBM_SPEC] * (2 * n_g), SEM_SPEC, SEM_SPEC, ANY], out_specs=tuple([HBM_SPEC] * (2 * n_g)),
        input_output_aliases={i: i for i in range(2 * n_g)},
        compiler_params=pltpu.CompilerParams(has_side_effects=DATAFLOW),
    )(*grads, *lands, send_sems, recv_sems, after)
    return list(out[:n_g]), list(out[n_g:])


def _pack_row_tile(rows):
    for t in range(min(rows, 512), 7, -1):
        if rows % t == 0 and t % 8 == 0:
            return t
    return rows


def _add_sibling_half(name, grads, from_sibling, c_idx):
    n, R, w = grads.shape
    half = R // 2
    tr = _pack_row_tile(half)
    steps = half // tr

    def body(c_ref, g_ref, s_ref, o_ref):
        o_ref[...] = (g_ref[...] + s_ref[...]).astype(BF16)

    return pl.pallas_call(
        body, name=name,
        grid_spec=pltpu.PrefetchScalarGridSpec(
            num_scalar_prefetch=1, grid=(n, steps),
            in_specs=[pl.BlockSpec((1, tr, w), lambda j, i, c_ref: (j, c_ref[0] * steps + i, 0)),
                      pl.BlockSpec((1, tr, w), lambda j, i, c_ref: (j, i, 0))],
            out_specs=pl.BlockSpec((1, tr, w), lambda j, i, c_ref: (j, i, 0))),
        out_shape=SDS((n, half, w), BF16),
        compiler_params=_params("parallel", "parallel"),
    )(c_idx, grads, from_sibling)


HBM_SPEC = pl.BlockSpec(memory_space=pltpu.HBM)
SEM_SPEC = pl.BlockSpec(memory_space=pltpu.SEMAPHORE)
DATAFLOW = pltpu.SideEffectType.DATAFLOW_SIDE_EFFECTING


def _chip_scatter_copies(p_refs, land_refs, send_sems, recv_sems):
    x, y, c = _my_place()
    return [pltpu.make_async_remote_copy(
        src_ref=p_refs[i].at[kb], dst_ref=land_refs[i].at[j], send_sem=send_sems.at[3 * i + j],
        recv_sem=recv_sems.at[3 * i + j], device_id=(bx, by, c), device_id_type=MESH)
        for i in range(len(p_refs)) for j, (bx, by, kb) in enumerate(_other_chips(x, y))]


def _scatter_start(name, partials):
    n_p = len(partials)

    def body(*refs):
        p_refs, land_refs = refs[:n_p], refs[n_p:2 * n_p]
        send_sems, recv_sems = refs[2 * n_p:2 * n_p + 2]
        for cp in _chip_scatter_copies(p_refs, land_refs, send_sems, recv_sems):
            cp.start()
        refs[-1][...] = jnp.zeros_like(refs[-1])

    lands = [pltpu.with_memory_space_constraint(lax.empty((N_CHIPS - 1,) + p.shape[1:], p.dtype), pltpu.HBM)
             for p in partials]
    out = pl.pallas_call(
        body, name=name,
        out_shape=(pltpu.SemaphoreType.DMA((3 * n_p,)), pltpu.SemaphoreType.DMA((3 * n_p,)),
                   *[pltpu.HBM(p.shape, p.dtype) for p in partials], *[pltpu.HBM(l.shape, l.dtype) for l in lands],
                   SDS((8, V7X_LANES), F32)),
        in_specs=[HBM_SPEC] * (2 * n_p), out_specs=(SEM_SPEC, SEM_SPEC, *[HBM_SPEC] * (2 * n_p), VMEM_WHOLE),
        input_output_aliases={i: 2 + i for i in range(2 * n_p)},
        compiler_params=pltpu.CompilerParams(has_side_effects=DATAFLOW),
    )(*[pltpu.with_memory_space_constraint(p, pltpu.HBM) for p in partials], *lands)
    return out[0], out[1], list(out[2:2 + n_p]), list(out[2 + n_p:2 + 2 * n_p]), out[-1]


def _scatter_wait(name, send_sems, recv_sems, partials, lands, after):
    n_p = len(partials)

    def body(*refs):
        p_refs, land_refs = refs[:n_p], refs[n_p:2 * n_p]
        send_sems, recv_sems = refs[2 * n_p:2 * n_p + 2]
        for cp in _chip_scatter_copies(p_refs, land_refs, send_sems, recv_sems):
            cp.wait_send()
            cp.wait_recv()

    out = pl.pallas_call(
        body, name=name,
        out_shape=(*[pltpu.HBM(p.shape, p.dtype) for p in partials], *[pltpu.HBM(l.shape, l.dtype) for l in lands]),
        in_specs=[*[HBM_SPEC] * (2 * n_p), SEM_SPEC, SEM_SPEC, ANY], out_specs=tuple([HBM_SPEC] * (2 * n_p)),
        input_output_aliases={i: i for i in range(2 * n_p)},
        compiler_params=pltpu.CompilerParams(has_side_effects=DATAFLOW),
    )(*partials, *lands, send_sems, recv_sems, after)
    return list(out[:n_p]), list(out[n_p:])


def _sum_chip_partials(name, grads, from_sibling, received, shard, layer, place):
    n, half, w = from_sibling.shape
    tr = _pack_row_tile(half)
    steps = half // tr

    def body(place_ref, g_ref, s_ref, r_ref, shard_ref, o_ref):
        own = g_ref[0] + s_ref[0]
        o_ref[...] = ((own + r_ref[0].astype(F32)) + r_ref[1].astype(F32)) + r_ref[2].astype(F32)

    return pl.pallas_call(
        body, name=name,
        grid_spec=pltpu.PrefetchScalarGridSpec(
            num_scalar_prefetch=1, grid=(steps,),
            in_specs=[pl.BlockSpec((1, tr, w), lambda i, place_ref: (place_ref[0], place_ref[1] * steps + i, 0)),
                      pl.BlockSpec((1, tr, w), lambda i, place_ref: (place_ref[0], i, 0)),
                      pl.BlockSpec((n - 1, tr, w), lambda i, place_ref: (0, i, 0)),
                      pl.BlockSpec(memory_space=pl.ANY)],
            out_specs=pl.BlockSpec((tr, w), lambda i, place_ref: ((2 * layer + place_ref[1]) * steps + i, 0))),
        out_shape=SDS(shard.shape, F32),
        input_output_aliases={4: 0},
        compiler_params=_params("parallel"),
    )(place, grads, from_sibling, received, shard)


def _join_halves(shards, layers):
    n_s = len(shards)
    n_sem = sum(layers)

    def body(*refs):
        in_refs = refs[:n_s]
        send_sems, recv_sems = refs[2 * n_s:]
        x, y, c = _my_place()
        copies, sem = [], 0
        for ref, n_layers in zip(in_refs, layers):
            half = ref.shape[0] // (2 * n_layers)
            for layer in range(n_layers):
                mine = pl.ds(pl.multiple_of(layer * 2 * half + c * half, 8), half)
                theirs = pl.ds(pl.multiple_of(layer * 2 * half + (1 - c) * half, 8), half)
                send = pltpu.make_async_remote_copy(
                    src_ref=ref.at[mine], dst_ref=ref.at[mine], send_sem=send_sems.at[sem], recv_sem=recv_sems.at[sem],
                    device_id=(x, y, 1 - c), device_id_type=MESH)
                send.start()
                arrival = pltpu.make_async_remote_copy(
                    src_ref=ref.at[theirs], dst_ref=ref.at[theirs], send_sem=send_sems.at[sem],
                    recv_sem=recv_sems.at[sem], device_id=(x, y, 1 - c), device_id_type=MESH)
                copies.append((send, arrival))
                sem += 1
        for send, arrival in copies:
            send.wait_send()
            arrival.wait_recv()

    return pl.pallas_call(
        body, name="join_halves", out_shape=[SDS(s.shape, s.dtype) for s in shards],
        in_specs=[ANY] * n_s, out_specs=[ANY] * n_s,
        input_output_aliases={i: i for i in range(n_s)},
        scratch_shapes=[pltpu.SemaphoreType.DMA((n_sem,)), pltpu.SemaphoreType.DMA((n_sem,))],
    )(*shards)


def _all_to_all_copies(v_ref, land_ref, send_sems, recv_sems):
    x, y, c = _my_place()
    me = 4 * x + 2 * y + c
    pairs = []
    for k in range(1, N_DEV):
        px, py, pc = (1 - x if k & 4 else x), (1 - y if k & 2 else y), (1 - c if k & 1 else c)
        sems = dict(send_sem=send_sems.at[k - 1], recv_sem=recv_sems.at[k - 1], device_id=(px, py, pc),
                    device_id_type=MESH)
        send = pltpu.make_async_remote_copy(src_ref=v_ref, dst_ref=land_ref.at[me], **sems)
        arrival = pltpu.make_async_remote_copy(src_ref=v_ref, dst_ref=land_ref.at[4 * px + 2 * py + pc], **sems)
        pairs.append((send, arrival))
    return pairs


def _small_reduce_start(name, v):
    def body(v_ref, land_ref, send_sems, recv_sems, v_out, land_out):
        for send, _ in _all_to_all_copies(v_ref, land_ref, send_sems, recv_sems):
            send.start()

    land = pltpu.with_memory_space_constraint(jnp.zeros((N_DEV,) + v.shape, v.dtype), pltpu.HBM)
    return pl.pallas_call(
        body, name=name,
        out_shape=(pltpu.SemaphoreType.DMA((N_DEV - 1,)), pltpu.SemaphoreType.DMA((N_DEV - 1,)),
                   pltpu.HBM(v.shape, v.dtype), pltpu.HBM(land.shape, land.dtype)),
        in_specs=[HBM_SPEC, HBM_SPEC], out_specs=(SEM_SPEC, SEM_SPEC, HBM_SPEC, HBM_SPEC),
        input_output_aliases={0: 2, 1: 3},
        compiler_params=pltpu.CompilerParams(has_side_effects=DATAFLOW),
    )(pltpu.with_memory_space_constraint(v, pltpu.HBM), land)


def _small_reduce_wait(name, send_sems, recv_sems, v, land, after):
    def body(v_ref, land_ref, send_sems, recv_sems, after_ref, v_out, land_out):
        for send, arrival in _all_to_all_copies(v_ref, land_ref, send_sems, recv_sems):
            send.wait_send()
            arrival.wait_recv()

    return pl.pallas_call(
        body, name=name, out_shape=(pltpu.HBM(v.shape, v.dtype), pltpu.HBM(land.shape, land.dtype)),
        in_specs=[HBM_SPEC, HBM_SPEC, SEM_SPEC, SEM_SPEC, ANY], out_specs=(HBM_SPEC, HBM_SPEC),
        input_output_aliases={0: 0, 1: 1},
        compiler_params=pltpu.CompilerParams(has_side_effects=DATAFLOW),
    )(v, land, send_sems, recv_sems, after)


def _sum_device_slots(name, v, land, me):
    r, w = v.shape

    def body(me_ref, v_ref, land_ref, o_ref):
        mine = v_ref[...]
        acc = jnp.where(me_ref[0] == 0, mine, land_ref[0])
        for d in range(1, N_DEV):
            acc = acc + jnp.where(me_ref[0] == d, mine, land_ref[d])
        o_ref[...] = acc

    return pl.pallas_call(
        body, name=name,
        grid_spec=pltpu.PrefetchScalarGridSpec(
            num_scalar_prefetch=1, grid=(1,),
            in_specs=[pl.BlockSpec((r, w), lambda i, me_ref: (0, 0)),
                      pl.BlockSpec((N_DEV, r, w), lambda i, me_ref: (0, 0, 0))],
            out_specs=pl.BlockSpec((r, w), lambda i, me_ref: (0, 0))),
        out_shape=SDS((r, w), F32),
        compiler_params=_params("arbitrary"),
    )(me, v, land)


def _adamw(name, w, g, m, v):
    rows, width = w.shape
    tr = _pack_row_tile(rows)

    def body(w_ref, g_ref, m_ref, v_ref, g_out_ref, d_ref, nm_ref, nv_ref):
        gg = g_ref[...]
        g_out_ref[...] = gg
        m_new = ADAM_B1 * m_ref[...] + (1.0 - ADAM_B1) * gg
        v_new = ADAM_B2 * v_ref[...] + (1.0 - ADAM_B2) * (gg * gg)
        m_hat = m_new / (1.0 - ADAM_B1 ** ADAM_STEP)
        v_hat = v_new / (1.0 - ADAM_B2 ** ADAM_STEP)
        d_ref[...] = -ADAM_LR * (m_hat / (jnp.sqrt(v_hat) + ADAM_EPS) + ADAM_WD * w_ref[...])
        nm_ref[...] = m_new
        nv_ref[...] = v_new

    spec = _rows(tr, width)
    return pl.pallas_call(
        body, name=name, grid=(rows // tr,),
        in_specs=[spec] * 4, out_specs=[spec] * 4,
        out_shape=[SDS((rows, width), F32)] * 4,
        compiler_params=_params("parallel"),
    )(w, g, m, v)


WEIGHT_NAMES = ['norm_mix', 'norm_ffn', 'attn_w_qkv', 'attn_b_qkv', 'attn_sinks', 'attn_w_o', 'attn_b_o',
                'conv_w_pw1', 'conv_b_pw1', 'conv_w_dw', 'conv_b_dw', 'conv_ln_g', 'conv_ln_b', 'conv_w_pw2',
                'conv_b_pw2', 'ffn_w1', 'ffn_w3', 'ffn_w2', 'norm_final']
BIG = ['attn_w_qkv', 'attn_w_o', 'conv_w_pw1', 'conv_w_pw2', 'ffn_w1', 'ffn_w3', 'ffn_w2']
COLUMN_SPLIT = ('attn_w_qkv', 'conv_w_pw1', 'ffn_w1', 'ffn_w3')
STORED_TRANSPOSED = ('attn_w_qkv', 'ffn_w1', 'ffn_w3')
SMALL_SPLIT = ['conv_b_pw1', 'conv_w_dw', 'conv_b_dw', 'conv_ln_g', 'conv_ln_b', 'conv_b_pw2']
SMALL_WHOLE = ['norm_mix', 'norm_ffn', 'attn_b_qkv', 'attn_sinks', 'attn_b_o', 'norm_final']


def _keeps_rows(shape):
    return len(shape) == 2 and shape[0] > 1 and shape[1] == PACK_W


def _pack_rows(arrays, dtype, row_multiple):
    blocks = [jnp.pad(a.astype(dtype), ((0, -a.shape[0] % V7X_SUBLANES), (0, 0)))
              for a in arrays if _keeps_rows(a.shape)]
    flat = jnp.concatenate([a.astype(dtype).reshape(-1) for a in arrays if not _keeps_rows(a.shape)])
    multiple = max(row_multiple, V7X_SUBLANES)
    rows = -(-(-(-flat.shape[0] // PACK_W)) // multiple) * multiple
    blocks.append(jnp.pad(flat, (0, rows * PACK_W - flat.shape[0])).reshape(rows, PACK_W))
    return jnp.concatenate(blocks, axis=0) if len(blocks) > 1 else blocks[0]


def _unpack_rows(pack, shapes):
    out, row = {}, 0
    for i, shape in enumerate(shapes):
        if _keeps_rows(shape):
            out[i] = pack[row:row + shape[0]]
            row += -(-shape[0] // V7X_SUBLANES) * V7X_SUBLANES
    flat, at = pack[row:].reshape(-1), 0
    for i, shape in enumerate(shapes):
        if not _keeps_rows(shape):
            size = 1
            for s in shape:
                size *= s
            out[i] = flat[at:at + size].reshape(shape)
            at += size
    return [out[i] for i in range(len(shapes))]


def _join_chip_axis(name, parts):
    axis = parts.ndim - 1 if name in COLUMN_SPLIT or name in SMALL_SPLIT else parts.ndim - 2
    moved = jnp.moveaxis(parts, 0, axis - 1)
    shape = list(moved.shape)
    shape[axis - 1:axis + 1] = [shape[axis - 1] * shape[axis]]
    return moved.reshape(shape)


def _split_chip_axis(name, whole, shard_shape):
    axis = len(shard_shape) - 1 if name in COLUMN_SPLIT or name in SMALL_SPLIT else len(shard_shape) - 2
    shape = list(whole.shape)
    shape[axis:axis + 1] = [N_CHIPS, shard_shape[axis]]
    return jnp.moveaxis(whole.reshape(shape), axis, 0)


def kernel(x, norm_mix, norm_ffn, attn_w_qkv, attn_b_qkv, attn_sinks, attn_w_o, attn_b_o, conv_w_pw1, conv_b_pw1, conv_w_dw, conv_b_dw, conv_ln_g, conv_ln_b, conv_w_pw2, conv_b_pw2, ffn_w1, ffn_w3, ffn_w2, norm_final, loss_target, m_norm_mix, m_norm_ffn, m_attn_w_qkv, m_attn_b_qkv, m_attn_sinks, m_attn_w_o, m_attn_b_o, m_conv_w_pw1, m_conv_b_pw1, m_conv_w_dw, m_conv_b_dw, m_conv_ln_g, m_conv_ln_b, m_conv_w_pw2, m_conv_b_pw2, m_ffn_w1, m_ffn_w3, m_ffn_w2, m_norm_final, v_norm_mix, v_norm_ffn, v_attn_w_qkv, v_attn_b_qkv, v_attn_sinks, v_attn_w_o, v_attn_b_o, v_conv_w_pw1, v_conv_b_pw1, v_conv_w_dw, v_conv_b_dw, v_conv_ln_g, v_conv_ln_b, v_conv_w_pw2, v_conv_b_pw2, v_ffn_w1, v_ffn_w3, v_ffn_w2, v_norm_final):
    w = dict(zip(WEIGHT_NAMES, (norm_mix, norm_ffn, attn_w_qkv, attn_b_qkv, attn_sinks, attn_w_o, attn_b_o,
                                conv_w_pw1, conv_b_pw1, conv_w_dw, conv_b_dw, conv_ln_g, conv_ln_b, conv_w_pw2,
                                conv_b_pw2, ffn_w1, ffn_w3, ffn_w2, norm_final)))
    m = dict(zip(WEIGHT_NAMES, (m_norm_mix, m_norm_ffn, m_attn_w_qkv, m_attn_b_qkv, m_attn_sinks, m_attn_w_o,
                                m_attn_b_o, m_conv_w_pw1, m_conv_b_pw1, m_conv_w_dw, m_conv_b_dw, m_conv_ln_g,
                                m_conv_ln_b, m_conv_w_pw2, m_conv_b_pw2, m_ffn_w1, m_ffn_w3, m_ffn_w2, m_norm_final)))
    v = dict(zip(WEIGHT_NAMES, (v_norm_mix, v_norm_ffn, v_attn_w_qkv, v_attn_b_qkv, v_attn_sinks, v_attn_w_o,
                                v_attn_b_o, v_conv_w_pw1, v_conv_b_pw1, v_conv_w_dw, v_conv_b_dw, v_conv_ln_g,
                                v_conv_ln_b, v_conv_w_pw2, v_conv_b_pw2, v_ffn_w1, v_ffn_w3, v_ffn_w2, v_norm_final)))
    T, D = x.shape[1], x.shape[2]
    c_idx = lax.axis_index("c").astype(jnp.int32).reshape(1)
    chip = (2 * lax.axis_index("x") + lax.axis_index("y")).astype(jnp.int32)

    def as_rows(n, a):
        a = jnp.swapaxes(a, -1, -2) if n in STORED_TRANSPOSED else a
        return a.reshape(-1, a.shape[-1])

    def from_rows(n, rows):
        shape = w[n].shape[:-2] + w[n].shape[:-3:-1] if n in STORED_TRANSPOSED else w[n].shape
        a = rows.reshape(shape)
        return jnp.swapaxes(a, -1, -2) if n in STORED_TRANSPOSED else a

    slabs = {n: _cast_into_slot(f"cast_{n}", lax.empty((N_CHIPS,) + as_rows(n, w[n]).shape, BF16), as_rows(n, w[n]),
                                chip.reshape(1)) for n in BIG}
    first, later = BIG[:2], BIG[2:]
    qkv_parts, w_o_parts = _gather_now("gather_attn", [slabs[n] for n in first])
    send_sems, recv_sems, travelling, gather_started = _gather_start("gather_start", [slabs[n] for n in later],
                                                                     qkv_parts)
    layers = ffn_w1.shape[0]
    small_shapes = [w[n].shape for n in SMALL_SPLIT]
    small_all = _gather_small("gather_small", _pack_rows([w[n] for n in SMALL_SPLIT], F32, 8))
    per_chip = [_unpack_rows(small_all[2 * j], small_shapes) for j in range(N_CHIPS)]
    full = {}
    for i, n in enumerate(SMALL_SPLIT):
        full[n] = _join_chip_axis(n, jnp.stack([per_chip[j][i] for j in range(N_CHIPS)]))

    def other_weights(after):
        landed = dict(zip(later, _gather_wait("gather_wait", send_sems, recv_sems, travelling, after)))
        now, then = ["ffn_w1", "ffn_w3"], ["conv_w_pw1", "conv_w_pw2", "ffn_w2"]
        ready = dict(zip(now, _swap_fetched_with_sibling("gather_swap", [landed[n] for n in now])))
        swap_send, swap_recv, swapping_rest, swap_started = _gather_start(
            "gather_swap_start", [landed[n] for n in then], ready[now[0]], stage="d2d")

        def rest_of_weights(after_next):
            rest = dict(zip(then, _gather_wait("gather_swap_wait", swap_send, swap_recv, swapping_rest, after_next,
                                               stage="d2d")))
            return {"conv_w_pw1": rest["conv_w_pw1"], "conv_w_pw2": rest["conv_w_pw2"].reshape(-1, D),
                    "ffn_w2": rest["ffn_w2"].reshape(N_CHIPS, layers, -1, D)}

        return {"ffn_w1": ready["ffn_w1"].reshape(N_CHIPS, layers, -1, D),
                "ffn_w3": ready["ffn_w3"].reshape(N_CHIPS, layers, -1, D),
                "swap_started": swap_started, "rest_of_weights": rest_of_weights}

    p = {
        "norm_mix": norm_mix, "norm_ffn": norm_ffn, "norm_final": norm_final.reshape(1, D),
        "attn_w_qkv": qkv_parts.reshape(-1, D), "attn_b_qkv": attn_b_qkv,
        "attn_sinks": attn_sinks, "attn_w_o": w_o_parts.reshape(-1, D), "attn_b_o": attn_b_o,
        "conv_b_pw1": full["conv_b_pw1"], "conv_w_dw": full["conv_w_dw"][0],
        "conv_b_dw": full["conv_b_dw"], "conv_ln_g": full["conv_ln_g"], "conv_ln_b": full["conv_ln_b"],
        "conv_b_pw2": full["conv_b_pw2"], "gather_started": gather_started, "other_weights": other_weights,
    }
    swapping, in_flight = {}, []

    def reduce_begin(tag, grads):
        keys = list(grads)
        *handles, begun = _sibling_swap_start(f"sibling_swap_start_{tag}", [grads[k] for k in keys])
        swapping[tag] = (keys, handles)
        return begun

    def reduce_send(tag, after):
        keys, (swap_send, swap_recv, grads, lands) = swapping[tag]
        grads, from_sibling = _sibling_swap_wait(f"sibling_swap_wait_{tag}", swap_send, swap_recv, grads, lands, after)
        partials = [_add_sibling_half(f"add_sibling_half_{tag}{i}", gr, fs, c_idx)
                    for i, (gr, fs) in enumerate(zip(grads, from_sibling))]
        *handles, sent = _scatter_start(f"scatter_start_{tag}", partials)
        in_flight.append((tag, keys, handles, grads, from_sibling))
        return sent

    loss_part, dx, g = _local_step(x[0], loss_target[0], p, reduce_begin, reduce_send)
    for n in SMALL_WHOLE + SMALL_SPLIT:
        g[n] = g[n].reshape((-1,) + g[n].shape[-2:]) if w[n].ndim == 3 else g[n].reshape(w[n].shape[:-1] + (-1,))

    small_pack = _pack_rows([loss_part] + [g[n] for n in SMALL_WHOLE] + [g[n] for n in SMALL_SPLIT], F32, 8)
    small_send, small_recv, small_pack, small_land = _small_reduce_start("small_reduce_start", small_pack)

    place = jnp.stack([chip, c_idx[0]])
    shard_grad = {n: lax.empty(as_rows(n, w[n]).shape, F32) for n in BIG}
    for tag, keys, (send_sems, recv_sems, partials, lands), grads, from_sibling in in_flight:
        _, received = _scatter_wait(f"scatter_wait_{tag}", send_sems, recv_sems, partials, lands, small_pack)
        for i, (n, layer) in enumerate(keys):
            shard_grad[n] = _sum_chip_partials(f"sum_chip_partials_{tag}{i}", grads[i], from_sibling[i], received[i],
                                               shard_grad[n], layer, place)
    g_big = dict(zip(BIG, _join_halves([shard_grad[n] for n in BIG], [w[n].shape[0] for n in BIG])))
    big_out = {}
    for n in BIG:
        step = _adamw(f"adamw_{n}", as_rows(n, w[n]), g_big[n], as_rows(n, m[n]), as_rows(n, v[n]))
        big_out[n] = [from_rows(n, a) for a in step]

    small_whole_shapes = [w[n].shape for n in SMALL_WHOLE]
    small_full_shapes = [g[n].shape for n in SMALL_SPLIT]
    small_pack, small_land = _small_reduce_wait("small_reduce_wait", small_send, small_recv, small_pack, small_land,
                                                big_out[BIG[-1]][1])
    reduced = _sum_device_slots("small_reduce_sum", small_pack, small_land, (2 * chip + c_idx[0]).reshape(1))
    pieces = _unpack_rows(reduced, [(1,)] + small_whole_shapes + small_full_shapes)
    loss = pieces[0].reshape(())
    g_small = dict(zip(SMALL_WHOLE, pieces[1:1 + len(SMALL_WHOLE)]))
    for n, whole in zip(SMALL_SPLIT, pieces[1 + len(SMALL_WHOLE):]):
        parts = _split_chip_axis(n, whole, w[n].shape)
        g_small[n] = lax.dynamic_index_in_dim(parts, chip, axis=0, keepdims=False)
    small = SMALL_WHOLE + SMALL_SPLIT
    _, d_small, m_small, v_small = _adamw(
        "adamw_small", _pack_rows([w[n] for n in small], F32, 8), _pack_rows([g_small[n] for n in small], F32, 8),
        _pack_rows([m[n] for n in small], F32, 8), _pack_rows([v[n] for n in small], F32, 8))

    outs = {}
    for slot, (tag, small_pack) in enumerate((("g", None), ("d", d_small), ("m", m_small), ("v", v_small))):
        vals = {n: big_out[n][slot] for n in BIG}
        if small_pack is None:
            vals.update(g_small)
        else:
            vals.update(zip(small, _unpack_rows(small_pack, [w[n].shape for n in small])))
        outs[tag] = vals
    return (loss, dx.reshape(1, T, D), *[outs["g"][n] for n in WEIGHT_NAMES], *[outs["d"][n] for n in WEIGHT_NAMES],
            *[outs["m"][n] for n in WEIGHT_NAMES], *[outs["v"][n] for n in WEIGHT_NAMES])
```

```python
import functools

import jax
import jax.numpy as jnp
from jax import lax
from jax.experimental import pallas as pl
from jax.experimental.pallas import tpu as pltpu

F32 = jnp.float32
BF16 = jnp.bfloat16
SDS = jax.ShapeDtypeStruct
MESH = pl.DeviceIdType.MESH

HEAD_DIM = 64
N_Q_HEADS = 16
N_KV_HEADS = 2
Q_PER_KV = N_Q_HEADS // N_KV_HEADS
ATTN_BLOCK = 128
ROPE_THETA = 10000.0
CONV_WIDTH = 31
CONV_HALO = 32
CONV_FIRST_TAP = CONV_HALO - CONV_WIDTH + 1
CONV_ROW_CHUNK = 64
CONV_LANE_CHUNK = 256
CONV_GRAD_UNROLL = 8
RMS_EPS = 1e-5
LN_EPS = 1e-5
ADAM_LR = 0.001
ADAM_B1 = 0.9
ADAM_B2 = 0.999
ADAM_EPS = 1e-08
ADAM_WD = 0.01
ADAM_STEP = 10

V7X_LANES = 128
V7X_SUBLANES = 8
V7X_VMEM_LIMIT_BYTES = 56 * 1024 * 1024

N_CHIPS = 4
N_DEV = 8
PACK_W = 1024

MASK_VALUE = -1e30


def _params(*semantics):
    return pltpu.CompilerParams(dimension_semantics=semantics, vmem_limit_bytes=V7X_VMEM_LIMIT_BYTES)


def _rows(tm, width):
    return pl.BlockSpec((tm, width), lambda i: (i, 0))


def _whole(shape):
    return pl.BlockSpec(shape, lambda *_: (0,) * len(shape))


def _rms_rstd(h):
    return lax.rsqrt(jnp.mean(h * h, axis=-1, keepdims=True) + RMS_EPS)


def _silu_and_grad(z):
    sg = jax.nn.sigmoid(z)
    return z * sg, sg * (1.0 + z * (1.0 - sg))


def _swap_rope_halves(t):
    w = t.shape[1]
    half = HEAD_DIM // 2
    lane = lax.broadcasted_iota(jnp.int32, t.shape, 1)
    upper = pltpu.roll(t, w - half, 1)
    lower = pltpu.roll(t, half, 1)
    return jnp.where(lane % HEAD_DIM < half, upper, lower)


def _rope(t, cos_ref, sin_ref):
    reps = t.shape[1] // V7X_LANES
    c = jnp.tile(cos_ref[...], (1, reps))
    s = jnp.tile(sin_ref[...], (1, reps))
    return t * c + _swap_rope_halves(t) * s


def _rope_transposed(dt, cos_ref, sin_ref):
    reps = dt.shape[1] // V7X_LANES
    c = jnp.tile(cos_ref[...], (1, reps))
    s = jnp.tile(sin_ref[...], (1, reps))
    return dt * c + _swap_rope_halves(dt * s)


def _rope_tables(seq_len):
    pos = jnp.arange(seq_len, dtype=F32)
    inv_freq = ROPE_THETA ** (-jnp.arange(0, HEAD_DIM, 2, dtype=F32) / HEAD_DIM)
    ang = pos[:, None] * jnp.tile(inv_freq, 2 * V7X_LANES // HEAD_DIM)[None, :]
    upper_half = jnp.arange(V7X_LANES) % HEAD_DIM >= HEAD_DIM // 2
    return jnp.cos(ang), jnp.where(upper_half[None, :], jnp.sin(ang), -jnp.sin(ang))


def _qkv_proj(h, g, w, b, cos, sin, after):
    T, D = h.shape
    N = w.shape[0]
    tm = min(512, T)
    rope_w = N - N_KV_HEADS * HEAD_DIM

    def body(h_ref, g_ref, w_ref, b_ref, cos_ref, sin_ref, _, y_ref, o_ref):
        hh = h_ref[...]
        y = (hh * _rms_rstd(hh) * g_ref[...]).astype(BF16)
        y_ref[...] = y
        acc = _dot_nt(y, w_ref[...]) + b_ref[...]
        o_ref[:, :rope_w] = _rope(acc[:, :rope_w], cos_ref, sin_ref).astype(BF16)
        o_ref[:, rope_w:] = acc[:, rope_w:].astype(BF16)

    return pl.pallas_call(
        body, name="qkv_proj", grid=(T // tm,),
        in_specs=[_rows(tm, D), _whole((1, D)), _whole((N, D)), _whole((1, N)),
                  _rows(tm, V7X_LANES), _rows(tm, V7X_LANES), pl.BlockSpec(memory_space=pl.ANY)],
        out_specs=[_rows(tm, D), _rows(tm, N)],
        out_shape=[SDS((T, D), BF16), SDS((T, N), BF16)],
        compiler_params=_params("parallel"),
    )(h, g, w, b, cos, sin, after)


def _pw1_proj(h, g, w, b):
    T, D = h.shape
    n = w.shape[2]
    N = N_CHIPS * n
    tm = min(512, T)

    def body(h_ref, g_ref, w_ref, b_ref, y_ref, o_ref):
        hh = h_ref[...]
        y = (hh * _rms_rstd(hh) * g_ref[...]).astype(BF16)
        y_ref[...] = y
        for j in range(N_CHIPS):
            cols = slice(j * n, (j + 1) * n)
            o_ref[:, cols] = jnp.dot(y, w_ref[j], preferred_element_type=F32) + b_ref[:, cols]

    return pl.pallas_call(
        body, name="pw1_proj", grid=(T // tm,),
        in_specs=[_rows(tm, D), _whole((1, D)), _whole((N_CHIPS, D, n)), _whole((1, N))],
        out_specs=[_rows(tm, D), _rows(tm, N)],
        out_shape=[SDS((T, D), BF16), SDS((T, N), F32)],
        compiler_params=_params("parallel"),
    )(h, g, w, b)


PAIRS_PER_KV = Q_PER_KV // 2


def _upper_lanes(shape):
    return lax.broadcasted_iota(jnp.int32, shape, 1) >= HEAD_DIM


def _swap_lane_halves(t):
    return pltpu.roll(t.astype(F32), HEAD_DIM, 1).astype(t.dtype)


def _kv_operands(g, t):
    swapped = _swap_lane_halves(t)
    in_lower, in_upper = (t, swapped) if g == 0 else (swapped, t)
    upper = _upper_lanes(t.shape)
    zero = jnp.zeros_like(t)
    return jnp.where(upper, zero, in_lower), jnp.where(upper, in_upper, zero)


def _group_heads(g):
    pairs = range(g * PAIRS_PER_KV, (g + 1) * PAIRS_PER_KV)
    return [2 * hp for hp in pairs] + [2 * hp + 1 for hp in pairs]


def _all_heads():
    return [h for g in range(N_KV_HEADS) for h in _group_heads(g)]


def _pair_rows(ref, g):
    pairs = range(g * PAIRS_PER_KV, (g + 1) * PAIRS_PER_KV)
    return jnp.concatenate([ref[:, hp * 2 * HEAD_DIM:(hp + 1) * 2 * HEAD_DIM] for hp in pairs], axis=0)


def _from_previous_block(rows):
    row = lax.broadcasted_iota(jnp.int32, (ATTN_BLOCK, ATTN_BLOCK), 0)
    col = lax.broadcasted_iota(jnp.int32, (ATTN_BLOCK, ATTN_BLOCK), 1)
    return jnp.concatenate([col > row] * (rows // ATTN_BLOCK), axis=0)


def _folded_probs(n, q_groups, k_prev_groups, k_cur_groups, sink_ref, prev_part):
    def scores(q, k):
        return lax.dot_general(q, k, (((1,), (1,)), ((), ())), preferred_element_type=F32)

    s_prev = jnp.concatenate([scores(q, k[i]) for q, k in zip(q_groups, k_prev_groups) for i in range(2)], axis=0)
    s_cur = jnp.concatenate([scores(q, k[i]) for q, k in zip(q_groups, k_cur_groups) for i in range(2)], axis=0)
    s_prev = jnp.where(n > 0, s_prev, MASK_VALUE * (HEAD_DIM ** 0.5))
    s = jnp.where(prev_part, s_prev, s_cur) * (HEAD_DIM ** -0.5)
    heads = [h for g in range(N_KV_HEADS) for h in _group_heads(g)]
    sink = jnp.concatenate([jnp.broadcast_to(sink_ref[0:1, h:h + 1], (ATTN_BLOCK, 1)) for h in heads], axis=0)
    m = jnp.maximum(jnp.max(s, axis=1, keepdims=True), sink)
    p = jnp.exp(s - m)
    e_sink = jnp.exp(sink - m)
    inv = 1.0 / (jnp.sum(p, axis=1, keepdims=True) + e_sink)
    return p * inv, e_sink * inv


def _split_folded(t, prev_part):
    tb = t.astype(BF16)
    zero = jnp.zeros_like(tb)
    return jnp.where(prev_part, tb, zero), jnp.where(prev_part, zero, tb)


def _attn_specs(T):
    nb = T // ATTN_BLOCK
    kcol = N_Q_HEADS * HEAD_DIM // V7X_LANES
    cur = lambda n: jnp.minimum(n, nb - 1)
    prev = lambda n: jnp.maximum(jnp.minimum(n, nb - 1) - 1, 0)
    q_spec = pl.BlockSpec((ATTN_BLOCK, N_Q_HEADS * HEAD_DIM), lambda n: (cur(n), 0))
    kc_spec = pl.BlockSpec((ATTN_BLOCK, V7X_LANES), lambda n: (cur(n), kcol))
    kp_spec = pl.BlockSpec((ATTN_BLOCK, V7X_LANES), lambda n: (prev(n), kcol))
    vc_spec = pl.BlockSpec((ATTN_BLOCK, V7X_LANES), lambda n: (cur(n), kcol + 1))
    vp_spec = pl.BlockSpec((ATTN_BLOCK, V7X_LANES), lambda n: (prev(n), kcol + 1))
    return q_spec, kc_spec, kp_spec, vc_spec, vp_spec


def _attn_fwd(qkv, sinks):
    T = qkv.shape[0]
    nb = T // ATTN_BLOCK
    qw = N_Q_HEADS * HEAD_DIM
    all_rows = N_Q_HEADS * ATTN_BLOCK

    def body(q_ref, kc_ref, kp_ref, vc_ref, vp_ref, sink_ref, o_ref, probs_ref, psink_ref):
        n = pl.program_id(0)
        prev_part = _from_previous_block(all_rows)
        half = PAIRS_PER_KV * ATTN_BLOCK
        groups = range(N_KV_HEADS)
        probs, p_sink = _folded_probs(n, [_pair_rows(q_ref, g) for g in groups],
                                      [_kv_operands(g, kp_ref[...]) for g in groups],
                                      [_kv_operands(g, kc_ref[...]) for g in groups], sink_ref, prev_part)
        probs_ref[...] = probs.astype(BF16)
        lane = lax.broadcasted_iota(jnp.int32, (ATTN_BLOCK, V7X_LANES), 1)
        sink_tile = jnp.zeros((ATTN_BLOCK, V7X_LANES), F32)
        for i, h in enumerate(_all_heads()):
            sink_tile = jnp.where(lane == h, p_sink[i * ATTN_BLOCK:(i + 1) * ATTN_BLOCK], sink_tile)
        psink_ref[...] = sink_tile
        p_prev, p_cur = _split_folded(probs, prev_part)
        for g in groups:
            v_prev, v_cur = _kv_operands(g, vp_ref[...]), _kv_operands(g, vc_ref[...])
            even, odd = slice(2 * g * half, (2 * g + 1) * half), slice((2 * g + 1) * half, (2 * g + 2) * half)
            o = (jnp.dot(p_prev[even], v_prev[0], preferred_element_type=F32)
                 + jnp.dot(p_cur[even], v_cur[0], preferred_element_type=F32)
                 + jnp.dot(p_prev[odd], v_prev[1], preferred_element_type=F32)
                 + jnp.dot(p_cur[odd], v_cur[1], preferred_element_type=F32))
            for i in range(PAIRS_PER_KV):
                hp = g * PAIRS_PER_KV + i
                o_ref[:, hp * 2 * HEAD_DIM:(hp + 1) * 2 * HEAD_DIM] = (
                    o[i * ATTN_BLOCK:(i + 1) * ATTN_BLOCK].astype(BF16))

    return pl.pallas_call(
        body, name="attn_fwd", grid=(nb,),
        in_specs=[*_attn_specs(T), _whole((1, N_Q_HEADS))],
        out_specs=[_rows(ATTN_BLOCK, qw), pl.BlockSpec((None, all_rows, ATTN_BLOCK), lambda n: (n, 0, 0)),
                   _rows(ATTN_BLOCK, V7X_LANES)],
        out_shape=[SDS((T, qw), BF16), SDS((nb, all_rows, ATTN_BLOCK), BF16), SDS((T, V7X_LANES), F32)],
        compiler_params=_params("parallel"),
    )(qkv, qkv, qkv, qkv, qkv, sinks)


def _mm_res(name, a, w, b, res, g):
    T, K = a.shape
    D = w.shape[1]
    tm = min(512, T)

    def body(a_ref, w_ref, b_ref, r_ref, g_ref, o_ref, f_ref):
        h = jnp.dot(a_ref[...], w_ref[...], preferred_element_type=F32) + b_ref[...] + r_ref[...]
        o_ref[...] = h
        f_ref[...] = (h * _rms_rstd(h) * g_ref[...]).astype(BF16)

    return pl.pallas_call(
        body, name=name, grid=(T // tm,),
        in_specs=[_rows(tm, K), _whole((K, D)), _whole((1, D)), _rows(tm, D), _whole((1, D))],
        out_specs=[_rows(tm, D), _rows(tm, D)],
        out_shape=[SDS((T, D), F32), SDS((T, D), BF16)],
        compiler_params=_params("parallel"),
    )(a, w, b, res, g)


def _ffn_down(name, s, w2, layer, res):
    _, T, n = s.shape
    D = w2.shape[3]
    tm = min(512, T)

    def body(s_ref, w_ref, r_ref, o_ref):
        acc = r_ref[...]
        for j in range(N_CHIPS):
            acc = acc + jnp.dot(s_ref[j], w_ref[j], preferred_element_type=F32)
        o_ref[...] = acc

    return pl.pallas_call(
        body, name=name, grid=(T // tm,),
        in_specs=[pl.BlockSpec((N_CHIPS, tm, n), lambda i: (0, i, 0)),
                  pl.BlockSpec((N_CHIPS, None, n, D), lambda i: (0, layer, 0, 0)), _rows(tm, D)],
        out_specs=_rows(tm, D),
        out_shape=SDS((T, D), F32),
        compiler_params=_params("parallel"),
    )(s, w2, res)


def _ffn_up(name, f, w1, w3, layer, after=None):
    T, D = f.shape
    n = w1.shape[2]
    tm = min(1024, T)

    def body(f_ref, w1_ref, w3_ref, *rest):
        act_ref, gg_ref, s_ref = rest[-3:]
        ff = f_ref[...]
        g1 = _dot_nt(ff, w1_ref[...])
        g3 = _dot_nt(ff, w3_ref[...])
        act, dact = _silu_and_grad(g1)
        act_ref[...] = act.astype(BF16)
        gg_ref[...] = (g3 * dact).astype(BF16)
        s_ref[...] = (act * g3).astype(BF16)

    slab = pl.BlockSpec((None, tm, n), lambda j, i: (j, i, 0))
    wslab = pl.BlockSpec((None, None, n, D), lambda j, i: (j, layer, 0, 0))
    hidden = SDS((N_CHIPS, T, n), BF16)
    return pl.pallas_call(
        body, name=name, grid=(N_CHIPS, T // tm),
        in_specs=[pl.BlockSpec((tm, D), lambda j, i: (i, 0)), wslab, wslab]
        + ([] if after is None else [pl.BlockSpec(memory_space=pl.ANY)]),
        out_specs=[slab, slab, slab],
        out_shape=[hidden, hidden, hidden],
        compiler_params=_params("parallel", "parallel"),
    )(f, w1, w3, *([] if after is None else [after]))


def _glu(a, d):
    return a[:, :d] * jax.nn.sigmoid(a[:, d:])


def _conv_tile(T):
    return min(256, T)


def _fill_shifted(sh_ref, tc):
    n = tc + CONV_HALO - V7X_SUBLANES
    for r in range(1, V7X_SUBLANES):
        sh_ref[r, 0:n, :] = sh_ref[0, pl.ds(r, n), :]


def _depthwise_taps(sh_ref, w_ref, offsets, bias_ref, out_ref, tc):
    D = out_ref.shape[1]

    def chunk(i, carry):
        t0 = pl.multiple_of(i * CONV_ROW_CHUNK, CONV_ROW_CHUNK)
        for cb in range(D // CONV_LANE_CHUNK):
            cs = slice(cb * CONV_LANE_CHUNK, (cb + 1) * CONV_LANE_CHUNK)
            acc = jnp.zeros((CONV_ROW_CHUNK, CONV_LANE_CHUNK), F32)
            for r in range(V7X_SUBLANES):
                taps = [(j, o // V7X_SUBLANES) for j, o in enumerate(offsets) if o % V7X_SUBLANES == r]
                if not taps:
                    continue
                span = CONV_ROW_CHUNK + V7X_SUBLANES * max(q for _, q in taps)
                rows = sh_ref[r, pl.ds(t0, span), cs]
                for j, q in taps:
                    acc = acc + rows[V7X_SUBLANES * q:V7X_SUBLANES * q + CONV_ROW_CHUNK] * w_ref[j:j + 1, cs]
            if bias_ref is not None:
                acc = acc + bias_ref[:, cs]
            out_ref[pl.ds(t0, CONV_ROW_CHUNK), cs] = acc
        return carry

    lax.fori_loop(0, tc // CONV_ROW_CHUNK, chunk, 0)


def _depthwise_tap_grads(dy_sh, x_sh, offsets, dw_ref, tc):
    D = dw_ref.shape[1]
    for cb in range(D // V7X_LANES):
        cs = slice(cb * V7X_LANES, (cb + 1) * V7X_LANES)

        def row_tiles(i, accs, cs=cs):
            for k in range(CONV_GRAD_UNROLL):
                t0 = pl.multiple_of(i * (CONV_GRAD_UNROLL * V7X_SUBLANES), V7X_SUBLANES) + k * V7X_SUBLANES
                d = dy_sh[0, pl.ds(t0, V7X_SUBLANES), cs]
                accs = tuple(
                    acc + d * x_sh[o % V7X_SUBLANES, pl.ds(t0 + o // V7X_SUBLANES * V7X_SUBLANES, V7X_SUBLANES), cs]
                    for acc, o in zip(accs, offsets))
            return accs

        zero = jnp.zeros((V7X_SUBLANES, V7X_LANES), F32)
        accs = lax.fori_loop(0, tc // (CONV_GRAD_UNROLL * V7X_SUBLANES), row_tiles, tuple(zero for _ in offsets))
        for j, acc in enumerate(accs):
            dw_ref[j:j + 1, cs] += jnp.sum(acc, axis=0, keepdims=True)


def _conv_fwd(a, w_dw, b_dw, ln_g, ln_b, w_pw2, b_pw2, res, g_next):
    T = a.shape[0]
    D = a.shape[1] // 2
    tc = _conv_tile(T)
    per = tc // CONV_HALO

    def body(a_ref, ah_ref, w_ref, bdw_ref, lg_ref, lb_ref, wp_ref, bp_ref, r_ref, g_ref,
             c_ref, act_ref, h_ref, f_ref, u_sh):
        i = pl.program_id(0)
        u_sh[0, 0:CONV_HALO, :] = jnp.where(i > 0, _glu(ah_ref[...], D), 0.0)
        u_sh[0, CONV_HALO:, :] = _glu(a_ref[...], D)
        _fill_shifted(u_sh, tc)
        _depthwise_taps(u_sh, w_ref, [CONV_FIRST_TAP + j for j in range(CONV_WIDTH)], bdw_ref, c_ref, tc)
        c = c_ref[...]
        xc = c - jnp.mean(c, axis=-1, keepdims=True)
        z = xc * lax.rsqrt(jnp.mean(xc * xc, axis=-1, keepdims=True) + LN_EPS)
        l = z * lg_ref[...] + lb_ref[...]
        act = (l * jax.nn.sigmoid(l)).astype(BF16)
        act_ref[...] = act
        h = jnp.dot(act, wp_ref[...], preferred_element_type=F32) + bp_ref[...] + r_ref[...]
        h_ref[...] = h
        f_ref[...] = (h * _rms_rstd(h) * g_ref[...]).astype(BF16)

    return pl.pallas_call(
        body, name="conv_fwd", grid=(T // tc,),
        in_specs=[_rows(tc, 2 * D),
                  pl.BlockSpec((CONV_HALO, 2 * D), lambda i: (jnp.maximum(i * per - 1, 0), 0)),
                  _whole((CONV_WIDTH, D)), _whole((1, D)), _whole((1, D)), _whole((1, D)),
                  _whole((D, D)), _whole((1, D)), _rows(tc, D), _whole((1, D))],
        out_specs=[_rows(tc, D), _rows(tc, D), _rows(tc, D), _rows(tc, D)],
        out_shape=[SDS((T, D), F32), SDS((T, D), BF16), SDS((T, D), F32), SDS((T, D), BF16)],
        scratch_shapes=[pltpu.VMEM((V7X_SUBLANES, tc + CONV_HALO, D), F32)],
        compiler_params=_params("parallel"),
    )(a, a, w_dw, b_dw, ln_g, ln_b, w_pw2, b_pw2, res, g_next)


def _ffn_down_loss(name, s, w2, layer, res, g, target):
    _, T, n = s.shape
    D = w2.shape[3]
    tm = min(512, T)

    def body(s_ref, w_ref, r_ref, g_ref, t_ref, dh_ref, loss_ref, dg_ref):
        @pl.when(pl.program_id(0) == 0)
        def _():
            loss_ref[...] = jnp.zeros_like(loss_ref)
            dg_ref[...] = jnp.zeros_like(dg_ref)

        hh = r_ref[...]
        for j in range(N_CHIPS):
            hh = hh + jnp.dot(s_ref[j], w_ref[j], preferred_element_type=F32)
        r = _rms_rstd(hh)
        g = g_ref[...]
        d = hh * r * g - t_ref[...]
        loss_ref[...] += 0.5 * jnp.sum(jnp.mean(d * d, axis=-1, keepdims=True), axis=0, keepdims=True)
        dout = d * (1.0 / D)
        dg_ref[...] += jnp.sum(dout * (hh * r), axis=0, keepdims=True)
        dxh = dout * g
        dh_ref[...] = r * dxh - hh * (r * r * r) * jnp.mean(dxh * hh, axis=-1, keepdims=True)

    return pl.pallas_call(
        body, name=name, grid=(T // tm,),
        in_specs=[pl.BlockSpec((N_CHIPS, tm, n), lambda i: (0, i, 0)),
                  pl.BlockSpec((N_CHIPS, None, n, D), lambda i: (0, layer, 0, 0)), _rows(tm, D),
                  _whole((1, D)), _rows(tm, D)],
        out_specs=[_rows(tm, D), _whole((1, 1)), _whole((1, D))],
        out_shape=[SDS((T, D), F32), SDS((1, 1), F32), SDS((1, D), F32)],
        compiler_params=_params("arbitrary"),
    )(s, w2, res, g, target)


def _ffn_bwd_down(name, dh, w2, layer, act, gate_grad, s, after=None):
    T, D = dh.shape
    n = w2.shape[2]
    tm = min(256, T)

    def body(dh_ref, w2_ref, act_ref, gg_ref, s_ref, *rest):
        dg1_ref, dg3_ref, dw_ref = rest[-3:]

        @pl.when(pl.program_id(0) == 0)
        def _():
            dw_ref[...] = jnp.zeros_like(dw_ref)

        dhb = dh_ref[...].astype(BF16)
        for j in range(N_CHIPS):
            ds = _dot_nt(dhb, w2_ref[j])
            dg1_ref[j] = (ds * gg_ref[j].astype(F32)).astype(BF16)
            dg3_ref[j] = (ds * act_ref[j].astype(F32)).astype(BF16)
            dw_ref[j] += _dot_tn(s_ref[j], dhb)

    slabs = pl.BlockSpec((N_CHIPS, tm, n), lambda i: (0, i, 0))
    hidden = SDS((N_CHIPS, T, n), BF16)
    return pl.pallas_call(
        body, name=name, grid=(T // tm,),
        in_specs=[_rows(tm, D),
                  pl.BlockSpec((N_CHIPS, None, n, D), lambda i: (0, layer, 0, 0), pipeline_mode=pl.Buffered(1)),
                  slabs, slabs, slabs] + ([] if after is None else [pl.BlockSpec(memory_space=pl.ANY)]),
        out_specs=[slabs, slabs, _whole((N_CHIPS, n, D))],
        out_shape=[hidden, hidden, SDS((N_CHIPS, n, D), F32)],
        compiler_params=_params("arbitrary"),
    )(dh, w2, act, gate_grad, s, *([] if after is None else [after]))


def _dot_tn(a, b):
    return lax.dot_general(a.astype(BF16), b.astype(BF16), (((0,), (0,)), ((), ())), preferred_element_type=F32)


def _dot_nt(a, b):
    return lax.dot_general(a.astype(BF16), b, (((1,), (1,)), ((), ())), preferred_element_type=F32)


def _mm_tn(name, a, b, col_chunks=1, after=None):
    a_slabs, b_slabs = a.ndim == 3, b.ndim == 3
    T = a.shape[-2]
    tt = min(1024, T)
    ka, nb = a.shape[-1], b.shape[-1]
    if a_slabs or b_slabs:
        out_dims = (N_CHIPS, ka, nb)
    elif col_chunks > 1:
        out_dims = (col_chunks, ka, nb // col_chunks)
    else:
        out_dims = (ka, nb)

    def body(a_ref, b_ref, *rest):
        o_ref = rest[-1]

        @pl.when(pl.program_id(0) == 0)
        def _():
            o_ref[...] = jnp.zeros_like(o_ref)

        if a_slabs:
            bb = b_ref[...].astype(BF16)
            for j in range(N_CHIPS):
                o_ref[j] += _dot_tn(a_ref[j], bb)
        elif b_slabs:
            aa = a_ref[...].astype(BF16)
            for j in range(N_CHIPS):
                o_ref[j] += _dot_tn(aa, b_ref[j])
        elif col_chunks > 1:
            aa = a_ref[...].astype(BF16)
            w = nb // col_chunks
            for j in range(col_chunks):
                o_ref[j] += _dot_tn(aa, b_ref[:, j * w:(j + 1) * w])
        else:
            o_ref[...] += _dot_tn(a_ref[...], b_ref[...])

    def spec(arr, slabs):
        if slabs:
            return pl.BlockSpec((N_CHIPS, tt, arr.shape[-1]), lambda t: (0, t, 0))
        return _rows(tt, arr.shape[-1])

    return pl.pallas_call(
        body, name=name, grid=(T // tt,),
        in_specs=[spec(a, a_slabs), spec(b, b_slabs)] + ([] if after is None else [pl.BlockSpec(memory_space=pl.ANY)]),
        out_specs=_whole(out_dims),
        out_shape=SDS(out_dims, F32),
        compiler_params=_params("arbitrary"),
    )(a, b, *([] if after is None else [after]))


def _mm_nt_normbwd(name, pairs, h, g, dh, after):
    T, D = h.shape
    tm = min(512, T)
    n_pairs = len(pairs)
    kinds = ["slabs" if dy.ndim == 3 else ("quarters" if w.ndim == 3 else "plain") for dy, w, _ in pairs]

    def body(*refs):
        dy_refs = refs[:n_pairs]
        w_refs = refs[n_pairs:2 * n_pairs]
        h_ref, g_ref, dh_ref, _, o_ref, dg_ref, cs_ref = refs[2 * n_pairs:]

        @pl.when(pl.program_id(0) == 0)
        def _():
            dg_ref[...] = jnp.zeros_like(dg_ref)
            cs_ref[...] = jnp.zeros_like(cs_ref)

        df = jnp.zeros((tm, D), F32)
        for dy_ref, w_ref, kd in zip(dy_refs, w_refs, kinds):
            if kd == "slabs":
                for j in range(N_CHIPS):
                    df = df + jnp.dot(dy_ref[j], w_ref[j], preferred_element_type=F32)
            elif kd == "quarters":
                n = w_ref.shape[2]
                for j in range(N_CHIPS):
                    df = df + _dot_nt(dy_ref[:, j * n:(j + 1) * n], w_ref[j])
            else:
                df = df + jnp.dot(dy_ref[...], w_ref[...], preferred_element_type=F32)
        hh = h_ref[...]
        r = _rms_rstd(hh)
        dg_ref[...] += jnp.sum(df * (hh * r), axis=0, keepdims=True)
        dxh = df * g_ref[...]
        out = dh_ref[...] + (r * dxh - hh * (r * r * r) * jnp.mean(dxh * hh, axis=-1, keepdims=True))
        o_ref[...] = out
        cs_ref[...] += jnp.sum(out, axis=0, keepdims=True)

    dy_specs, w_specs = [], []
    for (dy, w, layer), kd in zip(pairs, kinds):
        if kd == "slabs":
            dy_specs.append(pl.BlockSpec((N_CHIPS, tm, dy.shape[2]), lambda i: (0, i, 0)))
            w_specs.append(pl.BlockSpec((N_CHIPS, None, w.shape[2], D),
                                        functools.partial(lambda i, layer: (0, layer, 0, 0), layer=layer),
                                        pipeline_mode=pl.Buffered(1)))
        else:
            dy_specs.append(_rows(tm, dy.shape[1]))
            w_specs.append(_whole(w.shape))

    return pl.pallas_call(
        body, name=name, grid=(T // tm,),
        in_specs=[*dy_specs, *w_specs, _rows(tm, D), _whole((1, D)), _rows(tm, D), pl.BlockSpec(memory_space=pl.ANY)],
        out_specs=[_rows(tm, D), _whole((1, D)), _whole((1, D))],
        out_shape=[SDS((T, D), F32), SDS((1, D), F32), SDS((1, D), F32)],
        compiler_params=_params("arbitrary"),
    )(*[dy for dy, _, _ in pairs], *[w for _, w, _ in pairs], h, g, dh, after)


def _mm_nt(name, dy, w, out_dtype):
    T, N = dy.shape
    K = w.shape[0]
    tm = min(512, T)

    def body(dy_ref, w_ref, o_ref):
        o_ref[...] = lax.dot_general(dy_ref[...].astype(BF16), w_ref[...], (((1,), (1,)), ((), ())),
                                     preferred_element_type=F32).astype(out_dtype)

    return pl.pallas_call(
        body, name=name, grid=(T // tm,),
        in_specs=[_rows(tm, N), _whole((K, N))],
        out_specs=_rows(tm, K),
        out_shape=SDS((T, K), out_dtype),
        compiler_params=_params("parallel"),
    )(dy, w)


def _conv_bwd(dh, w_pw2, c, a, w_dw, ln_g, ln_b):
    T, D = c.shape
    tc = _conv_tile(T)
    per = tc // CONV_HALO
    n_tiles = T // tc
    last_halo = T // CONV_HALO - 1

    def ln_bwd(dact_v, c_v, lg, lb):
        xc = c_v - jnp.mean(c_v, axis=-1, keepdims=True)
        rstd = lax.rsqrt(jnp.mean(xc * xc, axis=-1, keepdims=True) + LN_EPS)
        z = xc * rstd
        _, dsilu = _silu_and_grad(z * lg + lb)
        dl = dact_v * dsilu
        dz = dl * lg
        dc = rstd * (dz - jnp.mean(dz, axis=-1, keepdims=True) - z * jnp.mean(dz * z, axis=-1, keepdims=True))
        return dc, dl, z

    def body(dh_ref, dhn_ref, wp_ref, c_ref, cn_ref, a_ref, ah_ref, w_ref, lg_ref, lb_ref,
             da_ref, dlg_ref, dlb_ref, dbdw_ref, dwdw_ref, dbpw1_ref, dc_sh, u_sh, du_scr):
        i = pl.program_id(0)

        @pl.when(i == 0)
        def _():
            for ref in (dlg_ref, dlb_ref, dbdw_ref, dwdw_ref, dbpw1_ref):
                ref[...] = jnp.zeros_like(ref)

        lg, lb = lg_ref[...], lb_ref[...]
        dh_rows = jnp.concatenate([dh_ref[...].astype(BF16), dhn_ref[...].astype(BF16)], axis=0)
        dact = _dot_nt(dh_rows, wp_ref[...])
        dc, dl, z = ln_bwd(dact[:tc], c_ref[...], lg, lb)
        dlg_ref[...] += jnp.sum(dl * z, axis=0, keepdims=True)
        dlb_ref[...] += jnp.sum(dl, axis=0, keepdims=True)
        dbdw_ref[...] += jnp.sum(dc, axis=0, keepdims=True)
        dcn, _, _ = ln_bwd(dact[tc:], cn_ref[...], lg, lb)
        dc_sh[0, 0:tc, :] = dc
        dc_sh[0, tc:, :] = jnp.where(i < n_tiles - 1, dcn, 0.0)
        _fill_shifted(dc_sh, tc)

        a_v = a_ref[...]
        a1 = a_v[:, :D]
        sg = jax.nn.sigmoid(a_v[:, D:])
        u_sh[0, 0:CONV_HALO, :] = jnp.where(i > 0, _glu(ah_ref[...], D), 0.0)
        u_sh[0, CONV_HALO:, :] = a1 * sg
        _fill_shifted(u_sh, tc)

        _depthwise_taps(dc_sh, w_ref, [CONV_WIDTH - 1 - j for j in range(CONV_WIDTH)], None, du_scr, tc)
        _depthwise_tap_grads(dc_sh, u_sh, [CONV_FIRST_TAP + j for j in range(CONV_WIDTH)], dwdw_ref, tc)

        du = du_scr[...]
        da1 = du * sg
        da2 = du * a1 * sg * (1.0 - sg)
        da_ref[:, :D] = da1.astype(BF16)
        da_ref[:, D:] = da2.astype(BF16)
        dbpw1_ref[:, :D] += jnp.sum(da1, axis=0, keepdims=True)
        dbpw1_ref[:, D:] += jnp.sum(da2, axis=0, keepdims=True)

    nxt = lambda i: (jnp.minimum((i + 1) * per, last_halo), 0)
    return pl.pallas_call(
        body, name="conv_bwd", grid=(n_tiles,),
        in_specs=[_rows(tc, D), pl.BlockSpec((CONV_HALO, D), nxt), _whole((D, D)),
                  _rows(tc, D), pl.BlockSpec((CONV_HALO, D), nxt),
                  _rows(tc, 2 * D),
                  pl.BlockSpec((CONV_HALO, 2 * D), lambda i: (jnp.maximum(i * per - 1, 0), 0)),
                  _whole((CONV_WIDTH, D)), _whole((1, D)), _whole((1, D))],
        out_specs=[_rows(tc, 2 * D), _whole((1, D)), _whole((1, D)), _whole((1, D)),
                   _whole((CONV_HALO, D)), _whole((1, 2 * D))],
        out_shape=[SDS((T, 2 * D), BF16), SDS((1, D), F32), SDS((1, D), F32), SDS((1, D), F32),
                   SDS((CONV_HALO, D), F32), SDS((1, 2 * D), F32)],
        scratch_shapes=[pltpu.VMEM((V7X_SUBLANES, tc + CONV_HALO, D), F32),
                        pltpu.VMEM((V7X_SUBLANES, tc + CONV_HALO, D), F32), pltpu.VMEM((tc, D), F32)],
        compiler_params=_params("arbitrary"),
    )(dh, dh, w_pw2, c, c, a, a, w_dw, ln_g, ln_b)


def _attn_bwd(qkv, dao, cos, sin, probs_saved, psink_saved):
    T = qkv.shape[0]
    nb = T // ATTN_BLOCK
    qw = N_Q_HEADS * HEAD_DIM
    kw = N_KV_HEADS * HEAD_DIM

    def body(q_ref, kc_ref, kp_ref, vc_ref, vp_ref, do_ref, cos_ref, sin_ref, cosp_ref, sinp_ref, probs_ref, psink_ref,
             dq_ref, dkv_ref, dsink_ref, dbq_ref, dbkv_ref, carry, prev_scr, cur_scr, dq_scr):
        n = pl.program_id(0)

        @pl.when(n == 0)
        def _():
            for ref in (dsink_ref, dbq_ref, dbkv_ref, carry):
                ref[...] = jnp.zeros_like(ref)

        @pl.when(n == nb)
        def _():
            prev_scr[...] = jnp.zeros_like(prev_scr)

        @pl.when(n < nb)
        def _():
            prev_part = _from_previous_block(N_Q_HEADS * ATTN_BLOCK)
            half = PAIRS_PER_KV * ATTN_BLOCK
            upper = _upper_lanes((ATTN_BLOCK, 2 * HEAD_DIM))
            groups = range(N_KV_HEADS)

            def nt(a, b):
                return lax.dot_general(a, b, (((1,), (1,)), ((), ())), preferred_element_type=F32)

            def kv_grad(even_rows, odd_rows, x):
                even = lax.dot_general(even_rows, x, (((0,), (0,)), ((), ())), preferred_element_type=F32)
                odd = lax.dot_general(odd_rows, x, (((0,), (0,)), ((), ())), preferred_element_type=F32)
                t = jnp.where(upper, odd, even)
                return t + _swap_lane_halves(t)

            q = [_pair_rows(q_ref, g) for g in groups]
            do = [_pair_rows(do_ref, g) for g in groups]
            k_prev = [_kv_operands(g, kp_ref[...]) for g in groups]
            k_cur = [_kv_operands(g, kc_ref[...]) for g in groups]
            v_prev = [_kv_operands(g, vp_ref[...]) for g in groups]
            v_cur = [_kv_operands(g, vc_ref[...]) for g in groups]
            probs = probs_ref[...].astype(F32)
            dp_prev = jnp.concatenate([nt(do[g], v_prev[g][i]) for g in groups for i in range(2)], axis=0)
            dp_cur = jnp.concatenate([nt(do[g], v_cur[g][i]) for g in groups for i in range(2)], axis=0)
            dp = jnp.where(prev_part, dp_prev, dp_cur)
            delta = jnp.sum(probs * dp, axis=1, keepdims=True)
            ds_prev, ds_cur = _split_folded(probs * (dp - delta) * (HEAD_DIM ** -0.5), prev_part)
            p_prev, p_cur = _split_folded(probs_ref[...], prev_part)
            for i, h in enumerate(_all_heads()):
                rows = slice(i * ATTN_BLOCK, (i + 1) * ATTN_BLOCK)
                dsink_ref[:, h:h + 1] += jnp.sum(-(psink_ref[:, h:h + 1] * delta[rows]), axis=0, keepdims=True)
            kv_grads = []
            for g in groups:
                even, odd = slice(2 * g * half, (2 * g + 1) * half), slice((2 * g + 1) * half, (2 * g + 2) * half)
                dq = (jnp.dot(ds_prev[even], k_prev[g][0], preferred_element_type=F32)
                      + jnp.dot(ds_cur[even], k_cur[g][0], preferred_element_type=F32)
                      + jnp.dot(ds_prev[odd], k_prev[g][1], preferred_element_type=F32)
                      + jnp.dot(ds_cur[odd], k_cur[g][1], preferred_element_type=F32))
                for i in range(PAIRS_PER_KV):
                    hp = g * PAIRS_PER_KV + i
                    dq_scr[:, hp * 2 * HEAD_DIM:(hp + 1) * 2 * HEAD_DIM] = dq[i * ATTN_BLOCK:(i + 1) * ATTN_BLOCK]
                kv_grads.append((kv_grad(ds_prev[even], ds_prev[odd], q[g]), kv_grad(ds_cur[even], ds_cur[odd], q[g]),
                                 kv_grad(p_prev[even], p_prev[odd], do[g]), kv_grad(p_cur[even], p_cur[odd], do[g])))
            (dkp0, dkc0, dvp0, dvc0), (dkp1, dkc1, dvp1, dvc1) = kv_grads
            prev_scr[:, :kw] = jnp.where(upper, dkp1, dkp0)
            prev_scr[:, kw:] = jnp.where(upper, dvp1, dvp0)
            cur_scr[:, :kw] = jnp.where(upper, dkc1, dkc0)
            cur_scr[:, kw:] = jnp.where(upper, dvc1, dvc0)
            dq_pre = _rope_transposed(dq_scr[...], cos_ref, sin_ref)
            dq_ref[...] = dq_pre.astype(BF16)
            dbq_ref[...] += jnp.sum(dq_pre, axis=0, keepdims=True)

        tot = carry[...] + prev_scr[...]
        dk_pre = _rope_transposed(tot[:, :kw], cosp_ref, sinp_ref)
        dkv_ref[:, :kw] = dk_pre.astype(BF16)
        dkv_ref[:, kw:] = tot[:, kw:].astype(BF16)
        dbkv_ref[:, :kw] += jnp.sum(dk_pre, axis=0, keepdims=True)
        dbkv_ref[:, kw:] += jnp.sum(tot[:, kw:], axis=0, keepdims=True)

        @pl.when(n < nb)
        def _():
            carry[...] = cur_scr[...]

    cur = lambda n: (jnp.minimum(n, nb - 1), 0)
    out_lag = lambda n: (jnp.maximum(n - 1, 0), 0)
    return pl.pallas_call(
        body, name="attn_bwd", grid=(nb + 1,),
        in_specs=[*_attn_specs(T),
                  pl.BlockSpec((ATTN_BLOCK, qw), cur),
                  pl.BlockSpec((ATTN_BLOCK, V7X_LANES), cur), pl.BlockSpec((ATTN_BLOCK, V7X_LANES), cur),
                  pl.BlockSpec((ATTN_BLOCK, V7X_LANES), out_lag), pl.BlockSpec((ATTN_BLOCK, V7X_LANES), out_lag),
                  pl.BlockSpec((None, N_Q_HEADS * ATTN_BLOCK, ATTN_BLOCK), lambda n: (jnp.minimum(n, nb - 1), 0, 0)),
                  pl.BlockSpec((ATTN_BLOCK, V7X_LANES), cur)],
        out_specs=[pl.BlockSpec((ATTN_BLOCK, qw), cur), pl.BlockSpec((ATTN_BLOCK, 2 * kw), out_lag),
                   _whole((1, N_Q_HEADS)), _whole((1, qw)), _whole((1, 2 * kw))],
        out_shape=[SDS((T, qw), BF16), SDS((T, 2 * kw), BF16),
                   SDS((1, N_Q_HEADS), F32), SDS((1, qw), F32), SDS((1, 2 * kw), F32)],
        scratch_shapes=[pltpu.VMEM((ATTN_BLOCK, 2 * kw), F32), pltpu.VMEM((ATTN_BLOCK, 2 * kw), F32),
                        pltpu.VMEM((ATTN_BLOCK, 2 * kw), F32), pltpu.VMEM((ATTN_BLOCK, qw), F32)],
        compiler_params=_params("arbitrary"),
    )(qkv, qkv, qkv, qkv, qkv, dao, cos, sin, cos, sin, probs_saved, psink_saved)


def _local_step(x, target, p, reduce_begin, reduce_send):
    T, D = x.shape
    cos, sin = _rope_tables(T)
    qw = N_Q_HEADS * HEAD_DIM
    nm, nf = p["norm_mix"], p["norm_ffn"]

    y0, qkv = _qkv_proj(x, nm[0:1], p["attn_w_qkv"], p["attn_b_qkv"], cos, sin, p["gather_started"])
    ao, attn_probs, sink_probs = _attn_fwd(qkv, p["attn_sinks"])
    h1, f0 = _mm_res("attn_out", ao, p["attn_w_o"], p["attn_b_o"], x, nf[0:1])
    p = {**p, **p["other_weights"](h1)}
    w1, w3 = p["ffn_w1"], p["ffn_w3"]
    act0, gg0, s0 = _ffn_up("ffn0_up", f0, w1, w3, 0, after=p["swap_started"])
    p = {**p, **p["rest_of_weights"](s0)}
    w2 = p["ffn_w2"]
    h2 = _ffn_down("ffn0_down", s0, w2, 0, h1)
    y1, a = _pw1_proj(h2, nm[1:2], p["conv_w_pw1"], p["conv_b_pw1"])
    c, act, h3, f1 = _conv_fwd(a, p["conv_w_dw"], p["conv_b_dw"], p["conv_ln_g"], p["conv_ln_b"],
                               p["conv_w_pw2"], p["conv_b_pw2"], h2, nf[1:2])
    act1, gg1, s1 = _ffn_up("ffn1_up", f1, w1, w3, 1)
    dh4, loss, d_norm_final = _ffn_down_loss("ffn1_down_loss", s1, w2, 1, h3, p["norm_final"], target)

    g = {}
    dg1, dg3, dw2_1 = _ffn_bwd_down("ffn1_bwd_down", dh4, w2, 1, act1, gg1, s1)
    dw1_1 = _mm_tn("ffn1_dw1", dg1, f1)
    dw3_1 = _mm_tn("ffn1_dw3", dg3, f1)
    begun = reduce_begin("ffn1", {("ffn_w1", 1): dw1_1, ("ffn_w3", 1): dw3_1, ("ffn_w2", 1): dw2_1})
    dh3, dnf1, db_pw2 = _mm_nt_normbwd("ffn1_bwd_in", [(dg1, w1, 1), (dg3, w3, 1)], h3, nf[1:2], dh4, begun)
    sent = reduce_send("ffn1", dh3)

    dw_pw2 = _mm_tn("conv_dw_pw2", act, dh3, after=sent)
    da, d_ln_g, d_ln_b, d_b_dw, d_w_dw, d_b_pw1 = _conv_bwd(dh3, p["conv_w_pw2"], c, a, p["conv_w_dw"],
                                                            p["conv_ln_g"], p["conv_ln_b"])
    dw_pw1 = _mm_tn("conv_dw_pw1", y1, da, col_chunks=N_CHIPS)
    begun = reduce_begin("conv", {("conv_w_pw2", 0): dw_pw2.reshape(N_CHIPS, -1, D), ("conv_w_pw1", 0): dw_pw1})
    dh2, dnm1, _ = _mm_nt_normbwd("conv_bwd_in", [(da, p["conv_w_pw1"], None)], h2, nm[1:2], dh3, begun)
    sent = reduce_send("conv", dh2)

    dg1, dg3, dw2_0 = _ffn_bwd_down("ffn0_bwd_down", dh2, w2, 0, act0, gg0, s0, after=sent)
    dw1_0 = _mm_tn("ffn0_dw1", dg1, f0)
    dw3_0 = _mm_tn("ffn0_dw3", dg3, f0)
    begun = reduce_begin("ffn0", {("ffn_w1", 0): dw1_0, ("ffn_w3", 0): dw3_0, ("ffn_w2", 0): dw2_0})
    dh1, dnf0, db_o = _mm_nt_normbwd("ffn0_bwd_in", [(dg1, w1, 0), (dg3, w3, 0)], h1, nf[0:1], dh2, begun)
    sent = reduce_send("ffn0", dh1)

    dw_o = _mm_tn("attn_dw_o", ao, dh1, after=sent)
    dao = _mm_nt("attn_bwd_out", dh1, p["attn_w_o"], BF16)
    dq, dkv, d_sinks, dbq, dbkv = _attn_bwd(qkv, dao, cos, sin, attn_probs, sink_probs)
    dwq = _mm_tn("attn_dw_q", dq, y0)
    dwkv = _mm_tn("attn_dw_kv", dkv, y0)
    wqkv = p["attn_w_qkv"]
    dwqkv = jnp.concatenate([dwq, dwkv], axis=0).reshape(N_CHIPS, -1, D)
    begun = reduce_begin("attn", {("attn_w_o", 0): dw_o.reshape(N_CHIPS, -1, D), ("attn_w_qkv", 0): dwqkv})
    sent = reduce_send("attn", begun)
    dx, dnm0, _ = _mm_nt_normbwd("attn_bwd_in", [(dq, wqkv[:qw], None), (dkv, wqkv[qw:], None)], x, nm[0:1], dh1,
                                 sent)

    g["norm_mix"] = jnp.concatenate([dnm0, dnm1], axis=0)
    g["norm_ffn"] = jnp.concatenate([dnf0, dnf1], axis=0)
    g["attn_b_qkv"] = jnp.concatenate([dbq, dbkv], axis=1)
    g["attn_sinks"] = d_sinks
    g["attn_b_o"] = db_o
    g["conv_b_pw1"] = d_b_pw1
    g["conv_w_dw"] = d_w_dw[:CONV_WIDTH]
    g["conv_b_dw"] = d_b_dw
    g["conv_ln_g"] = d_ln_g
    g["conv_ln_b"] = d_ln_b
    g["conv_b_pw2"] = db_pw2
    g["norm_final"] = d_norm_final
    return loss, dx, g


ANY = pl.BlockSpec(memory_space=pl.ANY)
VMEM_WHOLE = pl.BlockSpec(memory_space=pltpu.VMEM)


def _my_place():
    return lax.axis_index("x"), lax.axis_index("y"), lax.axis_index("c")


def _other_chips(x, y):
    places = [(1 - x, y), (x, 1 - y), (1 - x, 1 - y)]
    return [(bx, by, 2 * bx + by) for bx, by in places]


def _gather_small(name, v):
    r, w = v.shape

    def body(v_ref, o_ref, send_sems, recv_sems):
        x, y, c = _my_place()
        pairs = _all_to_all_copies(v_ref, o_ref, send_sems, recv_sems)
        for send, _ in pairs:
            send.start()
        o_ref[4 * x + 2 * y + c] = v_ref[...]
        for send, arrival in pairs:
            arrival.wait_recv()
            send.wait_send()

    return pl.pallas_call(
        body, name=name, out_shape=SDS((N_DEV, r, w), F32), in_specs=[VMEM_WHOLE], out_specs=VMEM_WHOLE,
        scratch_shapes=[pltpu.SemaphoreType.DMA((N_DEV - 1,)), pltpu.SemaphoreType.DMA((N_DEV - 1,))],
        compiler_params=pltpu.CompilerParams(vmem_limit_bytes=V7X_VMEM_LIMIT_BYTES),
    )(v)


def _cast_into_slot(name, gathered, shard, chip_idx, after=None):
    rows, cols = shard.shape
    tr = _pack_row_tile(rows)

    def body(k_ref, s_ref, *rest):
        rest[-1][...] = s_ref[...].astype(BF16)

    return pl.pallas_call(
        body, name=name,
        grid_spec=pltpu.PrefetchScalarGridSpec(
            num_scalar_prefetch=1, grid=(rows // tr,),
            in_specs=[pl.BlockSpec((tr, cols), lambda i, k_ref: (i, 0)), pl.BlockSpec(memory_space=pl.ANY)]
            + ([] if after is None else [pl.BlockSpec(memory_space=pl.ANY)]),
            out_specs=pl.BlockSpec((None, tr, cols), lambda i, k_ref: (k_ref[0], i, 0))),
        out_shape=SDS(gathered.shape, BF16),
        input_output_aliases={2: 0},
        compiler_params=_params("parallel"),
    )(chip_idx, shard, gathered, *([] if after is None else [after]))


def _row_halves(ref, c):
    half = ref.shape[1] // 2
    return pl.ds(pl.multiple_of(c * half, 16), half), pl.ds(pl.multiple_of((1 - c) * half, 16), half)


def _gather_ici_copies(refs, send_sems, recv_sems):
    x, y, c = _my_place()
    k = 2 * x + y
    pairs = []
    for i, ref in enumerate(refs):
        mine, _ = _row_halves(ref, c)
        for j, (bx, by, kb) in enumerate(_other_chips(x, y)):
            sems = dict(send_sem=send_sems.at[3 * i + j], recv_sem=recv_sems.at[3 * i + j], device_id_type=MESH)
            send = pltpu.make_async_remote_copy(src_ref=ref.at[k, mine], dst_ref=ref.at[k, mine],
                                                device_id=(bx, by, c), **sems)
            arrival = pltpu.make_async_remote_copy(src_ref=ref.at[kb, mine], dst_ref=ref.at[kb, mine],
                                                   device_id=(bx, by, c), **sems)
            pairs.append((send, arrival))
    return pairs


def _gather_d2d_copies(refs, send_sems, recv_sems):
    x, y, c = _my_place()
    pairs = []
    for i, ref in enumerate(refs):
        mine, theirs = _row_halves(ref, c)
        for j, (_, _, kb) in enumerate(_other_chips(x, y)):
            sem = 3 * i + j
            sems = dict(send_sem=send_sems.at[sem], recv_sem=recv_sems.at[sem], device_id=(x, y, 1 - c),
                        device_id_type=MESH)
            send = pltpu.make_async_remote_copy(src_ref=ref.at[kb, mine], dst_ref=ref.at[kb, mine], **sems)
            arrival = pltpu.make_async_remote_copy(src_ref=ref.at[kb, theirs], dst_ref=ref.at[kb, theirs], **sems)
            pairs.append((send, arrival))
    return pairs


def _run_copies(pairs):
    for send, _ in pairs:
        send.start()
    for send, arrival in pairs:
        send.wait_send()
        arrival.wait_recv()


def _gather_stage_copies(stage, refs, send_sems, recv_sems):
    if stage == "ici":
        return _gather_ici_copies(refs, send_sems, recv_sems)
    return _gather_d2d_copies(refs, send_sems, recv_sems)


def _gather_start(name, gathered, after, stage="ici"):
    n_w = len(gathered)

    def body(*refs):
        in_refs = refs[:n_w]
        send_sems, recv_sems = refs[n_w + 1:n_w + 3]
        for send, _ in _gather_stage_copies(stage, in_refs, send_sems, recv_sems):
            send.start()
        refs[-1][...] = jnp.zeros_like(refs[-1])

    out = pl.pallas_call(
        body, name=name,
        out_shape=(pltpu.SemaphoreType.DMA((3 * n_w,)), pltpu.SemaphoreType.DMA((3 * n_w,)),
                   *[pltpu.HBM(g.shape, g.dtype) for g in gathered], SDS((8, V7X_LANES), F32)),
        in_specs=[*[HBM_SPEC] * n_w, ANY], out_specs=(SEM_SPEC, SEM_SPEC, *[HBM_SPEC] * n_w, VMEM_WHOLE),
        input_output_aliases={i: 2 + i for i in range(n_w)},
        compiler_params=pltpu.CompilerParams(has_side_effects=DATAFLOW),
    )(*[pltpu.with_memory_space_constraint(g, pltpu.HBM) for g in gathered], after)
    return out[0], out[1], list(out[2:2 + n_w]), out[-1]


def _gather_wait(name, send_sems, recv_sems, gathered, after, stage="ici"):
    n_w = len(gathered)

    def body(*refs):
        in_refs = refs[:n_w]
        send_sems, recv_sems = refs[n_w:n_w + 2]
        for send, arrival in _gather_stage_copies(stage, in_refs, send_sems, recv_sems):
            send.wait_send()
            arrival.wait_recv()

    out = pl.pallas_call(
        body, name=name, out_shape=tuple(pltpu.HBM(g.shape, g.dtype) for g in gathered),
        in_specs=[*[HBM_SPEC] * n_w, SEM_SPEC, SEM_SPEC, ANY], out_specs=tuple([HBM_SPEC] * n_w),
        input_output_aliases={i: i for i in range(n_w)},
        compiler_params=pltpu.CompilerParams(has_side_effects=DATAFLOW),
    )(*gathered, send_sems, recv_sems, after)
    return list(out)


def _swap_fetched_with_sibling(name, gathered):
    n_w = len(gathered)

    def body(*refs):
        in_refs = refs[:n_w]
        send_sems, recv_sems = refs[2 * n_w:]
        _run_copies(_gather_d2d_copies(in_refs, send_sems, recv_sems))

    return pl.pallas_call(
        body, name=name, out_shape=[SDS(g.shape, g.dtype) for g in gathered],
        in_specs=[ANY] * n_w, out_specs=[ANY] * n_w, input_output_aliases={i: i for i in range(n_w)},
        scratch_shapes=[pltpu.SemaphoreType.DMA((3 * n_w,)), pltpu.SemaphoreType.DMA((3 * n_w,))],
    )(*gathered)


def _sibling_swap_copies(g_refs, land_refs, send_sems, recv_sems):
    x, y, c = _my_place()
    copies = []
    for i, g_ref in enumerate(g_refs):
        half = g_ref.shape[1] // 2
        theirs = pl.ds(pl.multiple_of((1 - c) * half, 8), half)
        copies.append(pltpu.make_async_remote_copy(
            src_ref=g_ref.at[:, theirs], dst_ref=land_refs[i], send_sem=send_sems.at[i], recv_sem=recv_sems.at[i],
            device_id=(x, y, 1 - c), device_id_type=MESH))
    return copies


def _sibling_swap_start(name, grads):
    n_g = len(grads)

    def body(*refs):
        g_refs, land_refs = refs[:n_g], refs[n_g:2 * n_g]
        send_sems, recv_sems = refs[2 * n_g:2 * n_g + 2]
        for cp in _sibling_swap_copies(g_refs, land_refs, send_sems, recv_sems):
            cp.start()
        refs[-1][...] = jnp.zeros_like(refs[-1])

    lands = [pltpu.with_memory_space_constraint(lax.empty((g.shape[0], g.shape[1] // 2, g.shape[2]), g.dtype),
                                                pltpu.HBM) for g in grads]
    out = pl.pallas_call(
        body, name=name,
        out_shape=(pltpu.SemaphoreType.DMA((n_g,)), pltpu.SemaphoreType.DMA((n_g,)),
                   *[pltpu.HBM(g.shape, g.dtype) for g in grads], *[pltpu.HBM(l.shape, l.dtype) for l in lands],
                   SDS((8, V7X_LANES), F32)),
        in_specs=[HBM_SPEC] * (2 * n_g), out_specs=(SEM_SPEC, SEM_SPEC, *[HBM_SPEC] * (2 * n_g), VMEM_WHOLE),
        input_output_aliases={i: 2 + i for i in range(2 * n_g)},
        compiler_params=pltpu.CompilerParams(has_side_effects=DATAFLOW),
    )(*[pltpu.with_memory_space_constraint(g, pltpu.HBM) for g in grads], *lands)
    return out[0], out[1], list(out[2:2 + n_g]), list(out[2 + n_g:2 + 2 * n_g]), out[-1]


def _sibling_swap_wait(name, send_sems, recv_sems, grads, lands, after):
    n_g = len(grads)

    def body(*refs):
        g_refs, land_refs = refs[:n_g], refs[n_g:2 * n_g]
        send_sems, recv_sems = refs[2 * n_g:2 * n_g + 2]
        for cp in _sibling_swap_copies(g_refs, land_refs, send_sems, recv_sems):
            cp.wait_send()
            cp.wait_recv()

    out = pl.pallas_call(
        body, name=name,
        out_shape=(*[pltpu.HBM(g.shape, g.dtype) for g in grads], *[pltpu.HBM(l.shape, l.dtype) for l in lands]),
        in_specs=[*[HBM_SPEC] * (2 * n_g), SEM_SPEC, SEM_SPEC, ANY], out_specs=tuple([HBM_SPEC] * (2 * n_g)),
        input_output_aliases={i: i for i in range(2 * n_g)},
        compiler_params=pltpu.CompilerParams(has_side_effects=DATAFLOW),
    )(*grads, *lands, send_sems, recv_sems, after)
    return list(out[:n_g]), list(out[n_g:])


def _pack_row_tile(rows):
    for t in range(min(rows, 512), 7, -1):
        if rows % t == 0 and t % 8 == 0:
            return t
    return rows


def _add_sibling_half(name, grads, from_sibling, c_idx):
    n, R, w = grads.shape
    half = R // 2
    tr = _pack_row_tile(half)
    steps = half // tr

    def body(c_ref, g_ref, s_ref, o_ref):
        o_ref[...] = (g_ref[...] + s_ref[...]).astype(BF16)

    return pl.pallas_call(
        body, name=name,
        grid_spec=pltpu.PrefetchScalarGridSpec(
            num_scalar_prefetch=1, grid=(n, steps),
            in_specs=[pl.BlockSpec((1, tr, w), lambda j, i, c_ref: (j, c_ref[0] * steps + i, 0)),
                      pl.BlockSpec((1, tr, w), lambda j, i, c_ref: (j, i, 0))],
            out_specs=pl.BlockSpec((1, tr, w), lambda j, i, c_ref: (j, i, 0))),
        out_shape=SDS((n, half, w), BF16),
        compiler_params=_params("parallel", "parallel"),
    )(c_idx, grads, from_sibling)


HBM_SPEC = pl.BlockSpec(memory_space=pltpu.HBM)
SEM_SPEC = pl.BlockSpec(memory_space=pltpu.SEMAPHORE)
DATAFLOW = pltpu.SideEffectType.DATAFLOW_SIDE_EFFECTING


def _chip_scatter_copies(p_refs, land_refs, send_sems, recv_sems):
    x, y, c = _my_place()
    return [pltpu.make_async_remote_copy(
        src_ref=p_refs[i].at[kb], dst_ref=land_refs[i].at[j], send_sem=send_sems.at[3 * i + j],
        recv_sem=recv_sems.at[3 * i + j], device_id=(bx, by, c), device_id_type=MESH)
        for i in range(len(p_refs)) for j, (bx, by, kb) in enumerate(_other_chips(x, y))]


def _scatter_start(name, partials):
    n_p = len(partials)

    def body(*refs):
        p_refs, land_refs = refs[:n_p], refs[n_p:2 * n_p]
        send_sems, recv_sems = refs[2 * n_p:2 * n_p + 2]
        for cp in _chip_scatter_copies(p_refs, land_refs, send_sems, recv_sems):
            cp.start()
        refs[-1][...] = jnp.zeros_like(refs[-1])

    lands = [pltpu.with_memory_space_constraint(lax.empty((N_CHIPS - 1,) + p.shape[1:], p.dtype), pltpu.HBM)
             for p in partials]
    out = pl.pallas_call(
        body, name=name,
        out_shape=(pltpu.SemaphoreType.DMA((3 * n_p,)), pltpu.SemaphoreType.DMA((3 * n_p,)),
                   *[pltpu.HBM(p.shape, p.dtype) for p in partials], *[pltpu.HBM(l.shape, l.dtype) for l in lands],
                   SDS((8, V7X_LANES), F32)),
        in_specs=[HBM_SPEC] * (2 * n_p), out_specs=(SEM_SPEC, SEM_SPEC, *[HBM_SPEC] * (2 * n_p), VMEM_WHOLE),
        input_output_aliases={i: 2 + i for i in range(2 * n_p)},
        compiler_params=pltpu.CompilerParams(has_side_effects=DATAFLOW),
    )(*[pltpu.with_memory_space_constraint(p, pltpu.HBM) for p in partials], *lands)
    return out[0], out[1], list(out[2:2 + n_p]), list(out[2 + n_p:2 + 2 * n_p]), out[-1]


def _scatter_wait(name, send_sems, recv_sems, partials, lands, after):
    n_p = len(partials)

    def body(*refs):
        p_refs, land_refs = refs[:n_p], refs[n_p:2 * n_p]
        send_sems, recv_sems = refs[2 * n_p:2 * n_p + 2]
        for cp in _chip_scatter_copies(p_refs, land_refs, send_sems, recv_sems):
            cp.wait_send()
            cp.wait_recv()

    out = pl.pallas_call(
        body, name=name,
        out_shape=(*[pltpu.HBM(p.shape, p.dtype) for p in partials], *[pltpu.HBM(l.shape, l.dtype) for l in lands]),
        in_specs=[*[HBM_SPEC] * (2 * n_p), SEM_SPEC, SEM_SPEC, ANY], out_specs=tuple([HBM_SPEC] * (2 * n_p)),
        input_output_aliases={i: i for i in range(2 * n_p)},
        compiler_params=pltpu.CompilerParams(has_side_effects=DATAFLOW),
    )(*partials, *lands, send_sems, recv_sems, after)
    return list(out[:n_p]), list(out[n_p:])


def _sum_chip_partials(name, grads, from_sibling, received, shard, layer, place):
    n, half, w = from_sibling.shape
    tr = _pack_row_tile(half)
    steps = half // tr

    def body(place_ref, g_ref, s_ref, r_ref, shard_ref, o_ref):
        own = g_ref[0] + s_ref[0]
        o_ref[...] = ((own + r_ref[0].astype(F32)) + r_ref[1].astype(F32)) + r_ref[2].astype(F32)

    return pl.pallas_call(
        body, name=name,
        grid_spec=pltpu.PrefetchScalarGridSpec(
            num_scalar_prefetch=1, grid=(steps,),
            in_specs=[pl.BlockSpec((1, tr, w), lambda i, place_ref: (place_ref[0], place_ref[1] * steps + i, 0)),
                      pl.BlockSpec((1, tr, w), lambda i, place_ref: (place_ref[0], i, 0)),
                      pl.BlockSpec((n - 1, tr, w), lambda i, place_ref: (0, i, 0)),
                      pl.BlockSpec(memory_space=pl.ANY)],
            out_specs=pl.BlockSpec((tr, w), lambda i, place_ref: ((2 * layer + place_ref[1]) * steps + i, 0))),
        out_shape=SDS(shard.shape, F32),
        input_output_aliases={4: 0},
        compiler_params=_params("parallel"),
    )(place, grads, from_sibling, received, shard)


def _join_halves(shards, layers):
    n_s = len(shards)
    n_sem = sum(layers)

    def body(*refs):
        in_refs = refs[:n_s]
        send_sems, recv_sems = refs[2 * n_s:]
        x, y, c = _my_place()
        copies, sem = [], 0
        for ref, n_layers in zip(in_refs, layers):
            half = ref.shape[0] // (2 * n_layers)
            for layer in range(n_layers):
                mine = pl.ds(pl.multiple_of(layer * 2 * half + c * half, 8), half)
                theirs = pl.ds(pl.multiple_of(layer * 2 * half + (1 - c) * half, 8), half)
                send = pltpu.make_async_remote_copy(
                    src_ref=ref.at[mine], dst_ref=ref.at[mine], send_sem=send_sems.at[sem], recv_sem=recv_sems.at[sem],
                    device_id=(x, y, 1 - c), device_id_type=MESH)
                send.start()
                arrival = pltpu.make_async_remote_copy(
                    src_ref=ref.at[theirs], dst_ref=ref.at[theirs], send_sem=send_sems.at[sem],
                    recv_sem=recv_sems.at[sem], device_id=(x, y, 1 - c), device_id_type=MESH)
                copies.append((send, arrival))
                sem += 1
        for send, arrival in copies:
            send.wait_send()
            arrival.wait_recv()

    return pl.pallas_call(
        body, name="join_halves", out_shape=[SDS(s.shape, s.dtype) for s in shards],
        in_specs=[ANY] * n_s, out_specs=[ANY] * n_s,
        input_output_aliases={i: i for i in range(n_s)},
        scratch_shapes=[pltpu.SemaphoreType.DMA((n_sem,)), pltpu.SemaphoreType.DMA((n_sem,))],
    )(*shards)


def _all_to_all_copies(v_ref, land_ref, send_sems, recv_sems):
    x, y, c = _my_place()
    me = 4 * x + 2 * y + c
    pairs = []
    for k in range(1, N_DEV):
        px, py, pc = (1 - x if k & 4 else x), (1 - y if k & 2 else y), (1 - c if k & 1 else c)
        sems = dict(send_sem=send_sems.at[k - 1], recv_sem=recv_sems.at[k - 1], device_id=(px, py, pc),
                    device_id_type=MESH)
        send = pltpu.make_async_remote_copy(src_ref=v_ref, dst_ref=land_ref.at[me], **sems)
        arrival = pltpu.make_async_remote_copy(src_ref=v_ref, dst_ref=land_ref.at[4 * px + 2 * py + pc], **sems)
        pairs.append((send, arrival))
    return pairs


def _small_reduce_start(name, v):
    def body(v_ref, land_ref, send_sems, recv_sems, v_out, land_out):
        for send, _ in _all_to_all_copies(v_ref, land_ref, send_sems, recv_sems):
            send.start()

    land = pltpu.with_memory_space_constraint(jnp.zeros((N_DEV,) + v.shape, v.dtype), pltpu.HBM)
    return pl.pallas_call(
        body, name=name,
        out_shape=(pltpu.SemaphoreType.DMA((N_DEV - 1,)), pltpu.SemaphoreType.DMA((N_DEV - 1,)),
                   pltpu.HBM(v.shape, v.dtype), pltpu.HBM(land.shape, land.dtype)),
        in_specs=[HBM_SPEC, HBM_SPEC], out_specs=(SEM_SPEC, SEM_SPEC, HBM_SPEC, HBM_SPEC),
        input_output_aliases={0: 2, 1: 3},
        compiler_params=pltpu.CompilerParams(has_side_effects=DATAFLOW),
    )(pltpu.with_memory_space_constraint(v, pltpu.HBM), land)


def _small_reduce_wait(name, send_sems, recv_sems, v, land, after):
    def body(v_ref, land_ref, send_sems, recv_sems, after_ref, v_out, land_out):
        for send, arrival in _all_to_all_copies(v_ref, land_ref, send_sems, recv_sems):
            send.wait_send()
            arrival.wait_recv()

    return pl.pallas_call(
        body, name=name, out_shape=(pltpu.HBM(v.shape, v.dtype), pltpu.HBM(land.shape, land.dtype)),
        in_specs=[HBM_SPEC, HBM_SPEC, SEM_SPEC, SEM_SPEC, ANY], out_specs=(HBM_SPEC, HBM_SPEC),
        input_output_aliases={0: 0, 1: 1},
        compiler_params=pltpu.CompilerParams(has_side_effects=DATAFLOW),
    )(v, land, send_sems, recv_sems, after)


def _sum_device_slots(name, v, land, me):
    r, w = v.shape

    def body(me_ref, v_ref, land_ref, o_ref):
        mine = v_ref[...]
        acc = jnp.where(me_ref[0] == 0, mine, land_ref[0])
        for d in range(1, N_DEV):
            acc = acc + jnp.where(me_ref[0] == d, mine, land_ref[d])
        o_ref[...] = acc

    return pl.pallas_call(
        body, name=name,
        grid_spec=pltpu.PrefetchScalarGridSpec(
            num_scalar_prefetch=1, grid=(1,),
            in_specs=[pl.BlockSpec((r, w), lambda i, me_ref: (0, 0)),
                      pl.BlockSpec((N_DEV, r, w), lambda i, me_ref: (0, 0, 0))],
            out_specs=pl.BlockSpec((r, w), lambda i, me_ref: (0, 0))),
        out_shape=SDS((r, w), F32),
        compiler_params=_params("arbitrary"),
    )(me, v, land)


def _adamw(name, w, g, m, v):
    rows, width = w.shape
    tr = _pack_row_tile(rows)

    def body(w_ref, g_ref, m_ref, v_ref, g_out_ref, d_ref, nm_ref, nv_ref):
        gg = g_ref[...]
        g_out_ref[...] = gg
        m_new = ADAM_B1 * m_ref[...] + (1.0 - ADAM_B1) * gg
        v_new = ADAM_B2 * v_ref[...] + (1.0 - ADAM_B2) * (gg * gg)
        m_hat = m_new / (1.0 - ADAM_B1 ** ADAM_STEP)
        v_hat = v_new / (1.0 - ADAM_B2 ** ADAM_STEP)
        d_ref[...] = -ADAM_LR * (m_hat / (jnp.sqrt(v_hat) + ADAM_EPS) + ADAM_WD * w_ref[...])
        nm_ref[...] = m_new
        nv_ref[...] = v_new

    spec = _rows(tr, width)
    return pl.pallas_call(
        body, name=name, grid=(rows // tr,),
        in_specs=[spec] * 4, out_specs=[spec] * 4,
        out_shape=[SDS((rows, width), F32)] * 4,
        compiler_params=_params("parallel"),
    )(w, g, m, v)


WEIGHT_NAMES = ['norm_mix', 'norm_ffn', 'attn_w_qkv', 'attn_b_qkv', 'attn_sinks', 'attn_w_o', 'attn_b_o',
                'conv_w_pw1', 'conv_b_pw1', 'conv_w_dw', 'conv_b_dw', 'conv_ln_g', 'conv_ln_b', 'conv_w_pw2',
                'conv_b_pw2', 'ffn_w1', 'ffn_w3', 'ffn_w2', 'norm_final']
BIG = ['attn_w_qkv', 'attn_w_o', 'conv_w_pw1', 'conv_w_pw2', 'ffn_w1', 'ffn_w3', 'ffn_w2']
COLUMN_SPLIT = ('attn_w_qkv', 'conv_w_pw1', 'ffn_w1', 'ffn_w3')
STORED_TRANSPOSED = ('attn_w_qkv', 'ffn_w1', 'ffn_w3')
SMALL_SPLIT = ['conv_b_pw1', 'conv_w_dw', 'conv_b_dw', 'conv_ln_g', 'conv_ln_b', 'conv_b_pw2']
SMALL_WHOLE = ['norm_mix', 'norm_ffn', 'attn_b_qkv', 'attn_sinks', 'attn_b_o', 'norm_final']


def _keeps_rows(shape):
    return len(shape) == 2 and shape[0] > 1 and shape[1] == PACK_W


def _pack_rows(arrays, dtype, row_multiple):
    blocks = [jnp.pad(a.astype(dtype), ((0, -a.shape[0] % V7X_SUBLANES), (0, 0)))
              for a in arrays if _keeps_rows(a.shape)]
    flat = jnp.concatenate([a.astype(dtype).reshape(-1) for a in arrays if not _keeps_rows(a.shape)])
    multiple = max(row_multiple, V7X_SUBLANES)
    rows = -(-(-(-flat.shape[0] // PACK_W)) // multiple) * multiple
    blocks.append(jnp.pad(flat, (0, rows * PACK_W - flat.shape[0])).reshape(rows, PACK_W))
    return jnp.concatenate(blocks, axis=0) if len(blocks) > 1 else blocks[0]


def _unpack_rows(pack, shapes):
    out, row = {}, 0
    for i, shape in enumerate(shapes):
        if _keeps_rows(shape):
            out[i] = pack[row:row + shape[0]]
            row += -(-shape[0] // V7X_SUBLANES) * V7X_SUBLANES
    flat, at = pack[row:].reshape(-1), 0
    for i, shape in enumerate(shapes):
        if not _keeps_rows(shape):
            size = 1
            for s in shape:
                size *= s
            out[i] = flat[at:at + size].reshape(shape)
            at += size
    return [out[i] for i in range(len(shapes))]


def _join_chip_axis(name, parts):
    axis = parts.ndim - 1 if name in COLUMN_SPLIT or name in SMALL_SPLIT else parts.ndim - 2
    moved = jnp.moveaxis(parts, 0, axis - 1)
    shape = list(moved.shape)
    shape[axis - 1:axis + 1] = [shape[axis - 1] * shape[axis]]
    return moved.reshape(shape)


def _split_chip_axis(name, whole, shard_shape):
    axis = len(shard_shape) - 1 if name in COLUMN_SPLIT or name in SMALL_SPLIT else len(shard_shape) - 2
    shape = list(whole.shape)
    shape[axis:axis + 1] = [N_CHIPS, shard_shape[axis]]
    return jnp.moveaxis(whole.reshape(shape), axis, 0)


def kernel(x, norm_mix, norm_ffn, attn_w_qkv, attn_b_qkv, attn_sinks, attn_w_o, attn_b_o, conv_w_pw1, conv_b_pw1, conv_w_dw, conv_b_dw, conv_ln_g, conv_ln_b, conv_w_pw2, conv_b_pw2, ffn_w1, ffn_w3, ffn_w2, norm_final, loss_target, m_norm_mix, m_norm_ffn, m_attn_w_qkv, m_attn_b_qkv, m_attn_sinks, m_attn_w_o, m_attn_b_o, m_conv_w_pw1, m_conv_b_pw1, m_conv_w_dw, m_conv_b_dw, m_conv_ln_g, m_conv_ln_b, m_conv_w_pw2, m_conv_b_pw2, m_ffn_w1, m_ffn_w3, m_ffn_w2, m_norm_final, v_norm_mix, v_norm_ffn, v_attn_w_qkv, v_attn_b_qkv, v_attn_sinks, v_attn_w_o, v_attn_b_o, v_conv_w_pw1, v_conv_b_pw1, v_conv_w_dw, v_conv_b_dw, v_conv_ln_g, v_conv_ln_b, v_conv_w_pw2, v_conv_b_pw2, v_ffn_w1, v_ffn_w3, v_ffn_w2, v_norm_final):
    w = dict(zip(WEIGHT_NAMES, (norm_mix, norm_ffn, attn_w_qkv, attn_b_qkv, attn_sinks, attn_w_o, attn_b_o,
                                conv_w_pw1, conv_b_pw1, conv_w_dw, conv_b_dw, conv_ln_g, conv_ln_b, conv_w_pw2,
                                conv_b_pw2, ffn_w1, ffn_w3, ffn_w2, norm_final)))
    m = dict(zip(WEIGHT_NAMES, (m_norm_mix, m_norm_ffn, m_attn_w_qkv, m_attn_b_qkv, m_attn_sinks, m_attn_w_o,
                                m_attn_b_o, m_conv_w_pw1, m_conv_b_pw1, m_conv_w_dw, m_conv_b_dw, m_conv_ln_g,
                                m_conv_ln_b, m_conv_w_pw2, m_conv_b_pw2, m_ffn_w1, m_ffn_w3, m_ffn_w2, m_norm_final)))
    v = dict(zip(WEIGHT_NAMES, (v_norm_mix, v_norm_ffn, v_attn_w_qkv, v_attn_b_qkv, v_attn_sinks, v_attn_w_o,
                                v_attn_b_o, v_conv_w_pw1, v_conv_b_pw1, v_conv_w_dw, v_conv_b_dw, v_conv_ln_g,
                                v_conv_ln_b, v_conv_w_pw2, v_conv_b_pw2, v_ffn_w1, v_ffn_w3, v_ffn_w2, v_norm_final)))
    T, D = x.shape[1], x.shape[2]
    c_idx = lax.axis_index("c").astype(jnp.int32).reshape(1)
    chip = (2 * lax.axis_index("x") + lax.axis_index("y")).astype(jnp.int32)

    def as_rows(n, a):
        a = jnp.swapaxes(a, -1, -2) if n in STORED_TRANSPOSED else a
        return a.reshape(-1, a.shape[-1])

    def from_rows(n, rows):
        shape = w[n].shape[:-2] + w[n].shape[:-3:-1] if n in STORED_TRANSPOSED else w[n].shape
        a = rows.reshape(shape)
        return jnp.swapaxes(a, -1, -2) if n in STORED_TRANSPOSED else a

    def cast(n, after=None):
        return _cast_into_slot(f"cast_{n}", lax.empty((N_CHIPS,) + as_rows(n, w[n]).shape, BF16), as_rows(n, w[n]),
                               chip.reshape(1), after)

    first, later = BIG[:2], BIG[2:]
    slabs = {n: cast(n) for n in first}
    attn_send, attn_recv, attn_travelling, attn_started = _gather_start(
        "gather_attn_start", [slabs[n] for n in first], slabs[first[0]])
    slabs.update({n: cast(n, attn_started) for n in later})
    layers = ffn_w1.shape[0]
    small_shapes = [w[n].shape for n in SMALL_SPLIT]
    small_all = _gather_small("gather_small", _pack_rows([w[n] for n in SMALL_SPLIT], F32, 8))
    attn_landed = _gather_wait("gather_attn_wait", attn_send, attn_recv, attn_travelling, small_all)
    qkv_parts, w_o_parts = _swap_fetched_with_sibling("gather_attn_swap", attn_landed)
    send_sems, recv_sems, travelling, gather_started = _gather_start("gather_start", [slabs[n] for n in later],
                                                                     qkv_parts)
    per_chip = [_unpack_rows(small_all[2 * j], small_shapes) for j in range(N_CHIPS)]
    full = {}
    for i, n in enumerate(SMALL_SPLIT):
        full[n] = _join_chip_axis(n, jnp.stack([per_chip[j][i] for j in range(N_CHIPS)]))

    def other_weights(after):
        landed = dict(zip(later, _gather_wait("gather_wait", send_sems, recv_sems, travelling, after)))
        now, then = ["ffn_w1", "ffn_w3"], ["conv_w_pw1", "conv_w_pw2", "ffn_w2"]
        ready = dict(zip(now, _swap_fetched_with_sibling("gather_swap", [landed[n] for n in now])))
        swap_send, swap_recv, swapping_rest, swap_started = _gather_start(
            "gather_swap_start", [landed[n] for n in then], ready[now[0]], stage="d2d")

        def rest_of_weights(after_next):
            rest = dict(zip(then, _gather_wait("gather_swap_wait", swap_send, swap_recv, swapping_rest, after_next,
                                               stage="d2d")))
            return {"conv_w_pw1": rest["conv_w_pw1"], "conv_w_pw2": rest["conv_w_pw2"].reshape(-1, D),
                    "ffn_w2": rest["ffn_w2"].reshape(N_CHIPS, layers, -1, D)}

        return {"ffn_w1": ready["ffn_w1"].reshape(N_CHIPS, layers, -1, D),
                "ffn_w3": ready["ffn_w3"].reshape(N_CHIPS, layers, -1, D),
                "swap_started": swap_started, "rest_of_weights": rest_of_weights}

    p = {
        "norm_mix": norm_mix, "norm_ffn": norm_ffn, "norm_final": norm_final.reshape(1, D),
        "attn_w_qkv": qkv_parts.reshape(-1, D), "attn_b_qkv": attn_b_qkv,
        "attn_sinks": attn_sinks, "attn_w_o": w_o_parts.reshape(-1, D), "attn_b_o": attn_b_o,
        "conv_b_pw1": full["conv_b_pw1"], "conv_w_dw": full["conv_w_dw"][0],
        "conv_b_dw": full["conv_b_dw"], "conv_ln_g": full["conv_ln_g"], "conv_ln_b": full["conv_ln_b"],
        "conv_b_pw2": full["conv_b_pw2"], "gather_started": gather_started, "other_weights": other_weights,
    }
    swapping, in_flight = {}, []

    def reduce_begin(tag, grads):
        keys = list(grads)
        *handles, begun = _sibling_swap_start(f"sibling_swap_start_{tag}", [grads[k] for k in keys])
        swapping[tag] = (keys, handles)
        return begun

    def reduce_send(tag, after):
        keys, (swap_send, swap_recv, grads, lands) = swapping[tag]
        grads, from_sibling = _sibling_swap_wait(f"sibling_swap_wait_{tag}", swap_send, swap_recv, grads, lands, after)
        partials = [_add_sibling_half(f"add_sibling_half_{tag}{i}", gr, fs, c_idx)
                    for i, (gr, fs) in enumerate(zip(grads, from_sibling))]
        *handles, sent = _scatter_start(f"scatter_start_{tag}", partials)
        in_flight.append((tag, keys, handles, grads, from_sibling))
        return sent

    loss_part, dx, g = _local_step(x[0], loss_target[0], p, reduce_begin, reduce_send)
    for n in SMALL_WHOLE + SMALL_SPLIT:
        g[n] = g[n].reshape((-1,) + g[n].shape[-2:]) if w[n].ndim == 3 else g[n].reshape(w[n].shape[:-1] + (-1,))

    small_pack = _pack_rows([loss_part] + [g[n] for n in SMALL_WHOLE] + [g[n] for n in SMALL_SPLIT], F32, 8)
    small_send, small_recv, small_pack, small_land = _small_reduce_start("small_reduce_start", small_pack)

    place = jnp.stack([chip, c_idx[0]])
    shard_grad = {n: lax.empty(as_rows(n, w[n]).shape, F32) for n in BIG}
    for tag, keys, (send_sems, recv_sems, partials, lands), grads, from_sibling in in_flight:
        _, received = _scatter_wait(f"scatter_wait_{tag}", send_sems, recv_sems, partials, lands, small_pack)
        for i, (n, layer) in enumerate(keys):
            shard_grad[n] = _sum_chip_partials(f"sum_chip_partials_{tag}{i}", grads[i], from_sibling[i], received[i],
                                               shard_grad[n], layer, place)
    g_big = dict(zip(BIG, _join_halves([shard_grad[n] for n in BIG], [w[n].shape[0] for n in BIG])))
    big_out = {}
    for n in BIG:
        step = _adamw(f"adamw_{n}", as_rows(n, w[n]), g_big[n], as_rows(n, m[n]), as_rows(n, v[n]))
        big_out[n] = [from_rows(n, a) for a in step]

    small_whole_shapes = [w[n].shape for n in SMALL_WHOLE]
    small_full_shapes = [g[n].shape for n in SMALL_SPLIT]
    small_pack, small_land = _small_reduce_wait("small_reduce_wait", small_send, small_recv, small_pack, small_land,
                                                big_out[BIG[-1]][1])
    reduced = _sum_device_slots("small_reduce_sum", small_pack, small_land, (2 * chip + c_idx[0]).reshape(1))
    pieces = _unpack_rows(reduced, [(1,)] + small_whole_shapes + small_full_shapes)
    loss = pieces[0].reshape(())
    g_small = dict(zip(SMALL_WHOLE, pieces[1:1 + len(SMALL_WHOLE)]))
    for n, whole in zip(SMALL_SPLIT, pieces[1 + len(SMALL_WHOLE):]):
        parts = _split_chip_axis(n, whole, w[n].shape)
        g_small[n] = lax.dynamic_index_in_dim(parts, chip, axis=0, keepdims=False)
    small = SMALL_WHOLE + SMALL_SPLIT
    _, d_small, m_small, v_small = _adamw(
        "adamw_small", _pack_rows([w[n] for n in small], F32, 8), _pack_rows([g_small[n] for n in small], F32, 8),
        _pack_rows([m[n] for n in small], F32, 8), _pack_rows([v[n] for n in small], F32, 8))

    outs = {}
    for slot, (tag, small_pack) in enumerate((("g", None), ("d", d_small), ("m", m_small), ("v", v_small))):
        vals = {n: big_out[n][slot] for n in BIG}
        if small_pack is None:
            vals.update(g_small)
        else:
            vals.update(zip(small, _unpack_rows(small_pack, [w[n].shape for n in small])))
        outs[tag] = vals
    return (loss, dx.reshape(1, T, D), *[outs["g"][n] for n in WEIGHT_NAMES], *[outs["d"][n] for n in WEIGHT_NAMES],
            *[outs["m"][n] for n in WEIGHT_NAMES], *[outs["v"][n] for n in WEIGHT_NAMES])
```

```python
import functools

import jax
import jax.numpy as jnp
from jax import lax
from jax.experimental import pallas as pl
from jax.experimental.pallas import tpu as pltpu

F32 = jnp.float32
BF16 = jnp.bfloat16
SDS = jax.ShapeDtypeStruct
MESH = pl.DeviceIdType.MESH

HEAD_DIM = 64
N_Q_HEADS = 16
N_KV_HEADS = 2
Q_PER_KV = N_Q_HEADS // N_KV_HEADS
ATTN_BLOCK = 128
ROPE_THETA = 10000.0
CONV_WIDTH = 31
CONV_HALO = 32
CONV_FIRST_TAP = CONV_HALO - CONV_WIDTH + 1
CONV_ROW_CHUNK = 64
CONV_LANE_CHUNK = 256
CONV_GRAD_UNROLL = 8
RMS_EPS = 1e-5
LN_EPS = 1e-5
ADAM_LR = 0.001
ADAM_B1 = 0.9
ADAM_B2 = 0.999
ADAM_EPS = 1e-08
ADAM_WD = 0.01
ADAM_STEP = 10

V7X_LANES = 128
V7X_SUBLANES = 8
V7X_VMEM_LIMIT_BYTES = 56 * 1024 * 1024

N_CHIPS = 4
N_DEV = 8
PACK_W = 1024

MASK_VALUE = -1e30


def _params(*semantics):
    return pltpu.CompilerParams(dimension_semantics=semantics, vmem_limit_bytes=V7X_VMEM_LIMIT_BYTES)


def _rows(tm, width):
    return pl.BlockSpec((tm, width), lambda i: (i, 0))


def _whole(shape):
    return pl.BlockSpec(shape, lambda *_: (0,) * len(shape))


def _rms_rstd(h):
    return lax.rsqrt(jnp.mean(h * h, axis=-1, keepdims=True) + RMS_EPS)


def _silu_and_grad(z):
    sg = jax.nn.sigmoid(z)
    return z * sg, sg * (1.0 + z * (1.0 - sg))


def _swap_rope_halves(t):
    w = t.shape[1]
    half = HEAD_DIM // 2
    lane = lax.broadcasted_iota(jnp.int32, t.shape, 1)
    upper = pltpu.roll(t, w - half, 1)
    lower = pltpu.roll(t, half, 1)
    return jnp.where(lane % HEAD_DIM < half, upper, lower)


def _rope(t, cos_ref, sin_ref):
    reps = t.shape[1] // V7X_LANES
    c = jnp.tile(cos_ref[...], (1, reps))
    s = jnp.tile(sin_ref[...], (1, reps))
    return t * c + _swap_rope_halves(t) * s


def _rope_transposed(dt, cos_ref, sin_ref):
    reps = dt.shape[1] // V7X_LANES
    c = jnp.tile(cos_ref[...], (1, reps))
    s = jnp.tile(sin_ref[...], (1, reps))
    return dt * c + _swap_rope_halves(dt * s)


def _rope_tables(seq_len):
    pos = jnp.arange(seq_len, dtype=F32)
    inv_freq = ROPE_THETA ** (-jnp.arange(0, HEAD_DIM, 2, dtype=F32) / HEAD_DIM)
    ang = pos[:, None] * jnp.tile(inv_freq, 2 * V7X_LANES // HEAD_DIM)[None, :]
    upper_half = jnp.arange(V7X_LANES) % HEAD_DIM >= HEAD_DIM // 2
    return jnp.cos(ang), jnp.where(upper_half[None, :], jnp.sin(ang), -jnp.sin(ang))


def _qkv_proj(h, g, w, b, cos, sin, after):
    T, D = h.shape
    N = w.shape[0]
    tm = min(512, T)
    rope_w = N - N_KV_HEADS * HEAD_DIM

    def body(h_ref, g_ref, w_ref, b_ref, cos_ref, sin_ref, _, y_ref, o_ref):
        hh = h_ref[...]
        y = (hh * _rms_rstd(hh) * g_ref[...]).astype(BF16)
        y_ref[...] = y
        acc = _dot_nt(y, w_ref[...]) + b_ref[...]
        o_ref[:, :rope_w] = _rope(acc[:, :rope_w], cos_ref, sin_ref).astype(BF16)
        o_ref[:, rope_w:] = acc[:, rope_w:].astype(BF16)

    return pl.pallas_call(
        body, name="qkv_proj", grid=(T // tm,),
        in_specs=[_rows(tm, D), _whole((1, D)), _whole((N, D)), _whole((1, N)),
                  _rows(tm, V7X_LANES), _rows(tm, V7X_LANES), pl.BlockSpec(memory_space=pl.ANY)],
        out_specs=[_rows(tm, D), _rows(tm, N)],
        out_shape=[SDS((T, D), BF16), SDS((T, N), BF16)],
        compiler_params=_params("parallel"),
    )(h, g, w, b, cos, sin, after)


def _pw1_proj(h, g, w, b):
    T, D = h.shape
    n = w.shape[2]
    N = N_CHIPS * n
    tm = min(512, T)

    def body(h_ref, g_ref, w_ref, b_ref, y_ref, o_ref):
        hh = h_ref[...]
        y = (hh * _rms_rstd(hh) * g_ref[...]).astype(BF16)
        y_ref[...] = y
        for j in range(N_CHIPS):
            cols = slice(j * n, (j + 1) * n)
            o_ref[:, cols] = jnp.dot(y, w_ref[j], preferred_element_type=F32) + b_ref[:, cols]

    return pl.pallas_call(
        body, name="pw1_proj", grid=(T // tm,),
        in_specs=[_rows(tm, D), _whole((1, D)), _whole((N_CHIPS, D, n)), _whole((1, N))],
        out_specs=[_rows(tm, D), _rows(tm, N)],
        out_shape=[SDS((T, D), BF16), SDS((T, N), F32)],
        compiler_params=_params("parallel"),
    )(h, g, w, b)


PAIRS_PER_KV = Q_PER_KV // 2


def _upper_lanes(shape):
    return lax.broadcasted_iota(jnp.int32, shape, 1) >= HEAD_DIM


def _swap_lane_halves(t):
    return pltpu.roll(t.astype(F32), HEAD_DIM, 1).astype(t.dtype)


def _kv_operands(g, t):
    swapped = _swap_lane_halves(t)
    in_lower, in_upper = (t, swapped) if g == 0 else (swapped, t)
    upper = _upper_lanes(t.shape)
    zero = jnp.zeros_like(t)
    return jnp.where(upper, zero, in_lower), jnp.where(upper, in_upper, zero)


def _group_heads(g):
    pairs = range(g * PAIRS_PER_KV, (g + 1) * PAIRS_PER_KV)
    return [2 * hp for hp in pairs] + [2 * hp + 1 for hp in pairs]


def _all_heads():
    return [h for g in range(N_KV_HEADS) for h in _group_heads(g)]


def _pair_rows(ref, g):
    pairs = range(g * PAIRS_PER_KV, (g + 1) * PAIRS_PER_KV)
    return jnp.concatenate([ref[:, hp * 2 * HEAD_DIM:(hp + 1) * 2 * HEAD_DIM] for hp in pairs], axis=0)


def _from_previous_block(rows):
    row = lax.broadcasted_iota(jnp.int32, (ATTN_BLOCK, ATTN_BLOCK), 0)
    col = lax.broadcasted_iota(jnp.int32, (ATTN_BLOCK, ATTN_BLOCK), 1)
    return jnp.concatenate([col > row] * (rows // ATTN_BLOCK), axis=0)


def _split_folded(t, prev_part):
    tb = t.astype(BF16)
    zero = jnp.zeros_like(tb)
    return jnp.where(prev_part, tb, zero), jnp.where(prev_part, zero, tb)


def _attn_specs(T):
    nb = T // ATTN_BLOCK
    kcol = N_Q_HEADS * HEAD_DIM // V7X_LANES
    cur = lambda n: jnp.minimum(n, nb - 1)
    prev = lambda n: jnp.maximum(jnp.minimum(n, nb - 1) - 1, 0)
    q_spec = pl.BlockSpec((ATTN_BLOCK, N_Q_HEADS * HEAD_DIM), lambda n: (cur(n), 0))
    kc_spec = pl.BlockSpec((ATTN_BLOCK, V7X_LANES), lambda n: (cur(n), kcol))
    kp_spec = pl.BlockSpec((ATTN_BLOCK, V7X_LANES), lambda n: (prev(n), kcol))
    vc_spec = pl.BlockSpec((ATTN_BLOCK, V7X_LANES), lambda n: (cur(n), kcol + 1))
    vp_spec = pl.BlockSpec((ATTN_BLOCK, V7X_LANES), lambda n: (prev(n), kcol + 1))
    return q_spec, kc_spec, kp_spec, vc_spec, vp_spec


def _attn_fwd(qkv, sinks):
    T = qkv.shape[0]
    nb = T // ATTN_BLOCK
    qw = N_Q_HEADS * HEAD_DIM
    all_rows = N_Q_HEADS * ATTN_BLOCK

    per_step = 4
    kcol = qw // V7X_LANES

    def body(q_ref, *refs):
        k_refs, v_refs = refs[:per_step + 1], refs[per_step + 1:2 * per_step + 2]
        sink_ref, o_ref, probs_ref, psink_ref = refs[2 * per_step + 2:]
        m = pl.program_id(0)
        half = PAIRS_PER_KV * ATTN_BLOCK
        groups = range(N_KV_HEADS)
        k_ops = [[_kv_operands(g, r[...]) for g in groups] for r in k_refs]
        v_ops = [[_kv_operands(g, r[...]) for g in groups] for r in v_refs]

        def scores(q, k):
            return lax.dot_general(q, k, (((1,), (1,)), ((), ())), preferred_element_type=F32)

        s_prev, s_cur = [], []
        for b in range(per_step):
            rows = slice(b * ATTN_BLOCK, (b + 1) * ATTN_BLOCK)
            for g in groups:
                pairs = range(g * PAIRS_PER_KV, (g + 1) * PAIRS_PER_KV)
                q = jnp.concatenate([q_ref[rows, hp * 2 * HEAD_DIM:(hp + 1) * 2 * HEAD_DIM] for hp in pairs], axis=0)
                for i in range(2):
                    from_prev = scores(q, k_ops[b][g][i])
                    if b == 0:
                        from_prev = jnp.where(m > 0, from_prev, MASK_VALUE * (HEAD_DIM ** 0.5))
                    s_prev.append(from_prev)
                    s_cur.append(scores(q, k_ops[b + 1][g][i]))
        prev_part = _from_previous_block(per_step * all_rows)
        s = jnp.where(prev_part, jnp.concatenate(s_prev, axis=0), jnp.concatenate(s_cur, axis=0)) * (HEAD_DIM ** -0.5)
        sink = jnp.concatenate([jnp.broadcast_to(sink_ref[0:1, h:h + 1], (ATTN_BLOCK, 1))
                                for h in _all_heads() * per_step], axis=0)
        top = jnp.maximum(jnp.max(s, axis=1, keepdims=True), sink)
        p = jnp.exp(s - top)
        e_sink = jnp.exp(sink - top)
        inv = 1.0 / (jnp.sum(p, axis=1, keepdims=True) + e_sink)
        probs, p_sink = p * inv, e_sink * inv
        p_prev, p_cur = _split_folded(probs, prev_part)
        lane = lax.broadcasted_iota(jnp.int32, (ATTN_BLOCK, V7X_LANES), 1)
        for b in range(per_step):
            rows = slice(b * ATTN_BLOCK, (b + 1) * ATTN_BLOCK)
            base = b * all_rows
            probs_ref[b] = probs[base:base + all_rows].astype(BF16)
            sink_tile = jnp.zeros((ATTN_BLOCK, V7X_LANES), F32)
            for i, h in enumerate(_all_heads()):
                sink_tile = jnp.where(lane == h, p_sink[base + i * ATTN_BLOCK:base + (i + 1) * ATTN_BLOCK], sink_tile)
            psink_ref[rows, :] = sink_tile
            for g in groups:
                even = slice(base + 2 * g * half, base + (2 * g + 1) * half)
                odd = slice(base + (2 * g + 1) * half, base + (2 * g + 2) * half)
                o = (jnp.dot(p_prev[even], v_ops[b][g][0], preferred_element_type=F32)
                     + jnp.dot(p_cur[even], v_ops[b + 1][g][0], preferred_element_type=F32)
                     + jnp.dot(p_prev[odd], v_ops[b][g][1], preferred_element_type=F32)
                     + jnp.dot(p_cur[odd], v_ops[b + 1][g][1], preferred_element_type=F32))
                for i in range(PAIRS_PER_KV):
                    hp = g * PAIRS_PER_KV + i
                    o_ref[rows, hp * 2 * HEAD_DIM:(hp + 1) * 2 * HEAD_DIM] = (
                        o[i * ATTN_BLOCK:(i + 1) * ATTN_BLOCK].astype(BF16))

    def kv_specs(col):
        return [pl.BlockSpec((ATTN_BLOCK, V7X_LANES),
                             functools.partial(lambda m, b: (jnp.maximum(per_step * m + b, 0), col), b=b))
                for b in range(-1, per_step)]

    return pl.pallas_call(
        body, name="attn_fwd", grid=(nb // per_step,),
        in_specs=[_rows(per_step * ATTN_BLOCK, qw), *kv_specs(kcol), *kv_specs(kcol + 1), _whole((1, N_Q_HEADS))],
        out_specs=[_rows(per_step * ATTN_BLOCK, qw),
                   pl.BlockSpec((per_step, all_rows, ATTN_BLOCK), lambda m: (m, 0, 0)),
                   _rows(per_step * ATTN_BLOCK, V7X_LANES)],
        out_shape=[SDS((T, qw), BF16), SDS((nb, all_rows, ATTN_BLOCK), BF16), SDS((T, V7X_LANES), F32)],
        compiler_params=_params("parallel"),
    )(*[qkv] * (2 * per_step + 3), sinks)


def _mm_res(name, a, w, b, res, g):
    T, K = a.shape
    D = w.shape[1]
    tm = min(512, T)

    def body(a_ref, w_ref, b_ref, r_ref, g_ref, o_ref, f_ref):
        h = jnp.dot(a_ref[...], w_ref[...], preferred_element_type=F32) + b_ref[...] + r_ref[...]
        o_ref[...] = h
        f_ref[...] = (h * _rms_rstd(h) * g_ref[...]).astype(BF16)

    return pl.pallas_call(
        body, name=name, grid=(T // tm,),
        in_specs=[_rows(tm, K), _whole((K, D)), _whole((1, D)), _rows(tm, D), _whole((1, D))],
        out_specs=[_rows(tm, D), _rows(tm, D)],
        out_shape=[SDS((T, D), F32), SDS((T, D), BF16)],
        compiler_params=_params("parallel"),
    )(a, w, b, res, g)


def _ffn_down(name, s, w2, layer, res):
    _, T, n = s.shape
    D = w2.shape[3]
    tm = min(512, T)

    def body(s_ref, w_ref, r_ref, o_ref):
        acc = r_ref[...]
        for j in range(N_CHIPS):
            acc = acc + jnp.dot(s_ref[j], w_ref[j], preferred_element_type=F32)
        o_ref[...] = acc

    return pl.pallas_call(
        body, name=name, grid=(T // tm,),
        in_specs=[pl.BlockSpec((N_CHIPS, tm, n), lambda i: (0, i, 0)),
                  pl.BlockSpec((N_CHIPS, None, n, D), lambda i: (0, layer, 0, 0)), _rows(tm, D)],
        out_specs=_rows(tm, D),
        out_shape=SDS((T, D), F32),
        compiler_params=_params("parallel"),
    )(s, w2, res)


def _ffn_up(name, f, w1, w3, layer, after=None):
    T, D = f.shape
    n = w1.shape[2]
    tm = min(1024, T)

    def body(f_ref, w1_ref, w3_ref, *rest):
        act_ref, gg_ref, s_ref = rest[-3:]
        ff = f_ref[...]
        g1 = _dot_nt(ff, w1_ref[...])
        g3 = _dot_nt(ff, w3_ref[...])
        act, dact = _silu_and_grad(g1)
        act_ref[...] = act.astype(BF16)
        gg_ref[...] = (g3 * dact).astype(BF16)
        s_ref[...] = (act * g3).astype(BF16)

    slab = pl.BlockSpec((None, tm, n), lambda j, i: (j, i, 0))
    wslab = pl.BlockSpec((None, None, n, D), lambda j, i: (j, layer, 0, 0))
    hidden = SDS((N_CHIPS, T, n), BF16)
    return pl.pallas_call(
        body, name=name, grid=(N_CHIPS, T // tm),
        in_specs=[pl.BlockSpec((tm, D), lambda j, i: (i, 0)), wslab, wslab]
        + ([] if after is None else [pl.BlockSpec(memory_space=pl.ANY)]),
        out_specs=[slab, slab, slab],
        out_shape=[hidden, hidden, hidden],
        compiler_params=_params("parallel", "parallel"),
    )(f, w1, w3, *([] if after is None else [after]))


def _glu(a, d):
    return a[:, :d] * jax.nn.sigmoid(a[:, d:])


def _conv_tile(T):
    return min(256, T)


def _fill_shifted(sh_ref, tc):
    n = tc + CONV_HALO - V7X_SUBLANES
    for r in range(1, V7X_SUBLANES):
        sh_ref[r, 0:n, :] = sh_ref[0, pl.ds(r, n), :]


def _depthwise_taps(sh_ref, w_ref, offsets, bias_ref, out_ref, tc):
    D = out_ref.shape[1]

    def chunk(i, carry):
        t0 = pl.multiple_of(i * CONV_ROW_CHUNK, CONV_ROW_CHUNK)
        for cb in range(D // CONV_LANE_CHUNK):
            cs = slice(cb * CONV_LANE_CHUNK, (cb + 1) * CONV_LANE_CHUNK)
            acc = jnp.zeros((CONV_ROW_CHUNK, CONV_LANE_CHUNK), F32)
            for r in range(V7X_SUBLANES):
                taps = [(j, o // V7X_SUBLANES) for j, o in enumerate(offsets) if o % V7X_SUBLANES == r]
                if not taps:
                    continue
                span = CONV_ROW_CHUNK + V7X_SUBLANES * max(q for _, q in taps)
                rows = sh_ref[r, pl.ds(t0, span), cs]
                for j, q in taps:
                    acc = acc + rows[V7X_SUBLANES * q:V7X_SUBLANES * q + CONV_ROW_CHUNK] * w_ref[j:j + 1, cs]
            if bias_ref is not None:
                acc = acc + bias_ref[:, cs]
            out_ref[pl.ds(t0, CONV_ROW_CHUNK), cs] = acc
        return carry

    lax.fori_loop(0, tc // CONV_ROW_CHUNK, chunk, 0)


def _depthwise_tap_grads(dy_sh, x_sh, offsets, dw_ref, tc):
    D = dw_ref.shape[1]
    for cb in range(D // V7X_LANES):
        cs = slice(cb * V7X_LANES, (cb + 1) * V7X_LANES)

        def row_tiles(i, accs, cs=cs):
            for k in range(CONV_GRAD_UNROLL):
                t0 = pl.multiple_of(i * (CONV_GRAD_UNROLL * V7X_SUBLANES), V7X_SUBLANES) + k * V7X_SUBLANES
                d = dy_sh[0, pl.ds(t0, V7X_SUBLANES), cs]
                accs = tuple(
                    acc + d * x_sh[o % V7X_SUBLANES, pl.ds(t0 + o // V7X_SUBLANES * V7X_SUBLANES, V7X_SUBLANES), cs]
                    for acc, o in zip(accs, offsets))
            return accs

        zero = jnp.zeros((V7X_SUBLANES, V7X_LANES), F32)
        accs = lax.fori_loop(0, tc // (CONV_GRAD_UNROLL * V7X_SUBLANES), row_tiles, tuple(zero for _ in offsets))
        for j, acc in enumerate(accs):
            dw_ref[j:j + 1, cs] += jnp.sum(acc, axis=0, keepdims=True)


def _conv_fwd(a, w_dw, b_dw, ln_g, ln_b, w_pw2, b_pw2, res, g_next):
    T = a.shape[0]
    D = a.shape[1] // 2
    tc = _conv_tile(T)
    per = tc // CONV_HALO

    def body(a_ref, ah_ref, w_ref, bdw_ref, lg_ref, lb_ref, wp_ref, bp_ref, r_ref, g_ref,
             c_ref, act_ref, h_ref, f_ref, u_sh):
        i = pl.program_id(0)
        u_sh[0, 0:CONV_HALO, :] = jnp.where(i > 0, _glu(ah_ref[...], D), 0.0)
        u_sh[0, CONV_HALO:, :] = _glu(a_ref[...], D)
        _fill_shifted(u_sh, tc)
        _depthwise_taps(u_sh, w_ref, [CONV_FIRST_TAP + j for j in range(CONV_WIDTH)], bdw_ref, c_ref, tc)
        c = c_ref[...]
        xc = c - jnp.mean(c, axis=-1, keepdims=True)
        z = xc * lax.rsqrt(jnp.mean(xc * xc, axis=-1, keepdims=True) + LN_EPS)
        l = z * lg_ref[...] + lb_ref[...]
        act = (l * jax.nn.sigmoid(l)).astype(BF16)
        act_ref[...] = act
        h = jnp.dot(act, wp_ref[...], preferred_element_type=F32) + bp_ref[...] + r_ref[...]
        h_ref[...] = h
        f_ref[...] = (h * _rms_rstd(h) * g_ref[...]).astype(BF16)

    return pl.pallas_call(
        body, name="conv_fwd", grid=(T // tc,),
        in_specs=[_rows(tc, 2 * D),
                  pl.BlockSpec((CONV_HALO, 2 * D), lambda i: (jnp.maximum(i * per - 1, 0), 0)),
                  _whole((CONV_WIDTH, D)), _whole((1, D)), _whole((1, D)), _whole((1, D)),
                  _whole((D, D)), _whole((1, D)), _rows(tc, D), _whole((1, D))],
        out_specs=[_rows(tc, D), _rows(tc, D), _rows(tc, D), _rows(tc, D)],
        out_shape=[SDS((T, D), F32), SDS((T, D), BF16), SDS((T, D), F32), SDS((T, D), BF16)],
        scratch_shapes=[pltpu.VMEM((V7X_SUBLANES, tc + CONV_HALO, D), F32)],
        compiler_params=_params("parallel"),
    )(a, a, w_dw, b_dw, ln_g, ln_b, w_pw2, b_pw2, res, g_next)


def _ffn_down_loss(name, s, w2, layer, res, g, target):
    _, T, n = s.shape
    D = w2.shape[3]
    tm = min(512, T)

    def body(s_ref, w_ref, r_ref, g_ref, t_ref, dh_ref, loss_ref, dg_ref):
        @pl.when(pl.program_id(0) == 0)
        def _():
            loss_ref[...] = jnp.zeros_like(loss_ref)
            dg_ref[...] = jnp.zeros_like(dg_ref)

        hh = r_ref[...]
        for j in range(N_CHIPS):
            hh = hh + jnp.dot(s_ref[j], w_ref[j], preferred_element_type=F32)
        r = _rms_rstd(hh)
        g = g_ref[...]
        d = hh * r * g - t_ref[...]
        loss_ref[...] += 0.5 * jnp.sum(jnp.mean(d * d, axis=-1, keepdims=True), axis=0, keepdims=True)
        dout = d * (1.0 / D)
        dg_ref[...] += jnp.sum(dout * (hh * r), axis=0, keepdims=True)
        dxh = dout * g
        dh_ref[...] = r * dxh - hh * (r * r * r) * jnp.mean(dxh * hh, axis=-1, keepdims=True)

    return pl.pallas_call(
        body, name=name, grid=(T // tm,),
        in_specs=[pl.BlockSpec((N_CHIPS, tm, n), lambda i: (0, i, 0)),
                  pl.BlockSpec((N_CHIPS, None, n, D), lambda i: (0, layer, 0, 0)), _rows(tm, D),
                  _whole((1, D)), _rows(tm, D)],
        out_specs=[_rows(tm, D), _whole((1, 1)), _whole((1, D))],
        out_shape=[SDS((T, D), F32), SDS((1, 1), F32), SDS((1, D), F32)],
        compiler_params=_params("arbitrary"),
    )(s, w2, res, g, target)


def _ffn_bwd_down(name, dh, w2, layer, act, gate_grad, s, after=None):
    T, D = dh.shape
    n = w2.shape[2]
    tm = min(256, T)

    def body(dh_ref, w2_ref, act_ref, gg_ref, s_ref, *rest):
        dg1_ref, dg3_ref, dw_ref = rest[-3:]

        @pl.when(pl.program_id(0) == 0)
        def _():
            dw_ref[...] = jnp.zeros_like(dw_ref)

        dhb = dh_ref[...].astype(BF16)
        for j in range(N_CHIPS):
            ds = _dot_nt(dhb, w2_ref[j])
            dg1_ref[j] = (ds * gg_ref[j].astype(F32)).astype(BF16)
            dg3_ref[j] = (ds * act_ref[j].astype(F32)).astype(BF16)
            dw_ref[j] += _dot_tn(s_ref[j], dhb)

    slabs = pl.BlockSpec((N_CHIPS, tm, n), lambda i: (0, i, 0))
    hidden = SDS((N_CHIPS, T, n), BF16)
    return pl.pallas_call(
        body, name=name, grid=(T // tm,),
        in_specs=[_rows(tm, D),
                  pl.BlockSpec((N_CHIPS, None, n, D), lambda i: (0, layer, 0, 0), pipeline_mode=pl.Buffered(1)),
                  slabs, slabs, slabs] + ([] if after is None else [pl.BlockSpec(memory_space=pl.ANY)]),
        out_specs=[slabs, slabs, _whole((N_CHIPS, n, D))],
        out_shape=[hidden, hidden, SDS((N_CHIPS, n, D), F32)],
        compiler_params=_params("arbitrary"),
    )(dh, w2, act, gate_grad, s, *([] if after is None else [after]))


def _dot_tn(a, b):
    return lax.dot_general(a.astype(BF16), b.astype(BF16), (((0,), (0,)), ((), ())), preferred_element_type=F32)


def _dot_nt(a, b):
    return lax.dot_general(a.astype(BF16), b, (((1,), (1,)), ((), ())), preferred_element_type=F32)


def _mm_tn(name, a, b, col_chunks=1, after=None):
    a_slabs, b_slabs = a.ndim == 3, b.ndim == 3
    T = a.shape[-2]
    tt = min(1024, T)
    ka, nb = a.shape[-1], b.shape[-1]
    if a_slabs or b_slabs:
        out_dims = (N_CHIPS, ka, nb)
    elif col_chunks > 1:
        out_dims = (col_chunks, ka, nb // col_chunks)
    else:
        out_dims = (ka, nb)

    def body(a_ref, b_ref, *rest):
        o_ref = rest[-1]

        @pl.when(pl.program_id(0) == 0)
        def _():
            o_ref[...] = jnp.zeros_like(o_ref)

        if a_slabs:
            bb = b_ref[...].astype(BF16)
            for j in range(N_CHIPS):
                o_ref[j] += _dot_tn(a_ref[j], bb)
        elif b_slabs:
            aa = a_ref[...].astype(BF16)
            for j in range(N_CHIPS):
                o_ref[j] += _dot_tn(aa, b_ref[j])
        elif col_chunks > 1:
            aa = a_ref[...].astype(BF16)
            w = nb // col_chunks
            for j in range(col_chunks):
                o_ref[j] += _dot_tn(aa, b_ref[:, j * w:(j + 1) * w])
        else:
            o_ref[...] += _dot_tn(a_ref[...], b_ref[...])

    def spec(arr, slabs):
        if slabs:
            return pl.BlockSpec((N_CHIPS, tt, arr.shape[-1]), lambda t: (0, t, 0))
        return _rows(tt, arr.shape[-1])

    return pl.pallas_call(
        body, name=name, grid=(T // tt,),
        in_specs=[spec(a, a_slabs), spec(b, b_slabs)] + ([] if after is None else [pl.BlockSpec(memory_space=pl.ANY)]),
        out_specs=_whole(out_dims),
        out_shape=SDS(out_dims, F32),
        compiler_params=_params("arbitrary"),
    )(a, b, *([] if after is None else [after]))


def _mm_nt_normbwd(name, pairs, h, g, dh, after):
    T, D = h.shape
    tm = min(512, T)
    n_pairs = len(pairs)
    kinds = ["slabs" if dy.ndim == 3 else ("quarters" if w.ndim == 3 else "plain") for dy, w, _ in pairs]

    def body(*refs):
        dy_refs = refs[:n_pairs]
        w_refs = refs[n_pairs:2 * n_pairs]
        h_ref, g_ref, dh_ref, _, o_ref, dg_ref, cs_ref = refs[2 * n_pairs:]

        @pl.when(pl.program_id(0) == 0)
        def _():
            dg_ref[...] = jnp.zeros_like(dg_ref)
            cs_ref[...] = jnp.zeros_like(cs_ref)

        df = jnp.zeros((tm, D), F32)
        for dy_ref, w_ref, kd in zip(dy_refs, w_refs, kinds):
            if kd == "slabs":
                for j in range(N_CHIPS):
                    df = df + jnp.dot(dy_ref[j], w_ref[j], preferred_element_type=F32)
            elif kd == "quarters":
                n = w_ref.shape[2]
                for j in range(N_CHIPS):
                    df = df + _dot_nt(dy_ref[:, j * n:(j + 1) * n], w_ref[j])
            else:
                df = df + jnp.dot(dy_ref[...], w_ref[...], preferred_element_type=F32)
        hh = h_ref[...]
        r = _rms_rstd(hh)
        dg_ref[...] += jnp.sum(df * (hh * r), axis=0, keepdims=True)
        dxh = df * g_ref[...]
        out = dh_ref[...] + (r * dxh - hh * (r * r * r) * jnp.mean(dxh * hh, axis=-1, keepdims=True))
        o_ref[...] = out
        cs_ref[...] += jnp.sum(out, axis=0, keepdims=True)

    dy_specs, w_specs = [], []
    for (dy, w, layer), kd in zip(pairs, kinds):
        if kd == "slabs":
            dy_specs.append(pl.BlockSpec((N_CHIPS, tm, dy.shape[2]), lambda i: (0, i, 0)))
            w_specs.append(pl.BlockSpec((N_CHIPS, None, w.shape[2], D),
                                        functools.partial(lambda i, layer: (0, layer, 0, 0), layer=layer),
                                        pipeline_mode=pl.Buffered(1)))
        else:
            dy_specs.append(_rows(tm, dy.shape[1]))
            w_specs.append(_whole(w.shape))

    return pl.pallas_call(
        body, name=name, grid=(T // tm,),
        in_specs=[*dy_specs, *w_specs, _rows(tm, D), _whole((1, D)), _rows(tm, D), pl.BlockSpec(memory_space=pl.ANY)],
        out_specs=[_rows(tm, D), _whole((1, D)), _whole((1, D))],
        out_shape=[SDS((T, D), F32), SDS((1, D), F32), SDS((1, D), F32)],
        compiler_params=_params("arbitrary"),
    )(*[dy for dy, _, _ in pairs], *[w for _, w, _ in pairs], h, g, dh, after)


def _mm_nt(name, dy, w, out_dtype):
    T, N = dy.shape
    K = w.shape[0]
    tm = min(512, T)

    def body(dy_ref, w_ref, o_ref):
        o_ref[...] = lax.dot_general(dy_ref[...].astype(BF16), w_ref[...], (((1,), (1,)), ((), ())),
                                     preferred_element_type=F32).astype(out_dtype)

    return pl.pallas_call(
        body, name=name, grid=(T // tm,),
        in_specs=[_rows(tm, N), _whole((K, N))],
        out_specs=_rows(tm, K),
        out_shape=SDS((T, K), out_dtype),
        compiler_params=_params("parallel"),
    )(dy, w)


def _conv_bwd(dh, w_pw2, c, a, w_dw, ln_g, ln_b):
    T, D = c.shape
    tc = _conv_tile(T)
    per = tc // CONV_HALO
    n_tiles = T // tc
    last_halo = T // CONV_HALO - 1

    def ln_bwd(dact_v, c_v, lg, lb):
        xc = c_v - jnp.mean(c_v, axis=-1, keepdims=True)
        rstd = lax.rsqrt(jnp.mean(xc * xc, axis=-1, keepdims=True) + LN_EPS)
        z = xc * rstd
        _, dsilu = _silu_and_grad(z * lg + lb)
        dl = dact_v * dsilu
        dz = dl * lg
        dc = rstd * (dz - jnp.mean(dz, axis=-1, keepdims=True) - z * jnp.mean(dz * z, axis=-1, keepdims=True))
        return dc, dl, z

    def body(dh_ref, dhn_ref, wp_ref, c_ref, cn_ref, a_ref, ah_ref, w_ref, lg_ref, lb_ref,
             da_ref, dlg_ref, dlb_ref, dbdw_ref, dwdw_ref, dbpw1_ref, dc_sh, u_sh, du_scr):
        i = pl.program_id(0)

        @pl.when(i == 0)
        def _():
            for ref in (dlg_ref, dlb_ref, dbdw_ref, dwdw_ref, dbpw1_ref):
                ref[...] = jnp.zeros_like(ref)

        lg, lb = lg_ref[...], lb_ref[...]
        dh_rows = jnp.concatenate([dh_ref[...].astype(BF16), dhn_ref[...].astype(BF16)], axis=0)
        dact = _dot_nt(dh_rows, wp_ref[...])
        dc, dl, z = ln_bwd(dact[:tc], c_ref[...], lg, lb)
        dlg_ref[...] += jnp.sum(dl * z, axis=0, keepdims=True)
        dlb_ref[...] += jnp.sum(dl, axis=0, keepdims=True)
        dbdw_ref[...] += jnp.sum(dc, axis=0, keepdims=True)
        dcn, _, _ = ln_bwd(dact[tc:], cn_ref[...], lg, lb)
        dc_sh[0, 0:tc, :] = dc
        dc_sh[0, tc:, :] = jnp.where(i < n_tiles - 1, dcn, 0.0)
        _fill_shifted(dc_sh, tc)

        a_v = a_ref[...]
        a1 = a_v[:, :D]
        sg = jax.nn.sigmoid(a_v[:, D:])
        u_sh[0, 0:CONV_HALO, :] = jnp.where(i > 0, _glu(ah_ref[...], D), 0.0)
        u_sh[0, CONV_HALO:, :] = a1 * sg
        _fill_shifted(u_sh, tc)

        _depthwise_taps(dc_sh, w_ref, [CONV_WIDTH - 1 - j for j in range(CONV_WIDTH)], None, du_scr, tc)
        _depthwise_tap_grads(dc_sh, u_sh, [CONV_FIRST_TAP + j for j in range(CONV_WIDTH)], dwdw_ref, tc)

        du = du_scr[...]
        da1 = du * sg
        da2 = du * a1 * sg * (1.0 - sg)
        da_ref[:, :D] = da1.astype(BF16)
        da_ref[:, D:] = da2.astype(BF16)
        dbpw1_ref[:, :D] += jnp.sum(da1, axis=0, keepdims=True)
        dbpw1_ref[:, D:] += jnp.sum(da2, axis=0, keepdims=True)

    nxt = lambda i: (jnp.minimum((i + 1) * per, last_halo), 0)
    return pl.pallas_call(
        body, name="conv_bwd", grid=(n_tiles,),
        in_specs=[_rows(tc, D), pl.BlockSpec((CONV_HALO, D), nxt), _whole((D, D)),
                  _rows(tc, D), pl.BlockSpec((CONV_HALO, D), nxt),
                  _rows(tc, 2 * D),
                  pl.BlockSpec((CONV_HALO, 2 * D), lambda i: (jnp.maximum(i * per - 1, 0), 0)),
                  _whole((CONV_WIDTH, D)), _whole((1, D)), _whole((1, D))],
        out_specs=[_rows(tc, 2 * D), _whole((1, D)), _whole((1, D)), _whole((1, D)),
                   _whole((CONV_HALO, D)), _whole((1, 2 * D))],
        out_shape=[SDS((T, 2 * D), BF16), SDS((1, D), F32), SDS((1, D), F32), SDS((1, D), F32),
                   SDS((CONV_HALO, D), F32), SDS((1, 2 * D), F32)],
        scratch_shapes=[pltpu.VMEM((V7X_SUBLANES, tc + CONV_HALO, D), F32),
                        pltpu.VMEM((V7X_SUBLANES, tc + CONV_HALO, D), F32), pltpu.VMEM((tc, D), F32)],
        compiler_params=_params("arbitrary"),
    )(dh, dh, w_pw2, c, c, a, a, w_dw, ln_g, ln_b)


def _attn_bwd(qkv, dao, cos, sin, probs_saved, psink_saved):
    T = qkv.shape[0]
    nb = T // ATTN_BLOCK
    qw = N_Q_HEADS * HEAD_DIM
    kw = N_KV_HEADS * HEAD_DIM

    def body(q_ref, kc_ref, kp_ref, vc_ref, vp_ref, do_ref, cos_ref, sin_ref, cosp_ref, sinp_ref, probs_ref, psink_ref,
             dq_ref, dkv_ref, dsink_ref, dbq_ref, dbkv_ref, carry, prev_scr, cur_scr, dq_scr):
        n = pl.program_id(0)

        @pl.when(n == 0)
        def _():
            for ref in (dsink_ref, dbq_ref, dbkv_ref, carry):
                ref[...] = jnp.zeros_like(ref)

        @pl.when(n == nb)
        def _():
            prev_scr[...] = jnp.zeros_like(prev_scr)

        @pl.when(n < nb)
        def _():
            prev_part = _from_previous_block(N_Q_HEADS * ATTN_BLOCK)
            half = PAIRS_PER_KV * ATTN_BLOCK
            upper = _upper_lanes((ATTN_BLOCK, 2 * HEAD_DIM))
            groups = range(N_KV_HEADS)

            def nt(a, b):
                return lax.dot_general(a, b, (((1,), (1,)), ((), ())), preferred_element_type=F32)

            def kv_grad(even_rows, odd_rows, x):
                even = lax.dot_general(even_rows, x, (((0,), (0,)), ((), ())), preferred_element_type=F32)
                odd = lax.dot_general(odd_rows, x, (((0,), (0,)), ((), ())), preferred_element_type=F32)
                t = jnp.where(upper, odd, even)
                return t + _swap_lane_halves(t)

            q = [_pair_rows(q_ref, g) for g in groups]
            do = [_pair_rows(do_ref, g) for g in groups]
            k_prev = [_kv_operands(g, kp_ref[...]) for g in groups]
            k_cur = [_kv_operands(g, kc_ref[...]) for g in groups]
            v_prev = [_kv_operands(g, vp_ref[...]) for g in groups]
            v_cur = [_kv_operands(g, vc_ref[...]) for g in groups]
            probs = probs_ref[...].astype(F32)
            dp_prev = jnp.concatenate([nt(do[g], v_prev[g][i]) for g in groups for i in range(2)], axis=0)
            dp_cur = jnp.concatenate([nt(do[g], v_cur[g][i]) for g in groups for i in range(2)], axis=0)
            dp = jnp.where(prev_part, dp_prev, dp_cur)
            delta = jnp.sum(probs * dp, axis=1, keepdims=True)
            ds_prev, ds_cur = _split_folded(probs * (dp - delta) * (HEAD_DIM ** -0.5), prev_part)
            p_prev, p_cur = _split_folded(probs_ref[...], prev_part)
            for i, h in enumerate(_all_heads()):
                rows = slice(i * ATTN_BLOCK, (i + 1) * ATTN_BLOCK)
                dsink_ref[:, h:h + 1] += jnp.sum(-(psink_ref[:, h:h + 1] * delta[rows]), axis=0, keepdims=True)
            kv_grads = []
            for g in groups:
                even, odd = slice(2 * g * half, (2 * g + 1) * half), slice((2 * g + 1) * half, (2 * g + 2) * half)
                dq = (jnp.dot(ds_prev[even], k_prev[g][0], preferred_element_type=F32)
                      + jnp.dot(ds_cur[even], k_cur[g][0], preferred_element_type=F32)
                      + jnp.dot(ds_prev[odd], k_prev[g][1], preferred_element_type=F32)
                      + jnp.dot(ds_cur[odd], k_cur[g][1], preferred_element_type=F32))
                for i in range(PAIRS_PER_KV):
                    hp = g * PAIRS_PER_KV + i
                    dq_scr[:, hp * 2 * HEAD_DIM:(hp + 1) * 2 * HEAD_DIM] = dq[i * ATTN_BLOCK:(i + 1) * ATTN_BLOCK]
                kv_grads.append((kv_grad(ds_prev[even], ds_prev[odd], q[g]), kv_grad(ds_cur[even], ds_cur[odd], q[g]),
                                 kv_grad(p_prev[even], p_prev[odd], do[g]), kv_grad(p_cur[even], p_cur[odd], do[g])))
            (dkp0, dkc0, dvp0, dvc0), (dkp1, dkc1, dvp1, dvc1) = kv_grads
            prev_scr[:, :kw] = jnp.where(upper, dkp1, dkp0)
            prev_scr[:, kw:] = jnp.where(upper, dvp1, dvp0)
            cur_scr[:, :kw] = jnp.where(upper, dkc1, dkc0)
            cur_scr[:, kw:] = jnp.where(upper, dvc1, dvc0)
            dq_pre = _rope_transposed(dq_scr[...], cos_ref, sin_ref)
            dq_ref[...] = dq_pre.astype(BF16)
            dbq_ref[...] += jnp.sum(dq_pre, axis=0, keepdims=True)

        tot = carry[...] + prev_scr[...]
        dk_pre = _rope_transposed(tot[:, :kw], cosp_ref, sinp_ref)
        dkv_ref[:, :kw] = dk_pre.astype(BF16)
        dkv_ref[:, kw:] = tot[:, kw:].astype(BF16)
        dbkv_ref[:, :kw] += jnp.sum(dk_pre, axis=0, keepdims=True)
        dbkv_ref[:, kw:] += jnp.sum(tot[:, kw:], axis=0, keepdims=True)

        @pl.when(n < nb)
        def _():
            carry[...] = cur_scr[...]

    cur = lambda n: (jnp.minimum(n, nb - 1), 0)
    out_lag = lambda n: (jnp.maximum(n - 1, 0), 0)
    return pl.pallas_call(
        body, name="attn_bwd", grid=(nb + 1,),
        in_specs=[*_attn_specs(T),
                  pl.BlockSpec((ATTN_BLOCK, qw), cur),
                  pl.BlockSpec((ATTN_BLOCK, V7X_LANES), cur), pl.BlockSpec((ATTN_BLOCK, V7X_LANES), cur),
                  pl.BlockSpec((ATTN_BLOCK, V7X_LANES), out_lag), pl.BlockSpec((ATTN_BLOCK, V7X_LANES), out_lag),
                  pl.BlockSpec((None, N_Q_HEADS * ATTN_BLOCK, ATTN_BLOCK), lambda n: (jnp.minimum(n, nb - 1), 0, 0)),
                  pl.BlockSpec((ATTN_BLOCK, V7X_LANES), cur)],
        out_specs=[pl.BlockSpec((ATTN_BLOCK, qw), cur), pl.BlockSpec((ATTN_BLOCK, 2 * kw), out_lag),
                   _whole((1, N_Q_HEADS)), _whole((1, qw)), _whole((1, 2 * kw))],
        out_shape=[SDS((T, qw), BF16), SDS((T, 2 * kw), BF16),
                   SDS((1, N_Q_HEADS), F32), SDS((1, qw), F32), SDS((1, 2 * kw), F32)],
        scratch_shapes=[pltpu.VMEM((ATTN_BLOCK, 2 * kw), F32), pltpu.VMEM((ATTN_BLOCK, 2 * kw), F32),
                        pltpu.VMEM((ATTN_BLOCK, 2 * kw), F32), pltpu.VMEM((ATTN_BLOCK, qw), F32)],
        compiler_params=_params("arbitrary"),
    )(qkv, qkv, qkv, qkv, qkv, dao, cos, sin, cos, sin, probs_saved, psink_saved)


def _local_step(x, target, p, reduce_begin, reduce_send):
    T, D = x.shape
    cos, sin = _rope_tables(T)
    qw = N_Q_HEADS * HEAD_DIM
    nm, nf = p["norm_mix"], p["norm_ffn"]

    y0, qkv = _qkv_proj(x, nm[0:1], p["attn_w_qkv"], p["attn_b_qkv"], cos, sin, p["gather_started"])
    ao, attn_probs, sink_probs = _attn_fwd(qkv, p["attn_sinks"])
    h1, f0 = _mm_res("attn_out", ao, p["attn_w_o"], p["attn_b_o"], x, nf[0:1])
    p = {**p, **p["other_weights"](h1)}
    w1, w3 = p["ffn_w1"], p["ffn_w3"]
    act0, gg0, s0 = _ffn_up("ffn0_up", f0, w1, w3, 0, after=p["swap_started"])
    p = {**p, **p["rest_of_weights"](s0)}
    w2 = p["ffn_w2"]
    h2 = _ffn_down("ffn0_down", s0, w2, 0, h1)
    y1, a = _pw1_proj(h2, nm[1:2], p["conv_w_pw1"], p["conv_b_pw1"])
    c, act, h3, f1 = _conv_fwd(a, p["conv_w_dw"], p["conv_b_dw"], p["conv_ln_g"], p["conv_ln_b"],
                               p["conv_w_pw2"], p["conv_b_pw2"], h2, nf[1:2])
    act1, gg1, s1 = _ffn_up("ffn1_up", f1, w1, w3, 1)
    dh4, loss, d_norm_final = _ffn_down_loss("ffn1_down_loss", s1, w2, 1, h3, p["norm_final"], target)

    g = {}
    dg1, dg3, dw2_1 = _ffn_bwd_down("ffn1_bwd_down", dh4, w2, 1, act1, gg1, s1)
    dw1_1 = _mm_tn("ffn1_dw1", dg1, f1)
    dw3_1 = _mm_tn("ffn1_dw3", dg3, f1)
    begun = reduce_begin("ffn1", {("ffn_w1", 1): dw1_1, ("ffn_w3", 1): dw3_1, ("ffn_w2", 1): dw2_1})
    dh3, dnf1, db_pw2 = _mm_nt_normbwd("ffn1_bwd_in", [(dg1, w1, 1), (dg3, w3, 1)], h3, nf[1:2], dh4, begun)
    sent = reduce_send("ffn1", dh3)

    dw_pw2 = _mm_tn("conv_dw_pw2", act, dh3, after=sent)
    da, d_ln_g, d_ln_b, d_b_dw, d_w_dw, d_b_pw1 = _conv_bwd(dh3, p["conv_w_pw2"], c, a, p["conv_w_dw"],
                                                            p["conv_ln_g"], p["conv_ln_b"])
    dw_pw1 = _mm_tn("conv_dw_pw1", y1, da, col_chunks=N_CHIPS)
    begun = reduce_begin("conv", {("conv_w_pw2", 0): dw_pw2.reshape(N_CHIPS, -1, D), ("conv_w_pw1", 0): dw_pw1})
    dh2, dnm1, _ = _mm_nt_normbwd("conv_bwd_in", [(da, p["conv_w_pw1"], None)], h2, nm[1:2], dh3, begun)
    sent = reduce_send("conv", dh2)

    dg1, dg3, dw2_0 = _ffn_bwd_down("ffn0_bwd_down", dh2, w2, 0, act0, gg0, s0, after=sent)
    dw1_0 = _mm_tn("ffn0_dw1", dg1, f0)
    dw3_0 = _mm_tn("ffn0_dw3", dg3, f0)
    begun = reduce_begin("ffn0", {("ffn_w1", 0): dw1_0, ("ffn_w3", 0): dw3_0, ("ffn_w2", 0): dw2_0})
    dh1, dnf0, db_o = _mm_nt_normbwd("ffn0_bwd_in", [(dg1, w1, 0), (dg3, w3, 0)], h1, nf[0:1], dh2, begun)
    sent = reduce_send("ffn0", dh1)

    dw_o = _mm_tn("attn_dw_o", ao, dh1, after=sent)
    dao = _mm_nt("attn_bwd_out", dh1, p["attn_w_o"], BF16)
    dq, dkv, d_sinks, dbq, dbkv = _attn_bwd(qkv, dao, cos, sin, attn_probs, sink_probs)
    dwq = _mm_tn("attn_dw_q", dq, y0)
    dwkv = _mm_tn("attn_dw_kv", dkv, y0)
    wqkv = p["attn_w_qkv"]
    dwqkv = jnp.concatenate([dwq, dwkv], axis=0).reshape(N_CHIPS, -1, D)
    begun = reduce_begin("attn", {("attn_w_o", 0): dw_o.reshape(N_CHIPS, -1, D), ("attn_w_qkv", 0): dwqkv})
    sent = reduce_send("attn", begun)
    dx, dnm0, _ = _mm_nt_normbwd("attn_bwd_in", [(dq, wqkv[:qw], None), (dkv, wqkv[qw:], None)], x, nm[0:1], dh1,
                                 sent)

    g["norm_mix"] = jnp.concatenate([dnm0, dnm1], axis=0)
    g["norm_ffn"] = jnp.concatenate([dnf0, dnf1], axis=0)
    g["attn_b_qkv"] = jnp.concatenate([dbq, dbkv], axis=1)
    g["attn_sinks"] = d_sinks
    g["attn_b_o"] = db_o
    g["conv_b_pw1"] = d_b_pw1
    g["conv_w_dw"] = d_w_dw[:CONV_WIDTH]
    g["conv_b_dw"] = d_b_dw
    g["conv_ln_g"] = d_ln_g
    g["conv_ln_b"] = d_ln_b
    g["conv_b_pw2"] = db_pw2
    g["norm_final"] = d_norm_final
    return loss, dx, g


ANY = pl.BlockSpec(memory_space=pl.ANY)
VMEM_WHOLE = pl.BlockSpec(memory_space=pltpu.VMEM)


def _my_place():
    return lax.axis_index("x"), lax.axis_index("y"), lax.axis_index("c")


def _other_chips(x, y):
    places = [(1 - x, y), (x, 1 - y), (1 - x, 1 - y)]
    return [(bx, by, 2 * bx + by) for bx, by in places]


def _gather_small(name, v):
    r, w = v.shape

    def body(v_ref, o_ref, send_sems, recv_sems):
        x, y, c = _my_place()
        pairs = _all_to_all_copies(v_ref, o_ref, send_sems, recv_sems)
        for send, _ in pairs:
            send.start()
        o_ref[4 * x + 2 * y + c] = v_ref[...]
        for send, arrival in pairs:
            arrival.wait_recv()
            send.wait_send()

    return pl.pallas_call(
        body, name=name, out_shape=SDS((N_DEV, r, w), F32), in_specs=[VMEM_WHOLE], out_specs=VMEM_WHOLE,
        scratch_shapes=[pltpu.SemaphoreType.DMA((N_DEV - 1,)), pltpu.SemaphoreType.DMA((N_DEV - 1,))],
        compiler_params=pltpu.CompilerParams(vmem_limit_bytes=V7X_VMEM_LIMIT_BYTES),
    )(v)


def _cast_into_slot(name, gathered, shard, chip_idx, after=None):
    rows, cols = shard.shape
    tr = _pack_row_tile(rows)

    def body(k_ref, s_ref, *rest):
        rest[-1][...] = s_ref[...].astype(BF16)

    return pl.pallas_call(
        body, name=name,
        grid_spec=pltpu.PrefetchScalarGridSpec(
            num_scalar_prefetch=1, grid=(rows // tr,),
            in_specs=[pl.BlockSpec((tr, cols), lambda i, k_ref: (i, 0)), pl.BlockSpec(memory_space=pl.ANY)]
            + ([] if after is None else [pl.BlockSpec(memory_space=pl.ANY)]),
            out_specs=pl.BlockSpec((None, tr, cols), lambda i, k_ref: (k_ref[0], i, 0))),
        out_shape=SDS(gathered.shape, BF16),
        input_output_aliases={2: 0},
        compiler_params=_params("parallel"),
    )(chip_idx, shard, gathered, *([] if after is None else [after]))


def _row_halves(ref, c):
    half = ref.shape[1] // 2
    return pl.ds(pl.multiple_of(c * half, 16), half), pl.ds(pl.multiple_of((1 - c) * half, 16), half)


def _gather_ici_copies(refs, send_sems, recv_sems):
    x, y, c = _my_place()
    k = 2 * x + y
    pairs = []
    for i, ref in enumerate(refs):
        mine, _ = _row_halves(ref, c)
        for j, (bx, by, kb) in enumerate(_other_chips(x, y)):
            sems = dict(send_sem=send_sems.at[3 * i + j], recv_sem=recv_sems.at[3 * i + j], device_id_type=MESH)
            send = pltpu.make_async_remote_copy(src_ref=ref.at[k, mine], dst_ref=ref.at[k, mine],
                                                device_id=(bx, by, c), **sems)
            arrival = pltpu.make_async_remote_copy(src_ref=ref.at[kb, mine], dst_ref=ref.at[kb, mine],
                                                   device_id=(bx, by, c), **sems)
            pairs.append((send, arrival))
    return pairs


def _gather_d2d_copies(refs, send_sems, recv_sems):
    x, y, c = _my_place()
    pairs = []
    for i, ref in enumerate(refs):
        mine, theirs = _row_halves(ref, c)
        for j, (_, _, kb) in enumerate(_other_chips(x, y)):
            sem = 3 * i + j
            sems = dict(send_sem=send_sems.at[sem], recv_sem=recv_sems.at[sem], device_id=(x, y, 1 - c),
                        device_id_type=MESH)
            send = pltpu.make_async_remote_copy(src_ref=ref.at[kb, mine], dst_ref=ref.at[kb, mine], **sems)
            arrival = pltpu.make_async_remote_copy(src_ref=ref.at[kb, theirs], dst_ref=ref.at[kb, theirs], **sems)
            pairs.append((send, arrival))
    return pairs


def _run_copies(pairs):
    for send, _ in pairs:
        send.start()
    for send, arrival in pairs:
        send.wait_send()
        arrival.wait_recv()


def _gather_stage_copies(stage, refs, send_sems, recv_sems):
    if stage == "ici":
        return _gather_ici_copies(refs, send_sems, recv_sems)
    return _gather_d2d_copies(refs, send_sems, recv_sems)


def _gather_start(name, gathered, after, stage="ici"):
    n_w = len(gathered)

    def body(*refs):
        in_refs = refs[:n_w]
        send_sems, recv_sems = refs[n_w + 1:n_w + 3]
        for send, _ in _gather_stage_copies(stage, in_refs, send_sems, recv_sems):
            send.start()
        refs[-1][...] = jnp.zeros_like(refs[-1])

    out = pl.pallas_call(
        body, name=name,
        out_shape=(pltpu.SemaphoreType.DMA((3 * n_w,)), pltpu.SemaphoreType.DMA((3 * n_w,)),
                   *[pltpu.HBM(g.shape, g.dtype) for g in gathered], SDS((8, V7X_LANES), F32)),
        in_specs=[*[HBM_SPEC] * n_w, ANY], out_specs=(SEM_SPEC, SEM_SPEC, *[HBM_SPEC] * n_w, VMEM_WHOLE),
        input_output_aliases={i: 2 + i for i in range(n_w)},
        compiler_params=pltpu.CompilerParams(has_side_effects=DATAFLOW),
    )(*[pltpu.with_memory_space_constraint(g, pltpu.HBM) for g in gathered], after)
    return out[0], out[1], list(out[2:2 + n_w]), out[-1]


def _gather_wait(name, send_sems, recv_sems, gathered, after, stage="ici"):
    n_w = len(gathered)

    def body(*refs):
        in_refs = refs[:n_w]
        send_sems, recv_sems = refs[n_w:n_w + 2]
        for send, arrival in _gather_stage_copies(stage, in_refs, send_sems, recv_sems):
            send.wait_send()
            arrival.wait_recv()

    out = pl.pallas_call(
        body, name=name, out_shape=tuple(pltpu.HBM(g.shape, g.dtype) for g in gathered),
        in_specs=[*[HBM_SPEC] * n_w, SEM_SPEC, SEM_SPEC, ANY], out_specs=tuple([HBM_SPEC] * n_w),
        input_output_aliases={i: i for i in range(n_w)},
        compiler_params=pltpu.CompilerParams(has_side_effects=DATAFLOW),
    )(*gathered, send_sems, recv_sems, after)
    return list(out)


def _swap_fetched_with_sibling(name, gathered):
    n_w = len(gathered)

    def body(*refs):
        in_refs = refs[:n_w]
        send_sems, recv_sems = refs[2 * n_w:]
        _run_copies(_gather_d2d_copies(in_refs, send_sems, recv_sems))

    return pl.pallas_call(
        body, name=name, out_shape=[SDS(g.shape, g.dtype) for g in gathered],
        in_specs=[ANY] * n_w, out_specs=[ANY] * n_w, input_output_aliases={i: i for i in range(n_w)},
        scratch_shapes=[pltpu.SemaphoreType.DMA((3 * n_w,)), pltpu.SemaphoreType.DMA((3 * n_w,))],
    )(*gathered)


def _sibling_swap_copies(g_refs, land_refs, send_sems, recv_sems):
    x, y, c = _my_place()
    copies = []
    for i, g_ref in enumerate(g_refs):
        half = g_ref.shape[1] // 2
        theirs = pl.ds(pl.multiple_of((1 - c) * half, 8), half)
        copies.append(pltpu.make_async_remote_copy(
            src_ref=g_ref.at[:, theirs], dst_ref=land_refs[i], send_sem=send_sems.at[i], recv_sem=recv_sems.at[i],
            device_id=(x, y, 1 - c), device_id_type=MESH))
    return copies


def _sibling_swap_start(name, grads):
    n_g = len(grads)

    def body(*refs):
        g_refs, land_refs = refs[:n_g], refs[n_g:2 * n_g]
        send_sems, recv_sems = refs[2 * n_g:2 * n_g + 2]
        for cp in _sibling_swap_copies(g_refs, land_refs, send_sems, recv_sems):
            cp.start()
        refs[-1][...] = jnp.zeros_like(refs[-1])

    lands = [pltpu.with_memory_space_constraint(lax.empty((g.shape[0], g.shape[1] // 2, g.shape[2]), g.dtype),
                                                pltpu.HBM) for g in grads]
    out = pl.pallas_call(
        body, name=name,
        out_shape=(pltpu.SemaphoreType.DMA((n_g,)), pltpu.SemaphoreType.DMA((n_g,)),
                   *[pltpu.HBM(g.shape, g.dtype) for g in grads], *[pltpu.HBM(l.shape, l.dtype) for l in lands],
                   SDS((8, V7X_LANES), F32)),
        in_specs=[HBM_SPEC] * (2 * n_g), out_specs=(SEM_SPEC, SEM_SPEC, *[HBM_SPEC] * (2 * n_g), VMEM_WHOLE),
        input_output_aliases={i: 2 + i for i in range(2 * n_g)},
        compiler_params=pltpu.CompilerParams(has_side_effects=DATAFLOW),
    )(*[pltpu.with_memory_space_constraint(g, pltpu.HBM) for g in grads], *lands)
    return out[0], out[1], list(out[2:2 + n_g]), list(out[2 + n_g:2 + 2 * n_g]), out[-1]


def _sibling_swap_wait(name, send_sems, recv_sems, grads, lands, after):
    n_g = len(grads)

    def body(*refs):
        g_refs, land_refs = refs[:n_g], refs[n_g:2 * n_g]
        send_sems, recv_sems = refs[2 * n_g:2 * n_g + 2]
        for cp in _sibling_swap_copies(g_refs, land_refs, send_sems, recv_sems):
            cp.wait_send()
            cp.wait_recv()

    out = pl.pallas_call(
        body, name=name,
        out_shape=(*[pltpu.HBM(g.shape, g.dtype) for g in grads], *[pltpu.HBM(l.shape, l.dtype) for l in lands]),
        in_specs=[*[HBM_SPEC] * (2 * n_g), SEM_SPEC, SEM_SPEC, ANY], out_specs=tuple([HBM_SPEC] * (2 * n_g)),
        input_output_aliases={i: i for i in range(2 * n_g)},
        compiler_params=pltpu.CompilerParams(has_side_effects=DATAFLOW),
    )(*grads, *lands, send_sems, recv_sems, after)
    return list(out[:n_g]), list(out[n_g:])


def _pack_row_tile(rows):
    for t in range(min(rows, 512), 7, -1):
        if rows % t == 0 and t % 8 == 0:
            return t
    return rows


def _add_sibling_half(name, grads, from_sibling, c_idx):
    n, R, w = grads.shape
    half = R // 2
    tr = _pack_row_tile(half)
    steps = half // tr

    def body(c_ref, g_ref, s_ref, o_ref):
        o_ref[...] = (g_ref[...] + s_ref[...]).astype(BF16)

    return pl.pallas_call(
        body, name=name,
        grid_spec=pltpu.PrefetchScalarGridSpec(
            num_scalar_prefetch=1, grid=(n, steps),
            in_specs=[pl.BlockSpec((1, tr, w), lambda j, i, c_ref: (j, c_ref[0] * steps + i, 0)),
                      pl.BlockSpec((1, tr, w), lambda j, i, c_ref: (j, i, 0))],
            out_specs=pl.BlockSpec((1, tr, w), lambda j, i, c_ref: (j, i, 0))),
        out_shape=SDS((n, half, w), BF16),
        compiler_params=_params("parallel", "parallel"),
    )(c_idx, grads, from_sibling)


HBM_SPEC = pl.BlockSpec(memory_space=pltpu.HBM)
SEM_SPEC = pl.BlockSpec(memory_space=pltpu.SEMAPHORE)
DATAFLOW = pltpu.SideEffectType.DATAFLOW_SIDE_EFFECTING


def _chip_scatter_copies(p_refs, land_refs, send_sems, recv_sems):
    x, y, c = _my_place()
    return [pltpu.make_async_remote_copy(
        src_ref=p_refs[i].at[kb], dst_ref=land_refs[i].at[j], send_sem=send_sems.at[3 * i + j],
        recv_sem=recv_sems.at[3 * i + j], device_id=(bx, by, c), device_id_type=MESH)
        for i in range(len(p_refs)) for j, (bx, by, kb) in enumerate(_other_chips(x, y))]


def _scatter_start(name, partials):
    n_p = len(partials)

    def body(*refs):
        p_refs, land_refs = refs[:n_p], refs[n_p:2 * n_p]
        send_sems, recv_sems = refs[2 * n_p:2 * n_p + 2]
        for cp in _chip_scatter_copies(p_refs, land_refs, send_sems, recv_sems):
            cp.start()
        refs[-1][...] = jnp.zeros_like(refs[-1])

    lands = [pltpu.with_memory_space_constraint(lax.empty((N_CHIPS - 1,) + p.shape[1:], p.dtype), pltpu.HBM)
             for p in partials]
    out = pl.pallas_call(
        body, name=name,
        out_shape=(pltpu.SemaphoreType.DMA((3 * n_p,)), pltpu.SemaphoreType.DMA((3 * n_p,)),
                   *[pltpu.HBM(p.shape, p.dtype) for p in partials], *[pltpu.HBM(l.shape, l.dtype) for l in lands],
                   SDS((8, V7X_LANES), F32)),
        in_specs=[HBM_SPEC] * (2 * n_p), out_specs=(SEM_SPEC, SEM_SPEC, *[HBM_SPEC] * (2 * n_p), VMEM_WHOLE),
        input_output_aliases={i: 2 + i for i in range(2 * n_p)},
        compiler_params=pltpu.CompilerParams(has_side_effects=DATAFLOW),
    )(*[pltpu.with_memory_space_constraint(p, pltpu.HBM) for p in partials], *lands)
    return out[0], out[1], list(out[2:2 + n_p]), list(out[2 + n_p:2 + 2 * n_p]), out[-1]


def _scatter_wait(name, send_sems, recv_sems, partials, lands, after):
    n_p = len(partials)

    def body(*refs):
        p_refs, land_refs = refs[:n_p], refs[n_p:2 * n_p]
        send_sems, recv_sems = refs[2 * n_p:2 * n_p + 2]
        for cp in _chip_scatter_copies(p_refs, land_refs, send_sems, recv_sems):
            cp.wait_send()
            cp.wait_recv()

    out = pl.pallas_call(
        body, name=name,
        out_shape=(*[pltpu.HBM(p.shape, p.dtype) for p in partials], *[pltpu.HBM(l.shape, l.dtype) for l in lands]),
        in_specs=[*[HBM_SPEC] * (2 * n_p), SEM_SPEC, SEM_SPEC, ANY], out_specs=tuple([HBM_SPEC] * (2 * n_p)),
        input_output_aliases={i: i for i in range(2 * n_p)},
        compiler_params=pltpu.CompilerParams(has_side_effects=DATAFLOW),
    )(*partials, *lands, send_sems, recv_sems, after)
    return list(out[:n_p]), list(out[n_p:])


def _sum_chip_partials(name, grads, from_sibling, received, shard, layer, place):
    n, half, w = from_sibling.shape
    tr = _pack_row_tile(half)
    steps = half // tr

    def body(place_ref, g_ref, s_ref, r_ref, shard_ref, o_ref):
        own = g_ref[0] + s_ref[0]
        o_ref[...] = ((own + r_ref[0].astype(F32)) + r_ref[1].astype(F32)) + r_ref[2].astype(F32)

    return pl.pallas_call(
        body, name=name,
        grid_spec=pltpu.PrefetchScalarGridSpec(
            num_scalar_prefetch=1, grid=(steps,),
            in_specs=[pl.BlockSpec((1, tr, w), lambda i, place_ref: (place_ref[0], place_ref[1] * steps + i, 0)),
                      pl.BlockSpec((1, tr, w), lambda i, place_ref: (place_ref[0], i, 0)),
                      pl.BlockSpec((n - 1, tr, w), lambda i, place_ref: (0, i, 0)),
                      pl.BlockSpec(memory_space=pl.ANY)],
            out_specs=pl.BlockSpec((tr, w), lambda i, place_ref: ((2 * layer + place_ref[1]) * steps + i, 0))),
        out_shape=SDS(shard.shape, F32),
        input_output_aliases={4: 0},
        compiler_params=_params("parallel"),
    )(place, grads, from_sibling, received, shard)


def _join_halves(shards, layers):
    n_s = len(shards)
    n_sem = sum(layers)

    def body(*refs):
        in_refs = refs[:n_s]
        send_sems, recv_sems = refs[2 * n_s:]
        x, y, c = _my_place()
        copies, sem = [], 0
        for ref, n_layers in zip(in_refs, layers):
            half = ref.shape[0] // (2 * n_layers)
            for layer in range(n_layers):
                mine = pl.ds(pl.multiple_of(layer * 2 * half + c * half, 8), half)
                theirs = pl.ds(pl.multiple_of(layer * 2 * half + (1 - c) * half, 8), half)
                send = pltpu.make_async_remote_copy(
                    src_ref=ref.at[mine], dst_ref=ref.at[mine], send_sem=send_sems.at[sem], recv_sem=recv_sems.at[sem],
                    device_id=(x, y, 1 - c), device_id_type=MESH)
                send.start()
                arrival = pltpu.make_async_remote_copy(
                    src_ref=ref.at[theirs], dst_ref=ref.at[theirs], send_sem=send_sems.at[sem],
                    recv_sem=recv_sems.at[sem], device_id=(x, y, 1 - c), device_id_type=MESH)
                copies.append((send, arrival))
                sem += 1
        for send, arrival in copies:
            send.wait_send()
            arrival.wait_recv()

    return pl.pallas_call(
        body, name="join_halves", out_shape=[SDS(s.shape, s.dtype) for s in shards],
        in_specs=[ANY] * n_s, out_specs=[ANY] * n_s,
        input_output_aliases={i: i for i in range(n_s)},
        scratch_shapes=[pltpu.SemaphoreType.DMA((n_sem,)), pltpu.SemaphoreType.DMA((n_sem,))],
    )(*shards)


def _all_to_all_copies(v_ref, land_ref, send_sems, recv_sems):
    x, y, c = _my_place()
    me = 4 * x + 2 * y + c
    pairs = []
    for k in range(1, N_DEV):
        px, py, pc = (1 - x if k & 4 else x), (1 - y if k & 2 else y), (1 - c if k & 1 else c)
        sems = dict(send_sem=send_sems.at[k - 1], recv_sem=recv_sems.at[k - 1], device_id=(px, py, pc),
                    device_id_type=MESH)
        send = pltpu.make_async_remote_copy(src_ref=v_ref, dst_ref=land_ref.at[me], **sems)
        arrival = pltpu.make_async_remote_copy(src_ref=v_ref, dst_ref=land_ref.at[4 * px + 2 * py + pc], **sems)
        pairs.append((send, arrival))
    return pairs


def _small_reduce_start(name, v):
    def body(v_ref, land_ref, send_sems, recv_sems, v_out, land_out):
        for send, _ in _all_to_all_copies(v_ref, land_ref, send_sems, recv_sems):
            send.start()

    land = pltpu.with_memory_space_constraint(jnp.zeros((N_DEV,) + v.shape, v.dtype), pltpu.HBM)
    return pl.pallas_call(
        body, name=name,
        out_shape=(pltpu.SemaphoreType.DMA((N_DEV - 1,)), pltpu.SemaphoreType.DMA((N_DEV - 1,)),
                   pltpu.HBM(v.shape, v.dtype), pltpu.HBM(land.shape, land.dtype)),
        in_specs=[HBM_SPEC, HBM_SPEC], out_specs=(SEM_SPEC, SEM_SPEC, HBM_SPEC, HBM_SPEC),
        input_output_aliases={0: 2, 1: 3},
        compiler_params=pltpu.CompilerParams(has_side_effects=DATAFLOW),
    )(pltpu.with_memory_space_constraint(v, pltpu.HBM), land)


def _small_reduce_wait(name, send_sems, recv_sems, v, land, after):
    def body(v_ref, land_ref, send_sems, recv_sems, after_ref, v_out, land_out):
        for send, arrival in _all_to_all_copies(v_ref, land_ref, send_sems, recv_sems):
            send.wait_send()
            arrival.wait_recv()

    return pl.pallas_call(
        body, name=name, out_shape=(pltpu.HBM(v.shape, v.dtype), pltpu.HBM(land.shape, land.dtype)),
        in_specs=[HBM_SPEC, HBM_SPEC, SEM_SPEC, SEM_SPEC, ANY], out_specs=(HBM_SPEC, HBM_SPEC),
        input_output_aliases={0: 0, 1: 1},
        compiler_params=pltpu.CompilerParams(has_side_effects=DATAFLOW),
    )(v, land, send_sems, recv_sems, after)


def _sum_device_slots(name, v, land, me):
    r, w = v.shape

    def body(me_ref, v_ref, land_ref, o_ref):
        mine = v_ref[...]
        acc = jnp.where(me_ref[0] == 0, mine, land_ref[0])
        for d in range(1, N_DEV):
            acc = acc + jnp.where(me_ref[0] == d, mine, land_ref[d])
        o_ref[...] = acc

    return pl.pallas_call(
        body, name=name,
        grid_spec=pltpu.PrefetchScalarGridSpec(
            num_scalar_prefetch=1, grid=(1,),
            in_specs=[pl.BlockSpec((r, w), lambda i, me_ref: (0, 0)),
                      pl.BlockSpec((N_DEV, r, w), lambda i, me_ref: (0, 0, 0))],
            out_specs=pl.BlockSpec((r, w), lambda i, me_ref: (0, 0))),
        out_shape=SDS((r, w), F32),
        compiler_params=_params("arbitrary"),
    )(me, v, land)


def _adamw(name, w, g, m, v):
    rows, width = w.shape
    tr = _pack_row_tile(rows)

    def body(w_ref, g_ref, m_ref, v_ref, g_out_ref, d_ref, nm_ref, nv_ref):
        gg = g_ref[...]
        g_out_ref[...] = gg
        m_new = ADAM_B1 * m_ref[...] + (1.0 - ADAM_B1) * gg
        v_new = ADAM_B2 * v_ref[...] + (1.0 - ADAM_B2) * (gg * gg)
        m_hat = m_new / (1.0 - ADAM_B1 ** ADAM_STEP)
        v_hat = v_new / (1.0 - ADAM_B2 ** ADAM_STEP)
        d_ref[...] = -ADAM_LR * (m_hat / (jnp.sqrt(v_hat) + ADAM_EPS) + ADAM_WD * w_ref[...])
        nm_ref[...] = m_new
        nv_ref[...] = v_new

    spec = _rows(tr, width)
    return pl.pallas_call(
        body, name=name, grid=(rows // tr,),
        in_specs=[spec] * 4, out_specs=[spec] * 4,
        out_shape=[SDS((rows, width), F32)] * 4,
        compiler_params=_params("parallel"),
    )(w, g, m, v)


WEIGHT_NAMES = ['norm_mix', 'norm_ffn', 'attn_w_qkv', 'attn_b_qkv', 'attn_sinks', 'attn_w_o', 'attn_b_o',
                'conv_w_pw1', 'conv_b_pw1', 'conv_w_dw', 'conv_b_dw', 'conv_ln_g', 'conv_ln_b', 'conv_w_pw2',
                'conv_b_pw2', 'ffn_w1', 'ffn_w3', 'ffn_w2', 'norm_final']
BIG = ['attn_w_qkv', 'attn_w_o', 'conv_w_pw1', 'conv_w_pw2', 'ffn_w1', 'ffn_w3', 'ffn_w2']
COLUMN_SPLIT = ('attn_w_qkv', 'conv_w_pw1', 'ffn_w1', 'ffn_w3')
STORED_TRANSPOSED = ('attn_w_qkv', 'ffn_w1', 'ffn_w3')
SMALL_SPLIT = ['conv_b_pw1', 'conv_w_dw', 'conv_b_dw', 'conv_ln_g', 'conv_ln_b', 'conv_b_pw2']
SMALL_WHOLE = ['norm_mix', 'norm_ffn', 'attn_b_qkv', 'attn_sinks', 'attn_b_o', 'norm_final']


def _keeps_rows(shape):
    return len(shape) == 2 and shape[0] > 1 and shape[1] == PACK_W


def _pack_rows(arrays, dtype, row_multiple):
    blocks = [jnp.pad(a.astype(dtype), ((0, -a.shape[0] % V7X_SUBLANES), (0, 0)))
              for a in arrays if _keeps_rows(a.shape)]
    flat = jnp.concatenate([a.astype(dtype).reshape(-1) for a in arrays if not _keeps_rows(a.shape)])
    multiple = max(row_multiple, V7X_SUBLANES)
    rows = -(-(-(-flat.shape[0] // PACK_W)) // multiple) * multiple
    blocks.append(jnp.pad(flat, (0, rows * PACK_W - flat.shape[0])).reshape(rows, PACK_W))
    return jnp.concatenate(blocks, axis=0) if len(blocks) > 1 else blocks[0]


def _unpack_rows(pack, shapes):
    out, row = {}, 0
    for i, shape in enumerate(shapes):
        if _keeps_rows(shape):
            out[i] = pack[row:row + shape[0]]
            row += -(-shape[0] // V7X_SUBLANES) * V7X_SUBLANES
    flat, at = pack[row:].reshape(-1), 0
    for i, shape in enumerate(shapes):
        if not _keeps_rows(shape):
            size = 1
            for s in shape:
                size *= s
            out[i] = flat[at:at + size].reshape(shape)
            at += size
    return [out[i] for i in range(len(shapes))]


def _join_chip_axis(name, parts):
    axis = parts.ndim - 1 if name in COLUMN_SPLIT or name in SMALL_SPLIT else parts.ndim - 2
    moved = jnp.moveaxis(parts, 0, axis - 1)
    shape = list(moved.shape)
    shape[axis - 1:axis + 1] = [shape[axis - 1] * shape[axis]]
    return moved.reshape(shape)


def _split_chip_axis(name, whole, shard_shape):
    axis = len(shard_shape) - 1 if name in COLUMN_SPLIT or name in SMALL_SPLIT else len(shard_shape) - 2
    shape = list(whole.shape)
    shape[axis:axis + 1] = [N_CHIPS, shard_shape[axis]]
    return jnp.moveaxis(whole.reshape(shape), axis, 0)


def kernel(x, norm_mix, norm_ffn, attn_w_qkv, attn_b_qkv, attn_sinks, attn_w_o, attn_b_o, conv_w_pw1, conv_b_pw1, conv_w_dw, conv_b_dw, conv_ln_g, conv_ln_b, conv_w_pw2, conv_b_pw2, ffn_w1, ffn_w3, ffn_w2, norm_final, loss_target, m_norm_mix, m_norm_ffn, m_attn_w_qkv, m_attn_b_qkv, m_attn_sinks, m_attn_w_o, m_attn_b_o, m_conv_w_pw1, m_conv_b_pw1, m_conv_w_dw, m_conv_b_dw, m_conv_ln_g, m_conv_ln_b, m_conv_w_pw2, m_conv_b_pw2, m_ffn_w1, m_ffn_w3, m_ffn_w2, m_norm_final, v_norm_mix, v_norm_ffn, v_attn_w_qkv, v_attn_b_qkv, v_attn_sinks, v_attn_w_o, v_attn_b_o, v_conv_w_pw1, v_conv_b_pw1, v_conv_w_dw, v_conv_b_dw, v_conv_ln_g, v_conv_ln_b, v_conv_w_pw2, v_conv_b_pw2, v_ffn_w1, v_ffn_w3, v_ffn_w2, v_norm_final):
    w = dict(zip(WEIGHT_NAMES, (norm_mix, norm_ffn, attn_w_qkv, attn_b_qkv, attn_sinks, attn_w_o, attn_b_o,
                                conv_w_pw1, conv_b_pw1, conv_w_dw, conv_b_dw, conv_ln_g, conv_ln_b, conv_w_pw2,
                                conv_b_pw2, ffn_w1, ffn_w3, ffn_w2, norm_final)))
    m = dict(zip(WEIGHT_NAMES, (m_norm_mix, m_norm_ffn, m_attn_w_qkv, m_attn_b_qkv, m_attn_sinks, m_attn_w_o,
                                m_attn_b_o, m_conv_w_pw1, m_conv_b_pw1, m_conv_w_dw, m_conv_b_dw, m_conv_ln_g,
                                m_conv_ln_b, m_conv_w_pw2, m_conv_b_pw2, m_ffn_w1, m_ffn_w3, m_ffn_w2, m_norm_final)))
    v = dict(zip(WEIGHT_NAMES, (v_norm_mix, v_norm_ffn, v_attn_w_qkv, v_attn_b_qkv, v_attn_sinks, v_attn_w_o,
                                v_attn_b_o, v_conv_w_pw1, v_conv_b_pw1, v_conv_w_dw, v_conv_b_dw, v_conv_ln_g,
                                v_conv_ln_b, v_conv_w_pw2, v_conv_b_pw2, v_ffn_w1, v_ffn_w3, v_ffn_w2, v_norm_final)))
    T, D = x.shape[1], x.shape[2]
    c_idx = lax.axis_index("c").astype(jnp.int32).reshape(1)
    chip = (2 * lax.axis_index("x") + lax.axis_index("y")).astype(jnp.int32)

    def as_rows(n, a):
        a = jnp.swapaxes(a, -1, -2) if n in STORED_TRANSPOSED else a
        return a.reshape(-1, a.shape[-1])

    def from_rows(n, rows):
        shape = w[n].shape[:-2] + w[n].shape[:-3:-1] if n in STORED_TRANSPOSED else w[n].shape
        a = rows.reshape(shape)
        return jnp.swapaxes(a, -1, -2) if n in STORED_TRANSPOSED else a

    def cast(n, after=None):
        return _cast_into_slot(f"cast_{n}", lax.empty((N_CHIPS,) + as_rows(n, w[n]).shape, BF16), as_rows(n, w[n]),
                               chip.reshape(1), after)

    first, later = BIG[:2], BIG[2:]
    slabs = {n: cast(n) for n in first}
    attn_send, attn_recv, attn_travelling, attn_started = _gather_start(
        "gather_attn_start", [slabs[n] for n in first], slabs[first[0]])
    slabs.update({n: cast(n, attn_started) for n in later})
    layers = ffn_w1.shape[0]
    small_shapes = [w[n].shape for n in SMALL_SPLIT]
    small_all = _gather_small("gather_small", _pack_rows([w[n] for n in SMALL_SPLIT], F32, 8))
    attn_landed = _gather_wait("gather_attn_wait", attn_send, attn_recv, attn_travelling, small_all)
    qkv_parts, w_o_parts = _swap_fetched_with_sibling("gather_attn_swap", attn_landed)
    send_sems, recv_sems, travelling, gather_started = _gather_start("gather_start", [slabs[n] for n in later],
                                                                     qkv_parts)
    per_chip = [_unpack_rows(small_all[2 * j], small_shapes) for j in range(N_CHIPS)]
    full = {}
    for i, n in enumerate(SMALL_SPLIT):
        full[n] = _join_chip_axis(n, jnp.stack([per_chip[j][i] for j in range(N_CHIPS)]))

    def other_weights(after):
        landed = dict(zip(later, _gather_wait("gather_wait", send_sems, recv_sems, travelling, after)))
        now, then = ["ffn_w1", "ffn_w3"], ["conv_w_pw1", "conv_w_pw2", "ffn_w2"]
        ready = dict(zip(now, _swap_fetched_with_sibling("gather_swap", [landed[n] for n in now])))
        swap_send, swap_recv, swapping_rest, swap_started = _gather_start(
            "gather_swap_start", [landed[n] for n in then], ready[now[0]], stage="d2d")

        def rest_of_weights(after_next):
            rest = dict(zip(then, _gather_wait("gather_swap_wait", swap_send, swap_recv, swapping_rest, after_next,
                                               stage="d2d")))
            return {"conv_w_pw1": rest["conv_w_pw1"], "conv_w_pw2": rest["conv_w_pw2"].reshape(-1, D),
                    "ffn_w2": rest["ffn_w2"].reshape(N_CHIPS, layers, -1, D)}

        return {"ffn_w1": ready["ffn_w1"].reshape(N_CHIPS, layers, -1, D),
                "ffn_w3": ready["ffn_w3"].reshape(N_CHIPS, layers, -1, D),
                "swap_started": swap_started, "rest_of_weights": rest_of_weights}

    p = {
        "norm_mix": norm_mix, "norm_ffn": norm_ffn, "norm_final": norm_final.reshape(1, D),
        "attn_w_qkv": qkv_parts.reshape(-1, D), "attn_b_qkv": attn_b_qkv,
        "attn_sinks": attn_sinks, "attn_w_o": w_o_parts.reshape(-1, D), "attn_b_o": attn_b_o,
        "conv_b_pw1": full["conv_b_pw1"], "conv_w_dw": full["conv_w_dw"][0],
        "conv_b_dw": full["conv_b_dw"], "conv_ln_g": full["conv_ln_g"], "conv_ln_b": full["conv_ln_b"],
        "conv_b_pw2": full["conv_b_pw2"], "gather_started": gather_started, "other_weights": other_weights,
    }
    swapping, in_flight = {}, []

    def reduce_begin(tag, grads):
        keys = list(grads)
        *handles, begun = _sibling_swap_start(f"sibling_swap_start_{tag}", [grads[k] for k in keys])
        swapping[tag] = (keys, handles)
        return begun

    def reduce_send(tag, after):
        keys, (swap_send, swap_recv, grads, lands) = swapping[tag]
        grads, from_sibling = _sibling_swap_wait(f"sibling_swap_wait_{tag}", swap_send, swap_recv, grads, lands, after)
        partials = [_add_sibling_half(f"add_sibling_half_{tag}{i}", gr, fs, c_idx)
                    for i, (gr, fs) in enumerate(zip(grads, from_sibling))]
        *handles, sent = _scatter_start(f"scatter_start_{tag}", partials)
        in_flight.append((tag, keys, handles, grads, from_sibling))
        return sent

    loss_part, dx, g = _local_step(x[0], loss_target[0], p, reduce_begin, reduce_send)
    for n in SMALL_WHOLE + SMALL_SPLIT:
        g[n] = g[n].reshape((-1,) + g[n].shape[-2:]) if w[n].ndim == 3 else g[n].reshape(w[n].shape[:-1] + (-1,))

    small_pack = _pack_rows([loss_part] + [g[n] for n in SMALL_WHOLE] + [g[n] for n in SMALL_SPLIT], F32, 8)
    small_send, small_recv, small_pack, small_land = _small_reduce_start("small_reduce_start", small_pack)

    place = jnp.stack([chip, c_idx[0]])
    shard_grad = {n: lax.empty(as_rows(n, w[n]).shape, F32) for n in BIG}
    for tag, keys, (send_sems, recv_sems, partials, lands), grads, from_sibling in in_flight:
        _, received = _scatter_wait(f"scatter_wait_{tag}", send_sems, recv_sems, partials, lands, small_pack)
        for i, (n, layer) in enumerate(keys):
            shard_grad[n] = _sum_chip_partials(f"sum_chip_partials_{tag}{i}", grads[i], from_sibling[i], received[i],
                                               shard_grad[n], layer, place)
    g_big = dict(zip(BIG, _join_halves([shard_grad[n] for n in BIG], [w[n].shape[0] for n in BIG])))
    big_out = {}
    for n in BIG:
        step = _adamw(f"adamw_{n}", as_rows(n, w[n]), g_big[n], as_rows(n, m[n]), as_rows(n, v[n]))
        big_out[n] = [from_rows(n, a) for a in step]

    small_whole_shapes = [w[n].shape for n in SMALL_WHOLE]
    small_full_shapes = [g[n].shape for n in SMALL_SPLIT]
    small_pack, small_land = _small_reduce_wait("small_reduce_wait", small_send, small_recv, small_pack, small_land,
                                                big_out[BIG[-1]][1])
    reduced = _sum_device_slots("small_reduce_sum", small_pack, small_land, (2 * chip + c_idx[0]).reshape(1))
    pieces = _unpack_rows(reduced, [(1,)] + small_whole_shapes + small_full_shapes)
    loss = pieces[0].reshape(())
    g_small = dict(zip(SMALL_WHOLE, pieces[1:1 + len(SMALL_WHOLE)]))
    for n, whole in zip(SMALL_SPLIT, pieces[1 + len(SMALL_WHOLE):]):
        parts = _split_chip_axis(n, whole, w[n].shape)
        g_small[n] = lax.dynamic_index_in_dim(parts, chip, axis=0, keepdims=False)
    small = SMALL_WHOLE + SMALL_SPLIT
    _, d_small, m_small, v_small = _adamw(
        "adamw_small", _pack_rows([w[n] for n in small], F32, 8), _pack_rows([g_small[n] for n in small], F32, 8),
        _pack_rows([m[n] for n in small], F32, 8), _pack_rows([v[n] for n in small], F32, 8))

    outs = {}
    for slot, (tag, small_pack) in enumerate((("g", None), ("d", d_small), ("m", m_small), ("v", v_small))):
        vals = {n: big_out[n][slot] for n in BIG}
        if small_pack is None:
            vals.update(g_small)
        else:
            vals.update(zip(small, _unpack_rows(small_pack, [w[n].shape for n in small])))
        outs[tag] = vals
    return (loss, dx.reshape(1, T, D), *[outs["g"][n] for n in WEIGHT_NAMES], *[outs["d"][n] for n in WEIGHT_NAMES],
            *[outs["m"][n] for n in WEIGHT_NAMES], *[outs["v"][n] for n in WEIGHT_NAMES])
```

```python
import functools

import jax
import jax.numpy as jnp
from jax import lax
from jax.experimental import pallas as pl
from jax.experimental.pallas import tpu as pltpu

F32 = jnp.float32
BF16 = jnp.bfloat16
SDS = jax.ShapeDtypeStruct
MESH = pl.DeviceIdType.MESH

HEAD_DIM = 64
N_Q_HEADS = 16
N_KV_HEADS = 2
Q_PER_KV = N_Q_HEADS // N_KV_HEADS
ATTN_BLOCK = 128
ROPE_THETA = 10000.0
CONV_WIDTH = 31
CONV_HALO = 32
CONV_FIRST_TAP = CONV_HALO - CONV_WIDTH + 1
CONV_ROW_CHUNK = 64
CONV_LANE_CHUNK = 256
CONV_GRAD_UNROLL = 8
RMS_EPS = 1e-5
LN_EPS = 1e-5
ADAM_LR = 0.001
ADAM_B1 = 0.9
ADAM_B2 = 0.999
ADAM_EPS = 1e-08
ADAM_WD = 0.01
ADAM_STEP = 10

V7X_LANES = 128
V7X_SUBLANES = 8
V7X_VMEM_LIMIT_BYTES = 56 * 1024 * 1024

N_CHIPS = 4
N_DEV = 8
PACK_W = 1024

MASK_VALUE = -1e30


def _params(*semantics):
    return pltpu.CompilerParams(dimension_semantics=semantics, vmem_limit_bytes=V7X_VMEM_LIMIT_BYTES)


def _rows(tm, width):
    return pl.BlockSpec((tm, width), lambda i: (i, 0))


def _whole(shape):
    return pl.BlockSpec(shape, lambda *_: (0,) * len(shape))


def _rms_rstd(h):
    return lax.rsqrt(jnp.mean(h * h, axis=-1, keepdims=True) + RMS_EPS)


def _silu_and_grad(z):
    sg = jax.nn.sigmoid(z)
    return z * sg, sg * (1.0 + z * (1.0 - sg))


def _swap_rope_halves(t):
    w = t.shape[1]
    half = HEAD_DIM // 2
    lane = lax.broadcasted_iota(jnp.int32, t.shape, 1)
    upper = pltpu.roll(t, w - half, 1)
    lower = pltpu.roll(t, half, 1)
    return jnp.where(lane % HEAD_DIM < half, upper, lower)


def _rope(t, cos_ref, sin_ref):
    reps = t.shape[1] // V7X_LANES
    c = jnp.tile(cos_ref[...], (1, reps))
    s = jnp.tile(sin_ref[...], (1, reps))
    return t * c + _swap_rope_halves(t) * s


def _rope_transposed(dt, cos_ref, sin_ref):
    reps = dt.shape[1] // V7X_LANES
    c = jnp.tile(cos_ref[...], (1, reps))
    s = jnp.tile(sin_ref[...], (1, reps))
    return dt * c + _swap_rope_halves(dt * s)


def _rope_tables(seq_len):
    pos = jnp.arange(seq_len, dtype=F32)
    inv_freq = ROPE_THETA ** (-jnp.arange(0, HEAD_DIM, 2, dtype=F32) / HEAD_DIM)
    ang = pos[:, None] * jnp.tile(inv_freq, 2 * V7X_LANES // HEAD_DIM)[None, :]
    upper_half = jnp.arange(V7X_LANES) % HEAD_DIM >= HEAD_DIM // 2
    return jnp.cos(ang), jnp.where(upper_half[None, :], jnp.sin(ang), -jnp.sin(ang))


def _qkv_proj(h, g, w, b, cos, sin, after):
    T, D = h.shape
    N = w.shape[0]
    tm = min(512, T)
    rope_w = N - N_KV_HEADS * HEAD_DIM

    def body(h_ref, g_ref, w_ref, b_ref, cos_ref, sin_ref, _, y_ref, o_ref):
        hh = h_ref[...]
        y = (hh * _rms_rstd(hh) * g_ref[...]).astype(BF16)
        y_ref[...] = y
        acc = _dot_nt(y, w_ref[...]) + b_ref[...]
        o_ref[:, :rope_w] = _rope(acc[:, :rope_w], cos_ref, sin_ref).astype(BF16)
        o_ref[:, rope_w:] = acc[:, rope_w:].astype(BF16)

    return pl.pallas_call(
        body, name="qkv_proj", grid=(T // tm,),
        in_specs=[_rows(tm, D), _whole((1, D)), _whole((N, D)), _whole((1, N)),
                  _rows(tm, V7X_LANES), _rows(tm, V7X_LANES), pl.BlockSpec(memory_space=pl.ANY)],
        out_specs=[_rows(tm, D), _rows(tm, N)],
        out_shape=[SDS((T, D), BF16), SDS((T, N), BF16)],
        compiler_params=_params("parallel"),
    )(h, g, w, b, cos, sin, after)


def _pw1_proj(h, g, w, b):
    T, D = h.shape
    n = w.shape[2]
    N = N_CHIPS * n
    tm = min(512, T)

    def body(h_ref, g_ref, w_ref, b_ref, y_ref, o_ref):
        hh = h_ref[...]
        y = (hh * _rms_rstd(hh) * g_ref[...]).astype(BF16)
        y_ref[...] = y
        for j in range(N_CHIPS):
            cols = slice(j * n, (j + 1) * n)
            o_ref[:, cols] = jnp.dot(y, w_ref[j], preferred_element_type=F32) + b_ref[:, cols]

    return pl.pallas_call(
        body, name="pw1_proj", grid=(T // tm,),
        in_specs=[_rows(tm, D), _whole((1, D)), _whole((N_CHIPS, D, n)), _whole((1, N))],
        out_specs=[_rows(tm, D), _rows(tm, N)],
        out_shape=[SDS((T, D), BF16), SDS((T, N), F32)],
        compiler_params=_params("parallel"),
    )(h, g, w, b)


PAIRS_PER_KV = Q_PER_KV // 2


def _upper_lanes(shape):
    return lax.broadcasted_iota(jnp.int32, shape, 1) >= HEAD_DIM


def _swap_lane_halves(t):
    return pltpu.roll(t.astype(F32), HEAD_DIM, 1).astype(t.dtype)


def _kv_operands(g, t):
    swapped = _swap_lane_halves(t)
    in_lower, in_upper = (t, swapped) if g == 0 else (swapped, t)
    upper = _upper_lanes(t.shape)
    zero = jnp.zeros_like(t)
    return jnp.where(upper, zero, in_lower), jnp.where(upper, in_upper, zero)


def _group_heads(g):
    pairs = range(g * PAIRS_PER_KV, (g + 1) * PAIRS_PER_KV)
    return [2 * hp for hp in pairs] + [2 * hp + 1 for hp in pairs]


def _all_heads():
    return [h for g in range(N_KV_HEADS) for h in _group_heads(g)]


def _pair_rows(ref, g):
    pairs = range(g * PAIRS_PER_KV, (g + 1) * PAIRS_PER_KV)
    return jnp.concatenate([ref[:, hp * 2 * HEAD_DIM:(hp + 1) * 2 * HEAD_DIM] for hp in pairs], axis=0)


def _from_previous_block(rows):
    row = lax.broadcasted_iota(jnp.int32, (ATTN_BLOCK, ATTN_BLOCK), 0)
    col = lax.broadcasted_iota(jnp.int32, (ATTN_BLOCK, ATTN_BLOCK), 1)
    return jnp.concatenate([col > row] * (rows // ATTN_BLOCK), axis=0)


def _split_folded(t, prev_part):
    tb = t.astype(BF16)
    zero = jnp.zeros_like(tb)
    return jnp.where(prev_part, tb, zero), jnp.where(prev_part, zero, tb)


def _attn_specs(T):
    nb = T // ATTN_BLOCK
    kcol = N_Q_HEADS * HEAD_DIM // V7X_LANES
    cur = lambda n: jnp.minimum(n, nb - 1)
    prev = lambda n: jnp.maximum(jnp.minimum(n, nb - 1) - 1, 0)
    q_spec = pl.BlockSpec((ATTN_BLOCK, N_Q_HEADS * HEAD_DIM), lambda n: (cur(n), 0))
    kc_spec = pl.BlockSpec((ATTN_BLOCK, V7X_LANES), lambda n: (cur(n), kcol))
    kp_spec = pl.BlockSpec((ATTN_BLOCK, V7X_LANES), lambda n: (prev(n), kcol))
    vc_spec = pl.BlockSpec((ATTN_BLOCK, V7X_LANES), lambda n: (cur(n), kcol + 1))
    vp_spec = pl.BlockSpec((ATTN_BLOCK, V7X_LANES), lambda n: (prev(n), kcol + 1))
    return q_spec, kc_spec, kp_spec, vc_spec, vp_spec


def _attn_fwd(qkv, sinks):
    T = qkv.shape[0]
    nb = T // ATTN_BLOCK
    qw = N_Q_HEADS * HEAD_DIM
    all_rows = N_Q_HEADS * ATTN_BLOCK

    per_step = 4
    kcol = qw // V7X_LANES

    def body(q_ref, *refs):
        k_refs, v_refs = refs[:per_step + 1], refs[per_step + 1:2 * per_step + 2]
        sink_ref, o_ref, probs_ref, psink_ref = refs[2 * per_step + 2:]
        m = pl.program_id(0)
        half = PAIRS_PER_KV * ATTN_BLOCK
        groups = range(N_KV_HEADS)
        k_ops = [[_kv_operands(g, r[...]) for g in groups] for r in k_refs]
        v_ops = [[_kv_operands(g, r[...]) for g in groups] for r in v_refs]

        def scores(q, k):
            return lax.dot_general(q, k, (((1,), (1,)), ((), ())), preferred_element_type=F32)

        s_prev, s_cur = [], []
        for b in range(per_step):
            rows = slice(b * ATTN_BLOCK, (b + 1) * ATTN_BLOCK)
            for g in groups:
                pairs = range(g * PAIRS_PER_KV, (g + 1) * PAIRS_PER_KV)
                q = jnp.concatenate([q_ref[rows, hp * 2 * HEAD_DIM:(hp + 1) * 2 * HEAD_DIM] for hp in pairs], axis=0)
                for i in range(2):
                    from_prev = scores(q, k_ops[b][g][i])
                    if b == 0:
                        from_prev = jnp.where(m > 0, from_prev, MASK_VALUE * (HEAD_DIM ** 0.5))
                    s_prev.append(from_prev)
                    s_cur.append(scores(q, k_ops[b + 1][g][i]))
        prev_part = _from_previous_block(per_step * all_rows)
        s = jnp.where(prev_part, jnp.concatenate(s_prev, axis=0), jnp.concatenate(s_cur, axis=0)) * (HEAD_DIM ** -0.5)
        sink = jnp.concatenate([jnp.broadcast_to(sink_ref[0:1, h:h + 1], (ATTN_BLOCK, 1))
                                for h in _all_heads() * per_step], axis=0)
        top = jnp.maximum(jnp.max(s, axis=1, keepdims=True), sink)
        p = jnp.exp(s - top)
        e_sink = jnp.exp(sink - top)
        inv = 1.0 / (jnp.sum(p, axis=1, keepdims=True) + e_sink)
        probs, p_sink = p * inv, e_sink * inv
        p_prev, p_cur = _split_folded(probs, prev_part)
        lane = lax.broadcasted_iota(jnp.int32, (ATTN_BLOCK, V7X_LANES), 1)
        for b in range(per_step):
            rows = slice(b * ATTN_BLOCK, (b + 1) * ATTN_BLOCK)
            base = b * all_rows
            probs_ref[b] = probs[base:base + all_rows].astype(BF16)
            sink_tile = jnp.zeros((ATTN_BLOCK, V7X_LANES), F32)
            for i, h in enumerate(_all_heads()):
                sink_tile = jnp.where(lane == h, p_sink[base + i * ATTN_BLOCK:base + (i + 1) * ATTN_BLOCK], sink_tile)
            psink_ref[rows, :] = sink_tile
            for g in groups:
                even = slice(base + 2 * g * half, base + (2 * g + 1) * half)
                odd = slice(base + (2 * g + 1) * half, base + (2 * g + 2) * half)
                o = (jnp.dot(p_prev[even], v_ops[b][g][0], preferred_element_type=F32)
                     + jnp.dot(p_cur[even], v_ops[b + 1][g][0], preferred_element_type=F32)
                     + jnp.dot(p_prev[odd], v_ops[b][g][1], preferred_element_type=F32)
                     + jnp.dot(p_cur[odd], v_ops[b + 1][g][1], preferred_element_type=F32))
                for i in range(PAIRS_PER_KV):
                    hp = g * PAIRS_PER_KV + i
                    o_ref[rows, hp * 2 * HEAD_DIM:(hp + 1) * 2 * HEAD_DIM] = (
                        o[i * ATTN_BLOCK:(i + 1) * ATTN_BLOCK].astype(BF16))

    def kv_specs(col):
        return [pl.BlockSpec((ATTN_BLOCK, V7X_LANES),
                             functools.partial(lambda m, b: (jnp.maximum(per_step * m + b, 0), col), b=b))
                for b in range(-1, per_step)]

    return pl.pallas_call(
        body, name="attn_fwd", grid=(nb // per_step,),
        in_specs=[_rows(per_step * ATTN_BLOCK, qw), *kv_specs(kcol), *kv_specs(kcol + 1), _whole((1, N_Q_HEADS))],
        out_specs=[_rows(per_step * ATTN_BLOCK, qw),
                   pl.BlockSpec((per_step, all_rows, ATTN_BLOCK), lambda m: (m, 0, 0)),
                   _rows(per_step * ATTN_BLOCK, V7X_LANES)],
        out_shape=[SDS((T, qw), BF16), SDS((nb, all_rows, ATTN_BLOCK), BF16), SDS((T, V7X_LANES), F32)],
        compiler_params=_params("parallel"),
    )(*[qkv] * (2 * per_step + 3), sinks)


def _mm_res(name, a, w, b, res, g):
    T, K = a.shape
    D = w.shape[1]
    tm = min(512, T)

    def body(a_ref, w_ref, b_ref, r_ref, g_ref, o_ref, f_ref):
        h = jnp.dot(a_ref[...], w_ref[...], preferred_element_type=F32) + b_ref[...] + r_ref[...]
        o_ref[...] = h
        f_ref[...] = (h * _rms_rstd(h) * g_ref[...]).astype(BF16)

    return pl.pallas_call(
        body, name=name, grid=(T // tm,),
        in_specs=[_rows(tm, K), _whole((K, D)), _whole((1, D)), _rows(tm, D), _whole((1, D))],
        out_specs=[_rows(tm, D), _rows(tm, D)],
        out_shape=[SDS((T, D), F32), SDS((T, D), BF16)],
        compiler_params=_params("parallel"),
    )(a, w, b, res, g)


def _ffn_down(name, s, w2, layer, res):
    _, T, n = s.shape
    D = w2.shape[3]
    tm = min(512, T)

    def body(s_ref, w_ref, r_ref, o_ref):
        acc = r_ref[...]
        for j in range(N_CHIPS):
            acc = acc + jnp.dot(s_ref[j], w_ref[j], preferred_element_type=F32)
        o_ref[...] = acc

    return pl.pallas_call(
        body, name=name, grid=(T // tm,),
        in_specs=[pl.BlockSpec((N_CHIPS, tm, n), lambda i: (0, i, 0)),
                  pl.BlockSpec((N_CHIPS, None, n, D), lambda i: (0, layer, 0, 0)), _rows(tm, D)],
        out_specs=_rows(tm, D),
        out_shape=SDS((T, D), F32),
        compiler_params=_params("parallel"),
    )(s, w2, res)


def _ffn_up(name, f, w1, w3, layer, after=None):
    T, D = f.shape
    n = w1.shape[2]
    tm = min(1024, T)

    def body(f_ref, w1_ref, w3_ref, *rest):
        act_ref, gg_ref, s_ref = rest[-3:]
        ff = f_ref[...]
        g1 = _dot_nt(ff, w1_ref[...])
        g3 = _dot_nt(ff, w3_ref[...])
        act, dact = _silu_and_grad(g1)
        act_ref[...] = act.astype(BF16)
        gg_ref[...] = (g3 * dact).astype(BF16)
        s_ref[...] = (act * g3).astype(BF16)

    slab = pl.BlockSpec((None, tm, n), lambda j, i: (j, i, 0))
    wslab = pl.BlockSpec((None, None, n, D), lambda j, i: (j, layer, 0, 0))
    hidden = SDS((N_CHIPS, T, n), BF16)
    return pl.pallas_call(
        body, name=name, grid=(N_CHIPS, T // tm),
        in_specs=[pl.BlockSpec((tm, D), lambda j, i: (i, 0)), wslab, wslab]
        + ([] if after is None else [pl.BlockSpec(memory_space=pl.ANY)]),
        out_specs=[slab, slab, slab],
        out_shape=[hidden, hidden, hidden],
        compiler_params=_params("parallel", "parallel"),
    )(f, w1, w3, *([] if after is None else [after]))


def _glu(a, d):
    return a[:, :d] * jax.nn.sigmoid(a[:, d:])


def _conv_tile(T):
    return min(256, T)


def _fill_shifted(sh_ref, tc):
    n = tc + CONV_HALO - V7X_SUBLANES
    for r in range(1, V7X_SUBLANES):
        sh_ref[r, 0:n, :] = sh_ref[0, pl.ds(r, n), :]


def _depthwise_taps(sh_ref, w_ref, offsets, bias_ref, out_ref, tc):
    D = out_ref.shape[1]

    def chunk(i, carry):
        t0 = pl.multiple_of(i * CONV_ROW_CHUNK, CONV_ROW_CHUNK)
        for cb in range(D // CONV_LANE_CHUNK):
            cs = slice(cb * CONV_LANE_CHUNK, (cb + 1) * CONV_LANE_CHUNK)
            acc = jnp.zeros((CONV_ROW_CHUNK, CONV_LANE_CHUNK), F32)
            for r in range(V7X_SUBLANES):
                taps = [(j, o // V7X_SUBLANES) for j, o in enumerate(offsets) if o % V7X_SUBLANES == r]
                if not taps:
                    continue
                span = CONV_ROW_CHUNK + V7X_SUBLANES * max(q for _, q in taps)
                rows = sh_ref[r, pl.ds(t0, span), cs]
                for j, q in taps:
                    acc = acc + rows[V7X_SUBLANES * q:V7X_SUBLANES * q + CONV_ROW_CHUNK] * w_ref[j:j + 1, cs]
            if bias_ref is not None:
                acc = acc + bias_ref[:, cs]
            out_ref[pl.ds(t0, CONV_ROW_CHUNK), cs] = acc
        return carry

    lax.fori_loop(0, tc // CONV_ROW_CHUNK, chunk, 0)


def _depthwise_tap_grads(dy_sh, x_sh, offsets, dw_ref, tc):
    D = dw_ref.shape[1]
    for cb in range(D // V7X_LANES):
        cs = slice(cb * V7X_LANES, (cb + 1) * V7X_LANES)

        def row_tiles(i, accs, cs=cs):
            for k in range(CONV_GRAD_UNROLL):
                t0 = pl.multiple_of(i * (CONV_GRAD_UNROLL * V7X_SUBLANES), V7X_SUBLANES) + k * V7X_SUBLANES
                d = dy_sh[0, pl.ds(t0, V7X_SUBLANES), cs]
                accs = tuple(
                    acc + d * x_sh[o % V7X_SUBLANES, pl.ds(t0 + o // V7X_SUBLANES * V7X_SUBLANES, V7X_SUBLANES), cs]
                    for acc, o in zip(accs, offsets))
            return accs

        zero = jnp.zeros((V7X_SUBLANES, V7X_LANES), F32)
        accs = lax.fori_loop(0, tc // (CONV_GRAD_UNROLL * V7X_SUBLANES), row_tiles, tuple(zero for _ in offsets))
        for j, acc in enumerate(accs):
            dw_ref[j:j + 1, cs] += jnp.sum(acc, axis=0, keepdims=True)


def _conv_fwd(a, w_dw, b_dw, ln_g, ln_b, w_pw2, b_pw2, res, g_next):
    T = a.shape[0]
    D = a.shape[1] // 2
    tc = _conv_tile(T)
    per = tc // CONV_HALO

    def body(a_ref, ah_ref, w_ref, bdw_ref, lg_ref, lb_ref, wp_ref, bp_ref, r_ref, g_ref,
             c_ref, act_ref, h_ref, f_ref, u_sh):
        i = pl.program_id(0)
        u_sh[0, 0:CONV_HALO, :] = jnp.where(i > 0, _glu(ah_ref[...], D), 0.0)
        u_sh[0, CONV_HALO:, :] = _glu(a_ref[...], D)
        _fill_shifted(u_sh, tc)
        _depthwise_taps(u_sh, w_ref, [CONV_FIRST_TAP + j for j in range(CONV_WIDTH)], bdw_ref, c_ref, tc)
        c = c_ref[...]
        xc = c - jnp.mean(c, axis=-1, keepdims=True)
        z = xc * lax.rsqrt(jnp.mean(xc * xc, axis=-1, keepdims=True) + LN_EPS)
        l = z * lg_ref[...] + lb_ref[...]
        act = (l * jax.nn.sigmoid(l)).astype(BF16)
        act_ref[...] = act
        h = jnp.dot(act, wp_ref[...], preferred_element_type=F32) + bp_ref[...] + r_ref[...]
        h_ref[...] = h
        f_ref[...] = (h * _rms_rstd(h) * g_ref[...]).astype(BF16)

    return pl.pallas_call(
        body, name="conv_fwd", grid=(T // tc,),
        in_specs=[_rows(tc, 2 * D),
                  pl.BlockSpec((CONV_HALO, 2 * D), lambda i: (jnp.maximum(i * per - 1, 0), 0)),
                  _whole((CONV_WIDTH, D)), _whole((1, D)), _whole((1, D)), _whole((1, D)),
                  _whole((D, D)), _whole((1, D)), _rows(tc, D), _whole((1, D))],
        out_specs=[_rows(tc, D), _rows(tc, D), _rows(tc, D), _rows(tc, D)],
        out_shape=[SDS((T, D), F32), SDS((T, D), BF16), SDS((T, D), F32), SDS((T, D), BF16)],
        scratch_shapes=[pltpu.VMEM((V7X_SUBLANES, tc + CONV_HALO, D), F32)],
        compiler_params=_params("parallel"),
    )(a, a, w_dw, b_dw, ln_g, ln_b, w_pw2, b_pw2, res, g_next)


def _ffn_down_loss(name, s, w2, layer, res, g, target):
    _, T, n = s.shape
    D = w2.shape[3]
    tm = min(512, T)

    def body(s_ref, w_ref, r_ref, g_ref, t_ref, dh_ref, loss_ref, dg_ref):
        @pl.when(pl.program_id(0) == 0)
        def _():
            loss_ref[...] = jnp.zeros_like(loss_ref)
            dg_ref[...] = jnp.zeros_like(dg_ref)

        hh = r_ref[...]
        for j in range(N_CHIPS):
            hh = hh + jnp.dot(s_ref[j], w_ref[j], preferred_element_type=F32)
        r = _rms_rstd(hh)
        g = g_ref[...]
        d = hh * r * g - t_ref[...]
        loss_ref[...] += 0.5 * jnp.sum(jnp.mean(d * d, axis=-1, keepdims=True), axis=0, keepdims=True)
        dout = d * (1.0 / D)
        dg_ref[...] += jnp.sum(dout * (hh * r), axis=0, keepdims=True)
        dxh = dout * g
        dh_ref[...] = r * dxh - hh * (r * r * r) * jnp.mean(dxh * hh, axis=-1, keepdims=True)

    return pl.pallas_call(
        body, name=name, grid=(T // tm,),
        in_specs=[pl.BlockSpec((N_CHIPS, tm, n), lambda i: (0, i, 0)),
                  pl.BlockSpec((N_CHIPS, None, n, D), lambda i: (0, layer, 0, 0)), _rows(tm, D),
                  _whole((1, D)), _rows(tm, D)],
        out_specs=[_rows(tm, D), _whole((1, 1)), _whole((1, D))],
        out_shape=[SDS((T, D), F32), SDS((1, 1), F32), SDS((1, D), F32)],
        compiler_params=_params("arbitrary"),
    )(s, w2, res, g, target)


def _ffn_bwd_down(name, dh, w2, layer, act, gate_grad, s, after=None):
    T, D = dh.shape
    n = w2.shape[2]
    tm = min(256, T)

    def body(dh_ref, w2_ref, act_ref, gg_ref, s_ref, *rest):
        dg1_ref, dg3_ref, dw_ref = rest[-3:]

        @pl.when(pl.program_id(0) == 0)
        def _():
            dw_ref[...] = jnp.zeros_like(dw_ref)

        dhb = dh_ref[...].astype(BF16)
        for j in range(N_CHIPS):
            ds = _dot_nt(dhb, w2_ref[j])
            dg1_ref[j] = (ds * gg_ref[j].astype(F32)).astype(BF16)
            dg3_ref[j] = (ds * act_ref[j].astype(F32)).astype(BF16)
            dw_ref[j] += _dot_tn(s_ref[j], dhb)

    slabs = pl.BlockSpec((N_CHIPS, tm, n), lambda i: (0, i, 0))
    hidden = SDS((N_CHIPS, T, n), BF16)
    return pl.pallas_call(
        body, name=name, grid=(T // tm,),
        in_specs=[_rows(tm, D),
                  pl.BlockSpec((N_CHIPS, None, n, D), lambda i: (0, layer, 0, 0), pipeline_mode=pl.Buffered(1)),
                  slabs, slabs, slabs] + ([] if after is None else [pl.BlockSpec(memory_space=pl.ANY)]),
        out_specs=[slabs, slabs, _whole((N_CHIPS, n, D))],
        out_shape=[hidden, hidden, SDS((N_CHIPS, n, D), F32)],
        compiler_params=_params("arbitrary"),
    )(dh, w2, act, gate_grad, s, *([] if after is None else [after]))


def _dot_tn(a, b):
    return lax.dot_general(a.astype(BF16), b.astype(BF16), (((0,), (0,)), ((), ())), preferred_element_type=F32)


def _dot_nt(a, b):
    return lax.dot_general(a.astype(BF16), b, (((1,), (1,)), ((), ())), preferred_element_type=F32)


def _mm_tn(name, a, b, col_chunks=1, after=None):
    a_slabs, b_slabs = a.ndim == 3, b.ndim == 3
    T = a.shape[-2]
    tt = min(1024, T)
    ka, nb = a.shape[-1], b.shape[-1]
    if a_slabs or b_slabs:
        out_dims = (N_CHIPS, ka, nb)
    elif col_chunks > 1:
        out_dims = (col_chunks, ka, nb // col_chunks)
    else:
        out_dims = (ka, nb)

    def body(a_ref, b_ref, *rest):
        o_ref = rest[-1]

        @pl.when(pl.program_id(0) == 0)
        def _():
            o_ref[...] = jnp.zeros_like(o_ref)

        if a_slabs:
            bb = b_ref[...].astype(BF16)
            for j in range(N_CHIPS):
                o_ref[j] += _dot_tn(a_ref[j], bb)
        elif b_slabs:
            aa = a_ref[...].astype(BF16)
            for j in range(N_CHIPS):
                o_ref[j] += _dot_tn(aa, b_ref[j])
        elif col_chunks > 1:
            aa = a_ref[...].astype(BF16)
            w = nb // col_chunks
            for j in range(col_chunks):
                o_ref[j] += _dot_tn(aa, b_ref[:, j * w:(j + 1) * w])
        else:
            o_ref[...] += _dot_tn(a_ref[...], b_ref[...])

    def spec(arr, slabs):
        if slabs:
            return pl.BlockSpec((N_CHIPS, tt, arr.shape[-1]), lambda t: (0, t, 0))
        return _rows(tt, arr.shape[-1])

    return pl.pallas_call(
        body, name=name, grid=(T // tt,),
        in_specs=[spec(a, a_slabs), spec(b, b_slabs)] + ([] if after is None else [pl.BlockSpec(memory_space=pl.ANY)]),
        out_specs=_whole(out_dims),
        out_shape=SDS(out_dims, F32),
        compiler_params=_params("arbitrary"),
    )(a, b, *([] if after is None else [after]))


def _mm_nt_normbwd(name, pairs, h, g, dh, after):
    T, D = h.shape
    tm = min(512, T)
    n_pairs = len(pairs)
    kinds = ["slabs" if dy.ndim == 3 else ("quarters" if w.ndim == 3 else "plain") for dy, w, _ in pairs]

    def body(*refs):
        dy_refs = refs[:n_pairs]
        w_refs = refs[n_pairs:2 * n_pairs]
        h_ref, g_ref, dh_ref, _, o_ref, dg_ref, cs_ref = refs[2 * n_pairs:]

        @pl.when(pl.program_id(0) == 0)
        def _():
            dg_ref[...] = jnp.zeros_like(dg_ref)
            cs_ref[...] = jnp.zeros_like(cs_ref)

        df = jnp.zeros((tm, D), F32)
        for dy_ref, w_ref, kd in zip(dy_refs, w_refs, kinds):
            if kd == "slabs":
                for j in range(N_CHIPS):
                    df = df + jnp.dot(dy_ref[j], w_ref[j], preferred_element_type=F32)
            elif kd == "quarters":
                n = w_ref.shape[2]
                for j in range(N_CHIPS):
                    df = df + _dot_nt(dy_ref[:, j * n:(j + 1) * n], w_ref[j])
            else:
                df = df + jnp.dot(dy_ref[...], w_ref[...], preferred_element_type=F32)
        hh = h_ref[...]
        r = _rms_rstd(hh)
        dg_ref[...] += jnp.sum(df * (hh * r), axis=0, keepdims=True)
        dxh = df * g_ref[...]
        out = dh_ref[...] + (r * dxh - hh * (r * r * r) * jnp.mean(dxh * hh, axis=-1, keepdims=True))
        o_ref[...] = out
        cs_ref[...] += jnp.sum(out, axis=0, keepdims=True)

    dy_specs, w_specs = [], []
    for (dy, w, layer), kd in zip(pairs, kinds):
        if kd == "slabs":
            dy_specs.append(pl.BlockSpec((N_CHIPS, tm, dy.shape[2]), lambda i: (0, i, 0)))
            w_specs.append(pl.BlockSpec((N_CHIPS, None, w.shape[2], D),
                                        functools.partial(lambda i, layer: (0, layer, 0, 0), layer=layer),
                                        pipeline_mode=pl.Buffered(1)))
        else:
            dy_specs.append(_rows(tm, dy.shape[1]))
            w_specs.append(_whole(w.shape))

    return pl.pallas_call(
        body, name=name, grid=(T // tm,),
        in_specs=[*dy_specs, *w_specs, _rows(tm, D), _whole((1, D)), _rows(tm, D), pl.BlockSpec(memory_space=pl.ANY)],
        out_specs=[_rows(tm, D), _whole((1, D)), _whole((1, D))],
        out_shape=[SDS((T, D), F32), SDS((1, D), F32), SDS((1, D), F32)],
        compiler_params=_params("arbitrary"),
    )(*[dy for dy, _, _ in pairs], *[w for _, w, _ in pairs], h, g, dh, after)


def _mm_nt(name, dy, w, out_dtype):
    T, N = dy.shape
    K = w.shape[0]
    tm = min(512, T)

    def body(dy_ref, w_ref, o_ref):
        o_ref[...] = lax.dot_general(dy_ref[...].astype(BF16), w_ref[...], (((1,), (1,)), ((), ())),
                                     preferred_element_type=F32).astype(out_dtype)

    return pl.pallas_call(
        body, name=name, grid=(T // tm,),
        in_specs=[_rows(tm, N), _whole((K, N))],
        out_specs=_rows(tm, K),
        out_shape=SDS((T, K), out_dtype),
        compiler_params=_params("parallel"),
    )(dy, w)


def _conv_bwd(dh, w_pw2, c, a, w_dw, ln_g, ln_b):
    T, D = c.shape
    tc = _conv_tile(T)
    per = tc // CONV_HALO
    n_tiles = T // tc
    last_halo = T // CONV_HALO - 1

    def ln_bwd(dact_v, c_v, lg, lb):
        xc = c_v - jnp.mean(c_v, axis=-1, keepdims=True)
        rstd = lax.rsqrt(jnp.mean(xc * xc, axis=-1, keepdims=True) + LN_EPS)
        z = xc * rstd
        _, dsilu = _silu_and_grad(z * lg + lb)
        dl = dact_v * dsilu
        dz = dl * lg
        dc = rstd * (dz - jnp.mean(dz, axis=-1, keepdims=True) - z * jnp.mean(dz * z, axis=-1, keepdims=True))
        return dc, dl, z

    def body(dh_ref, dhn_ref, wp_ref, c_ref, cn_ref, a_ref, ah_ref, w_ref, lg_ref, lb_ref,
             da_ref, dlg_ref, dlb_ref, dbdw_ref, dwdw_ref, dbpw1_ref, dc_sh, u_sh, du_scr):
        i = pl.program_id(0)

        @pl.when(i == 0)
        def _():
            for ref in (dlg_ref, dlb_ref, dbdw_ref, dwdw_ref, dbpw1_ref):
                ref[...] = jnp.zeros_like(ref)

        lg, lb = lg_ref[...], lb_ref[...]
        dh_rows = jnp.concatenate([dh_ref[...].astype(BF16), dhn_ref[...].astype(BF16)], axis=0)
        dact = _dot_nt(dh_rows, wp_ref[...])
        dc, dl, z = ln_bwd(dact[:tc], c_ref[...], lg, lb)
        dlg_ref[...] += jnp.sum(dl * z, axis=0, keepdims=True)
        dlb_ref[...] += jnp.sum(dl, axis=0, keepdims=True)
        dbdw_ref[...] += jnp.sum(dc, axis=0, keepdims=True)
        dcn, _, _ = ln_bwd(dact[tc:], cn_ref[...], lg, lb)
        dc_sh[0, 0:tc, :] = dc
        dc_sh[0, tc:, :] = jnp.where(i < n_tiles - 1, dcn, 0.0)
        _fill_shifted(dc_sh, tc)

        a_v = a_ref[...]
        a1 = a_v[:, :D]
        sg = jax.nn.sigmoid(a_v[:, D:])
        u_sh[0, 0:CONV_HALO, :] = jnp.where(i > 0, _glu(ah_ref[...], D), 0.0)
        u_sh[0, CONV_HALO:, :] = a1 * sg
        _fill_shifted(u_sh, tc)

        _depthwise_taps(dc_sh, w_ref, [CONV_WIDTH - 1 - j for j in range(CONV_WIDTH)], None, du_scr, tc)
        _depthwise_tap_grads(dc_sh, u_sh, [CONV_FIRST_TAP + j for j in range(CONV_WIDTH)], dwdw_ref, tc)

        du = du_scr[...]
        da1 = du * sg
        da2 = du * a1 * sg * (1.0 - sg)
        da_ref[:, :D] = da1.astype(BF16)
        da_ref[:, D:] = da2.astype(BF16)
        dbpw1_ref[:, :D] += jnp.sum(da1, axis=0, keepdims=True)
        dbpw1_ref[:, D:] += jnp.sum(da2, axis=0, keepdims=True)

    nxt = lambda i: (jnp.minimum((i + 1) * per, last_halo), 0)
    return pl.pallas_call(
        body, name="conv_bwd", grid=(n_tiles,),
        in_specs=[_rows(tc, D), pl.BlockSpec((CONV_HALO, D), nxt), _whole((D, D)),
                  _rows(tc, D), pl.BlockSpec((CONV_HALO, D), nxt),
                  _rows(tc, 2 * D),
                  pl.BlockSpec((CONV_HALO, 2 * D), lambda i: (jnp.maximum(i * per - 1, 0), 0)),
                  _whole((CONV_WIDTH, D)), _whole((1, D)), _whole((1, D))],
        out_specs=[_rows(tc, 2 * D), _whole((1, D)), _whole((1, D)), _whole((1, D)),
                   _whole((CONV_HALO, D)), _whole((1, 2 * D))],
        out_shape=[SDS((T, 2 * D), BF16), SDS((1, D), F32), SDS((1, D), F32), SDS((1, D), F32),
                   SDS((CONV_HALO, D), F32), SDS((1, 2 * D), F32)],
        scratch_shapes=[pltpu.VMEM((V7X_SUBLANES, tc + CONV_HALO, D), F32),
                        pltpu.VMEM((V7X_SUBLANES, tc + CONV_HALO, D), F32), pltpu.VMEM((tc, D), F32)],
        compiler_params=_params("arbitrary"),
    )(dh, dh, w_pw2, c, c, a, a, w_dw, ln_g, ln_b)


def _attn_bwd(qkv, dao, cos, sin, probs_saved, psink_saved):
    T = qkv.shape[0]
    nb = T // ATTN_BLOCK
    qw = N_Q_HEADS * HEAD_DIM
    kw = N_KV_HEADS * HEAD_DIM

    def body(q_ref, kc_ref, kp_ref, vc_ref, vp_ref, do_ref, cos_ref, sin_ref, cosp_ref, sinp_ref, probs_ref, psink_ref,
             dq_ref, dkv_ref, dsink_ref, dbq_ref, dbkv_ref, carry, prev_scr, cur_scr, dq_scr):
        n = pl.program_id(0)

        @pl.when(n == 0)
        def _():
            for ref in (dsink_ref, dbq_ref, dbkv_ref, carry):
                ref[...] = jnp.zeros_like(ref)

        @pl.when(n == nb)
        def _():
            prev_scr[...] = jnp.zeros_like(prev_scr)

        @pl.when(n < nb)
        def _():
            prev_part = _from_previous_block(N_Q_HEADS * ATTN_BLOCK)
            half = PAIRS_PER_KV * ATTN_BLOCK
            upper = _upper_lanes((ATTN_BLOCK, 2 * HEAD_DIM))
            groups = range(N_KV_HEADS)

            def nt(a, b):
                return lax.dot_general(a, b, (((1,), (1,)), ((), ())), preferred_element_type=F32)

            def kv_grad(even_rows, odd_rows, x):
                even = lax.dot_general(even_rows, x, (((0,), (0,)), ((), ())), preferred_element_type=F32)
                odd = lax.dot_general(odd_rows, x, (((0,), (0,)), ((), ())), preferred_element_type=F32)
                t = jnp.where(upper, odd, even)
                return t + _swap_lane_halves(t)

            q = [_pair_rows(q_ref, g) for g in groups]
            do = [_pair_rows(do_ref, g) for g in groups]
            k_prev = [_kv_operands(g, kp_ref[...]) for g in groups]
            k_cur = [_kv_operands(g, kc_ref[...]) for g in groups]
            v_prev = [_kv_operands(g, vp_ref[...]) for g in groups]
            v_cur = [_kv_operands(g, vc_ref[...]) for g in groups]
            probs = probs_ref[...].astype(F32)
            dp_prev = jnp.concatenate([nt(do[g], v_prev[g][i]) for g in groups for i in range(2)], axis=0)
            dp_cur = jnp.concatenate([nt(do[g], v_cur[g][i]) for g in groups for i in range(2)], axis=0)
            dp = jnp.where(prev_part, dp_prev, dp_cur)
            delta = jnp.sum(probs * dp, axis=1, keepdims=True)
            ds_prev, ds_cur = _split_folded(probs * (dp - delta) * (HEAD_DIM ** -0.5), prev_part)
            p_prev, p_cur = _split_folded(probs_ref[...], prev_part)
            for i, h in enumerate(_all_heads()):
                rows = slice(i * ATTN_BLOCK, (i + 1) * ATTN_BLOCK)
                dsink_ref[:, h:h + 1] += jnp.sum(-(psink_ref[:, h:h + 1] * delta[rows]), axis=0, keepdims=True)
            kv_grads = []
            for g in groups:
                even, odd = slice(2 * g * half, (2 * g + 1) * half), slice((2 * g + 1) * half, (2 * g + 2) * half)
                dq = (jnp.dot(ds_prev[even], k_prev[g][0], preferred_element_type=F32)
                      + jnp.dot(ds_cur[even], k_cur[g][0], preferred_element_type=F32)
                      + jnp.dot(ds_prev[odd], k_prev[g][1], preferred_element_type=F32)
                      + jnp.dot(ds_cur[odd], k_cur[g][1], preferred_element_type=F32))
                for i in range(PAIRS_PER_KV):
                    hp = g * PAIRS_PER_KV + i
                    dq_scr[:, hp * 2 * HEAD_DIM:(hp + 1) * 2 * HEAD_DIM] = dq[i * ATTN_BLOCK:(i + 1) * ATTN_BLOCK]
                kv_grads.append((kv_grad(ds_prev[even], ds_prev[odd], q[g]), kv_grad(ds_cur[even], ds_cur[odd], q[g]),
                                 kv_grad(p_prev[even], p_prev[odd], do[g]), kv_grad(p_cur[even], p_cur[odd], do[g])))
            (dkp0, dkc0, dvp0, dvc0), (dkp1, dkc1, dvp1, dvc1) = kv_grads
            prev_scr[:, :kw] = jnp.where(upper, dkp1, dkp0)
            prev_scr[:, kw:] = jnp.where(upper, dvp1, dvp0)
            cur_scr[:, :kw] = jnp.where(upper, dkc1, dkc0)
            cur_scr[:, kw:] = jnp.where(upper, dvc1, dvc0)
            dq_pre = _rope_transposed(dq_scr[...], cos_ref, sin_ref)
            dq_ref[...] = dq_pre.astype(BF16)
            dbq_ref[...] += jnp.sum(dq_pre, axis=0, keepdims=True)

        tot = carry[...] + prev_scr[...]
        dk_pre = _rope_transposed(tot[:, :kw], cosp_ref, sinp_ref)
        dkv_ref[:, :kw] = dk_pre.astype(BF16)
        dkv_ref[:, kw:] = tot[:, kw:].astype(BF16)
        dbkv_ref[:, :kw] += jnp.sum(dk_pre, axis=0, keepdims=True)
        dbkv_ref[:, kw:] += jnp.sum(tot[:, kw:], axis=0, keepdims=True)

        @pl.when(n < nb)
        def _():
            carry[...] = cur_scr[...]

    cur = lambda n: (jnp.minimum(n, nb - 1), 0)
    out_lag = lambda n: (jnp.maximum(n - 1, 0), 0)
    return pl.pallas_call(
        body, name="attn_bwd", grid=(nb + 1,),
        in_specs=[*_attn_specs(T),
                  pl.BlockSpec((ATTN_BLOCK, qw), cur),
                  pl.BlockSpec((ATTN_BLOCK, V7X_LANES), cur), pl.BlockSpec((ATTN_BLOCK, V7X_LANES), cur),
                  pl.BlockSpec((ATTN_BLOCK, V7X_LANES), out_lag), pl.BlockSpec((ATTN_BLOCK, V7X_LANES), out_lag),
                  pl.BlockSpec((None, N_Q_HEADS * ATTN_BLOCK, ATTN_BLOCK), lambda n: (jnp.minimum(n, nb - 1), 0, 0)),
                  pl.BlockSpec((ATTN_BLOCK, V7X_LANES), cur)],
        out_specs=[pl.BlockSpec((ATTN_BLOCK, qw), cur), pl.BlockSpec((ATTN_BLOCK, 2 * kw), out_lag),
                   _whole((1, N_Q_HEADS)), _whole((1, qw)), _whole((1, 2 * kw))],
        out_shape=[SDS((T, qw), BF16), SDS((T, 2 * kw), BF16),
                   SDS((1, N_Q_HEADS), F32), SDS((1, qw), F32), SDS((1, 2 * kw), F32)],
        scratch_shapes=[pltpu.VMEM((ATTN_BLOCK, 2 * kw), F32), pltpu.VMEM((ATTN_BLOCK, 2 * kw), F32),
                        pltpu.VMEM((ATTN_BLOCK, 2 * kw), F32), pltpu.VMEM((ATTN_BLOCK, qw), F32)],
        compiler_params=_params("arbitrary"),
    )(qkv, qkv, qkv, qkv, qkv, dao, cos, sin, cos, sin, probs_saved, psink_saved)


def _local_step(x, target, p, reduce_begin, reduce_send):
    T, D = x.shape
    cos, sin = _rope_tables(T)
    qw = N_Q_HEADS * HEAD_DIM
    nm, nf = p["norm_mix"], p["norm_ffn"]

    y0, qkv = _qkv_proj(x, nm[0:1], p["attn_w_qkv"], p["attn_b_qkv"], cos, sin, p["gather_started"])
    ao, attn_probs, sink_probs = _attn_fwd(qkv, p["attn_sinks"])
    h1, f0 = _mm_res("attn_out", ao, p["attn_w_o"], p["attn_b_o"], x, nf[0:1])
    p = {**p, **p["other_weights"](h1)}
    w1, w3 = p["ffn_w1"], p["ffn_w3"]
    act0, gg0, s0 = _ffn_up("ffn0_up", f0, w1, w3, 0, after=p["swap_started"])
    p = {**p, **p["rest_of_weights"](s0)}
    w2 = p["ffn_w2"]
    h2 = _ffn_down("ffn0_down", s0, w2, 0, h1)
    y1, a = _pw1_proj(h2, nm[1:2], p["conv_w_pw1"], p["conv_b_pw1"])
    c, act, h3, f1 = _conv_fwd(a, p["conv_w_dw"], p["conv_b_dw"], p["conv_ln_g"], p["conv_ln_b"],
                               p["conv_w_pw2"], p["conv_b_pw2"], h2, nf[1:2])
    act1, gg1, s1 = _ffn_up("ffn1_up", f1, w1, w3, 1)
    dh4, loss, d_norm_final = _ffn_down_loss("ffn1_down_loss", s1, w2, 1, h3, p["norm_final"], target)

    g = {}
    dg1, dg3, dw2_1 = _ffn_bwd_down("ffn1_bwd_down", dh4, w2, 1, act1, gg1, s1)
    dw1_1 = _mm_tn("ffn1_dw1", dg1, f1)
    dw3_1 = _mm_tn("ffn1_dw3", dg3, f1)
    begun = reduce_begin("ffn1", {("ffn_w1", 1): dw1_1, ("ffn_w3", 1): dw3_1, ("ffn_w2", 1): dw2_1})
    dh3, dnf1, db_pw2 = _mm_nt_normbwd("ffn1_bwd_in", [(dg1, w1, 1), (dg3, w3, 1)], h3, nf[1:2], dh4, begun)
    sent = reduce_send("ffn1", dh3)

    dw_pw2 = _mm_tn("conv_dw_pw2", act, dh3, after=sent)
    da, d_ln_g, d_ln_b, d_b_dw, d_w_dw, d_b_pw1 = _conv_bwd(dh3, p["conv_w_pw2"], c, a, p["conv_w_dw"],
                                                            p["conv_ln_g"], p["conv_ln_b"])
    dw_pw1 = _mm_tn("conv_dw_pw1", y1, da, col_chunks=N_CHIPS)
    begun = reduce_begin("conv", {("conv_w_pw2", 0): dw_pw2.reshape(N_CHIPS, -1, D), ("conv_w_pw1", 0): dw_pw1})
    dh2, dnm1, _ = _mm_nt_normbwd("conv_bwd_in", [(da, p["conv_w_pw1"], None)], h2, nm[1:2], dh3, begun)
    sent = reduce_send("conv", dh2)

    dg1, dg3, dw2_0 = _ffn_bwd_down("ffn0_bwd_down", dh2, w2, 0, act0, gg0, s0, after=sent)
    dw1_0 = _mm_tn("ffn0_dw1", dg1, f0)
    dw3_0 = _mm_tn("ffn0_dw3", dg3, f0)
    begun = reduce_begin("ffn0", {("ffn_w1", 0): dw1_0, ("ffn_w3", 0): dw3_0, ("ffn_w2", 0): dw2_0})
    dh1, dnf0, db_o = _mm_nt_normbwd("ffn0_bwd_in", [(dg1, w1, 0), (dg3, w3, 0)], h1, nf[0:1], dh2, begun)
    sent = reduce_send("ffn0", dh1)

    dw_o = _mm_tn("attn_dw_o", ao, dh1, after=sent)
    dao = _mm_nt("attn_bwd_out", dh1, p["attn_w_o"], BF16)
    dq, dkv, d_sinks, dbq, dbkv = _attn_bwd(qkv, dao, cos, sin, attn_probs, sink_probs)
    dwq = _mm_tn("attn_dw_q", dq, y0)
    dwkv = _mm_tn("attn_dw_kv", dkv, y0)
    wqkv = p["attn_w_qkv"]
    dwqkv = jnp.concatenate([dwq, dwkv], axis=0).reshape(N_CHIPS, -1, D)
    begun = reduce_begin("attn", {("attn_w_o", 0): dw_o.reshape(N_CHIPS, -1, D), ("attn_w_qkv", 0): dwqkv})
    sent = reduce_send("attn", begun)
    dx, dnm0, _ = _mm_nt_normbwd("attn_bwd_in", [(dq, wqkv[:qw], None), (dkv, wqkv[qw:], None)], x, nm[0:1], dh1,
                                 sent)

    g["norm_mix"] = jnp.concatenate([dnm0, dnm1], axis=0)
    g["norm_ffn"] = jnp.concatenate([dnf0, dnf1], axis=0)
    g["attn_b_qkv"] = jnp.concatenate([dbq, dbkv], axis=1)
    g["attn_sinks"] = d_sinks
    g["attn_b_o"] = db_o
    g["conv_b_pw1"] = d_b_pw1
    g["conv_w_dw"] = d_w_dw[:CONV_WIDTH]
    g["conv_b_dw"] = d_b_dw
    g["conv_ln_g"] = d_ln_g
    g["conv_ln_b"] = d_ln_b
    g["conv_b_pw2"] = db_pw2
    g["norm_final"] = d_norm_final
    return loss, dx, g


ANY = pl.BlockSpec(memory_space=pl.ANY)
VMEM_WHOLE = pl.BlockSpec(memory_space=pltpu.VMEM)


def _my_place():
    return lax.axis_index("x"), lax.axis_index("y"), lax.axis_index("c")


def _other_chips(x, y):
    places = [(1 - x, y), (x, 1 - y), (1 - x, 1 - y)]
    return [(bx, by, 2 * bx + by) for bx, by in places]


def _cast_into_slot(name, gathered, shard, chip_idx, after=None):
    rows, cols = shard.shape
    tr = _pack_row_tile(rows)

    def body(k_ref, s_ref, *rest):
        rest[-1][...] = s_ref[...].astype(BF16)

    return pl.pallas_call(
        body, name=name,
        grid_spec=pltpu.PrefetchScalarGridSpec(
            num_scalar_prefetch=1, grid=(rows // tr,),
            in_specs=[pl.BlockSpec((tr, cols), lambda i, k_ref: (i, 0)), pl.BlockSpec(memory_space=pl.ANY)]
            + ([] if after is None else [pl.BlockSpec(memory_space=pl.ANY)]),
            out_specs=pl.BlockSpec((None, tr, cols), lambda i, k_ref: (k_ref[0], i, 0))),
        out_shape=SDS(gathered.shape, BF16),
        input_output_aliases={2: 0},
        compiler_params=_params("parallel"),
    )(chip_idx, shard, gathered, *([] if after is None else [after]))


def _row_halves(ref, c):
    half = ref.shape[1] // 2
    return pl.ds(pl.multiple_of(c * half, 16), half), pl.ds(pl.multiple_of((1 - c) * half, 16), half)


def _gather_ici_copies(refs, send_sems, recv_sems):
    x, y, c = _my_place()
    k = 2 * x + y
    pairs = []
    for i, ref in enumerate(refs):
        mine, _ = _row_halves(ref, c)
        for j, (bx, by, kb) in enumerate(_other_chips(x, y)):
            sems = dict(send_sem=send_sems.at[3 * i + j], recv_sem=recv_sems.at[3 * i + j], device_id_type=MESH)
            send = pltpu.make_async_remote_copy(src_ref=ref.at[k, mine], dst_ref=ref.at[k, mine],
                                                device_id=(bx, by, c), **sems)
            arrival = pltpu.make_async_remote_copy(src_ref=ref.at[kb, mine], dst_ref=ref.at[kb, mine],
                                                   device_id=(bx, by, c), **sems)
            pairs.append((send, arrival))
    return pairs


def _gather_d2d_copies(refs, send_sems, recv_sems):
    x, y, c = _my_place()
    pairs = []
    for i, ref in enumerate(refs):
        mine, theirs = _row_halves(ref, c)
        for j, (_, _, kb) in enumerate(_other_chips(x, y)):
            sem = 3 * i + j
            sems = dict(send_sem=send_sems.at[sem], recv_sem=recv_sems.at[sem], device_id=(x, y, 1 - c),
                        device_id_type=MESH)
            send = pltpu.make_async_remote_copy(src_ref=ref.at[kb, mine], dst_ref=ref.at[kb, mine], **sems)
            arrival = pltpu.make_async_remote_copy(src_ref=ref.at[kb, theirs], dst_ref=ref.at[kb, theirs], **sems)
            pairs.append((send, arrival))
    return pairs


def _run_copies(pairs):
    for send, _ in pairs:
        send.start()
    for send, arrival in pairs:
        send.wait_send()
        arrival.wait_recv()


def _gather_stage_copies(stage, refs, send_sems, recv_sems):
    if stage == "ici":
        return _gather_ici_copies(refs, send_sems, recv_sems)
    return _gather_d2d_copies(refs, send_sems, recv_sems)


def _gather_start(name, gathered, after, stage="ici"):
    n_w = len(gathered)

    def body(*refs):
        in_refs = refs[:n_w]
        send_sems, recv_sems = refs[n_w + 1:n_w + 3]
        for send, _ in _gather_stage_copies(stage, in_refs, send_sems, recv_sems):
            send.start()
        refs[-1][...] = jnp.zeros_like(refs[-1])

    out = pl.pallas_call(
        body, name=name,
        out_shape=(pltpu.SemaphoreType.DMA((3 * n_w,)), pltpu.SemaphoreType.DMA((3 * n_w,)),
                   *[pltpu.HBM(g.shape, g.dtype) for g in gathered], SDS((8, V7X_LANES), F32)),
        in_specs=[*[HBM_SPEC] * n_w, ANY], out_specs=(SEM_SPEC, SEM_SPEC, *[HBM_SPEC] * n_w, VMEM_WHOLE),
        input_output_aliases={i: 2 + i for i in range(n_w)},
        compiler_params=pltpu.CompilerParams(has_side_effects=DATAFLOW),
    )(*[pltpu.with_memory_space_constraint(g, pltpu.HBM) for g in gathered], after)
    return out[0], out[1], list(out[2:2 + n_w]), out[-1]


def _gather_wait(name, send_sems, recv_sems, gathered, after, stage="ici"):
    n_w = len(gathered)

    def body(*refs):
        in_refs = refs[:n_w]
        send_sems, recv_sems = refs[n_w:n_w + 2]
        for send, arrival in _gather_stage_copies(stage, in_refs, send_sems, recv_sems):
            send.wait_send()
            arrival.wait_recv()

    out = pl.pallas_call(
        body, name=name, out_shape=tuple(pltpu.HBM(g.shape, g.dtype) for g in gathered),
        in_specs=[*[HBM_SPEC] * n_w, SEM_SPEC, SEM_SPEC, ANY], out_specs=tuple([HBM_SPEC] * n_w),
        input_output_aliases={i: i for i in range(n_w)},
        compiler_params=pltpu.CompilerParams(has_side_effects=DATAFLOW),
    )(*gathered, send_sems, recv_sems, after)
    return list(out)


def _swap_fetched_with_sibling(name, gathered):
    n_w = len(gathered)

    def body(*refs):
        in_refs = refs[:n_w]
        send_sems, recv_sems = refs[2 * n_w:]
        _run_copies(_gather_d2d_copies(in_refs, send_sems, recv_sems))

    return pl.pallas_call(
        body, name=name, out_shape=[SDS(g.shape, g.dtype) for g in gathered],
        in_specs=[ANY] * n_w, out_specs=[ANY] * n_w, input_output_aliases={i: i for i in range(n_w)},
        scratch_shapes=[pltpu.SemaphoreType.DMA((3 * n_w,)), pltpu.SemaphoreType.DMA((3 * n_w,))],
    )(*gathered)


def _sibling_swap_copies(g_refs, land_refs, send_sems, recv_sems):
    x, y, c = _my_place()
    copies = []
    for i, g_ref in enumerate(g_refs):
        half = g_ref.shape[1] // 2
        theirs = pl.ds(pl.multiple_of((1 - c) * half, 8), half)
        copies.append(pltpu.make_async_remote_copy(
            src_ref=g_ref.at[:, theirs], dst_ref=land_refs[i], send_sem=send_sems.at[i], recv_sem=recv_sems.at[i],
            device_id=(x, y, 1 - c), device_id_type=MESH))
    return copies


def _sibling_swap_start(name, grads):
    n_g = len(grads)

    def body(*refs):
        g_refs, land_refs = refs[:n_g], refs[n_g:2 * n_g]
        send_sems, recv_sems = refs[2 * n_g:2 * n_g + 2]
        for cp in _sibling_swap_copies(g_refs, land_refs, send_sems, recv_sems):
            cp.start()
        refs[-1][...] = jnp.zeros_like(refs[-1])

    lands = [pltpu.with_memory_space_constraint(lax.empty((g.shape[0], g.shape[1] // 2, g.shape[2]), g.dtype),
                                                pltpu.HBM) for g in grads]
    out = pl.pallas_call(
        body, name=name,
        out_shape=(pltpu.SemaphoreType.DMA((n_g,)), pltpu.SemaphoreType.DMA((n_g,)),
                   *[pltpu.HBM(g.shape, g.dtype) for g in grads], *[pltpu.HBM(l.shape, l.dtype) for l in lands],
                   SDS((8, V7X_LANES), F32)),
        in_specs=[HBM_SPEC] * (2 * n_g), out_specs=(SEM_SPEC, SEM_SPEC, *[HBM_SPEC] * (2 * n_g), VMEM_WHOLE),
        input_output_aliases={i: 2 + i for i in range(2 * n_g)},
        compiler_params=pltpu.CompilerParams(has_side_effects=DATAFLOW),
    )(*[pltpu.with_memory_space_constraint(g, pltpu.HBM) for g in grads], *lands)
    return out[0], out[1], list(out[2:2 + n_g]), list(out[2 + n_g:2 + 2 * n_g]), out[-1]


def _sibling_swap_wait(name, send_sems, recv_sems, grads, lands, after):
    n_g = len(grads)

    def body(*refs):
        g_refs, land_refs = refs[:n_g], refs[n_g:2 * n_g]
        send_sems, recv_sems = refs[2 * n_g:2 * n_g + 2]
        for cp in _sibling_swap_copies(g_refs, land_refs, send_sems, recv_sems):
            cp.wait_send()
            cp.wait_recv()

    out = pl.pallas_call(
        body, name=name,
        out_shape=(*[pltpu.HBM(g.shape, g.dtype) for g in grads], *[pltpu.HBM(l.shape, l.dtype) for l in lands]),
        in_specs=[*[HBM_SPEC] * (2 * n_g), SEM_SPEC, SEM_SPEC, ANY], out_specs=tuple([HBM_SPEC] * (2 * n_g)),
        input_output_aliases={i: i for i in range(2 * n_g)},
        compiler_params=pltpu.CompilerParams(has_side_effects=DATAFLOW),
    )(*grads, *lands, send_sems, recv_sems, after)
    return list(out[:n_g]), list(out[n_g:])


def _pack_row_tile(rows):
    for t in range(min(rows, 512), 7, -1):
        if rows % t == 0 and t % 8 == 0:
            return t
    return rows


def _add_sibling_half(name, grads, from_sibling, c_idx):
    n, R, w = grads.shape
    half = R // 2
    tr = _pack_row_tile(half)
    steps = half // tr

    def body(c_ref, g_ref, s_ref, o_ref):
        o_ref[...] = (g_ref[...] + s_ref[...]).astype(BF16)

    return pl.pallas_call(
        body, name=name,
        grid_spec=pltpu.PrefetchScalarGridSpec(
            num_scalar_prefetch=1, grid=(n, steps),
            in_specs=[pl.BlockSpec((1, tr, w), lambda j, i, c_ref: (j, c_ref[0] * steps + i, 0)),
                      pl.BlockSpec((1, tr, w), lambda j, i, c_ref: (j, i, 0))],
            out_specs=pl.BlockSpec((1, tr, w), lambda j, i, c_ref: (j, i, 0))),
        out_shape=SDS((n, half, w), BF16),
        compiler_params=_params("parallel", "parallel"),
    )(c_idx, grads, from_sibling)


HBM_SPEC = pl.BlockSpec(memory_space=pltpu.HBM)
SEM_SPEC = pl.BlockSpec(memory_space=pltpu.SEMAPHORE)
DATAFLOW = pltpu.SideEffectType.DATAFLOW_SIDE_EFFECTING


def _chip_scatter_copies(p_refs, land_refs, send_sems, recv_sems):
    x, y, c = _my_place()
    return [pltpu.make_async_remote_copy(
        src_ref=p_refs[i].at[kb], dst_ref=land_refs[i].at[j], send_sem=send_sems.at[3 * i + j],
        recv_sem=recv_sems.at[3 * i + j], device_id=(bx, by, c), device_id_type=MESH)
        for i in range(len(p_refs)) for j, (bx, by, kb) in enumerate(_other_chips(x, y))]


def _scatter_start(name, partials):
    n_p = len(partials)

    def body(*refs):
        p_refs, land_refs = refs[:n_p], refs[n_p:2 * n_p]
        send_sems, recv_sems = refs[2 * n_p:2 * n_p + 2]
        for cp in _chip_scatter_copies(p_refs, land_refs, send_sems, recv_sems):
            cp.start()
        refs[-1][...] = jnp.zeros_like(refs[-1])

    lands = [pltpu.with_memory_space_constraint(lax.empty((N_CHIPS - 1,) + p.shape[1:], p.dtype), pltpu.HBM)
             for p in partials]
    out = pl.pallas_call(
        body, name=name,
        out_shape=(pltpu.SemaphoreType.DMA((3 * n_p,)), pltpu.SemaphoreType.DMA((3 * n_p,)),
                   *[pltpu.HBM(p.shape, p.dtype) for p in partials], *[pltpu.HBM(l.shape, l.dtype) for l in lands],
                   SDS((8, V7X_LANES), F32)),
        in_specs=[HBM_SPEC] * (2 * n_p), out_specs=(SEM_SPEC, SEM_SPEC, *[HBM_SPEC] * (2 * n_p), VMEM_WHOLE),
        input_output_aliases={i: 2 + i for i in range(2 * n_p)},
        compiler_params=pltpu.CompilerParams(has_side_effects=DATAFLOW),
    )(*[pltpu.with_memory_space_constraint(p, pltpu.HBM) for p in partials], *lands)
    return out[0], out[1], list(out[2:2 + n_p]), list(out[2 + n_p:2 + 2 * n_p]), out[-1]


def _scatter_wait(name, send_sems, recv_sems, partials, lands, after):
    n_p = len(partials)

    def body(*refs):
        p_refs, land_refs = refs[:n_p], refs[n_p:2 * n_p]
        send_sems, recv_sems = refs[2 * n_p:2 * n_p + 2]
        for cp in _chip_scatter_copies(p_refs, land_refs, send_sems, recv_sems):
            cp.wait_send()
            cp.wait_recv()

    out = pl.pallas_call(
        body, name=name,
        out_shape=(*[pltpu.HBM(p.shape, p.dtype) for p in partials], *[pltpu.HBM(l.shape, l.dtype) for l in lands]),
        in_specs=[*[HBM_SPEC] * (2 * n_p), SEM_SPEC, SEM_SPEC, ANY], out_specs=tuple([HBM_SPEC] * (2 * n_p)),
        input_output_aliases={i: i for i in range(2 * n_p)},
        compiler_params=pltpu.CompilerParams(has_side_effects=DATAFLOW),
    )(*partials, *lands, send_sems, recv_sems, after)
    return list(out[:n_p]), list(out[n_p:])


def _sum_chip_partials(name, grads, from_sibling, received, shard, layer, place):
    n, half, w = from_sibling.shape
    tr = _pack_row_tile(half)
    steps = half // tr

    def body(place_ref, g_ref, s_ref, r_ref, shard_ref, o_ref):
        own = g_ref[0] + s_ref[0]
        o_ref[...] = ((own + r_ref[0].astype(F32)) + r_ref[1].astype(F32)) + r_ref[2].astype(F32)

    return pl.pallas_call(
        body, name=name,
        grid_spec=pltpu.PrefetchScalarGridSpec(
            num_scalar_prefetch=1, grid=(steps,),
            in_specs=[pl.BlockSpec((1, tr, w), lambda i, place_ref: (place_ref[0], place_ref[1] * steps + i, 0)),
                      pl.BlockSpec((1, tr, w), lambda i, place_ref: (place_ref[0], i, 0)),
                      pl.BlockSpec((n - 1, tr, w), lambda i, place_ref: (0, i, 0)),
                      pl.BlockSpec(memory_space=pl.ANY)],
            out_specs=pl.BlockSpec((tr, w), lambda i, place_ref: ((2 * layer + place_ref[1]) * steps + i, 0))),
        out_shape=SDS(shard.shape, F32),
        input_output_aliases={4: 0},
        compiler_params=_params("parallel"),
    )(place, grads, from_sibling, received, shard)


def _join_halves(shards, layers):
    n_s = len(shards)
    n_sem = sum(layers)

    def body(*refs):
        in_refs = refs[:n_s]
        send_sems, recv_sems = refs[2 * n_s:]
        x, y, c = _my_place()
        copies, sem = [], 0
        for ref, n_layers in zip(in_refs, layers):
            half = ref.shape[0] // (2 * n_layers)
            for layer in range(n_layers):
                mine = pl.ds(pl.multiple_of(layer * 2 * half + c * half, 8), half)
                theirs = pl.ds(pl.multiple_of(layer * 2 * half + (1 - c) * half, 8), half)
                send = pltpu.make_async_remote_copy(
                    src_ref=ref.at[mine], dst_ref=ref.at[mine], send_sem=send_sems.at[sem], recv_sem=recv_sems.at[sem],
                    device_id=(x, y, 1 - c), device_id_type=MESH)
                send.start()
                arrival = pltpu.make_async_remote_copy(
                    src_ref=ref.at[theirs], dst_ref=ref.at[theirs], send_sem=send_sems.at[sem],
                    recv_sem=recv_sems.at[sem], device_id=(x, y, 1 - c), device_id_type=MESH)
                copies.append((send, arrival))
                sem += 1
        for send, arrival in copies:
            send.wait_send()
            arrival.wait_recv()

    return pl.pallas_call(
        body, name="join_halves", out_shape=[SDS(s.shape, s.dtype) for s in shards],
        in_specs=[ANY] * n_s, out_specs=[ANY] * n_s,
        input_output_aliases={i: i for i in range(n_s)},
        scratch_shapes=[pltpu.SemaphoreType.DMA((n_sem,)), pltpu.SemaphoreType.DMA((n_sem,))],
    )(*shards)


def _all_to_all_copies(v_ref, land_ref, send_sems, recv_sems):
    x, y, c = _my_place()
    me = 4 * x + 2 * y + c
    pairs = []
    for k in range(1, N_DEV):
        px, py, pc = (1 - x if k & 4 else x), (1 - y if k & 2 else y), (1 - c if k & 1 else c)
        sems = dict(send_sem=send_sems.at[k - 1], recv_sem=recv_sems.at[k - 1], device_id=(px, py, pc),
                    device_id_type=MESH)
        send = pltpu.make_async_remote_copy(src_ref=v_ref, dst_ref=land_ref.at[me], **sems)
        arrival = pltpu.make_async_remote_copy(src_ref=v_ref, dst_ref=land_ref.at[4 * px + 2 * py + pc], **sems)
        pairs.append((send, arrival))
    return pairs


def _small_reduce_start(name, v):
    def body(v_ref, land_ref, send_sems, recv_sems, v_out, land_out):
        for send, _ in _all_to_all_copies(v_ref, land_ref, send_sems, recv_sems):
            send.start()

    land = pltpu.with_memory_space_constraint(jnp.zeros((N_DEV,) + v.shape, v.dtype), pltpu.HBM)
    return pl.pallas_call(
        body, name=name,
        out_shape=(pltpu.SemaphoreType.DMA((N_DEV - 1,)), pltpu.SemaphoreType.DMA((N_DEV - 1,)),
                   pltpu.HBM(v.shape, v.dtype), pltpu.HBM(land.shape, land.dtype)),
        in_specs=[HBM_SPEC, HBM_SPEC], out_specs=(SEM_SPEC, SEM_SPEC, HBM_SPEC, HBM_SPEC),
        input_output_aliases={0: 2, 1: 3},
        compiler_params=pltpu.CompilerParams(has_side_effects=DATAFLOW),
    )(pltpu.with_memory_space_constraint(v, pltpu.HBM), land)


def _small_reduce_wait(name, send_sems, recv_sems, v, land, after):
    def body(v_ref, land_ref, send_sems, recv_sems, after_ref, v_out, land_out):
        for send, arrival in _all_to_all_copies(v_ref, land_ref, send_sems, recv_sems):
            send.wait_send()
            arrival.wait_recv()

    return pl.pallas_call(
        body, name=name, out_shape=(pltpu.HBM(v.shape, v.dtype), pltpu.HBM(land.shape, land.dtype)),
        in_specs=[HBM_SPEC, HBM_SPEC, SEM_SPEC, SEM_SPEC, ANY], out_specs=(HBM_SPEC, HBM_SPEC),
        input_output_aliases={0: 0, 1: 1},
        compiler_params=pltpu.CompilerParams(has_side_effects=DATAFLOW),
    )(v, land, send_sems, recv_sems, after)


def _sum_device_slots(name, v, land, me):
    r, w = v.shape

    def body(me_ref, v_ref, land_ref, o_ref):
        mine = v_ref[...]
        acc = jnp.where(me_ref[0] == 0, mine, land_ref[0])
        for d in range(1, N_DEV):
            acc = acc + jnp.where(me_ref[0] == d, mine, land_ref[d])
        o_ref[...] = acc

    return pl.pallas_call(
        body, name=name,
        grid_spec=pltpu.PrefetchScalarGridSpec(
            num_scalar_prefetch=1, grid=(1,),
            in_specs=[pl.BlockSpec((r, w), lambda i, me_ref: (0, 0)),
                      pl.BlockSpec((N_DEV, r, w), lambda i, me_ref: (0, 0, 0))],
            out_specs=pl.BlockSpec((r, w), lambda i, me_ref: (0, 0))),
        out_shape=SDS((r, w), F32),
        compiler_params=_params("arbitrary"),
    )(me, v, land)


def _adamw(name, w, g, m, v):
    rows, width = w.shape
    tr = _pack_row_tile(rows)

    def body(w_ref, g_ref, m_ref, v_ref, g_out_ref, d_ref, nm_ref, nv_ref):
        gg = g_ref[...]
        g_out_ref[...] = gg
        m_new = ADAM_B1 * m_ref[...] + (1.0 - ADAM_B1) * gg
        v_new = ADAM_B2 * v_ref[...] + (1.0 - ADAM_B2) * (gg * gg)
        m_hat = m_new / (1.0 - ADAM_B1 ** ADAM_STEP)
        v_hat = v_new / (1.0 - ADAM_B2 ** ADAM_STEP)
        d_ref[...] = -ADAM_LR * (m_hat / (jnp.sqrt(v_hat) + ADAM_EPS) + ADAM_WD * w_ref[...])
        nm_ref[...] = m_new
        nv_ref[...] = v_new

    spec = _rows(tr, width)
    return pl.pallas_call(
        body, name=name, grid=(rows // tr,),
        in_specs=[spec] * 4, out_specs=[spec] * 4,
        out_shape=[SDS((rows, width), F32)] * 4,
        compiler_params=_params("parallel"),
    )(w, g, m, v)


WEIGHT_NAMES = ['norm_mix', 'norm_ffn', 'attn_w_qkv', 'attn_b_qkv', 'attn_sinks', 'attn_w_o', 'attn_b_o',
                'conv_w_pw1', 'conv_b_pw1', 'conv_w_dw', 'conv_b_dw', 'conv_ln_g', 'conv_ln_b', 'conv_w_pw2',
                'conv_b_pw2', 'ffn_w1', 'ffn_w3', 'ffn_w2', 'norm_final']
BIG = ['attn_w_qkv', 'attn_w_o', 'conv_w_pw1', 'conv_w_pw2', 'ffn_w1', 'ffn_w3', 'ffn_w2']
COLUMN_SPLIT = ('attn_w_qkv', 'conv_w_pw1', 'ffn_w1', 'ffn_w3')
STORED_TRANSPOSED = ('attn_w_qkv', 'ffn_w1', 'ffn_w3')
SMALL_SPLIT = ['conv_b_pw1', 'conv_w_dw', 'conv_b_dw', 'conv_ln_g', 'conv_ln_b', 'conv_b_pw2']
SMALL_WHOLE = ['norm_mix', 'norm_ffn', 'attn_b_qkv', 'attn_sinks', 'attn_b_o', 'norm_final']


def _keeps_rows(shape):
    return len(shape) == 2 and shape[0] > 1 and shape[1] == PACK_W


def _pack_rows(arrays, dtype, row_multiple):
    blocks = [jnp.pad(a.astype(dtype), ((0, -a.shape[0] % V7X_SUBLANES), (0, 0)))
              for a in arrays if _keeps_rows(a.shape)]
    flat = jnp.concatenate([a.astype(dtype).reshape(-1) for a in arrays if not _keeps_rows(a.shape)])
    multiple = max(row_multiple, V7X_SUBLANES)
    rows = -(-(-(-flat.shape[0] // PACK_W)) // multiple) * multiple
    blocks.append(jnp.pad(flat, (0, rows * PACK_W - flat.shape[0])).reshape(rows, PACK_W))
    return jnp.concatenate(blocks, axis=0) if len(blocks) > 1 else blocks[0]


def _unpack_rows(pack, shapes):
    out, row = {}, 0
    for i, shape in enumerate(shapes):
        if _keeps_rows(shape):
            out[i] = pack[row:row + shape[0]]
            row += -(-shape[0] // V7X_SUBLANES) * V7X_SUBLANES
    flat, at = pack[row:].reshape(-1), 0
    for i, shape in enumerate(shapes):
        if not _keeps_rows(shape):
            size = 1
            for s in shape:
                size *= s
            out[i] = flat[at:at + size].reshape(shape)
            at += size
    return [out[i] for i in range(len(shapes))]


def _join_chip_axis(name, parts):
    axis = parts.ndim - 1 if name in COLUMN_SPLIT or name in SMALL_SPLIT else parts.ndim - 2
    moved = jnp.moveaxis(parts, 0, axis - 1)
    shape = list(moved.shape)
    shape[axis - 1:axis + 1] = [shape[axis - 1] * shape[axis]]
    return moved.reshape(shape)


def _split_chip_axis(name, whole, shard_shape):
    axis = len(shard_shape) - 1 if name in COLUMN_SPLIT or name in SMALL_SPLIT else len(shard_shape) - 2
    shape = list(whole.shape)
    shape[axis:axis + 1] = [N_CHIPS, shard_shape[axis]]
    return jnp.moveaxis(whole.reshape(shape), axis, 0)


def kernel(x, norm_mix, norm_ffn, attn_w_qkv, attn_b_qkv, attn_sinks, attn_w_o, attn_b_o, conv_w_pw1, conv_b_pw1, conv_w_dw, conv_b_dw, conv_ln_g, conv_ln_b, conv_w_pw2, conv_b_pw2, ffn_w1, ffn_w3, ffn_w2, norm_final, loss_target, m_norm_mix, m_norm_ffn, m_attn_w_qkv, m_attn_b_qkv, m_attn_sinks, m_attn_w_o, m_attn_b_o, m_conv_w_pw1, m_conv_b_pw1, m_conv_w_dw, m_conv_b_dw, m_conv_ln_g, m_conv_ln_b, m_conv_w_pw2, m_conv_b_pw2, m_ffn_w1, m_ffn_w3, m_ffn_w2, m_norm_final, v_norm_mix, v_norm_ffn, v_attn_w_qkv, v_attn_b_qkv, v_attn_sinks, v_attn_w_o, v_attn_b_o, v_conv_w_pw1, v_conv_b_pw1, v_conv_w_dw, v_conv_b_dw, v_conv_ln_g, v_conv_ln_b, v_conv_w_pw2, v_conv_b_pw2, v_ffn_w1, v_ffn_w3, v_ffn_w2, v_norm_final):
    w = dict(zip(WEIGHT_NAMES, (norm_mix, norm_ffn, attn_w_qkv, attn_b_qkv, attn_sinks, attn_w_o, attn_b_o,
                                conv_w_pw1, conv_b_pw1, conv_w_dw, conv_b_dw, conv_ln_g, conv_ln_b, conv_w_pw2,
                                conv_b_pw2, ffn_w1, ffn_w3, ffn_w2, norm_final)))
    m = dict(zip(WEIGHT_NAMES, (m_norm_mix, m_norm_ffn, m_attn_w_qkv, m_attn_b_qkv, m_attn_sinks, m_attn_w_o,
                                m_attn_b_o, m_conv_w_pw1, m_conv_b_pw1, m_conv_w_dw, m_conv_b_dw, m_conv_ln_g,
                                m_conv_ln_b, m_conv_w_pw2, m_conv_b_pw2, m_ffn_w1, m_ffn_w3, m_ffn_w2, m_norm_final)))
    v = dict(zip(WEIGHT_NAMES, (v_norm_mix, v_norm_ffn, v_attn_w_qkv, v_attn_b_qkv, v_attn_sinks, v_attn_w_o,
                                v_attn_b_o, v_conv_w_pw1, v_conv_b_pw1, v_conv_w_dw, v_conv_b_dw, v_conv_ln_g,
                                v_conv_ln_b, v_conv_w_pw2, v_conv_b_pw2, v_ffn_w1, v_ffn_w3, v_ffn_w2, v_norm_final)))
    T, D = x.shape[1], x.shape[2]
    c_idx = lax.axis_index("c").astype(jnp.int32).reshape(1)
    chip = (2 * lax.axis_index("x") + lax.axis_index("y")).astype(jnp.int32)

    def as_rows(n, a):
        a = jnp.swapaxes(a, -1, -2) if n in STORED_TRANSPOSED else a
        return a.reshape(-1, a.shape[-1])

    def from_rows(n, rows):
        shape = w[n].shape[:-2] + w[n].shape[:-3:-1] if n in STORED_TRANSPOSED else w[n].shape
        a = rows.reshape(shape)
        return jnp.swapaxes(a, -1, -2) if n in STORED_TRANSPOSED else a

    def cast(n, after=None):
        return _cast_into_slot(f"cast_{n}", lax.empty((N_CHIPS,) + as_rows(n, w[n]).shape, BF16), as_rows(n, w[n]),
                               chip.reshape(1), after)

    first, later = BIG[:2], BIG[2:]
    slabs = {n: cast(n) for n in first}
    attn_send, attn_recv, attn_travelling, attn_started = _gather_start(
        "gather_attn_start", [slabs[n] for n in first], slabs[first[0]])
    slabs.update({n: cast(n, attn_started) for n in later})
    layers = ffn_w1.shape[0]
    small_shapes = [w[n].shape for n in SMALL_SPLIT]
    vec_send, vec_recv, vec_mine, vec_landing = _small_reduce_start(
        "small_gather_start", _pack_rows([w[n] for n in SMALL_SPLIT], F32, 8))
    attn_landed = _gather_wait("gather_attn_wait", attn_send, attn_recv, attn_travelling, vec_mine)
    qkv_parts, w_o_parts = _swap_fetched_with_sibling("gather_attn_swap", attn_landed)
    send_sems, recv_sems, travelling, gather_started = _gather_start("gather_start", [slabs[n] for n in later],
                                                                     qkv_parts)

    def small_vectors(after):
        mine, landed = _small_reduce_wait("small_gather_wait", vec_send, vec_recv, vec_mine, vec_landing, after)
        landed = lax.dynamic_update_index_in_dim(landed, mine, 2 * chip + c_idx[0], axis=0)
        per_chip = [_unpack_rows(landed[2 * j], small_shapes) for j in range(N_CHIPS)]
        whole = {n: _join_chip_axis(n, jnp.stack([per_chip[j][i] for j in range(N_CHIPS)]))
                 for i, n in enumerate(SMALL_SPLIT)}
        return {**{n: whole[n] for n in SMALL_SPLIT if n != "conv_w_dw"}, "conv_w_dw": whole["conv_w_dw"][0]}

    def other_weights(after):
        landed = dict(zip(later, _gather_wait("gather_wait", send_sems, recv_sems, travelling, after)))
        now, then = ["ffn_w1", "ffn_w3"], ["conv_w_pw1", "conv_w_pw2", "ffn_w2"]
        ready = dict(zip(now, _swap_fetched_with_sibling("gather_swap", [landed[n] for n in now])))
        swap_send, swap_recv, swapping_rest, swap_started = _gather_start(
            "gather_swap_start", [landed[n] for n in then], ready[now[0]], stage="d2d")

        def rest_of_weights(after_next):
            rest = dict(zip(then, _gather_wait("gather_swap_wait", swap_send, swap_recv, swapping_rest, after_next,
                                               stage="d2d")))
            return {"conv_w_pw1": rest["conv_w_pw1"], "conv_w_pw2": rest["conv_w_pw2"].reshape(-1, D),
                    "ffn_w2": rest["ffn_w2"].reshape(N_CHIPS, layers, -1, D), **small_vectors(after_next)}

        return {"ffn_w1": ready["ffn_w1"].reshape(N_CHIPS, layers, -1, D),
                "ffn_w3": ready["ffn_w3"].reshape(N_CHIPS, layers, -1, D),
                "swap_started": swap_started, "rest_of_weights": rest_of_weights}

    p = {
        "norm_mix": norm_mix, "norm_ffn": norm_ffn, "norm_final": norm_final.reshape(1, D),
        "attn_w_qkv": qkv_parts.reshape(-1, D), "attn_b_qkv": attn_b_qkv,
        "attn_sinks": attn_sinks, "attn_w_o": w_o_parts.reshape(-1, D), "attn_b_o": attn_b_o,
        "gather_started": gather_started, "other_weights": other_weights,
    }
    swapping, in_flight = {}, []

    def reduce_begin(tag, grads):
        keys = list(grads)
        *handles, begun = _sibling_swap_start(f"sibling_swap_start_{tag}", [grads[k] for k in keys])
        swapping[tag] = (keys, handles)
        return begun

    def reduce_send(tag, after):
        keys, (swap_send, swap_recv, grads, lands) = swapping[tag]
        grads, from_sibling = _sibling_swap_wait(f"sibling_swap_wait_{tag}", swap_send, swap_recv, grads, lands, after)
        partials = [_add_sibling_half(f"add_sibling_half_{tag}{i}", gr, fs, c_idx)
                    for i, (gr, fs) in enumerate(zip(grads, from_sibling))]
        *handles, sent = _scatter_start(f"scatter_start_{tag}", partials)
        in_flight.append((tag, keys, handles, grads, from_sibling))
        return sent

    loss_part, dx, g = _local_step(x[0], loss_target[0], p, reduce_begin, reduce_send)
    for n in SMALL_WHOLE + SMALL_SPLIT:
        g[n] = g[n].reshape((-1,) + g[n].shape[-2:]) if w[n].ndim == 3 else g[n].reshape(w[n].shape[:-1] + (-1,))

    small_pack = _pack_rows([loss_part] + [g[n] for n in SMALL_WHOLE] + [g[n] for n in SMALL_SPLIT], F32, 8)
    small_send, small_recv, small_pack, small_land = _small_reduce_start("small_reduce_start", small_pack)

    place = jnp.stack([chip, c_idx[0]])
    shard_grad = {n: lax.empty(as_rows(n, w[n]).shape, F32) for n in BIG}
    for tag, keys, (send_sems, recv_sems, partials, lands), grads, from_sibling in in_flight:
        _, received = _scatter_wait(f"scatter_wait_{tag}", send_sems, recv_sems, partials, lands, small_pack)
        for i, (n, layer) in enumerate(keys):
            shard_grad[n] = _sum_chip_partials(f"sum_chip_partials_{tag}{i}", grads[i], from_sibling[i], received[i],
                                               shard_grad[n], layer, place)
    g_big = dict(zip(BIG, _join_halves([shard_grad[n] for n in BIG], [w[n].shape[0] for n in BIG])))
    big_out = {}
    for n in BIG:
        step = _adamw(f"adamw_{n}", as_rows(n, w[n]), g_big[n], as_rows(n, m[n]), as_rows(n, v[n]))
        big_out[n] = [from_rows(n, a) for a in step]

    small_whole_shapes = [w[n].shape for n in SMALL_WHOLE]
    small_full_shapes = [g[n].shape for n in SMALL_SPLIT]
    small_pack, small_land = _small_reduce_wait("small_reduce_wait", small_send, small_recv, small_pack, small_land,
                                                big_out[BIG[-1]][1])
    reduced = _sum_device_slots("small_reduce_sum", small_pack, small_land, (2 * chip + c_idx[0]).reshape(1))
    pieces = _unpack_rows(reduced, [(1,)] + small_whole_shapes + small_full_shapes)
    loss = pieces[0].reshape(())
    g_small = dict(zip(SMALL_WHOLE, pieces[1:1 + len(SMALL_WHOLE)]))
    for n, whole in zip(SMALL_SPLIT, pieces[1 + len(SMALL_WHOLE):]):
        parts = _split_chip_axis(n, whole, w[n].shape)
        g_small[n] = lax.dynamic_index_in_dim(parts, chip, axis=0, keepdims=False)
    small = SMALL_WHOLE + SMALL_SPLIT
    _, d_small, m_small, v_small = _adamw(
        "adamw_small", _pack_rows([w[n] for n in small], F32, 8), _pack_rows([g_small[n] for n in small], F32, 8),
        _pack_rows([m[n] for n in small], F32, 8), _pack_rows([v[n] for n in small], F32, 8))

    outs = {}
    for slot, (tag, small_pack) in enumerate((("g", None), ("d", d_small), ("m", m_small), ("v", v_small))):
        vals = {n: big_out[n][slot] for n in BIG}
        if small_pack is None:
            vals.update(g_small)
        else:
            vals.update(zip(small, _unpack_rows(small_pack, [w[n].shape for n in small])))
        outs[tag] = vals
    return (loss, dx.reshape(1, T, D), *[outs["g"][n] for n in WEIGHT_NAMES], *[outs["d"][n] for n in WEIGHT_NAMES],
            *[outs["m"][n] for n in WEIGHT_NAMES], *[outs["v"][n] for n in WEIGHT_NAMES])
```

```python
import functools

import jax
import jax.numpy as jnp
from jax import lax
from jax.experimental import pallas as pl
from jax.experimental.pallas import tpu as pltpu

F32 = jnp.float32
BF16 = jnp.bfloat16
SDS = jax.ShapeDtypeStruct
MESH = pl.DeviceIdType.MESH

HEAD_DIM = 64
N_Q_HEADS = 16
N_KV_HEADS = 2
Q_PER_KV = N_Q_HEADS // N_KV_HEADS
ATTN_BLOCK = 128
ROPE_THETA = 10000.0
CONV_WIDTH = 31
CONV_HALO = 32
CONV_FIRST_TAP = CONV_HALO - CONV_WIDTH + 1
CONV_ROW_CHUNK = 64
CONV_LANE_CHUNK = 256
CONV_GRAD_UNROLL = 8
RMS_EPS = 1e-5
LN_EPS = 1e-5
ADAM_LR = 0.001
ADAM_B1 = 0.9
ADAM_B2 = 0.999
ADAM_EPS = 1e-08
ADAM_WD = 0.01
ADAM_STEP = 10

V7X_LANES = 128
V7X_SUBLANES = 8
V7X_VMEM_LIMIT_BYTES = 56 * 1024 * 1024

N_CHIPS = 4
N_DEV = 8
PACK_W = 1024

MASK_VALUE = -1e30


def _params(*semantics):
    return pltpu.CompilerParams(dimension_semantics=semantics, vmem_limit_bytes=V7X_VMEM_LIMIT_BYTES)


def _rows(tm, width):
    return pl.BlockSpec((tm, width), lambda i: (i, 0))


def _whole(shape):
    return pl.BlockSpec(shape, lambda *_: (0,) * len(shape))


def _rms_rstd(h):
    return lax.rsqrt(jnp.mean(h * h, axis=-1, keepdims=True) + RMS_EPS)


def _silu_and_grad(z):
    sg = jax.nn.sigmoid(z)
    return z * sg, sg * (1.0 + z * (1.0 - sg))


def _swap_rope_halves(t):
    w = t.shape[1]
    half = HEAD_DIM // 2
    lane = lax.broadcasted_iota(jnp.int32, t.shape, 1)
    upper = pltpu.roll(t, w - half, 1)
    lower = pltpu.roll(t, half, 1)
    return jnp.where(lane % HEAD_DIM < half, upper, lower)


def _rope(t, cos_ref, sin_ref):
    reps = t.shape[1] // V7X_LANES
    c = jnp.tile(cos_ref[...], (1, reps))
    s = jnp.tile(sin_ref[...], (1, reps))
    return t * c + _swap_rope_halves(t) * s


def _rope_transposed(dt, cos_ref, sin_ref):
    reps = dt.shape[1] // V7X_LANES
    c = jnp.tile(cos_ref[...], (1, reps))
    s = jnp.tile(sin_ref[...], (1, reps))
    return dt * c + _swap_rope_halves(dt * s)


def _rope_tables(seq_len):
    pos = jnp.arange(seq_len, dtype=F32)
    inv_freq = ROPE_THETA ** (-jnp.arange(0, HEAD_DIM, 2, dtype=F32) / HEAD_DIM)
    ang = pos[:, None] * jnp.tile(inv_freq, 2 * V7X_LANES // HEAD_DIM)[None, :]
    upper_half = jnp.arange(V7X_LANES) % HEAD_DIM >= HEAD_DIM // 2
    return jnp.cos(ang), jnp.where(upper_half[None, :], jnp.sin(ang), -jnp.sin(ang))


def _qkv_proj(h, g, w, b, cos, sin, after):
    T, D = h.shape
    N = w.shape[0]
    tm = min(512, T)
    rope_w = N - N_KV_HEADS * HEAD_DIM

    def body(h_ref, g_ref, w_ref, b_ref, cos_ref, sin_ref, _, y_ref, o_ref):
        hh = h_ref[...]
        y = (hh * _rms_rstd(hh) * g_ref[...]).astype(BF16)
        y_ref[...] = y
        acc = _dot_nt(y, w_ref[...]) + b_ref[...]
        o_ref[:, :rope_w] = _rope(acc[:, :rope_w], cos_ref, sin_ref).astype(BF16)
        o_ref[:, rope_w:] = acc[:, rope_w:].astype(BF16)

    return pl.pallas_call(
        body, name="qkv_proj", grid=(T // tm,),
        in_specs=[_rows(tm, D), _whole((1, D)), _whole((N, D)), _whole((1, N)),
                  _rows(tm, V7X_LANES), _rows(tm, V7X_LANES), pl.BlockSpec(memory_space=pl.ANY)],
        out_specs=[_rows(tm, D), _rows(tm, N)],
        out_shape=[SDS((T, D), BF16), SDS((T, N), BF16)],
        compiler_params=_params("parallel"),
    )(h, g, w, b, cos, sin, after)


def _pw1_proj(h, g, w, b):
    T, D = h.shape
    n = w.shape[2]
    N = N_CHIPS * n
    tm = min(512, T)

    def body(h_ref, g_ref, w_ref, b_ref, y_ref, o_ref):
        hh = h_ref[...]
        y = (hh * _rms_rstd(hh) * g_ref[...]).astype(BF16)
        y_ref[...] = y
        for j in range(N_CHIPS):
            cols = slice(j * n, (j + 1) * n)
            o_ref[:, cols] = (jnp.dot(y, w_ref[j], preferred_element_type=F32) + b_ref[:, cols]).astype(BF16)

    return pl.pallas_call(
        body, name="pw1_proj", grid=(T // tm,),
        in_specs=[_rows(tm, D), _whole((1, D)), _whole((N_CHIPS, D, n)), _whole((1, N))],
        out_specs=[_rows(tm, D), _rows(tm, N)],
        out_shape=[SDS((T, D), BF16), SDS((T, N), BF16)],
        compiler_params=_params("parallel"),
    )(h, g, w, b)


PAIRS_PER_KV = Q_PER_KV // 2


def _upper_lanes(shape):
    return lax.broadcasted_iota(jnp.int32, shape, 1) >= HEAD_DIM


def _swap_lane_halves(t):
    return pltpu.roll(t.astype(F32), HEAD_DIM, 1).astype(t.dtype)


def _kv_operands(g, t):
    swapped = _swap_lane_halves(t)
    in_lower, in_upper = (t, swapped) if g == 0 else (swapped, t)
    upper = _upper_lanes(t.shape)
    zero = jnp.zeros_like(t)
    return jnp.where(upper, zero, in_lower), jnp.where(upper, in_upper, zero)


def _group_heads(g):
    pairs = range(g * PAIRS_PER_KV, (g + 1) * PAIRS_PER_KV)
    return [2 * hp for hp in pairs] + [2 * hp + 1 for hp in pairs]


def _all_heads():
    return [h for g in range(N_KV_HEADS) for h in _group_heads(g)]


def _pair_rows(ref, g):
    pairs = range(g * PAIRS_PER_KV, (g + 1) * PAIRS_PER_KV)
    return jnp.concatenate([ref[:, hp * 2 * HEAD_DIM:(hp + 1) * 2 * HEAD_DIM] for hp in pairs], axis=0)


def _from_previous_block(rows):
    row = lax.broadcasted_iota(jnp.int32, (ATTN_BLOCK, ATTN_BLOCK), 0)
    col = lax.broadcasted_iota(jnp.int32, (ATTN_BLOCK, ATTN_BLOCK), 1)
    return jnp.concatenate([col > row] * (rows // ATTN_BLOCK), axis=0)


def _split_folded(t, prev_part):
    tb = t.astype(BF16)
    zero = jnp.zeros_like(tb)
    return jnp.where(prev_part, tb, zero), jnp.where(prev_part, zero, tb)


def _attn_specs(T):
    nb = T // ATTN_BLOCK
    kcol = N_Q_HEADS * HEAD_DIM // V7X_LANES
    cur = lambda n: jnp.minimum(n, nb - 1)
    prev = lambda n: jnp.maximum(jnp.minimum(n, nb - 1) - 1, 0)
    q_spec = pl.BlockSpec((ATTN_BLOCK, N_Q_HEADS * HEAD_DIM), lambda n: (cur(n), 0))
    kc_spec = pl.BlockSpec((ATTN_BLOCK, V7X_LANES), lambda n: (cur(n), kcol))
    kp_spec = pl.BlockSpec((ATTN_BLOCK, V7X_LANES), lambda n: (prev(n), kcol))
    vc_spec = pl.BlockSpec((ATTN_BLOCK, V7X_LANES), lambda n: (cur(n), kcol + 1))
    vp_spec = pl.BlockSpec((ATTN_BLOCK, V7X_LANES), lambda n: (prev(n), kcol + 1))
    return q_spec, kc_spec, kp_spec, vc_spec, vp_spec


def _attn_fwd(qkv, sinks):
    T = qkv.shape[0]
    nb = T // ATTN_BLOCK
    qw = N_Q_HEADS * HEAD_DIM
    all_rows = N_Q_HEADS * ATTN_BLOCK

    per_step = 4
    kcol = qw // V7X_LANES

    def body(q_ref, *refs):
        k_refs, v_refs = refs[:per_step + 1], refs[per_step + 1:2 * per_step + 2]
        sink_ref, o_ref, probs_ref, psink_ref = refs[2 * per_step + 2:]
        m = pl.program_id(0)
        half = PAIRS_PER_KV * ATTN_BLOCK
        groups = range(N_KV_HEADS)
        k_ops = [[_kv_operands(g, r[...]) for g in groups] for r in k_refs]
        v_ops = [[_kv_operands(g, r[...]) for g in groups] for r in v_refs]

        def scores(q, k):
            return lax.dot_general(q, k, (((1,), (1,)), ((), ())), preferred_element_type=F32)

        s_prev, s_cur = [], []
        for b in range(per_step):
            rows = slice(b * ATTN_BLOCK, (b + 1) * ATTN_BLOCK)
            for g in groups:
                pairs = range(g * PAIRS_PER_KV, (g + 1) * PAIRS_PER_KV)
                q = jnp.concatenate([q_ref[rows, hp * 2 * HEAD_DIM:(hp + 1) * 2 * HEAD_DIM] for hp in pairs], axis=0)
                for i in range(2):
                    from_prev = scores(q, k_ops[b][g][i])
                    if b == 0:
                        from_prev = jnp.where(m > 0, from_prev, MASK_VALUE * (HEAD_DIM ** 0.5))
                    s_prev.append(from_prev)
                    s_cur.append(scores(q, k_ops[b + 1][g][i]))
        prev_part = _from_previous_block(per_step * all_rows)
        s = jnp.where(prev_part, jnp.concatenate(s_prev, axis=0), jnp.concatenate(s_cur, axis=0)) * (HEAD_DIM ** -0.5)
        sink = jnp.concatenate([jnp.broadcast_to(sink_ref[0:1, h:h + 1], (ATTN_BLOCK, 1))
                                for h in _all_heads() * per_step], axis=0)
        top = jnp.maximum(jnp.max(s, axis=1, keepdims=True), sink)
        p = jnp.exp(s - top)
        e_sink = jnp.exp(sink - top)
        inv = 1.0 / (jnp.sum(p, axis=1, keepdims=True) + e_sink)
        probs, p_sink = p * inv, e_sink * inv
        p_prev, p_cur = _split_folded(probs, prev_part)
        lane = lax.broadcasted_iota(jnp.int32, (ATTN_BLOCK, V7X_LANES), 1)
        for b in range(per_step):
            rows = slice(b * ATTN_BLOCK, (b + 1) * ATTN_BLOCK)
            base = b * all_rows
            probs_ref[b] = probs[base:base + all_rows].astype(BF16)
            sink_tile = jnp.zeros((ATTN_BLOCK, V7X_LANES), F32)
            for i, h in enumerate(_all_heads()):
                sink_tile = jnp.where(lane == h, p_sink[base + i * ATTN_BLOCK:base + (i + 1) * ATTN_BLOCK], sink_tile)
            psink_ref[rows, :] = sink_tile
            for g in groups:
                even = slice(base + 2 * g * half, base + (2 * g + 1) * half)
                odd = slice(base + (2 * g + 1) * half, base + (2 * g + 2) * half)
                o = (jnp.dot(p_prev[even], v_ops[b][g][0], preferred_element_type=F32)
                     + jnp.dot(p_cur[even], v_ops[b + 1][g][0], preferred_element_type=F32)
                     + jnp.dot(p_prev[odd], v_ops[b][g][1], preferred_element_type=F32)
                     + jnp.dot(p_cur[odd], v_ops[b + 1][g][1], preferred_element_type=F32))
                for i in range(PAIRS_PER_KV):
                    hp = g * PAIRS_PER_KV + i
                    o_ref[rows, hp * 2 * HEAD_DIM:(hp + 1) * 2 * HEAD_DIM] = (
                        o[i * ATTN_BLOCK:(i + 1) * ATTN_BLOCK].astype(BF16))

    def kv_specs(col):
        return [pl.BlockSpec((ATTN_BLOCK, V7X_LANES),
                             functools.partial(lambda m, b: (jnp.maximum(per_step * m + b, 0), col), b=b))
                for b in range(-1, per_step)]

    return pl.pallas_call(
        body, name="attn_fwd", grid=(nb // per_step,),
        in_specs=[_rows(per_step * ATTN_BLOCK, qw), *kv_specs(kcol), *kv_specs(kcol + 1), _whole((1, N_Q_HEADS))],
        out_specs=[_rows(per_step * ATTN_BLOCK, qw),
                   pl.BlockSpec((per_step, all_rows, ATTN_BLOCK), lambda m: (m, 0, 0)),
                   _rows(per_step * ATTN_BLOCK, V7X_LANES)],
        out_shape=[SDS((T, qw), BF16), SDS((nb, all_rows, ATTN_BLOCK), BF16), SDS((T, V7X_LANES), F32)],
        compiler_params=_params("parallel"),
    )(*[qkv] * (2 * per_step + 3), sinks)


def _mm_res(name, a, w, b, res, g):
    T, K = a.shape
    D = w.shape[1]
    tm = min(512, T)

    def body(a_ref, w_ref, b_ref, r_ref, g_ref, o_ref, f_ref):
        h = jnp.dot(a_ref[...], w_ref[...], preferred_element_type=F32) + b_ref[...] + r_ref[...]
        o_ref[...] = h
        f_ref[...] = (h * _rms_rstd(h) * g_ref[...]).astype(BF16)

    return pl.pallas_call(
        body, name=name, grid=(T // tm,),
        in_specs=[_rows(tm, K), _whole((K, D)), _whole((1, D)), _rows(tm, D), _whole((1, D))],
        out_specs=[_rows(tm, D), _rows(tm, D)],
        out_shape=[SDS((T, D), F32), SDS((T, D), BF16)],
        compiler_params=_params("parallel"),
    )(a, w, b, res, g)


def _ffn_down(name, s, w2, layer, res):
    _, T, n = s.shape
    D = w2.shape[3]
    tm = min(512, T)

    def body(s_ref, w_ref, r_ref, o_ref):
        acc = r_ref[...]
        for j in range(N_CHIPS):
            acc = acc + jnp.dot(s_ref[j], w_ref[j], preferred_element_type=F32)
        o_ref[...] = acc

    return pl.pallas_call(
        body, name=name, grid=(T // tm,),
        in_specs=[pl.BlockSpec((N_CHIPS, tm, n), lambda i: (0, i, 0)),
                  pl.BlockSpec((N_CHIPS, None, n, D), lambda i: (0, layer, 0, 0)), _rows(tm, D)],
        out_specs=_rows(tm, D),
        out_shape=SDS((T, D), F32),
        compiler_params=_params("parallel"),
    )(s, w2, res)


def _ffn_up(name, f, w1, w3, layer, after=None):
    T, D = f.shape
    n = w1.shape[2]
    tm = min(1024, T)

    def body(f_ref, w1_ref, w3_ref, *rest):
        act_ref, gg_ref, s_ref = rest[-3:]
        ff = f_ref[...]
        g1 = _dot_nt(ff, w1_ref[...])
        g3 = _dot_nt(ff, w3_ref[...])
        act, dact = _silu_and_grad(g1)
        act_ref[...] = act.astype(BF16)
        gg_ref[...] = (g3 * dact).astype(BF16)
        s_ref[...] = (act * g3).astype(BF16)

    slab = pl.BlockSpec((None, tm, n), lambda j, i: (j, i, 0))
    wslab = pl.BlockSpec((None, None, n, D), lambda j, i: (j, layer, 0, 0))
    hidden = SDS((N_CHIPS, T, n), BF16)
    return pl.pallas_call(
        body, name=name, grid=(N_CHIPS, T // tm),
        in_specs=[pl.BlockSpec((tm, D), lambda j, i: (i, 0)), wslab, wslab]
        + ([] if after is None else [pl.BlockSpec(memory_space=pl.ANY)]),
        out_specs=[slab, slab, slab],
        out_shape=[hidden, hidden, hidden],
        compiler_params=_params("parallel", "parallel"),
    )(f, w1, w3, *([] if after is None else [after]))


def _glu(a, d):
    a = a.astype(F32)
    return a[:, :d] * jax.nn.sigmoid(a[:, d:])


def _conv_tile(T):
    return min(256, T)


def _fill_shifted(sh_ref, tc):
    n = tc + CONV_HALO - V7X_SUBLANES
    for r in range(1, V7X_SUBLANES):
        sh_ref[r, 0:n, :] = sh_ref[0, pl.ds(r, n), :]


def _depthwise_taps(sh_ref, w_ref, offsets, bias_ref, out_ref, tc):
    D = out_ref.shape[1]

    def chunk(i, carry):
        t0 = pl.multiple_of(i * CONV_ROW_CHUNK, CONV_ROW_CHUNK)
        for cb in range(D // CONV_LANE_CHUNK):
            cs = slice(cb * CONV_LANE_CHUNK, (cb + 1) * CONV_LANE_CHUNK)
            acc = jnp.zeros((CONV_ROW_CHUNK, CONV_LANE_CHUNK), F32)
            for r in range(V7X_SUBLANES):
                taps = [(j, o // V7X_SUBLANES) for j, o in enumerate(offsets) if o % V7X_SUBLANES == r]
                if not taps:
                    continue
                span = CONV_ROW_CHUNK + V7X_SUBLANES * max(q for _, q in taps)
                rows = sh_ref[r, pl.ds(t0, span), cs]
                for j, q in taps:
                    acc = acc + rows[V7X_SUBLANES * q:V7X_SUBLANES * q + CONV_ROW_CHUNK] * w_ref[j:j + 1, cs]
            if bias_ref is not None:
                acc = acc + bias_ref[:, cs]
            out_ref[pl.ds(t0, CONV_ROW_CHUNK), cs] = acc
        return carry

    lax.fori_loop(0, tc // CONV_ROW_CHUNK, chunk, 0)


def _depthwise_tap_grads(dy_sh, x_sh, offsets, dw_ref, tc):
    D = dw_ref.shape[1]
    for cb in range(D // V7X_LANES):
        cs = slice(cb * V7X_LANES, (cb + 1) * V7X_LANES)

        def row_tiles(i, accs, cs=cs):
            for k in range(CONV_GRAD_UNROLL):
                t0 = pl.multiple_of(i * (CONV_GRAD_UNROLL * V7X_SUBLANES), V7X_SUBLANES) + k * V7X_SUBLANES
                d = dy_sh[0, pl.ds(t0, V7X_SUBLANES), cs]
                accs = tuple(
                    acc + d * x_sh[o % V7X_SUBLANES, pl.ds(t0 + o // V7X_SUBLANES * V7X_SUBLANES, V7X_SUBLANES), cs]
                    for acc, o in zip(accs, offsets))
            return accs

        zero = jnp.zeros((V7X_SUBLANES, V7X_LANES), F32)
        accs = lax.fori_loop(0, tc // (CONV_GRAD_UNROLL * V7X_SUBLANES), row_tiles, tuple(zero for _ in offsets))
        for j, acc in enumerate(accs):
            dw_ref[j:j + 1, cs] += jnp.sum(acc, axis=0, keepdims=True)


def _conv_fwd(a, w_dw, b_dw, ln_g, ln_b, w_pw2, b_pw2, res, g_next):
    T = a.shape[0]
    D = a.shape[1] // 2
    tc = _conv_tile(T)
    per = tc // CONV_HALO

    def body(a_ref, ah_ref, w_ref, bdw_ref, lg_ref, lb_ref, wp_ref, bp_ref, r_ref, g_ref,
             c_ref, act_ref, h_ref, f_ref, u_sh):
        i = pl.program_id(0)
        u_sh[0, 0:CONV_HALO, :] = jnp.where(i > 0, _glu(ah_ref[...], D), 0.0)
        u_sh[0, CONV_HALO:, :] = _glu(a_ref[...], D)
        _fill_shifted(u_sh, tc)
        _depthwise_taps(u_sh, w_ref, [CONV_FIRST_TAP + j for j in range(CONV_WIDTH)], bdw_ref, c_ref, tc)
        c = c_ref[...]
        xc = c - jnp.mean(c, axis=-1, keepdims=True)
        z = xc * lax.rsqrt(jnp.mean(xc * xc, axis=-1, keepdims=True) + LN_EPS)
        l = z * lg_ref[...] + lb_ref[...]
        act = (l * jax.nn.sigmoid(l)).astype(BF16)
        act_ref[...] = act
        h = jnp.dot(act, wp_ref[...], preferred_element_type=F32) + bp_ref[...] + r_ref[...]
        h_ref[...] = h
        f_ref[...] = (h * _rms_rstd(h) * g_ref[...]).astype(BF16)

    return pl.pallas_call(
        body, name="conv_fwd", grid=(T // tc,),
        in_specs=[_rows(tc, 2 * D),
                  pl.BlockSpec((CONV_HALO, 2 * D), lambda i: (jnp.maximum(i * per - 1, 0), 0)),
                  _whole((CONV_WIDTH, D)), _whole((1, D)), _whole((1, D)), _whole((1, D)),
                  _whole((D, D)), _whole((1, D)), _rows(tc, D), _whole((1, D))],
        out_specs=[_rows(tc, D), _rows(tc, D), _rows(tc, D), _rows(tc, D)],
        out_shape=[SDS((T, D), F32), SDS((T, D), BF16), SDS((T, D), F32), SDS((T, D), BF16)],
        scratch_shapes=[pltpu.VMEM((V7X_SUBLANES, tc + CONV_HALO, D), F32)],
        compiler_params=_params("parallel"),
    )(a, a, w_dw, b_dw, ln_g, ln_b, w_pw2, b_pw2, res, g_next)


def _ffn_down_loss(name, s, w2, layer, res, g, target):
    _, T, n = s.shape
    D = w2.shape[3]
    tm = min(512, T)

    def body(s_ref, w_ref, r_ref, g_ref, t_ref, dh_ref, loss_ref, dg_ref):
        @pl.when(pl.program_id(0) == 0)
        def _():
            loss_ref[...] = jnp.zeros_like(loss_ref)
            dg_ref[...] = jnp.zeros_like(dg_ref)

        hh = r_ref[...]
        for j in range(N_CHIPS):
            hh = hh + jnp.dot(s_ref[j], w_ref[j], preferred_element_type=F32)
        r = _rms_rstd(hh)
        g = g_ref[...]
        d = hh * r * g - t_ref[...]
        loss_ref[...] += 0.5 * jnp.sum(jnp.mean(d * d, axis=-1, keepdims=True), axis=0, keepdims=True)
        dout = d * (1.0 / D)
        dg_ref[...] += jnp.sum(dout * (hh * r), axis=0, keepdims=True)
        dxh = dout * g
        dh_ref[...] = r * dxh - hh * (r * r * r) * jnp.mean(dxh * hh, axis=-1, keepdims=True)

    return pl.pallas_call(
        body, name=name, grid=(T // tm,),
        in_specs=[pl.BlockSpec((N_CHIPS, tm, n), lambda i: (0, i, 0)),
                  pl.BlockSpec((N_CHIPS, None, n, D), lambda i: (0, layer, 0, 0)), _rows(tm, D),
                  _whole((1, D)), _rows(tm, D)],
        out_specs=[_rows(tm, D), _whole((1, 1)), _whole((1, D))],
        out_shape=[SDS((T, D), F32), SDS((1, 1), F32), SDS((1, D), F32)],
        compiler_params=_params("arbitrary"),
    )(s, w2, res, g, target)


def _ffn_bwd_down(name, dh, w2, layer, act, gate_grad, s, after=None):
    T, D = dh.shape
    n = w2.shape[2]
    tm = min(256, T)

    def body(dh_ref, w2_ref, act_ref, gg_ref, s_ref, *rest):
        dg1_ref, dg3_ref, dw_ref = rest[-3:]

        @pl.when(pl.program_id(0) == 0)
        def _():
            dw_ref[...] = jnp.zeros_like(dw_ref)

        dhb = dh_ref[...].astype(BF16)
        for j in range(N_CHIPS):
            ds = _dot_nt(dhb, w2_ref[j])
            dg1_ref[j] = (ds * gg_ref[j].astype(F32)).astype(BF16)
            dg3_ref[j] = (ds * act_ref[j].astype(F32)).astype(BF16)
            dw_ref[j] += _dot_tn(s_ref[j], dhb)

    slabs = pl.BlockSpec((N_CHIPS, tm, n), lambda i: (0, i, 0))
    hidden = SDS((N_CHIPS, T, n), BF16)
    return pl.pallas_call(
        body, name=name, grid=(T // tm,),
        in_specs=[_rows(tm, D),
                  pl.BlockSpec((N_CHIPS, None, n, D), lambda i: (0, layer, 0, 0), pipeline_mode=pl.Buffered(1)),
                  slabs, slabs, slabs] + ([] if after is None else [pl.BlockSpec(memory_space=pl.ANY)]),
        out_specs=[slabs, slabs, _whole((N_CHIPS, n, D))],
        out_shape=[hidden, hidden, SDS((N_CHIPS, n, D), F32)],
        compiler_params=_params("arbitrary"),
    )(dh, w2, act, gate_grad, s, *([] if after is None else [after]))


def _dot_tn(a, b):
    return lax.dot_general(a.astype(BF16), b.astype(BF16), (((0,), (0,)), ((), ())), preferred_element_type=F32)


def _dot_nt(a, b):
    return lax.dot_general(a.astype(BF16), b, (((1,), (1,)), ((), ())), preferred_element_type=F32)


def _mm_tn(name, a, b, col_chunks=1, after=None):
    a_slabs, b_slabs = a.ndim == 3, b.ndim == 3
    T = a.shape[-2]
    tt = min(1024, T)
    ka, nb = a.shape[-1], b.shape[-1]
    if a_slabs or b_slabs:
        out_dims = (N_CHIPS, ka, nb)
    elif col_chunks > 1:
        out_dims = (col_chunks, ka, nb // col_chunks)
    else:
        out_dims = (ka, nb)

    def body(a_ref, b_ref, *rest):
        o_ref = rest[-1]

        @pl.when(pl.program_id(0) == 0)
        def _():
            o_ref[...] = jnp.zeros_like(o_ref)

        if a_slabs:
            bb = b_ref[...].astype(BF16)
            for j in range(N_CHIPS):
                o_ref[j] += _dot_tn(a_ref[j], bb)
        elif b_slabs:
            aa = a_ref[...].astype(BF16)
            for j in range(N_CHIPS):
                o_ref[j] += _dot_tn(aa, b_ref[j])
        elif col_chunks > 1:
            aa = a_ref[...].astype(BF16)
            w = nb // col_chunks
            for j in range(col_chunks):
                o_ref[j] += _dot_tn(aa, b_ref[:, j * w:(j + 1) * w])
        else:
            o_ref[...] += _dot_tn(a_ref[...], b_ref[...])

    def spec(arr, slabs):
        if slabs:
            return pl.BlockSpec((N_CHIPS, tt, arr.shape[-1]), lambda t: (0, t, 0))
        return _rows(tt, arr.shape[-1])

    return pl.pallas_call(
        body, name=name, grid=(T // tt,),
        in_specs=[spec(a, a_slabs), spec(b, b_slabs)] + ([] if after is None else [pl.BlockSpec(memory_space=pl.ANY)]),
        out_specs=_whole(out_dims),
        out_shape=SDS(out_dims, F32),
        compiler_params=_params("arbitrary"),
    )(a, b, *([] if after is None else [after]))


def _mm_nt_normbwd(name, pairs, h, g, dh, after):
    T, D = h.shape
    tm = min(512, T)
    n_pairs = len(pairs)
    kinds = ["slabs" if dy.ndim == 3 else ("quarters" if w.ndim == 3 else "plain") for dy, w, _ in pairs]

    def body(*refs):
        dy_refs = refs[:n_pairs]
        w_refs = refs[n_pairs:2 * n_pairs]
        h_ref, g_ref, dh_ref, _, o_ref, dg_ref, cs_ref = refs[2 * n_pairs:]

        @pl.when(pl.program_id(0) == 0)
        def _():
            dg_ref[...] = jnp.zeros_like(dg_ref)
            cs_ref[...] = jnp.zeros_like(cs_ref)

        df = jnp.zeros((tm, D), F32)
        for dy_ref, w_ref, kd in zip(dy_refs, w_refs, kinds):
            if kd == "slabs":
                for j in range(N_CHIPS):
                    df = df + jnp.dot(dy_ref[j], w_ref[j], preferred_element_type=F32)
            elif kd == "quarters":
                n = w_ref.shape[2]
                for j in range(N_CHIPS):
                    df = df + _dot_nt(dy_ref[:, j * n:(j + 1) * n], w_ref[j])
            else:
                df = df + jnp.dot(dy_ref[...], w_ref[...], preferred_element_type=F32)
        hh = h_ref[...]
        r = _rms_rstd(hh)
        dg_ref[...] += jnp.sum(df * (hh * r), axis=0, keepdims=True)
        dxh = df * g_ref[...]
        out = dh_ref[...] + (r * dxh - hh * (r * r * r) * jnp.mean(dxh * hh, axis=-1, keepdims=True))
        o_ref[...] = out
        cs_ref[...] += jnp.sum(out, axis=0, keepdims=True)

    dy_specs, w_specs = [], []
    for (dy, w, layer), kd in zip(pairs, kinds):
        if kd == "slabs":
            dy_specs.append(pl.BlockSpec((N_CHIPS, tm, dy.shape[2]), lambda i: (0, i, 0)))
            w_specs.append(pl.BlockSpec((N_CHIPS, None, w.shape[2], D),
                                        functools.partial(lambda i, layer: (0, layer, 0, 0), layer=layer),
                                        pipeline_mode=pl.Buffered(1)))
        else:
            dy_specs.append(_rows(tm, dy.shape[1]))
            w_specs.append(_whole(w.shape))

    return pl.pallas_call(
        body, name=name, grid=(T // tm,),
        in_specs=[*dy_specs, *w_specs, _rows(tm, D), _whole((1, D)), _rows(tm, D), pl.BlockSpec(memory_space=pl.ANY)],
        out_specs=[_rows(tm, D), _whole((1, D)), _whole((1, D))],
        out_shape=[SDS((T, D), F32), SDS((1, D), F32), SDS((1, D), F32)],
        compiler_params=_params("arbitrary"),
    )(*[dy for dy, _, _ in pairs], *[w for _, w, _ in pairs], h, g, dh, after)


def _mm_nt(name, dy, w, out_dtype):
    T, N = dy.shape
    K = w.shape[0]
    tm = min(512, T)

    def body(dy_ref, w_ref, o_ref):
        o_ref[...] = lax.dot_general(dy_ref[...].astype(BF16), w_ref[...], (((1,), (1,)), ((), ())),
                                     preferred_element_type=F32).astype(out_dtype)

    return pl.pallas_call(
        body, name=name, grid=(T // tm,),
        in_specs=[_rows(tm, N), _whole((K, N))],
        out_specs=_rows(tm, K),
        out_shape=SDS((T, K), out_dtype),
        compiler_params=_params("parallel"),
    )(dy, w)


def _conv_bwd(dh, w_pw2, c, a, w_dw, ln_g, ln_b):
    T, D = c.shape
    tc = _conv_tile(T)
    per = tc // CONV_HALO
    n_tiles = T // tc
    last_halo = T // CONV_HALO - 1

    def ln_bwd(dact_v, c_v, lg, lb):
        xc = c_v - jnp.mean(c_v, axis=-1, keepdims=True)
        rstd = lax.rsqrt(jnp.mean(xc * xc, axis=-1, keepdims=True) + LN_EPS)
        z = xc * rstd
        _, dsilu = _silu_and_grad(z * lg + lb)
        dl = dact_v * dsilu
        dz = dl * lg
        dc = rstd * (dz - jnp.mean(dz, axis=-1, keepdims=True) - z * jnp.mean(dz * z, axis=-1, keepdims=True))
        return dc, dl, z

    def body(dh_ref, dhn_ref, wp_ref, c_ref, cn_ref, a_ref, ah_ref, w_ref, lg_ref, lb_ref,
             da_ref, dlg_ref, dlb_ref, dbdw_ref, dwdw_ref, dbpw1_ref, dc_sh, u_sh, du_scr):
        i = pl.program_id(0)

        @pl.when(i == 0)
        def _():
            for ref in (dlg_ref, dlb_ref, dbdw_ref, dwdw_ref, dbpw1_ref):
                ref[...] = jnp.zeros_like(ref)

        lg, lb = lg_ref[...], lb_ref[...]
        dh_rows = jnp.concatenate([dh_ref[...].astype(BF16), dhn_ref[...].astype(BF16)], axis=0)
        dact = _dot_nt(dh_rows, wp_ref[...])
        dc, dl, z = ln_bwd(dact[:tc], c_ref[...], lg, lb)
        dlg_ref[...] += jnp.sum(dl * z, axis=0, keepdims=True)
        dlb_ref[...] += jnp.sum(dl, axis=0, keepdims=True)
        dbdw_ref[...] += jnp.sum(dc, axis=0, keepdims=True)
        dcn, _, _ = ln_bwd(dact[tc:], cn_ref[...], lg, lb)
        dc_sh[0, 0:tc, :] = dc
        dc_sh[0, tc:, :] = jnp.where(i < n_tiles - 1, dcn, 0.0)
        _fill_shifted(dc_sh, tc)

        a_v = a_ref[...].astype(F32)
        a1 = a_v[:, :D]
        sg = jax.nn.sigmoid(a_v[:, D:])
        u_sh[0, 0:CONV_HALO, :] = jnp.where(i > 0, _glu(ah_ref[...], D), 0.0)
        u_sh[0, CONV_HALO:, :] = a1 * sg
        _fill_shifted(u_sh, tc)

        _depthwise_taps(dc_sh, w_ref, [CONV_WIDTH - 1 - j for j in range(CONV_WIDTH)], None, du_scr, tc)
        _depthwise_tap_grads(dc_sh, u_sh, [CONV_FIRST_TAP + j for j in range(CONV_WIDTH)], dwdw_ref, tc)

        du = du_scr[...]
        da1 = du * sg
        da2 = du * a1 * sg * (1.0 - sg)
        da_ref[:, :D] = da1.astype(BF16)
        da_ref[:, D:] = da2.astype(BF16)
        dbpw1_ref[:, :D] += jnp.sum(da1, axis=0, keepdims=True)
        dbpw1_ref[:, D:] += jnp.sum(da2, axis=0, keepdims=True)

    nxt = lambda i: (jnp.minimum((i + 1) * per, last_halo), 0)
    return pl.pallas_call(
        body, name="conv_bwd", grid=(n_tiles,),
        in_specs=[_rows(tc, D), pl.BlockSpec((CONV_HALO, D), nxt), _whole((D, D)),
                  _rows(tc, D), pl.BlockSpec((CONV_HALO, D), nxt),
                  _rows(tc, 2 * D),
                  pl.BlockSpec((CONV_HALO, 2 * D), lambda i: (jnp.maximum(i * per - 1, 0), 0)),
                  _whole((CONV_WIDTH, D)), _whole((1, D)), _whole((1, D))],
        out_specs=[_rows(tc, 2 * D), _whole((1, D)), _whole((1, D)), _whole((1, D)),
                   _whole((CONV_HALO, D)), _whole((1, 2 * D))],
        out_shape=[SDS((T, 2 * D), BF16), SDS((1, D), F32), SDS((1, D), F32), SDS((1, D), F32),
                   SDS((CONV_HALO, D), F32), SDS((1, 2 * D), F32)],
        scratch_shapes=[pltpu.VMEM((V7X_SUBLANES, tc + CONV_HALO, D), F32),
                        pltpu.VMEM((V7X_SUBLANES, tc + CONV_HALO, D), F32), pltpu.VMEM((tc, D), F32)],
        compiler_params=_params("arbitrary"),
    )(dh, dh, w_pw2, c, c, a, a, w_dw, ln_g, ln_b)


def _attn_bwd(qkv, dao, cos, sin, probs_saved, psink_saved):
    T = qkv.shape[0]
    nb = T // ATTN_BLOCK
    qw = N_Q_HEADS * HEAD_DIM
    kw = N_KV_HEADS * HEAD_DIM

    def body(q_ref, kc_ref, kp_ref, vc_ref, vp_ref, do_ref, cos_ref, sin_ref, cosp_ref, sinp_ref, probs_ref, psink_ref,
             dq_ref, dkv_ref, dsink_ref, dbq_ref, dbkv_ref, carry, prev_scr, cur_scr, dq_scr):
        n = pl.program_id(0)

        @pl.when(n == 0)
        def _():
            for ref in (dsink_ref, dbq_ref, dbkv_ref, carry):
                ref[...] = jnp.zeros_like(ref)

        @pl.when(n == nb)
        def _():
            prev_scr[...] = jnp.zeros_like(prev_scr)

        @pl.when(n < nb)
        def _():
            prev_part = _from_previous_block(N_Q_HEADS * ATTN_BLOCK)
            half = PAIRS_PER_KV * ATTN_BLOCK
            upper = _upper_lanes((ATTN_BLOCK, 2 * HEAD_DIM))
            groups = range(N_KV_HEADS)

            def nt(a, b):
                return lax.dot_general(a, b, (((1,), (1,)), ((), ())), preferred_element_type=F32)

            def kv_grad(even_rows, odd_rows, x):
                even = lax.dot_general(even_rows, x, (((0,), (0,)), ((), ())), preferred_element_type=F32)
                odd = lax.dot_general(odd_rows, x, (((0,), (0,)), ((), ())), preferred_element_type=F32)
                t = jnp.where(upper, odd, even)
                return t + _swap_lane_halves(t)

            q = [_pair_rows(q_ref, g) for g in groups]
            do = [_pair_rows(do_ref, g) for g in groups]
            k_prev = [_kv_operands(g, kp_ref[...]) for g in groups]
            k_cur = [_kv_operands(g, kc_ref[...]) for g in groups]
            v_prev = [_kv_operands(g, vp_ref[...]) for g in groups]
            v_cur = [_kv_operands(g, vc_ref[...]) for g in groups]
            probs = probs_ref[...].astype(F32)
            dp_prev = jnp.concatenate([nt(do[g], v_prev[g][i]) for g in groups for i in range(2)], axis=0)
            dp_cur = jnp.concatenate([nt(do[g], v_cur[g][i]) for g in groups for i in range(2)], axis=0)
            dp = jnp.where(prev_part, dp_prev, dp_cur)
            delta = jnp.sum(probs * dp, axis=1, keepdims=True)
            ds_prev, ds_cur = _split_folded(probs * (dp - delta) * (HEAD_DIM ** -0.5), prev_part)
            p_prev, p_cur = _split_folded(probs_ref[...], prev_part)
            for i, h in enumerate(_all_heads()):
                rows = slice(i * ATTN_BLOCK, (i + 1) * ATTN_BLOCK)
                dsink_ref[:, h:h + 1] += jnp.sum(-(psink_ref[:, h:h + 1] * delta[rows]), axis=0, keepdims=True)
            kv_grads = []
            for g in groups:
                even, odd = slice(2 * g * half, (2 * g + 1) * half), slice((2 * g + 1) * half, (2 * g + 2) * half)
                dq = (jnp.dot(ds_prev[even], k_prev[g][0], preferred_element_type=F32)
                      + jnp.dot(ds_cur[even], k_cur[g][0], preferred_element_type=F32)
                      + jnp.dot(ds_prev[odd], k_prev[g][1], preferred_element_type=F32)
                      + jnp.dot(ds_cur[odd], k_cur[g][1], preferred_element_type=F32))
                for i in range(PAIRS_PER_KV):
                    hp = g * PAIRS_PER_KV + i
                    dq_scr[:, hp * 2 * HEAD_DIM:(hp + 1) * 2 * HEAD_DIM] = dq[i * ATTN_BLOCK:(i + 1) * ATTN_BLOCK]
                kv_grads.append((kv_grad(ds_prev[even], ds_prev[odd], q[g]), kv_grad(ds_cur[even], ds_cur[odd], q[g]),
                                 kv_grad(p_prev[even], p_prev[odd], do[g]), kv_grad(p_cur[even], p_cur[odd], do[g])))
            (dkp0, dkc0, dvp0, dvc0), (dkp1, dkc1, dvp1, dvc1) = kv_grads
            prev_scr[:, :kw] = jnp.where(upper, dkp1, dkp0)
            prev_scr[:, kw:] = jnp.where(upper, dvp1, dvp0)
            cur_scr[:, :kw] = jnp.where(upper, dkc1, dkc0)
            cur_scr[:, kw:] = jnp.where(upper, dvc1, dvc0)
            dq_pre = _rope_transposed(dq_scr[...], cos_ref, sin_ref)
            dq_ref[...] = dq_pre.astype(BF16)
            dbq_ref[...] += jnp.sum(dq_pre, axis=0, keepdims=True)

        tot = carry[...] + prev_scr[...]
        dk_pre = _rope_transposed(tot[:, :kw], cosp_ref, sinp_ref)
        dkv_ref[:, :kw] = dk_pre.astype(BF16)
        dkv_ref[:, kw:] = tot[:, kw:].astype(BF16)
        dbkv_ref[:, :kw] += jnp.sum(dk_pre, axis=0, keepdims=True)
        dbkv_ref[:, kw:] += jnp.sum(tot[:, kw:], axis=0, keepdims=True)

        @pl.when(n < nb)
        def _():
            carry[...] = cur_scr[...]

    cur = lambda n: (jnp.minimum(n, nb - 1), 0)
    out_lag = lambda n: (jnp.maximum(n - 1, 0), 0)
    return pl.pallas_call(
        body, name="attn_bwd", grid=(nb + 1,),
        in_specs=[*_attn_specs(T),
                  pl.BlockSpec((ATTN_BLOCK, qw), cur),
                  pl.BlockSpec((ATTN_BLOCK, V7X_LANES), cur), pl.BlockSpec((ATTN_BLOCK, V7X_LANES), cur),
                  pl.BlockSpec((ATTN_BLOCK, V7X_LANES), out_lag), pl.BlockSpec((ATTN_BLOCK, V7X_LANES), out_lag),
                  pl.BlockSpec((None, N_Q_HEADS * ATTN_BLOCK, ATTN_BLOCK), lambda n: (jnp.minimum(n, nb - 1), 0, 0)),
                  pl.BlockSpec((ATTN_BLOCK, V7X_LANES), cur)],
        out_specs=[pl.BlockSpec((ATTN_BLOCK, qw), cur), pl.BlockSpec((ATTN_BLOCK, 2 * kw), out_lag),
                   _whole((1, N_Q_HEADS)), _whole((1, qw)), _whole((1, 2 * kw))],
        out_shape=[SDS((T, qw), BF16), SDS((T, 2 * kw), BF16),
                   SDS((1, N_Q_HEADS), F32), SDS((1, qw), F32), SDS((1, 2 * kw), F32)],
        scratch_shapes=[pltpu.VMEM((ATTN_BLOCK, 2 * kw), F32), pltpu.VMEM((ATTN_BLOCK, 2 * kw), F32),
                        pltpu.VMEM((ATTN_BLOCK, 2 * kw), F32), pltpu.VMEM((ATTN_BLOCK, qw), F32)],
        compiler_params=_params("arbitrary"),
    )(qkv, qkv, qkv, qkv, qkv, dao, cos, sin, cos, sin, probs_saved, psink_saved)


def _local_step(x, target, p, reduce_begin, reduce_send):
    T, D = x.shape
    cos, sin = _rope_tables(T)
    qw = N_Q_HEADS * HEAD_DIM
    nm, nf = p["norm_mix"], p["norm_ffn"]

    y0, qkv = _qkv_proj(x, nm[0:1], p["attn_w_qkv"], p["attn_b_qkv"], cos, sin, p["gather_started"])
    ao, attn_probs, sink_probs = _attn_fwd(qkv, p["attn_sinks"])
    h1, f0 = _mm_res("attn_out", ao, p["attn_w_o"], p["attn_b_o"], x, nf[0:1])
    p = {**p, **p["other_weights"](h1)}
    w1, w3 = p["ffn_w1"], p["ffn_w3"]
    act0, gg0, s0 = _ffn_up("ffn0_up", f0, w1, w3, 0, after=p["swap_started"])
    p = {**p, **p["rest_of_weights"](s0)}
    w2 = p["ffn_w2"]
    h2 = _ffn_down("ffn0_down", s0, w2, 0, h1)
    y1, a = _pw1_proj(h2, nm[1:2], p["conv_w_pw1"], p["conv_b_pw1"])
    c, act, h3, f1 = _conv_fwd(a, p["conv_w_dw"], p["conv_b_dw"], p["conv_ln_g"], p["conv_ln_b"],
                               p["conv_w_pw2"], p["conv_b_pw2"], h2, nf[1:2])
    act1, gg1, s1 = _ffn_up("ffn1_up", f1, w1, w3, 1)
    dh4, loss, d_norm_final = _ffn_down_loss("ffn1_down_loss", s1, w2, 1, h3, p["norm_final"], target)

    g = {}
    dg1, dg3, dw2_1 = _ffn_bwd_down("ffn1_bwd_down", dh4, w2, 1, act1, gg1, s1)
    dw1_1 = _mm_tn("ffn1_dw1", dg1, f1)
    dw3_1 = _mm_tn("ffn1_dw3", dg3, f1)
    begun = reduce_begin("ffn1", {("ffn_w1", 1): dw1_1, ("ffn_w3", 1): dw3_1, ("ffn_w2", 1): dw2_1})
    dh3, dnf1, db_pw2 = _mm_nt_normbwd("ffn1_bwd_in", [(dg1, w1, 1), (dg3, w3, 1)], h3, nf[1:2], dh4, begun)
    sent = reduce_send("ffn1", dh3)

    dw_pw2 = _mm_tn("conv_dw_pw2", act, dh3, after=sent)
    da, d_ln_g, d_ln_b, d_b_dw, d_w_dw, d_b_pw1 = _conv_bwd(dh3, p["conv_w_pw2"], c, a, p["conv_w_dw"],
                                                            p["conv_ln_g"], p["conv_ln_b"])
    dw_pw1 = _mm_tn("conv_dw_pw1", y1, da, col_chunks=N_CHIPS)
    begun = reduce_begin("conv", {("conv_w_pw2", 0): dw_pw2.reshape(N_CHIPS, -1, D), ("conv_w_pw1", 0): dw_pw1})
    dh2, dnm1, _ = _mm_nt_normbwd("conv_bwd_in", [(da, p["conv_w_pw1"], None)], h2, nm[1:2], dh3, begun)
    sent = reduce_send("conv", dh2)

    dg1, dg3, dw2_0 = _ffn_bwd_down("ffn0_bwd_down", dh2, w2, 0, act0, gg0, s0, after=sent)
    dw1_0 = _mm_tn("ffn0_dw1", dg1, f0)
    dw3_0 = _mm_tn("ffn0_dw3", dg3, f0)
    begun = reduce_begin("ffn0", {("ffn_w1", 0): dw1_0, ("ffn_w3", 0): dw3_0, ("ffn_w2", 0): dw2_0})
    dh1, dnf0, db_o = _mm_nt_normbwd("ffn0_bwd_in", [(dg1, w1, 0), (dg3, w3, 0)], h1, nf[0:1], dh2, begun)
    sent = reduce_send("ffn0", dh1)

    dw_o = _mm_tn("attn_dw_o", ao, dh1, after=sent)
    dao = _mm_nt("attn_bwd_out", dh1, p["attn_w_o"], BF16)
    dq, dkv, d_sinks, dbq, dbkv = _attn_bwd(qkv, dao, cos, sin, attn_probs, sink_probs)
    dwq = _mm_tn("attn_dw_q", dq, y0)
    dwkv = _mm_tn("attn_dw_kv", dkv, y0)
    wqkv = p["attn_w_qkv"]
    dwqkv = jnp.concatenate([dwq, dwkv], axis=0).reshape(N_CHIPS, -1, D)
    begun = reduce_begin("attn", {("attn_w_o", 0): dw_o.reshape(N_CHIPS, -1, D), ("attn_w_qkv", 0): dwqkv})
    sent = reduce_send("attn", begun)
    dx, dnm0, _ = _mm_nt_normbwd("attn_bwd_in", [(dq, wqkv[:qw], None), (dkv, wqkv[qw:], None)], x, nm[0:1], dh1,
                                 sent)

    g["norm_mix"] = jnp.concatenate([dnm0, dnm1], axis=0)
    g["norm_ffn"] = jnp.concatenate([dnf0, dnf1], axis=0)
    g["attn_b_qkv"] = jnp.concatenate([dbq, dbkv], axis=1)
    g["attn_sinks"] = d_sinks
    g["attn_b_o"] = db_o
    g["conv_b_pw1"] = d_b_pw1
    g["conv_w_dw"] = d_w_dw[:CONV_WIDTH]
    g["conv_b_dw"] = d_b_dw
    g["conv_ln_g"] = d_ln_g
    g["conv_ln_b"] = d_ln_b
    g["conv_b_pw2"] = db_pw2
    g["norm_final"] = d_norm_final
    return loss, dx, g


ANY = pl.BlockSpec(memory_space=pl.ANY)
VMEM_WHOLE = pl.BlockSpec(memory_space=pltpu.VMEM)


def _my_place():
    return lax.axis_index("x"), lax.axis_index("y"), lax.axis_index("c")


def _other_chips(x, y):
    places = [(1 - x, y), (x, 1 - y), (1 - x, 1 - y)]
    return [(bx, by, 2 * bx + by) for bx, by in places]


def _cast_into_slot(name, gathered, shard, chip_idx, after=None):
    rows, cols = shard.shape
    tr = _pack_row_tile(rows)

    def body(k_ref, s_ref, *rest):
        rest[-1][...] = s_ref[...].astype(BF16)

    return pl.pallas_call(
        body, name=name,
        grid_spec=pltpu.PrefetchScalarGridSpec(
            num_scalar_prefetch=1, grid=(rows // tr,),
            in_specs=[pl.BlockSpec((tr, cols), lambda i, k_ref: (i, 0)), pl.BlockSpec(memory_space=pl.ANY)]
            + ([] if after is None else [pl.BlockSpec(memory_space=pl.ANY)]),
            out_specs=pl.BlockSpec((None, tr, cols), lambda i, k_ref: (k_ref[0], i, 0))),
        out_shape=SDS(gathered.shape, BF16),
        input_output_aliases={2: 0},
        compiler_params=_params("parallel"),
    )(chip_idx, shard, gathered, *([] if after is None else [after]))


def _row_halves(ref, c):
    half = ref.shape[1] // 2
    return pl.ds(pl.multiple_of(c * half, 16), half), pl.ds(pl.multiple_of((1 - c) * half, 16), half)


def _gather_ici_copies(refs, send_sems, recv_sems):
    x, y, c = _my_place()
    k = 2 * x + y
    pairs = []
    for i, ref in enumerate(refs):
        mine, _ = _row_halves(ref, c)
        for j, (bx, by, kb) in enumerate(_other_chips(x, y)):
            sems = dict(send_sem=send_sems.at[3 * i + j], recv_sem=recv_sems.at[3 * i + j], device_id_type=MESH)
            send = pltpu.make_async_remote_copy(src_ref=ref.at[k, mine], dst_ref=ref.at[k, mine],
                                                device_id=(bx, by, c), **sems)
            arrival = pltpu.make_async_remote_copy(src_ref=ref.at[kb, mine], dst_ref=ref.at[kb, mine],
                                                   device_id=(bx, by, c), **sems)
            pairs.append((send, arrival))
    return pairs


def _gather_d2d_copies(refs, send_sems, recv_sems):
    x, y, c = _my_place()
    pairs = []
    for i, ref in enumerate(refs):
        mine, theirs = _row_halves(ref, c)
        for j, (_, _, kb) in enumerate(_other_chips(x, y)):
            sem = 3 * i + j
            sems = dict(send_sem=send_sems.at[sem], recv_sem=recv_sems.at[sem], device_id=(x, y, 1 - c),
                        device_id_type=MESH)
            send = pltpu.make_async_remote_copy(src_ref=ref.at[kb, mine], dst_ref=ref.at[kb, mine], **sems)
            arrival = pltpu.make_async_remote_copy(src_ref=ref.at[kb, theirs], dst_ref=ref.at[kb, theirs], **sems)
            pairs.append((send, arrival))
    return pairs


def _run_copies(pairs):
    for send, _ in pairs:
        send.start()
    for send, arrival in pairs:
        send.wait_send()
        arrival.wait_recv()


def _gather_stage_copies(stage, refs, send_sems, recv_sems):
    if stage == "ici":
        return _gather_ici_copies(refs, send_sems, recv_sems)
    return _gather_d2d_copies(refs, send_sems, recv_sems)


def _gather_start(name, gathered, after, stage="ici"):
    n_w = len(gathered)

    def body(*refs):
        in_refs = refs[:n_w]
        send_sems, recv_sems = refs[n_w + 1:n_w + 3]
        for send, _ in _gather_stage_copies(stage, in_refs, send_sems, recv_sems):
            send.start()
        refs[-1][...] = jnp.zeros_like(refs[-1])

    out = pl.pallas_call(
        body, name=name,
        out_shape=(pltpu.SemaphoreType.DMA((3 * n_w,)), pltpu.SemaphoreType.DMA((3 * n_w,)),
                   *[pltpu.HBM(g.shape, g.dtype) for g in gathered], SDS((8, V7X_LANES), F32)),
        in_specs=[*[HBM_SPEC] * n_w, ANY], out_specs=(SEM_SPEC, SEM_SPEC, *[HBM_SPEC] * n_w, VMEM_WHOLE),
        input_output_aliases={i: 2 + i for i in range(n_w)},
        compiler_params=pltpu.CompilerParams(has_side_effects=DATAFLOW),
    )(*[pltpu.with_memory_space_constraint(g, pltpu.HBM) for g in gathered], after)
    return out[0], out[1], list(out[2:2 + n_w]), out[-1]


def _gather_wait(name, send_sems, recv_sems, gathered, after, stage="ici"):
    n_w = len(gathered)

    def body(*refs):
        in_refs = refs[:n_w]
        send_sems, recv_sems = refs[n_w:n_w + 2]
        for send, arrival in _gather_stage_copies(stage, in_refs, send_sems, recv_sems):
            send.wait_send()
            arrival.wait_recv()

    out = pl.pallas_call(
        body, name=name, out_shape=tuple(pltpu.HBM(g.shape, g.dtype) for g in gathered),
        in_specs=[*[HBM_SPEC] * n_w, SEM_SPEC, SEM_SPEC, ANY], out_specs=tuple([HBM_SPEC] * n_w),
        input_output_aliases={i: i for i in range(n_w)},
        compiler_params=pltpu.CompilerParams(has_side_effects=DATAFLOW),
    )(*gathered, send_sems, recv_sems, after)
    return list(out)


def _swap_fetched_with_sibling(name, gathered):
    n_w = len(gathered)

    def body(*refs):
        in_refs = refs[:n_w]
        send_sems, recv_sems = refs[2 * n_w:]
        _run_copies(_gather_d2d_copies(in_refs, send_sems, recv_sems))

    return pl.pallas_call(
        body, name=name, out_shape=[SDS(g.shape, g.dtype) for g in gathered],
        in_specs=[ANY] * n_w, out_specs=[ANY] * n_w, input_output_aliases={i: i for i in range(n_w)},
        scratch_shapes=[pltpu.SemaphoreType.DMA((3 * n_w,)), pltpu.SemaphoreType.DMA((3 * n_w,))],
    )(*gathered)


def _sibling_swap_copies(g_refs, land_refs, send_sems, recv_sems):
    x, y, c = _my_place()
    copies = []
    for i, g_ref in enumerate(g_refs):
        half = g_ref.shape[1] // 2
        theirs = pl.ds(pl.multiple_of((1 - c) * half, 8), half)
        copies.append(pltpu.make_async_remote_copy(
            src_ref=g_ref.at[:, theirs], dst_ref=land_refs[i], send_sem=send_sems.at[i], recv_sem=recv_sems.at[i],
            device_id=(x, y, 1 - c), device_id_type=MESH))
    return copies


def _sibling_swap_start(name, grads):
    n_g = len(grads)

    def body(*refs):
        g_refs, land_refs = refs[:n_g], refs[n_g:2 * n_g]
        send_sems, recv_sems = refs[2 * n_g:2 * n_g + 2]
        for cp in _sibling_swap_copies(g_refs, land_refs, send_sems, recv_sems):
            cp.start()
        refs[-1][...] = jnp.zeros_like(refs[-1])

    lands = [pltpu.with_memory_space_constraint(lax.empty((g.shape[0], g.shape[1] // 2, g.shape[2]), g.dtype),
                                                pltpu.HBM) for g in grads]
    out = pl.pallas_call(
        body, name=name,
        out_shape=(pltpu.SemaphoreType.DMA((n_g,)), pltpu.SemaphoreType.DMA((n_g,)),
                   *[pltpu.HBM(g.shape, g.dtype) for g in grads], *[pltpu.HBM(l.shape, l.dtype) for l in lands],
                   SDS((8, V7X_LANES), F32)),
        in_specs=[HBM_SPEC] * (2 * n_g), out_specs=(SEM_SPEC, SEM_SPEC, *[HBM_SPEC] * (2 * n_g), VMEM_WHOLE),
        input_output_aliases={i: 2 + i for i in range(2 * n_g)},
        compiler_params=pltpu.CompilerParams(has_side_effects=DATAFLOW),
    )(*[pltpu.with_memory_space_constraint(g, pltpu.HBM) for g in grads], *lands)
    return out[0], out[1], list(out[2:2 + n_g]), list(out[2 + n_g:2 + 2 * n_g]), out[-1]


def _sibling_swap_wait(name, send_sems, recv_sems, grads, lands, after):
    n_g = len(grads)

    def body(*refs):
        g_refs, land_refs = refs[:n_g], refs[n_g:2 * n_g]
        send_sems, recv_sems = refs[2 * n_g:2 * n_g + 2]
        for cp in _sibling_swap_copies(g_refs, land_refs, send_sems, recv_sems):
            cp.wait_send()
            cp.wait_recv()

    out = pl.pallas_call(
        body, name=name,
        out_shape=(*[pltpu.HBM(g.shape, g.dtype) for g in grads], *[pltpu.HBM(l.shape, l.dtype) for l in lands]),
        in_specs=[*[HBM_SPEC] * (2 * n_g), SEM_SPEC, SEM_SPEC, ANY], out_specs=tuple([HBM_SPEC] * (2 * n_g)),
        input_output_aliases={i: i for i in range(2 * n_g)},
        compiler_params=pltpu.CompilerParams(has_side_effects=DATAFLOW),
    )(*grads, *lands, send_sems, recv_sems, after)
    return list(out[:n_g]), list(out[n_g:])


def _pack_row_tile(rows):
    for t in range(min(rows, 512), 7, -1):
        if rows % t == 0 and t % 8 == 0:
            return t
    return rows


def _add_sibling_half(name, grads, from_sibling, c_idx):
    n, R, w = grads.shape
    half = R // 2
    tr = _pack_row_tile(half)
    steps = half // tr

    def body(c_ref, g_ref, s_ref, o_ref):
        o_ref[...] = (g_ref[...] + s_ref[...]).astype(BF16)

    return pl.pallas_call(
        body, name=name,
        grid_spec=pltpu.PrefetchScalarGridSpec(
            num_scalar_prefetch=1, grid=(n, steps),
            in_specs=[pl.BlockSpec((1, tr, w), lambda j, i, c_ref: (j, c_ref[0] * steps + i, 0)),
                      pl.BlockSpec((1, tr, w), lambda j, i, c_ref: (j, i, 0))],
            out_specs=pl.BlockSpec((1, tr, w), lambda j, i, c_ref: (j, i, 0))),
        out_shape=SDS((n, half, w), BF16),
        compiler_params=_params("parallel", "parallel"),
    )(c_idx, grads, from_sibling)


HBM_SPEC = pl.BlockSpec(memory_space=pltpu.HBM)
SEM_SPEC = pl.BlockSpec(memory_space=pltpu.SEMAPHORE)
DATAFLOW = pltpu.SideEffectType.DATAFLOW_SIDE_EFFECTING


def _chip_scatter_copies(p_refs, land_refs, send_sems, recv_sems):
    x, y, c = _my_place()
    return [pltpu.make_async_remote_copy(
        src_ref=p_refs[i].at[kb], dst_ref=land_refs[i].at[j], send_sem=send_sems.at[3 * i + j],
        recv_sem=recv_sems.at[3 * i + j], device_id=(bx, by, c), device_id_type=MESH)
        for i in range(len(p_refs)) for j, (bx, by, kb) in enumerate(_other_chips(x, y))]


def _scatter_start(name, partials):
    n_p = len(partials)

    def body(*refs):
        p_refs, land_refs = refs[:n_p], refs[n_p:2 * n_p]
        send_sems, recv_sems = refs[2 * n_p:2 * n_p + 2]
        for cp in _chip_scatter_copies(p_refs, land_refs, send_sems, recv_sems):
            cp.start()
        refs[-1][...] = jnp.zeros_like(refs[-1])

    lands = [pltpu.with_memory_space_constraint(lax.empty((N_CHIPS - 1,) + p.shape[1:], p.dtype), pltpu.HBM)
             for p in partials]
    out = pl.pallas_call(
        body, name=name,
        out_shape=(pltpu.SemaphoreType.DMA((3 * n_p,)), pltpu.SemaphoreType.DMA((3 * n_p,)),
                   *[pltpu.HBM(p.shape, p.dtype) for p in partials], *[pltpu.HBM(l.shape, l.dtype) for l in lands],
                   SDS((8, V7X_LANES), F32)),
        in_specs=[HBM_SPEC] * (2 * n_p), out_specs=(SEM_SPEC, SEM_SPEC, *[HBM_SPEC] * (2 * n_p), VMEM_WHOLE),
        input_output_aliases={i: 2 + i for i in range(2 * n_p)},
        compiler_params=pltpu.CompilerParams(has_side_effects=DATAFLOW),
    )(*[pltpu.with_memory_space_constraint(p, pltpu.HBM) for p in partials], *lands)
    return out[0], out[1], list(out[2:2 + n_p]), list(out[2 + n_p:2 + 2 * n_p]), out[-1]


def _scatter_wait(name, send_sems, recv_sems, partials, lands, after):
    n_p = len(partials)

    def body(*refs):
        p_refs, land_refs = refs[:n_p], refs[n_p:2 * n_p]
        send_sems, recv_sems = refs[2 * n_p:2 * n_p + 2]
        for cp in _chip_scatter_copies(p_refs, land_refs, send_sems, recv_sems):
            cp.wait_send()
            cp.wait_recv()

    out = pl.pallas_call(
        body, name=name,
        out_shape=(*[pltpu.HBM(p.shape, p.dtype) for p in partials], *[pltpu.HBM(l.shape, l.dtype) for l in lands]),
        in_specs=[*[HBM_SPEC] * (2 * n_p), SEM_SPEC, SEM_SPEC, ANY], out_specs=tuple([HBM_SPEC] * (2 * n_p)),
        input_output_aliases={i: i for i in range(2 * n_p)},
        compiler_params=pltpu.CompilerParams(has_side_effects=DATAFLOW),
    )(*partials, *lands, send_sems, recv_sems, after)
    return list(out[:n_p]), list(out[n_p:])


def _sum_chip_partials(name, grads, from_sibling, received, shard, layer, place):
    n, half, w = from_sibling.shape
    tr = _pack_row_tile(half)
    steps = half // tr

    def body(place_ref, g_ref, s_ref, r_ref, shard_ref, o_ref):
        own = g_ref[0] + s_ref[0]
        o_ref[...] = ((own + r_ref[0].astype(F32)) + r_ref[1].astype(F32)) + r_ref[2].astype(F32)

    return pl.pallas_call(
        body, name=name,
        grid_spec=pltpu.PrefetchScalarGridSpec(
            num_scalar_prefetch=1, grid=(steps,),
            in_specs=[pl.BlockSpec((1, tr, w), lambda i, place_ref: (place_ref[0], place_ref[1] * steps + i, 0)),
                      pl.BlockSpec((1, tr, w), lambda i, place_ref: (place_ref[0], i, 0)),
                      pl.BlockSpec((n - 1, tr, w), lambda i, place_ref: (0, i, 0)),
                      pl.BlockSpec(memory_space=pl.ANY)],
            out_specs=pl.BlockSpec((tr, w), lambda i, place_ref: ((2 * layer + place_ref[1]) * steps + i, 0))),
        out_shape=SDS(shard.shape, F32),
        input_output_aliases={4: 0},
        compiler_params=_params("parallel"),
    )(place, grads, from_sibling, received, shard)


def _join_halves(shards, layers):
    n_s = len(shards)
    n_sem = sum(layers)

    def body(*refs):
        in_refs = refs[:n_s]
        send_sems, recv_sems = refs[2 * n_s:]
        x, y, c = _my_place()
        copies, sem = [], 0
        for ref, n_layers in zip(in_refs, layers):
            half = ref.shape[0] // (2 * n_layers)
            for layer in range(n_layers):
                mine = pl.ds(pl.multiple_of(layer * 2 * half + c * half, 8), half)
                theirs = pl.ds(pl.multiple_of(layer * 2 * half + (1 - c) * half, 8), half)
                send = pltpu.make_async_remote_copy(
                    src_ref=ref.at[mine], dst_ref=ref.at[mine], send_sem=send_sems.at[sem], recv_sem=recv_sems.at[sem],
                    device_id=(x, y, 1 - c), device_id_type=MESH)
                send.start()
                arrival = pltpu.make_async_remote_copy(
                    src_ref=ref.at[theirs], dst_ref=ref.at[theirs], send_sem=send_sems.at[sem],
                    recv_sem=recv_sems.at[sem], device_id=(x, y, 1 - c), device_id_type=MESH)
                copies.append((send, arrival))
                sem += 1
        for send, arrival in copies:
            send.wait_send()
            arrival.wait_recv()

    return pl.pallas_call(
        body, name="join_halves", out_shape=[SDS(s.shape, s.dtype) for s in shards],
        in_specs=[ANY] * n_s, out_specs=[ANY] * n_s,
        input_output_aliases={i: i for i in range(n_s)},
        scratch_shapes=[pltpu.SemaphoreType.DMA((n_sem,)), pltpu.SemaphoreType.DMA((n_sem,))],
    )(*shards)


def _all_to_all_copies(v_ref, land_ref, send_sems, recv_sems):
    x, y, c = _my_place()
    me = 4 * x + 2 * y + c
    pairs = []
    for k in range(1, N_DEV):
        px, py, pc = (1 - x if k & 4 else x), (1 - y if k & 2 else y), (1 - c if k & 1 else c)
        sems = dict(send_sem=send_sems.at[k - 1], recv_sem=recv_sems.at[k - 1], device_id=(px, py, pc),
                    device_id_type=MESH)
        send = pltpu.make_async_remote_copy(src_ref=v_ref, dst_ref=land_ref.at[me], **sems)
        arrival = pltpu.make_async_remote_copy(src_ref=v_ref, dst_ref=land_ref.at[4 * px + 2 * py + pc], **sems)
        pairs.append((send, arrival))
    return pairs


def _small_reduce_start(name, v):
    def body(v_ref, land_ref, send_sems, recv_sems, v_out, land_out):
        for send, _ in _all_to_all_copies(v_ref, land_ref, send_sems, recv_sems):
            send.start()

    land = pltpu.with_memory_space_constraint(jnp.zeros((N_DEV,) + v.shape, v.dtype), pltpu.HBM)
    return pl.pallas_call(
        body, name=name,
        out_shape=(pltpu.SemaphoreType.DMA((N_DEV - 1,)), pltpu.SemaphoreType.DMA((N_DEV - 1,)),
                   pltpu.HBM(v.shape, v.dtype), pltpu.HBM(land.shape, land.dtype)),
        in_specs=[HBM_SPEC, HBM_SPEC], out_specs=(SEM_SPEC, SEM_SPEC, HBM_SPEC, HBM_SPEC),
        input_output_aliases={0: 2, 1: 3},
        compiler_params=pltpu.CompilerParams(has_side_effects=DATAFLOW),
    )(pltpu.with_memory_space_constraint(v, pltpu.HBM), land)


def _small_reduce_wait(name, send_sems, recv_sems, v, land, after):
    def body(v_ref, land_ref, send_sems, recv_sems, after_ref, v_out, land_out):
        for send, arrival in _all_to_all_copies(v_ref, land_ref, send_sems, recv_sems):
            send.wait_send()
            arrival.wait_recv()

    return pl.pallas_call(
        body, name=name, out_shape=(pltpu.HBM(v.shape, v.dtype), pltpu.HBM(land.shape, land.dtype)),
        in_specs=[HBM_SPEC, HBM_SPEC, SEM_SPEC, SEM_SPEC, ANY], out_specs=(HBM_SPEC, HBM_SPEC),
        input_output_aliases={0: 0, 1: 1},
        compiler_params=pltpu.CompilerParams(has_side_effects=DATAFLOW),
    )(v, land, send_sems, recv_sems, after)


def _sum_device_slots(name, v, land, me):
    r, w = v.shape

    def body(me_ref, v_ref, land_ref, o_ref):
        mine = v_ref[...]
        acc = jnp.where(me_ref[0] == 0, mine, land_ref[0])
        for d in range(1, N_DEV):
            acc = acc + jnp.where(me_ref[0] == d, mine, land_ref[d])
        o_ref[...] = acc

    return pl.pallas_call(
        body, name=name,
        grid_spec=pltpu.PrefetchScalarGridSpec(
            num_scalar_prefetch=1, grid=(1,),
            in_specs=[pl.BlockSpec((r, w), lambda i, me_ref: (0, 0)),
                      pl.BlockSpec((N_DEV, r, w), lambda i, me_ref: (0, 0, 0))],
            out_specs=pl.BlockSpec((r, w), lambda i, me_ref: (0, 0))),
        out_shape=SDS((r, w), F32),
        compiler_params=_params("arbitrary"),
    )(me, v, land)


def _adamw(name, w, g, m, v):
    rows, width = w.shape
    tr = _pack_row_tile(rows)

    def body(w_ref, g_ref, m_ref, v_ref, g_out_ref, d_ref, nm_ref, nv_ref):
        gg = g_ref[...]
        g_out_ref[...] = gg
        m_new = ADAM_B1 * m_ref[...] + (1.0 - ADAM_B1) * gg
        v_new = ADAM_B2 * v_ref[...] + (1.0 - ADAM_B2) * (gg * gg)
        m_hat = m_new / (1.0 - ADAM_B1 ** ADAM_STEP)
        v_hat = v_new / (1.0 - ADAM_B2 ** ADAM_STEP)
        d_ref[...] = -ADAM_LR * (m_hat / (jnp.sqrt(v_hat) + ADAM_EPS) + ADAM_WD * w_ref[...])
        nm_ref[...] = m_new
        nv_ref[...] = v_new

    spec = _rows(tr, width)
    return pl.pallas_call(
        body, name=name, grid=(rows // tr,),
        in_specs=[spec] * 4, out_specs=[spec] * 4,
        out_shape=[SDS((rows, width), F32)] * 4,
        compiler_params=_params("parallel"),
    )(w, g, m, v)


WEIGHT_NAMES = ['norm_mix', 'norm_ffn', 'attn_w_qkv', 'attn_b_qkv', 'attn_sinks', 'attn_w_o', 'attn_b_o',
                'conv_w_pw1', 'conv_b_pw1', 'conv_w_dw', 'conv_b_dw', 'conv_ln_g', 'conv_ln_b', 'conv_w_pw2',
                'conv_b_pw2', 'ffn_w1', 'ffn_w3', 'ffn_w2', 'norm_final']
BIG = ['attn_w_qkv', 'attn_w_o', 'conv_w_pw1', 'conv_w_pw2', 'ffn_w1', 'ffn_w3', 'ffn_w2']
COLUMN_SPLIT = ('attn_w_qkv', 'conv_w_pw1', 'ffn_w1', 'ffn_w3')
STORED_TRANSPOSED = ('attn_w_qkv', 'ffn_w1', 'ffn_w3')
SMALL_SPLIT = ['conv_b_pw1', 'conv_w_dw', 'conv_b_dw', 'conv_ln_g', 'conv_ln_b', 'conv_b_pw2']
SMALL_WHOLE = ['norm_mix', 'norm_ffn', 'attn_b_qkv', 'attn_sinks', 'attn_b_o', 'norm_final']


def _keeps_rows(shape):
    return len(shape) == 2 and shape[0] > 1 and shape[1] == PACK_W


def _pack_rows(arrays, dtype, row_multiple):
    blocks = [jnp.pad(a.astype(dtype), ((0, -a.shape[0] % V7X_SUBLANES), (0, 0)))
              for a in arrays if _keeps_rows(a.shape)]
    flat = jnp.concatenate([a.astype(dtype).reshape(-1) for a in arrays if not _keeps_rows(a.shape)])
    multiple = max(row_multiple, V7X_SUBLANES)
    rows = -(-(-(-flat.shape[0] // PACK_W)) // multiple) * multiple
    blocks.append(jnp.pad(flat, (0, rows * PACK_W - flat.shape[0])).reshape(rows, PACK_W))
    return jnp.concatenate(blocks, axis=0) if len(blocks) > 1 else blocks[0]


def _unpack_rows(pack, shapes):
    out, row = {}, 0
    for i, shape in enumerate(shapes):
        if _keeps_rows(shape):
            out[i] = pack[row:row + shape[0]]
            row += -(-shape[0] // V7X_SUBLANES) * V7X_SUBLANES
    flat, at = pack[row:].reshape(-1), 0
    for i, shape in enumerate(shapes):
        if not _keeps_rows(shape):
            size = 1
            for s in shape:
                size *= s
            out[i] = flat[at:at + size].reshape(shape)
            at += size
    return [out[i] for i in range(len(shapes))]


def _join_chip_axis(name, parts):
    axis = parts.ndim - 1 if name in COLUMN_SPLIT or name in SMALL_SPLIT else parts.ndim - 2
    moved = jnp.moveaxis(parts, 0, axis - 1)
    shape = list(moved.shape)
    shape[axis - 1:axis + 1] = [shape[axis - 1] * shape[axis]]
    return moved.reshape(shape)


def _split_chip_axis(name, whole, shard_shape):
    axis = len(shard_shape) - 1 if name in COLUMN_SPLIT or name in SMALL_SPLIT else len(shard_shape) - 2
    shape = list(whole.shape)
    shape[axis:axis + 1] = [N_CHIPS, shard_shape[axis]]
    return jnp.moveaxis(whole.reshape(shape), axis, 0)


def kernel(x, norm_mix, norm_ffn, attn_w_qkv, attn_b_qkv, attn_sinks, attn_w_o, attn_b_o, conv_w_pw1, conv_b_pw1, conv_w_dw, conv_b_dw, conv_ln_g, conv_ln_b, conv_w_pw2, conv_b_pw2, ffn_w1, ffn_w3, ffn_w2, norm_final, loss_target, m_norm_mix, m_norm_ffn, m_attn_w_qkv, m_attn_b_qkv, m_attn_sinks, m_attn_w_o, m_attn_b_o, m_conv_w_pw1, m_conv_b_pw1, m_conv_w_dw, m_conv_b_dw, m_conv_ln_g, m_conv_ln_b, m_conv_w_pw2, m_conv_b_pw2, m_ffn_w1, m_ffn_w3, m_ffn_w2, m_norm_final, v_norm_mix, v_norm_ffn, v_attn_w_qkv, v_attn_b_qkv, v_attn_sinks, v_attn_w_o, v_attn_b_o, v_conv_w_pw1, v_conv_b_pw1, v_conv_w_dw, v_conv_b_dw, v_conv_ln_g, v_conv_ln_b, v_conv_w_pw2, v_conv_b_pw2, v_ffn_w1, v_ffn_w3, v_ffn_w2, v_norm_final):
    w = dict(zip(WEIGHT_NAMES, (norm_mix, norm_ffn, attn_w_qkv, attn_b_qkv, attn_sinks, attn_w_o, attn_b_o,
                                conv_w_pw1, conv_b_pw1, conv_w_dw, conv_b_dw, conv_ln_g, conv_ln_b, conv_w_pw2,
                                conv_b_pw2, ffn_w1, ffn_w3, ffn_w2, norm_final)))
    m = dict(zip(WEIGHT_NAMES, (m_norm_mix, m_norm_ffn, m_attn_w_qkv, m_attn_b_qkv, m_attn_sinks, m_attn_w_o,
                                m_attn_b_o, m_conv_w_pw1, m_conv_b_pw1, m_conv_w_dw, m_conv_b_dw, m_conv_ln_g,
                                m_conv_ln_b, m_conv_w_pw2, m_conv_b_pw2, m_ffn_w1, m_ffn_w3, m_ffn_w2, m_norm_final)))
    v = dict(zip(WEIGHT_NAMES, (v_norm_mix, v_norm_ffn, v_attn_w_qkv, v_attn_b_qkv, v_attn_sinks, v_attn_w_o,
                                v_attn_b_o, v_conv_w_pw1, v_conv_b_pw1, v_conv_w_dw, v_conv_b_dw, v_conv_ln_g,
                                v_conv_ln_b, v_conv_w_pw2, v_conv_b_pw2, v_ffn_w1, v_ffn_w3, v_ffn_w2, v_norm_final)))
    T, D = x.shape[1], x.shape[2]
    c_idx = lax.axis_index("c").astype(jnp.int32).reshape(1)
    chip = (2 * lax.axis_index("x") + lax.axis_index("y")).astype(jnp.int32)

    def as_rows(n, a):
        a = jnp.swapaxes(a, -1, -2) if n in STORED_TRANSPOSED else a
        return a.reshape(-1, a.shape[-1])

    def from_rows(n, rows):
        shape = w[n].shape[:-2] + w[n].shape[:-3:-1] if n in STORED_TRANSPOSED else w[n].shape
        a = rows.reshape(shape)
        return jnp.swapaxes(a, -1, -2) if n in STORED_TRANSPOSED else a

    def cast(n, after=None):
        return _cast_into_slot(f"cast_{n}", lax.empty((N_CHIPS,) + as_rows(n, w[n]).shape, BF16), as_rows(n, w[n]),
                               chip.reshape(1), after)

    first, later = BIG[:2], BIG[2:]
    slabs = {n: cast(n) for n in first}
    attn_send, attn_recv, attn_travelling, attn_started = _gather_start(
        "gather_attn_start", [slabs[n] for n in first], slabs[first[0]])
    slabs.update({n: cast(n, attn_started) for n in later})
    layers = ffn_w1.shape[0]
    small_shapes = [w[n].shape for n in SMALL_SPLIT]
    vec_send, vec_recv, vec_mine, vec_landing = _small_reduce_start(
        "small_gather_start", _pack_rows([w[n] for n in SMALL_SPLIT], F32, 8))
    attn_landed = _gather_wait("gather_attn_wait", attn_send, attn_recv, attn_travelling, vec_mine)
    qkv_parts, w_o_parts = _swap_fetched_with_sibling("gather_attn_swap", attn_landed)
    send_sems, recv_sems, travelling, gather_started = _gather_start("gather_start", [slabs[n] for n in later],
                                                                     qkv_parts)

    def small_vectors(after):
        mine, landed = _small_reduce_wait("small_gather_wait", vec_send, vec_recv, vec_mine, vec_landing, after)
        landed = lax.dynamic_update_index_in_dim(landed, mine, 2 * chip + c_idx[0], axis=0)
        per_chip = [_unpack_rows(landed[2 * j], small_shapes) for j in range(N_CHIPS)]
        whole = {n: _join_chip_axis(n, jnp.stack([per_chip[j][i] for j in range(N_CHIPS)]))
                 for i, n in enumerate(SMALL_SPLIT)}
        return {**{n: whole[n] for n in SMALL_SPLIT if n != "conv_w_dw"}, "conv_w_dw": whole["conv_w_dw"][0]}

    def other_weights(after):
        landed = dict(zip(later, _gather_wait("gather_wait", send_sems, recv_sems, travelling, after)))
        now, then = ["ffn_w1", "ffn_w3"], ["conv_w_pw1", "conv_w_pw2", "ffn_w2"]
        ready = dict(zip(now, _swap_fetched_with_sibling("gather_swap", [landed[n] for n in now])))
        swap_send, swap_recv, swapping_rest, swap_started = _gather_start(
            "gather_swap_start", [landed[n] for n in then], ready[now[0]], stage="d2d")

        def rest_of_weights(after_next):
            rest = dict(zip(then, _gather_wait("gather_swap_wait", swap_send, swap_recv, swapping_rest, after_next,
                                               stage="d2d")))
            return {"conv_w_pw1": rest["conv_w_pw1"], "conv_w_pw2": rest["conv_w_pw2"].reshape(-1, D),
                    "ffn_w2": rest["ffn_w2"].reshape(N_CHIPS, layers, -1, D), **small_vectors(after_next)}

        return {"ffn_w1": ready["ffn_w1"].reshape(N_CHIPS, layers, -1, D),
                "ffn_w3": ready["ffn_w3"].reshape(N_CHIPS, layers, -1, D),
                "swap_started": swap_started, "rest_of_weights": rest_of_weights}

    p = {
        "norm_mix": norm_mix, "norm_ffn": norm_ffn, "norm_final": norm_final.reshape(1, D),
        "attn_w_qkv": qkv_parts.reshape(-1, D), "attn_b_qkv": attn_b_qkv,
        "attn_sinks": attn_sinks, "attn_w_o": w_o_parts.reshape(-1, D), "attn_b_o": attn_b_o,
        "gather_started": gather_started, "other_weights": other_weights,
    }
    swapping, in_flight = {}, []

    def reduce_begin(tag, grads):
        keys = list(grads)
        *handles, begun = _sibling_swap_start(f"sibling_swap_start_{tag}", [grads[k] for k in keys])
        swapping[tag] = (keys, handles)
        return begun

    def reduce_send(tag, after):
        keys, (swap_send, swap_recv, grads, lands) = swapping[tag]
        grads, from_sibling = _sibling_swap_wait(f"sibling_swap_wait_{tag}", swap_send, swap_recv, grads, lands, after)
        partials = [_add_sibling_half(f"add_sibling_half_{tag}{i}", gr, fs, c_idx)
                    for i, (gr, fs) in enumerate(zip(grads, from_sibling))]
        *handles, sent = _scatter_start(f"scatter_start_{tag}", partials)
        in_flight.append((tag, keys, handles, grads, from_sibling))
        return sent

    loss_part, dx, g = _local_step(x[0], loss_target[0], p, reduce_begin, reduce_send)
    for n in SMALL_WHOLE + SMALL_SPLIT:
        g[n] = g[n].reshape((-1,) + g[n].shape[-2:]) if w[n].ndim == 3 else g[n].reshape(w[n].shape[:-1] + (-1,))

    small_pack = _pack_rows([loss_part] + [g[n] for n in SMALL_WHOLE] + [g[n] for n in SMALL_SPLIT], F32, 8)
    small_send, small_recv, small_pack, small_land = _small_reduce_start("small_reduce_start", small_pack)

    place = jnp.stack([chip, c_idx[0]])
    shard_grad = {n: lax.empty(as_rows(n, w[n]).shape, F32) for n in BIG}
    for tag, keys, (send_sems, recv_sems, partials, lands), grads, from_sibling in in_flight:
        _, received = _scatter_wait(f"scatter_wait_{tag}", send_sems, recv_sems, partials, lands, small_pack)
        for i, (n, layer) in enumerate(keys):
            shard_grad[n] = _sum_chip_partials(f"sum_chip_partials_{tag}{i}", grads[i], from_sibling[i], received[i],
                                               shard_grad[n], layer, place)
    g_big = dict(zip(BIG, _join_halves([shard_grad[n] for n in BIG], [w[n].shape[0] for n in BIG])))
    big_out = {}
    for n in BIG:
        step = _adamw(f"adamw_{n}", as_rows(n, w[n]), g_big[n], as_rows(n, m[n]), as_rows(n, v[n]))
        big_out[n] = [from_rows(n, a) for a in step]

    small_whole_shapes = [w[n].shape for n in SMALL_WHOLE]
    small_full_shapes = [g[n].shape for n in SMALL_SPLIT]
    small_pack, small_land = _small_reduce_wait("small_reduce_wait", small_send, small_recv, small_pack, small_land,
                                                big_out[BIG[-1]][1])
    reduced = _sum_device_slots("small_reduce_sum", small_pack, small_land, (2 * chip + c_idx[0]).reshape(1))
    pieces = _unpack_rows(reduced, [(1,)] + small_whole_shapes + small_full_shapes)
    loss = pieces[0].reshape(())
    g_small = dict(zip(SMALL_WHOLE, pieces[1:1 + len(SMALL_WHOLE)]))
    for n, whole in zip(SMALL_SPLIT, pieces[1 + len(SMALL_WHOLE):]):
        parts = _split_chip_axis(n, whole, w[n].shape)
        g_small[n] = lax.dynamic_index_in_dim(parts, chip, axis=0, keepdims=False)
    small = SMALL_WHOLE + SMALL_SPLIT
    _, d_small, m_small, v_small = _adamw(
        "adamw_small", _pack_rows([w[n] for n in small], F32, 8), _pack_rows([g_small[n] for n in small], F32, 8),
        _pack_rows([m[n] for n in small], F32, 8), _pack_rows([v[n] for n in small], F32, 8))

    outs = {}
    for slot, (tag, small_pack) in enumerate((("g", None), ("d", d_small), ("m", m_small), ("v", v_small))):
        vals = {n: big_out[n][slot] for n in BIG}
        if small_pack is None:
            vals.update(g_small)
        else:
            vals.update(zip(small, _unpack_rows(small_pack, [w[n].shape for n in small])))
        outs[tag] = vals
    return (loss, dx.reshape(1, T, D), *[outs["g"][n] for n in WEIGHT_NAMES], *[outs["d"][n] for n in WEIGHT_NAMES],
            *[outs["m"][n] for n in WEIGHT_NAMES], *[outs["v"][n] for n in WEIGHT_NAMES])
```

```python
import functools

import jax
import jax.numpy as jnp
from jax import lax
from jax.experimental import pallas as pl
from jax.experimental.pallas import tpu as pltpu

F32 = jnp.float32
BF16 = jnp.bfloat16
SDS = jax.ShapeDtypeStruct
MESH = pl.DeviceIdType.MESH

HEAD_DIM = 64
N_Q_HEADS = 16
N_KV_HEADS = 2
Q_PER_KV = N_Q_HEADS // N_KV_HEADS
ATTN_BLOCK = 128
ROPE_THETA = 10000.0
CONV_WIDTH = 31
CONV_HALO = 32
CONV_FIRST_TAP = CONV_HALO - CONV_WIDTH + 1
CONV_ROW_CHUNK = 64
CONV_LANE_CHUNK = 256
CONV_GRAD_UNROLL = 8
RMS_EPS = 1e-5
LN_EPS = 1e-5
ADAM_LR = 0.001
ADAM_B1 = 0.9
ADAM_B2 = 0.999
ADAM_EPS = 1e-08
ADAM_WD = 0.01
ADAM_STEP = 10

V7X_LANES = 128
V7X_SUBLANES = 8
V7X_VMEM_LIMIT_BYTES = 56 * 1024 * 1024

N_CHIPS = 4
N_DEV = 8
PACK_W = 1024

MASK_VALUE = -1e30


def _params(*semantics):
    return pltpu.CompilerParams(dimension_semantics=semantics, vmem_limit_bytes=V7X_VMEM_LIMIT_BYTES)


def _rows(tm, width):
    return pl.BlockSpec((tm, width), lambda i: (i, 0))


def _whole(shape):
    return pl.BlockSpec(shape, lambda *_: (0,) * len(shape))


def _rms_rstd(h):
    return lax.rsqrt(jnp.mean(h * h, axis=-1, keepdims=True) + RMS_EPS)


def _silu_and_grad(z):
    sg = jax.nn.sigmoid(z)
    return z * sg, sg * (1.0 + z * (1.0 - sg))


def _swap_rope_halves(t):
    w = t.shape[1]
    half = HEAD_DIM // 2
    lane = lax.broadcasted_iota(jnp.int32, t.shape, 1)
    upper = pltpu.roll(t, w - half, 1)
    lower = pltpu.roll(t, half, 1)
    return jnp.where(lane % HEAD_DIM < half, upper, lower)


def _rope(t, cos_ref, sin_ref):
    reps = t.shape[1] // V7X_LANES
    c = jnp.tile(cos_ref[...], (1, reps))
    s = jnp.tile(sin_ref[...], (1, reps))
    return t * c + _swap_rope_halves(t) * s


def _rope_transposed(dt, cos_ref, sin_ref):
    reps = dt.shape[1] // V7X_LANES
    c = jnp.tile(cos_ref[...], (1, reps))
    s = jnp.tile(sin_ref[...], (1, reps))
    return dt * c + _swap_rope_halves(dt * s)


def _rope_tables(seq_len):
    pos = jnp.arange(seq_len, dtype=F32)
    inv_freq = ROPE_THETA ** (-jnp.arange(0, HEAD_DIM, 2, dtype=F32) / HEAD_DIM)
    ang = pos[:, None] * jnp.tile(inv_freq, 2 * V7X_LANES // HEAD_DIM)[None, :]
    upper_half = jnp.arange(V7X_LANES) % HEAD_DIM >= HEAD_DIM // 2
    return jnp.cos(ang), jnp.where(upper_half[None, :], jnp.sin(ang), -jnp.sin(ang))


def _qkv_proj(h, g, w, b, cos, sin, after):
    T, D = h.shape
    N = w.shape[0]
    tm = min(512, T)
    rope_w = N - N_KV_HEADS * HEAD_DIM

    def body(h_ref, g_ref, w_ref, b_ref, cos_ref, sin_ref, _, y_ref, o_ref):
        hh = h_ref[...]
        y = (hh * _rms_rstd(hh) * g_ref[...]).astype(BF16)
        y_ref[...] = y
        acc = _dot_nt(y, w_ref[...]) + b_ref[...]
        o_ref[:, :rope_w] = _rope(acc[:, :rope_w], cos_ref, sin_ref).astype(BF16)
        o_ref[:, rope_w:] = acc[:, rope_w:].astype(BF16)

    return pl.pallas_call(
        body, name="qkv_proj", grid=(T // tm,),
        in_specs=[_rows(tm, D), _whole((1, D)), _whole((N, D)), _whole((1, N)),
                  _rows(tm, V7X_LANES), _rows(tm, V7X_LANES), pl.BlockSpec(memory_space=pl.ANY)],
        out_specs=[_rows(tm, D), _rows(tm, N)],
        out_shape=[SDS((T, D), BF16), SDS((T, N), BF16)],
        compiler_params=_params("parallel"),
    )(h, g, w, b, cos, sin, after)


def _pw1_proj(h, g, w, b):
    T, D = h.shape
    n = w.shape[2]
    N = N_CHIPS * n
    tm = min(512, T)

    def body(h_ref, g_ref, w_ref, b_ref, y_ref, o_ref):
        hh = h_ref[...]
        y = (hh * _rms_rstd(hh) * g_ref[...]).astype(BF16)
        y_ref[...] = y
        for j in range(N_CHIPS):
            cols = slice(j * n, (j + 1) * n)
            o_ref[:, cols] = (jnp.dot(y, w_ref[j], preferred_element_type=F32) + b_ref[:, cols]).astype(BF16)

    return pl.pallas_call(
        body, name="pw1_proj", grid=(T // tm,),
        in_specs=[_rows(tm, D), _whole((1, D)), _whole((N_CHIPS, D, n)), _whole((1, N))],
        out_specs=[_rows(tm, D), _rows(tm, N)],
        out_shape=[SDS((T, D), BF16), SDS((T, N), BF16)],
        compiler_params=_params("parallel"),
    )(h, g, w, b)


PAIRS_PER_KV = Q_PER_KV // 2


def _upper_lanes(shape):
    return lax.broadcasted_iota(jnp.int32, shape, 1) >= HEAD_DIM


def _swap_lane_halves(t):
    return pltpu.roll(t.astype(F32), HEAD_DIM, 1).astype(t.dtype)


def _kv_operands(g, t):
    swapped = _swap_lane_halves(t)
    in_lower, in_upper = (t, swapped) if g == 0 else (swapped, t)
    upper = _upper_lanes(t.shape)
    zero = jnp.zeros_like(t)
    return jnp.where(upper, zero, in_lower), jnp.where(upper, in_upper, zero)


def _group_heads(g):
    pairs = range(g * PAIRS_PER_KV, (g + 1) * PAIRS_PER_KV)
    return [2 * hp for hp in pairs] + [2 * hp + 1 for hp in pairs]


def _all_heads():
    return [h for g in range(N_KV_HEADS) for h in _group_heads(g)]


def _pair_rows(ref, g):
    pairs = range(g * PAIRS_PER_KV, (g + 1) * PAIRS_PER_KV)
    return jnp.concatenate([ref[:, hp * 2 * HEAD_DIM:(hp + 1) * 2 * HEAD_DIM] for hp in pairs], axis=0)


def _from_previous_block(rows):
    row = lax.broadcasted_iota(jnp.int32, (ATTN_BLOCK, ATTN_BLOCK), 0)
    col = lax.broadcasted_iota(jnp.int32, (ATTN_BLOCK, ATTN_BLOCK), 1)
    return jnp.concatenate([col > row] * (rows // ATTN_BLOCK), axis=0)


def _split_folded(t, prev_part):
    tb = t.astype(BF16)
    zero = jnp.zeros_like(tb)
    return jnp.where(prev_part, tb, zero), jnp.where(prev_part, zero, tb)


def _attn_specs(T):
    nb = T // ATTN_BLOCK
    kcol = N_Q_HEADS * HEAD_DIM // V7X_LANES
    cur = lambda n: jnp.minimum(n, nb - 1)
    prev = lambda n: jnp.maximum(jnp.minimum(n, nb - 1) - 1, 0)
    q_spec = pl.BlockSpec((ATTN_BLOCK, N_Q_HEADS * HEAD_DIM), lambda n: (cur(n), 0))
    kc_spec = pl.BlockSpec((ATTN_BLOCK, V7X_LANES), lambda n: (cur(n), kcol))
    kp_spec = pl.BlockSpec((ATTN_BLOCK, V7X_LANES), lambda n: (prev(n), kcol))
    vc_spec = pl.BlockSpec((ATTN_BLOCK, V7X_LANES), lambda n: (cur(n), kcol + 1))
    vp_spec = pl.BlockSpec((ATTN_BLOCK, V7X_LANES), lambda n: (prev(n), kcol + 1))
    return q_spec, kc_spec, kp_spec, vc_spec, vp_spec


def _attn_fwd(qkv, sinks):
    T = qkv.shape[0]
    nb = T // ATTN_BLOCK
    qw = N_Q_HEADS * HEAD_DIM
    all_rows = N_Q_HEADS * ATTN_BLOCK

    per_step = 4
    kcol = qw // V7X_LANES

    def body(q_ref, *refs):
        k_refs, v_refs = refs[:per_step + 1], refs[per_step + 1:2 * per_step + 2]
        sink_ref, o_ref, probs_ref, psink_ref = refs[2 * per_step + 2:]
        m = pl.program_id(0)
        half = PAIRS_PER_KV * ATTN_BLOCK
        groups = range(N_KV_HEADS)
        k_ops = [[_kv_operands(g, r[...]) for g in groups] for r in k_refs]
        v_ops = [[_kv_operands(g, r[...]) for g in groups] for r in v_refs]

        def scores(q, k):
            return lax.dot_general(q, k, (((1,), (1,)), ((), ())), preferred_element_type=F32)

        s_prev, s_cur = [], []
        for b in range(per_step):
            rows = slice(b * ATTN_BLOCK, (b + 1) * ATTN_BLOCK)
            for g in groups:
                pairs = range(g * PAIRS_PER_KV, (g + 1) * PAIRS_PER_KV)
                q = jnp.concatenate([q_ref[rows, hp * 2 * HEAD_DIM:(hp + 1) * 2 * HEAD_DIM] for hp in pairs], axis=0)
                for i in range(2):
                    from_prev = scores(q, k_ops[b][g][i])
                    if b == 0:
                        from_prev = jnp.where(m > 0, from_prev, MASK_VALUE * (HEAD_DIM ** 0.5))
                    s_prev.append(from_prev)
                    s_cur.append(scores(q, k_ops[b + 1][g][i]))
        prev_part = _from_previous_block(per_step * all_rows)
        s = jnp.where(prev_part, jnp.concatenate(s_prev, axis=0), jnp.concatenate(s_cur, axis=0)) * (HEAD_DIM ** -0.5)
        sink = jnp.concatenate([jnp.broadcast_to(sink_ref[0:1, h:h + 1], (ATTN_BLOCK, 1))
                                for h in _all_heads() * per_step], axis=0)
        top = jnp.maximum(jnp.max(s, axis=1, keepdims=True), sink)
        p = jnp.exp(s - top)
        e_sink = jnp.exp(sink - top)
        inv = 1.0 / (jnp.sum(p, axis=1, keepdims=True) + e_sink)
        probs, p_sink = p * inv, e_sink * inv
        p_prev, p_cur = _split_folded(probs, prev_part)
        lane = lax.broadcasted_iota(jnp.int32, (ATTN_BLOCK, V7X_LANES), 1)
        for b in range(per_step):
            rows = slice(b * ATTN_BLOCK, (b + 1) * ATTN_BLOCK)
            base = b * all_rows
            probs_ref[b] = probs[base:base + all_rows].astype(BF16)
            sink_tile = jnp.zeros((ATTN_BLOCK, V7X_LANES), F32)
            for i, h in enumerate(_all_heads()):
                sink_tile = jnp.where(lane == h, p_sink[base + i * ATTN_BLOCK:base + (i + 1) * ATTN_BLOCK], sink_tile)
            psink_ref[rows, :] = sink_tile
            for g in groups:
                even = slice(base + 2 * g * half, base + (2 * g + 1) * half)
                odd = slice(base + (2 * g + 1) * half, base + (2 * g + 2) * half)
                o = (jnp.dot(p_prev[even], v_ops[b][g][0], preferred_element_type=F32)
                     + jnp.dot(p_cur[even], v_ops[b + 1][g][0], preferred_element_type=F32)
                     + jnp.dot(p_prev[odd], v_ops[b][g][1], preferred_element_type=F32)
                     + jnp.dot(p_cur[odd], v_ops[b + 1][g][1], preferred_element_type=F32))
                for i in range(PAIRS_PER_KV):
                    hp = g * PAIRS_PER_KV + i
                    o_ref[rows, hp * 2 * HEAD_DIM:(hp + 1) * 2 * HEAD_DIM] = (
                        o[i * ATTN_BLOCK:(i + 1) * ATTN_BLOCK].astype(BF16))

    def kv_specs(col):
        return [pl.BlockSpec((ATTN_BLOCK, V7X_LANES),
                             functools.partial(lambda m, b: (jnp.maximum(per_step * m + b, 0), col), b=b))
                for b in range(-1, per_step)]

    return pl.pallas_call(
        body, name="attn_fwd", grid=(nb // per_step,),
        in_specs=[_rows(per_step * ATTN_BLOCK, qw), *kv_specs(kcol), *kv_specs(kcol + 1), _whole((1, N_Q_HEADS))],
        out_specs=[_rows(per_step * ATTN_BLOCK, qw),
                   pl.BlockSpec((per_step, all_rows, ATTN_BLOCK), lambda m: (m, 0, 0)),
                   _rows(per_step * ATTN_BLOCK, V7X_LANES)],
        out_shape=[SDS((T, qw), BF16), SDS((nb, all_rows, ATTN_BLOCK), BF16), SDS((T, V7X_LANES), F32)],
        compiler_params=_params("parallel"),
    )(*[qkv] * (2 * per_step + 3), sinks)


def _mm_res(name, a, w, b, res, g):
    T, K = a.shape
    D = w.shape[1]
    tm = min(512, T)

    def body(a_ref, w_ref, b_ref, r_ref, g_ref, o_ref, f_ref):
        h = jnp.dot(a_ref[...], w_ref[...], preferred_element_type=F32) + b_ref[...] + r_ref[...]
        o_ref[...] = h
        f_ref[...] = (h * _rms_rstd(h) * g_ref[...]).astype(BF16)

    return pl.pallas_call(
        body, name=name, grid=(T // tm,),
        in_specs=[_rows(tm, K), _whole((K, D)), _whole((1, D)), _rows(tm, D), _whole((1, D))],
        out_specs=[_rows(tm, D), _rows(tm, D)],
        out_shape=[SDS((T, D), F32), SDS((T, D), BF16)],
        compiler_params=_params("parallel"),
    )(a, w, b, res, g)


def _ffn_down(name, s, w2, layer, res):
    _, T, n = s.shape
    D = w2.shape[3]
    tm = min(512, T)

    def body(s_ref, w_ref, r_ref, o_ref):
        acc = r_ref[...]
        for j in range(N_CHIPS):
            acc = acc + jnp.dot(s_ref[j], w_ref[j], preferred_element_type=F32)
        o_ref[...] = acc

    return pl.pallas_call(
        body, name=name, grid=(T // tm,),
        in_specs=[pl.BlockSpec((N_CHIPS, tm, n), lambda i: (0, i, 0)),
                  pl.BlockSpec((N_CHIPS, None, n, D), lambda i: (0, layer, 0, 0)), _rows(tm, D)],
        out_specs=_rows(tm, D),
        out_shape=SDS((T, D), F32),
        compiler_params=_params("parallel"),
    )(s, w2, res)


def _ffn_up(name, f, w1, w3, layer, after=None):
    T, D = f.shape
    n = w1.shape[2]
    tm = min(1024, T)

    def body(f_ref, w1_ref, w3_ref, *rest):
        act_ref, gg_ref, s_ref = rest[-3:]
        ff = f_ref[...]
        g1 = _dot_nt(ff, w1_ref[...])
        g3 = _dot_nt(ff, w3_ref[...])
        act, dact = _silu_and_grad(g1)
        act_ref[...] = act.astype(BF16)
        gg_ref[...] = (g3 * dact).astype(BF16)
        s_ref[...] = (act * g3).astype(BF16)

    slab = pl.BlockSpec((None, tm, n), lambda j, i: (j, i, 0))
    wslab = pl.BlockSpec((None, None, n, D), lambda j, i: (j, layer, 0, 0))
    hidden = SDS((N_CHIPS, T, n), BF16)
    return pl.pallas_call(
        body, name=name, grid=(N_CHIPS, T // tm),
        in_specs=[pl.BlockSpec((tm, D), lambda j, i: (i, 0)), wslab, wslab]
        + ([] if after is None else [pl.BlockSpec(memory_space=pl.ANY)]),
        out_specs=[slab, slab, slab],
        out_shape=[hidden, hidden, hidden],
        compiler_params=_params("parallel", "parallel"),
    )(f, w1, w3, *([] if after is None else [after]))


def _glu(a, d):
    a = a.astype(F32)
    return a[:, :d] * jax.nn.sigmoid(a[:, d:])


def _conv_tile(T):
    return min(256, T)


def _fill_shifted(sh_ref, tc):
    n = tc + CONV_HALO - V7X_SUBLANES
    for r in range(1, V7X_SUBLANES):
        sh_ref[r, 0:n, :] = sh_ref[0, pl.ds(r, n), :]


def _depthwise_taps(sh_ref, w_ref, offsets, bias_ref, out_ref, tc):
    D = out_ref.shape[1]

    def chunk(i, carry):
        t0 = pl.multiple_of(i * CONV_ROW_CHUNK, CONV_ROW_CHUNK)
        for cb in range(D // CONV_LANE_CHUNK):
            cs = slice(cb * CONV_LANE_CHUNK, (cb + 1) * CONV_LANE_CHUNK)
            acc = jnp.zeros((CONV_ROW_CHUNK, CONV_LANE_CHUNK), F32)
            for r in range(V7X_SUBLANES):
                taps = [(j, o // V7X_SUBLANES) for j, o in enumerate(offsets) if o % V7X_SUBLANES == r]
                if not taps:
                    continue
                span = CONV_ROW_CHUNK + V7X_SUBLANES * max(q for _, q in taps)
                rows = sh_ref[r, pl.ds(t0, span), cs]
                for j, q in taps:
                    acc = acc + rows[V7X_SUBLANES * q:V7X_SUBLANES * q + CONV_ROW_CHUNK] * w_ref[j:j + 1, cs]
            if bias_ref is not None:
                acc = acc + bias_ref[:, cs]
            out_ref[pl.ds(t0, CONV_ROW_CHUNK), cs] = acc
        return carry

    lax.fori_loop(0, tc // CONV_ROW_CHUNK, chunk, 0)


def _depthwise_tap_grads(dy_sh, x_sh, offsets, dw_ref, tc):
    D = dw_ref.shape[1]
    for cb in range(D // V7X_LANES):
        cs = slice(cb * V7X_LANES, (cb + 1) * V7X_LANES)

        def row_tiles(i, accs, cs=cs):
            for k in range(CONV_GRAD_UNROLL):
                t0 = pl.multiple_of(i * (CONV_GRAD_UNROLL * V7X_SUBLANES), V7X_SUBLANES) + k * V7X_SUBLANES
                d = dy_sh[0, pl.ds(t0, V7X_SUBLANES), cs]
                accs = tuple(
                    acc + d * x_sh[o % V7X_SUBLANES, pl.ds(t0 + o // V7X_SUBLANES * V7X_SUBLANES, V7X_SUBLANES), cs]
                    for acc, o in zip(accs, offsets))
            return accs

        zero = jnp.zeros((V7X_SUBLANES, V7X_LANES), F32)
        accs = lax.fori_loop(0, tc // (CONV_GRAD_UNROLL * V7X_SUBLANES), row_tiles, tuple(zero for _ in offsets))
        for j, acc in enumerate(accs):
            dw_ref[j:j + 1, cs] += jnp.sum(acc, axis=0, keepdims=True)


def _conv_fwd(a, w_dw, b_dw, ln_g, ln_b, w_pw2, b_pw2, res, g_next):
    T = a.shape[0]
    D = a.shape[1] // 2
    tc = _conv_tile(T)
    per = tc // CONV_HALO

    def body(a_ref, ah_ref, w_ref, bdw_ref, lg_ref, lb_ref, wp_ref, bp_ref, r_ref, g_ref,
             c_ref, act_ref, h_ref, f_ref, u_sh):
        i = pl.program_id(0)
        u_sh[0, 0:CONV_HALO, :] = jnp.where(i > 0, _glu(ah_ref[...], D), 0.0)
        u_sh[0, CONV_HALO:, :] = _glu(a_ref[...], D)
        _fill_shifted(u_sh, tc)
        _depthwise_taps(u_sh, w_ref, [CONV_FIRST_TAP + j for j in range(CONV_WIDTH)], bdw_ref, c_ref, tc)
        c = c_ref[...]
        xc = c - jnp.mean(c, axis=-1, keepdims=True)
        z = xc * lax.rsqrt(jnp.mean(xc * xc, axis=-1, keepdims=True) + LN_EPS)
        l = z * lg_ref[...] + lb_ref[...]
        act = (l * jax.nn.sigmoid(l)).astype(BF16)
        act_ref[...] = act
        h = jnp.dot(act, wp_ref[...], preferred_element_type=F32) + bp_ref[...] + r_ref[...]
        h_ref[...] = h
        f_ref[...] = (h * _rms_rstd(h) * g_ref[...]).astype(BF16)

    return pl.pallas_call(
        body, name="conv_fwd", grid=(T // tc,),
        in_specs=[_rows(tc, 2 * D),
                  pl.BlockSpec((CONV_HALO, 2 * D), lambda i: (jnp.maximum(i * per - 1, 0), 0)),
                  _whole((CONV_WIDTH, D)), _whole((1, D)), _whole((1, D)), _whole((1, D)),
                  _whole((D, D)), _whole((1, D)), _rows(tc, D), _whole((1, D))],
        out_specs=[_rows(tc, D), _rows(tc, D), _rows(tc, D), _rows(tc, D)],
        out_shape=[SDS((T, D), F32), SDS((T, D), BF16), SDS((T, D), F32), SDS((T, D), BF16)],
        scratch_shapes=[pltpu.VMEM((V7X_SUBLANES, tc + CONV_HALO, D), F32)],
        compiler_params=_params("parallel"),
    )(a, a, w_dw, b_dw, ln_g, ln_b, w_pw2, b_pw2, res, g_next)


def _ffn_down_loss(name, s, w2, layer, res, g, target):
    _, T, n = s.shape
    D = w2.shape[3]
    tm = min(512, T)

    def body(s_ref, w_ref, r_ref, g_ref, t_ref, dh_ref, loss_ref, dg_ref):
        @pl.when(pl.program_id(0) == 0)
        def _():
            loss_ref[...] = jnp.zeros_like(loss_ref)
            dg_ref[...] = jnp.zeros_like(dg_ref)

        hh = r_ref[...]
        for j in range(N_CHIPS):
            hh = hh + jnp.dot(s_ref[j], w_ref[j], preferred_element_type=F32)
        r = _rms_rstd(hh)
        g = g_ref[...]
        d = hh * r * g - t_ref[...]
        loss_ref[...] += 0.5 * jnp.sum(jnp.mean(d * d, axis=-1, keepdims=True), axis=0, keepdims=True)
        dout = d * (1.0 / D)
        dg_ref[...] += jnp.sum(dout * (hh * r), axis=0, keepdims=True)
        dxh = dout * g
        dh_ref[...] = r * dxh - hh * (r * r * r) * jnp.mean(dxh * hh, axis=-1, keepdims=True)

    return pl.pallas_call(
        body, name=name, grid=(T // tm,),
        in_specs=[pl.BlockSpec((N_CHIPS, tm, n), lambda i: (0, i, 0)),
                  pl.BlockSpec((N_CHIPS, None, n, D), lambda i: (0, layer, 0, 0)), _rows(tm, D),
                  _whole((1, D)), _rows(tm, D)],
        out_specs=[_rows(tm, D), _whole((1, 1)), _whole((1, D))],
        out_shape=[SDS((T, D), F32), SDS((1, 1), F32), SDS((1, D), F32)],
        compiler_params=_params("arbitrary"),
    )(s, w2, res, g, target)


def _ffn_bwd_down(name, dh, w2, layer, act, gate_grad, s, after=None):
    T, D = dh.shape
    n = w2.shape[2]
    tm = min(256, T)

    def body(dh_ref, w2_ref, act_ref, gg_ref, s_ref, *rest):
        dg1_ref, dg3_ref, dw_ref = rest[-3:]

        @pl.when(pl.program_id(0) == 0)
        def _():
            dw_ref[...] = jnp.zeros_like(dw_ref)

        dhb = dh_ref[...].astype(BF16)
        for j in range(N_CHIPS):
            ds = _dot_nt(dhb, w2_ref[j])
            dg1_ref[j] = (ds * gg_ref[j].astype(F32)).astype(BF16)
            dg3_ref[j] = (ds * act_ref[j].astype(F32)).astype(BF16)
            dw_ref[j] += _dot_tn(s_ref[j], dhb)

    slabs = pl.BlockSpec((N_CHIPS, tm, n), lambda i: (0, i, 0))
    hidden = SDS((N_CHIPS, T, n), BF16)
    return pl.pallas_call(
        body, name=name, grid=(T // tm,),
        in_specs=[_rows(tm, D),
                  pl.BlockSpec((N_CHIPS, None, n, D), lambda i: (0, layer, 0, 0), pipeline_mode=pl.Buffered(1)),
                  slabs, slabs, slabs] + ([] if after is None else [pl.BlockSpec(memory_space=pl.ANY)]),
        out_specs=[slabs, slabs, _whole((N_CHIPS, n, D))],
        out_shape=[hidden, hidden, SDS((N_CHIPS, n, D), F32)],
        compiler_params=_params("arbitrary"),
    )(dh, w2, act, gate_grad, s, *([] if after is None else [after]))


def _dot_tn(a, b):
    return lax.dot_general(a.astype(BF16), b.astype(BF16), (((0,), (0,)), ((), ())), preferred_element_type=F32)


def _dot_nt(a, b):
    return lax.dot_general(a.astype(BF16), b, (((1,), (1,)), ((), ())), preferred_element_type=F32)


def _mm_tn(name, a, b, col_chunks=1, after=None):
    a_slabs, b_slabs = a.ndim == 3, b.ndim == 3
    T = a.shape[-2]
    tt = min(1024, T)
    ka, nb = a.shape[-1], b.shape[-1]
    if a_slabs or b_slabs:
        out_dims = (N_CHIPS, ka, nb)
    elif col_chunks > 1:
        out_dims = (col_chunks, ka, nb // col_chunks)
    else:
        out_dims = (ka, nb)

    def body(a_ref, b_ref, *rest):
        o_ref = rest[-1]

        @pl.when(pl.program_id(0) == 0)
        def _():
            o_ref[...] = jnp.zeros_like(o_ref)

        if a_slabs:
            bb = b_ref[...].astype(BF16)
            for j in range(N_CHIPS):
                o_ref[j] += _dot_tn(a_ref[j], bb)
        elif b_slabs:
            aa = a_ref[...].astype(BF16)
            for j in range(N_CHIPS):
                o_ref[j] += _dot_tn(aa, b_ref[j])
        elif col_chunks > 1:
            aa = a_ref[...].astype(BF16)
            w = nb // col_chunks
            for j in range(col_chunks):
                o_ref[j] += _dot_tn(aa, b_ref[:, j * w:(j + 1) * w])
        else:
            o_ref[...] += _dot_tn(a_ref[...], b_ref[...])

    def spec(arr, slabs):
        if slabs:
            return pl.BlockSpec((N_CHIPS, tt, arr.shape[-1]), lambda t: (0, t, 0))
        return _rows(tt, arr.shape[-1])

    return pl.pallas_call(
        body, name=name, grid=(T // tt,),
        in_specs=[spec(a, a_slabs), spec(b, b_slabs)] + ([] if after is None else [pl.BlockSpec(memory_space=pl.ANY)]),
        out_specs=_whole(out_dims),
        out_shape=SDS(out_dims, F32),
        compiler_params=_params("arbitrary"),
    )(a, b, *([] if after is None else [after]))


def _mm_nt_normbwd(name, pairs, h, g, dh, after):
    T, D = h.shape
    tm = min(512, T)
    n_pairs = len(pairs)
    kinds = ["slabs" if dy.ndim == 3 else ("quarters" if w.ndim == 3 else "plain") for dy, w, _ in pairs]

    def body(*refs):
        dy_refs = refs[:n_pairs]
        w_refs = refs[n_pairs:2 * n_pairs]
        h_ref, g_ref, dh_ref, _, o_ref, dg_ref, cs_ref = refs[2 * n_pairs:]

        @pl.when(pl.program_id(0) == 0)
        def _():
            dg_ref[...] = jnp.zeros_like(dg_ref)
            cs_ref[...] = jnp.zeros_like(cs_ref)

        df = jnp.zeros((tm, D), F32)
        for dy_ref, w_ref, kd in zip(dy_refs, w_refs, kinds):
            if kd == "slabs":
                for j in range(N_CHIPS):
                    df = df + jnp.dot(dy_ref[j], w_ref[j], preferred_element_type=F32)
            elif kd == "quarters":
                n = w_ref.shape[2]
                for j in range(N_CHIPS):
                    df = df + _dot_nt(dy_ref[:, j * n:(j + 1) * n], w_ref[j])
            else:
                df = df + jnp.dot(dy_ref[...], w_ref[...], preferred_element_type=F32)
        hh = h_ref[...]
        r = _rms_rstd(hh)
        dg_ref[...] += jnp.sum(df * (hh * r), axis=0, keepdims=True)
        dxh = df * g_ref[...]
        out = dh_ref[...] + (r * dxh - hh * (r * r * r) * jnp.mean(dxh * hh, axis=-1, keepdims=True))
        o_ref[...] = out
        cs_ref[...] += jnp.sum(out, axis=0, keepdims=True)

    dy_specs, w_specs = [], []
    for (dy, w, layer), kd in zip(pairs, kinds):
        if kd == "slabs":
            dy_specs.append(pl.BlockSpec((N_CHIPS, tm, dy.shape[2]), lambda i: (0, i, 0)))
            w_specs.append(pl.BlockSpec((N_CHIPS, None, w.shape[2], D),
                                        functools.partial(lambda i, layer: (0, layer, 0, 0), layer=layer),
                                        pipeline_mode=pl.Buffered(1)))
        else:
            dy_specs.append(_rows(tm, dy.shape[1]))
            w_specs.append(_whole(w.shape))

    return pl.pallas_call(
        body, name=name, grid=(T // tm,),
        in_specs=[*dy_specs, *w_specs, _rows(tm, D), _whole((1, D)), _rows(tm, D), pl.BlockSpec(memory_space=pl.ANY)],
        out_specs=[_rows(tm, D), _whole((1, D)), _whole((1, D))],
        out_shape=[SDS((T, D), F32), SDS((1, D), F32), SDS((1, D), F32)],
        compiler_params=_params("arbitrary"),
    )(*[dy for dy, _, _ in pairs], *[w for _, w, _ in pairs], h, g, dh, after)


def _mm_nt(name, dy, w, out_dtype):
    T, N = dy.shape
    K = w.shape[0]
    tm = min(512, T)

    def body(dy_ref, w_ref, o_ref):
        o_ref[...] = lax.dot_general(dy_ref[...].astype(BF16), w_ref[...], (((1,), (1,)), ((), ())),
                                     preferred_element_type=F32).astype(out_dtype)

    return pl.pallas_call(
        body, name=name, grid=(T // tm,),
        in_specs=[_rows(tm, N), _whole((K, N))],
        out_specs=_rows(tm, K),
        out_shape=SDS((T, K), out_dtype),
        compiler_params=_params("parallel"),
    )(dy, w)


def _conv_bwd(dh, w_pw2, c, a, w_dw, ln_g, ln_b):
    T, D = c.shape
    tc = _conv_tile(T)
    per = tc // CONV_HALO
    n_tiles = T // tc
    last_halo = T // CONV_HALO - 1

    def ln_bwd(dact_v, c_v, lg, lb):
        xc = c_v - jnp.mean(c_v, axis=-1, keepdims=True)
        rstd = lax.rsqrt(jnp.mean(xc * xc, axis=-1, keepdims=True) + LN_EPS)
        z = xc * rstd
        _, dsilu = _silu_and_grad(z * lg + lb)
        dl = dact_v * dsilu
        dz = dl * lg
        dc = rstd * (dz - jnp.mean(dz, axis=-1, keepdims=True) - z * jnp.mean(dz * z, axis=-1, keepdims=True))
        return dc, dl, z

    def body(dh_ref, dhn_ref, wp_ref, c_ref, cn_ref, a_ref, ah_ref, w_ref, lg_ref, lb_ref,
             da_ref, dlg_ref, dlb_ref, dbdw_ref, dwdw_ref, dbpw1_ref, dc_sh, u_sh, du_scr):
        i = pl.program_id(0)

        @pl.when(i == 0)
        def _():
            for ref in (dlg_ref, dlb_ref, dbdw_ref, dwdw_ref, dbpw1_ref):
                ref[...] = jnp.zeros_like(ref)

        lg, lb = lg_ref[...], lb_ref[...]
        dh_rows = jnp.concatenate([dh_ref[...].astype(BF16), dhn_ref[...].astype(BF16)], axis=0)
        dact = _dot_nt(dh_rows, wp_ref[...])
        dc, dl, z = ln_bwd(dact[:tc], c_ref[...], lg, lb)
        dlg_ref[...] += jnp.sum(dl * z, axis=0, keepdims=True)
        dlb_ref[...] += jnp.sum(dl, axis=0, keepdims=True)
        dbdw_ref[...] += jnp.sum(dc, axis=0, keepdims=True)
        dcn, _, _ = ln_bwd(dact[tc:], cn_ref[...], lg, lb)
        dc_sh[0, 0:tc, :] = dc
        dc_sh[0, tc:, :] = jnp.where(i < n_tiles - 1, dcn, 0.0)
        _fill_shifted(dc_sh, tc)

        a_v = a_ref[...].astype(F32)
        a1 = a_v[:, :D]
        sg = jax.nn.sigmoid(a_v[:, D:])
        u_sh[0, 0:CONV_HALO, :] = jnp.where(i > 0, _glu(ah_ref[...], D), 0.0)
        u_sh[0, CONV_HALO:, :] = a1 * sg
        _fill_shifted(u_sh, tc)

        _depthwise_taps(dc_sh, w_ref, [CONV_WIDTH - 1 - j for j in range(CONV_WIDTH)], None, du_scr, tc)
        _depthwise_tap_grads(dc_sh, u_sh, [CONV_FIRST_TAP + j for j in range(CONV_WIDTH)], dwdw_ref, tc)

        du = du_scr[...]
        da1 = du * sg
        da2 = du * a1 * sg * (1.0 - sg)
        da_ref[:, :D] = da1.astype(BF16)
        da_ref[:, D:] = da2.astype(BF16)
        dbpw1_ref[:, :D] += jnp.sum(da1, axis=0, keepdims=True)
        dbpw1_ref[:, D:] += jnp.sum(da2, axis=0, keepdims=True)

    nxt = lambda i: (jnp.minimum((i + 1) * per, last_halo), 0)
    return pl.pallas_call(
        body, name="conv_bwd", grid=(n_tiles,),
        in_specs=[_rows(tc, D), pl.BlockSpec((CONV_HALO, D), nxt), _whole((D, D)),
                  _rows(tc, D), pl.BlockSpec((CONV_HALO, D), nxt),
                  _rows(tc, 2 * D),
                  pl.BlockSpec((CONV_HALO, 2 * D), lambda i: (jnp.maximum(i * per - 1, 0), 0)),
                  _whole((CONV_WIDTH, D)), _whole((1, D)), _whole((1, D))],
        out_specs=[_rows(tc, 2 * D), _whole((1, D)), _whole((1, D)), _whole((1, D)),
                   _whole((CONV_HALO, D)), _whole((1, 2 * D))],
        out_shape=[SDS((T, 2 * D), BF16), SDS((1, D), F32), SDS((1, D), F32), SDS((1, D), F32),
                   SDS((CONV_HALO, D), F32), SDS((1, 2 * D), F32)],
        scratch_shapes=[pltpu.VMEM((V7X_SUBLANES, tc + CONV_HALO, D), F32),
                        pltpu.VMEM((V7X_SUBLANES, tc + CONV_HALO, D), F32), pltpu.VMEM((tc, D), F32)],
        compiler_params=_params("arbitrary"),
    )(dh, dh, w_pw2, c, c, a, a, w_dw, ln_g, ln_b)


def _attn_bwd(qkv, dao, cos, sin, probs_saved, psink_saved):
    T = qkv.shape[0]
    nb = T // ATTN_BLOCK
    qw = N_Q_HEADS * HEAD_DIM
    kw = N_KV_HEADS * HEAD_DIM

    def body(q_ref, kc_ref, kp_ref, vc_ref, vp_ref, do_ref, cos_ref, sin_ref, cosp_ref, sinp_ref, probs_ref, psink_ref,
             dq_ref, dkv_ref, dsink_ref, dbq_ref, dbkv_ref, carry, prev_scr, cur_scr, dq_scr):
        n = pl.program_id(0)

        @pl.when(n == 0)
        def _():
            for ref in (dsink_ref, dbq_ref, dbkv_ref, carry):
                ref[...] = jnp.zeros_like(ref)

        @pl.when(n == nb)
        def _():
            prev_scr[...] = jnp.zeros_like(prev_scr)

        @pl.when(n < nb)
        def _():
            prev_part = _from_previous_block(N_Q_HEADS * ATTN_BLOCK)
            half = PAIRS_PER_KV * ATTN_BLOCK
            upper = _upper_lanes((ATTN_BLOCK, 2 * HEAD_DIM))
            groups = range(N_KV_HEADS)

            def nt(a, b):
                return lax.dot_general(a, b, (((1,), (1,)), ((), ())), preferred_element_type=F32)

            def kv_grad(even_rows, odd_rows, x):
                even = lax.dot_general(even_rows, x, (((0,), (0,)), ((), ())), preferred_element_type=F32)
                odd = lax.dot_general(odd_rows, x, (((0,), (0,)), ((), ())), preferred_element_type=F32)
                t = jnp.where(upper, odd, even)
                return t + _swap_lane_halves(t)

            q = [_pair_rows(q_ref, g) for g in groups]
            do = [_pair_rows(do_ref, g) for g in groups]
            k_prev = [_kv_operands(g, kp_ref[...]) for g in groups]
            k_cur = [_kv_operands(g, kc_ref[...]) for g in groups]
            v_prev = [_kv_operands(g, vp_ref[...]) for g in groups]
            v_cur = [_kv_operands(g, vc_ref[...]) for g in groups]
            probs = probs_ref[...].astype(F32)
            dp_prev = jnp.concatenate([nt(do[g], v_prev[g][i]) for g in groups for i in range(2)], axis=0)
            dp_cur = jnp.concatenate([nt(do[g], v_cur[g][i]) for g in groups for i in range(2)], axis=0)
            dp = jnp.where(prev_part, dp_prev, dp_cur)
            delta = jnp.sum(probs * dp, axis=1, keepdims=True)
            ds_prev, ds_cur = _split_folded(probs * (dp - delta) * (HEAD_DIM ** -0.5), prev_part)
            p_prev, p_cur = _split_folded(probs_ref[...], prev_part)
            for i, h in enumerate(_all_heads()):
                rows = slice(i * ATTN_BLOCK, (i + 1) * ATTN_BLOCK)
                dsink_ref[:, h:h + 1] += jnp.sum(-(psink_ref[:, h:h + 1] * delta[rows]), axis=0, keepdims=True)
            kv_grads = []
            for g in groups:
                even, odd = slice(2 * g * half, (2 * g + 1) * half), slice((2 * g + 1) * half, (2 * g + 2) * half)
                dq = (jnp.dot(ds_prev[even], k_prev[g][0], preferred_element_type=F32)
                      + jnp.dot(ds_cur[even], k_cur[g][0], preferred_element_type=F32)
                      + jnp.dot(ds_prev[odd], k_prev[g][1], preferred_element_type=F32)
                      + jnp.dot(ds_cur[odd], k_cur[g][1], preferred_element_type=F32))
                for i in range(PAIRS_PER_KV):
                    hp = g * PAIRS_PER_KV + i
                    dq_scr[:, hp * 2 * HEAD_DIM:(hp + 1) * 2 * HEAD_DIM] = dq[i * ATTN_BLOCK:(i + 1) * ATTN_BLOCK]
                kv_grads.append((kv_grad(ds_prev[even], ds_prev[odd], q[g]), kv_grad(ds_cur[even], ds_cur[odd], q[g]),
                                 kv_grad(p_prev[even], p_prev[odd], do[g]), kv_grad(p_cur[even], p_cur[odd], do[g])))
            (dkp0, dkc0, dvp0, dvc0), (dkp1, dkc1, dvp1, dvc1) = kv_grads
            prev_scr[:, :kw] = jnp.where(upper, dkp1, dkp0)
            prev_scr[:, kw:] = jnp.where(upper, dvp1, dvp0)
            cur_scr[:, :kw] = jnp.where(upper, dkc1, dkc0)
            cur_scr[:, kw:] = jnp.where(upper, dvc1, dvc0)
            dq_pre = _rope_transposed(dq_scr[...], cos_ref, sin_ref)
            dq_ref[...] = dq_pre.astype(BF16)
            dbq_ref[...] += jnp.sum(dq_pre, axis=0, keepdims=True)

        tot = carry[...] + prev_scr[...]
        dk_pre = _rope_transposed(tot[:, :kw], cosp_ref, sinp_ref)
        dkv_ref[:, :kw] = dk_pre.astype(BF16)
        dkv_ref[:, kw:] = tot[:, kw:].astype(BF16)
        dbkv_ref[:, :kw] += jnp.sum(dk_pre, axis=0, keepdims=True)
        dbkv_ref[:, kw:] += jnp.sum(tot[:, kw:], axis=0, keepdims=True)

        @pl.when(n < nb)
        def _():
            carry[...] = cur_scr[...]

    cur = lambda n: (jnp.minimum(n, nb - 1), 0)
    out_lag = lambda n: (jnp.maximum(n - 1, 0), 0)
    return pl.pallas_call(
        body, name="attn_bwd", grid=(nb + 1,),
        in_specs=[*_attn_specs(T),
                  pl.BlockSpec((ATTN_BLOCK, qw), cur),
                  pl.BlockSpec((ATTN_BLOCK, V7X_LANES), cur), pl.BlockSpec((ATTN_BLOCK, V7X_LANES), cur),
                  pl.BlockSpec((ATTN_BLOCK, V7X_LANES), out_lag), pl.BlockSpec((ATTN_BLOCK, V7X_LANES), out_lag),
                  pl.BlockSpec((None, N_Q_HEADS * ATTN_BLOCK, ATTN_BLOCK), lambda n: (jnp.minimum(n, nb - 1), 0, 0)),
                  pl.BlockSpec((ATTN_BLOCK, V7X_LANES), cur)],
        out_specs=[pl.BlockSpec((ATTN_BLOCK, qw), cur), pl.BlockSpec((ATTN_BLOCK, 2 * kw), out_lag),
                   _whole((1, N_Q_HEADS)), _whole((1, qw)), _whole((1, 2 * kw))],
        out_shape=[SDS((T, qw), BF16), SDS((T, 2 * kw), BF16),
                   SDS((1, N_Q_HEADS), F32), SDS((1, qw), F32), SDS((1, 2 * kw), F32)],
        scratch_shapes=[pltpu.VMEM((ATTN_BLOCK, 2 * kw), F32), pltpu.VMEM((ATTN_BLOCK, 2 * kw), F32),
                        pltpu.VMEM((ATTN_BLOCK, 2 * kw), F32), pltpu.VMEM((ATTN_BLOCK, qw), F32)],
        compiler_params=_params("arbitrary"),
    )(qkv, qkv, qkv, qkv, qkv, dao, cos, sin, cos, sin, probs_saved, psink_saved)


def _local_step(x, target, p, reduce_begin, reduce_send):
    T, D = x.shape
    cos, sin = _rope_tables(T)
    qw = N_Q_HEADS * HEAD_DIM
    nm, nf = p["norm_mix"], p["norm_ffn"]

    y0, qkv = _qkv_proj(x, nm[0:1], p["attn_w_qkv"], p["attn_b_qkv"], cos, sin, p["gather_started"])
    ao, attn_probs, sink_probs = _attn_fwd(qkv, p["attn_sinks"])
    h1, f0 = _mm_res("attn_out", ao, p["attn_w_o"], p["attn_b_o"], x, nf[0:1])
    p = {**p, **p["other_weights"](h1)}
    w1, w3 = p["ffn_w1"], p["ffn_w3"]
    act0, gg0, s0 = _ffn_up("ffn0_up", f0, w1, w3, 0, after=p["swap_started"])
    p = {**p, **p["rest_of_weights"](s0)}
    w2 = p["ffn_w2"]
    h2 = _ffn_down("ffn0_down", s0, w2, 0, h1)
    y1, a = _pw1_proj(h2, nm[1:2], p["conv_w_pw1"], p["conv_b_pw1"])
    c, act, h3, f1 = _conv_fwd(a, p["conv_w_dw"], p["conv_b_dw"], p["conv_ln_g"], p["conv_ln_b"],
                               p["conv_w_pw2"], p["conv_b_pw2"], h2, nf[1:2])
    act1, gg1, s1 = _ffn_up("ffn1_up", f1, w1, w3, 1)
    dh4, loss, d_norm_final = _ffn_down_loss("ffn1_down_loss", s1, w2, 1, h3, p["norm_final"], target)

    g = {}
    dg1, dg3, dw2_1 = _ffn_bwd_down("ffn1_bwd_down", dh4, w2, 1, act1, gg1, s1)
    dw1_1 = _mm_tn("ffn1_dw1", dg1, f1)
    dw3_1 = _mm_tn("ffn1_dw3", dg3, f1)
    begun = reduce_begin("ffn1", {("ffn_w1", 1): dw1_1, ("ffn_w3", 1): dw3_1, ("ffn_w2", 1): dw2_1})
    dh3, dnf1, db_pw2 = _mm_nt_normbwd("ffn1_bwd_in", [(dg1, w1, 1), (dg3, w3, 1)], h3, nf[1:2], dh4, begun)
    sent = reduce_send("ffn1", dh3)

    dw_pw2 = _mm_tn("conv_dw_pw2", act, dh3, after=sent)
    da, d_ln_g, d_ln_b, d_b_dw, d_w_dw, d_b_pw1 = _conv_bwd(dh3, p["conv_w_pw2"], c, a, p["conv_w_dw"],
                                                            p["conv_ln_g"], p["conv_ln_b"])
    dw_pw1 = _mm_tn("conv_dw_pw1", y1, da, col_chunks=N_CHIPS)
    begun = reduce_begin("conv", {("conv_w_pw2", 0): dw_pw2.reshape(N_CHIPS, -1, D), ("conv_w_pw1", 0): dw_pw1})
    dh2, dnm1, _ = _mm_nt_normbwd("conv_bwd_in", [(da, p["conv_w_pw1"], None)], h2, nm[1:2], dh3, begun)
    sent = reduce_send("conv", dh2)

    dg1, dg3, dw2_0 = _ffn_bwd_down("ffn0_bwd_down", dh2, w2, 0, act0, gg0, s0, after=sent)
    dw1_0 = _mm_tn("ffn0_dw1", dg1, f0)
    dw3_0 = _mm_tn("ffn0_dw3", dg3, f0)
    begun = reduce_begin("ffn0", {("ffn_w1", 0): dw1_0, ("ffn_w3", 0): dw3_0, ("ffn_w2", 0): dw2_0})
    dh1, dnf0, db_o = _mm_nt_normbwd("ffn0_bwd_in", [(dg1, w1, 0), (dg3, w3, 0)], h1, nf[0:1], dh2, begun)
    sent = reduce_send("ffn0", dh1)

    dw_o = _mm_tn("attn_dw_o", ao, dh1, after=sent)
    dao = _mm_nt("attn_bwd_out", dh1, p["attn_w_o"], BF16)
    dq, dkv, d_sinks, dbq, dbkv = _attn_bwd(qkv, dao, cos, sin, attn_probs, sink_probs)
    dwq = _mm_tn("attn_dw_q", dq, y0)
    dwkv = _mm_tn("attn_dw_kv", dkv, y0)
    wqkv = p["attn_w_qkv"]
    dwqkv = jnp.concatenate([dwq, dwkv], axis=0).reshape(N_CHIPS, -1, D)
    begun = reduce_begin("attn", {("attn_w_o", 0): dw_o.reshape(N_CHIPS, -1, D), ("attn_w_qkv", 0): dwqkv})
    sent = reduce_send("attn", begun)
    dx, dnm0, _ = _mm_nt_normbwd("attn_bwd_in", [(dq, wqkv[:qw], None), (dkv, wqkv[qw:], None)], x, nm[0:1], dh1,
                                 sent)

    g["norm_mix"] = jnp.concatenate([dnm0, dnm1], axis=0)
    g["norm_ffn"] = jnp.concatenate([dnf0, dnf1], axis=0)
    g["attn_b_qkv"] = jnp.concatenate([dbq, dbkv], axis=1)
    g["attn_sinks"] = d_sinks
    g["attn_b_o"] = db_o
    g["conv_b_pw1"] = d_b_pw1
    g["conv_w_dw"] = d_w_dw[:CONV_WIDTH]
    g["conv_b_dw"] = d_b_dw
    g["conv_ln_g"] = d_ln_g
    g["conv_ln_b"] = d_ln_b
    g["conv_b_pw2"] = db_pw2
    g["norm_final"] = d_norm_final
    return loss, dx, g


ANY = pl.BlockSpec(memory_space=pl.ANY)
VMEM_WHOLE = pl.BlockSpec(memory_space=pltpu.VMEM)


def _my_place():
    return lax.axis_index("x"), lax.axis_index("y"), lax.axis_index("c")


def _other_chips(x, y):
    places = [(1 - x, y), (x, 1 - y), (1 - x, 1 - y)]
    return [(bx, by, 2 * bx + by) for bx, by in places]


def _cast_into_slot(name, gathered, shard, chip_idx, after=None):
    rows, cols = shard.shape
    tr = _pack_row_tile(rows)

    def body(k_ref, s_ref, *rest):
        rest[-1][...] = s_ref[...].astype(BF16)

    return pl.pallas_call(
        body, name=name,
        grid_spec=pltpu.PrefetchScalarGridSpec(
            num_scalar_prefetch=1, grid=(rows // tr,),
            in_specs=[pl.BlockSpec((tr, cols), lambda i, k_ref: (i, 0)), pl.BlockSpec(memory_space=pl.ANY)]
            + ([] if after is None else [pl.BlockSpec(memory_space=pl.ANY)]),
            out_specs=pl.BlockSpec((None, tr, cols), lambda i, k_ref: (k_ref[0], i, 0))),
        out_shape=SDS(gathered.shape, BF16),
        input_output_aliases={2: 0},
        compiler_params=_params("parallel"),
    )(chip_idx, shard, gathered, *([] if after is None else [after]))


def _row_halves(ref, c):
    half = ref.shape[1] // 2
    return pl.ds(pl.multiple_of(c * half, 16), half), pl.ds(pl.multiple_of((1 - c) * half, 16), half)


def _gather_ici_copies(refs, send_sems, recv_sems):
    x, y, c = _my_place()
    k = 2 * x + y
    pairs = []
    for i, ref in enumerate(refs):
        mine, _ = _row_halves(ref, c)
        for j, (bx, by, kb) in enumerate(_other_chips(x, y)):
            sems = dict(send_sem=send_sems.at[3 * i + j], recv_sem=recv_sems.at[3 * i + j], device_id_type=MESH)
            send = pltpu.make_async_remote_copy(src_ref=ref.at[k, mine], dst_ref=ref.at[k, mine],
                                                device_id=(bx, by, c), **sems)
            arrival = pltpu.make_async_remote_copy(src_ref=ref.at[kb, mine], dst_ref=ref.at[kb, mine],
                                                   device_id=(bx, by, c), **sems)
            pairs.append((send, arrival))
    return pairs


def _gather_d2d_copies(refs, send_sems, recv_sems):
    x, y, c = _my_place()
    pairs = []
    for i, ref in enumerate(refs):
        mine, theirs = _row_halves(ref, c)
        for j, (_, _, kb) in enumerate(_other_chips(x, y)):
            sem = 3 * i + j
            sems = dict(send_sem=send_sems.at[sem], recv_sem=recv_sems.at[sem], device_id=(x, y, 1 - c),
                        device_id_type=MESH)
            send = pltpu.make_async_remote_copy(src_ref=ref.at[kb, mine], dst_ref=ref.at[kb, mine], **sems)
            arrival = pltpu.make_async_remote_copy(src_ref=ref.at[kb, theirs], dst_ref=ref.at[kb, theirs], **sems)
            pairs.append((send, arrival))
    return pairs


def _run_copies(pairs):
    for send, _ in pairs:
        send.start()
    for send, arrival in pairs:
        send.wait_send()
        arrival.wait_recv()


def _gather_stage_copies(stage, refs, send_sems, recv_sems):
    if stage == "ici":
        return _gather_ici_copies(refs, send_sems, recv_sems)
    return _gather_d2d_copies(refs, send_sems, recv_sems)


def _gather_start(name, gathered, after, stage="ici"):
    n_w = len(gathered)

    def body(*refs):
        in_refs = refs[:n_w]
        send_sems, recv_sems = refs[n_w + 1:n_w + 3]
        for send, _ in _gather_stage_copies(stage, in_refs, send_sems, recv_sems):
            send.start()
        refs[-1][...] = jnp.zeros_like(refs[-1])

    out = pl.pallas_call(
        body, name=name,
        out_shape=(pltpu.SemaphoreType.DMA((3 * n_w,)), pltpu.SemaphoreType.DMA((3 * n_w,)),
                   *[pltpu.HBM(g.shape, g.dtype) for g in gathered], SDS((8, V7X_LANES), F32)),
        in_specs=[*[HBM_SPEC] * n_w, ANY], out_specs=(SEM_SPEC, SEM_SPEC, *[HBM_SPEC] * n_w, VMEM_WHOLE),
        input_output_aliases={i: 2 + i for i in range(n_w)},
        compiler_params=pltpu.CompilerParams(has_side_effects=DATAFLOW),
    )(*[pltpu.with_memory_space_constraint(g, pltpu.HBM) for g in gathered], after)
    return out[0], out[1], list(out[2:2 + n_w]), out[-1]


def _gather_wait(name, send_sems, recv_sems, gathered, after, stage="ici"):
    n_w = len(gathered)

    def body(*refs):
        in_refs = refs[:n_w]
        send_sems, recv_sems = refs[n_w:n_w + 2]
        for send, arrival in _gather_stage_copies(stage, in_refs, send_sems, recv_sems):
            send.wait_send()
            arrival.wait_recv()

    out = pl.pallas_call(
        body, name=name, out_shape=tuple(pltpu.HBM(g.shape, g.dtype) for g in gathered),
        in_specs=[*[HBM_SPEC] * n_w, SEM_SPEC, SEM_SPEC, ANY], out_specs=tuple([HBM_SPEC] * n_w),
        input_output_aliases={i: i for i in range(n_w)},
        compiler_params=pltpu.CompilerParams(has_side_effects=DATAFLOW),
    )(*gathered, send_sems, recv_sems, after)
    return list(out)


def _swap_fetched_with_sibling(name, gathered, after=None):
    n_w = len(gathered)
    extra = [] if after is None else [after]

    def body(*refs):
        in_refs = refs[:n_w]
        send_sems, recv_sems = refs[-2:]
        _run_copies(_gather_d2d_copies(in_refs, send_sems, recv_sems))

    return pl.pallas_call(
        body, name=name, out_shape=[SDS(g.shape, g.dtype) for g in gathered],
        in_specs=[ANY] * (n_w + len(extra)), out_specs=[ANY] * n_w, input_output_aliases={i: i for i in range(n_w)},
        scratch_shapes=[pltpu.SemaphoreType.DMA((3 * n_w,)), pltpu.SemaphoreType.DMA((3 * n_w,))],
    )(*gathered, *extra)


def _sibling_swap_copies(g_refs, land_refs, send_sems, recv_sems):
    x, y, c = _my_place()
    copies = []
    for i, g_ref in enumerate(g_refs):
        half = g_ref.shape[1] // 2
        theirs = pl.ds(pl.multiple_of((1 - c) * half, 8), half)
        copies.append(pltpu.make_async_remote_copy(
            src_ref=g_ref.at[:, theirs], dst_ref=land_refs[i], send_sem=send_sems.at[i], recv_sem=recv_sems.at[i],
            device_id=(x, y, 1 - c), device_id_type=MESH))
    return copies


def _sibling_swap_start(name, grads):
    n_g = len(grads)

    def body(*refs):
        g_refs, land_refs = refs[:n_g], refs[n_g:2 * n_g]
        send_sems, recv_sems = refs[2 * n_g:2 * n_g + 2]
        for cp in _sibling_swap_copies(g_refs, land_refs, send_sems, recv_sems):
            cp.start()
        refs[-1][...] = jnp.zeros_like(refs[-1])

    lands = [pltpu.with_memory_space_constraint(lax.empty((g.shape[0], g.shape[1] // 2, g.shape[2]), g.dtype),
                                                pltpu.HBM) for g in grads]
    out = pl.pallas_call(
        body, name=name,
        out_shape=(pltpu.SemaphoreType.DMA((n_g,)), pltpu.SemaphoreType.DMA((n_g,)),
                   *[pltpu.HBM(g.shape, g.dtype) for g in grads], *[pltpu.HBM(l.shape, l.dtype) for l in lands],
                   SDS((8, V7X_LANES), F32)),
        in_specs=[HBM_SPEC] * (2 * n_g), out_specs=(SEM_SPEC, SEM_SPEC, *[HBM_SPEC] * (2 * n_g), VMEM_WHOLE),
        input_output_aliases={i: 2 + i for i in range(2 * n_g)},
        compiler_params=pltpu.CompilerParams(has_side_effects=DATAFLOW),
    )(*[pltpu.with_memory_space_constraint(g, pltpu.HBM) for g in grads], *lands)
    return out[0], out[1], list(out[2:2 + n_g]), list(out[2 + n_g:2 + 2 * n_g]), out[-1]


def _sibling_swap_wait(name, send_sems, recv_sems, grads, lands, after):
    n_g = len(grads)

    def body(*refs):
        g_refs, land_refs = refs[:n_g], refs[n_g:2 * n_g]
        send_sems, recv_sems = refs[2 * n_g:2 * n_g + 2]
        for cp in _sibling_swap_copies(g_refs, land_refs, send_sems, recv_sems):
            cp.wait_send()
            cp.wait_recv()

    out = pl.pallas_call(
        body, name=name,
        out_shape=(*[pltpu.HBM(g.shape, g.dtype) for g in grads], *[pltpu.HBM(l.shape, l.dtype) for l in lands]),
        in_specs=[*[HBM_SPEC] * (2 * n_g), SEM_SPEC, SEM_SPEC, ANY], out_specs=tuple([HBM_SPEC] * (2 * n_g)),
        input_output_aliases={i: i for i in range(2 * n_g)},
        compiler_params=pltpu.CompilerParams(has_side_effects=DATAFLOW),
    )(*grads, *lands, send_sems, recv_sems, after)
    return list(out[:n_g]), list(out[n_g:])


def _pack_row_tile(rows):
    for t in range(min(rows, 512), 7, -1):
        if rows % t == 0 and t % 8 == 0:
            return t
    return rows


def _add_sibling_half(name, grads, from_sibling, c_idx):
    n, R, w = grads.shape
    half = R // 2
    tr = _pack_row_tile(half)
    steps = half // tr

    def body(c_ref, g_ref, s_ref, o_ref):
        o_ref[...] = (g_ref[...] + s_ref[...]).astype(BF16)

    return pl.pallas_call(
        body, name=name,
        grid_spec=pltpu.PrefetchScalarGridSpec(
            num_scalar_prefetch=1, grid=(n, steps),
            in_specs=[pl.BlockSpec((1, tr, w), lambda j, i, c_ref: (j, c_ref[0] * steps + i, 0)),
                      pl.BlockSpec((1, tr, w), lambda j, i, c_ref: (j, i, 0))],
            out_specs=pl.BlockSpec((1, tr, w), lambda j, i, c_ref: (j, i, 0))),
        out_shape=SDS((n, half, w), BF16),
        compiler_params=_params("parallel", "parallel"),
    )(c_idx, grads, from_sibling)


HBM_SPEC = pl.BlockSpec(memory_space=pltpu.HBM)
SEM_SPEC = pl.BlockSpec(memory_space=pltpu.SEMAPHORE)
DATAFLOW = pltpu.SideEffectType.DATAFLOW_SIDE_EFFECTING


def _chip_scatter_copies(p_refs, land_refs, send_sems, recv_sems):
    x, y, c = _my_place()
    return [pltpu.make_async_remote_copy(
        src_ref=p_refs[i].at[kb], dst_ref=land_refs[i].at[j], send_sem=send_sems.at[3 * i + j],
        recv_sem=recv_sems.at[3 * i + j], device_id=(bx, by, c), device_id_type=MESH)
        for i in range(len(p_refs)) for j, (bx, by, kb) in enumerate(_other_chips(x, y))]


def _scatter_start(name, partials):
    n_p = len(partials)

    def body(*refs):
        p_refs, land_refs = refs[:n_p], refs[n_p:2 * n_p]
        send_sems, recv_sems = refs[2 * n_p:2 * n_p + 2]
        for cp in _chip_scatter_copies(p_refs, land_refs, send_sems, recv_sems):
            cp.start()
        refs[-1][...] = jnp.zeros_like(refs[-1])

    lands = [pltpu.with_memory_space_constraint(lax.empty((N_CHIPS - 1,) + p.shape[1:], p.dtype), pltpu.HBM)
             for p in partials]
    out = pl.pallas_call(
        body, name=name,
        out_shape=(pltpu.SemaphoreType.DMA((3 * n_p,)), pltpu.SemaphoreType.DMA((3 * n_p,)),
                   *[pltpu.HBM(p.shape, p.dtype) for p in partials], *[pltpu.HBM(l.shape, l.dtype) for l in lands],
                   SDS((8, V7X_LANES), F32)),
        in_specs=[HBM_SPEC] * (2 * n_p), out_specs=(SEM_SPEC, SEM_SPEC, *[HBM_SPEC] * (2 * n_p), VMEM_WHOLE),
        input_output_aliases={i: 2 + i for i in range(2 * n_p)},
        compiler_params=pltpu.CompilerParams(has_side_effects=DATAFLOW),
    )(*[pltpu.with_memory_space_constraint(p, pltpu.HBM) for p in partials], *lands)
    return out[0], out[1], list(out[2:2 + n_p]), list(out[2 + n_p:2 + 2 * n_p]), out[-1]


def _scatter_wait(name, send_sems, recv_sems, partials, lands, after):
    n_p = len(partials)

    def body(*refs):
        p_refs, land_refs = refs[:n_p], refs[n_p:2 * n_p]
        send_sems, recv_sems = refs[2 * n_p:2 * n_p + 2]
        for cp in _chip_scatter_copies(p_refs, land_refs, send_sems, recv_sems):
            cp.wait_send()
            cp.wait_recv()

    out = pl.pallas_call(
        body, name=name,
        out_shape=(*[pltpu.HBM(p.shape, p.dtype) for p in partials], *[pltpu.HBM(l.shape, l.dtype) for l in lands]),
        in_specs=[*[HBM_SPEC] * (2 * n_p), SEM_SPEC, SEM_SPEC, ANY], out_specs=tuple([HBM_SPEC] * (2 * n_p)),
        input_output_aliases={i: i for i in range(2 * n_p)},
        compiler_params=pltpu.CompilerParams(has_side_effects=DATAFLOW),
    )(*partials, *lands, send_sems, recv_sems, after)
    return list(out[:n_p]), list(out[n_p:])


def _sum_chip_partials(name, grads, from_sibling, received, shard, layer, place):
    n, half, w = from_sibling.shape
    tr = _pack_row_tile(half)
    steps = half // tr

    def body(place_ref, g_ref, s_ref, r_ref, shard_ref, o_ref):
        own = g_ref[0] + s_ref[0]
        o_ref[...] = ((own + r_ref[0].astype(F32)) + r_ref[1].astype(F32)) + r_ref[2].astype(F32)

    return pl.pallas_call(
        body, name=name,
        grid_spec=pltpu.PrefetchScalarGridSpec(
            num_scalar_prefetch=1, grid=(steps,),
            in_specs=[pl.BlockSpec((1, tr, w), lambda i, place_ref: (place_ref[0], place_ref[1] * steps + i, 0)),
                      pl.BlockSpec((1, tr, w), lambda i, place_ref: (place_ref[0], i, 0)),
                      pl.BlockSpec((n - 1, tr, w), lambda i, place_ref: (0, i, 0)),
                      pl.BlockSpec(memory_space=pl.ANY)],
            out_specs=pl.BlockSpec((tr, w), lambda i, place_ref: ((2 * layer + place_ref[1]) * steps + i, 0))),
        out_shape=SDS(shard.shape, F32),
        input_output_aliases={4: 0},
        compiler_params=_params("parallel"),
    )(place, grads, from_sibling, received, shard)


def _join_halves(shards, layers):
    n_s = len(shards)
    n_sem = sum(layers)

    def body(*refs):
        in_refs = refs[:n_s]
        send_sems, recv_sems = refs[2 * n_s:]
        x, y, c = _my_place()
        copies, sem = [], 0
        for ref, n_layers in zip(in_refs, layers):
            half = ref.shape[0] // (2 * n_layers)
            for layer in range(n_layers):
                mine = pl.ds(pl.multiple_of(layer * 2 * half + c * half, 8), half)
                theirs = pl.ds(pl.multiple_of(layer * 2 * half + (1 - c) * half, 8), half)
                send = pltpu.make_async_remote_copy(
                    src_ref=ref.at[mine], dst_ref=ref.at[mine], send_sem=send_sems.at[sem], recv_sem=recv_sems.at[sem],
                    device_id=(x, y, 1 - c), device_id_type=MESH)
                send.start()
                arrival = pltpu.make_async_remote_copy(
                    src_ref=ref.at[theirs], dst_ref=ref.at[theirs], send_sem=send_sems.at[sem],
                    recv_sem=recv_sems.at[sem], device_id=(x, y, 1 - c), device_id_type=MESH)
                copies.append((send, arrival))
                sem += 1
        for send, arrival in copies:
            send.wait_send()
            arrival.wait_recv()

    return pl.pallas_call(
        body, name="join_halves", out_shape=[SDS(s.shape, s.dtype) for s in shards],
        in_specs=[ANY] * n_s, out_specs=[ANY] * n_s,
        input_output_aliases={i: i for i in range(n_s)},
        scratch_shapes=[pltpu.SemaphoreType.DMA((n_sem,)), pltpu.SemaphoreType.DMA((n_sem,))],
    )(*shards)


def _all_to_all_copies(v_ref, land_ref, send_sems, recv_sems):
    x, y, c = _my_place()
    me = 4 * x + 2 * y + c
    pairs = []
    for k in range(1, N_DEV):
        px, py, pc = (1 - x if k & 4 else x), (1 - y if k & 2 else y), (1 - c if k & 1 else c)
        sems = dict(send_sem=send_sems.at[k - 1], recv_sem=recv_sems.at[k - 1], device_id=(px, py, pc),
                    device_id_type=MESH)
        send = pltpu.make_async_remote_copy(src_ref=v_ref, dst_ref=land_ref.at[me], **sems)
        arrival = pltpu.make_async_remote_copy(src_ref=v_ref, dst_ref=land_ref.at[4 * px + 2 * py + pc], **sems)
        pairs.append((send, arrival))
    return pairs


def _small_reduce_start(name, v):
    def body(v_ref, land_ref, send_sems, recv_sems, v_out, land_out):
        for send, _ in _all_to_all_copies(v_ref, land_ref, send_sems, recv_sems):
            send.start()

    land = pltpu.with_memory_space_constraint(jnp.zeros((N_DEV,) + v.shape, v.dtype), pltpu.HBM)
    return pl.pallas_call(
        body, name=name,
        out_shape=(pltpu.SemaphoreType.DMA((N_DEV - 1,)), pltpu.SemaphoreType.DMA((N_DEV - 1,)),
                   pltpu.HBM(v.shape, v.dtype), pltpu.HBM(land.shape, land.dtype)),
        in_specs=[HBM_SPEC, HBM_SPEC], out_specs=(SEM_SPEC, SEM_SPEC, HBM_SPEC, HBM_SPEC),
        input_output_aliases={0: 2, 1: 3},
        compiler_params=pltpu.CompilerParams(has_side_effects=DATAFLOW),
    )(pltpu.with_memory_space_constraint(v, pltpu.HBM), land)


def _small_reduce_wait(name, send_sems, recv_sems, v, land, after):
    def body(v_ref, land_ref, send_sems, recv_sems, after_ref, v_out, land_out):
        for send, arrival in _all_to_all_copies(v_ref, land_ref, send_sems, recv_sems):
            send.wait_send()
            arrival.wait_recv()

    return pl.pallas_call(
        body, name=name, out_shape=(pltpu.HBM(v.shape, v.dtype), pltpu.HBM(land.shape, land.dtype)),
        in_specs=[HBM_SPEC, HBM_SPEC, SEM_SPEC, SEM_SPEC, ANY], out_specs=(HBM_SPEC, HBM_SPEC),
        input_output_aliases={0: 0, 1: 1},
        compiler_params=pltpu.CompilerParams(has_side_effects=DATAFLOW),
    )(v, land, send_sems, recv_sems, after)


def _sum_device_slots(name, v, land, me):
    r, w = v.shape

    def body(me_ref, v_ref, land_ref, o_ref):
        mine = v_ref[...]
        acc = jnp.where(me_ref[0] == 0, mine, land_ref[0])
        for d in range(1, N_DEV):
            acc = acc + jnp.where(me_ref[0] == d, mine, land_ref[d])
        o_ref[...] = acc

    return pl.pallas_call(
        body, name=name,
        grid_spec=pltpu.PrefetchScalarGridSpec(
            num_scalar_prefetch=1, grid=(1,),
            in_specs=[pl.BlockSpec((r, w), lambda i, me_ref: (0, 0)),
                      pl.BlockSpec((N_DEV, r, w), lambda i, me_ref: (0, 0, 0))],
            out_specs=pl.BlockSpec((r, w), lambda i, me_ref: (0, 0))),
        out_shape=SDS((r, w), F32),
        compiler_params=_params("arbitrary"),
    )(me, v, land)


def _adamw(name, w, g, m, v):
    rows, width = w.shape
    tr = _pack_row_tile(rows)

    def body(w_ref, g_ref, m_ref, v_ref, g_out_ref, d_ref, nm_ref, nv_ref):
        gg = g_ref[...]
        g_out_ref[...] = gg
        m_new = ADAM_B1 * m_ref[...] + (1.0 - ADAM_B1) * gg
        v_new = ADAM_B2 * v_ref[...] + (1.0 - ADAM_B2) * (gg * gg)
        m_hat = m_new / (1.0 - ADAM_B1 ** ADAM_STEP)
        v_hat = v_new / (1.0 - ADAM_B2 ** ADAM_STEP)
        d_ref[...] = -ADAM_LR * (m_hat / (jnp.sqrt(v_hat) + ADAM_EPS) + ADAM_WD * w_ref[...])
        nm_ref[...] = m_new
        nv_ref[...] = v_new

    spec = _rows(tr, width)
    return pl.pallas_call(
        body, name=name, grid=(rows // tr,),
        in_specs=[spec] * 4, out_specs=[spec] * 4,
        out_shape=[SDS((rows, width), F32)] * 4,
        compiler_params=_params("parallel"),
    )(w, g, m, v)


WEIGHT_NAMES = ['norm_mix', 'norm_ffn', 'attn_w_qkv', 'attn_b_qkv', 'attn_sinks', 'attn_w_o', 'attn_b_o',
                'conv_w_pw1', 'conv_b_pw1', 'conv_w_dw', 'conv_b_dw', 'conv_ln_g', 'conv_ln_b', 'conv_w_pw2',
                'conv_b_pw2', 'ffn_w1', 'ffn_w3', 'ffn_w2', 'norm_final']
BIG = ['attn_w_qkv', 'attn_w_o', 'conv_w_pw1', 'conv_w_pw2', 'ffn_w1', 'ffn_w3', 'ffn_w2']
COLUMN_SPLIT = ('attn_w_qkv', 'conv_w_pw1', 'ffn_w1', 'ffn_w3')
STORED_TRANSPOSED = ('attn_w_qkv', 'ffn_w1', 'ffn_w3')
SMALL_SPLIT = ['conv_b_pw1', 'conv_w_dw', 'conv_b_dw', 'conv_ln_g', 'conv_ln_b', 'conv_b_pw2']
SMALL_WHOLE = ['norm_mix', 'norm_ffn', 'attn_b_qkv', 'attn_sinks', 'attn_b_o', 'norm_final']


def _keeps_rows(shape):
    return len(shape) == 2 and shape[0] > 1 and shape[1] == PACK_W


def _pack_rows(arrays, dtype, row_multiple):
    blocks = [jnp.pad(a.astype(dtype), ((0, -a.shape[0] % V7X_SUBLANES), (0, 0)))
              for a in arrays if _keeps_rows(a.shape)]
    flat = jnp.concatenate([a.astype(dtype).reshape(-1) for a in arrays if not _keeps_rows(a.shape)])
    multiple = max(row_multiple, V7X_SUBLANES)
    rows = -(-(-(-flat.shape[0] // PACK_W)) // multiple) * multiple
    blocks.append(jnp.pad(flat, (0, rows * PACK_W - flat.shape[0])).reshape(rows, PACK_W))
    return jnp.concatenate(blocks, axis=0) if len(blocks) > 1 else blocks[0]


def _unpack_rows(pack, shapes):
    out, row = {}, 0
    for i, shape in enumerate(shapes):
        if _keeps_rows(shape):
            out[i] = pack[row:row + shape[0]]
            row += -(-shape[0] // V7X_SUBLANES) * V7X_SUBLANES
    flat, at = pack[row:].reshape(-1), 0
    for i, shape in enumerate(shapes):
        if not _keeps_rows(shape):
            size = 1
            for s in shape:
                size *= s
            out[i] = flat[at:at + size].reshape(shape)
            at += size
    return [out[i] for i in range(len(shapes))]


def _join_chip_axis(name, parts):
    axis = parts.ndim - 1 if name in COLUMN_SPLIT or name in SMALL_SPLIT else parts.ndim - 2
    moved = jnp.moveaxis(parts, 0, axis - 1)
    shape = list(moved.shape)
    shape[axis - 1:axis + 1] = [shape[axis - 1] * shape[axis]]
    return moved.reshape(shape)


def _split_chip_axis(name, whole, shard_shape):
    axis = len(shard_shape) - 1 if name in COLUMN_SPLIT or name in SMALL_SPLIT else len(shard_shape) - 2
    shape = list(whole.shape)
    shape[axis:axis + 1] = [N_CHIPS, shard_shape[axis]]
    return jnp.moveaxis(whole.reshape(shape), axis, 0)


def kernel(x, norm_mix, norm_ffn, attn_w_qkv, attn_b_qkv, attn_sinks, attn_w_o, attn_b_o, conv_w_pw1, conv_b_pw1, conv_w_dw, conv_b_dw, conv_ln_g, conv_ln_b, conv_w_pw2, conv_b_pw2, ffn_w1, ffn_w3, ffn_w2, norm_final, loss_target, m_norm_mix, m_norm_ffn, m_attn_w_qkv, m_attn_b_qkv, m_attn_sinks, m_attn_w_o, m_attn_b_o, m_conv_w_pw1, m_conv_b_pw1, m_conv_w_dw, m_conv_b_dw, m_conv_ln_g, m_conv_ln_b, m_conv_w_pw2, m_conv_b_pw2, m_ffn_w1, m_ffn_w3, m_ffn_w2, m_norm_final, v_norm_mix, v_norm_ffn, v_attn_w_qkv, v_attn_b_qkv, v_attn_sinks, v_attn_w_o, v_attn_b_o, v_conv_w_pw1, v_conv_b_pw1, v_conv_w_dw, v_conv_b_dw, v_conv_ln_g, v_conv_ln_b, v_conv_w_pw2, v_conv_b_pw2, v_ffn_w1, v_ffn_w3, v_ffn_w2, v_norm_final):
    w = dict(zip(WEIGHT_NAMES, (norm_mix, norm_ffn, attn_w_qkv, attn_b_qkv, attn_sinks, attn_w_o, attn_b_o,
                                conv_w_pw1, conv_b_pw1, conv_w_dw, conv_b_dw, conv_ln_g, conv_ln_b, conv_w_pw2,
                                conv_b_pw2, ffn_w1, ffn_w3, ffn_w2, norm_final)))
    m = dict(zip(WEIGHT_NAMES, (m_norm_mix, m_norm_ffn, m_attn_w_qkv, m_attn_b_qkv, m_attn_sinks, m_attn_w_o,
                                m_attn_b_o, m_conv_w_pw1, m_conv_b_pw1, m_conv_w_dw, m_conv_b_dw, m_conv_ln_g,
                                m_conv_ln_b, m_conv_w_pw2, m_conv_b_pw2, m_ffn_w1, m_ffn_w3, m_ffn_w2, m_norm_final)))
    v = dict(zip(WEIGHT_NAMES, (v_norm_mix, v_norm_ffn, v_attn_w_qkv, v_attn_b_qkv, v_attn_sinks, v_attn_w_o,
                                v_attn_b_o, v_conv_w_pw1, v_conv_b_pw1, v_conv_w_dw, v_conv_b_dw, v_conv_ln_g,
                                v_conv_ln_b, v_conv_w_pw2, v_conv_b_pw2, v_ffn_w1, v_ffn_w3, v_ffn_w2, v_norm_final)))
    T, D = x.shape[1], x.shape[2]
    c_idx = lax.axis_index("c").astype(jnp.int32).reshape(1)
    chip = (2 * lax.axis_index("x") + lax.axis_index("y")).astype(jnp.int32)

    def as_rows(n, a):
        a = jnp.swapaxes(a, -1, -2) if n in STORED_TRANSPOSED else a
        return a.reshape(-1, a.shape[-1])

    def from_rows(n, rows):
        shape = w[n].shape[:-2] + w[n].shape[:-3:-1] if n in STORED_TRANSPOSED else w[n].shape
        a = rows.reshape(shape)
        return jnp.swapaxes(a, -1, -2) if n in STORED_TRANSPOSED else a

    def cast(n, after=None):
        return _cast_into_slot(f"cast_{n}", lax.empty((N_CHIPS,) + as_rows(n, w[n]).shape, BF16), as_rows(n, w[n]),
                               chip.reshape(1), after)

    first, later = BIG[:2], BIG[2:]
    slabs = {n: cast(n) for n in first}
    attn_send, attn_recv, attn_travelling, attn_started = _gather_start(
        "gather_attn_start", [slabs[n] for n in first], slabs[first[0]])
    slabs.update({n: cast(n, attn_started) for n in later})
    layers = ffn_w1.shape[0]
    small_shapes = [w[n].shape for n in SMALL_SPLIT]
    vec_send, vec_recv, vec_mine, vec_landing = _small_reduce_start(
        "small_gather_start", _pack_rows([w[n] for n in SMALL_SPLIT], F32, 8))
    attn_landed = _gather_wait("gather_attn_wait", attn_send, attn_recv, attn_travelling, vec_mine)
    send_sems, recv_sems, travelling, gather_started = _gather_start("gather_start", [slabs[n] for n in later],
                                                                     attn_landed[0])
    qkv_parts, w_o_parts = _swap_fetched_with_sibling("gather_attn_swap", attn_landed, gather_started)

    def small_vectors(after):
        mine, landed = _small_reduce_wait("small_gather_wait", vec_send, vec_recv, vec_mine, vec_landing, after)
        landed = lax.dynamic_update_index_in_dim(landed, mine, 2 * chip + c_idx[0], axis=0)
        per_chip = [_unpack_rows(landed[2 * j], small_shapes) for j in range(N_CHIPS)]
        whole = {n: _join_chip_axis(n, jnp.stack([per_chip[j][i] for j in range(N_CHIPS)]))
                 for i, n in enumerate(SMALL_SPLIT)}
        return {**{n: whole[n] for n in SMALL_SPLIT if n != "conv_w_dw"}, "conv_w_dw": whole["conv_w_dw"][0]}

    def other_weights(after):
        landed = dict(zip(later, _gather_wait("gather_wait", send_sems, recv_sems, travelling, after)))
        now, then = ["ffn_w1", "ffn_w3"], ["conv_w_pw1", "conv_w_pw2", "ffn_w2"]
        ready = dict(zip(now, _swap_fetched_with_sibling("gather_swap", [landed[n] for n in now])))
        swap_send, swap_recv, swapping_rest, swap_started = _gather_start(
            "gather_swap_start", [landed[n] for n in then], ready[now[0]], stage="d2d")

        def rest_of_weights(after_next):
            rest = dict(zip(then, _gather_wait("gather_swap_wait", swap_send, swap_recv, swapping_rest, after_next,
                                               stage="d2d")))
            return {"conv_w_pw1": rest["conv_w_pw1"], "conv_w_pw2": rest["conv_w_pw2"].reshape(-1, D),
                    "ffn_w2": rest["ffn_w2"].reshape(N_CHIPS, layers, -1, D), **small_vectors(after_next)}

        return {"ffn_w1": ready["ffn_w1"].reshape(N_CHIPS, layers, -1, D),
                "ffn_w3": ready["ffn_w3"].reshape(N_CHIPS, layers, -1, D),
                "swap_started": swap_started, "rest_of_weights": rest_of_weights}

    p = {
        "norm_mix": norm_mix, "norm_ffn": norm_ffn, "norm_final": norm_final.reshape(1, D),
        "attn_w_qkv": qkv_parts.reshape(-1, D), "attn_b_qkv": attn_b_qkv,
        "attn_sinks": attn_sinks, "attn_w_o": w_o_parts.reshape(-1, D), "attn_b_o": attn_b_o,
        "gather_started": gather_started, "other_weights": other_weights,
    }
    swapping, in_flight = {}, []

    def reduce_begin(tag, grads):
        keys = list(grads)
        *handles, begun = _sibling_swap_start(f"sibling_swap_start_{tag}", [grads[k] for k in keys])
        swapping[tag] = (keys, handles)
        return begun

    def reduce_send(tag, after):
        keys, (swap_send, swap_recv, grads, lands) = swapping[tag]
        grads, from_sibling = _sibling_swap_wait(f"sibling_swap_wait_{tag}", swap_send, swap_recv, grads, lands, after)
        partials = [_add_sibling_half(f"add_sibling_half_{tag}{i}", gr, fs, c_idx)
                    for i, (gr, fs) in enumerate(zip(grads, from_sibling))]
        *handles, sent = _scatter_start(f"scatter_start_{tag}", partials)
        in_flight.append((tag, keys, handles, grads, from_sibling))
        return sent

    loss_part, dx, g = _local_step(x[0], loss_target[0], p, reduce_begin, reduce_send)
    for n in SMALL_WHOLE + SMALL_SPLIT:
        g[n] = g[n].reshape((-1,) + g[n].shape[-2:]) if w[n].ndim == 3 else g[n].reshape(w[n].shape[:-1] + (-1,))

    small_pack = _pack_rows([loss_part] + [g[n] for n in SMALL_WHOLE] + [g[n] for n in SMALL_SPLIT], F32, 8)
    small_send, small_recv, small_pack, small_land = _small_reduce_start("small_reduce_start", small_pack)

    place = jnp.stack([chip, c_idx[0]])
    shard_grad = {n: lax.empty(as_rows(n, w[n]).shape, F32) for n in BIG}
    for tag, keys, (send_sems, recv_sems, partials, lands), grads, from_sibling in in_flight:
        _, received = _scatter_wait(f"scatter_wait_{tag}", send_sems, recv_sems, partials, lands, small_pack)
        for i, (n, layer) in enumerate(keys):
            shard_grad[n] = _sum_chip_partials(f"sum_chip_partials_{tag}{i}", grads[i], from_sibling[i], received[i],
                                               shard_grad[n], layer, place)
    g_big = dict(zip(BIG, _join_halves([shard_grad[n] for n in BIG], [w[n].shape[0] for n in BIG])))
    big_out = {}
    for n in BIG:
        step = _adamw(f"adamw_{n}", as_rows(n, w[n]), g_big[n], as_rows(n, m[n]), as_rows(n, v[n]))
        big_out[n] = [from_rows(n, a) for a in step]

    small_whole_shapes = [w[n].shape for n in SMALL_WHOLE]
    small_full_shapes = [g[n].shape for n in SMALL_SPLIT]
    small_pack, small_land = _small_reduce_wait("small_reduce_wait", small_send, small_recv, small_pack, small_land,
                                                big_out[BIG[-1]][1])
    reduced = _sum_device_slots("small_reduce_sum", small_pack, small_land, (2 * chip + c_idx[0]).reshape(1))
    pieces = _unpack_rows(reduced, [(1,)] + small_whole_shapes + small_full_shapes)
    loss = pieces[0].reshape(())
    g_small = dict(zip(SMALL_WHOLE, pieces[1:1 + len(SMALL_WHOLE)]))
    for n, whole in zip(SMALL_SPLIT, pieces[1 + len(SMALL_WHOLE):]):
        parts = _split_chip_axis(n, whole, w[n].shape)
        g_small[n] = lax.dynamic_index_in_dim(parts, chip, axis=0, keepdims=False)
    small = SMALL_WHOLE + SMALL_SPLIT
    _, d_small, m_small, v_small = _adamw(
        "adamw_small", _pack_rows([w[n] for n in small], F32, 8), _pack_rows([g_small[n] for n in small], F32, 8),
        _pack_rows([m[n] for n in small], F32, 8), _pack_rows([v[n] for n in small], F32, 8))

    outs = {}
    for slot, (tag, small_pack) in enumerate((("g", None), ("d", d_small), ("m", m_small), ("v", v_small))):
        vals = {n: big_out[n][slot] for n in BIG}
        if small_pack is None:
            vals.update(g_small)
        else:
            vals.update(zip(small, _unpack_rows(small_pack, [w[n].shape for n in small])))
        outs[tag] = vals
    return (loss, dx.reshape(1, T, D), *[outs["g"][n] for n in WEIGHT_NAMES], *[outs["d"][n] for n in WEIGHT_NAMES],
            *[outs["m"][n] for n in WEIGHT_NAMES], *[outs["v"][n] for n in WEIGHT_NAMES])
```

```python
import functools

import jax
import jax.numpy as jnp
from jax import lax
from jax.experimental import pallas as pl
from jax.experimental.pallas import tpu as pltpu

F32 = jnp.float32
BF16 = jnp.bfloat16
SDS = jax.ShapeDtypeStruct
MESH = pl.DeviceIdType.MESH

HEAD_DIM = 64
N_Q_HEADS = 16
N_KV_HEADS = 2
Q_PER_KV = N_Q_HEADS // N_KV_HEADS
ATTN_BLOCK = 128
ROPE_THETA = 10000.0
CONV_WIDTH = 31
CONV_HALO = 32
CONV_FIRST_TAP = CONV_HALO - CONV_WIDTH + 1
CONV_ROW_CHUNK = 64
CONV_LANE_CHUNK = 256
CONV_GRAD_UNROLL = 8
RMS_EPS = 1e-5
LN_EPS = 1e-5
ADAM_LR = 0.001
ADAM_B1 = 0.9
ADAM_B2 = 0.999
ADAM_EPS = 1e-08
ADAM_WD = 0.01
ADAM_STEP = 10

V7X_LANES = 128
V7X_SUBLANES = 8
V7X_VMEM_LIMIT_BYTES = 56 * 1024 * 1024

N_CHIPS = 4
N_DEV = 8
PACK_W = 1024

MASK_VALUE = -1e30


def _params(*semantics):
    return pltpu.CompilerParams(dimension_semantics=semantics, vmem_limit_bytes=V7X_VMEM_LIMIT_BYTES)


def _rows(tm, width):
    return pl.BlockSpec((tm, width), lambda i: (i, 0))


def _whole(shape):
    return pl.BlockSpec(shape, lambda *_: (0,) * len(shape))


def _rms_rstd(h):
    return lax.rsqrt(jnp.mean(h * h, axis=-1, keepdims=True) + RMS_EPS)


def _silu_and_grad(z):
    sg = jax.nn.sigmoid(z)
    return z * sg, sg * (1.0 + z * (1.0 - sg))


def _swap_rope_halves(t):
    w = t.shape[1]
    half = HEAD_DIM // 2
    lane = lax.broadcasted_iota(jnp.int32, t.shape, 1)
    upper = pltpu.roll(t, w - half, 1)
    lower = pltpu.roll(t, half, 1)
    return jnp.where(lane % HEAD_DIM < half, upper, lower)


def _rope(t, cos_ref, sin_ref):
    reps = t.shape[1] // V7X_LANES
    c = jnp.tile(cos_ref[...], (1, reps))
    s = jnp.tile(sin_ref[...], (1, reps))
    return t * c + _swap_rope_halves(t) * s


def _rope_transposed(dt, cos_ref, sin_ref):
    reps = dt.shape[1] // V7X_LANES
    c = jnp.tile(cos_ref[...], (1, reps))
    s = jnp.tile(sin_ref[...], (1, reps))
    return dt * c + _swap_rope_halves(dt * s)


def _rope_tables(seq_len):
    pos = jnp.arange(seq_len, dtype=F32)
    inv_freq = ROPE_THETA ** (-jnp.arange(0, HEAD_DIM, 2, dtype=F32) / HEAD_DIM)
    ang = pos[:, None] * jnp.tile(inv_freq, 2 * V7X_LANES // HEAD_DIM)[None, :]
    upper_half = jnp.arange(V7X_LANES) % HEAD_DIM >= HEAD_DIM // 2
    return jnp.cos(ang), jnp.where(upper_half[None, :], jnp.sin(ang), -jnp.sin(ang))


def _qkv_proj(h, g, w, b, cos, sin, after):
    T, D = h.shape
    N = w.shape[0]
    tm = min(1024, T)
    rope_w = N - N_KV_HEADS * HEAD_DIM

    def body(h_ref, g_ref, w_ref, b_ref, cos_ref, sin_ref, _, y_ref, o_ref):
        hh = h_ref[...]
        y = (hh * _rms_rstd(hh) * g_ref[...]).astype(BF16)
        y_ref[...] = y
        acc = _dot_nt(y, w_ref[...]) + b_ref[...]
        o_ref[:, :rope_w] = _rope(acc[:, :rope_w], cos_ref, sin_ref).astype(BF16)
        o_ref[:, rope_w:] = acc[:, rope_w:].astype(BF16)

    return pl.pallas_call(
        body, name="qkv_proj", grid=(T // tm,),
        in_specs=[_rows(tm, D), _whole((1, D)), _whole((N, D)), _whole((1, N)),
                  _rows(tm, V7X_LANES), _rows(tm, V7X_LANES), pl.BlockSpec(memory_space=pl.ANY)],
        out_specs=[_rows(tm, D), _rows(tm, N)],
        out_shape=[SDS((T, D), BF16), SDS((T, N), BF16)],
        compiler_params=_params("parallel"),
    )(h, g, w, b, cos, sin, after)


def _pw1_proj(h, g, w, b):
    T, D = h.shape
    n = w.shape[2]
    N = N_CHIPS * n
    tm = min(1024, T)

    def body(h_ref, g_ref, w_ref, b_ref, y_ref, o_ref):
        hh = h_ref[...]
        y = (hh * _rms_rstd(hh) * g_ref[...]).astype(BF16)
        y_ref[...] = y
        for j in range(N_CHIPS):
            cols = slice(j * n, (j + 1) * n)
            o_ref[:, cols] = (jnp.dot(y, w_ref[j], preferred_element_type=F32) + b_ref[:, cols]).astype(BF16)

    return pl.pallas_call(
        body, name="pw1_proj", grid=(T // tm,),
        in_specs=[_rows(tm, D), _whole((1, D)), _whole((N_CHIPS, D, n)), _whole((1, N))],
        out_specs=[_rows(tm, D), _rows(tm, N)],
        out_shape=[SDS((T, D), BF16), SDS((T, N), BF16)],
        compiler_params=_params("parallel"),
    )(h, g, w, b)


PAIRS_PER_KV = Q_PER_KV // 2


def _upper_lanes(shape):
    return lax.broadcasted_iota(jnp.int32, shape, 1) >= HEAD_DIM


def _swap_lane_halves(t):
    return pltpu.roll(t.astype(F32), HEAD_DIM, 1).astype(t.dtype)


def _kv_operands(g, t):
    swapped = _swap_lane_halves(t)
    in_lower, in_upper = (t, swapped) if g == 0 else (swapped, t)
    upper = _upper_lanes(t.shape)
    zero = jnp.zeros_like(t)
    return jnp.where(upper, zero, in_lower), jnp.where(upper, in_upper, zero)


def _group_heads(g):
    pairs = range(g * PAIRS_PER_KV, (g + 1) * PAIRS_PER_KV)
    return [2 * hp for hp in pairs] + [2 * hp + 1 for hp in pairs]


def _all_heads():
    return [h for g in range(N_KV_HEADS) for h in _group_heads(g)]


def _pair_rows(ref, g):
    pairs = range(g * PAIRS_PER_KV, (g + 1) * PAIRS_PER_KV)
    return jnp.concatenate([ref[:, hp * 2 * HEAD_DIM:(hp + 1) * 2 * HEAD_DIM] for hp in pairs], axis=0)


def _from_previous_block(rows):
    row = lax.broadcasted_iota(jnp.int32, (ATTN_BLOCK, ATTN_BLOCK), 0)
    col = lax.broadcasted_iota(jnp.int32, (ATTN_BLOCK, ATTN_BLOCK), 1)
    return jnp.concatenate([col > row] * (rows // ATTN_BLOCK), axis=0)


def _split_folded(t, prev_part):
    tb = t.astype(BF16)
    zero = jnp.zeros_like(tb)
    return jnp.where(prev_part, tb, zero), jnp.where(prev_part, zero, tb)


def _attn_specs(T):
    nb = T // ATTN_BLOCK
    kcol = N_Q_HEADS * HEAD_DIM // V7X_LANES
    cur = lambda n: jnp.minimum(n, nb - 1)
    prev = lambda n: jnp.maximum(jnp.minimum(n, nb - 1) - 1, 0)
    q_spec = pl.BlockSpec((ATTN_BLOCK, N_Q_HEADS * HEAD_DIM), lambda n: (cur(n), 0))
    kc_spec = pl.BlockSpec((ATTN_BLOCK, V7X_LANES), lambda n: (cur(n), kcol))
    kp_spec = pl.BlockSpec((ATTN_BLOCK, V7X_LANES), lambda n: (prev(n), kcol))
    vc_spec = pl.BlockSpec((ATTN_BLOCK, V7X_LANES), lambda n: (cur(n), kcol + 1))
    vp_spec = pl.BlockSpec((ATTN_BLOCK, V7X_LANES), lambda n: (prev(n), kcol + 1))
    return q_spec, kc_spec, kp_spec, vc_spec, vp_spec


def _attn_fwd(qkv, sinks):
    T = qkv.shape[0]
    nb = T // ATTN_BLOCK
    qw = N_Q_HEADS * HEAD_DIM
    all_rows = N_Q_HEADS * ATTN_BLOCK

    per_step = 4
    kcol = qw // V7X_LANES

    def body(q_ref, *refs):
        k_refs, v_refs = refs[:per_step + 1], refs[per_step + 1:2 * per_step + 2]
        sink_ref, o_ref, probs_ref, psink_ref = refs[2 * per_step + 2:]
        m = pl.program_id(0)
        half = PAIRS_PER_KV * ATTN_BLOCK
        groups = range(N_KV_HEADS)
        k_ops = [[_kv_operands(g, r[...]) for g in groups] for r in k_refs]
        v_ops = [[_kv_operands(g, r[...]) for g in groups] for r in v_refs]

        def scores(q, k):
            return lax.dot_general(q, k, (((1,), (1,)), ((), ())), preferred_element_type=F32)

        s_prev, s_cur = [], []
        for b in range(per_step):
            rows = slice(b * ATTN_BLOCK, (b + 1) * ATTN_BLOCK)
            for g in groups:
                pairs = range(g * PAIRS_PER_KV, (g + 1) * PAIRS_PER_KV)
                q = jnp.concatenate([q_ref[rows, hp * 2 * HEAD_DIM:(hp + 1) * 2 * HEAD_DIM] for hp in pairs], axis=0)
                for i in range(2):
                    from_prev = scores(q, k_ops[b][g][i])
                    if b == 0:
                        from_prev = jnp.where(m > 0, from_prev, MASK_VALUE * (HEAD_DIM ** 0.5))
                    s_prev.append(from_prev)
                    s_cur.append(scores(q, k_ops[b + 1][g][i]))
        prev_part = _from_previous_block(per_step * all_rows)
        s = jnp.where(prev_part, jnp.concatenate(s_prev, axis=0), jnp.concatenate(s_cur, axis=0)) * (HEAD_DIM ** -0.5)
        sink = jnp.concatenate([jnp.broadcast_to(sink_ref[0:1, h:h + 1], (ATTN_BLOCK, 1))
                                for h in _all_heads() * per_step], axis=0)
        top = jnp.maximum(jnp.max(s, axis=1, keepdims=True), sink)
        p = jnp.exp(s - top)
        e_sink = jnp.exp(sink - top)
        inv = 1.0 / (jnp.sum(p, axis=1, keepdims=True) + e_sink)
        probs, p_sink = p * inv, e_sink * inv
        p_prev, p_cur = _split_folded(probs, prev_part)
        lane = lax.broadcasted_iota(jnp.int32, (ATTN_BLOCK, V7X_LANES), 1)
        for b in range(per_step):
            rows = slice(b * ATTN_BLOCK, (b + 1) * ATTN_BLOCK)
            base = b * all_rows
            probs_ref[b] = probs[base:base + all_rows].astype(BF16)
            sink_tile = jnp.zeros((ATTN_BLOCK, V7X_LANES), F32)
            for i, h in enumerate(_all_heads()):
                sink_tile = jnp.where(lane == h, p_sink[base + i * ATTN_BLOCK:base + (i + 1) * ATTN_BLOCK], sink_tile)
            psink_ref[rows, :] = sink_tile
            for g in groups:
                even = slice(base + 2 * g * half, base + (2 * g + 1) * half)
                odd = slice(base + (2 * g + 1) * half, base + (2 * g + 2) * half)
                o = (jnp.dot(p_prev[even], v_ops[b][g][0], preferred_element_type=F32)
                     + jnp.dot(p_cur[even], v_ops[b + 1][g][0], preferred_element_type=F32)
                     + jnp.dot(p_prev[odd], v_ops[b][g][1], preferred_element_type=F32)
                     + jnp.dot(p_cur[odd], v_ops[b + 1][g][1], preferred_element_type=F32))
                for i in range(PAIRS_PER_KV):
                    hp = g * PAIRS_PER_KV + i
                    o_ref[rows, hp * 2 * HEAD_DIM:(hp + 1) * 2 * HEAD_DIM] = (
                        o[i * ATTN_BLOCK:(i + 1) * ATTN_BLOCK].astype(BF16))

    def kv_specs(col):
        return [pl.BlockSpec((ATTN_BLOCK, V7X_LANES),
                             functools.partial(lambda m, b: (jnp.maximum(per_step * m + b, 0), col), b=b))
                for b in range(-1, per_step)]

    return pl.pallas_call(
        body, name="attn_fwd", grid=(nb // per_step,),
        in_specs=[_rows(per_step * ATTN_BLOCK, qw), *kv_specs(kcol), *kv_specs(kcol + 1), _whole((1, N_Q_HEADS))],
        out_specs=[_rows(per_step * ATTN_BLOCK, qw),
                   pl.BlockSpec((per_step, all_rows, ATTN_BLOCK), lambda m: (m, 0, 0)),
                   _rows(per_step * ATTN_BLOCK, V7X_LANES)],
        out_shape=[SDS((T, qw), BF16), SDS((nb, all_rows, ATTN_BLOCK), BF16), SDS((T, V7X_LANES), F32)],
        compiler_params=_params("parallel"),
    )(*[qkv] * (2 * per_step + 3), sinks)


def _mm_res(name, a, w, b, res, g):
    T, K = a.shape
    D = w.shape[1]
    tm = min(1024, T)

    def body(a_ref, w_ref, b_ref, r_ref, g_ref, o_ref, f_ref):
        h = jnp.dot(a_ref[...], w_ref[...], preferred_element_type=F32) + b_ref[...] + r_ref[...]
        o_ref[...] = h
        f_ref[...] = (h * _rms_rstd(h) * g_ref[...]).astype(BF16)

    return pl.pallas_call(
        body, name=name, grid=(T // tm,),
        in_specs=[_rows(tm, K), _whole((K, D)), _whole((1, D)), _rows(tm, D), _whole((1, D))],
        out_specs=[_rows(tm, D), _rows(tm, D)],
        out_shape=[SDS((T, D), F32), SDS((T, D), BF16)],
        compiler_params=_params("parallel"),
    )(a, w, b, res, g)


def _ffn_down(name, s, w2, layer, res):
    _, T, n = s.shape
    D = w2.shape[3]
    tm = min(512, T)

    def body(s_ref, w_ref, r_ref, o_ref):
        acc = r_ref[...]
        for j in range(N_CHIPS):
            acc = acc + jnp.dot(s_ref[j], w_ref[j], preferred_element_type=F32)
        o_ref[...] = acc

    return pl.pallas_call(
        body, name=name, grid=(T // tm,),
        in_specs=[pl.BlockSpec((N_CHIPS, tm, n), lambda i: (0, i, 0)),
                  pl.BlockSpec((N_CHIPS, None, n, D), lambda i: (0, layer, 0, 0)), _rows(tm, D)],
        out_specs=_rows(tm, D),
        out_shape=SDS((T, D), F32),
        compiler_params=_params("parallel"),
    )(s, w2, res)


def _ffn_up(name, f, w1, w3, layer, after=None):
    T, D = f.shape
    n = w1.shape[2]
    tm = min(1024, T)

    def body(f_ref, w1_ref, w3_ref, *rest):
        act_ref, gg_ref, s_ref = rest[-3:]
        ff = f_ref[...]
        g1 = _dot_nt(ff, w1_ref[...])
        g3 = _dot_nt(ff, w3_ref[...])
        act, dact = _silu_and_grad(g1)
        act_ref[...] = act.astype(BF16)
        gg_ref[...] = (g3 * dact).astype(BF16)
        s_ref[...] = (act * g3).astype(BF16)

    slab = pl.BlockSpec((None, tm, n), lambda j, i: (j, i, 0))
    wslab = pl.BlockSpec((None, None, n, D), lambda j, i: (j, layer, 0, 0))
    hidden = SDS((N_CHIPS, T, n), BF16)
    return pl.pallas_call(
        body, name=name, grid=(N_CHIPS, T // tm),
        in_specs=[pl.BlockSpec((tm, D), lambda j, i: (i, 0)), wslab, wslab]
        + ([] if after is None else [pl.BlockSpec(memory_space=pl.ANY)]),
        out_specs=[slab, slab, slab],
        out_shape=[hidden, hidden, hidden],
        compiler_params=_params("parallel", "parallel"),
    )(f, w1, w3, *([] if after is None else [after]))


def _glu(a, d):
    a = a.astype(F32)
    return a[:, :d] * jax.nn.sigmoid(a[:, d:])


def _conv_tile(T):
    return min(256, T)


def _fill_shifted(sh_ref, tc):
    n = tc + CONV_HALO - V7X_SUBLANES
    for r in range(1, V7X_SUBLANES):
        sh_ref[r, 0:n, :] = sh_ref[0, pl.ds(r, n), :]


def _depthwise_taps(sh_ref, w_ref, offsets, bias_ref, out_ref, tc):
    D = out_ref.shape[1]

    def chunk(i, carry):
        t0 = pl.multiple_of(i * CONV_ROW_CHUNK, CONV_ROW_CHUNK)
        for cb in range(D // CONV_LANE_CHUNK):
            cs = slice(cb * CONV_LANE_CHUNK, (cb + 1) * CONV_LANE_CHUNK)
            acc = jnp.zeros((CONV_ROW_CHUNK, CONV_LANE_CHUNK), F32)
            for r in range(V7X_SUBLANES):
                taps = [(j, o // V7X_SUBLANES) for j, o in enumerate(offsets) if o % V7X_SUBLANES == r]
                if not taps:
                    continue
                span = CONV_ROW_CHUNK + V7X_SUBLANES * max(q for _, q in taps)
                rows = sh_ref[r, pl.ds(t0, span), cs]
                for j, q in taps:
                    acc = acc + rows[V7X_SUBLANES * q:V7X_SUBLANES * q + CONV_ROW_CHUNK] * w_ref[j:j + 1, cs]
            if bias_ref is not None:
                acc = acc + bias_ref[:, cs]
            out_ref[pl.ds(t0, CONV_ROW_CHUNK), cs] = acc
        return carry

    lax.fori_loop(0, tc // CONV_ROW_CHUNK, chunk, 0)


def _depthwise_tap_grads(dy_sh, x_sh, offsets, dw_ref, tc):
    D = dw_ref.shape[1]
    for cb in range(D // V7X_LANES):
        cs = slice(cb * V7X_LANES, (cb + 1) * V7X_LANES)

        def row_tiles(i, accs, cs=cs):
            for k in range(CONV_GRAD_UNROLL):
                t0 = pl.multiple_of(i * (CONV_GRAD_UNROLL * V7X_SUBLANES), V7X_SUBLANES) + k * V7X_SUBLANES
                d = dy_sh[0, pl.ds(t0, V7X_SUBLANES), cs]
                accs = tuple(
                    acc + d * x_sh[o % V7X_SUBLANES, pl.ds(t0 + o // V7X_SUBLANES * V7X_SUBLANES, V7X_SUBLANES), cs]
                    for acc, o in zip(accs, offsets))
            return accs

        zero = jnp.zeros((V7X_SUBLANES, V7X_LANES), F32)
        accs = lax.fori_loop(0, tc // (CONV_GRAD_UNROLL * V7X_SUBLANES), row_tiles, tuple(zero for _ in offsets))
        for j, acc in enumerate(accs):
            dw_ref[j:j + 1, cs] += jnp.sum(acc, axis=0, keepdims=True)


def _conv_fwd(a, w_dw, b_dw, ln_g, ln_b, w_pw2, b_pw2, res, g_next):
    T = a.shape[0]
    D = a.shape[1] // 2
    tc = _conv_tile(T)
    per = tc // CONV_HALO

    def body(a_ref, ah_ref, w_ref, bdw_ref, lg_ref, lb_ref, wp_ref, bp_ref, r_ref, g_ref,
             c_ref, act_ref, h_ref, f_ref, u_sh):
        i = pl.program_id(0)
        u_sh[0, 0:CONV_HALO, :] = jnp.where(i > 0, _glu(ah_ref[...], D), 0.0)
        u_sh[0, CONV_HALO:, :] = _glu(a_ref[...], D)
        _fill_shifted(u_sh, tc)
        _depthwise_taps(u_sh, w_ref, [CONV_FIRST_TAP + j for j in range(CONV_WIDTH)], bdw_ref, c_ref, tc)
        c = c_ref[...]
        xc = c - jnp.mean(c, axis=-1, keepdims=True)
        z = xc * lax.rsqrt(jnp.mean(xc * xc, axis=-1, keepdims=True) + LN_EPS)
        l = z * lg_ref[...] + lb_ref[...]
        act = (l * jax.nn.sigmoid(l)).astype(BF16)
        act_ref[...] = act
        h = jnp.dot(act, wp_ref[...], preferred_element_type=F32) + bp_ref[...] + r_ref[...]
        h_ref[...] = h
        f_ref[...] = (h * _rms_rstd(h) * g_ref[...]).astype(BF16)

    return pl.pallas_call(
        body, name="conv_fwd", grid=(T // tc,),
        in_specs=[_rows(tc, 2 * D),
                  pl.BlockSpec((CONV_HALO, 2 * D), lambda i: (jnp.maximum(i * per - 1, 0), 0)),
                  _whole((CONV_WIDTH, D)), _whole((1, D)), _whole((1, D)), _whole((1, D)),
                  _whole((D, D)), _whole((1, D)), _rows(tc, D), _whole((1, D))],
        out_specs=[_rows(tc, D), _rows(tc, D), _rows(tc, D), _rows(tc, D)],
        out_shape=[SDS((T, D), F32), SDS((T, D), BF16), SDS((T, D), F32), SDS((T, D), BF16)],
        scratch_shapes=[pltpu.VMEM((V7X_SUBLANES, tc + CONV_HALO, D), F32)],
        compiler_params=_params("parallel"),
    )(a, a, w_dw, b_dw, ln_g, ln_b, w_pw2, b_pw2, res, g_next)


def _ffn_down_loss(name, s, w2, layer, res, g, target):
    _, T, n = s.shape
    D = w2.shape[3]
    tm = min(512, T)

    def body(s_ref, w_ref, r_ref, g_ref, t_ref, dh_ref, loss_ref, dg_ref):
        @pl.when(pl.program_id(0) == 0)
        def _():
            loss_ref[...] = jnp.zeros_like(loss_ref)
            dg_ref[...] = jnp.zeros_like(dg_ref)

        hh = r_ref[...]
        for j in range(N_CHIPS):
            hh = hh + jnp.dot(s_ref[j], w_ref[j], preferred_element_type=F32)
        r = _rms_rstd(hh)
        g = g_ref[...]
        d = hh * r * g - t_ref[...]
        loss_ref[...] += 0.5 * jnp.sum(jnp.mean(d * d, axis=-1, keepdims=True), axis=0, keepdims=True)
        dout = d * (1.0 / D)
        dg_ref[...] += jnp.sum(dout * (hh * r), axis=0, keepdims=True)
        dxh = dout * g
        dh_ref[...] = r * dxh - hh * (r * r * r) * jnp.mean(dxh * hh, axis=-1, keepdims=True)

    return pl.pallas_call(
        body, name=name, grid=(T // tm,),
        in_specs=[pl.BlockSpec((N_CHIPS, tm, n), lambda i: (0, i, 0)),
                  pl.BlockSpec((N_CHIPS, None, n, D), lambda i: (0, layer, 0, 0)), _rows(tm, D),
                  _whole((1, D)), _rows(tm, D)],
        out_specs=[_rows(tm, D), _whole((1, 1)), _whole((1, D))],
        out_shape=[SDS((T, D), F32), SDS((1, 1), F32), SDS((1, D), F32)],
        compiler_params=_params("arbitrary"),
    )(s, w2, res, g, target)


def _ffn_bwd_down(name, dh, w2, layer, act, gate_grad, s, after=None):
    T, D = dh.shape
    n = w2.shape[2]
    tm = min(256, T)

    def body(dh_ref, w2_ref, act_ref, gg_ref, s_ref, *rest):
        dg1_ref, dg3_ref, dw_ref = rest[-3:]

        @pl.when(pl.program_id(0) == 0)
        def _():
            dw_ref[...] = jnp.zeros_like(dw_ref)

        dhb = dh_ref[...].astype(BF16)
        for j in range(N_CHIPS):
            ds = _dot_nt(dhb, w2_ref[j])
            dg1_ref[j] = (ds * gg_ref[j].astype(F32)).astype(BF16)
            dg3_ref[j] = (ds * act_ref[j].astype(F32)).astype(BF16)
            dw_ref[j] += _dot_tn(s_ref[j], dhb)

    slabs = pl.BlockSpec((N_CHIPS, tm, n), lambda i: (0, i, 0))
    hidden = SDS((N_CHIPS, T, n), BF16)
    return pl.pallas_call(
        body, name=name, grid=(T // tm,),
        in_specs=[_rows(tm, D),
                  pl.BlockSpec((N_CHIPS, None, n, D), lambda i: (0, layer, 0, 0), pipeline_mode=pl.Buffered(1)),
                  slabs, slabs, slabs] + ([] if after is None else [pl.BlockSpec(memory_space=pl.ANY)]),
        out_specs=[slabs, slabs, _whole((N_CHIPS, n, D))],
        out_shape=[hidden, hidden, SDS((N_CHIPS, n, D), F32)],
        compiler_params=_params("arbitrary"),
    )(dh, w2, act, gate_grad, s, *([] if after is None else [after]))


def _dot_tn(a, b):
    return lax.dot_general(a.astype(BF16), b.astype(BF16), (((0,), (0,)), ((), ())), preferred_element_type=F32)


def _dot_nt(a, b):
    return lax.dot_general(a.astype(BF16), b, (((1,), (1,)), ((), ())), preferred_element_type=F32)


def _mm_tn(name, a, b, col_chunks=1, after=None):
    a_slabs, b_slabs = a.ndim == 3, b.ndim == 3
    T = a.shape[-2]
    tt = min(1024, T)
    ka, nb = a.shape[-1], b.shape[-1]
    if a_slabs or b_slabs:
        out_dims = (N_CHIPS, ka, nb)
    elif col_chunks > 1:
        out_dims = (col_chunks, ka, nb // col_chunks)
    else:
        out_dims = (ka, nb)

    def body(a_ref, b_ref, *rest):
        o_ref = rest[-1]

        @pl.when(pl.program_id(0) == 0)
        def _():
            o_ref[...] = jnp.zeros_like(o_ref)

        if a_slabs:
            bb = b_ref[...].astype(BF16)
            for j in range(N_CHIPS):
                o_ref[j] += _dot_tn(a_ref[j], bb)
        elif b_slabs:
            aa = a_ref[...].astype(BF16)
            for j in range(N_CHIPS):
                o_ref[j] += _dot_tn(aa, b_ref[j])
        elif col_chunks > 1:
            aa = a_ref[...].astype(BF16)
            w = nb // col_chunks
            for j in range(col_chunks):
                o_ref[j] += _dot_tn(aa, b_ref[:, j * w:(j + 1) * w])
        else:
            o_ref[...] += _dot_tn(a_ref[...], b_ref[...])

    def spec(arr, slabs):
        if slabs:
            return pl.BlockSpec((N_CHIPS, tt, arr.shape[-1]), lambda t: (0, t, 0))
        return _rows(tt, arr.shape[-1])

    return pl.pallas_call(
        body, name=name, grid=(T // tt,),
        in_specs=[spec(a, a_slabs), spec(b, b_slabs)] + ([] if after is None else [pl.BlockSpec(memory_space=pl.ANY)]),
        out_specs=_whole(out_dims),
        out_shape=SDS(out_dims, F32),
        compiler_params=_params("arbitrary"),
    )(a, b, *([] if after is None else [after]))


def _mm_nt_normbwd(name, pairs, h, g, dh, after):
    T, D = h.shape
    tm = min(512, T)
    n_pairs = len(pairs)
    kinds = ["slabs" if dy.ndim == 3 else ("quarters" if w.ndim == 3 else "plain") for dy, w, _ in pairs]

    def body(*refs):
        dy_refs = refs[:n_pairs]
        w_refs = refs[n_pairs:2 * n_pairs]
        h_ref, g_ref, dh_ref, _, o_ref, dg_ref, cs_ref = refs[2 * n_pairs:]

        @pl.when(pl.program_id(0) == 0)
        def _():
            dg_ref[...] = jnp.zeros_like(dg_ref)
            cs_ref[...] = jnp.zeros_like(cs_ref)

        df = jnp.zeros((tm, D), F32)
        for dy_ref, w_ref, kd in zip(dy_refs, w_refs, kinds):
            if kd == "slabs":
                for j in range(N_CHIPS):
                    df = df + jnp.dot(dy_ref[j], w_ref[j], preferred_element_type=F32)
            elif kd == "quarters":
                n = w_ref.shape[2]
                for j in range(N_CHIPS):
                    df = df + _dot_nt(dy_ref[:, j * n:(j + 1) * n], w_ref[j])
            else:
                df = df + jnp.dot(dy_ref[...], w_ref[...], preferred_element_type=F32)
        hh = h_ref[...]
        r = _rms_rstd(hh)
        dg_ref[...] += jnp.sum(df * (hh * r), axis=0, keepdims=True)
        dxh = df * g_ref[...]
        out = dh_ref[...] + (r * dxh - hh * (r * r * r) * jnp.mean(dxh * hh, axis=-1, keepdims=True))
        o_ref[...] = out
        cs_ref[...] += jnp.sum(out, axis=0, keepdims=True)

    dy_specs, w_specs = [], []
    for (dy, w, layer), kd in zip(pairs, kinds):
        if kd == "slabs":
            dy_specs.append(pl.BlockSpec((N_CHIPS, tm, dy.shape[2]), lambda i: (0, i, 0)))
            w_specs.append(pl.BlockSpec((N_CHIPS, None, w.shape[2], D),
                                        functools.partial(lambda i, layer: (0, layer, 0, 0), layer=layer),
                                        pipeline_mode=pl.Buffered(1)))
        else:
            dy_specs.append(_rows(tm, dy.shape[1]))
            w_specs.append(_whole(w.shape))

    return pl.pallas_call(
        body, name=name, grid=(T // tm,),
        in_specs=[*dy_specs, *w_specs, _rows(tm, D), _whole((1, D)), _rows(tm, D), pl.BlockSpec(memory_space=pl.ANY)],
        out_specs=[_rows(tm, D), _whole((1, D)), _whole((1, D))],
        out_shape=[SDS((T, D), F32), SDS((1, D), F32), SDS((1, D), F32)],
        compiler_params=_params("arbitrary"),
    )(*[dy for dy, _, _ in pairs], *[w for _, w, _ in pairs], h, g, dh, after)


def _mm_nt(name, dy, w, out_dtype):
    T, N = dy.shape
    K = w.shape[0]
    tm = min(1024, T)

    def body(dy_ref, w_ref, o_ref):
        o_ref[...] = lax.dot_general(dy_ref[...].astype(BF16), w_ref[...], (((1,), (1,)), ((), ())),
                                     preferred_element_type=F32).astype(out_dtype)

    return pl.pallas_call(
        body, name=name, grid=(T // tm,),
        in_specs=[_rows(tm, N), _whole((K, N))],
        out_specs=_rows(tm, K),
        out_shape=SDS((T, K), out_dtype),
        compiler_params=_params("parallel"),
    )(dy, w)


def _conv_bwd(dh, w_pw2, c, a, w_dw, ln_g, ln_b):
    T, D = c.shape
    tc = _conv_tile(T)
    per = tc // CONV_HALO
    n_tiles = T // tc
    last_halo = T // CONV_HALO - 1

    def ln_bwd(dact_v, c_v, lg, lb):
        xc = c_v - jnp.mean(c_v, axis=-1, keepdims=True)
        rstd = lax.rsqrt(jnp.mean(xc * xc, axis=-1, keepdims=True) + LN_EPS)
        z = xc * rstd
        _, dsilu = _silu_and_grad(z * lg + lb)
        dl = dact_v * dsilu
        dz = dl * lg
        dc = rstd * (dz - jnp.mean(dz, axis=-1, keepdims=True) - z * jnp.mean(dz * z, axis=-1, keepdims=True))
        return dc, dl, z

    def body(dh_ref, dhn_ref, wp_ref, c_ref, cn_ref, a_ref, ah_ref, w_ref, lg_ref, lb_ref,
             da_ref, dlg_ref, dlb_ref, dbdw_ref, dwdw_ref, dbpw1_ref, dc_sh, u_sh, du_scr):
        i = pl.program_id(0)

        @pl.when(i == 0)
        def _():
            for ref in (dlg_ref, dlb_ref, dbdw_ref, dwdw_ref, dbpw1_ref):
                ref[...] = jnp.zeros_like(ref)

        lg, lb = lg_ref[...], lb_ref[...]
        dh_rows = jnp.concatenate([dh_ref[...].astype(BF16), dhn_ref[...].astype(BF16)], axis=0)
        dact = _dot_nt(dh_rows, wp_ref[...])
        dc, dl, z = ln_bwd(dact[:tc], c_ref[...], lg, lb)
        dlg_ref[...] += jnp.sum(dl * z, axis=0, keepdims=True)
        dlb_ref[...] += jnp.sum(dl, axis=0, keepdims=True)
        dbdw_ref[...] += jnp.sum(dc, axis=0, keepdims=True)
        dcn, _, _ = ln_bwd(dact[tc:], cn_ref[...], lg, lb)
        dc_sh[0, 0:tc, :] = dc
        dc_sh[0, tc:, :] = jnp.where(i < n_tiles - 1, dcn, 0.0)
        _fill_shifted(dc_sh, tc)

        a_v = a_ref[...].astype(F32)
        a1 = a_v[:, :D]
        sg = jax.nn.sigmoid(a_v[:, D:])
        u_sh[0, 0:CONV_HALO, :] = jnp.where(i > 0, _glu(ah_ref[...], D), 0.0)
        u_sh[0, CONV_HALO:, :] = a1 * sg
        _fill_shifted(u_sh, tc)

        _depthwise_taps(dc_sh, w_ref, [CONV_WIDTH - 1 - j for j in range(CONV_WIDTH)], None, du_scr, tc)
        _depthwise_tap_grads(dc_sh, u_sh, [CONV_FIRST_TAP + j for j in range(CONV_WIDTH)], dwdw_ref, tc)

        du = du_scr[...]
        da1 = du * sg
        da2 = du * a1 * sg * (1.0 - sg)
        da_ref[:, :D] = da1.astype(BF16)
        da_ref[:, D:] = da2.astype(BF16)
        dbpw1_ref[:, :D] += jnp.sum(da1, axis=0, keepdims=True)
        dbpw1_ref[:, D:] += jnp.sum(da2, axis=0, keepdims=True)

    nxt = lambda i: (jnp.minimum((i + 1) * per, last_halo), 0)
    return pl.pallas_call(
        body, name="conv_bwd", grid=(n_tiles,),
        in_specs=[_rows(tc, D), pl.BlockSpec((CONV_HALO, D), nxt), _whole((D, D)),
                  _rows(tc, D), pl.BlockSpec((CONV_HALO, D), nxt),
                  _rows(tc, 2 * D),
                  pl.BlockSpec((CONV_HALO, 2 * D), lambda i: (jnp.maximum(i * per - 1, 0), 0)),
                  _whole((CONV_WIDTH, D)), _whole((1, D)), _whole((1, D))],
        out_specs=[_rows(tc, 2 * D), _whole((1, D)), _whole((1, D)), _whole((1, D)),
                   _whole((CONV_HALO, D)), _whole((1, 2 * D))],
        out_shape=[SDS((T, 2 * D), BF16), SDS((1, D), F32), SDS((1, D), F32), SDS((1, D), F32),
                   SDS((CONV_HALO, D), F32), SDS((1, 2 * D), F32)],
        scratch_shapes=[pltpu.VMEM((V7X_SUBLANES, tc + CONV_HALO, D), F32),
                        pltpu.VMEM((V7X_SUBLANES, tc + CONV_HALO, D), F32), pltpu.VMEM((tc, D), F32)],
        compiler_params=_params("arbitrary"),
    )(dh, dh, w_pw2, c, c, a, a, w_dw, ln_g, ln_b)


def _attn_bwd(qkv, dao, cos, sin, probs_saved, psink_saved):
    T = qkv.shape[0]
    nb = T // ATTN_BLOCK
    qw = N_Q_HEADS * HEAD_DIM
    kw = N_KV_HEADS * HEAD_DIM

    def body(q_ref, kc_ref, kp_ref, vc_ref, vp_ref, do_ref, cos_ref, sin_ref, cosp_ref, sinp_ref, probs_ref, psink_ref,
             dq_ref, dkv_ref, dsink_ref, dbq_ref, dbkv_ref, carry, prev_scr, cur_scr, dq_scr):
        n = pl.program_id(0)

        @pl.when(n == 0)
        def _():
            for ref in (dsink_ref, dbq_ref, dbkv_ref, carry):
                ref[...] = jnp.zeros_like(ref)

        @pl.when(n == nb)
        def _():
            prev_scr[...] = jnp.zeros_like(prev_scr)

        @pl.when(n < nb)
        def _():
            prev_part = _from_previous_block(N_Q_HEADS * ATTN_BLOCK)
            half = PAIRS_PER_KV * ATTN_BLOCK
            upper = _upper_lanes((ATTN_BLOCK, 2 * HEAD_DIM))
            groups = range(N_KV_HEADS)

            def nt(a, b):
                return lax.dot_general(a, b, (((1,), (1,)), ((), ())), preferred_element_type=F32)

            def kv_grad(even_rows, odd_rows, x):
                even = lax.dot_general(even_rows, x, (((0,), (0,)), ((), ())), preferred_element_type=F32)
                odd = lax.dot_general(odd_rows, x, (((0,), (0,)), ((), ())), preferred_element_type=F32)
                t = jnp.where(upper, odd, even)
                return t + _swap_lane_halves(t)

            q = [_pair_rows(q_ref, g) for g in groups]
            do = [_pair_rows(do_ref, g) for g in groups]
            k_prev = [_kv_operands(g, kp_ref[...]) for g in groups]
            k_cur = [_kv_operands(g, kc_ref[...]) for g in groups]
            v_prev = [_kv_operands(g, vp_ref[...]) for g in groups]
            v_cur = [_kv_operands(g, vc_ref[...]) for g in groups]
            probs = probs_ref[...].astype(F32)
            dp_prev = jnp.concatenate([nt(do[g], v_prev[g][i]) for g in groups for i in range(2)], axis=0)
            dp_cur = jnp.concatenate([nt(do[g], v_cur[g][i]) for g in groups for i in range(2)], axis=0)
            dp = jnp.where(prev_part, dp_prev, dp_cur)
            delta = jnp.sum(probs * dp, axis=1, keepdims=True)
            ds_prev, ds_cur = _split_folded(probs * (dp - delta) * (HEAD_DIM ** -0.5), prev_part)
            p_prev, p_cur = _split_folded(probs_ref[...], prev_part)
            for i, h in enumerate(_all_heads()):
                rows = slice(i * ATTN_BLOCK, (i + 1) * ATTN_BLOCK)
                dsink_ref[:, h:h + 1] += jnp.sum(-(psink_ref[:, h:h + 1] * delta[rows]), axis=0, keepdims=True)
            kv_grads = []
            for g in groups:
                even, odd = slice(2 * g * half, (2 * g + 1) * half), slice((2 * g + 1) * half, (2 * g + 2) * half)
                dq = (jnp.dot(ds_prev[even], k_prev[g][0], preferred_element_type=F32)
                      + jnp.dot(ds_cur[even], k_cur[g][0], preferred_element_type=F32)
                      + jnp.dot(ds_prev[odd], k_prev[g][1], preferred_element_type=F32)
                      + jnp.dot(ds_cur[odd], k_cur[g][1], preferred_element_type=F32))
                for i in range(PAIRS_PER_KV):
                    hp = g * PAIRS_PER_KV + i
                    dq_scr[:, hp * 2 * HEAD_DIM:(hp + 1) * 2 * HEAD_DIM] = dq[i * ATTN_BLOCK:(i + 1) * ATTN_BLOCK]
                kv_grads.append((kv_grad(ds_prev[even], ds_prev[odd], q[g]), kv_grad(ds_cur[even], ds_cur[odd], q[g]),
                                 kv_grad(p_prev[even], p_prev[odd], do[g]), kv_grad(p_cur[even], p_cur[odd], do[g])))
            (dkp0, dkc0, dvp0, dvc0), (dkp1, dkc1, dvp1, dvc1) = kv_grads
            prev_scr[:, :kw] = jnp.where(upper, dkp1, dkp0)
            prev_scr[:, kw:] = jnp.where(upper, dvp1, dvp0)
            cur_scr[:, :kw] = jnp.where(upper, dkc1, dkc0)
            cur_scr[:, kw:] = jnp.where(upper, dvc1, dvc0)
            dq_pre = _rope_transposed(dq_scr[...], cos_ref, sin_ref)
            dq_ref[...] = dq_pre.astype(BF16)
            dbq_ref[...] += jnp.sum(dq_pre, axis=0, keepdims=True)

        tot = carry[...] + prev_scr[...]
        dk_pre = _rope_transposed(tot[:, :kw], cosp_ref, sinp_ref)
        dkv_ref[:, :kw] = dk_pre.astype(BF16)
        dkv_ref[:, kw:] = tot[:, kw:].astype(BF16)
        dbkv_ref[:, :kw] += jnp.sum(dk_pre, axis=0, keepdims=True)
        dbkv_ref[:, kw:] += jnp.sum(tot[:, kw:], axis=0, keepdims=True)

        @pl.when(n < nb)
        def _():
            carry[...] = cur_scr[...]

    cur = lambda n: (jnp.minimum(n, nb - 1), 0)
    out_lag = lambda n: (jnp.maximum(n - 1, 0), 0)
    return pl.pallas_call(
        body, name="attn_bwd", grid=(nb + 1,),
        in_specs=[*_attn_specs(T),
                  pl.BlockSpec((ATTN_BLOCK, qw), cur),
                  pl.BlockSpec((ATTN_BLOCK, V7X_LANES), cur), pl.BlockSpec((ATTN_BLOCK, V7X_LANES), cur),
                  pl.BlockSpec((ATTN_BLOCK, V7X_LANES), out_lag), pl.BlockSpec((ATTN_BLOCK, V7X_LANES), out_lag),
                  pl.BlockSpec((None, N_Q_HEADS * ATTN_BLOCK, ATTN_BLOCK), lambda n: (jnp.minimum(n, nb - 1), 0, 0)),
                  pl.BlockSpec((ATTN_BLOCK, V7X_LANES), cur)],
        out_specs=[pl.BlockSpec((ATTN_BLOCK, qw), cur), pl.BlockSpec((ATTN_BLOCK, 2 * kw), out_lag),
                   _whole((1, N_Q_HEADS)), _whole((1, qw)), _whole((1, 2 * kw))],
        out_shape=[SDS((T, qw), BF16), SDS((T, 2 * kw), BF16),
                   SDS((1, N_Q_HEADS), F32), SDS((1, qw), F32), SDS((1, 2 * kw), F32)],
        scratch_shapes=[pltpu.VMEM((ATTN_BLOCK, 2 * kw), F32), pltpu.VMEM((ATTN_BLOCK, 2 * kw), F32),
                        pltpu.VMEM((ATTN_BLOCK, 2 * kw), F32), pltpu.VMEM((ATTN_BLOCK, qw), F32)],
        compiler_params=_params("arbitrary"),
    )(qkv, qkv, qkv, qkv, qkv, dao, cos, sin, cos, sin, probs_saved, psink_saved)


def _local_step(x, target, p, reduce_begin, reduce_send):
    T, D = x.shape
    cos, sin = _rope_tables(T)
    qw = N_Q_HEADS * HEAD_DIM
    nm, nf = p["norm_mix"], p["norm_ffn"]

    y0, qkv = _qkv_proj(x, nm[0:1], p["attn_w_qkv"], p["attn_b_qkv"], cos, sin, p["gather_started"])
    ao, attn_probs, sink_probs = _attn_fwd(qkv, p["attn_sinks"])
    h1, f0 = _mm_res("attn_out", ao, p["attn_w_o"], p["attn_b_o"], x, nf[0:1])
    p = {**p, **p["other_weights"](h1)}
    w1, w3 = p["ffn_w1"], p["ffn_w3"]
    act0, gg0, s0 = _ffn_up("ffn0_up", f0, w1, w3, 0, after=p["swap_started"])
    p = {**p, **p["rest_of_weights"](s0)}
    w2 = p["ffn_w2"]
    h2 = _ffn_down("ffn0_down", s0, w2, 0, h1)
    y1, a = _pw1_proj(h2, nm[1:2], p["conv_w_pw1"], p["conv_b_pw1"])
    c, act, h3, f1 = _conv_fwd(a, p["conv_w_dw"], p["conv_b_dw"], p["conv_ln_g"], p["conv_ln_b"],
                               p["conv_w_pw2"], p["conv_b_pw2"], h2, nf[1:2])
    act1, gg1, s1 = _ffn_up("ffn1_up", f1, w1, w3, 1)
    dh4, loss, d_norm_final = _ffn_down_loss("ffn1_down_loss", s1, w2, 1, h3, p["norm_final"], target)

    g = {}
    dg1, dg3, dw2_1 = _ffn_bwd_down("ffn1_bwd_down", dh4, w2, 1, act1, gg1, s1)
    dw1_1 = _mm_tn("ffn1_dw1", dg1, f1)
    dw3_1 = _mm_tn("ffn1_dw3", dg3, f1)
    begun = reduce_begin("ffn1", {("ffn_w1", 1): dw1_1, ("ffn_w3", 1): dw3_1, ("ffn_w2", 1): dw2_1})
    dh3, dnf1, db_pw2 = _mm_nt_normbwd("ffn1_bwd_in", [(dg1, w1, 1), (dg3, w3, 1)], h3, nf[1:2], dh4, begun)
    sent = reduce_send("ffn1", dh3)

    dw_pw2 = _mm_tn("conv_dw_pw2", act, dh3, after=sent)
    da, d_ln_g, d_ln_b, d_b_dw, d_w_dw, d_b_pw1 = _conv_bwd(dh3, p["conv_w_pw2"], c, a, p["conv_w_dw"],
                                                            p["conv_ln_g"], p["conv_ln_b"])
    dw_pw1 = _mm_tn("conv_dw_pw1", y1, da, col_chunks=N_CHIPS)
    begun = reduce_begin("conv", {("conv_w_pw2", 0): dw_pw2.reshape(N_CHIPS, -1, D), ("conv_w_pw1", 0): dw_pw1})
    dh2, dnm1, _ = _mm_nt_normbwd("conv_bwd_in", [(da, p["conv_w_pw1"], None)], h2, nm[1:2], dh3, begun)
    sent = reduce_send("conv", dh2)

    dg1, dg3, dw2_0 = _ffn_bwd_down("ffn0_bwd_down", dh2, w2, 0, act0, gg0, s0, after=sent)
    dw1_0 = _mm_tn("ffn0_dw1", dg1, f0)
    dw3_0 = _mm_tn("ffn0_dw3", dg3, f0)
    begun = reduce_begin("ffn0", {("ffn_w1", 0): dw1_0, ("ffn_w3", 0): dw3_0, ("ffn_w2", 0): dw2_0})
    dh1, dnf0, db_o = _mm_nt_normbwd("ffn0_bwd_in", [(dg1, w1, 0), (dg3, w3, 0)], h1, nf[0:1], dh2, begun)
    sent = reduce_send("ffn0", dh1)

    dw_o = _mm_tn("attn_dw_o", ao, dh1, after=sent)
    dao = _mm_nt("attn_bwd_out", dh1, p["attn_w_o"], BF16)
    dq, dkv, d_sinks, dbq, dbkv = _attn_bwd(qkv, dao, cos, sin, attn_probs, sink_probs)
    dwq = _mm_tn("attn_dw_q", dq, y0)
    dwkv = _mm_tn("attn_dw_kv", dkv, y0)
    wqkv = p["attn_w_qkv"]
    dwqkv = jnp.concatenate([dwq, dwkv], axis=0).reshape(N_CHIPS, -1, D)
    begun = reduce_begin("attn", {("attn_w_o", 0): dw_o.reshape(N_CHIPS, -1, D), ("attn_w_qkv", 0): dwqkv})
    sent = reduce_send("attn", begun)
    dx, dnm0, _ = _mm_nt_normbwd("attn_bwd_in", [(dq, wqkv[:qw], None), (dkv, wqkv[qw:], None)], x, nm[0:1], dh1,
                                 sent)

    g["norm_mix"] = jnp.concatenate([dnm0, dnm1], axis=0)
    g["norm_ffn"] = jnp.concatenate([dnf0, dnf1], axis=0)
    g["attn_b_qkv"] = jnp.concatenate([dbq, dbkv], axis=1)
    g["attn_sinks"] = d_sinks
    g["attn_b_o"] = db_o
    g["conv_b_pw1"] = d_b_pw1
    g["conv_w_dw"] = d_w_dw[:CONV_WIDTH]
    g["conv_b_dw"] = d_b_dw
    g["conv_ln_g"] = d_ln_g
    g["conv_ln_b"] = d_ln_b
    g["conv_b_pw2"] = db_pw2
    g["norm_final"] = d_norm_final
    return loss, dx, g


ANY = pl.BlockSpec(memory_space=pl.ANY)
VMEM_WHOLE = pl.BlockSpec(memory_space=pltpu.VMEM)


def _my_place():
    return lax.axis_index("x"), lax.axis_index("y"), lax.axis_index("c")


def _other_chips(x, y):
    places = [(1 - x, y), (x, 1 - y), (1 - x, 1 - y)]
    return [(bx, by, 2 * bx + by) for bx, by in places]


def _cast_into_slot(name, gathered, shard, chip_idx, after=None):
    rows, cols = shard.shape
    tr = _pack_row_tile(rows)

    def body(k_ref, s_ref, *rest):
        rest[-1][...] = s_ref[...].astype(BF16)

    return pl.pallas_call(
        body, name=name,
        grid_spec=pltpu.PrefetchScalarGridSpec(
            num_scalar_prefetch=1, grid=(rows // tr,),
            in_specs=[pl.BlockSpec((tr, cols), lambda i, k_ref: (i, 0)), pl.BlockSpec(memory_space=pl.ANY)]
            + ([] if after is None else [pl.BlockSpec(memory_space=pl.ANY)]),
            out_specs=pl.BlockSpec((None, tr, cols), lambda i, k_ref: (k_ref[0], i, 0))),
        out_shape=SDS(gathered.shape, BF16),
        input_output_aliases={2: 0},
        compiler_params=_params("parallel"),
    )(chip_idx, shard, gathered, *([] if after is None else [after]))


def _row_halves(ref, c):
    half = ref.shape[1] // 2
    return pl.ds(pl.multiple_of(c * half, 16), half), pl.ds(pl.multiple_of((1 - c) * half, 16), half)


def _gather_ici_copies(refs, send_sems, recv_sems):
    x, y, c = _my_place()
    k = 2 * x + y
    pairs = []
    for i, ref in enumerate(refs):
        mine, _ = _row_halves(ref, c)
        for j, (bx, by, kb) in enumerate(_other_chips(x, y)):
            sems = dict(send_sem=send_sems.at[3 * i + j], recv_sem=recv_sems.at[3 * i + j], device_id_type=MESH)
            send = pltpu.make_async_remote_copy(src_ref=ref.at[k, mine], dst_ref=ref.at[k, mine],
                                                device_id=(bx, by, c), **sems)
            arrival = pltpu.make_async_remote_copy(src_ref=ref.at[kb, mine], dst_ref=ref.at[kb, mine],
                                                   device_id=(bx, by, c), **sems)
            pairs.append((send, arrival))
    return pairs


def _gather_d2d_copies(refs, send_sems, recv_sems):
    x, y, c = _my_place()
    pairs = []
    for i, ref in enumerate(refs):
        mine, theirs = _row_halves(ref, c)
        for j, (_, _, kb) in enumerate(_other_chips(x, y)):
            sem = 3 * i + j
            sems = dict(send_sem=send_sems.at[sem], recv_sem=recv_sems.at[sem], device_id=(x, y, 1 - c),
                        device_id_type=MESH)
            send = pltpu.make_async_remote_copy(src_ref=ref.at[kb, mine], dst_ref=ref.at[kb, mine], **sems)
            arrival = pltpu.make_async_remote_copy(src_ref=ref.at[kb, theirs], dst_ref=ref.at[kb, theirs], **sems)
            pairs.append((send, arrival))
    return pairs


def _run_copies(pairs):
    for send, _ in pairs:
        send.start()
    for send, arrival in pairs:
        send.wait_send()
        arrival.wait_recv()


def _gather_stage_copies(stage, refs, send_sems, recv_sems):
    if stage == "ici":
        return _gather_ici_copies(refs, send_sems, recv_sems)
    return _gather_d2d_copies(refs, send_sems, recv_sems)


def _gather_start(name, gathered, after, stage="ici"):
    n_w = len(gathered)

    def body(*refs):
        in_refs = refs[:n_w]
        send_sems, recv_sems = refs[n_w + 1:n_w + 3]
        for send, _ in _gather_stage_copies(stage, in_refs, send_sems, recv_sems):
            send.start()
        refs[-1][...] = jnp.zeros_like(refs[-1])

    out = pl.pallas_call(
        body, name=name,
        out_shape=(pltpu.SemaphoreType.DMA((3 * n_w,)), pltpu.SemaphoreType.DMA((3 * n_w,)),
                   *[pltpu.HBM(g.shape, g.dtype) for g in gathered], SDS((8, V7X_LANES), F32)),
        in_specs=[*[HBM_SPEC] * n_w, ANY], out_specs=(SEM_SPEC, SEM_SPEC, *[HBM_SPEC] * n_w, VMEM_WHOLE),
        input_output_aliases={i: 2 + i for i in range(n_w)},
        compiler_params=pltpu.CompilerParams(has_side_effects=DATAFLOW),
    )(*[pltpu.with_memory_space_constraint(g, pltpu.HBM) for g in gathered], after)
    return out[0], out[1], list(out[2:2 + n_w]), out[-1]


def _gather_wait(name, send_sems, recv_sems, gathered, after, stage="ici"):
    n_w = len(gathered)

    def body(*refs):
        in_refs = refs[:n_w]
        send_sems, recv_sems = refs[n_w:n_w + 2]
        for send, arrival in _gather_stage_copies(stage, in_refs, send_sems, recv_sems):
            send.wait_send()
            arrival.wait_recv()

    out = pl.pallas_call(
        body, name=name, out_shape=tuple(pltpu.HBM(g.shape, g.dtype) for g in gathered),
        in_specs=[*[HBM_SPEC] * n_w, SEM_SPEC, SEM_SPEC, ANY], out_specs=tuple([HBM_SPEC] * n_w),
        input_output_aliases={i: i for i in range(n_w)},
        compiler_params=pltpu.CompilerParams(has_side_effects=DATAFLOW),
    )(*gathered, send_sems, recv_sems, after)
    return list(out)


def _swap_fetched_with_sibling(name, gathered, after=None):
    n_w = len(gathered)
    extra = [] if after is None else [after]

    def body(*refs):
        in_refs = refs[:n_w]
        send_sems, recv_sems = refs[-2:]
        _run_copies(_gather_d2d_copies(in_refs, send_sems, recv_sems))

    return pl.pallas_call(
        body, name=name, out_shape=[SDS(g.shape, g.dtype) for g in gathered],
        in_specs=[ANY] * (n_w + len(extra)), out_specs=[ANY] * n_w, input_output_aliases={i: i for i in range(n_w)},
        scratch_shapes=[pltpu.SemaphoreType.DMA((3 * n_w,)), pltpu.SemaphoreType.DMA((3 * n_w,))],
    )(*gathered, *extra)


def _sibling_swap_copies(g_refs, land_refs, send_sems, recv_sems):
    x, y, c = _my_place()
    copies = []
    for i, g_ref in enumerate(g_refs):
        half = g_ref.shape[1] // 2
        theirs = pl.ds(pl.multiple_of((1 - c) * half, 8), half)
        copies.append(pltpu.make_async_remote_copy(
            src_ref=g_ref.at[:, theirs], dst_ref=land_refs[i], send_sem=send_sems.at[i], recv_sem=recv_sems.at[i],
            device_id=(x, y, 1 - c), device_id_type=MESH))
    return copies


def _sibling_swap_start(name, grads):
    n_g = len(grads)

    def body(*refs):
        g_refs, land_refs = refs[:n_g], refs[n_g:2 * n_g]
        send_sems, recv_sems = refs[2 * n_g:2 * n_g + 2]
        for cp in _sibling_swap_copies(g_refs, land_refs, send_sems, recv_sems):
            cp.start()
        refs[-1][...] = jnp.zeros_like(refs[-1])

    lands = [pltpu.with_memory_space_constraint(lax.empty((g.shape[0], g.shape[1] // 2, g.shape[2]), g.dtype),
                                                pltpu.HBM) for g in grads]
    out = pl.pallas_call(
        body, name=name,
        out_shape=(pltpu.SemaphoreType.DMA((n_g,)), pltpu.SemaphoreType.DMA((n_g,)),
                   *[pltpu.HBM(g.shape, g.dtype) for g in grads], *[pltpu.HBM(l.shape, l.dtype) for l in lands],
                   SDS((8, V7X_LANES), F32)),
        in_specs=[HBM_SPEC] * (2 * n_g), out_specs=(SEM_SPEC, SEM_SPEC, *[HBM_SPEC] * (2 * n_g), VMEM_WHOLE),
        input_output_aliases={i: 2 + i for i in range(2 * n_g)},
        compiler_params=pltpu.CompilerParams(has_side_effects=DATAFLOW),
    )(*[pltpu.with_memory_space_constraint(g, pltpu.HBM) for g in grads], *lands)
    return out[0], out[1], list(out[2:2 + n_g]), list(out[2 + n_g:2 + 2 * n_g]), out[-1]


def _sibling_swap_wait(name, send_sems, recv_sems, grads, lands, after):
    n_g = len(grads)

    def body(*refs):
        g_refs, land_refs = refs[:n_g], refs[n_g:2 * n_g]
        send_sems, recv_sems = refs[2 * n_g:2 * n_g + 2]
        for cp in _sibling_swap_copies(g_refs, land_refs, send_sems, recv_sems):
            cp.wait_send()
            cp.wait_recv()

    out = pl.pallas_call(
        body, name=name,
        out_shape=(*[pltpu.HBM(g.shape, g.dtype) for g in grads], *[pltpu.HBM(l.shape, l.dtype) for l in lands]),
        in_specs=[*[HBM_SPEC] * (2 * n_g), SEM_SPEC, SEM_SPEC, ANY], out_specs=tuple([HBM_SPEC] * (2 * n_g)),
        input_output_aliases={i: i for i in range(2 * n_g)},
        compiler_params=pltpu.CompilerParams(has_side_effects=DATAFLOW),
    )(*grads, *lands, send_sems, recv_sems, after)
    return list(out[:n_g]), list(out[n_g:])


def _pack_row_tile(rows):
    for t in range(min(rows, 512), 7, -1):
        if rows % t == 0 and t % 8 == 0:
            return t
    return rows


def _add_sibling_half(name, grads, from_sibling, c_idx):
    n, R, w = grads.shape
    half = R // 2
    tr = _pack_row_tile(half)
    steps = half // tr

    def body(c_ref, g_ref, s_ref, o_ref):
        o_ref[...] = (g_ref[...] + s_ref[...]).astype(BF16)

    return pl.pallas_call(
        body, name=name,
        grid_spec=pltpu.PrefetchScalarGridSpec(
            num_scalar_prefetch=1, grid=(n, steps),
            in_specs=[pl.BlockSpec((1, tr, w), lambda j, i, c_ref: (j, c_ref[0] * steps + i, 0)),
                      pl.BlockSpec((1, tr, w), lambda j, i, c_ref: (j, i, 0))],
            out_specs=pl.BlockSpec((1, tr, w), lambda j, i, c_ref: (j, i, 0))),
        out_shape=SDS((n, half, w), BF16),
        compiler_params=_params("parallel", "parallel"),
    )(c_idx, grads, from_sibling)


HBM_SPEC = pl.BlockSpec(memory_space=pltpu.HBM)
SEM_SPEC = pl.BlockSpec(memory_space=pltpu.SEMAPHORE)
DATAFLOW = pltpu.SideEffectType.DATAFLOW_SIDE_EFFECTING


def _chip_scatter_copies(p_refs, land_refs, send_sems, recv_sems):
    x, y, c = _my_place()
    return [pltpu.make_async_remote_copy(
        src_ref=p_refs[i].at[kb], dst_ref=land_refs[i].at[j], send_sem=send_sems.at[3 * i + j],
        recv_sem=recv_sems.at[3 * i + j], device_id=(bx, by, c), device_id_type=MESH)
        for i in range(len(p_refs)) for j, (bx, by, kb) in enumerate(_other_chips(x, y))]


def _scatter_start(name, partials):
    n_p = len(partials)

    def body(*refs):
        p_refs, land_refs = refs[:n_p], refs[n_p:2 * n_p]
        send_sems, recv_sems = refs[2 * n_p:2 * n_p + 2]
        for cp in _chip_scatter_copies(p_refs, land_refs, send_sems, recv_sems):
            cp.start()
        refs[-1][...] = jnp.zeros_like(refs[-1])

    lands = [pltpu.with_memory_space_constraint(lax.empty((N_CHIPS - 1,) + p.shape[1:], p.dtype), pltpu.HBM)
             for p in partials]
    out = pl.pallas_call(
        body, name=name,
        out_shape=(pltpu.SemaphoreType.DMA((3 * n_p,)), pltpu.SemaphoreType.DMA((3 * n_p,)),
                   *[pltpu.HBM(p.shape, p.dtype) for p in partials], *[pltpu.HBM(l.shape, l.dtype) for l in lands],
                   SDS((8, V7X_LANES), F32)),
        in_specs=[HBM_SPEC] * (2 * n_p), out_specs=(SEM_SPEC, SEM_SPEC, *[HBM_SPEC] * (2 * n_p), VMEM_WHOLE),
        input_output_aliases={i: 2 + i for i in range(2 * n_p)},
        compiler_params=pltpu.CompilerParams(has_side_effects=DATAFLOW),
    )(*[pltpu.with_memory_space_constraint(p, pltpu.HBM) for p in partials], *lands)
    return out[0], out[1], list(out[2:2 + n_p]), list(out[2 + n_p:2 + 2 * n_p]), out[-1]


def _scatter_wait(name, send_sems, recv_sems, partials, lands, after):
    n_p = len(partials)

    def body(*refs):
        p_refs, land_refs = refs[:n_p], refs[n_p:2 * n_p]
        send_sems, recv_sems = refs[2 * n_p:2 * n_p + 2]
        for cp in _chip_scatter_copies(p_refs, land_refs, send_sems, recv_sems):
            cp.wait_send()
            cp.wait_recv()

    out = pl.pallas_call(
        body, name=name,
        out_shape=(*[pltpu.HBM(p.shape, p.dtype) for p in partials], *[pltpu.HBM(l.shape, l.dtype) for l in lands]),
        in_specs=[*[HBM_SPEC] * (2 * n_p), SEM_SPEC, SEM_SPEC, ANY], out_specs=tuple([HBM_SPEC] * (2 * n_p)),
        input_output_aliases={i: i for i in range(2 * n_p)},
        compiler_params=pltpu.CompilerParams(has_side_effects=DATAFLOW),
    )(*partials, *lands, send_sems, recv_sems, after)
    return list(out[:n_p]), list(out[n_p:])


def _sum_chip_partials(name, grads, from_sibling, received, shard, layer, place):
    n, half, w = from_sibling.shape
    tr = _pack_row_tile(half)
    steps = half // tr

    def body(place_ref, g_ref, s_ref, r_ref, shard_ref, o_ref):
        own = g_ref[0] + s_ref[0]
        o_ref[...] = ((own + r_ref[0].astype(F32)) + r_ref[1].astype(F32)) + r_ref[2].astype(F32)

    return pl.pallas_call(
        body, name=name,
        grid_spec=pltpu.PrefetchScalarGridSpec(
            num_scalar_prefetch=1, grid=(steps,),
            in_specs=[pl.BlockSpec((1, tr, w), lambda i, place_ref: (place_ref[0], place_ref[1] * steps + i, 0)),
                      pl.BlockSpec((1, tr, w), lambda i, place_ref: (place_ref[0], i, 0)),
                      pl.BlockSpec((n - 1, tr, w), lambda i, place_ref: (0, i, 0)),
                      pl.BlockSpec(memory_space=pl.ANY)],
            out_specs=pl.BlockSpec((tr, w), lambda i, place_ref: ((2 * layer + place_ref[1]) * steps + i, 0))),
        out_shape=SDS(shard.shape, F32),
        input_output_aliases={4: 0},
        compiler_params=_params("parallel"),
    )(place, grads, from_sibling, received, shard)


def _join_halves(shards, layers):
    n_s = len(shards)
    n_sem = sum(layers)

    def body(*refs):
        in_refs = refs[:n_s]
        send_sems, recv_sems = refs[2 * n_s:]
        x, y, c = _my_place()
        copies, sem = [], 0
        for ref, n_layers in zip(in_refs, layers):
            half = ref.shape[0] // (2 * n_layers)
            for layer in range(n_layers):
                mine = pl.ds(pl.multiple_of(layer * 2 * half + c * half, 8), half)
                theirs = pl.ds(pl.multiple_of(layer * 2 * half + (1 - c) * half, 8), half)
                send = pltpu.make_async_remote_copy(
                    src_ref=ref.at[mine], dst_ref=ref.at[mine], send_sem=send_sems.at[sem], recv_sem=recv_sems.at[sem],
                    device_id=(x, y, 1 - c), device_id_type=MESH)
                send.start()
                arrival = pltpu.make_async_remote_copy(
                    src_ref=ref.at[theirs], dst_ref=ref.at[theirs], send_sem=send_sems.at[sem],
                    recv_sem=recv_sems.at[sem], device_id=(x, y, 1 - c), device_id_type=MESH)
                copies.append((send, arrival))
                sem += 1
        for send, arrival in copies:
            send.wait_send()
            arrival.wait_recv()

    return pl.pallas_call(
        body, name="join_halves", out_shape=[SDS(s.shape, s.dtype) for s in shards],
        in_specs=[ANY] * n_s, out_specs=[ANY] * n_s,
        input_output_aliases={i: i for i in range(n_s)},
        scratch_shapes=[pltpu.SemaphoreType.DMA((n_sem,)), pltpu.SemaphoreType.DMA((n_sem,))],
    )(*shards)


def _all_to_all_copies(v_ref, land_ref, send_sems, recv_sems):
    x, y, c = _my_place()
    me = 4 * x + 2 * y + c
    pairs = []
    for k in range(1, N_DEV):
        px, py, pc = (1 - x if k & 4 else x), (1 - y if k & 2 else y), (1 - c if k & 1 else c)
        sems = dict(send_sem=send_sems.at[k - 1], recv_sem=recv_sems.at[k - 1], device_id=(px, py, pc),
                    device_id_type=MESH)
        send = pltpu.make_async_remote_copy(src_ref=v_ref, dst_ref=land_ref.at[me], **sems)
        arrival = pltpu.make_async_remote_copy(src_ref=v_ref, dst_ref=land_ref.at[4 * px + 2 * py + pc], **sems)
        pairs.append((send, arrival))
    return pairs


def _small_reduce_start(name, v):
    def body(v_ref, land_ref, send_sems, recv_sems, v_out, land_out):
        for send, _ in _all_to_all_copies(v_ref, land_ref, send_sems, recv_sems):
            send.start()

    land = pltpu.with_memory_space_constraint(jnp.zeros((N_DEV,) + v.shape, v.dtype), pltpu.HBM)
    return pl.pallas_call(
        body, name=name,
        out_shape=(pltpu.SemaphoreType.DMA((N_DEV - 1,)), pltpu.SemaphoreType.DMA((N_DEV - 1,)),
                   pltpu.HBM(v.shape, v.dtype), pltpu.HBM(land.shape, land.dtype)),
        in_specs=[HBM_SPEC, HBM_SPEC], out_specs=(SEM_SPEC, SEM_SPEC, HBM_SPEC, HBM_SPEC),
        input_output_aliases={0: 2, 1: 3},
        compiler_params=pltpu.CompilerParams(has_side_effects=DATAFLOW),
    )(pltpu.with_memory_space_constraint(v, pltpu.HBM), land)


def _small_reduce_wait(name, send_sems, recv_sems, v, land, after):
    def body(v_ref, land_ref, send_sems, recv_sems, after_ref, v_out, land_out):
        for send, arrival in _all_to_all_copies(v_ref, land_ref, send_sems, recv_sems):
            send.wait_send()
            arrival.wait_recv()

    return pl.pallas_call(
        body, name=name, out_shape=(pltpu.HBM(v.shape, v.dtype), pltpu.HBM(land.shape, land.dtype)),
        in_specs=[HBM_SPEC, HBM_SPEC, SEM_SPEC, SEM_SPEC, ANY], out_specs=(HBM_SPEC, HBM_SPEC),
        input_output_aliases={0: 0, 1: 1},
        compiler_params=pltpu.CompilerParams(has_side_effects=DATAFLOW),
    )(v, land, send_sems, recv_sems, after)


def _sum_device_slots(name, v, land, me):
    r, w = v.shape

    def body(me_ref, v_ref, land_ref, o_ref):
        mine = v_ref[...]
        acc = jnp.where(me_ref[0] == 0, mine, land_ref[0])
        for d in range(1, N_DEV):
            acc = acc + jnp.where(me_ref[0] == d, mine, land_ref[d])
        o_ref[...] = acc

    return pl.pallas_call(
        body, name=name,
        grid_spec=pltpu.PrefetchScalarGridSpec(
            num_scalar_prefetch=1, grid=(1,),
            in_specs=[pl.BlockSpec((r, w), lambda i, me_ref: (0, 0)),
                      pl.BlockSpec((N_DEV, r, w), lambda i, me_ref: (0, 0, 0))],
            out_specs=pl.BlockSpec((r, w), lambda i, me_ref: (0, 0))),
        out_shape=SDS((r, w), F32),
        compiler_params=_params("arbitrary"),
    )(me, v, land)


def _adamw(name, w, g, m, v):
    rows, width = w.shape
    tr = _pack_row_tile(rows)

    def body(w_ref, g_ref, m_ref, v_ref, g_out_ref, d_ref, nm_ref, nv_ref):
        gg = g_ref[...]
        g_out_ref[...] = gg
        m_new = ADAM_B1 * m_ref[...] + (1.0 - ADAM_B1) * gg
        v_new = ADAM_B2 * v_ref[...] + (1.0 - ADAM_B2) * (gg * gg)
        m_hat = m_new / (1.0 - ADAM_B1 ** ADAM_STEP)
        v_hat = v_new / (1.0 - ADAM_B2 ** ADAM_STEP)
        d_ref[...] = -ADAM_LR * (m_hat / (jnp.sqrt(v_hat) + ADAM_EPS) + ADAM_WD * w_ref[...])
        nm_ref[...] = m_new
        nv_ref[...] = v_new

    spec = _rows(tr, width)
    return pl.pallas_call(
        body, name=name, grid=(rows // tr,),
        in_specs=[spec] * 4, out_specs=[spec] * 4,
        out_shape=[SDS((rows, width), F32)] * 4,
        compiler_params=_params("parallel"),
    )(w, g, m, v)


WEIGHT_NAMES = ['norm_mix', 'norm_ffn', 'attn_w_qkv', 'attn_b_qkv', 'attn_sinks', 'attn_w_o', 'attn_b_o',
                'conv_w_pw1', 'conv_b_pw1', 'conv_w_dw', 'conv_b_dw', 'conv_ln_g', 'conv_ln_b', 'conv_w_pw2',
                'conv_b_pw2', 'ffn_w1', 'ffn_w3', 'ffn_w2', 'norm_final']
BIG = ['attn_w_qkv', 'attn_w_o', 'conv_w_pw1', 'conv_w_pw2', 'ffn_w1', 'ffn_w3', 'ffn_w2']
COLUMN_SPLIT = ('attn_w_qkv', 'conv_w_pw1', 'ffn_w1', 'ffn_w3')
STORED_TRANSPOSED = ('attn_w_qkv', 'ffn_w1', 'ffn_w3')
SMALL_SPLIT = ['conv_b_pw1', 'conv_w_dw', 'conv_b_dw', 'conv_ln_g', 'conv_ln_b', 'conv_b_pw2']
SMALL_WHOLE = ['norm_mix', 'norm_ffn', 'attn_b_qkv', 'attn_sinks', 'attn_b_o', 'norm_final']


def _keeps_rows(shape):
    return len(shape) == 2 and shape[0] > 1 and shape[1] == PACK_W


def _pack_rows(arrays, dtype, row_multiple):
    blocks = [jnp.pad(a.astype(dtype), ((0, -a.shape[0] % V7X_SUBLANES), (0, 0)))
              for a in arrays if _keeps_rows(a.shape)]
    flat = jnp.concatenate([a.astype(dtype).reshape(-1) for a in arrays if not _keeps_rows(a.shape)])
    multiple = max(row_multiple, V7X_SUBLANES)
    rows = -(-(-(-flat.shape[0] // PACK_W)) // multiple) * multiple
    blocks.append(jnp.pad(flat, (0, rows * PACK_W - flat.shape[0])).reshape(rows, PACK_W))
    return jnp.concatenate(blocks, axis=0) if len(blocks) > 1 else blocks[0]


def _unpack_rows(pack, shapes):
    out, row = {}, 0
    for i, shape in enumerate(shapes):
        if _keeps_rows(shape):
            out[i] = pack[row:row + shape[0]]
            row += -(-shape[0] // V7X_SUBLANES) * V7X_SUBLANES
    flat, at = pack[row:].reshape(-1), 0
    for i, shape in enumerate(shapes):
        if not _keeps_rows(shape):
            size = 1
            for s in shape:
                size *= s
            out[i] = flat[at:at + size].reshape(shape)
            at += size
    return [out[i] for i in range(len(shapes))]


def _join_chip_axis(name, parts):
    axis = parts.ndim - 1 if name in COLUMN_SPLIT or name in SMALL_SPLIT else parts.ndim - 2
    moved = jnp.moveaxis(parts, 0, axis - 1)
    shape = list(moved.shape)
    shape[axis - 1:axis + 1] = [shape[axis - 1] * shape[axis]]
    return moved.reshape(shape)


def _split_chip_axis(name, whole, shard_shape):
    axis = len(shard_shape) - 1 if name in COLUMN_SPLIT or name in SMALL_SPLIT else len(shard_shape) - 2
    shape = list(whole.shape)
    shape[axis:axis + 1] = [N_CHIPS, shard_shape[axis]]
    return jnp.moveaxis(whole.reshape(shape), axis, 0)


def kernel(x, norm_mix, norm_ffn, attn_w_qkv, attn_b_qkv, attn_sinks, attn_w_o, attn_b_o, conv_w_pw1, conv_b_pw1, conv_w_dw, conv_b_dw, conv_ln_g, conv_ln_b, conv_w_pw2, conv_b_pw2, ffn_w1, ffn_w3, ffn_w2, norm_final, loss_target, m_norm_mix, m_norm_ffn, m_attn_w_qkv, m_attn_b_qkv, m_attn_sinks, m_attn_w_o, m_attn_b_o, m_conv_w_pw1, m_conv_b_pw1, m_conv_w_dw, m_conv_b_dw, m_conv_ln_g, m_conv_ln_b, m_conv_w_pw2, m_conv_b_pw2, m_ffn_w1, m_ffn_w3, m_ffn_w2, m_norm_final, v_norm_mix, v_norm_ffn, v_attn_w_qkv, v_attn_b_qkv, v_attn_sinks, v_attn_w_o, v_attn_b_o, v_conv_w_pw1, v_conv_b_pw1, v_conv_w_dw, v_conv_b_dw, v_conv_ln_g, v_conv_ln_b, v_conv_w_pw2, v_conv_b_pw2, v_ffn_w1, v_ffn_w3, v_ffn_w2, v_norm_final):
    w = dict(zip(WEIGHT_NAMES, (norm_mix, norm_ffn, attn_w_qkv, attn_b_qkv, attn_sinks, attn_w_o, attn_b_o,
                                conv_w_pw1, conv_b_pw1, conv_w_dw, conv_b_dw, conv_ln_g, conv_ln_b, conv_w_pw2,
                                conv_b_pw2, ffn_w1, ffn_w3, ffn_w2, norm_final)))
    m = dict(zip(WEIGHT_NAMES, (m_norm_mix, m_norm_ffn, m_attn_w_qkv, m_attn_b_qkv, m_attn_sinks, m_attn_w_o,
                                m_attn_b_o, m_conv_w_pw1, m_conv_b_pw1, m_conv_w_dw, m_conv_b_dw, m_conv_ln_g,
                                m_conv_ln_b, m_conv_w_pw2, m_conv_b_pw2, m_ffn_w1, m_ffn_w3, m_ffn_w2, m_norm_final)))
    v = dict(zip(WEIGHT_NAMES, (v_norm_mix, v_norm_ffn, v_attn_w_qkv, v_attn_b_qkv, v_attn_sinks, v_attn_w_o,
                                v_attn_b_o, v_conv_w_pw1, v_conv_b_pw1, v_conv_w_dw, v_conv_b_dw, v_conv_ln_g,
                                v_conv_ln_b, v_conv_w_pw2, v_conv_b_pw2, v_ffn_w1, v_ffn_w3, v_ffn_w2, v_norm_final)))
    T, D = x.shape[1], x.shape[2]
    c_idx = lax.axis_index("c").astype(jnp.int32).reshape(1)
    chip = (2 * lax.axis_index("x") + lax.axis_index("y")).astype(jnp.int32)

    def as_rows(n, a):
        a = jnp.swapaxes(a, -1, -2) if n in STORED_TRANSPOSED else a
        return a.reshape(-1, a.shape[-1])

    def from_rows(n, rows):
        shape = w[n].shape[:-2] + w[n].shape[:-3:-1] if n in STORED_TRANSPOSED else w[n].shape
        a = rows.reshape(shape)
        return jnp.swapaxes(a, -1, -2) if n in STORED_TRANSPOSED else a

    def cast(n, after=None):
        return _cast_into_slot(f"cast_{n}", lax.empty((N_CHIPS,) + as_rows(n, w[n]).shape, BF16), as_rows(n, w[n]),
                               chip.reshape(1), after)

    first, later = BIG[:2], BIG[2:]
    slabs = {n: cast(n) for n in first}
    attn_send, attn_recv, attn_travelling, attn_started = _gather_start(
        "gather_attn_start", [slabs[n] for n in first], slabs[first[0]])
    slabs.update({n: cast(n, attn_started) for n in later})
    layers = ffn_w1.shape[0]
    small_shapes = [w[n].shape for n in SMALL_SPLIT]
    vec_send, vec_recv, vec_mine, vec_landing = _small_reduce_start(
        "small_gather_start", _pack_rows([w[n] for n in SMALL_SPLIT], F32, 8))
    attn_landed = _gather_wait("gather_attn_wait", attn_send, attn_recv, attn_travelling, vec_mine)
    send_sems, recv_sems, travelling, gather_started = _gather_start("gather_start", [slabs[n] for n in later],
                                                                     attn_landed[0])
    qkv_parts, w_o_parts = _swap_fetched_with_sibling("gather_attn_swap", attn_landed, gather_started)

    def small_vectors(after):
        mine, landed = _small_reduce_wait("small_gather_wait", vec_send, vec_recv, vec_mine, vec_landing, after)
        landed = lax.dynamic_update_index_in_dim(landed, mine, 2 * chip + c_idx[0], axis=0)
        per_chip = [_unpack_rows(landed[2 * j], small_shapes) for j in range(N_CHIPS)]
        whole = {n: _join_chip_axis(n, jnp.stack([per_chip[j][i] for j in range(N_CHIPS)]))
                 for i, n in enumerate(SMALL_SPLIT)}
        return {**{n: whole[n] for n in SMALL_SPLIT if n != "conv_w_dw"}, "conv_w_dw": whole["conv_w_dw"][0]}

    def other_weights(after):
        landed = dict(zip(later, _gather_wait("gather_wait", send_sems, recv_sems, travelling, after)))
        now, then = ["ffn_w1", "ffn_w3"], ["conv_w_pw1", "conv_w_pw2", "ffn_w2"]
        ready = dict(zip(now, _swap_fetched_with_sibling("gather_swap", [landed[n] for n in now])))
        swap_send, swap_recv, swapping_rest, swap_started = _gather_start(
            "gather_swap_start", [landed[n] for n in then], ready[now[0]], stage="d2d")

        def rest_of_weights(after_next):
            rest = dict(zip(then, _gather_wait("gather_swap_wait", swap_send, swap_recv, swapping_rest, after_next,
                                               stage="d2d")))
            return {"conv_w_pw1": rest["conv_w_pw1"], "conv_w_pw2": rest["conv_w_pw2"].reshape(-1, D),
                    "ffn_w2": rest["ffn_w2"].reshape(N_CHIPS, layers, -1, D), **small_vectors(after_next)}

        return {"ffn_w1": ready["ffn_w1"].reshape(N_CHIPS, layers, -1, D),
                "ffn_w3": ready["ffn_w3"].reshape(N_CHIPS, layers, -1, D),
                "swap_started": swap_started, "rest_of_weights": rest_of_weights}

    p = {
        "norm_mix": norm_mix, "norm_ffn": norm_ffn, "norm_final": norm_final.reshape(1, D),
        "attn_w_qkv": qkv_parts.reshape(-1, D), "attn_b_qkv": attn_b_qkv,
        "attn_sinks": attn_sinks, "attn_w_o": w_o_parts.reshape(-1, D), "attn_b_o": attn_b_o,
        "gather_started": gather_started, "other_weights": other_weights,
    }
    swapping, in_flight = {}, []

    def reduce_begin(tag, grads):
        keys = list(grads)
        *handles, begun = _sibling_swap_start(f"sibling_swap_start_{tag}", [grads[k] for k in keys])
        swapping[tag] = (keys, handles)
        return begun

    def reduce_send(tag, after):
        keys, (swap_send, swap_recv, grads, lands) = swapping[tag]
        grads, from_sibling = _sibling_swap_wait(f"sibling_swap_wait_{tag}", swap_send, swap_recv, grads, lands, after)
        partials = [_add_sibling_half(f"add_sibling_half_{tag}{i}", gr, fs, c_idx)
                    for i, (gr, fs) in enumerate(zip(grads, from_sibling))]
        *handles, sent = _scatter_start(f"scatter_start_{tag}", partials)
        in_flight.append((tag, keys, handles, grads, from_sibling))
        return sent

    loss_part, dx, g = _local_step(x[0], loss_target[0], p, reduce_begin, reduce_send)
    for n in SMALL_WHOLE + SMALL_SPLIT:
        g[n] = g[n].reshape((-1,) + g[n].shape[-2:]) if w[n].ndim == 3 else g[n].reshape(w[n].shape[:-1] + (-1,))

    small_pack = _pack_rows([loss_part] + [g[n] for n in SMALL_WHOLE] + [g[n] for n in SMALL_SPLIT], F32, 8)
    small_send, small_recv, small_pack, small_land = _small_reduce_start("small_reduce_start", small_pack)

    place = jnp.stack([chip, c_idx[0]])
    shard_grad = {n: lax.empty(as_rows(n, w[n]).shape, F32) for n in BIG}
    for tag, keys, (send_sems, recv_sems, partials, lands), grads, from_sibling in in_flight:
        _, received = _scatter_wait(f"scatter_wait_{tag}", send_sems, recv_sems, partials, lands, small_pack)
        for i, (n, layer) in enumerate(keys):
            shard_grad[n] = _sum_chip_partials(f"sum_chip_partials_{tag}{i}", grads[i], from_sibling[i], received[i],
                                               shard_grad[n], layer, place)
    g_big = dict(zip(BIG, _join_halves([shard_grad[n] for n in BIG], [w[n].shape[0] for n in BIG])))
    big_out = {}
    for n in BIG:
        step = _adamw(f"adamw_{n}", as_rows(n, w[n]), g_big[n], as_rows(n, m[n]), as_rows(n, v[n]))
        big_out[n] = [from_rows(n, a) for a in step]

    small_whole_shapes = [w[n].shape for n in SMALL_WHOLE]
    small_full_shapes = [g[n].shape for n in SMALL_SPLIT]
    small_pack, small_land = _small_reduce_wait("small_reduce_wait", small_send, small_recv, small_pack, small_land,
                                                big_out[BIG[-1]][1])
    reduced = _sum_device_slots("small_reduce_sum", small_pack, small_land, (2 * chip + c_idx[0]).reshape(1))
    pieces = _unpack_rows(reduced, [(1,)] + small_whole_shapes + small_full_shapes)
    loss = pieces[0].reshape(())
    g_small = dict(zip(SMALL_WHOLE, pieces[1:1 + len(SMALL_WHOLE)]))
    for n, whole in zip(SMALL_SPLIT, pieces[1 + len(SMALL_WHOLE):]):
        parts = _split_chip_axis(n, whole, w[n].shape)
        g_small[n] = lax.dynamic_index_in_dim(parts, chip, axis=0, keepdims=False)
    small = SMALL_WHOLE + SMALL_SPLIT
    _, d_small, m_small, v_small = _adamw(
        "adamw_small", _pack_rows([w[n] for n in small], F32, 8), _pack_rows([g_small[n] for n in small], F32, 8),
        _pack_rows([m[n] for n in small], F32, 8), _pack_rows([v[n] for n in small], F32, 8))

    outs = {}
    for slot, (tag, small_pack) in enumerate((("g", None), ("d", d_small), ("m", m_small), ("v", v_small))):
        vals = {n: big_out[n][slot] for n in BIG}
        if small_pack is None:
            vals.update(g_small)
        else:
            vals.update(zip(small, _unpack_rows(small_pack, [w[n].shape for n in small])))
        outs[tag] = vals
    return (loss, dx.reshape(1, T, D), *[outs["g"][n] for n in WEIGHT_NAMES], *[outs["d"][n] for n in WEIGHT_NAMES],
            *[outs["m"][n] for n in WEIGHT_NAMES], *[outs["v"][n] for n in WEIGHT_NAMES])
```

```python
import functools

import jax
import jax.numpy as jnp
from jax import lax
from jax.experimental import pallas as pl
from jax.experimental.pallas import tpu as pltpu

F32 = jnp.float32
BF16 = jnp.bfloat16
SDS = jax.ShapeDtypeStruct
MESH = pl.DeviceIdType.MESH

HEAD_DIM = 64
N_Q_HEADS = 16
N_KV_HEADS = 2
Q_PER_KV = N_Q_HEADS // N_KV_HEADS
ATTN_BLOCK = 128
ROPE_THETA = 10000.0
CONV_WIDTH = 31
CONV_HALO = 32
CONV_FIRST_TAP = CONV_HALO - CONV_WIDTH + 1
CONV_ROW_CHUNK = 64
CONV_LANE_CHUNK = 256
CONV_GRAD_UNROLL = 8
RMS_EPS = 1e-5
LN_EPS = 1e-5
ADAM_LR = 0.001
ADAM_B1 = 0.9
ADAM_B2 = 0.999
ADAM_EPS = 1e-08
ADAM_WD = 0.01
ADAM_STEP = 10

V7X_LANES = 128
V7X_SUBLANES = 8
V7X_VMEM_LIMIT_BYTES = 56 * 1024 * 1024

N_CHIPS = 4
N_DEV = 8
PACK_W = 1024

MASK_VALUE = -1e30


def _params(*semantics):
    return pltpu.CompilerParams(dimension_semantics=semantics, vmem_limit_bytes=V7X_VMEM_LIMIT_BYTES)


def _rows(tm, width):
    return pl.BlockSpec((tm, width), lambda i: (i, 0))


def _whole(shape):
    return pl.BlockSpec(shape, lambda *_: (0,) * len(shape))


def _rms_rstd(h):
    return lax.rsqrt(jnp.mean(h * h, axis=-1, keepdims=True) + RMS_EPS)


def _silu_and_grad(z):
    sg = jax.nn.sigmoid(z)
    return z * sg, sg * (1.0 + z * (1.0 - sg))


def _swap_rope_halves(t):
    w = t.shape[1]
    half = HEAD_DIM // 2
    lane = lax.broadcasted_iota(jnp.int32, t.shape, 1)
    upper = pltpu.roll(t, w - half, 1)
    lower = pltpu.roll(t, half, 1)
    return jnp.where(lane % HEAD_DIM < half, upper, lower)


def _rope(t, cos_ref, sin_ref):
    reps = t.shape[1] // V7X_LANES
    c = jnp.tile(cos_ref[...], (1, reps))
    s = jnp.tile(sin_ref[...], (1, reps))
    return t * c + _swap_rope_halves(t) * s


def _rope_transposed(dt, cos_ref, sin_ref):
    reps = dt.shape[1] // V7X_LANES
    c = jnp.tile(cos_ref[...], (1, reps))
    s = jnp.tile(sin_ref[...], (1, reps))
    return dt * c + _swap_rope_halves(dt * s)


def _rope_tables(seq_len):
    pos = jnp.arange(seq_len, dtype=F32)
    inv_freq = ROPE_THETA ** (-jnp.arange(0, HEAD_DIM, 2, dtype=F32) / HEAD_DIM)
    ang = pos[:, None] * jnp.tile(inv_freq, 2 * V7X_LANES // HEAD_DIM)[None, :]
    upper_half = jnp.arange(V7X_LANES) % HEAD_DIM >= HEAD_DIM // 2
    return jnp.cos(ang), jnp.where(upper_half[None, :], jnp.sin(ang), -jnp.sin(ang))


def _qkv_proj(h, g, w, b, cos, sin, after):
    T, D = h.shape
    N = w.shape[0]
    tm = min(1024, T)
    rope_w = N - N_KV_HEADS * HEAD_DIM

    def body(h_ref, g_ref, w_ref, b_ref, cos_ref, sin_ref, _, y_ref, o_ref):
        hh = h_ref[...]
        y = (hh * _rms_rstd(hh) * g_ref[...]).astype(BF16)
        y_ref[...] = y
        acc = _dot_nt(y, w_ref[...]) + b_ref[...]
        o_ref[:, :rope_w] = _rope(acc[:, :rope_w], cos_ref, sin_ref).astype(BF16)
        o_ref[:, rope_w:] = acc[:, rope_w:].astype(BF16)

    return pl.pallas_call(
        body, name="qkv_proj", grid=(T // tm,),
        in_specs=[_rows(tm, D), _whole((1, D)), _whole((N, D)), _whole((1, N)),
                  _rows(tm, V7X_LANES), _rows(tm, V7X_LANES), pl.BlockSpec(memory_space=pl.ANY)],
        out_specs=[_rows(tm, D), _rows(tm, N)],
        out_shape=[SDS((T, D), BF16), SDS((T, N), BF16)],
        compiler_params=_params("parallel"),
    )(h, g, w, b, cos, sin, after)


def _pw1_proj(h, g, w, b):
    T, D = h.shape
    n = w.shape[2]
    N = N_CHIPS * n
    tm = min(1024, T)

    def body(h_ref, g_ref, w_ref, b_ref, y_ref, o_ref):
        hh = h_ref[...]
        y = (hh * _rms_rstd(hh) * g_ref[...]).astype(BF16)
        y_ref[...] = y
        for j in range(N_CHIPS):
            cols = slice(j * n, (j + 1) * n)
            o_ref[:, cols] = (jnp.dot(y, w_ref[j], preferred_element_type=F32) + b_ref[:, cols]).astype(BF16)

    return pl.pallas_call(
        body, name="pw1_proj", grid=(T // tm,),
        in_specs=[_rows(tm, D), _whole((1, D)), _whole((N_CHIPS, D, n)), _whole((1, N))],
        out_specs=[_rows(tm, D), _rows(tm, N)],
        out_shape=[SDS((T, D), BF16), SDS((T, N), BF16)],
        compiler_params=_params("parallel"),
    )(h, g, w, b)


PAIRS_PER_KV = Q_PER_KV // 2


def _upper_lanes(shape):
    return lax.broadcasted_iota(jnp.int32, shape, 1) >= HEAD_DIM


def _swap_lane_halves(t):
    return pltpu.roll(t.astype(F32), HEAD_DIM, 1).astype(t.dtype)


def _kv_operands(g, t):
    swapped = _swap_lane_halves(t)
    in_lower, in_upper = (t, swapped) if g == 0 else (swapped, t)
    upper = _upper_lanes(t.shape)
    zero = jnp.zeros_like(t)
    return jnp.where(upper, zero, in_lower), jnp.where(upper, in_upper, zero)


def _group_heads(g):
    pairs = range(g * PAIRS_PER_KV, (g + 1) * PAIRS_PER_KV)
    return [2 * hp for hp in pairs] + [2 * hp + 1 for hp in pairs]


def _all_heads():
    return [h for g in range(N_KV_HEADS) for h in _group_heads(g)]


def _pair_rows(ref, g):
    pairs = range(g * PAIRS_PER_KV, (g + 1) * PAIRS_PER_KV)
    return jnp.concatenate([ref[:, hp * 2 * HEAD_DIM:(hp + 1) * 2 * HEAD_DIM] for hp in pairs], axis=0)


def _from_previous_block(rows):
    row = lax.broadcasted_iota(jnp.int32, (ATTN_BLOCK, ATTN_BLOCK), 0)
    col = lax.broadcasted_iota(jnp.int32, (ATTN_BLOCK, ATTN_BLOCK), 1)
    return jnp.concatenate([col > row] * (rows // ATTN_BLOCK), axis=0)


def _split_folded(t, prev_part):
    tb = t.astype(BF16)
    zero = jnp.zeros_like(tb)
    return jnp.where(prev_part, tb, zero), jnp.where(prev_part, zero, tb)


def _attn_specs(T):
    nb = T // ATTN_BLOCK
    kcol = N_Q_HEADS * HEAD_DIM // V7X_LANES
    cur = lambda n: jnp.minimum(n, nb - 1)
    prev = lambda n: jnp.maximum(jnp.minimum(n, nb - 1) - 1, 0)
    q_spec = pl.BlockSpec((ATTN_BLOCK, N_Q_HEADS * HEAD_DIM), lambda n: (cur(n), 0))
    kc_spec = pl.BlockSpec((ATTN_BLOCK, V7X_LANES), lambda n: (cur(n), kcol))
    kp_spec = pl.BlockSpec((ATTN_BLOCK, V7X_LANES), lambda n: (prev(n), kcol))
    vc_spec = pl.BlockSpec((ATTN_BLOCK, V7X_LANES), lambda n: (cur(n), kcol + 1))
    vp_spec = pl.BlockSpec((ATTN_BLOCK, V7X_LANES), lambda n: (prev(n), kcol + 1))
    return q_spec, kc_spec, kp_spec, vc_spec, vp_spec


def _attn_fwd(qkv, sinks):
    T = qkv.shape[0]
    nb = T // ATTN_BLOCK
    qw = N_Q_HEADS * HEAD_DIM
    all_rows = N_Q_HEADS * ATTN_BLOCK

    per_step = 4
    kcol = qw // V7X_LANES

    def body(q_ref, *refs):
        k_refs, v_refs = refs[:per_step + 1], refs[per_step + 1:2 * per_step + 2]
        sink_ref, o_ref, probs_ref, psink_ref = refs[2 * per_step + 2:]
        m = pl.program_id(0)
        half = PAIRS_PER_KV * ATTN_BLOCK
        groups = range(N_KV_HEADS)
        k_ops = [[_kv_operands(g, r[...]) for g in groups] for r in k_refs]
        v_ops = [[_kv_operands(g, r[...]) for g in groups] for r in v_refs]

        def scores(q, k):
            return lax.dot_general(q, k, (((1,), (1,)), ((), ())), preferred_element_type=F32)

        s_prev, s_cur = [], []
        for b in range(per_step):
            rows = slice(b * ATTN_BLOCK, (b + 1) * ATTN_BLOCK)
            for g in groups:
                pairs = range(g * PAIRS_PER_KV, (g + 1) * PAIRS_PER_KV)
                q = jnp.concatenate([q_ref[rows, hp * 2 * HEAD_DIM:(hp + 1) * 2 * HEAD_DIM] for hp in pairs], axis=0)
                for i in range(2):
                    from_prev = scores(q, k_ops[b][g][i])
                    if b == 0:
                        from_prev = jnp.where(m > 0, from_prev, MASK_VALUE * (HEAD_DIM ** 0.5))
                    s_prev.append(from_prev)
                    s_cur.append(scores(q, k_ops[b + 1][g][i]))
        prev_part = _from_previous_block(per_step * all_rows)
        s = jnp.where(prev_part, jnp.concatenate(s_prev, axis=0), jnp.concatenate(s_cur, axis=0)) * (HEAD_DIM ** -0.5)
        sink = jnp.concatenate([jnp.broadcast_to(sink_ref[0:1, h:h + 1], (ATTN_BLOCK, 1))
                                for h in _all_heads() * per_step], axis=0)
        top = jnp.maximum(jnp.max(s, axis=1, keepdims=True), sink)
        p = jnp.exp(s - top)
        e_sink = jnp.exp(sink - top)
        inv = 1.0 / (jnp.sum(p, axis=1, keepdims=True) + e_sink)
        probs, p_sink = p * inv, e_sink * inv
        p_prev, p_cur = _split_folded(probs, prev_part)
        lane = lax.broadcasted_iota(jnp.int32, (ATTN_BLOCK, V7X_LANES), 1)
        for b in range(per_step):
            rows = slice(b * ATTN_BLOCK, (b + 1) * ATTN_BLOCK)
            base = b * all_rows
            probs_ref[b] = probs[base:base + all_rows].astype(BF16)
            sink_tile = jnp.zeros((ATTN_BLOCK, V7X_LANES), F32)
            for i, h in enumerate(_all_heads()):
                sink_tile = jnp.where(lane == h, p_sink[base + i * ATTN_BLOCK:base + (i + 1) * ATTN_BLOCK], sink_tile)
            psink_ref[rows, :] = sink_tile
            for g in groups:
                even = slice(base + 2 * g * half, base + (2 * g + 1) * half)
                odd = slice(base + (2 * g + 1) * half, base + (2 * g + 2) * half)
                o = (jnp.dot(p_prev[even], v_ops[b][g][0], preferred_element_type=F32)
                     + jnp.dot(p_cur[even], v_ops[b + 1][g][0], preferred_element_type=F32)
                     + jnp.dot(p_prev[odd], v_ops[b][g][1], preferred_element_type=F32)
                     + jnp.dot(p_cur[odd], v_ops[b + 1][g][1], preferred_element_type=F32))
                for i in range(PAIRS_PER_KV):
                    hp = g * PAIRS_PER_KV + i
                    o_ref[rows, hp * 2 * HEAD_DIM:(hp + 1) * 2 * HEAD_DIM] = (
                        o[i * ATTN_BLOCK:(i + 1) * ATTN_BLOCK].astype(BF16))

    def kv_specs(col):
        return [pl.BlockSpec((ATTN_BLOCK, V7X_LANES),
                             functools.partial(lambda m, b: (jnp.maximum(per_step * m + b, 0), col), b=b))
                for b in range(-1, per_step)]

    return pl.pallas_call(
        body, name="attn_fwd", grid=(nb // per_step,),
        in_specs=[_rows(per_step * ATTN_BLOCK, qw), *kv_specs(kcol), *kv_specs(kcol + 1), _whole((1, N_Q_HEADS))],
        out_specs=[_rows(per_step * ATTN_BLOCK, qw),
                   pl.BlockSpec((per_step, all_rows, ATTN_BLOCK), lambda m: (m, 0, 0)),
                   _rows(per_step * ATTN_BLOCK, V7X_LANES)],
        out_shape=[SDS((T, qw), BF16), SDS((nb, all_rows, ATTN_BLOCK), BF16), SDS((T, V7X_LANES), F32)],
        compiler_params=_params("parallel"),
    )(*[qkv] * (2 * per_step + 3), sinks)


def _mm_res(name, a, w, b, res, g):
    T, K = a.shape
    D = w.shape[1]
    tm = min(1024, T)

    def body(a_ref, w_ref, b_ref, r_ref, g_ref, o_ref, f_ref):
        h = jnp.dot(a_ref[...], w_ref[...], preferred_element_type=F32) + b_ref[...] + r_ref[...]
        o_ref[...] = h
        f_ref[...] = (h * _rms_rstd(h) * g_ref[...]).astype(BF16)

    return pl.pallas_call(
        body, name=name, grid=(T // tm,),
        in_specs=[_rows(tm, K), _whole((K, D)), _whole((1, D)), _rows(tm, D), _whole((1, D))],
        out_specs=[_rows(tm, D), _rows(tm, D)],
        out_shape=[SDS((T, D), F32), SDS((T, D), BF16)],
        compiler_params=_params("parallel"),
    )(a, w, b, res, g)


def _ffn_down(name, s, w2, layer, res):
    _, T, n = s.shape
    D = w2.shape[3]
    tm = min(1024, T)

    def body(s_ref, w_ref, r_ref, o_ref):
        acc = r_ref[...]
        for j in range(N_CHIPS):
            acc = acc + jnp.dot(s_ref[j], w_ref[j], preferred_element_type=F32)
        o_ref[...] = acc

    return pl.pallas_call(
        body, name=name, grid=(T // tm,),
        in_specs=[pl.BlockSpec((N_CHIPS, tm, n), lambda i: (0, i, 0)),
                  pl.BlockSpec((N_CHIPS, None, n, D), lambda i: (0, layer, 0, 0)), _rows(tm, D)],
        out_specs=_rows(tm, D),
        out_shape=SDS((T, D), F32),
        compiler_params=_params("parallel"),
    )(s, w2, res)


def _ffn_up(name, f, w1, w3, layer, after=None):
    T, D = f.shape
    n = w1.shape[2]
    tm = min(1024, T)

    def body(f_ref, w1_ref, w3_ref, *rest):
        act_ref, gg_ref, s_ref = rest[-3:]
        ff = f_ref[...]
        g1 = _dot_nt(ff, w1_ref[...])
        g3 = _dot_nt(ff, w3_ref[...])
        act, dact = _silu_and_grad(g1)
        act_ref[...] = act.astype(BF16)
        gg_ref[...] = (g3 * dact).astype(BF16)
        s_ref[...] = (act * g3).astype(BF16)

    slab = pl.BlockSpec((None, tm, n), lambda j, i: (j, i, 0))
    wslab = pl.BlockSpec((None, None, n, D), lambda j, i: (j, layer, 0, 0))
    hidden = SDS((N_CHIPS, T, n), BF16)
    return pl.pallas_call(
        body, name=name, grid=(N_CHIPS, T // tm),
        in_specs=[pl.BlockSpec((tm, D), lambda j, i: (i, 0)), wslab, wslab]
        + ([] if after is None else [pl.BlockSpec(memory_space=pl.ANY)]),
        out_specs=[slab, slab, slab],
        out_shape=[hidden, hidden, hidden],
        compiler_params=_params("parallel", "parallel"),
    )(f, w1, w3, *([] if after is None else [after]))


def _glu(a, d):
    a = a.astype(F32)
    return a[:, :d] * jax.nn.sigmoid(a[:, d:])


def _conv_tile(T):
    return min(256, T)


def _fill_shifted(sh_ref, tc):
    n = tc + CONV_HALO - V7X_SUBLANES
    for r in range(1, V7X_SUBLANES):
        sh_ref[r, 0:n, :] = sh_ref[0, pl.ds(r, n), :]


def _depthwise_taps(sh_ref, w_ref, offsets, bias_ref, out_ref, tc):
    D = out_ref.shape[1]

    def chunk(i, carry):
        t0 = pl.multiple_of(i * CONV_ROW_CHUNK, CONV_ROW_CHUNK)
        for cb in range(D // CONV_LANE_CHUNK):
            cs = slice(cb * CONV_LANE_CHUNK, (cb + 1) * CONV_LANE_CHUNK)
            acc = jnp.zeros((CONV_ROW_CHUNK, CONV_LANE_CHUNK), F32)
            for r in range(V7X_SUBLANES):
                taps = [(j, o // V7X_SUBLANES) for j, o in enumerate(offsets) if o % V7X_SUBLANES == r]
                if not taps:
                    continue
                span = CONV_ROW_CHUNK + V7X_SUBLANES * max(q for _, q in taps)
                rows = sh_ref[r, pl.ds(t0, span), cs]
                for j, q in taps:
                    acc = acc + rows[V7X_SUBLANES * q:V7X_SUBLANES * q + CONV_ROW_CHUNK] * w_ref[j:j + 1, cs]
            if bias_ref is not None:
                acc = acc + bias_ref[:, cs]
            out_ref[pl.ds(t0, CONV_ROW_CHUNK), cs] = acc
        return carry

    lax.fori_loop(0, tc // CONV_ROW_CHUNK, chunk, 0)


def _depthwise_tap_grads(dy_sh, x_sh, offsets, dw_ref, tc):
    D = dw_ref.shape[1]
    for cb in range(D // V7X_LANES):
        cs = slice(cb * V7X_LANES, (cb + 1) * V7X_LANES)

        def row_tiles(i, accs, cs=cs):
            for k in range(CONV_GRAD_UNROLL):
                t0 = pl.multiple_of(i * (CONV_GRAD_UNROLL * V7X_SUBLANES), V7X_SUBLANES) + k * V7X_SUBLANES
                d = dy_sh[0, pl.ds(t0, V7X_SUBLANES), cs]
                accs = tuple(
                    acc + d * x_sh[o % V7X_SUBLANES, pl.ds(t0 + o // V7X_SUBLANES * V7X_SUBLANES, V7X_SUBLANES), cs]
                    for acc, o in zip(accs, offsets))
            return accs

        zero = jnp.zeros((V7X_SUBLANES, V7X_LANES), F32)
        accs = lax.fori_loop(0, tc // (CONV_GRAD_UNROLL * V7X_SUBLANES), row_tiles, tuple(zero for _ in offsets))
        for j, acc in enumerate(accs):
            dw_ref[j:j + 1, cs] += jnp.sum(acc, axis=0, keepdims=True)


def _conv_fwd(a, w_dw, b_dw, ln_g, ln_b, w_pw2, b_pw2, res, g_next):
    T = a.shape[0]
    D = a.shape[1] // 2
    tc = _conv_tile(T)
    per = tc // CONV_HALO

    def body(a_ref, ah_ref, w_ref, bdw_ref, lg_ref, lb_ref, wp_ref, bp_ref, r_ref, g_ref,
             c_ref, act_ref, h_ref, f_ref, u_sh):
        i = pl.program_id(0)
        u_sh[0, 0:CONV_HALO, :] = jnp.where(i > 0, _glu(ah_ref[...], D), 0.0)
        u_sh[0, CONV_HALO:, :] = _glu(a_ref[...], D)
        _fill_shifted(u_sh, tc)
        _depthwise_taps(u_sh, w_ref, [CONV_FIRST_TAP + j for j in range(CONV_WIDTH)], bdw_ref, c_ref, tc)
        c = c_ref[...]
        xc = c - jnp.mean(c, axis=-1, keepdims=True)
        z = xc * lax.rsqrt(jnp.mean(xc * xc, axis=-1, keepdims=True) + LN_EPS)
        l = z * lg_ref[...] + lb_ref[...]
        act = (l * jax.nn.sigmoid(l)).astype(BF16)
        act_ref[...] = act
        h = jnp.dot(act, wp_ref[...], preferred_element_type=F32) + bp_ref[...] + r_ref[...]
        h_ref[...] = h
        f_ref[...] = (h * _rms_rstd(h) * g_ref[...]).astype(BF16)

    return pl.pallas_call(
        body, name="conv_fwd", grid=(T // tc,),
        in_specs=[_rows(tc, 2 * D),
                  pl.BlockSpec((CONV_HALO, 2 * D), lambda i: (jnp.maximum(i * per - 1, 0), 0)),
                  _whole((CONV_WIDTH, D)), _whole((1, D)), _whole((1, D)), _whole((1, D)),
                  _whole((D, D)), _whole((1, D)), _rows(tc, D), _whole((1, D))],
        out_specs=[_rows(tc, D), _rows(tc, D), _rows(tc, D), _rows(tc, D)],
        out_shape=[SDS((T, D), F32), SDS((T, D), BF16), SDS((T, D), F32), SDS((T, D), BF16)],
        scratch_shapes=[pltpu.VMEM((V7X_SUBLANES, tc + CONV_HALO, D), F32)],
        compiler_params=_params("parallel"),
    )(a, a, w_dw, b_dw, ln_g, ln_b, w_pw2, b_pw2, res, g_next)


def _ffn_down_loss(name, s, w2, layer, res, g, target):
    _, T, n = s.shape
    D = w2.shape[3]
    tm = min(1024, T)

    def body(s_ref, w_ref, r_ref, g_ref, t_ref, dh_ref, loss_ref, dg_ref):
        @pl.when(pl.program_id(0) == 0)
        def _():
            loss_ref[...] = jnp.zeros_like(loss_ref)
            dg_ref[...] = jnp.zeros_like(dg_ref)

        hh = r_ref[...]
        for j in range(N_CHIPS):
            hh = hh + jnp.dot(s_ref[j], w_ref[j], preferred_element_type=F32)
        r = _rms_rstd(hh)
        g = g_ref[...]
        d = hh * r * g - t_ref[...]
        loss_ref[...] += 0.5 * jnp.sum(jnp.mean(d * d, axis=-1, keepdims=True), axis=0, keepdims=True)
        dout = d * (1.0 / D)
        dg_ref[...] += jnp.sum(dout * (hh * r), axis=0, keepdims=True)
        dxh = dout * g
        dh_ref[...] = r * dxh - hh * (r * r * r) * jnp.mean(dxh * hh, axis=-1, keepdims=True)

    return pl.pallas_call(
        body, name=name, grid=(T // tm,),
        in_specs=[pl.BlockSpec((N_CHIPS, tm, n), lambda i: (0, i, 0)),
                  pl.BlockSpec((N_CHIPS, None, n, D), lambda i: (0, layer, 0, 0)), _rows(tm, D),
                  _whole((1, D)), _rows(tm, D)],
        out_specs=[_rows(tm, D), _whole((1, 1)), _whole((1, D))],
        out_shape=[SDS((T, D), F32), SDS((1, 1), F32), SDS((1, D), F32)],
        compiler_params=_params("arbitrary"),
    )(s, w2, res, g, target)


def _ffn_bwd_down(name, dh, w2, layer, act, gate_grad, s, after=None):
    T, D = dh.shape
    n = w2.shape[2]
    tm = min(256, T)

    def body(dh_ref, w2_ref, act_ref, gg_ref, s_ref, *rest):
        dg1_ref, dg3_ref, dw_ref = rest[-3:]

        @pl.when(pl.program_id(0) == 0)
        def _():
            dw_ref[...] = jnp.zeros_like(dw_ref)

        dhb = dh_ref[...].astype(BF16)
        for j in range(N_CHIPS):
            ds = _dot_nt(dhb, w2_ref[j])
            dg1_ref[j] = (ds * gg_ref[j].astype(F32)).astype(BF16)
            dg3_ref[j] = (ds * act_ref[j].astype(F32)).astype(BF16)
            dw_ref[j] += _dot_tn(s_ref[j], dhb)

    slabs = pl.BlockSpec((N_CHIPS, tm, n), lambda i: (0, i, 0))
    hidden = SDS((N_CHIPS, T, n), BF16)
    return pl.pallas_call(
        body, name=name, grid=(T // tm,),
        in_specs=[_rows(tm, D),
                  pl.BlockSpec((N_CHIPS, None, n, D), lambda i: (0, layer, 0, 0), pipeline_mode=pl.Buffered(1)),
                  slabs, slabs, slabs] + ([] if after is None else [pl.BlockSpec(memory_space=pl.ANY)]),
        out_specs=[slabs, slabs, _whole((N_CHIPS, n, D))],
        out_shape=[hidden, hidden, SDS((N_CHIPS, n, D), F32)],
        compiler_params=_params("arbitrary"),
    )(dh, w2, act, gate_grad, s, *([] if after is None else [after]))


def _dot_tn(a, b):
    return lax.dot_general(a.astype(BF16), b.astype(BF16), (((0,), (0,)), ((), ())), preferred_element_type=F32)


def _dot_nt(a, b):
    return lax.dot_general(a.astype(BF16), b, (((1,), (1,)), ((), ())), preferred_element_type=F32)


def _mm_tn(name, a, b, col_chunks=1, after=None):
    a_slabs, b_slabs = a.ndim == 3, b.ndim == 3
    T = a.shape[-2]
    tt = min(1024, T)
    ka, nb = a.shape[-1], b.shape[-1]
    if a_slabs or b_slabs:
        out_dims = (N_CHIPS, ka, nb)
    elif col_chunks > 1:
        out_dims = (col_chunks, ka, nb // col_chunks)
    else:
        out_dims = (ka, nb)

    def body(a_ref, b_ref, *rest):
        o_ref = rest[-1]

        @pl.when(pl.program_id(0) == 0)
        def _():
            o_ref[...] = jnp.zeros_like(o_ref)

        if a_slabs:
            bb = b_ref[...].astype(BF16)
            for j in range(N_CHIPS):
                o_ref[j] += _dot_tn(a_ref[j], bb)
        elif b_slabs:
            aa = a_ref[...].astype(BF16)
            for j in range(N_CHIPS):
                o_ref[j] += _dot_tn(aa, b_ref[j])
        elif col_chunks > 1:
            aa = a_ref[...].astype(BF16)
            w = nb // col_chunks
            for j in range(col_chunks):
                o_ref[j] += _dot_tn(aa, b_ref[:, j * w:(j + 1) * w])
        else:
            o_ref[...] += _dot_tn(a_ref[...], b_ref[...])

    def spec(arr, slabs):
        if slabs:
            return pl.BlockSpec((N_CHIPS, tt, arr.shape[-1]), lambda t: (0, t, 0))
        return _rows(tt, arr.shape[-1])

    return pl.pallas_call(
        body, name=name, grid=(T // tt,),
        in_specs=[spec(a, a_slabs), spec(b, b_slabs)] + ([] if after is None else [pl.BlockSpec(memory_space=pl.ANY)]),
        out_specs=_whole(out_dims),
        out_shape=SDS(out_dims, F32),
        compiler_params=_params("arbitrary"),
    )(a, b, *([] if after is None else [after]))


def _mm_nt_normbwd(name, pairs, h, g, dh, after):
    T, D = h.shape
    tm = min(512, T)
    n_pairs = len(pairs)
    kinds = ["slabs" if dy.ndim == 3 else ("quarters" if w.ndim == 3 else "plain") for dy, w, _ in pairs]

    def body(*refs):
        dy_refs = refs[:n_pairs]
        w_refs = refs[n_pairs:2 * n_pairs]
        h_ref, g_ref, dh_ref, _, o_ref, dg_ref, cs_ref = refs[2 * n_pairs:]

        @pl.when(pl.program_id(0) == 0)
        def _():
            dg_ref[...] = jnp.zeros_like(dg_ref)
            cs_ref[...] = jnp.zeros_like(cs_ref)

        df = jnp.zeros((tm, D), F32)
        for dy_ref, w_ref, kd in zip(dy_refs, w_refs, kinds):
            if kd == "slabs":
                for j in range(N_CHIPS):
                    df = df + jnp.dot(dy_ref[j], w_ref[j], preferred_element_type=F32)
            elif kd == "quarters":
                n = w_ref.shape[2]
                for j in range(N_CHIPS):
                    df = df + _dot_nt(dy_ref[:, j * n:(j + 1) * n], w_ref[j])
            else:
                df = df + jnp.dot(dy_ref[...], w_ref[...], preferred_element_type=F32)
        hh = h_ref[...]
        r = _rms_rstd(hh)
        dg_ref[...] += jnp.sum(df * (hh * r), axis=0, keepdims=True)
        dxh = df * g_ref[...]
        out = dh_ref[...] + (r * dxh - hh * (r * r * r) * jnp.mean(dxh * hh, axis=-1, keepdims=True))
        o_ref[...] = out
        cs_ref[...] += jnp.sum(out, axis=0, keepdims=True)

    dy_specs, w_specs = [], []
    for (dy, w, layer), kd in zip(pairs, kinds):
        if kd == "slabs":
            dy_specs.append(pl.BlockSpec((N_CHIPS, tm, dy.shape[2]), lambda i: (0, i, 0)))
            w_specs.append(pl.BlockSpec((N_CHIPS, None, w.shape[2], D),
                                        functools.partial(lambda i, layer: (0, layer, 0, 0), layer=layer),
                                        pipeline_mode=pl.Buffered(1)))
        else:
            dy_specs.append(_rows(tm, dy.shape[1]))
            w_specs.append(_whole(w.shape))

    return pl.pallas_call(
        body, name=name, grid=(T // tm,),
        in_specs=[*dy_specs, *w_specs, _rows(tm, D), _whole((1, D)), _rows(tm, D), pl.BlockSpec(memory_space=pl.ANY)],
        out_specs=[_rows(tm, D), _whole((1, D)), _whole((1, D))],
        out_shape=[SDS((T, D), F32), SDS((1, D), F32), SDS((1, D), F32)],
        compiler_params=_params("arbitrary"),
    )(*[dy for dy, _, _ in pairs], *[w for _, w, _ in pairs], h, g, dh, after)


def _mm_nt(name, dy, w, out_dtype):
    T, N = dy.shape
    K = w.shape[0]
    tm = min(1024, T)

    def body(dy_ref, w_ref, o_ref):
        o_ref[...] = lax.dot_general(dy_ref[...].astype(BF16), w_ref[...], (((1,), (1,)), ((), ())),
                                     preferred_element_type=F32).astype(out_dtype)

    return pl.pallas_call(
        body, name=name, grid=(T // tm,),
        in_specs=[_rows(tm, N), _whole((K, N))],
        out_specs=_rows(tm, K),
        out_shape=SDS((T, K), out_dtype),
        compiler_params=_params("parallel"),
    )(dy, w)


def _conv_bwd(dh, w_pw2, c, a, w_dw, ln_g, ln_b):
    T, D = c.shape
    tc = _conv_tile(T)
    per = tc // CONV_HALO
    n_tiles = T // tc
    last_halo = T // CONV_HALO - 1

    def ln_bwd(dact_v, c_v, lg, lb):
        xc = c_v - jnp.mean(c_v, axis=-1, keepdims=True)
        rstd = lax.rsqrt(jnp.mean(xc * xc, axis=-1, keepdims=True) + LN_EPS)
        z = xc * rstd
        _, dsilu = _silu_and_grad(z * lg + lb)
        dl = dact_v * dsilu
        dz = dl * lg
        dc = rstd * (dz - jnp.mean(dz, axis=-1, keepdims=True) - z * jnp.mean(dz * z, axis=-1, keepdims=True))
        return dc, dl, z

    def body(dh_ref, dhn_ref, wp_ref, c_ref, cn_ref, a_ref, ah_ref, w_ref, lg_ref, lb_ref,
             da_ref, dlg_ref, dlb_ref, dbdw_ref, dwdw_ref, dbpw1_ref, dc_sh, u_sh, du_scr):
        i = pl.program_id(0)

        @pl.when(i == 0)
        def _():
            for ref in (dlg_ref, dlb_ref, dbdw_ref, dwdw_ref, dbpw1_ref):
                ref[...] = jnp.zeros_like(ref)

        lg, lb = lg_ref[...], lb_ref[...]
        dh_rows = jnp.concatenate([dh_ref[...].astype(BF16), dhn_ref[...].astype(BF16)], axis=0)
        dact = _dot_nt(dh_rows, wp_ref[...])
        dc, dl, z = ln_bwd(dact[:tc], c_ref[...], lg, lb)
        dlg_ref[...] += jnp.sum(dl * z, axis=0, keepdims=True)
        dlb_ref[...] += jnp.sum(dl, axis=0, keepdims=True)
        dbdw_ref[...] += jnp.sum(dc, axis=0, keepdims=True)
        dcn, _, _ = ln_bwd(dact[tc:], cn_ref[...], lg, lb)
        dc_sh[0, 0:tc, :] = dc
        dc_sh[0, tc:, :] = jnp.where(i < n_tiles - 1, dcn, 0.0)
        _fill_shifted(dc_sh, tc)

        a_v = a_ref[...].astype(F32)
        a1 = a_v[:, :D]
        sg = jax.nn.sigmoid(a_v[:, D:])
        u_sh[0, 0:CONV_HALO, :] = jnp.where(i > 0, _glu(ah_ref[...], D), 0.0)
        u_sh[0, CONV_HALO:, :] = a1 * sg
        _fill_shifted(u_sh, tc)

        _depthwise_taps(dc_sh, w_ref, [CONV_WIDTH - 1 - j for j in range(CONV_WIDTH)], None, du_scr, tc)
        _depthwise_tap_grads(dc_sh, u_sh, [CONV_FIRST_TAP + j for j in range(CONV_WIDTH)], dwdw_ref, tc)

        du = du_scr[...]
        da1 = du * sg
        da2 = du * a1 * sg * (1.0 - sg)
        da_ref[:, :D] = da1.astype(BF16)
        da_ref[:, D:] = da2.astype(BF16)
        dbpw1_ref[:, :D] += jnp.sum(da1, axis=0, keepdims=True)
        dbpw1_ref[:, D:] += jnp.sum(da2, axis=0, keepdims=True)

    nxt = lambda i: (jnp.minimum((i + 1) * per, last_halo), 0)
    return pl.pallas_call(
        body, name="conv_bwd", grid=(n_tiles,),
        in_specs=[_rows(tc, D), pl.BlockSpec((CONV_HALO, D), nxt), _whole((D, D)),
                  _rows(tc, D), pl.BlockSpec((CONV_HALO, D), nxt),
                  _rows(tc, 2 * D),
                  pl.BlockSpec((CONV_HALO, 2 * D), lambda i: (jnp.maximum(i * per - 1, 0), 0)),
                  _whole((CONV_WIDTH, D)), _whole((1, D)), _whole((1, D))],
        out_specs=[_rows(tc, 2 * D), _whole((1, D)), _whole((1, D)), _whole((1, D)),
                   _whole((CONV_HALO, D)), _whole((1, 2 * D))],
        out_shape=[SDS((T, 2 * D), BF16), SDS((1, D), F32), SDS((1, D), F32), SDS((1, D), F32),
                   SDS((CONV_HALO, D), F32), SDS((1, 2 * D), F32)],
        scratch_shapes=[pltpu.VMEM((V7X_SUBLANES, tc + CONV_HALO, D), F32),
                        pltpu.VMEM((V7X_SUBLANES, tc + CONV_HALO, D), F32), pltpu.VMEM((tc, D), F32)],
        compiler_params=_params("arbitrary"),
    )(dh, dh, w_pw2, c, c, a, a, w_dw, ln_g, ln_b)


def _attn_bwd(qkv, dao, cos, sin, probs_saved, psink_saved):
    T = qkv.shape[0]
    nb = T // ATTN_BLOCK
    qw = N_Q_HEADS * HEAD_DIM
    kw = N_KV_HEADS * HEAD_DIM

    def body(q_ref, kc_ref, kp_ref, vc_ref, vp_ref, do_ref, cos_ref, sin_ref, cosp_ref, sinp_ref, probs_ref, psink_ref,
             dq_ref, dkv_ref, dsink_ref, dbq_ref, dbkv_ref, carry, prev_scr, cur_scr, dq_scr):
        n = pl.program_id(0)

        @pl.when(n == 0)
        def _():
            for ref in (dsink_ref, dbq_ref, dbkv_ref, carry):
                ref[...] = jnp.zeros_like(ref)

        @pl.when(n == nb)
        def _():
            prev_scr[...] = jnp.zeros_like(prev_scr)

        @pl.when(n < nb)
        def _():
            prev_part = _from_previous_block(N_Q_HEADS * ATTN_BLOCK)
            half = PAIRS_PER_KV * ATTN_BLOCK
            upper = _upper_lanes((ATTN_BLOCK, 2 * HEAD_DIM))
            groups = range(N_KV_HEADS)

            def nt(a, b):
                return lax.dot_general(a, b, (((1,), (1,)), ((), ())), preferred_element_type=F32)

            def kv_grad(even_rows, odd_rows, x):
                even = lax.dot_general(even_rows, x, (((0,), (0,)), ((), ())), preferred_element_type=F32)
                odd = lax.dot_general(odd_rows, x, (((0,), (0,)), ((), ())), preferred_element_type=F32)
                t = jnp.where(upper, odd, even)
                return t + _swap_lane_halves(t)

            q = [_pair_rows(q_ref, g) for g in groups]
            do = [_pair_rows(do_ref, g) for g in groups]
            k_prev = [_kv_operands(g, kp_ref[...]) for g in groups]
            k_cur = [_kv_operands(g, kc_ref[...]) for g in groups]
            v_prev = [_kv_operands(g, vp_ref[...]) for g in groups]
            v_cur = [_kv_operands(g, vc_ref[...]) for g in groups]
            probs = probs_ref[...].astype(F32)
            dp_prev = jnp.concatenate([nt(do[g], v_prev[g][i]) for g in groups for i in range(2)], axis=0)
            dp_cur = jnp.concatenate([nt(do[g], v_cur[g][i]) for g in groups for i in range(2)], axis=0)
            dp = jnp.where(prev_part, dp_prev, dp_cur)
            delta = jnp.sum(probs * dp, axis=1, keepdims=True)
            ds_prev, ds_cur = _split_folded(probs * (dp - delta) * (HEAD_DIM ** -0.5), prev_part)
            p_prev, p_cur = _split_folded(probs_ref[...], prev_part)
            for i, h in enumerate(_all_heads()):
                rows = slice(i * ATTN_BLOCK, (i + 1) * ATTN_BLOCK)
                dsink_ref[:, h:h + 1] += jnp.sum(-(psink_ref[:, h:h + 1] * delta[rows]), axis=0, keepdims=True)
            kv_grads = []
            for g in groups:
                even, odd = slice(2 * g * half, (2 * g + 1) * half), slice((2 * g + 1) * half, (2 * g + 2) * half)
                dq = (jnp.dot(ds_prev[even], k_prev[g][0], preferred_element_type=F32)
                      + jnp.dot(ds_cur[even], k_cur[g][0], preferred_element_type=F32)
                      + jnp.dot(ds_prev[odd], k_prev[g][1], preferred_element_type=F32)
                      + jnp.dot(ds_cur[odd], k_cur[g][1], preferred_element_type=F32))
                for i in range(PAIRS_PER_KV):
                    hp = g * PAIRS_PER_KV + i
                    dq_scr[:, hp * 2 * HEAD_DIM:(hp + 1) * 2 * HEAD_DIM] = dq[i * ATTN_BLOCK:(i + 1) * ATTN_BLOCK]
                kv_grads.append((kv_grad(ds_prev[even], ds_prev[odd], q[g]), kv_grad(ds_cur[even], ds_cur[odd], q[g]),
                                 kv_grad(p_prev[even], p_prev[odd], do[g]), kv_grad(p_cur[even], p_cur[odd], do[g])))
            (dkp0, dkc0, dvp0, dvc0), (dkp1, dkc1, dvp1, dvc1) = kv_grads
            prev_scr[:, :kw] = jnp.where(upper, dkp1, dkp0)
            prev_scr[:, kw:] = jnp.where(upper, dvp1, dvp0)
            cur_scr[:, :kw] = jnp.where(upper, dkc1, dkc0)
            cur_scr[:, kw:] = jnp.where(upper, dvc1, dvc0)
            dq_pre = _rope_transposed(dq_scr[...], cos_ref, sin_ref)
            dq_ref[...] = dq_pre.astype(BF16)
            dbq_ref[...] += jnp.sum(dq_pre, axis=0, keepdims=True)

        tot = carry[...] + prev_scr[...]
        dk_pre = _rope_transposed(tot[:, :kw], cosp_ref, sinp_ref)
        dkv_ref[:, :kw] = dk_pre.astype(BF16)
        dkv_ref[:, kw:] = tot[:, kw:].astype(BF16)
        dbkv_ref[:, :kw] += jnp.sum(dk_pre, axis=0, keepdims=True)
        dbkv_ref[:, kw:] += jnp.sum(tot[:, kw:], axis=0, keepdims=True)

        @pl.when(n < nb)
        def _():
            carry[...] = cur_scr[...]

    cur = lambda n: (jnp.minimum(n, nb - 1), 0)
    out_lag = lambda n: (jnp.maximum(n - 1, 0), 0)
    return pl.pallas_call(
        body, name="attn_bwd", grid=(nb + 1,),
        in_specs=[*_attn_specs(T),
                  pl.BlockSpec((ATTN_BLOCK, qw), cur),
                  pl.BlockSpec((ATTN_BLOCK, V7X_LANES), cur), pl.BlockSpec((ATTN_BLOCK, V7X_LANES), cur),
                  pl.BlockSpec((ATTN_BLOCK, V7X_LANES), out_lag), pl.BlockSpec((ATTN_BLOCK, V7X_LANES), out_lag),
                  pl.BlockSpec((None, N_Q_HEADS * ATTN_BLOCK, ATTN_BLOCK), lambda n: (jnp.minimum(n, nb - 1), 0, 0)),
                  pl.BlockSpec((ATTN_BLOCK, V7X_LANES), cur)],
        out_specs=[pl.BlockSpec((ATTN_BLOCK, qw), cur), pl.BlockSpec((ATTN_BLOCK, 2 * kw), out_lag),
                   _whole((1, N_Q_HEADS)), _whole((1, qw)), _whole((1, 2 * kw))],
        out_shape=[SDS((T, qw), BF16), SDS((T, 2 * kw), BF16),
                   SDS((1, N_Q_HEADS), F32), SDS((1, qw), F32), SDS((1, 2 * kw), F32)],
        scratch_shapes=[pltpu.VMEM((ATTN_BLOCK, 2 * kw), F32), pltpu.VMEM((ATTN_BLOCK, 2 * kw), F32),
                        pltpu.VMEM((ATTN_BLOCK, 2 * kw), F32), pltpu.VMEM((ATTN_BLOCK, qw), F32)],
        compiler_params=_params("arbitrary"),
    )(qkv, qkv, qkv, qkv, qkv, dao, cos, sin, cos, sin, probs_saved, psink_saved)


def _local_step(x, target, p, reduce_begin, reduce_send):
    T, D = x.shape
    cos, sin = _rope_tables(T)
    qw = N_Q_HEADS * HEAD_DIM
    nm, nf = p["norm_mix"], p["norm_ffn"]

    y0, qkv = _qkv_proj(x, nm[0:1], p["attn_w_qkv"], p["attn_b_qkv"], cos, sin, p["gather_started"])
    ao, attn_probs, sink_probs = _attn_fwd(qkv, p["attn_sinks"])
    h1, f0 = _mm_res("attn_out", ao, p["attn_w_o"], p["attn_b_o"], x, nf[0:1])
    p = {**p, **p["other_weights"](h1)}
    w1, w3 = p["ffn_w1"], p["ffn_w3"]
    act0, gg0, s0 = _ffn_up("ffn0_up", f0, w1, w3, 0, after=p["swap_started"])
    p = {**p, **p["rest_of_weights"](s0)}
    w2 = p["ffn_w2"]
    h2 = _ffn_down("ffn0_down", s0, w2, 0, h1)
    y1, a = _pw1_proj(h2, nm[1:2], p["conv_w_pw1"], p["conv_b_pw1"])
    c, act, h3, f1 = _conv_fwd(a, p["conv_w_dw"], p["conv_b_dw"], p["conv_ln_g"], p["conv_ln_b"],
                               p["conv_w_pw2"], p["conv_b_pw2"], h2, nf[1:2])
    act1, gg1, s1 = _ffn_up("ffn1_up", f1, w1, w3, 1)
    dh4, loss, d_norm_final = _ffn_down_loss("ffn1_down_loss", s1, w2, 1, h3, p["norm_final"], target)

    g = {}
    dg1, dg3, dw2_1 = _ffn_bwd_down("ffn1_bwd_down", dh4, w2, 1, act1, gg1, s1)
    dw1_1 = _mm_tn("ffn1_dw1", dg1, f1)
    dw3_1 = _mm_tn("ffn1_dw3", dg3, f1)
    begun = reduce_begin("ffn1", {("ffn_w1", 1): dw1_1, ("ffn_w3", 1): dw3_1, ("ffn_w2", 1): dw2_1})
    dh3, dnf1, db_pw2 = _mm_nt_normbwd("ffn1_bwd_in", [(dg1, w1, 1), (dg3, w3, 1)], h3, nf[1:2], dh4, begun)
    sent = reduce_send("ffn1", dh3)

    dw_pw2 = _mm_tn("conv_dw_pw2", act, dh3, after=sent)
    da, d_ln_g, d_ln_b, d_b_dw, d_w_dw, d_b_pw1 = _conv_bwd(dh3, p["conv_w_pw2"], c, a, p["conv_w_dw"],
                                                            p["conv_ln_g"], p["conv_ln_b"])
    dw_pw1 = _mm_tn("conv_dw_pw1", y1, da, col_chunks=N_CHIPS)
    begun = reduce_begin("conv", {("conv_w_pw2", 0): dw_pw2.reshape(N_CHIPS, -1, D), ("conv_w_pw1", 0): dw_pw1})
    dh2, dnm1, _ = _mm_nt_normbwd("conv_bwd_in", [(da, p["conv_w_pw1"], None)], h2, nm[1:2], dh3, begun)
    sent = reduce_send("conv", dh2)

    dg1, dg3, dw2_0 = _ffn_bwd_down("ffn0_bwd_down", dh2, w2, 0, act0, gg0, s0, after=sent)
    dw1_0 = _mm_tn("ffn0_dw1", dg1, f0)
    dw3_0 = _mm_tn("ffn0_dw3", dg3, f0)
    begun = reduce_begin("ffn0", {("ffn_w1", 0): dw1_0, ("ffn_w3", 0): dw3_0, ("ffn_w2", 0): dw2_0})
    dh1, dnf0, db_o = _mm_nt_normbwd("ffn0_bwd_in", [(dg1, w1, 0), (dg3, w3, 0)], h1, nf[0:1], dh2, begun)
    sent = reduce_send("ffn0", dh1)

    dw_o = _mm_tn("attn_dw_o", ao, dh1, after=sent)
    dao = _mm_nt("attn_bwd_out", dh1, p["attn_w_o"], BF16)
    dq, dkv, d_sinks, dbq, dbkv = _attn_bwd(qkv, dao, cos, sin, attn_probs, sink_probs)
    dwq = _mm_tn("attn_dw_q", dq, y0)
    dwkv = _mm_tn("attn_dw_kv", dkv, y0)
    wqkv = p["attn_w_qkv"]
    dwqkv = jnp.concatenate([dwq, dwkv], axis=0).reshape(N_CHIPS, -1, D)
    begun = reduce_begin("attn", {("attn_w_o", 0): dw_o.reshape(N_CHIPS, -1, D), ("attn_w_qkv", 0): dwqkv})
    sent = reduce_send("attn", begun)
    dx, dnm0, _ = _mm_nt_normbwd("attn_bwd_in", [(dq, wqkv[:qw], None), (dkv, wqkv[qw:], None)], x, nm[0:1], dh1,
                                 sent)

    g["norm_mix"] = jnp.concatenate([dnm0, dnm1], axis=0)
    g["norm_ffn"] = jnp.concatenate([dnf0, dnf1], axis=0)
    g["attn_b_qkv"] = jnp.concatenate([dbq, dbkv], axis=1)
    g["attn_sinks"] = d_sinks
    g["attn_b_o"] = db_o
    g["conv_b_pw1"] = d_b_pw1
    g["conv_w_dw"] = d_w_dw[:CONV_WIDTH]
    g["conv_b_dw"] = d_b_dw
    g["conv_ln_g"] = d_ln_g
    g["conv_ln_b"] = d_ln_b
    g["conv_b_pw2"] = db_pw2
    g["norm_final"] = d_norm_final
    return loss, dx, g


ANY = pl.BlockSpec(memory_space=pl.ANY)
VMEM_WHOLE = pl.BlockSpec(memory_space=pltpu.VMEM)


def _my_place():
    return lax.axis_index("x"), lax.axis_index("y"), lax.axis_index("c")


def _other_chips(x, y):
    places = [(1 - x, y), (x, 1 - y), (1 - x, 1 - y)]
    return [(bx, by, 2 * bx + by) for bx, by in places]


def _cast_into_slot(name, gathered, shard, chip_idx, after=None):
    rows, cols = shard.shape
    tr = _pack_row_tile(rows)

    def body(k_ref, s_ref, *rest):
        rest[-1][...] = s_ref[...].astype(BF16)

    return pl.pallas_call(
        body, name=name,
        grid_spec=pltpu.PrefetchScalarGridSpec(
            num_scalar_prefetch=1, grid=(rows // tr,),
            in_specs=[pl.BlockSpec((tr, cols), lambda i, k_ref: (i, 0)), pl.BlockSpec(memory_space=pl.ANY)]
            + ([] if after is None else [pl.BlockSpec(memory_space=pl.ANY)]),
            out_specs=pl.BlockSpec((None, tr, cols), lambda i, k_ref: (k_ref[0], i, 0))),
        out_shape=SDS(gathered.shape, BF16),
        input_output_aliases={2: 0},
        compiler_params=_params("parallel"),
    )(chip_idx, shard, gathered, *([] if after is None else [after]))


def _row_halves(ref, c):
    half = ref.shape[1] // 2
    return pl.ds(pl.multiple_of(c * half, 16), half), pl.ds(pl.multiple_of((1 - c) * half, 16), half)


def _gather_ici_copies(refs, send_sems, recv_sems):
    x, y, c = _my_place()
    k = 2 * x + y
    pairs = []
    for i, ref in enumerate(refs):
        mine, _ = _row_halves(ref, c)
        for j, (bx, by, kb) in enumerate(_other_chips(x, y)):
            sems = dict(send_sem=send_sems.at[3 * i + j], recv_sem=recv_sems.at[3 * i + j], device_id_type=MESH)
            send = pltpu.make_async_remote_copy(src_ref=ref.at[k, mine], dst_ref=ref.at[k, mine],
                                                device_id=(bx, by, c), **sems)
            arrival = pltpu.make_async_remote_copy(src_ref=ref.at[kb, mine], dst_ref=ref.at[kb, mine],
                                                   device_id=(bx, by, c), **sems)
            pairs.append((send, arrival))
    return pairs


def _gather_d2d_copies(refs, send_sems, recv_sems):
    x, y, c = _my_place()
    pairs = []
    for i, ref in enumerate(refs):
        mine, theirs = _row_halves(ref, c)
        for j, (_, _, kb) in enumerate(_other_chips(x, y)):
            sem = 3 * i + j
            sems = dict(send_sem=send_sems.at[sem], recv_sem=recv_sems.at[sem], device_id=(x, y, 1 - c),
                        device_id_type=MESH)
            send = pltpu.make_async_remote_copy(src_ref=ref.at[kb, mine], dst_ref=ref.at[kb, mine], **sems)
            arrival = pltpu.make_async_remote_copy(src_ref=ref.at[kb, theirs], dst_ref=ref.at[kb, theirs], **sems)
            pairs.append((send, arrival))
    return pairs


def _run_copies(pairs):
    for send, _ in pairs:
        send.start()
    for send, arrival in pairs:
        send.wait_send()
        arrival.wait_recv()


def _gather_stage_copies(stage, refs, send_sems, recv_sems):
    if stage == "ici":
        return _gather_ici_copies(refs, send_sems, recv_sems)
    return _gather_d2d_copies(refs, send_sems, recv_sems)


def _gather_start(name, gathered, after, stage="ici"):
    n_w = len(gathered)

    def body(*refs):
        in_refs = refs[:n_w]
        send_sems, recv_sems = refs[n_w + 1:n_w + 3]
        for send, _ in _gather_stage_copies(stage, in_refs, send_sems, recv_sems):
            send.start()
        refs[-1][...] = jnp.zeros_like(refs[-1])

    out = pl.pallas_call(
        body, name=name,
        out_shape=(pltpu.SemaphoreType.DMA((3 * n_w,)), pltpu.SemaphoreType.DMA((3 * n_w,)),
                   *[pltpu.HBM(g.shape, g.dtype) for g in gathered], SDS((8, V7X_LANES), F32)),
        in_specs=[*[HBM_SPEC] * n_w, ANY], out_specs=(SEM_SPEC, SEM_SPEC, *[HBM_SPEC] * n_w, VMEM_WHOLE),
        input_output_aliases={i: 2 + i for i in range(n_w)},
        compiler_params=pltpu.CompilerParams(has_side_effects=DATAFLOW),
    )(*[pltpu.with_memory_space_constraint(g, pltpu.HBM) for g in gathered], after)
    return out[0], out[1], list(out[2:2 + n_w]), out[-1]


def _gather_wait(name, send_sems, recv_sems, gathered, after, stage="ici"):
    n_w = len(gathered)

    def body(*refs):
        in_refs = refs[:n_w]
        send_sems, recv_sems = refs[n_w:n_w + 2]
        for send, arrival in _gather_stage_copies(stage, in_refs, send_sems, recv_sems):
            send.wait_send()
            arrival.wait_recv()

    out = pl.pallas_call(
        body, name=name, out_shape=tuple(pltpu.HBM(g.shape, g.dtype) for g in gathered),
        in_specs=[*[HBM_SPEC] * n_w, SEM_SPEC, SEM_SPEC, ANY], out_specs=tuple([HBM_SPEC] * n_w),
        input_output_aliases={i: i for i in range(n_w)},
        compiler_params=pltpu.CompilerParams(has_side_effects=DATAFLOW),
    )(*gathered, send_sems, recv_sems, after)
    return list(out)


def _swap_fetched_with_sibling(name, gathered, after=None):
    n_w = len(gathered)
    extra = [] if after is None else [after]

    def body(*refs):
        in_refs = refs[:n_w]
        send_sems, recv_sems = refs[-2:]
        _run_copies(_gather_d2d_copies(in_refs, send_sems, recv_sems))

    return pl.pallas_call(
        body, name=name, out_shape=[SDS(g.shape, g.dtype) for g in gathered],
        in_specs=[ANY] * (n_w + len(extra)), out_specs=[ANY] * n_w, input_output_aliases={i: i for i in range(n_w)},
        scratch_shapes=[pltpu.SemaphoreType.DMA((3 * n_w,)), pltpu.SemaphoreType.DMA((3 * n_w,))],
    )(*gathered, *extra)


def _sibling_swap_copies(g_refs, land_refs, send_sems, recv_sems):
    x, y, c = _my_place()
    copies = []
    for i, g_ref in enumerate(g_refs):
        half = g_ref.shape[1] // 2
        theirs = pl.ds(pl.multiple_of((1 - c) * half, 8), half)
        copies.append(pltpu.make_async_remote_copy(
            src_ref=g_ref.at[:, theirs], dst_ref=land_refs[i], send_sem=send_sems.at[i], recv_sem=recv_sems.at[i],
            device_id=(x, y, 1 - c), device_id_type=MESH))
    return copies


def _sibling_swap_start(name, grads):
    n_g = len(grads)

    def body(*refs):
        g_refs, land_refs = refs[:n_g], refs[n_g:2 * n_g]
        send_sems, recv_sems = refs[2 * n_g:2 * n_g + 2]
        for cp in _sibling_swap_copies(g_refs, land_refs, send_sems, recv_sems):
            cp.start()
        refs[-1][...] = jnp.zeros_like(refs[-1])

    lands = [pltpu.with_memory_space_constraint(lax.empty((g.shape[0], g.shape[1] // 2, g.shape[2]), g.dtype),
                                                pltpu.HBM) for g in grads]
    out = pl.pallas_call(
        body, name=name,
        out_shape=(pltpu.SemaphoreType.DMA((n_g,)), pltpu.SemaphoreType.DMA((n_g,)),
                   *[pltpu.HBM(g.shape, g.dtype) for g in grads], *[pltpu.HBM(l.shape, l.dtype) for l in lands],
                   SDS((8, V7X_LANES), F32)),
        in_specs=[HBM_SPEC] * (2 * n_g), out_specs=(SEM_SPEC, SEM_SPEC, *[HBM_SPEC] * (2 * n_g), VMEM_WHOLE),
        input_output_aliases={i: 2 + i for i in range(2 * n_g)},
        compiler_params=pltpu.CompilerParams(has_side_effects=DATAFLOW),
    )(*[pltpu.with_memory_space_constraint(g, pltpu.HBM) for g in grads], *lands)
    return out[0], out[1], list(out[2:2 + n_g]), list(out[2 + n_g:2 + 2 * n_g]), out[-1]


def _sibling_swap_wait(name, send_sems, recv_sems, grads, lands, after):
    n_g = len(grads)

    def body(*refs):
        g_refs, land_refs = refs[:n_g], refs[n_g:2 * n_g]
        send_sems, recv_sems = refs[2 * n_g:2 * n_g + 2]
        for cp in _sibling_swap_copies(g_refs, land_refs, send_sems, recv_sems):
            cp.wait_send()
            cp.wait_recv()

    out = pl.pallas_call(
        body, name=name,
        out_shape=(*[pltpu.HBM(g.shape, g.dtype) for g in grads], *[pltpu.HBM(l.shape, l.dtype) for l in lands]),
        in_specs=[*[HBM_SPEC] * (2 * n_g), SEM_SPEC, SEM_SPEC, ANY], out_specs=tuple([HBM_SPEC] * (2 * n_g)),
        input_output_aliases={i: i for i in range(2 * n_g)},
        compiler_params=pltpu.CompilerParams(has_side_effects=DATAFLOW),
    )(*grads, *lands, send_sems, recv_sems, after)
    return list(out[:n_g]), list(out[n_g:])


def _pack_row_tile(rows):
    for t in range(min(rows, 512), 7, -1):
        if rows % t == 0 and t % 8 == 0:
            return t
    return rows


def _add_sibling_half(name, grads, from_sibling, c_idx):
    n, R, w = grads.shape
    half = R // 2
    tr = _pack_row_tile(half)
    steps = half // tr

    def body(c_ref, g_ref, s_ref, o_ref):
        o_ref[...] = (g_ref[...] + s_ref[...]).astype(BF16)

    return pl.pallas_call(
        body, name=name,
        grid_spec=pltpu.PrefetchScalarGridSpec(
            num_scalar_prefetch=1, grid=(n, steps),
            in_specs=[pl.BlockSpec((1, tr, w), lambda j, i, c_ref: (j, c_ref[0] * steps + i, 0)),
                      pl.BlockSpec((1, tr, w), lambda j, i, c_ref: (j, i, 0))],
            out_specs=pl.BlockSpec((1, tr, w), lambda j, i, c_ref: (j, i, 0))),
        out_shape=SDS((n, half, w), BF16),
        compiler_params=_params("parallel", "parallel"),
    )(c_idx, grads, from_sibling)


HBM_SPEC = pl.BlockSpec(memory_space=pltpu.HBM)
SEM_SPEC = pl.BlockSpec(memory_space=pltpu.SEMAPHORE)
DATAFLOW = pltpu.SideEffectType.DATAFLOW_SIDE_EFFECTING


def _chip_scatter_copies(p_refs, land_refs, send_sems, recv_sems):
    x, y, c = _my_place()
    return [pltpu.make_async_remote_copy(
        src_ref=p_refs[i].at[kb], dst_ref=land_refs[i].at[j], send_sem=send_sems.at[3 * i + j],
        recv_sem=recv_sems.at[3 * i + j], device_id=(bx, by, c), device_id_type=MESH)
        for i in range(len(p_refs)) for j, (bx, by, kb) in enumerate(_other_chips(x, y))]


def _scatter_start(name, partials):
    n_p = len(partials)

    def body(*refs):
        p_refs, land_refs = refs[:n_p], refs[n_p:2 * n_p]
        send_sems, recv_sems = refs[2 * n_p:2 * n_p + 2]
        for cp in _chip_scatter_copies(p_refs, land_refs, send_sems, recv_sems):
            cp.start()
        refs[-1][...] = jnp.zeros_like(refs[-1])

    lands = [pltpu.with_memory_space_constraint(lax.empty((N_CHIPS - 1,) + p.shape[1:], p.dtype), pltpu.HBM)
             for p in partials]
    out = pl.pallas_call(
        body, name=name,
        out_shape=(pltpu.SemaphoreType.DMA((3 * n_p,)), pltpu.SemaphoreType.DMA((3 * n_p,)),
                   *[pltpu.HBM(p.shape, p.dtype) for p in partials], *[pltpu.HBM(l.shape, l.dtype) for l in lands],
                   SDS((8, V7X_LANES), F32)),
        in_specs=[HBM_SPEC] * (2 * n_p), out_specs=(SEM_SPEC, SEM_SPEC, *[HBM_SPEC] * (2 * n_p), VMEM_WHOLE),
        input_output_aliases={i: 2 + i for i in range(2 * n_p)},
        compiler_params=pltpu.CompilerParams(has_side_effects=DATAFLOW),
    )(*[pltpu.with_memory_space_constraint(p, pltpu.HBM) for p in partials], *lands)
    return out[0], out[1], list(out[2:2 + n_p]), list(out[2 + n_p:2 + 2 * n_p]), out[-1]


def _scatter_wait(name, send_sems, recv_sems, partials, lands, after):
    n_p = len(partials)

    def body(*refs):
        p_refs, land_refs = refs[:n_p], refs[n_p:2 * n_p]
        send_sems, recv_sems = refs[2 * n_p:2 * n_p + 2]
        for cp in _chip_scatter_copies(p_refs, land_refs, send_sems, recv_sems):
            cp.wait_send()
            cp.wait_recv()

    out = pl.pallas_call(
        body, name=name,
        out_shape=(*[pltpu.HBM(p.shape, p.dtype) for p in partials], *[pltpu.HBM(l.shape, l.dtype) for l in lands]),
        in_specs=[*[HBM_SPEC] * (2 * n_p), SEM_SPEC, SEM_SPEC, ANY], out_specs=tuple([HBM_SPEC] * (2 * n_p)),
        input_output_aliases={i: i for i in range(2 * n_p)},
        compiler_params=pltpu.CompilerParams(has_side_effects=DATAFLOW),
    )(*partials, *lands, send_sems, recv_sems, after)
    return list(out[:n_p]), list(out[n_p:])


def _sum_chip_partials(name, grads, from_sibling, received, shard, layer, place):
    n, half, w = from_sibling.shape
    tr = _pack_row_tile(half)
    steps = half // tr

    def body(place_ref, g_ref, s_ref, r_ref, shard_ref, o_ref):
        own = g_ref[0] + s_ref[0]
        o_ref[...] = ((own + r_ref[0].astype(F32)) + r_ref[1].astype(F32)) + r_ref[2].astype(F32)

    return pl.pallas_call(
        body, name=name,
        grid_spec=pltpu.PrefetchScalarGridSpec(
            num_scalar_prefetch=1, grid=(steps,),
            in_specs=[pl.BlockSpec((1, tr, w), lambda i, place_ref: (place_ref[0], place_ref[1] * steps + i, 0)),
                      pl.BlockSpec((1, tr, w), lambda i, place_ref: (place_ref[0], i, 0)),
                      pl.BlockSpec((n - 1, tr, w), lambda i, place_ref: (0, i, 0)),
                      pl.BlockSpec(memory_space=pl.ANY)],
            out_specs=pl.BlockSpec((tr, w), lambda i, place_ref: ((2 * layer + place_ref[1]) * steps + i, 0))),
        out_shape=SDS(shard.shape, F32),
        input_output_aliases={4: 0},
        compiler_params=_params("parallel"),
    )(place, grads, from_sibling, received, shard)


def _join_halves(shards, layers):
    n_s = len(shards)
    n_sem = sum(layers)

    def body(*refs):
        in_refs = refs[:n_s]
        send_sems, recv_sems = refs[2 * n_s:]
        x, y, c = _my_place()
        copies, sem = [], 0
        for ref, n_layers in zip(in_refs, layers):
            half = ref.shape[0] // (2 * n_layers)
            for layer in range(n_layers):
                mine = pl.ds(pl.multiple_of(layer * 2 * half + c * half, 8), half)
                theirs = pl.ds(pl.multiple_of(layer * 2 * half + (1 - c) * half, 8), half)
                send = pltpu.make_async_remote_copy(
                    src_ref=ref.at[mine], dst_ref=ref.at[mine], send_sem=send_sems.at[sem], recv_sem=recv_sems.at[sem],
                    device_id=(x, y, 1 - c), device_id_type=MESH)
                send.start()
                arrival = pltpu.make_async_remote_copy(
                    src_ref=ref.at[theirs], dst_ref=ref.at[theirs], send_sem=send_sems.at[sem],
                    recv_sem=recv_sems.at[sem], device_id=(x, y, 1 - c), device_id_type=MESH)
                copies.append((send, arrival))
                sem += 1
        for send, arrival in copies:
            send.wait_send()
            arrival.wait_recv()

    return pl.pallas_call(
        body, name="join_halves", out_shape=[SDS(s.shape, s.dtype) for s in shards],
        in_specs=[ANY] * n_s, out_specs=[ANY] * n_s,
        input_output_aliases={i: i for i in range(n_s)},
        scratch_shapes=[pltpu.SemaphoreType.DMA((n_sem,)), pltpu.SemaphoreType.DMA((n_sem,))],
    )(*shards)


def _all_to_all_copies(v_ref, land_ref, send_sems, recv_sems):
    x, y, c = _my_place()
    me = 4 * x + 2 * y + c
    pairs = []
    for k in range(1, N_DEV):
        px, py, pc = (1 - x if k & 4 else x), (1 - y if k & 2 else y), (1 - c if k & 1 else c)
        sems = dict(send_sem=send_sems.at[k - 1], recv_sem=recv_sems.at[k - 1], device_id=(px, py, pc),
                    device_id_type=MESH)
        send = pltpu.make_async_remote_copy(src_ref=v_ref, dst_ref=land_ref.at[me], **sems)
        arrival = pltpu.make_async_remote_copy(src_ref=v_ref, dst_ref=land_ref.at[4 * px + 2 * py + pc], **sems)
        pairs.append((send, arrival))
    return pairs


def _small_reduce_start(name, v):
    def body(v_ref, land_ref, send_sems, recv_sems, v_out, land_out):
        for send, _ in _all_to_all_copies(v_ref, land_ref, send_sems, recv_sems):
            send.start()

    land = pltpu.with_memory_space_constraint(jnp.zeros((N_DEV,) + v.shape, v.dtype), pltpu.HBM)
    return pl.pallas_call(
        body, name=name,
        out_shape=(pltpu.SemaphoreType.DMA((N_DEV - 1,)), pltpu.SemaphoreType.DMA((N_DEV - 1,)),
                   pltpu.HBM(v.shape, v.dtype), pltpu.HBM(land.shape, land.dtype)),
        in_specs=[HBM_SPEC, HBM_SPEC], out_specs=(SEM_SPEC, SEM_SPEC, HBM_SPEC, HBM_SPEC),
        input_output_aliases={0: 2, 1: 3},
        compiler_params=pltpu.CompilerParams(has_side_effects=DATAFLOW),
    )(pltpu.with_memory_space_constraint(v, pltpu.HBM), land)


def _small_reduce_wait(name, send_sems, recv_sems, v, land, after):
    def body(v_ref, land_ref, send_sems, recv_sems, after_ref, v_out, land_out):
        for send, arrival in _all_to_all_copies(v_ref, land_ref, send_sems, recv_sems):
            send.wait_send()
            arrival.wait_recv()

    return pl.pallas_call(
        body, name=name, out_shape=(pltpu.HBM(v.shape, v.dtype), pltpu.HBM(land.shape, land.dtype)),
        in_specs=[HBM_SPEC, HBM_SPEC, SEM_SPEC, SEM_SPEC, ANY], out_specs=(HBM_SPEC, HBM_SPEC),
        input_output_aliases={0: 0, 1: 1},
        compiler_params=pltpu.CompilerParams(has_side_effects=DATAFLOW),
    )(v, land, send_sems, recv_sems, after)


def _sum_device_slots(name, v, land, me):
    r, w = v.shape

    def body(me_ref, v_ref, land_ref, o_ref):
        mine = v_ref[...]
        acc = jnp.where(me_ref[0] == 0, mine, land_ref[0])
        for d in range(1, N_DEV):
            acc = acc + jnp.where(me_ref[0] == d, mine, land_ref[d])
        o_ref[...] = acc

    return pl.pallas_call(
        body, name=name,
        grid_spec=pltpu.PrefetchScalarGridSpec(
            num_scalar_prefetch=1, grid=(1,),
            in_specs=[pl.BlockSpec((r, w), lambda i, me_ref: (0, 0)),
                      pl.BlockSpec((N_DEV, r, w), lambda i, me_ref: (0, 0, 0))],
            out_specs=pl.BlockSpec((r, w), lambda i, me_ref: (0, 0))),
        out_shape=SDS((r, w), F32),
        compiler_params=_params("arbitrary"),
    )(me, v, land)


def _adamw(name, w, g, m, v):
    rows, width = w.shape
    tr = _pack_row_tile(rows)

    def body(w_ref, g_ref, m_ref, v_ref, g_out_ref, d_ref, nm_ref, nv_ref):
        gg = g_ref[...]
        g_out_ref[...] = gg
        m_new = ADAM_B1 * m_ref[...] + (1.0 - ADAM_B1) * gg
        v_new = ADAM_B2 * v_ref[...] + (1.0 - ADAM_B2) * (gg * gg)
        m_hat = m_new / (1.0 - ADAM_B1 ** ADAM_STEP)
        v_hat = v_new / (1.0 - ADAM_B2 ** ADAM_STEP)
        d_ref[...] = -ADAM_LR * (m_hat / (jnp.sqrt(v_hat) + ADAM_EPS) + ADAM_WD * w_ref[...])
        nm_ref[...] = m_new
        nv_ref[...] = v_new

    spec = _rows(tr, width)
    return pl.pallas_call(
        body, name=name, grid=(rows // tr,),
        in_specs=[spec] * 4, out_specs=[spec] * 4,
        out_shape=[SDS((rows, width), F32)] * 4,
        compiler_params=_params("parallel"),
    )(w, g, m, v)


WEIGHT_NAMES = ['norm_mix', 'norm_ffn', 'attn_w_qkv', 'attn_b_qkv', 'attn_sinks', 'attn_w_o', 'attn_b_o',
                'conv_w_pw1', 'conv_b_pw1', 'conv_w_dw', 'conv_b_dw', 'conv_ln_g', 'conv_ln_b', 'conv_w_pw2',
                'conv_b_pw2', 'ffn_w1', 'ffn_w3', 'ffn_w2', 'norm_final']
BIG = ['attn_w_qkv', 'attn_w_o', 'conv_w_pw1', 'conv_w_pw2', 'ffn_w1', 'ffn_w3', 'ffn_w2']
COLUMN_SPLIT = ('attn_w_qkv', 'conv_w_pw1', 'ffn_w1', 'ffn_w3')
STORED_TRANSPOSED = ('attn_w_qkv', 'ffn_w1', 'ffn_w3')
SMALL_SPLIT = ['conv_b_pw1', 'conv_w_dw', 'conv_b_dw', 'conv_ln_g', 'conv_ln_b', 'conv_b_pw2']
SMALL_WHOLE = ['norm_mix', 'norm_ffn', 'attn_b_qkv', 'attn_sinks', 'attn_b_o', 'norm_final']


def _keeps_rows(shape):
    return len(shape) == 2 and shape[0] > 1 and shape[1] == PACK_W


def _pack_rows(arrays, dtype, row_multiple):
    blocks = [jnp.pad(a.astype(dtype), ((0, -a.shape[0] % V7X_SUBLANES), (0, 0)))
              for a in arrays if _keeps_rows(a.shape)]
    flat = jnp.concatenate([a.astype(dtype).reshape(-1) for a in arrays if not _keeps_rows(a.shape)])
    multiple = max(row_multiple, V7X_SUBLANES)
    rows = -(-(-(-flat.shape[0] // PACK_W)) // multiple) * multiple
    blocks.append(jnp.pad(flat, (0, rows * PACK_W - flat.shape[0])).reshape(rows, PACK_W))
    return jnp.concatenate(blocks, axis=0) if len(blocks) > 1 else blocks[0]


def _unpack_rows(pack, shapes):
    out, row = {}, 0
    for i, shape in enumerate(shapes):
        if _keeps_rows(shape):
            out[i] = pack[row:row + shape[0]]
            row += -(-shape[0] // V7X_SUBLANES) * V7X_SUBLANES
    flat, at = pack[row:].reshape(-1), 0
    for i, shape in enumerate(shapes):
        if not _keeps_rows(shape):
            size = 1
            for s in shape:
                size *= s
            out[i] = flat[at:at + size].reshape(shape)
            at += size
    return [out[i] for i in range(len(shapes))]


def _join_chip_axis(name, parts):
    axis = parts.ndim - 1 if name in COLUMN_SPLIT or name in SMALL_SPLIT else parts.ndim - 2
    moved = jnp.moveaxis(parts, 0, axis - 1)
    shape = list(moved.shape)
    shape[axis - 1:axis + 1] = [shape[axis - 1] * shape[axis]]
    return moved.reshape(shape)


def _split_chip_axis(name, whole, shard_shape):
    axis = len(shard_shape) - 1 if name in COLUMN_SPLIT or name in SMALL_SPLIT else len(shard_shape) - 2
    shape = list(whole.shape)
    shape[axis:axis + 1] = [N_CHIPS, shard_shape[axis]]
    return jnp.moveaxis(whole.reshape(shape), axis, 0)


def kernel(x, norm_mix, norm_ffn, attn_w_qkv, attn_b_qkv, attn_sinks, attn_w_o, attn_b_o, conv_w_pw1, conv_b_pw1, conv_w_dw, conv_b_dw, conv_ln_g, conv_ln_b, conv_w_pw2, conv_b_pw2, ffn_w1, ffn_w3, ffn_w2, norm_final, loss_target, m_norm_mix, m_norm_ffn, m_attn_w_qkv, m_attn_b_qkv, m_attn_sinks, m_attn_w_o, m_attn_b_o, m_conv_w_pw1, m_conv_b_pw1, m_conv_w_dw, m_conv_b_dw, m_conv_ln_g, m_conv_ln_b, m_conv_w_pw2, m_conv_b_pw2, m_ffn_w1, m_ffn_w3, m_ffn_w2, m_norm_final, v_norm_mix, v_norm_ffn, v_attn_w_qkv, v_attn_b_qkv, v_attn_sinks, v_attn_w_o, v_attn_b_o, v_conv_w_pw1, v_conv_b_pw1, v_conv_w_dw, v_conv_b_dw, v_conv_ln_g, v_conv_ln_b, v_conv_w_pw2, v_conv_b_pw2, v_ffn_w1, v_ffn_w3, v_ffn_w2, v_norm_final):
    w = dict(zip(WEIGHT_NAMES, (norm_mix, norm_ffn, attn_w_qkv, attn_b_qkv, attn_sinks, attn_w_o, attn_b_o,
                                conv_w_pw1, conv_b_pw1, conv_w_dw, conv_b_dw, conv_ln_g, conv_ln_b, conv_w_pw2,
                                conv_b_pw2, ffn_w1, ffn_w3, ffn_w2, norm_final)))
    m = dict(zip(WEIGHT_NAMES, (m_norm_mix, m_norm_ffn, m_attn_w_qkv, m_attn_b_qkv, m_attn_sinks, m_attn_w_o,
                                m_attn_b_o, m_conv_w_pw1, m_conv_b_pw1, m_conv_w_dw, m_conv_b_dw, m_conv_ln_g,
                                m_conv_ln_b, m_conv_w_pw2, m_conv_b_pw2, m_ffn_w1, m_ffn_w3, m_ffn_w2, m_norm_final)))
    v = dict(zip(WEIGHT_NAMES, (v_norm_mix, v_norm_ffn, v_attn_w_qkv, v_attn_b_qkv, v_attn_sinks, v_attn_w_o,
                                v_attn_b_o, v_conv_w_pw1, v_conv_b_pw1, v_conv_w_dw, v_conv_b_dw, v_conv_ln_g,
                                v_conv_ln_b, v_conv_w_pw2, v_conv_b_pw2, v_ffn_w1, v_ffn_w3, v_ffn_w2, v_norm_final)))
    T, D = x.shape[1], x.shape[2]
    c_idx = lax.axis_index("c").astype(jnp.int32).reshape(1)
    chip = (2 * lax.axis_index("x") + lax.axis_index("y")).astype(jnp.int32)

    def as_rows(n, a):
        a = jnp.swapaxes(a, -1, -2) if n in STORED_TRANSPOSED else a
        return a.reshape(-1, a.shape[-1])

    def from_rows(n, rows):
        shape = w[n].shape[:-2] + w[n].shape[:-3:-1] if n in STORED_TRANSPOSED else w[n].shape
        a = rows.reshape(shape)
        return jnp.swapaxes(a, -1, -2) if n in STORED_TRANSPOSED else a

    def cast(n, after=None):
        return _cast_into_slot(f"cast_{n}", lax.empty((N_CHIPS,) + as_rows(n, w[n]).shape, BF16), as_rows(n, w[n]),
                               chip.reshape(1), after)

    first, later = BIG[:2], BIG[2:]
    slabs = {n: cast(n) for n in first}
    attn_send, attn_recv, attn_travelling, attn_started = _gather_start(
        "gather_attn_start", [slabs[n] for n in first], slabs[first[0]])
    slabs.update({n: cast(n, attn_started) for n in later})
    layers = ffn_w1.shape[0]
    small_shapes = [w[n].shape for n in SMALL_SPLIT]
    vec_send, vec_recv, vec_mine, vec_landing = _small_reduce_start(
        "small_gather_start", _pack_rows([w[n] for n in SMALL_SPLIT], F32, 8))
    attn_landed = _gather_wait("gather_attn_wait", attn_send, attn_recv, attn_travelling, vec_mine)
    send_sems, recv_sems, travelling, gather_started = _gather_start("gather_start", [slabs[n] for n in later],
                                                                     attn_landed[0])
    qkv_parts, w_o_parts = _swap_fetched_with_sibling("gather_attn_swap", attn_landed, gather_started)

    def small_vectors(after):
        mine, landed = _small_reduce_wait("small_gather_wait", vec_send, vec_recv, vec_mine, vec_landing, after)
        landed = lax.dynamic_update_index_in_dim(landed, mine, 2 * chip + c_idx[0], axis=0)
        per_chip = [_unpack_rows(landed[2 * j], small_shapes) for j in range(N_CHIPS)]
        whole = {n: _join_chip_axis(n, jnp.stack([per_chip[j][i] for j in range(N_CHIPS)]))
                 for i, n in enumerate(SMALL_SPLIT)}
        return {**{n: whole[n] for n in SMALL_SPLIT if n != "conv_w_dw"}, "conv_w_dw": whole["conv_w_dw"][0]}

    def other_weights(after):
        landed = dict(zip(later, _gather_wait("gather_wait", send_sems, recv_sems, travelling, after)))
        now, then = ["ffn_w1", "ffn_w3"], ["conv_w_pw1", "conv_w_pw2", "ffn_w2"]
        ready = dict(zip(now, _swap_fetched_with_sibling("gather_swap", [landed[n] for n in now])))
        swap_send, swap_recv, swapping_rest, swap_started = _gather_start(
            "gather_swap_start", [landed[n] for n in then], ready[now[0]], stage="d2d")

        def rest_of_weights(after_next):
            rest = dict(zip(then, _gather_wait("gather_swap_wait", swap_send, swap_recv, swapping_rest, after_next,
                                               stage="d2d")))
            return {"conv_w_pw1": rest["conv_w_pw1"], "conv_w_pw2": rest["conv_w_pw2"].reshape(-1, D),
                    "ffn_w2": rest["ffn_w2"].reshape(N_CHIPS, layers, -1, D), **small_vectors(after_next)}

        return {"ffn_w1": ready["ffn_w1"].reshape(N_CHIPS, layers, -1, D),
                "ffn_w3": ready["ffn_w3"].reshape(N_CHIPS, layers, -1, D),
                "swap_started": swap_started, "rest_of_weights": rest_of_weights}

    p = {
        "norm_mix": norm_mix, "norm_ffn": norm_ffn, "norm_final": norm_final.reshape(1, D),
        "attn_w_qkv": qkv_parts.reshape(-1, D), "attn_b_qkv": attn_b_qkv,
        "attn_sinks": attn_sinks, "attn_w_o": w_o_parts.reshape(-1, D), "attn_b_o": attn_b_o,
        "gather_started": gather_started, "other_weights": other_weights,
    }
    swapping, in_flight = {}, []

    def reduce_begin(tag, grads):
        keys = list(grads)
        *handles, begun = _sibling_swap_start(f"sibling_swap_start_{tag}", [grads[k] for k in keys])
        swapping[tag] = (keys, handles)
        return begun

    def reduce_send(tag, after):
        keys, (swap_send, swap_recv, grads, lands) = swapping[tag]
        grads, from_sibling = _sibling_swap_wait(f"sibling_swap_wait_{tag}", swap_send, swap_recv, grads, lands, after)
        partials = [_add_sibling_half(f"add_sibling_half_{tag}{i}", gr, fs, c_idx)
                    for i, (gr, fs) in enumerate(zip(grads, from_sibling))]
        *handles, sent = _scatter_start(f"scatter_start_{tag}", partials)
        in_flight.append((tag, keys, handles, grads, from_sibling))
        return sent

    loss_part, dx, g = _local_step(x[0], loss_target[0], p, reduce_begin, reduce_send)
    for n in SMALL_WHOLE + SMALL_SPLIT:
        g[n] = g[n].reshape((-1,) + g[n].shape[-2:]) if w[n].ndim == 3 else g[n].reshape(w[n].shape[:-1] + (-1,))

    small_pack = _pack_rows([loss_part] + [g[n] for n in SMALL_WHOLE] + [g[n] for n in SMALL_SPLIT], F32, 8)
    small_send, small_recv, small_pack, small_land = _small_reduce_start("small_reduce_start", small_pack)

    place = jnp.stack([chip, c_idx[0]])
    shard_grad = {n: lax.empty(as_rows(n, w[n]).shape, F32) for n in BIG}
    for tag, keys, (send_sems, recv_sems, partials, lands), grads, from_sibling in in_flight:
        _, received = _scatter_wait(f"scatter_wait_{tag}", send_sems, recv_sems, partials, lands, small_pack)
        for i, (n, layer) in enumerate(keys):
            shard_grad[n] = _sum_chip_partials(f"sum_chip_partials_{tag}{i}", grads[i], from_sibling[i], received[i],
                                               shard_grad[n], layer, place)
    g_big = dict(zip(BIG, _join_halves([shard_grad[n] for n in BIG], [w[n].shape[0] for n in BIG])))
    big_out = {}
    for n in BIG:
        step = _adamw(f"adamw_{n}", as_rows(n, w[n]), g_big[n], as_rows(n, m[n]), as_rows(n, v[n]))
        big_out[n] = [from_rows(n, a) for a in step]

    small_whole_shapes = [w[n].shape for n in SMALL_WHOLE]
    small_full_shapes = [g[n].shape for n in SMALL_SPLIT]
    small_pack, small_land = _small_reduce_wait("small_reduce_wait", small_send, small_recv, small_pack, small_land,
                                                big_out[BIG[-1]][1])
    reduced = _sum_device_slots("small_reduce_sum", small_pack, small_land, (2 * chip + c_idx[0]).reshape(1))
    pieces = _unpack_rows(reduced, [(1,)] + small_whole_shapes + small_full_shapes)
    loss = pieces[0].reshape(())
    g_small = dict(zip(SMALL_WHOLE, pieces[1:1 + len(SMALL_WHOLE)]))
    for n, whole in zip(SMALL_SPLIT, pieces[1 + len(SMALL_WHOLE):]):
        parts = _split_chip_axis(n, whole, w[n].shape)
        g_small[n] = lax.dynamic_index_in_dim(parts, chip, axis=0, keepdims=False)
    small = SMALL_WHOLE + SMALL_SPLIT
    _, d_small, m_small, v_small = _adamw(
        "adamw_small", _pack_rows([w[n] for n in small], F32, 8), _pack_rows([g_small[n] for n in small], F32, 8),
        _pack_rows([m[n] for n in small], F32, 8), _pack_rows([v[n] for n in small], F32, 8))

    outs = {}
    for slot, (tag, small_pack) in enumerate((("g", None), ("d", d_small), ("m", m_small), ("v", v_small))):
        vals = {n: big_out[n][slot] for n in BIG}
        if small_pack is None:
            vals.update(g_small)
        else:
            vals.update(zip(small, _unpack_rows(small_pack, [w[n].shape for n in small])))
        outs[tag] = vals
    return (loss, dx.reshape(1, T, D), *[outs["g"][n] for n in WEIGHT_NAMES], *[outs["d"][n] for n in WEIGHT_NAMES],
            *[outs["m"][n] for n in WEIGHT_NAMES], *[outs["v"][n] for n in WEIGHT_NAMES])
```
